```python
import jax, jax.numpy as jnp
from jax import lax
import numpy as np

D_MODEL = 1024
BATCH = 16
SEQ = 2048
DEPTH = 1

N_META = 16
CHUNK = 64
HG_HEADS = 4
HG_DK = 128
HG_DV = 128
GDN_HEADS = 4
GDN_DK = 128
GDN_DV = 128
CONV_W = 4
HG_WIDTH = HG_HEADS * HG_DK
GDN_WIDTH = GDN_HEADS * GDN_DV
D_MIX = HG_WIDTH + GDN_WIDTH
IN_COLS = 4 * HG_WIDTH + 4 * GDN_WIDTH + 2 * GDN_HEADS
EPS = 1e-6

kernel_name = "hymba_hgrn2_gated_deltanet_layer"


def rmsnorm(x, w):
    xf = x.astype(jnp.float32)
    y = xf * lax.rsqrt(jnp.mean(xf * xf, axis=-1, keepdims=True) + EPS)
    return (y * w.astype(jnp.float32)).astype(x.dtype)


def gated_rmsnorm(o, z, w):
    of = jnp.transpose(o, (0, 2, 1, 3)).astype(jnp.float32)
    y = of * lax.rsqrt(jnp.mean(of * of, axis=-1, keepdims=True) + EPS) * w.astype(jnp.float32)
    B, T = y.shape[0], y.shape[1]
    return (y.reshape(B, T, -1) * jax.nn.silu(z.astype(jnp.float32))).astype(z.dtype)


def heads(a, H):
    B, T, _ = a.shape
    return jnp.transpose(a.reshape(B, T, H, -1), (0, 2, 1, 3))


def causal_conv(x, w):
    C = x.shape[-1]
    return lax.conv_general_dilated(x, w[:, None, :].astype(x.dtype), window_strides=(1,),
                                    padding=((CONV_W - 1, 0),),
                                    dimension_numbers=('NWC', 'WIO', 'NWC'),
                                    feature_group_count=C)


def to_chunks(a, C):
    B, H, T = a.shape[:3]
    return jnp.moveaxis(a.reshape((B, H, T // C, C) + a.shape[3:]), 2, 0)


def from_chunks(o):
    nc, B, H, C, d = o.shape
    return jnp.moveaxis(o, 0, 2).reshape(B, H, nc * C, d)


def run_chunked(step, inputs, S0, C):
    xs = tuple(to_chunks(a, C) for a in inputs)
    S, o = lax.scan(step, S0, xs)
    return S, from_chunks(o)


def causal_mixer(step, inputs, S0):
    meta = tuple(a[:, :, :N_META] for a in inputs)
    real = tuple(a[:, :, N_META:] for a in inputs)
    S, o_meta = run_chunked(step, meta, S0, N_META)
    _, o_real = run_chunked(step, real, S, CHUNK)
    return jnp.concatenate([o_meta, o_real], axis=2)


def hgrn2_chunk(S, inp):
    q, k, v, logf = inp
    C = q.shape[2]
    G = jnp.cumsum(logf, axis=2)
    causal = jnp.tril(jnp.ones((C, C), dtype=bool))
    diff = G[:, :, :, None, :] - G[:, :, None, :, :]
    decay = jnp.exp(jnp.where(causal[None, None, :, :, None], diff, -jnp.inf))
    A = jnp.einsum('bhtd,bhtsd,bhsd->bhts', q, decay, k)
    o = jnp.einsum('bhts,bhsv->bhtv', A, v) + jnp.einsum('bhtd,bhdv->bhtv', q * jnp.exp(G), S)
    G_last = G[:, :, -1]
    S_new = jnp.exp(G_last)[..., None] * S + jnp.einsum(
        'bhsd,bhsv->bhdv', k * jnp.exp(G_last[:, :, None, :] - G), v)
    return S_new, o


def gdn_chunk(S, inp):
    q, k, v, g, beta = inp
    C = q.shape[2]
    dv = v.shape[-1]
    gam = jnp.cumsum(g, axis=-1)
    diff = gam[..., :, None] - gam[..., None, :]
    strict = jnp.tril(jnp.ones((C, C), dtype=bool), -1)
    incl = jnp.tril(jnp.ones((C, C), dtype=bool))
    dec_strict = jnp.exp(jnp.where(strict, diff, -jnp.inf))
    dec_incl = jnp.exp(jnp.where(incl, diff, -jnp.inf))
    A = beta[..., None] * jnp.einsum('bhtd,bhsd->bhts', k, k) * dec_strict
    lhs = jnp.eye(C, dtype=A.dtype) + A
    rhs = jnp.concatenate([beta[..., None] * v, (beta * jnp.exp(gam))[..., None] * k], axis=-1)
    sol = lax.linalg.triangular_solve(lhs, rhs, left_side=True, lower=True)
    U, W = sol[..., :dv], sol[..., dv:]
    u = U - jnp.einsum('bhtd,bhdv->bhtv', W, S)
    qk = jnp.einsum('bhtd,bhsd->bhts', q, k) * dec_incl
    o = jnp.einsum('bhtd,bhdv->bhtv', q * jnp.exp(gam)[..., None], S) + jnp.einsum('bhts,bhsv->bhtv', qk, u)
    g_last = gam[..., -1]
    S_new = jnp.exp(g_last)[..., None, None] * S + jnp.einsum(
        'bhsd,bhsv->bhdv', k * jnp.exp(g_last[..., None] - gam)[..., None], u)
    return S_new, o


def hybrid_layer(h, norm_w, w_in, conv_w, lower_bound, hg_norm_w, gdn_A_log, gdn_dt_bias, gdn_norm_w, w_out):
    B, T, _ = h.shape
    f32 = jnp.float32
    u = rmsnorm(h, norm_w)
    proj = jnp.einsum('btd,dc->btc', u, w_in)
    cuts = np.cumsum([HG_WIDTH, HG_WIDTH, HG_WIDTH, HG_WIDTH, 3 * GDN_WIDTH, GDN_WIDTH, GDN_HEADS])
    hg_q, hg_f, hg_i, hg_z, gd_qkv, gd_z, gd_a, gd_b = jnp.split(proj, [int(c) for c in cuts], axis=-1)

    q = heads(jax.nn.silu(hg_q.astype(f32)), HG_HEADS)
    lb = lower_bound.astype(f32)
    f = lb + (1.0 - lb) * jax.nn.sigmoid(hg_f.astype(f32))
    k = heads(1.0 - f, HG_HEADS)
    logf = heads(jnp.log(f), HG_HEADS)
    v = heads(hg_i.astype(f32), HG_HEADS)
    S0 = jnp.zeros((B, HG_HEADS, HG_DK, HG_DV), f32)
    o_hg = causal_mixer(hgrn2_chunk, (q, k, v, logf), S0)
    y_hg = gated_rmsnorm(o_hg, hg_z, hg_norm_w)

    qkv = jax.nn.silu(causal_conv(gd_qkv, conv_w).astype(f32))
    gq, gk, gv = jnp.split(qkv, 3, axis=-1)
    gq = heads(gq, GDN_HEADS)
    gk = heads(gk, GDN_HEADS)
    gv = heads(gv, GDN_HEADS)
    gq = gq * lax.rsqrt(jnp.sum(gq * gq, -1, keepdims=True) + EPS) * (GDN_DK ** -0.5)
    gk = gk * lax.rsqrt(jnp.sum(gk * gk, -1, keepdims=True) + EPS)
    g = -jnp.exp(gdn_A_log.astype(f32)) * jax.nn.softplus(gd_a.astype(f32) + gdn_dt_bias.astype(f32))
    g = jnp.transpose(g, (0, 2, 1))
    beta = jnp.transpose(jax.nn.sigmoid(gd_b.astype(f32)), (0, 2, 1))
    S0g = jnp.zeros((B, GDN_HEADS, GDN_DK, GDN_DV), f32)
    o_gd = causal_mixer(gdn_chunk, (gq, gk, gv, g, beta), S0g)
    y_gd = gated_rmsnorm(o_gd, gd_z, gdn_norm_w)

    y = jnp.concatenate([y_hg, y_gd], axis=-1)
    return jnp.einsum('btc,cd->btd', y, w_out).astype(h.dtype)


def _fwd_setup_inputs(seed: int = 0) -> dict:
    key = jax.random.key(seed)
    ks = jax.random.split(key, 14)
    f32 = jnp.float32
    x = jax.random.normal(ks[0], (BATCH, SEQ, D_MODEL), f32)
    meta_tokens = jax.random.normal(ks[1], (N_META, D_MODEL), f32)
    norm_w = 1.0 + 0.02 * jax.random.normal(ks[2], (DEPTH, D_MODEL), f32)
    w_in = jax.random.normal(ks[3], (DEPTH, D_MODEL, IN_COLS), f32) * D_MODEL ** -0.5
    conv_w = jax.random.normal(ks[4], (DEPTH, CONV_W, 3 * GDN_WIDTH), f32) * CONV_W ** -0.5
    hg_lb_logits = 0.5 * jax.random.normal(ks[5], (DEPTH + 1, HG_WIDTH), f32)
    hg_norm_w = 1.0 + 0.02 * jax.random.normal(ks[6], (DEPTH, HG_DV), f32)
    gdn_A_log = jnp.log(jax.random.uniform(ks[7], (DEPTH, GDN_HEADS), f32, 1.0, 16.0))
    dt = jnp.exp(jax.random.uniform(ks[8], (DEPTH, GDN_HEADS), f32, jnp.log(0.001), jnp.log(0.1)))
    gdn_dt_bias = dt + jnp.log(-jnp.expm1(-dt))
    gdn_norm_w = 1.0 + 0.02 * jax.random.normal(ks[9], (DEPTH, GDN_DV), f32)
    w_out = jax.random.normal(ks[10], (DEPTH, D_MIX, D_MODEL), f32) * D_MIX ** -0.5
    final_norm_w = 1.0 + 0.02 * jax.random.normal(ks[11], (D_MODEL,), f32)
    return {"x": x, "meta_tokens": meta_tokens, "norm_w": norm_w, "w_in": w_in, "conv_w": conv_w,
            "hg_lb_logits": hg_lb_logits, "hg_norm_w": hg_norm_w, "gdn_A_log": gdn_A_log,
            "gdn_dt_bias": gdn_dt_bias, "gdn_norm_w": gdn_norm_w, "w_out": w_out,
            "final_norm_w": final_norm_w}


def _fwd_reference(x, meta_tokens, norm_w, w_in, conv_w, hg_lb_logits, hg_norm_w, gdn_A_log,
              gdn_dt_bias, gdn_norm_w, w_out, final_norm_w):
    B = x.shape[0]
    meta = jnp.broadcast_to(meta_tokens[None].astype(x.dtype), (B, N_META, D_MODEL))
    h = jnp.concatenate([meta, x], axis=1)
    lower_bounds = jnp.cumsum(jax.nn.softmax(hg_lb_logits.astype(jnp.float32), axis=0), axis=0)
    for l in range(DEPTH):
        h = h + hybrid_layer(h, norm_w[l], w_in[l], conv_w[l], lower_bounds[l], hg_norm_w[l],
                             gdn_A_log[l], gdn_dt_bias[l], gdn_norm_w[l], w_out[l])
    y = rmsnorm(h, final_norm_w)
    return y[:, N_META:]


import jax as _jax
import jax.numpy as _jnp

TWIN_FORMAT = 'train_step'
FWD_PARAMS = ['x', 'meta_tokens', 'norm_w', 'w_in', 'conv_w', 'hg_lb_logits', 'hg_norm_w', 'gdn_A_log', 'gdn_dt_bias', 'gdn_norm_w', 'w_out', 'final_norm_w']
TWIN_WEIGHTS = ['meta_tokens', 'norm_w', 'w_in', 'conv_w', 'hg_lb_logits', 'hg_norm_w', 'gdn_A_log', 'gdn_dt_bias', 'gdn_norm_w', 'w_out', 'final_norm_w']
TWIN_DIFF_INPUT = 'x'
TWIN_INPUTS = ['x', 'meta_tokens', 'norm_w', 'w_in', 'conv_w', 'hg_lb_logits', 'hg_norm_w', 'gdn_A_log', 'gdn_dt_bias', 'gdn_norm_w', 'w_out', 'final_norm_w', 'loss_target', 'm_meta_tokens', 'm_norm_w', 'm_w_in', 'm_conv_w', 'm_hg_lb_logits', 'm_hg_norm_w', 'm_gdn_A_log', 'm_gdn_dt_bias', 'm_gdn_norm_w', 'm_w_out', 'm_final_norm_w', 'v_meta_tokens', 'v_norm_w', 'v_w_in', 'v_conv_w', 'v_hg_lb_logits', 'v_hg_norm_w', 'v_gdn_A_log', 'v_gdn_dt_bias', 'v_gdn_norm_w', 'v_w_out', 'v_final_norm_w']
TWIN_OUTPUTS = ['loss', 'grad_x', 'grad_meta_tokens', 'grad_norm_w', 'grad_w_in', 'grad_conv_w', 'grad_hg_lb_logits', 'grad_hg_norm_w', 'grad_gdn_A_log', 'grad_gdn_dt_bias', 'grad_gdn_norm_w', 'grad_w_out', 'grad_final_norm_w', 'delta_meta_tokens', 'delta_norm_w', 'delta_w_in', 'delta_conv_w', 'delta_hg_lb_logits', 'delta_hg_norm_w', 'delta_gdn_A_log', 'delta_gdn_dt_bias', 'delta_gdn_norm_w', 'delta_w_out', 'delta_final_norm_w', 'new_m_meta_tokens', 'new_m_norm_w', 'new_m_w_in', 'new_m_conv_w', 'new_m_hg_lb_logits', 'new_m_hg_norm_w', 'new_m_gdn_A_log', 'new_m_gdn_dt_bias', 'new_m_gdn_norm_w', 'new_m_w_out', 'new_m_final_norm_w', 'new_v_meta_tokens', 'new_v_norm_w', 'new_v_w_in', 'new_v_conv_w', 'new_v_hg_lb_logits', 'new_v_hg_norm_w', 'new_v_gdn_A_log', 'new_v_gdn_dt_bias', 'new_v_gdn_norm_w', 'new_v_w_out', 'new_v_final_norm_w']
TWIN_LEAF_KINDS = {'loss': 'loss', 'grad_x': 'grad_x', 'grad_meta_tokens': 'grad_w', 'grad_norm_w': 'grad_w', 'grad_w_in': 'grad_w', 'grad_conv_w': 'grad_w', 'grad_hg_lb_logits': 'grad_w', 'grad_hg_norm_w': 'grad_w', 'grad_gdn_A_log': 'grad_w', 'grad_gdn_dt_bias': 'grad_w', 'grad_gdn_norm_w': 'grad_w', 'grad_w_out': 'grad_w', 'grad_final_norm_w': 'grad_w', 'delta_meta_tokens': 'delta_w', 'delta_norm_w': 'delta_w', 'delta_w_in': 'delta_w', 'delta_conv_w': 'delta_w', 'delta_hg_lb_logits': 'delta_w', 'delta_hg_norm_w': 'delta_w', 'delta_gdn_A_log': 'delta_w', 'delta_gdn_dt_bias': 'delta_w', 'delta_gdn_norm_w': 'delta_w', 'delta_w_out': 'delta_w', 'delta_final_norm_w': 'delta_w', 'new_m_meta_tokens': 'new_m', 'new_m_norm_w': 'new_m', 'new_m_w_in': 'new_m', 'new_m_conv_w': 'new_m', 'new_m_hg_lb_logits': 'new_m', 'new_m_hg_norm_w': 'new_m', 'new_m_gdn_A_log': 'new_m', 'new_m_gdn_dt_bias': 'new_m', 'new_m_gdn_norm_w': 'new_m', 'new_m_w_out': 'new_m', 'new_m_final_norm_w': 'new_m', 'new_v_meta_tokens': 'new_v', 'new_v_norm_w': 'new_v', 'new_v_w_in': 'new_v', 'new_v_conv_w': 'new_v', 'new_v_hg_lb_logits': 'new_v', 'new_v_hg_norm_w': 'new_v', 'new_v_gdn_A_log': 'new_v', 'new_v_gdn_dt_bias': 'new_v', 'new_v_gdn_norm_w': 'new_v', 'new_v_w_out': 'new_v', 'new_v_final_norm_w': 'new_v'}


def _forward(args):
    return _fwd_reference(*[args[k] for k in FWD_PARAMS])


def _output_shape():
    out = _jax.eval_shape(lambda: _forward(_fwd_setup_inputs(0)))
    return out.shape, out.dtype

N_MICROBATCH = 1
ADAM_LR = 0.001
ADAM_B1 = 0.9
ADAM_B2 = 0.999
ADAM_EPS = 1e-08
ADAM_WD = 0.01
ADAM_STEP = 10
PER_EXAMPLE_BATCH_AXIS = {'x': 0, 'loss_target': 0}
SHARED_INPUTS = []
_WEIGHT_DTYPES = {'meta_tokens': _jnp.float32, 'norm_w': _jnp.float32, 'w_in': _jnp.float32, 'conv_w': _jnp.float32, 'hg_lb_logits': _jnp.float32, 'hg_norm_w': _jnp.float32, 'gdn_A_log': _jnp.float32, 'gdn_dt_bias': _jnp.float32, 'gdn_norm_w': _jnp.float32, 'w_out': _jnp.float32, 'final_norm_w': _jnp.float32}
MOMENT_SCALE = {'meta_tokens': 4.436917e-03, 'norm_w': 1.452731e-01, 'w_in': 7.242770e-02, 'conv_w': 7.549213e-02, 'hg_lb_logits': 9.142069e-03, 'hg_norm_w': 1.948264e-01, 'gdn_A_log': 8.687794e-02, 'gdn_dt_bias': 8.767929e-02, 'gdn_norm_w': 2.446253e-01, 'w_out': 9.063652e-02, 'final_norm_w': 3.201140e+01}


def _to_microbatches(a, axis):
    t = _jnp.moveaxis(a, axis, 0)
    t = t.reshape((N_MICROBATCH, t.shape[0] // N_MICROBATCH) + t.shape[1:])
    return _jnp.moveaxis(t, 1, axis + 1)


def setup_inputs(seed: int = 0) -> dict:
    inp = _fwd_setup_inputs(seed)
    key = _jax.random.fold_in(_jax.random.key(seed), 7919)
    shape, _ = _output_shape()
    out = dict(inp)
    out["loss_target"] = _jax.random.normal(_jax.random.fold_in(key, 0), shape, _jnp.float32)
    for i, name in enumerate(TWIN_WEIGHTS):
        w = inp[name].astype(_jnp.float32)
        if MOMENT_SCALE is None:
            s = _jnp.sqrt(_jnp.mean(_jnp.square(w)) + 1e-30)
        else:
            s = MOMENT_SCALE[name]
        km, kv = _jax.random.split(_jax.random.fold_in(key, i + 1))
        out[name] = w
        out["m_" + name] = s * _jax.random.normal(km, w.shape, _jnp.float32)
        out["v_" + name] = (s * s) * _jax.random.uniform(kv, w.shape, _jnp.float32, 0.5, 1.5)
    if N_MICROBATCH > 1:
        for name, axis in PER_EXAMPLE_BATCH_AXIS.items():
            out[name] = _to_microbatches(out[name], axis)
    return {'x': out['x'], 'meta_tokens': out['meta_tokens'], 'norm_w': out['norm_w'], 'w_in': out['w_in'], 'conv_w': out['conv_w'], 'hg_lb_logits': out['hg_lb_logits'], 'hg_norm_w': out['hg_norm_w'], 'gdn_A_log': out['gdn_A_log'], 'gdn_dt_bias': out['gdn_dt_bias'], 'gdn_norm_w': out['gdn_norm_w'], 'w_out': out['w_out'], 'final_norm_w': out['final_norm_w'], 'loss_target': out['loss_target'], 'm_meta_tokens': out['m_meta_tokens'], 'm_norm_w': out['m_norm_w'], 'm_w_in': out['m_w_in'], 'm_conv_w': out['m_conv_w'], 'm_hg_lb_logits': out['m_hg_lb_logits'], 'm_hg_norm_w': out['m_hg_norm_w'], 'm_gdn_A_log': out['m_gdn_A_log'], 'm_gdn_dt_bias': out['m_gdn_dt_bias'], 'm_gdn_norm_w': out['m_gdn_norm_w'], 'm_w_out': out['m_w_out'], 'm_final_norm_w': out['m_final_norm_w'], 'v_meta_tokens': out['v_meta_tokens'], 'v_norm_w': out['v_norm_w'], 'v_w_in': out['v_w_in'], 'v_conv_w': out['v_conv_w'], 'v_hg_lb_logits': out['v_hg_lb_logits'], 'v_hg_norm_w': out['v_hg_norm_w'], 'v_gdn_A_log': out['v_gdn_A_log'], 'v_gdn_dt_bias': out['v_gdn_dt_bias'], 'v_gdn_norm_w': out['v_gdn_norm_w'], 'v_w_out': out['v_w_out'], 'v_final_norm_w': out['v_final_norm_w']}


def _loss(weights, diff, rest, loss_target):
    with _jax.named_scope("forward"):
        args = {**rest, TWIN_DIFF_INPUT: diff, **{k: w.astype(_WEIGHT_DTYPES[k]) for k, w in weights.items()}}
        y = _forward(args)
    with _jax.named_scope("loss_head"):
        err = _jnp.square(y.astype(_jnp.float32) - loss_target)
        return 0.5 * _jnp.sum(_jnp.mean(err, axis=-1)) if err.ndim else 0.5 * err


def _adamw(w, g, m, v):
    m = ADAM_B1 * m + (1.0 - ADAM_B1) * g
    v = ADAM_B2 * v + (1.0 - ADAM_B2) * _jnp.square(g)
    m_hat = m / (1.0 - ADAM_B1 ** ADAM_STEP)
    v_hat = v / (1.0 - ADAM_B2 ** ADAM_STEP)
    delta = -ADAM_LR * (m_hat / (_jnp.sqrt(v_hat) + ADAM_EPS) + ADAM_WD * w)
    return delta, m, v


def reference(x, meta_tokens, norm_w, w_in, conv_w, hg_lb_logits, hg_norm_w, gdn_A_log, gdn_dt_bias, gdn_norm_w, w_out, final_norm_w, loss_target, m_meta_tokens, m_norm_w, m_w_in, m_conv_w, m_hg_lb_logits, m_hg_norm_w, m_gdn_A_log, m_gdn_dt_bias, m_gdn_norm_w, m_w_out, m_final_norm_w, v_meta_tokens, v_norm_w, v_w_in, v_conv_w, v_hg_lb_logits, v_hg_norm_w, v_gdn_A_log, v_gdn_dt_bias, v_gdn_norm_w, v_w_out, v_final_norm_w):
    given = dict(x=x, meta_tokens=meta_tokens, norm_w=norm_w, w_in=w_in, conv_w=conv_w, hg_lb_logits=hg_lb_logits, hg_norm_w=hg_norm_w, gdn_A_log=gdn_A_log, gdn_dt_bias=gdn_dt_bias, gdn_norm_w=gdn_norm_w, w_out=w_out, final_norm_w=final_norm_w, loss_target=loss_target, m_meta_tokens=m_meta_tokens, m_norm_w=m_norm_w, m_w_in=m_w_in, m_conv_w=m_conv_w, m_hg_lb_logits=m_hg_lb_logits, m_hg_norm_w=m_hg_norm_w, m_gdn_A_log=m_gdn_A_log, m_gdn_dt_bias=m_gdn_dt_bias, m_gdn_norm_w=m_gdn_norm_w, m_w_out=m_w_out, m_final_norm_w=m_final_norm_w, v_meta_tokens=v_meta_tokens, v_norm_w=v_norm_w, v_w_in=v_w_in, v_conv_w=v_conv_w, v_hg_lb_logits=v_hg_lb_logits, v_hg_norm_w=v_hg_norm_w, v_gdn_A_log=v_gdn_A_log, v_gdn_dt_bias=v_gdn_dt_bias, v_gdn_norm_w=v_gdn_norm_w, v_w_out=v_w_out, v_final_norm_w=v_final_norm_w)
    weights = {n: given[n] for n in TWIN_WEIGHTS}
    shared = {n: given[n] for n in SHARED_INPUTS}
    per_example = {n: given[n] for n in ['x']}
    grad_fn = _jax.value_and_grad(_loss, argnums=(0, 1))

    def one_microbatch(ex, loss_target):
        ex = dict(ex)
        diff = ex.pop(TWIN_DIFF_INPUT)
        return grad_fn(weights, diff, {**shared, **ex}, loss_target)

    if N_MICROBATCH == 1:
        loss, (grad_w, grad_x) = one_microbatch(per_example, given["loss_target"])
    else:
        def body(carry, xs):
            loss_sum, grad_sum = carry
            l_k, (gw_k, gx_k) = one_microbatch(xs[0], xs[1])
            with _jax.named_scope("update"):
                return (loss_sum + l_k, _jax.tree.map(_jnp.add, grad_sum, gw_k)), gx_k

        init = (_jnp.zeros((), _jnp.float32), _jax.tree.map(_jnp.zeros_like, weights))
        (loss, grad_w), grad_x = _jax.lax.scan(body, init, (per_example, given["loss_target"]))
    with _jax.named_scope("update"):
        delta_w, new_m, new_v = {}, {}, {}
        for n in TWIN_WEIGHTS:
            delta_w[n], new_m[n], new_v[n] = _adamw(weights[n], grad_w[n], given["m_" + n], given["v_" + n])
    return (loss, grad_x, *[grad_w[n] for n in TWIN_WEIGHTS], *[delta_w[n] for n in TWIN_WEIGHTS],
            *[new_m[n] for n in TWIN_WEIGHTS], *[new_v[n] for n in TWIN_WEIGHTS])
```

```python
import functools

import jax
import jax.numpy as jnp
from jax import lax
from jax.experimental import pallas as pl
from jax.experimental.pallas import tpu as pltpu

F32 = jnp.float32
BF16 = jnp.bfloat16
MXU_DTYPE = BF16
HI = lax.Precision.HIGHEST

D_MODEL = 1024
N_META = 16
CHUNK = 64
SUB = 16
HEADS = 4
DH = 128
WIDTH = HEADS * DH
QKV = 3 * WIDTH
CONV_TAPS = 4
HALO = 8
EPS = 1e-6
IN_COLS = 4 * WIDTH + 4 * WIDTH + 2 * HEADS
AB_PAD = 128
N_DEV = 8
VMEM_LIMIT = 56 * 1024 * 1024

ADAM_LR = 0.001
ADAM_B1 = 0.9
ADAM_B2 = 0.999
ADAM_EPS = 1e-08
ADAM_WD = 0.01
ADAM_STEP = 10

VMEM_SPEC = pl.BlockSpec(memory_space=pltpu.VMEM)
MESH = pl.DeviceIdType.MESH


def _mm(a, b):
    return jnp.dot(a.astype(MXU_DTYPE), b.astype(MXU_DTYPE), preferred_element_type=F32)


def _mm_nt(a, b):
    return lax.dot_general(a.astype(MXU_DTYPE), b.astype(MXU_DTYPE), (((1,), (1,)), ((), ())), preferred_element_type=F32)


def _mm_tn(a, b):
    return lax.dot_general(a.astype(MXU_DTYPE), b.astype(MXU_DTYPE), (((0,), (0,)), ((), ())), preferred_element_type=F32)


def _mm_hi(a, b):
    return jnp.dot(a, b, precision=HI, preferred_element_type=F32)


def _iota2(n, m):
    return lax.broadcasted_iota(jnp.int32, (n, m), 0), lax.broadcasted_iota(jnp.int32, (n, m), 1)


def _silu(x):
    return x * jax.nn.sigmoid(x)


def _gated_norm(o, z, nw):
    return o * lax.rsqrt(jnp.mean(o * o, axis=-1, keepdims=True) + EPS) * nw * _silu(z)


def _hg_diag(q, k, g, v):
    t = lax.broadcasted_iota(jnp.int32, (SUB, SUB, DH), 0)
    s = lax.broadcasted_iota(jnp.int32, (SUB, SUB, DH), 1)
    dec = jnp.exp(jnp.where(t >= s, g[:, None, :] - g[None, :, :], -jnp.inf))
    a = jnp.sum(q[:, None, :] * dec * k[None, :, :], axis=-1)
    return _mm(a, v)


def hg_chunk(p, st, logits, nw):
    l0, l1 = logits[0:1], logits[1:2]
    mx = jnp.maximum(l0, l1)
    e0, e1 = jnp.exp(l0 - mx), jnp.exp(l1 - mx)
    lb = e0 / (e0 + e1)
    q = _silu(p[:, 0:WIDTH])
    f = lb + (1.0 - lb) * jax.nn.sigmoid(p[:, WIDTH:2 * WIDTH])
    k = 1.0 - f
    logf = jnp.log(f)
    v = p[:, 2 * WIDTH:3 * WIDTH]
    z = p[:, 3 * WIDTH:4 * WIDTH]
    r, c = _iota2(CHUNK, CHUNK)
    same = (r // SUB) == (c // SUB)
    cum = _mm_hi(jnp.where(same & (c <= r), 1.0, 0.0), logf)
    tot = _mm_hi(jnp.where(same, 1.0, 0.0), logf)
    q_in = q * jnp.exp(cum)
    k_out = k * jnp.exp(tot - cum)
    ys, sts = [], []
    for h in range(HEADS):
        hs = slice(h * DH, (h + 1) * DH)
        sh = st[hs]
        outs = []
        for j in range(CHUNK // SUB):
            rs = slice(j * SUB, (j + 1) * SUB)
            o = _mm_nt(q_in[rs, hs], sh) + _hg_diag(q[rs, hs], k[rs, hs], cum[rs, hs], v[rs, hs])
            outs.append(o)
            sh = sh * jnp.exp(tot[j * SUB:j * SUB + 1, hs]) + _mm_tn(v[rs, hs], k_out[rs, hs])
        ys.append(_gated_norm(jnp.concatenate(outs, axis=0), z[:, hs], nw))
        sts.append(sh)
    return jnp.concatenate(ys, axis=1), jnp.concatenate(sts, axis=0)


def _tri_inv_impl(a):
    r, c = _iota2(CHUNK, CHUNK)
    same16 = (r // 16) == (c // 16)
    same32 = (r // 32) == (c // 32)
    a0 = jnp.where(same16, a, 0.0)
    a1 = jnp.where(same32 & jnp.logical_not(same16), a, 0.0)
    a2 = jnp.where(same32, 0.0, a)
    x = jnp.where(r == c, 1.0, 0.0) - a0
    pw = _mm_hi(a0, a0)
    for _ in range(2):
        x = x + _mm_hi(x, pw)
        pw = _mm_hi(pw, pw)
    x = x + _mm_hi(x, pw)
    x = x - _mm_hi(_mm_hi(x, a1), x)
    x = x - _mm_hi(_mm_hi(x, a2), x)
    return x


@jax.custom_vjp
def _tri_inv(a):
    return _tri_inv_impl(a)


def _tri_inv_fwd(a):
    t = _tri_inv_impl(a)
    return t, t


def _tri_inv_bwd(t, dt):
    m = lax.dot_general(t, dt, (((0,), (0,)), ((), ())), precision=HI, preferred_element_type=F32)
    return (-lax.dot_general(m, t, (((1,), (1,)), ((), ())), precision=HI, preferred_element_type=F32),)


_tri_inv.defvjp(_tri_inv_fwd, _tri_inv_bwd)


def gd_chunk(xx, ab, z, s, cw, alog, dtb, nw, inverse=_tri_inv):
    conv = cw[0:1] * xx[HALO - 3:HALO - 3 + CHUNK]
    for j in range(1, CONV_TAPS):
        conv = conv + cw[j:j + 1] * xx[HALO - 3 + j:HALO - 3 + j + CHUNK]
    act = _silu(conv)
    x = ab + dtb
    g_all = -jnp.exp(alog) * (jnp.maximum(x, 0.0) + jnp.log1p(jnp.exp(-jnp.abs(x))))
    beta_all = jax.nn.sigmoid(ab)
    r, c = _iota2(CHUNK, CHUNK)
    low = jnp.where(c <= r, 1.0, 0.0)
    gam_all = _mm_hi(low, g_all)
    glast_all = jnp.sum(g_all, axis=0, keepdims=True)
    strict = c < r
    ys, sts = [], []
    for h in range(HEADS):
        hs = slice(h * DH, (h + 1) * DH)
        q = act[:, hs]
        k = act[:, WIDTH + h * DH:WIDTH + (h + 1) * DH]
        v = act[:, 2 * WIDTH + h * DH:2 * WIDTH + (h + 1) * DH]
        q = q * lax.rsqrt(jnp.sum(q * q, axis=-1, keepdims=True) + EPS) * (DH ** -0.5)
        k = k * lax.rsqrt(jnp.sum(k * k, axis=-1, keepdims=True) + EPS)
        g = g_all[:, h:h + 1]
        beta = beta_all[:, HEADS + h:HEADS + h + 1]
        gam = gam_all[:, h:h + 1]
        glast = glast_all[:, h:h + 1]
        diff = _mm_hi(low, jnp.where(strict, g, 0.0))
        dec = jnp.exp(jnp.where(strict, diff, -jnp.inf))
        dec_incl = jnp.where(r == c, 1.0, dec)
        t_inv = inverse(beta * _mm_nt(k, k) * dec)
        eg = jnp.exp(gam)
        sol = _mm_hi(t_inv, jnp.concatenate([beta * v, (beta * eg) * k], axis=1))
        sh = s[hs]
        u = sol[:, 0:DH] - _mm(sol[:, DH:2 * DH], sh)
        o = _mm(q * eg, sh) + _mm(_mm_nt(q, k) * dec_incl, u)
        sts.append(jnp.exp(glast) * sh + _mm_tn(k * jnp.exp(glast - gam), u))
        ys.append(_gated_norm(o, z[:, hs], nw))
    return jnp.concatenate(ys, axis=1), jnp.concatenate(sts, axis=0)


def _cparams(*sem):
    return pltpu.CompilerParams(dimension_semantics=sem, vmem_limit_bytes=VMEM_LIMIT)


def _row_tile(n):
    for t in (256, 128, 64):
        if n % t == 0:
            return t
    raise ValueError(f"unsupported token count {n}")


def in_proj(h, norm_w, w_hg, w_gd, w_ab, name):
    n = h.shape[0]
    tm = _row_tile(n)

    def body(h_ref, nw_ref, whg_ref, wgd_ref, wab_ref, u_ref, phg_ref, pgd_ref, pab_ref):
        x = h_ref[...]
        u = (x * lax.rsqrt(jnp.mean(x * x, axis=-1, keepdims=True) + EPS) * nw_ref[...]).astype(MXU_DTYPE)
        u_ref[...] = u
        phg_ref[...] = jnp.dot(u, whg_ref[...], preferred_element_type=F32)
        pgd_ref[...] = jnp.dot(u, wgd_ref[...], preferred_element_type=F32)
        pab_ref[...] = jnp.dot(u, wab_ref[...], preferred_element_type=F32)

    row = lambda w: pl.BlockSpec((tm, w), lambda i: (i, 0))
    full = lambda a: pl.BlockSpec(a.shape, lambda i: (0, 0))
    return pl.pallas_call(
        body, grid=(n // tm,), name=name,
        in_specs=[row(D_MODEL), full(norm_w), full(w_hg), full(w_gd), full(w_ab)],
        out_specs=[row(D_MODEL), row(4 * WIDTH), row(4 * WIDTH), row(AB_PAD)],
        out_shape=[jax.ShapeDtypeStruct((n, D_MODEL), MXU_DTYPE), jax.ShapeDtypeStruct((n, 4 * WIDTH), F32),
                   jax.ShapeDtypeStruct((n, 4 * WIDTH), F32), jax.ShapeDtypeStruct((n, AB_PAD), F32)],
        compiler_params=_cparams("arbitrary"),
    )(h, norm_w, w_hg, w_gd, w_ab)


def out_proj_loss(x, tgt, y_hg, y_gd, w_out, fw):
    n = x.shape[0]
    tm = _row_tile(n)
    inv_d = 1.0 / D_MODEL

    def body(x_ref, t_ref, yh_ref, yg_ref, w_ref, fw_ref, dh_ref, dyh_ref, dyg_ref, dw_ref, loss_ref, dfw_ref):
        @pl.when(pl.program_id(0) == 0)
        def _():
            dw_ref[...] = jnp.zeros_like(dw_ref)
            loss_ref[...] = jnp.zeros_like(loss_ref)
            dfw_ref[...] = jnp.zeros_like(dfw_ref)

        yh, yg = yh_ref[...], yg_ref[...]
        wa, wb = w_ref[0:WIDTH, :], w_ref[WIDTH:2 * WIDTH, :]
        h2 = x_ref[...] + jnp.dot(yh, wa, preferred_element_type=F32) + jnp.dot(yg, wb, preferred_element_type=F32)
        r2 = lax.rsqrt(jnp.mean(h2 * h2, axis=-1, keepdims=True) + EPS)
        nrm = h2 * r2
        fwv = fw_ref[...]
        err = nrm * fwv - t_ref[...]
        loss_ref[...] += jnp.full(loss_ref.shape, 0.5 * inv_d * jnp.sum(err * err), F32)
        dout = err * inv_d
        dfw_ref[...] += jnp.sum(dout * nrm, axis=0, keepdims=True)
        dn = dout * fwv
        dh2 = r2 * (dn - nrm * jnp.mean(dn * nrm, axis=-1, keepdims=True))
        dh_ref[...] = dh2
        dhb = dh2.astype(MXU_DTYPE)
        dyh_ref[...] = lax.dot_general(dhb, wa, (((1,), (1,)), ((), ())), preferred_element_type=F32)
        dyg_ref[...] = lax.dot_general(dhb, wb, (((1,), (1,)), ((), ())), preferred_element_type=F32)
        dw_ref[0:WIDTH, :] += lax.dot_general(yh, dhb, (((0,), (0,)), ((), ())), preferred_element_type=F32)
        dw_ref[WIDTH:2 * WIDTH, :] += lax.dot_general(yg, dhb, (((0,), (0,)), ((), ())), preferred_element_type=F32)

    row = lambda w: pl.BlockSpec((tm, w), lambda i: (i, 0))
    full = lambda s: pl.BlockSpec(s, lambda i: (0, 0))
    return pl.pallas_call(
        body, grid=(n // tm,), name="out_proj_loss",
        in_specs=[row(D_MODEL), row(D_MODEL), row(WIDTH), row(WIDTH), full(w_out.shape), full(fw.shape)],
        out_specs=[row(D_MODEL), row(WIDTH), row(WIDTH), full((2 * WIDTH, D_MODEL)), full((8, 128)), full((1, D_MODEL))],
        out_shape=[jax.ShapeDtypeStruct((n, D_MODEL), F32), jax.ShapeDtypeStruct((n, WIDTH), F32),
                   jax.ShapeDtypeStruct((n, WIDTH), F32), jax.ShapeDtypeStruct((2 * WIDTH, D_MODEL), F32),
                   jax.ShapeDtypeStruct((8, 128), F32), jax.ShapeDtypeStruct((1, D_MODEL), F32)],
        compiler_params=_cparams("arbitrary"),
    )(x, tgt, y_hg, y_gd, w_out, fw)


def in_proj_bwd(dphg, dpgd, dpab, w_hg, w_gd, w_ab, h, dh2, norm_w, name):
    n = h.shape[0]
    tm = _row_tile(n)

    def body(dphg_ref, dpgd_ref, dpab_ref, whg_ref, wgd_ref, wab_ref, h_ref, dh2_ref, nw_ref, dx_ref, dnw_ref):
        @pl.when(pl.program_id(0) == 0)
        def _():
            dnw_ref[...] = jnp.zeros_like(dnw_ref)

        nt = (((1,), (1,)), ((), ()))
        du = lax.dot_general(dphg_ref[...].astype(MXU_DTYPE), whg_ref[...], nt, preferred_element_type=F32)
        du += lax.dot_general(dpgd_ref[...].astype(MXU_DTYPE), wgd_ref[...], nt, preferred_element_type=F32)
        du += lax.dot_general(dpab_ref[...].astype(MXU_DTYPE), wab_ref[...], nt, preferred_element_type=F32)
        x = h_ref[...]
        r = lax.rsqrt(jnp.mean(x * x, axis=-1, keepdims=True) + EPS)
        nrm = x * r
        dnw_ref[...] += jnp.sum(du * nrm, axis=0, keepdims=True)
        dn = du * nw_ref[...]
        dx_ref[...] = dh2_ref[...] + r * (dn - nrm * jnp.mean(dn * nrm, axis=-1, keepdims=True))

    row = lambda w: pl.BlockSpec((tm, w), lambda i: (i, 0))
    full = lambda a: pl.BlockSpec(a.shape, lambda i: (0, 0))
    return pl.pallas_call(
        body, grid=(n // tm,), name=name,
        in_specs=[row(4 * WIDTH), row(4 * WIDTH), row(AB_PAD), full(w_hg), full(w_gd), full(w_ab), row(D_MODEL),
                  row(D_MODEL), full(norm_w)],
        out_specs=[row(D_MODEL), pl.BlockSpec((1, D_MODEL), lambda i: (0, 0))],
        out_shape=[jax.ShapeDtypeStruct((n, D_MODEL), F32), jax.ShapeDtypeStruct((1, D_MODEL), F32)],
        compiler_params=_cparams("arbitrary"),
    )(dphg, dpgd, dpab, w_hg, w_gd, w_ab, h, dh2, norm_w)


def weight_grad(u, dp, u0, dp0, name):
    n, w = dp.shape
    tn = min(w, 1024)
    tm = 1024 if n % 1024 == 0 else _row_tile(n)
    n0 = u0.shape[0]

    def body(u_ref, dp_ref, u0_ref, dp0_ref, o_ref):
        @pl.when(pl.program_id(1) == 0)
        def _():
            o_ref[...] = _mm_tn(u0_ref[...], dp0_ref[...])

        o_ref[...] += _mm_tn(u_ref[...], dp_ref[...])

    return pl.pallas_call(
        body, grid=(w // tn, n // tm), name=name,
        in_specs=[pl.BlockSpec((tm, D_MODEL), lambda j, t: (t, 0)), pl.BlockSpec((tm, tn), lambda j, t: (t, j)),
                  pl.BlockSpec((n0, D_MODEL), lambda j, t: (0, 0)), pl.BlockSpec((n0, tn), lambda j, t: (0, j))],
        out_specs=pl.BlockSpec((D_MODEL, tn), lambda j, t: (0, j)),
        out_shape=jax.ShapeDtypeStruct((D_MODEL, w), F32),
        compiler_params=_cparams("arbitrary", "arbitrary"),
    )(u, dp, u0, dp0)


def _real(c):
    return jnp.maximum(c - 1, 0)


def hg_forward(p, p0, logits, nw):
    b, seq, _ = p.shape
    nc = seq // CHUNK + 1

    def body(p_ref, p0_ref, lg_ref, nw_ref, y_ref, ssave_ref, st):
        c = pl.program_id(1)

        @pl.when(c == 0)
        def _():
            st[...] = jnp.zeros_like(st)

        s_in = st[...]
        ssave_ref[...] = s_in
        y, s_new = hg_chunk(jnp.where(c == 0, p0_ref[...], p_ref[...]), s_in, lg_ref[...], nw_ref[...])
        y_ref[...] = y.astype(MXU_DTYPE)
        st[...] = s_new

    const = lambda a: pl.BlockSpec(a.shape, lambda s, c: (0, 0))
    return pl.pallas_call(
        body, grid=(b, nc), name="hgrn2_forward",
        in_specs=[pl.BlockSpec((None, CHUNK, 4 * WIDTH), lambda s, c: (s, _real(c), 0)), const(p0), const(logits), const(nw)],
        out_specs=[pl.BlockSpec((None, CHUNK, WIDTH), lambda s, c: (s, _real(c), 0)),
                   pl.BlockSpec((None, None, WIDTH, DH), lambda s, c: (s, c, 0, 0))],
        out_shape=[jax.ShapeDtypeStruct((b, seq, WIDTH), MXU_DTYPE), jax.ShapeDtypeStruct((b, nc, WIDTH, DH), F32)],
        scratch_shapes=[pltpu.VMEM((WIDTH, DH), F32)],
        compiler_params=_cparams("arbitrary", "arbitrary"),
    )(p, p0, logits, nw)


def hg_backward(p, p0, ssave, logits, nw, dy):
    b, seq, _ = p.shape
    nc = seq // CHUNK + 1

    def body(p_ref, p0_ref, ss_ref, lg_ref, nw_ref, dy_ref, dp_ref, dp0_ref, dlg_ref, dnw_ref, dst):
        i = pl.program_id(1)
        c = nc - 1 - i

        @pl.when(i == 0)
        def _():
            dst[...] = jnp.zeros_like(dst)

        @pl.when((pl.program_id(0) == 0) & (i == 0))
        def _():
            dlg_ref[...] = jnp.zeros_like(dlg_ref)
            dnw_ref[...] = jnp.zeros_like(dnw_ref)

        pin = jnp.where(c == 0, p0_ref[...], p_ref[...])
        dyv = jnp.where(c == 0, 0.0, dy_ref[...])
        _, vjp = jax.vjp(hg_chunk, pin, ss_ref[...], lg_ref[...], nw_ref[...])
        dp, ds, dlg, dnw = vjp((dyv, dst[...]))
        dst[...] = ds
        dlg_ref[...] += dlg
        dnw_ref[...] += dnw

        @pl.when(c > 0)
        def _():
            dp_ref[...] = dp

        @pl.when(c == 0)
        def _():
            dp0_ref[...] = dp

    rc = lambda i: _real(nc - 1 - i)
    const = lambda a: pl.BlockSpec(a.shape, lambda s, i: (0, 0))
    return pl.pallas_call(
        body, grid=(b, nc), name="hgrn2_backward",
        in_specs=[pl.BlockSpec((None, CHUNK, 4 * WIDTH), lambda s, i: (s, rc(i), 0)), const(p0),
                  pl.BlockSpec((None, None, WIDTH, DH), lambda s, i: (s, nc - 1 - i, 0, 0)), const(logits), const(nw),
                  pl.BlockSpec((None, CHUNK, WIDTH), lambda s, i: (s, rc(i), 0))],
        out_specs=[pl.BlockSpec((None, CHUNK, 4 * WIDTH), lambda s, i: (s, rc(i), 0)),
                   pl.BlockSpec((None, CHUNK, 4 * WIDTH), lambda s, i: (s, 0, 0)), const(logits), const(nw)],
        out_shape=[jax.ShapeDtypeStruct((b, seq, 4 * WIDTH), F32), jax.ShapeDtypeStruct((b, CHUNK, 4 * WIDTH), F32),
                   jax.ShapeDtypeStruct(logits.shape, F32), jax.ShapeDtypeStruct(nw.shape, F32)],
        scratch_shapes=[pltpu.VMEM((WIDTH, DH), F32)],
        compiler_params=_cparams("arbitrary", "arbitrary"),
    )(p, p0, ssave, logits, nw, dy)


def _gd_inputs(c, p_ref, halo_ref, p0_ref, ab_ref, ab0_ref):
    pin = jnp.where(c == 0, p0_ref[...], p_ref[...])
    ab = jnp.where(c == 0, ab0_ref[...], ab_ref[...])
    halo = jnp.where(c == 0, 0.0, jnp.where(c == 1, p0_ref[CHUNK - HALO:CHUNK, 0:QKV], halo_ref[...]))
    return jnp.concatenate([halo, pin[:, 0:QKV]], axis=0), ab, pin[:, QKV:QKV + WIDTH]


def _halo_block(c):
    return jnp.maximum((CHUNK // HALO) * (c - 1) - 1, 0)


def gd_forward(p, p0, ab, ab0, cw, alog, dtb, nw):
    b, seq, _ = p.shape
    nc = seq // CHUNK + 1

    def body(p_ref, halo_ref, p0_ref, ab_ref, ab0_ref, cw_ref, al_ref, dt_ref, nw_ref, y_ref, ssave_ref, st):
        c = pl.program_id(1)

        @pl.when(c == 0)
        def _():
            st[...] = jnp.zeros_like(st)

        s_in = st[...]
        ssave_ref[...] = s_in
        xx, abv, z = _gd_inputs(c, p_ref, halo_ref, p0_ref, ab_ref, ab0_ref)
        y, s_new = gd_chunk(xx, abv, z, s_in, cw_ref[...], al_ref[...], dt_ref[...], nw_ref[...], inverse=_tri_inv_impl)
        y_ref[...] = y.astype(MXU_DTYPE)
        st[...] = s_new

    const = lambda a: pl.BlockSpec(a.shape, lambda s, c: (0, 0))
    return pl.pallas_call(
        body, grid=(b, nc), name="gdn_forward",
        in_specs=[pl.BlockSpec((None, CHUNK, 4 * WIDTH), lambda s, c: (s, _real(c), 0)),
                  pl.BlockSpec((None, HALO, QKV), lambda s, c: (s, _halo_block(c), 0)), const(p0),
                  pl.BlockSpec((None, CHUNK, AB_PAD), lambda s, c: (s, _real(c), 0)), const(ab0), const(cw), const(alog),
                  const(dtb), const(nw)],
        out_specs=[pl.BlockSpec((None, CHUNK, WIDTH), lambda s, c: (s, _real(c), 0)),
                   pl.BlockSpec((None, None, WIDTH, DH), lambda s, c: (s, c, 0, 0))],
        out_shape=[jax.ShapeDtypeStruct((b, seq, WIDTH), MXU_DTYPE), jax.ShapeDtypeStruct((b, nc, WIDTH, DH), F32)],
        scratch_shapes=[pltpu.VMEM((WIDTH, DH), F32)],
        compiler_params=_cparams("arbitrary", "arbitrary"),
    )(p, p, p0, ab, ab0, cw, alog, dtb, nw)


def gd_backward(p, p0, ab, ab0, ssave, cw, alog, dtb, nw, dy):
    b, seq, _ = p.shape
    nc = seq // CHUNK + 1

    def body(p_ref, halo_ref, p0_ref, ab_ref, ab0_ref, ss_ref, cw_ref, al_ref, dt_ref, nw_ref, dy_ref,
             dp_ref, dp0_ref, dab_ref, dab0_ref, dcw_ref, dal_ref, ddt_ref, dnw_ref, dst, dhalo):
        i = pl.program_id(1)
        c = nc - 1 - i

        @pl.when(i == 0)
        def _():
            dst[...] = jnp.zeros_like(dst)
            dhalo[...] = jnp.zeros_like(dhalo)

        @pl.when((pl.program_id(0) == 0) & (i == 0))
        def _():
            dcw_ref[...] = jnp.zeros_like(dcw_ref)
            dal_ref[...] = jnp.zeros_like(dal_ref)
            ddt_ref[...] = jnp.zeros_like(ddt_ref)
            dnw_ref[...] = jnp.zeros_like(dnw_ref)

        xx, abv, z = _gd_inputs(c, p_ref, halo_ref, p0_ref, ab_ref, ab0_ref)
        dyv = jnp.where(c == 0, 0.0, dy_ref[...])
        _, vjp = jax.vjp(gd_chunk, xx, abv, z, ss_ref[...], cw_ref[...], al_ref[...], dt_ref[...], nw_ref[...])
        dxx, dab, dz, ds, dcw, dal, ddt, dnw = vjp((dyv, dst[...]))
        dst[...] = ds
        dcw_ref[...] += dcw
        dal_ref[...] += dal
        ddt_ref[...] += ddt
        dnw_ref[...] += dnw
        dqkv = dxx[HALO:HALO + CHUNK] + jnp.concatenate([jnp.zeros((CHUNK - HALO, QKV), F32), dhalo[...]], axis=0)
        dhalo[...] = dxx[0:HALO]
        dp = jnp.concatenate([dqkv, dz], axis=1)

        @pl.when(c > 0)
        def _():
            dp_ref[...] = dp
            dab_ref[...] = dab

        @pl.when(c == 0)
        def _():
            dp0_ref[...] = dp
            dab0_ref[...] = dab

    rc = lambda i: _real(nc - 1 - i)
    const = lambda a: pl.BlockSpec(a.shape, lambda s, i: (0, 0))
    return pl.pallas_call(
        body, grid=(b, nc), name="gdn_backward",
        in_specs=[pl.BlockSpec((None, CHUNK, 4 * WIDTH), lambda s, i: (s, rc(i), 0)),
                  pl.BlockSpec((None, HALO, QKV), lambda s, i: (s, _halo_block(nc - 1 - i), 0)), const(p0),
                  pl.BlockSpec((None, CHUNK, AB_PAD), lambda s, i: (s, rc(i), 0)), const(ab0),
                  pl.BlockSpec((None, None, WIDTH, DH), lambda s, i: (s, nc - 1 - i, 0, 0)), const(cw), const(alog),
                  const(dtb), const(nw), pl.BlockSpec((None, CHUNK, WIDTH), lambda s, i: (s, rc(i), 0))],
        out_specs=[pl.BlockSpec((None, CHUNK, 4 * WIDTH), lambda s, i: (s, rc(i), 0)),
                   pl.BlockSpec((None, CHUNK, 4 * WIDTH), lambda s, i: (s, 0, 0)),
                   pl.BlockSpec((None, CHUNK, AB_PAD), lambda s, i: (s, rc(i), 0)),
                   pl.BlockSpec((None, CHUNK, AB_PAD), lambda s, i: (s, 0, 0)), const(cw), const(alog), const(dtb), const(nw)],
        out_shape=[jax.ShapeDtypeStruct((b, seq, 4 * WIDTH), F32), jax.ShapeDtypeStruct((b, CHUNK, 4 * WIDTH), F32),
                   jax.ShapeDtypeStruct((b, seq, AB_PAD), F32), jax.ShapeDtypeStruct((b, CHUNK, AB_PAD), F32),
                   jax.ShapeDtypeStruct(cw.shape, F32), jax.ShapeDtypeStruct(alog.shape, F32),
                   jax.ShapeDtypeStruct(dtb.shape, F32), jax.ShapeDtypeStruct(nw.shape, F32)],
        scratch_shapes=[pltpu.VMEM((WIDTH, DH), F32), pltpu.VMEM((HALO, QKV), F32)],
        compiler_params=_cparams("arbitrary", "arbitrary"),
    )(p, p, p0, ab, ab0, ssave, cw, alog, dtb, nw, dy)


def _position():
    return lax.axis_index("x"), lax.axis_index("y"), lax.axis_index("c")


def all_gather(shard, dtype, name, reduce=False):
    rows, cols = shard.shape

    def body(x_ref, *rest):
        if reduce:
            sum_ref, out_ref, send_sems, recv_sems = rest
        else:
            out_ref, send_sems, recv_sems = rest
        x, y, c = _position()
        me, sibling = (x, y, c), (x, y, 1 - c)
        chips = [(1 - x, y), (x, 1 - y), (1 - x, 1 - y)]

        def block(px, py, pc):
            return out_ref.at[4 * px + 2 * py + pc]

        def copy(k, blk, to):
            return pltpu.make_async_remote_copy(src_ref=block(*blk), dst_ref=block(*blk), send_sem=send_sems.at[k],
                                                recv_sem=recv_sems.at[k], device_id=to, device_id_type=MESH)

        out_ref[4 * x + 2 * y + c] = x_ref[...].astype(dtype)
        first = [copy(0, me, sibling)] + [copy(1 + j, me, (*chip, c)) for j, chip in enumerate(chips)]
        for cp in first:
            cp.start()
        passed = [copy(4 + j, (*chip, c), sibling) for j, chip in enumerate(chips)]
        for j, chip in enumerate(chips):
            copy(1 + j, (*chip, c), me).wait_recv()
            passed[j].start()
        copy(0, sibling, me).wait_recv()
        for j, chip in enumerate(chips):
            copy(4 + j, (*chip, 1 - c), me).wait_recv()
        for cp in first + passed:
            cp.wait_send()
        if reduce:
            acc = out_ref[0].astype(F32)
            for d in range(1, N_DEV):
                acc = acc + out_ref[d].astype(F32)
            sum_ref[...] = acc

    gathered = jax.ShapeDtypeStruct((N_DEV, rows, cols), dtype)
    sems = [pltpu.SemaphoreType.DMA((N_DEV - 1,)), pltpu.SemaphoreType.DMA((N_DEV - 1,))]
    if reduce:
        return pl.pallas_call(
            body, name=name, in_specs=[VMEM_SPEC], out_specs=VMEM_SPEC,
            out_shape=jax.ShapeDtypeStruct((rows, cols), F32),
            scratch_shapes=[pltpu.VMEM((N_DEV, rows, cols), dtype)] + sems,
            compiler_params=pltpu.CompilerParams(vmem_limit_bytes=VMEM_LIMIT),
        )(shard)
    return pl.pallas_call(
        body, name=name, in_specs=[VMEM_SPEC], out_specs=VMEM_SPEC, out_shape=gathered, scratch_shapes=sems,
        compiler_params=pltpu.CompilerParams(vmem_limit_bytes=VMEM_LIMIT),
    )(shard)


def reduce_scatter(blocks, name):
    _, rows, cols = blocks.shape

    def body(b_ref, o_ref, recv, send_sems, recv_sems):
        x, y, c = _position()
        copies = []
        for m in range(1, N_DEV):
            px = 1 - x if m & 4 else x
            py = 1 - y if m & 2 else y
            pc = 1 - c if m & 1 else c
            cp = pltpu.make_async_remote_copy(src_ref=b_ref.at[4 * px + 2 * py + pc], dst_ref=recv.at[m - 1],
                                              send_sem=send_sems.at[m - 1], recv_sem=recv_sems.at[m - 1],
                                              device_id=(px, py, pc), device_id_type=MESH)
            cp.start()
            copies.append(cp)
        acc = b_ref[4 * x + 2 * y + c].astype(F32)
        for m in range(1, N_DEV):
            copies[m - 1].wait_recv()
            acc = acc + recv[m - 1].astype(F32)
        for cp in copies:
            cp.wait_send()
        o_ref[...] = acc

    return pl.pallas_call(
        body, name=name, in_specs=[VMEM_SPEC], out_specs=VMEM_SPEC, out_shape=jax.ShapeDtypeStruct((rows, cols), F32),
        scratch_shapes=[pltpu.VMEM((N_DEV - 1, rows, cols), blocks.dtype), pltpu.SemaphoreType.DMA((N_DEV - 1,)),
                        pltpu.SemaphoreType.DMA((N_DEV - 1,))],
        compiler_params=pltpu.CompilerParams(vmem_limit_bytes=VMEM_LIMIT),
    )(blocks)


def adamw(w, g, m, v, name):
    rows, cols = w.shape
    tr = 256 if rows % 256 == 0 else rows

    def body(w_ref, g_ref, m_ref, v_ref, d_ref, nm_ref, nv_ref):
        gv = g_ref[...]
        mn = ADAM_B1 * m_ref[...] + (1.0 - ADAM_B1) * gv
        vn = ADAM_B2 * v_ref[...] + (1.0 - ADAM_B2) * jnp.square(gv)
        m_hat = mn / (1.0 - ADAM_B1 ** ADAM_STEP)
        v_hat = vn / (1.0 - ADAM_B2 ** ADAM_STEP)
        d_ref[...] = -ADAM_LR * (m_hat / (jnp.sqrt(v_hat) + ADAM_EPS) + ADAM_WD * w_ref[...])
        nm_ref[...] = mn
        nv_ref[...] = vn

    spec = pl.BlockSpec((tr, cols), lambda i: (i, 0))
    shape = jax.ShapeDtypeStruct((rows, cols), F32)
    return pl.pallas_call(body, grid=(rows // tr,), name=name, in_specs=[spec] * 4, out_specs=[spec] * 3,
                          out_shape=[shape] * 3, compiler_params=_cparams("arbitrary"))(w, g, m, v)


def _pad_rows(a, rows=8):
    return jnp.pad(a, ((0, rows - a.shape[0]), (0, 0)))


def _pad_lanes(a, lanes=128):
    return jnp.pad(a, ((0, 0), (0, lanes - a.shape[1])))


def kernel(x, meta_tokens, norm_w, w_in, conv_w, hg_lb_logits, hg_norm_w, gdn_A_log, gdn_dt_bias, gdn_norm_w, w_out, final_norm_w, loss_target, m_meta_tokens, m_norm_w, m_w_in, m_conv_w, m_hg_lb_logits, m_hg_norm_w, m_gdn_A_log, m_gdn_dt_bias, m_gdn_norm_w, m_w_out, m_final_norm_w, v_meta_tokens, v_norm_w, v_w_in, v_conv_w, v_hg_lb_logits, v_hg_norm_w, v_gdn_A_log, v_gdn_dt_bias, v_gdn_norm_w, v_w_out, v_final_norm_w):
    b, seq, _ = x.shape
    n = b * seq
    dev = 4 * lax.axis_index("x") + 2 * lax.axis_index("y") + lax.axis_index("c")
    col_shard = IN_COLS // N_DEV

    w_in_g = all_gather(w_in[0], MXU_DTYPE, "gather_w_in")
    w_out_g = all_gather(w_out[0], MXU_DTYPE, "gather_w_out")
    small_w = jnp.concatenate([_pad_lanes(meta_tokens, 256), _pad_rows(_pad_lanes(conv_w[0], 256))], axis=0)
    small_g = all_gather(small_w, F32, "gather_small")
    meta_g = small_g[:, 0:N_META, 0:D_MODEL // N_DEV]
    conv_g = small_g[:, N_META:N_META + CONV_TAPS, 0:QKV // N_DEV]
    w_in_full = jnp.transpose(w_in_g, (1, 0, 2)).reshape(D_MODEL, IN_COLS)
    w_hg = w_in_full[:, 0:4 * WIDTH]
    w_gd = w_in_full[:, 4 * WIDTH:8 * WIDTH]
    w_ab = _pad_lanes(w_in_full[:, 8 * WIDTH:IN_COLS])
    w_out_full = w_out_g.reshape(2 * WIDTH, D_MODEL)
    cw = jnp.transpose(conv_g, (1, 0, 2)).reshape(CONV_TAPS, QKV)
    meta = jnp.transpose(meta_g, (1, 0, 2)).reshape(N_META, D_MODEL)
    alog = _pad_lanes(gdn_A_log)
    dtb = _pad_lanes(gdn_dt_bias)
    fw = final_norm_w.reshape(1, D_MODEL)

    h0 = jnp.concatenate([jnp.zeros((CHUNK - N_META, D_MODEL), F32), meta], axis=0)
    x2 = x.reshape(n, D_MODEL)
    u0, phg0, pgd0, pab0 = in_proj(h0, norm_w, w_hg, w_gd, w_ab, "in_proj_lead")
    u, phg, pgd, pab = in_proj(x2, norm_w, w_hg, w_gd, w_ab, "in_proj")
    phg3, pgd3, pab3 = phg.reshape(b, seq, 4 * WIDTH), pgd.reshape(b, seq, 4 * WIDTH), pab.reshape(b, seq, AB_PAD)
    y_hg, s_hg = hg_forward(phg3, phg0, hg_lb_logits, hg_norm_w)
    y_gd, s_gd = gd_forward(pgd3, pgd0, pab3, pab0, cw, alog, dtb, gdn_norm_w)

    dh2, dy_hg, dy_gd, g_w_out, loss_part, g_fw = out_proj_loss(
        x2, loss_target.reshape(n, D_MODEL), y_hg.reshape(n, WIDTH), y_gd.reshape(n, WIDTH), w_out_full, fw)

    dphg, dphg0, g_lb, g_hg_nw = hg_backward(phg3, phg0, s_hg, hg_lb_logits, hg_norm_w, dy_hg.reshape(b, seq, WIDTH))
    dpgd, dpgd0, dpab, dpab0, g_cw, g_alog, g_dtb, g_gd_nw = gd_backward(
        pgd3, pgd0, pab3, pab0, s_gd, cw, alog, dtb, gdn_norm_w, dy_gd.reshape(b, seq, WIDTH))
    dphg0, dpgd0, dpab0 = dphg0.sum(0), dpgd0.sum(0), dpab0.sum(0)
    dphg, dpgd, dpab = dphg.reshape(n, 4 * WIDTH), dpgd.reshape(n, 4 * WIDTH), dpab.reshape(n, AB_PAD)

    grad_x, g_nw = in_proj_bwd(dphg, dpgd, dpab, w_hg, w_gd, w_ab, x2, dh2, norm_w, "in_proj_bwd")
    dh0, g_nw0 = in_proj_bwd(dphg0, dpgd0, dpab0, w_hg, w_gd, w_ab, h0, jnp.zeros_like(h0), norm_w, "in_proj_bwd_lead")
    g_w_hg = weight_grad(u, dphg, u0, dphg0, "w_in_grad_hg")
    g_w_gd = weight_grad(u, dpgd, u0, dpgd0, "w_in_grad_gd")
    g_w_ab = weight_grad(u, dpab, u0, dpab0, "w_in_grad_ab")

    g_w_in_full = jnp.concatenate([g_w_hg, g_w_gd, g_w_ab[:, 0:2 * HEADS]], axis=1)
    g_w_in_blocks = jnp.transpose(g_w_in_full.reshape(D_MODEL, N_DEV, col_shard), (1, 0, 2)).astype(MXU_DTYPE)
    g_w_in = reduce_scatter(g_w_in_blocks, "reduce_w_in")
    g_w_out = reduce_scatter(g_w_out.reshape(N_DEV, (2 * WIDTH) // N_DEV, D_MODEL).astype(MXU_DTYPE), "reduce_w_out")
    small = jnp.concatenate([
        (g_nw + g_nw0).reshape(8, 128), g_lb.reshape(8, 128), _pad_rows(g_hg_nw), _pad_rows(g_alog), _pad_rows(g_dtb),
        _pad_rows(g_gd_nw), g_fw.reshape(8, 128), g_cw.reshape(48, 128), dh0[CHUNK - N_META:CHUNK].reshape(128, 128),
        loss_part], axis=0)
    small = all_gather(small, F32, "reduce_small", reduce=True)
    g_norm_w = small[0:8].reshape(1, D_MODEL)
    g_lb = small[8:16].reshape(2, WIDTH)
    g_hg_nw = small[16:17]
    g_alog = small[24:25, 0:HEADS]
    g_dtb = small[32:33, 0:HEADS]
    g_gd_nw = small[40:41]
    g_fw = small[48:56].reshape(1, D_MODEL)
    g_cw_full = small[56:104].reshape(CONV_TAPS, QKV)
    g_meta_full = small[104:232].reshape(N_META, D_MODEL)
    loss = small[232, 0]
    g_conv = lax.dynamic_slice_in_dim(g_cw_full, dev * (QKV // N_DEV), QKV // N_DEV, axis=1)
    g_meta = lax.dynamic_slice_in_dim(g_meta_full, dev * (D_MODEL // N_DEV), D_MODEL // N_DEV, axis=1)

    names = ["meta_tokens", "norm_w", "w_in", "conv_w", "hg_lb_logits", "hg_norm_w", "gdn_A_log", "gdn_dt_bias",
             "gdn_norm_w", "w_out", "final_norm_w"]
    weights = [meta_tokens, norm_w, w_in, conv_w, hg_lb_logits, hg_norm_w, gdn_A_log, gdn_dt_bias, gdn_norm_w, w_out,
               final_norm_w]
    moms = [m_meta_tokens, m_norm_w, m_w_in, m_conv_w, m_hg_lb_logits, m_hg_norm_w, m_gdn_A_log, m_gdn_dt_bias,
            m_gdn_norm_w, m_w_out, m_final_norm_w]
    vars_ = [v_meta_tokens, v_norm_w, v_w_in, v_conv_w, v_hg_lb_logits, v_hg_norm_w, v_gdn_A_log, v_gdn_dt_bias,
             v_gdn_norm_w, v_w_out, v_final_norm_w]
    grads2d = [g_meta, g_norm_w, g_w_in, g_conv, g_lb, g_hg_nw, g_alog, g_dtb, g_gd_nw, g_w_out, g_fw]
    grads, deltas, new_ms, new_vs = [], [], [], []
    for nm, w, g2, m, v in zip(names, weights, grads2d, moms, vars_):
        shape2 = g2.shape
        d, nm_, nv_ = adamw(w.reshape(shape2), g2, m.reshape(shape2), v.reshape(shape2), "adamw_" + nm)
        grads.append(g2.reshape(w.shape))
        deltas.append(d.reshape(w.shape))
        new_ms.append(nm_.reshape(w.shape))
        new_vs.append(nv_.reshape(w.shape))
    return (loss, grad_x.reshape(x.shape), *grads, *deltas, *new_ms, *new_vs)
```

```python
import jax
import jax.numpy as jnp
from jax import lax
from jax.experimental import pallas as pl
from jax.experimental.pallas import tpu as pltpu

F32 = jnp.float32
BF16 = jnp.bfloat16
MXU_DTYPE = BF16

D_MODEL = 1024
N_META = 16
CHUNK = 64
SUB = 16
HEADS = 4
DH = 128
WIDTH = HEADS * DH
QKV = 3 * WIDTH
CONV_TAPS = 4
HALO = 8
EPS = 1e-6
IN_COLS = 4 * WIDTH + 4 * WIDTH + 2 * HEADS
AB_PAD = 128
N_DEV = 8
LOCAL_CHUNKS = 2
VMEM_LIMIT = 56 * 1024 * 1024

ADAM_LR = 0.001
ADAM_B1 = 0.9
ADAM_B2 = 0.999
ADAM_EPS = 1e-08
ADAM_WD = 0.01
ADAM_STEP = 10

VMEM_SPEC = pl.BlockSpec(memory_space=pltpu.VMEM)
MESH = pl.DeviceIdType.MESH


def _mm_tn(a, b):
    return lax.dot_general(a.astype(MXU_DTYPE), b.astype(MXU_DTYPE), (((0,), (0,)), ((), ())), preferred_element_type=F32)


def _bmm(a, b):
    return lax.dot_general(a.astype(MXU_DTYPE), b.astype(MXU_DTYPE), (((2,), (1,)), ((0,), (0,))), preferred_element_type=F32)


def _bmm_nt(a, b):
    return lax.dot_general(a.astype(MXU_DTYPE), b.astype(MXU_DTYPE), (((2,), (2,)), ((0,), (0,))), preferred_element_type=F32)


def _bmm_tn(a, b):
    return lax.dot_general(a.astype(MXU_DTYPE), b.astype(MXU_DTYPE), (((1,), (1,)), ((0,), (0,))), preferred_element_type=F32)


def _iota2(n, m):
    return lax.broadcasted_iota(jnp.int32, (n, m), 0), lax.broadcasted_iota(jnp.int32, (n, m), 1)


def _silu(x):
    return x * jax.nn.sigmoid(x)


def _gated_norm(o, z, nw):
    return o * lax.rsqrt(jnp.mean(o * o, axis=-1, keepdims=True) + EPS) * nw * _silu(z)


def _heads(a, nb):
    return jnp.stack([a[c * CHUNK:(c + 1) * CHUNK, h * DH:(h + 1) * DH] for c in range(nb) for h in range(HEADS)], axis=0)


def _unheads(a3, nb):
    return jnp.concatenate(
        [jnp.concatenate([a3[c * HEADS + h] for h in range(HEADS)], axis=1) for c in range(nb)], axis=0)


def _ones_mm(upper, x):
    r, c = _iota2(CHUNK, 3 * CHUNK)
    cc = c & (CHUNK - 1)
    ones = jnp.where((cc >= r) if upper else (cc <= r), 1.0, 0.0).astype(BF16)
    hi = x.astype(BF16)
    r1 = x - hi.astype(F32)
    mid = r1.astype(BF16)
    lo = (r1 - mid.astype(F32)).astype(BF16)
    return jnp.dot(ones, jnp.concatenate([hi, mid, lo], axis=0), preferred_element_type=F32)


@jax.custom_vjp
def _cumsum(x):
    return _ones_mm(False, x)


_cumsum.defvjp(lambda x: (_ones_mm(False, x), None), lambda _, d: (_ones_mm(True, d),))


def _cumsum_chunks(x, nb):
    return jnp.concatenate([_cumsum(x[c * CHUNK:(c + 1) * CHUNK]) for c in range(nb)], axis=0)


def _hg_diag(q, k, g):
    t = lax.broadcasted_iota(jnp.int32, (SUB, SUB, DH), 0)
    s = lax.broadcasted_iota(jnp.int32, (SUB, SUB, DH), 1)
    dec = jnp.exp(jnp.where(t >= s, g[:, None, :] - g[None, :, :], -jnp.inf))
    return jnp.sum(q[:, None, :] * dec * k[None, :, :], axis=-1)


def hg_local(p, logits):
    nb = p.shape[0] // CHUNK
    l0, l1 = logits[0:1], logits[1:2]
    mx = jnp.maximum(l0, l1)
    e0, e1 = jnp.exp(l0 - mx), jnp.exp(l1 - mx)
    lb = e0 / (e0 + e1)
    q = _silu(p[:, 0:WIDTH])
    f = lb + (1.0 - lb) * jax.nn.sigmoid(p[:, WIDTH:2 * WIDTH])
    k = 1.0 - f
    cum = _cumsum_chunks(jnp.log(f), nb)
    q3, k3, v3, g3 = _heads(q, nb), _heads(k, nb), _heads(p[:, 2 * WIDTH:3 * WIDTH], nb), _heads(cum, nb)
    ng = nb * HEADS
    glast = g3[:, CHUNK - 1:CHUNK, :]
    _, col = _iota2(SUB, CHUNK)
    rows = [jnp.zeros((ng, SUB, CHUNK), F32)]
    for i in range(1, CHUNK // SUB):
        ref = g3[:, i * SUB - 1:i * SUB, :]
        qt = q3[:, i * SUB:(i + 1) * SUB, :] * jnp.exp(g3[:, i * SUB:(i + 1) * SUB, :] - ref)
        kt = k3 * jnp.exp(jnp.minimum(ref - g3, 0.0))
        rows.append(jnp.where(col < i * SUB, _bmm_nt(qt, kt), 0.0))
    o = _bmm(jnp.concatenate(rows, axis=1), v3)
    blocks = [(g, slice(i * SUB, (i + 1) * SUB)) for g in range(ng) for i in range(CHUNK // SUB)]
    a_d = jnp.stack([_hg_diag(q3[g, rs], k3[g, rs], g3[g, rs]) for g, rs in blocks], axis=0)
    o_d = _bmm(a_d, jnp.stack([v3[g, rs] for g, rs in blocks], axis=0))
    nsub = CHUNK // SUB
    o = o + jnp.stack([jnp.concatenate([o_d[g * nsub + i] for i in range(nsub)], axis=0) for g in range(ng)], axis=0)
    egs = tuple(jnp.concatenate([jnp.exp(glast[c * HEADS + h]) for h in range(HEADS)], axis=1) for c in range(nb))
    return _unheads(q3 * jnp.exp(g3), nb), _unheads(k3 * jnp.exp(glast - g3), nb), _unheads(o, nb), egs


def hg_scan(q_in, k_out, v, eg, o_intra, z, nw, st):
    o = o_intra + _bmm_nt(q_in, st)
    return _gated_norm(o, z, nw), st * eg + _bmm_tn(v, k_out)


def _tri_y_impl(a):
    r, c = _iota2(CHUNK, CHUNK)
    same16 = (r // SUB) == (c // SUB)
    same32 = (r // (2 * SUB)) == (c // (2 * SUB))
    a0 = jnp.where(same16, a, 0.0)
    y = -a0
    pw = _bmm(a0, a0)
    for _ in range(2):
        y = y + pw + _bmm(y, pw)
        pw = _bmm(pw, pw)
    y = y + pw + _bmm(y, pw)
    for ak in (jnp.where(same32 & jnp.logical_not(same16), a, 0.0), jnp.where(same32, 0.0, a)):
        m = ak + _bmm(y, ak)
        y = y - (m + _bmm(m, y))
    return y


@jax.custom_vjp
def _tri_y(a):
    return _tri_y_impl(a)


def _tri_y_fwd(a):
    y = _tri_y_impl(a)
    return y, y


def _tri_y_bwd(y, dy):
    n = dy + _bmm_tn(y, dy)
    return (-(n + _bmm_nt(n, y)),)


_tri_y.defvjp(_tri_y_fwd, _tri_y_bwd)


def gd_local(xx, ab, cw, alog, dtb, inverse=_tri_y):
    n = ab.shape[0]
    nb = n // CHUNK
    conv = cw[0:1] * xx[HALO - 3:HALO - 3 + n]
    for j in range(1, CONV_TAPS):
        conv = conv + cw[j:j + 1] * xx[HALO - 3 + j:HALO - 3 + j + n]
    act = _silu(conv)
    x = ab + dtb
    g_all = -jnp.exp(alog) * (jnp.maximum(x, 0.0) + jnp.log1p(jnp.exp(-jnp.abs(x))))
    beta_all = jax.nn.sigmoid(ab)
    gam_all = _cumsum_chunks(g_all, nb)
    q3, k3, v3 = _heads(act[:, 0:WIDTH], nb), _heads(act[:, WIDTH:2 * WIDTH], nb), _heads(act[:, 2 * WIDTH:QKV], nb)
    q3 = q3 * lax.rsqrt(jnp.sum(q3 * q3, axis=-1, keepdims=True) + EPS) * (DH ** -0.5)
    k3 = k3 * lax.rsqrt(jnp.sum(k3 * k3, axis=-1, keepdims=True) + EPS)
    pairs = [(c, h) for c in range(nb) for h in range(HEADS)]
    beta = jnp.stack([beta_all[c * CHUNK:(c + 1) * CHUNK, HEADS + h:HEADS + h + 1] for c, h in pairs], axis=0)
    gam = jnp.stack([gam_all[c * CHUNK:(c + 1) * CHUNK, h:h + 1] for c, h in pairs], axis=0)
    gam_t = [gam_all[c * CHUNK:(c + 1) * CHUNK].T for c in range(nb)]
    gam_row = jnp.stack([gam_t[c][h:h + 1, :] for c, h in pairs], axis=0)
    glast = gam[:, CHUNK - 1:CHUNK, :]
    r, c = _iota2(CHUNK, CHUNK)
    dec = jnp.exp(jnp.where(c < r, gam - gam_row, -jnp.inf))
    y = inverse(beta * _bmm_nt(k3, k3) * dec)
    eg = jnp.exp(gam)
    rhs = jnp.concatenate([beta * v3, (beta * eg) * k3], axis=2)
    sol = rhs + _bmm(y, rhs)
    qk = _bmm_nt(q3, k3) * jnp.where(r == c, 1.0, dec)
    eas = tuple(jnp.exp(gam_all[(c + 1) * CHUNK - 1:(c + 1) * CHUNK]) for c in range(nb))
    return (_unheads(sol[:, :, 0:DH], nb), _unheads(sol[:, :, DH:2 * DH], nb), _unheads(q3 * eg, nb),
            _unheads(k3 * jnp.exp(glast - gam), nb), jnp.concatenate([qk[g] for g in range(nb * HEADS)], axis=0), eas)


def gd_scan(uu, ww, qe, ke, qk, ea, z, nw, s):
    u = uu - _bmm(ww, s)
    o = _bmm(qe, s) + _bmm(qk, u)
    return _gated_norm(o, z, nw), ea * s + _bmm_tn(ke, u)


def _cparams(*sem):
    return pltpu.CompilerParams(dimension_semantics=sem, vmem_limit_bytes=VMEM_LIMIT)


def _row_tile(n):
    for t in (256, 128, 64):
        if n % t == 0:
            return t
    raise ValueError(f"unsupported token count {n}")


def in_proj(h, norm_w, w_hg, w_gd, w_ab, name):
    n = h.shape[0]
    tm = _row_tile(n)

    def body(h_ref, nw_ref, whg_ref, wgd_ref, wab_ref, u_ref, phg_ref, pgd_ref, pab_ref):
        x = h_ref[...]
        u = (x * lax.rsqrt(jnp.mean(x * x, axis=-1, keepdims=True) + EPS) * nw_ref[...]).astype(MXU_DTYPE)
        u_ref[...] = u
        phg_ref[...] = jnp.dot(u, whg_ref[...], preferred_element_type=F32)
        pgd_ref[...] = jnp.dot(u, wgd_ref[...], preferred_element_type=F32)
        pab_ref[...] = jnp.dot(u, wab_ref[...], preferred_element_type=F32)

    row = lambda w: pl.BlockSpec((tm, w), lambda i: (i, 0))
    full = lambda a: pl.BlockSpec(a.shape, lambda i: (0, 0))
    return pl.pallas_call(
        body, grid=(n // tm,), name=name,
        in_specs=[row(D_MODEL), full(norm_w), full(w_hg), full(w_gd), full(w_ab)],
        out_specs=[row(D_MODEL), row(4 * WIDTH), row(4 * WIDTH), row(AB_PAD)],
        out_shape=[jax.ShapeDtypeStruct((n, D_MODEL), MXU_DTYPE), jax.ShapeDtypeStruct((n, 4 * WIDTH), F32),
                   jax.ShapeDtypeStruct((n, 4 * WIDTH), F32), jax.ShapeDtypeStruct((n, AB_PAD), F32)],
        compiler_params=_cparams("arbitrary"),
    )(h, norm_w, w_hg, w_gd, w_ab)


def out_proj_loss(x, tgt, y_hg, y_gd, w_out, fw):
    n = x.shape[0]
    tm = _row_tile(n)
    inv_d = 1.0 / D_MODEL

    def body(x_ref, t_ref, yh_ref, yg_ref, w_ref, fw_ref, dh_ref, dyh_ref, dyg_ref, dw_ref, loss_ref, dfw_ref):
        @pl.when(pl.program_id(0) == 0)
        def _():
            dw_ref[...] = jnp.zeros_like(dw_ref)
            loss_ref[...] = jnp.zeros_like(loss_ref)
            dfw_ref[...] = jnp.zeros_like(dfw_ref)

        yh, yg = yh_ref[...], yg_ref[...]
        wa, wb = w_ref[0:WIDTH, :], w_ref[WIDTH:2 * WIDTH, :]
        h2 = x_ref[...] + jnp.dot(yh, wa, preferred_element_type=F32) + jnp.dot(yg, wb, preferred_element_type=F32)
        r2 = lax.rsqrt(jnp.mean(h2 * h2, axis=-1, keepdims=True) + EPS)
        nrm = h2 * r2
        fwv = fw_ref[...]
        err = nrm * fwv - t_ref[...]
        loss_ref[...] += jnp.full(loss_ref.shape, 0.5 * inv_d * jnp.sum(err * err), F32)
        dout = err * inv_d
        dfw_ref[...] += jnp.sum(dout * nrm, axis=0, keepdims=True)
        dn = dout * fwv
        dh2 = r2 * (dn - nrm * jnp.mean(dn * nrm, axis=-1, keepdims=True))
        dh_ref[...] = dh2
        dhb = dh2.astype(MXU_DTYPE)
        dyh_ref[...] = lax.dot_general(dhb, wa, (((1,), (1,)), ((), ())), preferred_element_type=F32)
        dyg_ref[...] = lax.dot_general(dhb, wb, (((1,), (1,)), ((), ())), preferred_element_type=F32)
        dw_ref[0:WIDTH, :] += lax.dot_general(yh, dhb, (((0,), (0,)), ((), ())), preferred_element_type=F32)
        dw_ref[WIDTH:2 * WIDTH, :] += lax.dot_general(yg, dhb, (((0,), (0,)), ((), ())), preferred_element_type=F32)

    row = lambda w: pl.BlockSpec((tm, w), lambda i: (i, 0))
    full = lambda s: pl.BlockSpec(s, lambda i: (0, 0))
    return pl.pallas_call(
        body, grid=(n // tm,), name="out_proj_loss",
        in_specs=[row(D_MODEL), row(D_MODEL), row(WIDTH), row(WIDTH), full(w_out.shape), full(fw.shape)],
        out_specs=[row(D_MODEL), row(WIDTH), row(WIDTH), full((2 * WIDTH, D_MODEL)), full((8, 128)), full((1, D_MODEL))],
        out_shape=[jax.ShapeDtypeStruct((n, D_MODEL), F32), jax.ShapeDtypeStruct((n, WIDTH), F32),
                   jax.ShapeDtypeStruct((n, WIDTH), F32), jax.ShapeDtypeStruct((2 * WIDTH, D_MODEL), F32),
                   jax.ShapeDtypeStruct((8, 128), F32), jax.ShapeDtypeStruct((1, D_MODEL), F32)],
        compiler_params=_cparams("arbitrary"),
    )(x, tgt, y_hg, y_gd, w_out, fw)


def in_proj_bwd(dphg, dpgd, dpab, w_hg, w_gd, w_ab, h, dh2, norm_w, name):
    n = h.shape[0]
    tm = _row_tile(n)

    def body(dphg_ref, dpgd_ref, dpab_ref, whg_ref, wgd_ref, wab_ref, h_ref, dh2_ref, nw_ref, dx_ref, dnw_ref):
        @pl.when(pl.program_id(0) == 0)
        def _():
            dnw_ref[...] = jnp.zeros_like(dnw_ref)

        nt = (((1,), (1,)), ((), ()))
        du = lax.dot_general(dphg_ref[...].astype(MXU_DTYPE), whg_ref[...], nt, preferred_element_type=F32)
        du += lax.dot_general(dpgd_ref[...].astype(MXU_DTYPE), wgd_ref[...], nt, preferred_element_type=F32)
        du += lax.dot_general(dpab_ref[...].astype(MXU_DTYPE), wab_ref[...], nt, preferred_element_type=F32)
        x = h_ref[...]
        r = lax.rsqrt(jnp.mean(x * x, axis=-1, keepdims=True) + EPS)
        nrm = x * r
        dnw_ref[...] += jnp.sum(du * nrm, axis=0, keepdims=True)
        dn = du * nw_ref[...]
        dx_ref[...] = dh2_ref[...] + r * (dn - nrm * jnp.mean(dn * nrm, axis=-1, keepdims=True))

    row = lambda w: pl.BlockSpec((tm, w), lambda i: (i, 0))
    full = lambda a: pl.BlockSpec(a.shape, lambda i: (0, 0))
    return pl.pallas_call(
        body, grid=(n // tm,), name=name,
        in_specs=[row(4 * WIDTH), row(4 * WIDTH), row(AB_PAD), full(w_hg), full(w_gd), full(w_ab), row(D_MODEL),
                  row(D_MODEL), full(norm_w)],
        out_specs=[row(D_MODEL), pl.BlockSpec((1, D_MODEL), lambda i: (0, 0))],
        out_shape=[jax.ShapeDtypeStruct((n, D_MODEL), F32), jax.ShapeDtypeStruct((1, D_MODEL), F32)],
        compiler_params=_cparams("arbitrary"),
    )(dphg, dpgd, dpab, w_hg, w_gd, w_ab, h, dh2, norm_w)


def weight_grad(u, dp, u0, dp0, name):
    n, w = dp.shape
    tn = min(w, 1024)
    tm = 1024 if n % 1024 == 0 else _row_tile(n)
    n0 = u0.shape[0]

    def body(u_ref, dp_ref, u0_ref, dp0_ref, o_ref):
        @pl.when(pl.program_id(1) == 0)
        def _():
            o_ref[...] = _mm_tn(u0_ref[...], dp0_ref[...])

        o_ref[...] += _mm_tn(u_ref[...], dp_ref[...])

    return pl.pallas_call(
        body, grid=(w // tn, n // tm), name=name,
        in_specs=[pl.BlockSpec((tm, D_MODEL), lambda j, t: (t, 0)), pl.BlockSpec((tm, tn), lambda j, t: (t, j)),
                  pl.BlockSpec((n0, D_MODEL), lambda j, t: (0, 0)), pl.BlockSpec((n0, tn), lambda j, t: (0, j))],
        out_specs=pl.BlockSpec((D_MODEL, tn), lambda j, t: (0, j)),
        out_shape=jax.ShapeDtypeStruct((D_MODEL, w), F32),
        compiler_params=_cparams("arbitrary", "arbitrary"),
    )(u, dp, u0, dp0)


def _real(c):
    return jnp.maximum(c - 1, 0)


def _sds(shape, dtype=F32):
    return jax.ShapeDtypeStruct(shape, dtype)


def _load_slabs(ref, b):
    return jnp.stack([ref[i, :, h * DH:(h + 1) * DH] for i in range(b) for h in range(HEADS)], axis=0)


def _lead_slabs(a, b):
    return jnp.stack([a[:, h * DH:(h + 1) * DH] for _ in range(b) for h in range(HEADS)], axis=0)


def _rows(a3, i):
    return jnp.concatenate([a3[i * HEADS + h] for h in range(HEADS)], axis=1)


def _store_slabs(ref, a3, b):
    for i in range(b):
        ref[i] = _rows(a3, i).astype(ref.dtype)


def _sum_rows(a3, b):
    out = _rows(a3, 0)
    for i in range(1, b):
        out = out + _rows(a3, i)
    return out


def _save_states(ref, s, b):
    for i in range(b):
        ref[i] = jnp.concatenate([s[i * HEADS + h] for h in range(HEADS)], axis=0)


def _load_states(ref, b):
    return jnp.stack([ref[i, h * DH:(h + 1) * DH, :] for i in range(b) for h in range(HEADS)], axis=0)


def hg_local_fwd(p, logits):
    b, seq, _ = p.shape
    rows = LOCAL_CHUNKS * CHUNK
    nreal = seq // CHUNK

    def body(p_ref, lg_ref, q_ref, k_ref, o_ref, eg_ref):
        q_in, k_out, o_intra, egs = hg_local(p_ref[...], lg_ref[...])
        q_ref[...], k_ref[...], o_ref[...] = q_in, k_out, o_intra
        for c in range(LOCAL_CHUNKS):
            eg_ref[c] = egs[c]

    slab = pl.BlockSpec((None, rows, WIDTH), lambda s, g: (s, g, 0))
    return pl.pallas_call(
        body, grid=(b, seq // rows), name="hgrn2_local",
        in_specs=[pl.BlockSpec((None, rows, 4 * WIDTH), lambda s, g: (s, g, 0)), pl.BlockSpec(logits.shape, lambda s, g: (0, 0))],
        out_specs=[slab, slab, slab, pl.BlockSpec((None, LOCAL_CHUNKS, 1, WIDTH), lambda s, g: (s, g, 0, 0))],
        out_shape=[_sds((b, seq, WIDTH))] * 3 + [_sds((b, nreal, 1, WIDTH))],
        compiler_params=_cparams("arbitrary", "arbitrary"),
    )(p, logits)


def hg_local_lead(p0, logits):
    def body(p_ref, lg_ref, q_ref, k_ref, o_ref, eg_ref):
        q_ref[...], k_ref[...], o_ref[...], (eg_ref[...],) = hg_local(p_ref[...], lg_ref[...])

    return pl.pallas_call(
        body, name="hgrn2_local_lead", in_specs=[VMEM_SPEC] * 2, out_specs=[VMEM_SPEC] * 4,
        out_shape=[_sds((CHUNK, WIDTH))] * 3 + [_sds((1, WIDTH))],
        compiler_params=pltpu.CompilerParams(vmem_limit_bytes=VMEM_LIMIT),
    )(p0, logits)


def _hg_scan_inputs(c, b, q_ref, k_ref, o_ref, v_ref, z_ref, eg_ref, q0_ref, k0_ref, o0_ref, p0_ref, eg0_ref):
    lead = c == 0
    pick = lambda real, lead_val: jnp.where(lead, _lead_slabs(lead_val, b), _load_slabs(real, b))
    eg = jnp.where(lead, jnp.stack([eg0_ref[:, h * DH:(h + 1) * DH] for _ in range(b) for h in range(HEADS)], axis=0),
                   jnp.stack([eg_ref[i, :, h * DH:(h + 1) * DH] for i in range(b) for h in range(HEADS)], axis=0))
    return (pick(q_ref, q0_ref[...]), pick(k_ref, k0_ref[...]), pick(v_ref, p0_ref[:, 2 * WIDTH:3 * WIDTH]), eg,
            pick(o_ref, o0_ref[...]), pick(z_ref, p0_ref[:, 3 * WIDTH:4 * WIDTH]))


def _scan_specs(b, nc, reverse):
    chunk = (lambda i: nc - 1 - i) if reverse else (lambda i: i)
    slab = lambda lane_block: pl.BlockSpec((b, CHUNK, WIDTH), lambda i: (0, _real(chunk(i)), lane_block))
    per_chunk = lambda *tail: pl.BlockSpec((b, None) + tail, lambda i: (0, _real(chunk(i))) + (0,) * len(tail))
    state = pl.BlockSpec((b, None, WIDTH, DH), lambda i: (0, chunk(i), 0, 0))
    const = lambda a: pl.BlockSpec(a.shape, lambda i: (0,) * a.ndim)
    return slab, per_chunk, state, const


def hg_scan_fwd(p, p0, local, lead, nw):
    b, seq, _ = p.shape
    nc = seq // CHUNK + 1
    q_in, k_out, o_intra, eg = local
    slab, per_chunk, state, const = _scan_specs(b, nc, False)

    def body(q_ref, k_ref, o_ref, v_ref, z_ref, eg_ref, q0_ref, k0_ref, o0_ref, p0_ref, eg0_ref, nw_ref, y_ref, ss_ref, st):
        c = pl.program_id(0)

        @pl.when(c == 0)
        def _():
            st[...] = jnp.zeros_like(st)

        s_in = st[...]
        _save_states(ss_ref, s_in, b)
        args = _hg_scan_inputs(c, b, q_ref, k_ref, o_ref, v_ref, z_ref, eg_ref, q0_ref, k0_ref, o0_ref, p0_ref, eg0_ref)
        y, s_new = hg_scan(*args, nw_ref[...], s_in)
        _store_slabs(y_ref, y, b)
        st[...] = s_new

    return pl.pallas_call(
        body, grid=(nc,), name="hgrn2_scan",
        in_specs=[slab(0), slab(0), slab(0), slab(2), slab(3), per_chunk(1, WIDTH)] + [const(a) for a in lead[0:3]]
        + [const(p0), const(lead[3]), const(nw)],
        out_specs=[slab(0), state],
        out_shape=[_sds((b, seq, WIDTH), MXU_DTYPE), _sds((b, nc, WIDTH, DH))],
        scratch_shapes=[pltpu.VMEM((b * HEADS, DH, DH), F32)],
        compiler_params=_cparams("arbitrary"),
    )(q_in, k_out, o_intra, p, p, eg, lead[0], lead[1], lead[2], p0, lead[3], nw)


def hg_scan_bwd(p, p0, local, lead, nw, ssave, dy):
    b, seq, _ = p.shape
    nc = seq // CHUNK + 1
    q_in, k_out, o_intra, eg = local
    slab, per_chunk, state, const = _scan_specs(b, nc, True)

    def body(q_ref, k_ref, o_ref, v_ref, z_ref, eg_ref, q0_ref, k0_ref, o0_ref, p0_ref, eg0_ref, nw_ref, ss_ref, dy_ref,
             dq_ref, dk_ref, do_ref, dv_ref, dz_ref, deg_ref, dq0_ref, dk0_ref, do0_ref, dv0_ref, dz0_ref, deg0_ref, dnw_ref,
             dst):
        i = pl.program_id(0)
        c = nc - 1 - i

        @pl.when(i == 0)
        def _():
            dst[...] = jnp.zeros_like(dst)
            dnw_ref[...] = jnp.zeros_like(dnw_ref)

        args = _hg_scan_inputs(c, b, q_ref, k_ref, o_ref, v_ref, z_ref, eg_ref, q0_ref, k0_ref, o0_ref, p0_ref, eg0_ref)
        s_in = _load_states(ss_ref, b)
        _, vjp = jax.vjp(hg_scan, *args, nw_ref[...], s_in)
        dyv = jnp.where(c == 0, 0.0, _load_slabs(dy_ref, b))
        dq, dk, dv, deg, do, dz, dnw, ds = vjp((dyv, dst[...]))
        dst[...] = ds
        dnw_ref[...] += dnw

        @pl.when(c > 0)
        def _():
            for ref, val in ((dq_ref, dq), (dk_ref, dk), (do_ref, do), (dv_ref, dv), (dz_ref, dz)):
                _store_slabs(ref, val, b)
            for j in range(b):
                deg_ref[j] = _rows(deg, j)

        @pl.when(c == 0)
        def _():
            for ref, val in ((dq0_ref, dq), (dk0_ref, dk), (do0_ref, do), (dv0_ref, dv), (dz0_ref, dz), (deg0_ref, deg)):
                ref[...] = _sum_rows(val, b)

    lead_out = [const(a) for a in lead[0:3]] + [const(lead[0]), const(lead[0]), const(lead[3])]
    return pl.pallas_call(
        body, grid=(nc,), name="hgrn2_scan_bwd",
        in_specs=[slab(0), slab(0), slab(0), slab(2), slab(3), per_chunk(1, WIDTH)] + [const(a) for a in lead[0:3]]
        + [const(p0), const(lead[3]), const(nw), state, slab(0)],
        out_specs=[slab(0)] * 5 + [per_chunk(1, WIDTH)] + lead_out + [const(nw)],
        out_shape=[_sds((b, seq, WIDTH))] * 5 + [_sds(eg.shape)] + [_sds((CHUNK, WIDTH))] * 5 + [_sds((1, WIDTH)), _sds(nw.shape)],
        scratch_shapes=[pltpu.VMEM((b * HEADS, DH, DH), F32)],
        compiler_params=_cparams("arbitrary"),
    )(q_in, k_out, o_intra, p, p, eg, lead[0], lead[1], lead[2], p0, lead[3], nw, ssave, dy)


def _hg_local_vjp(p, logits, dq, dk, do, degs, dv, dz):
    _, vjp = jax.vjp(hg_local, p, logits)
    dp, dlg = vjp((dq, dk, do, degs))
    return dp + jnp.concatenate([jnp.zeros((p.shape[0], 2 * WIDTH), F32), dv, dz], axis=1), dlg


def hg_local_bwd(p, logits, dq, dk, do, dv, dz, deg):
    b, seq, _ = p.shape
    rows = LOCAL_CHUNKS * CHUNK

    def body(p_ref, lg_ref, dq_ref, dk_ref, do_ref, dv_ref, dz_ref, deg_ref, dp_ref, dlg_ref):
        @pl.when((pl.program_id(0) == 0) & (pl.program_id(1) == 0))
        def _():
            dlg_ref[...] = jnp.zeros_like(dlg_ref)

        degs = tuple(deg_ref[c] for c in range(LOCAL_CHUNKS))
        dp, dlg = _hg_local_vjp(p_ref[...], lg_ref[...], dq_ref[...], dk_ref[...], do_ref[...], degs, dv_ref[...], dz_ref[...])
        dp_ref[...] = dp
        dlg_ref[...] += dlg

    slab = pl.BlockSpec((None, rows, WIDTH), lambda s, g: (s, g, 0))
    wide = pl.BlockSpec((None, rows, 4 * WIDTH), lambda s, g: (s, g, 0))
    lg = pl.BlockSpec(logits.shape, lambda s, g: (0, 0))
    return pl.pallas_call(
        body, grid=(b, seq // rows), name="hgrn2_local_bwd",
        in_specs=[wide, lg, slab, slab, slab, slab, slab, pl.BlockSpec((None, LOCAL_CHUNKS, 1, WIDTH), lambda s, g: (s, g, 0, 0))],
        out_specs=[wide, lg], out_shape=[_sds(p.shape), _sds(logits.shape)],
        compiler_params=_cparams("arbitrary", "arbitrary"),
    )(p, logits, dq, dk, do, dv, dz, deg)


def hg_local_bwd_lead(p0, logits, dq, dk, do, dv, dz, deg):
    def body(p_ref, lg_ref, dq_ref, dk_ref, do_ref, dv_ref, dz_ref, deg_ref, dp_ref, dlg_ref):
        dp_ref[...], dlg_ref[...] = _hg_local_vjp(p_ref[...], lg_ref[...], dq_ref[...], dk_ref[...], do_ref[...],
                                                  (deg_ref[...],), dv_ref[...], dz_ref[...])

    return pl.pallas_call(
        body, name="hgrn2_local_bwd_lead", in_specs=[VMEM_SPEC] * 8, out_specs=[VMEM_SPEC] * 2,
        out_shape=[_sds(p0.shape), _sds(logits.shape)], compiler_params=pltpu.CompilerParams(vmem_limit_bytes=VMEM_LIMIT),
    )(p0, logits, dq, dk, do, dv, dz, deg)


def _halo_block(g):
    return jnp.maximum((LOCAL_CHUNKS * CHUNK // HALO) * g - 1, 0)


def _gd_window(g, p_ref, halo_ref, p0_ref):
    halo = jnp.where(g == 0, p0_ref[CHUNK - HALO:CHUNK, 0:QKV], halo_ref[...])
    return jnp.concatenate([halo, p_ref[:, 0:QKV]], axis=0)


def gd_local_fwd(p, p0, ab, cw, alog, dtb):
    b, seq, _ = p.shape
    rows = LOCAL_CHUNKS * CHUNK
    nreal = seq // CHUNK

    def body(p_ref, halo_ref, p0_ref, ab_ref, cw_ref, al_ref, dt_ref, u_ref, w_ref, qe_ref, ke_ref, qk_ref, ea_ref):
        uu, ww, qe, ke, qk, eas = gd_local(_gd_window(pl.program_id(1), p_ref, halo_ref, p0_ref), ab_ref[...], cw_ref[...],
                                           al_ref[...], dt_ref[...], inverse=_tri_y_impl)
        u_ref[...], w_ref[...], qe_ref[...], ke_ref[...] = uu, ww, qe, ke
        for c in range(LOCAL_CHUNKS):
            qk_ref[c] = qk[c * HEADS * CHUNK:(c + 1) * HEADS * CHUNK]
            ea_ref[c] = eas[c]

    const = lambda a: pl.BlockSpec(a.shape, lambda s, g: (0, 0))
    slab = pl.BlockSpec((None, rows, WIDTH), lambda s, g: (s, g, 0))
    return pl.pallas_call(
        body, grid=(b, seq // rows), name="gdn_local",
        in_specs=[pl.BlockSpec((None, rows, 4 * WIDTH), lambda s, g: (s, g, 0)),
                  pl.BlockSpec((None, HALO, QKV), lambda s, g: (s, _halo_block(g), 0)), const(p0),
                  pl.BlockSpec((None, rows, AB_PAD), lambda s, g: (s, g, 0)), const(cw), const(alog), const(dtb)],
        out_specs=[slab] * 4 + [pl.BlockSpec((None, LOCAL_CHUNKS, HEADS * CHUNK, CHUNK), lambda s, g: (s, g, 0, 0)),
                                pl.BlockSpec((None, LOCAL_CHUNKS, 1, AB_PAD), lambda s, g: (s, g, 0, 0))],
        out_shape=[_sds((b, seq, WIDTH))] * 4 + [_sds((b, nreal, HEADS * CHUNK, CHUNK)), _sds((b, nreal, 1, AB_PAD))],
        compiler_params=_cparams("arbitrary", "arbitrary"),
    )(p, p, p0, ab, cw, alog, dtb)


def _lead_window(p0_ref):
    return jnp.concatenate([jnp.zeros((HALO, QKV), F32), p0_ref[:, 0:QKV]], axis=0)


def gd_local_lead(p0, ab0, cw, alog, dtb):
    def body(p0_ref, ab_ref, cw_ref, al_ref, dt_ref, u_ref, w_ref, qe_ref, ke_ref, qk_ref, ea_ref):
        u_ref[...], w_ref[...], qe_ref[...], ke_ref[...], qk_ref[...], (ea_ref[...],) = gd_local(
            _lead_window(p0_ref), ab_ref[...], cw_ref[...], al_ref[...], dt_ref[...], inverse=_tri_y_impl)

    return pl.pallas_call(
        body, name="gdn_local_lead", in_specs=[VMEM_SPEC] * 5, out_specs=[VMEM_SPEC] * 6,
        out_shape=[_sds((CHUNK, WIDTH))] * 4 + [_sds((HEADS * CHUNK, CHUNK)), _sds((1, AB_PAD))],
        compiler_params=pltpu.CompilerParams(vmem_limit_bytes=VMEM_LIMIT),
    )(p0, ab0, cw, alog, dtb)


def _gd_scan_inputs(c, b, u_ref, w_ref, qe_ref, ke_ref, qk_ref, ea_ref, z_ref, u0_ref, w0_ref, qe0_ref, ke0_ref, qk0_ref,
                    ea0_ref, p0_ref):
    lead = c == 0
    pick = lambda real, lead_val: jnp.where(lead, _lead_slabs(lead_val, b), _load_slabs(real, b))
    pairs = [(i, h) for i in range(b) for h in range(HEADS)]
    qk = jnp.where(lead, jnp.stack([qk0_ref[h * CHUNK:(h + 1) * CHUNK, :] for _, h in pairs], axis=0),
                   jnp.stack([qk_ref[i, h * CHUNK:(h + 1) * CHUNK, :] for i, h in pairs], axis=0))
    ea = jnp.where(lead, jnp.stack([ea0_ref[:, h:h + 1] for _, h in pairs], axis=0),
                   jnp.stack([ea_ref[i, :, h:h + 1] for i, h in pairs], axis=0))
    return (pick(u_ref, u0_ref[...]), pick(w_ref, w0_ref[...]), pick(qe_ref, qe0_ref[...]), pick(ke_ref, ke0_ref[...]), qk,
            ea, pick(z_ref, p0_ref[:, QKV:QKV + WIDTH]))


def gd_scan_fwd(p, p0, local, lead, nw):
    b, seq, _ = p.shape
    nc = seq // CHUNK + 1
    slab, per_chunk, state, const = _scan_specs(b, nc, False)

    def body(u_ref, w_ref, qe_ref, ke_ref, qk_ref, ea_ref, z_ref, u0_ref, w0_ref, qe0_ref, ke0_ref, qk0_ref, ea0_ref, p0_ref,
             nw_ref, y_ref, ss_ref, st):
        c = pl.program_id(0)

        @pl.when(c == 0)
        def _():
            st[...] = jnp.zeros_like(st)

        s_in = st[...]
        _save_states(ss_ref, s_in, b)
        args = _gd_scan_inputs(c, b, u_ref, w_ref, qe_ref, ke_ref, qk_ref, ea_ref, z_ref, u0_ref, w0_ref, qe0_ref, ke0_ref,
                               qk0_ref, ea0_ref, p0_ref)
        y, s_new = gd_scan(*args, nw_ref[...], s_in)
        _store_slabs(y_ref, y, b)
        st[...] = s_new

    return pl.pallas_call(
        body, grid=(nc,), name="gdn_scan",
        in_specs=[slab(0)] * 4 + [per_chunk(HEADS * CHUNK, CHUNK), per_chunk(1, AB_PAD), slab(3)] + [const(a) for a in lead]
        + [const(p0), const(nw)],
        out_specs=[slab(0), state],
        out_shape=[_sds((b, seq, WIDTH), MXU_DTYPE), _sds((b, nc, WIDTH, DH))],
        scratch_shapes=[pltpu.VMEM((b * HEADS, DH, DH), F32)],
        compiler_params=_cparams("arbitrary"),
    )(*local, p, *lead, p0, nw)


def gd_scan_bwd(p, p0, local, lead, nw, ssave, dy):
    b, seq, _ = p.shape
    nc = seq // CHUNK + 1
    slab, per_chunk, state, const = _scan_specs(b, nc, True)

    def body(u_ref, w_ref, qe_ref, ke_ref, qk_ref, ea_ref, z_ref, u0_ref, w0_ref, qe0_ref, ke0_ref, qk0_ref, ea0_ref, p0_ref,
             nw_ref, ss_ref, dy_ref, du_ref, dw_ref, dqe_ref, dke_ref, dqk_ref, dea_ref, dz_ref, du0_ref, dw0_ref, dqe0_ref,
             dke0_ref, dqk0_ref, dea0_ref, dz0_ref, dnw_ref, dst):
        i = pl.program_id(0)
        c = nc - 1 - i

        @pl.when(i == 0)
        def _():
            dst[...] = jnp.zeros_like(dst)
            dnw_ref[...] = jnp.zeros_like(dnw_ref)

        args = _gd_scan_inputs(c, b, u_ref, w_ref, qe_ref, ke_ref, qk_ref, ea_ref, z_ref, u0_ref, w0_ref, qe0_ref, ke0_ref,
                               qk0_ref, ea0_ref, p0_ref)
        s_in = _load_states(ss_ref, b)
        _, vjp = jax.vjp(gd_scan, *args, nw_ref[...], s_in)
        dyv = jnp.where(c == 0, 0.0, _load_slabs(dy_ref, b))
        du, dw, dqe, dke, dqk, dea, dz, dnw, ds = vjp((dyv, dst[...]))
        dst[...] = ds
        dnw_ref[...] += dnw
        lane = lax.broadcasted_iota(jnp.int32, (1, AB_PAD), 1)
        dea_rows = [sum(jnp.where(lane == h, dea[j * HEADS + h], 0.0) for h in range(HEADS)) for j in range(b)]
        dqk_rows = [jnp.concatenate([dqk[j * HEADS + h] for h in range(HEADS)], axis=0) for j in range(b)]

        @pl.when(c > 0)
        def _():
            for ref, val in ((du_ref, du), (dw_ref, dw), (dqe_ref, dqe), (dke_ref, dke), (dz_ref, dz)):
                _store_slabs(ref, val, b)
            for j in range(b):
                dqk_ref[j] = dqk_rows[j]
                dea_ref[j] = dea_rows[j]

        @pl.when(c == 0)
        def _():
            for ref, val in ((du0_ref, du), (dw0_ref, dw), (dqe0_ref, dqe), (dke0_ref, dke), (dz0_ref, dz)):
                ref[...] = _sum_rows(val, b)
            dqk0_ref[...] = sum(dqk_rows[1:], dqk_rows[0])
            dea0_ref[...] = sum(dea_rows[1:], dea_rows[0])

    uu, ww, qe, ke, qk, ea = local
    return pl.pallas_call(
        body, grid=(nc,), name="gdn_scan_bwd",
        in_specs=[slab(0)] * 4 + [per_chunk(HEADS * CHUNK, CHUNK), per_chunk(1, AB_PAD), slab(3)] + [const(a) for a in lead]
        + [const(p0), const(nw), state, slab(0)],
        out_specs=[slab(0)] * 4 + [per_chunk(HEADS * CHUNK, CHUNK), per_chunk(1, AB_PAD), slab(0)] + [const(a) for a in lead]
        + [const(lead[0]), const(nw)],
        out_shape=[_sds((b, seq, WIDTH))] * 4 + [_sds(qk.shape), _sds(ea.shape), _sds((b, seq, WIDTH))]
        + [_sds(a.shape) for a in lead] + [_sds(lead[0].shape), _sds(nw.shape)],
        scratch_shapes=[pltpu.VMEM((b * HEADS, DH, DH), F32)],
        compiler_params=_cparams("arbitrary"),
    )(*local, p, *lead, p0, nw, ssave, dy)


def gd_local_bwd(p, p0, ab, cw, alog, dtb, cot, dz):
    b, seq, _ = p.shape
    rows = LOCAL_CHUNKS * CHUNK
    ng = seq // rows
    du, dw, dqe, dke, dqk, dea = cot

    def body(p_ref, halo_ref, p0_ref, ab_ref, cw_ref, al_ref, dt_ref, du_ref, dw_ref, dqe_ref, dke_ref, dqk_ref, dea_ref, dz_ref,
             dp_ref, dab_ref, dhalo0_ref, dcw_ref, dal_ref, ddt_ref, dhalo):
        i = pl.program_id(1)
        g = ng - 1 - i

        @pl.when(i == 0)
        def _():
            dhalo[...] = jnp.zeros_like(dhalo)

        @pl.when((pl.program_id(0) == 0) & (i == 0))
        def _():
            dcw_ref[...] = jnp.zeros_like(dcw_ref)
            dal_ref[...] = jnp.zeros_like(dal_ref)
            ddt_ref[...] = jnp.zeros_like(ddt_ref)

        _, vjp = jax.vjp(gd_local, _gd_window(g, p_ref, halo_ref, p0_ref), ab_ref[...], cw_ref[...], al_ref[...], dt_ref[...])
        dqk_all = jnp.concatenate([dqk_ref[c] for c in range(LOCAL_CHUNKS)], axis=0)
        deas = tuple(dea_ref[c] for c in range(LOCAL_CHUNKS))
        dxx, dab, dcw, dal, ddt = vjp((du_ref[...], dw_ref[...], dqe_ref[...], dke_ref[...], dqk_all, deas))
        dqkv = dxx[HALO:HALO + rows] + jnp.concatenate([jnp.zeros((rows - HALO, QKV), F32), dhalo[...]], axis=0)
        dhalo[...] = dxx[0:HALO]
        dhalo0_ref[...] = dxx[0:HALO]
        dp_ref[...] = jnp.concatenate([dqkv, dz_ref[...]], axis=1)
        dab_ref[...] = dab
        dcw_ref[...] += dcw
        dal_ref[...] += dal
        ddt_ref[...] += ddt

    rg = lambda i: ng - 1 - i
    const = lambda a: pl.BlockSpec(a.shape, lambda s, i: (0, 0))
    slab = pl.BlockSpec((None, rows, WIDTH), lambda s, i: (s, rg(i), 0))
    wide = pl.BlockSpec((None, rows, 4 * WIDTH), lambda s, i: (s, rg(i), 0))
    gates = pl.BlockSpec((None, rows, AB_PAD), lambda s, i: (s, rg(i), 0))
    return pl.pallas_call(
        body, grid=(b, ng), name="gdn_local_bwd",
        in_specs=[wide, pl.BlockSpec((None, HALO, QKV), lambda s, i: (s, _halo_block(rg(i)), 0)), const(p0), gates, const(cw),
                  const(alog), const(dtb), slab, slab, slab, slab,
                  pl.BlockSpec((None, LOCAL_CHUNKS, HEADS * CHUNK, CHUNK), lambda s, i: (s, rg(i), 0, 0)),
                  pl.BlockSpec((None, LOCAL_CHUNKS, 1, AB_PAD), lambda s, i: (s, rg(i), 0, 0)), slab],
        out_specs=[wide, gates, pl.BlockSpec((None, HALO, QKV), lambda s, i: (s, 0, 0)), const(cw), const(alog), const(dtb)],
        out_shape=[_sds(p.shape), _sds(ab.shape), _sds((b, HALO, QKV)), _sds(cw.shape), _sds(alog.shape), _sds(dtb.shape)],
        scratch_shapes=[pltpu.VMEM((HALO, QKV), F32)],
        compiler_params=_cparams("arbitrary", "arbitrary"),
    )(p, p, p0, ab, cw, alog, dtb, du, dw, dqe, dke, dqk, dea, dz)


def gd_local_bwd_lead(p0, ab0, cw, alog, dtb, cot, dz, dtail):
    def body(p0_ref, ab_ref, cw_ref, al_ref, dt_ref, du_ref, dw_ref, dqe_ref, dke_ref, dqk_ref, dea_ref, dz_ref, dtail_ref,
             dp_ref, dab_ref, dcw_ref, dal_ref, ddt_ref):
        _, vjp = jax.vjp(gd_local, _lead_window(p0_ref), ab_ref[...], cw_ref[...], al_ref[...], dt_ref[...])
        dxx, dab, dcw, dal, ddt = vjp((du_ref[...], dw_ref[...], dqe_ref[...], dke_ref[...], dqk_ref[...], (dea_ref[...],)))
        dqkv = dxx[HALO:HALO + CHUNK] + jnp.concatenate([jnp.zeros((CHUNK - HALO, QKV), F32), dtail_ref[...]], axis=0)
        dp_ref[...] = jnp.concatenate([dqkv, dz_ref[...]], axis=1)
        dab_ref[...], dcw_ref[...], dal_ref[...], ddt_ref[...] = dab, dcw, dal, ddt

    return pl.pallas_call(
        body, name="gdn_local_bwd_lead", in_specs=[VMEM_SPEC] * 13, out_specs=[VMEM_SPEC] * 5,
        out_shape=[_sds(p0.shape), _sds(ab0.shape), _sds(cw.shape), _sds(alog.shape), _sds(dtb.shape)],
        compiler_params=pltpu.CompilerParams(vmem_limit_bytes=VMEM_LIMIT),
    )(p0, ab0, cw, alog, dtb, *cot, dz, dtail)


def _position():
    return lax.axis_index("x"), lax.axis_index("y"), lax.axis_index("c")


def all_gather(shard, dtype, name, reduce=False):
    rows, cols = shard.shape

    def body(x_ref, *rest):
        if reduce:
            sum_ref, out_ref, send_sems, recv_sems = rest
        else:
            out_ref, send_sems, recv_sems = rest
        x, y, c = _position()
        me, sibling = (x, y, c), (x, y, 1 - c)
        chips = [(1 - x, y), (x, 1 - y), (1 - x, 1 - y)]

        def block(px, py, pc):
            return out_ref.at[4 * px + 2 * py + pc]

        def copy(k, blk, to):
            return pltpu.make_async_remote_copy(src_ref=block(*blk), dst_ref=block(*blk), send_sem=send_sems.at[k],
                                                recv_sem=recv_sems.at[k], device_id=to, device_id_type=MESH)

        out_ref[4 * x + 2 * y + c] = x_ref[...].astype(dtype)
        first = [copy(0, me, sibling)] + [copy(1 + j, me, (*chip, c)) for j, chip in enumerate(chips)]
        for cp in first:
            cp.start()
        passed = [copy(4 + j, (*chip, c), sibling) for j, chip in enumerate(chips)]
        for j, chip in enumerate(chips):
            copy(1 + j, (*chip, c), me).wait_recv()
            passed[j].start()
        copy(0, sibling, me).wait_recv()
        for j, chip in enumerate(chips):
            copy(4 + j, (*chip, 1 - c), me).wait_recv()
        for cp in first + passed:
            cp.wait_send()
        if reduce:
            acc = out_ref[0].astype(F32)
            for d in range(1, N_DEV):
                acc = acc + out_ref[d].astype(F32)
            sum_ref[...] = acc

    gathered = jax.ShapeDtypeStruct((N_DEV, rows, cols), dtype)
    sems = [pltpu.SemaphoreType.DMA((N_DEV - 1,)), pltpu.SemaphoreType.DMA((N_DEV - 1,))]
    if reduce:
        return pl.pallas_call(
            body, name=name, in_specs=[VMEM_SPEC], out_specs=VMEM_SPEC,
            out_shape=jax.ShapeDtypeStruct((rows, cols), F32),
            scratch_shapes=[pltpu.VMEM((N_DEV, rows, cols), dtype)] + sems,
            compiler_params=pltpu.CompilerParams(vmem_limit_bytes=VMEM_LIMIT),
        )(shard)
    return pl.pallas_call(
        body, name=name, in_specs=[VMEM_SPEC], out_specs=VMEM_SPEC, out_shape=gathered, scratch_shapes=sems,
        compiler_params=pltpu.CompilerParams(vmem_limit_bytes=VMEM_LIMIT),
    )(shard)


def reduce_scatter(blocks, name):
    _, rows, cols = blocks.shape

    def body(b_ref, o_ref, recv, send_sems, recv_sems):
        x, y, c = _position()
        copies = []
        for m in range(1, N_DEV):
            px = 1 - x if m & 4 else x
            py = 1 - y if m & 2 else y
            pc = 1 - c if m & 1 else c
            cp = pltpu.make_async_remote_copy(src_ref=b_ref.at[4 * px + 2 * py + pc], dst_ref=recv.at[m - 1],
                                              send_sem=send_sems.at[m - 1], recv_sem=recv_sems.at[m - 1],
                                              device_id=(px, py, pc), device_id_type=MESH)
            cp.start()
            copies.append(cp)
        acc = b_ref[4 * x + 2 * y + c].astype(F32)
        for m in range(1, N_DEV):
            copies[m - 1].wait_recv()
            acc = acc + recv[m - 1].astype(F32)
        for cp in copies:
            cp.wait_send()
        o_ref[...] = acc

    return pl.pallas_call(
        body, name=name, in_specs=[VMEM_SPEC], out_specs=VMEM_SPEC, out_shape=jax.ShapeDtypeStruct((rows, cols), F32),
        scratch_shapes=[pltpu.VMEM((N_DEV - 1, rows, cols), blocks.dtype), pltpu.SemaphoreType.DMA((N_DEV - 1,)),
                        pltpu.SemaphoreType.DMA((N_DEV - 1,))],
        compiler_params=pltpu.CompilerParams(vmem_limit_bytes=VMEM_LIMIT),
    )(blocks)


def adamw(w, g, m, v, name):
    rows, cols = w.shape
    tr = 256 if rows % 256 == 0 else rows

    def body(w_ref, g_ref, m_ref, v_ref, d_ref, nm_ref, nv_ref):
        gv = g_ref[...]
        mn = ADAM_B1 * m_ref[...] + (1.0 - ADAM_B1) * gv
        vn = ADAM_B2 * v_ref[...] + (1.0 - ADAM_B2) * jnp.square(gv)
        m_hat = mn / (1.0 - ADAM_B1 ** ADAM_STEP)
        v_hat = vn / (1.0 - ADAM_B2 ** ADAM_STEP)
        d_ref[...] = -ADAM_LR * (m_hat / (jnp.sqrt(v_hat) + ADAM_EPS) + ADAM_WD * w_ref[...])
        nm_ref[...] = mn
        nv_ref[...] = vn

    spec = pl.BlockSpec((tr, cols), lambda i: (i, 0))
    shape = jax.ShapeDtypeStruct((rows, cols), F32)
    return pl.pallas_call(body, grid=(rows // tr,), name=name, in_specs=[spec] * 4, out_specs=[spec] * 3,
                          out_shape=[shape] * 3, compiler_params=_cparams("arbitrary"))(w, g, m, v)


def _pad_rows(a, rows=8):
    return jnp.pad(a, ((0, rows - a.shape[0]), (0, 0)))


def _pad_lanes(a, lanes=128):
    return jnp.pad(a, ((0, 0), (0, lanes - a.shape[1])))


def kernel(x, meta_tokens, norm_w, w_in, conv_w, hg_lb_logits, hg_norm_w, gdn_A_log, gdn_dt_bias, gdn_norm_w, w_out, final_norm_w, loss_target, m_meta_tokens, m_norm_w, m_w_in, m_conv_w, m_hg_lb_logits, m_hg_norm_w, m_gdn_A_log, m_gdn_dt_bias, m_gdn_norm_w, m_w_out, m_final_norm_w, v_meta_tokens, v_norm_w, v_w_in, v_conv_w, v_hg_lb_logits, v_hg_norm_w, v_gdn_A_log, v_gdn_dt_bias, v_gdn_norm_w, v_w_out, v_final_norm_w):
    b, seq, _ = x.shape
    n = b * seq
    dev = 4 * lax.axis_index("x") + 2 * lax.axis_index("y") + lax.axis_index("c")
    col_shard = IN_COLS // N_DEV

    w_in_g = all_gather(w_in[0], MXU_DTYPE, "gather_w_in")
    w_out_g = all_gather(w_out[0], MXU_DTYPE, "gather_w_out")
    small_w = jnp.concatenate([_pad_lanes(meta_tokens, 256), _pad_rows(_pad_lanes(conv_w[0], 256))], axis=0)
    small_g = all_gather(small_w, F32, "gather_small")
    meta_g = small_g[:, 0:N_META, 0:D_MODEL // N_DEV]
    conv_g = small_g[:, N_META:N_META + CONV_TAPS, 0:QKV // N_DEV]
    w_in_full = jnp.transpose(w_in_g, (1, 0, 2)).reshape(D_MODEL, IN_COLS)
    w_hg = w_in_full[:, 0:4 * WIDTH]
    w_gd = w_in_full[:, 4 * WIDTH:8 * WIDTH]
    w_ab = _pad_lanes(w_in_full[:, 8 * WIDTH:IN_COLS])
    w_out_full = w_out_g.reshape(2 * WIDTH, D_MODEL)
    cw = jnp.transpose(conv_g, (1, 0, 2)).reshape(CONV_TAPS, QKV)
    meta = jnp.transpose(meta_g, (1, 0, 2)).reshape(N_META, D_MODEL)
    alog = _pad_lanes(gdn_A_log)
    dtb = _pad_lanes(gdn_dt_bias)
    fw = final_norm_w.reshape(1, D_MODEL)

    h0 = jnp.concatenate([jnp.zeros((CHUNK - N_META, D_MODEL), F32), meta], axis=0)
    x2 = x.reshape(n, D_MODEL)
    u0, phg0, pgd0, pab0 = in_proj(h0, norm_w, w_hg, w_gd, w_ab, "in_proj_lead")
    u, phg, pgd, pab = in_proj(x2, norm_w, w_hg, w_gd, w_ab, "in_proj")
    phg3, pgd3, pab3 = phg.reshape(b, seq, 4 * WIDTH), pgd.reshape(b, seq, 4 * WIDTH), pab.reshape(b, seq, AB_PAD)
    hg_loc = hg_local_fwd(phg3, hg_lb_logits)
    hg_lead = hg_local_lead(phg0, hg_lb_logits)
    y_hg, s_hg = hg_scan_fwd(phg3, phg0, hg_loc, hg_lead, hg_norm_w)
    gd_loc = gd_local_fwd(pgd3, pgd0, pab3, cw, alog, dtb)
    gd_lead = gd_local_lead(pgd0, pab0, cw, alog, dtb)
    y_gd, s_gd = gd_scan_fwd(pgd3, pgd0, gd_loc, gd_lead, gdn_norm_w)

    dh2, dy_hg, dy_gd, g_w_out, loss_part, g_fw = out_proj_loss(
        x2, loss_target.reshape(n, D_MODEL), y_hg.reshape(n, WIDTH), y_gd.reshape(n, WIDTH), w_out_full, fw)

    hb = hg_scan_bwd(phg3, phg0, hg_loc, hg_lead, hg_norm_w, s_hg, dy_hg.reshape(b, seq, WIDTH))
    dphg, g_lb = hg_local_bwd(phg3, hg_lb_logits, *hb[0:6])
    dphg0, g_lb0 = hg_local_bwd_lead(phg0, hg_lb_logits, *hb[6:12])
    g_hg_nw = hb[12]
    gb = gd_scan_bwd(pgd3, pgd0, gd_loc, gd_lead, gdn_norm_w, s_gd, dy_gd.reshape(b, seq, WIDTH))
    dpgd, dpab, dtail, g_cw, g_alog, g_dtb = gd_local_bwd(pgd3, pgd0, pab3, cw, alog, dtb, gb[0:6], gb[6])
    dpgd0, dpab0, g_cw0, g_alog0, g_dtb0 = gd_local_bwd_lead(pgd0, pab0, cw, alog, dtb, gb[7:13], gb[13], dtail.sum(0))
    g_gd_nw = gb[14]
    dphg, dpgd, dpab = dphg.reshape(n, 4 * WIDTH), dpgd.reshape(n, 4 * WIDTH), dpab.reshape(n, AB_PAD)

    grad_x, g_nw = in_proj_bwd(dphg, dpgd, dpab, w_hg, w_gd, w_ab, x2, dh2, norm_w, "in_proj_bwd")
    dh0, g_nw0 = in_proj_bwd(dphg0, dpgd0, dpab0, w_hg, w_gd, w_ab, h0, jnp.zeros_like(h0), norm_w, "in_proj_bwd_lead")
    g_w_hg = weight_grad(u, dphg, u0, dphg0, "w_in_grad_hg")
    g_w_gd = weight_grad(u, dpgd, u0, dpgd0, "w_in_grad_gd")
    g_w_ab = weight_grad(u, dpab, u0, dpab0, "w_in_grad_ab")

    g_w_in_full = jnp.concatenate([g_w_hg, g_w_gd, g_w_ab[:, 0:2 * HEADS]], axis=1)
    g_w_in_blocks = jnp.transpose(g_w_in_full.reshape(D_MODEL, N_DEV, col_shard), (1, 0, 2)).astype(MXU_DTYPE)
    g_w_in = reduce_scatter(g_w_in_blocks, "reduce_w_in")
    g_w_out = reduce_scatter(g_w_out.reshape(N_DEV, (2 * WIDTH) // N_DEV, D_MODEL).astype(MXU_DTYPE), "reduce_w_out")
    small = jnp.concatenate([
        (g_nw + g_nw0).reshape(8, 128), (g_lb + g_lb0).reshape(8, 128), _pad_rows(g_hg_nw), _pad_rows(g_alog + g_alog0),
        _pad_rows(g_dtb + g_dtb0), _pad_rows(g_gd_nw), g_fw.reshape(8, 128), (g_cw + g_cw0).reshape(48, 128),
        dh0[CHUNK - N_META:CHUNK].reshape(128, 128), loss_part], axis=0)
    small = all_gather(small, F32, "reduce_small", reduce=True)
    g_norm_w = small[0:8].reshape(1, D_MODEL)
    g_lb = small[8:16].reshape(2, WIDTH)
    g_hg_nw = small[16:17]
    g_alog = small[24:25, 0:HEADS]
    g_dtb = small[32:33, 0:HEADS]
    g_gd_nw = small[40:41]
    g_fw = small[48:56].reshape(1, D_MODEL)
    g_cw_full = small[56:104].reshape(CONV_TAPS, QKV)
    g_meta_full = small[104:232].reshape(N_META, D_MODEL)
    loss = small[232, 0]
    g_conv = lax.dynamic_slice_in_dim(g_cw_full, dev * (QKV // N_DEV), QKV // N_DEV, axis=1)
    g_meta = lax.dynamic_slice_in_dim(g_meta_full, dev * (D_MODEL // N_DEV), D_MODEL // N_DEV, axis=1)

    names = ["meta_tokens", "norm_w", "w_in", "conv_w", "hg_lb_logits", "hg_norm_w", "gdn_A_log", "gdn_dt_bias",
             "gdn_norm_w", "w_out", "final_norm_w"]
    weights = [meta_tokens, norm_w, w_in, conv_w, hg_lb_logits, hg_norm_w, gdn_A_log, gdn_dt_bias, gdn_norm_w, w_out,
               final_norm_w]
    moms = [m_meta_tokens, m_norm_w, m_w_in, m_conv_w, m_hg_lb_logits, m_hg_norm_w, m_gdn_A_log, m_gdn_dt_bias,
            m_gdn_norm_w, m_w_out, m_final_norm_w]
    vars_ = [v_meta_tokens, v_norm_w, v_w_in, v_conv_w, v_hg_lb_logits, v_hg_norm_w, v_gdn_A_log, v_gdn_dt_bias,
             v_gdn_norm_w, v_w_out, v_final_norm_w]
    grads2d = [g_meta, g_norm_w, g_w_in, g_conv, g_lb, g_hg_nw, g_alog, g_dtb, g_gd_nw, g_w_out, g_fw]
    grads, deltas, new_ms, new_vs = [], [], [], []
    for nm, w, g2, m, v in zip(names, weights, grads2d, moms, vars_):
        shape2 = g2.shape
        d, nm_, nv_ = adamw(w.reshape(shape2), g2, m.reshape(shape2), v.reshape(shape2), "adamw_" + nm)
        grads.append(g2.reshape(w.shape))
        deltas.append(d.reshape(w.shape))
        new_ms.append(nm_.reshape(w.shape))
        new_vs.append(nv_.reshape(w.shape))
    return (loss, grad_x.reshape(x.shape), *grads, *deltas, *new_ms, *new_vs)
```

```python
import jax
import jax.numpy as jnp
from jax import lax
from jax.experimental import pallas as pl
from jax.experimental.pallas import tpu as pltpu

F32 = jnp.float32
BF16 = jnp.bfloat16
MXU_DTYPE = BF16

D_MODEL = 1024
N_META = 16
CHUNK = 64
SUB = 16
HEADS = 4
DH = 128
WIDTH = HEADS * DH
QKV = 3 * WIDTH
CONV_TAPS = 4
HALO = 8
EPS = 1e-6
IN_COLS = 4 * WIDTH + 4 * WIDTH + 2 * HEADS
AB_PAD = 128
N_DEV = 8
LOCAL_CHUNKS = 2
VMEM_LIMIT = 56 * 1024 * 1024

ADAM_LR = 0.001
ADAM_B1 = 0.9
ADAM_B2 = 0.999
ADAM_EPS = 1e-08
ADAM_WD = 0.01
ADAM_STEP = 10

VMEM_SPEC = pl.BlockSpec(memory_space=pltpu.VMEM)
MESH = pl.DeviceIdType.MESH


def _mm_tn(a, b):
    return lax.dot_general(a.astype(MXU_DTYPE), b.astype(MXU_DTYPE), (((0,), (0,)), ((), ())), preferred_element_type=F32)


def _bmm(a, b):
    return lax.dot_general(a.astype(MXU_DTYPE), b.astype(MXU_DTYPE), (((2,), (1,)), ((0,), (0,))), preferred_element_type=F32)


def _bmm_nt(a, b):
    return lax.dot_general(a.astype(MXU_DTYPE), b.astype(MXU_DTYPE), (((2,), (2,)), ((0,), (0,))), preferred_element_type=F32)


def _bmm_tn(a, b):
    return lax.dot_general(a.astype(MXU_DTYPE), b.astype(MXU_DTYPE), (((1,), (1,)), ((0,), (0,))), preferred_element_type=F32)


def _iota2(n, m):
    return lax.broadcasted_iota(jnp.int32, (n, m), 0), lax.broadcasted_iota(jnp.int32, (n, m), 1)


def _silu(x):
    return x * jax.nn.sigmoid(x)


def _gated_norm(o, z, nw):
    return o * lax.rsqrt(jnp.mean(o * o, axis=-1, keepdims=True) + EPS) * nw * _silu(z)


def _heads(a, nb):
    return jnp.stack([a[c * CHUNK:(c + 1) * CHUNK, h * DH:(h + 1) * DH] for c in range(nb) for h in range(HEADS)], axis=0)


def _unheads(a3, nb):
    return jnp.concatenate(
        [jnp.concatenate([a3[c * HEADS + h] for h in range(HEADS)], axis=1) for c in range(nb)], axis=0)


def _split3(x):
    hi = x.astype(BF16)
    r1 = x - hi.astype(F32)
    mid = r1.astype(BF16)
    return hi, mid, (r1 - mid.astype(F32)).astype(BF16)


def _select_mm(pattern, n_out, n_in, transposed, x):
    rows, inner = (n_in, n_out) if transposed else (n_out, n_in)
    r, c = _iota2(rows, 3 * inner)
    c = c - jnp.where(c >= inner, inner, 0) - jnp.where(c >= 2 * inner, inner, 0)
    s = jnp.where(pattern(c, r) if transposed else pattern(r, c), 1.0, 0.0).astype(BF16)
    return jnp.dot(s, jnp.concatenate(_split3(x), axis=0), preferred_element_type=F32)


def _select_rows(pattern, n_out, x):
    @jax.custom_vjp
    def apply(v):
        return _select_mm(pattern, n_out, CHUNK, False, v)

    apply.defvjp(lambda v: (_select_mm(pattern, n_out, CHUNK, False, v), None),
                 lambda _, d: (_select_mm(pattern, n_out, CHUNK, True, d),))
    return apply(x)


def _cumsum_chunks(x, nb):
    return jnp.concatenate([_select_rows(lambda i, j: j <= i, CHUNK, x[c * CHUNK:(c + 1) * CHUNK]) for c in range(nb)], axis=0)


HG_LEVELS = 6


def _hg_sums(i, j):
    lvl, t = i >> HG_LEVELS, i & (CHUNK - 1)
    last = t
    for l in range(1, HG_LEVELS + 1):
        width = HG_LEVELS + 1 - l
        last = jnp.where(lvl == l, ((t >> width) << width) + (CHUNK >> l) - 1, last)
    return j <= last


def hg_local(p, logits):
    nb = p.shape[0] // CHUNK
    l0, l1 = logits[0:1], logits[1:2]
    mx = jnp.maximum(l0, l1)
    e0, e1 = jnp.exp(l0 - mx), jnp.exp(l1 - mx)
    lb = e0 / (e0 + e1)
    q = _silu(p[:, 0:WIDTH])
    f = lb + (1.0 - lb) * jax.nn.sigmoid(p[:, WIDTH:2 * WIDTH])
    k = 1.0 - f
    logf = jnp.log(f)
    sums = [_select_rows(_hg_sums, (HG_LEVELS + 1) * CHUNK, logf[c * CHUNK:(c + 1) * CHUNK]) for c in range(nb)]
    level = lambda l: _heads(jnp.concatenate([s[l * CHUNK:(l + 1) * CHUNK] for s in sums], axis=0), nb)
    q3, k3, v3, g3 = _heads(q, nb), _heads(k, nb), _heads(p[:, 2 * WIDTH:3 * WIDTH], nb), level(0)
    r, c = _iota2(CHUNK, CHUNK)
    a = jnp.where(r == c, _bmm_nt(q3, k3), 0.0)
    for l in range(1, HG_LEVELS + 1):
        ref = level(l)
        qt = q3 * jnp.exp(jnp.minimum(g3 - ref, 0.0))
        kt = k3 * jnp.exp(jnp.minimum(ref - g3, 0.0))
        sh = HG_LEVELS - l
        pair = ((r >> (sh + 1)) == (c >> (sh + 1))) & (((r >> sh) & 1) == 1) & (((c >> sh) & 1) == 0)
        a = a + jnp.where(pair, _bmm_nt(qt, kt), 0.0)
    o = _bmm(a, v3)
    glast = g3[:, CHUNK - 1:CHUNK, :]
    egs = tuple(jnp.concatenate([jnp.exp(glast[c * HEADS + h]) for h in range(HEADS)], axis=1) for c in range(nb))
    return _unheads(q3 * jnp.exp(g3), nb), _unheads(k3 * jnp.exp(glast - g3), nb), _unheads(o, nb), egs


def hg_scan(q_in, k_out, v, eg, o_intra, z, nw, st):
    o = o_intra + _bmm_nt(q_in, st)
    return _gated_norm(o, z, nw), st * eg + _bmm_tn(v, k_out)


def _tri_y_impl(a):
    r, c = _iota2(CHUNK, CHUNK)
    same16 = (r // SUB) == (c // SUB)
    same32 = (r // (2 * SUB)) == (c // (2 * SUB))
    a0 = jnp.where(same16, a, 0.0)
    y = -a0
    pw = _bmm(a0, a0)
    for _ in range(2):
        y = y + pw + _bmm(y, pw)
        pw = _bmm(pw, pw)
    y = y + pw + _bmm(y, pw)
    for ak in (jnp.where(same32 & jnp.logical_not(same16), a, 0.0), jnp.where(same32, 0.0, a)):
        m = ak + _bmm(y, ak)
        y = y - (m + _bmm(m, y))
    return y


@jax.custom_vjp
def _tri_y(a):
    return _tri_y_impl(a)


def _tri_y_fwd(a):
    y = _tri_y_impl(a)
    return y, y


def _tri_y_bwd(y, dy):
    n = dy + _bmm_tn(y, dy)
    return (-(n + _bmm_nt(n, y)),)


_tri_y.defvjp(_tri_y_fwd, _tri_y_bwd)


def gd_local(xx, ab, cw, alog, dtb, inverse=_tri_y):
    n = ab.shape[0]
    nb = n // CHUNK
    conv = cw[0:1] * xx[HALO - 3:HALO - 3 + n]
    for j in range(1, CONV_TAPS):
        conv = conv + cw[j:j + 1] * xx[HALO - 3 + j:HALO - 3 + j + n]
    act = _silu(conv)
    x = ab + dtb
    g_all = -jnp.exp(alog) * (jnp.maximum(x, 0.0) + jnp.log1p(jnp.exp(-jnp.abs(x))))
    beta_all = jax.nn.sigmoid(ab)
    gam_all = _cumsum_chunks(g_all, nb)
    q3, k3, v3 = _heads(act[:, 0:WIDTH], nb), _heads(act[:, WIDTH:2 * WIDTH], nb), _heads(act[:, 2 * WIDTH:QKV], nb)
    q3 = q3 * lax.rsqrt(jnp.sum(q3 * q3, axis=-1, keepdims=True) + EPS) * (DH ** -0.5)
    k3 = k3 * lax.rsqrt(jnp.sum(k3 * k3, axis=-1, keepdims=True) + EPS)
    pairs = [(c, h) for c in range(nb) for h in range(HEADS)]
    beta = jnp.stack([beta_all[c * CHUNK:(c + 1) * CHUNK, HEADS + h:HEADS + h + 1] for c, h in pairs], axis=0)
    gam = jnp.stack([gam_all[c * CHUNK:(c + 1) * CHUNK, h:h + 1] for c, h in pairs], axis=0)
    gam_t = [gam_all[c * CHUNK:(c + 1) * CHUNK].T for c in range(nb)]
    gam_row = jnp.stack([gam_t[c][h:h + 1, :] for c, h in pairs], axis=0)
    glast = gam[:, CHUNK - 1:CHUNK, :]
    r, c = _iota2(CHUNK, CHUNK)
    dec = jnp.exp(jnp.where(c < r, gam - gam_row, -jnp.inf))
    y = inverse(beta * _bmm_nt(k3, k3) * dec)
    eg = jnp.exp(gam)
    rhs = jnp.concatenate([beta * v3, (beta * eg) * k3], axis=2)
    sol = rhs + _bmm(y, rhs)
    qk = _bmm_nt(q3, k3) * jnp.where(r == c, 1.0, dec)
    eas = tuple(jnp.exp(gam_all[(c + 1) * CHUNK - 1:(c + 1) * CHUNK]) for c in range(nb))
    return (_unheads(sol[:, :, 0:DH], nb), _unheads(sol[:, :, DH:2 * DH], nb), _unheads(q3 * eg, nb),
            _unheads(k3 * jnp.exp(glast - gam), nb), jnp.concatenate([qk[g] for g in range(nb * HEADS)], axis=0), eas)


def gd_scan(uu, ww, qe, ke, qk, ea, z, nw, s):
    u = uu - _bmm(ww, s)
    o = _bmm(qe, s) + _bmm(qk, u)
    return _gated_norm(o, z, nw), ea * s + _bmm_tn(ke, u)


def _cparams(*sem):
    return pltpu.CompilerParams(dimension_semantics=sem, vmem_limit_bytes=VMEM_LIMIT)


def _row_tile(n):
    for t in (256, 128, 64):
        if n % t == 0:
            return t
    raise ValueError(f"unsupported token count {n}")


def _w_in_specs():
    return [pl.BlockSpec((4 * WIDTH, D_MODEL), lambda *i: (0, 0)), pl.BlockSpec((4 * WIDTH, D_MODEL), lambda *i: (1, 0)),
            pl.BlockSpec((AB_PAD, D_MODEL), lambda *i: (8 * WIDTH // AB_PAD, 0))]


def in_proj(h, norm_w, w_t, name):
    n = h.shape[0]
    tm = _row_tile(n)
    nt = (((1,), (1,)), ((), ()))

    def body(h_ref, nw_ref, whg_ref, wgd_ref, wab_ref, u_ref, phg_ref, pgd_ref, pab_ref):
        x = h_ref[...]
        u = (x * lax.rsqrt(jnp.mean(x * x, axis=-1, keepdims=True) + EPS) * nw_ref[...]).astype(MXU_DTYPE)
        u_ref[...] = u
        phg_ref[...] = lax.dot_general(u, whg_ref[...], nt, preferred_element_type=F32)
        pgd_ref[...] = lax.dot_general(u, wgd_ref[...], nt, preferred_element_type=F32)
        pab_ref[...] = lax.dot_general(u, wab_ref[...], nt, preferred_element_type=F32)

    row = lambda w: pl.BlockSpec((tm, w), lambda i: (i, 0))
    return pl.pallas_call(
        body, grid=(n // tm,), name=name,
        in_specs=[row(D_MODEL), pl.BlockSpec(norm_w.shape, lambda i: (0, 0))] + _w_in_specs(),
        out_specs=[row(D_MODEL), row(4 * WIDTH), row(4 * WIDTH), row(AB_PAD)],
        out_shape=[jax.ShapeDtypeStruct((n, D_MODEL), MXU_DTYPE), jax.ShapeDtypeStruct((n, 4 * WIDTH), F32),
                   jax.ShapeDtypeStruct((n, 4 * WIDTH), F32), jax.ShapeDtypeStruct((n, AB_PAD), F32)],
        compiler_params=_cparams("arbitrary"),
    )(h, norm_w, w_t, w_t, w_t)


def out_proj_loss(x, tgt, y_hg, y_gd, w_out, fw):
    n = x.shape[0]
    tm = _row_tile(n)
    inv_d = 1.0 / D_MODEL

    def body(x_ref, t_ref, yh_ref, yg_ref, w_ref, fw_ref, dh_ref, dyh_ref, dyg_ref, dw_ref, loss_ref, dfw_ref):
        @pl.when(pl.program_id(0) == 0)
        def _():
            dw_ref[...] = jnp.zeros_like(dw_ref)
            loss_ref[...] = jnp.zeros_like(loss_ref)
            dfw_ref[...] = jnp.zeros_like(dfw_ref)

        yh, yg = yh_ref[...], yg_ref[...]
        wa, wb = w_ref[0:WIDTH, :], w_ref[WIDTH:2 * WIDTH, :]
        h2 = x_ref[...] + jnp.dot(yh, wa, preferred_element_type=F32) + jnp.dot(yg, wb, preferred_element_type=F32)
        r2 = lax.rsqrt(jnp.mean(h2 * h2, axis=-1, keepdims=True) + EPS)
        nrm = h2 * r2
        fwv = fw_ref[...]
        err = nrm * fwv - t_ref[...]
        loss_ref[...] += jnp.full(loss_ref.shape, 0.5 * inv_d * jnp.sum(err * err), F32)
        dout = err * inv_d
        dfw_ref[...] += jnp.sum(dout * nrm, axis=0, keepdims=True)
        dn = dout * fwv
        dh2 = r2 * (dn - nrm * jnp.mean(dn * nrm, axis=-1, keepdims=True))
        dh_ref[...] = dh2
        dhb = dh2.astype(MXU_DTYPE)
        dyh_ref[...] = lax.dot_general(dhb, wa, (((1,), (1,)), ((), ())), preferred_element_type=F32)
        dyg_ref[...] = lax.dot_general(dhb, wb, (((1,), (1,)), ((), ())), preferred_element_type=F32)
        dw_ref[0:WIDTH, :] += lax.dot_general(yh, dhb, (((0,), (0,)), ((), ())), preferred_element_type=F32)
        dw_ref[WIDTH:2 * WIDTH, :] += lax.dot_general(yg, dhb, (((0,), (0,)), ((), ())), preferred_element_type=F32)

    row = lambda w: pl.BlockSpec((tm, w), lambda i: (i, 0))
    full = lambda s: pl.BlockSpec(s, lambda i: (0, 0))
    return pl.pallas_call(
        body, grid=(n // tm,), name="out_proj_loss",
        in_specs=[row(D_MODEL), row(D_MODEL), row(WIDTH), row(WIDTH), full(w_out.shape), full(fw.shape)],
        out_specs=[row(D_MODEL), row(WIDTH), row(WIDTH), full((2 * WIDTH, D_MODEL)), full((8, 128)), full((1, D_MODEL))],
        out_shape=[jax.ShapeDtypeStruct((n, D_MODEL), F32), jax.ShapeDtypeStruct((n, WIDTH), F32),
                   jax.ShapeDtypeStruct((n, WIDTH), F32), jax.ShapeDtypeStruct((2 * WIDTH, D_MODEL), F32),
                   jax.ShapeDtypeStruct((8, 128), F32), jax.ShapeDtypeStruct((1, D_MODEL), F32)],
        compiler_params=_cparams("arbitrary"),
    )(x, tgt, y_hg, y_gd, w_out, fw)


def in_proj_bwd(dphg, dpgd, dpab, w_t, h, dh2, norm_w, name):
    n = h.shape[0]
    tm = _row_tile(n)

    def body(dphg_ref, dpgd_ref, dpab_ref, whg_ref, wgd_ref, wab_ref, h_ref, dh2_ref, nw_ref, dx_ref, dnw_ref):
        @pl.when(pl.program_id(0) == 0)
        def _():
            dnw_ref[...] = jnp.zeros_like(dnw_ref)

        du = jnp.dot(dphg_ref[...].astype(MXU_DTYPE), whg_ref[...], preferred_element_type=F32)
        du += jnp.dot(dpgd_ref[...].astype(MXU_DTYPE), wgd_ref[...], preferred_element_type=F32)
        du += jnp.dot(dpab_ref[...].astype(MXU_DTYPE), wab_ref[...], preferred_element_type=F32)
        x = h_ref[...]
        r = lax.rsqrt(jnp.mean(x * x, axis=-1, keepdims=True) + EPS)
        nrm = x * r
        dnw_ref[...] += jnp.sum(du * nrm, axis=0, keepdims=True)
        dn = du * nw_ref[...]
        dx_ref[...] = dh2_ref[...] + r * (dn - nrm * jnp.mean(dn * nrm, axis=-1, keepdims=True))

    row = lambda w: pl.BlockSpec((tm, w), lambda i: (i, 0))
    return pl.pallas_call(
        body, grid=(n // tm,), name=name,
        in_specs=[row(4 * WIDTH), row(4 * WIDTH), row(AB_PAD)] + _w_in_specs() + [row(D_MODEL), row(D_MODEL),
                                                                                   pl.BlockSpec(norm_w.shape, lambda i: (0, 0))],
        out_specs=[row(D_MODEL), pl.BlockSpec((1, D_MODEL), lambda i: (0, 0))],
        out_shape=[jax.ShapeDtypeStruct((n, D_MODEL), F32), jax.ShapeDtypeStruct((1, D_MODEL), F32)],
        compiler_params=_cparams("arbitrary"),
    )(dphg, dpgd, dpab, w_t, w_t, w_t, h, dh2, norm_w)


def weight_grad(u, dp, u0, dp0, name):
    n, w = dp.shape
    tn = min(w, 1024)
    tm = 1024 if n % 1024 == 0 else _row_tile(n)
    n0 = u0.shape[0]

    def body(u_ref, dp_ref, u0_ref, dp0_ref, o_ref):
        @pl.when(pl.program_id(1) == 0)
        def _():
            o_ref[...] = _mm_tn(dp0_ref[...], u0_ref[...])

        o_ref[...] += _mm_tn(dp_ref[...], u_ref[...])

    return pl.pallas_call(
        body, grid=(w // tn, n // tm), name=name,
        in_specs=[pl.BlockSpec((tm, D_MODEL), lambda j, t: (t, 0)), pl.BlockSpec((tm, tn), lambda j, t: (t, j)),
                  pl.BlockSpec((n0, D_MODEL), lambda j, t: (0, 0)), pl.BlockSpec((n0, tn), lambda j, t: (0, j))],
        out_specs=pl.BlockSpec((tn, D_MODEL), lambda j, t: (j, 0)),
        out_shape=jax.ShapeDtypeStruct((w, D_MODEL), F32),
        compiler_params=_cparams("arbitrary", "arbitrary"),
    )(u, dp, u0, dp0)


def _real(c):
    return jnp.maximum(c - 1, 0)


def _sds(shape, dtype=F32):
    return jax.ShapeDtypeStruct(shape, dtype)


def _load_slabs(ref, b):
    return jnp.stack([ref[i, :, h * DH:(h + 1) * DH] for i in range(b) for h in range(HEADS)], axis=0)


def _lead_slabs(a, b):
    return jnp.stack([a[:, h * DH:(h + 1) * DH] for _ in range(b) for h in range(HEADS)], axis=0)


def _rows(a3, i):
    return jnp.concatenate([a3[i * HEADS + h] for h in range(HEADS)], axis=1)


def _store_slabs(ref, a3, b):
    for i in range(b):
        ref[i] = _rows(a3, i).astype(ref.dtype)


def _sum_rows(a3, b):
    out = _rows(a3, 0)
    for i in range(1, b):
        out = out + _rows(a3, i)
    return out


def _save_states(ref, s, b):
    for i in range(b):
        ref[i] = jnp.concatenate([s[i * HEADS + h] for h in range(HEADS)], axis=0)


def _load_states(ref, b):
    return jnp.stack([ref[i, h * DH:(h + 1) * DH, :] for i in range(b) for h in range(HEADS)], axis=0)


def hg_local_fwd(p, logits):
    b, seq, _ = p.shape
    rows = LOCAL_CHUNKS * CHUNK
    nreal = seq // CHUNK

    def body(p_ref, lg_ref, q_ref, k_ref, o_ref, eg_ref):
        q_in, k_out, o_intra, egs = hg_local(p_ref[...], lg_ref[...])
        q_ref[...], k_ref[...], o_ref[...] = q_in, k_out, o_intra
        for c in range(LOCAL_CHUNKS):
            eg_ref[c] = egs[c]

    slab = pl.BlockSpec((None, rows, WIDTH), lambda s, g: (s, g, 0))
    return pl.pallas_call(
        body, grid=(b, seq // rows), name="hgrn2_local",
        in_specs=[pl.BlockSpec((None, rows, 4 * WIDTH), lambda s, g: (s, g, 0)), pl.BlockSpec(logits.shape, lambda s, g: (0, 0))],
        out_specs=[slab, slab, slab, pl.BlockSpec((None, LOCAL_CHUNKS, 1, WIDTH), lambda s, g: (s, g, 0, 0))],
        out_shape=[_sds((b, seq, WIDTH))] * 3 + [_sds((b, nreal, 1, WIDTH))],
        compiler_params=_cparams("arbitrary", "arbitrary"),
    )(p, logits)


def hg_local_lead(p0, logits):
    def body(p_ref, lg_ref, q_ref, k_ref, o_ref, eg_ref):
        q_ref[...], k_ref[...], o_ref[...], (eg_ref[...],) = hg_local(p_ref[...], lg_ref[...])

    return pl.pallas_call(
        body, name="hgrn2_local_lead", in_specs=[VMEM_SPEC] * 2, out_specs=[VMEM_SPEC] * 4,
        out_shape=[_sds((CHUNK, WIDTH))] * 3 + [_sds((1, WIDTH))],
        compiler_params=pltpu.CompilerParams(vmem_limit_bytes=VMEM_LIMIT),
    )(p0, logits)


def _hg_scan_inputs(c, b, q_ref, k_ref, o_ref, v_ref, z_ref, eg_ref, q0_ref, k0_ref, o0_ref, p0_ref, eg0_ref):
    lead = c == 0
    pick = lambda real, lead_val: jnp.where(lead, _lead_slabs(lead_val, b), _load_slabs(real, b))
    eg = jnp.where(lead, jnp.stack([eg0_ref[:, h * DH:(h + 1) * DH] for _ in range(b) for h in range(HEADS)], axis=0),
                   jnp.stack([eg_ref[i, :, h * DH:(h + 1) * DH] for i in range(b) for h in range(HEADS)], axis=0))
    return (pick(q_ref, q0_ref[...]), pick(k_ref, k0_ref[...]), pick(v_ref, p0_ref[:, 2 * WIDTH:3 * WIDTH]), eg,
            pick(o_ref, o0_ref[...]), pick(z_ref, p0_ref[:, 3 * WIDTH:4 * WIDTH]))


def _scan_specs(b, nc, reverse):
    chunk = (lambda i: nc - 1 - i) if reverse else (lambda i: i)
    slab = lambda lane_block: pl.BlockSpec((b, CHUNK, WIDTH), lambda i: (0, _real(chunk(i)), lane_block))
    per_chunk = lambda *tail: pl.BlockSpec((b, None) + tail, lambda i: (0, _real(chunk(i))) + (0,) * len(tail))
    state = pl.BlockSpec((b, None, WIDTH, DH), lambda i: (0, chunk(i), 0, 0))
    const = lambda a: pl.BlockSpec(a.shape, lambda i: (0,) * a.ndim)
    return slab, per_chunk, state, const


def hg_scan_fwd(p, p0, local, lead, nw):
    b, seq, _ = p.shape
    nc = seq // CHUNK + 1
    q_in, k_out, o_intra, eg = local
    slab, per_chunk, state, const = _scan_specs(b, nc, False)

    def body(q_ref, k_ref, o_ref, v_ref, z_ref, eg_ref, q0_ref, k0_ref, o0_ref, p0_ref, eg0_ref, nw_ref, y_ref, ss_ref, st):
        c = pl.program_id(0)

        @pl.when(c == 0)
        def _():
            st[...] = jnp.zeros_like(st)

        s_in = st[...]
        _save_states(ss_ref, s_in, b)
        args = _hg_scan_inputs(c, b, q_ref, k_ref, o_ref, v_ref, z_ref, eg_ref, q0_ref, k0_ref, o0_ref, p0_ref, eg0_ref)
        y, s_new = hg_scan(*args, nw_ref[...], s_in)
        _store_slabs(y_ref, y, b)
        st[...] = s_new

    return pl.pallas_call(
        body, grid=(nc,), name="hgrn2_scan",
        in_specs=[slab(0), slab(0), slab(0), slab(2), slab(3), per_chunk(1, WIDTH)] + [const(a) for a in lead[0:3]]
        + [const(p0), const(lead[3]), const(nw)],
        out_specs=[slab(0), state],
        out_shape=[_sds((b, seq, WIDTH), MXU_DTYPE), _sds((b, nc, WIDTH, DH))],
        scratch_shapes=[pltpu.VMEM((b * HEADS, DH, DH), F32)],
        compiler_params=_cparams("arbitrary"),
    )(q_in, k_out, o_intra, p, p, eg, lead[0], lead[1], lead[2], p0, lead[3], nw)


def hg_scan_bwd(p, p0, local, lead, nw, ssave, dy):
    b, seq, _ = p.shape
    nc = seq // CHUNK + 1
    q_in, k_out, o_intra, eg = local
    slab, per_chunk, state, const = _scan_specs(b, nc, True)

    def body(q_ref, k_ref, o_ref, v_ref, z_ref, eg_ref, q0_ref, k0_ref, o0_ref, p0_ref, eg0_ref, nw_ref, ss_ref, dy_ref,
             dq_ref, dk_ref, do_ref, dv_ref, dz_ref, deg_ref, dq0_ref, dk0_ref, do0_ref, dv0_ref, dz0_ref, deg0_ref, dnw_ref,
             dst):
        i = pl.program_id(0)
        c = nc - 1 - i

        @pl.when(i == 0)
        def _():
            dst[...] = jnp.zeros_like(dst)
            dnw_ref[...] = jnp.zeros_like(dnw_ref)

        args = _hg_scan_inputs(c, b, q_ref, k_ref, o_ref, v_ref, z_ref, eg_ref, q0_ref, k0_ref, o0_ref, p0_ref, eg0_ref)
        s_in = _load_states(ss_ref, b)
        _, vjp = jax.vjp(hg_scan, *args, nw_ref[...], s_in)
        dyv = jnp.where(c == 0, 0.0, _load_slabs(dy_ref, b))
        dq, dk, dv, deg, do, dz, dnw, ds = vjp((dyv, dst[...]))
        dst[...] = ds
        dnw_ref[...] += dnw

        @pl.when(c > 0)
        def _():
            for ref, val in ((dq_ref, dq), (dk_ref, dk), (do_ref, do), (dv_ref, dv), (dz_ref, dz)):
                _store_slabs(ref, val, b)
            for j in range(b):
                deg_ref[j] = _rows(deg, j)

        @pl.when(c == 0)
        def _():
            for ref, val in ((dq0_ref, dq), (dk0_ref, dk), (do0_ref, do), (dv0_ref, dv), (dz0_ref, dz), (deg0_ref, deg)):
                ref[...] = _sum_rows(val, b)

    lead_out = [const(a) for a in lead[0:3]] + [const(lead[0]), const(lead[0]), const(lead[3])]
    return pl.pallas_call(
        body, grid=(nc,), name="hgrn2_scan_bwd",
        in_specs=[slab(0), slab(0), slab(0), slab(2), slab(3), per_chunk(1, WIDTH)] + [const(a) for a in lead[0:3]]
        + [const(p0), const(lead[3]), const(nw), state, slab(0)],
        out_specs=[slab(0)] * 5 + [per_chunk(1, WIDTH)] + lead_out + [const(nw)],
        out_shape=[_sds((b, seq, WIDTH))] * 5 + [_sds(eg.shape)] + [_sds((CHUNK, WIDTH))] * 5 + [_sds((1, WIDTH)), _sds(nw.shape)],
        scratch_shapes=[pltpu.VMEM((b * HEADS, DH, DH), F32)],
        compiler_params=_cparams("arbitrary"),
    )(q_in, k_out, o_intra, p, p, eg, lead[0], lead[1], lead[2], p0, lead[3], nw, ssave, dy)


def _hg_local_vjp(p, logits, dq, dk, do, degs, dv, dz):
    _, vjp = jax.vjp(hg_local, p, logits)
    dp, dlg = vjp((dq, dk, do, degs))
    return dp + jnp.concatenate([jnp.zeros((p.shape[0], 2 * WIDTH), F32), dv, dz], axis=1), dlg


def hg_local_bwd(p, logits, dq, dk, do, dv, dz, deg):
    b, seq, _ = p.shape
    rows = LOCAL_CHUNKS * CHUNK

    def body(p_ref, lg_ref, dq_ref, dk_ref, do_ref, dv_ref, dz_ref, deg_ref, dp_ref, dlg_ref):
        @pl.when((pl.program_id(0) == 0) & (pl.program_id(1) == 0))
        def _():
            dlg_ref[...] = jnp.zeros_like(dlg_ref)

        degs = tuple(deg_ref[c] for c in range(LOCAL_CHUNKS))
        dp, dlg = _hg_local_vjp(p_ref[...], lg_ref[...], dq_ref[...], dk_ref[...], do_ref[...], degs, dv_ref[...], dz_ref[...])
        dp_ref[...] = dp
        dlg_ref[...] += dlg

    slab = pl.BlockSpec((None, rows, WIDTH), lambda s, g: (s, g, 0))
    wide = pl.BlockSpec((None, rows, 4 * WIDTH), lambda s, g: (s, g, 0))
    lg = pl.BlockSpec(logits.shape, lambda s, g: (0, 0))
    return pl.pallas_call(
        body, grid=(b, seq // rows), name="hgrn2_local_bwd",
        in_specs=[wide, lg, slab, slab, slab, slab, slab, pl.BlockSpec((None, LOCAL_CHUNKS, 1, WIDTH), lambda s, g: (s, g, 0, 0))],
        out_specs=[wide, lg], out_shape=[_sds(p.shape), _sds(logits.shape)],
        compiler_params=_cparams("arbitrary", "arbitrary"),
    )(p, logits, dq, dk, do, dv, dz, deg)


def hg_local_bwd_lead(p0, logits, dq, dk, do, dv, dz, deg):
    def body(p_ref, lg_ref, dq_ref, dk_ref, do_ref, dv_ref, dz_ref, deg_ref, dp_ref, dlg_ref):
        dp_ref[...], dlg_ref[...] = _hg_local_vjp(p_ref[...], lg_ref[...], dq_ref[...], dk_ref[...], do_ref[...],
                                                  (deg_ref[...],), dv_ref[...], dz_ref[...])

    return pl.pallas_call(
        body, name="hgrn2_local_bwd_lead", in_specs=[VMEM_SPEC] * 8, out_specs=[VMEM_SPEC] * 2,
        out_shape=[_sds(p0.shape), _sds(logits.shape)], compiler_params=pltpu.CompilerParams(vmem_limit_bytes=VMEM_LIMIT),
    )(p0, logits, dq, dk, do, dv, dz, deg)


def _halo_block(g):
    return jnp.maximum((LOCAL_CHUNKS * CHUNK // HALO) * g - 1, 0)


def _gd_window(g, p_ref, halo_ref, p0_ref):
    halo = jnp.where(g == 0, p0_ref[CHUNK - HALO:CHUNK, 0:QKV], halo_ref[...])
    return jnp.concatenate([halo, p_ref[:, 0:QKV]], axis=0)


def gd_local_fwd(p, p0, ab, cw, alog, dtb):
    b, seq, _ = p.shape
    rows = LOCAL_CHUNKS * CHUNK
    nreal = seq // CHUNK

    def body(p_ref, halo_ref, p0_ref, ab_ref, cw_ref, al_ref, dt_ref, u_ref, w_ref, qe_ref, ke_ref, qk_ref, ea_ref):
        uu, ww, qe, ke, qk, eas = gd_local(_gd_window(pl.program_id(1), p_ref, halo_ref, p0_ref), ab_ref[...], cw_ref[...],
                                           al_ref[...], dt_ref[...], inverse=_tri_y_impl)
        u_ref[...], w_ref[...], qe_ref[...], ke_ref[...] = uu, ww, qe, ke
        for c in range(LOCAL_CHUNKS):
            qk_ref[c] = qk[c * HEADS * CHUNK:(c + 1) * HEADS * CHUNK]
            ea_ref[c] = eas[c]

    const = lambda a: pl.BlockSpec(a.shape, lambda s, g: (0, 0))
    slab = pl.BlockSpec((None, rows, WIDTH), lambda s, g: (s, g, 0))
    return pl.pallas_call(
        body, grid=(b, seq // rows), name="gdn_local",
        in_specs=[pl.BlockSpec((None, rows, 4 * WIDTH), lambda s, g: (s, g, 0)),
                  pl.BlockSpec((None, HALO, QKV), lambda s, g: (s, _halo_block(g), 0)), const(p0),
                  pl.BlockSpec((None, rows, AB_PAD), lambda s, g: (s, g, 0)), const(cw), const(alog), const(dtb)],
        out_specs=[slab] * 4 + [pl.BlockSpec((None, LOCAL_CHUNKS, HEADS * CHUNK, CHUNK), lambda s, g: (s, g, 0, 0)),
                                pl.BlockSpec((None, LOCAL_CHUNKS, 1, AB_PAD), lambda s, g: (s, g, 0, 0))],
        out_shape=[_sds((b, seq, WIDTH))] * 4 + [_sds((b, nreal, HEADS * CHUNK, CHUNK)), _sds((b, nreal, 1, AB_PAD))],
        compiler_params=_cparams("arbitrary", "arbitrary"),
    )(p, p, p0, ab, cw, alog, dtb)


def _lead_window(p0_ref):
    return jnp.concatenate([jnp.zeros((HALO, QKV), F32), p0_ref[:, 0:QKV]], axis=0)


def gd_local_lead(p0, ab0, cw, alog, dtb):
    def body(p0_ref, ab_ref, cw_ref, al_ref, dt_ref, u_ref, w_ref, qe_ref, ke_ref, qk_ref, ea_ref):
        u_ref[...], w_ref[...], qe_ref[...], ke_ref[...], qk_ref[...], (ea_ref[...],) = gd_local(
            _lead_window(p0_ref), ab_ref[...], cw_ref[...], al_ref[...], dt_ref[...], inverse=_tri_y_impl)

    return pl.pallas_call(
        body, name="gdn_local_lead", in_specs=[VMEM_SPEC] * 5, out_specs=[VMEM_SPEC] * 6,
        out_shape=[_sds((CHUNK, WIDTH))] * 4 + [_sds((HEADS * CHUNK, CHUNK)), _sds((1, AB_PAD))],
        compiler_params=pltpu.CompilerParams(vmem_limit_bytes=VMEM_LIMIT),
    )(p0, ab0, cw, alog, dtb)


def _gd_scan_inputs(c, b, u_ref, w_ref, qe_ref, ke_ref, qk_ref, ea_ref, z_ref, u0_ref, w0_ref, qe0_ref, ke0_ref, qk0_ref,
                    ea0_ref, p0_ref):
    lead = c == 0
    pick = lambda real, lead_val: jnp.where(lead, _lead_slabs(lead_val, b), _load_slabs(real, b))
    pairs = [(i, h) for i in range(b) for h in range(HEADS)]
    qk = jnp.where(lead, jnp.stack([qk0_ref[h * CHUNK:(h + 1) * CHUNK, :] for _, h in pairs], axis=0),
                   jnp.stack([qk_ref[i, h * CHUNK:(h + 1) * CHUNK, :] for i, h in pairs], axis=0))
    ea = jnp.where(lead, jnp.stack([ea0_ref[:, h:h + 1] for _, h in pairs], axis=0),
                   jnp.stack([ea_ref[i, :, h:h + 1] for i, h in pairs], axis=0))
    return (pick(u_ref, u0_ref[...]), pick(w_ref, w0_ref[...]), pick(qe_ref, qe0_ref[...]), pick(ke_ref, ke0_ref[...]), qk,
            ea, pick(z_ref, p0_ref[:, QKV:QKV + WIDTH]))


def gd_scan_fwd(p, p0, local, lead, nw):
    b, seq, _ = p.shape
    nc = seq // CHUNK + 1
    slab, per_chunk, state, const = _scan_specs(b, nc, False)

    def body(u_ref, w_ref, qe_ref, ke_ref, qk_ref, ea_ref, z_ref, u0_ref, w0_ref, qe0_ref, ke0_ref, qk0_ref, ea0_ref, p0_ref,
             nw_ref, y_ref, ss_ref, st):
        c = pl.program_id(0)

        @pl.when(c == 0)
        def _():
            st[...] = jnp.zeros_like(st)

        s_in = st[...]
        _save_states(ss_ref, s_in, b)
        args = _gd_scan_inputs(c, b, u_ref, w_ref, qe_ref, ke_ref, qk_ref, ea_ref, z_ref, u0_ref, w0_ref, qe0_ref, ke0_ref,
                               qk0_ref, ea0_ref, p0_ref)
        y, s_new = gd_scan(*args, nw_ref[...], s_in)
        _store_slabs(y_ref, y, b)
        st[...] = s_new

    return pl.pallas_call(
        body, grid=(nc,), name="gdn_scan",
        in_specs=[slab(0)] * 4 + [per_chunk(HEADS * CHUNK, CHUNK), per_chunk(1, AB_PAD), slab(3)] + [const(a) for a in lead]
        + [const(p0), const(nw)],
        out_specs=[slab(0), state],
        out_shape=[_sds((b, seq, WIDTH), MXU_DTYPE), _sds((b, nc, WIDTH, DH))],
        scratch_shapes=[pltpu.VMEM((b * HEADS, DH, DH), F32)],
        compiler_params=_cparams("arbitrary"),
    )(*local, p, *lead, p0, nw)


def gd_scan_bwd(p, p0, local, lead, nw, ssave, dy):
    b, seq, _ = p.shape
    nc = seq // CHUNK + 1
    slab, per_chunk, state, const = _scan_specs(b, nc, True)

    def body(u_ref, w_ref, qe_ref, ke_ref, qk_ref, ea_ref, z_ref, u0_ref, w0_ref, qe0_ref, ke0_ref, qk0_ref, ea0_ref, p0_ref,
             nw_ref, ss_ref, dy_ref, du_ref, dw_ref, dqe_ref, dke_ref, dqk_ref, dea_ref, dz_ref, du0_ref, dw0_ref, dqe0_ref,
             dke0_ref, dqk0_ref, dea0_ref, dz0_ref, dnw_ref, dst):
        i = pl.program_id(0)
        c = nc - 1 - i

        @pl.when(i == 0)
        def _():
            dst[...] = jnp.zeros_like(dst)
            dnw_ref[...] = jnp.zeros_like(dnw_ref)

        args = _gd_scan_inputs(c, b, u_ref, w_ref, qe_ref, ke_ref, qk_ref, ea_ref, z_ref, u0_ref, w0_ref, qe0_ref, ke0_ref,
                               qk0_ref, ea0_ref, p0_ref)
        s_in = _load_states(ss_ref, b)
        _, vjp = jax.vjp(gd_scan, *args, nw_ref[...], s_in)
        dyv = jnp.where(c == 0, 0.0, _load_slabs(dy_ref, b))
        du, dw, dqe, dke, dqk, dea, dz, dnw, ds = vjp((dyv, dst[...]))
        dst[...] = ds
        dnw_ref[...] += dnw
        lane = lax.broadcasted_iota(jnp.int32, (1, AB_PAD), 1)
        dea_rows = [sum(jnp.where(lane == h, dea[j * HEADS + h], 0.0) for h in range(HEADS)) for j in range(b)]
        dqk_rows = [jnp.concatenate([dqk[j * HEADS + h] for h in range(HEADS)], axis=0) for j in range(b)]

        @pl.when(c > 0)
        def _():
            for ref, val in ((du_ref, du), (dw_ref, dw), (dqe_ref, dqe), (dke_ref, dke), (dz_ref, dz)):
                _store_slabs(ref, val, b)
            for j in range(b):
                dqk_ref[j] = dqk_rows[j]
                dea_ref[j] = dea_rows[j]

        @pl.when(c == 0)
        def _():
            for ref, val in ((du0_ref, du), (dw0_ref, dw), (dqe0_ref, dqe), (dke0_ref, dke), (dz0_ref, dz)):
                ref[...] = _sum_rows(val, b)
            dqk0_ref[...] = sum(dqk_rows[1:], dqk_rows[0])
            dea0_ref[...] = sum(dea_rows[1:], dea_rows[0])

    uu, ww, qe, ke, qk, ea = local
    return pl.pallas_call(
        body, grid=(nc,), name="gdn_scan_bwd",
        in_specs=[slab(0)] * 4 + [per_chunk(HEADS * CHUNK, CHUNK), per_chunk(1, AB_PAD), slab(3)] + [const(a) for a in lead]
        + [const(p0), const(nw), state, slab(0)],
        out_specs=[slab(0)] * 4 + [per_chunk(HEADS * CHUNK, CHUNK), per_chunk(1, AB_PAD), slab(0)] + [const(a) for a in lead]
        + [const(lead[0]), const(nw)],
        out_shape=[_sds((b, seq, WIDTH))] * 4 + [_sds(qk.shape), _sds(ea.shape), _sds((b, seq, WIDTH))]
        + [_sds(a.shape) for a in lead] + [_sds(lead[0].shape), _sds(nw.shape)],
        scratch_shapes=[pltpu.VMEM((b * HEADS, DH, DH), F32)],
        compiler_params=_cparams("arbitrary"),
    )(*local, p, *lead, p0, nw, ssave, dy)


def gd_local_bwd(p, p0, ab, cw, alog, dtb, cot, dz):
    b, seq, _ = p.shape
    rows = LOCAL_CHUNKS * CHUNK
    ng = seq // rows
    du, dw, dqe, dke, dqk, dea = cot

    def body(p_ref, halo_ref, p0_ref, ab_ref, cw_ref, al_ref, dt_ref, du_ref, dw_ref, dqe_ref, dke_ref, dqk_ref, dea_ref, dz_ref,
             dp_ref, dab_ref, dhalo0_ref, dcw_ref, dal_ref, ddt_ref, dhalo):
        i = pl.program_id(1)
        g = ng - 1 - i

        @pl.when(i == 0)
        def _():
            dhalo[...] = jnp.zeros_like(dhalo)

        @pl.when((pl.program_id(0) == 0) & (i == 0))
        def _():
            dcw_ref[...] = jnp.zeros_like(dcw_ref)
            dal_ref[...] = jnp.zeros_like(dal_ref)
            ddt_ref[...] = jnp.zeros_like(ddt_ref)

        _, vjp = jax.vjp(gd_local, _gd_window(g, p_ref, halo_ref, p0_ref), ab_ref[...], cw_ref[...], al_ref[...], dt_ref[...])
        dqk_all = jnp.concatenate([dqk_ref[c] for c in range(LOCAL_CHUNKS)], axis=0)
        deas = tuple(dea_ref[c] for c in range(LOCAL_CHUNKS))
        dxx, dab, dcw, dal, ddt = vjp((du_ref[...], dw_ref[...], dqe_ref[...], dke_ref[...], dqk_all, deas))
        dqkv = dxx[HALO:HALO + rows] + jnp.concatenate([jnp.zeros((rows - HALO, QKV), F32), dhalo[...]], axis=0)
        dhalo[...] = dxx[0:HALO]
        dhalo0_ref[...] = dxx[0:HALO]
        dp_ref[...] = jnp.concatenate([dqkv, dz_ref[...]], axis=1)
        dab_ref[...] = dab
        dcw_ref[...] += dcw
        dal_ref[...] += dal
        ddt_ref[...] += ddt

    rg = lambda i: ng - 1 - i
    const = lambda a: pl.BlockSpec(a.shape, lambda s, i: (0, 0))
    slab = pl.BlockSpec((None, rows, WIDTH), lambda s, i: (s, rg(i), 0))
    wide = pl.BlockSpec((None, rows, 4 * WIDTH), lambda s, i: (s, rg(i), 0))
    gates = pl.BlockSpec((None, rows, AB_PAD), lambda s, i: (s, rg(i), 0))
    return pl.pallas_call(
        body, grid=(b, ng), name="gdn_local_bwd",
        in_specs=[wide, pl.BlockSpec((None, HALO, QKV), lambda s, i: (s, _halo_block(rg(i)), 0)), const(p0), gates, const(cw),
                  const(alog), const(dtb), slab, slab, slab, slab,
                  pl.BlockSpec((None, LOCAL_CHUNKS, HEADS * CHUNK, CHUNK), lambda s, i: (s, rg(i), 0, 0)),
                  pl.BlockSpec((None, LOCAL_CHUNKS, 1, AB_PAD), lambda s, i: (s, rg(i), 0, 0)), slab],
        out_specs=[wide, gates, pl.BlockSpec((None, HALO, QKV), lambda s, i: (s, 0, 0)), const(cw), const(alog), const(dtb)],
        out_shape=[_sds(p.shape), _sds(ab.shape), _sds((b, HALO, QKV)), _sds(cw.shape), _sds(alog.shape), _sds(dtb.shape)],
        scratch_shapes=[pltpu.VMEM((HALO, QKV), F32)],
        compiler_params=_cparams("arbitrary", "arbitrary"),
    )(p, p, p0, ab, cw, alog, dtb, du, dw, dqe, dke, dqk, dea, dz)


def gd_local_bwd_lead(p0, ab0, cw, alog, dtb, cot, dz, dtail):
    def body(p0_ref, ab_ref, cw_ref, al_ref, dt_ref, du_ref, dw_ref, dqe_ref, dke_ref, dqk_ref, dea_ref, dz_ref, dtail_ref,
             dp_ref, dab_ref, dcw_ref, dal_ref, ddt_ref):
        _, vjp = jax.vjp(gd_local, _lead_window(p0_ref), ab_ref[...], cw_ref[...], al_ref[...], dt_ref[...])
        dxx, dab, dcw, dal, ddt = vjp((du_ref[...], dw_ref[...], dqe_ref[...], dke_ref[...], dqk_ref[...], (dea_ref[...],)))
        dqkv = dxx[HALO:HALO + CHUNK] + jnp.concatenate([jnp.zeros((CHUNK - HALO, QKV), F32), dtail_ref[...]], axis=0)
        dp_ref[...] = jnp.concatenate([dqkv, dz_ref[...]], axis=1)
        dab_ref[...], dcw_ref[...], dal_ref[...], ddt_ref[...] = dab, dcw, dal, ddt

    return pl.pallas_call(
        body, name="gdn_local_bwd_lead", in_specs=[VMEM_SPEC] * 13, out_specs=[VMEM_SPEC] * 5,
        out_shape=[_sds(p0.shape), _sds(ab0.shape), _sds(cw.shape), _sds(alog.shape), _sds(dtb.shape)],
        compiler_params=pltpu.CompilerParams(vmem_limit_bytes=VMEM_LIMIT),
    )(p0, ab0, cw, alog, dtb, *cot, dz, dtail)


def _position():
    return lax.axis_index("x"), lax.axis_index("y"), lax.axis_index("c")


def _exchange_blocks(buf, send_sems, recv_sems):
    x, y, c = _position()
    me, sibling = (x, y, c), (x, y, 1 - c)
    chips = [(1 - x, y), (x, 1 - y), (1 - x, 1 - y)]

    def block(px, py, pc):
        return buf.at[4 * px + 2 * py + pc]

    def copy(k, blk, to):
        return pltpu.make_async_remote_copy(src_ref=block(*blk), dst_ref=block(*blk), send_sem=send_sems.at[k],
                                            recv_sem=recv_sems.at[k], device_id=to, device_id_type=MESH)

    first = [copy(0, me, sibling)] + [copy(1 + j, me, (*chip, c)) for j, chip in enumerate(chips)]
    for cp in first:
        cp.start()
    passed = [copy(4 + j, (*chip, c), sibling) for j, chip in enumerate(chips)]
    for j, chip in enumerate(chips):
        copy(1 + j, (*chip, c), me).wait_recv()
        passed[j].start()
    copy(0, sibling, me).wait_recv()
    for j, chip in enumerate(chips):
        copy(4 + j, (*chip, 1 - c), me).wait_recv()
    for cp in first + passed:
        cp.wait_send()


def _gather_call(body, shard, out_shape, scratch, name):
    sems = [pltpu.SemaphoreType.DMA((N_DEV - 1,)), pltpu.SemaphoreType.DMA((N_DEV - 1,))]
    return pl.pallas_call(body, name=name, in_specs=[VMEM_SPEC], out_specs=VMEM_SPEC, out_shape=out_shape,
                          scratch_shapes=scratch + sems, compiler_params=pltpu.CompilerParams(vmem_limit_bytes=VMEM_LIMIT))(shard)


def all_gather(shard, dtype, name):
    def body(x_ref, out_ref, send_sems, recv_sems):
        x, y, c = _position()
        out_ref[4 * x + 2 * y + c] = x_ref[...].astype(dtype)
        _exchange_blocks(out_ref, send_sems, recv_sems)

    return _gather_call(body, shard, jax.ShapeDtypeStruct((N_DEV,) + shard.shape, dtype), [], name)


def all_reduce(part, name):
    def body(x_ref, sum_ref, buf, send_sems, recv_sems):
        x, y, c = _position()
        buf[4 * x + 2 * y + c] = x_ref[...]
        _exchange_blocks(buf, send_sems, recv_sems)
        acc = buf[0]
        for d in range(1, N_DEV):
            acc = acc + buf[d]
        sum_ref[...] = acc

    return _gather_call(body, part, jax.ShapeDtypeStruct(part.shape, F32), [pltpu.VMEM((N_DEV,) + part.shape, F32)], name)


def gather_rows(shard, dtype, pad_rows, name):
    rows, cols = shard.shape

    def body(x_ref, out_ref, buf, send_sems, recv_sems):
        x, y, c = _position()
        buf[4 * x + 2 * y + c] = x_ref[...].astype(dtype)
        _exchange_blocks(buf, send_sems, recv_sems)
        for d in range(N_DEV):
            out_ref[pl.ds(d * rows, rows), :] = buf[d]
        out_ref[pl.ds(N_DEV * rows, pad_rows), :] = jnp.zeros((pad_rows, cols), dtype)

    return _gather_call(body, shard, jax.ShapeDtypeStruct((N_DEV * rows + pad_rows, cols), dtype),
                        [pltpu.VMEM((N_DEV, rows, cols), dtype)], name)


def reduce_scatter_rows(parts, block_rows, name):
    cols = parts[0][0].shape[1]
    arrays = [a for a, _ in parts]
    n_parts = len(arrays)

    def pieces(j):
        out, base = [], 0
        for pi, (_, valid) in enumerate(parts):
            lo, hi = max(j * block_rows, base), min((j + 1) * block_rows, base + valid)
            if lo < hi:
                out.append((pi, lo - base, lo - j * block_rows, hi - lo))
            base += valid
        return out

    def body(*refs):
        part_refs, o_ref = refs[:n_parts], refs[n_parts]
        send1, recv1, send2, recv2, s1_sems, r1_sems, s2_sems, r2_sems = refs[n_parts + 1:]
        x, y, c = _position()
        chip = 2 * x + y

        def put(dst, j, add=None):
            for pi, src_row, dst_row, size in pieces(j):
                v = part_refs[pi][pl.ds(src_row, size), :]
                if add is not None:
                    v = v + add[pl.ds(dst_row, size), :].astype(F32)
                dst[pl.ds(dst_row, size), :] = v.astype(dst.dtype)

        for j in range(N_DEV):
            @pl.when((j & 1) != c)
            def _():
                put(send1.at[j >> 1], j)

        swaps = [pltpu.make_async_remote_copy(src_ref=send1.at[k], dst_ref=recv1.at[k], send_sem=s1_sems.at[k],
                                              recv_sem=r1_sems.at[k], device_id=(x, y, 1 - c), device_id_type=MESH)
                 for k in range(4)]
        for cp in swaps:
            cp.start()

        def to_chip(k):
            return pltpu.make_async_remote_copy(src_ref=send2.at[k], dst_ref=recv2.at[chip], send_sem=s2_sems.at[k],
                                                recv_sem=r2_sems.at[chip], device_id=(k >> 1, k & 1, c), device_id_type=MESH)

        for k in range(4):
            swaps[k].wait_recv()
            for j in (2 * k, 2 * k + 1):
                @pl.when(((j & 1) == c) & (k != chip))
                def _():
                    put(send2.at[k], j, add=recv1.at[k])
                    to_chip(k).start()

                @pl.when(((j & 1) == c) & (k == chip))
                def _():
                    put(o_ref, j, add=recv1.at[k])

        for k in range(4):
            @pl.when(k != chip)
            def _():
                pltpu.make_async_remote_copy(src_ref=send2.at[k], dst_ref=recv2.at[k], send_sem=s2_sems.at[k],
                                             recv_sem=r2_sems.at[k], device_id=(k >> 1, k & 1, c),
                                             device_id_type=MESH).wait_recv()
                o_ref[...] += recv2[k].astype(F32)

        for k in range(4):
            @pl.when(k != chip)
            def _():
                to_chip(k).wait_send()

        for cp in swaps:
            cp.wait_send()

    slots = pltpu.VMEM((4, block_rows, cols), MXU_DTYPE)
    sem = pltpu.SemaphoreType.DMA((4,))
    return pl.pallas_call(
        body, name=name, in_specs=[VMEM_SPEC] * n_parts, out_specs=VMEM_SPEC,
        out_shape=jax.ShapeDtypeStruct((block_rows, cols), F32),
        scratch_shapes=[slots, slots, slots, slots, sem, sem, sem, sem],
        compiler_params=pltpu.CompilerParams(vmem_limit_bytes=VMEM_LIMIT),
    )(*arrays)


def adamw(w, g, m, v, name):
    rows, cols = w.shape
    tr = 256 if rows % 256 == 0 else rows

    def body(w_ref, g_ref, m_ref, v_ref, d_ref, nm_ref, nv_ref):
        gv = g_ref[...]
        mn = ADAM_B1 * m_ref[...] + (1.0 - ADAM_B1) * gv
        vn = ADAM_B2 * v_ref[...] + (1.0 - ADAM_B2) * jnp.square(gv)
        m_hat = mn / (1.0 - ADAM_B1 ** ADAM_STEP)
        v_hat = vn / (1.0 - ADAM_B2 ** ADAM_STEP)
        d_ref[...] = -ADAM_LR * (m_hat / (jnp.sqrt(v_hat) + ADAM_EPS) + ADAM_WD * w_ref[...])
        nm_ref[...] = mn
        nv_ref[...] = vn

    spec = pl.BlockSpec((tr, cols), lambda i: (i, 0))
    shape = jax.ShapeDtypeStruct((rows, cols), F32)
    return pl.pallas_call(body, grid=(rows // tr,), name=name, in_specs=[spec] * 4, out_specs=[spec] * 3,
                          out_shape=[shape] * 3, compiler_params=_cparams("arbitrary"))(w, g, m, v)


def _pad_rows(a, rows=8):
    return jnp.pad(a, ((0, rows - a.shape[0]), (0, 0)))


def _pad_lanes(a, lanes=128):
    return jnp.pad(a, ((0, 0), (0, lanes - a.shape[1])))


def kernel(x, meta_tokens, norm_w, w_in, conv_w, hg_lb_logits, hg_norm_w, gdn_A_log, gdn_dt_bias, gdn_norm_w, w_out, final_norm_w, loss_target, m_meta_tokens, m_norm_w, m_w_in, m_conv_w, m_hg_lb_logits, m_hg_norm_w, m_gdn_A_log, m_gdn_dt_bias, m_gdn_norm_w, m_w_out, m_final_norm_w, v_meta_tokens, v_norm_w, v_w_in, v_conv_w, v_hg_lb_logits, v_hg_norm_w, v_gdn_A_log, v_gdn_dt_bias, v_gdn_norm_w, v_w_out, v_final_norm_w):
    b, seq, _ = x.shape
    n = b * seq
    dev = 4 * lax.axis_index("x") + 2 * lax.axis_index("y") + lax.axis_index("c")
    col_shard = IN_COLS // N_DEV

    w_t = gather_rows(w_in[0].T, MXU_DTYPE, AB_PAD - 2 * HEADS, "gather_w_in")
    w_out_g = all_gather(w_out[0], MXU_DTYPE, "gather_w_out")
    small_w = jnp.concatenate([_pad_lanes(meta_tokens, 256), _pad_rows(_pad_lanes(conv_w[0], 256))], axis=0)
    small_g = all_gather(small_w, F32, "gather_small")
    meta_g = small_g[:, 0:N_META, 0:D_MODEL // N_DEV]
    conv_g = small_g[:, N_META:N_META + CONV_TAPS, 0:QKV // N_DEV]
    w_out_full = w_out_g.reshape(2 * WIDTH, D_MODEL)
    cw = jnp.transpose(conv_g, (1, 0, 2)).reshape(CONV_TAPS, QKV)
    meta = jnp.transpose(meta_g, (1, 0, 2)).reshape(N_META, D_MODEL)
    alog = _pad_lanes(gdn_A_log)
    dtb = _pad_lanes(gdn_dt_bias)
    fw = final_norm_w.reshape(1, D_MODEL)

    h0 = jnp.concatenate([jnp.zeros((CHUNK - N_META, D_MODEL), F32), meta], axis=0)
    x2 = x.reshape(n, D_MODEL)
    u0, phg0, pgd0, pab0 = in_proj(h0, norm_w, w_t, "in_proj_lead")
    u, phg, pgd, pab = in_proj(x2, norm_w, w_t, "in_proj")
    phg3, pgd3, pab3 = phg.reshape(b, seq, 4 * WIDTH), pgd.reshape(b, seq, 4 * WIDTH), pab.reshape(b, seq, AB_PAD)
    hg_loc = hg_local_fwd(phg3, hg_lb_logits)
    hg_lead = hg_local_lead(phg0, hg_lb_logits)
    y_hg, s_hg = hg_scan_fwd(phg3, phg0, hg_loc, hg_lead, hg_norm_w)
    gd_loc = gd_local_fwd(pgd3, pgd0, pab3, cw, alog, dtb)
    gd_lead = gd_local_lead(pgd0, pab0, cw, alog, dtb)
    y_gd, s_gd = gd_scan_fwd(pgd3, pgd0, gd_loc, gd_lead, gdn_norm_w)

    dh2, dy_hg, dy_gd, g_w_out, loss_part, g_fw = out_proj_loss(
        x2, loss_target.reshape(n, D_MODEL), y_hg.reshape(n, WIDTH), y_gd.reshape(n, WIDTH), w_out_full, fw)

    hb = hg_scan_bwd(phg3, phg0, hg_loc, hg_lead, hg_norm_w, s_hg, dy_hg.reshape(b, seq, WIDTH))
    dphg, g_lb = hg_local_bwd(phg3, hg_lb_logits, *hb[0:6])
    dphg0, g_lb0 = hg_local_bwd_lead(phg0, hg_lb_logits, *hb[6:12])
    g_hg_nw = hb[12]
    gb = gd_scan_bwd(pgd3, pgd0, gd_loc, gd_lead, gdn_norm_w, s_gd, dy_gd.reshape(b, seq, WIDTH))
    dpgd, dpab, dtail, g_cw, g_alog, g_dtb = gd_local_bwd(pgd3, pgd0, pab3, cw, alog, dtb, gb[0:6], gb[6])
    dpgd0, dpab0, g_cw0, g_alog0, g_dtb0 = gd_local_bwd_lead(pgd0, pab0, cw, alog, dtb, gb[7:13], gb[13], dtail.sum(0))
    g_gd_nw = gb[14]
    dphg, dpgd, dpab = dphg.reshape(n, 4 * WIDTH), dpgd.reshape(n, 4 * WIDTH), dpab.reshape(n, AB_PAD)

    grad_x, g_nw = in_proj_bwd(dphg, dpgd, dpab, w_t, x2, dh2, norm_w, "in_proj_bwd")
    dh0, g_nw0 = in_proj_bwd(dphg0, dpgd0, dpab0, w_t, h0, jnp.zeros_like(h0), norm_w, "in_proj_bwd_lead")
    g_w_hg = weight_grad(u, dphg, u0, dphg0, "w_in_grad_hg")
    g_w_gd = weight_grad(u, dpgd, u0, dpgd0, "w_in_grad_gd")
    g_w_ab = weight_grad(u, dpab, u0, dpab0, "w_in_grad_ab")

    g_w_in_t = reduce_scatter_rows([(g_w_hg, 4 * WIDTH), (g_w_gd, 4 * WIDTH), (g_w_ab, 2 * HEADS)], col_shard,
                                   "reduce_w_in")
    g_w_out = reduce_scatter_rows([(g_w_out, 2 * WIDTH)], (2 * WIDTH) // N_DEV, "reduce_w_out")
    small = jnp.concatenate([
        (g_nw + g_nw0).reshape(8, 128), (g_lb + g_lb0).reshape(8, 128), _pad_rows(g_hg_nw), _pad_rows(g_alog + g_alog0),
        _pad_rows(g_dtb + g_dtb0), _pad_rows(g_gd_nw), g_fw.reshape(8, 128), (g_cw + g_cw0).reshape(48, 128),
        dh0[CHUNK - N_META:CHUNK].reshape(128, 128), loss_part], axis=0)
    small = all_reduce(small, "reduce_small")
    g_norm_w = small[0:8].reshape(1, D_MODEL)
    g_lb = small[8:16].reshape(2, WIDTH)
    g_hg_nw = small[16:17]
    g_alog = small[24:25, 0:HEADS]
    g_dtb = small[32:33, 0:HEADS]
    g_gd_nw = small[40:41]
    g_fw = small[48:56].reshape(1, D_MODEL)
    g_cw_full = small[56:104].reshape(CONV_TAPS, QKV)
    g_meta_full = small[104:232].reshape(N_META, D_MODEL)
    loss = small[232, 0]
    g_conv = lax.dynamic_slice_in_dim(g_cw_full, dev * (QKV // N_DEV), QKV // N_DEV, axis=1)
    g_meta = lax.dynamic_slice_in_dim(g_meta_full, dev * (D_MODEL // N_DEV), D_MODEL // N_DEV, axis=1)

    names = ["meta_tokens", "norm_w", "w_in", "conv_w", "hg_lb_logits", "hg_norm_w", "gdn_A_log", "gdn_dt_bias",
             "gdn_norm_w", "w_out", "final_norm_w"]
    weights = [meta_tokens, norm_w, w_in, conv_w, hg_lb_logits, hg_norm_w, gdn_A_log, gdn_dt_bias, gdn_norm_w, w_out,
               final_norm_w]
    moms = [m_meta_tokens, m_norm_w, m_w_in, m_conv_w, m_hg_lb_logits, m_hg_norm_w, m_gdn_A_log, m_gdn_dt_bias,
            m_gdn_norm_w, m_w_out, m_final_norm_w]
    vars_ = [v_meta_tokens, v_norm_w, v_w_in, v_conv_w, v_hg_lb_logits, v_hg_norm_w, v_gdn_A_log, v_gdn_dt_bias,
             v_gdn_norm_w, v_w_out, v_final_norm_w]
    grads2d = [g_meta, g_norm_w, g_w_in_t, g_conv, g_lb, g_hg_nw, g_alog, g_dtb, g_gd_nw, g_w_out, g_fw]
    grads, deltas, new_ms, new_vs = [], [], [], []
    for nm, w, g2, m, v in zip(names, weights, grads2d, moms, vars_):
        if nm == "w_in":
            to2d, back = (lambda a: a[0].T), (lambda a: a.T[None])
        else:
            to2d, back = (lambda a, s=g2.shape: a.reshape(s)), (lambda a, s=w.shape: a.reshape(s))
        d, nm_, nv_ = adamw(to2d(w), g2, to2d(m), to2d(v), "adamw_" + nm)
        grads.append(back(g2))
        deltas.append(back(d))
        new_ms.append(back(nm_))
        new_vs.append(back(nv_))
    return (loss, grad_x.reshape(x.shape), *grads, *deltas, *new_ms, *new_vs)
```

```python
import jax
import jax.numpy as jnp
from jax import lax
from jax.experimental import pallas as pl
from jax.experimental.pallas import tpu as pltpu

F32 = jnp.float32
BF16 = jnp.bfloat16
MXU_DTYPE = BF16

D_MODEL = 1024
N_META = 16
CHUNK = 64
SUB = 16
HEADS = 4
DH = 128
WIDTH = HEADS * DH
QKV = 3 * WIDTH
CONV_TAPS = 4
HALO = 8
EPS = 1e-6
IN_COLS = 4 * WIDTH + 4 * WIDTH + 2 * HEADS
AB_PAD = 128
N_DEV = 8
LOCAL_CHUNKS = 2
VMEM_LIMIT = 56 * 1024 * 1024

ADAM_LR = 0.001
ADAM_B1 = 0.9
ADAM_B2 = 0.999
ADAM_EPS = 1e-08
ADAM_WD = 0.01
ADAM_STEP = 10

VMEM_SPEC = pl.BlockSpec(memory_space=pltpu.VMEM)
MESH = pl.DeviceIdType.MESH


def _mm_tn(a, b):
    return lax.dot_general(a.astype(MXU_DTYPE), b.astype(MXU_DTYPE), (((0,), (0,)), ((), ())), preferred_element_type=F32)


def _bmm(a, b):
    return lax.dot_general(a.astype(MXU_DTYPE), b.astype(MXU_DTYPE), (((2,), (1,)), ((0,), (0,))), preferred_element_type=F32)


def _bmm_nt(a, b):
    return lax.dot_general(a.astype(MXU_DTYPE), b.astype(MXU_DTYPE), (((2,), (2,)), ((0,), (0,))), preferred_element_type=F32)


def _bmm_tn(a, b):
    return lax.dot_general(a.astype(MXU_DTYPE), b.astype(MXU_DTYPE), (((1,), (1,)), ((0,), (0,))), preferred_element_type=F32)


def _iota2(n, m):
    return lax.broadcasted_iota(jnp.int32, (n, m), 0), lax.broadcasted_iota(jnp.int32, (n, m), 1)


def _silu(x):
    return x * jax.nn.sigmoid(x)


def _gated_norm(o, z, nw):
    return o * lax.rsqrt(jnp.mean(o * o, axis=-1, keepdims=True) + EPS) * nw * _silu(z)


def _heads(a, nb):
    return jnp.stack([a[c * CHUNK:(c + 1) * CHUNK, h * DH:(h + 1) * DH] for c in range(nb) for h in range(HEADS)], axis=0)


def _unheads(a3, nb):
    return jnp.concatenate(
        [jnp.concatenate([a3[c * HEADS + h] for h in range(HEADS)], axis=1) for c in range(nb)], axis=0)


def _split3(x):
    hi = x.astype(BF16)
    r1 = x - hi.astype(F32)
    mid = r1.astype(BF16)
    return hi, mid, (r1 - mid.astype(F32)).astype(BF16)


def _select_mm(pattern, n_out, n_in, transposed, x):
    rows, inner = (n_in, n_out) if transposed else (n_out, n_in)
    r, c = _iota2(rows, 3 * inner)
    c = c - jnp.where(c >= inner, inner, 0) - jnp.where(c >= 2 * inner, inner, 0)
    s = jnp.where(pattern(c, r) if transposed else pattern(r, c), 1.0, 0.0).astype(BF16)
    return jnp.dot(s, jnp.concatenate(_split3(x), axis=0), preferred_element_type=F32)


def _select_rows(pattern, n_out, x):
    @jax.custom_vjp
    def apply(v):
        return _select_mm(pattern, n_out, CHUNK, False, v)

    apply.defvjp(lambda v: (_select_mm(pattern, n_out, CHUNK, False, v), None),
                 lambda _, d: (_select_mm(pattern, n_out, CHUNK, True, d),))
    return apply(x)


def _cumsum_chunks(x, nb):
    return jnp.concatenate([_select_rows(lambda i, j: j <= i, CHUNK, x[c * CHUNK:(c + 1) * CHUNK]) for c in range(nb)], axis=0)


HG_LEVELS = 6


def _hg_sums(i, j):
    lvl, t = i >> HG_LEVELS, i & (CHUNK - 1)
    last = t
    for l in range(1, HG_LEVELS + 1):
        width = HG_LEVELS + 1 - l
        last = jnp.where(lvl == l, ((t >> width) << width) + (CHUNK >> l) - 1, last)
    return j <= last


def hg_local(p, logits):
    nb = p.shape[0] // CHUNK
    l0, l1 = logits[0:1], logits[1:2]
    mx = jnp.maximum(l0, l1)
    e0, e1 = jnp.exp(l0 - mx), jnp.exp(l1 - mx)
    lb = e0 / (e0 + e1)
    q = _silu(p[:, 0:WIDTH])
    f = lb + (1.0 - lb) * jax.nn.sigmoid(p[:, WIDTH:2 * WIDTH])
    k = 1.0 - f
    logf = jnp.log(f)
    sums = [_select_rows(_hg_sums, (HG_LEVELS + 1) * CHUNK, logf[c * CHUNK:(c + 1) * CHUNK]) for c in range(nb)]
    level = lambda l: _heads(jnp.concatenate([s[l * CHUNK:(l + 1) * CHUNK] for s in sums], axis=0), nb)
    q3, k3, v3, g3 = _heads(q, nb), _heads(k, nb), _heads(p[:, 2 * WIDTH:3 * WIDTH], nb), level(0)
    r, c = _iota2(CHUNK, CHUNK)
    row = lax.broadcasted_iota(jnp.int32, (CHUNK, DH), 0)
    a = jnp.where(r == c, _bmm_nt(q3, k3), 0.0)
    for l in range(1, HG_LEVELS + 1):
        sh = HG_LEVELS - l
        qk = jnp.where(((row >> sh) & 1) == 1, q3, k3) * jnp.exp(-jnp.abs(g3 - level(l)))
        pair = ((r >> (sh + 1)) == (c >> (sh + 1))) & (((r >> sh) & 1) == 1) & (((c >> sh) & 1) == 0)
        a = a + jnp.where(pair, _bmm_nt(qk, qk), 0.0)
    o = _bmm(a, v3)
    glast = g3[:, CHUNK - 1:CHUNK, :]
    egs = tuple(jnp.concatenate([jnp.exp(glast[c * HEADS + h]) for h in range(HEADS)], axis=1) for c in range(nb))
    return _unheads(q3 * jnp.exp(g3), nb), _unheads(k3 * jnp.exp(glast - g3), nb), _unheads(o, nb), egs


def hg_scan(q_in, k_out, v, eg, o_intra, z, nw, st):
    o = o_intra + _bmm_nt(q_in, st)
    return _gated_norm(o, z, nw), st * eg + _bmm_tn(v, k_out)


def _tri_y_impl(a):
    r, c = _iota2(CHUNK, CHUNK)
    same16 = (r // SUB) == (c // SUB)
    same32 = (r // (2 * SUB)) == (c // (2 * SUB))
    a0 = jnp.where(same16, a, 0.0)
    y = -a0
    pw = _bmm(a0, a0)
    for _ in range(2):
        y = y + pw + _bmm(y, pw)
        pw = _bmm(pw, pw)
    y = y + pw + _bmm(y, pw)
    for ak in (jnp.where(same32 & jnp.logical_not(same16), a, 0.0), jnp.where(same32, 0.0, a)):
        m = ak + _bmm(y, ak)
        y = y - (m + _bmm(m, y))
    return y


@jax.custom_vjp
def _tri_y(a):
    return _tri_y_impl(a)


def _tri_y_fwd(a):
    y = _tri_y_impl(a)
    return y, y


def _tri_y_bwd(y, dy):
    n = dy + _bmm_tn(y, dy)
    return (-(n + _bmm_nt(n, y)),)


_tri_y.defvjp(_tri_y_fwd, _tri_y_bwd)


def gd_local(xx, ab, cw, alog, dtb, inverse=_tri_y):
    n = ab.shape[0]
    nb = n // CHUNK
    conv = cw[0:1] * xx[HALO - 3:HALO - 3 + n]
    for j in range(1, CONV_TAPS):
        conv = conv + cw[j:j + 1] * xx[HALO - 3 + j:HALO - 3 + j + n]
    act = _silu(conv)
    x = ab + dtb
    g_all = -jnp.exp(alog) * (jnp.maximum(x, 0.0) + jnp.log1p(jnp.exp(-jnp.abs(x))))
    beta_all = jax.nn.sigmoid(ab)
    gam_all = _cumsum_chunks(g_all, nb)
    q3, k3, v3 = _heads(act[:, 0:WIDTH], nb), _heads(act[:, WIDTH:2 * WIDTH], nb), _heads(act[:, 2 * WIDTH:QKV], nb)
    q3 = q3 * lax.rsqrt(jnp.sum(q3 * q3, axis=-1, keepdims=True) + EPS) * (DH ** -0.5)
    k3 = k3 * lax.rsqrt(jnp.sum(k3 * k3, axis=-1, keepdims=True) + EPS)
    pairs = [(c, h) for c in range(nb) for h in range(HEADS)]
    beta = jnp.stack([beta_all[c * CHUNK:(c + 1) * CHUNK, HEADS + h:HEADS + h + 1] for c, h in pairs], axis=0)
    gam = jnp.stack([gam_all[c * CHUNK:(c + 1) * CHUNK, h:h + 1] for c, h in pairs], axis=0)
    gam_t = [gam_all[c * CHUNK:(c + 1) * CHUNK].T for c in range(nb)]
    gam_row = jnp.stack([gam_t[c][h:h + 1, :] for c, h in pairs], axis=0)
    glast = gam[:, CHUNK - 1:CHUNK, :]
    r, c = _iota2(CHUNK, CHUNK)
    dec = jnp.exp(jnp.where(c < r, gam - gam_row, -jnp.inf))
    y = inverse(beta * _bmm_nt(k3, k3) * dec)
    eg = jnp.exp(gam)
    rhs = jnp.concatenate([beta * v3, (beta * eg) * k3], axis=2)
    sol = rhs + _bmm(y, rhs)
    qk = _bmm_nt(q3, k3) * jnp.where(r == c, 1.0, dec)
    eas = tuple(jnp.exp(gam_all[(c + 1) * CHUNK - 1:(c + 1) * CHUNK]) for c in range(nb))
    return (_unheads(sol[:, :, 0:DH], nb), _unheads(sol[:, :, DH:2 * DH], nb), _unheads(q3 * eg, nb),
            _unheads(k3 * jnp.exp(glast - gam), nb), jnp.concatenate([qk[g] for g in range(nb * HEADS)], axis=0), eas)


def gd_scan(uu, ww, qe, ke, qk, ea, z, nw, s):
    u = uu - _bmm(ww, s)
    o = _bmm(qe, s) + _bmm(qk, u)
    return _gated_norm(o, z, nw), ea * s + _bmm_tn(ke, u)


def _cparams(*sem):
    return pltpu.CompilerParams(dimension_semantics=sem, vmem_limit_bytes=VMEM_LIMIT)


def _row_tile(n):
    for t in (256, 128, 64):
        if n % t == 0:
            return t
    raise ValueError(f"unsupported token count {n}")


def _w_in_specs():
    return [pl.BlockSpec((4 * WIDTH, D_MODEL), lambda *i: (0, 0)), pl.BlockSpec((4 * WIDTH, D_MODEL), lambda *i: (1, 0)),
            pl.BlockSpec((AB_PAD, D_MODEL), lambda *i: (8 * WIDTH // AB_PAD, 0))]


def in_proj(h, norm_w, w_t, name):
    n = h.shape[0]
    tm = _row_tile(n)
    nt = (((1,), (1,)), ((), ()))

    def body(h_ref, nw_ref, whg_ref, wgd_ref, wab_ref, u_ref, phg_ref, pgd_ref, pab_ref):
        x = h_ref[...]
        u = (x * lax.rsqrt(jnp.mean(x * x, axis=-1, keepdims=True) + EPS) * nw_ref[...]).astype(MXU_DTYPE)
        u_ref[...] = u
        phg_ref[...] = lax.dot_general(u, whg_ref[...], nt, preferred_element_type=F32)
        pgd_ref[...] = lax.dot_general(u, wgd_ref[...], nt, preferred_element_type=F32)
        pab_ref[...] = lax.dot_general(u, wab_ref[...], nt, preferred_element_type=F32)

    row = lambda w: pl.BlockSpec((tm, w), lambda i: (i, 0))
    return pl.pallas_call(
        body, grid=(n // tm,), name=name,
        in_specs=[row(D_MODEL), pl.BlockSpec(norm_w.shape, lambda i: (0, 0))] + _w_in_specs(),
        out_specs=[row(D_MODEL), row(4 * WIDTH), row(4 * WIDTH), row(AB_PAD)],
        out_shape=[jax.ShapeDtypeStruct((n, D_MODEL), MXU_DTYPE), jax.ShapeDtypeStruct((n, 4 * WIDTH), F32),
                   jax.ShapeDtypeStruct((n, 4 * WIDTH), F32), jax.ShapeDtypeStruct((n, AB_PAD), F32)],
        compiler_params=_cparams("arbitrary"),
    )(h, norm_w, w_t, w_t, w_t)


def out_proj_loss(x, tgt, y_hg, y_gd, w_out, fw):
    n = x.shape[0]
    tm = _row_tile(n)
    inv_d = 1.0 / D_MODEL

    def body(x_ref, t_ref, yh_ref, yg_ref, w_ref, fw_ref, dh_ref, dyh_ref, dyg_ref, dw_ref, loss_ref, dfw_ref):
        @pl.when(pl.program_id(0) == 0)
        def _():
            dw_ref[...] = jnp.zeros_like(dw_ref)
            loss_ref[...] = jnp.zeros_like(loss_ref)
            dfw_ref[...] = jnp.zeros_like(dfw_ref)

        yh, yg = yh_ref[...], yg_ref[...]
        wa, wb = w_ref[0:WIDTH, :], w_ref[WIDTH:2 * WIDTH, :]
        h2 = x_ref[...] + jnp.dot(yh, wa, preferred_element_type=F32) + jnp.dot(yg, wb, preferred_element_type=F32)
        r2 = lax.rsqrt(jnp.mean(h2 * h2, axis=-1, keepdims=True) + EPS)
        nrm = h2 * r2
        fwv = fw_ref[...]
        err = nrm * fwv - t_ref[...]
        loss_ref[...] += jnp.full(loss_ref.shape, 0.5 * inv_d * jnp.sum(err * err), F32)
        dout = err * inv_d
        dfw_ref[...] += jnp.sum(dout * nrm, axis=0, keepdims=True)
        dn = dout * fwv
        dh2 = r2 * (dn - nrm * jnp.mean(dn * nrm, axis=-1, keepdims=True))
        dh_ref[...] = dh2
        dhb = dh2.astype(MXU_DTYPE)
        dyh_ref[...] = lax.dot_general(dhb, wa, (((1,), (1,)), ((), ())), preferred_element_type=F32)
        dyg_ref[...] = lax.dot_general(dhb, wb, (((1,), (1,)), ((), ())), preferred_element_type=F32)
        dw_ref[0:WIDTH, :] += lax.dot_general(yh, dhb, (((0,), (0,)), ((), ())), preferred_element_type=F32)
        dw_ref[WIDTH:2 * WIDTH, :] += lax.dot_general(yg, dhb, (((0,), (0,)), ((), ())), preferred_element_type=F32)

    row = lambda w: pl.BlockSpec((tm, w), lambda i: (i, 0))
    full = lambda s: pl.BlockSpec(s, lambda i: (0, 0))
    return pl.pallas_call(
        body, grid=(n // tm,), name="out_proj_loss",
        in_specs=[row(D_MODEL), row(D_MODEL), row(WIDTH), row(WIDTH), full(w_out.shape), full(fw.shape)],
        out_specs=[row(D_MODEL), row(WIDTH), row(WIDTH), full((2 * WIDTH, D_MODEL)), full((8, 128)), full((1, D_MODEL))],
        out_shape=[jax.ShapeDtypeStruct((n, D_MODEL), F32), jax.ShapeDtypeStruct((n, WIDTH), F32),
                   jax.ShapeDtypeStruct((n, WIDTH), F32), jax.ShapeDtypeStruct((2 * WIDTH, D_MODEL), F32),
                   jax.ShapeDtypeStruct((8, 128), F32), jax.ShapeDtypeStruct((1, D_MODEL), F32)],
        compiler_params=_cparams("arbitrary"),
    )(x, tgt, y_hg, y_gd, w_out, fw)


def in_proj_bwd(dphg, dpgd, dpab, w_t, h, dh2, norm_w, name):
    n = h.shape[0]
    tm = _row_tile(n)

    def body(dphg_ref, dpgd_ref, dpab_ref, whg_ref, wgd_ref, wab_ref, h_ref, dh2_ref, nw_ref, dx_ref, dnw_ref):
        @pl.when(pl.program_id(0) == 0)
        def _():
            dnw_ref[...] = jnp.zeros_like(dnw_ref)

        du = jnp.dot(dphg_ref[...].astype(MXU_DTYPE), whg_ref[...], preferred_element_type=F32)
        du += jnp.dot(dpgd_ref[...].astype(MXU_DTYPE), wgd_ref[...], preferred_element_type=F32)
        du += jnp.dot(dpab_ref[...].astype(MXU_DTYPE), wab_ref[...], preferred_element_type=F32)
        x = h_ref[...]
        r = lax.rsqrt(jnp.mean(x * x, axis=-1, keepdims=True) + EPS)
        nrm = x * r
        dnw_ref[...] += jnp.sum(du * nrm, axis=0, keepdims=True)
        dn = du * nw_ref[...]
        dx_ref[...] = dh2_ref[...] + r * (dn - nrm * jnp.mean(dn * nrm, axis=-1, keepdims=True))

    row = lambda w: pl.BlockSpec((tm, w), lambda i: (i, 0))
    return pl.pallas_call(
        body, grid=(n // tm,), name=name,
        in_specs=[row(4 * WIDTH), row(4 * WIDTH), row(AB_PAD)] + _w_in_specs() + [row(D_MODEL), row(D_MODEL),
                                                                                   pl.BlockSpec(norm_w.shape, lambda i: (0, 0))],
        out_specs=[row(D_MODEL), pl.BlockSpec((1, D_MODEL), lambda i: (0, 0))],
        out_shape=[jax.ShapeDtypeStruct((n, D_MODEL), F32), jax.ShapeDtypeStruct((1, D_MODEL), F32)],
        compiler_params=_cparams("arbitrary"),
    )(dphg, dpgd, dpab, w_t, w_t, w_t, h, dh2, norm_w)


def weight_grad(u, dp, u0, dp0, name):
    n, w = dp.shape
    tn = min(w, 1024)
    tm = 1024 if n % 1024 == 0 else _row_tile(n)
    n0 = u0.shape[0]

    def body(u_ref, dp_ref, u0_ref, dp0_ref, o_ref):
        @pl.when(pl.program_id(1) == 0)
        def _():
            o_ref[...] = _mm_tn(dp0_ref[...], u0_ref[...])

        o_ref[...] += _mm_tn(dp_ref[...], u_ref[...])

    return pl.pallas_call(
        body, grid=(w // tn, n // tm), name=name,
        in_specs=[pl.BlockSpec((tm, D_MODEL), lambda j, t: (t, 0)), pl.BlockSpec((tm, tn), lambda j, t: (t, j)),
                  pl.BlockSpec((n0, D_MODEL), lambda j, t: (0, 0)), pl.BlockSpec((n0, tn), lambda j, t: (0, j))],
        out_specs=pl.BlockSpec((tn, D_MODEL), lambda j, t: (j, 0)),
        out_shape=jax.ShapeDtypeStruct((w, D_MODEL), F32),
        compiler_params=_cparams("arbitrary", "arbitrary"),
    )(u, dp, u0, dp0)


def _real(c):
    return jnp.maximum(c - 1, 0)


def _sds(shape, dtype=F32):
    return jax.ShapeDtypeStruct(shape, dtype)


def _load_slabs(ref, b):
    return jnp.stack([ref[i, :, h * DH:(h + 1) * DH].astype(F32) for i in range(b) for h in range(HEADS)], axis=0)


def _lead_slabs(a, b):
    return jnp.stack([a[:, h * DH:(h + 1) * DH].astype(F32) for _ in range(b) for h in range(HEADS)], axis=0)


def _rows(a3, i):
    return jnp.concatenate([a3[i * HEADS + h] for h in range(HEADS)], axis=1)


def _store_slabs(ref, a3, b):
    for i in range(b):
        ref[i] = _rows(a3, i).astype(ref.dtype)


def _sum_rows(a3, b):
    out = _rows(a3, 0)
    for i in range(1, b):
        out = out + _rows(a3, i)
    return out


def _save_states(ref, s, b):
    for i in range(b):
        ref[i] = jnp.concatenate([s[i * HEADS + h] for h in range(HEADS)], axis=0)


def _load_states(ref, b):
    return jnp.stack([ref[i, h * DH:(h + 1) * DH, :] for i in range(b) for h in range(HEADS)], axis=0)


def hg_local_fwd(p, logits):
    b, seq, _ = p.shape
    rows = LOCAL_CHUNKS * CHUNK
    nreal = seq // CHUNK

    def body(p_ref, lg_ref, q_ref, k_ref, o_ref, eg_ref):
        q_in, k_out, o_intra, egs = hg_local(p_ref[...], lg_ref[...])
        q_ref[...], k_ref[...], o_ref[...] = q_in.astype(MXU_DTYPE), k_out.astype(MXU_DTYPE), o_intra
        for c in range(LOCAL_CHUNKS):
            eg_ref[c] = egs[c]

    slab = pl.BlockSpec((None, rows, WIDTH), lambda s, g: (s, g, 0))
    return pl.pallas_call(
        body, grid=(b, seq // rows), name="hgrn2_local",
        in_specs=[pl.BlockSpec((None, rows, 4 * WIDTH), lambda s, g: (s, g, 0)), pl.BlockSpec(logits.shape, lambda s, g: (0, 0))],
        out_specs=[slab, slab, slab, pl.BlockSpec((None, LOCAL_CHUNKS, 1, WIDTH), lambda s, g: (s, g, 0, 0))],
        out_shape=[_sds((b, seq, WIDTH), MXU_DTYPE)] * 2 + [_sds((b, seq, WIDTH)), _sds((b, nreal, 1, WIDTH))],
        compiler_params=_cparams("arbitrary", "arbitrary"),
    )(p, logits)


def hg_local_lead(p0, logits):
    def body(p_ref, lg_ref, q_ref, k_ref, o_ref, eg_ref):
        q_in, k_out, o_ref[...], (eg_ref[...],) = hg_local(p_ref[...], lg_ref[...])
        q_ref[...], k_ref[...] = q_in.astype(MXU_DTYPE), k_out.astype(MXU_DTYPE)

    return pl.pallas_call(
        body, name="hgrn2_local_lead", in_specs=[VMEM_SPEC] * 2, out_specs=[VMEM_SPEC] * 4,
        out_shape=[_sds((CHUNK, WIDTH), MXU_DTYPE)] * 2 + [_sds((CHUNK, WIDTH)), _sds((1, WIDTH))],
        compiler_params=pltpu.CompilerParams(vmem_limit_bytes=VMEM_LIMIT),
    )(p0, logits)


def _hg_scan_inputs(c, b, q_ref, k_ref, o_ref, v_ref, z_ref, eg_ref, q0_ref, k0_ref, o0_ref, p0_ref, eg0_ref):
    lead = c == 0
    pick = lambda real, lead_val: jnp.where(lead, _lead_slabs(lead_val, b), _load_slabs(real, b))
    eg = jnp.where(lead, jnp.stack([eg0_ref[:, h * DH:(h + 1) * DH] for _ in range(b) for h in range(HEADS)], axis=0),
                   jnp.stack([eg_ref[i, :, h * DH:(h + 1) * DH] for i in range(b) for h in range(HEADS)], axis=0))
    return (pick(q_ref, q0_ref[...]), pick(k_ref, k0_ref[...]), pick(v_ref, p0_ref[:, 2 * WIDTH:3 * WIDTH]), eg,
            pick(o_ref, o0_ref[...]), pick(z_ref, p0_ref[:, 3 * WIDTH:4 * WIDTH]))


def _scan_specs(b, nc, reverse):
    chunk = (lambda i: nc - 1 - i) if reverse else (lambda i: i)
    slab = lambda lane_block: pl.BlockSpec((b, CHUNK, WIDTH), lambda i: (0, _real(chunk(i)), lane_block))
    per_chunk = lambda *tail: pl.BlockSpec((b, None) + tail, lambda i: (0, _real(chunk(i))) + (0,) * len(tail))
    state = pl.BlockSpec((b, None, WIDTH, DH), lambda i: (0, chunk(i), 0, 0))
    const = lambda a: pl.BlockSpec(a.shape, lambda i: (0,) * a.ndim)
    return slab, per_chunk, state, const


def hg_scan_fwd(p, p0, local, lead, nw):
    b, seq, _ = p.shape
    nc = seq // CHUNK + 1
    q_in, k_out, o_intra, eg = local
    slab, per_chunk, state, const = _scan_specs(b, nc, False)

    def body(q_ref, k_ref, o_ref, v_ref, z_ref, eg_ref, q0_ref, k0_ref, o0_ref, p0_ref, eg0_ref, nw_ref, y_ref, ss_ref, st):
        c = pl.program_id(0)

        @pl.when(c == 0)
        def _():
            st[...] = jnp.zeros_like(st)

        s_in = st[...]
        _save_states(ss_ref, s_in, b)
        args = _hg_scan_inputs(c, b, q_ref, k_ref, o_ref, v_ref, z_ref, eg_ref, q0_ref, k0_ref, o0_ref, p0_ref, eg0_ref)
        y, s_new = hg_scan(*args, nw_ref[...], s_in)
        _store_slabs(y_ref, y, b)
        st[...] = s_new

    return pl.pallas_call(
        body, grid=(nc,), name="hgrn2_scan",
        in_specs=[slab(0), slab(0), slab(0), slab(2), slab(3), per_chunk(1, WIDTH)] + [const(a) for a in lead[0:3]]
        + [const(p0), const(lead[3]), const(nw)],
        out_specs=[slab(0), state],
        out_shape=[_sds((b, seq, WIDTH), MXU_DTYPE), _sds((b, nc, WIDTH, DH))],
        scratch_shapes=[pltpu.VMEM((b * HEADS, DH, DH), F32)],
        compiler_params=_cparams("arbitrary"),
    )(q_in, k_out, o_intra, p, p, eg, lead[0], lead[1], lead[2], p0, lead[3], nw)


def hg_scan_bwd(p, p0, local, lead, nw, ssave, dy):
    b, seq, _ = p.shape
    nc = seq // CHUNK + 1
    q_in, k_out, o_intra, eg = local
    slab, per_chunk, state, const = _scan_specs(b, nc, True)

    def body(q_ref, k_ref, o_ref, v_ref, z_ref, eg_ref, q0_ref, k0_ref, o0_ref, p0_ref, eg0_ref, nw_ref, ss_ref, dy_ref,
             dq_ref, dk_ref, do_ref, dv_ref, dz_ref, deg_ref, dq0_ref, dk0_ref, do0_ref, dv0_ref, dz0_ref, deg0_ref, dnw_ref,
             dst):
        i = pl.program_id(0)
        c = nc - 1 - i

        @pl.when(i == 0)
        def _():
            dst[...] = jnp.zeros_like(dst)
            dnw_ref[...] = jnp.zeros_like(dnw_ref)

        args = _hg_scan_inputs(c, b, q_ref, k_ref, o_ref, v_ref, z_ref, eg_ref, q0_ref, k0_ref, o0_ref, p0_ref, eg0_ref)
        s_in = _load_states(ss_ref, b)
        _, vjp = jax.vjp(hg_scan, *args, nw_ref[...], s_in)
        dyv = jnp.where(c == 0, 0.0, _load_slabs(dy_ref, b))
        dq, dk, dv, deg, do, dz, dnw, ds = vjp((dyv, dst[...]))
        dst[...] = ds
        dnw_ref[...] += dnw

        @pl.when(c > 0)
        def _():
            for ref, val in ((dq_ref, dq), (dk_ref, dk), (do_ref, do), (dv_ref, dv), (dz_ref, dz)):
                _store_slabs(ref, val, b)
            for j in range(b):
                deg_ref[j] = _rows(deg, j)

        @pl.when(c == 0)
        def _():
            for ref, val in ((dq0_ref, dq), (dk0_ref, dk), (do0_ref, do), (dv0_ref, dv), (dz0_ref, dz), (deg0_ref, deg)):
                ref[...] = _sum_rows(val, b)

    lead_out = [const(a) for a in lead[0:3]] + [const(lead[0]), const(lead[0]), const(lead[3])]
    return pl.pallas_call(
        body, grid=(nc,), name="hgrn2_scan_bwd",
        in_specs=[slab(0), slab(0), slab(0), slab(2), slab(3), per_chunk(1, WIDTH)] + [const(a) for a in lead[0:3]]
        + [const(p0), const(lead[3]), const(nw), state, slab(0)],
        out_specs=[slab(0)] * 5 + [per_chunk(1, WIDTH)] + lead_out + [const(nw)],
        out_shape=[_sds((b, seq, WIDTH))] * 5 + [_sds(eg.shape)] + [_sds((CHUNK, WIDTH))] * 5 + [_sds((1, WIDTH)), _sds(nw.shape)],
        scratch_shapes=[pltpu.VMEM((b * HEADS, DH, DH), F32)],
        compiler_params=_cparams("arbitrary"),
    )(q_in, k_out, o_intra, p, p, eg, lead[0], lead[1], lead[2], p0, lead[3], nw, ssave, dy)


def _hg_local_vjp(p, logits, dq, dk, do, degs, dv, dz):
    _, vjp = jax.vjp(hg_local, p, logits)
    dp, dlg = vjp((dq, dk, do, degs))
    return dp + jnp.concatenate([jnp.zeros((p.shape[0], 2 * WIDTH), F32), dv, dz], axis=1), dlg


def hg_local_bwd(p, logits, dq, dk, do, dv, dz, deg):
    b, seq, _ = p.shape
    rows = LOCAL_CHUNKS * CHUNK

    def body(p_ref, lg_ref, dq_ref, dk_ref, do_ref, dv_ref, dz_ref, deg_ref, dp_ref, dlg_ref):
        @pl.when((pl.program_id(0) == 0) & (pl.program_id(1) == 0))
        def _():
            dlg_ref[...] = jnp.zeros_like(dlg_ref)

        degs = tuple(deg_ref[c] for c in range(LOCAL_CHUNKS))
        dp, dlg = _hg_local_vjp(p_ref[...], lg_ref[...], dq_ref[...], dk_ref[...], do_ref[...], degs, dv_ref[...], dz_ref[...])
        dp_ref[...] = dp.astype(MXU_DTYPE)
        dlg_ref[...] += dlg

    slab = pl.BlockSpec((None, rows, WIDTH), lambda s, g: (s, g, 0))
    wide = pl.BlockSpec((None, rows, 4 * WIDTH), lambda s, g: (s, g, 0))
    lg = pl.BlockSpec(logits.shape, lambda s, g: (0, 0))
    return pl.pallas_call(
        body, grid=(b, seq // rows), name="hgrn2_local_bwd",
        in_specs=[wide, lg, slab, slab, slab, slab, slab, pl.BlockSpec((None, LOCAL_CHUNKS, 1, WIDTH), lambda s, g: (s, g, 0, 0))],
        out_specs=[wide, lg], out_shape=[_sds(p.shape, MXU_DTYPE), _sds(logits.shape)],
        compiler_params=_cparams("arbitrary", "arbitrary"),
    )(p, logits, dq, dk, do, dv, dz, deg)


def hg_local_bwd_lead(p0, logits, dq, dk, do, dv, dz, deg):
    def body(p_ref, lg_ref, dq_ref, dk_ref, do_ref, dv_ref, dz_ref, deg_ref, dp_ref, dlg_ref):
        dp, dlg_ref[...] = _hg_local_vjp(p_ref[...], lg_ref[...], dq_ref[...], dk_ref[...], do_ref[...],
                                         (deg_ref[...],), dv_ref[...], dz_ref[...])
        dp_ref[...] = dp.astype(MXU_DTYPE)

    return pl.pallas_call(
        body, name="hgrn2_local_bwd_lead", in_specs=[VMEM_SPEC] * 8, out_specs=[VMEM_SPEC] * 2,
        out_shape=[_sds(p0.shape, MXU_DTYPE), _sds(logits.shape)], compiler_params=pltpu.CompilerParams(vmem_limit_bytes=VMEM_LIMIT),
    )(p0, logits, dq, dk, do, dv, dz, deg)


def _halo_block(g):
    return jnp.maximum((LOCAL_CHUNKS * CHUNK // HALO) * g - 1, 0)


def _gd_window(g, p_ref, halo_ref, p0_ref):
    halo = jnp.where(g == 0, p0_ref[CHUNK - HALO:CHUNK, 0:QKV], halo_ref[...])
    return jnp.concatenate([halo, p_ref[:, 0:QKV]], axis=0)


def gd_local_fwd(p, p0, ab, cw, alog, dtb):
    b, seq, _ = p.shape
    rows = LOCAL_CHUNKS * CHUNK
    nreal = seq // CHUNK

    def body(p_ref, halo_ref, p0_ref, ab_ref, cw_ref, al_ref, dt_ref, u_ref, w_ref, qe_ref, ke_ref, qk_ref, ea_ref):
        uu, ww, qe, ke, qk, eas = gd_local(_gd_window(pl.program_id(1), p_ref, halo_ref, p0_ref), ab_ref[...], cw_ref[...],
                                           al_ref[...], dt_ref[...], inverse=_tri_y_impl)
        u_ref[...], w_ref[...], qe_ref[...], ke_ref[...] = uu, ww.astype(MXU_DTYPE), qe.astype(MXU_DTYPE), ke.astype(MXU_DTYPE)
        for c in range(LOCAL_CHUNKS):
            qk_ref[c] = qk[c * HEADS * CHUNK:(c + 1) * HEADS * CHUNK]
            ea_ref[c] = eas[c]

    const = lambda a: pl.BlockSpec(a.shape, lambda s, g: (0, 0))
    slab = pl.BlockSpec((None, rows, WIDTH), lambda s, g: (s, g, 0))
    return pl.pallas_call(
        body, grid=(b, seq // rows), name="gdn_local",
        in_specs=[pl.BlockSpec((None, rows, 4 * WIDTH), lambda s, g: (s, g, 0)),
                  pl.BlockSpec((None, HALO, QKV), lambda s, g: (s, _halo_block(g), 0)), const(p0),
                  pl.BlockSpec((None, rows, AB_PAD), lambda s, g: (s, g, 0)), const(cw), const(alog), const(dtb)],
        out_specs=[slab] * 4 + [pl.BlockSpec((None, LOCAL_CHUNKS, HEADS * CHUNK, CHUNK), lambda s, g: (s, g, 0, 0)),
                                pl.BlockSpec((None, LOCAL_CHUNKS, 1, AB_PAD), lambda s, g: (s, g, 0, 0))],
        out_shape=[_sds((b, seq, WIDTH))] + [_sds((b, seq, WIDTH), MXU_DTYPE)] * 3
        + [_sds((b, nreal, HEADS * CHUNK, CHUNK)), _sds((b, nreal, 1, AB_PAD))],
        compiler_params=_cparams("arbitrary", "arbitrary"),
    )(p, p, p0, ab, cw, alog, dtb)


def _lead_window(p0_ref):
    return jnp.concatenate([jnp.zeros((HALO, QKV), F32), p0_ref[:, 0:QKV]], axis=0)


def gd_local_lead(p0, ab0, cw, alog, dtb):
    def body(p0_ref, ab_ref, cw_ref, al_ref, dt_ref, u_ref, w_ref, qe_ref, ke_ref, qk_ref, ea_ref):
        u_ref[...], ww, qe, ke, qk_ref[...], (ea_ref[...],) = gd_local(
            _lead_window(p0_ref), ab_ref[...], cw_ref[...], al_ref[...], dt_ref[...], inverse=_tri_y_impl)
        w_ref[...], qe_ref[...], ke_ref[...] = ww.astype(MXU_DTYPE), qe.astype(MXU_DTYPE), ke.astype(MXU_DTYPE)

    return pl.pallas_call(
        body, name="gdn_local_lead", in_specs=[VMEM_SPEC] * 5, out_specs=[VMEM_SPEC] * 6,
        out_shape=[_sds((CHUNK, WIDTH))] + [_sds((CHUNK, WIDTH), MXU_DTYPE)] * 3 + [_sds((HEADS * CHUNK, CHUNK)), _sds((1, AB_PAD))],
        compiler_params=pltpu.CompilerParams(vmem_limit_bytes=VMEM_LIMIT),
    )(p0, ab0, cw, alog, dtb)


def _gd_scan_inputs(c, b, u_ref, w_ref, qe_ref, ke_ref, qk_ref, ea_ref, z_ref, u0_ref, w0_ref, qe0_ref, ke0_ref, qk0_ref,
                    ea0_ref, p0_ref):
    lead = c == 0
    pick = lambda real, lead_val: jnp.where(lead, _lead_slabs(lead_val, b), _load_slabs(real, b))
    pairs = [(i, h) for i in range(b) for h in range(HEADS)]
    qk = jnp.where(lead, jnp.stack([qk0_ref[h * CHUNK:(h + 1) * CHUNK, :] for _, h in pairs], axis=0),
                   jnp.stack([qk_ref[i, h * CHUNK:(h + 1) * CHUNK, :] for i, h in pairs], axis=0))
    ea = jnp.where(lead, jnp.stack([ea0_ref[:, h:h + 1] for _, h in pairs], axis=0),
                   jnp.stack([ea_ref[i, :, h:h + 1] for i, h in pairs], axis=0))
    return (pick(u_ref, u0_ref[...]), pick(w_ref, w0_ref[...]), pick(qe_ref, qe0_ref[...]), pick(ke_ref, ke0_ref[...]), qk,
            ea, pick(z_ref, p0_ref[:, QKV:QKV + WIDTH]))


def gd_scan_fwd(p, p0, local, lead, nw):
    b, seq, _ = p.shape
    nc = seq // CHUNK + 1
    slab, per_chunk, state, const = _scan_specs(b, nc, False)

    def body(u_ref, w_ref, qe_ref, ke_ref, qk_ref, ea_ref, z_ref, u0_ref, w0_ref, qe0_ref, ke0_ref, qk0_ref, ea0_ref, p0_ref,
             nw_ref, y_ref, ss_ref, st):
        c = pl.program_id(0)

        @pl.when(c == 0)
        def _():
            st[...] = jnp.zeros_like(st)

        s_in = st[...]
        _save_states(ss_ref, s_in, b)
        args = _gd_scan_inputs(c, b, u_ref, w_ref, qe_ref, ke_ref, qk_ref, ea_ref, z_ref, u0_ref, w0_ref, qe0_ref, ke0_ref,
                               qk0_ref, ea0_ref, p0_ref)
        y, s_new = gd_scan(*args, nw_ref[...], s_in)
        _store_slabs(y_ref, y, b)
        st[...] = s_new

    return pl.pallas_call(
        body, grid=(nc,), name="gdn_scan",
        in_specs=[slab(0)] * 4 + [per_chunk(HEADS * CHUNK, CHUNK), per_chunk(1, AB_PAD), slab(3)] + [const(a) for a in lead]
        + [const(p0), const(nw)],
        out_specs=[slab(0), state],
        out_shape=[_sds((b, seq, WIDTH), MXU_DTYPE), _sds((b, nc, WIDTH, DH))],
        scratch_shapes=[pltpu.VMEM((b * HEADS, DH, DH), F32)],
        compiler_params=_cparams("arbitrary"),
    )(*local, p, *lead, p0, nw)


def gd_scan_bwd(p, p0, local, lead, nw, ssave, dy):
    b, seq, _ = p.shape
    nc = seq // CHUNK + 1
    slab, per_chunk, state, const = _scan_specs(b, nc, True)

    def body(u_ref, w_ref, qe_ref, ke_ref, qk_ref, ea_ref, z_ref, u0_ref, w0_ref, qe0_ref, ke0_ref, qk0_ref, ea0_ref, p0_ref,
             nw_ref, ss_ref, dy_ref, du_ref, dw_ref, dqe_ref, dke_ref, dqk_ref, dea_ref, dz_ref, du0_ref, dw0_ref, dqe0_ref,
             dke0_ref, dqk0_ref, dea0_ref, dz0_ref, dnw_ref, dst):
        i = pl.program_id(0)
        c = nc - 1 - i

        @pl.when(i == 0)
        def _():
            dst[...] = jnp.zeros_like(dst)
            dnw_ref[...] = jnp.zeros_like(dnw_ref)

        args = _gd_scan_inputs(c, b, u_ref, w_ref, qe_ref, ke_ref, qk_ref, ea_ref, z_ref, u0_ref, w0_ref, qe0_ref, ke0_ref,
                               qk0_ref, ea0_ref, p0_ref)
        s_in = _load_states(ss_ref, b)
        _, vjp = jax.vjp(gd_scan, *args, nw_ref[...], s_in)
        dyv = jnp.where(c == 0, 0.0, _load_slabs(dy_ref, b))
        du, dw, dqe, dke, dqk, dea, dz, dnw, ds = vjp((dyv, dst[...]))
        dst[...] = ds
        dnw_ref[...] += dnw
        lane = lax.broadcasted_iota(jnp.int32, (1, AB_PAD), 1)
        dea_rows = [sum(jnp.where(lane == h, dea[j * HEADS + h], 0.0) for h in range(HEADS)) for j in range(b)]
        dqk_rows = [jnp.concatenate([dqk[j * HEADS + h] for h in range(HEADS)], axis=0) for j in range(b)]

        @pl.when(c > 0)
        def _():
            for ref, val in ((du_ref, du), (dw_ref, dw), (dqe_ref, dqe), (dke_ref, dke), (dz_ref, dz)):
                _store_slabs(ref, val, b)
            for j in range(b):
                dqk_ref[j] = dqk_rows[j]
                dea_ref[j] = dea_rows[j]

        @pl.when(c == 0)
        def _():
            for ref, val in ((du0_ref, du), (dw0_ref, dw), (dqe0_ref, dqe), (dke0_ref, dke), (dz0_ref, dz)):
                ref[...] = _sum_rows(val, b)
            dqk0_ref[...] = sum(dqk_rows[1:], dqk_rows[0])
            dea0_ref[...] = sum(dea_rows[1:], dea_rows[0])

    uu, ww, qe, ke, qk, ea = local
    return pl.pallas_call(
        body, grid=(nc,), name="gdn_scan_bwd",
        in_specs=[slab(0)] * 4 + [per_chunk(HEADS * CHUNK, CHUNK), per_chunk(1, AB_PAD), slab(3)] + [const(a) for a in lead]
        + [const(p0), const(nw), state, slab(0)],
        out_specs=[slab(0)] * 4 + [per_chunk(HEADS * CHUNK, CHUNK), per_chunk(1, AB_PAD), slab(0)] + [const(a) for a in lead]
        + [const(lead[0]), const(nw)],
        out_shape=[_sds((b, seq, WIDTH))] * 4 + [_sds(qk.shape), _sds(ea.shape), _sds((b, seq, WIDTH))]
        + [_sds(a.shape) for a in lead] + [_sds(lead[0].shape), _sds(nw.shape)],
        scratch_shapes=[pltpu.VMEM((b * HEADS, DH, DH), F32)],
        compiler_params=_cparams("arbitrary"),
    )(*local, p, *lead, p0, nw, ssave, dy)


def gd_local_bwd(p, p0, ab, cw, alog, dtb, cot, dz):
    b, seq, _ = p.shape
    rows = LOCAL_CHUNKS * CHUNK
    ng = seq // rows
    du, dw, dqe, dke, dqk, dea = cot

    def body(p_ref, halo_ref, p0_ref, ab_ref, cw_ref, al_ref, dt_ref, du_ref, dw_ref, dqe_ref, dke_ref, dqk_ref, dea_ref, dz_ref,
             dp_ref, dab_ref, dhalo0_ref, dcw_ref, dal_ref, ddt_ref, dhalo):
        i = pl.program_id(1)
        g = ng - 1 - i

        @pl.when(i == 0)
        def _():
            dhalo[...] = jnp.zeros_like(dhalo)

        @pl.when((pl.program_id(0) == 0) & (i == 0))
        def _():
            dcw_ref[...] = jnp.zeros_like(dcw_ref)
            dal_ref[...] = jnp.zeros_like(dal_ref)
            ddt_ref[...] = jnp.zeros_like(ddt_ref)

        _, vjp = jax.vjp(gd_local, _gd_window(g, p_ref, halo_ref, p0_ref), ab_ref[...], cw_ref[...], al_ref[...], dt_ref[...])
        dqk_all = jnp.concatenate([dqk_ref[c] for c in range(LOCAL_CHUNKS)], axis=0)
        deas = tuple(dea_ref[c] for c in range(LOCAL_CHUNKS))
        dxx, dab, dcw, dal, ddt = vjp((du_ref[...], dw_ref[...], dqe_ref[...], dke_ref[...], dqk_all, deas))
        dqkv = dxx[HALO:HALO + rows] + jnp.concatenate([jnp.zeros((rows - HALO, QKV), F32), dhalo[...]], axis=0)
        dhalo[...] = dxx[0:HALO]
        dhalo0_ref[...] = dxx[0:HALO]
        dp_ref[...] = jnp.concatenate([dqkv, dz_ref[...]], axis=1).astype(MXU_DTYPE)
        dab_ref[...] = dab.astype(MXU_DTYPE)
        dcw_ref[...] += dcw
        dal_ref[...] += dal
        ddt_ref[...] += ddt

    rg = lambda i: ng - 1 - i
    const = lambda a: pl.BlockSpec(a.shape, lambda s, i: (0, 0))
    slab = pl.BlockSpec((None, rows, WIDTH), lambda s, i: (s, rg(i), 0))
    wide = pl.BlockSpec((None, rows, 4 * WIDTH), lambda s, i: (s, rg(i), 0))
    gates = pl.BlockSpec((None, rows, AB_PAD), lambda s, i: (s, rg(i), 0))
    return pl.pallas_call(
        body, grid=(b, ng), name="gdn_local_bwd",
        in_specs=[wide, pl.BlockSpec((None, HALO, QKV), lambda s, i: (s, _halo_block(rg(i)), 0)), const(p0), gates, const(cw),
                  const(alog), const(dtb), slab, slab, slab, slab,
                  pl.BlockSpec((None, LOCAL_CHUNKS, HEADS * CHUNK, CHUNK), lambda s, i: (s, rg(i), 0, 0)),
                  pl.BlockSpec((None, LOCAL_CHUNKS, 1, AB_PAD), lambda s, i: (s, rg(i), 0, 0)), slab],
        out_specs=[wide, gates, pl.BlockSpec((None, HALO, QKV), lambda s, i: (s, 0, 0)), const(cw), const(alog), const(dtb)],
        out_shape=[_sds(p.shape, MXU_DTYPE), _sds(ab.shape, MXU_DTYPE), _sds((b, HALO, QKV)), _sds(cw.shape), _sds(alog.shape),
                   _sds(dtb.shape)],
        scratch_shapes=[pltpu.VMEM((HALO, QKV), F32)],
        compiler_params=_cparams("arbitrary", "arbitrary"),
    )(p, p, p0, ab, cw, alog, dtb, du, dw, dqe, dke, dqk, dea, dz)


def gd_local_bwd_lead(p0, ab0, cw, alog, dtb, cot, dz, dtail):
    def body(p0_ref, ab_ref, cw_ref, al_ref, dt_ref, du_ref, dw_ref, dqe_ref, dke_ref, dqk_ref, dea_ref, dz_ref, dtail_ref,
             dp_ref, dab_ref, dcw_ref, dal_ref, ddt_ref):
        _, vjp = jax.vjp(gd_local, _lead_window(p0_ref), ab_ref[...], cw_ref[...], al_ref[...], dt_ref[...])
        dxx, dab, dcw, dal, ddt = vjp((du_ref[...], dw_ref[...], dqe_ref[...], dke_ref[...], dqk_ref[...], (dea_ref[...],)))
        dqkv = dxx[HALO:HALO + CHUNK] + jnp.concatenate([jnp.zeros((CHUNK - HALO, QKV), F32), dtail_ref[...]], axis=0)
        dp_ref[...] = jnp.concatenate([dqkv, dz_ref[...]], axis=1).astype(MXU_DTYPE)
        dab_ref[...], dcw_ref[...], dal_ref[...], ddt_ref[...] = dab.astype(MXU_DTYPE), dcw, dal, ddt

    return pl.pallas_call(
        body, name="gdn_local_bwd_lead", in_specs=[VMEM_SPEC] * 13, out_specs=[VMEM_SPEC] * 5,
        out_shape=[_sds(p0.shape, MXU_DTYPE), _sds(ab0.shape, MXU_DTYPE), _sds(cw.shape), _sds(alog.shape), _sds(dtb.shape)],
        compiler_params=pltpu.CompilerParams(vmem_limit_bytes=VMEM_LIMIT),
    )(p0, ab0, cw, alog, dtb, *cot, dz, dtail)


def _position():
    return lax.axis_index("x"), lax.axis_index("y"), lax.axis_index("c")


def _exchange_blocks(bufs, send_sems, recv_sems):
    x, y, c = _position()
    me, sibling = (x, y, c), (x, y, 1 - c)
    chips = [(1 - x, y), (x, 1 - y), (1 - x, 1 - y)]
    per_buf = N_DEV - 1

    def copy(a, k, blk, to):
        rows = bufs[a].at[4 * blk[0] + 2 * blk[1] + blk[2]]
        return pltpu.make_async_remote_copy(src_ref=rows, dst_ref=rows, send_sem=send_sems.at[a * per_buf + k],
                                            recv_sem=recv_sems.at[a * per_buf + k], device_id=to, device_id_type=MESH)

    bufs_idx = range(len(bufs))
    first = [copy(a, 0, me, sibling) for a in bufs_idx] + [copy(a, 1 + j, me, (*chip, c)) for a in bufs_idx
                                                           for j, chip in enumerate(chips)]
    for cp in first:
        cp.start()
    passed = []
    for j, chip in enumerate(chips):
        for a in bufs_idx:
            copy(a, 1 + j, (*chip, c), me).wait_recv()
            passed.append(copy(a, 4 + j, (*chip, c), sibling))
            passed[-1].start()
    for a in bufs_idx:
        copy(a, 0, sibling, me).wait_recv()
        for j, chip in enumerate(chips):
            copy(a, 4 + j, (*chip, 1 - c), me).wait_recv()
    for cp in first + passed:
        cp.wait_send()


def _exchange_sems(n_bufs):
    return [pltpu.SemaphoreType.DMA((n_bufs * (N_DEV - 1),)), pltpu.SemaphoreType.DMA((n_bufs * (N_DEV - 1),))]


def all_reduce(part, name):
    def body(x_ref, sum_ref, buf, send_sems, recv_sems):
        x, y, c = _position()
        buf[4 * x + 2 * y + c] = x_ref[...]
        _exchange_blocks([buf], send_sems, recv_sems)
        acc = buf[0]
        for d in range(1, N_DEV):
            acc = acc + buf[d]
        sum_ref[...] = acc

    return pl.pallas_call(
        body, name=name, in_specs=[VMEM_SPEC], out_specs=VMEM_SPEC, out_shape=jax.ShapeDtypeStruct(part.shape, F32),
        scratch_shapes=[pltpu.VMEM((N_DEV,) + part.shape, F32)] + _exchange_sems(1),
        compiler_params=pltpu.CompilerParams(vmem_limit_bytes=VMEM_LIMIT))(part)


def gather_weights(w_in_t, w_out, small, pad_rows):
    rows, cols = w_in_t.shape

    def body(wi_ref, wo_ref, sm_ref, wi_out, wo_out, sm_out, wi_buf, send_sems, recv_sems):
        x, y, c = _position()
        me = 4 * x + 2 * y + c
        wi_buf[me] = wi_ref[...].astype(MXU_DTYPE)
        wo_out[me] = wo_ref[...].astype(MXU_DTYPE)
        sm_out[me] = sm_ref[...]
        _exchange_blocks([wi_buf, wo_out, sm_out], send_sems, recv_sems)
        for d in range(N_DEV):
            wi_out[pl.ds(d * rows, rows), :] = wi_buf[d]
        wi_out[pl.ds(N_DEV * rows, pad_rows), :] = jnp.zeros((pad_rows, cols), MXU_DTYPE)

    return pl.pallas_call(
        body, name="gather_weights", in_specs=[VMEM_SPEC] * 3, out_specs=[VMEM_SPEC] * 3,
        out_shape=[jax.ShapeDtypeStruct((N_DEV * rows + pad_rows, cols), MXU_DTYPE),
                   jax.ShapeDtypeStruct((N_DEV,) + w_out.shape, MXU_DTYPE), jax.ShapeDtypeStruct((N_DEV,) + small.shape, F32)],
        scratch_shapes=[pltpu.VMEM((N_DEV, rows, cols), MXU_DTYPE)] + _exchange_sems(3),
        compiler_params=pltpu.CompilerParams(vmem_limit_bytes=VMEM_LIMIT))(w_in_t, w_out, small)


def reduce_scatter_rows(tensors, name):
    n_t = len(tensors)
    arrays = [a for parts, _ in tensors for a, _ in parts]
    first_array = [sum(len(parts) for parts, _ in tensors[:t]) for t in range(n_t)]

    def pieces(t, j):
        parts, block_rows = tensors[t]
        out, base = [], 0
        for pi, (_, valid) in enumerate(parts):
            lo, hi = max(j * block_rows, base), min((j + 1) * block_rows, base + valid)
            if lo < hi:
                out.append((first_array[t] + pi, lo - base, lo - j * block_rows, hi - lo))
            base += valid
        return out

    def body(*refs):
        in_refs, out_refs = refs[:len(arrays)], refs[len(arrays):len(arrays) + n_t]
        bufs = refs[len(arrays) + n_t:len(arrays) + 5 * n_t]
        s1_sems, r1_sems, s2_sems, r2_sems = refs[len(arrays) + 5 * n_t:]
        x, y, c = _position()
        chip = 2 * x + y

        def put(t, dst, j, add=None):
            for ai, src_row, dst_row, size in pieces(t, j):
                v = in_refs[ai][pl.ds(src_row, size), :]
                if add is not None:
                    v = v + add[pl.ds(dst_row, size), :].astype(F32)
                dst[pl.ds(dst_row, size), :] = v.astype(dst.dtype)

        def swap(t, k):
            send1, recv1 = bufs[4 * t], bufs[4 * t + 1]
            return pltpu.make_async_remote_copy(src_ref=send1.at[k], dst_ref=recv1.at[k], send_sem=s1_sems.at[4 * t + k],
                                                recv_sem=r1_sems.at[4 * t + k], device_id=(x, y, 1 - c), device_id_type=MESH)

        def to_chip(t, k, slot):
            send2, recv2 = bufs[4 * t + 2], bufs[4 * t + 3]
            return pltpu.make_async_remote_copy(src_ref=send2.at[k], dst_ref=recv2.at[slot], send_sem=s2_sems.at[4 * t + k],
                                                recv_sem=r2_sems.at[4 * t + slot], device_id=(k >> 1, k & 1, c),
                                                device_id_type=MESH)

        for t in range(n_t):
            for j in range(N_DEV):
                @pl.when((j & 1) != c)
                def _():
                    put(t, bufs[4 * t].at[j >> 1], j)
            for k in range(4):
                swap(t, k).start()

        for t in range(n_t):
            recv1 = bufs[4 * t + 1]
            for k in range(4):
                swap(t, k).wait_recv()
                for j in (2 * k, 2 * k + 1):
                    @pl.when(((j & 1) == c) & (k != chip))
                    def _():
                        put(t, bufs[4 * t + 2].at[k], j, add=recv1.at[k])
                        to_chip(t, k, chip).start()

                    @pl.when(((j & 1) == c) & (k == chip))
                    def _():
                        put(t, out_refs[t], j, add=recv1.at[k])

        for t in range(n_t):
            for k in range(4):
                @pl.when(k != chip)
                def _():
                    to_chip(t, k, k).wait_recv()
                    out_refs[t][...] += bufs[4 * t + 3][k].astype(F32)

        for t in range(n_t):
            for k in range(4):
                @pl.when(k != chip)
                def _():
                    to_chip(t, k, chip).wait_send()
                swap(t, k).wait_send()

    scratch, out_shape = [], []
    for parts, block_rows in tensors:
        cols = parts[0][0].shape[1]
        scratch += [pltpu.VMEM((4, block_rows, cols), MXU_DTYPE)] * 4
        out_shape.append(jax.ShapeDtypeStruct((block_rows, cols), F32))
    scratch += [pltpu.SemaphoreType.DMA((4 * n_t,))] * 4
    return pl.pallas_call(
        body, name=name, in_specs=[VMEM_SPEC] * len(arrays), out_specs=[VMEM_SPEC] * n_t, out_shape=out_shape,
        scratch_shapes=scratch, compiler_params=pltpu.CompilerParams(vmem_limit_bytes=VMEM_LIMIT),
    )(*arrays)


def adamw(w, g, m, v, name):
    rows, cols = w.shape
    tr = 256 if rows % 256 == 0 else rows

    def body(w_ref, g_ref, m_ref, v_ref, d_ref, nm_ref, nv_ref):
        gv = g_ref[...]
        mn = ADAM_B1 * m_ref[...] + (1.0 - ADAM_B1) * gv
        vn = ADAM_B2 * v_ref[...] + (1.0 - ADAM_B2) * jnp.square(gv)
        m_hat = mn / (1.0 - ADAM_B1 ** ADAM_STEP)
        v_hat = vn / (1.0 - ADAM_B2 ** ADAM_STEP)
        d_ref[...] = -ADAM_LR * (m_hat / (jnp.sqrt(v_hat) + ADAM_EPS) + ADAM_WD * w_ref[...])
        nm_ref[...] = mn
        nv_ref[...] = vn

    spec = pl.BlockSpec((tr, cols), lambda i: (i, 0))
    shape = jax.ShapeDtypeStruct((rows, cols), F32)
    return pl.pallas_call(body, grid=(rows // tr,), name=name, in_specs=[spec] * 4, out_specs=[spec] * 3,
                          out_shape=[shape] * 3, compiler_params=_cparams("arbitrary"))(w, g, m, v)


def _pad_rows(a, rows=8):
    return jnp.pad(a, ((0, rows - a.shape[0]), (0, 0)))


def _pad_lanes(a, lanes=128):
    return jnp.pad(a, ((0, 0), (0, lanes - a.shape[1])))


def kernel(x, meta_tokens, norm_w, w_in, conv_w, hg_lb_logits, hg_norm_w, gdn_A_log, gdn_dt_bias, gdn_norm_w, w_out, final_norm_w, loss_target, m_meta_tokens, m_norm_w, m_w_in, m_conv_w, m_hg_lb_logits, m_hg_norm_w, m_gdn_A_log, m_gdn_dt_bias, m_gdn_norm_w, m_w_out, m_final_norm_w, v_meta_tokens, v_norm_w, v_w_in, v_conv_w, v_hg_lb_logits, v_hg_norm_w, v_gdn_A_log, v_gdn_dt_bias, v_gdn_norm_w, v_w_out, v_final_norm_w):
    b, seq, _ = x.shape
    n = b * seq
    dev = 4 * lax.axis_index("x") + 2 * lax.axis_index("y") + lax.axis_index("c")
    col_shard = IN_COLS // N_DEV

    small_w = jnp.concatenate([_pad_lanes(meta_tokens, 256), _pad_rows(_pad_lanes(conv_w[0], 256))], axis=0)
    w_t, w_out_g, small_g = gather_weights(w_in[0].T, w_out[0], small_w, AB_PAD - 2 * HEADS)
    meta_g = small_g[:, 0:N_META, 0:D_MODEL // N_DEV]
    conv_g = small_g[:, N_META:N_META + CONV_TAPS, 0:QKV // N_DEV]
    w_out_full = w_out_g.reshape(2 * WIDTH, D_MODEL)
    cw = jnp.transpose(conv_g, (1, 0, 2)).reshape(CONV_TAPS, QKV)
    meta = jnp.transpose(meta_g, (1, 0, 2)).reshape(N_META, D_MODEL)
    alog = _pad_lanes(gdn_A_log)
    dtb = _pad_lanes(gdn_dt_bias)
    fw = final_norm_w.reshape(1, D_MODEL)

    h0 = jnp.concatenate([jnp.zeros((CHUNK - N_META, D_MODEL), F32), meta], axis=0)
    x2 = x.reshape(n, D_MODEL)
    u0, phg0, pgd0, pab0 = in_proj(h0, norm_w, w_t, "in_proj_lead")
    u, phg, pgd, pab = in_proj(x2, norm_w, w_t, "in_proj")
    phg3, pgd3, pab3 = phg.reshape(b, seq, 4 * WIDTH), pgd.reshape(b, seq, 4 * WIDTH), pab.reshape(b, seq, AB_PAD)
    hg_loc = hg_local_fwd(phg3, hg_lb_logits)
    hg_lead = hg_local_lead(phg0, hg_lb_logits)
    y_hg, s_hg = hg_scan_fwd(phg3, phg0, hg_loc, hg_lead, hg_norm_w)
    gd_loc = gd_local_fwd(pgd3, pgd0, pab3, cw, alog, dtb)
    gd_lead = gd_local_lead(pgd0, pab0, cw, alog, dtb)
    y_gd, s_gd = gd_scan_fwd(pgd3, pgd0, gd_loc, gd_lead, gdn_norm_w)

    dh2, dy_hg, dy_gd, g_w_out, loss_part, g_fw = out_proj_loss(
        x2, loss_target.reshape(n, D_MODEL), y_hg.reshape(n, WIDTH), y_gd.reshape(n, WIDTH), w_out_full, fw)

    hb = hg_scan_bwd(phg3, phg0, hg_loc, hg_lead, hg_norm_w, s_hg, dy_hg.reshape(b, seq, WIDTH))
    dphg, g_lb = hg_local_bwd(phg3, hg_lb_logits, *hb[0:6])
    dphg0, g_lb0 = hg_local_bwd_lead(phg0, hg_lb_logits, *hb[6:12])
    g_hg_nw = hb[12]
    gb = gd_scan_bwd(pgd3, pgd0, gd_loc, gd_lead, gdn_norm_w, s_gd, dy_gd.reshape(b, seq, WIDTH))
    dpgd, dpab, dtail, g_cw, g_alog, g_dtb = gd_local_bwd(pgd3, pgd0, pab3, cw, alog, dtb, gb[0:6], gb[6])
    dpgd0, dpab0, g_cw0, g_alog0, g_dtb0 = gd_local_bwd_lead(pgd0, pab0, cw, alog, dtb, gb[7:13], gb[13], dtail.sum(0))
    g_gd_nw = gb[14]
    dphg, dpgd, dpab = dphg.reshape(n, 4 * WIDTH), dpgd.reshape(n, 4 * WIDTH), dpab.reshape(n, AB_PAD)

    grad_x, g_nw = in_proj_bwd(dphg, dpgd, dpab, w_t, x2, dh2, norm_w, "in_proj_bwd")
    dh0, g_nw0 = in_proj_bwd(dphg0, dpgd0, dpab0, w_t, h0, jnp.zeros_like(h0), norm_w, "in_proj_bwd_lead")
    g_w_hg = weight_grad(u, dphg, u0, dphg0, "w_in_grad_hg")
    g_w_gd = weight_grad(u, dpgd, u0, dpgd0, "w_in_grad_gd")
    g_w_ab = weight_grad(u, dpab, u0, dpab0, "w_in_grad_ab")

    g_w_in_t, g_w_out = reduce_scatter_rows(
        [([(g_w_hg, 4 * WIDTH), (g_w_gd, 4 * WIDTH), (g_w_ab, 2 * HEADS)], col_shard),
         ([(g_w_out, 2 * WIDTH)], (2 * WIDTH) // N_DEV)], "reduce_weights")
    small = jnp.concatenate([
        (g_nw + g_nw0).reshape(8, 128), (g_lb + g_lb0).reshape(8, 128), _pad_rows(g_hg_nw), _pad_rows(g_alog + g_alog0),
        _pad_rows(g_dtb + g_dtb0), _pad_rows(g_gd_nw), g_fw.reshape(8, 128), (g_cw + g_cw0).reshape(48, 128),
        dh0[CHUNK - N_META:CHUNK].reshape(128, 128), loss_part], axis=0)
    small = all_reduce(small, "reduce_small")
    g_norm_w = small[0:8].reshape(1, D_MODEL)
    g_lb = small[8:16].reshape(2, WIDTH)
    g_hg_nw = small[16:17]
    g_alog = small[24:25, 0:HEADS]
    g_dtb = small[32:33, 0:HEADS]
    g_gd_nw = small[40:41]
    g_fw = small[48:56].reshape(1, D_MODEL)
    g_cw_full = small[56:104].reshape(CONV_TAPS, QKV)
    g_meta_full = small[104:232].reshape(N_META, D_MODEL)
    loss = small[232, 0]
    g_conv = lax.dynamic_slice_in_dim(g_cw_full, dev * (QKV // N_DEV), QKV // N_DEV, axis=1)
    g_meta = lax.dynamic_slice_in_dim(g_meta_full, dev * (D_MODEL // N_DEV), D_MODEL // N_DEV, axis=1)

    names = ["meta_tokens", "norm_w", "w_in", "conv_w", "hg_lb_logits", "hg_norm_w", "gdn_A_log", "gdn_dt_bias",
             "gdn_norm_w", "w_out", "final_norm_w"]
    weights = [meta_tokens, norm_w, w_in, conv_w, hg_lb_logits, hg_norm_w, gdn_A_log, gdn_dt_bias, gdn_norm_w, w_out,
               final_norm_w]
    moms = [m_meta_tokens, m_norm_w, m_w_in, m_conv_w, m_hg_lb_logits, m_hg_norm_w, m_gdn_A_log, m_gdn_dt_bias,
            m_gdn_norm_w, m_w_out, m_final_norm_w]
    vars_ = [v_meta_tokens, v_norm_w, v_w_in, v_conv_w, v_hg_lb_logits, v_hg_norm_w, v_gdn_A_log, v_gdn_dt_bias,
             v_gdn_norm_w, v_w_out, v_final_norm_w]
    grads2d = [g_meta, g_norm_w, g_w_in_t, g_conv, g_lb, g_hg_nw, g_alog, g_dtb, g_gd_nw, g_w_out, g_fw]
    grads, deltas, new_ms, new_vs = [], [], [], []
    for nm, w, g2, m, v in zip(names, weights, grads2d, moms, vars_):
        if nm == "w_in":
            to2d, back = (lambda a: a[0].T), (lambda a: a.T[None])
        else:
            to2d, back = (lambda a, s=g2.shape: a.reshape(s)), (lambda a, s=w.shape: a.reshape(s))
        d, nm_, nv_ = adamw(to2d(w), g2, to2d(m), to2d(v), "adamw_" + nm)
        grads.append(back(g2))
        deltas.append(back(d))
        new_ms.append(back(nm_))
        new_vs.append(back(nv_))
    return (loss, grad_x.reshape(x.shape), *grads, *deltas, *new_ms, *new_vs)
```

```python
import jax
import jax.numpy as jnp
from jax import lax
from jax.experimental import pallas as pl
from jax.experimental.pallas import tpu as pltpu

F32 = jnp.float32
BF16 = jnp.bfloat16
MXU_DTYPE = BF16

D_MODEL = 1024
N_META = 16
CHUNK = 64
SUB = 16
HEADS = 4
DH = 128
WIDTH = HEADS * DH
QKV = 3 * WIDTH
CONV_TAPS = 4
HALO = 8
EPS = 1e-6
IN_COLS = 4 * WIDTH + 4 * WIDTH + 2 * HEADS
AB_PAD = 128
N_DEV = 8
LOCAL_CHUNKS = 2
VMEM_LIMIT = 56 * 1024 * 1024

ADAM_LR = 0.001
ADAM_B1 = 0.9
ADAM_B2 = 0.999
ADAM_EPS = 1e-08
ADAM_WD = 0.01
ADAM_STEP = 10

VMEM_SPEC = pl.BlockSpec(memory_space=pltpu.VMEM)
MESH = pl.DeviceIdType.MESH


def _mm_tn(a, b):
    return lax.dot_general(a.astype(MXU_DTYPE), b.astype(MXU_DTYPE), (((0,), (0,)), ((), ())), preferred_element_type=F32)


def _bmm(a, b):
    return lax.dot_general(a.astype(MXU_DTYPE), b.astype(MXU_DTYPE), (((2,), (1,)), ((0,), (0,))), preferred_element_type=F32)


def _bmm_nt(a, b):
    return lax.dot_general(a.astype(MXU_DTYPE), b.astype(MXU_DTYPE), (((2,), (2,)), ((0,), (0,))), preferred_element_type=F32)


def _bmm_tn(a, b):
    return lax.dot_general(a.astype(MXU_DTYPE), b.astype(MXU_DTYPE), (((1,), (1,)), ((0,), (0,))), preferred_element_type=F32)


def _iota2(n, m):
    return lax.broadcasted_iota(jnp.int32, (n, m), 0), lax.broadcasted_iota(jnp.int32, (n, m), 1)


def _silu(x):
    return x * jax.nn.sigmoid(x)


def _gated_norm(o, z, nw):
    return o * lax.rsqrt(jnp.mean(o * o, axis=-1, keepdims=True) + EPS) * nw * _silu(z)


def _heads(a, nb):
    return jnp.stack([a[c * CHUNK:(c + 1) * CHUNK, h * DH:(h + 1) * DH] for c in range(nb) for h in range(HEADS)], axis=0)


def _unheads(a3, nb):
    return jnp.concatenate(
        [jnp.concatenate([a3[c * HEADS + h] for h in range(HEADS)], axis=1) for c in range(nb)], axis=0)


def _split3(x):
    hi = x.astype(BF16)
    r1 = x - hi.astype(F32)
    mid = r1.astype(BF16)
    return hi, mid, (r1 - mid.astype(F32)).astype(BF16)


def _select_mm(pattern, n_out, n_in, transposed, x):
    rows, inner = (n_in, n_out) if transposed else (n_out, n_in)
    r, c = _iota2(rows, 3 * inner)
    c = c - jnp.where(c >= inner, inner, 0) - jnp.where(c >= 2 * inner, inner, 0)
    s = jnp.where(pattern(c, r) if transposed else pattern(r, c), 1.0, 0.0).astype(BF16)
    return jnp.dot(s, jnp.concatenate(_split3(x), axis=0), preferred_element_type=F32)


def _select_rows(pattern, n_out, x):
    @jax.custom_vjp
    def apply(v):
        return _select_mm(pattern, n_out, CHUNK, False, v)

    apply.defvjp(lambda v: (_select_mm(pattern, n_out, CHUNK, False, v), None),
                 lambda _, d: (_select_mm(pattern, n_out, CHUNK, True, d),))
    return apply(x)


def _cumsum_chunks(x, nb):
    return jnp.concatenate([_select_rows(lambda i, j: j <= i, CHUNK, x[c * CHUNK:(c + 1) * CHUNK]) for c in range(nb)], axis=0)


HG_LEVELS = 6


def _hg_sums(i, j):
    lvl, t = i >> HG_LEVELS, i & (CHUNK - 1)
    last = t
    for l in range(1, HG_LEVELS + 1):
        width = HG_LEVELS + 1 - l
        last = jnp.where(lvl == l, ((t >> width) << width) + (CHUNK >> l) - 1, last)
    return j <= last


def hg_local(p, logits):
    nb = p.shape[0] // CHUNK
    l0, l1 = logits[0:1], logits[1:2]
    mx = jnp.maximum(l0, l1)
    e0, e1 = jnp.exp(l0 - mx), jnp.exp(l1 - mx)
    lb = e0 / (e0 + e1)
    q = _silu(p[:, 0:WIDTH])
    f = lb + (1.0 - lb) * jax.nn.sigmoid(p[:, WIDTH:2 * WIDTH])
    k = 1.0 - f
    logf = jnp.log(f)
    sums = [_select_rows(_hg_sums, (HG_LEVELS + 1) * CHUNK, logf[c * CHUNK:(c + 1) * CHUNK]) for c in range(nb)]
    level = lambda l: _heads(jnp.concatenate([s[l * CHUNK:(l + 1) * CHUNK] for s in sums], axis=0), nb)
    q3, k3, v3, g3 = _heads(q, nb), _heads(k, nb), _heads(p[:, 2 * WIDTH:3 * WIDTH], nb), level(0)
    r, c = _iota2(CHUNK, CHUNK)
    row = lax.broadcasted_iota(jnp.int32, (CHUNK, DH), 0)
    a = jnp.where(r == c, _bmm_nt(q3, k3), 0.0)
    for l in range(1, HG_LEVELS + 1):
        sh = HG_LEVELS - l
        qk = jnp.where(((row >> sh) & 1) == 1, q3, k3) * jnp.exp(-jnp.abs(g3 - level(l)))
        pair = ((r >> (sh + 1)) == (c >> (sh + 1))) & (((r >> sh) & 1) == 1) & (((c >> sh) & 1) == 0)
        a = a + jnp.where(pair, _bmm_nt(qk, qk), 0.0)
    o = _bmm(a, v3)
    glast = g3[:, CHUNK - 1:CHUNK, :]
    egs = tuple(jnp.concatenate([jnp.exp(glast[c * HEADS + h]) for h in range(HEADS)], axis=1) for c in range(nb))
    return _unheads(q3 * jnp.exp(g3), nb), _unheads(k3 * jnp.exp(glast - g3), nb), _unheads(o, nb), egs


def hg_scan(q_in, k_out, v, eg, o_intra, z, nw, st):
    o = o_intra + _bmm_nt(q_in, st)
    return _gated_norm(o, z, nw), st * eg + _bmm_tn(v, k_out)


def _tri_y_impl(a):
    r, c = _iota2(CHUNK, CHUNK)
    same16 = (r // SUB) == (c // SUB)
    same32 = (r // (2 * SUB)) == (c // (2 * SUB))
    a0 = jnp.where(same16, a, 0.0)
    y = -a0
    pw = _bmm(a0, a0)
    for _ in range(2):
        y = y + pw + _bmm(y, pw)
        pw = _bmm(pw, pw)
    y = y + pw + _bmm(y, pw)
    for ak in (jnp.where(same32 & jnp.logical_not(same16), a, 0.0), jnp.where(same32, 0.0, a)):
        m = ak + _bmm(y, ak)
        y = y - (m + _bmm(m, y))
    return y


@jax.custom_vjp
def _tri_y(a):
    return _tri_y_impl(a)


def _tri_y_fwd(a):
    y = _tri_y_impl(a)
    return y, y


def _tri_y_bwd(y, dy):
    n = dy + _bmm_tn(y, dy)
    return (-(n + _bmm_nt(n, y)),)


_tri_y.defvjp(_tri_y_fwd, _tri_y_bwd)


def gd_local(xx, ab, cw, alog, dtb, inverse=_tri_y):
    n = ab.shape[0]
    nb = n // CHUNK
    conv = cw[0:1] * xx[HALO - 3:HALO - 3 + n]
    for j in range(1, CONV_TAPS):
        conv = conv + cw[j:j + 1] * xx[HALO - 3 + j:HALO - 3 + j + n]
    act = _silu(conv)
    x = ab + dtb
    g_all = -jnp.exp(alog) * (jnp.maximum(x, 0.0) + jnp.log1p(jnp.exp(-jnp.abs(x))))
    beta_all = jax.nn.sigmoid(ab)
    gam_all = _cumsum_chunks(g_all, nb)
    q3, k3, v3 = _heads(act[:, 0:WIDTH], nb), _heads(act[:, WIDTH:2 * WIDTH], nb), _heads(act[:, 2 * WIDTH:QKV], nb)
    q3 = q3 * lax.rsqrt(jnp.sum(q3 * q3, axis=-1, keepdims=True) + EPS) * (DH ** -0.5)
    k3 = k3 * lax.rsqrt(jnp.sum(k3 * k3, axis=-1, keepdims=True) + EPS)
    pairs = [(c, h) for c in range(nb) for h in range(HEADS)]
    beta = jnp.stack([beta_all[c * CHUNK:(c + 1) * CHUNK, HEADS + h:HEADS + h + 1] for c, h in pairs], axis=0)
    gam = jnp.stack([gam_all[c * CHUNK:(c + 1) * CHUNK, h:h + 1] for c, h in pairs], axis=0)
    gam_t = [gam_all[c * CHUNK:(c + 1) * CHUNK].T for c in range(nb)]
    gam_row = jnp.stack([gam_t[c][h:h + 1, :] for c, h in pairs], axis=0)
    glast = gam[:, CHUNK - 1:CHUNK, :]
    r, c = _iota2(CHUNK, CHUNK)
    dec = jnp.exp(jnp.where(c < r, gam - gam_row, -jnp.inf))
    y = inverse(beta * _bmm_nt(k3, k3) * dec)
    eg = jnp.exp(gam)
    rhs = jnp.concatenate([beta * v3, (beta * eg) * k3], axis=2)
    sol = rhs + _bmm(y, rhs)
    qk = _bmm_nt(q3, k3) * jnp.where(r == c, 1.0, dec)
    eas = tuple(jnp.exp(gam_all[(c + 1) * CHUNK - 1:(c + 1) * CHUNK]) for c in range(nb))
    return (_unheads(sol[:, :, 0:DH], nb), _unheads(sol[:, :, DH:2 * DH], nb), _unheads(q3 * eg, nb),
            _unheads(k3 * jnp.exp(glast - gam), nb), jnp.concatenate([qk[g] for g in range(nb * HEADS)], axis=0), eas)


def gd_scan(uu, ww, qe, ke, qk, ea, z, nw, s):
    u = uu - _bmm(ww, s)
    o = _bmm(qe, s) + _bmm(qk, u)
    return _gated_norm(o, z, nw), ea * s + _bmm_tn(ke, u)


def _cparams(*sem):
    return pltpu.CompilerParams(dimension_semantics=sem, vmem_limit_bytes=VMEM_LIMIT)


def _row_tile(n):
    for t in (256, 128, 64):
        if n % t == 0:
            return t
    raise ValueError(f"unsupported token count {n}")


def _w_in_specs():
    return [pl.BlockSpec((4 * WIDTH, D_MODEL), lambda *i: (0, 0)), pl.BlockSpec((4 * WIDTH, D_MODEL), lambda *i: (1, 0)),
            pl.BlockSpec((AB_PAD, D_MODEL), lambda *i: (8 * WIDTH // AB_PAD, 0))]


def in_proj(h, norm_w, w_t, name):
    n = h.shape[0]
    tm = _row_tile(n)
    nt = (((1,), (1,)), ((), ()))

    def body(h_ref, nw_ref, whg_ref, wgd_ref, wab_ref, u_ref, phg_ref, pgd_ref, pab_ref):
        x = h_ref[...]
        u = (x * lax.rsqrt(jnp.mean(x * x, axis=-1, keepdims=True) + EPS) * nw_ref[...]).astype(MXU_DTYPE)
        u_ref[...] = u
        phg_ref[...] = lax.dot_general(u, whg_ref[...], nt, preferred_element_type=F32)
        pgd_ref[...] = lax.dot_general(u, wgd_ref[...], nt, preferred_element_type=F32)
        pab_ref[...] = lax.dot_general(u, wab_ref[...], nt, preferred_element_type=F32)

    row = lambda w: pl.BlockSpec((tm, w), lambda i: (i, 0))
    return pl.pallas_call(
        body, grid=(n // tm,), name=name,
        in_specs=[row(D_MODEL), pl.BlockSpec(norm_w.shape, lambda i: (0, 0))] + _w_in_specs(),
        out_specs=[row(D_MODEL), row(4 * WIDTH), row(4 * WIDTH), row(AB_PAD)],
        out_shape=[jax.ShapeDtypeStruct((n, D_MODEL), MXU_DTYPE), jax.ShapeDtypeStruct((n, 4 * WIDTH), F32),
                   jax.ShapeDtypeStruct((n, 4 * WIDTH), F32), jax.ShapeDtypeStruct((n, AB_PAD), F32)],
        compiler_params=_cparams("arbitrary"),
    )(h, norm_w, w_t, w_t, w_t)


def out_proj_loss(x, tgt, y_hg, y_gd, w_out, fw):
    n = x.shape[0]
    tm = _row_tile(n)
    inv_d = 1.0 / D_MODEL

    def body(x_ref, t_ref, yh_ref, yg_ref, w_ref, fw_ref, dh_ref, dyh_ref, dyg_ref, dw_ref, loss_ref, dfw_ref):
        @pl.when(pl.program_id(0) == 0)
        def _():
            dw_ref[...] = jnp.zeros_like(dw_ref)
            loss_ref[...] = jnp.zeros_like(loss_ref)
            dfw_ref[...] = jnp.zeros_like(dfw_ref)

        yh, yg = yh_ref[...], yg_ref[...]
        wa, wb = w_ref[0:WIDTH, :], w_ref[WIDTH:2 * WIDTH, :]
        h2 = x_ref[...] + jnp.dot(yh, wa, preferred_element_type=F32) + jnp.dot(yg, wb, preferred_element_type=F32)
        r2 = lax.rsqrt(jnp.mean(h2 * h2, axis=-1, keepdims=True) + EPS)
        nrm = h2 * r2
        fwv = fw_ref[...]
        err = nrm * fwv - t_ref[...]
        loss_ref[...] += jnp.full(loss_ref.shape, 0.5 * inv_d * jnp.sum(err * err), F32)
        dout = err * inv_d
        dfw_ref[...] += jnp.sum(dout * nrm, axis=0, keepdims=True)
        dn = dout * fwv
        dh2 = r2 * (dn - nrm * jnp.mean(dn * nrm, axis=-1, keepdims=True))
        dh_ref[...] = dh2
        dhb = dh2.astype(MXU_DTYPE)
        dyh_ref[...] = lax.dot_general(dhb, wa, (((1,), (1,)), ((), ())), preferred_element_type=F32)
        dyg_ref[...] = lax.dot_general(dhb, wb, (((1,), (1,)), ((), ())), preferred_element_type=F32)
        dw_ref[0:WIDTH, :] += lax.dot_general(yh, dhb, (((0,), (0,)), ((), ())), preferred_element_type=F32)
        dw_ref[WIDTH:2 * WIDTH, :] += lax.dot_general(yg, dhb, (((0,), (0,)), ((), ())), preferred_element_type=F32)

    row = lambda w: pl.BlockSpec((tm, w), lambda i: (i, 0))
    full = lambda s: pl.BlockSpec(s, lambda i: (0, 0))
    return pl.pallas_call(
        body, grid=(n // tm,), name="out_proj_loss",
        in_specs=[row(D_MODEL), row(D_MODEL), row(WIDTH), row(WIDTH), full(w_out.shape), full(fw.shape)],
        out_specs=[row(D_MODEL), row(WIDTH), row(WIDTH), full((2 * WIDTH, D_MODEL)), full((8, 128)), full((1, D_MODEL))],
        out_shape=[jax.ShapeDtypeStruct((n, D_MODEL), F32), jax.ShapeDtypeStruct((n, WIDTH), F32),
                   jax.ShapeDtypeStruct((n, WIDTH), F32), jax.ShapeDtypeStruct((2 * WIDTH, D_MODEL), F32),
                   jax.ShapeDtypeStruct((8, 128), F32), jax.ShapeDtypeStruct((1, D_MODEL), F32)],
        compiler_params=_cparams("arbitrary"),
    )(x, tgt, y_hg, y_gd, w_out, fw)


def in_proj_bwd(dphg, dpgd, dpab, w_t, h, dh2, norm_w, name):
    n = h.shape[0]
    tm = _row_tile(n)

    def body(dphg_ref, dpgd_ref, dpab_ref, whg_ref, wgd_ref, wab_ref, h_ref, dh2_ref, nw_ref, dx_ref, dnw_ref):
        @pl.when(pl.program_id(0) == 0)
        def _():
            dnw_ref[...] = jnp.zeros_like(dnw_ref)

        du = jnp.dot(dphg_ref[...].astype(MXU_DTYPE), whg_ref[...], preferred_element_type=F32)
        du += jnp.dot(dpgd_ref[...].astype(MXU_DTYPE), wgd_ref[...], preferred_element_type=F32)
        du += jnp.dot(dpab_ref[...].astype(MXU_DTYPE), wab_ref[...], preferred_element_type=F32)
        x = h_ref[...]
        r = lax.rsqrt(jnp.mean(x * x, axis=-1, keepdims=True) + EPS)
        nrm = x * r
        dnw_ref[...] += jnp.sum(du * nrm, axis=0, keepdims=True)
        dn = du * nw_ref[...]
        dx_ref[...] = dh2_ref[...] + r * (dn - nrm * jnp.mean(dn * nrm, axis=-1, keepdims=True))

    row = lambda w: pl.BlockSpec((tm, w), lambda i: (i, 0))
    return pl.pallas_call(
        body, grid=(n // tm,), name=name,
        in_specs=[row(4 * WIDTH), row(4 * WIDTH), row(AB_PAD)] + _w_in_specs() + [row(D_MODEL), row(D_MODEL),
                                                                                   pl.BlockSpec(norm_w.shape, lambda i: (0, 0))],
        out_specs=[row(D_MODEL), pl.BlockSpec((1, D_MODEL), lambda i: (0, 0))],
        out_shape=[jax.ShapeDtypeStruct((n, D_MODEL), F32), jax.ShapeDtypeStruct((1, D_MODEL), F32)],
        compiler_params=_cparams("arbitrary"),
    )(dphg, dpgd, dpab, w_t, w_t, w_t, h, dh2, norm_w)


def weight_grad(u, dp, u0, dp0, name):
    n, w = dp.shape
    tn = min(w, 1024)
    tm = 1024 if n % 1024 == 0 else _row_tile(n)
    n0 = u0.shape[0]

    def body(u_ref, dp_ref, u0_ref, dp0_ref, o_ref):
        @pl.when(pl.program_id(1) == 0)
        def _():
            o_ref[...] = _mm_tn(dp0_ref[...], u0_ref[...])

        o_ref[...] += _mm_tn(dp_ref[...], u_ref[...])

    return pl.pallas_call(
        body, grid=(w // tn, n // tm), name=name,
        in_specs=[pl.BlockSpec((tm, D_MODEL), lambda j, t: (t, 0)), pl.BlockSpec((tm, tn), lambda j, t: (t, j)),
                  pl.BlockSpec((n0, D_MODEL), lambda j, t: (0, 0)), pl.BlockSpec((n0, tn), lambda j, t: (0, j))],
        out_specs=pl.BlockSpec((tn, D_MODEL), lambda j, t: (j, 0)),
        out_shape=jax.ShapeDtypeStruct((w, D_MODEL), F32),
        compiler_params=_cparams("arbitrary", "arbitrary"),
    )(u, dp, u0, dp0)


def _real(c):
    return jnp.maximum(c - 1, 0)


def _sds(shape, dtype=F32):
    return jax.ShapeDtypeStruct(shape, dtype)


def _load_slabs(ref, b):
    return jnp.stack([ref[i, :, h * DH:(h + 1) * DH].astype(F32) for i in range(b) for h in range(HEADS)], axis=0)


def _lead_slabs(a, b):
    return jnp.stack([a[:, h * DH:(h + 1) * DH].astype(F32) for _ in range(b) for h in range(HEADS)], axis=0)


def _rows(a3, i):
    return jnp.concatenate([a3[i * HEADS + h] for h in range(HEADS)], axis=1)


def _store_slabs(ref, a3, b):
    for i in range(b):
        ref[i] = _rows(a3, i).astype(ref.dtype)


def _sum_rows(a3, b):
    out = _rows(a3, 0)
    for i in range(1, b):
        out = out + _rows(a3, i)
    return out


def _save_states(ref, s, b):
    for i in range(b):
        ref[i] = jnp.concatenate([s[i * HEADS + h] for h in range(HEADS)], axis=0)


def _load_states(ref, b):
    return jnp.stack([ref[i, h * DH:(h + 1) * DH, :] for i in range(b) for h in range(HEADS)], axis=0)


def hg_local_fwd(p, logits):
    b, seq, _ = p.shape
    rows = LOCAL_CHUNKS * CHUNK
    nreal = seq // CHUNK

    def body(p_ref, lg_ref, q_ref, k_ref, o_ref, eg_ref):
        q_in, k_out, o_intra, egs = hg_local(p_ref[...], lg_ref[...])
        q_ref[...], k_ref[...], o_ref[...] = q_in.astype(MXU_DTYPE), k_out.astype(MXU_DTYPE), o_intra
        for c in range(LOCAL_CHUNKS):
            eg_ref[c] = egs[c]

    slab = pl.BlockSpec((None, rows, WIDTH), lambda s, g: (s, g, 0))
    return pl.pallas_call(
        body, grid=(b, seq // rows), name="hgrn2_local",
        in_specs=[pl.BlockSpec((None, rows, 4 * WIDTH), lambda s, g: (s, g, 0)), pl.BlockSpec(logits.shape, lambda s, g: (0, 0))],
        out_specs=[slab, slab, slab, pl.BlockSpec((None, LOCAL_CHUNKS, 1, WIDTH), lambda s, g: (s, g, 0, 0))],
        out_shape=[_sds((b, seq, WIDTH), MXU_DTYPE)] * 2 + [_sds((b, seq, WIDTH)), _sds((b, nreal, 1, WIDTH))],
        compiler_params=_cparams("arbitrary", "arbitrary"),
    )(p, logits)


def hg_local_lead(p0, logits):
    def body(p_ref, lg_ref, q_ref, k_ref, o_ref, eg_ref):
        q_in, k_out, o_ref[...], (eg_ref[...],) = hg_local(p_ref[...], lg_ref[...])
        q_ref[...], k_ref[...] = q_in.astype(MXU_DTYPE), k_out.astype(MXU_DTYPE)

    return pl.pallas_call(
        body, name="hgrn2_local_lead", in_specs=[VMEM_SPEC] * 2, out_specs=[VMEM_SPEC] * 4,
        out_shape=[_sds((CHUNK, WIDTH), MXU_DTYPE)] * 2 + [_sds((CHUNK, WIDTH)), _sds((1, WIDTH))],
        compiler_params=pltpu.CompilerParams(vmem_limit_bytes=VMEM_LIMIT),
    )(p0, logits)


def _hg_scan_inputs(c, b, q_ref, k_ref, o_ref, v_ref, z_ref, eg_ref, q0_ref, k0_ref, o0_ref, p0_ref, eg0_ref):
    lead = c == 0
    pick = lambda real, lead_val: jnp.where(lead, _lead_slabs(lead_val, b), _load_slabs(real, b))
    eg = jnp.where(lead, jnp.stack([eg0_ref[:, h * DH:(h + 1) * DH] for _ in range(b) for h in range(HEADS)], axis=0),
                   jnp.stack([eg_ref[i, :, h * DH:(h + 1) * DH] for i in range(b) for h in range(HEADS)], axis=0))
    return (pick(q_ref, q0_ref[...]), pick(k_ref, k0_ref[...]), pick(v_ref, p0_ref[:, 2 * WIDTH:3 * WIDTH]), eg,
            pick(o_ref, o0_ref[...]), pick(z_ref, p0_ref[:, 3 * WIDTH:4 * WIDTH]))


def _scan_specs(b, nc, reverse):
    chunk = (lambda i: nc - 1 - i) if reverse else (lambda i: i)
    slab = lambda lane_block: pl.BlockSpec((b, CHUNK, WIDTH), lambda i: (0, _real(chunk(i)), lane_block))
    per_chunk = lambda *tail: pl.BlockSpec((b, None) + tail, lambda i: (0, _real(chunk(i))) + (0,) * len(tail))
    state = pl.BlockSpec((b, None, WIDTH, DH), lambda i: (0, chunk(i), 0, 0))
    const = lambda a: pl.BlockSpec(a.shape, lambda i: (0,) * a.ndim)
    return slab, per_chunk, state, const


def run_scans(parts, nc, name):
    n_in = [len(p["args"]) for p in parts]
    n_out = [len(p["out_shape"]) for p in parts]
    n_scr = [len(p["scratch_shapes"]) for p in parts]

    def body(*refs):
        ins, outs, scr = refs[:sum(n_in)], refs[sum(n_in):sum(n_in) + sum(n_out)], refs[sum(n_in) + sum(n_out):]
        for i, part in enumerate(parts):
            part["body"](*ins[sum(n_in[:i]):sum(n_in[:i + 1])], *outs[sum(n_out[:i]):sum(n_out[:i + 1])],
                         *scr[sum(n_scr[:i]):sum(n_scr[:i + 1])])

    flat = lambda key: [v for p in parts for v in p[key]]
    out = pl.pallas_call(body, grid=(nc,), name=name, in_specs=flat("in_specs"), out_specs=flat("out_specs"),
                         out_shape=flat("out_shape"), scratch_shapes=flat("scratch_shapes"),
                         compiler_params=_cparams("arbitrary"))(*flat("args"))
    return [out[sum(n_out[:i]):sum(n_out[:i + 1])] for i in range(len(parts))]


def hg_scan_fwd(p, p0, local, lead, nw):
    b, seq, _ = p.shape
    nc = seq // CHUNK + 1
    q_in, k_out, o_intra, eg = local
    slab, per_chunk, state, const = _scan_specs(b, nc, False)

    def body(q_ref, k_ref, o_ref, v_ref, z_ref, eg_ref, q0_ref, k0_ref, o0_ref, p0_ref, eg0_ref, nw_ref, y_ref, ss_ref, st):
        c = pl.program_id(0)

        @pl.when(c == 0)
        def _():
            st[...] = jnp.zeros_like(st)

        s_in = st[...]
        _save_states(ss_ref, s_in, b)
        args = _hg_scan_inputs(c, b, q_ref, k_ref, o_ref, v_ref, z_ref, eg_ref, q0_ref, k0_ref, o0_ref, p0_ref, eg0_ref)
        y, s_new = hg_scan(*args, nw_ref[...], s_in)
        _store_slabs(y_ref, y, b)
        st[...] = s_new

    return dict(
        body=body, args=(q_in, k_out, o_intra, p, p, eg, lead[0], lead[1], lead[2], p0, lead[3], nw),
        in_specs=[slab(0), slab(0), slab(0), slab(2), slab(3), per_chunk(1, WIDTH)] + [const(a) for a in lead[0:3]]
        + [const(p0), const(lead[3]), const(nw)],
        out_specs=[slab(0), state],
        out_shape=[_sds((b, seq, WIDTH), MXU_DTYPE), _sds((b, nc, WIDTH, DH))],
        scratch_shapes=[pltpu.VMEM((b * HEADS, DH, DH), F32)])


def hg_scan_bwd(p, p0, local, lead, nw, ssave, dy):
    b, seq, _ = p.shape
    nc = seq // CHUNK + 1
    q_in, k_out, o_intra, eg = local
    slab, per_chunk, state, const = _scan_specs(b, nc, True)

    def body(q_ref, k_ref, o_ref, v_ref, z_ref, eg_ref, q0_ref, k0_ref, o0_ref, p0_ref, eg0_ref, nw_ref, ss_ref, dy_ref,
             dq_ref, dk_ref, do_ref, dv_ref, dz_ref, deg_ref, dq0_ref, dk0_ref, do0_ref, dv0_ref, dz0_ref, deg0_ref, dnw_ref,
             dst):
        i = pl.program_id(0)
        c = nc - 1 - i

        @pl.when(i == 0)
        def _():
            dst[...] = jnp.zeros_like(dst)
            dnw_ref[...] = jnp.zeros_like(dnw_ref)

        args = _hg_scan_inputs(c, b, q_ref, k_ref, o_ref, v_ref, z_ref, eg_ref, q0_ref, k0_ref, o0_ref, p0_ref, eg0_ref)
        s_in = _load_states(ss_ref, b)
        _, vjp = jax.vjp(hg_scan, *args, nw_ref[...], s_in)
        dyv = jnp.where(c == 0, 0.0, _load_slabs(dy_ref, b))
        dq, dk, dv, deg, do, dz, dnw, ds = vjp((dyv, dst[...]))
        dst[...] = ds
        dnw_ref[...] += dnw

        @pl.when(c > 0)
        def _():
            for ref, val in ((dq_ref, dq), (dk_ref, dk), (do_ref, do), (dv_ref, dv), (dz_ref, dz)):
                _store_slabs(ref, val, b)
            for j in range(b):
                deg_ref[j] = _rows(deg, j)

        @pl.when(c == 0)
        def _():
            for ref, val in ((dq0_ref, dq), (dk0_ref, dk), (do0_ref, do), (dv0_ref, dv), (dz0_ref, dz), (deg0_ref, deg)):
                ref[...] = _sum_rows(val, b)

    lead_out = [const(a) for a in lead[0:3]] + [const(lead[0]), const(lead[0]), const(lead[3])]
    return dict(
        body=body, args=(q_in, k_out, o_intra, p, p, eg, lead[0], lead[1], lead[2], p0, lead[3], nw, ssave, dy),
        in_specs=[slab(0), slab(0), slab(0), slab(2), slab(3), per_chunk(1, WIDTH)] + [const(a) for a in lead[0:3]]
        + [const(p0), const(lead[3]), const(nw), state, slab(0)],
        out_specs=[slab(0)] * 5 + [per_chunk(1, WIDTH)] + lead_out + [const(nw)],
        out_shape=[_sds((b, seq, WIDTH))] * 5 + [_sds(eg.shape)] + [_sds((CHUNK, WIDTH))] * 5 + [_sds((1, WIDTH)), _sds(nw.shape)],
        scratch_shapes=[pltpu.VMEM((b * HEADS, DH, DH), F32)])


def _hg_local_vjp(p, logits, dq, dk, do, degs, dv, dz):
    _, vjp = jax.vjp(hg_local, p, logits)
    dp, dlg = vjp((dq, dk, do, degs))
    return dp + jnp.concatenate([jnp.zeros((p.shape[0], 2 * WIDTH), F32), dv, dz], axis=1), dlg


def hg_local_bwd(p, logits, dq, dk, do, dv, dz, deg):
    b, seq, _ = p.shape
    rows = LOCAL_CHUNKS * CHUNK

    def body(p_ref, lg_ref, dq_ref, dk_ref, do_ref, dv_ref, dz_ref, deg_ref, dp_ref, dlg_ref):
        @pl.when((pl.program_id(0) == 0) & (pl.program_id(1) == 0))
        def _():
            dlg_ref[...] = jnp.zeros_like(dlg_ref)

        degs = tuple(deg_ref[c] for c in range(LOCAL_CHUNKS))
        dp, dlg = _hg_local_vjp(p_ref[...], lg_ref[...], dq_ref[...], dk_ref[...], do_ref[...], degs, dv_ref[...], dz_ref[...])
        dp_ref[...] = dp.astype(MXU_DTYPE)
        dlg_ref[...] += dlg

    slab = pl.BlockSpec((None, rows, WIDTH), lambda s, g: (s, g, 0))
    wide = pl.BlockSpec((None, rows, 4 * WIDTH), lambda s, g: (s, g, 0))
    lg = pl.BlockSpec(logits.shape, lambda s, g: (0, 0))
    return pl.pallas_call(
        body, grid=(b, seq // rows), name="hgrn2_local_bwd",
        in_specs=[wide, lg, slab, slab, slab, slab, slab, pl.BlockSpec((None, LOCAL_CHUNKS, 1, WIDTH), lambda s, g: (s, g, 0, 0))],
        out_specs=[wide, lg], out_shape=[_sds(p.shape, MXU_DTYPE), _sds(logits.shape)],
        compiler_params=_cparams("arbitrary", "arbitrary"),
    )(p, logits, dq, dk, do, dv, dz, deg)


def hg_local_bwd_lead(p0, logits, dq, dk, do, dv, dz, deg):
    def body(p_ref, lg_ref, dq_ref, dk_ref, do_ref, dv_ref, dz_ref, deg_ref, dp_ref, dlg_ref):
        dp, dlg_ref[...] = _hg_local_vjp(p_ref[...], lg_ref[...], dq_ref[...], dk_ref[...], do_ref[...],
                                         (deg_ref[...],), dv_ref[...], dz_ref[...])
        dp_ref[...] = dp.astype(MXU_DTYPE)

    return pl.pallas_call(
        body, name="hgrn2_local_bwd_lead", in_specs=[VMEM_SPEC] * 8, out_specs=[VMEM_SPEC] * 2,
        out_shape=[_sds(p0.shape, MXU_DTYPE), _sds(logits.shape)], compiler_params=pltpu.CompilerParams(vmem_limit_bytes=VMEM_LIMIT),
    )(p0, logits, dq, dk, do, dv, dz, deg)


def _halo_block(g):
    return jnp.maximum((LOCAL_CHUNKS * CHUNK // HALO) * g - 1, 0)


def _gd_window(g, p_ref, halo_ref, p0_ref):
    halo = jnp.where(g == 0, p0_ref[CHUNK - HALO:CHUNK, 0:QKV], halo_ref[...])
    return jnp.concatenate([halo, p_ref[:, 0:QKV]], axis=0)


def gd_local_fwd(p, p0, ab, cw, alog, dtb):
    b, seq, _ = p.shape
    rows = LOCAL_CHUNKS * CHUNK
    nreal = seq // CHUNK

    def body(p_ref, halo_ref, p0_ref, ab_ref, cw_ref, al_ref, dt_ref, u_ref, w_ref, qe_ref, ke_ref, qk_ref, ea_ref):
        uu, ww, qe, ke, qk, eas = gd_local(_gd_window(pl.program_id(1), p_ref, halo_ref, p0_ref), ab_ref[...], cw_ref[...],
                                           al_ref[...], dt_ref[...], inverse=_tri_y_impl)
        u_ref[...], w_ref[...], qe_ref[...], ke_ref[...] = uu, ww.astype(MXU_DTYPE), qe.astype(MXU_DTYPE), ke.astype(MXU_DTYPE)
        for c in range(LOCAL_CHUNKS):
            qk_ref[c] = qk[c * HEADS * CHUNK:(c + 1) * HEADS * CHUNK]
            ea_ref[c] = eas[c]

    const = lambda a: pl.BlockSpec(a.shape, lambda s, g: (0, 0))
    slab = pl.BlockSpec((None, rows, WIDTH), lambda s, g: (s, g, 0))
    return pl.pallas_call(
        body, grid=(b, seq // rows), name="gdn_local",
        in_specs=[pl.BlockSpec((None, rows, 4 * WIDTH), lambda s, g: (s, g, 0)),
                  pl.BlockSpec((None, HALO, QKV), lambda s, g: (s, _halo_block(g), 0)), const(p0),
                  pl.BlockSpec((None, rows, AB_PAD), lambda s, g: (s, g, 0)), const(cw), const(alog), const(dtb)],
        out_specs=[slab] * 4 + [pl.BlockSpec((None, LOCAL_CHUNKS, HEADS * CHUNK, CHUNK), lambda s, g: (s, g, 0, 0)),
                                pl.BlockSpec((None, LOCAL_CHUNKS, 1, AB_PAD), lambda s, g: (s, g, 0, 0))],
        out_shape=[_sds((b, seq, WIDTH))] + [_sds((b, seq, WIDTH), MXU_DTYPE)] * 3
        + [_sds((b, nreal, HEADS * CHUNK, CHUNK)), _sds((b, nreal, 1, AB_PAD))],
        compiler_params=_cparams("arbitrary", "arbitrary"),
    )(p, p, p0, ab, cw, alog, dtb)


def _lead_window(p0_ref):
    return jnp.concatenate([jnp.zeros((HALO, QKV), F32), p0_ref[:, 0:QKV]], axis=0)


def gd_local_lead(p0, ab0, cw, alog, dtb):
    def body(p0_ref, ab_ref, cw_ref, al_ref, dt_ref, u_ref, w_ref, qe_ref, ke_ref, qk_ref, ea_ref):
        u_ref[...], ww, qe, ke, qk_ref[...], (ea_ref[...],) = gd_local(
            _lead_window(p0_ref), ab_ref[...], cw_ref[...], al_ref[...], dt_ref[...], inverse=_tri_y_impl)
        w_ref[...], qe_ref[...], ke_ref[...] = ww.astype(MXU_DTYPE), qe.astype(MXU_DTYPE), ke.astype(MXU_DTYPE)

    return pl.pallas_call(
        body, name="gdn_local_lead", in_specs=[VMEM_SPEC] * 5, out_specs=[VMEM_SPEC] * 6,
        out_shape=[_sds((CHUNK, WIDTH))] + [_sds((CHUNK, WIDTH), MXU_DTYPE)] * 3 + [_sds((HEADS * CHUNK, CHUNK)), _sds((1, AB_PAD))],
        compiler_params=pltpu.CompilerParams(vmem_limit_bytes=VMEM_LIMIT),
    )(p0, ab0, cw, alog, dtb)


def _gd_scan_inputs(c, b, u_ref, w_ref, qe_ref, ke_ref, qk_ref, ea_ref, z_ref, u0_ref, w0_ref, qe0_ref, ke0_ref, qk0_ref,
                    ea0_ref, p0_ref):
    lead = c == 0
    pick = lambda real, lead_val: jnp.where(lead, _lead_slabs(lead_val, b), _load_slabs(real, b))
    pairs = [(i, h) for i in range(b) for h in range(HEADS)]
    qk = jnp.where(lead, jnp.stack([qk0_ref[h * CHUNK:(h + 1) * CHUNK, :] for _, h in pairs], axis=0),
                   jnp.stack([qk_ref[i, h * CHUNK:(h + 1) * CHUNK, :] for i, h in pairs], axis=0))
    ea = jnp.where(lead, jnp.stack([ea0_ref[:, h:h + 1] for _, h in pairs], axis=0),
                   jnp.stack([ea_ref[i, :, h:h + 1] for i, h in pairs], axis=0))
    return (pick(u_ref, u0_ref[...]), pick(w_ref, w0_ref[...]), pick(qe_ref, qe0_ref[...]), pick(ke_ref, ke0_ref[...]), qk,
            ea, pick(z_ref, p0_ref[:, QKV:QKV + WIDTH]))


def gd_scan_fwd(p, p0, local, lead, nw):
    b, seq, _ = p.shape
    nc = seq // CHUNK + 1
    slab, per_chunk, state, const = _scan_specs(b, nc, False)

    def body(u_ref, w_ref, qe_ref, ke_ref, qk_ref, ea_ref, z_ref, u0_ref, w0_ref, qe0_ref, ke0_ref, qk0_ref, ea0_ref, p0_ref,
             nw_ref, y_ref, ss_ref, st):
        c = pl.program_id(0)

        @pl.when(c == 0)
        def _():
            st[...] = jnp.zeros_like(st)

        s_in = st[...]
        _save_states(ss_ref, s_in, b)
        args = _gd_scan_inputs(c, b, u_ref, w_ref, qe_ref, ke_ref, qk_ref, ea_ref, z_ref, u0_ref, w0_ref, qe0_ref, ke0_ref,
                               qk0_ref, ea0_ref, p0_ref)
        y, s_new = gd_scan(*args, nw_ref[...], s_in)
        _store_slabs(y_ref, y, b)
        st[...] = s_new

    return dict(
        body=body, args=(*local, p, *lead, p0, nw),
        in_specs=[slab(0)] * 4 + [per_chunk(HEADS * CHUNK, CHUNK), per_chunk(1, AB_PAD), slab(3)] + [const(a) for a in lead]
        + [const(p0), const(nw)],
        out_specs=[slab(0), state],
        out_shape=[_sds((b, seq, WIDTH), MXU_DTYPE), _sds((b, nc, WIDTH, DH))],
        scratch_shapes=[pltpu.VMEM((b * HEADS, DH, DH), F32)])


def gd_scan_bwd(p, p0, local, lead, nw, ssave, dy):
    b, seq, _ = p.shape
    nc = seq // CHUNK + 1
    slab, per_chunk, state, const = _scan_specs(b, nc, True)

    def body(u_ref, w_ref, qe_ref, ke_ref, qk_ref, ea_ref, z_ref, u0_ref, w0_ref, qe0_ref, ke0_ref, qk0_ref, ea0_ref, p0_ref,
             nw_ref, ss_ref, dy_ref, du_ref, dw_ref, dqe_ref, dke_ref, dqk_ref, dea_ref, dz_ref, du0_ref, dw0_ref, dqe0_ref,
             dke0_ref, dqk0_ref, dea0_ref, dz0_ref, dnw_ref, dst):
        i = pl.program_id(0)
        c = nc - 1 - i

        @pl.when(i == 0)
        def _():
            dst[...] = jnp.zeros_like(dst)
            dnw_ref[...] = jnp.zeros_like(dnw_ref)

        args = _gd_scan_inputs(c, b, u_ref, w_ref, qe_ref, ke_ref, qk_ref, ea_ref, z_ref, u0_ref, w0_ref, qe0_ref, ke0_ref,
                               qk0_ref, ea0_ref, p0_ref)
        s_in = _load_states(ss_ref, b)
        _, vjp = jax.vjp(gd_scan, *args, nw_ref[...], s_in)
        dyv = jnp.where(c == 0, 0.0, _load_slabs(dy_ref, b))
        du, dw, dqe, dke, dqk, dea, dz, dnw, ds = vjp((dyv, dst[...]))
        dst[...] = ds
        dnw_ref[...] += dnw
        lane = lax.broadcasted_iota(jnp.int32, (1, AB_PAD), 1)
        dea_rows = [sum(jnp.where(lane == h, dea[j * HEADS + h], 0.0) for h in range(HEADS)) for j in range(b)]
        dqk_rows = [jnp.concatenate([dqk[j * HEADS + h] for h in range(HEADS)], axis=0) for j in range(b)]

        @pl.when(c > 0)
        def _():
            for ref, val in ((du_ref, du), (dw_ref, dw), (dqe_ref, dqe), (dke_ref, dke), (dz_ref, dz)):
                _store_slabs(ref, val, b)
            for j in range(b):
                dqk_ref[j] = dqk_rows[j]
                dea_ref[j] = dea_rows[j]

        @pl.when(c == 0)
        def _():
            for ref, val in ((du0_ref, du), (dw0_ref, dw), (dqe0_ref, dqe), (dke0_ref, dke), (dz0_ref, dz)):
                ref[...] = _sum_rows(val, b)
            dqk0_ref[...] = sum(dqk_rows[1:], dqk_rows[0])
            dea0_ref[...] = sum(dea_rows[1:], dea_rows[0])

    uu, ww, qe, ke, qk, ea = local
    return dict(
        body=body, args=(*local, p, *lead, p0, nw, ssave, dy),
        in_specs=[slab(0)] * 4 + [per_chunk(HEADS * CHUNK, CHUNK), per_chunk(1, AB_PAD), slab(3)] + [const(a) for a in lead]
        + [const(p0), const(nw), state, slab(0)],
        out_specs=[slab(0)] * 4 + [per_chunk(HEADS * CHUNK, CHUNK), per_chunk(1, AB_PAD), slab(0)] + [const(a) for a in lead]
        + [const(lead[0]), const(nw)],
        out_shape=[_sds((b, seq, WIDTH))] * 4 + [_sds(qk.shape), _sds(ea.shape), _sds((b, seq, WIDTH))]
        + [_sds(a.shape) for a in lead] + [_sds(lead[0].shape), _sds(nw.shape)],
        scratch_shapes=[pltpu.VMEM((b * HEADS, DH, DH), F32)])


def gd_local_bwd(p, p0, ab, cw, alog, dtb, cot, dz):
    b, seq, _ = p.shape
    rows = LOCAL_CHUNKS * CHUNK
    ng = seq // rows
    du, dw, dqe, dke, dqk, dea = cot

    def body(p_ref, halo_ref, p0_ref, ab_ref, cw_ref, al_ref, dt_ref, du_ref, dw_ref, dqe_ref, dke_ref, dqk_ref, dea_ref, dz_ref,
             dp_ref, dab_ref, dhalo0_ref, dcw_ref, dal_ref, ddt_ref, dhalo):
        i = pl.program_id(1)
        g = ng - 1 - i

        @pl.when(i == 0)
        def _():
            dhalo[...] = jnp.zeros_like(dhalo)

        @pl.when((pl.program_id(0) == 0) & (i == 0))
        def _():
            dcw_ref[...] = jnp.zeros_like(dcw_ref)
            dal_ref[...] = jnp.zeros_like(dal_ref)
            ddt_ref[...] = jnp.zeros_like(ddt_ref)

        _, vjp = jax.vjp(gd_local, _gd_window(g, p_ref, halo_ref, p0_ref), ab_ref[...], cw_ref[...], al_ref[...], dt_ref[...])
        dqk_all = jnp.concatenate([dqk_ref[c] for c in range(LOCAL_CHUNKS)], axis=0)
        deas = tuple(dea_ref[c] for c in range(LOCAL_CHUNKS))
        dxx, dab, dcw, dal, ddt = vjp((du_ref[...], dw_ref[...], dqe_ref[...], dke_ref[...], dqk_all, deas))
        dqkv = dxx[HALO:HALO + rows] + jnp.concatenate([jnp.zeros((rows - HALO, QKV), F32), dhalo[...]], axis=0)
        dhalo[...] = dxx[0:HALO]
        dhalo0_ref[...] = dxx[0:HALO]
        dp_ref[...] = jnp.concatenate([dqkv, dz_ref[...]], axis=1).astype(MXU_DTYPE)
        dab_ref[...] = dab.astype(MXU_DTYPE)
        dcw_ref[...] += dcw
        dal_ref[...] += dal
        ddt_ref[...] += ddt

    rg = lambda i: ng - 1 - i
    const = lambda a: pl.BlockSpec(a.shape, lambda s, i: (0, 0))
    slab = pl.BlockSpec((None, rows, WIDTH), lambda s, i: (s, rg(i), 0))
    wide = pl.BlockSpec((None, rows, 4 * WIDTH), lambda s, i: (s, rg(i), 0))
    gates = pl.BlockSpec((None, rows, AB_PAD), lambda s, i: (s, rg(i), 0))
    return pl.pallas_call(
        body, grid=(b, ng), name="gdn_local_bwd",
        in_specs=[wide, pl.BlockSpec((None, HALO, QKV), lambda s, i: (s, _halo_block(rg(i)), 0)), const(p0), gates, const(cw),
                  const(alog), const(dtb), slab, slab, slab, slab,
                  pl.BlockSpec((None, LOCAL_CHUNKS, HEADS * CHUNK, CHUNK), lambda s, i: (s, rg(i), 0, 0)),
                  pl.BlockSpec((None, LOCAL_CHUNKS, 1, AB_PAD), lambda s, i: (s, rg(i), 0, 0)), slab],
        out_specs=[wide, gates, pl.BlockSpec((None, HALO, QKV), lambda s, i: (s, 0, 0)), const(cw), const(alog), const(dtb)],
        out_shape=[_sds(p.shape, MXU_DTYPE), _sds(ab.shape, MXU_DTYPE), _sds((b, HALO, QKV)), _sds(cw.shape), _sds(alog.shape),
                   _sds(dtb.shape)],
        scratch_shapes=[pltpu.VMEM((HALO, QKV), F32)],
        compiler_params=_cparams("arbitrary", "arbitrary"),
    )(p, p, p0, ab, cw, alog, dtb, du, dw, dqe, dke, dqk, dea, dz)


def gd_local_bwd_lead(p0, ab0, cw, alog, dtb, cot, dz, dtail):
    def body(p0_ref, ab_ref, cw_ref, al_ref, dt_ref, du_ref, dw_ref, dqe_ref, dke_ref, dqk_ref, dea_ref, dz_ref, dtail_ref,
             dp_ref, dab_ref, dcw_ref, dal_ref, ddt_ref):
        _, vjp = jax.vjp(gd_local, _lead_window(p0_ref), ab_ref[...], cw_ref[...], al_ref[...], dt_ref[...])
        dxx, dab, dcw, dal, ddt = vjp((du_ref[...], dw_ref[...], dqe_ref[...], dke_ref[...], dqk_ref[...], (dea_ref[...],)))
        dqkv = dxx[HALO:HALO + CHUNK] + jnp.concatenate([jnp.zeros((CHUNK - HALO, QKV), F32), dtail_ref[...]], axis=0)
        dp_ref[...] = jnp.concatenate([dqkv, dz_ref[...]], axis=1).astype(MXU_DTYPE)
        dab_ref[...], dcw_ref[...], dal_ref[...], ddt_ref[...] = dab.astype(MXU_DTYPE), dcw, dal, ddt

    return pl.pallas_call(
        body, name="gdn_local_bwd_lead", in_specs=[VMEM_SPEC] * 13, out_specs=[VMEM_SPEC] * 5,
        out_shape=[_sds(p0.shape, MXU_DTYPE), _sds(ab0.shape, MXU_DTYPE), _sds(cw.shape), _sds(alog.shape), _sds(dtb.shape)],
        compiler_params=pltpu.CompilerParams(vmem_limit_bytes=VMEM_LIMIT),
    )(p0, ab0, cw, alog, dtb, *cot, dz, dtail)


def _position():
    return lax.axis_index("x"), lax.axis_index("y"), lax.axis_index("c")


def _exchange_blocks(bufs, send_sems, recv_sems):
    x, y, c = _position()
    me, sibling = (x, y, c), (x, y, 1 - c)
    chips = [(1 - x, y), (x, 1 - y), (1 - x, 1 - y)]
    per_buf = N_DEV - 1

    def copy(a, k, blk, to):
        rows = bufs[a].at[4 * blk[0] + 2 * blk[1] + blk[2]]
        return pltpu.make_async_remote_copy(src_ref=rows, dst_ref=rows, send_sem=send_sems.at[a * per_buf + k],
                                            recv_sem=recv_sems.at[a * per_buf + k], device_id=to, device_id_type=MESH)

    bufs_idx = range(len(bufs))
    first = [copy(a, 0, me, sibling) for a in bufs_idx] + [copy(a, 1 + j, me, (*chip, c)) for a in bufs_idx
                                                           for j, chip in enumerate(chips)]
    for cp in first:
        cp.start()
    passed = []
    for j, chip in enumerate(chips):
        for a in bufs_idx:
            copy(a, 1 + j, (*chip, c), me).wait_recv()
            passed.append(copy(a, 4 + j, (*chip, c), sibling))
            passed[-1].start()
    for a in bufs_idx:
        copy(a, 0, sibling, me).wait_recv()
        for j, chip in enumerate(chips):
            copy(a, 4 + j, (*chip, 1 - c), me).wait_recv()
    for cp in first + passed:
        cp.wait_send()


def _exchange_sems(n_bufs):
    return [pltpu.SemaphoreType.DMA((n_bufs * (N_DEV - 1),)), pltpu.SemaphoreType.DMA((n_bufs * (N_DEV - 1),))]


def all_reduce(part, name):
    def body(x_ref, sum_ref, buf, send_sems, recv_sems):
        x, y, c = _position()
        buf[4 * x + 2 * y + c] = x_ref[...]
        _exchange_blocks([buf], send_sems, recv_sems)
        acc = buf[0]
        for d in range(1, N_DEV):
            acc = acc + buf[d]
        sum_ref[...] = acc

    return pl.pallas_call(
        body, name=name, in_specs=[VMEM_SPEC], out_specs=VMEM_SPEC, out_shape=jax.ShapeDtypeStruct(part.shape, F32),
        scratch_shapes=[pltpu.VMEM((N_DEV,) + part.shape, F32)] + _exchange_sems(1),
        compiler_params=pltpu.CompilerParams(vmem_limit_bytes=VMEM_LIMIT))(part)


def gather_weights(w_in_t, w_out, small, pad_rows):
    rows, cols = w_in_t.shape

    def body(wi_ref, wo_ref, sm_ref, wi_out, wo_out, sm_out, wi_buf, send_sems, recv_sems):
        x, y, c = _position()
        me = 4 * x + 2 * y + c
        wi_buf[me] = wi_ref[...].astype(MXU_DTYPE)
        wo_out[me] = wo_ref[...].astype(MXU_DTYPE)
        sm_out[me] = sm_ref[...]
        _exchange_blocks([wi_buf, wo_out, sm_out], send_sems, recv_sems)
        for d in range(N_DEV):
            wi_out[pl.ds(d * rows, rows), :] = wi_buf[d]
        wi_out[pl.ds(N_DEV * rows, pad_rows), :] = jnp.zeros((pad_rows, cols), MXU_DTYPE)

    return pl.pallas_call(
        body, name="gather_weights", in_specs=[VMEM_SPEC] * 3, out_specs=[VMEM_SPEC] * 3,
        out_shape=[jax.ShapeDtypeStruct((N_DEV * rows + pad_rows, cols), MXU_DTYPE),
                   jax.ShapeDtypeStruct((N_DEV,) + w_out.shape, MXU_DTYPE), jax.ShapeDtypeStruct((N_DEV,) + small.shape, F32)],
        scratch_shapes=[pltpu.VMEM((N_DEV, rows, cols), MXU_DTYPE)] + _exchange_sems(3),
        compiler_params=pltpu.CompilerParams(vmem_limit_bytes=VMEM_LIMIT))(w_in_t, w_out, small)


def reduce_scatter_rows(tensors, name):
    n_t = len(tensors)
    arrays = [a for parts, _ in tensors for a, _ in parts]
    first_array = [sum(len(parts) for parts, _ in tensors[:t]) for t in range(n_t)]

    def pieces(t, j):
        parts, block_rows = tensors[t]
        out, base = [], 0
        for pi, (_, valid) in enumerate(parts):
            lo, hi = max(j * block_rows, base), min((j + 1) * block_rows, base + valid)
            if lo < hi:
                out.append((first_array[t] + pi, lo - base, lo - j * block_rows, hi - lo))
            base += valid
        return out

    def body(*refs):
        in_refs, out_refs = refs[:len(arrays)], refs[len(arrays):len(arrays) + n_t]
        bufs = refs[len(arrays) + n_t:len(arrays) + 5 * n_t]
        s1_sems, r1_sems, s2_sems, r2_sems = refs[len(arrays) + 5 * n_t:]
        x, y, c = _position()
        chip = 2 * x + y

        def put(t, dst, j, add=None):
            for ai, src_row, dst_row, size in pieces(t, j):
                v = in_refs[ai][pl.ds(src_row, size), :]
                if add is not None:
                    v = v + add[pl.ds(dst_row, size), :].astype(F32)
                dst[pl.ds(dst_row, size), :] = v.astype(dst.dtype)

        def swap(t, k):
            send1, recv1 = bufs[4 * t], bufs[4 * t + 1]
            return pltpu.make_async_remote_copy(src_ref=send1.at[k], dst_ref=recv1.at[k], send_sem=s1_sems.at[4 * t + k],
                                                recv_sem=r1_sems.at[4 * t + k], device_id=(x, y, 1 - c), device_id_type=MESH)

        def to_chip(t, k, slot):
            send2, recv2 = bufs[4 * t + 2], bufs[4 * t + 3]
            return pltpu.make_async_remote_copy(src_ref=send2.at[k], dst_ref=recv2.at[slot], send_sem=s2_sems.at[4 * t + k],
                                                recv_sem=r2_sems.at[4 * t + slot], device_id=(k >> 1, k & 1, c),
                                                device_id_type=MESH)

        for t in range(n_t):
            for j in range(N_DEV):
                @pl.when((j & 1) != c)
                def _():
                    put(t, bufs[4 * t].at[j >> 1], j)
            for k in range(4):
                swap(t, k).start()

        for t in range(n_t):
            recv1 = bufs[4 * t + 1]
            for k in range(4):
                swap(t, k).wait_recv()
                for j in (2 * k, 2 * k + 1):
                    @pl.when(((j & 1) == c) & (k != chip))
                    def _():
                        put(t, bufs[4 * t + 2].at[k], j, add=recv1.at[k])
                        to_chip(t, k, chip).start()

                    @pl.when(((j & 1) == c) & (k == chip))
                    def _():
                        put(t, out_refs[t], j, add=recv1.at[k])

        for t in range(n_t):
            for k in range(4):
                @pl.when(k != chip)
                def _():
                    to_chip(t, k, k).wait_recv()
                    out_refs[t][...] += bufs[4 * t + 3][k].astype(F32)

        for t in range(n_t):
            for k in range(4):
                @pl.when(k != chip)
                def _():
                    to_chip(t, k, chip).wait_send()
                swap(t, k).wait_send()

    scratch, out_shape = [], []
    for parts, block_rows in tensors:
        cols = parts[0][0].shape[1]
        scratch += [pltpu.VMEM((4, block_rows, cols), MXU_DTYPE)] * 4
        out_shape.append(jax.ShapeDtypeStruct((block_rows, cols), F32))
    scratch += [pltpu.SemaphoreType.DMA((4 * n_t,))] * 4
    return pl.pallas_call(
        body, name=name, in_specs=[VMEM_SPEC] * len(arrays), out_specs=[VMEM_SPEC] * n_t, out_shape=out_shape,
        scratch_shapes=scratch, compiler_params=pltpu.CompilerParams(vmem_limit_bytes=VMEM_LIMIT),
    )(*arrays)


def adamw(w, g, m, v, name):
    rows, cols = w.shape
    tr = 256 if rows % 256 == 0 else rows

    def body(w_ref, g_ref, m_ref, v_ref, d_ref, nm_ref, nv_ref):
        gv = g_ref[...]
        mn = ADAM_B1 * m_ref[...] + (1.0 - ADAM_B1) * gv
        vn = ADAM_B2 * v_ref[...] + (1.0 - ADAM_B2) * jnp.square(gv)
        m_hat = mn / (1.0 - ADAM_B1 ** ADAM_STEP)
        v_hat = vn / (1.0 - ADAM_B2 ** ADAM_STEP)
        d_ref[...] = -ADAM_LR * (m_hat / (jnp.sqrt(v_hat) + ADAM_EPS) + ADAM_WD * w_ref[...])
        nm_ref[...] = mn
        nv_ref[...] = vn

    spec = pl.BlockSpec((tr, cols), lambda i: (i, 0))
    shape = jax.ShapeDtypeStruct((rows, cols), F32)
    return pl.pallas_call(body, grid=(rows // tr,), name=name, in_specs=[spec] * 4, out_specs=[spec] * 3,
                          out_shape=[shape] * 3, compiler_params=_cparams("arbitrary"))(w, g, m, v)


def _pad_rows(a, rows=8):
    return jnp.pad(a, ((0, rows - a.shape[0]), (0, 0)))


def _pad_lanes(a, lanes=128):
    return jnp.pad(a, ((0, 0), (0, lanes - a.shape[1])))


def kernel(x, meta_tokens, norm_w, w_in, conv_w, hg_lb_logits, hg_norm_w, gdn_A_log, gdn_dt_bias, gdn_norm_w, w_out, final_norm_w, loss_target, m_meta_tokens, m_norm_w, m_w_in, m_conv_w, m_hg_lb_logits, m_hg_norm_w, m_gdn_A_log, m_gdn_dt_bias, m_gdn_norm_w, m_w_out, m_final_norm_w, v_meta_tokens, v_norm_w, v_w_in, v_conv_w, v_hg_lb_logits, v_hg_norm_w, v_gdn_A_log, v_gdn_dt_bias, v_gdn_norm_w, v_w_out, v_final_norm_w):
    b, seq, _ = x.shape
    n = b * seq
    dev = 4 * lax.axis_index("x") + 2 * lax.axis_index("y") + lax.axis_index("c")
    col_shard = IN_COLS // N_DEV

    small_w = jnp.concatenate([_pad_lanes(meta_tokens, 256), _pad_rows(_pad_lanes(conv_w[0], 256))], axis=0)
    w_t, w_out_g, small_g = gather_weights(w_in[0].T, w_out[0], small_w, AB_PAD - 2 * HEADS)
    meta_g = small_g[:, 0:N_META, 0:D_MODEL // N_DEV]
    conv_g = small_g[:, N_META:N_META + CONV_TAPS, 0:QKV // N_DEV]
    w_out_full = w_out_g.reshape(2 * WIDTH, D_MODEL)
    cw = jnp.transpose(conv_g, (1, 0, 2)).reshape(CONV_TAPS, QKV)
    meta = jnp.transpose(meta_g, (1, 0, 2)).reshape(N_META, D_MODEL)
    alog = _pad_lanes(gdn_A_log)
    dtb = _pad_lanes(gdn_dt_bias)
    fw = final_norm_w.reshape(1, D_MODEL)

    h0 = jnp.concatenate([jnp.zeros((CHUNK - N_META, D_MODEL), F32), meta], axis=0)
    x2 = x.reshape(n, D_MODEL)
    u0, phg0, pgd0, pab0 = in_proj(h0, norm_w, w_t, "in_proj_lead")
    u, phg, pgd, pab = in_proj(x2, norm_w, w_t, "in_proj")
    phg3, pgd3, pab3 = phg.reshape(b, seq, 4 * WIDTH), pgd.reshape(b, seq, 4 * WIDTH), pab.reshape(b, seq, AB_PAD)
    nc = seq // CHUNK + 1
    hg_loc = hg_local_fwd(phg3, hg_lb_logits)
    hg_lead = hg_local_lead(phg0, hg_lb_logits)
    gd_loc = gd_local_fwd(pgd3, pgd0, pab3, cw, alog, dtb)
    gd_lead = gd_local_lead(pgd0, pab0, cw, alog, dtb)
    (y_hg, s_hg), (y_gd, s_gd) = run_scans([hg_scan_fwd(phg3, phg0, hg_loc, hg_lead, hg_norm_w),
                                            gd_scan_fwd(pgd3, pgd0, gd_loc, gd_lead, gdn_norm_w)], nc, "scans")

    dh2, dy_hg, dy_gd, g_w_out, loss_part, g_fw = out_proj_loss(
        x2, loss_target.reshape(n, D_MODEL), y_hg.reshape(n, WIDTH), y_gd.reshape(n, WIDTH), w_out_full, fw)

    hb, gb = run_scans([hg_scan_bwd(phg3, phg0, hg_loc, hg_lead, hg_norm_w, s_hg, dy_hg.reshape(b, seq, WIDTH)),
                        gd_scan_bwd(pgd3, pgd0, gd_loc, gd_lead, gdn_norm_w, s_gd, dy_gd.reshape(b, seq, WIDTH))],
                       nc, "scans_bwd")
    dphg, g_lb = hg_local_bwd(phg3, hg_lb_logits, *hb[0:6])
    dphg0, g_lb0 = hg_local_bwd_lead(phg0, hg_lb_logits, *hb[6:12])
    g_hg_nw = hb[12]
    dpgd, dpab, dtail, g_cw, g_alog, g_dtb = gd_local_bwd(pgd3, pgd0, pab3, cw, alog, dtb, gb[0:6], gb[6])
    dpgd0, dpab0, g_cw0, g_alog0, g_dtb0 = gd_local_bwd_lead(pgd0, pab0, cw, alog, dtb, gb[7:13], gb[13], dtail.sum(0))
    g_gd_nw = gb[14]
    dphg, dpgd, dpab = dphg.reshape(n, 4 * WIDTH), dpgd.reshape(n, 4 * WIDTH), dpab.reshape(n, AB_PAD)

    grad_x, g_nw = in_proj_bwd(dphg, dpgd, dpab, w_t, x2, dh2, norm_w, "in_proj_bwd")
    dh0, g_nw0 = in_proj_bwd(dphg0, dpgd0, dpab0, w_t, h0, jnp.zeros_like(h0), norm_w, "in_proj_bwd_lead")
    g_w_hg = weight_grad(u, dphg, u0, dphg0, "w_in_grad_hg")
    g_w_gd = weight_grad(u, dpgd, u0, dpgd0, "w_in_grad_gd")
    g_w_ab = weight_grad(u, dpab, u0, dpab0, "w_in_grad_ab")

    g_w_in_t, g_w_out = reduce_scatter_rows(
        [([(g_w_hg, 4 * WIDTH), (g_w_gd, 4 * WIDTH), (g_w_ab, 2 * HEADS)], col_shard),
         ([(g_w_out, 2 * WIDTH)], (2 * WIDTH) // N_DEV)], "reduce_weights")
    small = jnp.concatenate([
        (g_nw + g_nw0).reshape(8, 128), (g_lb + g_lb0).reshape(8, 128), _pad_rows(g_hg_nw), _pad_rows(g_alog + g_alog0),
        _pad_rows(g_dtb + g_dtb0), _pad_rows(g_gd_nw), g_fw.reshape(8, 128), (g_cw + g_cw0).reshape(48, 128),
        dh0[CHUNK - N_META:CHUNK].reshape(128, 128), loss_part], axis=0)
    small = all_reduce(small, "reduce_small")
    g_norm_w = small[0:8].reshape(1, D_MODEL)
    g_lb = small[8:16].reshape(2, WIDTH)
    g_hg_nw = small[16:17]
    g_alog = small[24:25, 0:HEADS]
    g_dtb = small[32:33, 0:HEADS]
    g_gd_nw = small[40:41]
    g_fw = small[48:56].reshape(1, D_MODEL)
    g_cw_full = small[56:104].reshape(CONV_TAPS, QKV)
    g_meta_full = small[104:232].reshape(N_META, D_MODEL)
    loss = small[232, 0]
    g_conv = lax.dynamic_slice_in_dim(g_cw_full, dev * (QKV // N_DEV), QKV // N_DEV, axis=1)
    g_meta = lax.dynamic_slice_in_dim(g_meta_full, dev * (D_MODEL // N_DEV), D_MODEL // N_DEV, axis=1)

    names = ["meta_tokens", "norm_w", "w_in", "conv_w", "hg_lb_logits", "hg_norm_w", "gdn_A_log", "gdn_dt_bias",
             "gdn_norm_w", "w_out", "final_norm_w"]
    weights = [meta_tokens, norm_w, w_in, conv_w, hg_lb_logits, hg_norm_w, gdn_A_log, gdn_dt_bias, gdn_norm_w, w_out,
               final_norm_w]
    moms = [m_meta_tokens, m_norm_w, m_w_in, m_conv_w, m_hg_lb_logits, m_hg_norm_w, m_gdn_A_log, m_gdn_dt_bias,
            m_gdn_norm_w, m_w_out, m_final_norm_w]
    vars_ = [v_meta_tokens, v_norm_w, v_w_in, v_conv_w, v_hg_lb_logits, v_hg_norm_w, v_gdn_A_log, v_gdn_dt_bias,
             v_gdn_norm_w, v_w_out, v_final_norm_w]
    grads2d = [g_meta, g_norm_w, g_w_in_t, g_conv, g_lb, g_hg_nw, g_alog, g_dtb, g_gd_nw, g_w_out, g_fw]
    grads, deltas, new_ms, new_vs = [], [], [], []
    for nm, w, g2, m, v in zip(names, weights, grads2d, moms, vars_):
        if nm == "w_in":
            to2d, back = (lambda a: a[0].T), (lambda a: a.T[None])
        else:
            to2d, back = (lambda a, s=g2.shape: a.reshape(s)), (lambda a, s=w.shape: a.reshape(s))
        d, nm_, nv_ = adamw(to2d(w), g2, to2d(m), to2d(v), "adamw_" + nm)
        grads.append(back(g2))
        deltas.append(back(d))
        new_ms.append(back(nm_))
        new_vs.append(back(nv_))
    return (loss, grad_x.reshape(x.shape), *grads, *deltas, *new_ms, *new_vs)
```

```python
import jax
import jax.numpy as jnp
from jax import lax
from jax.experimental import pallas as pl
from jax.experimental.pallas import tpu as pltpu

F32 = jnp.float32
BF16 = jnp.bfloat16
MXU_DTYPE = BF16

D_MODEL = 1024
N_META = 16
CHUNK = 64
SUB = 16
HEADS = 4
DH = 128
WIDTH = HEADS * DH
QKV = 3 * WIDTH
CONV_TAPS = 4
HALO = 8
EPS = 1e-6
IN_COLS = 4 * WIDTH + 4 * WIDTH + 2 * HEADS
AB_PAD = 128
N_DEV = 8
LOCAL_CHUNKS = 2
VMEM_LIMIT = 56 * 1024 * 1024

ADAM_LR = 0.001
ADAM_B1 = 0.9
ADAM_B2 = 0.999
ADAM_EPS = 1e-08
ADAM_WD = 0.01
ADAM_STEP = 10

VMEM_SPEC = pl.BlockSpec(memory_space=pltpu.VMEM)
MESH = pl.DeviceIdType.MESH


def _mm_tn(a, b):
    return lax.dot_general(a.astype(MXU_DTYPE), b.astype(MXU_DTYPE), (((0,), (0,)), ((), ())), preferred_element_type=F32)


def _bmm(a, b):
    return lax.dot_general(a.astype(MXU_DTYPE), b.astype(MXU_DTYPE), (((2,), (1,)), ((0,), (0,))), preferred_element_type=F32)


def _bmm_nt(a, b):
    return lax.dot_general(a.astype(MXU_DTYPE), b.astype(MXU_DTYPE), (((2,), (2,)), ((0,), (0,))), preferred_element_type=F32)


def _bmm_tn(a, b):
    return lax.dot_general(a.astype(MXU_DTYPE), b.astype(MXU_DTYPE), (((1,), (1,)), ((0,), (0,))), preferred_element_type=F32)


def _iota2(n, m):
    return lax.broadcasted_iota(jnp.int32, (n, m), 0), lax.broadcasted_iota(jnp.int32, (n, m), 1)


def _silu(x):
    return x * jax.nn.sigmoid(x)


def _gated_norm(o, z, nw):
    return o * lax.rsqrt(jnp.mean(o * o, axis=-1, keepdims=True) + EPS) * nw * _silu(z)


def _heads(a, nb):
    return jnp.stack([a[c * CHUNK:(c + 1) * CHUNK, h * DH:(h + 1) * DH] for c in range(nb) for h in range(HEADS)], axis=0)


def _unheads(a3, nb):
    return jnp.concatenate(
        [jnp.concatenate([a3[c * HEADS + h] for h in range(HEADS)], axis=1) for c in range(nb)], axis=0)


def _split3(x):
    hi = x.astype(BF16)
    r1 = x - hi.astype(F32)
    mid = r1.astype(BF16)
    return hi, mid, (r1 - mid.astype(F32)).astype(BF16)


def _select_mm(pattern, n_out, n_in, transposed, x):
    rows, inner = (n_in, n_out) if transposed else (n_out, n_in)
    r, c = _iota2(rows, 3 * inner)
    c = c - jnp.where(c >= inner, inner, 0) - jnp.where(c >= 2 * inner, inner, 0)
    s = jnp.where(pattern(c, r) if transposed else pattern(r, c), 1.0, 0.0).astype(BF16)
    return jnp.dot(s, jnp.concatenate(_split3(x), axis=0), preferred_element_type=F32)


def _select_rows(pattern, n_out, x):
    @jax.custom_vjp
    def apply(v):
        return _select_mm(pattern, n_out, CHUNK, False, v)

    apply.defvjp(lambda v: (_select_mm(pattern, n_out, CHUNK, False, v), None),
                 lambda _, d: (_select_mm(pattern, n_out, CHUNK, True, d),))
    return apply(x)


def _cumsum_chunks(x, nb):
    return jnp.concatenate([_select_rows(lambda i, j: j <= i, CHUNK, x[c * CHUNK:(c + 1) * CHUNK]) for c in range(nb)], axis=0)


HG_LEVELS = 6


def _hg_sums(i, j):
    lvl, t = i >> HG_LEVELS, i & (CHUNK - 1)
    last = t
    for l in range(1, HG_LEVELS + 1):
        width = HG_LEVELS + 1 - l
        last = jnp.where(lvl == l, ((t >> width) << width) + (CHUNK >> l) - 1, last)
    return j <= last


def hg_local(p, logits):
    nb = p.shape[0] // CHUNK
    l0, l1 = logits[0:1], logits[1:2]
    mx = jnp.maximum(l0, l1)
    e0, e1 = jnp.exp(l0 - mx), jnp.exp(l1 - mx)
    lb = e0 / (e0 + e1)
    q = _silu(p[:, 0:WIDTH])
    f = lb + (1.0 - lb) * jax.nn.sigmoid(p[:, WIDTH:2 * WIDTH])
    k = 1.0 - f
    logf = jnp.log(f)
    sums = [_select_rows(_hg_sums, (HG_LEVELS + 1) * CHUNK, logf[c * CHUNK:(c + 1) * CHUNK]) for c in range(nb)]
    level = lambda l: _heads(jnp.concatenate([s[l * CHUNK:(l + 1) * CHUNK] for s in sums], axis=0), nb)
    q3, k3, v3, g3 = _heads(q, nb), _heads(k, nb), _heads(p[:, 2 * WIDTH:3 * WIDTH], nb), level(0)
    r, c = _iota2(CHUNK, CHUNK)
    row = lax.broadcasted_iota(jnp.int32, (CHUNK, DH), 0)
    a = jnp.where(r == c, _bmm_nt(q3, k3), 0.0)
    for l in range(1, HG_LEVELS + 1):
        sh = HG_LEVELS - l
        qk = jnp.where(((row >> sh) & 1) == 1, q3, k3) * jnp.exp(-jnp.abs(g3 - level(l)))
        pair = ((r >> (sh + 1)) == (c >> (sh + 1))) & (((r >> sh) & 1) == 1) & (((c >> sh) & 1) == 0)
        a = a + jnp.where(pair, _bmm_nt(qk, qk), 0.0)
    o = _bmm(a, v3)
    glast = g3[:, CHUNK - 1:CHUNK, :]
    egs = tuple(jnp.concatenate([jnp.exp(glast[c * HEADS + h]) for h in range(HEADS)], axis=1) for c in range(nb))
    return _unheads(q3 * jnp.exp(g3), nb), _unheads(k3 * jnp.exp(glast - g3), nb), _unheads(o, nb), egs


def hg_scan(q_in, k_out, v, eg, o_intra, z, nw, st):
    o = o_intra + _bmm_nt(q_in, st)
    return _gated_norm(o, z, nw), st * eg + _bmm_tn(v, k_out)


def _tri_y_impl(a):
    r, c = _iota2(CHUNK, CHUNK)
    same16 = (r // SUB) == (c // SUB)
    same32 = (r // (2 * SUB)) == (c // (2 * SUB))
    a0 = jnp.where(same16, a, 0.0)
    y = -a0
    pw = _bmm(a0, a0)
    for _ in range(2):
        y = y + pw + _bmm(y, pw)
        pw = _bmm(pw, pw)
    y = y + pw + _bmm(y, pw)
    for ak in (jnp.where(same32 & jnp.logical_not(same16), a, 0.0), jnp.where(same32, 0.0, a)):
        m = ak + _bmm(y, ak)
        y = y - (m + _bmm(m, y))
    return y


@jax.custom_vjp
def _tri_y(a):
    return _tri_y_impl(a)


def _tri_y_fwd(a):
    y = _tri_y_impl(a)
    return y, y


def _tri_y_bwd(y, dy):
    n = dy + _bmm_tn(y, dy)
    return (-(n + _bmm_nt(n, y)),)


_tri_y.defvjp(_tri_y_fwd, _tri_y_bwd)


def gd_local(xx, ab, cw, alog, dtb, inverse=_tri_y):
    n = ab.shape[0]
    nb = n // CHUNK
    conv = cw[0:1] * xx[HALO - 3:HALO - 3 + n]
    for j in range(1, CONV_TAPS):
        conv = conv + cw[j:j + 1] * xx[HALO - 3 + j:HALO - 3 + j + n]
    act = _silu(conv)
    x = ab + dtb
    g_all = -jnp.exp(alog) * (jnp.maximum(x, 0.0) + jnp.log1p(jnp.exp(-jnp.abs(x))))
    beta_all = jax.nn.sigmoid(ab)
    gam_all = _cumsum_chunks(g_all, nb)
    q3, k3, v3 = _heads(act[:, 0:WIDTH], nb), _heads(act[:, WIDTH:2 * WIDTH], nb), _heads(act[:, 2 * WIDTH:QKV], nb)
    q3 = q3 * lax.rsqrt(jnp.sum(q3 * q3, axis=-1, keepdims=True) + EPS) * (DH ** -0.5)
    k3 = k3 * lax.rsqrt(jnp.sum(k3 * k3, axis=-1, keepdims=True) + EPS)
    pairs = [(c, h) for c in range(nb) for h in range(HEADS)]
    beta = jnp.stack([beta_all[c * CHUNK:(c + 1) * CHUNK, HEADS + h:HEADS + h + 1] for c, h in pairs], axis=0)
    gam = jnp.stack([gam_all[c * CHUNK:(c + 1) * CHUNK, h:h + 1] for c, h in pairs], axis=0)
    gam_t = [gam_all[c * CHUNK:(c + 1) * CHUNK].T for c in range(nb)]
    gam_row = jnp.stack([gam_t[c][h:h + 1, :] for c, h in pairs], axis=0)
    glast = gam[:, CHUNK - 1:CHUNK, :]
    r, c = _iota2(CHUNK, CHUNK)
    dec = jnp.exp(jnp.where(c < r, gam - gam_row, -jnp.inf))
    y = inverse(beta * _bmm_nt(k3, k3) * dec)
    eg = jnp.exp(gam)
    rhs = jnp.concatenate([beta * v3, (beta * eg) * k3], axis=2)
    sol = rhs + _bmm(y, rhs)
    qk = _bmm_nt(q3, k3) * jnp.where(r == c, 1.0, dec)
    eas = tuple(jnp.exp(gam_all[(c + 1) * CHUNK - 1:(c + 1) * CHUNK]) for c in range(nb))
    return (_unheads(sol[:, :, 0:DH], nb), _unheads(sol[:, :, DH:2 * DH], nb), _unheads(q3 * eg, nb),
            _unheads(k3 * jnp.exp(glast - gam), nb), jnp.concatenate([qk[g] for g in range(nb * HEADS)], axis=0), eas)


def gd_scan(uu, ww, qe, ke, qk, ea, z, nw, s):
    u = uu - _bmm(ww, s)
    o = _bmm(qe, s) + _bmm(qk, u)
    return _gated_norm(o, z, nw), ea * s + _bmm_tn(ke, u)


def _cparams(*sem):
    return pltpu.CompilerParams(dimension_semantics=sem, vmem_limit_bytes=VMEM_LIMIT)


def _row_tile(n):
    for t in (256, 128, 64):
        if n % t == 0:
            return t
    raise ValueError(f"unsupported token count {n}")


def _w_in_specs():
    return [pl.BlockSpec((4 * WIDTH, D_MODEL), lambda *i: (0, 0)), pl.BlockSpec((4 * WIDTH, D_MODEL), lambda *i: (1, 0)),
            pl.BlockSpec((AB_PAD, D_MODEL), lambda *i: (8 * WIDTH // AB_PAD, 0))]


def in_proj(h, norm_w, w_t, name):
    n = h.shape[0]
    tm = _row_tile(n)
    nt = (((1,), (1,)), ((), ()))

    def body(h_ref, nw_ref, whg_ref, wgd_ref, wab_ref, u_ref, phg_ref, pgd_ref, pab_ref):
        x = h_ref[...]
        u = (x * lax.rsqrt(jnp.mean(x * x, axis=-1, keepdims=True) + EPS) * nw_ref[...]).astype(MXU_DTYPE)
        u_ref[...] = u
        phg_ref[...] = lax.dot_general(u, whg_ref[...], nt, preferred_element_type=F32)
        pgd_ref[...] = lax.dot_general(u, wgd_ref[...], nt, preferred_element_type=F32)
        pab_ref[...] = lax.dot_general(u, wab_ref[...], nt, preferred_element_type=F32)

    row = lambda w: pl.BlockSpec((tm, w), lambda i: (i, 0))
    return pl.pallas_call(
        body, grid=(n // tm,), name=name,
        in_specs=[row(D_MODEL), pl.BlockSpec(norm_w.shape, lambda i: (0, 0))] + _w_in_specs(),
        out_specs=[row(D_MODEL), row(4 * WIDTH), row(4 * WIDTH), row(AB_PAD)],
        out_shape=[jax.ShapeDtypeStruct((n, D_MODEL), MXU_DTYPE), jax.ShapeDtypeStruct((n, 4 * WIDTH), F32),
                   jax.ShapeDtypeStruct((n, 4 * WIDTH), F32), jax.ShapeDtypeStruct((n, AB_PAD), F32)],
        compiler_params=_cparams("arbitrary"),
    )(h, norm_w, w_t, w_t, w_t)


def out_proj_loss(x, tgt, y_hg, y_gd, w_out, fw):
    n = x.shape[0]
    tm = _row_tile(n)
    inv_d = 1.0 / D_MODEL

    def body(x_ref, t_ref, yh_ref, yg_ref, w_ref, fw_ref, dh_ref, dyh_ref, dyg_ref, dw_ref, loss_ref, dfw_ref):
        @pl.when(pl.program_id(0) == 0)
        def _():
            dw_ref[...] = jnp.zeros_like(dw_ref)
            loss_ref[...] = jnp.zeros_like(loss_ref)
            dfw_ref[...] = jnp.zeros_like(dfw_ref)

        yh, yg = yh_ref[...], yg_ref[...]
        wa, wb = w_ref[0:WIDTH, :], w_ref[WIDTH:2 * WIDTH, :]
        h2 = x_ref[...] + jnp.dot(yh, wa, preferred_element_type=F32) + jnp.dot(yg, wb, preferred_element_type=F32)
        r2 = lax.rsqrt(jnp.mean(h2 * h2, axis=-1, keepdims=True) + EPS)
        nrm = h2 * r2
        fwv = fw_ref[...]
        err = nrm * fwv - t_ref[...]
        loss_ref[...] += jnp.full(loss_ref.shape, 0.5 * inv_d * jnp.sum(err * err), F32)
        dout = err * inv_d
        dfw_ref[...] += jnp.sum(dout * nrm, axis=0, keepdims=True)
        dn = dout * fwv
        dh2 = r2 * (dn - nrm * jnp.mean(dn * nrm, axis=-1, keepdims=True))
        dh_ref[...] = dh2
        dhb = dh2.astype(MXU_DTYPE)
        dyh_ref[...] = lax.dot_general(dhb, wa, (((1,), (1,)), ((), ())), preferred_element_type=F32)
        dyg_ref[...] = lax.dot_general(dhb, wb, (((1,), (1,)), ((), ())), preferred_element_type=F32)
        dw_ref[0:WIDTH, :] += lax.dot_general(yh, dhb, (((0,), (0,)), ((), ())), preferred_element_type=F32)
        dw_ref[WIDTH:2 * WIDTH, :] += lax.dot_general(yg, dhb, (((0,), (0,)), ((), ())), preferred_element_type=F32)

    row = lambda w: pl.BlockSpec((tm, w), lambda i: (i, 0))
    full = lambda s: pl.BlockSpec(s, lambda i: (0, 0))
    return pl.pallas_call(
        body, grid=(n // tm,), name="out_proj_loss",
        in_specs=[row(D_MODEL), row(D_MODEL), row(WIDTH), row(WIDTH), full(w_out.shape), full(fw.shape)],
        out_specs=[row(D_MODEL), row(WIDTH), row(WIDTH), full((2 * WIDTH, D_MODEL)), full((8, 128)), full((1, D_MODEL))],
        out_shape=[jax.ShapeDtypeStruct((n, D_MODEL), F32), jax.ShapeDtypeStruct((n, WIDTH), F32),
                   jax.ShapeDtypeStruct((n, WIDTH), F32), jax.ShapeDtypeStruct((2 * WIDTH, D_MODEL), F32),
                   jax.ShapeDtypeStruct((8, 128), F32), jax.ShapeDtypeStruct((1, D_MODEL), F32)],
        compiler_params=_cparams("arbitrary"),
    )(x, tgt, y_hg, y_gd, w_out, fw)


def in_proj_bwd(dphg, dpgd, dpab, w_t, h, dh2, norm_w, name):
    n = h.shape[0]
    tm = _row_tile(n)

    def body(dphg_ref, dpgd_ref, dpab_ref, whg_ref, wgd_ref, wab_ref, h_ref, dh2_ref, nw_ref, dx_ref, dnw_ref):
        @pl.when(pl.program_id(0) == 0)
        def _():
            dnw_ref[...] = jnp.zeros_like(dnw_ref)

        du = jnp.dot(dphg_ref[...].astype(MXU_DTYPE), whg_ref[...], preferred_element_type=F32)
        du += jnp.dot(dpgd_ref[...].astype(MXU_DTYPE), wgd_ref[...], preferred_element_type=F32)
        du += jnp.dot(dpab_ref[...].astype(MXU_DTYPE), wab_ref[...], preferred_element_type=F32)
        x = h_ref[...]
        r = lax.rsqrt(jnp.mean(x * x, axis=-1, keepdims=True) + EPS)
        nrm = x * r
        dnw_ref[...] += jnp.sum(du * nrm, axis=0, keepdims=True)
        dn = du * nw_ref[...]
        dx_ref[...] = dh2_ref[...] + r * (dn - nrm * jnp.mean(dn * nrm, axis=-1, keepdims=True))

    row = lambda w: pl.BlockSpec((tm, w), lambda i: (i, 0))
    return pl.pallas_call(
        body, grid=(n // tm,), name=name,
        in_specs=[row(4 * WIDTH), row(4 * WIDTH), row(AB_PAD)] + _w_in_specs() + [row(D_MODEL), row(D_MODEL),
                                                                                   pl.BlockSpec(norm_w.shape, lambda i: (0, 0))],
        out_specs=[row(D_MODEL), pl.BlockSpec((1, D_MODEL), lambda i: (0, 0))],
        out_shape=[jax.ShapeDtypeStruct((n, D_MODEL), F32), jax.ShapeDtypeStruct((1, D_MODEL), F32)],
        compiler_params=_cparams("arbitrary"),
    )(dphg, dpgd, dpab, w_t, w_t, w_t, h, dh2, norm_w)


def weight_grad(u, dp, u0, dp0, name, gates=None):
    n, w = dp.shape
    tn = min(w, 1024)
    tm = 1024 if n % 1024 == 0 else _row_tile(n)
    n0 = u0.shape[0]

    def body(u_ref, dp_ref, u0_ref, dp0_ref, *rest):
        o_ref = rest[-2] if gates else rest[-1]

        @pl.when(pl.program_id(1) == 0)
        def _():
            o_ref[...] = _mm_tn(dp0_ref[...], u0_ref[...])

        o_ref[...] += _mm_tn(dp_ref[...], u_ref[...])
        if gates:
            dpab_ref, dpab0_ref, _, oab_ref = rest

            @pl.when((pl.program_id(0) == 0) & (pl.program_id(1) == 0))
            def _():
                oab_ref[...] = _mm_tn(dpab0_ref[...], u0_ref[...])

            @pl.when(pl.program_id(0) == 0)
            def _():
                oab_ref[...] += _mm_tn(dpab_ref[...], u_ref[...])

    in_specs = [pl.BlockSpec((tm, D_MODEL), lambda j, t: (t, 0)), pl.BlockSpec((tm, tn), lambda j, t: (t, j)),
                pl.BlockSpec((n0, D_MODEL), lambda j, t: (0, 0)), pl.BlockSpec((n0, tn), lambda j, t: (0, j))]
    out_specs = [pl.BlockSpec((tn, D_MODEL), lambda j, t: (j, 0))]
    out_shape = [jax.ShapeDtypeStruct((w, D_MODEL), F32)]
    args = [u, dp, u0, dp0]
    if gates:
        in_specs += [pl.BlockSpec((tm, AB_PAD), lambda j, t: (t, 0)), pl.BlockSpec((n0, AB_PAD), lambda j, t: (0, 0))]
        out_specs.append(pl.BlockSpec((AB_PAD, D_MODEL), lambda j, t: (0, 0)))
        out_shape.append(jax.ShapeDtypeStruct((AB_PAD, D_MODEL), F32))
        args += list(gates)
    out = pl.pallas_call(body, grid=(w // tn, n // tm), name=name, in_specs=in_specs, out_specs=out_specs,
                         out_shape=out_shape, compiler_params=_cparams("arbitrary", "arbitrary"))(*args)
    return out if gates else out[0]


def _real(c):
    return jnp.maximum(c - 1, 0)


def _sds(shape, dtype=F32):
    return jax.ShapeDtypeStruct(shape, dtype)


def _load_slabs(ref, b):
    return jnp.stack([ref[i, :, h * DH:(h + 1) * DH].astype(F32) for i in range(b) for h in range(HEADS)], axis=0)


def _lead_slabs(a, b):
    return jnp.stack([a[:, h * DH:(h + 1) * DH].astype(F32) for _ in range(b) for h in range(HEADS)], axis=0)


def _rows(a3, i):
    return jnp.concatenate([a3[i * HEADS + h] for h in range(HEADS)], axis=1)


def _store_slabs(ref, a3, b):
    for i in range(b):
        ref[i] = _rows(a3, i).astype(ref.dtype)


def _sum_rows(a3, b):
    out = _rows(a3, 0)
    for i in range(1, b):
        out = out + _rows(a3, i)
    return out


def _save_states(ref, s, b):
    for i in range(b):
        ref[i] = jnp.concatenate([s[i * HEADS + h] for h in range(HEADS)], axis=0)


def _load_states(ref, b):
    return jnp.stack([ref[i, h * DH:(h + 1) * DH, :] for i in range(b) for h in range(HEADS)], axis=0)


def hg_local_fwd(p, logits):
    b, seq, _ = p.shape
    rows = LOCAL_CHUNKS * CHUNK
    nreal = seq // CHUNK

    def body(p_ref, lg_ref, q_ref, k_ref, o_ref, eg_ref):
        q_in, k_out, o_intra, egs = hg_local(p_ref[...], lg_ref[...])
        q_ref[...], k_ref[...], o_ref[...] = q_in.astype(MXU_DTYPE), k_out.astype(MXU_DTYPE), o_intra
        for c in range(LOCAL_CHUNKS):
            eg_ref[c] = egs[c]

    slab = pl.BlockSpec((None, rows, WIDTH), lambda s, g: (s, g, 0))
    return pl.pallas_call(
        body, grid=(b, seq // rows), name="hgrn2_local",
        in_specs=[pl.BlockSpec((None, rows, 4 * WIDTH), lambda s, g: (s, g, 0)), pl.BlockSpec(logits.shape, lambda s, g: (0, 0))],
        out_specs=[slab, slab, slab, pl.BlockSpec((None, LOCAL_CHUNKS, 1, WIDTH), lambda s, g: (s, g, 0, 0))],
        out_shape=[_sds((b, seq, WIDTH), MXU_DTYPE)] * 2 + [_sds((b, seq, WIDTH)), _sds((b, nreal, 1, WIDTH))],
        compiler_params=_cparams("arbitrary", "arbitrary"),
    )(p, logits)


def hg_local_lead(p0, logits):
    def body(p_ref, lg_ref, q_ref, k_ref, o_ref, eg_ref):
        q_in, k_out, o_ref[...], (eg_ref[...],) = hg_local(p_ref[...], lg_ref[...])
        q_ref[...], k_ref[...] = q_in.astype(MXU_DTYPE), k_out.astype(MXU_DTYPE)

    return pl.pallas_call(
        body, name="hgrn2_local_lead", in_specs=[VMEM_SPEC] * 2, out_specs=[VMEM_SPEC] * 4,
        out_shape=[_sds((CHUNK, WIDTH), MXU_DTYPE)] * 2 + [_sds((CHUNK, WIDTH)), _sds((1, WIDTH))],
        compiler_params=pltpu.CompilerParams(vmem_limit_bytes=VMEM_LIMIT),
    )(p0, logits)


def _hg_scan_inputs(c, b, q_ref, k_ref, o_ref, v_ref, z_ref, eg_ref, q0_ref, k0_ref, o0_ref, p0_ref, eg0_ref):
    lead = c == 0
    pick = lambda real, lead_val: jnp.where(lead, _lead_slabs(lead_val, b), _load_slabs(real, b))
    eg = jnp.where(lead, jnp.stack([eg0_ref[:, h * DH:(h + 1) * DH] for _ in range(b) for h in range(HEADS)], axis=0),
                   jnp.stack([eg_ref[i, :, h * DH:(h + 1) * DH] for i in range(b) for h in range(HEADS)], axis=0))
    return (pick(q_ref, q0_ref[...]), pick(k_ref, k0_ref[...]), pick(v_ref, p0_ref[:, 2 * WIDTH:3 * WIDTH]), eg,
            pick(o_ref, o0_ref[...]), pick(z_ref, p0_ref[:, 3 * WIDTH:4 * WIDTH]))


def _scan_specs(b, nc, reverse):
    chunk = (lambda i: nc - 1 - i) if reverse else (lambda i: i)
    slab = lambda lane_block: pl.BlockSpec((b, CHUNK, WIDTH), lambda i: (0, _real(chunk(i)), lane_block))
    per_chunk = lambda *tail: pl.BlockSpec((b, None) + tail, lambda i: (0, _real(chunk(i))) + (0,) * len(tail))
    state = pl.BlockSpec((b, None, WIDTH, DH), lambda i: (0, chunk(i), 0, 0))
    const = lambda a: pl.BlockSpec(a.shape, lambda i: (0,) * a.ndim)
    return slab, per_chunk, state, const


def run_scans(parts, nc, name):
    n_in = [len(p["args"]) for p in parts]
    n_out = [len(p["out_shape"]) for p in parts]
    n_scr = [len(p["scratch_shapes"]) for p in parts]

    def body(*refs):
        ins, outs, scr = refs[:sum(n_in)], refs[sum(n_in):sum(n_in) + sum(n_out)], refs[sum(n_in) + sum(n_out):]
        for i, part in enumerate(parts):
            part["body"](*ins[sum(n_in[:i]):sum(n_in[:i + 1])], *outs[sum(n_out[:i]):sum(n_out[:i + 1])],
                         *scr[sum(n_scr[:i]):sum(n_scr[:i + 1])])

    flat = lambda key: [v for p in parts for v in p[key]]
    out = pl.pallas_call(body, grid=(nc,), name=name, in_specs=flat("in_specs"), out_specs=flat("out_specs"),
                         out_shape=flat("out_shape"), scratch_shapes=flat("scratch_shapes"),
                         compiler_params=_cparams("arbitrary"))(*flat("args"))
    return [out[sum(n_out[:i]):sum(n_out[:i + 1])] for i in range(len(parts))]


def hg_scan_fwd(p, p0, local, lead, nw):
    b, seq, _ = p.shape
    nc = seq // CHUNK + 1
    q_in, k_out, o_intra, eg = local
    slab, per_chunk, state, const = _scan_specs(b, nc, False)

    def body(q_ref, k_ref, o_ref, v_ref, z_ref, eg_ref, q0_ref, k0_ref, o0_ref, p0_ref, eg0_ref, nw_ref, y_ref, ss_ref, st):
        c = pl.program_id(0)

        @pl.when(c == 0)
        def _():
            st[...] = jnp.zeros_like(st)

        s_in = st[...]
        _save_states(ss_ref, s_in, b)
        args = _hg_scan_inputs(c, b, q_ref, k_ref, o_ref, v_ref, z_ref, eg_ref, q0_ref, k0_ref, o0_ref, p0_ref, eg0_ref)
        y, s_new = hg_scan(*args, nw_ref[...], s_in)
        _store_slabs(y_ref, y, b)
        st[...] = s_new

    return dict(
        body=body, args=(q_in, k_out, o_intra, p, p, eg, lead[0], lead[1], lead[2], p0, lead[3], nw),
        in_specs=[slab(0), slab(0), slab(0), slab(2), slab(3), per_chunk(1, WIDTH)] + [const(a) for a in lead[0:3]]
        + [const(p0), const(lead[3]), const(nw)],
        out_specs=[slab(0), state],
        out_shape=[_sds((b, seq, WIDTH), MXU_DTYPE), _sds((b, nc, WIDTH, DH))],
        scratch_shapes=[pltpu.VMEM((b * HEADS, DH, DH), F32)])


def hg_scan_bwd(p, p0, local, lead, nw, ssave, dy):
    b, seq, _ = p.shape
    nc = seq // CHUNK + 1
    q_in, k_out, o_intra, eg = local
    slab, per_chunk, state, const = _scan_specs(b, nc, True)

    def body(q_ref, k_ref, o_ref, v_ref, z_ref, eg_ref, q0_ref, k0_ref, o0_ref, p0_ref, eg0_ref, nw_ref, ss_ref, dy_ref,
             dq_ref, dk_ref, do_ref, dv_ref, dz_ref, deg_ref, dq0_ref, dk0_ref, do0_ref, dv0_ref, dz0_ref, deg0_ref, dnw_ref,
             dst):
        i = pl.program_id(0)
        c = nc - 1 - i

        @pl.when(i == 0)
        def _():
            dst[...] = jnp.zeros_like(dst)
            dnw_ref[...] = jnp.zeros_like(dnw_ref)

        args = _hg_scan_inputs(c, b, q_ref, k_ref, o_ref, v_ref, z_ref, eg_ref, q0_ref, k0_ref, o0_ref, p0_ref, eg0_ref)
        s_in = _load_states(ss_ref, b)
        _, vjp = jax.vjp(hg_scan, *args, nw_ref[...], s_in)
        dyv = jnp.where(c == 0, 0.0, _load_slabs(dy_ref, b))
        dq, dk, dv, deg, do, dz, dnw, ds = vjp((dyv, dst[...]))
        dst[...] = ds
        dnw_ref[...] += dnw

        @pl.when(c > 0)
        def _():
            for ref, val in ((dq_ref, dq), (dk_ref, dk), (do_ref, do), (dv_ref, dv), (dz_ref, dz)):
                _store_slabs(ref, val, b)
            for j in range(b):
                deg_ref[j] = _rows(deg, j)

        @pl.when(c == 0)
        def _():
            for ref, val in ((dq0_ref, dq), (dk0_ref, dk), (do0_ref, do), (dv0_ref, dv), (dz0_ref, dz), (deg0_ref, deg)):
                ref[...] = _sum_rows(val, b)

    lead_out = [const(a) for a in lead[0:3]] + [const(lead[0]), const(lead[0]), const(lead[3])]
    return dict(
        body=body, args=(q_in, k_out, o_intra, p, p, eg, lead[0], lead[1], lead[2], p0, lead[3], nw, ssave, dy),
        in_specs=[slab(0), slab(0), slab(0), slab(2), slab(3), per_chunk(1, WIDTH)] + [const(a) for a in lead[0:3]]
        + [const(p0), const(lead[3]), const(nw), state, slab(0)],
        out_specs=[slab(0)] * 5 + [per_chunk(1, WIDTH)] + lead_out + [const(nw)],
        out_shape=[_sds((b, seq, WIDTH))] * 5 + [_sds(eg.shape)] + [_sds((CHUNK, WIDTH))] * 5 + [_sds((1, WIDTH)), _sds(nw.shape)],
        scratch_shapes=[pltpu.VMEM((b * HEADS, DH, DH), F32)])


def _hg_local_vjp(p, logits, dq, dk, do, degs, dv, dz):
    _, vjp = jax.vjp(hg_local, p, logits)
    dp, dlg = vjp((dq, dk, do, degs))
    return dp + jnp.concatenate([jnp.zeros((p.shape[0], 2 * WIDTH), F32), dv, dz], axis=1), dlg


def hg_local_bwd(p, logits, dq, dk, do, dv, dz, deg):
    b, seq, _ = p.shape
    rows = LOCAL_CHUNKS * CHUNK

    def body(p_ref, lg_ref, dq_ref, dk_ref, do_ref, dv_ref, dz_ref, deg_ref, dp_ref, dlg_ref):
        @pl.when((pl.program_id(0) == 0) & (pl.program_id(1) == 0))
        def _():
            dlg_ref[...] = jnp.zeros_like(dlg_ref)

        degs = tuple(deg_ref[c] for c in range(LOCAL_CHUNKS))
        dp, dlg = _hg_local_vjp(p_ref[...], lg_ref[...], dq_ref[...], dk_ref[...], do_ref[...], degs, dv_ref[...], dz_ref[...])
        dp_ref[...] = dp.astype(MXU_DTYPE)
        dlg_ref[...] += dlg

    slab = pl.BlockSpec((None, rows, WIDTH), lambda s, g: (s, g, 0))
    wide = pl.BlockSpec((None, rows, 4 * WIDTH), lambda s, g: (s, g, 0))
    lg = pl.BlockSpec(logits.shape, lambda s, g: (0, 0))
    return pl.pallas_call(
        body, grid=(b, seq // rows), name="hgrn2_local_bwd",
        in_specs=[wide, lg, slab, slab, slab, slab, slab, pl.BlockSpec((None, LOCAL_CHUNKS, 1, WIDTH), lambda s, g: (s, g, 0, 0))],
        out_specs=[wide, lg], out_shape=[_sds(p.shape, MXU_DTYPE), _sds(logits.shape)],
        compiler_params=_cparams("arbitrary", "arbitrary"),
    )(p, logits, dq, dk, do, dv, dz, deg)


def hg_local_bwd_lead(p0, logits, dq, dk, do, dv, dz, deg):
    def body(p_ref, lg_ref, dq_ref, dk_ref, do_ref, dv_ref, dz_ref, deg_ref, dp_ref, dlg_ref):
        dp, dlg_ref[...] = _hg_local_vjp(p_ref[...], lg_ref[...], dq_ref[...], dk_ref[...], do_ref[...],
                                         (deg_ref[...],), dv_ref[...], dz_ref[...])
        dp_ref[...] = dp.astype(MXU_DTYPE)

    return pl.pallas_call(
        body, name="hgrn2_local_bwd_lead", in_specs=[VMEM_SPEC] * 8, out_specs=[VMEM_SPEC] * 2,
        out_shape=[_sds(p0.shape, MXU_DTYPE), _sds(logits.shape)], compiler_params=pltpu.CompilerParams(vmem_limit_bytes=VMEM_LIMIT),
    )(p0, logits, dq, dk, do, dv, dz, deg)


def _halo_block(g):
    return jnp.maximum((LOCAL_CHUNKS * CHUNK // HALO) * g - 1, 0)


def _gd_window(g, p_ref, halo_ref, p0_ref):
    halo = jnp.where(g == 0, p0_ref[CHUNK - HALO:CHUNK, 0:QKV], halo_ref[...])
    return jnp.concatenate([halo, p_ref[:, 0:QKV]], axis=0)


def gd_local_fwd(p, p0, ab, cw, alog, dtb):
    b, seq, _ = p.shape
    rows = LOCAL_CHUNKS * CHUNK
    nreal = seq // CHUNK

    def body(p_ref, halo_ref, p0_ref, ab_ref, cw_ref, al_ref, dt_ref, u_ref, w_ref, qe_ref, ke_ref, qk_ref, ea_ref):
        uu, ww, qe, ke, qk, eas = gd_local(_gd_window(pl.program_id(1), p_ref, halo_ref, p0_ref), ab_ref[...], cw_ref[...],
                                           al_ref[...], dt_ref[...], inverse=_tri_y_impl)
        u_ref[...], w_ref[...], qe_ref[...], ke_ref[...] = uu, ww.astype(MXU_DTYPE), qe.astype(MXU_DTYPE), ke.astype(MXU_DTYPE)
        for c in range(LOCAL_CHUNKS):
            qk_ref[c] = qk[c * HEADS * CHUNK:(c + 1) * HEADS * CHUNK]
            ea_ref[c] = eas[c]

    const = lambda a: pl.BlockSpec(a.shape, lambda s, g: (0, 0))
    slab = pl.BlockSpec((None, rows, WIDTH), lambda s, g: (s, g, 0))
    return pl.pallas_call(
        body, grid=(b, seq // rows), name="gdn_local",
        in_specs=[pl.BlockSpec((None, rows, 4 * WIDTH), lambda s, g: (s, g, 0)),
                  pl.BlockSpec((None, HALO, QKV), lambda s, g: (s, _halo_block(g), 0)), const(p0),
                  pl.BlockSpec((None, rows, AB_PAD), lambda s, g: (s, g, 0)), const(cw), const(alog), const(dtb)],
        out_specs=[slab] * 4 + [pl.BlockSpec((None, LOCAL_CHUNKS, HEADS * CHUNK, CHUNK), lambda s, g: (s, g, 0, 0)),
                                pl.BlockSpec((None, LOCAL_CHUNKS, 1, AB_PAD), lambda s, g: (s, g, 0, 0))],
        out_shape=[_sds((b, seq, WIDTH))] + [_sds((b, seq, WIDTH), MXU_DTYPE)] * 3
        + [_sds((b, nreal, HEADS * CHUNK, CHUNK)), _sds((b, nreal, 1, AB_PAD))],
        compiler_params=_cparams("arbitrary", "arbitrary"),
    )(p, p, p0, ab, cw, alog, dtb)


def _lead_window(p0_ref):
    return jnp.concatenate([jnp.zeros((HALO, QKV), F32), p0_ref[:, 0:QKV]], axis=0)


def gd_local_lead(p0, ab0, cw, alog, dtb):
    def body(p0_ref, ab_ref, cw_ref, al_ref, dt_ref, u_ref, w_ref, qe_ref, ke_ref, qk_ref, ea_ref):
        u_ref[...], ww, qe, ke, qk_ref[...], (ea_ref[...],) = gd_local(
            _lead_window(p0_ref), ab_ref[...], cw_ref[...], al_ref[...], dt_ref[...], inverse=_tri_y_impl)
        w_ref[...], qe_ref[...], ke_ref[...] = ww.astype(MXU_DTYPE), qe.astype(MXU_DTYPE), ke.astype(MXU_DTYPE)

    return pl.pallas_call(
        body, name="gdn_local_lead", in_specs=[VMEM_SPEC] * 5, out_specs=[VMEM_SPEC] * 6,
        out_shape=[_sds((CHUNK, WIDTH))] + [_sds((CHUNK, WIDTH), MXU_DTYPE)] * 3 + [_sds((HEADS * CHUNK, CHUNK)), _sds((1, AB_PAD))],
        compiler_params=pltpu.CompilerParams(vmem_limit_bytes=VMEM_LIMIT),
    )(p0, ab0, cw, alog, dtb)


def _gd_scan_inputs(c, b, u_ref, w_ref, qe_ref, ke_ref, qk_ref, ea_ref, z_ref, u0_ref, w0_ref, qe0_ref, ke0_ref, qk0_ref,
                    ea0_ref, p0_ref):
    lead = c == 0
    pick = lambda real, lead_val: jnp.where(lead, _lead_slabs(lead_val, b), _load_slabs(real, b))
    pairs = [(i, h) for i in range(b) for h in range(HEADS)]
    qk = jnp.where(lead, jnp.stack([qk0_ref[h * CHUNK:(h + 1) * CHUNK, :] for _, h in pairs], axis=0),
                   jnp.stack([qk_ref[i, h * CHUNK:(h + 1) * CHUNK, :] for i, h in pairs], axis=0))
    ea = jnp.where(lead, jnp.stack([ea0_ref[:, h:h + 1] for _, h in pairs], axis=0),
                   jnp.stack([ea_ref[i, :, h:h + 1] for i, h in pairs], axis=0))
    return (pick(u_ref, u0_ref[...]), pick(w_ref, w0_ref[...]), pick(qe_ref, qe0_ref[...]), pick(ke_ref, ke0_ref[...]), qk,
            ea, pick(z_ref, p0_ref[:, QKV:QKV + WIDTH]))


def gd_scan_fwd(p, p0, local, lead, nw):
    b, seq, _ = p.shape
    nc = seq // CHUNK + 1
    slab, per_chunk, state, const = _scan_specs(b, nc, False)

    def body(u_ref, w_ref, qe_ref, ke_ref, qk_ref, ea_ref, z_ref, u0_ref, w0_ref, qe0_ref, ke0_ref, qk0_ref, ea0_ref, p0_ref,
             nw_ref, y_ref, ss_ref, st):
        c = pl.program_id(0)

        @pl.when(c == 0)
        def _():
            st[...] = jnp.zeros_like(st)

        s_in = st[...]
        _save_states(ss_ref, s_in, b)
        args = _gd_scan_inputs(c, b, u_ref, w_ref, qe_ref, ke_ref, qk_ref, ea_ref, z_ref, u0_ref, w0_ref, qe0_ref, ke0_ref,
                               qk0_ref, ea0_ref, p0_ref)
        y, s_new = gd_scan(*args, nw_ref[...], s_in)
        _store_slabs(y_ref, y, b)
        st[...] = s_new

    return dict(
        body=body, args=(*local, p, *lead, p0, nw),
        in_specs=[slab(0)] * 4 + [per_chunk(HEADS * CHUNK, CHUNK), per_chunk(1, AB_PAD), slab(3)] + [const(a) for a in lead]
        + [const(p0), const(nw)],
        out_specs=[slab(0), state],
        out_shape=[_sds((b, seq, WIDTH), MXU_DTYPE), _sds((b, nc, WIDTH, DH))],
        scratch_shapes=[pltpu.VMEM((b * HEADS, DH, DH), F32)])


def gd_scan_bwd(p, p0, local, lead, nw, ssave, dy):
    b, seq, _ = p.shape
    nc = seq // CHUNK + 1
    slab, per_chunk, state, const = _scan_specs(b, nc, True)

    def body(u_ref, w_ref, qe_ref, ke_ref, qk_ref, ea_ref, z_ref, u0_ref, w0_ref, qe0_ref, ke0_ref, qk0_ref, ea0_ref, p0_ref,
             nw_ref, ss_ref, dy_ref, du_ref, dw_ref, dqe_ref, dke_ref, dqk_ref, dea_ref, dz_ref, du0_ref, dw0_ref, dqe0_ref,
             dke0_ref, dqk0_ref, dea0_ref, dz0_ref, dnw_ref, dst):
        i = pl.program_id(0)
        c = nc - 1 - i

        @pl.when(i == 0)
        def _():
            dst[...] = jnp.zeros_like(dst)
            dnw_ref[...] = jnp.zeros_like(dnw_ref)

        args = _gd_scan_inputs(c, b, u_ref, w_ref, qe_ref, ke_ref, qk_ref, ea_ref, z_ref, u0_ref, w0_ref, qe0_ref, ke0_ref,
                               qk0_ref, ea0_ref, p0_ref)
        s_in = _load_states(ss_ref, b)
        _, vjp = jax.vjp(gd_scan, *args, nw_ref[...], s_in)
        dyv = jnp.where(c == 0, 0.0, _load_slabs(dy_ref, b))
        du, dw, dqe, dke, dqk, dea, dz, dnw, ds = vjp((dyv, dst[...]))
        dst[...] = ds
        dnw_ref[...] += dnw
        lane = lax.broadcasted_iota(jnp.int32, (1, AB_PAD), 1)
        dea_rows = [sum(jnp.where(lane == h, dea[j * HEADS + h], 0.0) for h in range(HEADS)) for j in range(b)]
        dqk_rows = [jnp.concatenate([dqk[j * HEADS + h] for h in range(HEADS)], axis=0) for j in range(b)]

        @pl.when(c > 0)
        def _():
            for ref, val in ((du_ref, du), (dw_ref, dw), (dqe_ref, dqe), (dke_ref, dke), (dz_ref, dz)):
                _store_slabs(ref, val, b)
            for j in range(b):
                dqk_ref[j] = dqk_rows[j]
                dea_ref[j] = dea_rows[j]

        @pl.when(c == 0)
        def _():
            for ref, val in ((du0_ref, du), (dw0_ref, dw), (dqe0_ref, dqe), (dke0_ref, dke), (dz0_ref, dz)):
                ref[...] = _sum_rows(val, b)
            dqk0_ref[...] = sum(dqk_rows[1:], dqk_rows[0])
            dea0_ref[...] = sum(dea_rows[1:], dea_rows[0])

    uu, ww, qe, ke, qk, ea = local
    return dict(
        body=body, args=(*local, p, *lead, p0, nw, ssave, dy),
        in_specs=[slab(0)] * 4 + [per_chunk(HEADS * CHUNK, CHUNK), per_chunk(1, AB_PAD), slab(3)] + [const(a) for a in lead]
        + [const(p0), const(nw), state, slab(0)],
        out_specs=[slab(0)] * 4 + [per_chunk(HEADS * CHUNK, CHUNK), per_chunk(1, AB_PAD), slab(0)] + [const(a) for a in lead]
        + [const(lead[0]), const(nw)],
        out_shape=[_sds((b, seq, WIDTH))] * 4 + [_sds(qk.shape), _sds(ea.shape), _sds((b, seq, WIDTH))]
        + [_sds(a.shape) for a in lead] + [_sds(lead[0].shape), _sds(nw.shape)],
        scratch_shapes=[pltpu.VMEM((b * HEADS, DH, DH), F32)])


def gd_local_bwd(p, p0, ab, cw, alog, dtb, cot, dz):
    b, seq, _ = p.shape
    rows = LOCAL_CHUNKS * CHUNK
    ng = seq // rows
    du, dw, dqe, dke, dqk, dea = cot

    def body(p_ref, halo_ref, p0_ref, ab_ref, cw_ref, al_ref, dt_ref, du_ref, dw_ref, dqe_ref, dke_ref, dqk_ref, dea_ref, dz_ref,
             dp_ref, dab_ref, dhalo0_ref, dcw_ref, dal_ref, ddt_ref, dhalo):
        i = pl.program_id(1)
        g = ng - 1 - i

        @pl.when(i == 0)
        def _():
            dhalo[...] = jnp.zeros_like(dhalo)

        @pl.when((pl.program_id(0) == 0) & (i == 0))
        def _():
            dcw_ref[...] = jnp.zeros_like(dcw_ref)
            dal_ref[...] = jnp.zeros_like(dal_ref)
            ddt_ref[...] = jnp.zeros_like(ddt_ref)

        _, vjp = jax.vjp(gd_local, _gd_window(g, p_ref, halo_ref, p0_ref), ab_ref[...], cw_ref[...], al_ref[...], dt_ref[...])
        dqk_all = jnp.concatenate([dqk_ref[c] for c in range(LOCAL_CHUNKS)], axis=0)
        deas = tuple(dea_ref[c] for c in range(LOCAL_CHUNKS))
        dxx, dab, dcw, dal, ddt = vjp((du_ref[...], dw_ref[...], dqe_ref[...], dke_ref[...], dqk_all, deas))
        dqkv = dxx[HALO:HALO + rows] + jnp.concatenate([jnp.zeros((rows - HALO, QKV), F32), dhalo[...]], axis=0)
        dhalo[...] = dxx[0:HALO]
        dhalo0_ref[...] = dxx[0:HALO]
        dp_ref[...] = jnp.concatenate([dqkv, dz_ref[...]], axis=1).astype(MXU_DTYPE)
        dab_ref[...] = dab.astype(MXU_DTYPE)
        dcw_ref[...] += dcw
        dal_ref[...] += dal
        ddt_ref[...] += ddt

    rg = lambda i: ng - 1 - i
    const = lambda a: pl.BlockSpec(a.shape, lambda s, i: (0, 0))
    slab = pl.BlockSpec((None, rows, WIDTH), lambda s, i: (s, rg(i), 0))
    wide = pl.BlockSpec((None, rows, 4 * WIDTH), lambda s, i: (s, rg(i), 0))
    gates = pl.BlockSpec((None, rows, AB_PAD), lambda s, i: (s, rg(i), 0))
    return pl.pallas_call(
        body, grid=(b, ng), name="gdn_local_bwd",
        in_specs=[wide, pl.BlockSpec((None, HALO, QKV), lambda s, i: (s, _halo_block(rg(i)), 0)), const(p0), gates, const(cw),
                  const(alog), const(dtb), slab, slab, slab, slab,
                  pl.BlockSpec((None, LOCAL_CHUNKS, HEADS * CHUNK, CHUNK), lambda s, i: (s, rg(i), 0, 0)),
                  pl.BlockSpec((None, LOCAL_CHUNKS, 1, AB_PAD), lambda s, i: (s, rg(i), 0, 0)), slab],
        out_specs=[wide, gates, pl.BlockSpec((None, HALO, QKV), lambda s, i: (s, 0, 0)), const(cw), const(alog), const(dtb)],
        out_shape=[_sds(p.shape, MXU_DTYPE), _sds(ab.shape, MXU_DTYPE), _sds((b, HALO, QKV)), _sds(cw.shape), _sds(alog.shape),
                   _sds(dtb.shape)],
        scratch_shapes=[pltpu.VMEM((HALO, QKV), F32)],
        compiler_params=_cparams("arbitrary", "arbitrary"),
    )(p, p, p0, ab, cw, alog, dtb, du, dw, dqe, dke, dqk, dea, dz)


def gd_local_bwd_lead(p0, ab0, cw, alog, dtb, cot, dz, dtail):
    def body(p0_ref, ab_ref, cw_ref, al_ref, dt_ref, du_ref, dw_ref, dqe_ref, dke_ref, dqk_ref, dea_ref, dz_ref, dtail_ref,
             dp_ref, dab_ref, dcw_ref, dal_ref, ddt_ref):
        _, vjp = jax.vjp(gd_local, _lead_window(p0_ref), ab_ref[...], cw_ref[...], al_ref[...], dt_ref[...])
        dxx, dab, dcw, dal, ddt = vjp((du_ref[...], dw_ref[...], dqe_ref[...], dke_ref[...], dqk_ref[...], (dea_ref[...],)))
        dqkv = dxx[HALO:HALO + CHUNK] + jnp.concatenate([jnp.zeros((CHUNK - HALO, QKV), F32), dtail_ref[...]], axis=0)
        dp_ref[...] = jnp.concatenate([dqkv, dz_ref[...]], axis=1).astype(MXU_DTYPE)
        dab_ref[...], dcw_ref[...], dal_ref[...], ddt_ref[...] = dab.astype(MXU_DTYPE), dcw, dal, ddt

    return pl.pallas_call(
        body, name="gdn_local_bwd_lead", in_specs=[VMEM_SPEC] * 13, out_specs=[VMEM_SPEC] * 5,
        out_shape=[_sds(p0.shape, MXU_DTYPE), _sds(ab0.shape, MXU_DTYPE), _sds(cw.shape), _sds(alog.shape), _sds(dtb.shape)],
        compiler_params=pltpu.CompilerParams(vmem_limit_bytes=VMEM_LIMIT),
    )(p0, ab0, cw, alog, dtb, *cot, dz, dtail)


def _position():
    return lax.axis_index("x"), lax.axis_index("y"), lax.axis_index("c")


def _exchange_blocks(bufs, send_sems, recv_sems):
    x, y, c = _position()
    me, sibling = (x, y, c), (x, y, 1 - c)
    chips = [(1 - x, y), (x, 1 - y), (1 - x, 1 - y)]
    per_buf = N_DEV - 1

    def copy(a, k, blk, to):
        rows = bufs[a].at[4 * blk[0] + 2 * blk[1] + blk[2]]
        return pltpu.make_async_remote_copy(src_ref=rows, dst_ref=rows, send_sem=send_sems.at[a * per_buf + k],
                                            recv_sem=recv_sems.at[a * per_buf + k], device_id=to, device_id_type=MESH)

    bufs_idx = range(len(bufs))
    first = [copy(a, 0, me, sibling) for a in bufs_idx] + [copy(a, 1 + j, me, (*chip, c)) for a in bufs_idx
                                                           for j, chip in enumerate(chips)]
    for cp in first:
        cp.start()
    passed = []
    for j, chip in enumerate(chips):
        for a in bufs_idx:
            copy(a, 1 + j, (*chip, c), me).wait_recv()
            passed.append(copy(a, 4 + j, (*chip, c), sibling))
            passed[-1].start()
    for a in bufs_idx:
        copy(a, 0, sibling, me).wait_recv()
        for j, chip in enumerate(chips):
            copy(a, 4 + j, (*chip, 1 - c), me).wait_recv()
    for cp in first + passed:
        cp.wait_send()


def _exchange_sems(n_bufs):
    return [pltpu.SemaphoreType.DMA((n_bufs * (N_DEV - 1),)), pltpu.SemaphoreType.DMA((n_bufs * (N_DEV - 1),))]


def gather_weights(w_in_t, w_out, small, pad_rows):
    rows, _, cols = w_in_t.shape

    def body(wi_ref, wo_ref, sm_ref, wi_out, wo_out, sm_out, wi_buf, send_sems, recv_sems):
        x, y, c = _position()
        me = 4 * x + 2 * y + c
        wi_buf[me] = wi_ref[:, 0, :].astype(MXU_DTYPE)
        wo_out[me] = wo_ref[...].astype(MXU_DTYPE)
        sm_out[me] = sm_ref[...]
        _exchange_blocks([wi_buf, wo_out, sm_out], send_sems, recv_sems)
        for d in range(N_DEV):
            wi_out[pl.ds(d * rows, rows), :] = wi_buf[d]
        wi_out[pl.ds(N_DEV * rows, pad_rows), :] = jnp.zeros((pad_rows, cols), MXU_DTYPE)

    return pl.pallas_call(
        body, name="gather_weights", in_specs=[VMEM_SPEC] * 3, out_specs=[VMEM_SPEC] * 3,
        out_shape=[jax.ShapeDtypeStruct((N_DEV * rows + pad_rows, cols), MXU_DTYPE),
                   jax.ShapeDtypeStruct((N_DEV,) + w_out.shape, MXU_DTYPE), jax.ShapeDtypeStruct((N_DEV,) + small.shape, F32)],
        scratch_shapes=[pltpu.VMEM((N_DEV, rows, cols), MXU_DTYPE)] + _exchange_sems(3),
        compiler_params=pltpu.CompilerParams(vmem_limit_bytes=VMEM_LIMIT))(w_in_t, w_out, small)


def reduce_gradients(tensors, small, name):
    n_t = len(tensors)
    arrays = [a for parts, _ in tensors for a, _ in parts]
    first_array = [sum(len(parts) for parts, _ in tensors[:t]) for t in range(n_t)]

    def pieces(t, j):
        parts, block_rows = tensors[t]
        out, base = [], 0
        for pi, (_, valid) in enumerate(parts):
            lo, hi = max(j * block_rows, base), min((j + 1) * block_rows, base + valid)
            if lo < hi:
                out.append((first_array[t] + pi, lo - base, lo - j * block_rows, hi - lo))
            base += valid
        return out

    def body(*refs):
        n_a = len(arrays)
        in_refs, small_ref = refs[:n_a], refs[n_a]
        out_refs, small_sum = refs[n_a + 1:n_a + 1 + n_t], refs[n_a + 1 + n_t]
        bufs, small_buf = refs[n_a + 2 + n_t:n_a + 2 + 5 * n_t], refs[n_a + 2 + 5 * n_t]
        s1_sems, r1_sems, s2_sems, r2_sems, small_send, small_recv = refs[n_a + 3 + 5 * n_t:]
        x, y, c = _position()
        chip = 2 * x + y

        def put(t, dst, j, add=None):
            for ai, src_row, dst_row, size in pieces(t, j):
                v = in_refs[ai][pl.ds(src_row, size), :]
                if add is not None:
                    v = v + add[pl.ds(dst_row, size), :].astype(F32)
                dst[pl.ds(dst_row, size), :] = v.astype(dst.dtype)

        def swap(t, k):
            send1, recv1 = bufs[4 * t], bufs[4 * t + 1]
            return pltpu.make_async_remote_copy(src_ref=send1.at[k], dst_ref=recv1.at[k], send_sem=s1_sems.at[4 * t + k],
                                                recv_sem=r1_sems.at[4 * t + k], device_id=(x, y, 1 - c), device_id_type=MESH)

        def to_chip(t, k, slot):
            send2, recv2 = bufs[4 * t + 2], bufs[4 * t + 3]
            return pltpu.make_async_remote_copy(src_ref=send2.at[k], dst_ref=recv2.at[slot], send_sem=s2_sems.at[4 * t + k],
                                                recv_sem=r2_sems.at[4 * t + slot], device_id=(k >> 1, k & 1, c),
                                                device_id_type=MESH)

        for t in range(n_t):
            for j in range(N_DEV):
                @pl.when((j & 1) != c)
                def _():
                    put(t, bufs[4 * t].at[j >> 1], j)
            for k in range(4):
                swap(t, k).start()

        small_buf[4 * x + 2 * y + c] = small_ref[...]
        _exchange_blocks([small_buf], small_send, small_recv)
        total = small_buf[0]
        for d in range(1, N_DEV):
            total = total + small_buf[d]
        small_sum[...] = total

        for t in range(n_t):
            recv1 = bufs[4 * t + 1]
            for k in range(4):
                swap(t, k).wait_recv()
                for j in (2 * k, 2 * k + 1):
                    @pl.when(((j & 1) == c) & (k != chip))
                    def _():
                        put(t, bufs[4 * t + 2].at[k], j, add=recv1.at[k])
                        to_chip(t, k, chip).start()

                    @pl.when(((j & 1) == c) & (k == chip))
                    def _():
                        put(t, out_refs[t], j, add=recv1.at[k])

        for t in range(n_t):
            for k in range(4):
                @pl.when(k != chip)
                def _():
                    to_chip(t, k, k).wait_recv()
                    out_refs[t][...] += bufs[4 * t + 3][k].astype(F32)

        for t in range(n_t):
            for k in range(4):
                @pl.when(k != chip)
                def _():
                    to_chip(t, k, chip).wait_send()
                swap(t, k).wait_send()

    scratch, out_shape = [], []
    for parts, block_rows in tensors:
        cols = parts[0][0].shape[1]
        scratch += [pltpu.VMEM((4, block_rows, cols), MXU_DTYPE)] * 4
        out_shape.append(jax.ShapeDtypeStruct((block_rows, cols), F32))
    out_shape.append(jax.ShapeDtypeStruct(small.shape, F32))
    scratch += [pltpu.VMEM((N_DEV,) + small.shape, F32)] + [pltpu.SemaphoreType.DMA((4 * n_t,))] * 4 + _exchange_sems(1)
    return pl.pallas_call(
        body, name=name, in_specs=[VMEM_SPEC] * (len(arrays) + 1), out_specs=[VMEM_SPEC] * (n_t + 1), out_shape=out_shape,
        scratch_shapes=scratch, compiler_params=pltpu.CompilerParams(vmem_limit_bytes=VMEM_LIMIT),
    )(*arrays, small)


def _adamw_step(w, g, m, v):
    mn = ADAM_B1 * m + (1.0 - ADAM_B1) * g
    vn = ADAM_B2 * v + (1.0 - ADAM_B2) * jnp.square(g)
    m_hat = mn / (1.0 - ADAM_B1 ** ADAM_STEP)
    v_hat = vn / (1.0 - ADAM_B2 ** ADAM_STEP)
    return -ADAM_LR * (m_hat / (jnp.sqrt(v_hat) + ADAM_EPS) + ADAM_WD * w), mn, vn


def adamw(w, g, m, v, name):
    rows, cols = w.shape
    tr = 256 if rows % 256 == 0 else rows

    def body(w_ref, g_ref, m_ref, v_ref, d_ref, nm_ref, nv_ref):
        d_ref[...], nm_ref[...], nv_ref[...] = _adamw_step(w_ref[...], g_ref[...], m_ref[...], v_ref[...])

    spec = pl.BlockSpec((tr, cols), lambda i: (i, 0))
    shape = jax.ShapeDtypeStruct((rows, cols), F32)
    return pl.pallas_call(body, grid=(rows // tr,), name=name, in_specs=[spec] * 4, out_specs=[spec] * 3,
                          out_shape=[shape] * 3, compiler_params=_cparams("arbitrary"))(w, g, m, v)


def adamw_w_in(w, g_t, m, v):
    def body(w_ref, g_ref, m_ref, v_ref, go_ref, d_ref, nm_ref, nv_ref):
        g = g_ref[...]
        go_ref[:, 0, :] = g
        d_ref[:, 0, :], nm_ref[:, 0, :], nv_ref[:, 0, :] = _adamw_step(w_ref[:, 0, :], g, m_ref[:, 0, :], v_ref[:, 0, :])

    return pl.pallas_call(body, name="adamw_w_in", in_specs=[VMEM_SPEC] * 4, out_specs=[VMEM_SPEC] * 4,
                          out_shape=[jax.ShapeDtypeStruct(w.shape, F32)] * 4,
                          compiler_params=pltpu.CompilerParams(vmem_limit_bytes=VMEM_LIMIT))(w, g_t, m, v)


def _pad_rows(a, rows=8):
    return jnp.pad(a, ((0, rows - a.shape[0]), (0, 0)))


def _pad_lanes(a, lanes=128):
    return jnp.pad(a, ((0, 0), (0, lanes - a.shape[1])))


def kernel(x, meta_tokens, norm_w, w_in, conv_w, hg_lb_logits, hg_norm_w, gdn_A_log, gdn_dt_bias, gdn_norm_w, w_out, final_norm_w, loss_target, m_meta_tokens, m_norm_w, m_w_in, m_conv_w, m_hg_lb_logits, m_hg_norm_w, m_gdn_A_log, m_gdn_dt_bias, m_gdn_norm_w, m_w_out, m_final_norm_w, v_meta_tokens, v_norm_w, v_w_in, v_conv_w, v_hg_lb_logits, v_hg_norm_w, v_gdn_A_log, v_gdn_dt_bias, v_gdn_norm_w, v_w_out, v_final_norm_w):
    b, seq, _ = x.shape
    n = b * seq
    dev = 4 * lax.axis_index("x") + 2 * lax.axis_index("y") + lax.axis_index("c")
    col_shard = IN_COLS // N_DEV

    small_w = jnp.concatenate([_pad_lanes(meta_tokens, 256), _pad_rows(_pad_lanes(conv_w[0], 256))], axis=0)
    w_t, w_out_g, small_g = gather_weights(jnp.transpose(w_in, (2, 0, 1)), w_out[0], small_w, AB_PAD - 2 * HEADS)
    meta_g = small_g[:, 0:N_META, 0:D_MODEL // N_DEV]
    conv_g = small_g[:, N_META:N_META + CONV_TAPS, 0:QKV // N_DEV]
    w_out_full = w_out_g.reshape(2 * WIDTH, D_MODEL)
    cw = jnp.transpose(conv_g, (1, 0, 2)).reshape(CONV_TAPS, QKV)
    meta = jnp.transpose(meta_g, (1, 0, 2)).reshape(N_META, D_MODEL)
    alog = _pad_lanes(gdn_A_log)
    dtb = _pad_lanes(gdn_dt_bias)
    fw = final_norm_w.reshape(1, D_MODEL)

    h0 = jnp.concatenate([jnp.zeros((CHUNK - N_META, D_MODEL), F32), meta], axis=0)
    x2 = x.reshape(n, D_MODEL)
    u0, phg0, pgd0, pab0 = in_proj(h0, norm_w, w_t, "in_proj_lead")
    u, phg, pgd, pab = in_proj(x2, norm_w, w_t, "in_proj")
    phg3, pgd3, pab3 = phg.reshape(b, seq, 4 * WIDTH), pgd.reshape(b, seq, 4 * WIDTH), pab.reshape(b, seq, AB_PAD)
    nc = seq // CHUNK + 1
    hg_loc = hg_local_fwd(phg3, hg_lb_logits)
    hg_lead = hg_local_lead(phg0, hg_lb_logits)
    gd_loc = gd_local_fwd(pgd3, pgd0, pab3, cw, alog, dtb)
    gd_lead = gd_local_lead(pgd0, pab0, cw, alog, dtb)
    (y_hg, s_hg), (y_gd, s_gd) = run_scans([hg_scan_fwd(phg3, phg0, hg_loc, hg_lead, hg_norm_w),
                                            gd_scan_fwd(pgd3, pgd0, gd_loc, gd_lead, gdn_norm_w)], nc, "scans")

    dh2, dy_hg, dy_gd, g_w_out, loss_part, g_fw = out_proj_loss(
        x2, loss_target.reshape(n, D_MODEL), y_hg.reshape(n, WIDTH), y_gd.reshape(n, WIDTH), w_out_full, fw)

    hb, gb = run_scans([hg_scan_bwd(phg3, phg0, hg_loc, hg_lead, hg_norm_w, s_hg, dy_hg.reshape(b, seq, WIDTH)),
                        gd_scan_bwd(pgd3, pgd0, gd_loc, gd_lead, gdn_norm_w, s_gd, dy_gd.reshape(b, seq, WIDTH))],
                       nc, "scans_bwd")
    dphg, g_lb = hg_local_bwd(phg3, hg_lb_logits, *hb[0:6])
    dphg0, g_lb0 = hg_local_bwd_lead(phg0, hg_lb_logits, *hb[6:12])
    g_hg_nw = hb[12]
    dpgd, dpab, dtail, g_cw, g_alog, g_dtb = gd_local_bwd(pgd3, pgd0, pab3, cw, alog, dtb, gb[0:6], gb[6])
    dpgd0, dpab0, g_cw0, g_alog0, g_dtb0 = gd_local_bwd_lead(pgd0, pab0, cw, alog, dtb, gb[7:13], gb[13], dtail.sum(0))
    g_gd_nw = gb[14]
    dphg, dpgd, dpab = dphg.reshape(n, 4 * WIDTH), dpgd.reshape(n, 4 * WIDTH), dpab.reshape(n, AB_PAD)

    grad_x, g_nw = in_proj_bwd(dphg, dpgd, dpab, w_t, x2, dh2, norm_w, "in_proj_bwd")
    dh0, g_nw0 = in_proj_bwd(dphg0, dpgd0, dpab0, w_t, h0, jnp.zeros_like(h0), norm_w, "in_proj_bwd_lead")
    g_w_hg = weight_grad(u, dphg, u0, dphg0, "w_in_grad_hg")
    g_w_gd, g_w_ab = weight_grad(u, dpgd, u0, dpgd0, "w_in_grad_gd", gates=(dpab, dpab0))

    small = jnp.concatenate([
        (g_nw + g_nw0).reshape(8, 128), (g_lb + g_lb0).reshape(8, 128), _pad_rows(g_hg_nw), _pad_rows(g_alog + g_alog0),
        _pad_rows(g_dtb + g_dtb0), _pad_rows(g_gd_nw), g_fw.reshape(8, 128), (g_cw + g_cw0).reshape(48, 128),
        dh0[CHUNK - N_META:CHUNK].reshape(128, 128), loss_part], axis=0)
    g_w_in_t, g_w_out, small = reduce_gradients(
        [([(g_w_hg, 4 * WIDTH), (g_w_gd, 4 * WIDTH), (g_w_ab, 2 * HEADS)], col_shard),
         ([(g_w_out, 2 * WIDTH)], (2 * WIDTH) // N_DEV)], small, "reduce_gradients")
    g_norm_w = small[0:8].reshape(1, D_MODEL)
    g_lb = small[8:16].reshape(2, WIDTH)
    g_hg_nw = small[16:17]
    g_alog = small[24:25, 0:HEADS]
    g_dtb = small[32:33, 0:HEADS]
    g_gd_nw = small[40:41]
    g_fw = small[48:56].reshape(1, D_MODEL)
    g_cw_full = small[56:104].reshape(CONV_TAPS, QKV)
    g_meta_full = small[104:232].reshape(N_META, D_MODEL)
    loss = small[232, 0]
    g_conv = lax.dynamic_slice_in_dim(g_cw_full, dev * (QKV // N_DEV), QKV // N_DEV, axis=1)
    g_meta = lax.dynamic_slice_in_dim(g_meta_full, dev * (D_MODEL // N_DEV), D_MODEL // N_DEV, axis=1)

    names = ["meta_tokens", "norm_w", "w_in", "conv_w", "hg_lb_logits", "hg_norm_w", "gdn_A_log", "gdn_dt_bias",
             "gdn_norm_w", "w_out", "final_norm_w"]
    weights = [meta_tokens, norm_w, w_in, conv_w, hg_lb_logits, hg_norm_w, gdn_A_log, gdn_dt_bias, gdn_norm_w, w_out,
               final_norm_w]
    moms = [m_meta_tokens, m_norm_w, m_w_in, m_conv_w, m_hg_lb_logits, m_hg_norm_w, m_gdn_A_log, m_gdn_dt_bias,
            m_gdn_norm_w, m_w_out, m_final_norm_w]
    vars_ = [v_meta_tokens, v_norm_w, v_w_in, v_conv_w, v_hg_lb_logits, v_hg_norm_w, v_gdn_A_log, v_gdn_dt_bias,
             v_gdn_norm_w, v_w_out, v_final_norm_w]
    grads2d = [g_meta, g_norm_w, g_w_in_t, g_conv, g_lb, g_hg_nw, g_alog, g_dtb, g_gd_nw, g_w_out, g_fw]
    grads, deltas, new_ms, new_vs = [], [], [], []
    for nm, w, g2, m, v in zip(names, weights, grads2d, moms, vars_):
        if nm == "w_in":
            to3, back = (lambda a: jnp.transpose(a, (2, 0, 1))), (lambda a: jnp.transpose(a, (1, 2, 0)))
            g2, d, nm_, nv_ = adamw_w_in(to3(w), g2, to3(m), to3(v))
        else:
            to2d, back = (lambda a, s=g2.shape: a.reshape(s)), (lambda a, s=w.shape: a.reshape(s))
            d, nm_, nv_ = adamw(to2d(w), g2, to2d(m), to2d(v), "adamw_" + nm)
        grads.append(back(g2))
        deltas.append(back(d))
        new_ms.append(back(nm_))
        new_vs.append(back(nv_))
    return (loss, grad_x.reshape(x.shape), *grads, *deltas, *new_ms, *new_vs)
```

```python
import jax
import jax.numpy as jnp
from jax import lax
from jax.experimental import pallas as pl
from jax.experimental.pallas import tpu as pltpu

F32 = jnp.float32
BF16 = jnp.bfloat16
MXU_DTYPE = BF16

D_MODEL = 1024
N_META = 16
CHUNK = 64
SUB = 16
HEADS = 4
DH = 128
WIDTH = HEADS * DH
QKV = 3 * WIDTH
CONV_TAPS = 4
HALO = 8
EPS = 1e-6
IN_COLS = 4 * WIDTH + 4 * WIDTH + 2 * HEADS
AB_PAD = 128
N_DEV = 8
LOCAL_CHUNKS = 2
VMEM_LIMIT = 56 * 1024 * 1024

ADAM_LR = 0.001
ADAM_B1 = 0.9
ADAM_B2 = 0.999
ADAM_EPS = 1e-08
ADAM_WD = 0.01
ADAM_STEP = 10

VMEM_SPEC = pl.BlockSpec(memory_space=pltpu.VMEM)
MESH = pl.DeviceIdType.MESH


def _mm_tn(a, b):
    return lax.dot_general(a.astype(MXU_DTYPE), b.astype(MXU_DTYPE), (((0,), (0,)), ((), ())), preferred_element_type=F32)


def _bmm(a, b):
    return lax.dot_general(a.astype(MXU_DTYPE), b.astype(MXU_DTYPE), (((2,), (1,)), ((0,), (0,))), preferred_element_type=F32)


def _bmm_nt(a, b):
    return lax.dot_general(a.astype(MXU_DTYPE), b.astype(MXU_DTYPE), (((2,), (2,)), ((0,), (0,))), preferred_element_type=F32)


def _bmm_tn(a, b):
    return lax.dot_general(a.astype(MXU_DTYPE), b.astype(MXU_DTYPE), (((1,), (1,)), ((0,), (0,))), preferred_element_type=F32)


def _iota2(n, m):
    return lax.broadcasted_iota(jnp.int32, (n, m), 0), lax.broadcasted_iota(jnp.int32, (n, m), 1)


def _silu(x):
    return x * jax.nn.sigmoid(x)


def _gated_norm(o, z, nw):
    return o * lax.rsqrt(jnp.mean(o * o, axis=-1, keepdims=True) + EPS) * nw * _silu(z)


def _heads(a, nb):
    return jnp.stack([a[c * CHUNK:(c + 1) * CHUNK, h * DH:(h + 1) * DH] for c in range(nb) for h in range(HEADS)], axis=0)


def _unheads(a3, nb):
    return jnp.concatenate(
        [jnp.concatenate([a3[c * HEADS + h] for h in range(HEADS)], axis=1) for c in range(nb)], axis=0)


def _split3(x):
    hi = x.astype(BF16)
    r1 = x - hi.astype(F32)
    mid = r1.astype(BF16)
    return hi, mid, (r1 - mid.astype(F32)).astype(BF16)


def _select_mm(pattern, n_out, n_in, transposed, x):
    rows, inner = (n_in, n_out) if transposed else (n_out, n_in)
    r, c = _iota2(rows, 3 * inner)
    c = c - jnp.where(c >= inner, inner, 0) - jnp.where(c >= 2 * inner, inner, 0)
    s = jnp.where(pattern(c, r) if transposed else pattern(r, c), 1.0, 0.0).astype(BF16)
    return jnp.dot(s, jnp.concatenate(_split3(x), axis=0), preferred_element_type=F32)


def _select_rows(pattern, n_out, x):
    @jax.custom_vjp
    def apply(v):
        return _select_mm(pattern, n_out, CHUNK, False, v)

    apply.defvjp(lambda v: (_select_mm(pattern, n_out, CHUNK, False, v), None),
                 lambda _, d: (_select_mm(pattern, n_out, CHUNK, True, d),))
    return apply(x)


def _cumsum_chunks(x, nb):
    return jnp.concatenate([_select_rows(lambda i, j: j <= i, CHUNK, x[c * CHUNK:(c + 1) * CHUNK]) for c in range(nb)], axis=0)


HG_LEVELS = 6


def _hg_sums(i, j):
    lvl, t = i >> HG_LEVELS, i & (CHUNK - 1)
    last = t
    for l in range(1, HG_LEVELS + 1):
        width = HG_LEVELS + 1 - l
        last = jnp.where(lvl == l, ((t >> width) << width) + (CHUNK >> l) - 1, last)
    return j <= last


def hg_local(p, logits):
    nb = p.shape[0] // CHUNK
    l0, l1 = logits[0:1], logits[1:2]
    mx = jnp.maximum(l0, l1)
    e0, e1 = jnp.exp(l0 - mx), jnp.exp(l1 - mx)
    lb = e0 / (e0 + e1)
    q = _silu(p[:, 0:WIDTH])
    f = lb + (1.0 - lb) * jax.nn.sigmoid(p[:, WIDTH:2 * WIDTH])
    k = 1.0 - f
    logf = jnp.log(f)
    sums = [_select_rows(_hg_sums, (HG_LEVELS + 1) * CHUNK, logf[c * CHUNK:(c + 1) * CHUNK]) for c in range(nb)]
    level = lambda l: _heads(jnp.concatenate([s[l * CHUNK:(l + 1) * CHUNK] for s in sums], axis=0), nb)
    q3, k3, v3, g3 = _heads(q, nb), _heads(k, nb), _heads(p[:, 2 * WIDTH:3 * WIDTH], nb), level(0)
    r, c = _iota2(CHUNK, CHUNK)
    row = lax.broadcasted_iota(jnp.int32, (CHUNK, DH), 0)
    a = jnp.where(r == c, _bmm_nt(q3, k3), 0.0)
    for l in range(1, HG_LEVELS + 1):
        sh = HG_LEVELS - l
        qk = jnp.where(((row >> sh) & 1) == 1, q3, k3) * jnp.exp(-jnp.abs(g3 - level(l)))
        pair = ((r >> (sh + 1)) == (c >> (sh + 1))) & (((r >> sh) & 1) == 1) & (((c >> sh) & 1) == 0)
        a = a + jnp.where(pair, _bmm_nt(qk, qk), 0.0)
    o = _bmm(a, v3)
    glast = g3[:, CHUNK - 1:CHUNK, :]
    egs = tuple(jnp.concatenate([jnp.exp(glast[c * HEADS + h]) for h in range(HEADS)], axis=1) for c in range(nb))
    return _unheads(q3 * jnp.exp(g3), nb), _unheads(k3 * jnp.exp(glast - g3), nb), _unheads(o, nb), egs


def hg_scan(q_in, k_out, v, eg, o_intra, z, nw, st):
    o = o_intra + _bmm_nt(q_in, st)
    return _gated_norm(o, z, nw), st * eg + _bmm_tn(v, k_out)


def _tri_y_impl(a):
    r, c = _iota2(CHUNK, CHUNK)
    same16 = (r // SUB) == (c // SUB)
    same32 = (r // (2 * SUB)) == (c // (2 * SUB))
    a0 = jnp.where(same16, a, 0.0)
    y = -a0
    pw = _bmm(a0, a0)
    for _ in range(2):
        y = y + pw + _bmm(y, pw)
        pw = _bmm(pw, pw)
    y = y + pw + _bmm(y, pw)
    for ak in (jnp.where(same32 & jnp.logical_not(same16), a, 0.0), jnp.where(same32, 0.0, a)):
        m = ak + _bmm(y, ak)
        y = y - (m + _bmm(m, y))
    return y


@jax.custom_vjp
def _tri_y(a):
    return _tri_y_impl(a)


def _tri_y_fwd(a):
    y = _tri_y_impl(a)
    return y, y


def _tri_y_bwd(y, dy):
    n = dy + _bmm_tn(y, dy)
    return (-(n + _bmm_nt(n, y)),)


_tri_y.defvjp(_tri_y_fwd, _tri_y_bwd)


def gd_local(xx, ab, cw, alog, dtb, inverse=_tri_y):
    n = ab.shape[0]
    nb = n // CHUNK
    conv = cw[0:1] * xx[HALO - 3:HALO - 3 + n]
    for j in range(1, CONV_TAPS):
        conv = conv + cw[j:j + 1] * xx[HALO - 3 + j:HALO - 3 + j + n]
    act = _silu(conv)
    x = ab + dtb
    g_all = -jnp.exp(alog) * (jnp.maximum(x, 0.0) + jnp.log1p(jnp.exp(-jnp.abs(x))))
    beta_all = jax.nn.sigmoid(ab)
    gam_all = _cumsum_chunks(g_all, nb)
    q3, k3, v3 = _heads(act[:, 0:WIDTH], nb), _heads(act[:, WIDTH:2 * WIDTH], nb), _heads(act[:, 2 * WIDTH:QKV], nb)
    q3 = q3 * lax.rsqrt(jnp.sum(q3 * q3, axis=-1, keepdims=True) + EPS) * (DH ** -0.5)
    k3 = k3 * lax.rsqrt(jnp.sum(k3 * k3, axis=-1, keepdims=True) + EPS)
    pairs = [(c, h) for c in range(nb) for h in range(HEADS)]
    beta = jnp.stack([beta_all[c * CHUNK:(c + 1) * CHUNK, HEADS + h:HEADS + h + 1] for c, h in pairs], axis=0)
    gam = jnp.stack([gam_all[c * CHUNK:(c + 1) * CHUNK, h:h + 1] for c, h in pairs], axis=0)
    gam_t = [gam_all[c * CHUNK:(c + 1) * CHUNK].T for c in range(nb)]
    gam_row = jnp.stack([gam_t[c][h:h + 1, :] for c, h in pairs], axis=0)
    glast = gam[:, CHUNK - 1:CHUNK, :]
    r, c = _iota2(CHUNK, CHUNK)
    dec = jnp.exp(jnp.where(c < r, gam - gam_row, -jnp.inf))
    y = inverse(beta * _bmm_nt(k3, k3) * dec)
    eg = jnp.exp(gam)
    rhs = jnp.concatenate([beta * v3, (beta * eg) * k3], axis=2)
    sol = rhs + _bmm(y, rhs)
    qk = _bmm_nt(q3, k3) * jnp.where(r == c, 1.0, dec)
    eas = tuple(jnp.exp(gam_all[(c + 1) * CHUNK - 1:(c + 1) * CHUNK]) for c in range(nb))
    return (_unheads(sol[:, :, 0:DH], nb), _unheads(sol[:, :, DH:2 * DH], nb), _unheads(q3 * eg, nb),
            _unheads(k3 * jnp.exp(glast - gam), nb), jnp.concatenate([qk[g] for g in range(nb * HEADS)], axis=0), eas)


def gd_scan(uu, ww, qe, ke, qk, ea, z, nw, s):
    u = uu - _bmm(ww, s)
    o = _bmm(qe, s) + _bmm(qk, u)
    return _gated_norm(o, z, nw), ea * s + _bmm_tn(ke, u)


def _cparams(*sem):
    return pltpu.CompilerParams(dimension_semantics=sem, vmem_limit_bytes=VMEM_LIMIT)


def _row_tile(n):
    for t in (512, 256, 128, 64):
        if n % t == 0:
            return t
    raise ValueError(f"unsupported token count {n}")


def _w_in_specs():
    once = pl.Buffered(1)
    return [pl.BlockSpec((4 * WIDTH, D_MODEL), lambda *i: (0, 0), pipeline_mode=once),
            pl.BlockSpec((4 * WIDTH, D_MODEL), lambda *i: (1, 0), pipeline_mode=once),
            pl.BlockSpec((AB_PAD, D_MODEL), lambda *i: (8 * WIDTH // AB_PAD, 0), pipeline_mode=once)]


def in_proj(h, norm_w, w_t, name, keep_u):
    n = h.shape[0]
    tm = _row_tile(n)
    nt = (((1,), (1,)), ((), ()))

    def body(h_ref, nw_ref, whg_ref, wgd_ref, wab_ref, phg_ref, pgd_ref, pab_ref, *u_ref):
        x = h_ref[...]
        u = (x * lax.rsqrt(jnp.mean(x * x, axis=-1, keepdims=True) + EPS) * nw_ref[...]).astype(MXU_DTYPE)
        phg_ref[...] = lax.dot_general(u, whg_ref[...], nt, preferred_element_type=F32)
        pgd_ref[...] = lax.dot_general(u, wgd_ref[...], nt, preferred_element_type=F32)
        pab_ref[...] = lax.dot_general(u, wab_ref[...], nt, preferred_element_type=F32)
        if keep_u:
            u_ref[0][...] = u

    row = lambda w: pl.BlockSpec((tm, w), lambda i: (i, 0))
    return pl.pallas_call(
        body, grid=(n // tm,), name=name,
        in_specs=[row(D_MODEL), pl.BlockSpec(norm_w.shape, lambda i: (0, 0))] + _w_in_specs(),
        out_specs=[row(4 * WIDTH), row(4 * WIDTH), row(AB_PAD)] + [row(D_MODEL)] * keep_u,
        out_shape=[jax.ShapeDtypeStruct((n, 4 * WIDTH), F32), jax.ShapeDtypeStruct((n, 4 * WIDTH), F32),
                   jax.ShapeDtypeStruct((n, AB_PAD), F32)] + [jax.ShapeDtypeStruct((n, D_MODEL), MXU_DTYPE)] * keep_u,
        compiler_params=_cparams("arbitrary"),
    )(h, norm_w, w_t, w_t, w_t)


def out_proj_loss(x, tgt, y_hg, y_gd, w_out, fw):
    n = x.shape[0]
    tm = _row_tile(n)
    inv_d = 1.0 / D_MODEL

    def body(x_ref, t_ref, yh_ref, yg_ref, w_ref, fw_ref, dh_ref, dyh_ref, dyg_ref, dw_ref, loss_ref, dfw_ref):
        @pl.when(pl.program_id(0) == 0)
        def _():
            dw_ref[...] = jnp.zeros_like(dw_ref)
            loss_ref[...] = jnp.zeros_like(loss_ref)
            dfw_ref[...] = jnp.zeros_like(dfw_ref)

        yh, yg = yh_ref[...], yg_ref[...]
        wa, wb = w_ref[0:WIDTH, :], w_ref[WIDTH:2 * WIDTH, :]
        h2 = x_ref[...] + jnp.dot(yh, wa, preferred_element_type=F32) + jnp.dot(yg, wb, preferred_element_type=F32)
        r2 = lax.rsqrt(jnp.mean(h2 * h2, axis=-1, keepdims=True) + EPS)
        nrm = h2 * r2
        fwv = fw_ref[...]
        err = nrm * fwv - t_ref[...]
        loss_ref[...] += jnp.full(loss_ref.shape, 0.5 * inv_d * jnp.sum(err * err), F32)
        dout = err * inv_d
        dfw_ref[...] += jnp.sum(dout * nrm, axis=0, keepdims=True)
        dn = dout * fwv
        dh2 = r2 * (dn - nrm * jnp.mean(dn * nrm, axis=-1, keepdims=True))
        dh_ref[...] = dh2
        dhb = dh2.astype(MXU_DTYPE)
        dyh_ref[...] = lax.dot_general(dhb, wa, (((1,), (1,)), ((), ())), preferred_element_type=F32)
        dyg_ref[...] = lax.dot_general(dhb, wb, (((1,), (1,)), ((), ())), preferred_element_type=F32)
        dw_ref[0:WIDTH, :] += lax.dot_general(yh, dhb, (((0,), (0,)), ((), ())), preferred_element_type=F32)
        dw_ref[WIDTH:2 * WIDTH, :] += lax.dot_general(yg, dhb, (((0,), (0,)), ((), ())), preferred_element_type=F32)

    row = lambda w: pl.BlockSpec((tm, w), lambda i: (i, 0))
    full = lambda s: pl.BlockSpec(s, lambda i: (0, 0))
    return pl.pallas_call(
        body, grid=(n // tm,), name="out_proj_loss",
        in_specs=[row(D_MODEL), row(D_MODEL), row(WIDTH), row(WIDTH), full(w_out.shape), full(fw.shape)],
        out_specs=[row(D_MODEL), row(WIDTH), row(WIDTH), full((2 * WIDTH, D_MODEL)), full((8, 128)), full((1, D_MODEL))],
        out_shape=[jax.ShapeDtypeStruct((n, D_MODEL), F32), jax.ShapeDtypeStruct((n, WIDTH), F32),
                   jax.ShapeDtypeStruct((n, WIDTH), F32), jax.ShapeDtypeStruct((2 * WIDTH, D_MODEL), F32),
                   jax.ShapeDtypeStruct((8, 128), F32), jax.ShapeDtypeStruct((1, D_MODEL), F32)],
        compiler_params=_cparams("arbitrary"),
    )(x, tgt, y_hg, y_gd, w_out, fw)


def in_proj_bwd(dphg, dpgd, dpab, w_t, h, dh2, norm_w, name):
    n = h.shape[0]
    tm = _row_tile(n)

    def body(dphg_ref, dpgd_ref, dpab_ref, whg_ref, wgd_ref, wab_ref, h_ref, dh2_ref, nw_ref, dx_ref, dnw_ref):
        @pl.when(pl.program_id(0) == 0)
        def _():
            dnw_ref[...] = jnp.zeros_like(dnw_ref)

        du = jnp.dot(dphg_ref[...].astype(MXU_DTYPE), whg_ref[...], preferred_element_type=F32)
        du += jnp.dot(dpgd_ref[...].astype(MXU_DTYPE), wgd_ref[...], preferred_element_type=F32)
        du += jnp.dot(dpab_ref[...].astype(MXU_DTYPE), wab_ref[...], preferred_element_type=F32)
        x = h_ref[...]
        r = lax.rsqrt(jnp.mean(x * x, axis=-1, keepdims=True) + EPS)
        nrm = x * r
        dnw_ref[...] += jnp.sum(du * nrm, axis=0, keepdims=True)
        dn = du * nw_ref[...]
        dx_ref[...] = dh2_ref[...] + r * (dn - nrm * jnp.mean(dn * nrm, axis=-1, keepdims=True))

    row = lambda w: pl.BlockSpec((tm, w), lambda i: (i, 0))
    return pl.pallas_call(
        body, grid=(n // tm,), name=name,
        in_specs=[row(4 * WIDTH), row(4 * WIDTH), row(AB_PAD)] + _w_in_specs() + [row(D_MODEL), row(D_MODEL),
                                                                                   pl.BlockSpec(norm_w.shape, lambda i: (0, 0))],
        out_specs=[row(D_MODEL), pl.BlockSpec((1, D_MODEL), lambda i: (0, 0))],
        out_shape=[jax.ShapeDtypeStruct((n, D_MODEL), F32), jax.ShapeDtypeStruct((1, D_MODEL), F32)],
        compiler_params=_cparams("arbitrary"),
    )(dphg, dpgd, dpab, w_t, w_t, w_t, h, dh2, norm_w)


def in_proj_bwd_weights(dphg, dpgd, dpab, w_t, h, dh2, norm_w, u0, dphg0, dpgd0, dpab0):
    n = h.shape[0]
    tm = _row_tile(n)
    steps = n // tm

    def body(dphg_ref, dpgd_ref, dpab_ref, whg_ref, wgd_ref, wab_ref, h_ref, dh2_ref, nw_ref, u0_ref, d0hg_ref, d0gd_ref,
             d0ab_ref, dx_ref, dnw_ref, ghg_ref, ggd_ref, gab_ref, acc_hg, acc_gd, acc_ab):
        i = pl.program_id(0)

        @pl.when(i == 0)
        def _():
            dnw_ref[...] = jnp.zeros_like(dnw_ref)
            acc_hg[...] = _mm_tn(d0hg_ref[...], u0_ref[...])
            acc_gd[...] = _mm_tn(d0gd_ref[...], u0_ref[...])
            acc_ab[...] = _mm_tn(d0ab_ref[...], u0_ref[...])

        dphg_v, dpgd_v, dpab_v = dphg_ref[...], dpgd_ref[...], dpab_ref[...]
        du = jnp.dot(dphg_v, whg_ref[...], preferred_element_type=F32)
        du += jnp.dot(dpgd_v, wgd_ref[...], preferred_element_type=F32)
        du += jnp.dot(dpab_v, wab_ref[...], preferred_element_type=F32)
        x = h_ref[...]
        r = lax.rsqrt(jnp.mean(x * x, axis=-1, keepdims=True) + EPS)
        nrm = x * r
        nwv = nw_ref[...]
        u = (nrm * nwv).astype(MXU_DTYPE)
        acc_hg[...] += _mm_tn(dphg_v, u)
        acc_gd[...] += _mm_tn(dpgd_v, u)
        acc_ab[...] += _mm_tn(dpab_v, u)
        dnw_ref[...] += jnp.sum(du * nrm, axis=0, keepdims=True)
        dn = du * nwv
        dx_ref[...] = dh2_ref[...] + r * (dn - nrm * jnp.mean(dn * nrm, axis=-1, keepdims=True))

        @pl.when(i == steps - 1)
        def _():
            pltpu.sync_copy(acc_hg, ghg_ref)
            pltpu.sync_copy(acc_gd, ggd_ref)
            pltpu.sync_copy(acc_ab, gab_ref)

    row = lambda w: pl.BlockSpec((tm, w), lambda i: (i, 0))
    full = lambda a: pl.BlockSpec(a.shape, lambda i: (0, 0))
    anywhere = pl.BlockSpec(memory_space=pl.ANY)
    return pl.pallas_call(
        body, grid=(steps,), name="in_proj_bwd",
        in_specs=[row(4 * WIDTH), row(4 * WIDTH), row(AB_PAD)] + _w_in_specs() + [row(D_MODEL), row(D_MODEL), full(norm_w),
                                                                                   full(u0), full(dphg0), full(dpgd0), full(dpab0)],
        out_specs=[row(D_MODEL), pl.BlockSpec((1, D_MODEL), lambda i: (0, 0)), anywhere, anywhere, anywhere],
        out_shape=[jax.ShapeDtypeStruct((n, D_MODEL), F32), jax.ShapeDtypeStruct((1, D_MODEL), F32),
                   jax.ShapeDtypeStruct((4 * WIDTH, D_MODEL), F32), jax.ShapeDtypeStruct((4 * WIDTH, D_MODEL), F32),
                   jax.ShapeDtypeStruct((AB_PAD, D_MODEL), F32)],
        scratch_shapes=[pltpu.VMEM((4 * WIDTH, D_MODEL), F32), pltpu.VMEM((4 * WIDTH, D_MODEL), F32),
                        pltpu.VMEM((AB_PAD, D_MODEL), F32)],
        compiler_params=_cparams("arbitrary"),
    )(dphg, dpgd, dpab, w_t, w_t, w_t, h, dh2, norm_w, u0, dphg0, dpgd0, dpab0)


def _real(c):
    return jnp.maximum(c - 1, 0)


def _sds(shape, dtype=F32):
    return jax.ShapeDtypeStruct(shape, dtype)


def _load_slabs(ref, b):
    return jnp.stack([ref[i, :, h * DH:(h + 1) * DH].astype(F32) for i in range(b) for h in range(HEADS)], axis=0)


def _lead_slabs(a, b):
    return jnp.stack([a[:, h * DH:(h + 1) * DH].astype(F32) for _ in range(b) for h in range(HEADS)], axis=0)


def _rows(a3, i):
    return jnp.concatenate([a3[i * HEADS + h] for h in range(HEADS)], axis=1)


def _store_slabs(ref, a3, b):
    for i in range(b):
        ref[i] = _rows(a3, i).astype(ref.dtype)


def _sum_rows(a3, b):
    out = _rows(a3, 0)
    for i in range(1, b):
        out = out + _rows(a3, i)
    return out


def _save_states(ref, s, b):
    for i in range(b):
        ref[i] = jnp.concatenate([s[i * HEADS + h] for h in range(HEADS)], axis=0)


def _load_states(ref, b):
    return jnp.stack([ref[i, h * DH:(h + 1) * DH, :] for i in range(b) for h in range(HEADS)], axis=0)


def hg_local_fwd(p, logits):
    b, seq, _ = p.shape
    rows = LOCAL_CHUNKS * CHUNK
    nreal = seq // CHUNK

    def body(p_ref, lg_ref, q_ref, k_ref, o_ref, eg_ref):
        q_in, k_out, o_intra, egs = hg_local(p_ref[...], lg_ref[...])
        q_ref[...], k_ref[...], o_ref[...] = q_in.astype(MXU_DTYPE), k_out.astype(MXU_DTYPE), o_intra
        for c in range(LOCAL_CHUNKS):
            eg_ref[c] = egs[c]

    slab = pl.BlockSpec((None, rows, WIDTH), lambda s, g: (s, g, 0))
    return pl.pallas_call(
        body, grid=(b, seq // rows), name="hgrn2_local",
        in_specs=[pl.BlockSpec((None, rows, 4 * WIDTH), lambda s, g: (s, g, 0)), pl.BlockSpec(logits.shape, lambda s, g: (0, 0))],
        out_specs=[slab, slab, slab, pl.BlockSpec((None, LOCAL_CHUNKS, 1, WIDTH), lambda s, g: (s, g, 0, 0))],
        out_shape=[_sds((b, seq, WIDTH), MXU_DTYPE)] * 2 + [_sds((b, seq, WIDTH)), _sds((b, nreal, 1, WIDTH))],
        compiler_params=_cparams("arbitrary", "arbitrary"),
    )(p, logits)


def hg_local_lead(p0, logits):
    def body(p_ref, lg_ref, q_ref, k_ref, o_ref, eg_ref):
        q_in, k_out, o_ref[...], (eg_ref[...],) = hg_local(p_ref[...], lg_ref[...])
        q_ref[...], k_ref[...] = q_in.astype(MXU_DTYPE), k_out.astype(MXU_DTYPE)

    return pl.pallas_call(
        body, name="hgrn2_local_lead", in_specs=[VMEM_SPEC] * 2, out_specs=[VMEM_SPEC] * 4,
        out_shape=[_sds((CHUNK, WIDTH), MXU_DTYPE)] * 2 + [_sds((CHUNK, WIDTH)), _sds((1, WIDTH))],
        compiler_params=pltpu.CompilerParams(vmem_limit_bytes=VMEM_LIMIT),
    )(p0, logits)


def _hg_scan_inputs(c, b, q_ref, k_ref, o_ref, v_ref, z_ref, eg_ref, q0_ref, k0_ref, o0_ref, p0_ref, eg0_ref):
    lead = c == 0
    pick = lambda real, lead_val: jnp.where(lead, _lead_slabs(lead_val, b), _load_slabs(real, b))
    eg = jnp.where(lead, jnp.stack([eg0_ref[:, h * DH:(h + 1) * DH] for _ in range(b) for h in range(HEADS)], axis=0),
                   jnp.stack([eg_ref[i, :, h * DH:(h + 1) * DH] for i in range(b) for h in range(HEADS)], axis=0))
    return (pick(q_ref, q0_ref[...]), pick(k_ref, k0_ref[...]), pick(v_ref, p0_ref[:, 2 * WIDTH:3 * WIDTH]), eg,
            pick(o_ref, o0_ref[...]), pick(z_ref, p0_ref[:, 3 * WIDTH:4 * WIDTH]))


def _scan_specs(b, nc, reverse):
    chunk = (lambda i: nc - 1 - i) if reverse else (lambda i: i)
    slab = lambda lane_block: pl.BlockSpec((b, CHUNK, WIDTH), lambda i: (0, _real(chunk(i)), lane_block))
    per_chunk = lambda *tail: pl.BlockSpec((b, None) + tail, lambda i: (0, _real(chunk(i))) + (0,) * len(tail))
    state = pl.BlockSpec((b, None, WIDTH, DH), lambda i: (0, chunk(i), 0, 0))
    const = lambda a: pl.BlockSpec(a.shape, lambda i: (0,) * a.ndim)
    return slab, per_chunk, state, const


def run_scans(parts, nc, name):
    n_in = [len(p["args"]) for p in parts]
    n_out = [len(p["out_shape"]) for p in parts]
    n_scr = [len(p["scratch_shapes"]) for p in parts]

    def body(*refs):
        ins, outs, scr = refs[:sum(n_in)], refs[sum(n_in):sum(n_in) + sum(n_out)], refs[sum(n_in) + sum(n_out):]
        for i, part in enumerate(parts):
            part["body"](*ins[sum(n_in[:i]):sum(n_in[:i + 1])], *outs[sum(n_out[:i]):sum(n_out[:i + 1])],
                         *scr[sum(n_scr[:i]):sum(n_scr[:i + 1])])

    flat = lambda key: [v for p in parts for v in p[key]]
    out = pl.pallas_call(body, grid=(nc,), name=name, in_specs=flat("in_specs"), out_specs=flat("out_specs"),
                         out_shape=flat("out_shape"), scratch_shapes=flat("scratch_shapes"),
                         compiler_params=_cparams("arbitrary"))(*flat("args"))
    return [out[sum(n_out[:i]):sum(n_out[:i + 1])] for i in range(len(parts))]


def hg_scan_fwd(p, p0, local, lead, nw):
    b, seq, _ = p.shape
    nc = seq // CHUNK + 1
    q_in, k_out, o_intra, eg = local
    slab, per_chunk, state, const = _scan_specs(b, nc, False)

    def body(q_ref, k_ref, o_ref, v_ref, z_ref, eg_ref, q0_ref, k0_ref, o0_ref, p0_ref, eg0_ref, nw_ref, y_ref, ss_ref, st):
        c = pl.program_id(0)

        @pl.when(c == 0)
        def _():
            st[...] = jnp.zeros_like(st)

        s_in = st[...]
        _save_states(ss_ref, s_in, b)
        args = _hg_scan_inputs(c, b, q_ref, k_ref, o_ref, v_ref, z_ref, eg_ref, q0_ref, k0_ref, o0_ref, p0_ref, eg0_ref)
        y, s_new = hg_scan(*args, nw_ref[...], s_in)
        _store_slabs(y_ref, y, b)
        st[...] = s_new

    return dict(
        body=body, args=(q_in, k_out, o_intra, p, p, eg, lead[0], lead[1], lead[2], p0, lead[3], nw),
        in_specs=[slab(0), slab(0), slab(0), slab(2), slab(3), per_chunk(1, WIDTH)] + [const(a) for a in lead[0:3]]
        + [const(p0), const(lead[3]), const(nw)],
        out_specs=[slab(0), state],
        out_shape=[_sds((b, seq, WIDTH), MXU_DTYPE), _sds((b, nc, WIDTH, DH))],
        scratch_shapes=[pltpu.VMEM((b * HEADS, DH, DH), F32)])


def hg_scan_bwd(p, p0, local, lead, nw, ssave, dy):
    b, seq, _ = p.shape
    nc = seq // CHUNK + 1
    q_in, k_out, o_intra, eg = local
    slab, per_chunk, state, const = _scan_specs(b, nc, True)

    def body(q_ref, k_ref, o_ref, v_ref, z_ref, eg_ref, q0_ref, k0_ref, o0_ref, p0_ref, eg0_ref, nw_ref, ss_ref, dy_ref,
             dq_ref, dk_ref, do_ref, dv_ref, dz_ref, deg_ref, dq0_ref, dk0_ref, do0_ref, dv0_ref, dz0_ref, deg0_ref, dnw_ref,
             dst):
        i = pl.program_id(0)
        c = nc - 1 - i

        @pl.when(i == 0)
        def _():
            dst[...] = jnp.zeros_like(dst)
            dnw_ref[...] = jnp.zeros_like(dnw_ref)

        args = _hg_scan_inputs(c, b, q_ref, k_ref, o_ref, v_ref, z_ref, eg_ref, q0_ref, k0_ref, o0_ref, p0_ref, eg0_ref)
        s_in = _load_states(ss_ref, b)
        _, vjp = jax.vjp(hg_scan, *args, nw_ref[...], s_in)
        dyv = jnp.where(c == 0, 0.0, _load_slabs(dy_ref, b))
        dq, dk, dv, deg, do, dz, dnw, ds = vjp((dyv, dst[...]))
        dst[...] = ds
        dnw_ref[...] += dnw

        @pl.when(c > 0)
        def _():
            for ref, val in ((dq_ref, dq), (dk_ref, dk), (do_ref, do), (dv_ref, dv), (dz_ref, dz)):
                _store_slabs(ref, val, b)
            for j in range(b):
                deg_ref[j] = _rows(deg, j)

        @pl.when(c == 0)
        def _():
            for ref, val in ((dq0_ref, dq), (dk0_ref, dk), (do0_ref, do), (dv0_ref, dv), (dz0_ref, dz), (deg0_ref, deg)):
                ref[...] = _sum_rows(val, b)

    lead_out = [const(a) for a in lead[0:3]] + [const(lead[0]), const(lead[0]), const(lead[3])]
    return dict(
        body=body, args=(q_in, k_out, o_intra, p, p, eg, lead[0], lead[1], lead[2], p0, lead[3], nw, ssave, dy),
        in_specs=[slab(0), slab(0), slab(0), slab(2), slab(3), per_chunk(1, WIDTH)] + [const(a) for a in lead[0:3]]
        + [const(p0), const(lead[3]), const(nw), state, slab(0)],
        out_specs=[slab(0)] * 5 + [per_chunk(1, WIDTH)] + lead_out + [const(nw)],
        out_shape=[_sds((b, seq, WIDTH))] * 5 + [_sds(eg.shape)] + [_sds((CHUNK, WIDTH))] * 5 + [_sds((1, WIDTH)), _sds(nw.shape)],
        scratch_shapes=[pltpu.VMEM((b * HEADS, DH, DH), F32)])


def _hg_local_vjp(p, logits, dq, dk, do, degs, dv, dz):
    _, vjp = jax.vjp(hg_local, p, logits)
    dp, dlg = vjp((dq, dk, do, degs))
    return dp + jnp.concatenate([jnp.zeros((p.shape[0], 2 * WIDTH), F32), dv, dz], axis=1), dlg


def hg_local_bwd(p, logits, dq, dk, do, dv, dz, deg):
    b, seq, _ = p.shape
    rows = LOCAL_CHUNKS * CHUNK

    def body(p_ref, lg_ref, dq_ref, dk_ref, do_ref, dv_ref, dz_ref, deg_ref, dp_ref, dlg_ref):
        @pl.when((pl.program_id(0) == 0) & (pl.program_id(1) == 0))
        def _():
            dlg_ref[...] = jnp.zeros_like(dlg_ref)

        degs = tuple(deg_ref[c] for c in range(LOCAL_CHUNKS))
        dp, dlg = _hg_local_vjp(p_ref[...], lg_ref[...], dq_ref[...], dk_ref[...], do_ref[...], degs, dv_ref[...], dz_ref[...])
        dp_ref[...] = dp.astype(MXU_DTYPE)
        dlg_ref[...] += dlg

    slab = pl.BlockSpec((None, rows, WIDTH), lambda s, g: (s, g, 0))
    wide = pl.BlockSpec((None, rows, 4 * WIDTH), lambda s, g: (s, g, 0))
    lg = pl.BlockSpec(logits.shape, lambda s, g: (0, 0))
    return pl.pallas_call(
        body, grid=(b, seq // rows), name="hgrn2_local_bwd",
        in_specs=[wide, lg, slab, slab, slab, slab, slab, pl.BlockSpec((None, LOCAL_CHUNKS, 1, WIDTH), lambda s, g: (s, g, 0, 0))],
        out_specs=[wide, lg], out_shape=[_sds(p.shape, MXU_DTYPE), _sds(logits.shape)],
        compiler_params=_cparams("arbitrary", "arbitrary"),
    )(p, logits, dq, dk, do, dv, dz, deg)


def hg_local_bwd_lead(p0, logits, dq, dk, do, dv, dz, deg):
    def body(p_ref, lg_ref, dq_ref, dk_ref, do_ref, dv_ref, dz_ref, deg_ref, dp_ref, dlg_ref):
        dp, dlg_ref[...] = _hg_local_vjp(p_ref[...], lg_ref[...], dq_ref[...], dk_ref[...], do_ref[...],
                                         (deg_ref[...],), dv_ref[...], dz_ref[...])
        dp_ref[...] = dp.astype(MXU_DTYPE)

    return pl.pallas_call(
        body, name="hgrn2_local_bwd_lead", in_specs=[VMEM_SPEC] * 8, out_specs=[VMEM_SPEC] * 2,
        out_shape=[_sds(p0.shape, MXU_DTYPE), _sds(logits.shape)], compiler_params=pltpu.CompilerParams(vmem_limit_bytes=VMEM_LIMIT),
    )(p0, logits, dq, dk, do, dv, dz, deg)


def _halo_block(g):
    return jnp.maximum((LOCAL_CHUNKS * CHUNK // HALO) * g - 1, 0)


def _gd_window(g, p_ref, halo_ref, p0_ref):
    halo = jnp.where(g == 0, p0_ref[CHUNK - HALO:CHUNK, 0:QKV], halo_ref[...])
    return jnp.concatenate([halo, p_ref[:, 0:QKV]], axis=0)


def gd_local_fwd(p, p0, ab, cw, alog, dtb):
    b, seq, _ = p.shape
    rows = LOCAL_CHUNKS * CHUNK
    nreal = seq // CHUNK

    def body(p_ref, halo_ref, p0_ref, ab_ref, cw_ref, al_ref, dt_ref, u_ref, w_ref, qe_ref, ke_ref, qk_ref, ea_ref):
        uu, ww, qe, ke, qk, eas = gd_local(_gd_window(pl.program_id(1), p_ref, halo_ref, p0_ref), ab_ref[...], cw_ref[...],
                                           al_ref[...], dt_ref[...], inverse=_tri_y_impl)
        u_ref[...], w_ref[...], qe_ref[...], ke_ref[...] = uu, ww.astype(MXU_DTYPE), qe.astype(MXU_DTYPE), ke.astype(MXU_DTYPE)
        for c in range(LOCAL_CHUNKS):
            qk_ref[c] = qk[c * HEADS * CHUNK:(c + 1) * HEADS * CHUNK]
            ea_ref[c] = eas[c]

    const = lambda a: pl.BlockSpec(a.shape, lambda s, g: (0, 0))
    slab = pl.BlockSpec((None, rows, WIDTH), lambda s, g: (s, g, 0))
    return pl.pallas_call(
        body, grid=(b, seq // rows), name="gdn_local",
        in_specs=[pl.BlockSpec((None, rows, 4 * WIDTH), lambda s, g: (s, g, 0)),
                  pl.BlockSpec((None, HALO, QKV), lambda s, g: (s, _halo_block(g), 0)), const(p0),
                  pl.BlockSpec((None, rows, AB_PAD), lambda s, g: (s, g, 0)), const(cw), const(alog), const(dtb)],
        out_specs=[slab] * 4 + [pl.BlockSpec((None, LOCAL_CHUNKS, HEADS * CHUNK, CHUNK), lambda s, g: (s, g, 0, 0)),
                                pl.BlockSpec((None, LOCAL_CHUNKS, 1, AB_PAD), lambda s, g: (s, g, 0, 0))],
        out_shape=[_sds((b, seq, WIDTH))] + [_sds((b, seq, WIDTH), MXU_DTYPE)] * 3
        + [_sds((b, nreal, HEADS * CHUNK, CHUNK)), _sds((b, nreal, 1, AB_PAD))],
        compiler_params=_cparams("arbitrary", "arbitrary"),
    )(p, p, p0, ab, cw, alog, dtb)


def _lead_window(p0_ref):
    return jnp.concatenate([jnp.zeros((HALO, QKV), F32), p0_ref[:, 0:QKV]], axis=0)


def gd_local_lead(p0, ab0, cw, alog, dtb):
    def body(p0_ref, ab_ref, cw_ref, al_ref, dt_ref, u_ref, w_ref, qe_ref, ke_ref, qk_ref, ea_ref):
        u_ref[...], ww, qe, ke, qk_ref[...], (ea_ref[...],) = gd_local(
            _lead_window(p0_ref), ab_ref[...], cw_ref[...], al_ref[...], dt_ref[...], inverse=_tri_y_impl)
        w_ref[...], qe_ref[...], ke_ref[...] = ww.astype(MXU_DTYPE), qe.astype(MXU_DTYPE), ke.astype(MXU_DTYPE)

    return pl.pallas_call(
        body, name="gdn_local_lead", in_specs=[VMEM_SPEC] * 5, out_specs=[VMEM_SPEC] * 6,
        out_shape=[_sds((CHUNK, WIDTH))] + [_sds((CHUNK, WIDTH), MXU_DTYPE)] * 3 + [_sds((HEADS * CHUNK, CHUNK)), _sds((1, AB_PAD))],
        compiler_params=pltpu.CompilerParams(vmem_limit_bytes=VMEM_LIMIT),
    )(p0, ab0, cw, alog, dtb)


def _gd_scan_inputs(c, b, u_ref, w_ref, qe_ref, ke_ref, qk_ref, ea_ref, z_ref, u0_ref, w0_ref, qe0_ref, ke0_ref, qk0_ref,
                    ea0_ref, p0_ref):
    lead = c == 0
    pick = lambda real, lead_val: jnp.where(lead, _lead_slabs(lead_val, b), _load_slabs(real, b))
    pairs = [(i, h) for i in range(b) for h in range(HEADS)]
    qk = jnp.where(lead, jnp.stack([qk0_ref[h * CHUNK:(h + 1) * CHUNK, :] for _, h in pairs], axis=0),
                   jnp.stack([qk_ref[i, h * CHUNK:(h + 1) * CHUNK, :] for i, h in pairs], axis=0))
    ea = jnp.where(lead, jnp.stack([ea0_ref[:, h:h + 1] for _, h in pairs], axis=0),
                   jnp.stack([ea_ref[i, :, h:h + 1] for i, h in pairs], axis=0))
    return (pick(u_ref, u0_ref[...]), pick(w_ref, w0_ref[...]), pick(qe_ref, qe0_ref[...]), pick(ke_ref, ke0_ref[...]), qk,
            ea, pick(z_ref, p0_ref[:, QKV:QKV + WIDTH]))


def gd_scan_fwd(p, p0, local, lead, nw):
    b, seq, _ = p.shape
    nc = seq // CHUNK + 1
    slab, per_chunk, state, const = _scan_specs(b, nc, False)

    def body(u_ref, w_ref, qe_ref, ke_ref, qk_ref, ea_ref, z_ref, u0_ref, w0_ref, qe0_ref, ke0_ref, qk0_ref, ea0_ref, p0_ref,
             nw_ref, y_ref, ss_ref, st):
        c = pl.program_id(0)

        @pl.when(c == 0)
        def _():
            st[...] = jnp.zeros_like(st)

        s_in = st[...]
        _save_states(ss_ref, s_in, b)
        args = _gd_scan_inputs(c, b, u_ref, w_ref, qe_ref, ke_ref, qk_ref, ea_ref, z_ref, u0_ref, w0_ref, qe0_ref, ke0_ref,
                               qk0_ref, ea0_ref, p0_ref)
        y, s_new = gd_scan(*args, nw_ref[...], s_in)
        _store_slabs(y_ref, y, b)
        st[...] = s_new

    return dict(
        body=body, args=(*local, p, *lead, p0, nw),
        in_specs=[slab(0)] * 4 + [per_chunk(HEADS * CHUNK, CHUNK), per_chunk(1, AB_PAD), slab(3)] + [const(a) for a in lead]
        + [const(p0), const(nw)],
        out_specs=[slab(0), state],
        out_shape=[_sds((b, seq, WIDTH), MXU_DTYPE), _sds((b, nc, WIDTH, DH))],
        scratch_shapes=[pltpu.VMEM((b * HEADS, DH, DH), F32)])


def gd_scan_bwd(p, p0, local, lead, nw, ssave, dy):
    b, seq, _ = p.shape
    nc = seq // CHUNK + 1
    slab, per_chunk, state, const = _scan_specs(b, nc, True)

    def body(u_ref, w_ref, qe_ref, ke_ref, qk_ref, ea_ref, z_ref, u0_ref, w0_ref, qe0_ref, ke0_ref, qk0_ref, ea0_ref, p0_ref,
             nw_ref, ss_ref, dy_ref, du_ref, dw_ref, dqe_ref, dke_ref, dqk_ref, dea_ref, dz_ref, du0_ref, dw0_ref, dqe0_ref,
             dke0_ref, dqk0_ref, dea0_ref, dz0_ref, dnw_ref, dst):
        i = pl.program_id(0)
        c = nc - 1 - i

        @pl.when(i == 0)
        def _():
            dst[...] = jnp.zeros_like(dst)
            dnw_ref[...] = jnp.zeros_like(dnw_ref)

        args = _gd_scan_inputs(c, b, u_ref, w_ref, qe_ref, ke_ref, qk_ref, ea_ref, z_ref, u0_ref, w0_ref, qe0_ref, ke0_ref,
                               qk0_ref, ea0_ref, p0_ref)
        s_in = _load_states(ss_ref, b)
        _, vjp = jax.vjp(gd_scan, *args, nw_ref[...], s_in)
        dyv = jnp.where(c == 0, 0.0, _load_slabs(dy_ref, b))
        du, dw, dqe, dke, dqk, dea, dz, dnw, ds = vjp((dyv, dst[...]))
        dst[...] = ds
        dnw_ref[...] += dnw
        lane = lax.broadcasted_iota(jnp.int32, (1, AB_PAD), 1)
        dea_rows = [sum(jnp.where(lane == h, dea[j * HEADS + h], 0.0) for h in range(HEADS)) for j in range(b)]
        dqk_rows = [jnp.concatenate([dqk[j * HEADS + h] for h in range(HEADS)], axis=0) for j in range(b)]

        @pl.when(c > 0)
        def _():
            for ref, val in ((du_ref, du), (dw_ref, dw), (dqe_ref, dqe), (dke_ref, dke), (dz_ref, dz)):
                _store_slabs(ref, val, b)
            for j in range(b):
                dqk_ref[j] = dqk_rows[j]
                dea_ref[j] = dea_rows[j]

        @pl.when(c == 0)
        def _():
            for ref, val in ((du0_ref, du), (dw0_ref, dw), (dqe0_ref, dqe), (dke0_ref, dke), (dz0_ref, dz)):
                ref[...] = _sum_rows(val, b)
            dqk0_ref[...] = sum(dqk_rows[1:], dqk_rows[0])
            dea0_ref[...] = sum(dea_rows[1:], dea_rows[0])

    uu, ww, qe, ke, qk, ea = local
    return dict(
        body=body, args=(*local, p, *lead, p0, nw, ssave, dy),
        in_specs=[slab(0)] * 4 + [per_chunk(HEADS * CHUNK, CHUNK), per_chunk(1, AB_PAD), slab(3)] + [const(a) for a in lead]
        + [const(p0), const(nw), state, slab(0)],
        out_specs=[slab(0)] * 4 + [per_chunk(HEADS * CHUNK, CHUNK), per_chunk(1, AB_PAD), slab(0)] + [const(a) for a in lead]
        + [const(lead[0]), const(nw)],
        out_shape=[_sds((b, seq, WIDTH))] * 4 + [_sds(qk.shape), _sds(ea.shape), _sds((b, seq, WIDTH))]
        + [_sds(a.shape) for a in lead] + [_sds(lead[0].shape), _sds(nw.shape)],
        scratch_shapes=[pltpu.VMEM((b * HEADS, DH, DH), F32)])


def gd_local_bwd(p, p0, ab, cw, alog, dtb, cot, dz):
    b, seq, _ = p.shape
    rows = LOCAL_CHUNKS * CHUNK
    ng = seq // rows
    du, dw, dqe, dke, dqk, dea = cot

    def body(p_ref, halo_ref, p0_ref, ab_ref, cw_ref, al_ref, dt_ref, du_ref, dw_ref, dqe_ref, dke_ref, dqk_ref, dea_ref, dz_ref,
             dp_ref, dab_ref, dhalo0_ref, dcw_ref, dal_ref, ddt_ref, dhalo):
        i = pl.program_id(1)
        g = ng - 1 - i

        @pl.when(i == 0)
        def _():
            dhalo[...] = jnp.zeros_like(dhalo)

        @pl.when((pl.program_id(0) == 0) & (i == 0))
        def _():
            dcw_ref[...] = jnp.zeros_like(dcw_ref)
            dal_ref[...] = jnp.zeros_like(dal_ref)
            ddt_ref[...] = jnp.zeros_like(ddt_ref)

        _, vjp = jax.vjp(gd_local, _gd_window(g, p_ref, halo_ref, p0_ref), ab_ref[...], cw_ref[...], al_ref[...], dt_ref[...])
        dqk_all = jnp.concatenate([dqk_ref[c] for c in range(LOCAL_CHUNKS)], axis=0)
        deas = tuple(dea_ref[c] for c in range(LOCAL_CHUNKS))
        dxx, dab, dcw, dal, ddt = vjp((du_ref[...], dw_ref[...], dqe_ref[...], dke_ref[...], dqk_all, deas))
        dqkv = dxx[HALO:HALO + rows] + jnp.concatenate([jnp.zeros((rows - HALO, QKV), F32), dhalo[...]], axis=0)
        dhalo[...] = dxx[0:HALO]
        dhalo0_ref[...] = dxx[0:HALO]
        dp_ref[...] = jnp.concatenate([dqkv, dz_ref[...]], axis=1).astype(MXU_DTYPE)
        dab_ref[...] = dab.astype(MXU_DTYPE)
        dcw_ref[...] += dcw
        dal_ref[...] += dal
        ddt_ref[...] += ddt

    rg = lambda i: ng - 1 - i
    const = lambda a: pl.BlockSpec(a.shape, lambda s, i: (0, 0))
    slab = pl.BlockSpec((None, rows, WIDTH), lambda s, i: (s, rg(i), 0))
    wide = pl.BlockSpec((None, rows, 4 * WIDTH), lambda s, i: (s, rg(i), 0))
    gates = pl.BlockSpec((None, rows, AB_PAD), lambda s, i: (s, rg(i), 0))
    return pl.pallas_call(
        body, grid=(b, ng), name="gdn_local_bwd",
        in_specs=[wide, pl.BlockSpec((None, HALO, QKV), lambda s, i: (s, _halo_block(rg(i)), 0)), const(p0), gates, const(cw),
                  const(alog), const(dtb), slab, slab, slab, slab,
                  pl.BlockSpec((None, LOCAL_CHUNKS, HEADS * CHUNK, CHUNK), lambda s, i: (s, rg(i), 0, 0)),
                  pl.BlockSpec((None, LOCAL_CHUNKS, 1, AB_PAD), lambda s, i: (s, rg(i), 0, 0)), slab],
        out_specs=[wide, gates, pl.BlockSpec((None, HALO, QKV), lambda s, i: (s, 0, 0)), const(cw), const(alog), const(dtb)],
        out_shape=[_sds(p.shape, MXU_DTYPE), _sds(ab.shape, MXU_DTYPE), _sds((b, HALO, QKV)), _sds(cw.shape), _sds(alog.shape),
                   _sds(dtb.shape)],
        scratch_shapes=[pltpu.VMEM((HALO, QKV), F32)],
        compiler_params=_cparams("arbitrary", "arbitrary"),
    )(p, p, p0, ab, cw, alog, dtb, du, dw, dqe, dke, dqk, dea, dz)


def gd_local_bwd_lead(p0, ab0, cw, alog, dtb, cot, dz, dtail):
    def body(p0_ref, ab_ref, cw_ref, al_ref, dt_ref, du_ref, dw_ref, dqe_ref, dke_ref, dqk_ref, dea_ref, dz_ref, dtail_ref,
             dp_ref, dab_ref, dcw_ref, dal_ref, ddt_ref):
        _, vjp = jax.vjp(gd_local, _lead_window(p0_ref), ab_ref[...], cw_ref[...], al_ref[...], dt_ref[...])
        dxx, dab, dcw, dal, ddt = vjp((du_ref[...], dw_ref[...], dqe_ref[...], dke_ref[...], dqk_ref[...], (dea_ref[...],)))
        dqkv = dxx[HALO:HALO + CHUNK] + jnp.concatenate([jnp.zeros((CHUNK - HALO, QKV), F32), dtail_ref[...]], axis=0)
        dp_ref[...] = jnp.concatenate([dqkv, dz_ref[...]], axis=1).astype(MXU_DTYPE)
        dab_ref[...], dcw_ref[...], dal_ref[...], ddt_ref[...] = dab.astype(MXU_DTYPE), dcw, dal, ddt

    return pl.pallas_call(
        body, name="gdn_local_bwd_lead", in_specs=[VMEM_SPEC] * 13, out_specs=[VMEM_SPEC] * 5,
        out_shape=[_sds(p0.shape, MXU_DTYPE), _sds(ab0.shape, MXU_DTYPE), _sds(cw.shape), _sds(alog.shape), _sds(dtb.shape)],
        compiler_params=pltpu.CompilerParams(vmem_limit_bytes=VMEM_LIMIT),
    )(p0, ab0, cw, alog, dtb, *cot, dz, dtail)


def _position():
    return lax.axis_index("x"), lax.axis_index("y"), lax.axis_index("c")


def _exchange_blocks(bufs, send_sems, recv_sems):
    x, y, c = _position()
    me, sibling = (x, y, c), (x, y, 1 - c)
    chips = [(1 - x, y), (x, 1 - y), (1 - x, 1 - y)]
    per_buf = N_DEV - 1

    def copy(a, k, blk, to):
        rows = bufs[a].at[4 * blk[0] + 2 * blk[1] + blk[2]]
        return pltpu.make_async_remote_copy(src_ref=rows, dst_ref=rows, send_sem=send_sems.at[a * per_buf + k],
                                            recv_sem=recv_sems.at[a * per_buf + k], device_id=to, device_id_type=MESH)

    bufs_idx = range(len(bufs))
    first = [copy(a, 0, me, sibling) for a in bufs_idx] + [copy(a, 1 + j, me, (*chip, c)) for a in bufs_idx
                                                           for j, chip in enumerate(chips)]
    for cp in first:
        cp.start()
    passed = []
    for j, chip in enumerate(chips):
        for a in bufs_idx:
            copy(a, 1 + j, (*chip, c), me).wait_recv()
            passed.append(copy(a, 4 + j, (*chip, c), sibling))
            passed[-1].start()
    for a in bufs_idx:
        copy(a, 0, sibling, me).wait_recv()
        for j, chip in enumerate(chips):
            copy(a, 4 + j, (*chip, 1 - c), me).wait_recv()
    for cp in first + passed:
        cp.wait_send()


def _exchange_sems(n_bufs):
    return [pltpu.SemaphoreType.DMA((n_bufs * (N_DEV - 1),)), pltpu.SemaphoreType.DMA((n_bufs * (N_DEV - 1),))]


def gather_weights(w_in_t, w_out, small, pad_rows):
    rows, _, cols = w_in_t.shape

    def body(wi_ref, wo_ref, sm_ref, wi_out, wo_out, sm_out, wi_buf, send_sems, recv_sems):
        x, y, c = _position()
        me = 4 * x + 2 * y + c
        wi_buf[me] = wi_ref[:, 0, :].astype(MXU_DTYPE)
        wo_out[me] = wo_ref[...].astype(MXU_DTYPE)
        sm_out[me] = sm_ref[...]
        _exchange_blocks([wi_buf, wo_out, sm_out], send_sems, recv_sems)
        for d in range(N_DEV):
            wi_out[pl.ds(d * rows, rows), :] = wi_buf[d]
        wi_out[pl.ds(N_DEV * rows, pad_rows), :] = jnp.zeros((pad_rows, cols), MXU_DTYPE)

    return pl.pallas_call(
        body, name="gather_weights", in_specs=[VMEM_SPEC] * 3, out_specs=[VMEM_SPEC] * 3,
        out_shape=[jax.ShapeDtypeStruct((N_DEV * rows + pad_rows, cols), MXU_DTYPE),
                   jax.ShapeDtypeStruct((N_DEV,) + w_out.shape, MXU_DTYPE), jax.ShapeDtypeStruct((N_DEV,) + small.shape, F32)],
        scratch_shapes=[pltpu.VMEM((N_DEV, rows, cols), MXU_DTYPE)] + _exchange_sems(3),
        compiler_params=pltpu.CompilerParams(vmem_limit_bytes=VMEM_LIMIT))(w_in_t, w_out, small)


def reduce_gradients(tensors, small, name):
    n_t = len(tensors)
    arrays = [a for parts, _ in tensors for a, _ in parts]
    first_array = [sum(len(parts) for parts, _ in tensors[:t]) for t in range(n_t)]

    def pieces(t, j):
        parts, block_rows = tensors[t]
        out, base = [], 0
        for pi, (_, valid) in enumerate(parts):
            lo, hi = max(j * block_rows, base), min((j + 1) * block_rows, base + valid)
            if lo < hi:
                out.append((first_array[t] + pi, lo - base, lo - j * block_rows, hi - lo))
            base += valid
        return out

    def body(*refs):
        n_a = len(arrays)
        in_refs, small_ref = refs[:n_a], refs[n_a]
        out_refs, small_sum = refs[n_a + 1:n_a + 1 + n_t], refs[n_a + 1 + n_t]
        bufs, small_buf = refs[n_a + 2 + n_t:n_a + 2 + 5 * n_t], refs[n_a + 2 + 5 * n_t]
        s1_sems, r1_sems, s2_sems, r2_sems, small_send, small_recv = refs[n_a + 3 + 5 * n_t:]
        x, y, c = _position()
        chip = 2 * x + y

        def put(t, dst, j, add=None):
            for ai, src_row, dst_row, size in pieces(t, j):
                v = in_refs[ai][pl.ds(src_row, size), :]
                if add is not None:
                    v = v + add[pl.ds(dst_row, size), :].astype(F32)
                dst[pl.ds(dst_row, size), :] = v.astype(dst.dtype)

        def swap(t, k):
            send1, recv1 = bufs[4 * t], bufs[4 * t + 1]
            return pltpu.make_async_remote_copy(src_ref=send1.at[k], dst_ref=recv1.at[k], send_sem=s1_sems.at[4 * t + k],
                                                recv_sem=r1_sems.at[4 * t + k], device_id=(x, y, 1 - c), device_id_type=MESH)

        def to_chip(t, k, slot):
            send2, recv2 = bufs[4 * t + 2], bufs[4 * t + 3]
            return pltpu.make_async_remote_copy(src_ref=send2.at[k], dst_ref=recv2.at[slot], send_sem=s2_sems.at[4 * t + k],
                                                recv_sem=r2_sems.at[4 * t + slot], device_id=(k >> 1, k & 1, c),
                                                device_id_type=MESH)

        for t in range(n_t):
            for j in range(N_DEV):
                @pl.when((j & 1) != c)
                def _():
                    put(t, bufs[4 * t].at[j >> 1], j)
            for k in range(4):
                swap(t, k).start()

        small_buf[4 * x + 2 * y + c] = small_ref[...]
        _exchange_blocks([small_buf], small_send, small_recv)
        total = small_buf[0]
        for d in range(1, N_DEV):
            total = total + small_buf[d]
        small_sum[...] = total

        for t in range(n_t):
            recv1 = bufs[4 * t + 1]
            for k in range(4):
                swap(t, k).wait_recv()
                for j in (2 * k, 2 * k + 1):
                    @pl.when(((j & 1) == c) & (k != chip))
                    def _():
                        put(t, bufs[4 * t + 2].at[k], j, add=recv1.at[k])
                        to_chip(t, k, chip).start()

                    @pl.when(((j & 1) == c) & (k == chip))
                    def _():
                        put(t, out_refs[t], j, add=recv1.at[k])

        for t in range(n_t):
            for k in range(4):
                @pl.when(k != chip)
                def _():
                    to_chip(t, k, k).wait_recv()
                    out_refs[t][...] += bufs[4 * t + 3][k].astype(F32)

        for t in range(n_t):
            for k in range(4):
                @pl.when(k != chip)
                def _():
                    to_chip(t, k, chip).wait_send()
                swap(t, k).wait_send()

    scratch, out_shape = [], []
    for parts, block_rows in tensors:
        cols = parts[0][0].shape[1]
        scratch += [pltpu.VMEM((4, block_rows, cols), MXU_DTYPE)] * 4
        out_shape.append(jax.ShapeDtypeStruct((block_rows, cols), F32))
    out_shape.append(jax.ShapeDtypeStruct(small.shape, F32))
    scratch += [pltpu.VMEM((N_DEV,) + small.shape, F32)] + [pltpu.SemaphoreType.DMA((4 * n_t,))] * 4 + _exchange_sems(1)
    return pl.pallas_call(
        body, name=name, in_specs=[VMEM_SPEC] * (len(arrays) + 1), out_specs=[VMEM_SPEC] * (n_t + 1), out_shape=out_shape,
        scratch_shapes=scratch, compiler_params=pltpu.CompilerParams(vmem_limit_bytes=VMEM_LIMIT),
    )(*arrays, small)


def _adamw_step(w, g, m, v):
    mn = ADAM_B1 * m + (1.0 - ADAM_B1) * g
    vn = ADAM_B2 * v + (1.0 - ADAM_B2) * jnp.square(g)
    m_hat = mn / (1.0 - ADAM_B1 ** ADAM_STEP)
    v_hat = vn / (1.0 - ADAM_B2 ** ADAM_STEP)
    return -ADAM_LR * (m_hat / (jnp.sqrt(v_hat) + ADAM_EPS) + ADAM_WD * w), mn, vn


def adamw(w, g, m, v, name):
    rows, cols = w.shape
    tr = 256 if rows % 256 == 0 else rows

    def body(w_ref, g_ref, m_ref, v_ref, d_ref, nm_ref, nv_ref):
        d_ref[...], nm_ref[...], nv_ref[...] = _adamw_step(w_ref[...], g_ref[...], m_ref[...], v_ref[...])

    spec = pl.BlockSpec((tr, cols), lambda i: (i, 0))
    shape = jax.ShapeDtypeStruct((rows, cols), F32)
    return pl.pallas_call(body, grid=(rows // tr,), name=name, in_specs=[spec] * 4, out_specs=[spec] * 3,
                          out_shape=[shape] * 3, compiler_params=_cparams("arbitrary"))(w, g, m, v)


def adamw_w_in(w, g_t, m, v):
    def body(w_ref, g_ref, m_ref, v_ref, go_ref, d_ref, nm_ref, nv_ref):
        g = g_ref[...]
        go_ref[:, 0, :] = g
        d_ref[:, 0, :], nm_ref[:, 0, :], nv_ref[:, 0, :] = _adamw_step(w_ref[:, 0, :], g, m_ref[:, 0, :], v_ref[:, 0, :])

    return pl.pallas_call(body, name="adamw_w_in", in_specs=[VMEM_SPEC] * 4, out_specs=[VMEM_SPEC] * 4,
                          out_shape=[jax.ShapeDtypeStruct(w.shape, F32)] * 4,
                          compiler_params=pltpu.CompilerParams(vmem_limit_bytes=VMEM_LIMIT))(w, g_t, m, v)


def _pad_rows(a, rows=8):
    return jnp.pad(a, ((0, rows - a.shape[0]), (0, 0)))


def _pad_lanes(a, lanes=128):
    return jnp.pad(a, ((0, 0), (0, lanes - a.shape[1])))


def kernel(x, meta_tokens, norm_w, w_in, conv_w, hg_lb_logits, hg_norm_w, gdn_A_log, gdn_dt_bias, gdn_norm_w, w_out, final_norm_w, loss_target, m_meta_tokens, m_norm_w, m_w_in, m_conv_w, m_hg_lb_logits, m_hg_norm_w, m_gdn_A_log, m_gdn_dt_bias, m_gdn_norm_w, m_w_out, m_final_norm_w, v_meta_tokens, v_norm_w, v_w_in, v_conv_w, v_hg_lb_logits, v_hg_norm_w, v_gdn_A_log, v_gdn_dt_bias, v_gdn_norm_w, v_w_out, v_final_norm_w):
    b, seq, _ = x.shape
    n = b * seq
    dev = 4 * lax.axis_index("x") + 2 * lax.axis_index("y") + lax.axis_index("c")
    col_shard = IN_COLS // N_DEV

    small_w = jnp.concatenate([_pad_lanes(meta_tokens, 256), _pad_rows(_pad_lanes(conv_w[0], 256))], axis=0)
    w_t, w_out_g, small_g = gather_weights(jnp.transpose(w_in, (2, 0, 1)), w_out[0], small_w, AB_PAD - 2 * HEADS)
    meta_g = small_g[:, 0:N_META, 0:D_MODEL // N_DEV]
    conv_g = small_g[:, N_META:N_META + CONV_TAPS, 0:QKV // N_DEV]
    w_out_full = w_out_g.reshape(2 * WIDTH, D_MODEL)
    cw = jnp.transpose(conv_g, (1, 0, 2)).reshape(CONV_TAPS, QKV)
    meta = jnp.transpose(meta_g, (1, 0, 2)).reshape(N_META, D_MODEL)
    alog = _pad_lanes(gdn_A_log)
    dtb = _pad_lanes(gdn_dt_bias)
    fw = final_norm_w.reshape(1, D_MODEL)

    h0 = jnp.concatenate([jnp.zeros((CHUNK - N_META, D_MODEL), F32), meta], axis=0)
    x2 = x.reshape(n, D_MODEL)
    phg0, pgd0, pab0, u0 = in_proj(h0, norm_w, w_t, "in_proj_lead", True)
    phg, pgd, pab = in_proj(x2, norm_w, w_t, "in_proj", False)
    phg3, pgd3, pab3 = phg.reshape(b, seq, 4 * WIDTH), pgd.reshape(b, seq, 4 * WIDTH), pab.reshape(b, seq, AB_PAD)
    nc = seq // CHUNK + 1
    hg_loc = hg_local_fwd(phg3, hg_lb_logits)
    hg_lead = hg_local_lead(phg0, hg_lb_logits)
    gd_loc = gd_local_fwd(pgd3, pgd0, pab3, cw, alog, dtb)
    gd_lead = gd_local_lead(pgd0, pab0, cw, alog, dtb)
    (y_hg, s_hg), (y_gd, s_gd) = run_scans([hg_scan_fwd(phg3, phg0, hg_loc, hg_lead, hg_norm_w),
                                            gd_scan_fwd(pgd3, pgd0, gd_loc, gd_lead, gdn_norm_w)], nc, "scans")

    dh2, dy_hg, dy_gd, g_w_out, loss_part, g_fw = out_proj_loss(
        x2, loss_target.reshape(n, D_MODEL), y_hg.reshape(n, WIDTH), y_gd.reshape(n, WIDTH), w_out_full, fw)

    hb, gb = run_scans([hg_scan_bwd(phg3, phg0, hg_loc, hg_lead, hg_norm_w, s_hg, dy_hg.reshape(b, seq, WIDTH)),
                        gd_scan_bwd(pgd3, pgd0, gd_loc, gd_lead, gdn_norm_w, s_gd, dy_gd.reshape(b, seq, WIDTH))],
                       nc, "scans_bwd")
    dphg, g_lb = hg_local_bwd(phg3, hg_lb_logits, *hb[0:6])
    dphg0, g_lb0 = hg_local_bwd_lead(phg0, hg_lb_logits, *hb[6:12])
    g_hg_nw = hb[12]
    dpgd, dpab, dtail, g_cw, g_alog, g_dtb = gd_local_bwd(pgd3, pgd0, pab3, cw, alog, dtb, gb[0:6], gb[6])
    dpgd0, dpab0, g_cw0, g_alog0, g_dtb0 = gd_local_bwd_lead(pgd0, pab0, cw, alog, dtb, gb[7:13], gb[13], dtail.sum(0))
    g_gd_nw = gb[14]
    dphg, dpgd, dpab = dphg.reshape(n, 4 * WIDTH), dpgd.reshape(n, 4 * WIDTH), dpab.reshape(n, AB_PAD)

    grad_x, g_nw, g_w_hg, g_w_gd, g_w_ab = in_proj_bwd_weights(dphg, dpgd, dpab, w_t, x2, dh2, norm_w, u0, dphg0, dpgd0, dpab0)
    dh0, g_nw0 = in_proj_bwd(dphg0, dpgd0, dpab0, w_t, h0, jnp.zeros_like(h0), norm_w, "in_proj_bwd_lead")

    small = jnp.concatenate([
        (g_nw + g_nw0).reshape(8, 128), (g_lb + g_lb0).reshape(8, 128), _pad_rows(g_hg_nw), _pad_rows(g_alog + g_alog0),
        _pad_rows(g_dtb + g_dtb0), _pad_rows(g_gd_nw), g_fw.reshape(8, 128), (g_cw + g_cw0).reshape(48, 128),
        dh0[CHUNK - N_META:CHUNK].reshape(128, 128), loss_part], axis=0)
    g_w_in_t, g_w_out, small = reduce_gradients(
        [([(g_w_hg, 4 * WIDTH), (g_w_gd, 4 * WIDTH), (g_w_ab, 2 * HEADS)], col_shard),
         ([(g_w_out, 2 * WIDTH)], (2 * WIDTH) // N_DEV)], small, "reduce_gradients")
    g_norm_w = small[0:8].reshape(1, D_MODEL)
    g_lb = small[8:16].reshape(2, WIDTH)
    g_hg_nw = small[16:17]
    g_alog = small[24:25, 0:HEADS]
    g_dtb = small[32:33, 0:HEADS]
    g_gd_nw = small[40:41]
    g_fw = small[48:56].reshape(1, D_MODEL)
    g_cw_full = small[56:104].reshape(CONV_TAPS, QKV)
    g_meta_full = small[104:232].reshape(N_META, D_MODEL)
    loss = small[232, 0]
    g_conv = lax.dynamic_slice_in_dim(g_cw_full, dev * (QKV // N_DEV), QKV // N_DEV, axis=1)
    g_meta = lax.dynamic_slice_in_dim(g_meta_full, dev * (D_MODEL // N_DEV), D_MODEL // N_DEV, axis=1)

    names = ["meta_tokens", "norm_w", "w_in", "conv_w", "hg_lb_logits", "hg_norm_w", "gdn_A_log", "gdn_dt_bias",
             "gdn_norm_w", "w_out", "final_norm_w"]
    weights = [meta_tokens, norm_w, w_in, conv_w, hg_lb_logits, hg_norm_w, gdn_A_log, gdn_dt_bias, gdn_norm_w, w_out,
               final_norm_w]
    moms = [m_meta_tokens, m_norm_w, m_w_in, m_conv_w, m_hg_lb_logits, m_hg_norm_w, m_gdn_A_log, m_gdn_dt_bias,
            m_gdn_norm_w, m_w_out, m_final_norm_w]
    vars_ = [v_meta_tokens, v_norm_w, v_w_in, v_conv_w, v_hg_lb_logits, v_hg_norm_w, v_gdn_A_log, v_gdn_dt_bias,
             v_gdn_norm_w, v_w_out, v_final_norm_w]
    grads2d = [g_meta, g_norm_w, g_w_in_t, g_conv, g_lb, g_hg_nw, g_alog, g_dtb, g_gd_nw, g_w_out, g_fw]
    grads, deltas, new_ms, new_vs = [], [], [], []
    for nm, w, g2, m, v in zip(names, weights, grads2d, moms, vars_):
        if nm == "w_in":
            to3, back = (lambda a: jnp.transpose(a, (2, 0, 1))), (lambda a: jnp.transpose(a, (1, 2, 0)))
            g2, d, nm_, nv_ = adamw_w_in(to3(w), g2, to3(m), to3(v))
        else:
            to2d, back = (lambda a, s=g2.shape: a.reshape(s)), (lambda a, s=w.shape: a.reshape(s))
            d, nm_, nv_ = adamw(to2d(w), g2, to2d(m), to2d(v), "adamw_" + nm)
        grads.append(back(g2))
        deltas.append(back(d))
        new_ms.append(back(nm_))
        new_vs.append(back(nv_))
    return (loss, grad_x.reshape(x.shape), *grads, *deltas, *new_ms, *new_vs)
```

```python
import jax
import jax.numpy as jnp
from jax import lax
from jax.experimental import pallas as pl
from jax.experimental.pallas import tpu as pltpu

F32 = jnp.float32
BF16 = jnp.bfloat16
MXU_DTYPE = BF16

D_MODEL = 1024
N_META = 16
CHUNK = 64
SUB = 16
HEADS = 4
DH = 128
WIDTH = HEADS * DH
QKV = 3 * WIDTH
CONV_TAPS = 4
HALO = 8
EPS = 1e-6
IN_COLS = 4 * WIDTH + 4 * WIDTH + 2 * HEADS
AB_PAD = 128
N_DEV = 8
LOCAL_CHUNKS = 2
VMEM_LIMIT = 56 * 1024 * 1024
VMEM_LIMIT_LARGE = 60 * 1024 * 1024

ADAM_LR = 0.001
ADAM_B1 = 0.9
ADAM_B2 = 0.999
ADAM_EPS = 1e-08
ADAM_WD = 0.01
ADAM_STEP = 10

VMEM_SPEC = pl.BlockSpec(memory_space=pltpu.VMEM)
MESH = pl.DeviceIdType.MESH


def _mm_tn(a, b):
    return lax.dot_general(a.astype(MXU_DTYPE), b.astype(MXU_DTYPE), (((0,), (0,)), ((), ())), preferred_element_type=F32)


def _bmm(a, b):
    return lax.dot_general(a.astype(MXU_DTYPE), b.astype(MXU_DTYPE), (((2,), (1,)), ((0,), (0,))), preferred_element_type=F32)


def _bmm_nt(a, b):
    return lax.dot_general(a.astype(MXU_DTYPE), b.astype(MXU_DTYPE), (((2,), (2,)), ((0,), (0,))), preferred_element_type=F32)


def _bmm_tn(a, b):
    return lax.dot_general(a.astype(MXU_DTYPE), b.astype(MXU_DTYPE), (((1,), (1,)), ((0,), (0,))), preferred_element_type=F32)


def _iota2(n, m):
    return lax.broadcasted_iota(jnp.int32, (n, m), 0), lax.broadcasted_iota(jnp.int32, (n, m), 1)


def _silu(x):
    return x * jax.nn.sigmoid(x)


def _gated_norm(o, z, nw):
    return o * lax.rsqrt(jnp.mean(o * o, axis=-1, keepdims=True) + EPS) * nw * _silu(z)


def _heads(a, nb):
    return jnp.stack([a[c * CHUNK:(c + 1) * CHUNK, h * DH:(h + 1) * DH] for c in range(nb) for h in range(HEADS)], axis=0)


def _unheads(a3, nb):
    return jnp.concatenate(
        [jnp.concatenate([a3[c * HEADS + h] for h in range(HEADS)], axis=1) for c in range(nb)], axis=0)


def _split3(x):
    hi = x.astype(BF16)
    r1 = x - hi.astype(F32)
    mid = r1.astype(BF16)
    return hi, mid, (r1 - mid.astype(F32)).astype(BF16)


def _select_mm(pattern, n_out, n_in, transposed, x):
    rows, inner = (n_in, n_out) if transposed else (n_out, n_in)
    r, c = _iota2(rows, 3 * inner)
    c = c - jnp.where(c >= inner, inner, 0) - jnp.where(c >= 2 * inner, inner, 0)
    s = jnp.where(pattern(c, r) if transposed else pattern(r, c), 1.0, 0.0).astype(BF16)
    return jnp.dot(s, jnp.concatenate(_split3(x), axis=0), preferred_element_type=F32)


def _select_rows(pattern, n_out, x):
    @jax.custom_vjp
    def apply(v):
        return _select_mm(pattern, n_out, CHUNK, False, v)

    apply.defvjp(lambda v: (_select_mm(pattern, n_out, CHUNK, False, v), None),
                 lambda _, d: (_select_mm(pattern, n_out, CHUNK, True, d),))
    return apply(x)


def _cumsum_chunks(x, nb):
    return jnp.concatenate([_select_rows(lambda i, j: j <= i, CHUNK, x[c * CHUNK:(c + 1) * CHUNK]) for c in range(nb)], axis=0)


HG_LEVELS = 6


def _hg_sums(i, j):
    lvl, t = i >> HG_LEVELS, i & (CHUNK - 1)
    last = t
    for l in range(1, HG_LEVELS + 1):
        width = HG_LEVELS + 1 - l
        last = jnp.where(lvl == l, ((t >> width) << width) + (CHUNK >> l) - 1, last)
    return j <= last


def hg_local(p, logits):
    nb = p.shape[0] // CHUNK
    l0, l1 = logits[0:1], logits[1:2]
    mx = jnp.maximum(l0, l1)
    e0, e1 = jnp.exp(l0 - mx), jnp.exp(l1 - mx)
    lb = e0 / (e0 + e1)
    q = _silu(p[:, 0:WIDTH])
    f = lb + (1.0 - lb) * jax.nn.sigmoid(p[:, WIDTH:2 * WIDTH])
    k = 1.0 - f
    logf = jnp.log(f)
    sums = [_select_rows(_hg_sums, (HG_LEVELS + 1) * CHUNK, logf[c * CHUNK:(c + 1) * CHUNK]) for c in range(nb)]
    level = lambda l: _heads(jnp.concatenate([s[l * CHUNK:(l + 1) * CHUNK] for s in sums], axis=0), nb)
    q3, k3, v3, g3 = _heads(q, nb), _heads(k, nb), _heads(p[:, 2 * WIDTH:3 * WIDTH], nb), level(0)
    r, c = _iota2(CHUNK, CHUNK)
    row = lax.broadcasted_iota(jnp.int32, (CHUNK, DH), 0)
    a = jnp.where(r == c, _bmm_nt(q3, k3), 0.0)
    for l in range(1, HG_LEVELS + 1):
        sh = HG_LEVELS - l
        qk = jnp.where(((row >> sh) & 1) == 1, q3, k3) * jnp.exp(-jnp.abs(g3 - level(l)))
        pair = ((r >> (sh + 1)) == (c >> (sh + 1))) & (((r >> sh) & 1) == 1) & (((c >> sh) & 1) == 0)
        a = a + jnp.where(pair, _bmm_nt(qk, qk), 0.0)
    o = _bmm(a, v3)
    glast = g3[:, CHUNK - 1:CHUNK, :]
    egs = tuple(jnp.concatenate([jnp.exp(glast[c * HEADS + h]) for h in range(HEADS)], axis=1) for c in range(nb))
    return _unheads(q3 * jnp.exp(g3), nb), _unheads(k3 * jnp.exp(glast - g3), nb), _unheads(o, nb), egs


def hg_scan(q_in, k_out, v, eg, o_intra, z, nw, st):
    o = o_intra + _bmm_nt(q_in, st)
    return _gated_norm(o, z, nw), st * eg + _bmm_tn(v, k_out)


def _tri_y_impl(a):
    r, c = _iota2(CHUNK, CHUNK)
    same16 = (r // SUB) == (c // SUB)
    same32 = (r // (2 * SUB)) == (c // (2 * SUB))
    a0 = jnp.where(same16, a, 0.0)
    y = -a0
    pw = _bmm(a0, a0)
    for _ in range(2):
        y = y + pw + _bmm(y, pw)
        pw = _bmm(pw, pw)
    y = y + pw + _bmm(y, pw)
    for ak in (jnp.where(same32 & jnp.logical_not(same16), a, 0.0), jnp.where(same32, 0.0, a)):
        m = ak + _bmm(y, ak)
        y = y - (m + _bmm(m, y))
    return y


@jax.custom_vjp
def _tri_y(a):
    return _tri_y_impl(a)


def _tri_y_fwd(a):
    y = _tri_y_impl(a)
    return y, y


def _tri_y_bwd(y, dy):
    n = dy + _bmm_tn(y, dy)
    return (-(n + _bmm_nt(n, y)),)


_tri_y.defvjp(_tri_y_fwd, _tri_y_bwd)


def _rows_down(x, s):
    rows = x.shape[0]

    @jax.custom_vjp
    def rotate(v):
        return pltpu.roll(v, s, 0)

    rotate.defvjp(lambda v: (pltpu.roll(v, s, 0), None), lambda _, d: (pltpu.roll(d, rows - s, 0),))
    return rotate(x)


def gd_local(xx, ab, cw, alog, dtb, inverse=_tri_y):
    n = ab.shape[0]
    nb = n // CHUNK
    conv = cw[CONV_TAPS - 1:CONV_TAPS] * xx[HALO:HALO + n]
    for j in range(CONV_TAPS - 1):
        conv = conv + cw[j:j + 1] * _rows_down(xx, CONV_TAPS - 1 - j)[HALO:HALO + n]
    act = _silu(conv)
    x = ab + dtb
    g_all = -jnp.exp(alog) * (jnp.maximum(x, 0.0) + jnp.log1p(jnp.exp(-jnp.abs(x))))
    beta_all = jax.nn.sigmoid(ab)
    gam_all = _cumsum_chunks(g_all, nb)
    q3, k3, v3 = _heads(act[:, 0:WIDTH], nb), _heads(act[:, WIDTH:2 * WIDTH], nb), _heads(act[:, 2 * WIDTH:QKV], nb)
    q3 = q3 * lax.rsqrt(jnp.sum(q3 * q3, axis=-1, keepdims=True) + EPS) * (DH ** -0.5)
    k3 = k3 * lax.rsqrt(jnp.sum(k3 * k3, axis=-1, keepdims=True) + EPS)
    pairs = [(c, h) for c in range(nb) for h in range(HEADS)]
    beta = jnp.stack([beta_all[c * CHUNK:(c + 1) * CHUNK, HEADS + h:HEADS + h + 1] for c, h in pairs], axis=0)
    gam = jnp.stack([gam_all[c * CHUNK:(c + 1) * CHUNK, h:h + 1] for c, h in pairs], axis=0)
    gam_t = [gam_all[c * CHUNK:(c + 1) * CHUNK].T for c in range(nb)]
    gam_row = jnp.stack([gam_t[c][h:h + 1, :] for c, h in pairs], axis=0)
    glast = gam[:, CHUNK - 1:CHUNK, :]
    r, c = _iota2(CHUNK, CHUNK)
    dec = jnp.exp(jnp.where(c < r, gam - gam_row, -jnp.inf))
    y = inverse(beta * _bmm_nt(k3, k3) * dec)
    eg = jnp.exp(gam)
    rhs = jnp.concatenate([beta * v3, (beta * eg) * k3], axis=2)
    sol = rhs + _bmm(y, rhs)
    qk = _bmm_nt(q3, k3) * jnp.where(r == c, 1.0, dec)
    eas = tuple(jnp.exp(gam_all[(c + 1) * CHUNK - 1:(c + 1) * CHUNK]) for c in range(nb))
    return (_unheads(sol[:, :, 0:DH], nb), _unheads(sol[:, :, DH:2 * DH], nb), _unheads(q3 * eg, nb),
            _unheads(k3 * jnp.exp(glast - gam), nb), jnp.concatenate([qk[g] for g in range(nb * HEADS)], axis=0), eas)


def gd_scan(uu, ww, qe, ke, qk, ea, z, nw, s):
    u = uu - _bmm(ww, s)
    o = _bmm(qe, s) + _bmm(qk, u)
    return _gated_norm(o, z, nw), ea * s + _bmm_tn(ke, u)


def _cparams(*sem):
    return pltpu.CompilerParams(dimension_semantics=sem, vmem_limit_bytes=VMEM_LIMIT)


def _row_tile(n):
    for t in (512, 256, 128, 64):
        if n % t == 0:
            return t
    raise ValueError(f"unsupported token count {n}")


def _w_in_specs():
    once = pl.Buffered(1)
    return [pl.BlockSpec((4 * WIDTH, D_MODEL), lambda *i: (0, 0), pipeline_mode=once),
            pl.BlockSpec((4 * WIDTH, D_MODEL), lambda *i: (1, 0), pipeline_mode=once),
            pl.BlockSpec((AB_PAD, D_MODEL), lambda *i: (8 * WIDTH // AB_PAD, 0), pipeline_mode=once)]


def in_proj(h, h0, norm_w, w_t):
    n = h.shape[0]
    tm = _row_tile(n)
    nt = (((1,), (1,)), ((), ()))

    def body(h_ref, h0_ref, nw_ref, whg_ref, wgd_ref, wab_ref, phg_ref, pgd_ref, pab_ref, phg0_ref, pgd0_ref, pab0_ref, u0_ref):
        def project(x, hg_ref, gd_ref, ab_ref):
            u = (x * lax.rsqrt(jnp.mean(x * x, axis=-1, keepdims=True) + EPS) * nw_ref[...]).astype(MXU_DTYPE)
            hg_ref[...] = lax.dot_general(u, whg_ref[...], nt, preferred_element_type=F32)
            gd_ref[...] = lax.dot_general(u, wgd_ref[...], nt, preferred_element_type=F32)
            ab_ref[...] = lax.dot_general(u, wab_ref[...], nt, preferred_element_type=F32)
            return u

        @pl.when(pl.program_id(0) == 0)
        def _():
            u0_ref[...] = project(h0_ref[...], phg0_ref, pgd0_ref, pab0_ref)

        project(h_ref[...], phg_ref, pgd_ref, pab_ref)

    n0 = h0.shape[0]
    row = lambda w: pl.BlockSpec((tm, w), lambda i: (i, 0))
    lead = lambda w: pl.BlockSpec((n0, w), lambda i: (0, 0))
    widths = [4 * WIDTH, 4 * WIDTH, AB_PAD]
    return pl.pallas_call(
        body, grid=(n // tm,), name="in_proj",
        in_specs=[row(D_MODEL), lead(D_MODEL), pl.BlockSpec(norm_w.shape, lambda i: (0, 0))] + _w_in_specs(),
        out_specs=[row(w) for w in widths] + [lead(w) for w in widths] + [lead(D_MODEL)],
        out_shape=[jax.ShapeDtypeStruct((n, w), F32) for w in widths] + [jax.ShapeDtypeStruct((n0, w), F32) for w in widths]
        + [jax.ShapeDtypeStruct((n0, D_MODEL), MXU_DTYPE)],
        compiler_params=_cparams("arbitrary"),
    )(h, h0, norm_w, w_t, w_t, w_t)


def out_proj_loss(x, tgt, y_hg, y_gd, w_out, fw):
    n = x.shape[0]
    tm = _row_tile(n)
    inv_d = 1.0 / D_MODEL

    def body(x_ref, t_ref, yh_ref, yg_ref, w_ref, fw_ref, dh_ref, dyh_ref, dyg_ref, dw_ref, loss_ref, dfw_ref):
        @pl.when(pl.program_id(0) == 0)
        def _():
            dw_ref[...] = jnp.zeros_like(dw_ref)
            loss_ref[...] = jnp.zeros_like(loss_ref)
            dfw_ref[...] = jnp.zeros_like(dfw_ref)

        yh, yg = yh_ref[...], yg_ref[...]
        wa, wb = w_ref[0:WIDTH, :], w_ref[WIDTH:2 * WIDTH, :]
        h2 = x_ref[...] + jnp.dot(yh, wa, preferred_element_type=F32) + jnp.dot(yg, wb, preferred_element_type=F32)
        r2 = lax.rsqrt(jnp.mean(h2 * h2, axis=-1, keepdims=True) + EPS)
        nrm = h2 * r2
        fwv = fw_ref[...]
        err = nrm * fwv - t_ref[...]
        loss_ref[...] += jnp.full(loss_ref.shape, 0.5 * inv_d * jnp.sum(err * err), F32)
        dout = err * inv_d
        dfw_ref[...] += jnp.sum(dout * nrm, axis=0, keepdims=True)
        dn = dout * fwv
        dh2 = r2 * (dn - nrm * jnp.mean(dn * nrm, axis=-1, keepdims=True))
        dh_ref[...] = dh2
        dhb = dh2.astype(MXU_DTYPE)
        dyh_ref[...] = lax.dot_general(dhb, wa, (((1,), (1,)), ((), ())), preferred_element_type=F32)
        dyg_ref[...] = lax.dot_general(dhb, wb, (((1,), (1,)), ((), ())), preferred_element_type=F32)
        dw_ref[0:WIDTH, :] += lax.dot_general(yh, dhb, (((0,), (0,)), ((), ())), preferred_element_type=F32)
        dw_ref[WIDTH:2 * WIDTH, :] += lax.dot_general(yg, dhb, (((0,), (0,)), ((), ())), preferred_element_type=F32)

    row = lambda w: pl.BlockSpec((tm, w), lambda i: (i, 0))
    full = lambda s: pl.BlockSpec(s, lambda i: (0, 0))
    return pl.pallas_call(
        body, grid=(n // tm,), name="out_proj_loss",
        in_specs=[row(D_MODEL), row(D_MODEL), row(WIDTH), row(WIDTH), full(w_out.shape), full(fw.shape)],
        out_specs=[row(D_MODEL), row(WIDTH), row(WIDTH), full((2 * WIDTH, D_MODEL)), full((8, 128)), full((1, D_MODEL))],
        out_shape=[jax.ShapeDtypeStruct((n, D_MODEL), F32), jax.ShapeDtypeStruct((n, WIDTH), F32),
                   jax.ShapeDtypeStruct((n, WIDTH), F32), jax.ShapeDtypeStruct((2 * WIDTH, D_MODEL), F32),
                   jax.ShapeDtypeStruct((8, 128), F32), jax.ShapeDtypeStruct((1, D_MODEL), F32)],
        compiler_params=_cparams("arbitrary"),
    )(x, tgt, y_hg, y_gd, w_out, fw)


def in_proj_bwd(dphg, dpgd, dpab, w_t, h, dh2, norm_w, h0, u0, dphg0, dpgd0, dpab0):
    n = h.shape[0]
    tm = _row_tile(n)
    steps = n // tm

    def body(dphg_ref, dpgd_ref, dpab_ref, whg_ref, wgd_ref, wab_ref, h_ref, dh2_ref, nw_ref, h0_ref, u0_ref, d0hg_ref,
             d0gd_ref, d0ab_ref, dx_ref, dx0_ref, dnw_ref, ghg_ref, ggd_ref, gab_ref, acc_hg, acc_gd, acc_ab):
        i = pl.program_id(0)
        nwv = nw_ref[...]

        def norm_bwd(dps, x):
            du = jnp.dot(dps[0], whg_ref[...], preferred_element_type=F32)
            du += jnp.dot(dps[1], wgd_ref[...], preferred_element_type=F32)
            du += jnp.dot(dps[2], wab_ref[...], preferred_element_type=F32)
            r = lax.rsqrt(jnp.mean(x * x, axis=-1, keepdims=True) + EPS)
            nrm = x * r
            dn = du * nwv
            return r * (dn - nrm * jnp.mean(dn * nrm, axis=-1, keepdims=True)), nrm, jnp.sum(du * nrm, axis=0, keepdims=True)

        def accumulate(dps, u, first):
            for acc, dp in zip((acc_hg, acc_gd, acc_ab), dps):
                step = min(acc.shape[0], 512)
                for lo in range(0, acc.shape[0], step):
                    part = _mm_tn(dp[:, lo:lo + step], u)
                    acc[lo:lo + step, :] = part if first else acc[lo:lo + step, :] + part

        @pl.when(i == 0)
        def _():
            dps0 = (d0hg_ref[...], d0gd_ref[...], d0ab_ref[...])
            dx0_ref[...], _, dnw_ref[...] = norm_bwd(dps0, h0_ref[...])
            accumulate(dps0, u0_ref[...], True)

        dps = (dphg_ref[...], dpgd_ref[...], dpab_ref[...])
        dx, nrm, dnw = norm_bwd(dps, h_ref[...])
        dx_ref[...] = dh2_ref[...] + dx
        dnw_ref[...] += dnw
        accumulate(dps, (nrm * nwv).astype(MXU_DTYPE), False)

        @pl.when(i == steps - 1)
        def _():
            pltpu.sync_copy(acc_hg, ghg_ref)
            pltpu.sync_copy(acc_gd, ggd_ref)
            pltpu.sync_copy(acc_ab, gab_ref)

    row = lambda w: pl.BlockSpec((tm, w), lambda i: (i, 0))
    full = lambda a: pl.BlockSpec(a.shape, lambda i: (0, 0), pipeline_mode=pl.Buffered(1))
    anywhere = pl.BlockSpec(memory_space=pl.ANY)
    return pl.pallas_call(
        body, grid=(steps,), name="in_proj_bwd",
        in_specs=[row(4 * WIDTH), row(4 * WIDTH), row(AB_PAD)] + _w_in_specs() + [row(D_MODEL), row(D_MODEL), full(norm_w),
                                                                                   full(h0), full(u0), full(dphg0), full(dpgd0),
                                                                                   full(dpab0)],
        out_specs=[row(D_MODEL), pl.BlockSpec(h0.shape, lambda i: (0, 0)), pl.BlockSpec((1, D_MODEL), lambda i: (0, 0)),
                   anywhere, anywhere, anywhere],
        out_shape=[jax.ShapeDtypeStruct((n, D_MODEL), F32), jax.ShapeDtypeStruct(h0.shape, F32),
                   jax.ShapeDtypeStruct((1, D_MODEL), F32), jax.ShapeDtypeStruct((4 * WIDTH, D_MODEL), F32),
                   jax.ShapeDtypeStruct((4 * WIDTH, D_MODEL), F32), jax.ShapeDtypeStruct((AB_PAD, D_MODEL), F32)],
        scratch_shapes=[pltpu.VMEM((4 * WIDTH, D_MODEL), F32), pltpu.VMEM((4 * WIDTH, D_MODEL), F32),
                        pltpu.VMEM((AB_PAD, D_MODEL), F32)],
        compiler_params=pltpu.CompilerParams(dimension_semantics=("arbitrary",), vmem_limit_bytes=VMEM_LIMIT_LARGE),
    )(dphg, dpgd, dpab, w_t, w_t, w_t, h, dh2, norm_w, h0, u0, dphg0, dpgd0, dpab0)


def _real(c):
    return jnp.maximum(c - 1, 0)


def _sds(shape, dtype=F32):
    return jax.ShapeDtypeStruct(shape, dtype)


def _load_slabs(ref, b):
    return jnp.stack([ref[i, :, h * DH:(h + 1) * DH].astype(F32) for i in range(b) for h in range(HEADS)], axis=0)


def _lead_slabs(a, b):
    return jnp.stack([a[:, h * DH:(h + 1) * DH].astype(F32) for _ in range(b) for h in range(HEADS)], axis=0)


def _rows(a3, i):
    return jnp.concatenate([a3[i * HEADS + h] for h in range(HEADS)], axis=1)


def _store_slabs(ref, a3, b):
    for i in range(b):
        ref[i] = _rows(a3, i).astype(ref.dtype)


def _sum_rows(a3, b):
    out = _rows(a3, 0)
    for i in range(1, b):
        out = out + _rows(a3, i)
    return out


def _save_states(ref, s, b):
    for i in range(b):
        ref[i] = jnp.concatenate([s[i * HEADS + h] for h in range(HEADS)], axis=0)


def _load_states(ref, b):
    return jnp.stack([ref[i, h * DH:(h + 1) * DH, :] for i in range(b) for h in range(HEADS)], axis=0)


def hg_local_fwd(p, logits):
    b, seq, _ = p.shape
    rows = LOCAL_CHUNKS * CHUNK
    nreal = seq // CHUNK

    def body(p_ref, lg_ref, q_ref, k_ref, o_ref, eg_ref):
        q_in, k_out, o_intra, egs = hg_local(p_ref[...], lg_ref[...])
        q_ref[...], k_ref[...], o_ref[...] = q_in.astype(MXU_DTYPE), k_out.astype(MXU_DTYPE), o_intra
        for c in range(LOCAL_CHUNKS):
            eg_ref[c] = egs[c]

    slab = pl.BlockSpec((None, rows, WIDTH), lambda s, g: (s, g, 0))
    return pl.pallas_call(
        body, grid=(b, seq // rows), name="hgrn2_local",
        in_specs=[pl.BlockSpec((None, rows, 4 * WIDTH), lambda s, g: (s, g, 0)), pl.BlockSpec(logits.shape, lambda s, g: (0, 0))],
        out_specs=[slab, slab, slab, pl.BlockSpec((None, LOCAL_CHUNKS, 1, WIDTH), lambda s, g: (s, g, 0, 0))],
        out_shape=[_sds((b, seq, WIDTH), MXU_DTYPE)] * 2 + [_sds((b, seq, WIDTH)), _sds((b, nreal, 1, WIDTH))],
        compiler_params=_cparams("arbitrary", "arbitrary"),
    )(p, logits)


def hg_local_lead(p0, logits):
    def body(p_ref, lg_ref, q_ref, k_ref, o_ref, eg_ref):
        q_in, k_out, o_ref[...], (eg_ref[...],) = hg_local(p_ref[...], lg_ref[...])
        q_ref[...], k_ref[...] = q_in.astype(MXU_DTYPE), k_out.astype(MXU_DTYPE)

    return pl.pallas_call(
        body, name="hgrn2_local_lead", in_specs=[VMEM_SPEC] * 2, out_specs=[VMEM_SPEC] * 4,
        out_shape=[_sds((CHUNK, WIDTH), MXU_DTYPE)] * 2 + [_sds((CHUNK, WIDTH)), _sds((1, WIDTH))],
        compiler_params=pltpu.CompilerParams(vmem_limit_bytes=VMEM_LIMIT),
    )(p0, logits)


def _hg_scan_inputs(c, b, q_ref, k_ref, o_ref, v_ref, z_ref, eg_ref, q0_ref, k0_ref, o0_ref, p0_ref, eg0_ref):
    lead = c == 0
    pick = lambda real, lead_val: jnp.where(lead, _lead_slabs(lead_val, b), _load_slabs(real, b))
    eg = jnp.where(lead, jnp.stack([eg0_ref[:, h * DH:(h + 1) * DH] for _ in range(b) for h in range(HEADS)], axis=0),
                   jnp.stack([eg_ref[i, :, h * DH:(h + 1) * DH] for i in range(b) for h in range(HEADS)], axis=0))
    return (pick(q_ref, q0_ref[...]), pick(k_ref, k0_ref[...]), pick(v_ref, p0_ref[:, 2 * WIDTH:3 * WIDTH]), eg,
            pick(o_ref, o0_ref[...]), pick(z_ref, p0_ref[:, 3 * WIDTH:4 * WIDTH]))


def _scan_specs(b, nc, reverse):
    chunk = (lambda i: nc - 1 - i) if reverse else (lambda i: i)
    slab = lambda lane_block: pl.BlockSpec((b, CHUNK, WIDTH), lambda i: (0, _real(chunk(i)), lane_block))
    per_chunk = lambda *tail: pl.BlockSpec((b, None) + tail, lambda i: (0, _real(chunk(i))) + (0,) * len(tail))
    state = pl.BlockSpec((b, None, WIDTH, DH), lambda i: (0, chunk(i), 0, 0))
    const = lambda a: pl.BlockSpec(a.shape, lambda i: (0,) * a.ndim)
    return slab, per_chunk, state, const


def run_scans(parts, nc, name):
    n_in = [len(p["args"]) for p in parts]
    n_out = [len(p["out_shape"]) for p in parts]
    n_scr = [len(p["scratch_shapes"]) for p in parts]

    def body(*refs):
        ins, outs, scr = refs[:sum(n_in)], refs[sum(n_in):sum(n_in) + sum(n_out)], refs[sum(n_in) + sum(n_out):]
        for i, part in enumerate(parts):
            part["body"](*ins[sum(n_in[:i]):sum(n_in[:i + 1])], *outs[sum(n_out[:i]):sum(n_out[:i + 1])],
                         *scr[sum(n_scr[:i]):sum(n_scr[:i + 1])])

    flat = lambda key: [v for p in parts for v in p[key]]
    out = pl.pallas_call(body, grid=(nc,), name=name, in_specs=flat("in_specs"), out_specs=flat("out_specs"),
                         out_shape=flat("out_shape"), scratch_shapes=flat("scratch_shapes"),
                         compiler_params=_cparams("arbitrary"))(*flat("args"))
    return [out[sum(n_out[:i]):sum(n_out[:i + 1])] for i in range(len(parts))]


def hg_scan_fwd(p, p0, local, lead, nw):
    b, seq, _ = p.shape
    nc = seq // CHUNK + 1
    q_in, k_out, o_intra, eg = local
    slab, per_chunk, state, const = _scan_specs(b, nc, False)

    def body(q_ref, k_ref, o_ref, v_ref, z_ref, eg_ref, q0_ref, k0_ref, o0_ref, p0_ref, eg0_ref, nw_ref, y_ref, ss_ref, st):
        c = pl.program_id(0)

        @pl.when(c == 0)
        def _():
            st[...] = jnp.zeros_like(st)

        s_in = st[...]
        _save_states(ss_ref, s_in, b)
        args = _hg_scan_inputs(c, b, q_ref, k_ref, o_ref, v_ref, z_ref, eg_ref, q0_ref, k0_ref, o0_ref, p0_ref, eg0_ref)
        y, s_new = hg_scan(*args, nw_ref[...], s_in)
        _store_slabs(y_ref, y, b)
        st[...] = s_new

    return dict(
        body=body, args=(q_in, k_out, o_intra, p, p, eg, lead[0], lead[1], lead[2], p0, lead[3], nw),
        in_specs=[slab(0), slab(0), slab(0), slab(2), slab(3), per_chunk(1, WIDTH)] + [const(a) for a in lead[0:3]]
        + [const(p0), const(lead[3]), const(nw)],
        out_specs=[slab(0), state],
        out_shape=[_sds((b, seq, WIDTH), MXU_DTYPE), _sds((b, nc, WIDTH, DH))],
        scratch_shapes=[pltpu.VMEM((b * HEADS, DH, DH), F32)])


def hg_scan_bwd(p, p0, local, lead, nw, ssave, dy):
    b, seq, _ = p.shape
    nc = seq // CHUNK + 1
    q_in, k_out, o_intra, eg = local
    slab, per_chunk, state, const = _scan_specs(b, nc, True)

    def body(q_ref, k_ref, o_ref, v_ref, z_ref, eg_ref, q0_ref, k0_ref, o0_ref, p0_ref, eg0_ref, nw_ref, ss_ref, dy_ref,
             dq_ref, dk_ref, do_ref, dv_ref, dz_ref, deg_ref, dq0_ref, dk0_ref, do0_ref, dv0_ref, dz0_ref, deg0_ref, dnw_ref,
             dst):
        i = pl.program_id(0)
        c = nc - 1 - i

        @pl.when(i == 0)
        def _():
            dst[...] = jnp.zeros_like(dst)
            dnw_ref[...] = jnp.zeros_like(dnw_ref)

        args = _hg_scan_inputs(c, b, q_ref, k_ref, o_ref, v_ref, z_ref, eg_ref, q0_ref, k0_ref, o0_ref, p0_ref, eg0_ref)
        s_in = _load_states(ss_ref, b)
        _, vjp = jax.vjp(hg_scan, *args, nw_ref[...], s_in)
        dyv = jnp.where(c == 0, 0.0, _load_slabs(dy_ref, b))
        dq, dk, dv, deg, do, dz, dnw, ds = vjp((dyv, dst[...]))
        dst[...] = ds
        dnw_ref[...] += dnw

        @pl.when(c > 0)
        def _():
            for ref, val in ((dq_ref, dq), (dk_ref, dk), (do_ref, do), (dv_ref, dv), (dz_ref, dz)):
                _store_slabs(ref, val, b)
            for j in range(b):
                deg_ref[j] = _rows(deg, j)

        @pl.when(c == 0)
        def _():
            for ref, val in ((dq0_ref, dq), (dk0_ref, dk), (do0_ref, do), (dv0_ref, dv), (dz0_ref, dz), (deg0_ref, deg)):
                ref[...] = _sum_rows(val, b)

    lead_out = [const(a) for a in lead[0:3]] + [const(lead[0]), const(lead[0]), const(lead[3])]
    return dict(
        body=body, args=(q_in, k_out, o_intra, p, p, eg, lead[0], lead[1], lead[2], p0, lead[3], nw, ssave, dy),
        in_specs=[slab(0), slab(0), slab(0), slab(2), slab(3), per_chunk(1, WIDTH)] + [const(a) for a in lead[0:3]]
        + [const(p0), const(lead[3]), const(nw), state, slab(0)],
        out_specs=[slab(0)] * 5 + [per_chunk(1, WIDTH)] + lead_out + [const(nw)],
        out_shape=[_sds((b, seq, WIDTH))] * 5 + [_sds(eg.shape)] + [_sds((CHUNK, WIDTH))] * 5 + [_sds((1, WIDTH)), _sds(nw.shape)],
        scratch_shapes=[pltpu.VMEM((b * HEADS, DH, DH), F32)])


def _hg_local_vjp(p, logits, dq, dk, do, degs, dv, dz):
    _, vjp = jax.vjp(hg_local, p, logits)
    dp, dlg = vjp((dq, dk, do, degs))
    return dp + jnp.concatenate([jnp.zeros((p.shape[0], 2 * WIDTH), F32), dv, dz], axis=1), dlg


def hg_local_bwd(p, logits, dq, dk, do, dv, dz, deg):
    b, seq, _ = p.shape
    rows = LOCAL_CHUNKS * CHUNK

    def body(p_ref, lg_ref, dq_ref, dk_ref, do_ref, dv_ref, dz_ref, deg_ref, dp_ref, dlg_ref):
        @pl.when((pl.program_id(0) == 0) & (pl.program_id(1) == 0))
        def _():
            dlg_ref[...] = jnp.zeros_like(dlg_ref)

        degs = tuple(deg_ref[c] for c in range(LOCAL_CHUNKS))
        dp, dlg = _hg_local_vjp(p_ref[...], lg_ref[...], dq_ref[...], dk_ref[...], do_ref[...], degs, dv_ref[...], dz_ref[...])
        dp_ref[...] = dp.astype(MXU_DTYPE)
        dlg_ref[...] += dlg

    slab = pl.BlockSpec((None, rows, WIDTH), lambda s, g: (s, g, 0))
    wide = pl.BlockSpec((None, rows, 4 * WIDTH), lambda s, g: (s, g, 0))
    lg = pl.BlockSpec(logits.shape, lambda s, g: (0, 0))
    return pl.pallas_call(
        body, grid=(b, seq // rows), name="hgrn2_local_bwd",
        in_specs=[wide, lg, slab, slab, slab, slab, slab, pl.BlockSpec((None, LOCAL_CHUNKS, 1, WIDTH), lambda s, g: (s, g, 0, 0))],
        out_specs=[wide, lg], out_shape=[_sds(p.shape, MXU_DTYPE), _sds(logits.shape)],
        compiler_params=_cparams("arbitrary", "arbitrary"),
    )(p, logits, dq, dk, do, dv, dz, deg)


def hg_local_bwd_lead(p0, logits, dq, dk, do, dv, dz, deg):
    def body(p_ref, lg_ref, dq_ref, dk_ref, do_ref, dv_ref, dz_ref, deg_ref, dp_ref, dlg_ref):
        dp, dlg_ref[...] = _hg_local_vjp(p_ref[...], lg_ref[...], dq_ref[...], dk_ref[...], do_ref[...],
                                         (deg_ref[...],), dv_ref[...], dz_ref[...])
        dp_ref[...] = dp.astype(MXU_DTYPE)

    return pl.pallas_call(
        body, name="hgrn2_local_bwd_lead", in_specs=[VMEM_SPEC] * 8, out_specs=[VMEM_SPEC] * 2,
        out_shape=[_sds(p0.shape, MXU_DTYPE), _sds(logits.shape)], compiler_params=pltpu.CompilerParams(vmem_limit_bytes=VMEM_LIMIT),
    )(p0, logits, dq, dk, do, dv, dz, deg)


def _halo_block(g):
    return jnp.maximum((LOCAL_CHUNKS * CHUNK // HALO) * g - 1, 0)


def _gd_window(g, p_ref, halo_ref, p0_ref):
    halo = jnp.where(g == 0, p0_ref[CHUNK - HALO:CHUNK, 0:QKV], halo_ref[...])
    return jnp.concatenate([halo, p_ref[:, 0:QKV]], axis=0)


def gd_local_fwd(p, p0, ab, cw, alog, dtb):
    b, seq, _ = p.shape
    rows = LOCAL_CHUNKS * CHUNK
    nreal = seq // CHUNK

    def body(p_ref, halo_ref, p0_ref, ab_ref, cw_ref, al_ref, dt_ref, u_ref, w_ref, qe_ref, ke_ref, qk_ref, ea_ref):
        uu, ww, qe, ke, qk, eas = gd_local(_gd_window(pl.program_id(1), p_ref, halo_ref, p0_ref), ab_ref[...], cw_ref[...],
                                           al_ref[...], dt_ref[...], inverse=_tri_y_impl)
        u_ref[...], w_ref[...], qe_ref[...], ke_ref[...] = uu, ww.astype(MXU_DTYPE), qe.astype(MXU_DTYPE), ke.astype(MXU_DTYPE)
        for c in range(LOCAL_CHUNKS):
            qk_ref[c] = qk[c * HEADS * CHUNK:(c + 1) * HEADS * CHUNK]
            ea_ref[c] = eas[c]

    const = lambda a: pl.BlockSpec(a.shape, lambda s, g: (0, 0))
    slab = pl.BlockSpec((None, rows, WIDTH), lambda s, g: (s, g, 0))
    return pl.pallas_call(
        body, grid=(b, seq // rows), name="gdn_local",
        in_specs=[pl.BlockSpec((None, rows, 4 * WIDTH), lambda s, g: (s, g, 0)),
                  pl.BlockSpec((None, HALO, QKV), lambda s, g: (s, _halo_block(g), 0)), const(p0),
                  pl.BlockSpec((None, rows, AB_PAD), lambda s, g: (s, g, 0)), const(cw), const(alog), const(dtb)],
        out_specs=[slab] * 4 + [pl.BlockSpec((None, LOCAL_CHUNKS, HEADS * CHUNK, CHUNK), lambda s, g: (s, g, 0, 0)),
                                pl.BlockSpec((None, LOCAL_CHUNKS, 1, AB_PAD), lambda s, g: (s, g, 0, 0))],
        out_shape=[_sds((b, seq, WIDTH))] + [_sds((b, seq, WIDTH), MXU_DTYPE)] * 3
        + [_sds((b, nreal, HEADS * CHUNK, CHUNK)), _sds((b, nreal, 1, AB_PAD))],
        compiler_params=_cparams("arbitrary", "arbitrary"),
    )(p, p, p0, ab, cw, alog, dtb)


def _lead_window(p0_ref):
    return jnp.concatenate([jnp.zeros((HALO, QKV), F32), p0_ref[:, 0:QKV]], axis=0)


def gd_local_lead(p0, ab0, cw, alog, dtb):
    def body(p0_ref, ab_ref, cw_ref, al_ref, dt_ref, u_ref, w_ref, qe_ref, ke_ref, qk_ref, ea_ref):
        u_ref[...], ww, qe, ke, qk_ref[...], (ea_ref[...],) = gd_local(
            _lead_window(p0_ref), ab_ref[...], cw_ref[...], al_ref[...], dt_ref[...], inverse=_tri_y_impl)
        w_ref[...], qe_ref[...], ke_ref[...] = ww.astype(MXU_DTYPE), qe.astype(MXU_DTYPE), ke.astype(MXU_DTYPE)

    return pl.pallas_call(
        body, name="gdn_local_lead", in_specs=[VMEM_SPEC] * 5, out_specs=[VMEM_SPEC] * 6,
        out_shape=[_sds((CHUNK, WIDTH))] + [_sds((CHUNK, WIDTH), MXU_DTYPE)] * 3 + [_sds((HEADS * CHUNK, CHUNK)), _sds((1, AB_PAD))],
        compiler_params=pltpu.CompilerParams(vmem_limit_bytes=VMEM_LIMIT),
    )(p0, ab0, cw, alog, dtb)


def _gd_scan_inputs(c, b, u_ref, w_ref, qe_ref, ke_ref, qk_ref, ea_ref, z_ref, u0_ref, w0_ref, qe0_ref, ke0_ref, qk0_ref,
                    ea0_ref, p0_ref):
    lead = c == 0
    pick = lambda real, lead_val: jnp.where(lead, _lead_slabs(lead_val, b), _load_slabs(real, b))
    pairs = [(i, h) for i in range(b) for h in range(HEADS)]
    qk = jnp.where(lead, jnp.stack([qk0_ref[h * CHUNK:(h + 1) * CHUNK, :] for _, h in pairs], axis=0),
                   jnp.stack([qk_ref[i, h * CHUNK:(h + 1) * CHUNK, :] for i, h in pairs], axis=0))
    ea = jnp.where(lead, jnp.stack([ea0_ref[:, h:h + 1] for _, h in pairs], axis=0),
                   jnp.stack([ea_ref[i, :, h:h + 1] for i, h in pairs], axis=0))
    return (pick(u_ref, u0_ref[...]), pick(w_ref, w0_ref[...]), pick(qe_ref, qe0_ref[...]), pick(ke_ref, ke0_ref[...]), qk,
            ea, pick(z_ref, p0_ref[:, QKV:QKV + WIDTH]))


def gd_scan_fwd(p, p0, local, lead, nw):
    b, seq, _ = p.shape
    nc = seq // CHUNK + 1
    slab, per_chunk, state, const = _scan_specs(b, nc, False)

    def body(u_ref, w_ref, qe_ref, ke_ref, qk_ref, ea_ref, z_ref, u0_ref, w0_ref, qe0_ref, ke0_ref, qk0_ref, ea0_ref, p0_ref,
             nw_ref, y_ref, ss_ref, st):
        c = pl.program_id(0)

        @pl.when(c == 0)
        def _():
            st[...] = jnp.zeros_like(st)

        s_in = st[...]
        _save_states(ss_ref, s_in, b)
        args = _gd_scan_inputs(c, b, u_ref, w_ref, qe_ref, ke_ref, qk_ref, ea_ref, z_ref, u0_ref, w0_ref, qe0_ref, ke0_ref,
                               qk0_ref, ea0_ref, p0_ref)
        y, s_new = gd_scan(*args, nw_ref[...], s_in)
        _store_slabs(y_ref, y, b)
        st[...] = s_new

    return dict(
        body=body, args=(*local, p, *lead, p0, nw),
        in_specs=[slab(0)] * 4 + [per_chunk(HEADS * CHUNK, CHUNK), per_chunk(1, AB_PAD), slab(3)] + [const(a) for a in lead]
        + [const(p0), const(nw)],
        out_specs=[slab(0), state],
        out_shape=[_sds((b, seq, WIDTH), MXU_DTYPE), _sds((b, nc, WIDTH, DH))],
        scratch_shapes=[pltpu.VMEM((b * HEADS, DH, DH), F32)])


def gd_scan_bwd(p, p0, local, lead, nw, ssave, dy):
    b, seq, _ = p.shape
    nc = seq // CHUNK + 1
    slab, per_chunk, state, const = _scan_specs(b, nc, True)

    def body(u_ref, w_ref, qe_ref, ke_ref, qk_ref, ea_ref, z_ref, u0_ref, w0_ref, qe0_ref, ke0_ref, qk0_ref, ea0_ref, p0_ref,
             nw_ref, ss_ref, dy_ref, du_ref, dw_ref, dqe_ref, dke_ref, dqk_ref, dea_ref, dz_ref, du0_ref, dw0_ref, dqe0_ref,
             dke0_ref, dqk0_ref, dea0_ref, dz0_ref, dnw_ref, dst):
        i = pl.program_id(0)
        c = nc - 1 - i

        @pl.when(i == 0)
        def _():
            dst[...] = jnp.zeros_like(dst)
            dnw_ref[...] = jnp.zeros_like(dnw_ref)

        args = _gd_scan_inputs(c, b, u_ref, w_ref, qe_ref, ke_ref, qk_ref, ea_ref, z_ref, u0_ref, w0_ref, qe0_ref, ke0_ref,
                               qk0_ref, ea0_ref, p0_ref)
        s_in = _load_states(ss_ref, b)
        _, vjp = jax.vjp(gd_scan, *args, nw_ref[...], s_in)
        dyv = jnp.where(c == 0, 0.0, _load_slabs(dy_ref, b))
        du, dw, dqe, dke, dqk, dea, dz, dnw, ds = vjp((dyv, dst[...]))
        dst[...] = ds
        dnw_ref[...] += dnw
        lane = lax.broadcasted_iota(jnp.int32, (1, AB_PAD), 1)
        dea_rows = [sum(jnp.where(lane == h, dea[j * HEADS + h], 0.0) for h in range(HEADS)) for j in range(b)]
        dqk_rows = [jnp.concatenate([dqk[j * HEADS + h] for h in range(HEADS)], axis=0) for j in range(b)]

        @pl.when(c > 0)
        def _():
            for ref, val in ((du_ref, du), (dw_ref, dw), (dqe_ref, dqe), (dke_ref, dke), (dz_ref, dz)):
                _store_slabs(ref, val, b)
            for j in range(b):
                dqk_ref[j] = dqk_rows[j]
                dea_ref[j] = dea_rows[j]

        @pl.when(c == 0)
        def _():
            for ref, val in ((du0_ref, du), (dw0_ref, dw), (dqe0_ref, dqe), (dke0_ref, dke), (dz0_ref, dz)):
                ref[...] = _sum_rows(val, b)
            dqk0_ref[...] = sum(dqk_rows[1:], dqk_rows[0])
            dea0_ref[...] = sum(dea_rows[1:], dea_rows[0])

    uu, ww, qe, ke, qk, ea = local
    return dict(
        body=body, args=(*local, p, *lead, p0, nw, ssave, dy),
        in_specs=[slab(0)] * 4 + [per_chunk(HEADS * CHUNK, CHUNK), per_chunk(1, AB_PAD), slab(3)] + [const(a) for a in lead]
        + [const(p0), const(nw), state, slab(0)],
        out_specs=[slab(0)] * 4 + [per_chunk(HEADS * CHUNK, CHUNK), per_chunk(1, AB_PAD), slab(0)] + [const(a) for a in lead]
        + [const(lead[0]), const(nw)],
        out_shape=[_sds((b, seq, WIDTH))] * 4 + [_sds(qk.shape), _sds(ea.shape), _sds((b, seq, WIDTH))]
        + [_sds(a.shape) for a in lead] + [_sds(lead[0].shape), _sds(nw.shape)],
        scratch_shapes=[pltpu.VMEM((b * HEADS, DH, DH), F32)])


def gd_local_bwd(p, p0, ab, cw, alog, dtb, cot, dz):
    b, seq, _ = p.shape
    rows = LOCAL_CHUNKS * CHUNK
    ng = seq // rows
    du, dw, dqe, dke, dqk, dea = cot

    def body(p_ref, halo_ref, p0_ref, ab_ref, cw_ref, al_ref, dt_ref, du_ref, dw_ref, dqe_ref, dke_ref, dqk_ref, dea_ref, dz_ref,
             dp_ref, dab_ref, dhalo0_ref, dcw_ref, dal_ref, ddt_ref, dhalo):
        i = pl.program_id(1)
        g = ng - 1 - i

        @pl.when(i == 0)
        def _():
            dhalo[...] = jnp.zeros_like(dhalo)

        @pl.when((pl.program_id(0) == 0) & (i == 0))
        def _():
            dcw_ref[...] = jnp.zeros_like(dcw_ref)
            dal_ref[...] = jnp.zeros_like(dal_ref)
            ddt_ref[...] = jnp.zeros_like(ddt_ref)

        _, vjp = jax.vjp(gd_local, _gd_window(g, p_ref, halo_ref, p0_ref), ab_ref[...], cw_ref[...], al_ref[...], dt_ref[...])
        dqk_all = jnp.concatenate([dqk_ref[c] for c in range(LOCAL_CHUNKS)], axis=0)
        deas = tuple(dea_ref[c] for c in range(LOCAL_CHUNKS))
        dxx, dab, dcw, dal, ddt = vjp((du_ref[...], dw_ref[...], dqe_ref[...], dke_ref[...], dqk_all, deas))
        dqkv = dxx[HALO:HALO + rows] + jnp.concatenate([jnp.zeros((rows - HALO, QKV), F32), dhalo[...]], axis=0)
        dhalo[...] = dxx[0:HALO]
        dhalo0_ref[...] = dxx[0:HALO]
        dp_ref[...] = jnp.concatenate([dqkv, dz_ref[...]], axis=1).astype(MXU_DTYPE)
        dab_ref[...] = dab.astype(MXU_DTYPE)
        dcw_ref[...] += dcw
        dal_ref[...] += dal
        ddt_ref[...] += ddt

    rg = lambda i: ng - 1 - i
    const = lambda a: pl.BlockSpec(a.shape, lambda s, i: (0, 0))
    slab = pl.BlockSpec((None, rows, WIDTH), lambda s, i: (s, rg(i), 0))
    wide = pl.BlockSpec((None, rows, 4 * WIDTH), lambda s, i: (s, rg(i), 0))
    gates = pl.BlockSpec((None, rows, AB_PAD), lambda s, i: (s, rg(i), 0))
    return pl.pallas_call(
        body, grid=(b, ng), name="gdn_local_bwd",
        in_specs=[wide, pl.BlockSpec((None, HALO, QKV), lambda s, i: (s, _halo_block(rg(i)), 0)), const(p0), gates, const(cw),
                  const(alog), const(dtb), slab, slab, slab, slab,
                  pl.BlockSpec((None, LOCAL_CHUNKS, HEADS * CHUNK, CHUNK), lambda s, i: (s, rg(i), 0, 0)),
                  pl.BlockSpec((None, LOCAL_CHUNKS, 1, AB_PAD), lambda s, i: (s, rg(i), 0, 0)), slab],
        out_specs=[wide, gates, pl.BlockSpec((None, HALO, QKV), lambda s, i: (s, 0, 0)), const(cw), const(alog), const(dtb)],
        out_shape=[_sds(p.shape, MXU_DTYPE), _sds(ab.shape, MXU_DTYPE), _sds((b, HALO, QKV)), _sds(cw.shape), _sds(alog.shape),
                   _sds(dtb.shape)],
        scratch_shapes=[pltpu.VMEM((HALO, QKV), F32)],
        compiler_params=_cparams("arbitrary", "arbitrary"),
    )(p, p, p0, ab, cw, alog, dtb, du, dw, dqe, dke, dqk, dea, dz)


def gd_local_bwd_lead(p0, ab0, cw, alog, dtb, cot, dz, dtail):
    def body(p0_ref, ab_ref, cw_ref, al_ref, dt_ref, du_ref, dw_ref, dqe_ref, dke_ref, dqk_ref, dea_ref, dz_ref, dtail_ref,
             dp_ref, dab_ref, dcw_ref, dal_ref, ddt_ref):
        _, vjp = jax.vjp(gd_local, _lead_window(p0_ref), ab_ref[...], cw_ref[...], al_ref[...], dt_ref[...])
        dxx, dab, dcw, dal, ddt = vjp((du_ref[...], dw_ref[...], dqe_ref[...], dke_ref[...], dqk_ref[...], (dea_ref[...],)))
        dqkv = dxx[HALO:HALO + CHUNK] + jnp.concatenate([jnp.zeros((CHUNK - HALO, QKV), F32), dtail_ref[...]], axis=0)
        dp_ref[...] = jnp.concatenate([dqkv, dz_ref[...]], axis=1).astype(MXU_DTYPE)
        dab_ref[...], dcw_ref[...], dal_ref[...], ddt_ref[...] = dab.astype(MXU_DTYPE), dcw, dal, ddt

    return pl.pallas_call(
        body, name="gdn_local_bwd_lead", in_specs=[VMEM_SPEC] * 13, out_specs=[VMEM_SPEC] * 5,
        out_shape=[_sds(p0.shape, MXU_DTYPE), _sds(ab0.shape, MXU_DTYPE), _sds(cw.shape), _sds(alog.shape), _sds(dtb.shape)],
        compiler_params=pltpu.CompilerParams(vmem_limit_bytes=VMEM_LIMIT),
    )(p0, ab0, cw, alog, dtb, *cot, dz, dtail)


def _position():
    return lax.axis_index("x"), lax.axis_index("y"), lax.axis_index("c")


def _exchange_blocks(bufs, send_sems, recv_sems):
    x, y, c = _position()
    me, sibling = (x, y, c), (x, y, 1 - c)
    chips = [(1 - x, y), (x, 1 - y), (1 - x, 1 - y)]
    per_buf = N_DEV - 1

    def copy(a, k, blk, to):
        rows = bufs[a].at[4 * blk[0] + 2 * blk[1] + blk[2]]
        return pltpu.make_async_remote_copy(src_ref=rows, dst_ref=rows, send_sem=send_sems.at[a * per_buf + k],
                                            recv_sem=recv_sems.at[a * per_buf + k], device_id=to, device_id_type=MESH)

    bufs_idx = range(len(bufs))
    first = [copy(a, 0, me, sibling) for a in bufs_idx] + [copy(a, 1 + j, me, (*chip, c)) for a in bufs_idx
                                                           for j, chip in enumerate(chips)]
    for cp in first:
        cp.start()
    passed = []
    for j, chip in enumerate(chips):
        for a in bufs_idx:
            copy(a, 1 + j, (*chip, c), me).wait_recv()
            passed.append(copy(a, 4 + j, (*chip, c), sibling))
            passed[-1].start()
    for a in bufs_idx:
        copy(a, 0, sibling, me).wait_recv()
        for j, chip in enumerate(chips):
            copy(a, 4 + j, (*chip, 1 - c), me).wait_recv()
    for cp in first + passed:
        cp.wait_send()


def _exchange_sems(n_bufs):
    return [pltpu.SemaphoreType.DMA((n_bufs * (N_DEV - 1),)), pltpu.SemaphoreType.DMA((n_bufs * (N_DEV - 1),))]


def gather_weights(w_in_t, w_out, small, pad_rows):
    rows, _, cols = w_in_t.shape

    def body(wi_ref, wo_ref, sm_ref, wi_out, wo_out, sm_out, wi_buf, send_sems, recv_sems):
        x, y, c = _position()
        me = 4 * x + 2 * y + c
        wi_buf[me] = wi_ref[:, 0, :].astype(MXU_DTYPE)
        wo_out[me] = wo_ref[...].astype(MXU_DTYPE)
        sm_out[me] = sm_ref[...]
        _exchange_blocks([wi_buf, wo_out, sm_out], send_sems, recv_sems)
        for d in range(N_DEV):
            wi_out[pl.ds(d * rows, rows), :] = wi_buf[d]
        wi_out[pl.ds(N_DEV * rows, pad_rows), :] = jnp.zeros((pad_rows, cols), MXU_DTYPE)

    return pl.pallas_call(
        body, name="gather_weights", in_specs=[VMEM_SPEC] * 3, out_specs=[VMEM_SPEC] * 3,
        out_shape=[jax.ShapeDtypeStruct((N_DEV * rows + pad_rows, cols), MXU_DTYPE),
                   jax.ShapeDtypeStruct((N_DEV,) + w_out.shape, MXU_DTYPE), jax.ShapeDtypeStruct((N_DEV,) + small.shape, F32)],
        scratch_shapes=[pltpu.VMEM((N_DEV, rows, cols), MXU_DTYPE)] + _exchange_sems(3),
        compiler_params=pltpu.CompilerParams(vmem_limit_bytes=VMEM_LIMIT))(w_in_t, w_out, small)


def reduce_gradients(tensors, small, name):
    n_t = len(tensors)
    arrays = [a for parts, _ in tensors for a, _ in parts]
    first_array = [sum(len(parts) for parts, _ in tensors[:t]) for t in range(n_t)]

    def pieces(t, j):
        parts, block_rows = tensors[t]
        out, base = [], 0
        for pi, (_, valid) in enumerate(parts):
            lo, hi = max(j * block_rows, base), min((j + 1) * block_rows, base + valid)
            if lo < hi:
                out.append((first_array[t] + pi, lo - base, lo - j * block_rows, hi - lo))
            base += valid
        return out

    def body(*refs):
        n_a = len(arrays)
        in_refs, small_ref = refs[:n_a], refs[n_a]
        out_refs, small_sum = refs[n_a + 1:n_a + 1 + n_t], refs[n_a + 1 + n_t]
        bufs, small_buf = refs[n_a + 2 + n_t:n_a + 2 + 5 * n_t], refs[n_a + 2 + 5 * n_t]
        s1_sems, r1_sems, s2_sems, r2_sems, small_send, small_recv = refs[n_a + 3 + 5 * n_t:]
        x, y, c = _position()
        chip = 2 * x + y

        def put(t, dst, j, add=None):
            for ai, src_row, dst_row, size in pieces(t, j):
                v = in_refs[ai][pl.ds(src_row, size), :]
                if add is not None:
                    v = v + add[pl.ds(dst_row, size), :].astype(F32)
                dst[pl.ds(dst_row, size), :] = v.astype(dst.dtype)

        def swap(t, k):
            send1, recv1 = bufs[4 * t], bufs[4 * t + 1]
            return pltpu.make_async_remote_copy(src_ref=send1.at[k], dst_ref=recv1.at[k], send_sem=s1_sems.at[4 * t + k],
                                                recv_sem=r1_sems.at[4 * t + k], device_id=(x, y, 1 - c), device_id_type=MESH)

        def to_chip(t, k, slot):
            send2, recv2 = bufs[4 * t + 2], bufs[4 * t + 3]
            return pltpu.make_async_remote_copy(src_ref=send2.at[k], dst_ref=recv2.at[slot], send_sem=s2_sems.at[4 * t + k],
                                                recv_sem=r2_sems.at[4 * t + slot], device_id=(k >> 1, k & 1, c),
                                                device_id_type=MESH)

        for t in range(n_t):
            for j in range(N_DEV):
                @pl.when((j & 1) != c)
                def _():
                    put(t, bufs[4 * t].at[j >> 1], j)
            for k in range(4):
                swap(t, k).start()

        small_buf[4 * x + 2 * y + c] = small_ref[...]
        _exchange_blocks([small_buf], small_send, small_recv)
        total = small_buf[0]
        for d in range(1, N_DEV):
            total = total + small_buf[d]
        small_sum[...] = total

        for t in range(n_t):
            recv1 = bufs[4 * t + 1]
            for k in range(4):
                swap(t, k).wait_recv()
                for j in (2 * k, 2 * k + 1):
                    @pl.when(((j & 1) == c) & (k != chip))
                    def _():
                        put(t, bufs[4 * t + 2].at[k], j, add=recv1.at[k])
                        to_chip(t, k, chip).start()

                    @pl.when(((j & 1) == c) & (k == chip))
                    def _():
                        put(t, out_refs[t], j, add=recv1.at[k])

        for t in range(n_t):
            for k in range(4):
                @pl.when(k != chip)
                def _():
                    to_chip(t, k, k).wait_recv()
                    out_refs[t][...] += bufs[4 * t + 3][k].astype(F32)

        for t in range(n_t):
            for k in range(4):
                @pl.when(k != chip)
                def _():
                    to_chip(t, k, chip).wait_send()
                swap(t, k).wait_send()

    scratch, out_shape = [], []
    for parts, block_rows in tensors:
        cols = parts[0][0].shape[1]
        scratch += [pltpu.VMEM((4, block_rows, cols), MXU_DTYPE)] * 4
        out_shape.append(jax.ShapeDtypeStruct((block_rows, cols), F32))
    out_shape.append(jax.ShapeDtypeStruct(small.shape, F32))
    scratch += [pltpu.VMEM((N_DEV,) + small.shape, F32)] + [pltpu.SemaphoreType.DMA((4 * n_t,))] * 4 + _exchange_sems(1)
    return pl.pallas_call(
        body, name=name, in_specs=[VMEM_SPEC] * (len(arrays) + 1), out_specs=[VMEM_SPEC] * (n_t + 1), out_shape=out_shape,
        scratch_shapes=scratch, compiler_params=pltpu.CompilerParams(vmem_limit_bytes=VMEM_LIMIT),
    )(*arrays, small)


def _adamw_step(w, g, m, v):
    mn = ADAM_B1 * m + (1.0 - ADAM_B1) * g
    vn = ADAM_B2 * v + (1.0 - ADAM_B2) * jnp.square(g)
    m_hat = mn / (1.0 - ADAM_B1 ** ADAM_STEP)
    v_hat = vn / (1.0 - ADAM_B2 ** ADAM_STEP)
    return -ADAM_LR * (m_hat / (jnp.sqrt(v_hat) + ADAM_EPS) + ADAM_WD * w), mn, vn


def adamw(w, g, m, v, name):
    rows, cols = w.shape
    tr = 256 if rows % 256 == 0 else rows

    def body(w_ref, g_ref, m_ref, v_ref, d_ref, nm_ref, nv_ref):
        d_ref[...], nm_ref[...], nv_ref[...] = _adamw_step(w_ref[...], g_ref[...], m_ref[...], v_ref[...])

    spec = pl.BlockSpec((tr, cols), lambda i: (i, 0))
    shape = jax.ShapeDtypeStruct((rows, cols), F32)
    return pl.pallas_call(body, grid=(rows // tr,), name=name, in_specs=[spec] * 4, out_specs=[spec] * 3,
                          out_shape=[shape] * 3, compiler_params=_cparams("arbitrary"))(w, g, m, v)


def adamw_w_in(w, g_t, m, v):
    def body(w_ref, g_ref, m_ref, v_ref, go_ref, d_ref, nm_ref, nv_ref):
        g = g_ref[...]
        go_ref[:, 0, :] = g
        d_ref[:, 0, :], nm_ref[:, 0, :], nv_ref[:, 0, :] = _adamw_step(w_ref[:, 0, :], g, m_ref[:, 0, :], v_ref[:, 0, :])

    return pl.pallas_call(body, name="adamw_w_in", in_specs=[VMEM_SPEC] * 4, out_specs=[VMEM_SPEC] * 4,
                          out_shape=[jax.ShapeDtypeStruct(w.shape, F32)] * 4,
                          compiler_params=pltpu.CompilerParams(vmem_limit_bytes=VMEM_LIMIT))(w, g_t, m, v)


def _pad_rows(a, rows=8):
    return jnp.pad(a, ((0, rows - a.shape[0]), (0, 0)))


def _pad_lanes(a, lanes=128):
    return jnp.pad(a, ((0, 0), (0, lanes - a.shape[1])))


def kernel(x, meta_tokens, norm_w, w_in, conv_w, hg_lb_logits, hg_norm_w, gdn_A_log, gdn_dt_bias, gdn_norm_w, w_out, final_norm_w, loss_target, m_meta_tokens, m_norm_w, m_w_in, m_conv_w, m_hg_lb_logits, m_hg_norm_w, m_gdn_A_log, m_gdn_dt_bias, m_gdn_norm_w, m_w_out, m_final_norm_w, v_meta_tokens, v_norm_w, v_w_in, v_conv_w, v_hg_lb_logits, v_hg_norm_w, v_gdn_A_log, v_gdn_dt_bias, v_gdn_norm_w, v_w_out, v_final_norm_w):
    b, seq, _ = x.shape
    n = b * seq
    dev = 4 * lax.axis_index("x") + 2 * lax.axis_index("y") + lax.axis_index("c")
    col_shard = IN_COLS // N_DEV

    small_w = jnp.concatenate([_pad_lanes(meta_tokens, 256), _pad_rows(_pad_lanes(conv_w[0], 256))], axis=0)
    w_t, w_out_g, small_g = gather_weights(jnp.transpose(w_in, (2, 0, 1)), w_out[0], small_w, AB_PAD - 2 * HEADS)
    meta_g = small_g[:, 0:N_META, 0:D_MODEL // N_DEV]
    conv_g = small_g[:, N_META:N_META + CONV_TAPS, 0:QKV // N_DEV]
    w_out_full = w_out_g.reshape(2 * WIDTH, D_MODEL)
    cw = jnp.transpose(conv_g, (1, 0, 2)).reshape(CONV_TAPS, QKV)
    meta = jnp.transpose(meta_g, (1, 0, 2)).reshape(N_META, D_MODEL)
    alog = _pad_lanes(gdn_A_log)
    dtb = _pad_lanes(gdn_dt_bias)
    fw = final_norm_w.reshape(1, D_MODEL)

    h0 = jnp.concatenate([jnp.zeros((CHUNK - N_META, D_MODEL), F32), meta], axis=0)
    x2 = x.reshape(n, D_MODEL)
    phg, pgd, pab, phg0, pgd0, pab0, u0 = in_proj(x2, h0, norm_w, w_t)
    phg3, pgd3, pab3 = phg.reshape(b, seq, 4 * WIDTH), pgd.reshape(b, seq, 4 * WIDTH), pab.reshape(b, seq, AB_PAD)
    nc = seq // CHUNK + 1
    hg_loc = hg_local_fwd(phg3, hg_lb_logits)
    hg_lead = hg_local_lead(phg0, hg_lb_logits)
    gd_loc = gd_local_fwd(pgd3, pgd0, pab3, cw, alog, dtb)
    gd_lead = gd_local_lead(pgd0, pab0, cw, alog, dtb)
    (y_hg, s_hg), (y_gd, s_gd) = run_scans([hg_scan_fwd(phg3, phg0, hg_loc, hg_lead, hg_norm_w),
                                            gd_scan_fwd(pgd3, pgd0, gd_loc, gd_lead, gdn_norm_w)], nc, "scans")

    dh2, dy_hg, dy_gd, g_w_out, loss_part, g_fw = out_proj_loss(
        x2, loss_target.reshape(n, D_MODEL), y_hg.reshape(n, WIDTH), y_gd.reshape(n, WIDTH), w_out_full, fw)

    hb, gb = run_scans([hg_scan_bwd(phg3, phg0, hg_loc, hg_lead, hg_norm_w, s_hg, dy_hg.reshape(b, seq, WIDTH)),
                        gd_scan_bwd(pgd3, pgd0, gd_loc, gd_lead, gdn_norm_w, s_gd, dy_gd.reshape(b, seq, WIDTH))],
                       nc, "scans_bwd")
    dphg, g_lb = hg_local_bwd(phg3, hg_lb_logits, *hb[0:6])
    dphg0, g_lb0 = hg_local_bwd_lead(phg0, hg_lb_logits, *hb[6:12])
    g_hg_nw = hb[12]
    dpgd, dpab, dtail, g_cw, g_alog, g_dtb = gd_local_bwd(pgd3, pgd0, pab3, cw, alog, dtb, gb[0:6], gb[6])
    dpgd0, dpab0, g_cw0, g_alog0, g_dtb0 = gd_local_bwd_lead(pgd0, pab0, cw, alog, dtb, gb[7:13], gb[13], dtail.sum(0))
    g_gd_nw = gb[14]
    dphg, dpgd, dpab = dphg.reshape(n, 4 * WIDTH), dpgd.reshape(n, 4 * WIDTH), dpab.reshape(n, AB_PAD)

    grad_x, dh0, g_nw, g_w_hg, g_w_gd, g_w_ab = in_proj_bwd(dphg, dpgd, dpab, w_t, x2, dh2, norm_w, h0, u0, dphg0, dpgd0, dpab0)

    small = jnp.concatenate([
        g_nw.reshape(8, 128), (g_lb + g_lb0).reshape(8, 128), _pad_rows(g_hg_nw), _pad_rows(g_alog + g_alog0),
        _pad_rows(g_dtb + g_dtb0), _pad_rows(g_gd_nw), g_fw.reshape(8, 128), (g_cw + g_cw0).reshape(48, 128),
        dh0[CHUNK - N_META:CHUNK].reshape(128, 128), loss_part], axis=0)
    g_w_in_t, g_w_out, small = reduce_gradients(
        [([(g_w_hg, 4 * WIDTH), (g_w_gd, 4 * WIDTH), (g_w_ab, 2 * HEADS)], col_shard),
         ([(g_w_out, 2 * WIDTH)], (2 * WIDTH) // N_DEV)], small, "reduce_gradients")
    g_norm_w = small[0:8].reshape(1, D_MODEL)
    g_lb = small[8:16].reshape(2, WIDTH)
    g_hg_nw = small[16:17]
    g_alog = small[24:25, 0:HEADS]
    g_dtb = small[32:33, 0:HEADS]
    g_gd_nw = small[40:41]
    g_fw = small[48:56].reshape(1, D_MODEL)
    g_cw_full = small[56:104].reshape(CONV_TAPS, QKV)
    g_meta_full = small[104:232].reshape(N_META, D_MODEL)
    loss = small[232, 0]
    g_conv = lax.dynamic_slice_in_dim(g_cw_full, dev * (QKV // N_DEV), QKV // N_DEV, axis=1)
    g_meta = lax.dynamic_slice_in_dim(g_meta_full, dev * (D_MODEL // N_DEV), D_MODEL // N_DEV, axis=1)

    names = ["meta_tokens", "norm_w", "w_in", "conv_w", "hg_lb_logits", "hg_norm_w", "gdn_A_log", "gdn_dt_bias",
             "gdn_norm_w", "w_out", "final_norm_w"]
    weights = [meta_tokens, norm_w, w_in, conv_w, hg_lb_logits, hg_norm_w, gdn_A_log, gdn_dt_bias, gdn_norm_w, w_out,
               final_norm_w]
    moms = [m_meta_tokens, m_norm_w, m_w_in, m_conv_w, m_hg_lb_logits, m_hg_norm_w, m_gdn_A_log, m_gdn_dt_bias,
            m_gdn_norm_w, m_w_out, m_final_norm_w]
    vars_ = [v_meta_tokens, v_norm_w, v_w_in, v_conv_w, v_hg_lb_logits, v_hg_norm_w, v_gdn_A_log, v_gdn_dt_bias,
             v_gdn_norm_w, v_w_out, v_final_norm_w]
    grads2d = [g_meta, g_norm_w, g_w_in_t, g_conv, g_lb, g_hg_nw, g_alog, g_dtb, g_gd_nw, g_w_out, g_fw]
    grads, deltas, new_ms, new_vs = [], [], [], []
    for nm, w, g2, m, v in zip(names, weights, grads2d, moms, vars_):
        if nm == "w_in":
            to3, back = (lambda a: jnp.transpose(a, (2, 0, 1))), (lambda a: jnp.transpose(a, (1, 2, 0)))
            g2, d, nm_, nv_ = adamw_w_in(to3(w), g2, to3(m), to3(v))
        else:
            to2d, back = (lambda a, s=g2.shape: a.reshape(s)), (lambda a, s=w.shape: a.reshape(s))
            d, nm_, nv_ = adamw(to2d(w), g2, to2d(m), to2d(v), "adamw_" + nm)
        grads.append(back(g2))
        deltas.append(back(d))
        new_ms.append(back(nm_))
        new_vs.append(back(nv_))
    return (loss, grad_x.reshape(x.shape), *grads, *deltas, *new_ms, *new_vs)
```

```python
import jax
import jax.numpy as jnp
from jax import lax
from jax.experimental import pallas as pl
from jax.experimental.pallas import tpu as pltpu

F32 = jnp.float32
BF16 = jnp.bfloat16
MXU_DTYPE = BF16

D_MODEL = 1024
N_META = 16
CHUNK = 64
SUB = 16
HEADS = 4
DH = 128
WIDTH = HEADS * DH
QKV = 3 * WIDTH
CONV_TAPS = 4
HALO = 8
EPS = 1e-6
IN_COLS = 4 * WIDTH + 4 * WIDTH + 2 * HEADS
AB_PAD = 128
N_DEV = 8
LOCAL_CHUNKS = 4
VMEM_LIMIT = 56 * 1024 * 1024
VMEM_LIMIT_LARGE = 60 * 1024 * 1024

ADAM_LR = 0.001
ADAM_B1 = 0.9
ADAM_B2 = 0.999
ADAM_EPS = 1e-08
ADAM_WD = 0.01
ADAM_STEP = 10

VMEM_SPEC = pl.BlockSpec(memory_space=pltpu.VMEM)
MESH = pl.DeviceIdType.MESH


def _mm_tn(a, b):
    return lax.dot_general(a.astype(MXU_DTYPE), b.astype(MXU_DTYPE), (((0,), (0,)), ((), ())), preferred_element_type=F32)


def _bmm(a, b):
    return lax.dot_general(a.astype(MXU_DTYPE), b.astype(MXU_DTYPE), (((2,), (1,)), ((0,), (0,))), preferred_element_type=F32)


def _bmm_nt(a, b):
    return lax.dot_general(a.astype(MXU_DTYPE), b.astype(MXU_DTYPE), (((2,), (2,)), ((0,), (0,))), preferred_element_type=F32)


def _bmm_tn(a, b):
    return lax.dot_general(a.astype(MXU_DTYPE), b.astype(MXU_DTYPE), (((1,), (1,)), ((0,), (0,))), preferred_element_type=F32)


def _iota2(n, m):
    return lax.broadcasted_iota(jnp.int32, (n, m), 0), lax.broadcasted_iota(jnp.int32, (n, m), 1)


def _silu(x):
    return x * jax.nn.sigmoid(x)


def _gated_norm(o, z, nw):
    return o * lax.rsqrt(jnp.mean(o * o, axis=-1, keepdims=True) + EPS) * nw * _silu(z)


def _heads(a, nb):
    return jnp.stack([a[c * CHUNK:(c + 1) * CHUNK, h * DH:(h + 1) * DH] for c in range(nb) for h in range(HEADS)], axis=0)


def _unheads(a3, nb):
    return jnp.concatenate(
        [jnp.concatenate([a3[c * HEADS + h] for h in range(HEADS)], axis=1) for c in range(nb)], axis=0)


def _split3(x):
    hi = x.astype(BF16)
    r1 = x - hi.astype(F32)
    mid = r1.astype(BF16)
    return hi, mid, (r1 - mid.astype(F32)).astype(BF16)


def _select_mm(pattern, n_out, n_in, transposed, x):
    rows, inner = (n_in, n_out) if transposed else (n_out, n_in)
    r, c = _iota2(rows, 3 * inner)
    c = c - jnp.where(c >= inner, inner, 0) - jnp.where(c >= 2 * inner, inner, 0)
    s = jnp.where(pattern(c, r) if transposed else pattern(r, c), 1.0, 0.0).astype(BF16)
    return jnp.dot(s, jnp.concatenate(_split3(x), axis=0), preferred_element_type=F32)


def _select_rows(pattern, n_out, x):
    @jax.custom_vjp
    def apply(v):
        return _select_mm(pattern, n_out, CHUNK, False, v)

    apply.defvjp(lambda v: (_select_mm(pattern, n_out, CHUNK, False, v), None),
                 lambda _, d: (_select_mm(pattern, n_out, CHUNK, True, d),))
    return apply(x)


def _cumsum_chunks(x, nb):
    return jnp.concatenate([_select_rows(lambda i, j: j <= i, CHUNK, x[c * CHUNK:(c + 1) * CHUNK]) for c in range(nb)], axis=0)


HG_LEVELS = 6


def _hg_sums(i, j):
    lvl, t = i >> HG_LEVELS, i & (CHUNK - 1)
    last = t
    for l in range(1, HG_LEVELS + 1):
        width = HG_LEVELS + 1 - l
        last = jnp.where(lvl == l, ((t >> width) << width) + (CHUNK >> l) - 1, last)
    return j <= last


def hg_local(p, logits):
    nb = p.shape[0] // CHUNK
    l0, l1 = logits[0:1], logits[1:2]
    mx = jnp.maximum(l0, l1)
    e0, e1 = jnp.exp(l0 - mx), jnp.exp(l1 - mx)
    lb = e0 / (e0 + e1)
    q = _silu(p[:, 0:WIDTH])
    f = lb + (1.0 - lb) * jax.nn.sigmoid(p[:, WIDTH:2 * WIDTH])
    k = 1.0 - f
    logf = jnp.log(f)
    sums = [_select_rows(_hg_sums, (HG_LEVELS + 1) * CHUNK, logf[c * CHUNK:(c + 1) * CHUNK]) for c in range(nb)]
    level = lambda l: _heads(jnp.concatenate([s[l * CHUNK:(l + 1) * CHUNK] for s in sums], axis=0), nb)
    q3, k3, v3, g3 = _heads(q, nb), _heads(k, nb), _heads(p[:, 2 * WIDTH:3 * WIDTH], nb), level(0)
    r, c = _iota2(CHUNK, CHUNK)
    row = lax.broadcasted_iota(jnp.int32, (CHUNK, DH), 0)
    a = jnp.where(r == c, _bmm_nt(q3, k3), 0.0)
    for l in range(1, HG_LEVELS + 1):
        sh = HG_LEVELS - l
        qk = jnp.where(((row >> sh) & 1) == 1, q3, k3) * jnp.exp(-jnp.abs(g3 - level(l)))
        pair = ((r >> (sh + 1)) == (c >> (sh + 1))) & (((r >> sh) & 1) == 1) & (((c >> sh) & 1) == 0)
        a = a + jnp.where(pair, _bmm_nt(qk, qk), 0.0)
    o = _bmm(a, v3)
    glast = g3[:, CHUNK - 1:CHUNK, :]
    egs = tuple(jnp.concatenate([jnp.exp(glast[c * HEADS + h]) for h in range(HEADS)], axis=1) for c in range(nb))
    return _unheads(q3 * jnp.exp(g3), nb), _unheads(k3 * jnp.exp(glast - g3), nb), _unheads(o, nb), egs


def hg_scan(q_in, k_out, v, eg, o_intra, z, nw, st):
    o = o_intra + _bmm_nt(q_in, st)
    return _gated_norm(o, z, nw), st * eg + _bmm_tn(v, k_out)


def _tri_y_impl(a):
    r, c = _iota2(CHUNK, CHUNK)
    same16 = (r // SUB) == (c // SUB)
    same32 = (r // (2 * SUB)) == (c // (2 * SUB))
    a0 = jnp.where(same16, a, 0.0)
    y = -a0
    pw = _bmm(a0, a0)
    for _ in range(2):
        y = y + pw + _bmm(y, pw)
        pw = _bmm(pw, pw)
    y = y + pw + _bmm(y, pw)
    for ak in (jnp.where(same32 & jnp.logical_not(same16), a, 0.0), jnp.where(same32, 0.0, a)):
        m = ak + _bmm(y, ak)
        y = y - (m + _bmm(m, y))
    return y


@jax.custom_vjp
def _tri_y(a):
    return _tri_y_impl(a)


def _tri_y_fwd(a):
    y = _tri_y_impl(a)
    return y, y


def _tri_y_bwd(y, dy):
    n = dy + _bmm_tn(y, dy)
    return (-(n + _bmm_nt(n, y)),)


_tri_y.defvjp(_tri_y_fwd, _tri_y_bwd)


def _saved_inverse(y):
    @jax.custom_vjp
    def inverse(a):
        return y

    inverse.defvjp(lambda a: (y, None), lambda _, dy: _tri_y_bwd(y, dy))
    return inverse


def _head_rows(a3, nb):
    return jnp.concatenate([a3[g] for g in range(nb * HEADS)], axis=0)


def _rows_down(x, s):
    rows = x.shape[0]

    @jax.custom_vjp
    def rotate(v):
        return pltpu.roll(v, s, 0)

    rotate.defvjp(lambda v: (pltpu.roll(v, s, 0), None), lambda _, d: (pltpu.roll(d, rows - s, 0),))
    return rotate(x)


def gd_local(xx, ab, cw, alog, dtb, inverse=_tri_y):
    n = ab.shape[0]
    nb = n // CHUNK
    conv = cw[CONV_TAPS - 1:CONV_TAPS] * xx[HALO:HALO + n]
    for j in range(CONV_TAPS - 1):
        conv = conv + cw[j:j + 1] * _rows_down(xx, CONV_TAPS - 1 - j)[HALO:HALO + n]
    act = _silu(conv)
    x = ab + dtb
    g_all = -jnp.exp(alog) * (jnp.maximum(x, 0.0) + jnp.log1p(jnp.exp(-jnp.abs(x))))
    beta_all = jax.nn.sigmoid(ab)
    gam_all = _cumsum_chunks(g_all, nb)
    q3, k3, v3 = _heads(act[:, 0:WIDTH], nb), _heads(act[:, WIDTH:2 * WIDTH], nb), _heads(act[:, 2 * WIDTH:QKV], nb)
    q3 = q3 * lax.rsqrt(jnp.sum(q3 * q3, axis=-1, keepdims=True) + EPS) * (DH ** -0.5)
    k3 = k3 * lax.rsqrt(jnp.sum(k3 * k3, axis=-1, keepdims=True) + EPS)
    pairs = [(c, h) for c in range(nb) for h in range(HEADS)]
    beta = jnp.stack([beta_all[c * CHUNK:(c + 1) * CHUNK, HEADS + h:HEADS + h + 1] for c, h in pairs], axis=0)
    gam = jnp.stack([gam_all[c * CHUNK:(c + 1) * CHUNK, h:h + 1] for c, h in pairs], axis=0)
    gam_t = [gam_all[c * CHUNK:(c + 1) * CHUNK].T for c in range(nb)]
    gam_row = jnp.stack([gam_t[c][h:h + 1, :] for c, h in pairs], axis=0)
    glast = gam[:, CHUNK - 1:CHUNK, :]
    r, c = _iota2(CHUNK, CHUNK)
    dec = jnp.exp(jnp.where(c < r, gam - gam_row, -jnp.inf))
    y = inverse(beta * _bmm_nt(k3, k3) * dec)
    eg = jnp.exp(gam)
    rhs = jnp.concatenate([beta * v3, (beta * eg) * k3], axis=2)
    sol = rhs + _bmm(y, rhs)
    qk = _bmm_nt(q3, k3) * jnp.where(r == c, 1.0, dec)
    eas = tuple(jnp.exp(gam_all[(c + 1) * CHUNK - 1:(c + 1) * CHUNK]) for c in range(nb))
    return (_unheads(sol[:, :, 0:DH], nb), _unheads(sol[:, :, DH:2 * DH], nb), _unheads(q3 * eg, nb),
            _unheads(k3 * jnp.exp(glast - gam), nb), _head_rows(qk, nb), eas), _head_rows(y, nb)


def gd_scan(uu, ww, qe, ke, qk, ea, z, nw, s):
    u = uu - _bmm(ww, s)
    o = _bmm(qe, s) + _bmm(qk, u)
    return _gated_norm(o, z, nw), ea * s + _bmm_tn(ke, u)


def _cparams(*sem):
    return pltpu.CompilerParams(dimension_semantics=sem, vmem_limit_bytes=VMEM_LIMIT)


def _row_tile(n):
    for t in (512, 256, 128, 64):
        if n % t == 0:
            return t
    raise ValueError(f"unsupported token count {n}")


def _w_in_specs():
    once = pl.Buffered(1)
    return [pl.BlockSpec((4 * WIDTH, D_MODEL), lambda *i: (0, 0), pipeline_mode=once),
            pl.BlockSpec((4 * WIDTH, D_MODEL), lambda *i: (1, 0), pipeline_mode=once),
            pl.BlockSpec((AB_PAD, D_MODEL), lambda *i: (8 * WIDTH // AB_PAD, 0), pipeline_mode=once)]


def in_proj(h, h0, norm_w, w_t):
    n = h.shape[0]
    tm = _row_tile(n)
    nt = (((1,), (1,)), ((), ()))

    def body(h_ref, h0_ref, nw_ref, whg_ref, wgd_ref, wab_ref, phg_ref, pgd_ref, pab_ref, phg0_ref, pgd0_ref, pab0_ref, u0_ref):
        def project(x, hg_ref, gd_ref, ab_ref):
            u = (x * lax.rsqrt(jnp.mean(x * x, axis=-1, keepdims=True) + EPS) * nw_ref[...]).astype(MXU_DTYPE)
            hg_ref[...] = lax.dot_general(u, whg_ref[...], nt, preferred_element_type=F32)
            gd_ref[...] = lax.dot_general(u, wgd_ref[...], nt, preferred_element_type=F32)
            ab_ref[...] = lax.dot_general(u, wab_ref[...], nt, preferred_element_type=F32)
            return u

        @pl.when(pl.program_id(0) == 0)
        def _():
            u0_ref[...] = project(h0_ref[...], phg0_ref, pgd0_ref, pab0_ref)

        project(h_ref[...], phg_ref, pgd_ref, pab_ref)

    n0 = h0.shape[0]
    row = lambda w: pl.BlockSpec((tm, w), lambda i: (i, 0))
    lead = lambda w: pl.BlockSpec((n0, w), lambda i: (0, 0))
    widths = [4 * WIDTH, 4 * WIDTH, AB_PAD]
    return pl.pallas_call(
        body, grid=(n // tm,), name="in_proj",
        in_specs=[row(D_MODEL), lead(D_MODEL), pl.BlockSpec(norm_w.shape, lambda i: (0, 0))] + _w_in_specs(),
        out_specs=[row(w) for w in widths] + [lead(w) for w in widths] + [lead(D_MODEL)],
        out_shape=[jax.ShapeDtypeStruct((n, w), F32) for w in widths] + [jax.ShapeDtypeStruct((n0, w), F32) for w in widths]
        + [jax.ShapeDtypeStruct((n0, D_MODEL), MXU_DTYPE)],
        compiler_params=_cparams("arbitrary"),
    )(h, h0, norm_w, w_t, w_t, w_t)


def out_proj_loss(x, tgt, y_hg, y_gd, w_out, fw):
    n = x.shape[0]
    tm = _row_tile(n)
    inv_d = 1.0 / D_MODEL

    def body(x_ref, t_ref, yh_ref, yg_ref, w_ref, fw_ref, dh_ref, dyh_ref, dyg_ref, dw_ref, loss_ref, dfw_ref):
        @pl.when(pl.program_id(0) == 0)
        def _():
            dw_ref[...] = jnp.zeros_like(dw_ref)
            loss_ref[...] = jnp.zeros_like(loss_ref)
            dfw_ref[...] = jnp.zeros_like(dfw_ref)

        yh, yg = yh_ref[...], yg_ref[...]
        wa, wb = w_ref[0:WIDTH, :], w_ref[WIDTH:2 * WIDTH, :]
        h2 = x_ref[...] + jnp.dot(yh, wa, preferred_element_type=F32) + jnp.dot(yg, wb, preferred_element_type=F32)
        r2 = lax.rsqrt(jnp.mean(h2 * h2, axis=-1, keepdims=True) + EPS)
        nrm = h2 * r2
        fwv = fw_ref[...]
        err = nrm * fwv - t_ref[...]
        loss_ref[...] += jnp.full(loss_ref.shape, 0.5 * inv_d * jnp.sum(err * err), F32)
        dout = err * inv_d
        dfw_ref[...] += jnp.sum(dout * nrm, axis=0, keepdims=True)
        dn = dout * fwv
        dh2 = r2 * (dn - nrm * jnp.mean(dn * nrm, axis=-1, keepdims=True))
        dh_ref[...] = dh2
        dhb = dh2.astype(MXU_DTYPE)
        dyh_ref[...] = lax.dot_general(dhb, wa, (((1,), (1,)), ((), ())), preferred_element_type=F32)
        dyg_ref[...] = lax.dot_general(dhb, wb, (((1,), (1,)), ((), ())), preferred_element_type=F32)
        dw_ref[0:WIDTH, :] += lax.dot_general(yh, dhb, (((0,), (0,)), ((), ())), preferred_element_type=F32)
        dw_ref[WIDTH:2 * WIDTH, :] += lax.dot_general(yg, dhb, (((0,), (0,)), ((), ())), preferred_element_type=F32)

    row = lambda w: pl.BlockSpec((tm, w), lambda i: (i, 0))
    full = lambda s: pl.BlockSpec(s, lambda i: (0, 0))
    return pl.pallas_call(
        body, grid=(n // tm,), name="out_proj_loss",
        in_specs=[row(D_MODEL), row(D_MODEL), row(WIDTH), row(WIDTH), full(w_out.shape), full(fw.shape)],
        out_specs=[row(D_MODEL), row(WIDTH), row(WIDTH), full((2 * WIDTH, D_MODEL)), full((8, 128)), full((1, D_MODEL))],
        out_shape=[jax.ShapeDtypeStruct((n, D_MODEL), F32), jax.ShapeDtypeStruct((n, WIDTH), F32),
                   jax.ShapeDtypeStruct((n, WIDTH), F32), jax.ShapeDtypeStruct((2 * WIDTH, D_MODEL), F32),
                   jax.ShapeDtypeStruct((8, 128), F32), jax.ShapeDtypeStruct((1, D_MODEL), F32)],
        compiler_params=_cparams("arbitrary"),
    )(x, tgt, y_hg, y_gd, w_out, fw)


def in_proj_bwd(dphg, dpgd, dpab, w_t, h, dh2, norm_w, h0, u0, dphg0, dpgd0, dpab0):
    n = h.shape[0]
    tm = _row_tile(n)
    steps = n // tm

    def body(dphg_ref, dpgd_ref, dpab_ref, whg_ref, wgd_ref, wab_ref, h_ref, dh2_ref, nw_ref, h0_ref, u0_ref, d0hg_ref,
             d0gd_ref, d0ab_ref, dx_ref, dx0_ref, dnw_ref, ghg_ref, ggd_ref, gab_ref, acc_hg, acc_gd, acc_ab):
        i = pl.program_id(0)
        nwv = nw_ref[...]

        def norm_bwd(dps, x):
            du = jnp.dot(dps[0], whg_ref[...], preferred_element_type=F32)
            du += jnp.dot(dps[1], wgd_ref[...], preferred_element_type=F32)
            du += jnp.dot(dps[2], wab_ref[...], preferred_element_type=F32)
            r = lax.rsqrt(jnp.mean(x * x, axis=-1, keepdims=True) + EPS)
            nrm = x * r
            dn = du * nwv
            return r * (dn - nrm * jnp.mean(dn * nrm, axis=-1, keepdims=True)), nrm, jnp.sum(du * nrm, axis=0, keepdims=True)

        def accumulate(dps, u, first):
            for acc, dp in zip((acc_hg, acc_gd, acc_ab), dps):
                step = min(acc.shape[0], 512)
                for lo in range(0, acc.shape[0], step):
                    part = _mm_tn(dp[:, lo:lo + step], u)
                    acc[lo:lo + step, :] = part if first else acc[lo:lo + step, :] + part

        @pl.when(i == 0)
        def _():
            dps0 = (d0hg_ref[...], d0gd_ref[...], d0ab_ref[...])
            dx0_ref[...], _, dnw_ref[...] = norm_bwd(dps0, h0_ref[...])
            accumulate(dps0, u0_ref[...], True)

        dps = (dphg_ref[...], dpgd_ref[...], dpab_ref[...])
        dx, nrm, dnw = norm_bwd(dps, h_ref[...])
        dx_ref[...] = dh2_ref[...] + dx
        dnw_ref[...] += dnw
        accumulate(dps, (nrm * nwv).astype(MXU_DTYPE), False)

        @pl.when(i == steps - 1)
        def _():
            pltpu.sync_copy(acc_hg, ghg_ref)
            pltpu.sync_copy(acc_gd, ggd_ref)
            pltpu.sync_copy(acc_ab, gab_ref)

    row = lambda w: pl.BlockSpec((tm, w), lambda i: (i, 0))
    full = lambda a: pl.BlockSpec(a.shape, lambda i: (0, 0), pipeline_mode=pl.Buffered(1))
    anywhere = pl.BlockSpec(memory_space=pl.ANY)
    return pl.pallas_call(
        body, grid=(steps,), name="in_proj_bwd",
        in_specs=[row(4 * WIDTH), row(4 * WIDTH), row(AB_PAD)] + _w_in_specs() + [row(D_MODEL), row(D_MODEL), full(norm_w),
                                                                                   full(h0), full(u0), full(dphg0), full(dpgd0),
                                                                                   full(dpab0)],
        out_specs=[row(D_MODEL), pl.BlockSpec(h0.shape, lambda i: (0, 0)), pl.BlockSpec((1, D_MODEL), lambda i: (0, 0)),
                   anywhere, anywhere, anywhere],
        out_shape=[jax.ShapeDtypeStruct((n, D_MODEL), F32), jax.ShapeDtypeStruct(h0.shape, F32),
                   jax.ShapeDtypeStruct((1, D_MODEL), F32), jax.ShapeDtypeStruct((4 * WIDTH, D_MODEL), F32),
                   jax.ShapeDtypeStruct((4 * WIDTH, D_MODEL), F32), jax.ShapeDtypeStruct((AB_PAD, D_MODEL), F32)],
        scratch_shapes=[pltpu.VMEM((4 * WIDTH, D_MODEL), F32), pltpu.VMEM((4 * WIDTH, D_MODEL), F32),
                        pltpu.VMEM((AB_PAD, D_MODEL), F32)],
        compiler_params=pltpu.CompilerParams(dimension_semantics=("arbitrary",), vmem_limit_bytes=VMEM_LIMIT_LARGE),
    )(dphg, dpgd, dpab, w_t, w_t, w_t, h, dh2, norm_w, h0, u0, dphg0, dpgd0, dpab0)


def _real(c):
    return jnp.maximum(c - 1, 0)


def _sds(shape, dtype=F32):
    return jax.ShapeDtypeStruct(shape, dtype)


def _load_slabs(ref, b):
    return jnp.stack([ref[i, :, h * DH:(h + 1) * DH].astype(F32) for i in range(b) for h in range(HEADS)], axis=0)


def _lead_slabs(a, b):
    return jnp.stack([a[:, h * DH:(h + 1) * DH].astype(F32) for _ in range(b) for h in range(HEADS)], axis=0)


def _rows(a3, i):
    return jnp.concatenate([a3[i * HEADS + h] for h in range(HEADS)], axis=1)


def _store_slabs(ref, a3, b):
    for i in range(b):
        ref[i] = _rows(a3, i).astype(ref.dtype)


def _sum_rows(a3, b):
    out = _rows(a3, 0)
    for i in range(1, b):
        out = out + _rows(a3, i)
    return out


def _save_states(ref, s, b):
    for i in range(b):
        ref[i] = jnp.concatenate([s[i * HEADS + h] for h in range(HEADS)], axis=0)


def _load_states(ref, b):
    return jnp.stack([ref[i, h * DH:(h + 1) * DH, :] for i in range(b) for h in range(HEADS)], axis=0)


def hg_local_fwd(p, logits):
    b, seq, _ = p.shape
    rows = LOCAL_CHUNKS * CHUNK
    nreal = seq // CHUNK

    def body(p_ref, lg_ref, q_ref, k_ref, o_ref, eg_ref):
        q_in, k_out, o_intra, egs = hg_local(p_ref[...], lg_ref[...])
        q_ref[...], k_ref[...], o_ref[...] = q_in.astype(MXU_DTYPE), k_out.astype(MXU_DTYPE), o_intra
        for c in range(LOCAL_CHUNKS):
            eg_ref[c] = egs[c]

    slab = pl.BlockSpec((None, rows, WIDTH), lambda s, g: (s, g, 0))
    return pl.pallas_call(
        body, grid=(b, seq // rows), name="hgrn2_local",
        in_specs=[pl.BlockSpec((None, rows, 4 * WIDTH), lambda s, g: (s, g, 0)), pl.BlockSpec(logits.shape, lambda s, g: (0, 0))],
        out_specs=[slab, slab, slab, pl.BlockSpec((None, LOCAL_CHUNKS, 1, WIDTH), lambda s, g: (s, g, 0, 0))],
        out_shape=[_sds((b, seq, WIDTH), MXU_DTYPE)] * 2 + [_sds((b, seq, WIDTH)), _sds((b, nreal, 1, WIDTH))],
        compiler_params=_cparams("arbitrary", "arbitrary"),
    )(p, logits)


def hg_local_lead(p0, logits):
    def body(p_ref, lg_ref, q_ref, k_ref, o_ref, eg_ref):
        q_in, k_out, o_ref[...], (eg_ref[...],) = hg_local(p_ref[...], lg_ref[...])
        q_ref[...], k_ref[...] = q_in.astype(MXU_DTYPE), k_out.astype(MXU_DTYPE)

    return pl.pallas_call(
        body, name="hgrn2_local_lead", in_specs=[VMEM_SPEC] * 2, out_specs=[VMEM_SPEC] * 4,
        out_shape=[_sds((CHUNK, WIDTH), MXU_DTYPE)] * 2 + [_sds((CHUNK, WIDTH)), _sds((1, WIDTH))],
        compiler_params=pltpu.CompilerParams(vmem_limit_bytes=VMEM_LIMIT),
    )(p0, logits)


def _hg_scan_inputs(c, b, q_ref, k_ref, o_ref, v_ref, z_ref, eg_ref, q0_ref, k0_ref, o0_ref, p0_ref, eg0_ref):
    lead = c == 0
    pick = lambda real, lead_val: jnp.where(lead, _lead_slabs(lead_val, b), _load_slabs(real, b))
    eg = jnp.where(lead, jnp.stack([eg0_ref[:, h * DH:(h + 1) * DH] for _ in range(b) for h in range(HEADS)], axis=0),
                   jnp.stack([eg_ref[i, :, h * DH:(h + 1) * DH] for i in range(b) for h in range(HEADS)], axis=0))
    return (pick(q_ref, q0_ref[...]), pick(k_ref, k0_ref[...]), pick(v_ref, p0_ref[:, 2 * WIDTH:3 * WIDTH]), eg,
            pick(o_ref, o0_ref[...]), pick(z_ref, p0_ref[:, 3 * WIDTH:4 * WIDTH]))


def _scan_specs(b, nc, reverse):
    chunk = (lambda i: nc - 1 - i) if reverse else (lambda i: i)
    slab = lambda lane_block: pl.BlockSpec((b, CHUNK, WIDTH), lambda i: (0, _real(chunk(i)), lane_block))
    per_chunk = lambda *tail: pl.BlockSpec((b, None) + tail, lambda i: (0, _real(chunk(i))) + (0,) * len(tail))
    state = pl.BlockSpec((b, None, WIDTH, DH), lambda i: (0, chunk(i), 0, 0))
    const = lambda a: pl.BlockSpec(a.shape, lambda i: (0,) * a.ndim)
    return slab, per_chunk, state, const


def run_scans(parts, nc, name):
    n_in = [len(p["args"]) for p in parts]
    n_out = [len(p["out_shape"]) for p in parts]
    n_scr = [len(p["scratch_shapes"]) for p in parts]

    def body(*refs):
        ins, outs, scr = refs[:sum(n_in)], refs[sum(n_in):sum(n_in) + sum(n_out)], refs[sum(n_in) + sum(n_out):]
        for i, part in enumerate(parts):
            part["body"](*ins[sum(n_in[:i]):sum(n_in[:i + 1])], *outs[sum(n_out[:i]):sum(n_out[:i + 1])],
                         *scr[sum(n_scr[:i]):sum(n_scr[:i + 1])])

    flat = lambda key: [v for p in parts for v in p[key]]
    out = pl.pallas_call(body, grid=(nc,), name=name, in_specs=flat("in_specs"), out_specs=flat("out_specs"),
                         out_shape=flat("out_shape"), scratch_shapes=flat("scratch_shapes"),
                         compiler_params=_cparams("arbitrary"))(*flat("args"))
    return [out[sum(n_out[:i]):sum(n_out[:i + 1])] for i in range(len(parts))]


def hg_scan_fwd(p, p0, local, lead, nw):
    b, seq, _ = p.shape
    nc = seq // CHUNK + 1
    q_in, k_out, o_intra, eg = local
    slab, per_chunk, state, const = _scan_specs(b, nc, False)

    def body(q_ref, k_ref, o_ref, v_ref, z_ref, eg_ref, q0_ref, k0_ref, o0_ref, p0_ref, eg0_ref, nw_ref, y_ref, ss_ref, st):
        c = pl.program_id(0)

        @pl.when(c == 0)
        def _():
            st[...] = jnp.zeros_like(st)

        s_in = st[...]
        _save_states(ss_ref, s_in, b)
        args = _hg_scan_inputs(c, b, q_ref, k_ref, o_ref, v_ref, z_ref, eg_ref, q0_ref, k0_ref, o0_ref, p0_ref, eg0_ref)
        y, s_new = hg_scan(*args, nw_ref[...], s_in)
        _store_slabs(y_ref, y, b)
        st[...] = s_new

    return dict(
        body=body, args=(q_in, k_out, o_intra, p, p, eg, lead[0], lead[1], lead[2], p0, lead[3], nw),
        in_specs=[slab(0), slab(0), slab(0), slab(2), slab(3), per_chunk(1, WIDTH)] + [const(a) for a in lead[0:3]]
        + [const(p0), const(lead[3]), const(nw)],
        out_specs=[slab(0), state],
        out_shape=[_sds((b, seq, WIDTH), MXU_DTYPE), _sds((b, nc, WIDTH, DH))],
        scratch_shapes=[pltpu.VMEM((b * HEADS, DH, DH), F32)])


def hg_scan_bwd(p, p0, local, lead, nw, ssave, dy):
    b, seq, _ = p.shape
    nc = seq // CHUNK + 1
    q_in, k_out, o_intra, eg = local
    slab, per_chunk, state, const = _scan_specs(b, nc, True)

    def body(q_ref, k_ref, o_ref, v_ref, z_ref, eg_ref, q0_ref, k0_ref, o0_ref, p0_ref, eg0_ref, nw_ref, ss_ref, dy_ref,
             dq_ref, dk_ref, do_ref, dv_ref, dz_ref, deg_ref, dq0_ref, dk0_ref, do0_ref, dv0_ref, dz0_ref, deg0_ref, dnw_ref,
             dst):
        i = pl.program_id(0)
        c = nc - 1 - i

        @pl.when(i == 0)
        def _():
            dst[...] = jnp.zeros_like(dst)
            dnw_ref[...] = jnp.zeros_like(dnw_ref)

        args = _hg_scan_inputs(c, b, q_ref, k_ref, o_ref, v_ref, z_ref, eg_ref, q0_ref, k0_ref, o0_ref, p0_ref, eg0_ref)
        s_in = _load_states(ss_ref, b)
        _, vjp = jax.vjp(hg_scan, *args, nw_ref[...], s_in)
        dyv = jnp.where(c == 0, 0.0, _load_slabs(dy_ref, b))
        dq, dk, dv, deg, do, dz, dnw, ds = vjp((dyv, dst[...]))
        dst[...] = ds
        dnw_ref[...] += dnw

        @pl.when(c > 0)
        def _():
            for ref, val in ((dq_ref, dq), (dk_ref, dk), (do_ref, do), (dv_ref, dv), (dz_ref, dz)):
                _store_slabs(ref, val, b)
            for j in range(b):
                deg_ref[j] = _rows(deg, j)

        @pl.when(c == 0)
        def _():
            for ref, val in ((dq0_ref, dq), (dk0_ref, dk), (do0_ref, do), (dv0_ref, dv), (dz0_ref, dz), (deg0_ref, deg)):
                ref[...] = _sum_rows(val, b)

    lead_out = [const(a) for a in lead[0:3]] + [const(lead[0]), const(lead[0]), const(lead[3])]
    return dict(
        body=body, args=(q_in, k_out, o_intra, p, p, eg, lead[0], lead[1], lead[2], p0, lead[3], nw, ssave, dy),
        in_specs=[slab(0), slab(0), slab(0), slab(2), slab(3), per_chunk(1, WIDTH)] + [const(a) for a in lead[0:3]]
        + [const(p0), const(lead[3]), const(nw), state, slab(0)],
        out_specs=[slab(0)] * 5 + [per_chunk(1, WIDTH)] + lead_out + [const(nw)],
        out_shape=[_sds((b, seq, WIDTH))] * 5 + [_sds(eg.shape)] + [_sds((CHUNK, WIDTH))] * 5 + [_sds((1, WIDTH)), _sds(nw.shape)],
        scratch_shapes=[pltpu.VMEM((b * HEADS, DH, DH), F32)])


def _hg_local_vjp(p, logits, dq, dk, do, degs, dv, dz):
    _, vjp = jax.vjp(hg_local, p, logits)
    dp, dlg = vjp((dq, dk, do, degs))
    return dp + jnp.concatenate([jnp.zeros((p.shape[0], 2 * WIDTH), F32), dv, dz], axis=1), dlg


def hg_local_bwd(p, logits, dq, dk, do, dv, dz, deg):
    b, seq, _ = p.shape
    rows = LOCAL_CHUNKS * CHUNK

    def body(p_ref, lg_ref, dq_ref, dk_ref, do_ref, dv_ref, dz_ref, deg_ref, dp_ref, dlg_ref):
        @pl.when((pl.program_id(0) == 0) & (pl.program_id(1) == 0))
        def _():
            dlg_ref[...] = jnp.zeros_like(dlg_ref)

        degs = tuple(deg_ref[c] for c in range(LOCAL_CHUNKS))
        dp, dlg = _hg_local_vjp(p_ref[...], lg_ref[...], dq_ref[...], dk_ref[...], do_ref[...], degs, dv_ref[...], dz_ref[...])
        dp_ref[...] = dp.astype(MXU_DTYPE)
        dlg_ref[...] += dlg

    slab = pl.BlockSpec((None, rows, WIDTH), lambda s, g: (s, g, 0))
    wide = pl.BlockSpec((None, rows, 4 * WIDTH), lambda s, g: (s, g, 0))
    lg = pl.BlockSpec(logits.shape, lambda s, g: (0, 0))
    return pl.pallas_call(
        body, grid=(b, seq // rows), name="hgrn2_local_bwd",
        in_specs=[wide, lg, slab, slab, slab, slab, slab, pl.BlockSpec((None, LOCAL_CHUNKS, 1, WIDTH), lambda s, g: (s, g, 0, 0))],
        out_specs=[wide, lg], out_shape=[_sds(p.shape, MXU_DTYPE), _sds(logits.shape)],
        compiler_params=_cparams("arbitrary", "arbitrary"),
    )(p, logits, dq, dk, do, dv, dz, deg)


def hg_local_bwd_lead(p0, logits, dq, dk, do, dv, dz, deg):
    def body(p_ref, lg_ref, dq_ref, dk_ref, do_ref, dv_ref, dz_ref, deg_ref, dp_ref, dlg_ref):
        dp, dlg_ref[...] = _hg_local_vjp(p_ref[...], lg_ref[...], dq_ref[...], dk_ref[...], do_ref[...],
                                         (deg_ref[...],), dv_ref[...], dz_ref[...])
        dp_ref[...] = dp.astype(MXU_DTYPE)

    return pl.pallas_call(
        body, name="hgrn2_local_bwd_lead", in_specs=[VMEM_SPEC] * 8, out_specs=[VMEM_SPEC] * 2,
        out_shape=[_sds(p0.shape, MXU_DTYPE), _sds(logits.shape)], compiler_params=pltpu.CompilerParams(vmem_limit_bytes=VMEM_LIMIT),
    )(p0, logits, dq, dk, do, dv, dz, deg)


def _halo_block(g):
    return jnp.maximum((LOCAL_CHUNKS * CHUNK // HALO) * g - 1, 0)


def _gd_window(g, p_ref, halo_ref, p0_ref):
    halo = jnp.where(g == 0, p0_ref[CHUNK - HALO:CHUNK, 0:QKV], halo_ref[...])
    return jnp.concatenate([halo, p_ref[:, 0:QKV]], axis=0)


def gd_local_fwd(p, p0, ab, cw, alog, dtb):
    b, seq, _ = p.shape
    rows = LOCAL_CHUNKS * CHUNK
    nreal = seq // CHUNK

    def body(p_ref, halo_ref, p0_ref, ab_ref, cw_ref, al_ref, dt_ref, u_ref, w_ref, qe_ref, ke_ref, qk_ref, ea_ref, inv_ref):
        (uu, ww, qe, ke, qk, eas), inv = gd_local(_gd_window(pl.program_id(1), p_ref, halo_ref, p0_ref), ab_ref[...],
                                                  cw_ref[...], al_ref[...], dt_ref[...], inverse=_tri_y_impl)
        u_ref[...], w_ref[...], qe_ref[...], ke_ref[...] = uu, ww.astype(MXU_DTYPE), qe.astype(MXU_DTYPE), ke.astype(MXU_DTYPE)
        for c in range(LOCAL_CHUNKS):
            qk_ref[c] = qk[c * HEADS * CHUNK:(c + 1) * HEADS * CHUNK]
            inv_ref[c] = inv[c * HEADS * CHUNK:(c + 1) * HEADS * CHUNK]
            ea_ref[c] = eas[c]

    const = lambda a: pl.BlockSpec(a.shape, lambda s, g: (0, 0))
    slab = pl.BlockSpec((None, rows, WIDTH), lambda s, g: (s, g, 0))
    mats = pl.BlockSpec((None, LOCAL_CHUNKS, HEADS * CHUNK, CHUNK), lambda s, g: (s, g, 0, 0))
    out = pl.pallas_call(
        body, grid=(b, seq // rows), name="gdn_local",
        in_specs=[pl.BlockSpec((None, rows, 4 * WIDTH), lambda s, g: (s, g, 0)),
                  pl.BlockSpec((None, HALO, QKV), lambda s, g: (s, _halo_block(g), 0)), const(p0),
                  pl.BlockSpec((None, rows, AB_PAD), lambda s, g: (s, g, 0)), const(cw), const(alog), const(dtb)],
        out_specs=[slab] * 4 + [mats, pl.BlockSpec((None, LOCAL_CHUNKS, 1, AB_PAD), lambda s, g: (s, g, 0, 0)), mats],
        out_shape=[_sds((b, seq, WIDTH))] + [_sds((b, seq, WIDTH), MXU_DTYPE)] * 3
        + [_sds((b, nreal, HEADS * CHUNK, CHUNK)), _sds((b, nreal, 1, AB_PAD)), _sds((b, nreal, HEADS * CHUNK, CHUNK))],
        compiler_params=_cparams("arbitrary", "arbitrary"),
    )(p, p, p0, ab, cw, alog, dtb)
    return out[0:6], out[6]


def _lead_window(p0_ref):
    return jnp.concatenate([jnp.zeros((HALO, QKV), F32), p0_ref[:, 0:QKV]], axis=0)


def gd_local_lead(p0, ab0, cw, alog, dtb):
    def body(p0_ref, ab_ref, cw_ref, al_ref, dt_ref, u_ref, w_ref, qe_ref, ke_ref, qk_ref, ea_ref, inv_ref):
        (u_ref[...], ww, qe, ke, qk_ref[...], (ea_ref[...],)), inv_ref[...] = gd_local(
            _lead_window(p0_ref), ab_ref[...], cw_ref[...], al_ref[...], dt_ref[...], inverse=_tri_y_impl)
        w_ref[...], qe_ref[...], ke_ref[...] = ww.astype(MXU_DTYPE), qe.astype(MXU_DTYPE), ke.astype(MXU_DTYPE)

    out = pl.pallas_call(
        body, name="gdn_local_lead", in_specs=[VMEM_SPEC] * 5, out_specs=[VMEM_SPEC] * 7,
        out_shape=[_sds((CHUNK, WIDTH))] + [_sds((CHUNK, WIDTH), MXU_DTYPE)] * 3
        + [_sds((HEADS * CHUNK, CHUNK)), _sds((1, AB_PAD)), _sds((HEADS * CHUNK, CHUNK))],
        compiler_params=pltpu.CompilerParams(vmem_limit_bytes=VMEM_LIMIT),
    )(p0, ab0, cw, alog, dtb)
    return out[0:6], out[6]


def _gd_scan_inputs(c, b, u_ref, w_ref, qe_ref, ke_ref, qk_ref, ea_ref, z_ref, u0_ref, w0_ref, qe0_ref, ke0_ref, qk0_ref,
                    ea0_ref, p0_ref):
    lead = c == 0
    pick = lambda real, lead_val: jnp.where(lead, _lead_slabs(lead_val, b), _load_slabs(real, b))
    pairs = [(i, h) for i in range(b) for h in range(HEADS)]
    qk = jnp.where(lead, jnp.stack([qk0_ref[h * CHUNK:(h + 1) * CHUNK, :] for _, h in pairs], axis=0),
                   jnp.stack([qk_ref[i, h * CHUNK:(h + 1) * CHUNK, :] for i, h in pairs], axis=0))
    ea = jnp.where(lead, jnp.stack([ea0_ref[:, h:h + 1] for _, h in pairs], axis=0),
                   jnp.stack([ea_ref[i, :, h:h + 1] for i, h in pairs], axis=0))
    return (pick(u_ref, u0_ref[...]), pick(w_ref, w0_ref[...]), pick(qe_ref, qe0_ref[...]), pick(ke_ref, ke0_ref[...]), qk,
            ea, pick(z_ref, p0_ref[:, QKV:QKV + WIDTH]))


def gd_scan_fwd(p, p0, local, lead, nw):
    b, seq, _ = p.shape
    nc = seq // CHUNK + 1
    slab, per_chunk, state, const = _scan_specs(b, nc, False)

    def body(u_ref, w_ref, qe_ref, ke_ref, qk_ref, ea_ref, z_ref, u0_ref, w0_ref, qe0_ref, ke0_ref, qk0_ref, ea0_ref, p0_ref,
             nw_ref, y_ref, ss_ref, st):
        c = pl.program_id(0)

        @pl.when(c == 0)
        def _():
            st[...] = jnp.zeros_like(st)

        s_in = st[...]
        _save_states(ss_ref, s_in, b)
        args = _gd_scan_inputs(c, b, u_ref, w_ref, qe_ref, ke_ref, qk_ref, ea_ref, z_ref, u0_ref, w0_ref, qe0_ref, ke0_ref,
                               qk0_ref, ea0_ref, p0_ref)
        y, s_new = gd_scan(*args, nw_ref[...], s_in)
        _store_slabs(y_ref, y, b)
        st[...] = s_new

    return dict(
        body=body, args=(*local, p, *lead, p0, nw),
        in_specs=[slab(0)] * 4 + [per_chunk(HEADS * CHUNK, CHUNK), per_chunk(1, AB_PAD), slab(3)] + [const(a) for a in lead]
        + [const(p0), const(nw)],
        out_specs=[slab(0), state],
        out_shape=[_sds((b, seq, WIDTH), MXU_DTYPE), _sds((b, nc, WIDTH, DH))],
        scratch_shapes=[pltpu.VMEM((b * HEADS, DH, DH), F32)])


def gd_scan_bwd(p, p0, local, lead, nw, ssave, dy):
    b, seq, _ = p.shape
    nc = seq // CHUNK + 1
    slab, per_chunk, state, const = _scan_specs(b, nc, True)

    def body(u_ref, w_ref, qe_ref, ke_ref, qk_ref, ea_ref, z_ref, u0_ref, w0_ref, qe0_ref, ke0_ref, qk0_ref, ea0_ref, p0_ref,
             nw_ref, ss_ref, dy_ref, du_ref, dw_ref, dqe_ref, dke_ref, dqk_ref, dea_ref, dz_ref, du0_ref, dw0_ref, dqe0_ref,
             dke0_ref, dqk0_ref, dea0_ref, dz0_ref, dnw_ref, dst):
        i = pl.program_id(0)
        c = nc - 1 - i

        @pl.when(i == 0)
        def _():
            dst[...] = jnp.zeros_like(dst)
            dnw_ref[...] = jnp.zeros_like(dnw_ref)

        args = _gd_scan_inputs(c, b, u_ref, w_ref, qe_ref, ke_ref, qk_ref, ea_ref, z_ref, u0_ref, w0_ref, qe0_ref, ke0_ref,
                               qk0_ref, ea0_ref, p0_ref)
        s_in = _load_states(ss_ref, b)
        _, vjp = jax.vjp(gd_scan, *args, nw_ref[...], s_in)
        dyv = jnp.where(c == 0, 0.0, _load_slabs(dy_ref, b))
        du, dw, dqe, dke, dqk, dea, dz, dnw, ds = vjp((dyv, dst[...]))
        dst[...] = ds
        dnw_ref[...] += dnw
        lane = lax.broadcasted_iota(jnp.int32, (1, AB_PAD), 1)
        dea_rows = [sum(jnp.where(lane == h, dea[j * HEADS + h], 0.0) for h in range(HEADS)) for j in range(b)]
        dqk_rows = [jnp.concatenate([dqk[j * HEADS + h] for h in range(HEADS)], axis=0) for j in range(b)]

        @pl.when(c > 0)
        def _():
            for ref, val in ((du_ref, du), (dw_ref, dw), (dqe_ref, dqe), (dke_ref, dke), (dz_ref, dz)):
                _store_slabs(ref, val, b)
            for j in range(b):
                dqk_ref[j] = dqk_rows[j]
                dea_ref[j] = dea_rows[j]

        @pl.when(c == 0)
        def _():
            for ref, val in ((du0_ref, du), (dw0_ref, dw), (dqe0_ref, dqe), (dke0_ref, dke), (dz0_ref, dz)):
                ref[...] = _sum_rows(val, b)
            dqk0_ref[...] = sum(dqk_rows[1:], dqk_rows[0])
            dea0_ref[...] = sum(dea_rows[1:], dea_rows[0])

    uu, ww, qe, ke, qk, ea = local
    return dict(
        body=body, args=(*local, p, *lead, p0, nw, ssave, dy),
        in_specs=[slab(0)] * 4 + [per_chunk(HEADS * CHUNK, CHUNK), per_chunk(1, AB_PAD), slab(3)] + [const(a) for a in lead]
        + [const(p0), const(nw), state, slab(0)],
        out_specs=[slab(0)] * 4 + [per_chunk(HEADS * CHUNK, CHUNK), per_chunk(1, AB_PAD), slab(0)] + [const(a) for a in lead]
        + [const(lead[0]), const(nw)],
        out_shape=[_sds((b, seq, WIDTH))] * 4 + [_sds(qk.shape), _sds(ea.shape), _sds((b, seq, WIDTH))]
        + [_sds(a.shape) for a in lead] + [_sds(lead[0].shape), _sds(nw.shape)],
        scratch_shapes=[pltpu.VMEM((b * HEADS, DH, DH), F32)])


def _gd_local_vjp(inv_rows, xx, ab, cw, alog, dtb):
    nb = ab.shape[0] // CHUNK
    inv = jnp.stack([inv_rows[g * CHUNK:(g + 1) * CHUNK] for g in range(nb * HEADS)], axis=0)
    _, vjp, _ = jax.vjp(lambda *a: gd_local(*a, inverse=_saved_inverse(inv)), xx, ab, cw, alog, dtb, has_aux=True)
    return vjp


def gd_local_bwd(p, p0, ab, cw, alog, dtb, inv, cot, dz):
    b, seq, _ = p.shape
    rows = LOCAL_CHUNKS * CHUNK
    ng = seq // rows
    du, dw, dqe, dke, dqk, dea = cot

    def body(p_ref, halo_ref, p0_ref, ab_ref, cw_ref, al_ref, dt_ref, inv_ref, du_ref, dw_ref, dqe_ref, dke_ref, dqk_ref,
             dea_ref, dz_ref, dp_ref, dab_ref, dhalo0_ref, dcw_ref, dal_ref, ddt_ref, dhalo):
        i = pl.program_id(1)
        g = ng - 1 - i

        @pl.when(i == 0)
        def _():
            dhalo[...] = jnp.zeros_like(dhalo)

        @pl.when((pl.program_id(0) == 0) & (i == 0))
        def _():
            dcw_ref[...] = jnp.zeros_like(dcw_ref)
            dal_ref[...] = jnp.zeros_like(dal_ref)
            ddt_ref[...] = jnp.zeros_like(ddt_ref)

        inv_rows = jnp.concatenate([inv_ref[c] for c in range(LOCAL_CHUNKS)], axis=0)
        vjp = _gd_local_vjp(inv_rows, _gd_window(g, p_ref, halo_ref, p0_ref), ab_ref[...], cw_ref[...], al_ref[...], dt_ref[...])
        dqk_all = jnp.concatenate([dqk_ref[c] for c in range(LOCAL_CHUNKS)], axis=0)
        deas = tuple(dea_ref[c] for c in range(LOCAL_CHUNKS))
        dxx, dab, dcw, dal, ddt = vjp((du_ref[...], dw_ref[...], dqe_ref[...], dke_ref[...], dqk_all, deas))
        dqkv = dxx[HALO:HALO + rows] + jnp.concatenate([jnp.zeros((rows - HALO, QKV), F32), dhalo[...]], axis=0)
        dhalo[...] = dxx[0:HALO]
        dhalo0_ref[...] = dxx[0:HALO]
        dp_ref[...] = jnp.concatenate([dqkv, dz_ref[...]], axis=1).astype(MXU_DTYPE)
        dab_ref[...] = dab.astype(MXU_DTYPE)
        dcw_ref[...] += dcw
        dal_ref[...] += dal
        ddt_ref[...] += ddt

    rg = lambda i: ng - 1 - i
    const = lambda a: pl.BlockSpec(a.shape, lambda s, i: (0, 0))
    slab = pl.BlockSpec((None, rows, WIDTH), lambda s, i: (s, rg(i), 0))
    wide = pl.BlockSpec((None, rows, 4 * WIDTH), lambda s, i: (s, rg(i), 0))
    gates = pl.BlockSpec((None, rows, AB_PAD), lambda s, i: (s, rg(i), 0))
    mats = pl.BlockSpec((None, LOCAL_CHUNKS, HEADS * CHUNK, CHUNK), lambda s, i: (s, rg(i), 0, 0))
    return pl.pallas_call(
        body, grid=(b, ng), name="gdn_local_bwd",
        in_specs=[wide, pl.BlockSpec((None, HALO, QKV), lambda s, i: (s, _halo_block(rg(i)), 0)), const(p0), gates, const(cw),
                  const(alog), const(dtb), mats, slab, slab, slab, slab, mats,
                  pl.BlockSpec((None, LOCAL_CHUNKS, 1, AB_PAD), lambda s, i: (s, rg(i), 0, 0)), slab],
        out_specs=[wide, gates, pl.BlockSpec((None, HALO, QKV), lambda s, i: (s, 0, 0)), const(cw), const(alog), const(dtb)],
        out_shape=[_sds(p.shape, MXU_DTYPE), _sds(ab.shape, MXU_DTYPE), _sds((b, HALO, QKV)), _sds(cw.shape), _sds(alog.shape),
                   _sds(dtb.shape)],
        scratch_shapes=[pltpu.VMEM((HALO, QKV), F32)],
        compiler_params=_cparams("arbitrary", "arbitrary"),
    )(p, p, p0, ab, cw, alog, dtb, inv, du, dw, dqe, dke, dqk, dea, dz)


def gd_local_bwd_lead(p0, ab0, cw, alog, dtb, inv, cot, dz, dtail):
    def body(p0_ref, ab_ref, cw_ref, al_ref, dt_ref, inv_ref, du_ref, dw_ref, dqe_ref, dke_ref, dqk_ref, dea_ref, dz_ref,
             dtail_ref, dp_ref, dab_ref, dcw_ref, dal_ref, ddt_ref):
        vjp = _gd_local_vjp(inv_ref[...], _lead_window(p0_ref), ab_ref[...], cw_ref[...], al_ref[...], dt_ref[...])
        dxx, dab, dcw, dal, ddt = vjp((du_ref[...], dw_ref[...], dqe_ref[...], dke_ref[...], dqk_ref[...], (dea_ref[...],)))
        dqkv = dxx[HALO:HALO + CHUNK] + jnp.concatenate([jnp.zeros((CHUNK - HALO, QKV), F32), dtail_ref[...]], axis=0)
        dp_ref[...] = jnp.concatenate([dqkv, dz_ref[...]], axis=1).astype(MXU_DTYPE)
        dab_ref[...], dcw_ref[...], dal_ref[...], ddt_ref[...] = dab.astype(MXU_DTYPE), dcw, dal, ddt

    return pl.pallas_call(
        body, name="gdn_local_bwd_lead", in_specs=[VMEM_SPEC] * 14, out_specs=[VMEM_SPEC] * 5,
        out_shape=[_sds(p0.shape, MXU_DTYPE), _sds(ab0.shape, MXU_DTYPE), _sds(cw.shape), _sds(alog.shape), _sds(dtb.shape)],
        compiler_params=pltpu.CompilerParams(vmem_limit_bytes=VMEM_LIMIT),
    )(p0, ab0, cw, alog, dtb, inv, *cot, dz, dtail)


def _position():
    return lax.axis_index("x"), lax.axis_index("y"), lax.axis_index("c")


def _exchange_blocks(bufs, send_sems, recv_sems):
    x, y, c = _position()
    me, sibling = (x, y, c), (x, y, 1 - c)
    chips = [(1 - x, y), (x, 1 - y), (1 - x, 1 - y)]
    per_buf = N_DEV - 1

    def copy(a, k, blk, to):
        rows = bufs[a].at[4 * blk[0] + 2 * blk[1] + blk[2]]
        return pltpu.make_async_remote_copy(src_ref=rows, dst_ref=rows, send_sem=send_sems.at[a * per_buf + k],
                                            recv_sem=recv_sems.at[a * per_buf + k], device_id=to, device_id_type=MESH)

    bufs_idx = range(len(bufs))
    first = [copy(a, 0, me, sibling) for a in bufs_idx] + [copy(a, 1 + j, me, (*chip, c)) for a in bufs_idx
                                                           for j, chip in enumerate(chips)]
    for cp in first:
        cp.start()
    passed = []
    for j, chip in enumerate(chips):
        for a in bufs_idx:
            copy(a, 1 + j, (*chip, c), me).wait_recv()
            passed.append(copy(a, 4 + j, (*chip, c), sibling))
            passed[-1].start()
    for a in bufs_idx:
        copy(a, 0, sibling, me).wait_recv()
        for j, chip in enumerate(chips):
            copy(a, 4 + j, (*chip, 1 - c), me).wait_recv()
    for cp in first + passed:
        cp.wait_send()


def _exchange_sems(n_bufs):
    return [pltpu.SemaphoreType.DMA((n_bufs * (N_DEV - 1),)), pltpu.SemaphoreType.DMA((n_bufs * (N_DEV - 1),))]


def gather_weights(w_in_t, w_out, small, pad_rows):
    rows, _, cols = w_in_t.shape

    def body(wi_ref, wo_ref, sm_ref, wi_out, wo_out, sm_out, wi_buf, send_sems, recv_sems):
        x, y, c = _position()
        me = 4 * x + 2 * y + c
        wi_buf[me] = wi_ref[:, 0, :].astype(MXU_DTYPE)
        wo_out[me] = wo_ref[...].astype(MXU_DTYPE)
        sm_out[me] = sm_ref[...]
        _exchange_blocks([wi_buf, wo_out, sm_out], send_sems, recv_sems)
        for d in range(N_DEV):
            wi_out[pl.ds(d * rows, rows), :] = wi_buf[d]
        wi_out[pl.ds(N_DEV * rows, pad_rows), :] = jnp.zeros((pad_rows, cols), MXU_DTYPE)

    return pl.pallas_call(
        body, name="gather_weights", in_specs=[VMEM_SPEC] * 3, out_specs=[VMEM_SPEC] * 3,
        out_shape=[jax.ShapeDtypeStruct((N_DEV * rows + pad_rows, cols), MXU_DTYPE),
                   jax.ShapeDtypeStruct((N_DEV,) + w_out.shape, MXU_DTYPE), jax.ShapeDtypeStruct((N_DEV,) + small.shape, F32)],
        scratch_shapes=[pltpu.VMEM((N_DEV, rows, cols), MXU_DTYPE)] + _exchange_sems(3),
        compiler_params=pltpu.CompilerParams(vmem_limit_bytes=VMEM_LIMIT))(w_in_t, w_out, small)


def reduce_gradients(tensors, small, name):
    n_t = len(tensors)
    arrays = [a for parts, _ in tensors for a, _ in parts]
    first_array = [sum(len(parts) for parts, _ in tensors[:t]) for t in range(n_t)]

    def pieces(t, j):
        parts, block_rows = tensors[t]
        out, base = [], 0
        for pi, (_, valid) in enumerate(parts):
            lo, hi = max(j * block_rows, base), min((j + 1) * block_rows, base + valid)
            if lo < hi:
                out.append((first_array[t] + pi, lo - base, lo - j * block_rows, hi - lo))
            base += valid
        return out

    def body(*refs):
        n_a = len(arrays)
        in_refs, small_ref = refs[:n_a], refs[n_a]
        out_refs, small_sum = refs[n_a + 1:n_a + 1 + n_t], refs[n_a + 1 + n_t]
        bufs, small_buf = refs[n_a + 2 + n_t:n_a + 2 + 5 * n_t], refs[n_a + 2 + 5 * n_t]
        s1_sems, r1_sems, s2_sems, r2_sems, small_send, small_recv = refs[n_a + 3 + 5 * n_t:]
        x, y, c = _position()
        chip = 2 * x + y

        def put(t, dst, j, add=None):
            for ai, src_row, dst_row, size in pieces(t, j):
                v = in_refs[ai][pl.ds(src_row, size), :]
                if add is not None:
                    v = v + add[pl.ds(dst_row, size), :].astype(F32)
                dst[pl.ds(dst_row, size), :] = v.astype(dst.dtype)

        def swap(t, k):
            send1, recv1 = bufs[4 * t], bufs[4 * t + 1]
            return pltpu.make_async_remote_copy(src_ref=send1.at[k], dst_ref=recv1.at[k], send_sem=s1_sems.at[4 * t + k],
                                                recv_sem=r1_sems.at[4 * t + k], device_id=(x, y, 1 - c), device_id_type=MESH)

        def to_chip(t, k, slot):
            send2, recv2 = bufs[4 * t + 2], bufs[4 * t + 3]
            return pltpu.make_async_remote_copy(src_ref=send2.at[k], dst_ref=recv2.at[slot], send_sem=s2_sems.at[4 * t + k],
                                                recv_sem=r2_sems.at[4 * t + slot], device_id=(k >> 1, k & 1, c),
                                                device_id_type=MESH)

        for t in range(n_t):
            for j in range(N_DEV):
                @pl.when((j & 1) != c)
                def _():
                    put(t, bufs[4 * t].at[j >> 1], j)
            for k in range(4):
                swap(t, k).start()

        small_buf[4 * x + 2 * y + c] = small_ref[...]
        _exchange_blocks([small_buf], small_send, small_recv)
        total = small_buf[0]
        for d in range(1, N_DEV):
            total = total + small_buf[d]
        small_sum[...] = total

        for t in range(n_t):
            recv1 = bufs[4 * t + 1]
            for k in range(4):
                swap(t, k).wait_recv()
                for j in (2 * k, 2 * k + 1):
                    @pl.when(((j & 1) == c) & (k != chip))
                    def _():
                        put(t, bufs[4 * t + 2].at[k], j, add=recv1.at[k])
                        to_chip(t, k, chip).start()

                    @pl.when(((j & 1) == c) & (k == chip))
                    def _():
                        put(t, out_refs[t], j, add=recv1.at[k])

        for t in range(n_t):
            for k in range(4):
                @pl.when(k != chip)
                def _():
                    to_chip(t, k, k).wait_recv()
                    out_refs[t][...] += bufs[4 * t + 3][k].astype(F32)

        for t in range(n_t):
            for k in range(4):
                @pl.when(k != chip)
                def _():
                    to_chip(t, k, chip).wait_send()
                swap(t, k).wait_send()

    scratch, out_shape = [], []
    for parts, block_rows in tensors:
        cols = parts[0][0].shape[1]
        scratch += [pltpu.VMEM((4, block_rows, cols), MXU_DTYPE)] * 4
        out_shape.append(jax.ShapeDtypeStruct((block_rows, cols), F32))
    out_shape.append(jax.ShapeDtypeStruct(small.shape, F32))
    scratch += [pltpu.VMEM((N_DEV,) + small.shape, F32)] + [pltpu.SemaphoreType.DMA((4 * n_t,))] * 4 + _exchange_sems(1)
    return pl.pallas_call(
        body, name=name, in_specs=[VMEM_SPEC] * (len(arrays) + 1), out_specs=[VMEM_SPEC] * (n_t + 1), out_shape=out_shape,
        scratch_shapes=scratch, compiler_params=pltpu.CompilerParams(vmem_limit_bytes=VMEM_LIMIT),
    )(*arrays, small)


def _adamw_step(w, g, m, v):
    mn = ADAM_B1 * m + (1.0 - ADAM_B1) * g
    vn = ADAM_B2 * v + (1.0 - ADAM_B2) * jnp.square(g)
    m_hat = mn / (1.0 - ADAM_B1 ** ADAM_STEP)
    v_hat = vn / (1.0 - ADAM_B2 ** ADAM_STEP)
    return -ADAM_LR * (m_hat / (jnp.sqrt(v_hat) + ADAM_EPS) + ADAM_WD * w), mn, vn


def adamw(w, g, m, v, name):
    rows, cols = w.shape
    tr = 256 if rows % 256 == 0 else rows

    def body(w_ref, g_ref, m_ref, v_ref, d_ref, nm_ref, nv_ref):
        d_ref[...], nm_ref[...], nv_ref[...] = _adamw_step(w_ref[...], g_ref[...], m_ref[...], v_ref[...])

    spec = pl.BlockSpec((tr, cols), lambda i: (i, 0))
    shape = jax.ShapeDtypeStruct((rows, cols), F32)
    return pl.pallas_call(body, grid=(rows // tr,), name=name, in_specs=[spec] * 4, out_specs=[spec] * 3,
                          out_shape=[shape] * 3, compiler_params=_cparams("arbitrary"))(w, g, m, v)


def adamw_w_in(w, g_t, m, v):
    def body(w_ref, g_ref, m_ref, v_ref, go_ref, d_ref, nm_ref, nv_ref):
        g = g_ref[...]
        go_ref[:, 0, :] = g
        d_ref[:, 0, :], nm_ref[:, 0, :], nv_ref[:, 0, :] = _adamw_step(w_ref[:, 0, :], g, m_ref[:, 0, :], v_ref[:, 0, :])

    return pl.pallas_call(body, name="adamw_w_in", in_specs=[VMEM_SPEC] * 4, out_specs=[VMEM_SPEC] * 4,
                          out_shape=[jax.ShapeDtypeStruct(w.shape, F32)] * 4,
                          compiler_params=pltpu.CompilerParams(vmem_limit_bytes=VMEM_LIMIT))(w, g_t, m, v)


def _pad_rows(a, rows=8):
    return jnp.pad(a, ((0, rows - a.shape[0]), (0, 0)))


def _pad_lanes(a, lanes=128):
    return jnp.pad(a, ((0, 0), (0, lanes - a.shape[1])))


def kernel(x, meta_tokens, norm_w, w_in, conv_w, hg_lb_logits, hg_norm_w, gdn_A_log, gdn_dt_bias, gdn_norm_w, w_out, final_norm_w, loss_target, m_meta_tokens, m_norm_w, m_w_in, m_conv_w, m_hg_lb_logits, m_hg_norm_w, m_gdn_A_log, m_gdn_dt_bias, m_gdn_norm_w, m_w_out, m_final_norm_w, v_meta_tokens, v_norm_w, v_w_in, v_conv_w, v_hg_lb_logits, v_hg_norm_w, v_gdn_A_log, v_gdn_dt_bias, v_gdn_norm_w, v_w_out, v_final_norm_w):
    b, seq, _ = x.shape
    n = b * seq
    dev = 4 * lax.axis_index("x") + 2 * lax.axis_index("y") + lax.axis_index("c")
    col_shard = IN_COLS // N_DEV

    small_w = jnp.concatenate([_pad_lanes(meta_tokens, 256), _pad_rows(_pad_lanes(conv_w[0], 256))], axis=0)
    w_t, w_out_g, small_g = gather_weights(jnp.transpose(w_in, (2, 0, 1)), w_out[0], small_w, AB_PAD - 2 * HEADS)
    meta_g = small_g[:, 0:N_META, 0:D_MODEL // N_DEV]
    conv_g = small_g[:, N_META:N_META + CONV_TAPS, 0:QKV // N_DEV]
    w_out_full = w_out_g.reshape(2 * WIDTH, D_MODEL)
    cw = jnp.transpose(conv_g, (1, 0, 2)).reshape(CONV_TAPS, QKV)
    meta = jnp.transpose(meta_g, (1, 0, 2)).reshape(N_META, D_MODEL)
    alog = _pad_lanes(gdn_A_log)
    dtb = _pad_lanes(gdn_dt_bias)
    fw = final_norm_w.reshape(1, D_MODEL)

    h0 = jnp.concatenate([jnp.zeros((CHUNK - N_META, D_MODEL), F32), meta], axis=0)
    x2 = x.reshape(n, D_MODEL)
    phg, pgd, pab, phg0, pgd0, pab0, u0 = in_proj(x2, h0, norm_w, w_t)
    phg3, pgd3, pab3 = phg.reshape(b, seq, 4 * WIDTH), pgd.reshape(b, seq, 4 * WIDTH), pab.reshape(b, seq, AB_PAD)
    nc = seq // CHUNK + 1
    hg_loc = hg_local_fwd(phg3, hg_lb_logits)
    hg_lead = hg_local_lead(phg0, hg_lb_logits)
    gd_loc, gd_inv = gd_local_fwd(pgd3, pgd0, pab3, cw, alog, dtb)
    gd_lead, gd_inv0 = gd_local_lead(pgd0, pab0, cw, alog, dtb)
    (y_hg, s_hg), (y_gd, s_gd) = run_scans([hg_scan_fwd(phg3, phg0, hg_loc, hg_lead, hg_norm_w),
                                            gd_scan_fwd(pgd3, pgd0, gd_loc, gd_lead, gdn_norm_w)], nc, "scans")

    dh2, dy_hg, dy_gd, g_w_out, loss_part, g_fw = out_proj_loss(
        x2, loss_target.reshape(n, D_MODEL), y_hg.reshape(n, WIDTH), y_gd.reshape(n, WIDTH), w_out_full, fw)

    hb, gb = run_scans([hg_scan_bwd(phg3, phg0, hg_loc, hg_lead, hg_norm_w, s_hg, dy_hg.reshape(b, seq, WIDTH)),
                        gd_scan_bwd(pgd3, pgd0, gd_loc, gd_lead, gdn_norm_w, s_gd, dy_gd.reshape(b, seq, WIDTH))],
                       nc, "scans_bwd")
    dphg, g_lb = hg_local_bwd(phg3, hg_lb_logits, *hb[0:6])
    dphg0, g_lb0 = hg_local_bwd_lead(phg0, hg_lb_logits, *hb[6:12])
    g_hg_nw = hb[12]
    dpgd, dpab, dtail, g_cw, g_alog, g_dtb = gd_local_bwd(pgd3, pgd0, pab3, cw, alog, dtb, gd_inv, gb[0:6], gb[6])
    dpgd0, dpab0, g_cw0, g_alog0, g_dtb0 = gd_local_bwd_lead(pgd0, pab0, cw, alog, dtb, gd_inv0, gb[7:13], gb[13],
                                                             dtail.sum(0))
    g_gd_nw = gb[14]
    dphg, dpgd, dpab = dphg.reshape(n, 4 * WIDTH), dpgd.reshape(n, 4 * WIDTH), dpab.reshape(n, AB_PAD)

    grad_x, dh0, g_nw, g_w_hg, g_w_gd, g_w_ab = in_proj_bwd(dphg, dpgd, dpab, w_t, x2, dh2, norm_w, h0, u0, dphg0, dpgd0, dpab0)

    small = jnp.concatenate([
        g_nw.reshape(8, 128), (g_lb + g_lb0).reshape(8, 128), _pad_rows(g_hg_nw), _pad_rows(g_alog + g_alog0),
        _pad_rows(g_dtb + g_dtb0), _pad_rows(g_gd_nw), g_fw.reshape(8, 128), (g_cw + g_cw0).reshape(48, 128),
        dh0[CHUNK - N_META:CHUNK].reshape(128, 128), loss_part], axis=0)
    g_w_in_t, g_w_out, small = reduce_gradients(
        [([(g_w_hg, 4 * WIDTH), (g_w_gd, 4 * WIDTH), (g_w_ab, 2 * HEADS)], col_shard),
         ([(g_w_out, 2 * WIDTH)], (2 * WIDTH) // N_DEV)], small, "reduce_gradients")
    g_norm_w = small[0:8].reshape(1, D_MODEL)
    g_lb = small[8:16].reshape(2, WIDTH)
    g_hg_nw = small[16:17]
    g_alog = small[24:25, 0:HEADS]
    g_dtb = small[32:33, 0:HEADS]
    g_gd_nw = small[40:41]
    g_fw = small[48:56].reshape(1, D_MODEL)
    g_cw_full = small[56:104].reshape(CONV_TAPS, QKV)
    g_meta_full = small[104:232].reshape(N_META, D_MODEL)
    loss = small[232, 0]
    g_conv = lax.dynamic_slice_in_dim(g_cw_full, dev * (QKV // N_DEV), QKV // N_DEV, axis=1)
    g_meta = lax.dynamic_slice_in_dim(g_meta_full, dev * (D_MODEL // N_DEV), D_MODEL // N_DEV, axis=1)

    names = ["meta_tokens", "norm_w", "w_in", "conv_w", "hg_lb_logits", "hg_norm_w", "gdn_A_log", "gdn_dt_bias",
             "gdn_norm_w", "w_out", "final_norm_w"]
    weights = [meta_tokens, norm_w, w_in, conv_w, hg_lb_logits, hg_norm_w, gdn_A_log, gdn_dt_bias, gdn_norm_w, w_out,
               final_norm_w]
    moms = [m_meta_tokens, m_norm_w, m_w_in, m_conv_w, m_hg_lb_logits, m_hg_norm_w, m_gdn_A_log, m_gdn_dt_bias,
            m_gdn_norm_w, m_w_out, m_final_norm_w]
    vars_ = [v_meta_tokens, v_norm_w, v_w_in, v_conv_w, v_hg_lb_logits, v_hg_norm_w, v_gdn_A_log, v_gdn_dt_bias,
             v_gdn_norm_w, v_w_out, v_final_norm_w]
    grads2d = [g_meta, g_norm_w, g_w_in_t, g_conv, g_lb, g_hg_nw, g_alog, g_dtb, g_gd_nw, g_w_out, g_fw]
    grads, deltas, new_ms, new_vs = [], [], [], []
    for nm, w, g2, m, v in zip(names, weights, grads2d, moms, vars_):
        if nm == "w_in":
            to3, back = (lambda a: jnp.transpose(a, (2, 0, 1))), (lambda a: jnp.transpose(a, (1, 2, 0)))
            g2, d, nm_, nv_ = adamw_w_in(to3(w), g2, to3(m), to3(v))
        else:
            to2d, back = (lambda a, s=g2.shape: a.reshape(s)), (lambda a, s=w.shape: a.reshape(s))
            d, nm_, nv_ = adamw(to2d(w), g2, to2d(m), to2d(v), "adamw_" + nm)
        grads.append(back(g2))
        deltas.append(back(d))
        new_ms.append(back(nm_))
        new_vs.append(back(nv_))
    return (loss, grad_x.reshape(x.shape), *grads, *deltas, *new_ms, *new_vs)
```

```python
import jax
import jax.numpy as jnp
from jax import lax
from jax.experimental import pallas as pl
from jax.experimental.pallas import tpu as pltpu

F32 = jnp.float32
BF16 = jnp.bfloat16
MXU_DTYPE = BF16

D_MODEL = 1024
N_META = 16
CHUNK = 64
SUB = 16
HEADS = 4
DH = 128
WIDTH = HEADS * DH
QKV = 3 * WIDTH
CONV_TAPS = 4
HALO = 8
EPS = 1e-6
IN_COLS = 4 * WIDTH + 4 * WIDTH + 2 * HEADS
AB_PAD = 128
N_DEV = 8
LOCAL_CHUNKS = 4
SCAN_CHUNKS = 2
VMEM_LIMIT = 56 * 1024 * 1024
VMEM_LIMIT_LARGE = 60 * 1024 * 1024

ADAM_LR = 0.001
ADAM_B1 = 0.9
ADAM_B2 = 0.999
ADAM_EPS = 1e-08
ADAM_WD = 0.01
ADAM_STEP = 10

VMEM_SPEC = pl.BlockSpec(memory_space=pltpu.VMEM)
MESH = pl.DeviceIdType.MESH


def _mm_tn(a, b):
    return lax.dot_general(a.astype(MXU_DTYPE), b.astype(MXU_DTYPE), (((0,), (0,)), ((), ())), preferred_element_type=F32)


def _bmm(a, b):
    return lax.dot_general(a.astype(MXU_DTYPE), b.astype(MXU_DTYPE), (((2,), (1,)), ((0,), (0,))), preferred_element_type=F32)


def _bmm_nt(a, b):
    return lax.dot_general(a.astype(MXU_DTYPE), b.astype(MXU_DTYPE), (((2,), (2,)), ((0,), (0,))), preferred_element_type=F32)


def _bmm_tn(a, b):
    return lax.dot_general(a.astype(MXU_DTYPE), b.astype(MXU_DTYPE), (((1,), (1,)), ((0,), (0,))), preferred_element_type=F32)


def _iota2(n, m):
    return lax.broadcasted_iota(jnp.int32, (n, m), 0), lax.broadcasted_iota(jnp.int32, (n, m), 1)


def _silu(x):
    return x * jax.nn.sigmoid(x)


def _gated_norm(o, z, nw):
    return o * lax.rsqrt(jnp.mean(o * o, axis=-1, keepdims=True) + EPS) * nw * _silu(z)


def _heads(a, nb):
    return jnp.stack([a[c * CHUNK:(c + 1) * CHUNK, h * DH:(h + 1) * DH] for c in range(nb) for h in range(HEADS)], axis=0)


def _unheads(a3, nb):
    return jnp.concatenate(
        [jnp.concatenate([a3[c * HEADS + h] for h in range(HEADS)], axis=1) for c in range(nb)], axis=0)


def _split3(x):
    hi = x.astype(BF16)
    r1 = x - hi.astype(F32)
    mid = r1.astype(BF16)
    return hi, mid, (r1 - mid.astype(F32)).astype(BF16)


def _select_mm(pattern, n_out, n_in, transposed, x):
    rows, inner = (n_in, n_out) if transposed else (n_out, n_in)
    r, c = _iota2(rows, 3 * inner)
    c = c - jnp.where(c >= inner, inner, 0) - jnp.where(c >= 2 * inner, inner, 0)
    s = jnp.where(pattern(c, r) if transposed else pattern(r, c), 1.0, 0.0).astype(BF16)
    return jnp.dot(s, jnp.concatenate(_split3(x), axis=0), preferred_element_type=F32)


def _select_rows(pattern, n_out, x):
    @jax.custom_vjp
    def apply(v):
        return _select_mm(pattern, n_out, CHUNK, False, v)

    apply.defvjp(lambda v: (_select_mm(pattern, n_out, CHUNK, False, v), None),
                 lambda _, d: (_select_mm(pattern, n_out, CHUNK, True, d),))
    return apply(x)


def _cumsum_chunks(x, nb):
    return jnp.concatenate([_select_rows(lambda i, j: j <= i, CHUNK, x[c * CHUNK:(c + 1) * CHUNK]) for c in range(nb)], axis=0)


HG_LEVELS = 6


def _hg_sums(i, j):
    lvl, t = i >> HG_LEVELS, i & (CHUNK - 1)
    last = t
    for l in range(1, HG_LEVELS + 1):
        width = HG_LEVELS + 1 - l
        last = jnp.where(lvl == l, ((t >> width) << width) + (CHUNK >> l) - 1, last)
    return j <= last


def hg_local(p, logits):
    nb = p.shape[0] // CHUNK
    l0, l1 = logits[0:1], logits[1:2]
    mx = jnp.maximum(l0, l1)
    e0, e1 = jnp.exp(l0 - mx), jnp.exp(l1 - mx)
    lb = e0 / (e0 + e1)
    q = _silu(p[:, 0:WIDTH])
    f = lb + (1.0 - lb) * jax.nn.sigmoid(p[:, WIDTH:2 * WIDTH])
    k = 1.0 - f
    logf = jnp.log(f)
    sums = [_select_rows(_hg_sums, (HG_LEVELS + 1) * CHUNK, logf[c * CHUNK:(c + 1) * CHUNK]) for c in range(nb)]
    level = lambda l: _heads(jnp.concatenate([s[l * CHUNK:(l + 1) * CHUNK] for s in sums], axis=0), nb)
    q3, k3, v3, g3 = _heads(q, nb), _heads(k, nb), _heads(p[:, 2 * WIDTH:3 * WIDTH], nb), level(0)
    r, c = _iota2(CHUNK, CHUNK)
    row = lax.broadcasted_iota(jnp.int32, (CHUNK, DH), 0)
    a = jnp.where(r == c, _bmm_nt(q3, k3), 0.0)
    for l in range(1, HG_LEVELS + 1):
        sh = HG_LEVELS - l
        qk = jnp.where(((row >> sh) & 1) == 1, q3, k3) * jnp.exp(-jnp.abs(g3 - level(l)))
        pair = ((r >> (sh + 1)) == (c >> (sh + 1))) & (((r >> sh) & 1) == 1) & (((c >> sh) & 1) == 0)
        a = a + jnp.where(pair, _bmm_nt(qk, qk), 0.0)
    o = _bmm(a, v3)
    glast = g3[:, CHUNK - 1:CHUNK, :]
    egs = tuple(jnp.concatenate([jnp.exp(glast[c * HEADS + h]) for h in range(HEADS)], axis=1) for c in range(nb))
    return _unheads(q3 * jnp.exp(g3), nb), _unheads(k3 * jnp.exp(glast - g3), nb), _unheads(o, nb), egs


def hg_scan(q_in, k_out, v, eg, o_intra, z, nw, st):
    o = o_intra + _bmm_nt(q_in, st)
    return _gated_norm(o, z, nw), st * eg + _bmm_tn(v, k_out)


def _tri_y_impl(a):
    r, c = _iota2(CHUNK, CHUNK)
    same16 = (r // SUB) == (c // SUB)
    same32 = (r // (2 * SUB)) == (c // (2 * SUB))
    a0 = jnp.where(same16, a, 0.0)
    y = -a0
    pw = _bmm(a0, a0)
    for _ in range(2):
        y = y + pw + _bmm(y, pw)
        pw = _bmm(pw, pw)
    y = y + pw + _bmm(y, pw)
    for ak in (jnp.where(same32 & jnp.logical_not(same16), a, 0.0), jnp.where(same32, 0.0, a)):
        m = ak + _bmm(y, ak)
        y = y - (m + _bmm(m, y))
    return y


@jax.custom_vjp
def _tri_y(a):
    return _tri_y_impl(a)


def _tri_y_fwd(a):
    y = _tri_y_impl(a)
    return y, y


def _tri_y_bwd(y, dy):
    n = dy + _bmm_tn(y, dy)
    return (-(n + _bmm_nt(n, y)),)


_tri_y.defvjp(_tri_y_fwd, _tri_y_bwd)


def _saved_inverse(y):
    @jax.custom_vjp
    def inverse(a):
        return y

    inverse.defvjp(lambda a: (y, None), lambda _, dy: _tri_y_bwd(y, dy))
    return inverse


def _head_rows(a3, nb):
    return jnp.concatenate([a3[g] for g in range(nb * HEADS)], axis=0)


def _rows_down(x, s):
    rows = x.shape[0]

    @jax.custom_vjp
    def rotate(v):
        return pltpu.roll(v, s, 0)

    rotate.defvjp(lambda v: (pltpu.roll(v, s, 0), None), lambda _, d: (pltpu.roll(d, rows - s, 0),))
    return rotate(x)


def gd_local(xx, ab, cw, alog, dtb, inverse=_tri_y):
    n = ab.shape[0]
    nb = n // CHUNK
    conv = cw[CONV_TAPS - 1:CONV_TAPS] * xx[HALO:HALO + n]
    for j in range(CONV_TAPS - 1):
        conv = conv + cw[j:j + 1] * _rows_down(xx, CONV_TAPS - 1 - j)[HALO:HALO + n]
    act = _silu(conv)
    x = ab + dtb
    g_all = -jnp.exp(alog) * (jnp.maximum(x, 0.0) + jnp.log1p(jnp.exp(-jnp.abs(x))))
    beta_all = jax.nn.sigmoid(ab)
    gam_all = _cumsum_chunks(g_all, nb)
    q3, k3, v3 = _heads(act[:, 0:WIDTH], nb), _heads(act[:, WIDTH:2 * WIDTH], nb), _heads(act[:, 2 * WIDTH:QKV], nb)
    q3 = q3 * lax.rsqrt(jnp.sum(q3 * q3, axis=-1, keepdims=True) + EPS) * (DH ** -0.5)
    k3 = k3 * lax.rsqrt(jnp.sum(k3 * k3, axis=-1, keepdims=True) + EPS)
    pairs = [(c, h) for c in range(nb) for h in range(HEADS)]
    beta = jnp.stack([beta_all[c * CHUNK:(c + 1) * CHUNK, HEADS + h:HEADS + h + 1] for c, h in pairs], axis=0)
    gam = jnp.stack([gam_all[c * CHUNK:(c + 1) * CHUNK, h:h + 1] for c, h in pairs], axis=0)
    gam_t = [gam_all[c * CHUNK:(c + 1) * CHUNK].T for c in range(nb)]
    gam_row = jnp.stack([gam_t[c][h:h + 1, :] for c, h in pairs], axis=0)
    glast = gam[:, CHUNK - 1:CHUNK, :]
    r, c = _iota2(CHUNK, CHUNK)
    dec = jnp.exp(jnp.where(c < r, gam - gam_row, -jnp.inf))
    y = inverse(beta * _bmm_nt(k3, k3) * dec)
    eg = jnp.exp(gam)
    rhs = jnp.concatenate([beta * v3, (beta * eg) * k3], axis=2)
    sol = rhs + _bmm(y, rhs)
    qk = _bmm_nt(q3, k3) * jnp.where(r == c, 1.0, dec)
    eas = tuple(jnp.exp(gam_all[(c + 1) * CHUNK - 1:(c + 1) * CHUNK]) for c in range(nb))
    return (_unheads(sol[:, :, 0:DH], nb), _unheads(sol[:, :, DH:2 * DH], nb), _unheads(q3 * eg, nb),
            _unheads(k3 * jnp.exp(glast - gam), nb), _head_rows(qk, nb), eas), _head_rows(y, nb)


def gd_scan(uu, ww, qe, ke, qk, ea, z, nw, s):
    u = uu - _bmm(ww, s)
    o = _bmm(qe, s) + _bmm(qk, u)
    return _gated_norm(o, z, nw), ea * s + _bmm_tn(ke, u)


def _cparams(*sem):
    return pltpu.CompilerParams(dimension_semantics=sem, vmem_limit_bytes=VMEM_LIMIT)


def _row_tile(n):
    for t in (512, 256, 128, 64):
        if n % t == 0:
            return t
    raise ValueError(f"unsupported token count {n}")


def _w_in_specs():
    once = pl.Buffered(1)
    return [pl.BlockSpec((4 * WIDTH, D_MODEL), lambda *i: (0, 0), pipeline_mode=once),
            pl.BlockSpec((4 * WIDTH, D_MODEL), lambda *i: (1, 0), pipeline_mode=once),
            pl.BlockSpec((AB_PAD, D_MODEL), lambda *i: (8 * WIDTH // AB_PAD, 0), pipeline_mode=once)]


def in_proj(h, h0, norm_w, w_t):
    n = h.shape[0]
    tm = _row_tile(n)
    nt = (((1,), (1,)), ((), ()))

    def body(h_ref, h0_ref, nw_ref, whg_ref, wgd_ref, wab_ref, phg_ref, pgd_ref, pab_ref, phg0_ref, pgd0_ref, pab0_ref, u0_ref):
        def project(x, hg_ref, gd_ref, ab_ref):
            u = (x * lax.rsqrt(jnp.mean(x * x, axis=-1, keepdims=True) + EPS) * nw_ref[...]).astype(MXU_DTYPE)
            hg_ref[...] = lax.dot_general(u, whg_ref[...], nt, preferred_element_type=F32)
            gd_ref[...] = lax.dot_general(u, wgd_ref[...], nt, preferred_element_type=F32)
            ab_ref[...] = lax.dot_general(u, wab_ref[...], nt, preferred_element_type=F32)
            return u

        @pl.when(pl.program_id(0) == 0)
        def _():
            u0_ref[...] = project(h0_ref[...], phg0_ref, pgd0_ref, pab0_ref)

        project(h_ref[...], phg_ref, pgd_ref, pab_ref)

    n0 = h0.shape[0]
    row = lambda w: pl.BlockSpec((tm, w), lambda i: (i, 0))
    lead = lambda w: pl.BlockSpec((n0, w), lambda i: (0, 0))
    widths = [4 * WIDTH, 4 * WIDTH, AB_PAD]
    return pl.pallas_call(
        body, grid=(n // tm,), name="in_proj",
        in_specs=[row(D_MODEL), lead(D_MODEL), pl.BlockSpec(norm_w.shape, lambda i: (0, 0))] + _w_in_specs(),
        out_specs=[row(w) for w in widths] + [lead(w) for w in widths] + [lead(D_MODEL)],
        out_shape=[jax.ShapeDtypeStruct((n, w), F32) for w in widths] + [jax.ShapeDtypeStruct((n0, w), F32) for w in widths]
        + [jax.ShapeDtypeStruct((n0, D_MODEL), MXU_DTYPE)],
        compiler_params=_cparams("arbitrary"),
    )(h, h0, norm_w, w_t, w_t, w_t)


def out_proj_loss(x, tgt, y_hg, y_gd, w_out, fw):
    n = x.shape[0]
    tm = _row_tile(n)
    inv_d = 1.0 / D_MODEL

    def body(x_ref, t_ref, yh_ref, yg_ref, w_ref, fw_ref, dh_ref, dyh_ref, dyg_ref, dw_ref, loss_ref, dfw_ref):
        @pl.when(pl.program_id(0) == 0)
        def _():
            dw_ref[...] = jnp.zeros_like(dw_ref)
            loss_ref[...] = jnp.zeros_like(loss_ref)
            dfw_ref[...] = jnp.zeros_like(dfw_ref)

        yh, yg = yh_ref[...], yg_ref[...]
        wa, wb = w_ref[0:WIDTH, :], w_ref[WIDTH:2 * WIDTH, :]
        h2 = x_ref[...] + jnp.dot(yh, wa, preferred_element_type=F32) + jnp.dot(yg, wb, preferred_element_type=F32)
        r2 = lax.rsqrt(jnp.mean(h2 * h2, axis=-1, keepdims=True) + EPS)
        nrm = h2 * r2
        fwv = fw_ref[...]
        err = nrm * fwv - t_ref[...]
        loss_ref[...] += jnp.full(loss_ref.shape, 0.5 * inv_d * jnp.sum(err * err), F32)
        dout = err * inv_d
        dfw_ref[...] += jnp.sum(dout * nrm, axis=0, keepdims=True)
        dn = dout * fwv
        dh2 = r2 * (dn - nrm * jnp.mean(dn * nrm, axis=-1, keepdims=True))
        dh_ref[...] = dh2
        dhb = dh2.astype(MXU_DTYPE)
        dyh_ref[...] = lax.dot_general(dhb, wa, (((1,), (1,)), ((), ())), preferred_element_type=F32)
        dyg_ref[...] = lax.dot_general(dhb, wb, (((1,), (1,)), ((), ())), preferred_element_type=F32)
        dw_ref[0:WIDTH, :] += lax.dot_general(yh, dhb, (((0,), (0,)), ((), ())), preferred_element_type=F32)
        dw_ref[WIDTH:2 * WIDTH, :] += lax.dot_general(yg, dhb, (((0,), (0,)), ((), ())), preferred_element_type=F32)

    row = lambda w: pl.BlockSpec((tm, w), lambda i: (i, 0))
    full = lambda s: pl.BlockSpec(s, lambda i: (0, 0))
    return pl.pallas_call(
        body, grid=(n // tm,), name="out_proj_loss",
        in_specs=[row(D_MODEL), row(D_MODEL), row(WIDTH), row(WIDTH), full(w_out.shape), full(fw.shape)],
        out_specs=[row(D_MODEL), row(WIDTH), row(WIDTH), full((2 * WIDTH, D_MODEL)), full((8, 128)), full((1, D_MODEL))],
        out_shape=[jax.ShapeDtypeStruct((n, D_MODEL), F32), jax.ShapeDtypeStruct((n, WIDTH), F32),
                   jax.ShapeDtypeStruct((n, WIDTH), F32), jax.ShapeDtypeStruct((2 * WIDTH, D_MODEL), F32),
                   jax.ShapeDtypeStruct((8, 128), F32), jax.ShapeDtypeStruct((1, D_MODEL), F32)],
        compiler_params=_cparams("arbitrary"),
    )(x, tgt, y_hg, y_gd, w_out, fw)


def in_proj_bwd(dphg, dpgd, dpab, w_t, h, dh2, norm_w, h0, u0, dphg0, dpgd0, dpab0):
    n = h.shape[0]
    tm = _row_tile(n)
    steps = n // tm

    def body(dphg_ref, dpgd_ref, dpab_ref, whg_ref, wgd_ref, wab_ref, h_ref, dh2_ref, nw_ref, h0_ref, u0_ref, d0hg_ref,
             d0gd_ref, d0ab_ref, dx_ref, dx0_ref, dnw_ref, ghg_ref, ggd_ref, gab_ref, acc_hg, acc_gd, acc_ab):
        i = pl.program_id(0)
        nwv = nw_ref[...]

        def norm_bwd(dps, x):
            du = jnp.dot(dps[0], whg_ref[...], preferred_element_type=F32)
            du += jnp.dot(dps[1], wgd_ref[...], preferred_element_type=F32)
            du += jnp.dot(dps[2], wab_ref[...], preferred_element_type=F32)
            r = lax.rsqrt(jnp.mean(x * x, axis=-1, keepdims=True) + EPS)
            nrm = x * r
            dn = du * nwv
            return r * (dn - nrm * jnp.mean(dn * nrm, axis=-1, keepdims=True)), nrm, jnp.sum(du * nrm, axis=0, keepdims=True)

        def accumulate(dps, u, first):
            for acc, dp in zip((acc_hg, acc_gd, acc_ab), dps):
                step = min(acc.shape[0], 512)
                for lo in range(0, acc.shape[0], step):
                    part = _mm_tn(dp[:, lo:lo + step], u)
                    acc[lo:lo + step, :] = part if first else acc[lo:lo + step, :] + part

        @pl.when(i == 0)
        def _():
            dps0 = (d0hg_ref[...], d0gd_ref[...], d0ab_ref[...])
            dx0_ref[...], _, dnw_ref[...] = norm_bwd(dps0, h0_ref[...])
            accumulate(dps0, u0_ref[...], True)

        dps = (dphg_ref[...], dpgd_ref[...], dpab_ref[...])
        dx, nrm, dnw = norm_bwd(dps, h_ref[...])
        dx_ref[...] = dh2_ref[...] + dx
        dnw_ref[...] += dnw
        accumulate(dps, (nrm * nwv).astype(MXU_DTYPE), False)

        @pl.when(i == steps - 1)
        def _():
            pltpu.sync_copy(acc_hg, ghg_ref)
            pltpu.sync_copy(acc_gd, ggd_ref)
            pltpu.sync_copy(acc_ab, gab_ref)

    row = lambda w: pl.BlockSpec((tm, w), lambda i: (i, 0))
    full = lambda a: pl.BlockSpec(a.shape, lambda i: (0, 0), pipeline_mode=pl.Buffered(1))
    anywhere = pl.BlockSpec(memory_space=pl.ANY)
    return pl.pallas_call(
        body, grid=(steps,), name="in_proj_bwd",
        in_specs=[row(4 * WIDTH), row(4 * WIDTH), row(AB_PAD)] + _w_in_specs() + [row(D_MODEL), row(D_MODEL), full(norm_w),
                                                                                   full(h0), full(u0), full(dphg0), full(dpgd0),
                                                                                   full(dpab0)],
        out_specs=[row(D_MODEL), pl.BlockSpec(h0.shape, lambda i: (0, 0)), pl.BlockSpec((1, D_MODEL), lambda i: (0, 0)),
                   anywhere, anywhere, anywhere],
        out_shape=[jax.ShapeDtypeStruct((n, D_MODEL), F32), jax.ShapeDtypeStruct(h0.shape, F32),
                   jax.ShapeDtypeStruct((1, D_MODEL), F32), jax.ShapeDtypeStruct((4 * WIDTH, D_MODEL), F32),
                   jax.ShapeDtypeStruct((4 * WIDTH, D_MODEL), F32), jax.ShapeDtypeStruct((AB_PAD, D_MODEL), F32)],
        scratch_shapes=[pltpu.VMEM((4 * WIDTH, D_MODEL), F32), pltpu.VMEM((4 * WIDTH, D_MODEL), F32),
                        pltpu.VMEM((AB_PAD, D_MODEL), F32)],
        compiler_params=pltpu.CompilerParams(dimension_semantics=("arbitrary",), vmem_limit_bytes=VMEM_LIMIT_LARGE),
    )(dphg, dpgd, dpab, w_t, w_t, w_t, h, dh2, norm_w, h0, u0, dphg0, dpgd0, dpab0)


def _sds(shape, dtype=F32):
    return jax.ShapeDtypeStruct(shape, dtype)


def _pairs(b):
    return [(i, h) for i in range(b) for h in range(HEADS)]


def _load_slabs(ref, b, k):
    return jnp.stack([ref[i, k * CHUNK:(k + 1) * CHUNK, h * DH:(h + 1) * DH].astype(F32) for i, h in _pairs(b)], axis=0)


def _lead_slabs(a, b):
    return jnp.stack([a[:, h * DH:(h + 1) * DH].astype(F32) for _, h in _pairs(b)], axis=0)


def _rows(a3, i):
    return jnp.concatenate([a3[i * HEADS + h] for h in range(HEADS)], axis=1)


def _store_slabs(ref, a3, b, k):
    for i in range(b):
        ref[i, k * CHUNK:(k + 1) * CHUNK, :] = _rows(a3, i).astype(ref.dtype)


def _sum_rows(a3, b):
    out = _rows(a3, 0)
    for i in range(1, b):
        out = out + _rows(a3, i)
    return out


def _save_states(ref, s, b, k):
    for i in range(b):
        ref[i, k] = jnp.concatenate([s[i * HEADS + h] for h in range(HEADS)], axis=0)


def _load_states(ref, b, k):
    return jnp.stack([ref[i, k, h * DH:(h + 1) * DH, :] for i, h in _pairs(b)], axis=0)


def hg_local_fwd(p, logits):
    b, seq, _ = p.shape
    rows = LOCAL_CHUNKS * CHUNK
    nreal = seq // CHUNK

    def body(p_ref, lg_ref, q_ref, k_ref, o_ref, eg_ref):
        q_in, k_out, o_intra, egs = hg_local(p_ref[...], lg_ref[...])
        q_ref[...], k_ref[...], o_ref[...] = q_in.astype(MXU_DTYPE), k_out.astype(MXU_DTYPE), o_intra
        for c in range(LOCAL_CHUNKS):
            eg_ref[c] = egs[c]

    slab = pl.BlockSpec((None, rows, WIDTH), lambda s, g: (s, g, 0))
    return pl.pallas_call(
        body, grid=(b, seq // rows), name="hgrn2_local",
        in_specs=[pl.BlockSpec((None, rows, 4 * WIDTH), lambda s, g: (s, g, 0)), pl.BlockSpec(logits.shape, lambda s, g: (0, 0))],
        out_specs=[slab, slab, slab, pl.BlockSpec((None, LOCAL_CHUNKS, 1, WIDTH), lambda s, g: (s, g, 0, 0))],
        out_shape=[_sds((b, seq, WIDTH), MXU_DTYPE)] * 2 + [_sds((b, seq, WIDTH)), _sds((b, nreal, 1, WIDTH))],
        compiler_params=_cparams("arbitrary", "arbitrary"),
    )(p, logits)


def hg_local_lead(p0, logits):
    def body(p_ref, lg_ref, q_ref, k_ref, o_ref, eg_ref):
        q_in, k_out, o_ref[...], (eg_ref[...],) = hg_local(p_ref[...], lg_ref[...])
        q_ref[...], k_ref[...] = q_in.astype(MXU_DTYPE), k_out.astype(MXU_DTYPE)

    return pl.pallas_call(
        body, name="hgrn2_local_lead", in_specs=[VMEM_SPEC] * 2, out_specs=[VMEM_SPEC] * 4,
        out_shape=[_sds((CHUNK, WIDTH), MXU_DTYPE)] * 2 + [_sds((CHUNK, WIDTH)), _sds((1, WIDTH))],
        compiler_params=pltpu.CompilerParams(vmem_limit_bytes=VMEM_LIMIT),
    )(p0, logits)


def _hg_scan_args(b, k, q_ref, k_ref, o_ref, v_ref, z_ref, eg_ref):
    eg = jnp.stack([eg_ref[i, k, :, h * DH:(h + 1) * DH] for i, h in _pairs(b)], axis=0)
    return (_load_slabs(q_ref, b, k), _load_slabs(k_ref, b, k), _load_slabs(v_ref, b, k), eg, _load_slabs(o_ref, b, k),
            _load_slabs(z_ref, b, k))


def _hg_lead_args(b, q0_ref, k0_ref, o0_ref, p0_ref, eg0_ref):
    eg = jnp.stack([eg0_ref[:, h * DH:(h + 1) * DH] for _, h in _pairs(b)], axis=0)
    return (_lead_slabs(q0_ref[...], b), _lead_slabs(k0_ref[...], b), _lead_slabs(p0_ref[:, 2 * WIDTH:3 * WIDTH], b), eg,
            _lead_slabs(o0_ref[...], b), _lead_slabs(p0_ref[:, 3 * WIDTH:4 * WIDTH], b))


def _scan_specs(b, ng, reverse):
    group = (lambda i: ng - 1 - i) if reverse else (lambda i: i)
    slab = lambda lane_block: pl.BlockSpec((b, SCAN_CHUNKS * CHUNK, WIDTH), lambda i: (0, group(i), lane_block))
    per_chunk = lambda *tail: pl.BlockSpec((b, SCAN_CHUNKS) + tail, lambda i: (0, group(i)) + (0,) * len(tail))
    const = lambda a: pl.BlockSpec(a.shape, lambda i: (0,) * a.ndim)
    return slab, per_chunk, const


def run_scans(parts, nc, name):
    n_in = [len(p["args"]) for p in parts]
    n_out = [len(p["out_shape"]) for p in parts]
    n_scr = [len(p["scratch_shapes"]) for p in parts]

    def body(*refs):
        ins, outs, scr = refs[:sum(n_in)], refs[sum(n_in):sum(n_in) + sum(n_out)], refs[sum(n_in) + sum(n_out):]
        for i, part in enumerate(parts):
            part["body"](*ins[sum(n_in[:i]):sum(n_in[:i + 1])], *outs[sum(n_out[:i]):sum(n_out[:i + 1])],
                         *scr[sum(n_scr[:i]):sum(n_scr[:i + 1])])

    flat = lambda key: [v for p in parts for v in p[key]]
    out = pl.pallas_call(body, grid=(nc,), name=name, in_specs=flat("in_specs"), out_specs=flat("out_specs"),
                         out_shape=flat("out_shape"), scratch_shapes=flat("scratch_shapes"),
                         compiler_params=_cparams("arbitrary"))(*flat("args"))
    return [out[sum(n_out[:i]):sum(n_out[:i + 1])] for i in range(len(parts))]


def hg_scan_fwd(p, p0, local, lead, nw):
    b, seq, _ = p.shape
    q_in, k_out, o_intra, eg = local
    slab, per_chunk, const = _scan_specs(b, seq // (SCAN_CHUNKS * CHUNK), False)

    def body(q_ref, k_ref, o_ref, v_ref, z_ref, eg_ref, q0_ref, k0_ref, o0_ref, p0_ref, eg0_ref, nw_ref, y_ref, ss_ref, st):
        @pl.when(pl.program_id(0) == 0)
        def _():
            st[...] = hg_scan(*_hg_lead_args(b, q0_ref, k0_ref, o0_ref, p0_ref, eg0_ref), nw_ref[...], jnp.zeros(st.shape, F32))[1]

        s = st[...]
        for k in range(SCAN_CHUNKS):
            _save_states(ss_ref, s, b, k)
            y, s = hg_scan(*_hg_scan_args(b, k, q_ref, k_ref, o_ref, v_ref, z_ref, eg_ref), nw_ref[...], s)
            _store_slabs(y_ref, y, b, k)
        st[...] = s

    return dict(
        body=body, args=(q_in, k_out, o_intra, p, p, eg, lead[0], lead[1], lead[2], p0, lead[3], nw),
        in_specs=[slab(0), slab(0), slab(0), slab(2), slab(3), per_chunk(1, WIDTH)] + [const(a) for a in lead[0:3]]
        + [const(p0), const(lead[3]), const(nw)],
        out_specs=[slab(0), per_chunk(WIDTH, DH)],
        out_shape=[_sds((b, seq, WIDTH), MXU_DTYPE), _sds((b, seq // CHUNK, WIDTH, DH))],
        scratch_shapes=[pltpu.VMEM((b * HEADS, DH, DH), F32)])


def hg_scan_bwd(p, p0, local, lead, nw, ssave, dy):
    b, seq, _ = p.shape
    ng = seq // (SCAN_CHUNKS * CHUNK)
    q_in, k_out, o_intra, eg = local
    slab, per_chunk, const = _scan_specs(b, ng, True)

    def body(q_ref, k_ref, o_ref, v_ref, z_ref, eg_ref, q0_ref, k0_ref, o0_ref, p0_ref, eg0_ref, nw_ref, ss_ref, dy_ref,
             dq_ref, dk_ref, do_ref, dv_ref, dz_ref, deg_ref, dq0_ref, dk0_ref, do0_ref, dv0_ref, dz0_ref, deg0_ref, dnw_ref,
             dst):
        i = pl.program_id(0)

        @pl.when(i == 0)
        def _():
            dst[...] = jnp.zeros_like(dst)
            dnw_ref[...] = jnp.zeros_like(dnw_ref)

        ds = dst[...]
        for k in reversed(range(SCAN_CHUNKS)):
            args = _hg_scan_args(b, k, q_ref, k_ref, o_ref, v_ref, z_ref, eg_ref)
            _, vjp = jax.vjp(hg_scan, *args, nw_ref[...], _load_states(ss_ref, b, k))
            dq, dk, dv, deg, do, dz, dnw, ds = vjp((_load_slabs(dy_ref, b, k), ds))
            dnw_ref[...] += dnw
            for ref, val in ((dq_ref, dq), (dk_ref, dk), (do_ref, do), (dv_ref, dv), (dz_ref, dz)):
                _store_slabs(ref, val, b, k)
            for j in range(b):
                deg_ref[j, k] = _rows(deg, j)
        dst[...] = ds

        @pl.when(i == ng - 1)
        def _():
            args = _hg_lead_args(b, q0_ref, k0_ref, o0_ref, p0_ref, eg0_ref)
            _, vjp = jax.vjp(hg_scan, *args, nw_ref[...], jnp.zeros(dst.shape, F32))
            dq, dk, dv, deg, do, dz, dnw, _ = vjp((jnp.zeros((b * HEADS, CHUNK, DH), F32), ds))
            dnw_ref[...] += dnw
            for ref, val in ((dq0_ref, dq), (dk0_ref, dk), (do0_ref, do), (dv0_ref, dv), (dz0_ref, dz), (deg0_ref, deg)):
                ref[...] = _sum_rows(val, b)

    lead_out = [const(a) for a in lead[0:3]] + [const(lead[0]), const(lead[0]), const(lead[3])]
    return dict(
        body=body, args=(q_in, k_out, o_intra, p, p, eg, lead[0], lead[1], lead[2], p0, lead[3], nw, ssave, dy),
        in_specs=[slab(0), slab(0), slab(0), slab(2), slab(3), per_chunk(1, WIDTH)] + [const(a) for a in lead[0:3]]
        + [const(p0), const(lead[3]), const(nw), per_chunk(WIDTH, DH), slab(0)],
        out_specs=[slab(0)] * 5 + [per_chunk(1, WIDTH)] + lead_out + [const(nw)],
        out_shape=[_sds((b, seq, WIDTH))] * 5 + [_sds(eg.shape)] + [_sds((CHUNK, WIDTH))] * 5 + [_sds((1, WIDTH)), _sds(nw.shape)],
        scratch_shapes=[pltpu.VMEM((b * HEADS, DH, DH), F32)])


def _hg_local_vjp(p, logits, dq, dk, do, degs, dv, dz):
    _, vjp = jax.vjp(hg_local, p, logits)
    dp, dlg = vjp((dq, dk, do, degs))
    return dp + jnp.concatenate([jnp.zeros((p.shape[0], 2 * WIDTH), F32), dv, dz], axis=1), dlg


def hg_local_bwd(p, logits, dq, dk, do, dv, dz, deg):
    b, seq, _ = p.shape
    rows = LOCAL_CHUNKS * CHUNK

    def body(p_ref, lg_ref, dq_ref, dk_ref, do_ref, dv_ref, dz_ref, deg_ref, dp_ref, dlg_ref):
        @pl.when((pl.program_id(0) == 0) & (pl.program_id(1) == 0))
        def _():
            dlg_ref[...] = jnp.zeros_like(dlg_ref)

        degs = tuple(deg_ref[c] for c in range(LOCAL_CHUNKS))
        dp, dlg = _hg_local_vjp(p_ref[...], lg_ref[...], dq_ref[...], dk_ref[...], do_ref[...], degs, dv_ref[...], dz_ref[...])
        dp_ref[...] = dp.astype(MXU_DTYPE)
        dlg_ref[...] += dlg

    slab = pl.BlockSpec((None, rows, WIDTH), lambda s, g: (s, g, 0))
    wide = pl.BlockSpec((None, rows, 4 * WIDTH), lambda s, g: (s, g, 0))
    lg = pl.BlockSpec(logits.shape, lambda s, g: (0, 0))
    return pl.pallas_call(
        body, grid=(b, seq // rows), name="hgrn2_local_bwd",
        in_specs=[wide, lg, slab, slab, slab, slab, slab, pl.BlockSpec((None, LOCAL_CHUNKS, 1, WIDTH), lambda s, g: (s, g, 0, 0))],
        out_specs=[wide, lg], out_shape=[_sds(p.shape, MXU_DTYPE), _sds(logits.shape)],
        compiler_params=_cparams("arbitrary", "arbitrary"),
    )(p, logits, dq, dk, do, dv, dz, deg)


def hg_local_bwd_lead(p0, logits, dq, dk, do, dv, dz, deg):
    def body(p_ref, lg_ref, dq_ref, dk_ref, do_ref, dv_ref, dz_ref, deg_ref, dp_ref, dlg_ref):
        dp, dlg_ref[...] = _hg_local_vjp(p_ref[...], lg_ref[...], dq_ref[...], dk_ref[...], do_ref[...],
                                         (deg_ref[...],), dv_ref[...], dz_ref[...])
        dp_ref[...] = dp.astype(MXU_DTYPE)

    return pl.pallas_call(
        body, name="hgrn2_local_bwd_lead", in_specs=[VMEM_SPEC] * 8, out_specs=[VMEM_SPEC] * 2,
        out_shape=[_sds(p0.shape, MXU_DTYPE), _sds(logits.shape)], compiler_params=pltpu.CompilerParams(vmem_limit_bytes=VMEM_LIMIT),
    )(p0, logits, dq, dk, do, dv, dz, deg)


def _halo_block(g):
    return jnp.maximum((LOCAL_CHUNKS * CHUNK // HALO) * g - 1, 0)


def _gd_window(g, p_ref, halo_ref, p0_ref):
    halo = jnp.where(g == 0, p0_ref[CHUNK - HALO:CHUNK, 0:QKV], halo_ref[...])
    return jnp.concatenate([halo, p_ref[:, 0:QKV]], axis=0)


def gd_local_fwd(p, p0, ab, cw, alog, dtb):
    b, seq, _ = p.shape
    rows = LOCAL_CHUNKS * CHUNK
    nreal = seq // CHUNK

    def body(p_ref, halo_ref, p0_ref, ab_ref, cw_ref, al_ref, dt_ref, u_ref, w_ref, qe_ref, ke_ref, qk_ref, ea_ref, inv_ref):
        (uu, ww, qe, ke, qk, eas), inv = gd_local(_gd_window(pl.program_id(1), p_ref, halo_ref, p0_ref), ab_ref[...],
                                                  cw_ref[...], al_ref[...], dt_ref[...], inverse=_tri_y_impl)
        u_ref[...], w_ref[...], qe_ref[...], ke_ref[...] = uu, ww.astype(MXU_DTYPE), qe.astype(MXU_DTYPE), ke.astype(MXU_DTYPE)
        for c in range(LOCAL_CHUNKS):
            qk_ref[c] = qk[c * HEADS * CHUNK:(c + 1) * HEADS * CHUNK]
            inv_ref[c] = inv[c * HEADS * CHUNK:(c + 1) * HEADS * CHUNK]
            ea_ref[c] = eas[c]

    const = lambda a: pl.BlockSpec(a.shape, lambda s, g: (0, 0))
    slab = pl.BlockSpec((None, rows, WIDTH), lambda s, g: (s, g, 0))
    mats = pl.BlockSpec((None, LOCAL_CHUNKS, HEADS * CHUNK, CHUNK), lambda s, g: (s, g, 0, 0))
    out = pl.pallas_call(
        body, grid=(b, seq // rows), name="gdn_local",
        in_specs=[pl.BlockSpec((None, rows, 4 * WIDTH), lambda s, g: (s, g, 0)),
                  pl.BlockSpec((None, HALO, QKV), lambda s, g: (s, _halo_block(g), 0)), const(p0),
                  pl.BlockSpec((None, rows, AB_PAD), lambda s, g: (s, g, 0)), const(cw), const(alog), const(dtb)],
        out_specs=[slab] * 4 + [mats, pl.BlockSpec((None, LOCAL_CHUNKS, 1, AB_PAD), lambda s, g: (s, g, 0, 0)), mats],
        out_shape=[_sds((b, seq, WIDTH))] + [_sds((b, seq, WIDTH), MXU_DTYPE)] * 3
        + [_sds((b, nreal, HEADS * CHUNK, CHUNK)), _sds((b, nreal, 1, AB_PAD)), _sds((b, nreal, HEADS * CHUNK, CHUNK))],
        compiler_params=_cparams("arbitrary", "arbitrary"),
    )(p, p, p0, ab, cw, alog, dtb)
    return out[0:6], out[6]


def _lead_window(p0_ref):
    return jnp.concatenate([jnp.zeros((HALO, QKV), F32), p0_ref[:, 0:QKV]], axis=0)


def gd_local_lead(p0, ab0, cw, alog, dtb):
    def body(p0_ref, ab_ref, cw_ref, al_ref, dt_ref, u_ref, w_ref, qe_ref, ke_ref, qk_ref, ea_ref, inv_ref):
        (u_ref[...], ww, qe, ke, qk_ref[...], (ea_ref[...],)), inv_ref[...] = gd_local(
            _lead_window(p0_ref), ab_ref[...], cw_ref[...], al_ref[...], dt_ref[...], inverse=_tri_y_impl)
        w_ref[...], qe_ref[...], ke_ref[...] = ww.astype(MXU_DTYPE), qe.astype(MXU_DTYPE), ke.astype(MXU_DTYPE)

    out = pl.pallas_call(
        body, name="gdn_local_lead", in_specs=[VMEM_SPEC] * 5, out_specs=[VMEM_SPEC] * 7,
        out_shape=[_sds((CHUNK, WIDTH))] + [_sds((CHUNK, WIDTH), MXU_DTYPE)] * 3
        + [_sds((HEADS * CHUNK, CHUNK)), _sds((1, AB_PAD)), _sds((HEADS * CHUNK, CHUNK))],
        compiler_params=pltpu.CompilerParams(vmem_limit_bytes=VMEM_LIMIT),
    )(p0, ab0, cw, alog, dtb)
    return out[0:6], out[6]


def _gd_scan_args(b, k, u_ref, w_ref, qe_ref, ke_ref, qk_ref, ea_ref, z_ref):
    qk = jnp.stack([qk_ref[i, k, h * CHUNK:(h + 1) * CHUNK, :] for i, h in _pairs(b)], axis=0)
    ea = jnp.stack([ea_ref[i, k, :, h:h + 1] for i, h in _pairs(b)], axis=0)
    return (_load_slabs(u_ref, b, k), _load_slabs(w_ref, b, k), _load_slabs(qe_ref, b, k), _load_slabs(ke_ref, b, k), qk, ea,
            _load_slabs(z_ref, b, k))


def _gd_lead_args(b, u0_ref, w0_ref, qe0_ref, ke0_ref, qk0_ref, ea0_ref, p0_ref):
    qk = jnp.stack([qk0_ref[h * CHUNK:(h + 1) * CHUNK, :] for _, h in _pairs(b)], axis=0)
    ea = jnp.stack([ea0_ref[:, h:h + 1] for _, h in _pairs(b)], axis=0)
    return (_lead_slabs(u0_ref[...], b), _lead_slabs(w0_ref[...], b), _lead_slabs(qe0_ref[...], b), _lead_slabs(ke0_ref[...], b),
            qk, ea, _lead_slabs(p0_ref[:, QKV:QKV + WIDTH], b))


def gd_scan_fwd(p, p0, local, lead, nw):
    b, seq, _ = p.shape
    slab, per_chunk, const = _scan_specs(b, seq // (SCAN_CHUNKS * CHUNK), False)

    def body(u_ref, w_ref, qe_ref, ke_ref, qk_ref, ea_ref, z_ref, u0_ref, w0_ref, qe0_ref, ke0_ref, qk0_ref, ea0_ref, p0_ref,
             nw_ref, y_ref, ss_ref, st):
        @pl.when(pl.program_id(0) == 0)
        def _():
            lead_args = _gd_lead_args(b, u0_ref, w0_ref, qe0_ref, ke0_ref, qk0_ref, ea0_ref, p0_ref)
            st[...] = gd_scan(*lead_args, nw_ref[...], jnp.zeros(st.shape, F32))[1]

        s = st[...]
        for k in range(SCAN_CHUNKS):
            _save_states(ss_ref, s, b, k)
            y, s = gd_scan(*_gd_scan_args(b, k, u_ref, w_ref, qe_ref, ke_ref, qk_ref, ea_ref, z_ref), nw_ref[...], s)
            _store_slabs(y_ref, y, b, k)
        st[...] = s

    return dict(
        body=body, args=(*local, p, *lead, p0, nw),
        in_specs=[slab(0)] * 4 + [per_chunk(HEADS * CHUNK, CHUNK), per_chunk(1, AB_PAD), slab(3)] + [const(a) for a in lead]
        + [const(p0), const(nw)],
        out_specs=[slab(0), per_chunk(WIDTH, DH)],
        out_shape=[_sds((b, seq, WIDTH), MXU_DTYPE), _sds((b, seq // CHUNK, WIDTH, DH))],
        scratch_shapes=[pltpu.VMEM((b * HEADS, DH, DH), F32)])


def gd_scan_bwd(p, p0, local, lead, nw, ssave, dy):
    b, seq, _ = p.shape
    ng = seq // (SCAN_CHUNKS * CHUNK)
    slab, per_chunk, const = _scan_specs(b, ng, True)

    def body(u_ref, w_ref, qe_ref, ke_ref, qk_ref, ea_ref, z_ref, u0_ref, w0_ref, qe0_ref, ke0_ref, qk0_ref, ea0_ref, p0_ref,
             nw_ref, ss_ref, dy_ref, du_ref, dw_ref, dqe_ref, dke_ref, dqk_ref, dea_ref, dz_ref, du0_ref, dw0_ref, dqe0_ref,
             dke0_ref, dqk0_ref, dea0_ref, dz0_ref, dnw_ref, dst):
        i = pl.program_id(0)
        lane = lax.broadcasted_iota(jnp.int32, (1, AB_PAD), 1)

        def gate_rows(dea, j):
            return sum(jnp.where(lane == h, dea[j * HEADS + h], 0.0) for h in range(HEADS))

        def matrix_rows(dqk, j):
            return jnp.concatenate([dqk[j * HEADS + h] for h in range(HEADS)], axis=0)

        @pl.when(i == 0)
        def _():
            dst[...] = jnp.zeros_like(dst)
            dnw_ref[...] = jnp.zeros_like(dnw_ref)

        ds = dst[...]
        for k in reversed(range(SCAN_CHUNKS)):
            args = _gd_scan_args(b, k, u_ref, w_ref, qe_ref, ke_ref, qk_ref, ea_ref, z_ref)
            _, vjp = jax.vjp(gd_scan, *args, nw_ref[...], _load_states(ss_ref, b, k))
            du, dw, dqe, dke, dqk, dea, dz, dnw, ds = vjp((_load_slabs(dy_ref, b, k), ds))
            dnw_ref[...] += dnw
            for ref, val in ((du_ref, du), (dw_ref, dw), (dqe_ref, dqe), (dke_ref, dke), (dz_ref, dz)):
                _store_slabs(ref, val, b, k)
            for j in range(b):
                dqk_ref[j, k] = matrix_rows(dqk, j)
                dea_ref[j, k] = gate_rows(dea, j)
        dst[...] = ds

        @pl.when(i == ng - 1)
        def _():
            args = _gd_lead_args(b, u0_ref, w0_ref, qe0_ref, ke0_ref, qk0_ref, ea0_ref, p0_ref)
            _, vjp = jax.vjp(gd_scan, *args, nw_ref[...], jnp.zeros(dst.shape, F32))
            du, dw, dqe, dke, dqk, dea, dz, dnw, _ = vjp((jnp.zeros((b * HEADS, CHUNK, DH), F32), ds))
            dnw_ref[...] += dnw
            for ref, val in ((du0_ref, du), (dw0_ref, dw), (dqe0_ref, dqe), (dke0_ref, dke), (dz0_ref, dz)):
                ref[...] = _sum_rows(val, b)
            dqk0_ref[...] = sum((matrix_rows(dqk, j) for j in range(1, b)), matrix_rows(dqk, 0))
            dea0_ref[...] = sum((gate_rows(dea, j) for j in range(1, b)), gate_rows(dea, 0))

    uu, ww, qe, ke, qk, ea = local
    return dict(
        body=body, args=(*local, p, *lead, p0, nw, ssave, dy),
        in_specs=[slab(0)] * 4 + [per_chunk(HEADS * CHUNK, CHUNK), per_chunk(1, AB_PAD), slab(3)] + [const(a) for a in lead]
        + [const(p0), const(nw), per_chunk(WIDTH, DH), slab(0)],
        out_specs=[slab(0)] * 4 + [per_chunk(HEADS * CHUNK, CHUNK), per_chunk(1, AB_PAD), slab(0)] + [const(a) for a in lead]
        + [const(lead[0]), const(nw)],
        out_shape=[_sds((b, seq, WIDTH))] * 4 + [_sds(qk.shape), _sds(ea.shape), _sds((b, seq, WIDTH))]
        + [_sds(a.shape) for a in lead] + [_sds(lead[0].shape), _sds(nw.shape)],
        scratch_shapes=[pltpu.VMEM((b * HEADS, DH, DH), F32)])


def _gd_local_vjp(inv_rows, xx, ab, cw, alog, dtb):
    nb = ab.shape[0] // CHUNK
    inv = jnp.stack([inv_rows[g * CHUNK:(g + 1) * CHUNK] for g in range(nb * HEADS)], axis=0)
    _, vjp, _ = jax.vjp(lambda *a: gd_local(*a, inverse=_saved_inverse(inv)), xx, ab, cw, alog, dtb, has_aux=True)
    return vjp


def gd_local_bwd(p, p0, ab, cw, alog, dtb, inv, cot, dz):
    b, seq, _ = p.shape
    rows = LOCAL_CHUNKS * CHUNK
    ng = seq // rows
    du, dw, dqe, dke, dqk, dea = cot

    def body(p_ref, halo_ref, p0_ref, ab_ref, cw_ref, al_ref, dt_ref, inv_ref, du_ref, dw_ref, dqe_ref, dke_ref, dqk_ref,
             dea_ref, dz_ref, dp_ref, dab_ref, dhalo0_ref, dcw_ref, dal_ref, ddt_ref, dhalo):
        i = pl.program_id(1)
        g = ng - 1 - i

        @pl.when(i == 0)
        def _():
            dhalo[...] = jnp.zeros_like(dhalo)

        @pl.when((pl.program_id(0) == 0) & (i == 0))
        def _():
            dcw_ref[...] = jnp.zeros_like(dcw_ref)
            dal_ref[...] = jnp.zeros_like(dal_ref)
            ddt_ref[...] = jnp.zeros_like(ddt_ref)

        inv_rows = jnp.concatenate([inv_ref[c] for c in range(LOCAL_CHUNKS)], axis=0)
        vjp = _gd_local_vjp(inv_rows, _gd_window(g, p_ref, halo_ref, p0_ref), ab_ref[...], cw_ref[...], al_ref[...], dt_ref[...])
        dqk_all = jnp.concatenate([dqk_ref[c] for c in range(LOCAL_CHUNKS)], axis=0)
        deas = tuple(dea_ref[c] for c in range(LOCAL_CHUNKS))
        dxx, dab, dcw, dal, ddt = vjp((du_ref[...], dw_ref[...], dqe_ref[...], dke_ref[...], dqk_all, deas))
        dqkv = dxx[HALO:HALO + rows] + jnp.concatenate([jnp.zeros((rows - HALO, QKV), F32), dhalo[...]], axis=0)
        dhalo[...] = dxx[0:HALO]
        dhalo0_ref[...] = dxx[0:HALO]
        dp_ref[...] = jnp.concatenate([dqkv, dz_ref[...]], axis=1).astype(MXU_DTYPE)
        dab_ref[...] = dab.astype(MXU_DTYPE)
        dcw_ref[...] += dcw
        dal_ref[...] += dal
        ddt_ref[...] += ddt

    rg = lambda i: ng - 1 - i
    const = lambda a: pl.BlockSpec(a.shape, lambda s, i: (0, 0))
    slab = pl.BlockSpec((None, rows, WIDTH), lambda s, i: (s, rg(i), 0))
    wide = pl.BlockSpec((None, rows, 4 * WIDTH), lambda s, i: (s, rg(i), 0))
    gates = pl.BlockSpec((None, rows, AB_PAD), lambda s, i: (s, rg(i), 0))
    mats = pl.BlockSpec((None, LOCAL_CHUNKS, HEADS * CHUNK, CHUNK), lambda s, i: (s, rg(i), 0, 0))
    return pl.pallas_call(
        body, grid=(b, ng), name="gdn_local_bwd",
        in_specs=[wide, pl.BlockSpec((None, HALO, QKV), lambda s, i: (s, _halo_block(rg(i)), 0)), const(p0), gates, const(cw),
                  const(alog), const(dtb), mats, slab, slab, slab, slab, mats,
                  pl.BlockSpec((None, LOCAL_CHUNKS, 1, AB_PAD), lambda s, i: (s, rg(i), 0, 0)), slab],
        out_specs=[wide, gates, pl.BlockSpec((None, HALO, QKV), lambda s, i: (s, 0, 0)), const(cw), const(alog), const(dtb)],
        out_shape=[_sds(p.shape, MXU_DTYPE), _sds(ab.shape, MXU_DTYPE), _sds((b, HALO, QKV)), _sds(cw.shape), _sds(alog.shape),
                   _sds(dtb.shape)],
        scratch_shapes=[pltpu.VMEM((HALO, QKV), F32)],
        compiler_params=_cparams("arbitrary", "arbitrary"),
    )(p, p, p0, ab, cw, alog, dtb, inv, du, dw, dqe, dke, dqk, dea, dz)


def gd_local_bwd_lead(p0, ab0, cw, alog, dtb, inv, cot, dz, dtail):
    def body(p0_ref, ab_ref, cw_ref, al_ref, dt_ref, inv_ref, du_ref, dw_ref, dqe_ref, dke_ref, dqk_ref, dea_ref, dz_ref,
             dtail_ref, dp_ref, dab_ref, dcw_ref, dal_ref, ddt_ref):
        vjp = _gd_local_vjp(inv_ref[...], _lead_window(p0_ref), ab_ref[...], cw_ref[...], al_ref[...], dt_ref[...])
        dxx, dab, dcw, dal, ddt = vjp((du_ref[...], dw_ref[...], dqe_ref[...], dke_ref[...], dqk_ref[...], (dea_ref[...],)))
        dqkv = dxx[HALO:HALO + CHUNK] + jnp.concatenate([jnp.zeros((CHUNK - HALO, QKV), F32), dtail_ref[...]], axis=0)
        dp_ref[...] = jnp.concatenate([dqkv, dz_ref[...]], axis=1).astype(MXU_DTYPE)
        dab_ref[...], dcw_ref[...], dal_ref[...], ddt_ref[...] = dab.astype(MXU_DTYPE), dcw, dal, ddt

    return pl.pallas_call(
        body, name="gdn_local_bwd_lead", in_specs=[VMEM_SPEC] * 14, out_specs=[VMEM_SPEC] * 5,
        out_shape=[_sds(p0.shape, MXU_DTYPE), _sds(ab0.shape, MXU_DTYPE), _sds(cw.shape), _sds(alog.shape), _sds(dtb.shape)],
        compiler_params=pltpu.CompilerParams(vmem_limit_bytes=VMEM_LIMIT),
    )(p0, ab0, cw, alog, dtb, inv, *cot, dz, dtail)


def _position():
    return lax.axis_index("x"), lax.axis_index("y"), lax.axis_index("c")


def _exchange_blocks(bufs, send_sems, recv_sems):
    x, y, c = _position()
    me, sibling = (x, y, c), (x, y, 1 - c)
    chips = [(1 - x, y), (x, 1 - y), (1 - x, 1 - y)]
    per_buf = N_DEV - 1

    def copy(a, k, blk, to):
        rows = bufs[a].at[4 * blk[0] + 2 * blk[1] + blk[2]]
        return pltpu.make_async_remote_copy(src_ref=rows, dst_ref=rows, send_sem=send_sems.at[a * per_buf + k],
                                            recv_sem=recv_sems.at[a * per_buf + k], device_id=to, device_id_type=MESH)

    bufs_idx = range(len(bufs))
    first = [copy(a, 0, me, sibling) for a in bufs_idx] + [copy(a, 1 + j, me, (*chip, c)) for a in bufs_idx
                                                           for j, chip in enumerate(chips)]
    for cp in first:
        cp.start()
    passed = []
    for j, chip in enumerate(chips):
        for a in bufs_idx:
            copy(a, 1 + j, (*chip, c), me).wait_recv()
            passed.append(copy(a, 4 + j, (*chip, c), sibling))
            passed[-1].start()
    for a in bufs_idx:
        copy(a, 0, sibling, me).wait_recv()
        for j, chip in enumerate(chips):
            copy(a, 4 + j, (*chip, 1 - c), me).wait_recv()
    for cp in first + passed:
        cp.wait_send()


def _exchange_sems(n_bufs):
    return [pltpu.SemaphoreType.DMA((n_bufs * (N_DEV - 1),)), pltpu.SemaphoreType.DMA((n_bufs * (N_DEV - 1),))]


def gather_weights(w_in_t, w_out, small, pad_rows):
    rows, _, cols = w_in_t.shape

    def body(wi_ref, wo_ref, sm_ref, wi_out, wo_out, sm_out, wi_buf, send_sems, recv_sems):
        x, y, c = _position()
        me = 4 * x + 2 * y + c
        wi_buf[me] = wi_ref[:, 0, :].astype(MXU_DTYPE)
        wo_out[me] = wo_ref[...].astype(MXU_DTYPE)
        sm_out[me] = sm_ref[...]
        _exchange_blocks([wi_buf, wo_out, sm_out], send_sems, recv_sems)
        for d in range(N_DEV):
            wi_out[pl.ds(d * rows, rows), :] = wi_buf[d]
        wi_out[pl.ds(N_DEV * rows, pad_rows), :] = jnp.zeros((pad_rows, cols), MXU_DTYPE)

    return pl.pallas_call(
        body, name="gather_weights", in_specs=[VMEM_SPEC] * 3, out_specs=[VMEM_SPEC] * 3,
        out_shape=[jax.ShapeDtypeStruct((N_DEV * rows + pad_rows, cols), MXU_DTYPE),
                   jax.ShapeDtypeStruct((N_DEV,) + w_out.shape, MXU_DTYPE), jax.ShapeDtypeStruct((N_DEV,) + small.shape, F32)],
        scratch_shapes=[pltpu.VMEM((N_DEV, rows, cols), MXU_DTYPE)] + _exchange_sems(3),
        compiler_params=pltpu.CompilerParams(vmem_limit_bytes=VMEM_LIMIT))(w_in_t, w_out, small)


def reduce_gradients(tensors, small, name):
    n_t = len(tensors)
    arrays = [a for parts, _ in tensors for a, _ in parts]
    first_array = [sum(len(parts) for parts, _ in tensors[:t]) for t in range(n_t)]

    def pieces(t, j):
        parts, block_rows = tensors[t]
        out, base = [], 0
        for pi, (_, valid) in enumerate(parts):
            lo, hi = max(j * block_rows, base), min((j + 1) * block_rows, base + valid)
            if lo < hi:
                out.append((first_array[t] + pi, lo - base, lo - j * block_rows, hi - lo))
            base += valid
        return out

    def body(*refs):
        n_a = len(arrays)
        in_refs, small_ref = refs[:n_a], refs[n_a]
        out_refs, small_sum = refs[n_a + 1:n_a + 1 + n_t], refs[n_a + 1 + n_t]
        bufs, small_buf = refs[n_a + 2 + n_t:n_a + 2 + 5 * n_t], refs[n_a + 2 + 5 * n_t]
        s1_sems, r1_sems, s2_sems, r2_sems, small_send, small_recv = refs[n_a + 3 + 5 * n_t:]
        x, y, c = _position()
        chip = 2 * x + y

        def put(t, dst, j, add=None):
            for ai, src_row, dst_row, size in pieces(t, j):
                v = in_refs[ai][pl.ds(src_row, size), :]
                if add is not None:
                    v = v + add[pl.ds(dst_row, size), :].astype(F32)
                dst[pl.ds(dst_row, size), :] = v.astype(dst.dtype)

        def swap(t, k):
            send1, recv1 = bufs[4 * t], bufs[4 * t + 1]
            return pltpu.make_async_remote_copy(src_ref=send1.at[k], dst_ref=recv1.at[k], send_sem=s1_sems.at[4 * t + k],
                                                recv_sem=r1_sems.at[4 * t + k], device_id=(x, y, 1 - c), device_id_type=MESH)

        def to_chip(t, k, slot):
            send2, recv2 = bufs[4 * t + 2], bufs[4 * t + 3]
            return pltpu.make_async_remote_copy(src_ref=send2.at[k], dst_ref=recv2.at[slot], send_sem=s2_sems.at[4 * t + k],
                                                recv_sem=r2_sems.at[4 * t + slot], device_id=(k >> 1, k & 1, c),
                                                device_id_type=MESH)

        for t in range(n_t):
            for j in range(N_DEV):
                @pl.when((j & 1) != c)
                def _():
                    put(t, bufs[4 * t].at[j >> 1], j)
            for k in range(4):
                swap(t, k).start()

        small_buf[4 * x + 2 * y + c] = small_ref[...]
        _exchange_blocks([small_buf], small_send, small_recv)
        total = small_buf[0]
        for d in range(1, N_DEV):
            total = total + small_buf[d]
        small_sum[...] = total

        for t in range(n_t):
            recv1 = bufs[4 * t + 1]
            for k in range(4):
                swap(t, k).wait_recv()
                for j in (2 * k, 2 * k + 1):
                    @pl.when(((j & 1) == c) & (k != chip))
                    def _():
                        put(t, bufs[4 * t + 2].at[k], j, add=recv1.at[k])
                        to_chip(t, k, chip).start()

                    @pl.when(((j & 1) == c) & (k == chip))
                    def _():
                        put(t, out_refs[t], j, add=recv1.at[k])

        for t in range(n_t):
            for k in range(4):
                @pl.when(k != chip)
                def _():
                    to_chip(t, k, k).wait_recv()
                    out_refs[t][...] += bufs[4 * t + 3][k].astype(F32)

        for t in range(n_t):
            for k in range(4):
                @pl.when(k != chip)
                def _():
                    to_chip(t, k, chip).wait_send()
                swap(t, k).wait_send()

    scratch, out_shape = [], []
    for parts, block_rows in tensors:
        cols = parts[0][0].shape[1]
        scratch += [pltpu.VMEM((4, block_rows, cols), MXU_DTYPE)] * 4
        out_shape.append(jax.ShapeDtypeStruct((block_rows, cols), F32))
    out_shape.append(jax.ShapeDtypeStruct(small.shape, F32))
    scratch += [pltpu.VMEM((N_DEV,) + small.shape, F32)] + [pltpu.SemaphoreType.DMA((4 * n_t,))] * 4 + _exchange_sems(1)
    return pl.pallas_call(
        body, name=name, in_specs=[VMEM_SPEC] * (len(arrays) + 1), out_specs=[VMEM_SPEC] * (n_t + 1), out_shape=out_shape,
        scratch_shapes=scratch, compiler_params=pltpu.CompilerParams(vmem_limit_bytes=VMEM_LIMIT),
    )(*arrays, small)


def _adamw_step(w, g, m, v):
    mn = ADAM_B1 * m + (1.0 - ADAM_B1) * g
    vn = ADAM_B2 * v + (1.0 - ADAM_B2) * jnp.square(g)
    m_hat = mn / (1.0 - ADAM_B1 ** ADAM_STEP)
    v_hat = vn / (1.0 - ADAM_B2 ** ADAM_STEP)
    return -ADAM_LR * (m_hat / (jnp.sqrt(v_hat) + ADAM_EPS) + ADAM_WD * w), mn, vn


def adamw(w, g, m, v, name):
    rows, cols = w.shape
    tr = 256 if rows % 256 == 0 else rows

    def body(w_ref, g_ref, m_ref, v_ref, d_ref, nm_ref, nv_ref):
        d_ref[...], nm_ref[...], nv_ref[...] = _adamw_step(w_ref[...], g_ref[...], m_ref[...], v_ref[...])

    spec = pl.BlockSpec((tr, cols), lambda i: (i, 0))
    shape = jax.ShapeDtypeStruct((rows, cols), F32)
    return pl.pallas_call(body, grid=(rows // tr,), name=name, in_specs=[spec] * 4, out_specs=[spec] * 3,
                          out_shape=[shape] * 3, compiler_params=_cparams("arbitrary"))(w, g, m, v)


def adamw_w_in(w, g_t, m, v):
    def body(w_ref, g_ref, m_ref, v_ref, go_ref, d_ref, nm_ref, nv_ref):
        g = g_ref[...]
        go_ref[:, 0, :] = g
        d_ref[:, 0, :], nm_ref[:, 0, :], nv_ref[:, 0, :] = _adamw_step(w_ref[:, 0, :], g, m_ref[:, 0, :], v_ref[:, 0, :])

    return pl.pallas_call(body, name="adamw_w_in", in_specs=[VMEM_SPEC] * 4, out_specs=[VMEM_SPEC] * 4,
                          out_shape=[jax.ShapeDtypeStruct(w.shape, F32)] * 4,
                          compiler_params=pltpu.CompilerParams(vmem_limit_bytes=VMEM_LIMIT))(w, g_t, m, v)


def _pad_rows(a, rows=8):
    return jnp.pad(a, ((0, rows - a.shape[0]), (0, 0)))


def _pad_lanes(a, lanes=128):
    return jnp.pad(a, ((0, 0), (0, lanes - a.shape[1])))


def kernel(x, meta_tokens, norm_w, w_in, conv_w, hg_lb_logits, hg_norm_w, gdn_A_log, gdn_dt_bias, gdn_norm_w, w_out, final_norm_w, loss_target, m_meta_tokens, m_norm_w, m_w_in, m_conv_w, m_hg_lb_logits, m_hg_norm_w, m_gdn_A_log, m_gdn_dt_bias, m_gdn_norm_w, m_w_out, m_final_norm_w, v_meta_tokens, v_norm_w, v_w_in, v_conv_w, v_hg_lb_logits, v_hg_norm_w, v_gdn_A_log, v_gdn_dt_bias, v_gdn_norm_w, v_w_out, v_final_norm_w):
    b, seq, _ = x.shape
    n = b * seq
    dev = 4 * lax.axis_index("x") + 2 * lax.axis_index("y") + lax.axis_index("c")
    col_shard = IN_COLS // N_DEV

    small_w = jnp.concatenate([_pad_lanes(meta_tokens, 256), _pad_rows(_pad_lanes(conv_w[0], 256))], axis=0)
    w_t, w_out_g, small_g = gather_weights(jnp.transpose(w_in, (2, 0, 1)), w_out[0], small_w, AB_PAD - 2 * HEADS)
    meta_g = small_g[:, 0:N_META, 0:D_MODEL // N_DEV]
    conv_g = small_g[:, N_META:N_META + CONV_TAPS, 0:QKV // N_DEV]
    w_out_full = w_out_g.reshape(2 * WIDTH, D_MODEL)
    cw = jnp.transpose(conv_g, (1, 0, 2)).reshape(CONV_TAPS, QKV)
    meta = jnp.transpose(meta_g, (1, 0, 2)).reshape(N_META, D_MODEL)
    alog = _pad_lanes(gdn_A_log)
    dtb = _pad_lanes(gdn_dt_bias)
    fw = final_norm_w.reshape(1, D_MODEL)

    h0 = jnp.concatenate([jnp.zeros((CHUNK - N_META, D_MODEL), F32), meta], axis=0)
    x2 = x.reshape(n, D_MODEL)
    phg, pgd, pab, phg0, pgd0, pab0, u0 = in_proj(x2, h0, norm_w, w_t)
    phg3, pgd3, pab3 = phg.reshape(b, seq, 4 * WIDTH), pgd.reshape(b, seq, 4 * WIDTH), pab.reshape(b, seq, AB_PAD)
    nc = seq // (SCAN_CHUNKS * CHUNK)
    hg_loc = hg_local_fwd(phg3, hg_lb_logits)
    hg_lead = hg_local_lead(phg0, hg_lb_logits)
    gd_loc, gd_inv = gd_local_fwd(pgd3, pgd0, pab3, cw, alog, dtb)
    gd_lead, gd_inv0 = gd_local_lead(pgd0, pab0, cw, alog, dtb)
    (y_hg, s_hg), (y_gd, s_gd) = run_scans([hg_scan_fwd(phg3, phg0, hg_loc, hg_lead, hg_norm_w),
                                            gd_scan_fwd(pgd3, pgd0, gd_loc, gd_lead, gdn_norm_w)], nc, "scans")

    dh2, dy_hg, dy_gd, g_w_out, loss_part, g_fw = out_proj_loss(
        x2, loss_target.reshape(n, D_MODEL), y_hg.reshape(n, WIDTH), y_gd.reshape(n, WIDTH), w_out_full, fw)

    hb, gb = run_scans([hg_scan_bwd(phg3, phg0, hg_loc, hg_lead, hg_norm_w, s_hg, dy_hg.reshape(b, seq, WIDTH)),
                        gd_scan_bwd(pgd3, pgd0, gd_loc, gd_lead, gdn_norm_w, s_gd, dy_gd.reshape(b, seq, WIDTH))],
                       nc, "scans_bwd")
    dphg, g_lb = hg_local_bwd(phg3, hg_lb_logits, *hb[0:6])
    dphg0, g_lb0 = hg_local_bwd_lead(phg0, hg_lb_logits, *hb[6:12])
    g_hg_nw = hb[12]
    dpgd, dpab, dtail, g_cw, g_alog, g_dtb = gd_local_bwd(pgd3, pgd0, pab3, cw, alog, dtb, gd_inv, gb[0:6], gb[6])
    dpgd0, dpab0, g_cw0, g_alog0, g_dtb0 = gd_local_bwd_lead(pgd0, pab0, cw, alog, dtb, gd_inv0, gb[7:13], gb[13],
                                                             dtail.sum(0))
    g_gd_nw = gb[14]
    dphg, dpgd, dpab = dphg.reshape(n, 4 * WIDTH), dpgd.reshape(n, 4 * WIDTH), dpab.reshape(n, AB_PAD)

    grad_x, dh0, g_nw, g_w_hg, g_w_gd, g_w_ab = in_proj_bwd(dphg, dpgd, dpab, w_t, x2, dh2, norm_w, h0, u0, dphg0, dpgd0, dpab0)

    small = jnp.concatenate([
        g_nw.reshape(8, 128), (g_lb + g_lb0).reshape(8, 128), _pad_rows(g_hg_nw), _pad_rows(g_alog + g_alog0),
        _pad_rows(g_dtb + g_dtb0), _pad_rows(g_gd_nw), g_fw.reshape(8, 128), (g_cw + g_cw0).reshape(48, 128),
        dh0[CHUNK - N_META:CHUNK].reshape(128, 128), loss_part], axis=0)
    g_w_in_t, g_w_out, small = reduce_gradients(
        [([(g_w_hg, 4 * WIDTH), (g_w_gd, 4 * WIDTH), (g_w_ab, 2 * HEADS)], col_shard),
         ([(g_w_out, 2 * WIDTH)], (2 * WIDTH) // N_DEV)], small, "reduce_gradients")
    g_norm_w = small[0:8].reshape(1, D_MODEL)
    g_lb = small[8:16].reshape(2, WIDTH)
    g_hg_nw = small[16:17]
    g_alog = small[24:25, 0:HEADS]
    g_dtb = small[32:33, 0:HEADS]
    g_gd_nw = small[40:41]
    g_fw = small[48:56].reshape(1, D_MODEL)
    g_cw_full = small[56:104].reshape(CONV_TAPS, QKV)
    g_meta_full = small[104:232].reshape(N_META, D_MODEL)
    loss = small[232, 0]
    g_conv = lax.dynamic_slice_in_dim(g_cw_full, dev * (QKV // N_DEV), QKV // N_DEV, axis=1)
    g_meta = lax.dynamic_slice_in_dim(g_meta_full, dev * (D_MODEL // N_DEV), D_MODEL // N_DEV, axis=1)

    names = ["meta_tokens", "norm_w", "w_in", "conv_w", "hg_lb_logits", "hg_norm_w", "gdn_A_log", "gdn_dt_bias",
             "gdn_norm_w", "w_out", "final_norm_w"]
    weights = [meta_tokens, norm_w, w_in, conv_w, hg_lb_logits, hg_norm_w, gdn_A_log, gdn_dt_bias, gdn_norm_w, w_out,
               final_norm_w]
    moms = [m_meta_tokens, m_norm_w, m_w_in, m_conv_w, m_hg_lb_logits, m_hg_norm_w, m_gdn_A_log, m_gdn_dt_bias,
            m_gdn_norm_w, m_w_out, m_final_norm_w]
    vars_ = [v_meta_tokens, v_norm_w, v_w_in, v_conv_w, v_hg_lb_logits, v_hg_norm_w, v_gdn_A_log, v_gdn_dt_bias,
             v_gdn_norm_w, v_w_out, v_final_norm_w]
    grads2d = [g_meta, g_norm_w, g_w_in_t, g_conv, g_lb, g_hg_nw, g_alog, g_dtb, g_gd_nw, g_w_out, g_fw]
    grads, deltas, new_ms, new_vs = [], [], [], []
    for nm, w, g2, m, v in zip(names, weights, grads2d, moms, vars_):
        if nm == "w_in":
            to3, back = (lambda a: jnp.transpose(a, (2, 0, 1))), (lambda a: jnp.transpose(a, (1, 2, 0)))
            g2, d, nm_, nv_ = adamw_w_in(to3(w), g2, to3(m), to3(v))
        else:
            to2d, back = (lambda a, s=g2.shape: a.reshape(s)), (lambda a, s=w.shape: a.reshape(s))
            d, nm_, nv_ = adamw(to2d(w), g2, to2d(m), to2d(v), "adamw_" + nm)
        grads.append(back(g2))
        deltas.append(back(d))
        new_ms.append(back(nm_))
        new_vs.append(back(nv_))
    return (loss, grad_x.reshape(x.shape), *grads, *deltas, *new_ms, *new_vs)
```

```python
import jax
import jax.numpy as jnp
from jax import lax
from jax.experimental import pallas as pl
from jax.experimental.pallas import tpu as pltpu

F32 = jnp.float32
BF16 = jnp.bfloat16
MXU_DTYPE = BF16

D_MODEL = 1024
N_META = 16
CHUNK = 64
SUB = 16
HEADS = 4
DH = 128
WIDTH = HEADS * DH
QKV = 3 * WIDTH
CONV_TAPS = 4
HALO = 8
EPS = 1e-6
IN_COLS = 4 * WIDTH + 4 * WIDTH + 2 * HEADS
AB_PAD = 128
N_DEV = 8
LOCAL_CHUNKS = 4
SCAN_CHUNKS = 2
VMEM_LIMIT = 56 * 1024 * 1024
VMEM_LIMIT_LARGE = 60 * 1024 * 1024

ADAM_LR = 0.001
ADAM_B1 = 0.9
ADAM_B2 = 0.999
ADAM_EPS = 1e-08
ADAM_WD = 0.01
ADAM_STEP = 10

VMEM_SPEC = pl.BlockSpec(memory_space=pltpu.VMEM)
MESH = pl.DeviceIdType.MESH


def _mm_tn(a, b):
    return lax.dot_general(a.astype(MXU_DTYPE), b.astype(MXU_DTYPE), (((0,), (0,)), ((), ())), preferred_element_type=F32)


def _bmm(a, b):
    return lax.dot_general(a.astype(MXU_DTYPE), b.astype(MXU_DTYPE), (((2,), (1,)), ((0,), (0,))), preferred_element_type=F32)


def _bmm_nt(a, b):
    return lax.dot_general(a.astype(MXU_DTYPE), b.astype(MXU_DTYPE), (((2,), (2,)), ((0,), (0,))), preferred_element_type=F32)


def _bmm_tn(a, b):
    return lax.dot_general(a.astype(MXU_DTYPE), b.astype(MXU_DTYPE), (((1,), (1,)), ((0,), (0,))), preferred_element_type=F32)


def _iota2(n, m):
    return lax.broadcasted_iota(jnp.int32, (n, m), 0), lax.broadcasted_iota(jnp.int32, (n, m), 1)


def _silu(x):
    return x * jax.nn.sigmoid(x)


def _gated_norm(o, z, nw):
    return o * lax.rsqrt(jnp.mean(o * o, axis=-1, keepdims=True) + EPS) * nw * _silu(z)


def _heads(a, nb):
    return jnp.stack([a[c * CHUNK:(c + 1) * CHUNK, h * DH:(h + 1) * DH] for c in range(nb) for h in range(HEADS)], axis=0)


def _unheads(a3, nb):
    return jnp.concatenate(
        [jnp.concatenate([a3[c * HEADS + h] for h in range(HEADS)], axis=1) for c in range(nb)], axis=0)


def _split3(x):
    hi = x.astype(BF16)
    r1 = x - hi.astype(F32)
    mid = r1.astype(BF16)
    return hi, mid, (r1 - mid.astype(F32)).astype(BF16)


def _select_mm(pattern, n_out, n_in, transposed, x):
    rows, inner = (n_in, n_out) if transposed else (n_out, n_in)
    r, c = _iota2(rows, 3 * inner)
    c = c - jnp.where(c >= inner, inner, 0) - jnp.where(c >= 2 * inner, inner, 0)
    s = jnp.where(pattern(c, r) if transposed else pattern(r, c), 1.0, 0.0).astype(BF16)
    return jnp.dot(s, jnp.concatenate(_split3(x), axis=0), preferred_element_type=F32)


def _select_rows(pattern, n_out, x):
    @jax.custom_vjp
    def apply(v):
        return _select_mm(pattern, n_out, CHUNK, False, v)

    apply.defvjp(lambda v: (_select_mm(pattern, n_out, CHUNK, False, v), None),
                 lambda _, d: (_select_mm(pattern, n_out, CHUNK, True, d),))
    return apply(x)


def _cumsum_chunks(x, nb):
    return jnp.concatenate([_select_rows(lambda i, j: j <= i, CHUNK, x[c * CHUNK:(c + 1) * CHUNK]) for c in range(nb)], axis=0)


HG_LEVELS = 6


def _hg_sums(i, j):
    lvl, t = i >> HG_LEVELS, i & (CHUNK - 1)
    last = t
    for l in range(1, HG_LEVELS + 1):
        width = HG_LEVELS + 1 - l
        last = jnp.where(lvl == l, ((t >> width) << width) + (CHUNK >> l) - 1, last)
    return j <= last


def hg_local(p, logits):
    nb = p.shape[0] // CHUNK
    l0, l1 = logits[0:1], logits[1:2]
    mx = jnp.maximum(l0, l1)
    e0, e1 = jnp.exp(l0 - mx), jnp.exp(l1 - mx)
    lb = e0 / (e0 + e1)
    q = _silu(p[:, 0:WIDTH])
    f = lb + (1.0 - lb) * jax.nn.sigmoid(p[:, WIDTH:2 * WIDTH])
    k = 1.0 - f
    logf = jnp.log(f)
    sums = [_select_rows(_hg_sums, (HG_LEVELS + 1) * CHUNK, logf[c * CHUNK:(c + 1) * CHUNK]) for c in range(nb)]
    level = lambda l: _heads(jnp.concatenate([s[l * CHUNK:(l + 1) * CHUNK] for s in sums], axis=0), nb)
    q3, k3, v3, g3 = _heads(q, nb), _heads(k, nb), _heads(p[:, 2 * WIDTH:3 * WIDTH], nb), level(0)
    r, c = _iota2(CHUNK, CHUNK)
    row = lax.broadcasted_iota(jnp.int32, (CHUNK, DH), 0)
    a = jnp.where(r == c, _bmm_nt(q3, k3), 0.0)
    for l in range(1, HG_LEVELS + 1):
        sh = HG_LEVELS - l
        qk = jnp.where(((row >> sh) & 1) == 1, q3, k3) * jnp.exp(-jnp.abs(g3 - level(l)))
        pair = ((r >> (sh + 1)) == (c >> (sh + 1))) & (((r >> sh) & 1) == 1) & (((c >> sh) & 1) == 0)
        a = a + jnp.where(pair, _bmm_nt(qk, qk), 0.0)
    o = _bmm(a, v3)
    glast = g3[:, CHUNK - 1:CHUNK, :]
    egs = tuple(jnp.concatenate([jnp.exp(glast[c * HEADS + h]) for h in range(HEADS)], axis=1) for c in range(nb))
    return _unheads(q3 * jnp.exp(g3), nb), _unheads(k3 * jnp.exp(glast - g3), nb), _unheads(o, nb), egs


def hg_scan(q_in, k_out, v, eg, o_intra, z, nw, st):
    o = o_intra + _bmm_nt(q_in, st)
    return _gated_norm(o, z, nw), st * eg + _bmm_tn(v, k_out)


def _tri_y_impl(a):
    r, c = _iota2(CHUNK, CHUNK)
    same16 = (r // SUB) == (c // SUB)
    same32 = (r // (2 * SUB)) == (c // (2 * SUB))
    a0 = jnp.where(same16, a, 0.0)
    y = -a0
    pw = _bmm(a0, a0)
    for _ in range(2):
        y = y + pw + _bmm(y, pw)
        pw = _bmm(pw, pw)
    y = y + pw + _bmm(y, pw)
    for ak in (jnp.where(same32 & jnp.logical_not(same16), a, 0.0), jnp.where(same32, 0.0, a)):
        m = ak + _bmm(y, ak)
        y = y - (m + _bmm(m, y))
    return y


@jax.custom_vjp
def _tri_y(a):
    return _tri_y_impl(a)


def _tri_y_fwd(a):
    y = _tri_y_impl(a)
    return y, y


def _tri_y_bwd(y, dy):
    n = dy + _bmm_tn(y, dy)
    return (-(n + _bmm_nt(n, y)),)


_tri_y.defvjp(_tri_y_fwd, _tri_y_bwd)


def _saved_inverse(y):
    @jax.custom_vjp
    def inverse(a):
        return y

    inverse.defvjp(lambda a: (y, None), lambda _, dy: _tri_y_bwd(y, dy))
    return inverse


def _head_rows(a3, nb):
    return jnp.concatenate([a3[g] for g in range(nb * HEADS)], axis=0)


def _rows_down(x, s):
    rows = x.shape[0]

    @jax.custom_vjp
    def rotate(v):
        return pltpu.roll(v, s, 0)

    rotate.defvjp(lambda v: (pltpu.roll(v, s, 0), None), lambda _, d: (pltpu.roll(d, rows - s, 0),))
    return rotate(x)


def gd_local(xx, ab, cw, alog, dtb, inverse=_tri_y):
    n = ab.shape[0]
    nb = n // CHUNK
    conv = cw[CONV_TAPS - 1:CONV_TAPS] * xx[HALO:HALO + n]
    for j in range(CONV_TAPS - 1):
        conv = conv + cw[j:j + 1] * _rows_down(xx, CONV_TAPS - 1 - j)[HALO:HALO + n]
    act = _silu(conv)
    x = ab + dtb
    g_all = -jnp.exp(alog) * (jnp.maximum(x, 0.0) + jnp.log1p(jnp.exp(-jnp.abs(x))))
    beta_all = jax.nn.sigmoid(ab)
    gam_all = _cumsum_chunks(g_all, nb)
    q3, k3, v3 = _heads(act[:, 0:WIDTH], nb), _heads(act[:, WIDTH:2 * WIDTH], nb), _heads(act[:, 2 * WIDTH:QKV], nb)
    q3 = q3 * lax.rsqrt(jnp.sum(q3 * q3, axis=-1, keepdims=True) + EPS) * (DH ** -0.5)
    k3 = k3 * lax.rsqrt(jnp.sum(k3 * k3, axis=-1, keepdims=True) + EPS)
    pairs = [(c, h) for c in range(nb) for h in range(HEADS)]
    beta = jnp.stack([beta_all[c * CHUNK:(c + 1) * CHUNK, HEADS + h:HEADS + h + 1] for c, h in pairs], axis=0)
    gam = jnp.stack([gam_all[c * CHUNK:(c + 1) * CHUNK, h:h + 1] for c, h in pairs], axis=0)
    gam_t = [gam_all[c * CHUNK:(c + 1) * CHUNK].T for c in range(nb)]
    gam_row = jnp.stack([gam_t[c][h:h + 1, :] for c, h in pairs], axis=0)
    glast = gam[:, CHUNK - 1:CHUNK, :]
    r, c = _iota2(CHUNK, CHUNK)
    dec = jnp.exp(jnp.where(c < r, gam - gam_row, -jnp.inf))
    y = inverse(beta * _bmm_nt(k3, k3) * dec)
    eg = jnp.exp(gam)
    rhs = jnp.concatenate([beta * v3, (beta * eg) * k3], axis=2)
    sol = rhs + _bmm(y, rhs)
    qk = _bmm_nt(q3, k3) * jnp.where(r == c, 1.0, dec)
    eas = tuple(jnp.exp(gam_all[(c + 1) * CHUNK - 1:(c + 1) * CHUNK]) for c in range(nb))
    return (_unheads(sol[:, :, 0:DH], nb), _unheads(sol[:, :, DH:2 * DH], nb), _unheads(q3 * eg, nb),
            _unheads(k3 * jnp.exp(glast - gam), nb), _head_rows(qk, nb), eas), _head_rows(y, nb)


def gd_scan(uu, ww, qe, ke, qk, ea, z, nw, s):
    u = uu - _bmm(ww, s)
    o = _bmm(qe, s) + _bmm(qk, u)
    return _gated_norm(o, z, nw), ea * s + _bmm_tn(ke, u)


def _cparams(*sem):
    return pltpu.CompilerParams(dimension_semantics=sem, vmem_limit_bytes=VMEM_LIMIT)


def _row_tile(n):
    for t in (512, 256, 128, 64):
        if n % t == 0:
            return t
    raise ValueError(f"unsupported token count {n}")


def _w_in_specs():
    once = pl.Buffered(1)
    return [pl.BlockSpec((4 * WIDTH, D_MODEL), lambda *i: (0, 0), pipeline_mode=once),
            pl.BlockSpec((4 * WIDTH, D_MODEL), lambda *i: (1, 0), pipeline_mode=once),
            pl.BlockSpec((AB_PAD, D_MODEL), lambda *i: (8 * WIDTH // AB_PAD, 0), pipeline_mode=once)]


def in_proj(h, h0, norm_w, w_t):
    n = h.shape[0]
    tm = _row_tile(n)
    nt = (((1,), (1,)), ((), ()))

    def body(h_ref, h0_ref, nw_ref, whg_ref, wgd_ref, wab_ref, phg_ref, pgd_ref, pab_ref, phg0_ref, pgd0_ref, pab0_ref, u0_ref):
        def project(x, hg_ref, gd_ref, ab_ref):
            u = (x * lax.rsqrt(jnp.mean(x * x, axis=-1, keepdims=True) + EPS) * nw_ref[...]).astype(MXU_DTYPE)
            hg_ref[...] = lax.dot_general(u, whg_ref[...], nt, preferred_element_type=F32)
            gd_ref[...] = lax.dot_general(u, wgd_ref[...], nt, preferred_element_type=F32)
            ab_ref[...] = lax.dot_general(u, wab_ref[...], nt, preferred_element_type=F32)
            return u

        @pl.when(pl.program_id(0) == 0)
        def _():
            u0_ref[...] = project(h0_ref[...], phg0_ref, pgd0_ref, pab0_ref)

        project(h_ref[...], phg_ref, pgd_ref, pab_ref)

    n0 = h0.shape[0]
    row = lambda w: pl.BlockSpec((tm, w), lambda i: (i, 0))
    lead = lambda w: pl.BlockSpec((n0, w), lambda i: (0, 0))
    widths = [4 * WIDTH, 4 * WIDTH, AB_PAD]
    return pl.pallas_call(
        body, grid=(n // tm,), name="in_proj",
        in_specs=[row(D_MODEL), lead(D_MODEL), pl.BlockSpec(norm_w.shape, lambda i: (0, 0))] + _w_in_specs(),
        out_specs=[row(w) for w in widths] + [lead(w) for w in widths] + [lead(D_MODEL)],
        out_shape=[jax.ShapeDtypeStruct((n, w), F32) for w in widths] + [jax.ShapeDtypeStruct((n0, w), F32) for w in widths]
        + [jax.ShapeDtypeStruct((n0, D_MODEL), MXU_DTYPE)],
        compiler_params=_cparams("arbitrary"),
    )(h, h0, norm_w, w_t, w_t, w_t)


def out_proj_loss(x, tgt, y_hg, y_gd, w_out, fw):
    n = x.shape[0]
    tm = _row_tile(n)
    inv_d = 1.0 / D_MODEL

    def body(x_ref, t_ref, yh_ref, yg_ref, w_ref, fw_ref, dh_ref, dyh_ref, dyg_ref, dw_ref, loss_ref, dfw_ref):
        @pl.when(pl.program_id(0) == 0)
        def _():
            dw_ref[...] = jnp.zeros_like(dw_ref)
            loss_ref[...] = jnp.zeros_like(loss_ref)
            dfw_ref[...] = jnp.zeros_like(dfw_ref)

        yh, yg = yh_ref[...], yg_ref[...]
        wa, wb = w_ref[0:WIDTH, :], w_ref[WIDTH:2 * WIDTH, :]
        h2 = x_ref[...] + jnp.dot(yh, wa, preferred_element_type=F32) + jnp.dot(yg, wb, preferred_element_type=F32)
        r2 = lax.rsqrt(jnp.mean(h2 * h2, axis=-1, keepdims=True) + EPS)
        nrm = h2 * r2
        fwv = fw_ref[...]
        err = nrm * fwv - t_ref[...]
        loss_ref[...] += jnp.full(loss_ref.shape, 0.5 * inv_d * jnp.sum(err * err), F32)
        dout = err * inv_d
        dfw_ref[...] += jnp.sum(dout * nrm, axis=0, keepdims=True)
        dn = dout * fwv
        dh2 = r2 * (dn - nrm * jnp.mean(dn * nrm, axis=-1, keepdims=True))
        dh_ref[...] = dh2
        dhb = dh2.astype(MXU_DTYPE)
        dyh_ref[...] = lax.dot_general(dhb, wa, (((1,), (1,)), ((), ())), preferred_element_type=F32)
        dyg_ref[...] = lax.dot_general(dhb, wb, (((1,), (1,)), ((), ())), preferred_element_type=F32)
        dw_ref[0:WIDTH, :] += lax.dot_general(yh, dhb, (((0,), (0,)), ((), ())), preferred_element_type=F32)
        dw_ref[WIDTH:2 * WIDTH, :] += lax.dot_general(yg, dhb, (((0,), (0,)), ((), ())), preferred_element_type=F32)

    row = lambda w: pl.BlockSpec((tm, w), lambda i: (i, 0))
    full = lambda s: pl.BlockSpec(s, lambda i: (0, 0))
    return pl.pallas_call(
        body, grid=(n // tm,), name="out_proj_loss",
        in_specs=[row(D_MODEL), row(D_MODEL), row(WIDTH), row(WIDTH), full(w_out.shape), full(fw.shape)],
        out_specs=[row(D_MODEL), row(WIDTH), row(WIDTH), full((2 * WIDTH, D_MODEL)), full((8, 128)), full((1, D_MODEL))],
        out_shape=[jax.ShapeDtypeStruct((n, D_MODEL), F32), jax.ShapeDtypeStruct((n, WIDTH), F32),
                   jax.ShapeDtypeStruct((n, WIDTH), F32), jax.ShapeDtypeStruct((2 * WIDTH, D_MODEL), F32),
                   jax.ShapeDtypeStruct((8, 128), F32), jax.ShapeDtypeStruct((1, D_MODEL), F32)],
        compiler_params=_cparams("arbitrary"),
    )(x, tgt, y_hg, y_gd, w_out, fw)


def in_proj_bwd(dphg, dpgd, dpab, w_t, h, dh2, norm_w, h0, u0, dphg0, dpgd0, dpab0):
    n = h.shape[0]
    tm = _row_tile(n)
    steps = n // tm

    def body(dphg_ref, dpgd_ref, dpab_ref, whg_ref, wgd_ref, wab_ref, h_ref, dh2_ref, nw_ref, h0_ref, u0_ref, d0hg_ref,
             d0gd_ref, d0ab_ref, dx_ref, dx0_ref, dnw_ref, ghg_ref, ggd_ref, gab_ref, acc_hg, acc_gd, acc_ab):
        i = pl.program_id(0)
        nwv = nw_ref[...]

        def norm_bwd(dps, x):
            du = jnp.dot(dps[0], whg_ref[...], preferred_element_type=F32)
            du += jnp.dot(dps[1], wgd_ref[...], preferred_element_type=F32)
            du += jnp.dot(dps[2], wab_ref[...], preferred_element_type=F32)
            r = lax.rsqrt(jnp.mean(x * x, axis=-1, keepdims=True) + EPS)
            nrm = x * r
            dn = du * nwv
            return r * (dn - nrm * jnp.mean(dn * nrm, axis=-1, keepdims=True)), nrm, jnp.sum(du * nrm, axis=0, keepdims=True)

        def accumulate(dps, u, first):
            for acc, dp in zip((acc_hg, acc_gd, acc_ab), dps):
                step = min(acc.shape[0], 512)
                for lo in range(0, acc.shape[0], step):
                    part = _mm_tn(dp[:, lo:lo + step], u)
                    acc[lo:lo + step, :] = part if first else acc[lo:lo + step, :] + part

        @pl.when(i == 0)
        def _():
            dps0 = (d0hg_ref[...], d0gd_ref[...], d0ab_ref[...])
            dx0_ref[...], _, dnw_ref[...] = norm_bwd(dps0, h0_ref[...])
            accumulate(dps0, u0_ref[...], True)

        dps = (dphg_ref[...], dpgd_ref[...], dpab_ref[...])
        dx, nrm, dnw = norm_bwd(dps, h_ref[...])
        dx_ref[...] = dh2_ref[...] + dx
        dnw_ref[...] += dnw
        accumulate(dps, (nrm * nwv).astype(MXU_DTYPE), False)

        @pl.when(i == steps - 1)
        def _():
            pltpu.sync_copy(acc_hg, ghg_ref)
            pltpu.sync_copy(acc_gd, ggd_ref)
            pltpu.sync_copy(acc_ab, gab_ref)

    row = lambda w: pl.BlockSpec((tm, w), lambda i: (i, 0))
    full = lambda a: pl.BlockSpec(a.shape, lambda i: (0, 0), pipeline_mode=pl.Buffered(1))
    anywhere = pl.BlockSpec(memory_space=pl.ANY)
    return pl.pallas_call(
        body, grid=(steps,), name="in_proj_bwd",
        in_specs=[row(4 * WIDTH), row(4 * WIDTH), row(AB_PAD)] + _w_in_specs() + [row(D_MODEL), row(D_MODEL), full(norm_w),
                                                                                   full(h0), full(u0), full(dphg0), full(dpgd0),
                                                                                   full(dpab0)],
        out_specs=[row(D_MODEL), pl.BlockSpec(h0.shape, lambda i: (0, 0)), pl.BlockSpec((1, D_MODEL), lambda i: (0, 0)),
                   anywhere, anywhere, anywhere],
        out_shape=[jax.ShapeDtypeStruct((n, D_MODEL), F32), jax.ShapeDtypeStruct(h0.shape, F32),
                   jax.ShapeDtypeStruct((1, D_MODEL), F32), jax.ShapeDtypeStruct((4 * WIDTH, D_MODEL), F32),
                   jax.ShapeDtypeStruct((4 * WIDTH, D_MODEL), F32), jax.ShapeDtypeStruct((AB_PAD, D_MODEL), F32)],
        scratch_shapes=[pltpu.VMEM((4 * WIDTH, D_MODEL), F32), pltpu.VMEM((4 * WIDTH, D_MODEL), F32),
                        pltpu.VMEM((AB_PAD, D_MODEL), F32)],
        compiler_params=pltpu.CompilerParams(dimension_semantics=("arbitrary",), vmem_limit_bytes=VMEM_LIMIT_LARGE),
    )(dphg, dpgd, dpab, w_t, w_t, w_t, h, dh2, norm_w, h0, u0, dphg0, dpgd0, dpab0)


def _sds(shape, dtype=F32):
    return jax.ShapeDtypeStruct(shape, dtype)


def _pairs(b):
    return [(i, h) for i in range(b) for h in range(HEADS)]


def _load_slabs(ref, b, k):
    return jnp.stack([ref[i, k * CHUNK:(k + 1) * CHUNK, h * DH:(h + 1) * DH].astype(F32) for i, h in _pairs(b)], axis=0)


def _lead_slabs(a, b):
    return jnp.stack([a[:, h * DH:(h + 1) * DH].astype(F32) for _, h in _pairs(b)], axis=0)


def _rows(a3, i):
    return jnp.concatenate([a3[i * HEADS + h] for h in range(HEADS)], axis=1)


def _store_slabs(ref, a3, b, k):
    for i in range(b):
        ref[i, k * CHUNK:(k + 1) * CHUNK, :] = _rows(a3, i).astype(ref.dtype)


def _sum_rows(a3, b):
    out = _rows(a3, 0)
    for i in range(1, b):
        out = out + _rows(a3, i)
    return out


def _save_states(ref, s, b, k):
    for i in range(b):
        ref[i, k] = jnp.concatenate([s[i * HEADS + h] for h in range(HEADS)], axis=0)


def _load_states(ref, b, k):
    return jnp.stack([ref[i, k, h * DH:(h + 1) * DH, :] for i, h in _pairs(b)], axis=0)


def hg_local_fwd(p, logits):
    b, seq, _ = p.shape
    rows = LOCAL_CHUNKS * CHUNK
    nreal = seq // CHUNK

    def body(p_ref, lg_ref, q_ref, k_ref, o_ref, eg_ref):
        q_in, k_out, o_intra, egs = hg_local(p_ref[...], lg_ref[...])
        q_ref[...], k_ref[...], o_ref[...] = q_in.astype(MXU_DTYPE), k_out.astype(MXU_DTYPE), o_intra
        for c in range(LOCAL_CHUNKS):
            eg_ref[c] = egs[c]

    slab = pl.BlockSpec((None, rows, WIDTH), lambda s, g: (s, g, 0))
    return pl.pallas_call(
        body, grid=(b, seq // rows), name="hgrn2_local",
        in_specs=[pl.BlockSpec((None, rows, 4 * WIDTH), lambda s, g: (s, g, 0)), pl.BlockSpec(logits.shape, lambda s, g: (0, 0))],
        out_specs=[slab, slab, slab, pl.BlockSpec((None, LOCAL_CHUNKS, 1, WIDTH), lambda s, g: (s, g, 0, 0))],
        out_shape=[_sds((b, seq, WIDTH), MXU_DTYPE)] * 2 + [_sds((b, seq, WIDTH)), _sds((b, nreal, 1, WIDTH))],
        compiler_params=_cparams("arbitrary", "arbitrary"),
    )(p, logits)


def hg_local_lead(p0, logits):
    def body(p_ref, lg_ref, q_ref, k_ref, o_ref, eg_ref):
        q_in, k_out, o_ref[...], (eg_ref[...],) = hg_local(p_ref[...], lg_ref[...])
        q_ref[...], k_ref[...] = q_in.astype(MXU_DTYPE), k_out.astype(MXU_DTYPE)

    return pl.pallas_call(
        body, name="hgrn2_local_lead", in_specs=[VMEM_SPEC] * 2, out_specs=[VMEM_SPEC] * 4,
        out_shape=[_sds((CHUNK, WIDTH), MXU_DTYPE)] * 2 + [_sds((CHUNK, WIDTH)), _sds((1, WIDTH))],
        compiler_params=pltpu.CompilerParams(vmem_limit_bytes=VMEM_LIMIT),
    )(p0, logits)


def _hg_scan_args(b, k, q_ref, k_ref, o_ref, v_ref, z_ref, eg_ref):
    eg = jnp.stack([eg_ref[i, k, :, h * DH:(h + 1) * DH] for i, h in _pairs(b)], axis=0)
    return (_load_slabs(q_ref, b, k), _load_slabs(k_ref, b, k), _load_slabs(v_ref, b, k), eg, _load_slabs(o_ref, b, k),
            _load_slabs(z_ref, b, k))


def _hg_lead_args(b, q0_ref, k0_ref, o0_ref, p0_ref, eg0_ref):
    eg = jnp.stack([eg0_ref[:, h * DH:(h + 1) * DH] for _, h in _pairs(b)], axis=0)
    return (_lead_slabs(q0_ref[...], b), _lead_slabs(k0_ref[...], b), _lead_slabs(p0_ref[:, 2 * WIDTH:3 * WIDTH], b), eg,
            _lead_slabs(o0_ref[...], b), _lead_slabs(p0_ref[:, 3 * WIDTH:4 * WIDTH], b))


def _scan_specs(b, ng, reverse):
    group = (lambda i: ng - 1 - i) if reverse else (lambda i: i)
    slab = lambda lane_block: pl.BlockSpec((b, SCAN_CHUNKS * CHUNK, WIDTH), lambda i: (0, group(i), lane_block))
    per_chunk = lambda *tail: pl.BlockSpec((b, SCAN_CHUNKS) + tail, lambda i: (0, group(i)) + (0,) * len(tail))
    const = lambda a: pl.BlockSpec(a.shape, lambda i: (0,) * a.ndim)
    return slab, per_chunk, const


def run_scans(parts, nc, name):
    n_in = [len(p["args"]) for p in parts]
    n_out = [len(p["out_shape"]) for p in parts]
    n_scr = [len(p["scratch_shapes"]) for p in parts]

    def body(*refs):
        ins, outs, scr = refs[:sum(n_in)], refs[sum(n_in):sum(n_in) + sum(n_out)], refs[sum(n_in) + sum(n_out):]
        for i, part in enumerate(parts):
            part["body"](*ins[sum(n_in[:i]):sum(n_in[:i + 1])], *outs[sum(n_out[:i]):sum(n_out[:i + 1])],
                         *scr[sum(n_scr[:i]):sum(n_scr[:i + 1])])

    flat = lambda key: [v for p in parts for v in p[key]]
    out = pl.pallas_call(body, grid=(nc,), name=name, in_specs=flat("in_specs"), out_specs=flat("out_specs"),
                         out_shape=flat("out_shape"), scratch_shapes=flat("scratch_shapes"),
                         compiler_params=_cparams("arbitrary"))(*flat("args"))
    return [out[sum(n_out[:i]):sum(n_out[:i + 1])] for i in range(len(parts))]


def hg_scan_fwd(p, p0, local, lead, nw):
    b, seq, _ = p.shape
    q_in, k_out, o_intra, eg = local
    slab, per_chunk, const = _scan_specs(b, seq // (SCAN_CHUNKS * CHUNK), False)

    def body(q_ref, k_ref, o_ref, v_ref, z_ref, eg_ref, q0_ref, k0_ref, o0_ref, p0_ref, eg0_ref, nw_ref, y_ref, ss_ref, st):
        @pl.when(pl.program_id(0) == 0)
        def _():
            st[...] = hg_scan(*_hg_lead_args(b, q0_ref, k0_ref, o0_ref, p0_ref, eg0_ref), nw_ref[...], jnp.zeros(st.shape, F32))[1]

        s = st[...]
        for k in range(SCAN_CHUNKS):
            _save_states(ss_ref, s, b, k)
            y, s = hg_scan(*_hg_scan_args(b, k, q_ref, k_ref, o_ref, v_ref, z_ref, eg_ref), nw_ref[...], s)
            _store_slabs(y_ref, y, b, k)
        st[...] = s

    return dict(
        body=body, args=(q_in, k_out, o_intra, p, p, eg, lead[0], lead[1], lead[2], p0, lead[3], nw),
        in_specs=[slab(0), slab(0), slab(0), slab(2), slab(3), per_chunk(1, WIDTH)] + [const(a) for a in lead[0:3]]
        + [const(p0), const(lead[3]), const(nw)],
        out_specs=[slab(0), per_chunk(WIDTH, DH)],
        out_shape=[_sds((b, seq, WIDTH), MXU_DTYPE), _sds((b, seq // CHUNK, WIDTH, DH))],
        scratch_shapes=[pltpu.VMEM((b * HEADS, DH, DH), F32)])


def hg_scan_bwd(p, p0, local, lead, nw, ssave, dy):
    b, seq, _ = p.shape
    ng = seq // (SCAN_CHUNKS * CHUNK)
    q_in, k_out, o_intra, eg = local
    slab, per_chunk, const = _scan_specs(b, ng, True)

    def body(q_ref, k_ref, o_ref, v_ref, z_ref, eg_ref, q0_ref, k0_ref, o0_ref, p0_ref, eg0_ref, nw_ref, ss_ref, dy_ref,
             dq_ref, dk_ref, do_ref, dv_ref, dz_ref, deg_ref, dq0_ref, dk0_ref, do0_ref, dv0_ref, dz0_ref, deg0_ref, dnw_ref,
             dst):
        i = pl.program_id(0)

        @pl.when(i == 0)
        def _():
            dst[...] = jnp.zeros_like(dst)
            dnw_ref[...] = jnp.zeros_like(dnw_ref)

        ds = dst[...]
        for k in reversed(range(SCAN_CHUNKS)):
            args = _hg_scan_args(b, k, q_ref, k_ref, o_ref, v_ref, z_ref, eg_ref)
            _, vjp = jax.vjp(hg_scan, *args, nw_ref[...], _load_states(ss_ref, b, k))
            dq, dk, dv, deg, do, dz, dnw, ds = vjp((_load_slabs(dy_ref, b, k), ds))
            dnw_ref[...] += dnw
            for ref, val in ((dq_ref, dq), (dk_ref, dk), (do_ref, do), (dv_ref, dv), (dz_ref, dz)):
                _store_slabs(ref, val, b, k)
            for j in range(b):
                deg_ref[j, k] = _rows(deg, j)
        dst[...] = ds

        @pl.when(i == ng - 1)
        def _():
            args = _hg_lead_args(b, q0_ref, k0_ref, o0_ref, p0_ref, eg0_ref)
            _, vjp = jax.vjp(hg_scan, *args, nw_ref[...], jnp.zeros(dst.shape, F32))
            dq, dk, dv, deg, do, dz, dnw, _ = vjp((jnp.zeros((b * HEADS, CHUNK, DH), F32), ds))
            dnw_ref[...] += dnw
            for ref, val in ((dq0_ref, dq), (dk0_ref, dk), (do0_ref, do), (dv0_ref, dv), (dz0_ref, dz), (deg0_ref, deg)):
                ref[...] = _sum_rows(val, b)

    lead_out = [const(a) for a in lead[0:3]] + [const(lead[0]), const(lead[0]), const(lead[3])]
    return dict(
        body=body, args=(q_in, k_out, o_intra, p, p, eg, lead[0], lead[1], lead[2], p0, lead[3], nw, ssave, dy),
        in_specs=[slab(0), slab(0), slab(0), slab(2), slab(3), per_chunk(1, WIDTH)] + [const(a) for a in lead[0:3]]
        + [const(p0), const(lead[3]), const(nw), per_chunk(WIDTH, DH), slab(0)],
        out_specs=[slab(0)] * 5 + [per_chunk(1, WIDTH)] + lead_out + [const(nw)],
        out_shape=[_sds((b, seq, WIDTH))] * 5 + [_sds(eg.shape)] + [_sds((CHUNK, WIDTH))] * 5 + [_sds((1, WIDTH)), _sds(nw.shape)],
        scratch_shapes=[pltpu.VMEM((b * HEADS, DH, DH), F32)])


def _hg_local_vjp(p, logits, dq, dk, do, degs, dv, dz):
    _, vjp = jax.vjp(hg_local, p, logits)
    dp, dlg = vjp((dq, dk, do, degs))
    return dp + jnp.concatenate([jnp.zeros((p.shape[0], 2 * WIDTH), F32), dv, dz], axis=1), dlg


def hg_local_bwd(p, logits, dq, dk, do, dv, dz, deg):
    b, seq, _ = p.shape
    rows = LOCAL_CHUNKS * CHUNK

    def body(p_ref, lg_ref, dq_ref, dk_ref, do_ref, dv_ref, dz_ref, deg_ref, dp_ref, dlg_ref):
        @pl.when((pl.program_id(0) == 0) & (pl.program_id(1) == 0))
        def _():
            dlg_ref[...] = jnp.zeros_like(dlg_ref)

        degs = tuple(deg_ref[c] for c in range(LOCAL_CHUNKS))
        dp, dlg = _hg_local_vjp(p_ref[...], lg_ref[...], dq_ref[...], dk_ref[...], do_ref[...], degs, dv_ref[...], dz_ref[...])
        dp_ref[...] = dp.astype(MXU_DTYPE)
        dlg_ref[...] += dlg

    slab = pl.BlockSpec((None, rows, WIDTH), lambda s, g: (s, g, 0))
    wide = pl.BlockSpec((None, rows, 4 * WIDTH), lambda s, g: (s, g, 0))
    lg = pl.BlockSpec(logits.shape, lambda s, g: (0, 0))
    return pl.pallas_call(
        body, grid=(b, seq // rows), name="hgrn2_local_bwd",
        in_specs=[wide, lg, slab, slab, slab, slab, slab, pl.BlockSpec((None, LOCAL_CHUNKS, 1, WIDTH), lambda s, g: (s, g, 0, 0))],
        out_specs=[wide, lg], out_shape=[_sds(p.shape, MXU_DTYPE), _sds(logits.shape)],
        compiler_params=_cparams("arbitrary", "arbitrary"),
    )(p, logits, dq, dk, do, dv, dz, deg)


def hg_local_bwd_lead(p0, logits, dq, dk, do, dv, dz, deg):
    def body(p_ref, lg_ref, dq_ref, dk_ref, do_ref, dv_ref, dz_ref, deg_ref, dp_ref, dlg_ref):
        dp, dlg_ref[...] = _hg_local_vjp(p_ref[...], lg_ref[...], dq_ref[...], dk_ref[...], do_ref[...],
                                         (deg_ref[...],), dv_ref[...], dz_ref[...])
        dp_ref[...] = dp.astype(MXU_DTYPE)

    return pl.pallas_call(
        body, name="hgrn2_local_bwd_lead", in_specs=[VMEM_SPEC] * 8, out_specs=[VMEM_SPEC] * 2,
        out_shape=[_sds(p0.shape, MXU_DTYPE), _sds(logits.shape)], compiler_params=pltpu.CompilerParams(vmem_limit_bytes=VMEM_LIMIT),
    )(p0, logits, dq, dk, do, dv, dz, deg)


def _halo_block(g):
    return jnp.maximum((LOCAL_CHUNKS * CHUNK // HALO) * g - 1, 0)


def _gd_window(g, p_ref, halo_ref, p0_ref):
    halo = jnp.where(g == 0, p0_ref[CHUNK - HALO:CHUNK, 0:QKV], halo_ref[...])
    return jnp.concatenate([halo, p_ref[:, 0:QKV]], axis=0)


def gd_local_fwd(p, p0, ab, cw, alog, dtb):
    b, seq, _ = p.shape
    rows = LOCAL_CHUNKS * CHUNK
    nreal = seq // CHUNK

    def body(p_ref, halo_ref, p0_ref, ab_ref, cw_ref, al_ref, dt_ref, u_ref, w_ref, qe_ref, ke_ref, qk_ref, ea_ref, inv_ref):
        (uu, ww, qe, ke, qk, eas), inv = gd_local(_gd_window(pl.program_id(1), p_ref, halo_ref, p0_ref), ab_ref[...],
                                                  cw_ref[...], al_ref[...], dt_ref[...], inverse=_tri_y_impl)
        u_ref[...], w_ref[...], qe_ref[...], ke_ref[...] = uu, ww.astype(MXU_DTYPE), qe.astype(MXU_DTYPE), ke.astype(MXU_DTYPE)
        for c in range(LOCAL_CHUNKS):
            qk_ref[c] = qk[c * HEADS * CHUNK:(c + 1) * HEADS * CHUNK]
            inv_ref[c] = inv[c * HEADS * CHUNK:(c + 1) * HEADS * CHUNK]
            ea_ref[c] = eas[c]

    const = lambda a: pl.BlockSpec(a.shape, lambda s, g: (0, 0))
    slab = pl.BlockSpec((None, rows, WIDTH), lambda s, g: (s, g, 0))
    mats = pl.BlockSpec((None, LOCAL_CHUNKS, HEADS * CHUNK, CHUNK), lambda s, g: (s, g, 0, 0))
    out = pl.pallas_call(
        body, grid=(b, seq // rows), name="gdn_local",
        in_specs=[pl.BlockSpec((None, rows, 4 * WIDTH), lambda s, g: (s, g, 0)),
                  pl.BlockSpec((None, HALO, QKV), lambda s, g: (s, _halo_block(g), 0)), const(p0),
                  pl.BlockSpec((None, rows, AB_PAD), lambda s, g: (s, g, 0)), const(cw), const(alog), const(dtb)],
        out_specs=[slab] * 4 + [mats, pl.BlockSpec((None, LOCAL_CHUNKS, 1, AB_PAD), lambda s, g: (s, g, 0, 0)), mats],
        out_shape=[_sds((b, seq, WIDTH))] + [_sds((b, seq, WIDTH), MXU_DTYPE)] * 3
        + [_sds((b, nreal, HEADS * CHUNK, CHUNK)), _sds((b, nreal, 1, AB_PAD)), _sds((b, nreal, HEADS * CHUNK, CHUNK))],
        compiler_params=_cparams("arbitrary", "arbitrary"),
    )(p, p, p0, ab, cw, alog, dtb)
    return out[0:6], out[6]


def _lead_window(p0_ref):
    return jnp.concatenate([jnp.zeros((HALO, QKV), F32), p0_ref[:, 0:QKV]], axis=0)


def gd_local_lead(p0, ab0, cw, alog, dtb):
    def body(p0_ref, ab_ref, cw_ref, al_ref, dt_ref, u_ref, w_ref, qe_ref, ke_ref, qk_ref, ea_ref, inv_ref):
        (u_ref[...], ww, qe, ke, qk_ref[...], (ea_ref[...],)), inv_ref[...] = gd_local(
            _lead_window(p0_ref), ab_ref[...], cw_ref[...], al_ref[...], dt_ref[...], inverse=_tri_y_impl)
        w_ref[...], qe_ref[...], ke_ref[...] = ww.astype(MXU_DTYPE), qe.astype(MXU_DTYPE), ke.astype(MXU_DTYPE)

    out = pl.pallas_call(
        body, name="gdn_local_lead", in_specs=[VMEM_SPEC] * 5, out_specs=[VMEM_SPEC] * 7,
        out_shape=[_sds((CHUNK, WIDTH))] + [_sds((CHUNK, WIDTH), MXU_DTYPE)] * 3
        + [_sds((HEADS * CHUNK, CHUNK)), _sds((1, AB_PAD)), _sds((HEADS * CHUNK, CHUNK))],
        compiler_params=pltpu.CompilerParams(vmem_limit_bytes=VMEM_LIMIT),
    )(p0, ab0, cw, alog, dtb)
    return out[0:6], out[6]


def _gd_scan_args(b, k, u_ref, w_ref, qe_ref, ke_ref, qk_ref, ea_ref, z_ref):
    qk = jnp.stack([qk_ref[i, k, h * CHUNK:(h + 1) * CHUNK, :] for i, h in _pairs(b)], axis=0)
    ea = jnp.stack([ea_ref[i, k, :, h:h + 1] for i, h in _pairs(b)], axis=0)
    return (_load_slabs(u_ref, b, k), _load_slabs(w_ref, b, k), _load_slabs(qe_ref, b, k), _load_slabs(ke_ref, b, k), qk, ea,
            _load_slabs(z_ref, b, k))


def _gd_lead_args(b, u0_ref, w0_ref, qe0_ref, ke0_ref, qk0_ref, ea0_ref, p0_ref):
    qk = jnp.stack([qk0_ref[h * CHUNK:(h + 1) * CHUNK, :] for _, h in _pairs(b)], axis=0)
    ea = jnp.stack([ea0_ref[:, h:h + 1] for _, h in _pairs(b)], axis=0)
    return (_lead_slabs(u0_ref[...], b), _lead_slabs(w0_ref[...], b), _lead_slabs(qe0_ref[...], b), _lead_slabs(ke0_ref[...], b),
            qk, ea, _lead_slabs(p0_ref[:, QKV:QKV + WIDTH], b))


def gd_scan_fwd(p, p0, local, lead, nw):
    b, seq, _ = p.shape
    slab, per_chunk, const = _scan_specs(b, seq // (SCAN_CHUNKS * CHUNK), False)

    def body(u_ref, w_ref, qe_ref, ke_ref, qk_ref, ea_ref, z_ref, u0_ref, w0_ref, qe0_ref, ke0_ref, qk0_ref, ea0_ref, p0_ref,
             nw_ref, y_ref, ss_ref, st):
        @pl.when(pl.program_id(0) == 0)
        def _():
            lead_args = _gd_lead_args(b, u0_ref, w0_ref, qe0_ref, ke0_ref, qk0_ref, ea0_ref, p0_ref)
            st[...] = gd_scan(*lead_args, nw_ref[...], jnp.zeros(st.shape, F32))[1]

        s = st[...]
        for k in range(SCAN_CHUNKS):
            _save_states(ss_ref, s, b, k)
            y, s = gd_scan(*_gd_scan_args(b, k, u_ref, w_ref, qe_ref, ke_ref, qk_ref, ea_ref, z_ref), nw_ref[...], s)
            _store_slabs(y_ref, y, b, k)
        st[...] = s

    return dict(
        body=body, args=(*local, p, *lead, p0, nw),
        in_specs=[slab(0)] * 4 + [per_chunk(HEADS * CHUNK, CHUNK), per_chunk(1, AB_PAD), slab(3)] + [const(a) for a in lead]
        + [const(p0), const(nw)],
        out_specs=[slab(0), per_chunk(WIDTH, DH)],
        out_shape=[_sds((b, seq, WIDTH), MXU_DTYPE), _sds((b, seq // CHUNK, WIDTH, DH))],
        scratch_shapes=[pltpu.VMEM((b * HEADS, DH, DH), F32)])


def gd_scan_bwd(p, p0, local, lead, nw, ssave, dy):
    b, seq, _ = p.shape
    ng = seq // (SCAN_CHUNKS * CHUNK)
    slab, per_chunk, const = _scan_specs(b, ng, True)

    def body(u_ref, w_ref, qe_ref, ke_ref, qk_ref, ea_ref, z_ref, u0_ref, w0_ref, qe0_ref, ke0_ref, qk0_ref, ea0_ref, p0_ref,
             nw_ref, ss_ref, dy_ref, du_ref, dw_ref, dqe_ref, dke_ref, dqk_ref, dea_ref, dz_ref, du0_ref, dw0_ref, dqe0_ref,
             dke0_ref, dqk0_ref, dea0_ref, dz0_ref, dnw_ref, dst):
        i = pl.program_id(0)
        lane = lax.broadcasted_iota(jnp.int32, (1, AB_PAD), 1)

        def gate_rows(dea, j):
            return sum(jnp.where(lane == h, dea[j * HEADS + h], 0.0) for h in range(HEADS))

        def matrix_rows(dqk, j):
            return jnp.concatenate([dqk[j * HEADS + h] for h in range(HEADS)], axis=0)

        @pl.when(i == 0)
        def _():
            dst[...] = jnp.zeros_like(dst)
            dnw_ref[...] = jnp.zeros_like(dnw_ref)

        ds = dst[...]
        for k in reversed(range(SCAN_CHUNKS)):
            args = _gd_scan_args(b, k, u_ref, w_ref, qe_ref, ke_ref, qk_ref, ea_ref, z_ref)
            _, vjp = jax.vjp(gd_scan, *args, nw_ref[...], _load_states(ss_ref, b, k))
            du, dw, dqe, dke, dqk, dea, dz, dnw, ds = vjp((_load_slabs(dy_ref, b, k), ds))
            dnw_ref[...] += dnw
            for ref, val in ((du_ref, du), (dw_ref, dw), (dqe_ref, dqe), (dke_ref, dke), (dz_ref, dz)):
                _store_slabs(ref, val, b, k)
            for j in range(b):
                dqk_ref[j, k] = matrix_rows(dqk, j)
                dea_ref[j, k] = gate_rows(dea, j)
        dst[...] = ds

        @pl.when(i == ng - 1)
        def _():
            args = _gd_lead_args(b, u0_ref, w0_ref, qe0_ref, ke0_ref, qk0_ref, ea0_ref, p0_ref)
            _, vjp = jax.vjp(gd_scan, *args, nw_ref[...], jnp.zeros(dst.shape, F32))
            du, dw, dqe, dke, dqk, dea, dz, dnw, _ = vjp((jnp.zeros((b * HEADS, CHUNK, DH), F32), ds))
            dnw_ref[...] += dnw
            for ref, val in ((du0_ref, du), (dw0_ref, dw), (dqe0_ref, dqe), (dke0_ref, dke), (dz0_ref, dz)):
                ref[...] = _sum_rows(val, b)
            dqk0_ref[...] = sum((matrix_rows(dqk, j) for j in range(1, b)), matrix_rows(dqk, 0))
            dea0_ref[...] = sum((gate_rows(dea, j) for j in range(1, b)), gate_rows(dea, 0))

    uu, ww, qe, ke, qk, ea = local
    return dict(
        body=body, args=(*local, p, *lead, p0, nw, ssave, dy),
        in_specs=[slab(0)] * 4 + [per_chunk(HEADS * CHUNK, CHUNK), per_chunk(1, AB_PAD), slab(3)] + [const(a) for a in lead]
        + [const(p0), const(nw), per_chunk(WIDTH, DH), slab(0)],
        out_specs=[slab(0)] * 4 + [per_chunk(HEADS * CHUNK, CHUNK), per_chunk(1, AB_PAD), slab(0)] + [const(a) for a in lead]
        + [const(lead[0]), const(nw)],
        out_shape=[_sds((b, seq, WIDTH))] * 4 + [_sds(qk.shape), _sds(ea.shape), _sds((b, seq, WIDTH))]
        + [_sds(a.shape) for a in lead] + [_sds(lead[0].shape), _sds(nw.shape)],
        scratch_shapes=[pltpu.VMEM((b * HEADS, DH, DH), F32)])


def _gd_local_vjp(inv_rows, xx, ab, cw, alog, dtb):
    nb = ab.shape[0] // CHUNK
    inv = jnp.stack([inv_rows[g * CHUNK:(g + 1) * CHUNK] for g in range(nb * HEADS)], axis=0)
    _, vjp, _ = jax.vjp(lambda *a: gd_local(*a, inverse=_saved_inverse(inv)), xx, ab, cw, alog, dtb, has_aux=True)
    return vjp


def gd_local_bwd(p, p0, ab, cw, alog, dtb, inv, cot, dz):
    b, seq, _ = p.shape
    rows = LOCAL_CHUNKS * CHUNK
    ng = seq // rows
    du, dw, dqe, dke, dqk, dea = cot

    def body(p_ref, halo_ref, p0_ref, ab_ref, cw_ref, al_ref, dt_ref, inv_ref, du_ref, dw_ref, dqe_ref, dke_ref, dqk_ref,
             dea_ref, dz_ref, dp_ref, dab_ref, dhalo0_ref, dcw_ref, dal_ref, ddt_ref, dhalo):
        i = pl.program_id(1)
        g = ng - 1 - i

        @pl.when(i == 0)
        def _():
            dhalo[...] = jnp.zeros_like(dhalo)

        @pl.when((pl.program_id(0) == 0) & (i == 0))
        def _():
            dcw_ref[...] = jnp.zeros_like(dcw_ref)
            dal_ref[...] = jnp.zeros_like(dal_ref)
            ddt_ref[...] = jnp.zeros_like(ddt_ref)

        inv_rows = jnp.concatenate([inv_ref[c] for c in range(LOCAL_CHUNKS)], axis=0)
        vjp = _gd_local_vjp(inv_rows, _gd_window(g, p_ref, halo_ref, p0_ref), ab_ref[...], cw_ref[...], al_ref[...], dt_ref[...])
        dqk_all = jnp.concatenate([dqk_ref[c] for c in range(LOCAL_CHUNKS)], axis=0)
        deas = tuple(dea_ref[c] for c in range(LOCAL_CHUNKS))
        dxx, dab, dcw, dal, ddt = vjp((du_ref[...], dw_ref[...], dqe_ref[...], dke_ref[...], dqk_all, deas))
        dqkv = dxx[HALO:HALO + rows] + jnp.concatenate([jnp.zeros((rows - HALO, QKV), F32), dhalo[...]], axis=0)
        dhalo[...] = dxx[0:HALO]
        dhalo0_ref[...] = dxx[0:HALO]
        dp_ref[...] = jnp.concatenate([dqkv, dz_ref[...]], axis=1).astype(MXU_DTYPE)
        dab_ref[...] = dab.astype(MXU_DTYPE)
        dcw_ref[...] += dcw
        dal_ref[...] += dal
        ddt_ref[...] += ddt

    rg = lambda i: ng - 1 - i
    const = lambda a: pl.BlockSpec(a.shape, lambda s, i: (0, 0))
    slab = pl.BlockSpec((None, rows, WIDTH), lambda s, i: (s, rg(i), 0))
    wide = pl.BlockSpec((None, rows, 4 * WIDTH), lambda s, i: (s, rg(i), 0))
    gates = pl.BlockSpec((None, rows, AB_PAD), lambda s, i: (s, rg(i), 0))
    mats = pl.BlockSpec((None, LOCAL_CHUNKS, HEADS * CHUNK, CHUNK), lambda s, i: (s, rg(i), 0, 0))
    return pl.pallas_call(
        body, grid=(b, ng), name="gdn_local_bwd",
        in_specs=[wide, pl.BlockSpec((None, HALO, QKV), lambda s, i: (s, _halo_block(rg(i)), 0)), const(p0), gates, const(cw),
                  const(alog), const(dtb), mats, slab, slab, slab, slab, mats,
                  pl.BlockSpec((None, LOCAL_CHUNKS, 1, AB_PAD), lambda s, i: (s, rg(i), 0, 0)), slab],
        out_specs=[wide, gates, pl.BlockSpec((None, HALO, QKV), lambda s, i: (s, 0, 0)), const(cw), const(alog), const(dtb)],
        out_shape=[_sds(p.shape, MXU_DTYPE), _sds(ab.shape, MXU_DTYPE), _sds((b, HALO, QKV)), _sds(cw.shape), _sds(alog.shape),
                   _sds(dtb.shape)],
        scratch_shapes=[pltpu.VMEM((HALO, QKV), F32)],
        compiler_params=_cparams("arbitrary", "arbitrary"),
    )(p, p, p0, ab, cw, alog, dtb, inv, du, dw, dqe, dke, dqk, dea, dz)


def gd_local_bwd_lead(p0, ab0, cw, alog, dtb, inv, cot, dz, dtail):
    def body(p0_ref, ab_ref, cw_ref, al_ref, dt_ref, inv_ref, du_ref, dw_ref, dqe_ref, dke_ref, dqk_ref, dea_ref, dz_ref,
             dtail_ref, dp_ref, dab_ref, dcw_ref, dal_ref, ddt_ref):
        vjp = _gd_local_vjp(inv_ref[...], _lead_window(p0_ref), ab_ref[...], cw_ref[...], al_ref[...], dt_ref[...])
        dxx, dab, dcw, dal, ddt = vjp((du_ref[...], dw_ref[...], dqe_ref[...], dke_ref[...], dqk_ref[...], (dea_ref[...],)))
        dqkv = dxx[HALO:HALO + CHUNK] + jnp.concatenate([jnp.zeros((CHUNK - HALO, QKV), F32), dtail_ref[...]], axis=0)
        dp_ref[...] = jnp.concatenate([dqkv, dz_ref[...]], axis=1).astype(MXU_DTYPE)
        dab_ref[...], dcw_ref[...], dal_ref[...], ddt_ref[...] = dab.astype(MXU_DTYPE), dcw, dal, ddt

    return pl.pallas_call(
        body, name="gdn_local_bwd_lead", in_specs=[VMEM_SPEC] * 14, out_specs=[VMEM_SPEC] * 5,
        out_shape=[_sds(p0.shape, MXU_DTYPE), _sds(ab0.shape, MXU_DTYPE), _sds(cw.shape), _sds(alog.shape), _sds(dtb.shape)],
        compiler_params=pltpu.CompilerParams(vmem_limit_bytes=VMEM_LIMIT),
    )(p0, ab0, cw, alog, dtb, inv, *cot, dz, dtail)


def _position():
    return lax.axis_index("x"), lax.axis_index("y"), lax.axis_index("c")


EXCHANGE_COPIES = 10


def _exchange_blocks(bufs, send_sems, recv_sems):
    x, y, c = _position()
    here, x_nbr, y_nbr, diag = (x, y), (1 - x, y), (x, 1 - y), (1 - x, 1 - y)
    sibling = (x, y, 1 - c)
    me = (x, y, c)
    n = range(len(bufs))

    def rows(a, chip, core, half=None):
        block = bufs[a].at[4 * chip[0] + 2 * chip[1] + core]
        if half is None:
            return block
        total = bufs[a].shape[1]
        tile = 8 * (4 // jnp.dtype(bufs[a].dtype).itemsize)
        split = total // 2 // tile * tile
        return block.at[pl.ds(0, split)] if half == 0 else block.at[pl.ds(split, total - split)]

    def copy(a, k, region, to):
        return pltpu.make_async_remote_copy(src_ref=region, dst_ref=region, send_sem=send_sems.at[a * EXCHANGE_COPIES + k],
                                            recv_sem=recv_sems.at[a * EXCHANGE_COPIES + k], device_id=to, device_id_type=MESH)

    sent = [copy(a, 0, rows(a, here, c), sibling) for a in n]
    sent += [cp for a in n for cp in (copy(a, 1, rows(a, here, c, 0), (*x_nbr, c)), copy(a, 4, rows(a, here, c, 1), (*y_nbr, c)))]
    sent += [cp for a in n for cp in (copy(a, 2, rows(a, here, c, 1), (*x_nbr, c)), copy(a, 3, rows(a, here, c, 0), (*y_nbr, c)))]
    for cp in sent:
        cp.start()

    def after(arrivals, a, k, region, to):
        for cp in arrivals:
            cp.wait_recv()
        sent.append(copy(a, k, region, to))
        sent[-1].start()

    for a in n:
        after([copy(a, 1, rows(a, x_nbr, c, 0), me)], a, 5, rows(a, x_nbr, c, 0), (*y_nbr, c))
        after([copy(a, 4, rows(a, y_nbr, c, 1), me)], a, 6, rows(a, y_nbr, c, 1), (*x_nbr, c))
    for a in n:
        after([copy(a, 2, rows(a, x_nbr, c, 1), me)], a, 7, rows(a, x_nbr, c), sibling)
        after([copy(a, 3, rows(a, y_nbr, c, 0), me)], a, 8, rows(a, y_nbr, c), sibling)
    for a in n:
        after([copy(a, 5, rows(a, diag, c, 0), me), copy(a, 6, rows(a, diag, c, 1), me)], a, 9, rows(a, diag, c), sibling)
    for a in n:
        copy(a, 0, rows(a, here, 1 - c), me).wait_recv()
        for k, chip in ((7, x_nbr), (8, y_nbr), (9, diag)):
            copy(a, k, rows(a, chip, 1 - c), me).wait_recv()
    for cp in sent:
        cp.wait_send()


def _exchange_sems(n_bufs):
    return [pltpu.SemaphoreType.DMA((n_bufs * EXCHANGE_COPIES,)), pltpu.SemaphoreType.DMA((n_bufs * EXCHANGE_COPIES,))]


def gather_weights(w_in_t, w_out, small, pad_rows):
    rows, _, cols = w_in_t.shape
    buf_rows = -(-rows // 16) * 16

    def body(wi_ref, wo_ref, sm_ref, wi_out, wo_out, sm_out, wi_buf, send_sems, recv_sems):
        x, y, c = _position()
        me = 4 * x + 2 * y + c
        wi_buf[me, pl.ds(0, rows), :] = wi_ref[:, 0, :].astype(MXU_DTYPE)
        wi_buf[me, pl.ds(rows, buf_rows - rows), :] = jnp.zeros((buf_rows - rows, cols), MXU_DTYPE)
        wo_out[me] = wo_ref[...].astype(MXU_DTYPE)
        sm_out[me] = sm_ref[...]
        _exchange_blocks([wi_buf, wo_out, sm_out], send_sems, recv_sems)
        for d in range(N_DEV):
            wi_out[pl.ds(d * rows, rows), :] = wi_buf[d, pl.ds(0, rows), :]
        wi_out[pl.ds(N_DEV * rows, pad_rows), :] = jnp.zeros((pad_rows, cols), MXU_DTYPE)

    return pl.pallas_call(
        body, name="gather_weights", in_specs=[VMEM_SPEC] * 3, out_specs=[VMEM_SPEC] * 3,
        out_shape=[jax.ShapeDtypeStruct((N_DEV * rows + pad_rows, cols), MXU_DTYPE),
                   jax.ShapeDtypeStruct((N_DEV,) + w_out.shape, MXU_DTYPE), jax.ShapeDtypeStruct((N_DEV,) + small.shape, F32)],
        scratch_shapes=[pltpu.VMEM((N_DEV, buf_rows, cols), MXU_DTYPE)] + _exchange_sems(3),
        compiler_params=pltpu.CompilerParams(vmem_limit_bytes=VMEM_LIMIT))(w_in_t, w_out, small)


def reduce_gradients(tensors, small, name):
    n_t = len(tensors)
    arrays = [a for parts, _ in tensors for a, _ in parts]
    first_array = [sum(len(parts) for parts, _ in tensors[:t]) for t in range(n_t)]

    def pieces(t, j):
        parts, block_rows = tensors[t]
        out, base = [], 0
        for pi, (_, valid) in enumerate(parts):
            lo, hi = max(j * block_rows, base), min((j + 1) * block_rows, base + valid)
            if lo < hi:
                out.append((first_array[t] + pi, lo - base, lo - j * block_rows, hi - lo))
            base += valid
        return out

    def body(*refs):
        n_a = len(arrays)
        in_refs, small_ref = refs[:n_a], refs[n_a]
        out_refs, small_sum = refs[n_a + 1:n_a + 1 + n_t], refs[n_a + 1 + n_t]
        bufs, small_buf = refs[n_a + 2 + n_t:n_a + 2 + 5 * n_t], refs[n_a + 2 + 5 * n_t]
        s1_sems, r1_sems, s2_sems, r2_sems, small_send, small_recv = refs[n_a + 3 + 5 * n_t:]
        x, y, c = _position()
        chip = 2 * x + y

        def put(t, dst, j, add=None):
            for ai, src_row, dst_row, size in pieces(t, j):
                v = in_refs[ai][pl.ds(src_row, size), :]
                if add is not None:
                    v = v + add[pl.ds(dst_row, size), :].astype(F32)
                dst[pl.ds(dst_row, size), :] = v.astype(dst.dtype)

        def swap(t, k):
            send1, recv1 = bufs[4 * t], bufs[4 * t + 1]
            return pltpu.make_async_remote_copy(src_ref=send1.at[k], dst_ref=recv1.at[k], send_sem=s1_sems.at[4 * t + k],
                                                recv_sem=r1_sems.at[4 * t + k], device_id=(x, y, 1 - c), device_id_type=MESH)

        def to_chip(t, k, slot):
            send2, recv2 = bufs[4 * t + 2], bufs[4 * t + 3]
            return pltpu.make_async_remote_copy(src_ref=send2.at[k], dst_ref=recv2.at[slot], send_sem=s2_sems.at[4 * t + k],
                                                recv_sem=r2_sems.at[4 * t + slot], device_id=(k >> 1, k & 1, c),
                                                device_id_type=MESH)

        for t in range(n_t):
            for j in range(N_DEV):
                @pl.when((j & 1) != c)
                def _():
                    put(t, bufs[4 * t].at[j >> 1], j)
            for k in range(4):
                swap(t, k).start()

        small_buf[4 * x + 2 * y + c] = small_ref[...]
        _exchange_blocks([small_buf], small_send, small_recv)
        total = small_buf[0]
        for d in range(1, N_DEV):
            total = total + small_buf[d]
        small_sum[...] = total

        for t in range(n_t):
            recv1 = bufs[4 * t + 1]
            for k in range(4):
                swap(t, k).wait_recv()
                for j in (2 * k, 2 * k + 1):
                    @pl.when(((j & 1) == c) & (k != chip))
                    def _():
                        put(t, bufs[4 * t + 2].at[k], j, add=recv1.at[k])
                        to_chip(t, k, chip).start()

                    @pl.when(((j & 1) == c) & (k == chip))
                    def _():
                        put(t, out_refs[t], j, add=recv1.at[k])

        for t in range(n_t):
            for k in range(4):
                @pl.when(k != chip)
                def _():
                    to_chip(t, k, k).wait_recv()
                    out_refs[t][...] += bufs[4 * t + 3][k].astype(F32)

        for t in range(n_t):
            for k in range(4):
                @pl.when(k != chip)
                def _():
                    to_chip(t, k, chip).wait_send()
                swap(t, k).wait_send()

    scratch, out_shape = [], []
    for parts, block_rows in tensors:
        cols = parts[0][0].shape[1]
        scratch += [pltpu.VMEM((4, block_rows, cols), MXU_DTYPE)] * 4
        out_shape.append(jax.ShapeDtypeStruct((block_rows, cols), F32))
    out_shape.append(jax.ShapeDtypeStruct(small.shape, F32))
    scratch += [pltpu.VMEM((N_DEV,) + small.shape, F32)] + [pltpu.SemaphoreType.DMA((4 * n_t,))] * 4 + _exchange_sems(1)
    return pl.pallas_call(
        body, name=name, in_specs=[VMEM_SPEC] * (len(arrays) + 1), out_specs=[VMEM_SPEC] * (n_t + 1), out_shape=out_shape,
        scratch_shapes=scratch, compiler_params=pltpu.CompilerParams(vmem_limit_bytes=VMEM_LIMIT),
    )(*arrays, small)


def _adamw_step(w, g, m, v):
    mn = ADAM_B1 * m + (1.0 - ADAM_B1) * g
    vn = ADAM_B2 * v + (1.0 - ADAM_B2) * jnp.square(g)
    m_hat = mn / (1.0 - ADAM_B1 ** ADAM_STEP)
    v_hat = vn / (1.0 - ADAM_B2 ** ADAM_STEP)
    return -ADAM_LR * (m_hat / (jnp.sqrt(v_hat) + ADAM_EPS) + ADAM_WD * w), mn, vn


def adamw(w, g, m, v, name):
    rows, cols = w.shape
    tr = 256 if rows % 256 == 0 else rows

    def body(w_ref, g_ref, m_ref, v_ref, d_ref, nm_ref, nv_ref):
        d_ref[...], nm_ref[...], nv_ref[...] = _adamw_step(w_ref[...], g_ref[...], m_ref[...], v_ref[...])

    spec = pl.BlockSpec((tr, cols), lambda i: (i, 0))
    shape = jax.ShapeDtypeStruct((rows, cols), F32)
    return pl.pallas_call(body, grid=(rows // tr,), name=name, in_specs=[spec] * 4, out_specs=[spec] * 3,
                          out_shape=[shape] * 3, compiler_params=_cparams("arbitrary"))(w, g, m, v)


def adamw_w_in(w, g_t, m, v):
    def body(w_ref, g_ref, m_ref, v_ref, go_ref, d_ref, nm_ref, nv_ref):
        g = g_ref[...]
        go_ref[:, 0, :] = g
        d_ref[:, 0, :], nm_ref[:, 0, :], nv_ref[:, 0, :] = _adamw_step(w_ref[:, 0, :], g, m_ref[:, 0, :], v_ref[:, 0, :])

    return pl.pallas_call(body, name="adamw_w_in", in_specs=[VMEM_SPEC] * 4, out_specs=[VMEM_SPEC] * 4,
                          out_shape=[jax.ShapeDtypeStruct(w.shape, F32)] * 4,
                          compiler_params=pltpu.CompilerParams(vmem_limit_bytes=VMEM_LIMIT))(w, g_t, m, v)


def _pad_rows(a, rows=8):
    return jnp.pad(a, ((0, rows - a.shape[0]), (0, 0)))


def _pad_lanes(a, lanes=128):
    return jnp.pad(a, ((0, 0), (0, lanes - a.shape[1])))


def kernel(x, meta_tokens, norm_w, w_in, conv_w, hg_lb_logits, hg_norm_w, gdn_A_log, gdn_dt_bias, gdn_norm_w, w_out, final_norm_w, loss_target, m_meta_tokens, m_norm_w, m_w_in, m_conv_w, m_hg_lb_logits, m_hg_norm_w, m_gdn_A_log, m_gdn_dt_bias, m_gdn_norm_w, m_w_out, m_final_norm_w, v_meta_tokens, v_norm_w, v_w_in, v_conv_w, v_hg_lb_logits, v_hg_norm_w, v_gdn_A_log, v_gdn_dt_bias, v_gdn_norm_w, v_w_out, v_final_norm_w):
    b, seq, _ = x.shape
    n = b * seq
    dev = 4 * lax.axis_index("x") + 2 * lax.axis_index("y") + lax.axis_index("c")
    col_shard = IN_COLS // N_DEV

    small_w = jnp.concatenate([_pad_lanes(meta_tokens, 256), _pad_rows(_pad_lanes(conv_w[0], 256))], axis=0)
    w_t, w_out_g, small_g = gather_weights(jnp.transpose(w_in, (2, 0, 1)), w_out[0], small_w, AB_PAD - 2 * HEADS)
    meta_g = small_g[:, 0:N_META, 0:D_MODEL // N_DEV]
    conv_g = small_g[:, N_META:N_META + CONV_TAPS, 0:QKV // N_DEV]
    w_out_full = w_out_g.reshape(2 * WIDTH, D_MODEL)
    cw = jnp.transpose(conv_g, (1, 0, 2)).reshape(CONV_TAPS, QKV)
    meta = jnp.transpose(meta_g, (1, 0, 2)).reshape(N_META, D_MODEL)
    alog = _pad_lanes(gdn_A_log)
    dtb = _pad_lanes(gdn_dt_bias)
    fw = final_norm_w.reshape(1, D_MODEL)

    h0 = jnp.concatenate([jnp.zeros((CHUNK - N_META, D_MODEL), F32), meta], axis=0)
    x2 = x.reshape(n, D_MODEL)
    phg, pgd, pab, phg0, pgd0, pab0, u0 = in_proj(x2, h0, norm_w, w_t)
    phg3, pgd3, pab3 = phg.reshape(b, seq, 4 * WIDTH), pgd.reshape(b, seq, 4 * WIDTH), pab.reshape(b, seq, AB_PAD)
    nc = seq // (SCAN_CHUNKS * CHUNK)
    hg_loc = hg_local_fwd(phg3, hg_lb_logits)
    hg_lead = hg_local_lead(phg0, hg_lb_logits)
    gd_loc, gd_inv = gd_local_fwd(pgd3, pgd0, pab3, cw, alog, dtb)
    gd_lead, gd_inv0 = gd_local_lead(pgd0, pab0, cw, alog, dtb)
    (y_hg, s_hg), (y_gd, s_gd) = run_scans([hg_scan_fwd(phg3, phg0, hg_loc, hg_lead, hg_norm_w),
                                            gd_scan_fwd(pgd3, pgd0, gd_loc, gd_lead, gdn_norm_w)], nc, "scans")

    dh2, dy_hg, dy_gd, g_w_out, loss_part, g_fw = out_proj_loss(
        x2, loss_target.reshape(n, D_MODEL), y_hg.reshape(n, WIDTH), y_gd.reshape(n, WIDTH), w_out_full, fw)

    hb, gb = run_scans([hg_scan_bwd(phg3, phg0, hg_loc, hg_lead, hg_norm_w, s_hg, dy_hg.reshape(b, seq, WIDTH)),
                        gd_scan_bwd(pgd3, pgd0, gd_loc, gd_lead, gdn_norm_w, s_gd, dy_gd.reshape(b, seq, WIDTH))],
                       nc, "scans_bwd")
    dphg, g_lb = hg_local_bwd(phg3, hg_lb_logits, *hb[0:6])
    dphg0, g_lb0 = hg_local_bwd_lead(phg0, hg_lb_logits, *hb[6:12])
    g_hg_nw = hb[12]
    dpgd, dpab, dtail, g_cw, g_alog, g_dtb = gd_local_bwd(pgd3, pgd0, pab3, cw, alog, dtb, gd_inv, gb[0:6], gb[6])
    dpgd0, dpab0, g_cw0, g_alog0, g_dtb0 = gd_local_bwd_lead(pgd0, pab0, cw, alog, dtb, gd_inv0, gb[7:13], gb[13],
                                                             dtail.sum(0))
    g_gd_nw = gb[14]
    dphg, dpgd, dpab = dphg.reshape(n, 4 * WIDTH), dpgd.reshape(n, 4 * WIDTH), dpab.reshape(n, AB_PAD)

    grad_x, dh0, g_nw, g_w_hg, g_w_gd, g_w_ab = in_proj_bwd(dphg, dpgd, dpab, w_t, x2, dh2, norm_w, h0, u0, dphg0, dpgd0, dpab0)

    small = jnp.concatenate([
        g_nw.reshape(8, 128), (g_lb + g_lb0).reshape(8, 128), _pad_rows(g_hg_nw), _pad_rows(g_alog + g_alog0),
        _pad_rows(g_dtb + g_dtb0), _pad_rows(g_gd_nw), g_fw.reshape(8, 128), (g_cw + g_cw0).reshape(48, 128),
        dh0[CHUNK - N_META:CHUNK].reshape(128, 128), loss_part], axis=0)
    g_w_in_t, g_w_out, small = reduce_gradients(
        [([(g_w_hg, 4 * WIDTH), (g_w_gd, 4 * WIDTH), (g_w_ab, 2 * HEADS)], col_shard),
         ([(g_w_out, 2 * WIDTH)], (2 * WIDTH) // N_DEV)], small, "reduce_gradients")
    g_norm_w = small[0:8].reshape(1, D_MODEL)
    g_lb = small[8:16].reshape(2, WIDTH)
    g_hg_nw = small[16:17]
    g_alog = small[24:25, 0:HEADS]
    g_dtb = small[32:33, 0:HEADS]
    g_gd_nw = small[40:41]
    g_fw = small[48:56].reshape(1, D_MODEL)
    g_cw_full = small[56:104].reshape(CONV_TAPS, QKV)
    g_meta_full = small[104:232].reshape(N_META, D_MODEL)
    loss = small[232, 0]
    g_conv = lax.dynamic_slice_in_dim(g_cw_full, dev * (QKV // N_DEV), QKV // N_DEV, axis=1)
    g_meta = lax.dynamic_slice_in_dim(g_meta_full, dev * (D_MODEL // N_DEV), D_MODEL // N_DEV, axis=1)

    names = ["meta_tokens", "norm_w", "w_in", "conv_w", "hg_lb_logits", "hg_norm_w", "gdn_A_log", "gdn_dt_bias",
             "gdn_norm_w", "w_out", "final_norm_w"]
    weights = [meta_tokens, norm_w, w_in, conv_w, hg_lb_logits, hg_norm_w, gdn_A_log, gdn_dt_bias, gdn_norm_w, w_out,
               final_norm_w]
    moms = [m_meta_tokens, m_norm_w, m_w_in, m_conv_w, m_hg_lb_logits, m_hg_norm_w, m_gdn_A_log, m_gdn_dt_bias,
            m_gdn_norm_w, m_w_out, m_final_norm_w]
    vars_ = [v_meta_tokens, v_norm_w, v_w_in, v_conv_w, v_hg_lb_logits, v_hg_norm_w, v_gdn_A_log, v_gdn_dt_bias,
             v_gdn_norm_w, v_w_out, v_final_norm_w]
    grads2d = [g_meta, g_norm_w, g_w_in_t, g_conv, g_lb, g_hg_nw, g_alog, g_dtb, g_gd_nw, g_w_out, g_fw]
    grads, deltas, new_ms, new_vs = [], [], [], []
    for nm, w, g2, m, v in zip(names, weights, grads2d, moms, vars_):
        if nm == "w_in":
            to3, back = (lambda a: jnp.transpose(a, (2, 0, 1))), (lambda a: jnp.transpose(a, (1, 2, 0)))
            g2, d, nm_, nv_ = adamw_w_in(to3(w), g2, to3(m), to3(v))
        else:
            to2d, back = (lambda a, s=g2.shape: a.reshape(s)), (lambda a, s=w.shape: a.reshape(s))
            d, nm_, nv_ = adamw(to2d(w), g2, to2d(m), to2d(v), "adamw_" + nm)
        grads.append(back(g2))
        deltas.append(back(d))
        new_ms.append(back(nm_))
        new_vs.append(back(nv_))
    return (loss, grad_x.reshape(x.shape), *grads, *deltas, *new_ms, *new_vs)
```

```python
import jax
import jax.numpy as jnp
from jax import lax
from jax.experimental import pallas as pl
from jax.experimental.pallas import tpu as pltpu

F32 = jnp.float32
BF16 = jnp.bfloat16
MXU_DTYPE = BF16

D_MODEL = 1024
N_META = 16
CHUNK = 64
SUB = 16
HEADS = 4
DH = 128
WIDTH = HEADS * DH
QKV = 3 * WIDTH
CONV_TAPS = 4
HALO = 8
EPS = 1e-6
IN_COLS = 4 * WIDTH + 4 * WIDTH + 2 * HEADS
AB_PAD = 128
N_DEV = 8
LOCAL_CHUNKS = 4
SCAN_CHUNKS = 2
VMEM_LIMIT = 56 * 1024 * 1024
VMEM_LIMIT_LARGE = 60 * 1024 * 1024

ADAM_LR = 0.001
ADAM_B1 = 0.9
ADAM_B2 = 0.999
ADAM_EPS = 1e-08
ADAM_WD = 0.01
ADAM_STEP = 10

VMEM_SPEC = pl.BlockSpec(memory_space=pltpu.VMEM)
MESH = pl.DeviceIdType.MESH


def _mm_tn(a, b):
    return lax.dot_general(a.astype(MXU_DTYPE), b.astype(MXU_DTYPE), (((0,), (0,)), ((), ())), preferred_element_type=F32)


def _bmm(a, b):
    return lax.dot_general(a.astype(MXU_DTYPE), b.astype(MXU_DTYPE), (((2,), (1,)), ((0,), (0,))), preferred_element_type=F32)


def _bmm_nt(a, b):
    return lax.dot_general(a.astype(MXU_DTYPE), b.astype(MXU_DTYPE), (((2,), (2,)), ((0,), (0,))), preferred_element_type=F32)


def _bmm_tn(a, b):
    return lax.dot_general(a.astype(MXU_DTYPE), b.astype(MXU_DTYPE), (((1,), (1,)), ((0,), (0,))), preferred_element_type=F32)


def _iota2(n, m):
    return lax.broadcasted_iota(jnp.int32, (n, m), 0), lax.broadcasted_iota(jnp.int32, (n, m), 1)


def _silu(x):
    return x * jax.nn.sigmoid(x)


def _gated_norm(o, z, nw):
    return o * lax.rsqrt(jnp.mean(o * o, axis=-1, keepdims=True) + EPS) * nw * _silu(z)


def _heads(a, nb):
    return jnp.stack([a[c * CHUNK:(c + 1) * CHUNK, h * DH:(h + 1) * DH] for c in range(nb) for h in range(HEADS)], axis=0)


def _unheads(a3, nb):
    return jnp.concatenate(
        [jnp.concatenate([a3[c * HEADS + h] for h in range(HEADS)], axis=1) for c in range(nb)], axis=0)


def _split3(x):
    hi = x.astype(BF16)
    r1 = x - hi.astype(F32)
    mid = r1.astype(BF16)
    return hi, mid, (r1 - mid.astype(F32)).astype(BF16)


def _select_mm(pattern, n_out, n_in, transposed, x):
    rows, inner = (n_in, n_out) if transposed else (n_out, n_in)
    r, c = _iota2(rows, 3 * inner)
    c = c - jnp.where(c >= inner, inner, 0) - jnp.where(c >= 2 * inner, inner, 0)
    s = jnp.where(pattern(c, r) if transposed else pattern(r, c), 1.0, 0.0).astype(BF16)
    return jnp.dot(s, jnp.concatenate(_split3(x), axis=0), preferred_element_type=F32)


def _select_rows(pattern, n_out, x):
    @jax.custom_vjp
    def apply(v):
        return _select_mm(pattern, n_out, CHUNK, False, v)

    apply.defvjp(lambda v: (_select_mm(pattern, n_out, CHUNK, False, v), None),
                 lambda _, d: (_select_mm(pattern, n_out, CHUNK, True, d),))
    return apply(x)


def _cumsum_chunks(x, nb):
    return jnp.concatenate([_select_rows(lambda i, j: j <= i, CHUNK, x[c * CHUNK:(c + 1) * CHUNK]) for c in range(nb)], axis=0)


HG_LEVELS = 6


def _hg_sums(i, j):
    lvl, t = i >> HG_LEVELS, i & (CHUNK - 1)
    last = t
    for l in range(1, HG_LEVELS + 1):
        width = HG_LEVELS + 1 - l
        last = jnp.where(lvl == l, ((t >> width) << width) + (CHUNK >> l) - 1, last)
    return j <= last


def hg_local(p, logits):
    nb = p.shape[0] // CHUNK
    l0, l1 = logits[0:1], logits[1:2]
    mx = jnp.maximum(l0, l1)
    e0, e1 = jnp.exp(l0 - mx), jnp.exp(l1 - mx)
    lb = e0 / (e0 + e1)
    q = _silu(p[:, 0:WIDTH])
    f = lb + (1.0 - lb) * jax.nn.sigmoid(p[:, WIDTH:2 * WIDTH])
    k = 1.0 - f
    logf = jnp.log(f)
    sums = [_select_rows(_hg_sums, (HG_LEVELS + 1) * CHUNK, logf[c * CHUNK:(c + 1) * CHUNK]) for c in range(nb)]
    level = lambda l: _heads(jnp.concatenate([s[l * CHUNK:(l + 1) * CHUNK] for s in sums], axis=0), nb)
    q3, k3, v3, g3 = _heads(q, nb), _heads(k, nb), _heads(p[:, 2 * WIDTH:3 * WIDTH], nb), level(0)
    r, c = _iota2(CHUNK, CHUNK)
    row = lax.broadcasted_iota(jnp.int32, (CHUNK, DH), 0)
    a = jnp.where(r == c, _bmm_nt(q3, k3), 0.0)
    for l in range(1, HG_LEVELS + 1):
        sh = HG_LEVELS - l
        qk = jnp.where(((row >> sh) & 1) == 1, q3, k3) * jnp.exp(-jnp.abs(g3 - level(l)))
        pair = ((r >> (sh + 1)) == (c >> (sh + 1))) & (((r >> sh) & 1) == 1) & (((c >> sh) & 1) == 0)
        a = a + jnp.where(pair, _bmm_nt(qk, qk), 0.0)
    o = _bmm(a, v3)
    glast = g3[:, CHUNK - 1:CHUNK, :]
    egs = tuple(jnp.concatenate([jnp.exp(glast[c * HEADS + h]) for h in range(HEADS)], axis=1) for c in range(nb))
    return _unheads(q3 * jnp.exp(g3), nb), _unheads(k3 * jnp.exp(glast - g3), nb), _unheads(o, nb), egs


def hg_scan(q_in, k_out, v, eg, o_intra, z, nw, st):
    o = o_intra + _bmm_nt(q_in, st)
    return _gated_norm(o, z, nw), st * eg + _bmm_tn(v, k_out)


def _tri_y_impl(a):
    r, c = _iota2(CHUNK, CHUNK)
    same16 = (r // SUB) == (c // SUB)
    same32 = (r // (2 * SUB)) == (c // (2 * SUB))
    a0 = jnp.where(same16, a, 0.0)
    y = -a0
    pw = _bmm(a0, a0)
    for _ in range(2):
        y = y + pw + _bmm(y, pw)
        pw = _bmm(pw, pw)
    y = y + pw + _bmm(y, pw)
    for ak in (jnp.where(same32 & jnp.logical_not(same16), a, 0.0), jnp.where(same32, 0.0, a)):
        m = ak + _bmm(y, ak)
        y = y - (m + _bmm(m, y))
    return y


@jax.custom_vjp
def _tri_y(a):
    return _tri_y_impl(a)


def _tri_y_fwd(a):
    y = _tri_y_impl(a)
    return y, y


def _tri_y_bwd(y, dy):
    n = dy + _bmm_tn(y, dy)
    return (-(n + _bmm_nt(n, y)),)


_tri_y.defvjp(_tri_y_fwd, _tri_y_bwd)


def _saved_inverse(y):
    @jax.custom_vjp
    def inverse(a):
        return y

    inverse.defvjp(lambda a: (y, None), lambda _, dy: _tri_y_bwd(y, dy))
    return inverse


def _head_rows(a3, nb):
    return jnp.concatenate([a3[g] for g in range(nb * HEADS)], axis=0)


def _rows_down(x, s):
    rows = x.shape[0]

    @jax.custom_vjp
    def rotate(v):
        return pltpu.roll(v, s, 0)

    rotate.defvjp(lambda v: (pltpu.roll(v, s, 0), None), lambda _, d: (pltpu.roll(d, rows - s, 0),))
    return rotate(x)


def gd_local(xx, ab, cw, alog, dtb, inverse=_tri_y):
    n = ab.shape[0]
    nb = n // CHUNK
    conv = cw[CONV_TAPS - 1:CONV_TAPS] * xx[HALO:HALO + n]
    for j in range(CONV_TAPS - 1):
        conv = conv + cw[j:j + 1] * _rows_down(xx, CONV_TAPS - 1 - j)[HALO:HALO + n]
    act = _silu(conv)
    x = ab + dtb
    g_all = -jnp.exp(alog) * (jnp.maximum(x, 0.0) + jnp.log1p(jnp.exp(-jnp.abs(x))))
    beta_all = jax.nn.sigmoid(ab)
    gam_all = _cumsum_chunks(g_all, nb)
    q3, k3, v3 = _heads(act[:, 0:WIDTH], nb), _heads(act[:, WIDTH:2 * WIDTH], nb), _heads(act[:, 2 * WIDTH:QKV], nb)
    q3 = q3 * lax.rsqrt(jnp.sum(q3 * q3, axis=-1, keepdims=True) + EPS) * (DH ** -0.5)
    k3 = k3 * lax.rsqrt(jnp.sum(k3 * k3, axis=-1, keepdims=True) + EPS)
    pairs = [(c, h) for c in range(nb) for h in range(HEADS)]
    beta = jnp.stack([beta_all[c * CHUNK:(c + 1) * CHUNK, HEADS + h:HEADS + h + 1] for c, h in pairs], axis=0)
    gam = jnp.stack([gam_all[c * CHUNK:(c + 1) * CHUNK, h:h + 1] for c, h in pairs], axis=0)
    gam_t = [gam_all[c * CHUNK:(c + 1) * CHUNK].T for c in range(nb)]
    gam_row = jnp.stack([gam_t[c][h:h + 1, :] for c, h in pairs], axis=0)
    glast = gam[:, CHUNK - 1:CHUNK, :]
    r, c = _iota2(CHUNK, CHUNK)
    dec = jnp.exp(jnp.where(c < r, gam - gam_row, -jnp.inf))
    y = inverse(beta * _bmm_nt(k3, k3) * dec)
    eg = jnp.exp(gam)
    rhs = jnp.concatenate([beta * v3, (beta * eg) * k3], axis=2)
    sol = rhs + _bmm(y, rhs)
    qk = _bmm_nt(q3, k3) * jnp.where(r == c, 1.0, dec)
    eas = tuple(jnp.exp(gam_all[(c + 1) * CHUNK - 1:(c + 1) * CHUNK]) for c in range(nb))
    return (_unheads(sol[:, :, 0:DH], nb), _unheads(sol[:, :, DH:2 * DH], nb), _unheads(q3 * eg, nb),
            _unheads(k3 * jnp.exp(glast - gam), nb), _head_rows(qk, nb), eas), _head_rows(y, nb)


def gd_scan(uu, ww, qe, ke, qk, ea, z, nw, s):
    u = uu - _bmm(ww, s)
    o = _bmm(qe, s) + _bmm(qk, u)
    return _gated_norm(o, z, nw), ea * s + _bmm_tn(ke, u)


def _cparams(*sem):
    return pltpu.CompilerParams(dimension_semantics=sem, vmem_limit_bytes=VMEM_LIMIT)


def _row_tile(n):
    for t in (512, 256, 128, 64):
        if n % t == 0:
            return t
    raise ValueError(f"unsupported token count {n}")


def _w_in_specs():
    once = pl.Buffered(1)
    return [pl.BlockSpec((4 * WIDTH, D_MODEL), lambda *i: (0, 0), pipeline_mode=once),
            pl.BlockSpec((4 * WIDTH, D_MODEL), lambda *i: (1, 0), pipeline_mode=once),
            pl.BlockSpec((AB_PAD, D_MODEL), lambda *i: (8 * WIDTH // AB_PAD, 0), pipeline_mode=once)]


def in_proj(h, h0, norm_w, w_t):
    n = h.shape[0]
    tm = _row_tile(n)
    nt = (((1,), (1,)), ((), ()))

    def body(h_ref, h0_ref, nw_ref, whg_ref, wgd_ref, wab_ref, phg_ref, pgd_ref, pab_ref, phg0_ref, pgd0_ref, pab0_ref, u0_ref):
        def project(x, hg_ref, gd_ref, ab_ref):
            u = (x * lax.rsqrt(jnp.mean(x * x, axis=-1, keepdims=True) + EPS) * nw_ref[...]).astype(MXU_DTYPE)
            hg_ref[...] = lax.dot_general(u, whg_ref[...], nt, preferred_element_type=F32)
            gd_ref[...] = lax.dot_general(u, wgd_ref[...], nt, preferred_element_type=F32)
            ab_ref[...] = lax.dot_general(u, wab_ref[...], nt, preferred_element_type=F32)
            return u

        @pl.when(pl.program_id(0) == 0)
        def _():
            u0_ref[...] = project(h0_ref[...], phg0_ref, pgd0_ref, pab0_ref)

        project(h_ref[...], phg_ref, pgd_ref, pab_ref)

    n0 = h0.shape[0]
    row = lambda w: pl.BlockSpec((tm, w), lambda i: (i, 0))
    lead = lambda w: pl.BlockSpec((n0, w), lambda i: (0, 0))
    widths = [4 * WIDTH, 4 * WIDTH, AB_PAD]
    return pl.pallas_call(
        body, grid=(n // tm,), name="in_proj",
        in_specs=[row(D_MODEL), lead(D_MODEL), pl.BlockSpec(norm_w.shape, lambda i: (0, 0))] + _w_in_specs(),
        out_specs=[row(w) for w in widths] + [lead(w) for w in widths] + [lead(D_MODEL)],
        out_shape=[jax.ShapeDtypeStruct((n, w), F32) for w in widths] + [jax.ShapeDtypeStruct((n0, w), F32) for w in widths]
        + [jax.ShapeDtypeStruct((n0, D_MODEL), MXU_DTYPE)],
        compiler_params=_cparams("arbitrary"),
    )(h, h0, norm_w, w_t, w_t, w_t)


def out_proj_loss(x, tgt, y_hg, y_gd, w_out, fw):
    n = x.shape[0]
    tm = _row_tile(n)
    inv_d = 1.0 / D_MODEL

    def body(x_ref, t_ref, yh_ref, yg_ref, w_ref, fw_ref, dh_ref, dyh_ref, dyg_ref, dw_ref, loss_ref, dfw_ref):
        @pl.when(pl.program_id(0) == 0)
        def _():
            dw_ref[...] = jnp.zeros_like(dw_ref)
            loss_ref[...] = jnp.zeros_like(loss_ref)
            dfw_ref[...] = jnp.zeros_like(dfw_ref)

        yh, yg = yh_ref[...], yg_ref[...]
        wa, wb = w_ref[0:WIDTH, :], w_ref[WIDTH:2 * WIDTH, :]
        h2 = x_ref[...] + jnp.dot(yh, wa, preferred_element_type=F32) + jnp.dot(yg, wb, preferred_element_type=F32)
        r2 = lax.rsqrt(jnp.mean(h2 * h2, axis=-1, keepdims=True) + EPS)
        nrm = h2 * r2
        fwv = fw_ref[...]
        err = nrm * fwv - t_ref[...]
        loss_ref[...] += jnp.full(loss_ref.shape, 0.5 * inv_d * jnp.sum(err * err), F32)
        dout = err * inv_d
        dfw_ref[...] += jnp.sum(dout * nrm, axis=0, keepdims=True)
        dn = dout * fwv
        dh2 = r2 * (dn - nrm * jnp.mean(dn * nrm, axis=-1, keepdims=True))
        dh_ref[...] = dh2
        dhb = dh2.astype(MXU_DTYPE)
        dyh_ref[...] = lax.dot_general(dhb, wa, (((1,), (1,)), ((), ())), preferred_element_type=F32)
        dyg_ref[...] = lax.dot_general(dhb, wb, (((1,), (1,)), ((), ())), preferred_element_type=F32)
        dw_ref[0:WIDTH, :] += lax.dot_general(yh, dhb, (((0,), (0,)), ((), ())), preferred_element_type=F32)
        dw_ref[WIDTH:2 * WIDTH, :] += lax.dot_general(yg, dhb, (((0,), (0,)), ((), ())), preferred_element_type=F32)

    row = lambda w: pl.BlockSpec((tm, w), lambda i: (i, 0))
    full = lambda s: pl.BlockSpec(s, lambda i: (0, 0))
    return pl.pallas_call(
        body, grid=(n // tm,), name="out_proj_loss",
        in_specs=[row(D_MODEL), row(D_MODEL), row(WIDTH), row(WIDTH), full(w_out.shape), full(fw.shape)],
        out_specs=[row(D_MODEL), row(WIDTH), row(WIDTH), full((2 * WIDTH, D_MODEL)), full((8, 128)), full((1, D_MODEL))],
        out_shape=[jax.ShapeDtypeStruct((n, D_MODEL), F32), jax.ShapeDtypeStruct((n, WIDTH), F32),
                   jax.ShapeDtypeStruct((n, WIDTH), F32), jax.ShapeDtypeStruct((2 * WIDTH, D_MODEL), F32),
                   jax.ShapeDtypeStruct((8, 128), F32), jax.ShapeDtypeStruct((1, D_MODEL), F32)],
        compiler_params=_cparams("arbitrary"),
    )(x, tgt, y_hg, y_gd, w_out, fw)


def in_proj_bwd(dphg, dpgd, dpab, w_t, h, dh2, norm_w, h0, u0, dphg0, dpgd0, dpab0):
    n = h.shape[0]
    tm = _row_tile(n)
    steps = n // tm

    def body(dphg_ref, dpgd_ref, dpab_ref, whg_ref, wgd_ref, wab_ref, h_ref, dh2_ref, nw_ref, h0_ref, u0_ref, d0hg_ref,
             d0gd_ref, d0ab_ref, dx_ref, dx0_ref, dnw_ref, ghg_ref, ggd_ref, gab_ref, acc_hg, acc_gd, acc_ab):
        i = pl.program_id(0)
        nwv = nw_ref[...]

        def norm_bwd(dps, x):
            du = jnp.dot(dps[0], whg_ref[...], preferred_element_type=F32)
            du += jnp.dot(dps[1], wgd_ref[...], preferred_element_type=F32)
            du += jnp.dot(dps[2], wab_ref[...], preferred_element_type=F32)
            r = lax.rsqrt(jnp.mean(x * x, axis=-1, keepdims=True) + EPS)
            nrm = x * r
            dn = du * nwv
            return r * (dn - nrm * jnp.mean(dn * nrm, axis=-1, keepdims=True)), nrm, jnp.sum(du * nrm, axis=0, keepdims=True)

        def accumulate(dps, u, first):
            for acc, dp in zip((acc_hg, acc_gd, acc_ab), dps):
                step = min(acc.shape[0], 512)
                for lo in range(0, acc.shape[0], step):
                    part = _mm_tn(dp[:, lo:lo + step], u)
                    acc[lo:lo + step, :] = part if first else acc[lo:lo + step, :] + part

        @pl.when(i == 0)
        def _():
            dps0 = (d0hg_ref[...], d0gd_ref[...], d0ab_ref[...])
            dx0_ref[...], _, dnw_ref[...] = norm_bwd(dps0, h0_ref[...])
            accumulate(dps0, u0_ref[...], True)

        dps = (dphg_ref[...], dpgd_ref[...], dpab_ref[...])
        dx, nrm, dnw = norm_bwd(dps, h_ref[...])
        dx_ref[...] = dh2_ref[...] + dx
        dnw_ref[...] += dnw
        accumulate(dps, (nrm * nwv).astype(MXU_DTYPE), False)

        @pl.when(i == steps - 1)
        def _():
            pltpu.sync_copy(acc_hg, ghg_ref)
            pltpu.sync_copy(acc_gd, ggd_ref)
            pltpu.sync_copy(acc_ab, gab_ref)

    row = lambda w: pl.BlockSpec((tm, w), lambda i: (i, 0))
    full = lambda a: pl.BlockSpec(a.shape, lambda i: (0, 0), pipeline_mode=pl.Buffered(1))
    anywhere = pl.BlockSpec(memory_space=pl.ANY)
    return pl.pallas_call(
        body, grid=(steps,), name="in_proj_bwd",
        in_specs=[row(4 * WIDTH), row(4 * WIDTH), row(AB_PAD)] + _w_in_specs() + [row(D_MODEL), row(D_MODEL), full(norm_w),
                                                                                   full(h0), full(u0), full(dphg0), full(dpgd0),
                                                                                   full(dpab0)],
        out_specs=[row(D_MODEL), pl.BlockSpec(h0.shape, lambda i: (0, 0)), pl.BlockSpec((1, D_MODEL), lambda i: (0, 0)),
                   anywhere, anywhere, anywhere],
        out_shape=[jax.ShapeDtypeStruct((n, D_MODEL), F32), jax.ShapeDtypeStruct(h0.shape, F32),
                   jax.ShapeDtypeStruct((1, D_MODEL), F32), jax.ShapeDtypeStruct((4 * WIDTH, D_MODEL), F32),
                   jax.ShapeDtypeStruct((4 * WIDTH, D_MODEL), F32), jax.ShapeDtypeStruct((AB_PAD, D_MODEL), F32)],
        scratch_shapes=[pltpu.VMEM((4 * WIDTH, D_MODEL), F32), pltpu.VMEM((4 * WIDTH, D_MODEL), F32),
                        pltpu.VMEM((AB_PAD, D_MODEL), F32)],
        compiler_params=pltpu.CompilerParams(dimension_semantics=("arbitrary",), vmem_limit_bytes=VMEM_LIMIT_LARGE),
    )(dphg, dpgd, dpab, w_t, w_t, w_t, h, dh2, norm_w, h0, u0, dphg0, dpgd0, dpab0)


def _sds(shape, dtype=F32):
    return jax.ShapeDtypeStruct(shape, dtype)


def _pairs(b):
    return [(i, h) for i in range(b) for h in range(HEADS)]


def _load_slabs(ref, b, k):
    return jnp.stack([ref[i, k * CHUNK:(k + 1) * CHUNK, h * DH:(h + 1) * DH].astype(F32) for i, h in _pairs(b)], axis=0)


def _lead_slabs(a, b):
    return jnp.stack([a[:, h * DH:(h + 1) * DH].astype(F32) for _, h in _pairs(b)], axis=0)


def _rows(a3, i):
    return jnp.concatenate([a3[i * HEADS + h] for h in range(HEADS)], axis=1)


def _store_slabs(ref, a3, b, k):
    for i in range(b):
        ref[i, k * CHUNK:(k + 1) * CHUNK, :] = _rows(a3, i).astype(ref.dtype)


def _sum_rows(a3, b):
    out = _rows(a3, 0)
    for i in range(1, b):
        out = out + _rows(a3, i)
    return out


def _save_states(ref, s, b, k):
    for i in range(b):
        ref[i, k] = jnp.concatenate([s[i * HEADS + h] for h in range(HEADS)], axis=0)


def _load_states(ref, b, k):
    return jnp.stack([ref[i, k, h * DH:(h + 1) * DH, :] for i, h in _pairs(b)], axis=0)


def hg_local_fwd(p, logits):
    b, seq, _ = p.shape
    rows = LOCAL_CHUNKS * CHUNK
    nreal = seq // CHUNK

    def body(p_ref, lg_ref, q_ref, k_ref, o_ref, eg_ref):
        q_in, k_out, o_intra, egs = hg_local(p_ref[...], lg_ref[...])
        q_ref[...], k_ref[...], o_ref[...] = q_in.astype(MXU_DTYPE), k_out.astype(MXU_DTYPE), o_intra
        for c in range(LOCAL_CHUNKS):
            eg_ref[c] = egs[c]

    slab = pl.BlockSpec((None, rows, WIDTH), lambda s, g: (s, g, 0))
    return pl.pallas_call(
        body, grid=(b, seq // rows), name="hgrn2_local",
        in_specs=[pl.BlockSpec((None, rows, 4 * WIDTH), lambda s, g: (s, g, 0)), pl.BlockSpec(logits.shape, lambda s, g: (0, 0))],
        out_specs=[slab, slab, slab, pl.BlockSpec((None, LOCAL_CHUNKS, 1, WIDTH), lambda s, g: (s, g, 0, 0))],
        out_shape=[_sds((b, seq, WIDTH), MXU_DTYPE)] * 2 + [_sds((b, seq, WIDTH)), _sds((b, nreal, 1, WIDTH))],
        compiler_params=_cparams("arbitrary", "arbitrary"),
    )(p, logits)


def hg_local_lead(p0, logits):
    def body(p_ref, lg_ref, q_ref, k_ref, o_ref, eg_ref):
        q_in, k_out, o_ref[...], (eg_ref[...],) = hg_local(p_ref[...], lg_ref[...])
        q_ref[...], k_ref[...] = q_in.astype(MXU_DTYPE), k_out.astype(MXU_DTYPE)

    return pl.pallas_call(
        body, name="hgrn2_local_lead", in_specs=[VMEM_SPEC] * 2, out_specs=[VMEM_SPEC] * 4,
        out_shape=[_sds((CHUNK, WIDTH), MXU_DTYPE)] * 2 + [_sds((CHUNK, WIDTH)), _sds((1, WIDTH))],
        compiler_params=pltpu.CompilerParams(vmem_limit_bytes=VMEM_LIMIT),
    )(p0, logits)


def _hg_scan_args(b, k, q_ref, k_ref, o_ref, v_ref, z_ref, eg_ref):
    eg = jnp.stack([eg_ref[i, k, :, h * DH:(h + 1) * DH] for i, h in _pairs(b)], axis=0)
    return (_load_slabs(q_ref, b, k), _load_slabs(k_ref, b, k), _load_slabs(v_ref, b, k), eg, _load_slabs(o_ref, b, k),
            _load_slabs(z_ref, b, k))


def _hg_lead_args(b, q0_ref, k0_ref, o0_ref, p0_ref, eg0_ref):
    eg = jnp.stack([eg0_ref[:, h * DH:(h + 1) * DH] for _, h in _pairs(b)], axis=0)
    return (_lead_slabs(q0_ref[...], b), _lead_slabs(k0_ref[...], b), _lead_slabs(p0_ref[:, 2 * WIDTH:3 * WIDTH], b), eg,
            _lead_slabs(o0_ref[...], b), _lead_slabs(p0_ref[:, 3 * WIDTH:4 * WIDTH], b))


def _scan_specs(b, ng, reverse):
    group = (lambda i: ng - 1 - i) if reverse else (lambda i: i)
    slab = lambda lane_block: pl.BlockSpec((b, SCAN_CHUNKS * CHUNK, WIDTH), lambda i: (0, group(i), lane_block))
    per_chunk = lambda *tail: pl.BlockSpec((b, SCAN_CHUNKS) + tail, lambda i: (0, group(i)) + (0,) * len(tail))
    const = lambda a: pl.BlockSpec(a.shape, lambda i: (0,) * a.ndim)
    return slab, per_chunk, const


def run_scans(parts, nc, name):
    n_in = [len(p["args"]) for p in parts]
    n_out = [len(p["out_shape"]) for p in parts]
    n_scr = [len(p["scratch_shapes"]) for p in parts]

    def body(*refs):
        ins, outs, scr = refs[:sum(n_in)], refs[sum(n_in):sum(n_in) + sum(n_out)], refs[sum(n_in) + sum(n_out):]
        for i, part in enumerate(parts):
            part["body"](*ins[sum(n_in[:i]):sum(n_in[:i + 1])], *outs[sum(n_out[:i]):sum(n_out[:i + 1])],
                         *scr[sum(n_scr[:i]):sum(n_scr[:i + 1])])

    flat = lambda key: [v for p in parts for v in p[key]]
    out = pl.pallas_call(body, grid=(nc,), name=name, in_specs=flat("in_specs"), out_specs=flat("out_specs"),
                         out_shape=flat("out_shape"), scratch_shapes=flat("scratch_shapes"),
                         compiler_params=_cparams("arbitrary"))(*flat("args"))
    return [out[sum(n_out[:i]):sum(n_out[:i + 1])] for i in range(len(parts))]


def hg_scan_fwd(p, p0, local, lead, nw):
    b, seq, _ = p.shape
    q_in, k_out, o_intra, eg = local
    slab, per_chunk, const = _scan_specs(b, seq // (SCAN_CHUNKS * CHUNK), False)

    def body(q_ref, k_ref, o_ref, v_ref, z_ref, eg_ref, q0_ref, k0_ref, o0_ref, p0_ref, eg0_ref, nw_ref, y_ref, ss_ref, st):
        @pl.when(pl.program_id(0) == 0)
        def _():
            st[...] = hg_scan(*_hg_lead_args(b, q0_ref, k0_ref, o0_ref, p0_ref, eg0_ref), nw_ref[...], jnp.zeros(st.shape, F32))[1]

        s = st[...]
        for k in range(SCAN_CHUNKS):
            _save_states(ss_ref, s, b, k)
            y, s = hg_scan(*_hg_scan_args(b, k, q_ref, k_ref, o_ref, v_ref, z_ref, eg_ref), nw_ref[...], s)
            _store_slabs(y_ref, y, b, k)
        st[...] = s

    return dict(
        body=body, args=(q_in, k_out, o_intra, p, p, eg, lead[0], lead[1], lead[2], p0, lead[3], nw),
        in_specs=[slab(0), slab(0), slab(0), slab(2), slab(3), per_chunk(1, WIDTH)] + [const(a) for a in lead[0:3]]
        + [const(p0), const(lead[3]), const(nw)],
        out_specs=[slab(0), per_chunk(WIDTH, DH)],
        out_shape=[_sds((b, seq, WIDTH), MXU_DTYPE), _sds((b, seq // CHUNK, WIDTH, DH))],
        scratch_shapes=[pltpu.VMEM((b * HEADS, DH, DH), F32)])


def hg_scan_bwd(p, p0, local, lead, nw, ssave, dy):
    b, seq, _ = p.shape
    ng = seq // (SCAN_CHUNKS * CHUNK)
    q_in, k_out, o_intra, eg = local
    slab, per_chunk, const = _scan_specs(b, ng, True)

    def body(q_ref, k_ref, o_ref, v_ref, z_ref, eg_ref, q0_ref, k0_ref, o0_ref, p0_ref, eg0_ref, nw_ref, ss_ref, dy_ref,
             dq_ref, dk_ref, do_ref, dv_ref, dz_ref, deg_ref, dq0_ref, dk0_ref, do0_ref, dv0_ref, dz0_ref, deg0_ref, dnw_ref,
             dst):
        i = pl.program_id(0)

        @pl.when(i == 0)
        def _():
            dst[...] = jnp.zeros_like(dst)
            dnw_ref[...] = jnp.zeros_like(dnw_ref)

        ds = dst[...]
        for k in reversed(range(SCAN_CHUNKS)):
            args = _hg_scan_args(b, k, q_ref, k_ref, o_ref, v_ref, z_ref, eg_ref)
            _, vjp = jax.vjp(hg_scan, *args, nw_ref[...], _load_states(ss_ref, b, k))
            dq, dk, dv, deg, do, dz, dnw, ds = vjp((_load_slabs(dy_ref, b, k), ds))
            dnw_ref[...] += dnw
            for ref, val in ((dq_ref, dq), (dk_ref, dk), (do_ref, do), (dv_ref, dv), (dz_ref, dz)):
                _store_slabs(ref, val, b, k)
            for j in range(b):
                deg_ref[j, k] = _rows(deg, j)
        dst[...] = ds

        @pl.when(i == ng - 1)
        def _():
            args = _hg_lead_args(b, q0_ref, k0_ref, o0_ref, p0_ref, eg0_ref)
            _, vjp = jax.vjp(hg_scan, *args, nw_ref[...], jnp.zeros(dst.shape, F32))
            dq, dk, dv, deg, do, dz, dnw, _ = vjp((jnp.zeros((b * HEADS, CHUNK, DH), F32), ds))
            dnw_ref[...] += dnw
            for ref, val in ((dq0_ref, dq), (dk0_ref, dk), (do0_ref, do), (dv0_ref, dv), (dz0_ref, dz), (deg0_ref, deg)):
                ref[...] = _sum_rows(val, b)

    lead_out = [const(a) for a in lead[0:3]] + [const(lead[0]), const(lead[0]), const(lead[3])]
    return dict(
        body=body, args=(q_in, k_out, o_intra, p, p, eg, lead[0], lead[1], lead[2], p0, lead[3], nw, ssave, dy),
        in_specs=[slab(0), slab(0), slab(0), slab(2), slab(3), per_chunk(1, WIDTH)] + [const(a) for a in lead[0:3]]
        + [const(p0), const(lead[3]), const(nw), per_chunk(WIDTH, DH), slab(0)],
        out_specs=[slab(0)] * 5 + [per_chunk(1, WIDTH)] + lead_out + [const(nw)],
        out_shape=[_sds((b, seq, WIDTH))] * 5 + [_sds(eg.shape)] + [_sds((CHUNK, WIDTH))] * 5 + [_sds((1, WIDTH)), _sds(nw.shape)],
        scratch_shapes=[pltpu.VMEM((b * HEADS, DH, DH), F32)])


def _hg_local_vjp(p, logits, dq, dk, do, degs, dv, dz):
    _, vjp = jax.vjp(hg_local, p, logits)
    dp, dlg = vjp((dq, dk, do, degs))
    return dp + jnp.concatenate([jnp.zeros((p.shape[0], 2 * WIDTH), F32), dv, dz], axis=1), dlg


def hg_local_bwd(p, logits, dq, dk, do, dv, dz, deg):
    b, seq, _ = p.shape
    rows = LOCAL_CHUNKS * CHUNK

    def body(p_ref, lg_ref, dq_ref, dk_ref, do_ref, dv_ref, dz_ref, deg_ref, dp_ref, dlg_ref):
        @pl.when((pl.program_id(0) == 0) & (pl.program_id(1) == 0))
        def _():
            dlg_ref[...] = jnp.zeros_like(dlg_ref)

        degs = tuple(deg_ref[c] for c in range(LOCAL_CHUNKS))
        dp, dlg = _hg_local_vjp(p_ref[...], lg_ref[...], dq_ref[...], dk_ref[...], do_ref[...], degs, dv_ref[...], dz_ref[...])
        dp_ref[...] = dp.astype(MXU_DTYPE)
        dlg_ref[...] += dlg

    slab = pl.BlockSpec((None, rows, WIDTH), lambda s, g: (s, g, 0))
    wide = pl.BlockSpec((None, rows, 4 * WIDTH), lambda s, g: (s, g, 0))
    lg = pl.BlockSpec(logits.shape, lambda s, g: (0, 0))
    return pl.pallas_call(
        body, grid=(b, seq // rows), name="hgrn2_local_bwd",
        in_specs=[wide, lg, slab, slab, slab, slab, slab, pl.BlockSpec((None, LOCAL_CHUNKS, 1, WIDTH), lambda s, g: (s, g, 0, 0))],
        out_specs=[wide, lg], out_shape=[_sds(p.shape, MXU_DTYPE), _sds(logits.shape)],
        compiler_params=_cparams("arbitrary", "arbitrary"),
    )(p, logits, dq, dk, do, dv, dz, deg)


def hg_local_bwd_lead(p0, logits, dq, dk, do, dv, dz, deg):
    def body(p_ref, lg_ref, dq_ref, dk_ref, do_ref, dv_ref, dz_ref, deg_ref, dp_ref, dlg_ref):
        dp, dlg_ref[...] = _hg_local_vjp(p_ref[...], lg_ref[...], dq_ref[...], dk_ref[...], do_ref[...],
                                         (deg_ref[...],), dv_ref[...], dz_ref[...])
        dp_ref[...] = dp.astype(MXU_DTYPE)

    return pl.pallas_call(
        body, name="hgrn2_local_bwd_lead", in_specs=[VMEM_SPEC] * 8, out_specs=[VMEM_SPEC] * 2,
        out_shape=[_sds(p0.shape, MXU_DTYPE), _sds(logits.shape)], compiler_params=pltpu.CompilerParams(vmem_limit_bytes=VMEM_LIMIT),
    )(p0, logits, dq, dk, do, dv, dz, deg)


def _halo_block(g):
    return jnp.maximum((LOCAL_CHUNKS * CHUNK // HALO) * g - 1, 0)


def _gd_window(g, p_ref, halo_ref, p0_ref):
    halo = jnp.where(g == 0, p0_ref[CHUNK - HALO:CHUNK, 0:QKV], halo_ref[...])
    return jnp.concatenate([halo, p_ref[:, 0:QKV]], axis=0)


def gd_local_fwd(p, p0, ab, cw, alog, dtb):
    b, seq, _ = p.shape
    rows = LOCAL_CHUNKS * CHUNK
    nreal = seq // CHUNK

    def body(p_ref, halo_ref, p0_ref, ab_ref, cw_ref, al_ref, dt_ref, u_ref, w_ref, qe_ref, ke_ref, qk_ref, ea_ref, inv_ref):
        (uu, ww, qe, ke, qk, eas), inv = gd_local(_gd_window(pl.program_id(1), p_ref, halo_ref, p0_ref), ab_ref[...],
                                                  cw_ref[...], al_ref[...], dt_ref[...], inverse=_tri_y_impl)
        u_ref[...], w_ref[...], qe_ref[...], ke_ref[...] = uu, ww.astype(MXU_DTYPE), qe.astype(MXU_DTYPE), ke.astype(MXU_DTYPE)
        for c in range(LOCAL_CHUNKS):
            qk_ref[c] = qk[c * HEADS * CHUNK:(c + 1) * HEADS * CHUNK]
            inv_ref[c] = inv[c * HEADS * CHUNK:(c + 1) * HEADS * CHUNK]
            ea_ref[c] = eas[c]

    const = lambda a: pl.BlockSpec(a.shape, lambda s, g: (0, 0))
    slab = pl.BlockSpec((None, rows, WIDTH), lambda s, g: (s, g, 0))
    mats = pl.BlockSpec((None, LOCAL_CHUNKS, HEADS * CHUNK, CHUNK), lambda s, g: (s, g, 0, 0))
    out = pl.pallas_call(
        body, grid=(b, seq // rows), name="gdn_local",
        in_specs=[pl.BlockSpec((None, rows, 4 * WIDTH), lambda s, g: (s, g, 0)),
                  pl.BlockSpec((None, HALO, QKV), lambda s, g: (s, _halo_block(g), 0)), const(p0),
                  pl.BlockSpec((None, rows, AB_PAD), lambda s, g: (s, g, 0)), const(cw), const(alog), const(dtb)],
        out_specs=[slab] * 4 + [mats, pl.BlockSpec((None, LOCAL_CHUNKS, 1, AB_PAD), lambda s, g: (s, g, 0, 0)), mats],
        out_shape=[_sds((b, seq, WIDTH))] + [_sds((b, seq, WIDTH), MXU_DTYPE)] * 3
        + [_sds((b, nreal, HEADS * CHUNK, CHUNK)), _sds((b, nreal, 1, AB_PAD)), _sds((b, nreal, HEADS * CHUNK, CHUNK))],
        compiler_params=_cparams("arbitrary", "arbitrary"),
    )(p, p, p0, ab, cw, alog, dtb)
    return out[0:6], out[6]


def _lead_window(p0_ref):
    return jnp.concatenate([jnp.zeros((HALO, QKV), F32), p0_ref[:, 0:QKV]], axis=0)


def gd_local_lead(p0, ab0, cw, alog, dtb):
    def body(p0_ref, ab_ref, cw_ref, al_ref, dt_ref, u_ref, w_ref, qe_ref, ke_ref, qk_ref, ea_ref, inv_ref):
        (u_ref[...], ww, qe, ke, qk_ref[...], (ea_ref[...],)), inv_ref[...] = gd_local(
            _lead_window(p0_ref), ab_ref[...], cw_ref[...], al_ref[...], dt_ref[...], inverse=_tri_y_impl)
        w_ref[...], qe_ref[...], ke_ref[...] = ww.astype(MXU_DTYPE), qe.astype(MXU_DTYPE), ke.astype(MXU_DTYPE)

    out = pl.pallas_call(
        body, name="gdn_local_lead", in_specs=[VMEM_SPEC] * 5, out_specs=[VMEM_SPEC] * 7,
        out_shape=[_sds((CHUNK, WIDTH))] + [_sds((CHUNK, WIDTH), MXU_DTYPE)] * 3
        + [_sds((HEADS * CHUNK, CHUNK)), _sds((1, AB_PAD)), _sds((HEADS * CHUNK, CHUNK))],
        compiler_params=pltpu.CompilerParams(vmem_limit_bytes=VMEM_LIMIT),
    )(p0, ab0, cw, alog, dtb)
    return out[0:6], out[6]


def _gd_scan_args(b, k, u_ref, w_ref, qe_ref, ke_ref, qk_ref, ea_ref, z_ref):
    qk = jnp.stack([qk_ref[i, k, h * CHUNK:(h + 1) * CHUNK, :] for i, h in _pairs(b)], axis=0)
    ea = jnp.stack([ea_ref[i, k, :, h:h + 1] for i, h in _pairs(b)], axis=0)
    return (_load_slabs(u_ref, b, k), _load_slabs(w_ref, b, k), _load_slabs(qe_ref, b, k), _load_slabs(ke_ref, b, k), qk, ea,
            _load_slabs(z_ref, b, k))


def _gd_lead_args(b, u0_ref, w0_ref, qe0_ref, ke0_ref, qk0_ref, ea0_ref, p0_ref):
    qk = jnp.stack([qk0_ref[h * CHUNK:(h + 1) * CHUNK, :] for _, h in _pairs(b)], axis=0)
    ea = jnp.stack([ea0_ref[:, h:h + 1] for _, h in _pairs(b)], axis=0)
    return (_lead_slabs(u0_ref[...], b), _lead_slabs(w0_ref[...], b), _lead_slabs(qe0_ref[...], b), _lead_slabs(ke0_ref[...], b),
            qk, ea, _lead_slabs(p0_ref[:, QKV:QKV + WIDTH], b))


def gd_scan_fwd(p, p0, local, lead, nw):
    b, seq, _ = p.shape
    slab, per_chunk, const = _scan_specs(b, seq // (SCAN_CHUNKS * CHUNK), False)

    def body(u_ref, w_ref, qe_ref, ke_ref, qk_ref, ea_ref, z_ref, u0_ref, w0_ref, qe0_ref, ke0_ref, qk0_ref, ea0_ref, p0_ref,
             nw_ref, y_ref, ss_ref, st):
        @pl.when(pl.program_id(0) == 0)
        def _():
            lead_args = _gd_lead_args(b, u0_ref, w0_ref, qe0_ref, ke0_ref, qk0_ref, ea0_ref, p0_ref)
            st[...] = gd_scan(*lead_args, nw_ref[...], jnp.zeros(st.shape, F32))[1]

        s = st[...]
        for k in range(SCAN_CHUNKS):
            _save_states(ss_ref, s, b, k)
            y, s = gd_scan(*_gd_scan_args(b, k, u_ref, w_ref, qe_ref, ke_ref, qk_ref, ea_ref, z_ref), nw_ref[...], s)
            _store_slabs(y_ref, y, b, k)
        st[...] = s

    return dict(
        body=body, args=(*local, p, *lead, p0, nw),
        in_specs=[slab(0)] * 4 + [per_chunk(HEADS * CHUNK, CHUNK), per_chunk(1, AB_PAD), slab(3)] + [const(a) for a in lead]
        + [const(p0), const(nw)],
        out_specs=[slab(0), per_chunk(WIDTH, DH)],
        out_shape=[_sds((b, seq, WIDTH), MXU_DTYPE), _sds((b, seq // CHUNK, WIDTH, DH))],
        scratch_shapes=[pltpu.VMEM((b * HEADS, DH, DH), F32)])


def gd_scan_bwd(p, p0, local, lead, nw, ssave, dy):
    b, seq, _ = p.shape
    ng = seq // (SCAN_CHUNKS * CHUNK)
    slab, per_chunk, const = _scan_specs(b, ng, True)

    def body(u_ref, w_ref, qe_ref, ke_ref, qk_ref, ea_ref, z_ref, u0_ref, w0_ref, qe0_ref, ke0_ref, qk0_ref, ea0_ref, p0_ref,
             nw_ref, ss_ref, dy_ref, du_ref, dw_ref, dqe_ref, dke_ref, dqk_ref, dea_ref, dz_ref, du0_ref, dw0_ref, dqe0_ref,
             dke0_ref, dqk0_ref, dea0_ref, dz0_ref, dnw_ref, dst):
        i = pl.program_id(0)
        lane = lax.broadcasted_iota(jnp.int32, (1, AB_PAD), 1)

        def gate_rows(dea, j):
            return sum(jnp.where(lane == h, dea[j * HEADS + h], 0.0) for h in range(HEADS))

        def matrix_rows(dqk, j):
            return jnp.concatenate([dqk[j * HEADS + h] for h in range(HEADS)], axis=0)

        @pl.when(i == 0)
        def _():
            dst[...] = jnp.zeros_like(dst)
            dnw_ref[...] = jnp.zeros_like(dnw_ref)

        ds = dst[...]
        for k in reversed(range(SCAN_CHUNKS)):
            args = _gd_scan_args(b, k, u_ref, w_ref, qe_ref, ke_ref, qk_ref, ea_ref, z_ref)
            _, vjp = jax.vjp(gd_scan, *args, nw_ref[...], _load_states(ss_ref, b, k))
            du, dw, dqe, dke, dqk, dea, dz, dnw, ds = vjp((_load_slabs(dy_ref, b, k), ds))
            dnw_ref[...] += dnw
            for ref, val in ((du_ref, du), (dw_ref, dw), (dqe_ref, dqe), (dke_ref, dke), (dz_ref, dz)):
                _store_slabs(ref, val, b, k)
            for j in range(b):
                dqk_ref[j, k] = matrix_rows(dqk, j)
                dea_ref[j, k] = gate_rows(dea, j)
        dst[...] = ds

        @pl.when(i == ng - 1)
        def _():
            args = _gd_lead_args(b, u0_ref, w0_ref, qe0_ref, ke0_ref, qk0_ref, ea0_ref, p0_ref)
            _, vjp = jax.vjp(gd_scan, *args, nw_ref[...], jnp.zeros(dst.shape, F32))
            du, dw, dqe, dke, dqk, dea, dz, dnw, _ = vjp((jnp.zeros((b * HEADS, CHUNK, DH), F32), ds))
            dnw_ref[...] += dnw
            for ref, val in ((du0_ref, du), (dw0_ref, dw), (dqe0_ref, dqe), (dke0_ref, dke), (dz0_ref, dz)):
                ref[...] = _sum_rows(val, b)
            dqk0_ref[...] = sum((matrix_rows(dqk, j) for j in range(1, b)), matrix_rows(dqk, 0))
            dea0_ref[...] = sum((gate_rows(dea, j) for j in range(1, b)), gate_rows(dea, 0))

    uu, ww, qe, ke, qk, ea = local
    return dict(
        body=body, args=(*local, p, *lead, p0, nw, ssave, dy),
        in_specs=[slab(0)] * 4 + [per_chunk(HEADS * CHUNK, CHUNK), per_chunk(1, AB_PAD), slab(3)] + [const(a) for a in lead]
        + [const(p0), const(nw), per_chunk(WIDTH, DH), slab(0)],
        out_specs=[slab(0)] * 4 + [per_chunk(HEADS * CHUNK, CHUNK), per_chunk(1, AB_PAD), slab(0)] + [const(a) for a in lead]
        + [const(lead[0]), const(nw)],
        out_shape=[_sds((b, seq, WIDTH))] * 4 + [_sds(qk.shape), _sds(ea.shape), _sds((b, seq, WIDTH))]
        + [_sds(a.shape) for a in lead] + [_sds(lead[0].shape), _sds(nw.shape)],
        scratch_shapes=[pltpu.VMEM((b * HEADS, DH, DH), F32)])


def _gd_local_vjp(inv_rows, xx, ab, cw, alog, dtb):
    nb = ab.shape[0] // CHUNK
    inv = jnp.stack([inv_rows[g * CHUNK:(g + 1) * CHUNK] for g in range(nb * HEADS)], axis=0)
    _, vjp, _ = jax.vjp(lambda *a: gd_local(*a, inverse=_saved_inverse(inv)), xx, ab, cw, alog, dtb, has_aux=True)
    return vjp


def gd_local_bwd(p, p0, ab, cw, alog, dtb, inv, cot, dz):
    b, seq, _ = p.shape
    rows = LOCAL_CHUNKS * CHUNK
    ng = seq // rows
    du, dw, dqe, dke, dqk, dea = cot

    def body(p_ref, halo_ref, p0_ref, ab_ref, cw_ref, al_ref, dt_ref, inv_ref, du_ref, dw_ref, dqe_ref, dke_ref, dqk_ref,
             dea_ref, dz_ref, dp_ref, dab_ref, dhalo0_ref, dcw_ref, dal_ref, ddt_ref, dhalo):
        i = pl.program_id(1)
        g = ng - 1 - i

        @pl.when(i == 0)
        def _():
            dhalo[...] = jnp.zeros_like(dhalo)

        @pl.when((pl.program_id(0) == 0) & (i == 0))
        def _():
            dcw_ref[...] = jnp.zeros_like(dcw_ref)
            dal_ref[...] = jnp.zeros_like(dal_ref)
            ddt_ref[...] = jnp.zeros_like(ddt_ref)

        inv_rows = jnp.concatenate([inv_ref[c] for c in range(LOCAL_CHUNKS)], axis=0)
        vjp = _gd_local_vjp(inv_rows, _gd_window(g, p_ref, halo_ref, p0_ref), ab_ref[...], cw_ref[...], al_ref[...], dt_ref[...])
        dqk_all = jnp.concatenate([dqk_ref[c] for c in range(LOCAL_CHUNKS)], axis=0)
        deas = tuple(dea_ref[c] for c in range(LOCAL_CHUNKS))
        dxx, dab, dcw, dal, ddt = vjp((du_ref[...], dw_ref[...], dqe_ref[...], dke_ref[...], dqk_all, deas))
        dqkv = dxx[HALO:HALO + rows] + jnp.concatenate([jnp.zeros((rows - HALO, QKV), F32), dhalo[...]], axis=0)
        dhalo[...] = dxx[0:HALO]
        dhalo0_ref[...] = dxx[0:HALO]
        dp_ref[...] = jnp.concatenate([dqkv, dz_ref[...]], axis=1).astype(MXU_DTYPE)
        dab_ref[...] = dab.astype(MXU_DTYPE)
        dcw_ref[...] += dcw
        dal_ref[...] += dal
        ddt_ref[...] += ddt

    rg = lambda i: ng - 1 - i
    const = lambda a: pl.BlockSpec(a.shape, lambda s, i: (0, 0))
    slab = pl.BlockSpec((None, rows, WIDTH), lambda s, i: (s, rg(i), 0))
    wide = pl.BlockSpec((None, rows, 4 * WIDTH), lambda s, i: (s, rg(i), 0))
    gates = pl.BlockSpec((None, rows, AB_PAD), lambda s, i: (s, rg(i), 0))
    mats = pl.BlockSpec((None, LOCAL_CHUNKS, HEADS * CHUNK, CHUNK), lambda s, i: (s, rg(i), 0, 0))
    return pl.pallas_call(
        body, grid=(b, ng), name="gdn_local_bwd",
        in_specs=[wide, pl.BlockSpec((None, HALO, QKV), lambda s, i: (s, _halo_block(rg(i)), 0)), const(p0), gates, const(cw),
                  const(alog), const(dtb), mats, slab, slab, slab, slab, mats,
                  pl.BlockSpec((None, LOCAL_CHUNKS, 1, AB_PAD), lambda s, i: (s, rg(i), 0, 0)), slab],
        out_specs=[wide, gates, pl.BlockSpec((None, HALO, QKV), lambda s, i: (s, 0, 0)), const(cw), const(alog), const(dtb)],
        out_shape=[_sds(p.shape, MXU_DTYPE), _sds(ab.shape, MXU_DTYPE), _sds((b, HALO, QKV)), _sds(cw.shape), _sds(alog.shape),
                   _sds(dtb.shape)],
        scratch_shapes=[pltpu.VMEM((HALO, QKV), F32)],
        compiler_params=_cparams("arbitrary", "arbitrary"),
    )(p, p, p0, ab, cw, alog, dtb, inv, du, dw, dqe, dke, dqk, dea, dz)


def gd_local_bwd_lead(p0, ab0, cw, alog, dtb, inv, cot, dz, dtail):
    def body(p0_ref, ab_ref, cw_ref, al_ref, dt_ref, inv_ref, du_ref, dw_ref, dqe_ref, dke_ref, dqk_ref, dea_ref, dz_ref,
             dtail_ref, dp_ref, dab_ref, dcw_ref, dal_ref, ddt_ref):
        vjp = _gd_local_vjp(inv_ref[...], _lead_window(p0_ref), ab_ref[...], cw_ref[...], al_ref[...], dt_ref[...])
        dxx, dab, dcw, dal, ddt = vjp((du_ref[...], dw_ref[...], dqe_ref[...], dke_ref[...], dqk_ref[...], (dea_ref[...],)))
        dqkv = dxx[HALO:HALO + CHUNK] + jnp.concatenate([jnp.zeros((CHUNK - HALO, QKV), F32), dtail_ref[...]], axis=0)
        dp_ref[...] = jnp.concatenate([dqkv, dz_ref[...]], axis=1).astype(MXU_DTYPE)
        dab_ref[...], dcw_ref[...], dal_ref[...], ddt_ref[...] = dab.astype(MXU_DTYPE), dcw, dal, ddt

    return pl.pallas_call(
        body, name="gdn_local_bwd_lead", in_specs=[VMEM_SPEC] * 14, out_specs=[VMEM_SPEC] * 5,
        out_shape=[_sds(p0.shape, MXU_DTYPE), _sds(ab0.shape, MXU_DTYPE), _sds(cw.shape), _sds(alog.shape), _sds(dtb.shape)],
        compiler_params=pltpu.CompilerParams(vmem_limit_bytes=VMEM_LIMIT),
    )(p0, ab0, cw, alog, dtb, inv, *cot, dz, dtail)


def _position():
    return lax.axis_index("x"), lax.axis_index("y"), lax.axis_index("c")


EXCHANGE_COPIES = 10


def _exchange_blocks(bufs, send_sems, recv_sems):
    x, y, c = _position()
    here, x_nbr, y_nbr, diag = (x, y), (1 - x, y), (x, 1 - y), (1 - x, 1 - y)
    sibling = (x, y, 1 - c)
    me = (x, y, c)
    n = range(len(bufs))

    def rows(a, chip, core, half=None):
        block = bufs[a].at[4 * chip[0] + 2 * chip[1] + core]
        if half is None:
            return block
        total = bufs[a].shape[1]
        tile = 8 * (4 // jnp.dtype(bufs[a].dtype).itemsize)
        split = total // 2 // tile * tile
        return block.at[pl.ds(0, split)] if half == 0 else block.at[pl.ds(split, total - split)]

    def copy(a, k, region, to):
        return pltpu.make_async_remote_copy(src_ref=region, dst_ref=region, send_sem=send_sems.at[a * EXCHANGE_COPIES + k],
                                            recv_sem=recv_sems.at[a * EXCHANGE_COPIES + k], device_id=to, device_id_type=MESH)

    sent = [copy(a, 0, rows(a, here, c), sibling) for a in n]
    sent += [cp for a in n for cp in (copy(a, 1, rows(a, here, c, 0), (*x_nbr, c)), copy(a, 4, rows(a, here, c, 1), (*y_nbr, c)))]
    sent += [cp for a in n for cp in (copy(a, 2, rows(a, here, c, 1), (*x_nbr, c)), copy(a, 3, rows(a, here, c, 0), (*y_nbr, c)))]
    for cp in sent:
        cp.start()

    def after(arrivals, a, k, region, to):
        for cp in arrivals:
            cp.wait_recv()
        sent.append(copy(a, k, region, to))
        sent[-1].start()

    for a in n:
        after([copy(a, 1, rows(a, x_nbr, c, 0), me)], a, 5, rows(a, x_nbr, c, 0), (*y_nbr, c))
        after([copy(a, 4, rows(a, y_nbr, c, 1), me)], a, 6, rows(a, y_nbr, c, 1), (*x_nbr, c))
    for a in n:
        after([copy(a, 2, rows(a, x_nbr, c, 1), me)], a, 7, rows(a, x_nbr, c), sibling)
        after([copy(a, 3, rows(a, y_nbr, c, 0), me)], a, 8, rows(a, y_nbr, c), sibling)
    for a in n:
        after([copy(a, 5, rows(a, diag, c, 0), me), copy(a, 6, rows(a, diag, c, 1), me)], a, 9, rows(a, diag, c), sibling)
    for a in n:
        copy(a, 0, rows(a, here, 1 - c), me).wait_recv()
        for k, chip in ((7, x_nbr), (8, y_nbr), (9, diag)):
            copy(a, k, rows(a, chip, 1 - c), me).wait_recv()
    for cp in sent:
        cp.wait_send()


def _exchange_sems(n_bufs):
    return [pltpu.SemaphoreType.DMA((n_bufs * EXCHANGE_COPIES,)), pltpu.SemaphoreType.DMA((n_bufs * EXCHANGE_COPIES,))]


def gather_weights(w_in_t, w_out, small, pad_rows):
    rows, _, cols = w_in_t.shape
    buf_rows = -(-rows // 16) * 16

    def body(wi_ref, wo_ref, sm_ref, wi_out, wo_out, sm_out, wi_buf, send_sems, recv_sems):
        x, y, c = _position()
        me = 4 * x + 2 * y + c
        wi_buf[me, pl.ds(0, rows), :] = wi_ref[:, 0, :].astype(MXU_DTYPE)
        wi_buf[me, pl.ds(rows, buf_rows - rows), :] = jnp.zeros((buf_rows - rows, cols), MXU_DTYPE)
        wo_out[me] = wo_ref[...].astype(MXU_DTYPE)
        sm_out[me] = sm_ref[...]
        _exchange_blocks([wi_buf, wo_out, sm_out], send_sems, recv_sems)
        for d in range(N_DEV):
            wi_out[pl.ds(d * rows, rows), :] = wi_buf[d, pl.ds(0, rows), :]
        wi_out[pl.ds(N_DEV * rows, pad_rows), :] = jnp.zeros((pad_rows, cols), MXU_DTYPE)

    return pl.pallas_call(
        body, name="gather_weights", in_specs=[VMEM_SPEC] * 3, out_specs=[VMEM_SPEC] * 3,
        out_shape=[jax.ShapeDtypeStruct((N_DEV * rows + pad_rows, cols), MXU_DTYPE),
                   jax.ShapeDtypeStruct((N_DEV,) + w_out.shape, MXU_DTYPE), jax.ShapeDtypeStruct((N_DEV,) + small.shape, F32)],
        scratch_shapes=[pltpu.VMEM((N_DEV, buf_rows, cols), MXU_DTYPE)] + _exchange_sems(3),
        compiler_params=pltpu.CompilerParams(vmem_limit_bytes=VMEM_LIMIT))(w_in_t, w_out, small)


HOPS = 6


def reduce_gradients(tensors, small, name):
    n_t = len(tensors)
    arrays = [a for parts, _ in tensors for a, _ in parts]
    first_array = [sum(len(parts) for parts, _ in tensors[:t]) for t in range(n_t)]

    def pieces(t, j):
        parts, block_rows = tensors[t]
        out, base = [], 0
        for pi, (_, valid) in enumerate(parts):
            lo, hi = max(j * block_rows, base), min((j + 1) * block_rows, base + valid)
            if lo < hi:
                out.append((first_array[t] + pi, lo - base, lo - j * block_rows, hi - lo))
            base += valid
        return out

    def body(*refs):
        n_a = len(arrays)
        in_refs, small_ref = refs[:n_a], refs[n_a]
        out_refs, small_sum = refs[n_a + 1:n_a + 1 + n_t], refs[n_a + 1 + n_t]
        bufs, small_buf = refs[n_a + 2 + n_t:n_a + 2 + 5 * n_t], refs[n_a + 2 + 5 * n_t]
        s1_sems, r1_sems, s2_sems, r2_sems, small_send, small_recv = refs[n_a + 3 + 5 * n_t:]
        x, y, c = _position()
        chip = 2 * x + y

        def put(t, dst, j, add=None):
            for ai, src_row, dst_row, size in pieces(t, j):
                v = in_refs[ai][pl.ds(src_row, size), :]
                if add is not None:
                    v = v + add[pl.ds(dst_row, size), :].astype(F32)
                dst[pl.ds(dst_row, size), :] = v.astype(dst.dtype)

        def swap(t, k):
            send1, recv1 = bufs[4 * t], bufs[4 * t + 1]
            return pltpu.make_async_remote_copy(src_ref=send1.at[k], dst_ref=recv1.at[k], send_sem=s1_sems.at[4 * t + k],
                                                recv_sem=r1_sems.at[4 * t + k], device_id=(x, y, 1 - c), device_id_type=MESH)

        to_x, to_y, to_diag = 2 * (1 - x) + y, 2 * x + (1 - y), 2 * (1 - x) + (1 - y)
        x_dev, y_dev = (1 - x, y, c), (x, 1 - y, c)

        def half(ref, h):
            total = ref.shape[0]
            split = total // 2 // 16 * 16
            return ref.at[pl.ds(0, split)] if h == 0 else ref.at[pl.ds(split, total - split)]

        def hop(t, copy_id, src, dst, to):
            return pltpu.make_async_remote_copy(src_ref=src, dst_ref=dst, send_sem=s2_sems.at[HOPS * t + copy_id],
                                                recv_sem=r2_sems.at[HOPS * t + copy_id], device_id=to, device_id_type=MESH)

        def hops(t):
            send2, landing = bufs[4 * t + 2], bufs[4 * t + 3]
            return [hop(t, 0, half(send2.at[to_diag], 0), half(landing.at[0], 0), x_dev),
                    hop(t, 1, half(send2.at[to_diag], 1), half(landing.at[0], 1), y_dev),
                    hop(t, 2, half(send2.at[to_x], 0), half(landing.at[1], 0), x_dev),
                    hop(t, 3, half(send2.at[to_y], 1), half(landing.at[2], 1), y_dev),
                    hop(t, 4, half(send2.at[to_x], 1), half(landing.at[1], 1), x_dev),
                    hop(t, 5, half(send2.at[to_y], 0), half(landing.at[2], 0), y_dev)]

        def add_relay(t, slot, h):
            dst, src = half(bufs[4 * t + 2].at[slot], h), half(bufs[4 * t + 3].at[0], h)
            dst[...] = (dst[...].astype(F32) + src[...].astype(F32)).astype(dst.dtype)

        for t in range(n_t):
            send2 = bufs[4 * t + 2]
            pad = send2.shape[1] - tensors[t][1]
            if pad:
                send2[:, pl.ds(tensors[t][1], pad), :] = jnp.zeros((4, pad, send2.shape[2]), send2.dtype)
            for j in range(N_DEV):
                @pl.when((j & 1) != c)
                def _():
                    put(t, bufs[4 * t].at[j >> 1], j)
            for k in range(4):
                swap(t, k).start()

        small_buf[4 * x + 2 * y + c] = small_ref[...]
        _exchange_blocks([small_buf], small_send, small_recv)
        total = small_buf[0]
        for d in range(1, N_DEV):
            total = total + small_buf[d]
        small_sum[...] = total

        for t in range(n_t):
            recv1 = bufs[4 * t + 1]
            for k in range(4):
                swap(t, k).wait_recv()
                for j in (2 * k, 2 * k + 1):
                    @pl.when(((j & 1) == c) & (k != chip))
                    def _():
                        put(t, bufs[4 * t + 2].at[k], j, add=recv1.at[k])

                    @pl.when(((j & 1) == c) & (k == chip))
                    def _():
                        put(t, out_refs[t], j, add=recv1.at[k])
            for cp in hops(t)[0:4]:
                cp.start()

        for t in range(n_t):
            cps = hops(t)
            cps[0].wait_recv()
            add_relay(t, to_y, 0)
            cps[5].start()
            cps[1].wait_recv()
            add_relay(t, to_x, 1)
            cps[4].start()

        for t in range(n_t):
            cps, rows = hops(t), tensors[t][1]
            for first, second, slot in ((cps[2], cps[4], 1), (cps[3], cps[5], 2)):
                first.wait_recv()
                second.wait_recv()
                out_refs[t][...] += bufs[4 * t + 3][slot, pl.ds(0, rows), :].astype(F32)

        for t in range(n_t):
            for cp in hops(t):
                cp.wait_send()
            for k in range(4):
                swap(t, k).wait_send()

    scratch, out_shape = [], []
    for parts, block_rows in tensors:
        cols = parts[0][0].shape[1]
        tiled_rows = -(-block_rows // 16) * 16
        scratch += [pltpu.VMEM((4, block_rows, cols), MXU_DTYPE)] * 2
        scratch += [pltpu.VMEM((4, tiled_rows, cols), MXU_DTYPE), pltpu.VMEM((3, tiled_rows, cols), MXU_DTYPE)]
        out_shape.append(jax.ShapeDtypeStruct((block_rows, cols), F32))
    out_shape.append(jax.ShapeDtypeStruct(small.shape, F32))
    scratch += [pltpu.VMEM((N_DEV,) + small.shape, F32)] + [pltpu.SemaphoreType.DMA((4 * n_t,))] * 2
    scratch += [pltpu.SemaphoreType.DMA((HOPS * n_t,))] * 2 + _exchange_sems(1)
    return pl.pallas_call(
        body, name=name, in_specs=[VMEM_SPEC] * (len(arrays) + 1), out_specs=[VMEM_SPEC] * (n_t + 1), out_shape=out_shape,
        scratch_shapes=scratch, compiler_params=pltpu.CompilerParams(vmem_limit_bytes=VMEM_LIMIT),
    )(*arrays, small)


def _adamw_step(w, g, m, v):
    mn = ADAM_B1 * m + (1.0 - ADAM_B1) * g
    vn = ADAM_B2 * v + (1.0 - ADAM_B2) * jnp.square(g)
    m_hat = mn / (1.0 - ADAM_B1 ** ADAM_STEP)
    v_hat = vn / (1.0 - ADAM_B2 ** ADAM_STEP)
    return -ADAM_LR * (m_hat / (jnp.sqrt(v_hat) + ADAM_EPS) + ADAM_WD * w), mn, vn


def adamw(w, g, m, v, name):
    rows, cols = w.shape
    tr = 256 if rows % 256 == 0 else rows

    def body(w_ref, g_ref, m_ref, v_ref, d_ref, nm_ref, nv_ref):
        d_ref[...], nm_ref[...], nv_ref[...] = _adamw_step(w_ref[...], g_ref[...], m_ref[...], v_ref[...])

    spec = pl.BlockSpec((tr, cols), lambda i: (i, 0))
    shape = jax.ShapeDtypeStruct((rows, cols), F32)
    return pl.pallas_call(body, grid=(rows // tr,), name=name, in_specs=[spec] * 4, out_specs=[spec] * 3,
                          out_shape=[shape] * 3, compiler_params=_cparams("arbitrary"))(w, g, m, v)


def adamw_w_in(w, g_t, m, v):
    def body(w_ref, g_ref, m_ref, v_ref, go_ref, d_ref, nm_ref, nv_ref):
        g = g_ref[...]
        go_ref[:, 0, :] = g
        d_ref[:, 0, :], nm_ref[:, 0, :], nv_ref[:, 0, :] = _adamw_step(w_ref[:, 0, :], g, m_ref[:, 0, :], v_ref[:, 0, :])

    return pl.pallas_call(body, name="adamw_w_in", in_specs=[VMEM_SPEC] * 4, out_specs=[VMEM_SPEC] * 4,
                          out_shape=[jax.ShapeDtypeStruct(w.shape, F32)] * 4,
                          compiler_params=pltpu.CompilerParams(vmem_limit_bytes=VMEM_LIMIT))(w, g_t, m, v)


def _pad_rows(a, rows=8):
    return jnp.pad(a, ((0, rows - a.shape[0]), (0, 0)))


def _pad_lanes(a, lanes=128):
    return jnp.pad(a, ((0, 0), (0, lanes - a.shape[1])))


def kernel(x, meta_tokens, norm_w, w_in, conv_w, hg_lb_logits, hg_norm_w, gdn_A_log, gdn_dt_bias, gdn_norm_w, w_out, final_norm_w, loss_target, m_meta_tokens, m_norm_w, m_w_in, m_conv_w, m_hg_lb_logits, m_hg_norm_w, m_gdn_A_log, m_gdn_dt_bias, m_gdn_norm_w, m_w_out, m_final_norm_w, v_meta_tokens, v_norm_w, v_w_in, v_conv_w, v_hg_lb_logits, v_hg_norm_w, v_gdn_A_log, v_gdn_dt_bias, v_gdn_norm_w, v_w_out, v_final_norm_w):
    b, seq, _ = x.shape
    n = b * seq
    dev = 4 * lax.axis_index("x") + 2 * lax.axis_index("y") + lax.axis_index("c")
    col_shard = IN_COLS // N_DEV

    small_w = jnp.concatenate([_pad_lanes(meta_tokens, 256), _pad_rows(_pad_lanes(conv_w[0], 256))], axis=0)
    w_t, w_out_g, small_g = gather_weights(jnp.transpose(w_in, (2, 0, 1)), w_out[0], small_w, AB_PAD - 2 * HEADS)
    meta_g = small_g[:, 0:N_META, 0:D_MODEL // N_DEV]
    conv_g = small_g[:, N_META:N_META + CONV_TAPS, 0:QKV // N_DEV]
    w_out_full = w_out_g.reshape(2 * WIDTH, D_MODEL)
    cw = jnp.transpose(conv_g, (1, 0, 2)).reshape(CONV_TAPS, QKV)
    meta = jnp.transpose(meta_g, (1, 0, 2)).reshape(N_META, D_MODEL)
    alog = _pad_lanes(gdn_A_log)
    dtb = _pad_lanes(gdn_dt_bias)
    fw = final_norm_w.reshape(1, D_MODEL)

    h0 = jnp.concatenate([jnp.zeros((CHUNK - N_META, D_MODEL), F32), meta], axis=0)
    x2 = x.reshape(n, D_MODEL)
    phg, pgd, pab, phg0, pgd0, pab0, u0 = in_proj(x2, h0, norm_w, w_t)
    phg3, pgd3, pab3 = phg.reshape(b, seq, 4 * WIDTH), pgd.reshape(b, seq, 4 * WIDTH), pab.reshape(b, seq, AB_PAD)
    nc = seq // (SCAN_CHUNKS * CHUNK)
    hg_loc = hg_local_fwd(phg3, hg_lb_logits)
    hg_lead = hg_local_lead(phg0, hg_lb_logits)
    gd_loc, gd_inv = gd_local_fwd(pgd3, pgd0, pab3, cw, alog, dtb)
    gd_lead, gd_inv0 = gd_local_lead(pgd0, pab0, cw, alog, dtb)
    (y_hg, s_hg), (y_gd, s_gd) = run_scans([hg_scan_fwd(phg3, phg0, hg_loc, hg_lead, hg_norm_w),
                                            gd_scan_fwd(pgd3, pgd0, gd_loc, gd_lead, gdn_norm_w)], nc, "scans")

    dh2, dy_hg, dy_gd, g_w_out, loss_part, g_fw = out_proj_loss(
        x2, loss_target.reshape(n, D_MODEL), y_hg.reshape(n, WIDTH), y_gd.reshape(n, WIDTH), w_out_full, fw)

    hb, gb = run_scans([hg_scan_bwd(phg3, phg0, hg_loc, hg_lead, hg_norm_w, s_hg, dy_hg.reshape(b, seq, WIDTH)),
                        gd_scan_bwd(pgd3, pgd0, gd_loc, gd_lead, gdn_norm_w, s_gd, dy_gd.reshape(b, seq, WIDTH))],
                       nc, "scans_bwd")
    dphg, g_lb = hg_local_bwd(phg3, hg_lb_logits, *hb[0:6])
    dphg0, g_lb0 = hg_local_bwd_lead(phg0, hg_lb_logits, *hb[6:12])
    g_hg_nw = hb[12]
    dpgd, dpab, dtail, g_cw, g_alog, g_dtb = gd_local_bwd(pgd3, pgd0, pab3, cw, alog, dtb, gd_inv, gb[0:6], gb[6])
    dpgd0, dpab0, g_cw0, g_alog0, g_dtb0 = gd_local_bwd_lead(pgd0, pab0, cw, alog, dtb, gd_inv0, gb[7:13], gb[13],
                                                             dtail.sum(0))
    g_gd_nw = gb[14]
    dphg, dpgd, dpab = dphg.reshape(n, 4 * WIDTH), dpgd.reshape(n, 4 * WIDTH), dpab.reshape(n, AB_PAD)

    grad_x, dh0, g_nw, g_w_hg, g_w_gd, g_w_ab = in_proj_bwd(dphg, dpgd, dpab, w_t, x2, dh2, norm_w, h0, u0, dphg0, dpgd0, dpab0)

    small = jnp.concatenate([
        g_nw.reshape(8, 128), (g_lb + g_lb0).reshape(8, 128), _pad_rows(g_hg_nw), _pad_rows(g_alog + g_alog0),
        _pad_rows(g_dtb + g_dtb0), _pad_rows(g_gd_nw), g_fw.reshape(8, 128), (g_cw + g_cw0).reshape(48, 128),
        dh0[CHUNK - N_META:CHUNK].reshape(128, 128), loss_part], axis=0)
    g_w_in_t, g_w_out, small = reduce_gradients(
        [([(g_w_hg, 4 * WIDTH), (g_w_gd, 4 * WIDTH), (g_w_ab, 2 * HEADS)], col_shard),
         ([(g_w_out, 2 * WIDTH)], (2 * WIDTH) // N_DEV)], small, "reduce_gradients")
    g_norm_w = small[0:8].reshape(1, D_MODEL)
    g_lb = small[8:16].reshape(2, WIDTH)
    g_hg_nw = small[16:17]
    g_alog = small[24:25, 0:HEADS]
    g_dtb = small[32:33, 0:HEADS]
    g_gd_nw = small[40:41]
    g_fw = small[48:56].reshape(1, D_MODEL)
    g_cw_full = small[56:104].reshape(CONV_TAPS, QKV)
    g_meta_full = small[104:232].reshape(N_META, D_MODEL)
    loss = small[232, 0]
    g_conv = lax.dynamic_slice_in_dim(g_cw_full, dev * (QKV // N_DEV), QKV // N_DEV, axis=1)
    g_meta = lax.dynamic_slice_in_dim(g_meta_full, dev * (D_MODEL // N_DEV), D_MODEL // N_DEV, axis=1)

    names = ["meta_tokens", "norm_w", "w_in", "conv_w", "hg_lb_logits", "hg_norm_w", "gdn_A_log", "gdn_dt_bias",
             "gdn_norm_w", "w_out", "final_norm_w"]
    weights = [meta_tokens, norm_w, w_in, conv_w, hg_lb_logits, hg_norm_w, gdn_A_log, gdn_dt_bias, gdn_norm_w, w_out,
               final_norm_w]
    moms = [m_meta_tokens, m_norm_w, m_w_in, m_conv_w, m_hg_lb_logits, m_hg_norm_w, m_gdn_A_log, m_gdn_dt_bias,
            m_gdn_norm_w, m_w_out, m_final_norm_w]
    vars_ = [v_meta_tokens, v_norm_w, v_w_in, v_conv_w, v_hg_lb_logits, v_hg_norm_w, v_gdn_A_log, v_gdn_dt_bias,
             v_gdn_norm_w, v_w_out, v_final_norm_w]
    grads2d = [g_meta, g_norm_w, g_w_in_t, g_conv, g_lb, g_hg_nw, g_alog, g_dtb, g_gd_nw, g_w_out, g_fw]
    grads, deltas, new_ms, new_vs = [], [], [], []
    for nm, w, g2, m, v in zip(names, weights, grads2d, moms, vars_):
        if nm == "w_in":
            to3, back = (lambda a: jnp.transpose(a, (2, 0, 1))), (lambda a: jnp.transpose(a, (1, 2, 0)))
            g2, d, nm_, nv_ = adamw_w_in(to3(w), g2, to3(m), to3(v))
        else:
            to2d, back = (lambda a, s=g2.shape: a.reshape(s)), (lambda a, s=w.shape: a.reshape(s))
            d, nm_, nv_ = adamw(to2d(w), g2, to2d(m), to2d(v), "adamw_" + nm)
        grads.append(back(g2))
        deltas.append(back(d))
        new_ms.append(back(nm_))
        new_vs.append(back(nv_))
    return (loss, grad_x.reshape(x.shape), *grads, *deltas, *new_ms, *new_vs)
```

```python
import jax
import jax.numpy as jnp
from jax import lax
from jax.experimental import pallas as pl
from jax.experimental.pallas import tpu as pltpu

F32 = jnp.float32
BF16 = jnp.bfloat16
MXU_DTYPE = BF16

D_MODEL = 1024
N_META = 16
CHUNK = 64
SUB = 16
ROW_TILE_BF16 = 16
HEADS = 4
DH = 128
WIDTH = HEADS * DH
QKV = 3 * WIDTH
CONV_TAPS = 4
HALO = 8
EPS = 1e-6
IN_COLS = 4 * WIDTH + 4 * WIDTH + 2 * HEADS
AB_PAD = 128
N_DEV = 8
LOCAL_CHUNKS = 4
SCAN_CHUNKS = 2
VMEM_LIMIT = 56 * 1024 * 1024
VMEM_LIMIT_LARGE = 60 * 1024 * 1024

ADAM_LR = 0.001
ADAM_B1 = 0.9
ADAM_B2 = 0.999
ADAM_EPS = 1e-08
ADAM_WD = 0.01
ADAM_STEP = 10

VMEM_SPEC = pl.BlockSpec(memory_space=pltpu.VMEM)
MESH = pl.DeviceIdType.MESH


def _mm_tn(a, b):
    return lax.dot_general(a.astype(MXU_DTYPE), b.astype(MXU_DTYPE), (((0,), (0,)), ((), ())), preferred_element_type=F32)


def _bmm(a, b):
    return lax.dot_general(a.astype(MXU_DTYPE), b.astype(MXU_DTYPE), (((2,), (1,)), ((0,), (0,))), preferred_element_type=F32)


def _bmm_nt(a, b):
    return lax.dot_general(a.astype(MXU_DTYPE), b.astype(MXU_DTYPE), (((2,), (2,)), ((0,), (0,))), preferred_element_type=F32)


def _bmm_tn(a, b):
    return lax.dot_general(a.astype(MXU_DTYPE), b.astype(MXU_DTYPE), (((1,), (1,)), ((0,), (0,))), preferred_element_type=F32)


def _iota2(n, m):
    return lax.broadcasted_iota(jnp.int32, (n, m), 0), lax.broadcasted_iota(jnp.int32, (n, m), 1)


def _silu(x):
    return x * jax.nn.sigmoid(x)


def _gated_norm(o, z, nw):
    return o * lax.rsqrt(jnp.mean(o * o, axis=-1, keepdims=True) + EPS) * nw * _silu(z)


def _heads(a, nb):
    return jnp.stack([a[c * CHUNK:(c + 1) * CHUNK, h * DH:(h + 1) * DH] for c in range(nb) for h in range(HEADS)], axis=0)


def _unheads(a3, nb):
    return jnp.concatenate(
        [jnp.concatenate([a3[c * HEADS + h] for h in range(HEADS)], axis=1) for c in range(nb)], axis=0)


def _split3(x):
    hi = x.astype(BF16)
    r1 = x - hi.astype(F32)
    mid = r1.astype(BF16)
    return hi, mid, (r1 - mid.astype(F32)).astype(BF16)


def _select_mm(pattern, n_out, n_in, transposed, x):
    rows, inner = (n_in, n_out) if transposed else (n_out, n_in)
    r, c = _iota2(rows, 3 * inner)
    c = c - jnp.where(c >= inner, inner, 0) - jnp.where(c >= 2 * inner, inner, 0)
    s = jnp.where(pattern(c, r) if transposed else pattern(r, c), 1.0, 0.0).astype(BF16)
    return jnp.dot(s, jnp.concatenate(_split3(x), axis=0), preferred_element_type=F32)


def _select_rows(pattern, n_out, x):
    @jax.custom_vjp
    def apply(v):
        return _select_mm(pattern, n_out, CHUNK, False, v)

    apply.defvjp(lambda v: (_select_mm(pattern, n_out, CHUNK, False, v), None),
                 lambda _, d: (_select_mm(pattern, n_out, CHUNK, True, d),))
    return apply(x)


def _cumsum_chunks(x, nb):
    return jnp.concatenate([_select_rows(lambda i, j: j <= i, CHUNK, x[c * CHUNK:(c + 1) * CHUNK]) for c in range(nb)], axis=0)


HG_LEVELS = 6


def _hg_sums(i, j):
    lvl, t = i >> HG_LEVELS, i & (CHUNK - 1)
    last = t
    for l in range(1, HG_LEVELS + 1):
        width = HG_LEVELS + 1 - l
        last = jnp.where(lvl == l, ((t >> width) << width) + (CHUNK >> l) - 1, last)
    return j <= last


def hg_local(p, logits):
    nb = p.shape[0] // CHUNK
    l0, l1 = logits[0:1], logits[1:2]
    mx = jnp.maximum(l0, l1)
    e0, e1 = jnp.exp(l0 - mx), jnp.exp(l1 - mx)
    lb = e0 / (e0 + e1)
    q = _silu(p[:, 0:WIDTH])
    f = lb + (1.0 - lb) * jax.nn.sigmoid(p[:, WIDTH:2 * WIDTH])
    k = 1.0 - f
    logf = jnp.log(f)
    sums = [_select_rows(_hg_sums, (HG_LEVELS + 1) * CHUNK, logf[c * CHUNK:(c + 1) * CHUNK]) for c in range(nb)]
    level = lambda l: _heads(jnp.concatenate([s[l * CHUNK:(l + 1) * CHUNK] for s in sums], axis=0), nb)
    q3, k3, v3, g3 = _heads(q, nb), _heads(k, nb), _heads(p[:, 2 * WIDTH:3 * WIDTH], nb), level(0)
    r, c = _iota2(CHUNK, CHUNK)
    row = lax.broadcasted_iota(jnp.int32, (CHUNK, DH), 0)
    a = jnp.where(r == c, _bmm_nt(q3, k3), 0.0)
    for l in range(1, HG_LEVELS + 1):
        sh = HG_LEVELS - l
        qk = jnp.where(((row >> sh) & 1) == 1, q3, k3) * jnp.exp(-jnp.abs(g3 - level(l)))
        pair = ((r >> (sh + 1)) == (c >> (sh + 1))) & (((r >> sh) & 1) == 1) & (((c >> sh) & 1) == 0)
        a = a + jnp.where(pair, _bmm_nt(qk, qk), 0.0)
    o = _bmm(a, v3)
    glast = g3[:, CHUNK - 1:CHUNK, :]
    egs = tuple(jnp.concatenate([jnp.exp(glast[c * HEADS + h]) for h in range(HEADS)], axis=1) for c in range(nb))
    return _unheads(q3 * jnp.exp(g3), nb), _unheads(k3 * jnp.exp(glast - g3), nb), _unheads(o, nb), egs


def hg_scan(q_in, k_out, v, eg, o_intra, z, nw, st):
    o = o_intra + _bmm_nt(q_in, st)
    return _gated_norm(o, z, nw), st * eg + _bmm_tn(v, k_out)


def _tri_y_impl(a):
    r, c = _iota2(CHUNK, CHUNK)
    same16 = (r // SUB) == (c // SUB)
    same32 = (r // (2 * SUB)) == (c // (2 * SUB))
    a0 = jnp.where(same16, a, 0.0)
    y = -a0
    pw = _bmm(a0, a0)
    for _ in range(2):
        y = y + pw + _bmm(y, pw)
        pw = _bmm(pw, pw)
    y = y + pw + _bmm(y, pw)
    for ak in (jnp.where(same32 & jnp.logical_not(same16), a, 0.0), jnp.where(same32, 0.0, a)):
        m = ak + _bmm(y, ak)
        y = y - (m + _bmm(m, y))
    return y


@jax.custom_vjp
def _tri_y(a):
    return _tri_y_impl(a)


def _tri_y_fwd(a):
    y = _tri_y_impl(a)
    return y, y


def _tri_y_bwd(y, dy):
    n = dy + _bmm_tn(y, dy)
    return (-(n + _bmm_nt(n, y)),)


_tri_y.defvjp(_tri_y_fwd, _tri_y_bwd)


def _saved_inverse(y):
    @jax.custom_vjp
    def inverse(a):
        return y

    inverse.defvjp(lambda a: (y, None), lambda _, dy: _tri_y_bwd(y, dy))
    return inverse


def _head_rows(a3, nb):
    return jnp.concatenate([a3[g] for g in range(nb * HEADS)], axis=0)


def _rows_down(x, s):
    rows = x.shape[0]

    @jax.custom_vjp
    def rotate(v):
        return pltpu.roll(v, s, 0)

    rotate.defvjp(lambda v: (pltpu.roll(v, s, 0), None), lambda _, d: (pltpu.roll(d, rows - s, 0),))
    return rotate(x)


def gd_local(xx, ab, cw, alog, dtb, inverse=_tri_y):
    n = ab.shape[0]
    nb = n // CHUNK
    conv = cw[CONV_TAPS - 1:CONV_TAPS] * xx[HALO:HALO + n]
    for j in range(CONV_TAPS - 1):
        conv = conv + cw[j:j + 1] * _rows_down(xx, CONV_TAPS - 1 - j)[HALO:HALO + n]
    act = _silu(conv)
    x = ab + dtb
    g_all = -jnp.exp(alog) * (jnp.maximum(x, 0.0) + jnp.log1p(jnp.exp(-jnp.abs(x))))
    beta_all = jax.nn.sigmoid(ab)
    gam_all = _cumsum_chunks(g_all, nb)
    q3, k3, v3 = _heads(act[:, 0:WIDTH], nb), _heads(act[:, WIDTH:2 * WIDTH], nb), _heads(act[:, 2 * WIDTH:QKV], nb)
    q3 = q3 * lax.rsqrt(jnp.sum(q3 * q3, axis=-1, keepdims=True) + EPS) * (DH ** -0.5)
    k3 = k3 * lax.rsqrt(jnp.sum(k3 * k3, axis=-1, keepdims=True) + EPS)
    pairs = [(c, h) for c in range(nb) for h in range(HEADS)]
    beta = jnp.stack([beta_all[c * CHUNK:(c + 1) * CHUNK, HEADS + h:HEADS + h + 1] for c, h in pairs], axis=0)
    gam = jnp.stack([gam_all[c * CHUNK:(c + 1) * CHUNK, h:h + 1] for c, h in pairs], axis=0)
    gam_t = [gam_all[c * CHUNK:(c + 1) * CHUNK].T for c in range(nb)]
    gam_row = jnp.stack([gam_t[c][h:h + 1, :] for c, h in pairs], axis=0)
    glast = gam[:, CHUNK - 1:CHUNK, :]
    r, c = _iota2(CHUNK, CHUNK)
    dec = jnp.exp(jnp.where(c < r, gam - gam_row, -jnp.inf))
    y = inverse(beta * _bmm_nt(k3, k3) * dec)
    eg = jnp.exp(gam)
    rhs = jnp.concatenate([beta * v3, (beta * eg) * k3], axis=2)
    sol = rhs + _bmm(y, rhs)
    qk = _bmm_nt(q3, k3) * jnp.where(r == c, 1.0, dec)
    eas = tuple(jnp.exp(gam_all[(c + 1) * CHUNK - 1:(c + 1) * CHUNK]) for c in range(nb))
    return (_unheads(sol[:, :, 0:DH], nb), _unheads(sol[:, :, DH:2 * DH], nb), _unheads(q3 * eg, nb),
            _unheads(k3 * jnp.exp(glast - gam), nb), _head_rows(qk, nb), eas), _head_rows(y, nb)


def gd_scan(uu, ww, qe, ke, qk, ea, z, nw, s):
    u = uu - _bmm(ww, s)
    o = _bmm(qe, s) + _bmm(qk, u)
    return _gated_norm(o, z, nw), ea * s + _bmm_tn(ke, u)


def _cparams(*sem):
    return pltpu.CompilerParams(dimension_semantics=sem, vmem_limit_bytes=VMEM_LIMIT)


def _row_tile(n):
    for t in (512, 256, 128, 64):
        if n % t == 0:
            return t
    raise ValueError(f"unsupported token count {n}")


def _w_in_specs():
    once = pl.Buffered(1)
    return [pl.BlockSpec((4 * WIDTH, D_MODEL), lambda *i: (0, 0), pipeline_mode=once),
            pl.BlockSpec((4 * WIDTH, D_MODEL), lambda *i: (1, 0), pipeline_mode=once),
            pl.BlockSpec((AB_PAD, D_MODEL), lambda *i: (8 * WIDTH // AB_PAD, 0), pipeline_mode=once)]


def in_proj(h, h0, norm_w, w_t):
    n = h.shape[0]
    tm = _row_tile(n)
    nt = (((1,), (1,)), ((), ()))

    def body(h_ref, h0_ref, nw_ref, whg_ref, wgd_ref, wab_ref, phg_ref, pgd_ref, pab_ref, phg0_ref, pgd0_ref, pab0_ref, u0_ref):
        def project(x, hg_ref, gd_ref, ab_ref):
            u = (x * lax.rsqrt(jnp.mean(x * x, axis=-1, keepdims=True) + EPS) * nw_ref[...]).astype(MXU_DTYPE)
            hg_ref[...] = lax.dot_general(u, whg_ref[...], nt, preferred_element_type=F32)
            gd_ref[...] = lax.dot_general(u, wgd_ref[...], nt, preferred_element_type=F32)
            ab_ref[...] = lax.dot_general(u, wab_ref[...], nt, preferred_element_type=F32)
            return u

        @pl.when(pl.program_id(0) == 0)
        def _():
            u0_ref[...] = project(h0_ref[...], phg0_ref, pgd0_ref, pab0_ref)

        project(h_ref[...], phg_ref, pgd_ref, pab_ref)

    n0 = h0.shape[0]
    row = lambda w: pl.BlockSpec((tm, w), lambda i: (i, 0))
    lead = lambda w: pl.BlockSpec((n0, w), lambda i: (0, 0))
    widths = [4 * WIDTH, 4 * WIDTH, AB_PAD]
    return pl.pallas_call(
        body, grid=(n // tm,), name="in_proj",
        in_specs=[row(D_MODEL), lead(D_MODEL), pl.BlockSpec(norm_w.shape, lambda i: (0, 0))] + _w_in_specs(),
        out_specs=[row(w) for w in widths] + [lead(w) for w in widths] + [lead(D_MODEL)],
        out_shape=[jax.ShapeDtypeStruct((n, w), F32) for w in widths] + [jax.ShapeDtypeStruct((n0, w), F32) for w in widths]
        + [jax.ShapeDtypeStruct((n0, D_MODEL), MXU_DTYPE)],
        compiler_params=_cparams("arbitrary"),
    )(h, h0, norm_w, w_t, w_t, w_t)


def out_proj_loss(x, tgt, y_hg, y_gd, w_out, fw):
    n = x.shape[0]
    tm = _row_tile(n)
    inv_d = 1.0 / D_MODEL

    def body(x_ref, t_ref, yh_ref, yg_ref, w_ref, fw_ref, dh_ref, dyh_ref, dyg_ref, dw_ref, loss_ref, dfw_ref):
        @pl.when(pl.program_id(0) == 0)
        def _():
            dw_ref[...] = jnp.zeros_like(dw_ref)
            loss_ref[...] = jnp.zeros_like(loss_ref)
            dfw_ref[...] = jnp.zeros_like(dfw_ref)

        yh, yg = yh_ref[...], yg_ref[...]
        wa, wb = w_ref[0:WIDTH, :], w_ref[WIDTH:2 * WIDTH, :]
        h2 = x_ref[...] + jnp.dot(yh, wa, preferred_element_type=F32) + jnp.dot(yg, wb, preferred_element_type=F32)
        r2 = lax.rsqrt(jnp.mean(h2 * h2, axis=-1, keepdims=True) + EPS)
        nrm = h2 * r2
        fwv = fw_ref[...]
        err = nrm * fwv - t_ref[...]
        loss_ref[...] += jnp.full(loss_ref.shape, 0.5 * inv_d * jnp.sum(err * err), F32)
        dout = err * inv_d
        dfw_ref[...] += jnp.sum(dout * nrm, axis=0, keepdims=True)
        dn = dout * fwv
        dh2 = r2 * (dn - nrm * jnp.mean(dn * nrm, axis=-1, keepdims=True))
        dh_ref[...] = dh2
        dhb = dh2.astype(MXU_DTYPE)
        dyh_ref[...] = lax.dot_general(dhb, wa, (((1,), (1,)), ((), ())), preferred_element_type=F32)
        dyg_ref[...] = lax.dot_general(dhb, wb, (((1,), (1,)), ((), ())), preferred_element_type=F32)
        dw_ref[0:WIDTH, :] += lax.dot_general(yh, dhb, (((0,), (0,)), ((), ())), preferred_element_type=F32)
        dw_ref[WIDTH:2 * WIDTH, :] += lax.dot_general(yg, dhb, (((0,), (0,)), ((), ())), preferred_element_type=F32)

    row = lambda w: pl.BlockSpec((tm, w), lambda i: (i, 0))
    full = lambda s: pl.BlockSpec(s, lambda i: (0, 0))
    return pl.pallas_call(
        body, grid=(n // tm,), name="out_proj_loss",
        in_specs=[row(D_MODEL), row(D_MODEL), row(WIDTH), row(WIDTH), full(w_out.shape), full(fw.shape)],
        out_specs=[row(D_MODEL), row(WIDTH), row(WIDTH), full((2 * WIDTH, D_MODEL)), full((8, 128)), full((1, D_MODEL))],
        out_shape=[jax.ShapeDtypeStruct((n, D_MODEL), F32), jax.ShapeDtypeStruct((n, WIDTH), F32),
                   jax.ShapeDtypeStruct((n, WIDTH), F32), jax.ShapeDtypeStruct((2 * WIDTH, D_MODEL), F32),
                   jax.ShapeDtypeStruct((8, 128), F32), jax.ShapeDtypeStruct((1, D_MODEL), F32)],
        compiler_params=_cparams("arbitrary"),
    )(x, tgt, y_hg, y_gd, w_out, fw)


def in_proj_bwd(dphg, dpgd, dpab, w_t, h, dh2, norm_w, h0, u0, dphg0, dpgd0, dpab0):
    n = h.shape[0]
    tm = _row_tile(n)
    steps = n // tm

    def body(dphg_ref, dpgd_ref, dpab_ref, whg_ref, wgd_ref, wab_ref, h_ref, dh2_ref, nw_ref, h0_ref, u0_ref, d0hg_ref,
             d0gd_ref, d0ab_ref, dx_ref, dx0_ref, dnw_ref, ghg_ref, ggd_ref, gab_ref, acc_hg, acc_gd, acc_ab):
        i = pl.program_id(0)
        nwv = nw_ref[...]

        def norm_bwd(dps, x):
            du = jnp.dot(dps[0], whg_ref[...], preferred_element_type=F32)
            du += jnp.dot(dps[1], wgd_ref[...], preferred_element_type=F32)
            du += jnp.dot(dps[2], wab_ref[...], preferred_element_type=F32)
            r = lax.rsqrt(jnp.mean(x * x, axis=-1, keepdims=True) + EPS)
            nrm = x * r
            dn = du * nwv
            return r * (dn - nrm * jnp.mean(dn * nrm, axis=-1, keepdims=True)), nrm, jnp.sum(du * nrm, axis=0, keepdims=True)

        def accumulate(dps, u, first):
            for acc, dp in zip((acc_hg, acc_gd, acc_ab), dps):
                step = min(acc.shape[0], 512)
                for lo in range(0, acc.shape[0], step):
                    part = _mm_tn(dp[:, lo:lo + step], u)
                    acc[lo:lo + step, :] = part if first else acc[lo:lo + step, :] + part

        @pl.when(i == 0)
        def _():
            dps0 = (d0hg_ref[...], d0gd_ref[...], d0ab_ref[...])
            dx0_ref[...], _, dnw_ref[...] = norm_bwd(dps0, h0_ref[...])
            accumulate(dps0, u0_ref[...], True)

        dps = (dphg_ref[...], dpgd_ref[...], dpab_ref[...])
        dx, nrm, dnw = norm_bwd(dps, h_ref[...])
        dx_ref[...] = dh2_ref[...] + dx
        dnw_ref[...] += dnw
        accumulate(dps, (nrm * nwv).astype(MXU_DTYPE), False)

        @pl.when(i == steps - 1)
        def _():
            pltpu.sync_copy(acc_hg, ghg_ref)
            pltpu.sync_copy(acc_gd, ggd_ref)
            pltpu.sync_copy(acc_ab, gab_ref)

    row = lambda w: pl.BlockSpec((tm, w), lambda i: (i, 0))
    full = lambda a: pl.BlockSpec(a.shape, lambda i: (0, 0), pipeline_mode=pl.Buffered(1))
    anywhere = pl.BlockSpec(memory_space=pl.ANY)
    return pl.pallas_call(
        body, grid=(steps,), name="in_proj_bwd",
        in_specs=[row(4 * WIDTH), row(4 * WIDTH), row(AB_PAD)] + _w_in_specs() + [row(D_MODEL), row(D_MODEL), full(norm_w),
                                                                                   full(h0), full(u0), full(dphg0), full(dpgd0),
                                                                                   full(dpab0)],
        out_specs=[row(D_MODEL), pl.BlockSpec(h0.shape, lambda i: (0, 0)), pl.BlockSpec((1, D_MODEL), lambda i: (0, 0)),
                   anywhere, anywhere, anywhere],
        out_shape=[jax.ShapeDtypeStruct((n, D_MODEL), F32), jax.ShapeDtypeStruct(h0.shape, F32),
                   jax.ShapeDtypeStruct((1, D_MODEL), F32), jax.ShapeDtypeStruct((4 * WIDTH, D_MODEL), F32),
                   jax.ShapeDtypeStruct((4 * WIDTH, D_MODEL), F32), jax.ShapeDtypeStruct((AB_PAD, D_MODEL), F32)],
        scratch_shapes=[pltpu.VMEM((4 * WIDTH, D_MODEL), F32), pltpu.VMEM((4 * WIDTH, D_MODEL), F32),
                        pltpu.VMEM((AB_PAD, D_MODEL), F32)],
        compiler_params=pltpu.CompilerParams(dimension_semantics=("arbitrary",), vmem_limit_bytes=VMEM_LIMIT_LARGE),
    )(dphg, dpgd, dpab, w_t, w_t, w_t, h, dh2, norm_w, h0, u0, dphg0, dpgd0, dpab0)


def _sds(shape, dtype=F32):
    return jax.ShapeDtypeStruct(shape, dtype)


def _pairs(b):
    return [(i, h) for i in range(b) for h in range(HEADS)]


def _load_slabs(ref, b, k):
    return jnp.stack([ref[i, k * CHUNK:(k + 1) * CHUNK, h * DH:(h + 1) * DH].astype(F32) for i, h in _pairs(b)], axis=0)


def _lead_slabs(a, b):
    return jnp.stack([a[:, h * DH:(h + 1) * DH].astype(F32) for _, h in _pairs(b)], axis=0)


def _rows(a3, i):
    return jnp.concatenate([a3[i * HEADS + h] for h in range(HEADS)], axis=1)


def _store_slabs(ref, a3, b, k):
    for i in range(b):
        ref[i, k * CHUNK:(k + 1) * CHUNK, :] = _rows(a3, i).astype(ref.dtype)


def _sum_rows(a3, b):
    out = _rows(a3, 0)
    for i in range(1, b):
        out = out + _rows(a3, i)
    return out


def _save_states(ref, s, b, k):
    for i in range(b):
        ref[i, k] = jnp.concatenate([s[i * HEADS + h] for h in range(HEADS)], axis=0)


def _load_states(ref, b, k):
    return jnp.stack([ref[i, k, h * DH:(h + 1) * DH, :] for i, h in _pairs(b)], axis=0)


def hg_local_fwd(p, p0, logits):
    b, seq, _ = p.shape
    rows = LOCAL_CHUNKS * CHUNK
    nreal = seq // CHUNK

    def body(p_ref, p0_ref, lg_ref, q_ref, k_ref, o_ref, eg_ref, q0_ref, k0_ref, o0_ref, eg0_ref):
        @pl.when((pl.program_id(0) == 0) & (pl.program_id(1) == 0))
        def _():
            q_in, k_out, o0_ref[...], (eg0_ref[...],) = hg_local(p0_ref[...], lg_ref[...])
            q0_ref[...], k0_ref[...] = q_in.astype(MXU_DTYPE), k_out.astype(MXU_DTYPE)

        q_in, k_out, o_intra, egs = hg_local(p_ref[...], lg_ref[...])
        q_ref[...], k_ref[...], o_ref[...] = q_in.astype(MXU_DTYPE), k_out.astype(MXU_DTYPE), o_intra
        for c in range(LOCAL_CHUNKS):
            eg_ref[c] = egs[c]

    slab = pl.BlockSpec((None, rows, WIDTH), lambda s, g: (s, g, 0))
    const = lambda shape: pl.BlockSpec(shape, lambda s, g: (0, 0))
    lead_shapes = [(CHUNK, WIDTH)] * 3 + [(1, WIDTH)]
    out = pl.pallas_call(
        body, grid=(b, seq // rows), name="hgrn2_local",
        in_specs=[pl.BlockSpec((None, rows, 4 * WIDTH), lambda s, g: (s, g, 0)), const(p0.shape), const(logits.shape)],
        out_specs=[slab, slab, slab, pl.BlockSpec((None, LOCAL_CHUNKS, 1, WIDTH), lambda s, g: (s, g, 0, 0))]
        + [const(s) for s in lead_shapes],
        out_shape=[_sds((b, seq, WIDTH), MXU_DTYPE)] * 2 + [_sds((b, seq, WIDTH)), _sds((b, nreal, 1, WIDTH))]
        + [_sds(lead_shapes[0], MXU_DTYPE)] * 2 + [_sds(lead_shapes[2]), _sds(lead_shapes[3])],
        compiler_params=_cparams("arbitrary", "arbitrary"),
    )(p, p0, logits)
    return out[0:4], out[4:8]


def _hg_scan_args(b, k, q_ref, k_ref, o_ref, v_ref, z_ref, eg_ref):
    eg = jnp.stack([eg_ref[i, k, :, h * DH:(h + 1) * DH] for i, h in _pairs(b)], axis=0)
    return (_load_slabs(q_ref, b, k), _load_slabs(k_ref, b, k), _load_slabs(v_ref, b, k), eg, _load_slabs(o_ref, b, k),
            _load_slabs(z_ref, b, k))


def _hg_lead_args(b, q0_ref, k0_ref, o0_ref, p0_ref, eg0_ref):
    eg = jnp.stack([eg0_ref[:, h * DH:(h + 1) * DH] for _, h in _pairs(b)], axis=0)
    return (_lead_slabs(q0_ref[...], b), _lead_slabs(k0_ref[...], b), _lead_slabs(p0_ref[:, 2 * WIDTH:3 * WIDTH], b), eg,
            _lead_slabs(o0_ref[...], b), _lead_slabs(p0_ref[:, 3 * WIDTH:4 * WIDTH], b))


def _scan_specs(b, ng, reverse):
    group = (lambda i: ng - 1 - i) if reverse else (lambda i: i)
    slab = lambda lane_block: pl.BlockSpec((b, SCAN_CHUNKS * CHUNK, WIDTH), lambda i: (0, group(i), lane_block))
    per_chunk = lambda *tail: pl.BlockSpec((b, SCAN_CHUNKS) + tail, lambda i: (0, group(i)) + (0,) * len(tail))
    const = lambda a: pl.BlockSpec(a.shape, lambda i: (0,) * a.ndim)
    return slab, per_chunk, const


def run_scans(parts, nc, name):
    n_in = [len(p["args"]) for p in parts]
    n_out = [len(p["out_shape"]) for p in parts]
    n_scr = [len(p["scratch_shapes"]) for p in parts]

    def body(*refs):
        ins, outs, scr = refs[:sum(n_in)], refs[sum(n_in):sum(n_in) + sum(n_out)], refs[sum(n_in) + sum(n_out):]
        for i, part in enumerate(parts):
            part["body"](*ins[sum(n_in[:i]):sum(n_in[:i + 1])], *outs[sum(n_out[:i]):sum(n_out[:i + 1])],
                         *scr[sum(n_scr[:i]):sum(n_scr[:i + 1])])

    flat = lambda key: [v for p in parts for v in p[key]]
    out = pl.pallas_call(body, grid=(nc,), name=name, in_specs=flat("in_specs"), out_specs=flat("out_specs"),
                         out_shape=flat("out_shape"), scratch_shapes=flat("scratch_shapes"),
                         compiler_params=_cparams("arbitrary"))(*flat("args"))
    return [out[sum(n_out[:i]):sum(n_out[:i + 1])] for i in range(len(parts))]


def hg_scan_fwd(p, p0, local, lead, nw):
    b, seq, _ = p.shape
    q_in, k_out, o_intra, eg = local
    slab, per_chunk, const = _scan_specs(b, seq // (SCAN_CHUNKS * CHUNK), False)

    def body(q_ref, k_ref, o_ref, v_ref, z_ref, eg_ref, q0_ref, k0_ref, o0_ref, p0_ref, eg0_ref, nw_ref, y_ref, ss_ref, st):
        @pl.when(pl.program_id(0) == 0)
        def _():
            st[...] = hg_scan(*_hg_lead_args(b, q0_ref, k0_ref, o0_ref, p0_ref, eg0_ref), nw_ref[...], jnp.zeros(st.shape, F32))[1]

        s = st[...]
        for k in range(SCAN_CHUNKS):
            _save_states(ss_ref, s, b, k)
            y, s = hg_scan(*_hg_scan_args(b, k, q_ref, k_ref, o_ref, v_ref, z_ref, eg_ref), nw_ref[...], s)
            _store_slabs(y_ref, y, b, k)
        st[...] = s

    return dict(
        body=body, args=(q_in, k_out, o_intra, p, p, eg, lead[0], lead[1], lead[2], p0, lead[3], nw),
        in_specs=[slab(0), slab(0), slab(0), slab(2), slab(3), per_chunk(1, WIDTH)] + [const(a) for a in lead[0:3]]
        + [const(p0), const(lead[3]), const(nw)],
        out_specs=[slab(0), per_chunk(WIDTH, DH)],
        out_shape=[_sds((b, seq, WIDTH), MXU_DTYPE), _sds((b, seq // CHUNK, WIDTH, DH))],
        scratch_shapes=[pltpu.VMEM((b * HEADS, DH, DH), F32)])


def hg_scan_bwd(p, p0, local, lead, nw, ssave, dy):
    b, seq, _ = p.shape
    ng = seq // (SCAN_CHUNKS * CHUNK)
    q_in, k_out, o_intra, eg = local
    slab, per_chunk, const = _scan_specs(b, ng, True)

    def body(q_ref, k_ref, o_ref, v_ref, z_ref, eg_ref, q0_ref, k0_ref, o0_ref, p0_ref, eg0_ref, nw_ref, ss_ref, dy_ref,
             dq_ref, dk_ref, do_ref, dv_ref, dz_ref, deg_ref, dq0_ref, dk0_ref, do0_ref, dv0_ref, dz0_ref, deg0_ref, dnw_ref,
             dst):
        i = pl.program_id(0)

        @pl.when(i == 0)
        def _():
            dst[...] = jnp.zeros_like(dst)
            dnw_ref[...] = jnp.zeros_like(dnw_ref)

        ds = dst[...]
        for k in reversed(range(SCAN_CHUNKS)):
            args = _hg_scan_args(b, k, q_ref, k_ref, o_ref, v_ref, z_ref, eg_ref)
            _, vjp = jax.vjp(hg_scan, *args, nw_ref[...], _load_states(ss_ref, b, k))
            dq, dk, dv, deg, do, dz, dnw, ds = vjp((_load_slabs(dy_ref, b, k), ds))
            dnw_ref[...] += dnw
            for ref, val in ((dq_ref, dq), (dk_ref, dk), (do_ref, do), (dv_ref, dv), (dz_ref, dz)):
                _store_slabs(ref, val, b, k)
            for j in range(b):
                deg_ref[j, k] = _rows(deg, j)
        dst[...] = ds

        @pl.when(i == ng - 1)
        def _():
            args = _hg_lead_args(b, q0_ref, k0_ref, o0_ref, p0_ref, eg0_ref)
            _, vjp = jax.vjp(hg_scan, *args, nw_ref[...], jnp.zeros(dst.shape, F32))
            dq, dk, dv, deg, do, dz, dnw, _ = vjp((jnp.zeros((b * HEADS, CHUNK, DH), F32), ds))
            dnw_ref[...] += dnw
            for ref, val in ((dq0_ref, dq), (dk0_ref, dk), (do0_ref, do), (dv0_ref, dv), (dz0_ref, dz), (deg0_ref, deg)):
                ref[...] = _sum_rows(val, b)

    lead_out = [const(a) for a in lead[0:3]] + [const(lead[0]), const(lead[0]), const(lead[3])]
    return dict(
        body=body, args=(q_in, k_out, o_intra, p, p, eg, lead[0], lead[1], lead[2], p0, lead[3], nw, ssave, dy),
        in_specs=[slab(0), slab(0), slab(0), slab(2), slab(3), per_chunk(1, WIDTH)] + [const(a) for a in lead[0:3]]
        + [const(p0), const(lead[3]), const(nw), per_chunk(WIDTH, DH), slab(0)],
        out_specs=[slab(0)] * 5 + [per_chunk(1, WIDTH)] + lead_out + [const(nw)],
        out_shape=[_sds((b, seq, WIDTH))] * 5 + [_sds(eg.shape)] + [_sds((CHUNK, WIDTH))] * 5 + [_sds((1, WIDTH)), _sds(nw.shape)],
        scratch_shapes=[pltpu.VMEM((b * HEADS, DH, DH), F32)])


def _hg_local_vjp(p, logits, dq, dk, do, degs, dv, dz):
    _, vjp = jax.vjp(hg_local, p, logits)
    dp, dlg = vjp((dq, dk, do, degs))
    return dp + jnp.concatenate([jnp.zeros((p.shape[0], 2 * WIDTH), F32), dv, dz], axis=1), dlg


def hg_local_bwd(p, p0, logits, cot, cot0):
    b, seq, _ = p.shape
    rows = LOCAL_CHUNKS * CHUNK

    def body(p_ref, p0_ref, lg_ref, dq_ref, dk_ref, do_ref, dv_ref, dz_ref, deg_ref, dq0_ref, dk0_ref, do0_ref, dv0_ref, dz0_ref,
             deg0_ref, dp_ref, dp0_ref, dlg_ref):
        @pl.when((pl.program_id(0) == 0) & (pl.program_id(1) == 0))
        def _():
            dp0, dlg_ref[...] = _hg_local_vjp(p0_ref[...], lg_ref[...], dq0_ref[...], dk0_ref[...], do0_ref[...],
                                              (deg0_ref[...],), dv0_ref[...], dz0_ref[...])
            dp0_ref[...] = dp0.astype(MXU_DTYPE)

        degs = tuple(deg_ref[c] for c in range(LOCAL_CHUNKS))
        dp, dlg = _hg_local_vjp(p_ref[...], lg_ref[...], dq_ref[...], dk_ref[...], do_ref[...], degs, dv_ref[...], dz_ref[...])
        dp_ref[...] = dp.astype(MXU_DTYPE)
        dlg_ref[...] += dlg

    slab = pl.BlockSpec((None, rows, WIDTH), lambda s, g: (s, g, 0))
    wide = pl.BlockSpec((None, rows, 4 * WIDTH), lambda s, g: (s, g, 0))
    const = lambda a: pl.BlockSpec(a.shape, lambda s, g: (0, 0))
    return pl.pallas_call(
        body, grid=(b, seq // rows), name="hgrn2_local_bwd",
        in_specs=[wide, const(p0), const(logits), slab, slab, slab, slab, slab,
                  pl.BlockSpec((None, LOCAL_CHUNKS, 1, WIDTH), lambda s, g: (s, g, 0, 0))] + [const(a) for a in cot0],
        out_specs=[wide, const(p0), const(logits)],
        out_shape=[_sds(p.shape, MXU_DTYPE), _sds(p0.shape, MXU_DTYPE), _sds(logits.shape)],
        compiler_params=_cparams("arbitrary", "arbitrary"),
    )(p, p0, logits, *cot, *cot0)


def _halo_block(g):
    return jnp.maximum((LOCAL_CHUNKS * CHUNK // HALO) * g - 1, 0)


def _gd_window(g, p_ref, halo_ref, p0_ref):
    halo = jnp.where(g == 0, p0_ref[CHUNK - HALO:CHUNK, 0:QKV], halo_ref[...])
    return jnp.concatenate([halo, p_ref[:, 0:QKV]], axis=0)


def _lead_window(p0_ref):
    return jnp.concatenate([jnp.zeros((HALO, QKV), F32), p0_ref[:, 0:QKV]], axis=0)


def gd_local_fwd(p, p0, ab, ab0, cw, alog, dtb):
    b, seq, _ = p.shape
    rows = LOCAL_CHUNKS * CHUNK
    nreal = seq // CHUNK

    def body(p_ref, halo_ref, p0_ref, ab_ref, ab0_ref, cw_ref, al_ref, dt_ref, u_ref, w_ref, qe_ref, ke_ref, qk_ref, ea_ref,
             inv_ref, u0_ref, w0_ref, qe0_ref, ke0_ref, qk0_ref, ea0_ref, inv0_ref):
        @pl.when((pl.program_id(0) == 0) & (pl.program_id(1) == 0))
        def _():
            (u0_ref[...], ww, qe, ke, qk0_ref[...], (ea0_ref[...],)), inv0_ref[...] = gd_local(
                _lead_window(p0_ref), ab0_ref[...], cw_ref[...], al_ref[...], dt_ref[...], inverse=_tri_y_impl)
            w0_ref[...], qe0_ref[...], ke0_ref[...] = ww.astype(MXU_DTYPE), qe.astype(MXU_DTYPE), ke.astype(MXU_DTYPE)

        (uu, ww, qe, ke, qk, eas), inv = gd_local(_gd_window(pl.program_id(1), p_ref, halo_ref, p0_ref), ab_ref[...],
                                                  cw_ref[...], al_ref[...], dt_ref[...], inverse=_tri_y_impl)
        u_ref[...], w_ref[...], qe_ref[...], ke_ref[...] = uu, ww.astype(MXU_DTYPE), qe.astype(MXU_DTYPE), ke.astype(MXU_DTYPE)
        for c in range(LOCAL_CHUNKS):
            qk_ref[c] = qk[c * HEADS * CHUNK:(c + 1) * HEADS * CHUNK]
            inv_ref[c] = inv[c * HEADS * CHUNK:(c + 1) * HEADS * CHUNK]
            ea_ref[c] = eas[c]

    const = lambda shape: pl.BlockSpec(shape, lambda s, g: (0, 0))
    slab = pl.BlockSpec((None, rows, WIDTH), lambda s, g: (s, g, 0))
    mats = pl.BlockSpec((None, LOCAL_CHUNKS, HEADS * CHUNK, CHUNK), lambda s, g: (s, g, 0, 0))
    lead_out = [_sds((CHUNK, WIDTH))] + [_sds((CHUNK, WIDTH), MXU_DTYPE)] * 3 + [_sds((HEADS * CHUNK, CHUNK)), _sds((1, AB_PAD)),
                                                                                _sds((HEADS * CHUNK, CHUNK))]
    out = pl.pallas_call(
        body, grid=(b, seq // rows), name="gdn_local",
        in_specs=[pl.BlockSpec((None, rows, 4 * WIDTH), lambda s, g: (s, g, 0)),
                  pl.BlockSpec((None, HALO, QKV), lambda s, g: (s, _halo_block(g), 0)), const(p0.shape),
                  pl.BlockSpec((None, rows, AB_PAD), lambda s, g: (s, g, 0)), const(ab0.shape), const(cw.shape),
                  const(alog.shape), const(dtb.shape)],
        out_specs=[slab] * 4 + [mats, pl.BlockSpec((None, LOCAL_CHUNKS, 1, AB_PAD), lambda s, g: (s, g, 0, 0)), mats]
        + [const(s.shape) for s in lead_out],
        out_shape=[_sds((b, seq, WIDTH))] + [_sds((b, seq, WIDTH), MXU_DTYPE)] * 3
        + [_sds((b, nreal, HEADS * CHUNK, CHUNK)), _sds((b, nreal, 1, AB_PAD)), _sds((b, nreal, HEADS * CHUNK, CHUNK))] + lead_out,
        compiler_params=_cparams("arbitrary", "arbitrary"),
    )(p, p, p0, ab, ab0, cw, alog, dtb)
    return out[0:6], out[6], out[7:13], out[13]


def _gd_scan_args(b, k, u_ref, w_ref, qe_ref, ke_ref, qk_ref, ea_ref, z_ref):
    qk = jnp.stack([qk_ref[i, k, h * CHUNK:(h + 1) * CHUNK, :] for i, h in _pairs(b)], axis=0)
    ea = jnp.stack([ea_ref[i, k, :, h:h + 1] for i, h in _pairs(b)], axis=0)
    return (_load_slabs(u_ref, b, k), _load_slabs(w_ref, b, k), _load_slabs(qe_ref, b, k), _load_slabs(ke_ref, b, k), qk, ea,
            _load_slabs(z_ref, b, k))


def _gd_lead_args(b, u0_ref, w0_ref, qe0_ref, ke0_ref, qk0_ref, ea0_ref, p0_ref):
    qk = jnp.stack([qk0_ref[h * CHUNK:(h + 1) * CHUNK, :] for _, h in _pairs(b)], axis=0)
    ea = jnp.stack([ea0_ref[:, h:h + 1] for _, h in _pairs(b)], axis=0)
    return (_lead_slabs(u0_ref[...], b), _lead_slabs(w0_ref[...], b), _lead_slabs(qe0_ref[...], b), _lead_slabs(ke0_ref[...], b),
            qk, ea, _lead_slabs(p0_ref[:, QKV:QKV + WIDTH], b))


def gd_scan_fwd(p, p0, local, lead, nw):
    b, seq, _ = p.shape
    slab, per_chunk, const = _scan_specs(b, seq // (SCAN_CHUNKS * CHUNK), False)

    def body(u_ref, w_ref, qe_ref, ke_ref, qk_ref, ea_ref, z_ref, u0_ref, w0_ref, qe0_ref, ke0_ref, qk0_ref, ea0_ref, p0_ref,
             nw_ref, y_ref, ss_ref, st):
        @pl.when(pl.program_id(0) == 0)
        def _():
            lead_args = _gd_lead_args(b, u0_ref, w0_ref, qe0_ref, ke0_ref, qk0_ref, ea0_ref, p0_ref)
            st[...] = gd_scan(*lead_args, nw_ref[...], jnp.zeros(st.shape, F32))[1]

        s = st[...]
        for k in range(SCAN_CHUNKS):
            _save_states(ss_ref, s, b, k)
            y, s = gd_scan(*_gd_scan_args(b, k, u_ref, w_ref, qe_ref, ke_ref, qk_ref, ea_ref, z_ref), nw_ref[...], s)
            _store_slabs(y_ref, y, b, k)
        st[...] = s

    return dict(
        body=body, args=(*local, p, *lead, p0, nw),
        in_specs=[slab(0)] * 4 + [per_chunk(HEADS * CHUNK, CHUNK), per_chunk(1, AB_PAD), slab(3)] + [const(a) for a in lead]
        + [const(p0), const(nw)],
        out_specs=[slab(0), per_chunk(WIDTH, DH)],
        out_shape=[_sds((b, seq, WIDTH), MXU_DTYPE), _sds((b, seq // CHUNK, WIDTH, DH))],
        scratch_shapes=[pltpu.VMEM((b * HEADS, DH, DH), F32)])


def gd_scan_bwd(p, p0, local, lead, nw, ssave, dy):
    b, seq, _ = p.shape
    ng = seq // (SCAN_CHUNKS * CHUNK)
    slab, per_chunk, const = _scan_specs(b, ng, True)

    def body(u_ref, w_ref, qe_ref, ke_ref, qk_ref, ea_ref, z_ref, u0_ref, w0_ref, qe0_ref, ke0_ref, qk0_ref, ea0_ref, p0_ref,
             nw_ref, ss_ref, dy_ref, du_ref, dw_ref, dqe_ref, dke_ref, dqk_ref, dea_ref, dz_ref, du0_ref, dw0_ref, dqe0_ref,
             dke0_ref, dqk0_ref, dea0_ref, dz0_ref, dnw_ref, dst):
        i = pl.program_id(0)
        lane = lax.broadcasted_iota(jnp.int32, (1, AB_PAD), 1)

        def gate_rows(dea, j):
            return sum(jnp.where(lane == h, dea[j * HEADS + h], 0.0) for h in range(HEADS))

        def matrix_rows(dqk, j):
            return jnp.concatenate([dqk[j * HEADS + h] for h in range(HEADS)], axis=0)

        @pl.when(i == 0)
        def _():
            dst[...] = jnp.zeros_like(dst)
            dnw_ref[...] = jnp.zeros_like(dnw_ref)

        ds = dst[...]
        for k in reversed(range(SCAN_CHUNKS)):
            args = _gd_scan_args(b, k, u_ref, w_ref, qe_ref, ke_ref, qk_ref, ea_ref, z_ref)
            _, vjp = jax.vjp(gd_scan, *args, nw_ref[...], _load_states(ss_ref, b, k))
            du, dw, dqe, dke, dqk, dea, dz, dnw, ds = vjp((_load_slabs(dy_ref, b, k), ds))
            dnw_ref[...] += dnw
            for ref, val in ((du_ref, du), (dw_ref, dw), (dqe_ref, dqe), (dke_ref, dke), (dz_ref, dz)):
                _store_slabs(ref, val, b, k)
            for j in range(b):
                dqk_ref[j, k] = matrix_rows(dqk, j)
                dea_ref[j, k] = gate_rows(dea, j)
        dst[...] = ds

        @pl.when(i == ng - 1)
        def _():
            args = _gd_lead_args(b, u0_ref, w0_ref, qe0_ref, ke0_ref, qk0_ref, ea0_ref, p0_ref)
            _, vjp = jax.vjp(gd_scan, *args, nw_ref[...], jnp.zeros(dst.shape, F32))
            du, dw, dqe, dke, dqk, dea, dz, dnw, _ = vjp((jnp.zeros((b * HEADS, CHUNK, DH), F32), ds))
            dnw_ref[...] += dnw
            for ref, val in ((du0_ref, du), (dw0_ref, dw), (dqe0_ref, dqe), (dke0_ref, dke), (dz0_ref, dz)):
                ref[...] = _sum_rows(val, b)
            dqk0_ref[...] = sum((matrix_rows(dqk, j) for j in range(1, b)), matrix_rows(dqk, 0))
            dea0_ref[...] = sum((gate_rows(dea, j) for j in range(1, b)), gate_rows(dea, 0))

    uu, ww, qe, ke, qk, ea = local
    return dict(
        body=body, args=(*local, p, *lead, p0, nw, ssave, dy),
        in_specs=[slab(0)] * 4 + [per_chunk(HEADS * CHUNK, CHUNK), per_chunk(1, AB_PAD), slab(3)] + [const(a) for a in lead]
        + [const(p0), const(nw), per_chunk(WIDTH, DH), slab(0)],
        out_specs=[slab(0)] * 4 + [per_chunk(HEADS * CHUNK, CHUNK), per_chunk(1, AB_PAD), slab(0)] + [const(a) for a in lead]
        + [const(lead[0]), const(nw)],
        out_shape=[_sds((b, seq, WIDTH))] * 4 + [_sds(qk.shape), _sds(ea.shape), _sds((b, seq, WIDTH))]
        + [_sds(a.shape) for a in lead] + [_sds(lead[0].shape), _sds(nw.shape)],
        scratch_shapes=[pltpu.VMEM((b * HEADS, DH, DH), F32)])


def _gd_local_vjp(inv_rows, xx, ab, cw, alog, dtb):
    nb = ab.shape[0] // CHUNK
    inv = jnp.stack([inv_rows[g * CHUNK:(g + 1) * CHUNK] for g in range(nb * HEADS)], axis=0)
    _, vjp, _ = jax.vjp(lambda *a: gd_local(*a, inverse=_saved_inverse(inv)), xx, ab, cw, alog, dtb, has_aux=True)
    return vjp


def gd_local_bwd(p, p0, ab, ab0, cw, alog, dtb, inv, inv0, cot, dz, cot0, dz0):
    b, seq, _ = p.shape
    rows = LOCAL_CHUNKS * CHUNK
    ng = seq // rows
    du, dw, dqe, dke, dqk, dea = cot

    def body(p_ref, halo_ref, p0_ref, ab_ref, ab0_ref, cw_ref, al_ref, dt_ref, inv_ref, inv0_ref, du_ref, dw_ref, dqe_ref,
             dke_ref, dqk_ref, dea_ref, dz_ref, du0_ref, dw0_ref, dqe0_ref, dke0_ref, dqk0_ref, dea0_ref, dz0_ref,
             dp_ref, dab_ref, dp0_ref, dab0_ref, dcw_ref, dal_ref, ddt_ref, dhalo, dtail):
        s, i = pl.program_id(0), pl.program_id(1)
        g = ng - 1 - i

        @pl.when(i == 0)
        def _():
            dhalo[...] = jnp.zeros_like(dhalo)

        @pl.when((s == 0) & (i == 0))
        def _():
            dtail[...] = jnp.zeros_like(dtail)
            dcw_ref[...] = jnp.zeros_like(dcw_ref)
            dal_ref[...] = jnp.zeros_like(dal_ref)
            ddt_ref[...] = jnp.zeros_like(ddt_ref)

        def finish(dxx, dab, dcw, dal, ddt, before, n, dz_val, dp_out, dab_out):
            dqkv = dxx[HALO:HALO + n] + jnp.concatenate([jnp.zeros((n - HALO, QKV), F32), before], axis=0)
            dp_out[...] = jnp.concatenate([dqkv, dz_val], axis=1).astype(MXU_DTYPE)
            dab_out[...] = dab.astype(MXU_DTYPE)
            dcw_ref[...] += dcw
            dal_ref[...] += dal
            ddt_ref[...] += ddt

        inv_rows = jnp.concatenate([inv_ref[c] for c in range(LOCAL_CHUNKS)], axis=0)
        vjp = _gd_local_vjp(inv_rows, _gd_window(g, p_ref, halo_ref, p0_ref), ab_ref[...], cw_ref[...], al_ref[...], dt_ref[...])
        dqk_all = jnp.concatenate([dqk_ref[c] for c in range(LOCAL_CHUNKS)], axis=0)
        deas = tuple(dea_ref[c] for c in range(LOCAL_CHUNKS))
        grads = vjp((du_ref[...], dw_ref[...], dqe_ref[...], dke_ref[...], dqk_all, deas))
        finish(*grads, dhalo[...], rows, dz_ref[...], dp_ref, dab_ref)
        dhalo[...] = grads[0][0:HALO]

        @pl.when(g == 0)
        def _():
            dtail[...] += grads[0][0:HALO]

        @pl.when((s == b - 1) & (g == 0))
        def _():
            vjp0 = _gd_local_vjp(inv0_ref[...], _lead_window(p0_ref), ab0_ref[...], cw_ref[...], al_ref[...], dt_ref[...])
            grads0 = vjp0((du0_ref[...], dw0_ref[...], dqe0_ref[...], dke0_ref[...], dqk0_ref[...], (dea0_ref[...],)))
            finish(*grads0, dtail[...], CHUNK, dz0_ref[...], dp0_ref, dab0_ref)

    rg = lambda i: ng - 1 - i
    const = lambda a: pl.BlockSpec(a.shape, lambda s, i: (0, 0))
    slab = pl.BlockSpec((None, rows, WIDTH), lambda s, i: (s, rg(i), 0))
    wide = pl.BlockSpec((None, rows, 4 * WIDTH), lambda s, i: (s, rg(i), 0))
    gates = pl.BlockSpec((None, rows, AB_PAD), lambda s, i: (s, rg(i), 0))
    mats = pl.BlockSpec((None, LOCAL_CHUNKS, HEADS * CHUNK, CHUNK), lambda s, i: (s, rg(i), 0, 0))
    return pl.pallas_call(
        body, grid=(b, ng), name="gdn_local_bwd",
        in_specs=[wide, pl.BlockSpec((None, HALO, QKV), lambda s, i: (s, _halo_block(rg(i)), 0)), const(p0), gates, const(ab0),
                  const(cw), const(alog), const(dtb), mats, const(inv0), slab, slab, slab, slab, mats,
                  pl.BlockSpec((None, LOCAL_CHUNKS, 1, AB_PAD), lambda s, i: (s, rg(i), 0, 0)), slab]
        + [const(a) for a in cot0] + [const(dz0)],
        out_specs=[wide, gates, const(p0), const(ab0), const(cw), const(alog), const(dtb)],
        out_shape=[_sds(p.shape, MXU_DTYPE), _sds(ab.shape, MXU_DTYPE), _sds(p0.shape, MXU_DTYPE), _sds(ab0.shape, MXU_DTYPE),
                   _sds(cw.shape), _sds(alog.shape), _sds(dtb.shape)],
        scratch_shapes=[pltpu.VMEM((HALO, QKV), F32), pltpu.VMEM((HALO, QKV), F32)],
        compiler_params=_cparams("arbitrary", "arbitrary"),
    )(p, p, p0, ab, ab0, cw, alog, dtb, inv, inv0, du, dw, dqe, dke, dqk, dea, dz, *cot0, dz0)


def _position():
    return lax.axis_index("x"), lax.axis_index("y"), lax.axis_index("c")


EXCHANGE_COPIES = 10


def _exchange_blocks(bufs, send_sems, recv_sems):
    x, y, c = _position()
    here, x_nbr, y_nbr, diag = (x, y), (1 - x, y), (x, 1 - y), (1 - x, 1 - y)
    sibling = (x, y, 1 - c)
    me = (x, y, c)
    n = range(len(bufs))

    def rows(a, chip, core, half=None):
        block = bufs[a].at[4 * chip[0] + 2 * chip[1] + core]
        if half is None:
            return block
        total = bufs[a].shape[1]
        tile = 8 * (4 // jnp.dtype(bufs[a].dtype).itemsize)
        split = total // 2 // tile * tile
        return block.at[pl.ds(0, split)] if half == 0 else block.at[pl.ds(split, total - split)]

    def copy(a, k, region, to):
        return pltpu.make_async_remote_copy(src_ref=region, dst_ref=region, send_sem=send_sems.at[a * EXCHANGE_COPIES + k],
                                            recv_sem=recv_sems.at[a * EXCHANGE_COPIES + k], device_id=to, device_id_type=MESH)

    sent = [copy(a, 0, rows(a, here, c), sibling) for a in n]
    sent += [cp for a in n for cp in (copy(a, 1, rows(a, here, c, 0), (*x_nbr, c)), copy(a, 4, rows(a, here, c, 1), (*y_nbr, c)))]
    sent += [cp for a in n for cp in (copy(a, 2, rows(a, here, c, 1), (*x_nbr, c)), copy(a, 3, rows(a, here, c, 0), (*y_nbr, c)))]
    for cp in sent:
        cp.start()

    def after(arrivals, a, k, region, to):
        for cp in arrivals:
            cp.wait_recv()
        sent.append(copy(a, k, region, to))
        sent[-1].start()

    for a in n:
        after([copy(a, 1, rows(a, x_nbr, c, 0), me)], a, 5, rows(a, x_nbr, c, 0), (*y_nbr, c))
        after([copy(a, 4, rows(a, y_nbr, c, 1), me)], a, 6, rows(a, y_nbr, c, 1), (*x_nbr, c))
    for a in n:
        after([copy(a, 2, rows(a, x_nbr, c, 1), me)], a, 7, rows(a, x_nbr, c), sibling)
        after([copy(a, 3, rows(a, y_nbr, c, 0), me)], a, 8, rows(a, y_nbr, c), sibling)
    for a in n:
        after([copy(a, 5, rows(a, diag, c, 0), me), copy(a, 6, rows(a, diag, c, 1), me)], a, 9, rows(a, diag, c), sibling)
    for a in n:
        copy(a, 0, rows(a, here, 1 - c), me).wait_recv()
        for k, chip in ((7, x_nbr), (8, y_nbr), (9, diag)):
            copy(a, k, rows(a, chip, 1 - c), me).wait_recv()
    for cp in sent:
        cp.wait_send()


def _exchange_sems(n_bufs):
    return [pltpu.SemaphoreType.DMA((n_bufs * EXCHANGE_COPIES,)), pltpu.SemaphoreType.DMA((n_bufs * EXCHANGE_COPIES,))]


def gather_weights(w_in_t, w_out, small, pad_rows):
    rows, _, cols = w_in_t.shape
    buf_rows = -(-rows // ROW_TILE_BF16) * ROW_TILE_BF16

    def body(wi_ref, wo_ref, sm_ref, wi_out, wo_out, sm_out, wi_buf, send_sems, recv_sems):
        x, y, c = _position()
        me = 4 * x + 2 * y + c
        wi_buf[me, pl.ds(0, rows), :] = wi_ref[:, 0, :].astype(MXU_DTYPE)
        wi_buf[me, pl.ds(rows, buf_rows - rows), :] = jnp.zeros((buf_rows - rows, cols), MXU_DTYPE)
        wo_out[me] = wo_ref[...].astype(MXU_DTYPE)
        sm_out[me] = sm_ref[...]
        _exchange_blocks([wi_buf, wo_out, sm_out], send_sems, recv_sems)
        for d in range(N_DEV):
            wi_out[pl.ds(d * rows, rows), :] = wi_buf[d, pl.ds(0, rows), :]
        wi_out[pl.ds(N_DEV * rows, pad_rows), :] = jnp.zeros((pad_rows, cols), MXU_DTYPE)

    return pl.pallas_call(
        body, name="gather_weights", in_specs=[VMEM_SPEC] * 3, out_specs=[VMEM_SPEC] * 3,
        out_shape=[jax.ShapeDtypeStruct((N_DEV * rows + pad_rows, cols), MXU_DTYPE),
                   jax.ShapeDtypeStruct((N_DEV,) + w_out.shape, MXU_DTYPE), jax.ShapeDtypeStruct((N_DEV,) + small.shape, F32)],
        scratch_shapes=[pltpu.VMEM((N_DEV, buf_rows, cols), MXU_DTYPE)] + _exchange_sems(3),
        compiler_params=pltpu.CompilerParams(vmem_limit_bytes=VMEM_LIMIT))(w_in_t, w_out, small)


HOPS = 6


def reduce_gradients(tensors, small, name):
    n_t = len(tensors)
    arrays = [a for parts, _ in tensors for a, _ in parts]
    first_array = [sum(len(parts) for parts, _ in tensors[:t]) for t in range(n_t)]

    def pieces(t, j):
        parts, block_rows = tensors[t]
        out, base = [], 0
        for pi, (_, valid) in enumerate(parts):
            lo, hi = max(j * block_rows, base), min((j + 1) * block_rows, base + valid)
            if lo < hi:
                out.append((first_array[t] + pi, lo - base, lo - j * block_rows, hi - lo))
            base += valid
        return out

    def body(*refs):
        n_a = len(arrays)
        in_refs, small_ref = refs[:n_a], refs[n_a]
        out_refs, small_sum = refs[n_a + 1:n_a + 1 + n_t], refs[n_a + 1 + n_t]
        bufs, small_buf = refs[n_a + 2 + n_t:n_a + 2 + 5 * n_t], refs[n_a + 2 + 5 * n_t]
        s1_sems, r1_sems, s2_sems, r2_sems, small_send, small_recv = refs[n_a + 3 + 5 * n_t:]
        x, y, c = _position()
        chip = 2 * x + y

        def put(t, dst, j, add=None):
            for ai, src_row, dst_row, size in pieces(t, j):
                v = in_refs[ai][pl.ds(src_row, size), :]
                if add is not None:
                    v = v + add[pl.ds(dst_row, size), :].astype(F32)
                dst[pl.ds(dst_row, size), :] = v.astype(dst.dtype)

        def swap(t, k):
            send1, recv1 = bufs[4 * t], bufs[4 * t + 1]
            return pltpu.make_async_remote_copy(src_ref=send1.at[k], dst_ref=recv1.at[k], send_sem=s1_sems.at[4 * t + k],
                                                recv_sem=r1_sems.at[4 * t + k], device_id=(x, y, 1 - c), device_id_type=MESH)

        to_x, to_y, to_diag = 2 * (1 - x) + y, 2 * x + (1 - y), 2 * (1 - x) + (1 - y)
        x_dev, y_dev = (1 - x, y, c), (x, 1 - y, c)

        def half(ref, h):
            total = ref.shape[0]
            split = total // 2 // ROW_TILE_BF16 * ROW_TILE_BF16
            return ref.at[pl.ds(0, split)] if h == 0 else ref.at[pl.ds(split, total - split)]

        def hop(t, copy_id, src, dst, to):
            return pltpu.make_async_remote_copy(src_ref=src, dst_ref=dst, send_sem=s2_sems.at[HOPS * t + copy_id],
                                                recv_sem=r2_sems.at[HOPS * t + copy_id], device_id=to, device_id_type=MESH)

        def hops(t):
            send2, landing = bufs[4 * t + 2], bufs[4 * t + 3]
            return [hop(t, 0, half(send2.at[to_diag], 0), half(landing.at[0], 0), x_dev),
                    hop(t, 1, half(send2.at[to_diag], 1), half(landing.at[0], 1), y_dev),
                    hop(t, 2, half(send2.at[to_x], 0), half(landing.at[1], 0), x_dev),
                    hop(t, 3, half(send2.at[to_y], 1), half(landing.at[2], 1), y_dev),
                    hop(t, 4, half(send2.at[to_x], 1), half(landing.at[1], 1), x_dev),
                    hop(t, 5, half(send2.at[to_y], 0), half(landing.at[2], 0), y_dev)]

        def add_relay(t, slot, h):
            dst, src = half(bufs[4 * t + 2].at[slot], h), half(bufs[4 * t + 3].at[0], h)
            dst[...] = (dst[...].astype(F32) + src[...].astype(F32)).astype(dst.dtype)

        for t in range(n_t):
            send2 = bufs[4 * t + 2]
            pad = send2.shape[1] - tensors[t][1]
            if pad:
                send2[:, pl.ds(tensors[t][1], pad), :] = jnp.zeros((4, pad, send2.shape[2]), send2.dtype)
            for j in range(N_DEV):
                @pl.when((j & 1) != c)
                def _():
                    put(t, bufs[4 * t].at[j >> 1], j)
            for k in range(4):
                swap(t, k).start()

        small_buf[4 * x + 2 * y + c] = small_ref[...]
        _exchange_blocks([small_buf], small_send, small_recv)
        total = small_buf[0]
        for d in range(1, N_DEV):
            total = total + small_buf[d]
        small_sum[...] = total

        for t in range(n_t):
            recv1 = bufs[4 * t + 1]
            for k in range(4):
                swap(t, k).wait_recv()
                for j in (2 * k, 2 * k + 1):
                    @pl.when(((j & 1) == c) & (k != chip))
                    def _():
                        put(t, bufs[4 * t + 2].at[k], j, add=recv1.at[k])

                    @pl.when(((j & 1) == c) & (k == chip))
                    def _():
                        put(t, out_refs[t], j, add=recv1.at[k])
            for cp in hops(t)[0:4]:
                cp.start()

        for t in range(n_t):
            cps = hops(t)
            cps[0].wait_recv()
            add_relay(t, to_y, 0)
            cps[5].start()
            cps[1].wait_recv()
            add_relay(t, to_x, 1)
            cps[4].start()

        for t in range(n_t):
            cps, rows = hops(t), tensors[t][1]
            for first, second, slot in ((cps[2], cps[4], 1), (cps[3], cps[5], 2)):
                first.wait_recv()
                second.wait_recv()
                out_refs[t][...] += bufs[4 * t + 3][slot, pl.ds(0, rows), :].astype(F32)

        for t in range(n_t):
            for cp in hops(t):
                cp.wait_send()
            for k in range(4):
                swap(t, k).wait_send()

    scratch, out_shape = [], []
    for parts, block_rows in tensors:
        cols = parts[0][0].shape[1]
        tiled_rows = -(-block_rows // ROW_TILE_BF16) * ROW_TILE_BF16
        scratch += [pltpu.VMEM((4, block_rows, cols), MXU_DTYPE)] * 2
        scratch += [pltpu.VMEM((4, tiled_rows, cols), MXU_DTYPE), pltpu.VMEM((3, tiled_rows, cols), MXU_DTYPE)]
        out_shape.append(jax.ShapeDtypeStruct((block_rows, cols), F32))
    out_shape.append(jax.ShapeDtypeStruct(small.shape, F32))
    scratch += [pltpu.VMEM((N_DEV,) + small.shape, F32)] + [pltpu.SemaphoreType.DMA((4 * n_t,))] * 2
    scratch += [pltpu.SemaphoreType.DMA((HOPS * n_t,))] * 2 + _exchange_sems(1)
    return pl.pallas_call(
        body, name=name, in_specs=[VMEM_SPEC] * (len(arrays) + 1), out_specs=[VMEM_SPEC] * (n_t + 1), out_shape=out_shape,
        scratch_shapes=scratch, compiler_params=pltpu.CompilerParams(vmem_limit_bytes=VMEM_LIMIT),
    )(*arrays, small)


def _adamw_step(w, g, m, v):
    mn = ADAM_B1 * m + (1.0 - ADAM_B1) * g
    vn = ADAM_B2 * v + (1.0 - ADAM_B2) * jnp.square(g)
    m_hat = mn / (1.0 - ADAM_B1 ** ADAM_STEP)
    v_hat = vn / (1.0 - ADAM_B2 ** ADAM_STEP)
    return -ADAM_LR * (m_hat / (jnp.sqrt(v_hat) + ADAM_EPS) + ADAM_WD * w), mn, vn


def adamw(w, g, m, v, name):
    rows, cols = w.shape
    tr = 256 if rows % 256 == 0 else rows

    def body(w_ref, g_ref, m_ref, v_ref, d_ref, nm_ref, nv_ref):
        d_ref[...], nm_ref[...], nv_ref[...] = _adamw_step(w_ref[...], g_ref[...], m_ref[...], v_ref[...])

    spec = pl.BlockSpec((tr, cols), lambda i: (i, 0))
    shape = jax.ShapeDtypeStruct((rows, cols), F32)
    return pl.pallas_call(body, grid=(rows // tr,), name=name, in_specs=[spec] * 4, out_specs=[spec] * 3,
                          out_shape=[shape] * 3, compiler_params=_cparams("arbitrary"))(w, g, m, v)


def adamw_w_in(w, g_t, m, v):
    def body(w_ref, g_ref, m_ref, v_ref, go_ref, d_ref, nm_ref, nv_ref):
        g = g_ref[...]
        go_ref[:, 0, :] = g
        d_ref[:, 0, :], nm_ref[:, 0, :], nv_ref[:, 0, :] = _adamw_step(w_ref[:, 0, :], g, m_ref[:, 0, :], v_ref[:, 0, :])

    return pl.pallas_call(body, name="adamw_w_in", in_specs=[VMEM_SPEC] * 4, out_specs=[VMEM_SPEC] * 4,
                          out_shape=[jax.ShapeDtypeStruct(w.shape, F32)] * 4,
                          compiler_params=pltpu.CompilerParams(vmem_limit_bytes=VMEM_LIMIT))(w, g_t, m, v)


def _pad_rows(a, rows=8):
    return jnp.pad(a, ((0, rows - a.shape[0]), (0, 0)))


def _pad_lanes(a, lanes=128):
    return jnp.pad(a, ((0, 0), (0, lanes - a.shape[1])))


def kernel(x, meta_tokens, norm_w, w_in, conv_w, hg_lb_logits, hg_norm_w, gdn_A_log, gdn_dt_bias, gdn_norm_w, w_out, final_norm_w, loss_target, m_meta_tokens, m_norm_w, m_w_in, m_conv_w, m_hg_lb_logits, m_hg_norm_w, m_gdn_A_log, m_gdn_dt_bias, m_gdn_norm_w, m_w_out, m_final_norm_w, v_meta_tokens, v_norm_w, v_w_in, v_conv_w, v_hg_lb_logits, v_hg_norm_w, v_gdn_A_log, v_gdn_dt_bias, v_gdn_norm_w, v_w_out, v_final_norm_w):
    b, seq, _ = x.shape
    n = b * seq
    dev = 4 * lax.axis_index("x") + 2 * lax.axis_index("y") + lax.axis_index("c")
    col_shard = IN_COLS // N_DEV

    small_w = jnp.concatenate([_pad_lanes(meta_tokens, 256), _pad_rows(_pad_lanes(conv_w[0], 256))], axis=0)
    w_t, w_out_g, small_g = gather_weights(jnp.transpose(w_in, (2, 0, 1)), w_out[0], small_w, AB_PAD - 2 * HEADS)
    meta_g = small_g[:, 0:N_META, 0:D_MODEL // N_DEV]
    conv_g = small_g[:, N_META:N_META + CONV_TAPS, 0:QKV // N_DEV]
    w_out_full = w_out_g.reshape(2 * WIDTH, D_MODEL)
    cw = jnp.transpose(conv_g, (1, 0, 2)).reshape(CONV_TAPS, QKV)
    meta = jnp.transpose(meta_g, (1, 0, 2)).reshape(N_META, D_MODEL)
    alog = _pad_lanes(gdn_A_log)
    dtb = _pad_lanes(gdn_dt_bias)
    fw = final_norm_w.reshape(1, D_MODEL)

    h0 = jnp.concatenate([jnp.zeros((CHUNK - N_META, D_MODEL), F32), meta], axis=0)
    x2 = x.reshape(n, D_MODEL)
    phg, pgd, pab, phg0, pgd0, pab0, u0 = in_proj(x2, h0, norm_w, w_t)
    phg3, pgd3, pab3 = phg.reshape(b, seq, 4 * WIDTH), pgd.reshape(b, seq, 4 * WIDTH), pab.reshape(b, seq, AB_PAD)
    nc = seq // (SCAN_CHUNKS * CHUNK)
    hg_loc, hg_lead = hg_local_fwd(phg3, phg0, hg_lb_logits)
    gd_loc, gd_inv, gd_lead, gd_inv0 = gd_local_fwd(pgd3, pgd0, pab3, pab0, cw, alog, dtb)
    (y_hg, s_hg), (y_gd, s_gd) = run_scans([hg_scan_fwd(phg3, phg0, hg_loc, hg_lead, hg_norm_w),
                                            gd_scan_fwd(pgd3, pgd0, gd_loc, gd_lead, gdn_norm_w)], nc, "scans")

    dh2, dy_hg, dy_gd, g_w_out, loss_part, g_fw = out_proj_loss(
        x2, loss_target.reshape(n, D_MODEL), y_hg.reshape(n, WIDTH), y_gd.reshape(n, WIDTH), w_out_full, fw)

    hb, gb = run_scans([hg_scan_bwd(phg3, phg0, hg_loc, hg_lead, hg_norm_w, s_hg, dy_hg.reshape(b, seq, WIDTH)),
                        gd_scan_bwd(pgd3, pgd0, gd_loc, gd_lead, gdn_norm_w, s_gd, dy_gd.reshape(b, seq, WIDTH))],
                       nc, "scans_bwd")
    dphg, dphg0, g_lb = hg_local_bwd(phg3, phg0, hg_lb_logits, hb[0:6], hb[6:12])
    g_hg_nw = hb[12]
    dpgd, dpab, dpgd0, dpab0, g_cw, g_alog, g_dtb = gd_local_bwd(pgd3, pgd0, pab3, pab0, cw, alog, dtb, gd_inv, gd_inv0,
                                                                 gb[0:6], gb[6], gb[7:13], gb[13])
    g_gd_nw = gb[14]
    dphg, dpgd, dpab = dphg.reshape(n, 4 * WIDTH), dpgd.reshape(n, 4 * WIDTH), dpab.reshape(n, AB_PAD)

    grad_x, dh0, g_nw, g_w_hg, g_w_gd, g_w_ab = in_proj_bwd(dphg, dpgd, dpab, w_t, x2, dh2, norm_w, h0, u0, dphg0, dpgd0, dpab0)

    small = jnp.concatenate([
        g_nw.reshape(8, 128), g_lb.reshape(8, 128), _pad_rows(g_hg_nw), _pad_rows(g_alog), _pad_rows(g_dtb), _pad_rows(g_gd_nw),
        g_fw.reshape(8, 128), g_cw.reshape(48, 128),
        dh0[CHUNK - N_META:CHUNK].reshape(128, 128), loss_part], axis=0)
    g_w_in_t, g_w_out, small = reduce_gradients(
        [([(g_w_hg, 4 * WIDTH), (g_w_gd, 4 * WIDTH), (g_w_ab, 2 * HEADS)], col_shard),
         ([(g_w_out, 2 * WIDTH)], (2 * WIDTH) // N_DEV)], small, "reduce_gradients")
    g_norm_w = small[0:8].reshape(1, D_MODEL)
    g_lb = small[8:16].reshape(2, WIDTH)
    g_hg_nw = small[16:17]
    g_alog = small[24:25, 0:HEADS]
    g_dtb = small[32:33, 0:HEADS]
    g_gd_nw = small[40:41]
    g_fw = small[48:56].reshape(1, D_MODEL)
    g_cw_full = small[56:104].reshape(CONV_TAPS, QKV)
    g_meta_full = small[104:232].reshape(N_META, D_MODEL)
    loss = small[232, 0]
    g_conv = lax.dynamic_slice_in_dim(g_cw_full, dev * (QKV // N_DEV), QKV // N_DEV, axis=1)
    g_meta = lax.dynamic_slice_in_dim(g_meta_full, dev * (D_MODEL // N_DEV), D_MODEL // N_DEV, axis=1)

    names = ["meta_tokens", "norm_w", "w_in", "conv_w", "hg_lb_logits", "hg_norm_w", "gdn_A_log", "gdn_dt_bias",
             "gdn_norm_w", "w_out", "final_norm_w"]
    weights = [meta_tokens, norm_w, w_in, conv_w, hg_lb_logits, hg_norm_w, gdn_A_log, gdn_dt_bias, gdn_norm_w, w_out,
               final_norm_w]
    moms = [m_meta_tokens, m_norm_w, m_w_in, m_conv_w, m_hg_lb_logits, m_hg_norm_w, m_gdn_A_log, m_gdn_dt_bias,
            m_gdn_norm_w, m_w_out, m_final_norm_w]
    vars_ = [v_meta_tokens, v_norm_w, v_w_in, v_conv_w, v_hg_lb_logits, v_hg_norm_w, v_gdn_A_log, v_gdn_dt_bias,
             v_gdn_norm_w, v_w_out, v_final_norm_w]
    grads2d = [g_meta, g_norm_w, g_w_in_t, g_conv, g_lb, g_hg_nw, g_alog, g_dtb, g_gd_nw, g_w_out, g_fw]
    grads, deltas, new_ms, new_vs = [], [], [], []
    for nm, w, g2, m, v in zip(names, weights, grads2d, moms, vars_):
        if nm == "w_in":
            to3, back = (lambda a: jnp.transpose(a, (2, 0, 1))), (lambda a: jnp.transpose(a, (1, 2, 0)))
            g2, d, nm_, nv_ = adamw_w_in(to3(w), g2, to3(m), to3(v))
        else:
            to2d, back = (lambda a, s=g2.shape: a.reshape(s)), (lambda a, s=w.shape: a.reshape(s))
            d, nm_, nv_ = adamw(to2d(w), g2, to2d(m), to2d(v), "adamw_" + nm)
        grads.append(back(g2))
        deltas.append(back(d))
        new_ms.append(back(nm_))
        new_vs.append(back(nv_))
    return (loss, grad_x.reshape(x.shape), *grads, *deltas, *new_ms, *new_vs)
```

```python
import jax
import jax.numpy as jnp
from jax import lax
from jax.experimental import pallas as pl
from jax.experimental.pallas import tpu as pltpu

F32 = jnp.float32
BF16 = jnp.bfloat16
MXU_DTYPE = BF16

D_MODEL = 1024
N_META = 16
CHUNK = 64
SUB = 16
ROW_TILE_BF16 = 16
HEADS = 4
DH = 128
WIDTH = HEADS * DH
QKV = 3 * WIDTH
CONV_TAPS = 4
HALO = 8
EPS = 1e-6
IN_COLS = 4 * WIDTH + 4 * WIDTH + 2 * HEADS
AB_PAD = 128
N_DEV = 8
LOCAL_CHUNKS = 4
SCAN_CHUNKS = 2
VMEM_LIMIT = 56 * 1024 * 1024
VMEM_LIMIT_LARGE = 60 * 1024 * 1024

ADAM_LR = 0.001
ADAM_B1 = 0.9
ADAM_B2 = 0.999
ADAM_EPS = 1e-08
ADAM_WD = 0.01
ADAM_STEP = 10

VMEM_SPEC = pl.BlockSpec(memory_space=pltpu.VMEM)
MESH = pl.DeviceIdType.MESH


def _mm_tn(a, b):
    return lax.dot_general(a.astype(MXU_DTYPE), b.astype(MXU_DTYPE), (((0,), (0,)), ((), ())), preferred_element_type=F32)


def _bmm(a, b):
    return lax.dot_general(a.astype(MXU_DTYPE), b.astype(MXU_DTYPE), (((2,), (1,)), ((0,), (0,))), preferred_element_type=F32)


def _bmm_nt(a, b):
    return lax.dot_general(a.astype(MXU_DTYPE), b.astype(MXU_DTYPE), (((2,), (2,)), ((0,), (0,))), preferred_element_type=F32)


def _bmm_tn(a, b):
    return lax.dot_general(a.astype(MXU_DTYPE), b.astype(MXU_DTYPE), (((1,), (1,)), ((0,), (0,))), preferred_element_type=F32)


def _iota2(n, m):
    return lax.broadcasted_iota(jnp.int32, (n, m), 0), lax.broadcasted_iota(jnp.int32, (n, m), 1)


def _silu(x):
    return x * jax.nn.sigmoid(x)


def _gated_norm(o, z, nw):
    return o * lax.rsqrt(jnp.mean(o * o, axis=-1, keepdims=True) + EPS) * nw * _silu(z)


def _heads(a, nb):
    return jnp.stack([a[c * CHUNK:(c + 1) * CHUNK, h * DH:(h + 1) * DH] for c in range(nb) for h in range(HEADS)], axis=0)


def _unheads(a3, nb):
    return jnp.concatenate(
        [jnp.concatenate([a3[c * HEADS + h] for h in range(HEADS)], axis=1) for c in range(nb)], axis=0)


def _split3(x):
    hi = x.astype(BF16)
    r1 = x - hi.astype(F32)
    mid = r1.astype(BF16)
    return hi, mid, (r1 - mid.astype(F32)).astype(BF16)


def _select_rows(pattern, n_out, x):
    def forward(v):
        r, c = _iota2(n_out, 3 * CHUNK)
        s = jnp.where(pattern(r, c & (CHUNK - 1)), 1.0, 0.0).astype(BF16)
        return jnp.dot(s, jnp.concatenate(_split3(v), axis=0), preferred_element_type=F32)

    def backward(_, d):
        r, c = _iota2(CHUNK, 2 * n_out)
        s_t = jnp.where(pattern(c - jnp.where(c >= n_out, n_out, 0), r), 1.0, 0.0).astype(BF16)
        hi = d.astype(BF16)
        return (jnp.dot(s_t, jnp.concatenate([hi, (d - hi.astype(F32)).astype(BF16)], axis=0), preferred_element_type=F32),)

    apply = jax.custom_vjp(forward)
    apply.defvjp(lambda v: (forward(v), None), backward)
    return apply(x)


def _cumsum_chunks(x, nb):
    return jnp.concatenate([_select_rows(lambda i, j: j <= i, CHUNK, x[c * CHUNK:(c + 1) * CHUNK]) for c in range(nb)], axis=0)


HG_LEVELS = 6


def _hg_sums(i, j):
    lvl, t = i >> HG_LEVELS, i & (CHUNK - 1)
    last = t
    for l in range(1, HG_LEVELS + 1):
        width = HG_LEVELS + 1 - l
        last = jnp.where(lvl == l, ((t >> width) << width) + (CHUNK >> l) - 1, last)
    return j <= last


def hg_local(p, logits):
    nb = p.shape[0] // CHUNK
    l0, l1 = logits[0:1], logits[1:2]
    mx = jnp.maximum(l0, l1)
    e0, e1 = jnp.exp(l0 - mx), jnp.exp(l1 - mx)
    lb = e0 / (e0 + e1)
    q = _silu(p[:, 0:WIDTH])
    f = lb + (1.0 - lb) * jax.nn.sigmoid(p[:, WIDTH:2 * WIDTH])
    k = 1.0 - f
    logf = jnp.log(f)
    sums = [_select_rows(_hg_sums, (HG_LEVELS + 1) * CHUNK, logf[c * CHUNK:(c + 1) * CHUNK]) for c in range(nb)]
    level = lambda l: _heads(jnp.concatenate([s[l * CHUNK:(l + 1) * CHUNK] for s in sums], axis=0), nb)
    q3, k3, v3, g3 = _heads(q, nb), _heads(k, nb), _heads(p[:, 2 * WIDTH:3 * WIDTH], nb), level(0)
    r, c = _iota2(CHUNK, CHUNK)
    row = lax.broadcasted_iota(jnp.int32, (CHUNK, DH), 0)
    a = jnp.where(r == c, _bmm_nt(q3, k3), 0.0)
    for l in range(1, HG_LEVELS + 1):
        sh = HG_LEVELS - l
        qk = jnp.where(((row >> sh) & 1) == 1, q3, k3) * jnp.exp(-jnp.abs(g3 - level(l)))
        pair = ((r >> (sh + 1)) == (c >> (sh + 1))) & (((r >> sh) & 1) == 1) & (((c >> sh) & 1) == 0)
        a = a + jnp.where(pair, _bmm_nt(qk, qk), 0.0)
    o = _bmm(a, v3)
    glast = g3[:, CHUNK - 1:CHUNK, :]
    egs = tuple(jnp.concatenate([jnp.exp(glast[c * HEADS + h]) for h in range(HEADS)], axis=1) for c in range(nb))
    return _unheads(q3 * jnp.exp(g3), nb), _unheads(k3 * jnp.exp(glast - g3), nb), _unheads(o, nb), egs


def hg_scan(q_in, k_out, v, eg, o_intra, z, nw, st):
    o = o_intra + _bmm_nt(q_in, st)
    return _gated_norm(o, z, nw), st * eg + _bmm_tn(v, k_out)


def _tri_y_impl(a):
    r, c = _iota2(CHUNK, CHUNK)
    same16 = (r // SUB) == (c // SUB)
    same32 = (r // (2 * SUB)) == (c // (2 * SUB))
    a0 = jnp.where(same16, a, 0.0)
    y = -a0
    pw = _bmm(a0, a0)
    for _ in range(2):
        y = y + pw + _bmm(y, pw)
        pw = _bmm(pw, pw)
    y = y + pw + _bmm(y, pw)
    for ak in (jnp.where(same32 & jnp.logical_not(same16), a, 0.0), jnp.where(same32, 0.0, a)):
        m = ak + _bmm(y, ak)
        y = y - (m + _bmm(m, y))
    return y


@jax.custom_vjp
def _tri_y(a):
    return _tri_y_impl(a)


def _tri_y_fwd(a):
    y = _tri_y_impl(a)
    return y, y


def _tri_y_bwd(y, dy):
    n = dy + _bmm_tn(y, dy)
    return (-(n + _bmm_nt(n, y)),)


_tri_y.defvjp(_tri_y_fwd, _tri_y_bwd)


def _saved_inverse(y):
    @jax.custom_vjp
    def inverse(a):
        return y

    inverse.defvjp(lambda a: (y, None), lambda _, dy: _tri_y_bwd(y, dy))
    return inverse


def _head_rows(a3, nb):
    return jnp.concatenate([a3[g] for g in range(nb * HEADS)], axis=0)


def _rows_down(x, s):
    rows = x.shape[0]

    @jax.custom_vjp
    def rotate(v):
        return pltpu.roll(v, s, 0)

    rotate.defvjp(lambda v: (pltpu.roll(v, s, 0), None), lambda _, d: (pltpu.roll(d, rows - s, 0),))
    return rotate(x)


def gd_local(xx, ab, cw, alog, dtb, inverse=_tri_y):
    n = ab.shape[0]
    nb = n // CHUNK
    conv = cw[CONV_TAPS - 1:CONV_TAPS] * xx[HALO:HALO + n]
    for j in range(CONV_TAPS - 1):
        conv = conv + cw[j:j + 1] * _rows_down(xx, CONV_TAPS - 1 - j)[HALO:HALO + n]
    act = _silu(conv)
    x = ab + dtb
    g_all = -jnp.exp(alog) * (jnp.maximum(x, 0.0) + jnp.log1p(jnp.exp(-jnp.abs(x))))
    beta_all = jax.nn.sigmoid(ab)
    gam_all = _cumsum_chunks(g_all, nb)
    q3, k3, v3 = _heads(act[:, 0:WIDTH], nb), _heads(act[:, WIDTH:2 * WIDTH], nb), _heads(act[:, 2 * WIDTH:QKV], nb)
    q3 = q3 * lax.rsqrt(jnp.sum(q3 * q3, axis=-1, keepdims=True) + EPS) * (DH ** -0.5)
    k3 = k3 * lax.rsqrt(jnp.sum(k3 * k3, axis=-1, keepdims=True) + EPS)
    pairs = [(c, h) for c in range(nb) for h in range(HEADS)]
    beta = jnp.stack([beta_all[c * CHUNK:(c + 1) * CHUNK, HEADS + h:HEADS + h + 1] for c, h in pairs], axis=0)
    gam = jnp.stack([gam_all[c * CHUNK:(c + 1) * CHUNK, h:h + 1] for c, h in pairs], axis=0)
    gam_t = [gam_all[c * CHUNK:(c + 1) * CHUNK].T for c in range(nb)]
    gam_row = jnp.stack([gam_t[c][h:h + 1, :] for c, h in pairs], axis=0)
    glast = gam[:, CHUNK - 1:CHUNK, :]
    r, c = _iota2(CHUNK, CHUNK)
    dec = jnp.exp(jnp.where(c < r, gam - gam_row, -jnp.inf))
    y = inverse(beta * _bmm_nt(k3, k3) * dec)
    eg = jnp.exp(gam)
    rhs = jnp.concatenate([beta * v3, (beta * eg) * k3], axis=2)
    sol = rhs + _bmm(y, rhs)
    qk = _bmm_nt(q3, k3) * jnp.where(r == c, 1.0, dec)
    eas = tuple(jnp.exp(gam_all[(c + 1) * CHUNK - 1:(c + 1) * CHUNK]) for c in range(nb))
    return (_unheads(sol[:, :, 0:DH], nb), _unheads(sol[:, :, DH:2 * DH], nb), _unheads(q3 * eg, nb),
            _unheads(k3 * jnp.exp(glast - gam), nb), _head_rows(qk, nb), eas), _head_rows(y, nb)


def gd_scan(uu, ww, qe, ke, qk, ea, z, nw, s):
    u = uu - _bmm(ww, s)
    o = _bmm(qe, s) + _bmm(qk, u)
    return _gated_norm(o, z, nw), ea * s + _bmm_tn(ke, u)


def _cparams(*sem):
    return pltpu.CompilerParams(dimension_semantics=sem, vmem_limit_bytes=VMEM_LIMIT)


def _row_tile(n):
    for t in (512, 256, 128, 64):
        if n % t == 0:
            return t
    raise ValueError(f"unsupported token count {n}")


def _w_in_specs():
    once = pl.Buffered(1)
    return [pl.BlockSpec((4 * WIDTH, D_MODEL), lambda *i: (0, 0), pipeline_mode=once),
            pl.BlockSpec((4 * WIDTH, D_MODEL), lambda *i: (1, 0), pipeline_mode=once),
            pl.BlockSpec((AB_PAD, D_MODEL), lambda *i: (8 * WIDTH // AB_PAD, 0), pipeline_mode=once)]


def in_proj(h, h0, norm_w, w_t):
    n = h.shape[0]
    tm = _row_tile(n)
    nt = (((1,), (1,)), ((), ()))

    def body(h_ref, h0_ref, nw_ref, whg_ref, wgd_ref, wab_ref, phg_ref, pgd_ref, pab_ref, phg0_ref, pgd0_ref, pab0_ref, u0_ref):
        def project(x, hg_ref, gd_ref, ab_ref):
            u = (x * lax.rsqrt(jnp.mean(x * x, axis=-1, keepdims=True) + EPS) * nw_ref[...]).astype(MXU_DTYPE)
            hg_ref[...] = lax.dot_general(u, whg_ref[...], nt, preferred_element_type=F32)
            gd_ref[...] = lax.dot_general(u, wgd_ref[...], nt, preferred_element_type=F32)
            ab_ref[...] = lax.dot_general(u, wab_ref[...], nt, preferred_element_type=F32)
            return u

        @pl.when(pl.program_id(0) == 0)
        def _():
            u0_ref[...] = project(h0_ref[...], phg0_ref, pgd0_ref, pab0_ref)

        project(h_ref[...], phg_ref, pgd_ref, pab_ref)

    n0 = h0.shape[0]
    row = lambda w: pl.BlockSpec((tm, w), lambda i: (i, 0))
    lead = lambda w: pl.BlockSpec((n0, w), lambda i: (0, 0))
    widths = [4 * WIDTH, 4 * WIDTH, AB_PAD]
    return pl.pallas_call(
        body, grid=(n // tm,), name="in_proj",
        in_specs=[row(D_MODEL), lead(D_MODEL), pl.BlockSpec(norm_w.shape, lambda i: (0, 0))] + _w_in_specs(),
        out_specs=[row(w) for w in widths] + [lead(w) for w in widths] + [lead(D_MODEL)],
        out_shape=[jax.ShapeDtypeStruct((n, w), F32) for w in widths] + [jax.ShapeDtypeStruct((n0, w), F32) for w in widths]
        + [jax.ShapeDtypeStruct((n0, D_MODEL), MXU_DTYPE)],
        compiler_params=_cparams("arbitrary"),
    )(h, h0, norm_w, w_t, w_t, w_t)


def out_proj_loss(x, tgt, y_hg, y_gd, w_out, fw):
    n = x.shape[0]
    tm = _row_tile(n)
    inv_d = 1.0 / D_MODEL

    def body(x_ref, t_ref, yh_ref, yg_ref, w_ref, fw_ref, dh_ref, dyh_ref, dyg_ref, dw_ref, loss_ref, dfw_ref):
        @pl.when(pl.program_id(0) == 0)
        def _():
            dw_ref[...] = jnp.zeros_like(dw_ref)
            loss_ref[...] = jnp.zeros_like(loss_ref)
            dfw_ref[...] = jnp.zeros_like(dfw_ref)

        yh, yg = yh_ref[...], yg_ref[...]
        wa, wb = w_ref[0:WIDTH, :], w_ref[WIDTH:2 * WIDTH, :]
        h2 = x_ref[...] + jnp.dot(yh, wa, preferred_element_type=F32) + jnp.dot(yg, wb, preferred_element_type=F32)
        r2 = lax.rsqrt(jnp.mean(h2 * h2, axis=-1, keepdims=True) + EPS)
        nrm = h2 * r2
        fwv = fw_ref[...]
        err = nrm * fwv - t_ref[...]
        loss_ref[...] += jnp.full(loss_ref.shape, 0.5 * inv_d * jnp.sum(err * err), F32)
        dout = err * inv_d
        dfw_ref[...] += jnp.sum(dout * nrm, axis=0, keepdims=True)
        dn = dout * fwv
        dh2 = r2 * (dn - nrm * jnp.mean(dn * nrm, axis=-1, keepdims=True))
        dh_ref[...] = dh2
        dhb = dh2.astype(MXU_DTYPE)
        dyh_ref[...] = lax.dot_general(dhb, wa, (((1,), (1,)), ((), ())), preferred_element_type=F32)
        dyg_ref[...] = lax.dot_general(dhb, wb, (((1,), (1,)), ((), ())), preferred_element_type=F32)
        dw_ref[0:WIDTH, :] += lax.dot_general(yh, dhb, (((0,), (0,)), ((), ())), preferred_element_type=F32)
        dw_ref[WIDTH:2 * WIDTH, :] += lax.dot_general(yg, dhb, (((0,), (0,)), ((), ())), preferred_element_type=F32)

    row = lambda w: pl.BlockSpec((tm, w), lambda i: (i, 0))
    full = lambda s: pl.BlockSpec(s, lambda i: (0, 0))
    return pl.pallas_call(
        body, grid=(n // tm,), name="out_proj_loss",
        in_specs=[row(D_MODEL), row(D_MODEL), row(WIDTH), row(WIDTH), full(w_out.shape), full(fw.shape)],
        out_specs=[row(D_MODEL), row(WIDTH), row(WIDTH), full((2 * WIDTH, D_MODEL)), full((8, 128)), full((1, D_MODEL))],
        out_shape=[jax.ShapeDtypeStruct((n, D_MODEL), F32), jax.ShapeDtypeStruct((n, WIDTH), F32),
                   jax.ShapeDtypeStruct((n, WIDTH), F32), jax.ShapeDtypeStruct((2 * WIDTH, D_MODEL), F32),
                   jax.ShapeDtypeStruct((8, 128), F32), jax.ShapeDtypeStruct((1, D_MODEL), F32)],
        compiler_params=_cparams("arbitrary"),
    )(x, tgt, y_hg, y_gd, w_out, fw)


def in_proj_bwd(dphg, dpgd, dpab, w_t, h, dh2, norm_w, h0, u0, dphg0, dpgd0, dpab0):
    n = h.shape[0]
    tm = _row_tile(n)
    steps = n // tm

    def body(dphg_ref, dpgd_ref, dpab_ref, whg_ref, wgd_ref, wab_ref, h_ref, dh2_ref, nw_ref, h0_ref, u0_ref, d0hg_ref,
             d0gd_ref, d0ab_ref, dx_ref, dx0_ref, dnw_ref, ghg_ref, ggd_ref, gab_ref, acc_hg, acc_gd, acc_ab):
        i = pl.program_id(0)
        nwv = nw_ref[...]

        def norm_bwd(dps, x):
            du = jnp.dot(dps[0], whg_ref[...], preferred_element_type=F32)
            du += jnp.dot(dps[1], wgd_ref[...], preferred_element_type=F32)
            du += jnp.dot(dps[2], wab_ref[...], preferred_element_type=F32)
            r = lax.rsqrt(jnp.mean(x * x, axis=-1, keepdims=True) + EPS)
            nrm = x * r
            dn = du * nwv
            return r * (dn - nrm * jnp.mean(dn * nrm, axis=-1, keepdims=True)), nrm, jnp.sum(du * nrm, axis=0, keepdims=True)

        def accumulate(dps, u, first):
            for acc, dp in zip((acc_hg, acc_gd, acc_ab), dps):
                step = min(acc.shape[0], 512)
                for lo in range(0, acc.shape[0], step):
                    part = _mm_tn(dp[:, lo:lo + step], u)
                    acc[lo:lo + step, :] = part if first else acc[lo:lo + step, :] + part

        @pl.when(i == 0)
        def _():
            dps0 = (d0hg_ref[...], d0gd_ref[...], d0ab_ref[...])
            dx0_ref[...], _, dnw_ref[...] = norm_bwd(dps0, h0_ref[...])
            accumulate(dps0, u0_ref[...], True)

        dps = (dphg_ref[...], dpgd_ref[...], dpab_ref[...])
        dx, nrm, dnw = norm_bwd(dps, h_ref[...])
        dx_ref[...] = dh2_ref[...] + dx
        dnw_ref[...] += dnw
        accumulate(dps, (nrm * nwv).astype(MXU_DTYPE), False)

        @pl.when(i == steps - 1)
        def _():
            pltpu.sync_copy(acc_hg, ghg_ref)
            pltpu.sync_copy(acc_gd, ggd_ref)
            pltpu.sync_copy(acc_ab, gab_ref)

    row = lambda w: pl.BlockSpec((tm, w), lambda i: (i, 0))
    full = lambda a: pl.BlockSpec(a.shape, lambda i: (0, 0), pipeline_mode=pl.Buffered(1))
    anywhere = pl.BlockSpec(memory_space=pl.ANY)
    return pl.pallas_call(
        body, grid=(steps,), name="in_proj_bwd",
        in_specs=[row(4 * WIDTH), row(4 * WIDTH), row(AB_PAD)] + _w_in_specs() + [row(D_MODEL), row(D_MODEL), full(norm_w),
                                                                                   full(h0), full(u0), full(dphg0), full(dpgd0),
                                                                                   full(dpab0)],
        out_specs=[row(D_MODEL), pl.BlockSpec(h0.shape, lambda i: (0, 0)), pl.BlockSpec((1, D_MODEL), lambda i: (0, 0)),
                   anywhere, anywhere, anywhere],
        out_shape=[jax.ShapeDtypeStruct((n, D_MODEL), F32), jax.ShapeDtypeStruct(h0.shape, F32),
                   jax.ShapeDtypeStruct((1, D_MODEL), F32), jax.ShapeDtypeStruct((4 * WIDTH, D_MODEL), F32),
                   jax.ShapeDtypeStruct((4 * WIDTH, D_MODEL), F32), jax.ShapeDtypeStruct((AB_PAD, D_MODEL), F32)],
        scratch_shapes=[pltpu.VMEM((4 * WIDTH, D_MODEL), F32), pltpu.VMEM((4 * WIDTH, D_MODEL), F32),
                        pltpu.VMEM((AB_PAD, D_MODEL), F32)],
        compiler_params=pltpu.CompilerParams(dimension_semantics=("arbitrary",), vmem_limit_bytes=VMEM_LIMIT_LARGE),
    )(dphg, dpgd, dpab, w_t, w_t, w_t, h, dh2, norm_w, h0, u0, dphg0, dpgd0, dpab0)


def _sds(shape, dtype=F32):
    return jax.ShapeDtypeStruct(shape, dtype)


def _pairs(b):
    return [(i, h) for i in range(b) for h in range(HEADS)]


def _load_slabs(ref, b, k):
    return jnp.stack([ref[i, k * CHUNK:(k + 1) * CHUNK, h * DH:(h + 1) * DH].astype(F32) for i, h in _pairs(b)], axis=0)


def _lead_slabs(a, b):
    return jnp.stack([a[:, h * DH:(h + 1) * DH].astype(F32) for _, h in _pairs(b)], axis=0)


def _rows(a3, i):
    return jnp.concatenate([a3[i * HEADS + h] for h in range(HEADS)], axis=1)


def _store_slabs(ref, a3, b, k):
    for i in range(b):
        ref[i, k * CHUNK:(k + 1) * CHUNK, :] = _rows(a3, i).astype(ref.dtype)


def _sum_rows(a3, b):
    out = _rows(a3, 0)
    for i in range(1, b):
        out = out + _rows(a3, i)
    return out


def _save_states(ref, s, b, k):
    for i in range(b):
        ref[i, k] = jnp.concatenate([s[i * HEADS + h] for h in range(HEADS)], axis=0)


def _load_states(ref, b, k):
    return jnp.stack([ref[i, k, h * DH:(h + 1) * DH, :] for i, h in _pairs(b)], axis=0)


def hg_local_fwd(p, p0, logits):
    b, seq, _ = p.shape
    rows = LOCAL_CHUNKS * CHUNK
    nreal = seq // CHUNK

    def body(p_ref, p0_ref, lg_ref, q_ref, k_ref, o_ref, eg_ref, q0_ref, k0_ref, o0_ref, eg0_ref):
        @pl.when((pl.program_id(0) == 0) & (pl.program_id(1) == 0))
        def _():
            q_in, k_out, o0_ref[...], (eg0_ref[...],) = hg_local(p0_ref[...], lg_ref[...])
            q0_ref[...], k0_ref[...] = q_in.astype(MXU_DTYPE), k_out.astype(MXU_DTYPE)

        q_in, k_out, o_intra, egs = hg_local(p_ref[...], lg_ref[...])
        q_ref[...], k_ref[...], o_ref[...] = q_in.astype(MXU_DTYPE), k_out.astype(MXU_DTYPE), o_intra
        for c in range(LOCAL_CHUNKS):
            eg_ref[c] = egs[c]

    slab = pl.BlockSpec((None, rows, WIDTH), lambda s, g: (s, g, 0))
    const = lambda shape: pl.BlockSpec(shape, lambda s, g: (0, 0))
    lead_shapes = [(CHUNK, WIDTH)] * 3 + [(1, WIDTH)]
    out = pl.pallas_call(
        body, grid=(b, seq // rows), name="hgrn2_local",
        in_specs=[pl.BlockSpec((None, rows, 4 * WIDTH), lambda s, g: (s, g, 0)), const(p0.shape), const(logits.shape)],
        out_specs=[slab, slab, slab, pl.BlockSpec((None, LOCAL_CHUNKS, 1, WIDTH), lambda s, g: (s, g, 0, 0))]
        + [const(s) for s in lead_shapes],
        out_shape=[_sds((b, seq, WIDTH), MXU_DTYPE)] * 2 + [_sds((b, seq, WIDTH)), _sds((b, nreal, 1, WIDTH))]
        + [_sds(lead_shapes[0], MXU_DTYPE)] * 2 + [_sds(lead_shapes[2]), _sds(lead_shapes[3])],
        compiler_params=_cparams("arbitrary", "arbitrary"),
    )(p, p0, logits)
    return out[0:4], out[4:8]


def _hg_scan_args(b, k, q_ref, k_ref, o_ref, v_ref, z_ref, eg_ref):
    eg = jnp.stack([eg_ref[i, k, :, h * DH:(h + 1) * DH] for i, h in _pairs(b)], axis=0)
    return (_load_slabs(q_ref, b, k), _load_slabs(k_ref, b, k), _load_slabs(v_ref, b, k), eg, _load_slabs(o_ref, b, k),
            _load_slabs(z_ref, b, k))


def _hg_lead_args(b, q0_ref, k0_ref, o0_ref, p0_ref, eg0_ref):
    eg = jnp.stack([eg0_ref[:, h * DH:(h + 1) * DH] for _, h in _pairs(b)], axis=0)
    return (_lead_slabs(q0_ref[...], b), _lead_slabs(k0_ref[...], b), _lead_slabs(p0_ref[:, 2 * WIDTH:3 * WIDTH], b), eg,
            _lead_slabs(o0_ref[...], b), _lead_slabs(p0_ref[:, 3 * WIDTH:4 * WIDTH], b))


def _scan_specs(b, ng, reverse):
    group = (lambda i: ng - 1 - i) if reverse else (lambda i: i)
    slab = lambda lane_block: pl.BlockSpec((b, SCAN_CHUNKS * CHUNK, WIDTH), lambda i: (0, group(i), lane_block))
    per_chunk = lambda *tail: pl.BlockSpec((b, SCAN_CHUNKS) + tail, lambda i: (0, group(i)) + (0,) * len(tail))
    const = lambda a: pl.BlockSpec(a.shape, lambda i: (0,) * a.ndim)
    return slab, per_chunk, const


def run_scans(parts, nc, name):
    n_in = [len(p["args"]) for p in parts]
    n_out = [len(p["out_shape"]) for p in parts]
    n_scr = [len(p["scratch_shapes"]) for p in parts]

    def body(*refs):
        ins, outs, scr = refs[:sum(n_in)], refs[sum(n_in):sum(n_in) + sum(n_out)], refs[sum(n_in) + sum(n_out):]
        for i, part in enumerate(parts):
            part["body"](*ins[sum(n_in[:i]):sum(n_in[:i + 1])], *outs[sum(n_out[:i]):sum(n_out[:i + 1])],
                         *scr[sum(n_scr[:i]):sum(n_scr[:i + 1])])

    flat = lambda key: [v for p in parts for v in p[key]]
    out = pl.pallas_call(body, grid=(nc,), name=name, in_specs=flat("in_specs"), out_specs=flat("out_specs"),
                         out_shape=flat("out_shape"), scratch_shapes=flat("scratch_shapes"),
                         compiler_params=_cparams("arbitrary"))(*flat("args"))
    return [out[sum(n_out[:i]):sum(n_out[:i + 1])] for i in range(len(parts))]


def hg_scan_fwd(p, p0, local, lead, nw):
    b, seq, _ = p.shape
    q_in, k_out, o_intra, eg = local
    slab, per_chunk, const = _scan_specs(b, seq // (SCAN_CHUNKS * CHUNK), False)

    def body(q_ref, k_ref, o_ref, v_ref, z_ref, eg_ref, q0_ref, k0_ref, o0_ref, p0_ref, eg0_ref, nw_ref, y_ref, ss_ref, st):
        @pl.when(pl.program_id(0) == 0)
        def _():
            st[...] = hg_scan(*_hg_lead_args(b, q0_ref, k0_ref, o0_ref, p0_ref, eg0_ref), nw_ref[...], jnp.zeros(st.shape, F32))[1]

        s = st[...]
        for k in range(SCAN_CHUNKS):
            _save_states(ss_ref, s, b, k)
            y, s = hg_scan(*_hg_scan_args(b, k, q_ref, k_ref, o_ref, v_ref, z_ref, eg_ref), nw_ref[...], s)
            _store_slabs(y_ref, y, b, k)
        st[...] = s

    return dict(
        body=body, args=(q_in, k_out, o_intra, p, p, eg, lead[0], lead[1], lead[2], p0, lead[3], nw),
        in_specs=[slab(0), slab(0), slab(0), slab(2), slab(3), per_chunk(1, WIDTH)] + [const(a) for a in lead[0:3]]
        + [const(p0), const(lead[3]), const(nw)],
        out_specs=[slab(0), per_chunk(WIDTH, DH)],
        out_shape=[_sds((b, seq, WIDTH), MXU_DTYPE), _sds((b, seq // CHUNK, WIDTH, DH))],
        scratch_shapes=[pltpu.VMEM((b * HEADS, DH, DH), F32)])


def hg_scan_bwd(p, p0, local, lead, nw, ssave, dy):
    b, seq, _ = p.shape
    ng = seq // (SCAN_CHUNKS * CHUNK)
    q_in, k_out, o_intra, eg = local
    slab, per_chunk, const = _scan_specs(b, ng, True)

    def body(q_ref, k_ref, o_ref, v_ref, z_ref, eg_ref, q0_ref, k0_ref, o0_ref, p0_ref, eg0_ref, nw_ref, ss_ref, dy_ref,
             dq_ref, dk_ref, do_ref, dv_ref, dz_ref, deg_ref, dq0_ref, dk0_ref, do0_ref, dv0_ref, dz0_ref, deg0_ref, dnw_ref,
             dst):
        i = pl.program_id(0)

        @pl.when(i == 0)
        def _():
            dst[...] = jnp.zeros_like(dst)
            dnw_ref[...] = jnp.zeros_like(dnw_ref)

        ds = dst[...]
        for k in reversed(range(SCAN_CHUNKS)):
            args = _hg_scan_args(b, k, q_ref, k_ref, o_ref, v_ref, z_ref, eg_ref)
            _, vjp = jax.vjp(hg_scan, *args, nw_ref[...], _load_states(ss_ref, b, k))
            dq, dk, dv, deg, do, dz, dnw, ds = vjp((_load_slabs(dy_ref, b, k), ds))
            dnw_ref[...] += dnw
            for ref, val in ((dq_ref, dq), (dk_ref, dk), (do_ref, do), (dv_ref, dv), (dz_ref, dz)):
                _store_slabs(ref, val, b, k)
            for j in range(b):
                deg_ref[j, k] = _rows(deg, j)
        dst[...] = ds

        @pl.when(i == ng - 1)
        def _():
            args = _hg_lead_args(b, q0_ref, k0_ref, o0_ref, p0_ref, eg0_ref)
            _, vjp = jax.vjp(hg_scan, *args, nw_ref[...], jnp.zeros(dst.shape, F32))
            dq, dk, dv, deg, do, dz, dnw, _ = vjp((jnp.zeros((b * HEADS, CHUNK, DH), F32), ds))
            dnw_ref[...] += dnw
            for ref, val in ((dq0_ref, dq), (dk0_ref, dk), (do0_ref, do), (dv0_ref, dv), (dz0_ref, dz), (deg0_ref, deg)):
                ref[...] = _sum_rows(val, b)

    lead_out = [const(a) for a in lead[0:3]] + [const(lead[0]), const(lead[0]), const(lead[3])]
    return dict(
        body=body, args=(q_in, k_out, o_intra, p, p, eg, lead[0], lead[1], lead[2], p0, lead[3], nw, ssave, dy),
        in_specs=[slab(0), slab(0), slab(0), slab(2), slab(3), per_chunk(1, WIDTH)] + [const(a) for a in lead[0:3]]
        + [const(p0), const(lead[3]), const(nw), per_chunk(WIDTH, DH), slab(0)],
        out_specs=[slab(0)] * 5 + [per_chunk(1, WIDTH)] + lead_out + [const(nw)],
        out_shape=[_sds((b, seq, WIDTH))] * 5 + [_sds(eg.shape)] + [_sds((CHUNK, WIDTH))] * 5 + [_sds((1, WIDTH)), _sds(nw.shape)],
        scratch_shapes=[pltpu.VMEM((b * HEADS, DH, DH), F32)])


def _hg_local_vjp(p, logits, dq, dk, do, degs, dv, dz):
    _, vjp = jax.vjp(hg_local, p, logits)
    dp, dlg = vjp((dq, dk, do, degs))
    return dp + jnp.concatenate([jnp.zeros((p.shape[0], 2 * WIDTH), F32), dv, dz], axis=1), dlg


def hg_local_bwd(p, p0, logits, cot, cot0):
    b, seq, _ = p.shape
    rows = LOCAL_CHUNKS * CHUNK

    def body(p_ref, p0_ref, lg_ref, dq_ref, dk_ref, do_ref, dv_ref, dz_ref, deg_ref, dq0_ref, dk0_ref, do0_ref, dv0_ref, dz0_ref,
             deg0_ref, dp_ref, dp0_ref, dlg_ref):
        @pl.when((pl.program_id(0) == 0) & (pl.program_id(1) == 0))
        def _():
            dp0, dlg_ref[...] = _hg_local_vjp(p0_ref[...], lg_ref[...], dq0_ref[...], dk0_ref[...], do0_ref[...],
                                              (deg0_ref[...],), dv0_ref[...], dz0_ref[...])
            dp0_ref[...] = dp0.astype(MXU_DTYPE)

        degs = tuple(deg_ref[c] for c in range(LOCAL_CHUNKS))
        dp, dlg = _hg_local_vjp(p_ref[...], lg_ref[...], dq_ref[...], dk_ref[...], do_ref[...], degs, dv_ref[...], dz_ref[...])
        dp_ref[...] = dp.astype(MXU_DTYPE)
        dlg_ref[...] += dlg

    slab = pl.BlockSpec((None, rows, WIDTH), lambda s, g: (s, g, 0))
    wide = pl.BlockSpec((None, rows, 4 * WIDTH), lambda s, g: (s, g, 0))
    const = lambda a: pl.BlockSpec(a.shape, lambda s, g: (0, 0))
    return pl.pallas_call(
        body, grid=(b, seq // rows), name="hgrn2_local_bwd",
        in_specs=[wide, const(p0), const(logits), slab, slab, slab, slab, slab,
                  pl.BlockSpec((None, LOCAL_CHUNKS, 1, WIDTH), lambda s, g: (s, g, 0, 0))] + [const(a) for a in cot0],
        out_specs=[wide, const(p0), const(logits)],
        out_shape=[_sds(p.shape, MXU_DTYPE), _sds(p0.shape, MXU_DTYPE), _sds(logits.shape)],
        compiler_params=_cparams("arbitrary", "arbitrary"),
    )(p, p0, logits, *cot, *cot0)


def _halo_block(g):
    return jnp.maximum((LOCAL_CHUNKS * CHUNK // HALO) * g - 1, 0)


def _gd_window(g, p_ref, halo_ref, p0_ref):
    halo = jnp.where(g == 0, p0_ref[CHUNK - HALO:CHUNK, 0:QKV], halo_ref[...])
    return jnp.concatenate([halo, p_ref[:, 0:QKV]], axis=0)


def _lead_window(p0_ref):
    return jnp.concatenate([jnp.zeros((HALO, QKV), F32), p0_ref[:, 0:QKV]], axis=0)


def gd_local_fwd(p, p0, ab, ab0, cw, alog, dtb):
    b, seq, _ = p.shape
    rows = LOCAL_CHUNKS * CHUNK
    nreal = seq // CHUNK

    def body(p_ref, halo_ref, p0_ref, ab_ref, ab0_ref, cw_ref, al_ref, dt_ref, u_ref, w_ref, qe_ref, ke_ref, qk_ref, ea_ref,
             inv_ref, u0_ref, w0_ref, qe0_ref, ke0_ref, qk0_ref, ea0_ref, inv0_ref):
        @pl.when((pl.program_id(0) == 0) & (pl.program_id(1) == 0))
        def _():
            (u0_ref[...], ww, qe, ke, qk0_ref[...], (ea0_ref[...],)), inv0_ref[...] = gd_local(
                _lead_window(p0_ref), ab0_ref[...], cw_ref[...], al_ref[...], dt_ref[...], inverse=_tri_y_impl)
            w0_ref[...], qe0_ref[...], ke0_ref[...] = ww.astype(MXU_DTYPE), qe.astype(MXU_DTYPE), ke.astype(MXU_DTYPE)

        (uu, ww, qe, ke, qk, eas), inv = gd_local(_gd_window(pl.program_id(1), p_ref, halo_ref, p0_ref), ab_ref[...],
                                                  cw_ref[...], al_ref[...], dt_ref[...], inverse=_tri_y_impl)
        u_ref[...], w_ref[...], qe_ref[...], ke_ref[...] = uu, ww.astype(MXU_DTYPE), qe.astype(MXU_DTYPE), ke.astype(MXU_DTYPE)
        for c in range(LOCAL_CHUNKS):
            qk_ref[c] = qk[c * HEADS * CHUNK:(c + 1) * HEADS * CHUNK]
            inv_ref[c] = inv[c * HEADS * CHUNK:(c + 1) * HEADS * CHUNK]
            ea_ref[c] = eas[c]

    const = lambda shape: pl.BlockSpec(shape, lambda s, g: (0, 0))
    slab = pl.BlockSpec((None, rows, WIDTH), lambda s, g: (s, g, 0))
    mats = pl.BlockSpec((None, LOCAL_CHUNKS, HEADS * CHUNK, CHUNK), lambda s, g: (s, g, 0, 0))
    lead_out = [_sds((CHUNK, WIDTH))] + [_sds((CHUNK, WIDTH), MXU_DTYPE)] * 3 + [_sds((HEADS * CHUNK, CHUNK)), _sds((1, AB_PAD)),
                                                                                _sds((HEADS * CHUNK, CHUNK))]
    out = pl.pallas_call(
        body, grid=(b, seq // rows), name="gdn_local",
        in_specs=[pl.BlockSpec((None, rows, 4 * WIDTH), lambda s, g: (s, g, 0)),
                  pl.BlockSpec((None, HALO, QKV), lambda s, g: (s, _halo_block(g), 0)), const(p0.shape),
                  pl.BlockSpec((None, rows, AB_PAD), lambda s, g: (s, g, 0)), const(ab0.shape), const(cw.shape),
                  const(alog.shape), const(dtb.shape)],
        out_specs=[slab] * 4 + [mats, pl.BlockSpec((None, LOCAL_CHUNKS, 1, AB_PAD), lambda s, g: (s, g, 0, 0)), mats]
        + [const(s.shape) for s in lead_out],
        out_shape=[_sds((b, seq, WIDTH))] + [_sds((b, seq, WIDTH), MXU_DTYPE)] * 3
        + [_sds((b, nreal, HEADS * CHUNK, CHUNK)), _sds((b, nreal, 1, AB_PAD)), _sds((b, nreal, HEADS * CHUNK, CHUNK))] + lead_out,
        compiler_params=_cparams("arbitrary", "arbitrary"),
    )(p, p, p0, ab, ab0, cw, alog, dtb)
    return out[0:6], out[6], out[7:13], out[13]


def _gd_scan_args(b, k, u_ref, w_ref, qe_ref, ke_ref, qk_ref, ea_ref, z_ref):
    qk = jnp.stack([qk_ref[i, k, h * CHUNK:(h + 1) * CHUNK, :] for i, h in _pairs(b)], axis=0)
    ea = jnp.stack([ea_ref[i, k, :, h:h + 1] for i, h in _pairs(b)], axis=0)
    return (_load_slabs(u_ref, b, k), _load_slabs(w_ref, b, k), _load_slabs(qe_ref, b, k), _load_slabs(ke_ref, b, k), qk, ea,
            _load_slabs(z_ref, b, k))


def _gd_lead_args(b, u0_ref, w0_ref, qe0_ref, ke0_ref, qk0_ref, ea0_ref, p0_ref):
    qk = jnp.stack([qk0_ref[h * CHUNK:(h + 1) * CHUNK, :] for _, h in _pairs(b)], axis=0)
    ea = jnp.stack([ea0_ref[:, h:h + 1] for _, h in _pairs(b)], axis=0)
    return (_lead_slabs(u0_ref[...], b), _lead_slabs(w0_ref[...], b), _lead_slabs(qe0_ref[...], b), _lead_slabs(ke0_ref[...], b),
            qk, ea, _lead_slabs(p0_ref[:, QKV:QKV + WIDTH], b))


def gd_scan_fwd(p, p0, local, lead, nw):
    b, seq, _ = p.shape
    slab, per_chunk, const = _scan_specs(b, seq // (SCAN_CHUNKS * CHUNK), False)

    def body(u_ref, w_ref, qe_ref, ke_ref, qk_ref, ea_ref, z_ref, u0_ref, w0_ref, qe0_ref, ke0_ref, qk0_ref, ea0_ref, p0_ref,
             nw_ref, y_ref, ss_ref, st):
        @pl.when(pl.program_id(0) == 0)
        def _():
            lead_args = _gd_lead_args(b, u0_ref, w0_ref, qe0_ref, ke0_ref, qk0_ref, ea0_ref, p0_ref)
            st[...] = gd_scan(*lead_args, nw_ref[...], jnp.zeros(st.shape, F32))[1]

        s = st[...]
        for k in range(SCAN_CHUNKS):
            _save_states(ss_ref, s, b, k)
            y, s = gd_scan(*_gd_scan_args(b, k, u_ref, w_ref, qe_ref, ke_ref, qk_ref, ea_ref, z_ref), nw_ref[...], s)
            _store_slabs(y_ref, y, b, k)
        st[...] = s

    return dict(
        body=body, args=(*local, p, *lead, p0, nw),
        in_specs=[slab(0)] * 4 + [per_chunk(HEADS * CHUNK, CHUNK), per_chunk(1, AB_PAD), slab(3)] + [const(a) for a in lead]
        + [const(p0), const(nw)],
        out_specs=[slab(0), per_chunk(WIDTH, DH)],
        out_shape=[_sds((b, seq, WIDTH), MXU_DTYPE), _sds((b, seq // CHUNK, WIDTH, DH))],
        scratch_shapes=[pltpu.VMEM((b * HEADS, DH, DH), F32)])


def gd_scan_bwd(p, p0, local, lead, nw, ssave, dy):
    b, seq, _ = p.shape
    ng = seq // (SCAN_CHUNKS * CHUNK)
    slab, per_chunk, const = _scan_specs(b, ng, True)

    def body(u_ref, w_ref, qe_ref, ke_ref, qk_ref, ea_ref, z_ref, u0_ref, w0_ref, qe0_ref, ke0_ref, qk0_ref, ea0_ref, p0_ref,
             nw_ref, ss_ref, dy_ref, du_ref, dw_ref, dqe_ref, dke_ref, dqk_ref, dea_ref, dz_ref, du0_ref, dw0_ref, dqe0_ref,
             dke0_ref, dqk0_ref, dea0_ref, dz0_ref, dnw_ref, dst):
        i = pl.program_id(0)
        lane = lax.broadcasted_iota(jnp.int32, (1, AB_PAD), 1)

        def gate_rows(dea, j):
            return sum(jnp.where(lane == h, dea[j * HEADS + h], 0.0) for h in range(HEADS))

        def matrix_rows(dqk, j):
            return jnp.concatenate([dqk[j * HEADS + h] for h in range(HEADS)], axis=0)

        @pl.when(i == 0)
        def _():
            dst[...] = jnp.zeros_like(dst)
            dnw_ref[...] = jnp.zeros_like(dnw_ref)

        ds = dst[...]
        for k in reversed(range(SCAN_CHUNKS)):
            args = _gd_scan_args(b, k, u_ref, w_ref, qe_ref, ke_ref, qk_ref, ea_ref, z_ref)
            _, vjp = jax.vjp(gd_scan, *args, nw_ref[...], _load_states(ss_ref, b, k))
            du, dw, dqe, dke, dqk, dea, dz, dnw, ds = vjp((_load_slabs(dy_ref, b, k), ds))
            dnw_ref[...] += dnw
            for ref, val in ((du_ref, du), (dw_ref, dw), (dqe_ref, dqe), (dke_ref, dke), (dz_ref, dz)):
                _store_slabs(ref, val, b, k)
            for j in range(b):
                dqk_ref[j, k] = matrix_rows(dqk, j)
                dea_ref[j, k] = gate_rows(dea, j)
        dst[...] = ds

        @pl.when(i == ng - 1)
        def _():
            args = _gd_lead_args(b, u0_ref, w0_ref, qe0_ref, ke0_ref, qk0_ref, ea0_ref, p0_ref)
            _, vjp = jax.vjp(gd_scan, *args, nw_ref[...], jnp.zeros(dst.shape, F32))
            du, dw, dqe, dke, dqk, dea, dz, dnw, _ = vjp((jnp.zeros((b * HEADS, CHUNK, DH), F32), ds))
            dnw_ref[...] += dnw
            for ref, val in ((du0_ref, du), (dw0_ref, dw), (dqe0_ref, dqe), (dke0_ref, dke), (dz0_ref, dz)):
                ref[...] = _sum_rows(val, b)
            dqk0_ref[...] = sum((matrix_rows(dqk, j) for j in range(1, b)), matrix_rows(dqk, 0))
            dea0_ref[...] = sum((gate_rows(dea, j) for j in range(1, b)), gate_rows(dea, 0))

    uu, ww, qe, ke, qk, ea = local
    return dict(
        body=body, args=(*local, p, *lead, p0, nw, ssave, dy),
        in_specs=[slab(0)] * 4 + [per_chunk(HEADS * CHUNK, CHUNK), per_chunk(1, AB_PAD), slab(3)] + [const(a) for a in lead]
        + [const(p0), const(nw), per_chunk(WIDTH, DH), slab(0)],
        out_specs=[slab(0)] * 4 + [per_chunk(HEADS * CHUNK, CHUNK), per_chunk(1, AB_PAD), slab(0)] + [const(a) for a in lead]
        + [const(lead[0]), const(nw)],
        out_shape=[_sds((b, seq, WIDTH))] * 4 + [_sds(qk.shape), _sds(ea.shape), _sds((b, seq, WIDTH))]
        + [_sds(a.shape) for a in lead] + [_sds(lead[0].shape), _sds(nw.shape)],
        scratch_shapes=[pltpu.VMEM((b * HEADS, DH, DH), F32)])


def _gd_local_vjp(inv_rows, xx, ab, cw, alog, dtb):
    nb = ab.shape[0] // CHUNK
    inv = jnp.stack([inv_rows[g * CHUNK:(g + 1) * CHUNK] for g in range(nb * HEADS)], axis=0)
    _, vjp, _ = jax.vjp(lambda *a: gd_local(*a, inverse=_saved_inverse(inv)), xx, ab, cw, alog, dtb, has_aux=True)
    return vjp


def gd_local_bwd(p, p0, ab, ab0, cw, alog, dtb, inv, inv0, cot, dz, cot0, dz0):
    b, seq, _ = p.shape
    rows = LOCAL_CHUNKS * CHUNK
    ng = seq // rows
    du, dw, dqe, dke, dqk, dea = cot

    def body(p_ref, halo_ref, p0_ref, ab_ref, ab0_ref, cw_ref, al_ref, dt_ref, inv_ref, inv0_ref, du_ref, dw_ref, dqe_ref,
             dke_ref, dqk_ref, dea_ref, dz_ref, du0_ref, dw0_ref, dqe0_ref, dke0_ref, dqk0_ref, dea0_ref, dz0_ref,
             dp_ref, dab_ref, dp0_ref, dab0_ref, dcw_ref, dal_ref, ddt_ref, dhalo, dtail):
        s, i = pl.program_id(0), pl.program_id(1)
        g = ng - 1 - i

        @pl.when(i == 0)
        def _():
            dhalo[...] = jnp.zeros_like(dhalo)

        @pl.when((s == 0) & (i == 0))
        def _():
            dtail[...] = jnp.zeros_like(dtail)
            dcw_ref[...] = jnp.zeros_like(dcw_ref)
            dal_ref[...] = jnp.zeros_like(dal_ref)
            ddt_ref[...] = jnp.zeros_like(ddt_ref)

        def finish(dxx, dab, dcw, dal, ddt, before, n, dz_val, dp_out, dab_out):
            dqkv = dxx[HALO:HALO + n] + jnp.concatenate([jnp.zeros((n - HALO, QKV), F32), before], axis=0)
            dp_out[...] = jnp.concatenate([dqkv, dz_val], axis=1).astype(MXU_DTYPE)
            dab_out[...] = dab.astype(MXU_DTYPE)
            dcw_ref[...] += dcw
            dal_ref[...] += dal
            ddt_ref[...] += ddt

        inv_rows = jnp.concatenate([inv_ref[c] for c in range(LOCAL_CHUNKS)], axis=0)
        vjp = _gd_local_vjp(inv_rows, _gd_window(g, p_ref, halo_ref, p0_ref), ab_ref[...], cw_ref[...], al_ref[...], dt_ref[...])
        dqk_all = jnp.concatenate([dqk_ref[c] for c in range(LOCAL_CHUNKS)], axis=0)
        deas = tuple(dea_ref[c] for c in range(LOCAL_CHUNKS))
        grads = vjp((du_ref[...], dw_ref[...], dqe_ref[...], dke_ref[...], dqk_all, deas))
        finish(*grads, dhalo[...], rows, dz_ref[...], dp_ref, dab_ref)
        dhalo[...] = grads[0][0:HALO]

        @pl.when(g == 0)
        def _():
            dtail[...] += grads[0][0:HALO]

        @pl.when((s == b - 1) & (g == 0))
        def _():
            vjp0 = _gd_local_vjp(inv0_ref[...], _lead_window(p0_ref), ab0_ref[...], cw_ref[...], al_ref[...], dt_ref[...])
            grads0 = vjp0((du0_ref[...], dw0_ref[...], dqe0_ref[...], dke0_ref[...], dqk0_ref[...], (dea0_ref[...],)))
            finish(*grads0, dtail[...], CHUNK, dz0_ref[...], dp0_ref, dab0_ref)

    rg = lambda i: ng - 1 - i
    const = lambda a: pl.BlockSpec(a.shape, lambda s, i: (0, 0))
    slab = pl.BlockSpec((None, rows, WIDTH), lambda s, i: (s, rg(i), 0))
    wide = pl.BlockSpec((None, rows, 4 * WIDTH), lambda s, i: (s, rg(i), 0))
    gates = pl.BlockSpec((None, rows, AB_PAD), lambda s, i: (s, rg(i), 0))
    mats = pl.BlockSpec((None, LOCAL_CHUNKS, HEADS * CHUNK, CHUNK), lambda s, i: (s, rg(i), 0, 0))
    return pl.pallas_call(
        body, grid=(b, ng), name="gdn_local_bwd",
        in_specs=[wide, pl.BlockSpec((None, HALO, QKV), lambda s, i: (s, _halo_block(rg(i)), 0)), const(p0), gates, const(ab0),
                  const(cw), const(alog), const(dtb), mats, const(inv0), slab, slab, slab, slab, mats,
                  pl.BlockSpec((None, LOCAL_CHUNKS, 1, AB_PAD), lambda s, i: (s, rg(i), 0, 0)), slab]
        + [const(a) for a in cot0] + [const(dz0)],
        out_specs=[wide, gates, const(p0), const(ab0), const(cw), const(alog), const(dtb)],
        out_shape=[_sds(p.shape, MXU_DTYPE), _sds(ab.shape, MXU_DTYPE), _sds(p0.shape, MXU_DTYPE), _sds(ab0.shape, MXU_DTYPE),
                   _sds(cw.shape), _sds(alog.shape), _sds(dtb.shape)],
        scratch_shapes=[pltpu.VMEM((HALO, QKV), F32), pltpu.VMEM((HALO, QKV), F32)],
        compiler_params=_cparams("arbitrary", "arbitrary"),
    )(p, p, p0, ab, ab0, cw, alog, dtb, inv, inv0, du, dw, dqe, dke, dqk, dea, dz, *cot0, dz0)


def _position():
    return lax.axis_index("x"), lax.axis_index("y"), lax.axis_index("c")


EXCHANGE_COPIES = 10


def _exchange_blocks(bufs, send_sems, recv_sems):
    x, y, c = _position()
    here, x_nbr, y_nbr, diag = (x, y), (1 - x, y), (x, 1 - y), (1 - x, 1 - y)
    sibling = (x, y, 1 - c)
    me = (x, y, c)
    n = range(len(bufs))

    def rows(a, chip, core, half=None):
        block = bufs[a].at[4 * chip[0] + 2 * chip[1] + core]
        if half is None:
            return block
        total = bufs[a].shape[1]
        tile = 8 * (4 // jnp.dtype(bufs[a].dtype).itemsize)
        split = total // 2 // tile * tile
        return block.at[pl.ds(0, split)] if half == 0 else block.at[pl.ds(split, total - split)]

    def copy(a, k, region, to):
        return pltpu.make_async_remote_copy(src_ref=region, dst_ref=region, send_sem=send_sems.at[a * EXCHANGE_COPIES + k],
                                            recv_sem=recv_sems.at[a * EXCHANGE_COPIES + k], device_id=to, device_id_type=MESH)

    sent = [copy(a, 0, rows(a, here, c), sibling) for a in n]
    sent += [cp for a in n for cp in (copy(a, 1, rows(a, here, c, 0), (*x_nbr, c)), copy(a, 4, rows(a, here, c, 1), (*y_nbr, c)))]
    sent += [cp for a in n for cp in (copy(a, 2, rows(a, here, c, 1), (*x_nbr, c)), copy(a, 3, rows(a, here, c, 0), (*y_nbr, c)))]
    for cp in sent:
        cp.start()

    def after(arrivals, a, k, region, to):
        for cp in arrivals:
            cp.wait_recv()
        sent.append(copy(a, k, region, to))
        sent[-1].start()

    for a in n:
        after([copy(a, 1, rows(a, x_nbr, c, 0), me)], a, 5, rows(a, x_nbr, c, 0), (*y_nbr, c))
        after([copy(a, 4, rows(a, y_nbr, c, 1), me)], a, 6, rows(a, y_nbr, c, 1), (*x_nbr, c))
    for a in n:
        after([copy(a, 2, rows(a, x_nbr, c, 1), me)], a, 7, rows(a, x_nbr, c), sibling)
        after([copy(a, 3, rows(a, y_nbr, c, 0), me)], a, 8, rows(a, y_nbr, c), sibling)
    for a in n:
        after([copy(a, 5, rows(a, diag, c, 0), me), copy(a, 6, rows(a, diag, c, 1), me)], a, 9, rows(a, diag, c), sibling)
    for a in n:
        copy(a, 0, rows(a, here, 1 - c), me).wait_recv()
        for k, chip in ((7, x_nbr), (8, y_nbr), (9, diag)):
            copy(a, k, rows(a, chip, 1 - c), me).wait_recv()
    for cp in sent:
        cp.wait_send()


def _exchange_sems(n_bufs):
    return [pltpu.SemaphoreType.DMA((n_bufs * EXCHANGE_COPIES,)), pltpu.SemaphoreType.DMA((n_bufs * EXCHANGE_COPIES,))]


def gather_weights(w_in_t, w_out, small, pad_rows):
    rows, _, cols = w_in_t.shape
    buf_rows = -(-rows // ROW_TILE_BF16) * ROW_TILE_BF16

    def body(wi_ref, wo_ref, sm_ref, wi_out, wo_out, sm_out, wi_buf, send_sems, recv_sems):
        x, y, c = _position()
        me = 4 * x + 2 * y + c
        wi_buf[me, pl.ds(0, rows), :] = wi_ref[:, 0, :].astype(MXU_DTYPE)
        wi_buf[me, pl.ds(rows, buf_rows - rows), :] = jnp.zeros((buf_rows - rows, cols), MXU_DTYPE)
        wo_out[me] = wo_ref[...].astype(MXU_DTYPE)
        sm_out[me] = sm_ref[...]
        _exchange_blocks([wi_buf, wo_out, sm_out], send_sems, recv_sems)
        for d in range(N_DEV):
            wi_out[pl.ds(d * rows, rows), :] = wi_buf[d, pl.ds(0, rows), :]
        wi_out[pl.ds(N_DEV * rows, pad_rows), :] = jnp.zeros((pad_rows, cols), MXU_DTYPE)

    return pl.pallas_call(
        body, name="gather_weights", in_specs=[VMEM_SPEC] * 3, out_specs=[VMEM_SPEC] * 3,
        out_shape=[jax.ShapeDtypeStruct((N_DEV * rows + pad_rows, cols), MXU_DTYPE),
                   jax.ShapeDtypeStruct((N_DEV,) + w_out.shape, MXU_DTYPE), jax.ShapeDtypeStruct((N_DEV,) + small.shape, F32)],
        scratch_shapes=[pltpu.VMEM((N_DEV, buf_rows, cols), MXU_DTYPE)] + _exchange_sems(3),
        compiler_params=pltpu.CompilerParams(vmem_limit_bytes=VMEM_LIMIT))(w_in_t, w_out, small)


HOPS = 6


def reduce_gradients(tensors, small, name):
    n_t = len(tensors)
    arrays = [a for parts, _ in tensors for a, _ in parts]
    first_array = [sum(len(parts) for parts, _ in tensors[:t]) for t in range(n_t)]

    def pieces(t, j):
        parts, block_rows = tensors[t]
        out, base = [], 0
        for pi, (_, valid) in enumerate(parts):
            lo, hi = max(j * block_rows, base), min((j + 1) * block_rows, base + valid)
            if lo < hi:
                out.append((first_array[t] + pi, lo - base, lo - j * block_rows, hi - lo))
            base += valid
        return out

    def body(*refs):
        n_a = len(arrays)
        in_refs, small_ref = refs[:n_a], refs[n_a]
        out_refs, small_sum = refs[n_a + 1:n_a + 1 + n_t], refs[n_a + 1 + n_t]
        bufs, small_buf = refs[n_a + 2 + n_t:n_a + 2 + 5 * n_t], refs[n_a + 2 + 5 * n_t]
        s1_sems, r1_sems, s2_sems, r2_sems, small_send, small_recv = refs[n_a + 3 + 5 * n_t:]
        x, y, c = _position()
        chip = 2 * x + y

        def put(t, dst, j, add=None):
            for ai, src_row, dst_row, size in pieces(t, j):
                v = in_refs[ai][pl.ds(src_row, size), :]
                if add is not None:
                    v = v + add[pl.ds(dst_row, size), :].astype(F32)
                dst[pl.ds(dst_row, size), :] = v.astype(dst.dtype)

        def swap(t, k):
            send1, recv1 = bufs[4 * t], bufs[4 * t + 1]
            return pltpu.make_async_remote_copy(src_ref=send1.at[k], dst_ref=recv1.at[k], send_sem=s1_sems.at[4 * t + k],
                                                recv_sem=r1_sems.at[4 * t + k], device_id=(x, y, 1 - c), device_id_type=MESH)

        to_x, to_y, to_diag = 2 * (1 - x) + y, 2 * x + (1 - y), 2 * (1 - x) + (1 - y)
        x_dev, y_dev = (1 - x, y, c), (x, 1 - y, c)

        def half(ref, h):
            total = ref.shape[0]
            split = total // 2 // ROW_TILE_BF16 * ROW_TILE_BF16
            return ref.at[pl.ds(0, split)] if h == 0 else ref.at[pl.ds(split, total - split)]

        def hop(t, copy_id, src, dst, to):
            return pltpu.make_async_remote_copy(src_ref=src, dst_ref=dst, send_sem=s2_sems.at[HOPS * t + copy_id],
                                                recv_sem=r2_sems.at[HOPS * t + copy_id], device_id=to, device_id_type=MESH)

        def hops(t):
            send2, landing = bufs[4 * t + 2], bufs[4 * t + 3]
            return [hop(t, 0, half(send2.at[to_diag], 0), half(landing.at[0], 0), x_dev),
                    hop(t, 1, half(send2.at[to_diag], 1), half(landing.at[0], 1), y_dev),
                    hop(t, 2, half(send2.at[to_x], 0), half(landing.at[1], 0), x_dev),
                    hop(t, 3, half(send2.at[to_y], 1), half(landing.at[2], 1), y_dev),
                    hop(t, 4, half(send2.at[to_x], 1), half(landing.at[1], 1), x_dev),
                    hop(t, 5, half(send2.at[to_y], 0), half(landing.at[2], 0), y_dev)]

        def add_relay(t, slot, h):
            dst, src = half(bufs[4 * t + 2].at[slot], h), half(bufs[4 * t + 3].at[0], h)
            dst[...] = (dst[...].astype(F32) + src[...].astype(F32)).astype(dst.dtype)

        for t in range(n_t):
            send2 = bufs[4 * t + 2]
            pad = send2.shape[1] - tensors[t][1]
            if pad:
                send2[:, pl.ds(tensors[t][1], pad), :] = jnp.zeros((4, pad, send2.shape[2]), send2.dtype)
            for j in range(N_DEV):
                @pl.when((j & 1) != c)
                def _():
                    put(t, bufs[4 * t].at[j >> 1], j)
            for k in range(4):
                swap(t, k).start()

        small_buf[4 * x + 2 * y + c] = small_ref[...]
        _exchange_blocks([small_buf], small_send, small_recv)
        total = small_buf[0]
        for d in range(1, N_DEV):
            total = total + small_buf[d]
        small_sum[...] = total

        for t in range(n_t):
            recv1 = bufs[4 * t + 1]
            for k in range(4):
                swap(t, k).wait_recv()
                for j in (2 * k, 2 * k + 1):
                    @pl.when(((j & 1) == c) & (k != chip))
                    def _():
                        put(t, bufs[4 * t + 2].at[k], j, add=recv1.at[k])

                    @pl.when(((j & 1) == c) & (k == chip))
                    def _():
                        put(t, out_refs[t], j, add=recv1.at[k])
            for cp in hops(t)[0:4]:
                cp.start()

        for t in range(n_t):
            cps = hops(t)
            cps[0].wait_recv()
            add_relay(t, to_y, 0)
            cps[5].start()
            cps[1].wait_recv()
            add_relay(t, to_x, 1)
            cps[4].start()

        for t in range(n_t):
            cps, rows = hops(t), tensors[t][1]
            for first, second, slot in ((cps[2], cps[4], 1), (cps[3], cps[5], 2)):
                first.wait_recv()
                second.wait_recv()
                out_refs[t][...] += bufs[4 * t + 3][slot, pl.ds(0, rows), :].astype(F32)

        for t in range(n_t):
            for cp in hops(t):
                cp.wait_send()
            for k in range(4):
                swap(t, k).wait_send()

    scratch, out_shape = [], []
    for parts, block_rows in tensors:
        cols = parts[0][0].shape[1]
        tiled_rows = -(-block_rows // ROW_TILE_BF16) * ROW_TILE_BF16
        scratch += [pltpu.VMEM((4, block_rows, cols), MXU_DTYPE)] * 2
        scratch += [pltpu.VMEM((4, tiled_rows, cols), MXU_DTYPE), pltpu.VMEM((3, tiled_rows, cols), MXU_DTYPE)]
        out_shape.append(jax.ShapeDtypeStruct((block_rows, cols), F32))
    out_shape.append(jax.ShapeDtypeStruct(small.shape, F32))
    scratch += [pltpu.VMEM((N_DEV,) + small.shape, F32)] + [pltpu.SemaphoreType.DMA((4 * n_t,))] * 2
    scratch += [pltpu.SemaphoreType.DMA((HOPS * n_t,))] * 2 + _exchange_sems(1)
    return pl.pallas_call(
        body, name=name, in_specs=[VMEM_SPEC] * (len(arrays) + 1), out_specs=[VMEM_SPEC] * (n_t + 1), out_shape=out_shape,
        scratch_shapes=scratch, compiler_params=pltpu.CompilerParams(vmem_limit_bytes=VMEM_LIMIT),
    )(*arrays, small)


def _adamw_step(w, g, m, v):
    mn = ADAM_B1 * m + (1.0 - ADAM_B1) * g
    vn = ADAM_B2 * v + (1.0 - ADAM_B2) * jnp.square(g)
    m_hat = mn / (1.0 - ADAM_B1 ** ADAM_STEP)
    v_hat = vn / (1.0 - ADAM_B2 ** ADAM_STEP)
    return -ADAM_LR * (m_hat / (jnp.sqrt(v_hat) + ADAM_EPS) + ADAM_WD * w), mn, vn


def adamw(w, g, m, v, name):
    rows, cols = w.shape
    tr = 256 if rows % 256 == 0 else rows

    def body(w_ref, g_ref, m_ref, v_ref, d_ref, nm_ref, nv_ref):
        d_ref[...], nm_ref[...], nv_ref[...] = _adamw_step(w_ref[...], g_ref[...], m_ref[...], v_ref[...])

    spec = pl.BlockSpec((tr, cols), lambda i: (i, 0))
    shape = jax.ShapeDtypeStruct((rows, cols), F32)
    return pl.pallas_call(body, grid=(rows // tr,), name=name, in_specs=[spec] * 4, out_specs=[spec] * 3,
                          out_shape=[shape] * 3, compiler_params=_cparams("arbitrary"))(w, g, m, v)


def adamw_w_in(w, g_t, m, v):
    def body(w_ref, g_ref, m_ref, v_ref, go_ref, d_ref, nm_ref, nv_ref):
        g = g_ref[...]
        go_ref[:, 0, :] = g
        d_ref[:, 0, :], nm_ref[:, 0, :], nv_ref[:, 0, :] = _adamw_step(w_ref[:, 0, :], g, m_ref[:, 0, :], v_ref[:, 0, :])

    return pl.pallas_call(body, name="adamw_w_in", in_specs=[VMEM_SPEC] * 4, out_specs=[VMEM_SPEC] * 4,
                          out_shape=[jax.ShapeDtypeStruct(w.shape, F32)] * 4,
                          compiler_params=pltpu.CompilerParams(vmem_limit_bytes=VMEM_LIMIT))(w, g_t, m, v)


def _pad_rows(a, rows=8):
    return jnp.pad(a, ((0, rows - a.shape[0]), (0, 0)))


def _pad_lanes(a, lanes=128):
    return jnp.pad(a, ((0, 0), (0, lanes - a.shape[1])))


def kernel(x, meta_tokens, norm_w, w_in, conv_w, hg_lb_logits, hg_norm_w, gdn_A_log, gdn_dt_bias, gdn_norm_w, w_out, final_norm_w, loss_target, m_meta_tokens, m_norm_w, m_w_in, m_conv_w, m_hg_lb_logits, m_hg_norm_w, m_gdn_A_log, m_gdn_dt_bias, m_gdn_norm_w, m_w_out, m_final_norm_w, v_meta_tokens, v_norm_w, v_w_in, v_conv_w, v_hg_lb_logits, v_hg_norm_w, v_gdn_A_log, v_gdn_dt_bias, v_gdn_norm_w, v_w_out, v_final_norm_w):
    b, seq, _ = x.shape
    n = b * seq
    dev = 4 * lax.axis_index("x") + 2 * lax.axis_index("y") + lax.axis_index("c")
    col_shard = IN_COLS // N_DEV

    small_w = jnp.concatenate([_pad_lanes(meta_tokens, 256), _pad_rows(_pad_lanes(conv_w[0], 256))], axis=0)
    w_t, w_out_g, small_g = gather_weights(jnp.transpose(w_in, (2, 0, 1)), w_out[0], small_w, AB_PAD - 2 * HEADS)
    meta_g = small_g[:, 0:N_META, 0:D_MODEL // N_DEV]
    conv_g = small_g[:, N_META:N_META + CONV_TAPS, 0:QKV // N_DEV]
    w_out_full = w_out_g.reshape(2 * WIDTH, D_MODEL)
    cw = jnp.transpose(conv_g, (1, 0, 2)).reshape(CONV_TAPS, QKV)
    meta = jnp.transpose(meta_g, (1, 0, 2)).reshape(N_META, D_MODEL)
    alog = _pad_lanes(gdn_A_log)
    dtb = _pad_lanes(gdn_dt_bias)
    fw = final_norm_w.reshape(1, D_MODEL)

    h0 = jnp.concatenate([jnp.zeros((CHUNK - N_META, D_MODEL), F32), meta], axis=0)
    x2 = x.reshape(n, D_MODEL)
    phg, pgd, pab, phg0, pgd0, pab0, u0 = in_proj(x2, h0, norm_w, w_t)
    phg3, pgd3, pab3 = phg.reshape(b, seq, 4 * WIDTH), pgd.reshape(b, seq, 4 * WIDTH), pab.reshape(b, seq, AB_PAD)
    nc = seq // (SCAN_CHUNKS * CHUNK)
    hg_loc, hg_lead = hg_local_fwd(phg3, phg0, hg_lb_logits)
    gd_loc, gd_inv, gd_lead, gd_inv0 = gd_local_fwd(pgd3, pgd0, pab3, pab0, cw, alog, dtb)
    (y_hg, s_hg), (y_gd, s_gd) = run_scans([hg_scan_fwd(phg3, phg0, hg_loc, hg_lead, hg_norm_w),
                                            gd_scan_fwd(pgd3, pgd0, gd_loc, gd_lead, gdn_norm_w)], nc, "scans")

    dh2, dy_hg, dy_gd, g_w_out, loss_part, g_fw = out_proj_loss(
        x2, loss_target.reshape(n, D_MODEL), y_hg.reshape(n, WIDTH), y_gd.reshape(n, WIDTH), w_out_full, fw)

    hb, gb = run_scans([hg_scan_bwd(phg3, phg0, hg_loc, hg_lead, hg_norm_w, s_hg, dy_hg.reshape(b, seq, WIDTH)),
                        gd_scan_bwd(pgd3, pgd0, gd_loc, gd_lead, gdn_norm_w, s_gd, dy_gd.reshape(b, seq, WIDTH))],
                       nc, "scans_bwd")
    dphg, dphg0, g_lb = hg_local_bwd(phg3, phg0, hg_lb_logits, hb[0:6], hb[6:12])
    g_hg_nw = hb[12]
    dpgd, dpab, dpgd0, dpab0, g_cw, g_alog, g_dtb = gd_local_bwd(pgd3, pgd0, pab3, pab0, cw, alog, dtb, gd_inv, gd_inv0,
                                                                 gb[0:6], gb[6], gb[7:13], gb[13])
    g_gd_nw = gb[14]
    dphg, dpgd, dpab = dphg.reshape(n, 4 * WIDTH), dpgd.reshape(n, 4 * WIDTH), dpab.reshape(n, AB_PAD)

    grad_x, dh0, g_nw, g_w_hg, g_w_gd, g_w_ab = in_proj_bwd(dphg, dpgd, dpab, w_t, x2, dh2, norm_w, h0, u0, dphg0, dpgd0, dpab0)

    small = jnp.concatenate([
        g_nw.reshape(8, 128), g_lb.reshape(8, 128), _pad_rows(g_hg_nw), _pad_rows(g_alog), _pad_rows(g_dtb), _pad_rows(g_gd_nw),
        g_fw.reshape(8, 128), g_cw.reshape(48, 128),
        dh0[CHUNK - N_META:CHUNK].reshape(128, 128), loss_part], axis=0)
    g_w_in_t, g_w_out, small = reduce_gradients(
        [([(g_w_hg, 4 * WIDTH), (g_w_gd, 4 * WIDTH), (g_w_ab, 2 * HEADS)], col_shard),
         ([(g_w_out, 2 * WIDTH)], (2 * WIDTH) // N_DEV)], small, "reduce_gradients")
    g_norm_w = small[0:8].reshape(1, D_MODEL)
    g_lb = small[8:16].reshape(2, WIDTH)
    g_hg_nw = small[16:17]
    g_alog = small[24:25, 0:HEADS]
    g_dtb = small[32:33, 0:HEADS]
    g_gd_nw = small[40:41]
    g_fw = small[48:56].reshape(1, D_MODEL)
    g_cw_full = small[56:104].reshape(CONV_TAPS, QKV)
    g_meta_full = small[104:232].reshape(N_META, D_MODEL)
    loss = small[232, 0]
    g_conv = lax.dynamic_slice_in_dim(g_cw_full, dev * (QKV // N_DEV), QKV // N_DEV, axis=1)
    g_meta = lax.dynamic_slice_in_dim(g_meta_full, dev * (D_MODEL // N_DEV), D_MODEL // N_DEV, axis=1)

    names = ["meta_tokens", "norm_w", "w_in", "conv_w", "hg_lb_logits", "hg_norm_w", "gdn_A_log", "gdn_dt_bias",
             "gdn_norm_w", "w_out", "final_norm_w"]
    weights = [meta_tokens, norm_w, w_in, conv_w, hg_lb_logits, hg_norm_w, gdn_A_log, gdn_dt_bias, gdn_norm_w, w_out,
               final_norm_w]
    moms = [m_meta_tokens, m_norm_w, m_w_in, m_conv_w, m_hg_lb_logits, m_hg_norm_w, m_gdn_A_log, m_gdn_dt_bias,
            m_gdn_norm_w, m_w_out, m_final_norm_w]
    vars_ = [v_meta_tokens, v_norm_w, v_w_in, v_conv_w, v_hg_lb_logits, v_hg_norm_w, v_gdn_A_log, v_gdn_dt_bias,
             v_gdn_norm_w, v_w_out, v_final_norm_w]
    grads2d = [g_meta, g_norm_w, g_w_in_t, g_conv, g_lb, g_hg_nw, g_alog, g_dtb, g_gd_nw, g_w_out, g_fw]
    grads, deltas, new_ms, new_vs = [], [], [], []
    for nm, w, g2, m, v in zip(names, weights, grads2d, moms, vars_):
        if nm == "w_in":
            to3, back = (lambda a: jnp.transpose(a, (2, 0, 1))), (lambda a: jnp.transpose(a, (1, 2, 0)))
            g2, d, nm_, nv_ = adamw_w_in(to3(w), g2, to3(m), to3(v))
        else:
            to2d, back = (lambda a, s=g2.shape: a.reshape(s)), (lambda a, s=w.shape: a.reshape(s))
            d, nm_, nv_ = adamw(to2d(w), g2, to2d(m), to2d(v), "adamw_" + nm)
        grads.append(back(g2))
        deltas.append(back(d))
        new_ms.append(back(nm_))
        new_vs.append(back(nv_))
    return (loss, grad_x.reshape(x.shape), *grads, *deltas, *new_ms, *new_vs)
```

```python
import jax
import jax.numpy as jnp
from jax import lax
from jax.experimental import pallas as pl
from jax.experimental.pallas import tpu as pltpu

F32 = jnp.float32
BF16 = jnp.bfloat16
MXU_DTYPE = BF16

D_MODEL = 1024
N_META = 16
CHUNK = 64
SUB = 16
ROW_TILE_BF16 = 16
HEADS = 4
DH = 128
WIDTH = HEADS * DH
QKV = 3 * WIDTH
CONV_TAPS = 4
HALO = 8
EPS = 1e-6
IN_COLS = 4 * WIDTH + 4 * WIDTH + 2 * HEADS
AB_PAD = 128
N_DEV = 8
LOCAL_CHUNKS = 4
SCAN_CHUNKS_FWD = 4
SCAN_CHUNKS = 2
VMEM_LIMIT = 56 * 1024 * 1024
VMEM_LIMIT_LARGE = 60 * 1024 * 1024

ADAM_LR = 0.001
ADAM_B1 = 0.9
ADAM_B2 = 0.999
ADAM_EPS = 1e-08
ADAM_WD = 0.01
ADAM_STEP = 10

VMEM_SPEC = pl.BlockSpec(memory_space=pltpu.VMEM)
MESH = pl.DeviceIdType.MESH


def _mm_tn(a, b):
    return lax.dot_general(a.astype(MXU_DTYPE), b.astype(MXU_DTYPE), (((0,), (0,)), ((), ())), preferred_element_type=F32)


def _bmm(a, b):
    return lax.dot_general(a.astype(MXU_DTYPE), b.astype(MXU_DTYPE), (((2,), (1,)), ((0,), (0,))), preferred_element_type=F32)


def _bmm_nt(a, b):
    return lax.dot_general(a.astype(MXU_DTYPE), b.astype(MXU_DTYPE), (((2,), (2,)), ((0,), (0,))), preferred_element_type=F32)


def _bmm_tn(a, b):
    return lax.dot_general(a.astype(MXU_DTYPE), b.astype(MXU_DTYPE), (((1,), (1,)), ((0,), (0,))), preferred_element_type=F32)


def _iota2(n, m):
    return lax.broadcasted_iota(jnp.int32, (n, m), 0), lax.broadcasted_iota(jnp.int32, (n, m), 1)


def _silu(x):
    return x * jax.nn.sigmoid(x)


def _gated_norm(o, z, nw):
    return o * lax.rsqrt(jnp.mean(o * o, axis=-1, keepdims=True) + EPS) * nw * _silu(z)


def _heads(a, nb):
    return jnp.stack([a[c * CHUNK:(c + 1) * CHUNK, h * DH:(h + 1) * DH] for c in range(nb) for h in range(HEADS)], axis=0)


def _unheads(a3, nb):
    return jnp.concatenate(
        [jnp.concatenate([a3[c * HEADS + h] for h in range(HEADS)], axis=1) for c in range(nb)], axis=0)


def _split3(x):
    hi = x.astype(BF16)
    r1 = x - hi.astype(F32)
    mid = r1.astype(BF16)
    return hi, mid, (r1 - mid.astype(F32)).astype(BF16)


def _select_rows(pattern, n_out, x):
    def forward(v):
        r, c = _iota2(n_out, 3 * CHUNK)
        s = jnp.where(pattern(r, c & (CHUNK - 1)), 1.0, 0.0).astype(BF16)
        return jnp.dot(s, jnp.concatenate(_split3(v), axis=0), preferred_element_type=F32)

    def backward(_, d):
        r, c = _iota2(CHUNK, 2 * n_out)
        s_t = jnp.where(pattern(c - jnp.where(c >= n_out, n_out, 0), r), 1.0, 0.0).astype(BF16)
        hi = d.astype(BF16)
        return (jnp.dot(s_t, jnp.concatenate([hi, (d - hi.astype(F32)).astype(BF16)], axis=0), preferred_element_type=F32),)

    apply = jax.custom_vjp(forward)
    apply.defvjp(lambda v: (forward(v), None), backward)
    return apply(x)


def _cumsum_chunks(x, nb):
    return jnp.concatenate([_select_rows(lambda i, j: j <= i, CHUNK, x[c * CHUNK:(c + 1) * CHUNK]) for c in range(nb)], axis=0)


HG_LEVELS = 6


def _hg_sums(i, j):
    lvl, t = i >> HG_LEVELS, i & (CHUNK - 1)
    last = t
    for l in range(1, HG_LEVELS + 1):
        width = HG_LEVELS + 1 - l
        last = jnp.where(lvl == l, ((t >> width) << width) + (CHUNK >> l) - 1, last)
    return j <= last


def hg_local(p, logits):
    nb = p.shape[0] // CHUNK
    l0, l1 = logits[0:1], logits[1:2]
    mx = jnp.maximum(l0, l1)
    e0, e1 = jnp.exp(l0 - mx), jnp.exp(l1 - mx)
    lb = e0 / (e0 + e1)
    q = _silu(p[:, 0:WIDTH])
    f = lb + (1.0 - lb) * jax.nn.sigmoid(p[:, WIDTH:2 * WIDTH])
    k = 1.0 - f
    logf = jnp.log(f)
    sums = [_select_rows(_hg_sums, (HG_LEVELS + 1) * CHUNK, logf[c * CHUNK:(c + 1) * CHUNK]) for c in range(nb)]
    level = lambda l: _heads(jnp.concatenate([s[l * CHUNK:(l + 1) * CHUNK] for s in sums], axis=0), nb)
    q3, k3, v3, g3 = _heads(q, nb), _heads(k, nb), _heads(p[:, 2 * WIDTH:3 * WIDTH], nb), level(0)
    r, c = _iota2(CHUNK, CHUNK)
    row = lax.broadcasted_iota(jnp.int32, (CHUNK, DH), 0)
    a = jnp.where(r == c, _bmm_nt(q3, k3), 0.0)
    for l in range(1, HG_LEVELS + 1):
        sh = HG_LEVELS - l
        qk = jnp.where(((row >> sh) & 1) == 1, q3, k3) * jnp.exp(-jnp.abs(g3 - level(l)))
        pair = ((r >> (sh + 1)) == (c >> (sh + 1))) & (((r >> sh) & 1) == 1) & (((c >> sh) & 1) == 0)
        a = a + jnp.where(pair, _bmm_nt(qk, qk), 0.0)
    o = _bmm(a, v3)
    glast = g3[:, CHUNK - 1:CHUNK, :]
    egs = tuple(jnp.concatenate([jnp.exp(glast[c * HEADS + h]) for h in range(HEADS)], axis=1) for c in range(nb))
    return _unheads(q3 * jnp.exp(g3), nb), _unheads(k3 * jnp.exp(glast - g3), nb), _unheads(o, nb), egs


def hg_scan(q_in, k_out, v, eg, o_intra, z, nw, st):
    o = o_intra + _bmm_nt(q_in, st)
    return _gated_norm(o, z, nw), st * eg + _bmm_tn(v, k_out)


def _tri_y_impl(a):
    r, c = _iota2(CHUNK, CHUNK)
    same16 = (r // SUB) == (c // SUB)
    same32 = (r // (2 * SUB)) == (c // (2 * SUB))
    a0 = jnp.where(same16, a, 0.0)
    y = -a0
    pw = _bmm(a0, a0)
    for _ in range(2):
        y = y + pw + _bmm(y, pw)
        pw = _bmm(pw, pw)
    y = y + pw + _bmm(y, pw)
    for ak in (jnp.where(same32 & jnp.logical_not(same16), a, 0.0), jnp.where(same32, 0.0, a)):
        m = ak + _bmm(y, ak)
        y = y - (m + _bmm(m, y))
    return y


@jax.custom_vjp
def _tri_y(a):
    return _tri_y_impl(a)


def _tri_y_fwd(a):
    y = _tri_y_impl(a)
    return y, y


def _tri_y_bwd(y, dy):
    n = dy + _bmm_tn(y, dy)
    return (-(n + _bmm_nt(n, y)),)


_tri_y.defvjp(_tri_y_fwd, _tri_y_bwd)


def _saved_inverse(y):
    @jax.custom_vjp
    def inverse(a):
        return y

    inverse.defvjp(lambda a: (y, None), lambda _, dy: _tri_y_bwd(y, dy))
    return inverse


def _head_rows(a3, nb):
    return jnp.concatenate([a3[g] for g in range(nb * HEADS)], axis=0)


def _rows_down(x, s):
    rows = x.shape[0]

    @jax.custom_vjp
    def rotate(v):
        return pltpu.roll(v, s, 0)

    rotate.defvjp(lambda v: (pltpu.roll(v, s, 0), None), lambda _, d: (pltpu.roll(d, rows - s, 0),))
    return rotate(x)


def gd_local(xx, ab, cw, alog, dtb, inverse=_tri_y):
    n = ab.shape[0]
    nb = n // CHUNK
    conv = cw[CONV_TAPS - 1:CONV_TAPS] * xx[HALO:HALO + n]
    for j in range(CONV_TAPS - 1):
        conv = conv + cw[j:j + 1] * _rows_down(xx, CONV_TAPS - 1 - j)[HALO:HALO + n]
    act = _silu(conv)
    x = ab + dtb
    g_all = -jnp.exp(alog) * (jnp.maximum(x, 0.0) + jnp.log1p(jnp.exp(-jnp.abs(x))))
    beta_all = jax.nn.sigmoid(ab)
    gam_all = _cumsum_chunks(g_all, nb)
    q3, k3, v3 = _heads(act[:, 0:WIDTH], nb), _heads(act[:, WIDTH:2 * WIDTH], nb), _heads(act[:, 2 * WIDTH:QKV], nb)
    q3 = q3 * lax.rsqrt(jnp.sum(q3 * q3, axis=-1, keepdims=True) + EPS) * (DH ** -0.5)
    k3 = k3 * lax.rsqrt(jnp.sum(k3 * k3, axis=-1, keepdims=True) + EPS)
    pairs = [(c, h) for c in range(nb) for h in range(HEADS)]
    beta = jnp.stack([beta_all[c * CHUNK:(c + 1) * CHUNK, HEADS + h:HEADS + h + 1] for c, h in pairs], axis=0)
    gam = jnp.stack([gam_all[c * CHUNK:(c + 1) * CHUNK, h:h + 1] for c, h in pairs], axis=0)
    gam_t = [gam_all[c * CHUNK:(c + 1) * CHUNK].T for c in range(nb)]
    gam_row = jnp.stack([gam_t[c][h:h + 1, :] for c, h in pairs], axis=0)
    glast = gam[:, CHUNK - 1:CHUNK, :]
    r, c = _iota2(CHUNK, CHUNK)
    dec = jnp.exp(jnp.where(c < r, gam - gam_row, -jnp.inf))
    y = inverse(beta * _bmm_nt(k3, k3) * dec)
    eg = jnp.exp(gam)
    rhs = jnp.concatenate([beta * v3, (beta * eg) * k3], axis=2)
    sol = rhs + _bmm(y, rhs)
    qk = _bmm_nt(q3, k3) * jnp.where(r == c, 1.0, dec)
    eas = tuple(jnp.exp(gam_all[(c + 1) * CHUNK - 1:(c + 1) * CHUNK]) for c in range(nb))
    return (_unheads(sol[:, :, 0:DH], nb), _unheads(sol[:, :, DH:2 * DH], nb), _unheads(q3 * eg, nb),
            _unheads(k3 * jnp.exp(glast - gam), nb), _head_rows(qk, nb), eas), _head_rows(y, nb)


def gd_scan(uu, ww, qe, ke, qk, ea, z, nw, s):
    u = uu - _bmm(ww, s)
    o = _bmm(qe, s) + _bmm(qk, u)
    return _gated_norm(o, z, nw), ea * s + _bmm_tn(ke, u)


def _cparams(*sem):
    return pltpu.CompilerParams(dimension_semantics=sem, vmem_limit_bytes=VMEM_LIMIT)


def _row_tile(n):
    for t in (512, 256, 128, 64):
        if n % t == 0:
            return t
    raise ValueError(f"unsupported token count {n}")


def _w_in_specs():
    once = pl.Buffered(1)
    return [pl.BlockSpec((4 * WIDTH, D_MODEL), lambda *i: (0, 0), pipeline_mode=once),
            pl.BlockSpec((4 * WIDTH, D_MODEL), lambda *i: (1, 0), pipeline_mode=once),
            pl.BlockSpec((AB_PAD, D_MODEL), lambda *i: (8 * WIDTH // AB_PAD, 0), pipeline_mode=once)]


def in_proj(h, h0, norm_w, w_t):
    n = h.shape[0]
    tm = _row_tile(n)
    nt = (((1,), (1,)), ((), ()))

    def body(h_ref, h0_ref, nw_ref, whg_ref, wgd_ref, wab_ref, phg_ref, pgd_ref, pab_ref, phg0_ref, pgd0_ref, pab0_ref, u0_ref):
        def project(x, hg_ref, gd_ref, ab_ref):
            u = (x * lax.rsqrt(jnp.mean(x * x, axis=-1, keepdims=True) + EPS) * nw_ref[...]).astype(MXU_DTYPE)
            hg_ref[...] = lax.dot_general(u, whg_ref[...], nt, preferred_element_type=F32)
            gd_ref[...] = lax.dot_general(u, wgd_ref[...], nt, preferred_element_type=F32)
            ab_ref[...] = lax.dot_general(u, wab_ref[...], nt, preferred_element_type=F32)
            return u

        @pl.when(pl.program_id(0) == 0)
        def _():
            u0_ref[...] = project(h0_ref[...], phg0_ref, pgd0_ref, pab0_ref)

        project(h_ref[...], phg_ref, pgd_ref, pab_ref)

    n0 = h0.shape[0]
    row = lambda w: pl.BlockSpec((tm, w), lambda i: (i, 0))
    lead = lambda w: pl.BlockSpec((n0, w), lambda i: (0, 0))
    widths = [4 * WIDTH, 4 * WIDTH, AB_PAD]
    return pl.pallas_call(
        body, grid=(n // tm,), name="in_proj",
        in_specs=[row(D_MODEL), lead(D_MODEL), pl.BlockSpec(norm_w.shape, lambda i: (0, 0))] + _w_in_specs(),
        out_specs=[row(w) for w in widths] + [lead(w) for w in widths] + [lead(D_MODEL)],
        out_shape=[jax.ShapeDtypeStruct((n, w), F32) for w in widths] + [jax.ShapeDtypeStruct((n0, w), F32) for w in widths]
        + [jax.ShapeDtypeStruct((n0, D_MODEL), MXU_DTYPE)],
        compiler_params=_cparams("arbitrary"),
    )(h, h0, norm_w, w_t, w_t, w_t)


def out_proj_loss(x, tgt, y_hg, y_gd, w_out, fw):
    n = x.shape[0]
    tm = 1024 if n % 1024 == 0 else _row_tile(n)
    inv_d = 1.0 / D_MODEL

    def body(x_ref, t_ref, yh_ref, yg_ref, w_ref, fw_ref, dh_ref, dyh_ref, dyg_ref, dw_ref, loss_ref, dfw_ref):
        @pl.when(pl.program_id(0) == 0)
        def _():
            dw_ref[...] = jnp.zeros_like(dw_ref)
            loss_ref[...] = jnp.zeros_like(loss_ref)
            dfw_ref[...] = jnp.zeros_like(dfw_ref)

        yh, yg = yh_ref[...], yg_ref[...]
        wa, wb = w_ref[0:WIDTH, :], w_ref[WIDTH:2 * WIDTH, :]
        h2 = x_ref[...] + jnp.dot(yh, wa, preferred_element_type=F32) + jnp.dot(yg, wb, preferred_element_type=F32)
        r2 = lax.rsqrt(jnp.mean(h2 * h2, axis=-1, keepdims=True) + EPS)
        nrm = h2 * r2
        fwv = fw_ref[...]
        err = nrm * fwv - t_ref[...]
        loss_ref[...] += jnp.full(loss_ref.shape, 0.5 * inv_d * jnp.sum(err * err), F32)
        dout = err * inv_d
        dfw_ref[...] += jnp.sum(dout * nrm, axis=0, keepdims=True)
        dn = dout * fwv
        dh2 = r2 * (dn - nrm * jnp.mean(dn * nrm, axis=-1, keepdims=True))
        dh_ref[...] = dh2
        dhb = dh2.astype(MXU_DTYPE)
        dyh_ref[...] = lax.dot_general(dhb, wa, (((1,), (1,)), ((), ())), preferred_element_type=F32)
        dyg_ref[...] = lax.dot_general(dhb, wb, (((1,), (1,)), ((), ())), preferred_element_type=F32)
        dw_ref[0:WIDTH, :] += lax.dot_general(yh, dhb, (((0,), (0,)), ((), ())), preferred_element_type=F32)
        dw_ref[WIDTH:2 * WIDTH, :] += lax.dot_general(yg, dhb, (((0,), (0,)), ((), ())), preferred_element_type=F32)

    row = lambda w: pl.BlockSpec((tm, w), lambda i: (i, 0))
    full = lambda s: pl.BlockSpec(s, lambda i: (0, 0))
    return pl.pallas_call(
        body, grid=(n // tm,), name="out_proj_loss",
        in_specs=[row(D_MODEL), row(D_MODEL), row(WIDTH), row(WIDTH), full(w_out.shape), full(fw.shape)],
        out_specs=[row(D_MODEL), row(WIDTH), row(WIDTH), full((2 * WIDTH, D_MODEL)), full((8, 128)), full((1, D_MODEL))],
        out_shape=[jax.ShapeDtypeStruct((n, D_MODEL), F32), jax.ShapeDtypeStruct((n, WIDTH), F32),
                   jax.ShapeDtypeStruct((n, WIDTH), F32), jax.ShapeDtypeStruct((2 * WIDTH, D_MODEL), F32),
                   jax.ShapeDtypeStruct((8, 128), F32), jax.ShapeDtypeStruct((1, D_MODEL), F32)],
        compiler_params=_cparams("arbitrary"),
    )(x, tgt, y_hg, y_gd, w_out, fw)


def in_proj_bwd(dphg, dpgd, dpab, w_t, h, dh2, norm_w, h0, u0, dphg0, dpgd0, dpab0):
    n = h.shape[0]
    tm = _row_tile(n)
    steps = n // tm

    def body(dphg_ref, dpgd_ref, dpab_ref, whg_ref, wgd_ref, wab_ref, h_ref, dh2_ref, nw_ref, h0_ref, u0_ref, d0hg_ref,
             d0gd_ref, d0ab_ref, dx_ref, dx0_ref, dnw_ref, ghg_ref, ggd_ref, gab_ref, acc_hg, acc_gd, acc_ab):
        i = pl.program_id(0)
        nwv = nw_ref[...]

        def norm_bwd(dps, x):
            du = jnp.dot(dps[0], whg_ref[...], preferred_element_type=F32)
            du += jnp.dot(dps[1], wgd_ref[...], preferred_element_type=F32)
            du += jnp.dot(dps[2], wab_ref[...], preferred_element_type=F32)
            r = lax.rsqrt(jnp.mean(x * x, axis=-1, keepdims=True) + EPS)
            nrm = x * r
            dn = du * nwv
            return r * (dn - nrm * jnp.mean(dn * nrm, axis=-1, keepdims=True)), nrm, jnp.sum(du * nrm, axis=0, keepdims=True)

        def accumulate(dps, u, first):
            for acc, dp in zip((acc_hg, acc_gd, acc_ab), dps):
                step = min(acc.shape[0], 512)
                for lo in range(0, acc.shape[0], step):
                    part = _mm_tn(dp[:, lo:lo + step], u)
                    acc[lo:lo + step, :] = part if first else acc[lo:lo + step, :] + part

        @pl.when(i == 0)
        def _():
            dps0 = (d0hg_ref[...], d0gd_ref[...], d0ab_ref[...])
            dx0_ref[...], _, dnw_ref[...] = norm_bwd(dps0, h0_ref[...])
            accumulate(dps0, u0_ref[...], True)

        dps = (dphg_ref[...], dpgd_ref[...], dpab_ref[...])
        dx, nrm, dnw = norm_bwd(dps, h_ref[...])
        dx_ref[...] = dh2_ref[...] + dx
        dnw_ref[...] += dnw
        accumulate(dps, (nrm * nwv).astype(MXU_DTYPE), False)

        @pl.when(i == steps - 1)
        def _():
            pltpu.sync_copy(acc_hg, ghg_ref)
            pltpu.sync_copy(acc_gd, ggd_ref)
            pltpu.sync_copy(acc_ab, gab_ref)

    row = lambda w: pl.BlockSpec((tm, w), lambda i: (i, 0))
    full = lambda a: pl.BlockSpec(a.shape, lambda i: (0, 0), pipeline_mode=pl.Buffered(1))
    anywhere = pl.BlockSpec(memory_space=pl.ANY)
    return pl.pallas_call(
        body, grid=(steps,), name="in_proj_bwd",
        in_specs=[row(4 * WIDTH), row(4 * WIDTH), row(AB_PAD)] + _w_in_specs() + [row(D_MODEL), row(D_MODEL), full(norm_w),
                                                                                   full(h0), full(u0), full(dphg0), full(dpgd0),
                                                                                   full(dpab0)],
        out_specs=[row(D_MODEL), pl.BlockSpec(h0.shape, lambda i: (0, 0)), pl.BlockSpec((1, D_MODEL), lambda i: (0, 0)),
                   anywhere, anywhere, anywhere],
        out_shape=[jax.ShapeDtypeStruct((n, D_MODEL), F32), jax.ShapeDtypeStruct(h0.shape, F32),
                   jax.ShapeDtypeStruct((1, D_MODEL), F32), jax.ShapeDtypeStruct((4 * WIDTH, D_MODEL), F32),
                   jax.ShapeDtypeStruct((4 * WIDTH, D_MODEL), F32), jax.ShapeDtypeStruct((AB_PAD, D_MODEL), F32)],
        scratch_shapes=[pltpu.VMEM((4 * WIDTH, D_MODEL), F32), pltpu.VMEM((4 * WIDTH, D_MODEL), F32),
                        pltpu.VMEM((AB_PAD, D_MODEL), F32)],
        compiler_params=pltpu.CompilerParams(dimension_semantics=("arbitrary",), vmem_limit_bytes=VMEM_LIMIT_LARGE),
    )(dphg, dpgd, dpab, w_t, w_t, w_t, h, dh2, norm_w, h0, u0, dphg0, dpgd0, dpab0)


def _sds(shape, dtype=F32):
    return jax.ShapeDtypeStruct(shape, dtype)


def _pairs(b):
    return [(i, h) for i in range(b) for h in range(HEADS)]


def _load_slabs(ref, b, k):
    return jnp.stack([ref[i, k * CHUNK:(k + 1) * CHUNK, h * DH:(h + 1) * DH].astype(F32) for i, h in _pairs(b)], axis=0)


def _lead_slabs(a, b):
    return jnp.stack([a[:, h * DH:(h + 1) * DH].astype(F32) for _, h in _pairs(b)], axis=0)


def _rows(a3, i):
    return jnp.concatenate([a3[i * HEADS + h] for h in range(HEADS)], axis=1)


def _store_slabs(ref, a3, b, k):
    for i in range(b):
        ref[i, k * CHUNK:(k + 1) * CHUNK, :] = _rows(a3, i).astype(ref.dtype)


def _sum_rows(a3, b):
    out = _rows(a3, 0)
    for i in range(1, b):
        out = out + _rows(a3, i)
    return out


def _save_states(ref, s, b, k):
    for i in range(b):
        ref[i, k] = jnp.concatenate([s[i * HEADS + h] for h in range(HEADS)], axis=0)


def _load_states(ref, b, k):
    return jnp.stack([ref[i, k, h * DH:(h + 1) * DH, :] for i, h in _pairs(b)], axis=0)


def hg_local_fwd(p, p0, logits):
    b, seq, _ = p.shape
    rows = LOCAL_CHUNKS * CHUNK
    nreal = seq // CHUNK

    def body(p_ref, p0_ref, lg_ref, q_ref, k_ref, o_ref, eg_ref, q0_ref, k0_ref, o0_ref, eg0_ref):
        @pl.when((pl.program_id(0) == 0) & (pl.program_id(1) == 0))
        def _():
            q_in, k_out, o0_ref[...], (eg0_ref[...],) = hg_local(p0_ref[...], lg_ref[...])
            q0_ref[...], k0_ref[...] = q_in.astype(MXU_DTYPE), k_out.astype(MXU_DTYPE)

        q_in, k_out, o_intra, egs = hg_local(p_ref[...], lg_ref[...])
        q_ref[...], k_ref[...], o_ref[...] = q_in.astype(MXU_DTYPE), k_out.astype(MXU_DTYPE), o_intra
        for c in range(LOCAL_CHUNKS):
            eg_ref[c] = egs[c]

    slab = pl.BlockSpec((None, rows, WIDTH), lambda s, g: (s, g, 0))
    const = lambda shape: pl.BlockSpec(shape, lambda s, g: (0, 0))
    lead_shapes = [(CHUNK, WIDTH)] * 3 + [(1, WIDTH)]
    out = pl.pallas_call(
        body, grid=(b, seq // rows), name="hgrn2_local",
        in_specs=[pl.BlockSpec((None, rows, 4 * WIDTH), lambda s, g: (s, g, 0)), const(p0.shape), const(logits.shape)],
        out_specs=[slab, slab, slab, pl.BlockSpec((None, LOCAL_CHUNKS, 1, WIDTH), lambda s, g: (s, g, 0, 0))]
        + [const(s) for s in lead_shapes],
        out_shape=[_sds((b, seq, WIDTH), MXU_DTYPE)] * 2 + [_sds((b, seq, WIDTH)), _sds((b, nreal, 1, WIDTH))]
        + [_sds(lead_shapes[0], MXU_DTYPE)] * 2 + [_sds(lead_shapes[2]), _sds(lead_shapes[3])],
        compiler_params=_cparams("arbitrary", "arbitrary"),
    )(p, p0, logits)
    return out[0:4], out[4:8]


def _hg_scan_args(b, k, q_ref, k_ref, o_ref, v_ref, z_ref, eg_ref):
    eg = jnp.stack([eg_ref[i, k, :, h * DH:(h + 1) * DH] for i, h in _pairs(b)], axis=0)
    return (_load_slabs(q_ref, b, k), _load_slabs(k_ref, b, k), _load_slabs(v_ref, b, k), eg, _load_slabs(o_ref, b, k),
            _load_slabs(z_ref, b, k))


def _hg_lead_args(b, q0_ref, k0_ref, o0_ref, p0_ref, eg0_ref):
    eg = jnp.stack([eg0_ref[:, h * DH:(h + 1) * DH] for _, h in _pairs(b)], axis=0)
    return (_lead_slabs(q0_ref[...], b), _lead_slabs(k0_ref[...], b), _lead_slabs(p0_ref[:, 2 * WIDTH:3 * WIDTH], b), eg,
            _lead_slabs(o0_ref[...], b), _lead_slabs(p0_ref[:, 3 * WIDTH:4 * WIDTH], b))


def _scan_specs(b, ng, reverse, chunks):
    group = (lambda i: ng - 1 - i) if reverse else (lambda i: i)
    slab = lambda lane_block: pl.BlockSpec((b, chunks * CHUNK, WIDTH), lambda i: (0, group(i), lane_block))
    per_chunk = lambda *tail: pl.BlockSpec((b, chunks) + tail, lambda i: (0, group(i)) + (0,) * len(tail))
    const = lambda a: pl.BlockSpec(a.shape, lambda i: (0,) * a.ndim)
    return slab, per_chunk, const


def run_scans(parts, nc, name):
    n_in = [len(p["args"]) for p in parts]
    n_out = [len(p["out_shape"]) for p in parts]
    n_scr = [len(p["scratch_shapes"]) for p in parts]

    def body(*refs):
        ins, outs, scr = refs[:sum(n_in)], refs[sum(n_in):sum(n_in) + sum(n_out)], refs[sum(n_in) + sum(n_out):]
        for i, part in enumerate(parts):
            part["body"](*ins[sum(n_in[:i]):sum(n_in[:i + 1])], *outs[sum(n_out[:i]):sum(n_out[:i + 1])],
                         *scr[sum(n_scr[:i]):sum(n_scr[:i + 1])])

    flat = lambda key: [v for p in parts for v in p[key]]
    out = pl.pallas_call(body, grid=(nc,), name=name, in_specs=flat("in_specs"), out_specs=flat("out_specs"),
                         out_shape=flat("out_shape"), scratch_shapes=flat("scratch_shapes"),
                         compiler_params=_cparams("arbitrary"))(*flat("args"))
    return [out[sum(n_out[:i]):sum(n_out[:i + 1])] for i in range(len(parts))]


def hg_scan_fwd(p, p0, local, lead, nw):
    b, seq, _ = p.shape
    q_in, k_out, o_intra, eg = local
    slab, per_chunk, const = _scan_specs(b, seq // (SCAN_CHUNKS_FWD * CHUNK), False, SCAN_CHUNKS_FWD)

    def body(q_ref, k_ref, o_ref, v_ref, z_ref, eg_ref, q0_ref, k0_ref, o0_ref, p0_ref, eg0_ref, nw_ref, y_ref, ss_ref, st):
        @pl.when(pl.program_id(0) == 0)
        def _():
            st[...] = hg_scan(*_hg_lead_args(b, q0_ref, k0_ref, o0_ref, p0_ref, eg0_ref), nw_ref[...], jnp.zeros(st.shape, F32))[1]

        s = st[...]
        for k in range(SCAN_CHUNKS_FWD):
            _save_states(ss_ref, s, b, k)
            y, s = hg_scan(*_hg_scan_args(b, k, q_ref, k_ref, o_ref, v_ref, z_ref, eg_ref), nw_ref[...], s)
            _store_slabs(y_ref, y, b, k)
        st[...] = s

    return dict(
        body=body, args=(q_in, k_out, o_intra, p, p, eg, lead[0], lead[1], lead[2], p0, lead[3], nw),
        in_specs=[slab(0), slab(0), slab(0), slab(2), slab(3), per_chunk(1, WIDTH)] + [const(a) for a in lead[0:3]]
        + [const(p0), const(lead[3]), const(nw)],
        out_specs=[slab(0), per_chunk(WIDTH, DH)],
        out_shape=[_sds((b, seq, WIDTH), MXU_DTYPE), _sds((b, seq // CHUNK, WIDTH, DH))],
        scratch_shapes=[pltpu.VMEM((b * HEADS, DH, DH), F32)])


def hg_scan_bwd(p, p0, local, lead, nw, ssave, dy):
    b, seq, _ = p.shape
    ng = seq // (SCAN_CHUNKS * CHUNK)
    q_in, k_out, o_intra, eg = local
    slab, per_chunk, const = _scan_specs(b, ng, True, SCAN_CHUNKS)

    def body(q_ref, k_ref, o_ref, v_ref, z_ref, eg_ref, q0_ref, k0_ref, o0_ref, p0_ref, eg0_ref, nw_ref, ss_ref, dy_ref,
             dq_ref, dk_ref, do_ref, dv_ref, dz_ref, deg_ref, dq0_ref, dk0_ref, do0_ref, dv0_ref, dz0_ref, deg0_ref, dnw_ref,
             dst):
        i = pl.program_id(0)

        @pl.when(i == 0)
        def _():
            dst[...] = jnp.zeros_like(dst)
            dnw_ref[...] = jnp.zeros_like(dnw_ref)

        ds = dst[...]
        for k in reversed(range(SCAN_CHUNKS)):
            args = _hg_scan_args(b, k, q_ref, k_ref, o_ref, v_ref, z_ref, eg_ref)
            _, vjp = jax.vjp(hg_scan, *args, nw_ref[...], _load_states(ss_ref, b, k))
            dq, dk, dv, deg, do, dz, dnw, ds = vjp((_load_slabs(dy_ref, b, k), ds))
            dnw_ref[...] += dnw
            for ref, val in ((dq_ref, dq), (dk_ref, dk), (do_ref, do), (dv_ref, dv), (dz_ref, dz)):
                _store_slabs(ref, val, b, k)
            for j in range(b):
                deg_ref[j, k] = _rows(deg, j)
        dst[...] = ds

        @pl.when(i == ng - 1)
        def _():
            args = _hg_lead_args(b, q0_ref, k0_ref, o0_ref, p0_ref, eg0_ref)
            _, vjp = jax.vjp(hg_scan, *args, nw_ref[...], jnp.zeros(dst.shape, F32))
            dq, dk, dv, deg, do, dz, dnw, _ = vjp((jnp.zeros((b * HEADS, CHUNK, DH), F32), ds))
            dnw_ref[...] += dnw
            for ref, val in ((dq0_ref, dq), (dk0_ref, dk), (do0_ref, do), (dv0_ref, dv), (dz0_ref, dz), (deg0_ref, deg)):
                ref[...] = _sum_rows(val, b)

    lead_out = [const(a) for a in lead[0:3]] + [const(lead[0]), const(lead[0]), const(lead[3])]
    return dict(
        body=body, args=(q_in, k_out, o_intra, p, p, eg, lead[0], lead[1], lead[2], p0, lead[3], nw, ssave, dy),
        in_specs=[slab(0), slab(0), slab(0), slab(2), slab(3), per_chunk(1, WIDTH)] + [const(a) for a in lead[0:3]]
        + [const(p0), const(lead[3]), const(nw), per_chunk(WIDTH, DH), slab(0)],
        out_specs=[slab(0)] * 5 + [per_chunk(1, WIDTH)] + lead_out + [const(nw)],
        out_shape=[_sds((b, seq, WIDTH))] * 5 + [_sds(eg.shape)] + [_sds((CHUNK, WIDTH))] * 5 + [_sds((1, WIDTH)), _sds(nw.shape)],
        scratch_shapes=[pltpu.VMEM((b * HEADS, DH, DH), F32)])


def _hg_local_vjp(p, logits, dq, dk, do, degs, dv, dz):
    _, vjp = jax.vjp(hg_local, p, logits)
    dp, dlg = vjp((dq, dk, do, degs))
    return dp + jnp.concatenate([jnp.zeros((p.shape[0], 2 * WIDTH), F32), dv, dz], axis=1), dlg


def hg_local_bwd(p, p0, logits, cot, cot0):
    b, seq, _ = p.shape
    rows = LOCAL_CHUNKS * CHUNK

    def body(p_ref, p0_ref, lg_ref, dq_ref, dk_ref, do_ref, dv_ref, dz_ref, deg_ref, dq0_ref, dk0_ref, do0_ref, dv0_ref, dz0_ref,
             deg0_ref, dp_ref, dp0_ref, dlg_ref):
        @pl.when((pl.program_id(0) == 0) & (pl.program_id(1) == 0))
        def _():
            dp0, dlg_ref[...] = _hg_local_vjp(p0_ref[...], lg_ref[...], dq0_ref[...], dk0_ref[...], do0_ref[...],
                                              (deg0_ref[...],), dv0_ref[...], dz0_ref[...])
            dp0_ref[...] = dp0.astype(MXU_DTYPE)

        degs = tuple(deg_ref[c] for c in range(LOCAL_CHUNKS))
        dp, dlg = _hg_local_vjp(p_ref[...], lg_ref[...], dq_ref[...], dk_ref[...], do_ref[...], degs, dv_ref[...], dz_ref[...])
        dp_ref[...] = dp.astype(MXU_DTYPE)
        dlg_ref[...] += dlg

    slab = pl.BlockSpec((None, rows, WIDTH), lambda s, g: (s, g, 0))
    wide = pl.BlockSpec((None, rows, 4 * WIDTH), lambda s, g: (s, g, 0))
    const = lambda a: pl.BlockSpec(a.shape, lambda s, g: (0, 0))
    return pl.pallas_call(
        body, grid=(b, seq // rows), name="hgrn2_local_bwd",
        in_specs=[wide, const(p0), const(logits), slab, slab, slab, slab, slab,
                  pl.BlockSpec((None, LOCAL_CHUNKS, 1, WIDTH), lambda s, g: (s, g, 0, 0))] + [const(a) for a in cot0],
        out_specs=[wide, const(p0), const(logits)],
        out_shape=[_sds(p.shape, MXU_DTYPE), _sds(p0.shape, MXU_DTYPE), _sds(logits.shape)],
        compiler_params=_cparams("arbitrary", "arbitrary"),
    )(p, p0, logits, *cot, *cot0)


def _halo_block(g):
    return jnp.maximum((LOCAL_CHUNKS * CHUNK // HALO) * g - 1, 0)


def _gd_window(g, p_ref, halo_ref, p0_ref):
    halo = jnp.where(g == 0, p0_ref[CHUNK - HALO:CHUNK, 0:QKV], halo_ref[...])
    return jnp.concatenate([halo, p_ref[:, 0:QKV]], axis=0)


def _lead_window(p0_ref):
    return jnp.concatenate([jnp.zeros((HALO, QKV), F32), p0_ref[:, 0:QKV]], axis=0)


def gd_local_fwd(p, p0, ab, ab0, cw, alog, dtb):
    b, seq, _ = p.shape
    rows = LOCAL_CHUNKS * CHUNK
    nreal = seq // CHUNK

    def body(p_ref, halo_ref, p0_ref, ab_ref, ab0_ref, cw_ref, al_ref, dt_ref, u_ref, w_ref, qe_ref, ke_ref, qk_ref, ea_ref,
             inv_ref, u0_ref, w0_ref, qe0_ref, ke0_ref, qk0_ref, ea0_ref, inv0_ref):
        @pl.when((pl.program_id(0) == 0) & (pl.program_id(1) == 0))
        def _():
            (u0_ref[...], ww, qe, ke, qk0_ref[...], (ea0_ref[...],)), inv0_ref[...] = gd_local(
                _lead_window(p0_ref), ab0_ref[...], cw_ref[...], al_ref[...], dt_ref[...], inverse=_tri_y_impl)
            w0_ref[...], qe0_ref[...], ke0_ref[...] = ww.astype(MXU_DTYPE), qe.astype(MXU_DTYPE), ke.astype(MXU_DTYPE)

        (uu, ww, qe, ke, qk, eas), inv = gd_local(_gd_window(pl.program_id(1), p_ref, halo_ref, p0_ref), ab_ref[...],
                                                  cw_ref[...], al_ref[...], dt_ref[...], inverse=_tri_y_impl)
        u_ref[...], w_ref[...], qe_ref[...], ke_ref[...] = uu, ww.astype(MXU_DTYPE), qe.astype(MXU_DTYPE), ke.astype(MXU_DTYPE)
        for c in range(LOCAL_CHUNKS):
            qk_ref[c] = qk[c * HEADS * CHUNK:(c + 1) * HEADS * CHUNK]
            inv_ref[c] = inv[c * HEADS * CHUNK:(c + 1) * HEADS * CHUNK]
            ea_ref[c] = eas[c]

    const = lambda shape: pl.BlockSpec(shape, lambda s, g: (0, 0))
    slab = pl.BlockSpec((None, rows, WIDTH), lambda s, g: (s, g, 0))
    mats = pl.BlockSpec((None, LOCAL_CHUNKS, HEADS * CHUNK, CHUNK), lambda s, g: (s, g, 0, 0))
    lead_out = [_sds((CHUNK, WIDTH))] + [_sds((CHUNK, WIDTH), MXU_DTYPE)] * 3 + [_sds((HEADS * CHUNK, CHUNK)), _sds((1, AB_PAD)),
                                                                                _sds((HEADS * CHUNK, CHUNK))]
    out = pl.pallas_call(
        body, grid=(b, seq // rows), name="gdn_local",
        in_specs=[pl.BlockSpec((None, rows, 4 * WIDTH), lambda s, g: (s, g, 0)),
                  pl.BlockSpec((None, HALO, QKV), lambda s, g: (s, _halo_block(g), 0)), const(p0.shape),
                  pl.BlockSpec((None, rows, AB_PAD), lambda s, g: (s, g, 0)), const(ab0.shape), const(cw.shape),
                  const(alog.shape), const(dtb.shape)],
        out_specs=[slab] * 4 + [mats, pl.BlockSpec((None, LOCAL_CHUNKS, 1, AB_PAD), lambda s, g: (s, g, 0, 0)), mats]
        + [const(s.shape) for s in lead_out],
        out_shape=[_sds((b, seq, WIDTH))] + [_sds((b, seq, WIDTH), MXU_DTYPE)] * 3
        + [_sds((b, nreal, HEADS * CHUNK, CHUNK)), _sds((b, nreal, 1, AB_PAD)), _sds((b, nreal, HEADS * CHUNK, CHUNK))] + lead_out,
        compiler_params=_cparams("arbitrary", "arbitrary"),
    )(p, p, p0, ab, ab0, cw, alog, dtb)
    return out[0:6], out[6], out[7:13], out[13]


def _gd_scan_args(b, k, u_ref, w_ref, qe_ref, ke_ref, qk_ref, ea_ref, z_ref):
    qk = jnp.stack([qk_ref[i, k, h * CHUNK:(h + 1) * CHUNK, :] for i, h in _pairs(b)], axis=0)
    ea = jnp.stack([ea_ref[i, k, :, h:h + 1] for i, h in _pairs(b)], axis=0)
    return (_load_slabs(u_ref, b, k), _load_slabs(w_ref, b, k), _load_slabs(qe_ref, b, k), _load_slabs(ke_ref, b, k), qk, ea,
            _load_slabs(z_ref, b, k))


def _gd_lead_args(b, u0_ref, w0_ref, qe0_ref, ke0_ref, qk0_ref, ea0_ref, p0_ref):
    qk = jnp.stack([qk0_ref[h * CHUNK:(h + 1) * CHUNK, :] for _, h in _pairs(b)], axis=0)
    ea = jnp.stack([ea0_ref[:, h:h + 1] for _, h in _pairs(b)], axis=0)
    return (_lead_slabs(u0_ref[...], b), _lead_slabs(w0_ref[...], b), _lead_slabs(qe0_ref[...], b), _lead_slabs(ke0_ref[...], b),
            qk, ea, _lead_slabs(p0_ref[:, QKV:QKV + WIDTH], b))


def gd_scan_fwd(p, p0, local, lead, nw):
    b, seq, _ = p.shape
    slab, per_chunk, const = _scan_specs(b, seq // (SCAN_CHUNKS_FWD * CHUNK), False, SCAN_CHUNKS_FWD)

    def body(u_ref, w_ref, qe_ref, ke_ref, qk_ref, ea_ref, z_ref, u0_ref, w0_ref, qe0_ref, ke0_ref, qk0_ref, ea0_ref, p0_ref,
             nw_ref, y_ref, ss_ref, st):
        @pl.when(pl.program_id(0) == 0)
        def _():
            lead_args = _gd_lead_args(b, u0_ref, w0_ref, qe0_ref, ke0_ref, qk0_ref, ea0_ref, p0_ref)
            st[...] = gd_scan(*lead_args, nw_ref[...], jnp.zeros(st.shape, F32))[1]

        s = st[...]
        for k in range(SCAN_CHUNKS_FWD):
            _save_states(ss_ref, s, b, k)
            y, s = gd_scan(*_gd_scan_args(b, k, u_ref, w_ref, qe_ref, ke_ref, qk_ref, ea_ref, z_ref), nw_ref[...], s)
            _store_slabs(y_ref, y, b, k)
        st[...] = s

    return dict(
        body=body, args=(*local, p, *lead, p0, nw),
        in_specs=[slab(0)] * 4 + [per_chunk(HEADS * CHUNK, CHUNK), per_chunk(1, AB_PAD), slab(3)] + [const(a) for a in lead]
        + [const(p0), const(nw)],
        out_specs=[slab(0), per_chunk(WIDTH, DH)],
        out_shape=[_sds((b, seq, WIDTH), MXU_DTYPE), _sds((b, seq // CHUNK, WIDTH, DH))],
        scratch_shapes=[pltpu.VMEM((b * HEADS, DH, DH), F32)])


def gd_scan_bwd(p, p0, local, lead, nw, ssave, dy):
    b, seq, _ = p.shape
    ng = seq // (SCAN_CHUNKS * CHUNK)
    slab, per_chunk, const = _scan_specs(b, ng, True, SCAN_CHUNKS)

    def body(u_ref, w_ref, qe_ref, ke_ref, qk_ref, ea_ref, z_ref, u0_ref, w0_ref, qe0_ref, ke0_ref, qk0_ref, ea0_ref, p0_ref,
             nw_ref, ss_ref, dy_ref, du_ref, dw_ref, dqe_ref, dke_ref, dqk_ref, dea_ref, dz_ref, du0_ref, dw0_ref, dqe0_ref,
             dke0_ref, dqk0_ref, dea0_ref, dz0_ref, dnw_ref, dst):
        i = pl.program_id(0)
        lane = lax.broadcasted_iota(jnp.int32, (1, AB_PAD), 1)

        def gate_rows(dea, j):
            return sum(jnp.where(lane == h, dea[j * HEADS + h], 0.0) for h in range(HEADS))

        def matrix_rows(dqk, j):
            return jnp.concatenate([dqk[j * HEADS + h] for h in range(HEADS)], axis=0)

        @pl.when(i == 0)
        def _():
            dst[...] = jnp.zeros_like(dst)
            dnw_ref[...] = jnp.zeros_like(dnw_ref)

        ds = dst[...]
        for k in reversed(range(SCAN_CHUNKS)):
            args = _gd_scan_args(b, k, u_ref, w_ref, qe_ref, ke_ref, qk_ref, ea_ref, z_ref)
            _, vjp = jax.vjp(gd_scan, *args, nw_ref[...], _load_states(ss_ref, b, k))
            du, dw, dqe, dke, dqk, dea, dz, dnw, ds = vjp((_load_slabs(dy_ref, b, k), ds))
            dnw_ref[...] += dnw
            for ref, val in ((du_ref, du), (dw_ref, dw), (dqe_ref, dqe), (dke_ref, dke), (dz_ref, dz)):
                _store_slabs(ref, val, b, k)
            for j in range(b):
                dqk_ref[j, k] = matrix_rows(dqk, j)
                dea_ref[j, k] = gate_rows(dea, j)
        dst[...] = ds

        @pl.when(i == ng - 1)
        def _():
            args = _gd_lead_args(b, u0_ref, w0_ref, qe0_ref, ke0_ref, qk0_ref, ea0_ref, p0_ref)
            _, vjp = jax.vjp(gd_scan, *args, nw_ref[...], jnp.zeros(dst.shape, F32))
            du, dw, dqe, dke, dqk, dea, dz, dnw, _ = vjp((jnp.zeros((b * HEADS, CHUNK, DH), F32), ds))
            dnw_ref[...] += dnw
            for ref, val in ((du0_ref, du), (dw0_ref, dw), (dqe0_ref, dqe), (dke0_ref, dke), (dz0_ref, dz)):
                ref[...] = _sum_rows(val, b)
            dqk0_ref[...] = sum((matrix_rows(dqk, j) for j in range(1, b)), matrix_rows(dqk, 0))
            dea0_ref[...] = sum((gate_rows(dea, j) for j in range(1, b)), gate_rows(dea, 0))

    uu, ww, qe, ke, qk, ea = local
    return dict(
        body=body, args=(*local, p, *lead, p0, nw, ssave, dy),
        in_specs=[slab(0)] * 4 + [per_chunk(HEADS * CHUNK, CHUNK), per_chunk(1, AB_PAD), slab(3)] + [const(a) for a in lead]
        + [const(p0), const(nw), per_chunk(WIDTH, DH), slab(0)],
        out_specs=[slab(0)] * 4 + [per_chunk(HEADS * CHUNK, CHUNK), per_chunk(1, AB_PAD), slab(0)] + [const(a) for a in lead]
        + [const(lead[0]), const(nw)],
        out_shape=[_sds((b, seq, WIDTH))] * 4 + [_sds(qk.shape), _sds(ea.shape), _sds((b, seq, WIDTH))]
        + [_sds(a.shape) for a in lead] + [_sds(lead[0].shape), _sds(nw.shape)],
        scratch_shapes=[pltpu.VMEM((b * HEADS, DH, DH), F32)])


def _gd_local_vjp(inv_rows, xx, ab, cw, alog, dtb):
    nb = ab.shape[0] // CHUNK
    inv = jnp.stack([inv_rows[g * CHUNK:(g + 1) * CHUNK] for g in range(nb * HEADS)], axis=0)
    _, vjp, _ = jax.vjp(lambda *a: gd_local(*a, inverse=_saved_inverse(inv)), xx, ab, cw, alog, dtb, has_aux=True)
    return vjp


def gd_local_bwd(p, p0, ab, ab0, cw, alog, dtb, inv, inv0, cot, dz, cot0, dz0):
    b, seq, _ = p.shape
    rows = LOCAL_CHUNKS * CHUNK
    ng = seq // rows
    du, dw, dqe, dke, dqk, dea = cot

    def body(p_ref, halo_ref, p0_ref, ab_ref, ab0_ref, cw_ref, al_ref, dt_ref, inv_ref, inv0_ref, du_ref, dw_ref, dqe_ref,
             dke_ref, dqk_ref, dea_ref, dz_ref, du0_ref, dw0_ref, dqe0_ref, dke0_ref, dqk0_ref, dea0_ref, dz0_ref,
             dp_ref, dab_ref, dp0_ref, dab0_ref, dcw_ref, dal_ref, ddt_ref, dhalo, dtail):
        s, i = pl.program_id(0), pl.program_id(1)
        g = ng - 1 - i

        @pl.when(i == 0)
        def _():
            dhalo[...] = jnp.zeros_like(dhalo)

        @pl.when((s == 0) & (i == 0))
        def _():
            dtail[...] = jnp.zeros_like(dtail)
            dcw_ref[...] = jnp.zeros_like(dcw_ref)
            dal_ref[...] = jnp.zeros_like(dal_ref)
            ddt_ref[...] = jnp.zeros_like(ddt_ref)

        def finish(dxx, dab, dcw, dal, ddt, before, n, dz_val, dp_out, dab_out):
            dqkv = dxx[HALO:HALO + n] + jnp.concatenate([jnp.zeros((n - HALO, QKV), F32), before], axis=0)
            dp_out[...] = jnp.concatenate([dqkv, dz_val], axis=1).astype(MXU_DTYPE)
            dab_out[...] = dab.astype(MXU_DTYPE)
            dcw_ref[...] += dcw
            dal_ref[...] += dal
            ddt_ref[...] += ddt

        inv_rows = jnp.concatenate([inv_ref[c] for c in range(LOCAL_CHUNKS)], axis=0)
        vjp = _gd_local_vjp(inv_rows, _gd_window(g, p_ref, halo_ref, p0_ref), ab_ref[...], cw_ref[...], al_ref[...], dt_ref[...])
        dqk_all = jnp.concatenate([dqk_ref[c] for c in range(LOCAL_CHUNKS)], axis=0)
        deas = tuple(dea_ref[c] for c in range(LOCAL_CHUNKS))
        grads = vjp((du_ref[...], dw_ref[...], dqe_ref[...], dke_ref[...], dqk_all, deas))
        finish(*grads, dhalo[...], rows, dz_ref[...], dp_ref, dab_ref)
        dhalo[...] = grads[0][0:HALO]

        @pl.when(g == 0)
        def _():
            dtail[...] += grads[0][0:HALO]

        @pl.when((s == b - 1) & (g == 0))
        def _():
            vjp0 = _gd_local_vjp(inv0_ref[...], _lead_window(p0_ref), ab0_ref[...], cw_ref[...], al_ref[...], dt_ref[...])
            grads0 = vjp0((du0_ref[...], dw0_ref[...], dqe0_ref[...], dke0_ref[...], dqk0_ref[...], (dea0_ref[...],)))
            finish(*grads0, dtail[...], CHUNK, dz0_ref[...], dp0_ref, dab0_ref)

    rg = lambda i: ng - 1 - i
    const = lambda a: pl.BlockSpec(a.shape, lambda s, i: (0, 0))
    slab = pl.BlockSpec((None, rows, WIDTH), lambda s, i: (s, rg(i), 0))
    wide = pl.BlockSpec((None, rows, 4 * WIDTH), lambda s, i: (s, rg(i), 0))
    gates = pl.BlockSpec((None, rows, AB_PAD), lambda s, i: (s, rg(i), 0))
    mats = pl.BlockSpec((None, LOCAL_CHUNKS, HEADS * CHUNK, CHUNK), lambda s, i: (s, rg(i), 0, 0))
    return pl.pallas_call(
        body, grid=(b, ng), name="gdn_local_bwd",
        in_specs=[wide, pl.BlockSpec((None, HALO, QKV), lambda s, i: (s, _halo_block(rg(i)), 0)), const(p0), gates, const(ab0),
                  const(cw), const(alog), const(dtb), mats, const(inv0), slab, slab, slab, slab, mats,
                  pl.BlockSpec((None, LOCAL_CHUNKS, 1, AB_PAD), lambda s, i: (s, rg(i), 0, 0)), slab]
        + [const(a) for a in cot0] + [const(dz0)],
        out_specs=[wide, gates, const(p0), const(ab0), const(cw), const(alog), const(dtb)],
        out_shape=[_sds(p.shape, MXU_DTYPE), _sds(ab.shape, MXU_DTYPE), _sds(p0.shape, MXU_DTYPE), _sds(ab0.shape, MXU_DTYPE),
                   _sds(cw.shape), _sds(alog.shape), _sds(dtb.shape)],
        scratch_shapes=[pltpu.VMEM((HALO, QKV), F32), pltpu.VMEM((HALO, QKV), F32)],
        compiler_params=_cparams("arbitrary", "arbitrary"),
    )(p, p, p0, ab, ab0, cw, alog, dtb, inv, inv0, du, dw, dqe, dke, dqk, dea, dz, *cot0, dz0)


def _position():
    return lax.axis_index("x"), lax.axis_index("y"), lax.axis_index("c")


EXCHANGE_COPIES = 10


def _exchange_blocks(bufs, send_sems, recv_sems):
    x, y, c = _position()
    here, x_nbr, y_nbr, diag = (x, y), (1 - x, y), (x, 1 - y), (1 - x, 1 - y)
    sibling = (x, y, 1 - c)
    me = (x, y, c)
    n = range(len(bufs))

    def rows(a, chip, core, half=None):
        block = bufs[a].at[4 * chip[0] + 2 * chip[1] + core]
        if half is None:
            return block
        total = bufs[a].shape[1]
        tile = 8 * (4 // jnp.dtype(bufs[a].dtype).itemsize)
        split = total // 2 // tile * tile
        return block.at[pl.ds(0, split)] if half == 0 else block.at[pl.ds(split, total - split)]

    def copy(a, k, region, to):
        return pltpu.make_async_remote_copy(src_ref=region, dst_ref=region, send_sem=send_sems.at[a * EXCHANGE_COPIES + k],
                                            recv_sem=recv_sems.at[a * EXCHANGE_COPIES + k], device_id=to, device_id_type=MESH)

    sent = [copy(a, 0, rows(a, here, c), sibling) for a in n]
    sent += [cp for a in n for cp in (copy(a, 1, rows(a, here, c, 0), (*x_nbr, c)), copy(a, 4, rows(a, here, c, 1), (*y_nbr, c)))]
    sent += [cp for a in n for cp in (copy(a, 2, rows(a, here, c, 1), (*x_nbr, c)), copy(a, 3, rows(a, here, c, 0), (*y_nbr, c)))]
    for cp in sent:
        cp.start()

    def after(arrivals, a, k, region, to):
        for cp in arrivals:
            cp.wait_recv()
        sent.append(copy(a, k, region, to))
        sent[-1].start()

    for a in n:
        after([copy(a, 1, rows(a, x_nbr, c, 0), me)], a, 5, rows(a, x_nbr, c, 0), (*y_nbr, c))
        after([copy(a, 4, rows(a, y_nbr, c, 1), me)], a, 6, rows(a, y_nbr, c, 1), (*x_nbr, c))
    for a in n:
        after([copy(a, 2, rows(a, x_nbr, c, 1), me)], a, 7, rows(a, x_nbr, c), sibling)
        after([copy(a, 3, rows(a, y_nbr, c, 0), me)], a, 8, rows(a, y_nbr, c), sibling)
    for a in n:
        after([copy(a, 5, rows(a, diag, c, 0), me), copy(a, 6, rows(a, diag, c, 1), me)], a, 9, rows(a, diag, c), sibling)
    for a in n:
        copy(a, 0, rows(a, here, 1 - c), me).wait_recv()
        for k, chip in ((7, x_nbr), (8, y_nbr), (9, diag)):
            copy(a, k, rows(a, chip, 1 - c), me).wait_recv()
    for cp in sent:
        cp.wait_send()


def _exchange_sems(n_bufs):
    return [pltpu.SemaphoreType.DMA((n_bufs * EXCHANGE_COPIES,)), pltpu.SemaphoreType.DMA((n_bufs * EXCHANGE_COPIES,))]


def gather_weights(w_in_t, w_out, small, pad_rows):
    rows, _, cols = w_in_t.shape
    buf_rows = -(-rows // ROW_TILE_BF16) * ROW_TILE_BF16

    def body(wi_ref, wo_ref, sm_ref, wi_out, wo_out, sm_out, wi_buf, send_sems, recv_sems):
        x, y, c = _position()
        me = 4 * x + 2 * y + c
        wi_buf[me, pl.ds(0, rows), :] = wi_ref[:, 0, :].astype(MXU_DTYPE)
        wi_buf[me, pl.ds(rows, buf_rows - rows), :] = jnp.zeros((buf_rows - rows, cols), MXU_DTYPE)
        wo_out[me] = wo_ref[...].astype(MXU_DTYPE)
        sm_out[me] = sm_ref[...]
        _exchange_blocks([wi_buf, wo_out, sm_out], send_sems, recv_sems)
        for d in range(N_DEV):
            wi_out[pl.ds(d * rows, rows), :] = wi_buf[d, pl.ds(0, rows), :]
        wi_out[pl.ds(N_DEV * rows, pad_rows), :] = jnp.zeros((pad_rows, cols), MXU_DTYPE)

    return pl.pallas_call(
        body, name="gather_weights", in_specs=[VMEM_SPEC] * 3, out_specs=[VMEM_SPEC] * 3,
        out_shape=[jax.ShapeDtypeStruct((N_DEV * rows + pad_rows, cols), MXU_DTYPE),
                   jax.ShapeDtypeStruct((N_DEV,) + w_out.shape, MXU_DTYPE), jax.ShapeDtypeStruct((N_DEV,) + small.shape, F32)],
        scratch_shapes=[pltpu.VMEM((N_DEV, buf_rows, cols), MXU_DTYPE)] + _exchange_sems(3),
        compiler_params=pltpu.CompilerParams(vmem_limit_bytes=VMEM_LIMIT))(w_in_t, w_out, small)


HOPS = 6


def reduce_gradients(tensors, small, name):
    n_t = len(tensors)
    arrays = [a for parts, _ in tensors for a, _ in parts]
    first_array = [sum(len(parts) for parts, _ in tensors[:t]) for t in range(n_t)]

    def pieces(t, j):
        parts, block_rows = tensors[t]
        out, base = [], 0
        for pi, (_, valid) in enumerate(parts):
            lo, hi = max(j * block_rows, base), min((j + 1) * block_rows, base + valid)
            if lo < hi:
                out.append((first_array[t] + pi, lo - base, lo - j * block_rows, hi - lo))
            base += valid
        return out

    def body(*refs):
        n_a = len(arrays)
        in_refs, small_ref = refs[:n_a], refs[n_a]
        out_refs, small_sum = refs[n_a + 1:n_a + 1 + n_t], refs[n_a + 1 + n_t]
        bufs, small_buf = refs[n_a + 2 + n_t:n_a + 2 + 5 * n_t], refs[n_a + 2 + 5 * n_t]
        s1_sems, r1_sems, s2_sems, r2_sems, small_send, small_recv = refs[n_a + 3 + 5 * n_t:]
        x, y, c = _position()
        chip = 2 * x + y

        def put(t, dst, j, add=None):
            for ai, src_row, dst_row, size in pieces(t, j):
                v = in_refs[ai][pl.ds(src_row, size), :]
                if add is not None:
                    v = v + add[pl.ds(dst_row, size), :].astype(F32)
                dst[pl.ds(dst_row, size), :] = v.astype(dst.dtype)

        def swap(t, k):
            send1, recv1 = bufs[4 * t], bufs[4 * t + 1]
            return pltpu.make_async_remote_copy(src_ref=send1.at[k], dst_ref=recv1.at[k], send_sem=s1_sems.at[4 * t + k],
                                                recv_sem=r1_sems.at[4 * t + k], device_id=(x, y, 1 - c), device_id_type=MESH)

        to_x, to_y, to_diag = 2 * (1 - x) + y, 2 * x + (1 - y), 2 * (1 - x) + (1 - y)
        x_dev, y_dev = (1 - x, y, c), (x, 1 - y, c)

        def half(ref, h):
            total = ref.shape[0]
            split = total // 2 // ROW_TILE_BF16 * ROW_TILE_BF16
            return ref.at[pl.ds(0, split)] if h == 0 else ref.at[pl.ds(split, total - split)]

        def hop(t, copy_id, src, dst, to):
            return pltpu.make_async_remote_copy(src_ref=src, dst_ref=dst, send_sem=s2_sems.at[HOPS * t + copy_id],
                                                recv_sem=r2_sems.at[HOPS * t + copy_id], device_id=to, device_id_type=MESH)

        def hops(t):
            send2, landing = bufs[4 * t + 2], bufs[4 * t + 3]
            return [hop(t, 0, half(send2.at[to_diag], 0), half(landing.at[0], 0), x_dev),
                    hop(t, 1, half(send2.at[to_diag], 1), half(landing.at[0], 1), y_dev),
                    hop(t, 2, half(send2.at[to_x], 0), half(landing.at[1], 0), x_dev),
                    hop(t, 3, half(send2.at[to_y], 1), half(landing.at[2], 1), y_dev),
                    hop(t, 4, half(send2.at[to_x], 1), half(landing.at[1], 1), x_dev),
                    hop(t, 5, half(send2.at[to_y], 0), half(landing.at[2], 0), y_dev)]

        def add_relay(t, slot, h):
            dst, src = half(bufs[4 * t + 2].at[slot], h), half(bufs[4 * t + 3].at[0], h)
            dst[...] = (dst[...].astype(F32) + src[...].astype(F32)).astype(dst.dtype)

        for t in range(n_t):
            send2 = bufs[4 * t + 2]
            pad = send2.shape[1] - tensors[t][1]
            if pad:
                send2[:, pl.ds(tensors[t][1], pad), :] = jnp.zeros((4, pad, send2.shape[2]), send2.dtype)
            for j in range(N_DEV):
                @pl.when((j & 1) != c)
                def _():
                    put(t, bufs[4 * t].at[j >> 1], j)
            for k in range(4):
                swap(t, k).start()

        small_buf[4 * x + 2 * y + c] = small_ref[...]
        _exchange_blocks([small_buf], small_send, small_recv)
        total = small_buf[0]
        for d in range(1, N_DEV):
            total = total + small_buf[d]
        small_sum[...] = total

        for t in range(n_t):
            recv1 = bufs[4 * t + 1]
            for k in range(4):
                swap(t, k).wait_recv()
                for j in (2 * k, 2 * k + 1):
                    @pl.when(((j & 1) == c) & (k != chip))
                    def _():
                        put(t, bufs[4 * t + 2].at[k], j, add=recv1.at[k])

                    @pl.when(((j & 1) == c) & (k == chip))
                    def _():
                        put(t, out_refs[t], j, add=recv1.at[k])
            for cp in hops(t)[0:4]:
                cp.start()

        for t in range(n_t):
            cps = hops(t)
            cps[0].wait_recv()
            add_relay(t, to_y, 0)
            cps[5].start()
            cps[1].wait_recv()
            add_relay(t, to_x, 1)
            cps[4].start()

        for t in range(n_t):
            cps, rows = hops(t), tensors[t][1]
            for first, second, slot in ((cps[2], cps[4], 1), (cps[3], cps[5], 2)):
                first.wait_recv()
                second.wait_recv()
                out_refs[t][...] += bufs[4 * t + 3][slot, pl.ds(0, rows), :].astype(F32)

        for t in range(n_t):
            for cp in hops(t):
                cp.wait_send()
            for k in range(4):
                swap(t, k).wait_send()

    scratch, out_shape = [], []
    for parts, block_rows in tensors:
        cols = parts[0][0].shape[1]
        tiled_rows = -(-block_rows // ROW_TILE_BF16) * ROW_TILE_BF16
        scratch += [pltpu.VMEM((4, block_rows, cols), MXU_DTYPE)] * 2
        scratch += [pltpu.VMEM((4, tiled_rows, cols), MXU_DTYPE), pltpu.VMEM((3, tiled_rows, cols), MXU_DTYPE)]
        out_shape.append(jax.ShapeDtypeStruct((block_rows, cols), F32))
    out_shape.append(jax.ShapeDtypeStruct(small.shape, F32))
    scratch += [pltpu.VMEM((N_DEV,) + small.shape, F32)] + [pltpu.SemaphoreType.DMA((4 * n_t,))] * 2
    scratch += [pltpu.SemaphoreType.DMA((HOPS * n_t,))] * 2 + _exchange_sems(1)
    return pl.pallas_call(
        body, name=name, in_specs=[VMEM_SPEC] * (len(arrays) + 1), out_specs=[VMEM_SPEC] * (n_t + 1), out_shape=out_shape,
        scratch_shapes=scratch, compiler_params=pltpu.CompilerParams(vmem_limit_bytes=VMEM_LIMIT),
    )(*arrays, small)


def _adamw_step(w, g, m, v):
    mn = ADAM_B1 * m + (1.0 - ADAM_B1) * g
    vn = ADAM_B2 * v + (1.0 - ADAM_B2) * jnp.square(g)
    m_hat = mn / (1.0 - ADAM_B1 ** ADAM_STEP)
    v_hat = vn / (1.0 - ADAM_B2 ** ADAM_STEP)
    return -ADAM_LR * (m_hat / (jnp.sqrt(v_hat) + ADAM_EPS) + ADAM_WD * w), mn, vn


def adamw(w, g, m, v, name):
    rows, cols = w.shape
    tr = 256 if rows % 256 == 0 else rows

    def body(w_ref, g_ref, m_ref, v_ref, d_ref, nm_ref, nv_ref):
        d_ref[...], nm_ref[...], nv_ref[...] = _adamw_step(w_ref[...], g_ref[...], m_ref[...], v_ref[...])

    spec = pl.BlockSpec((tr, cols), lambda i: (i, 0))
    shape = jax.ShapeDtypeStruct((rows, cols), F32)
    return pl.pallas_call(body, grid=(rows // tr,), name=name, in_specs=[spec] * 4, out_specs=[spec] * 3,
                          out_shape=[shape] * 3, compiler_params=_cparams("arbitrary"))(w, g, m, v)


def adamw_w_in(w, g_t, m, v):
    def body(w_ref, g_ref, m_ref, v_ref, go_ref, d_ref, nm_ref, nv_ref):
        g = g_ref[...]
        go_ref[:, 0, :] = g
        d_ref[:, 0, :], nm_ref[:, 0, :], nv_ref[:, 0, :] = _adamw_step(w_ref[:, 0, :], g, m_ref[:, 0, :], v_ref[:, 0, :])

    return pl.pallas_call(body, name="adamw_w_in", in_specs=[VMEM_SPEC] * 4, out_specs=[VMEM_SPEC] * 4,
                          out_shape=[jax.ShapeDtypeStruct(w.shape, F32)] * 4,
                          compiler_params=pltpu.CompilerParams(vmem_limit_bytes=VMEM_LIMIT))(w, g_t, m, v)


def _pad_rows(a, rows=8):
    return jnp.pad(a, ((0, rows - a.shape[0]), (0, 0)))


def _pad_lanes(a, lanes=128):
    return jnp.pad(a, ((0, 0), (0, lanes - a.shape[1])))


def kernel(x, meta_tokens, norm_w, w_in, conv_w, hg_lb_logits, hg_norm_w, gdn_A_log, gdn_dt_bias, gdn_norm_w, w_out, final_norm_w, loss_target, m_meta_tokens, m_norm_w, m_w_in, m_conv_w, m_hg_lb_logits, m_hg_norm_w, m_gdn_A_log, m_gdn_dt_bias, m_gdn_norm_w, m_w_out, m_final_norm_w, v_meta_tokens, v_norm_w, v_w_in, v_conv_w, v_hg_lb_logits, v_hg_norm_w, v_gdn_A_log, v_gdn_dt_bias, v_gdn_norm_w, v_w_out, v_final_norm_w):
    b, seq, _ = x.shape
    n = b * seq
    dev = 4 * lax.axis_index("x") + 2 * lax.axis_index("y") + lax.axis_index("c")
    col_shard = IN_COLS // N_DEV

    small_w = jnp.concatenate([_pad_lanes(meta_tokens, 256), _pad_rows(_pad_lanes(conv_w[0], 256))], axis=0)
    w_t, w_out_g, small_g = gather_weights(jnp.transpose(w_in, (2, 0, 1)), w_out[0], small_w, AB_PAD - 2 * HEADS)
    meta_g = small_g[:, 0:N_META, 0:D_MODEL // N_DEV]
    conv_g = small_g[:, N_META:N_META + CONV_TAPS, 0:QKV // N_DEV]
    w_out_full = w_out_g.reshape(2 * WIDTH, D_MODEL)
    cw = jnp.transpose(conv_g, (1, 0, 2)).reshape(CONV_TAPS, QKV)
    meta = jnp.transpose(meta_g, (1, 0, 2)).reshape(N_META, D_MODEL)
    alog = _pad_lanes(gdn_A_log)
    dtb = _pad_lanes(gdn_dt_bias)
    fw = final_norm_w.reshape(1, D_MODEL)

    h0 = jnp.concatenate([jnp.zeros((CHUNK - N_META, D_MODEL), F32), meta], axis=0)
    x2 = x.reshape(n, D_MODEL)
    phg, pgd, pab, phg0, pgd0, pab0, u0 = in_proj(x2, h0, norm_w, w_t)
    phg3, pgd3, pab3 = phg.reshape(b, seq, 4 * WIDTH), pgd.reshape(b, seq, 4 * WIDTH), pab.reshape(b, seq, AB_PAD)
    hg_loc, hg_lead = hg_local_fwd(phg3, phg0, hg_lb_logits)
    gd_loc, gd_inv, gd_lead, gd_inv0 = gd_local_fwd(pgd3, pgd0, pab3, pab0, cw, alog, dtb)
    (y_hg, s_hg), (y_gd, s_gd) = run_scans([hg_scan_fwd(phg3, phg0, hg_loc, hg_lead, hg_norm_w),
                                            gd_scan_fwd(pgd3, pgd0, gd_loc, gd_lead, gdn_norm_w)],
                                           seq // (SCAN_CHUNKS_FWD * CHUNK), "scans")

    dh2, dy_hg, dy_gd, g_w_out, loss_part, g_fw = out_proj_loss(
        x2, loss_target.reshape(n, D_MODEL), y_hg.reshape(n, WIDTH), y_gd.reshape(n, WIDTH), w_out_full, fw)

    hb, gb = run_scans([hg_scan_bwd(phg3, phg0, hg_loc, hg_lead, hg_norm_w, s_hg, dy_hg.reshape(b, seq, WIDTH)),
                        gd_scan_bwd(pgd3, pgd0, gd_loc, gd_lead, gdn_norm_w, s_gd, dy_gd.reshape(b, seq, WIDTH))],
                       seq // (SCAN_CHUNKS * CHUNK), "scans_bwd")
    dphg, dphg0, g_lb = hg_local_bwd(phg3, phg0, hg_lb_logits, hb[0:6], hb[6:12])
    g_hg_nw = hb[12]
    dpgd, dpab, dpgd0, dpab0, g_cw, g_alog, g_dtb = gd_local_bwd(pgd3, pgd0, pab3, pab0, cw, alog, dtb, gd_inv, gd_inv0,
                                                                 gb[0:6], gb[6], gb[7:13], gb[13])
    g_gd_nw = gb[14]
    dphg, dpgd, dpab = dphg.reshape(n, 4 * WIDTH), dpgd.reshape(n, 4 * WIDTH), dpab.reshape(n, AB_PAD)

    grad_x, dh0, g_nw, g_w_hg, g_w_gd, g_w_ab = in_proj_bwd(dphg, dpgd, dpab, w_t, x2, dh2, norm_w, h0, u0, dphg0, dpgd0, dpab0)

    small = jnp.concatenate([
        g_nw.reshape(8, 128), g_lb.reshape(8, 128), _pad_rows(g_hg_nw), _pad_rows(g_alog), _pad_rows(g_dtb), _pad_rows(g_gd_nw),
        g_fw.reshape(8, 128), g_cw.reshape(48, 128),
        dh0[CHUNK - N_META:CHUNK].reshape(128, 128), loss_part], axis=0)
    g_w_in_t, g_w_out, small = reduce_gradients(
        [([(g_w_hg, 4 * WIDTH), (g_w_gd, 4 * WIDTH), (g_w_ab, 2 * HEADS)], col_shard),
         ([(g_w_out, 2 * WIDTH)], (2 * WIDTH) // N_DEV)], small, "reduce_gradients")
    g_norm_w = small[0:8].reshape(1, D_MODEL)
    g_lb = small[8:16].reshape(2, WIDTH)
    g_hg_nw = small[16:17]
    g_alog = small[24:25, 0:HEADS]
    g_dtb = small[32:33, 0:HEADS]
    g_gd_nw = small[40:41]
    g_fw = small[48:56].reshape(1, D_MODEL)
    g_cw_full = small[56:104].reshape(CONV_TAPS, QKV)
    g_meta_full = small[104:232].reshape(N_META, D_MODEL)
    loss = small[232, 0]
    g_conv = lax.dynamic_slice_in_dim(g_cw_full, dev * (QKV // N_DEV), QKV // N_DEV, axis=1)
    g_meta = lax.dynamic_slice_in_dim(g_meta_full, dev * (D_MODEL // N_DEV), D_MODEL // N_DEV, axis=1)

    names = ["meta_tokens", "norm_w", "w_in", "conv_w", "hg_lb_logits", "hg_norm_w", "gdn_A_log", "gdn_dt_bias",
             "gdn_norm_w", "w_out", "final_norm_w"]
    weights = [meta_tokens, norm_w, w_in, conv_w, hg_lb_logits, hg_norm_w, gdn_A_log, gdn_dt_bias, gdn_norm_w, w_out,
               final_norm_w]
    moms = [m_meta_tokens, m_norm_w, m_w_in, m_conv_w, m_hg_lb_logits, m_hg_norm_w, m_gdn_A_log, m_gdn_dt_bias,
            m_gdn_norm_w, m_w_out, m_final_norm_w]
    vars_ = [v_meta_tokens, v_norm_w, v_w_in, v_conv_w, v_hg_lb_logits, v_hg_norm_w, v_gdn_A_log, v_gdn_dt_bias,
             v_gdn_norm_w, v_w_out, v_final_norm_w]
    grads2d = [g_meta, g_norm_w, g_w_in_t, g_conv, g_lb, g_hg_nw, g_alog, g_dtb, g_gd_nw, g_w_out, g_fw]
    grads, deltas, new_ms, new_vs = [], [], [], []
    for nm, w, g2, m, v in zip(names, weights, grads2d, moms, vars_):
        if nm == "w_in":
            to3, back = (lambda a: jnp.transpose(a, (2, 0, 1))), (lambda a: jnp.transpose(a, (1, 2, 0)))
            g2, d, nm_, nv_ = adamw_w_in(to3(w), g2, to3(m), to3(v))
        else:
            to2d, back = (lambda a, s=g2.shape: a.reshape(s)), (lambda a, s=w.shape: a.reshape(s))
            d, nm_, nv_ = adamw(to2d(w), g2, to2d(m), to2d(v), "adamw_" + nm)
        grads.append(back(g2))
        deltas.append(back(d))
        new_ms.append(back(nm_))
        new_vs.append(back(nv_))
    return (loss, grad_x.reshape(x.shape), *grads, *deltas, *new_ms, *new_vs)
```

```python
import jax
import jax.numpy as jnp
import numpy as np
from jax import lax
from jax.experimental import pallas as pl
from jax.experimental.pallas import tpu as pltpu

F32 = jnp.float32
BF16 = jnp.bfloat16
MXU_DTYPE = BF16

D_MODEL = 1024
N_META = 16
CHUNK = 64
SUB = 16
ROW_TILE_BF16 = 16
HEADS = 4
DH = 128
WIDTH = HEADS * DH
QKV = 3 * WIDTH
CONV_TAPS = 4
HALO = 8
EPS = 1e-6
IN_COLS = 4 * WIDTH + 4 * WIDTH + 2 * HEADS
AB_PAD = 128
N_DEV = 8
LOCAL_CHUNKS = 4
SCAN_CHUNKS_FWD = 4
SCAN_CHUNKS = 2
VMEM_LIMIT = 56 * 1024 * 1024
VMEM_LIMIT_LARGE = 60 * 1024 * 1024

ADAM_LR = 0.001
ADAM_B1 = 0.9
ADAM_B2 = 0.999
ADAM_EPS = 1e-08
ADAM_WD = 0.01
ADAM_STEP = 10

VMEM_SPEC = pl.BlockSpec(memory_space=pltpu.VMEM)
MESH = pl.DeviceIdType.MESH


def _mm_tn(a, b):
    return lax.dot_general(a.astype(MXU_DTYPE), b.astype(MXU_DTYPE), (((0,), (0,)), ((), ())), preferred_element_type=F32)


def _bmm(a, b):
    return lax.dot_general(a.astype(MXU_DTYPE), b.astype(MXU_DTYPE), (((2,), (1,)), ((0,), (0,))), preferred_element_type=F32)


def _bmm_nt(a, b):
    return lax.dot_general(a.astype(MXU_DTYPE), b.astype(MXU_DTYPE), (((2,), (2,)), ((0,), (0,))), preferred_element_type=F32)


def _bmm_tn(a, b):
    return lax.dot_general(a.astype(MXU_DTYPE), b.astype(MXU_DTYPE), (((1,), (1,)), ((0,), (0,))), preferred_element_type=F32)


def _iota2(n, m):
    return lax.broadcasted_iota(jnp.int32, (n, m), 0), lax.broadcasted_iota(jnp.int32, (n, m), 1)


def _silu(x):
    return x * jax.nn.sigmoid(x)


def _gated_norm(o, z, nw):
    return o * lax.rsqrt(jnp.mean(o * o, axis=-1, keepdims=True) + EPS) * nw * _silu(z)


def _heads(a, nb):
    return jnp.stack([a[c * CHUNK:(c + 1) * CHUNK, h * DH:(h + 1) * DH] for c in range(nb) for h in range(HEADS)], axis=0)


def _unheads(a3, nb):
    return jnp.concatenate(
        [jnp.concatenate([a3[c * HEADS + h] for h in range(HEADS)], axis=1) for c in range(nb)], axis=0)


def _split3(x):
    hi = x.astype(BF16)
    r1 = x - hi.astype(F32)
    mid = r1.astype(BF16)
    return hi, mid, (r1 - mid.astype(F32)).astype(BF16)


def _summation_matrices(pattern, n_out):
    s = pattern(np.arange(n_out)[:, None], np.arange(CHUNK)[None, :]).astype(np.float32)
    return jnp.asarray(np.tile(s, (1, 3)), BF16), jnp.asarray(np.tile(s.T, (1, 2)), BF16)


def _select_rows(mats, chunks):
    width = chunks[0].shape[1]
    out = _summation(*mats, jnp.concatenate(chunks, axis=1))
    return [out[:, c * width:(c + 1) * width] for c in range(len(chunks))]


def _summation_impl(s, v):
    return jnp.dot(s, jnp.concatenate(_split3(v), axis=0), preferred_element_type=F32)


@jax.custom_vjp
def _summation(s, s_t, v):
    return _summation_impl(s, v)


def _summation_fwd(s, s_t, v):
    return _summation_impl(s, v), s_t


def _summation_bwd(s_t, d):
    hi = d.astype(BF16)
    return None, None, jnp.dot(s_t, jnp.concatenate([hi, (d - hi.astype(F32)).astype(BF16)], axis=0),
                               preferred_element_type=F32)


_summation.defvjp(_summation_fwd, _summation_bwd)


def _chunks(x, nb):
    return [x[c * CHUNK:(c + 1) * CHUNK] for c in range(nb)]


def _running_sum(i, j):
    return j <= i


HG_LEVELS = 6


def _hg_sums(i, j):
    lvl, t = i >> HG_LEVELS, i & (CHUNK - 1)
    last = t
    for l in range(1, HG_LEVELS + 1):
        width = HG_LEVELS + 1 - l
        last = np.where(lvl == l, ((t >> width) << width) + (CHUNK >> l) - 1, last)
    return j <= last


def hg_local(p, logits, sum_mats):
    nb = p.shape[0] // CHUNK
    l0, l1 = logits[0:1], logits[1:2]
    mx = jnp.maximum(l0, l1)
    e0, e1 = jnp.exp(l0 - mx), jnp.exp(l1 - mx)
    lb = e0 / (e0 + e1)
    q = _silu(p[:, 0:WIDTH])
    f = lb + (1.0 - lb) * jax.nn.sigmoid(p[:, WIDTH:2 * WIDTH])
    k = 1.0 - f
    logf = jnp.log(f)
    sums = _select_rows(sum_mats, _chunks(logf, nb))
    level = lambda l: _heads(jnp.concatenate([s[l * CHUNK:(l + 1) * CHUNK] for s in sums], axis=0), nb)
    q3, k3, v3, g3 = _heads(q, nb), _heads(k, nb), _heads(p[:, 2 * WIDTH:3 * WIDTH], nb), level(0)
    r, c = _iota2(CHUNK, CHUNK)
    row = lax.broadcasted_iota(jnp.int32, (CHUNK, DH), 0)
    a = jnp.where(r == c, _bmm_nt(q3, k3), 0.0)
    for l in range(1, HG_LEVELS + 1):
        sh = HG_LEVELS - l
        qk = jnp.where(((row >> sh) & 1) == 1, q3, k3) * jnp.exp(-jnp.abs(g3 - level(l)))
        pair = ((r >> (sh + 1)) == (c >> (sh + 1))) & (((r >> sh) & 1) == 1) & (((c >> sh) & 1) == 0)
        a = a + jnp.where(pair, _bmm_nt(qk, qk), 0.0)
    o = _bmm(a, v3)
    glast = g3[:, CHUNK - 1:CHUNK, :]
    egs = tuple(jnp.concatenate([jnp.exp(glast[c * HEADS + h]) for h in range(HEADS)], axis=1) for c in range(nb))
    return _unheads(q3 * jnp.exp(g3), nb), _unheads(k3 * jnp.exp(glast - g3), nb), _unheads(o, nb), egs


def hg_scan(q_in, k_out, v, eg, o_intra, z, nw, st):
    o = o_intra + _bmm_nt(q_in, st)
    return _gated_norm(o, z, nw), st * eg + _bmm_tn(v, k_out)


def _tri_y_impl(a):
    r, c = _iota2(CHUNK, CHUNK)
    same16 = (r // SUB) == (c // SUB)
    same32 = (r // (2 * SUB)) == (c // (2 * SUB))
    a0 = jnp.where(same16, a, 0.0)
    y = -a0
    pw = _bmm(a0, a0)
    for _ in range(2):
        y = y + pw + _bmm(y, pw)
        pw = _bmm(pw, pw)
    y = y + pw + _bmm(y, pw)
    for ak in (jnp.where(same32 & jnp.logical_not(same16), a, 0.0), jnp.where(same32, 0.0, a)):
        m = ak + _bmm(y, ak)
        y = y - (m + _bmm(m, y))
    return y


@jax.custom_vjp
def _tri_y(a):
    return _tri_y_impl(a)


def _tri_y_fwd(a):
    y = _tri_y_impl(a)
    return y, y


def _tri_y_bwd(y, dy):
    n = dy + _bmm_tn(y, dy)
    return (-(n + _bmm_nt(n, y)),)


_tri_y.defvjp(_tri_y_fwd, _tri_y_bwd)


def _saved_inverse(y):
    @jax.custom_vjp
    def inverse(a):
        return y

    inverse.defvjp(lambda a: (y, None), lambda _, dy: _tri_y_bwd(y, dy))
    return inverse


def _head_rows(a3, nb):
    return jnp.concatenate([a3[g] for g in range(nb * HEADS)], axis=0)


def _rows_down(x, s):
    rows = x.shape[0]

    @jax.custom_vjp
    def rotate(v):
        return pltpu.roll(v, s, 0)

    rotate.defvjp(lambda v: (pltpu.roll(v, s, 0), None), lambda _, d: (pltpu.roll(d, rows - s, 0),))
    return rotate(x)


def gd_local(xx, ab, cw, alog, dtb, sum_mats, inverse=_tri_y):
    n = ab.shape[0]
    nb = n // CHUNK
    conv = cw[CONV_TAPS - 1:CONV_TAPS] * xx[HALO:HALO + n]
    for j in range(CONV_TAPS - 1):
        conv = conv + cw[j:j + 1] * _rows_down(xx, CONV_TAPS - 1 - j)[HALO:HALO + n]
    act = _silu(conv)
    x = ab + dtb
    g_all = -jnp.exp(alog) * (jnp.maximum(x, 0.0) + jnp.log1p(jnp.exp(-jnp.abs(x))))
    beta_all = jax.nn.sigmoid(ab)
    gam_all = jnp.concatenate(_select_rows(sum_mats, _chunks(g_all, nb)), axis=0)
    q3, k3, v3 = _heads(act[:, 0:WIDTH], nb), _heads(act[:, WIDTH:2 * WIDTH], nb), _heads(act[:, 2 * WIDTH:QKV], nb)
    q3 = q3 * lax.rsqrt(jnp.sum(q3 * q3, axis=-1, keepdims=True) + EPS) * (DH ** -0.5)
    k3 = k3 * lax.rsqrt(jnp.sum(k3 * k3, axis=-1, keepdims=True) + EPS)
    pairs = [(c, h) for c in range(nb) for h in range(HEADS)]
    beta = jnp.stack([beta_all[c * CHUNK:(c + 1) * CHUNK, HEADS + h:HEADS + h + 1] for c, h in pairs], axis=0)
    gam = jnp.stack([gam_all[c * CHUNK:(c + 1) * CHUNK, h:h + 1] for c, h in pairs], axis=0)
    gam_t = [gam_all[c * CHUNK:(c + 1) * CHUNK].T for c in range(nb)]
    gam_row = jnp.stack([gam_t[c][h:h + 1, :] for c, h in pairs], axis=0)
    glast = gam[:, CHUNK - 1:CHUNK, :]
    r, c = _iota2(CHUNK, CHUNK)
    dec = jnp.exp(jnp.where(c < r, gam - gam_row, -jnp.inf))
    y = inverse(beta * _bmm_nt(k3, k3) * dec)
    eg = jnp.exp(gam)
    rhs = jnp.concatenate([beta * v3, (beta * eg) * k3], axis=2)
    sol = rhs + _bmm(y, rhs)
    qk = _bmm_nt(q3, k3) * jnp.where(r == c, 1.0, dec)
    eas = tuple(jnp.exp(gam_all[(c + 1) * CHUNK - 1:(c + 1) * CHUNK]) for c in range(nb))
    return (_unheads(sol[:, :, 0:DH], nb), _unheads(sol[:, :, DH:2 * DH], nb), _unheads(q3 * eg, nb),
            _unheads(k3 * jnp.exp(glast - gam), nb), _head_rows(qk, nb), eas), _head_rows(y, nb)


def gd_scan(uu, ww, qe, ke, qk, ea, z, nw, s):
    u = uu - _bmm(ww, s)
    o = _bmm(qe, s) + _bmm(qk, u)
    return _gated_norm(o, z, nw), ea * s + _bmm_tn(ke, u)


def _cparams(*sem):
    return pltpu.CompilerParams(dimension_semantics=sem, vmem_limit_bytes=VMEM_LIMIT)


def _row_tile(n):
    for t in (512, 256, 128, 64):
        if n % t == 0:
            return t
    raise ValueError(f"unsupported token count {n}")


def _w_in_specs():
    once = pl.Buffered(1)
    return [pl.BlockSpec((4 * WIDTH, D_MODEL), lambda *i: (0, 0), pipeline_mode=once),
            pl.BlockSpec((4 * WIDTH, D_MODEL), lambda *i: (1, 0), pipeline_mode=once),
            pl.BlockSpec((AB_PAD, D_MODEL), lambda *i: (8 * WIDTH // AB_PAD, 0), pipeline_mode=once)]


def in_proj(h, h0, norm_w, w_t):
    n = h.shape[0]
    tm = _row_tile(n)
    nt = (((1,), (1,)), ((), ()))

    def body(h_ref, h0_ref, nw_ref, whg_ref, wgd_ref, wab_ref, phg_ref, pgd_ref, pab_ref, phg0_ref, pgd0_ref, pab0_ref, u0_ref):
        def project(x, hg_ref, gd_ref, ab_ref):
            u = (x * lax.rsqrt(jnp.mean(x * x, axis=-1, keepdims=True) + EPS) * nw_ref[...]).astype(MXU_DTYPE)
            hg_ref[...] = lax.dot_general(u, whg_ref[...], nt, preferred_element_type=F32)
            gd_ref[...] = lax.dot_general(u, wgd_ref[...], nt, preferred_element_type=F32)
            ab_ref[...] = lax.dot_general(u, wab_ref[...], nt, preferred_element_type=F32)
            return u

        @pl.when(pl.program_id(0) == 0)
        def _():
            u0_ref[...] = project(h0_ref[...], phg0_ref, pgd0_ref, pab0_ref)

        project(h_ref[...], phg_ref, pgd_ref, pab_ref)

    n0 = h0.shape[0]
    row = lambda w: pl.BlockSpec((tm, w), lambda i: (i, 0))
    lead = lambda w: pl.BlockSpec((n0, w), lambda i: (0, 0))
    widths = [4 * WIDTH, 4 * WIDTH, AB_PAD]
    return pl.pallas_call(
        body, grid=(n // tm,), name="in_proj",
        in_specs=[row(D_MODEL), lead(D_MODEL), pl.BlockSpec(norm_w.shape, lambda i: (0, 0))] + _w_in_specs(),
        out_specs=[row(w) for w in widths] + [lead(w) for w in widths] + [lead(D_MODEL)],
        out_shape=[jax.ShapeDtypeStruct((n, w), F32) for w in widths] + [jax.ShapeDtypeStruct((n0, w), F32) for w in widths]
        + [jax.ShapeDtypeStruct((n0, D_MODEL), MXU_DTYPE)],
        compiler_params=_cparams("arbitrary"),
    )(h, h0, norm_w, w_t, w_t, w_t)


def out_proj_loss(x, tgt, y_hg, y_gd, w_out, fw):
    n = x.shape[0]
    tm = _row_tile(n)
    inv_d = 1.0 / D_MODEL

    def body(x_ref, t_ref, yh_ref, yg_ref, w_ref, fw_ref, dh_ref, dyh_ref, dyg_ref, dw_ref, loss_ref, dfw_ref):
        @pl.when(pl.program_id(0) == 0)
        def _():
            dw_ref[...] = jnp.zeros_like(dw_ref)
            loss_ref[...] = jnp.zeros_like(loss_ref)
            dfw_ref[...] = jnp.zeros_like(dfw_ref)

        yh, yg = yh_ref[...], yg_ref[...]
        wa, wb = w_ref[0:WIDTH, :], w_ref[WIDTH:2 * WIDTH, :]
        h2 = x_ref[...] + jnp.dot(yh, wa, preferred_element_type=F32) + jnp.dot(yg, wb, preferred_element_type=F32)
        r2 = lax.rsqrt(jnp.mean(h2 * h2, axis=-1, keepdims=True) + EPS)
        nrm = h2 * r2
        fwv = fw_ref[...]
        err = nrm * fwv - t_ref[...]
        loss_ref[...] += jnp.full(loss_ref.shape, 0.5 * inv_d * jnp.sum(err * err), F32)
        dout = err * inv_d
        dfw_ref[...] += jnp.sum(dout * nrm, axis=0, keepdims=True)
        dn = dout * fwv
        dh2 = r2 * (dn - nrm * jnp.mean(dn * nrm, axis=-1, keepdims=True))
        dh_ref[...] = dh2
        dhb = dh2.astype(MXU_DTYPE)
        dyh_ref[...] = lax.dot_general(dhb, wa, (((1,), (1,)), ((), ())), preferred_element_type=F32)
        dyg_ref[...] = lax.dot_general(dhb, wb, (((1,), (1,)), ((), ())), preferred_element_type=F32)
        dw_ref[0:WIDTH, :] += lax.dot_general(yh, dhb, (((0,), (0,)), ((), ())), preferred_element_type=F32)
        dw_ref[WIDTH:2 * WIDTH, :] += lax.dot_general(yg, dhb, (((0,), (0,)), ((), ())), preferred_element_type=F32)

    row = lambda w: pl.BlockSpec((tm, w), lambda i: (i, 0))
    full = lambda s: pl.BlockSpec(s, lambda i: (0, 0))
    return pl.pallas_call(
        body, grid=(n // tm,), name="out_proj_loss",
        in_specs=[row(D_MODEL), row(D_MODEL), row(WIDTH), row(WIDTH), full(w_out.shape), full(fw.shape)],
        out_specs=[row(D_MODEL), row(WIDTH), row(WIDTH), full((2 * WIDTH, D_MODEL)), full((8, 128)), full((1, D_MODEL))],
        out_shape=[jax.ShapeDtypeStruct((n, D_MODEL), F32), jax.ShapeDtypeStruct((n, WIDTH), F32),
                   jax.ShapeDtypeStruct((n, WIDTH), F32), jax.ShapeDtypeStruct((2 * WIDTH, D_MODEL), F32),
                   jax.ShapeDtypeStruct((8, 128), F32), jax.ShapeDtypeStruct((1, D_MODEL), F32)],
        compiler_params=_cparams("arbitrary"),
    )(x, tgt, y_hg, y_gd, w_out, fw)


def in_proj_bwd(dphg, dpgd, dpab, w_t, h, dh2, norm_w, h0, u0, dphg0, dpgd0, dpab0):
    n = h.shape[0]
    tm = _row_tile(n)
    steps = n // tm

    def body(dphg_ref, dpgd_ref, dpab_ref, whg_ref, wgd_ref, wab_ref, h_ref, dh2_ref, nw_ref, h0_ref, u0_ref, d0hg_ref,
             d0gd_ref, d0ab_ref, dx_ref, dx0_ref, dnw_ref, ghg_ref, ggd_ref, gab_ref, acc_hg, acc_gd, acc_ab):
        i = pl.program_id(0)
        nwv = nw_ref[...]

        def norm_bwd(dps, x):
            du = jnp.dot(dps[0], whg_ref[...], preferred_element_type=F32)
            du += jnp.dot(dps[1], wgd_ref[...], preferred_element_type=F32)
            du += jnp.dot(dps[2], wab_ref[...], preferred_element_type=F32)
            r = lax.rsqrt(jnp.mean(x * x, axis=-1, keepdims=True) + EPS)
            nrm = x * r
            dn = du * nwv
            return r * (dn - nrm * jnp.mean(dn * nrm, axis=-1, keepdims=True)), nrm, jnp.sum(du * nrm, axis=0, keepdims=True)

        def accumulate(dps, u, first):
            for acc, dp in zip((acc_hg, acc_gd, acc_ab), dps):
                step = min(acc.shape[0], 512)
                for lo in range(0, acc.shape[0], step):
                    part = _mm_tn(dp[:, lo:lo + step], u)
                    acc[lo:lo + step, :] = part if first else acc[lo:lo + step, :] + part

        @pl.when(i == 0)
        def _():
            dps0 = (d0hg_ref[...], d0gd_ref[...], d0ab_ref[...])
            dx0_ref[...], _, dnw_ref[...] = norm_bwd(dps0, h0_ref[...])
            accumulate(dps0, u0_ref[...], True)

        dps = (dphg_ref[...], dpgd_ref[...], dpab_ref[...])
        dx, nrm, dnw = norm_bwd(dps, h_ref[...])
        dx_ref[...] = dh2_ref[...] + dx
        dnw_ref[...] += dnw
        accumulate(dps, (nrm * nwv).astype(MXU_DTYPE), False)

        @pl.when(i == steps - 1)
        def _():
            pltpu.sync_copy(acc_hg, ghg_ref)
            pltpu.sync_copy(acc_gd, ggd_ref)
            pltpu.sync_copy(acc_ab, gab_ref)

    row = lambda w: pl.BlockSpec((tm, w), lambda i: (i, 0))
    full = lambda a: pl.BlockSpec(a.shape, lambda i: (0, 0), pipeline_mode=pl.Buffered(1))
    anywhere = pl.BlockSpec(memory_space=pl.ANY)
    return pl.pallas_call(
        body, grid=(steps,), name="in_proj_bwd",
        in_specs=[row(4 * WIDTH), row(4 * WIDTH), row(AB_PAD)] + _w_in_specs() + [row(D_MODEL), row(D_MODEL), full(norm_w),
                                                                                   full(h0), full(u0), full(dphg0), full(dpgd0),
                                                                                   full(dpab0)],
        out_specs=[row(D_MODEL), pl.BlockSpec(h0.shape, lambda i: (0, 0)), pl.BlockSpec((1, D_MODEL), lambda i: (0, 0)),
                   anywhere, anywhere, anywhere],
        out_shape=[jax.ShapeDtypeStruct((n, D_MODEL), F32), jax.ShapeDtypeStruct(h0.shape, F32),
                   jax.ShapeDtypeStruct((1, D_MODEL), F32), jax.ShapeDtypeStruct((4 * WIDTH, D_MODEL), F32),
                   jax.ShapeDtypeStruct((4 * WIDTH, D_MODEL), F32), jax.ShapeDtypeStruct((AB_PAD, D_MODEL), F32)],
        scratch_shapes=[pltpu.VMEM((4 * WIDTH, D_MODEL), F32), pltpu.VMEM((4 * WIDTH, D_MODEL), F32),
                        pltpu.VMEM((AB_PAD, D_MODEL), F32)],
        compiler_params=pltpu.CompilerParams(dimension_semantics=("arbitrary",), vmem_limit_bytes=VMEM_LIMIT_LARGE),
    )(dphg, dpgd, dpab, w_t, w_t, w_t, h, dh2, norm_w, h0, u0, dphg0, dpgd0, dpab0)


def _sds(shape, dtype=F32):
    return jax.ShapeDtypeStruct(shape, dtype)


def _pairs(b):
    return [(i, h) for i in range(b) for h in range(HEADS)]


def _load_slabs(ref, b, k):
    return jnp.stack([ref[i, k * CHUNK:(k + 1) * CHUNK, h * DH:(h + 1) * DH].astype(F32) for i, h in _pairs(b)], axis=0)


def _lead_slabs(a, b):
    return jnp.stack([a[:, h * DH:(h + 1) * DH].astype(F32) for _, h in _pairs(b)], axis=0)


def _rows(a3, i):
    return jnp.concatenate([a3[i * HEADS + h] for h in range(HEADS)], axis=1)


def _store_slabs(ref, a3, b, k):
    for i in range(b):
        ref[i, k * CHUNK:(k + 1) * CHUNK, :] = _rows(a3, i).astype(ref.dtype)


def _sum_rows(a3, b):
    out = _rows(a3, 0)
    for i in range(1, b):
        out = out + _rows(a3, i)
    return out


def _save_states(ref, s, b, k):
    for i in range(b):
        ref[i, k] = jnp.concatenate([s[i * HEADS + h] for h in range(HEADS)], axis=0)


def _load_states(ref, b, k):
    return jnp.stack([ref[i, k, h * DH:(h + 1) * DH, :] for i, h in _pairs(b)], axis=0)


def hg_local_fwd(p, p0, logits):
    b, seq, _ = p.shape
    rows = LOCAL_CHUNKS * CHUNK
    nreal = seq // CHUNK

    def body(p_ref, p0_ref, lg_ref, s_ref, st_ref, q_ref, k_ref, o_ref, eg_ref, q0_ref, k0_ref, o0_ref, eg0_ref):
        sum_mats = (s_ref[...], st_ref[...])

        @pl.when((pl.program_id(0) == 0) & (pl.program_id(1) == 0))
        def _():
            q_in, k_out, o0_ref[...], (eg0_ref[...],) = hg_local(p0_ref[...], lg_ref[...], sum_mats)
            q0_ref[...], k0_ref[...] = q_in.astype(MXU_DTYPE), k_out.astype(MXU_DTYPE)

        q_in, k_out, o_intra, egs = hg_local(p_ref[...], lg_ref[...], sum_mats)
        q_ref[...], k_ref[...], o_ref[...] = q_in.astype(MXU_DTYPE), k_out.astype(MXU_DTYPE), o_intra
        for c in range(LOCAL_CHUNKS):
            eg_ref[c] = egs[c]

    slab = pl.BlockSpec((None, rows, WIDTH), lambda s, g: (s, g, 0))
    const = lambda shape: pl.BlockSpec(shape, lambda s, g: (0, 0))
    lead_shapes = [(CHUNK, WIDTH)] * 3 + [(1, WIDTH)]
    sum_mats = _summation_matrices(_hg_sums, (HG_LEVELS + 1) * CHUNK)
    out = pl.pallas_call(
        body, grid=(b, seq // rows), name="hgrn2_local",
        in_specs=[pl.BlockSpec((None, rows, 4 * WIDTH), lambda s, g: (s, g, 0)), const(p0.shape), const(logits.shape)]
        + [const(a.shape) for a in sum_mats],
        out_specs=[slab, slab, slab, pl.BlockSpec((None, LOCAL_CHUNKS, 1, WIDTH), lambda s, g: (s, g, 0, 0))]
        + [const(s) for s in lead_shapes],
        out_shape=[_sds((b, seq, WIDTH), MXU_DTYPE)] * 2 + [_sds((b, seq, WIDTH)), _sds((b, nreal, 1, WIDTH))]
        + [_sds(lead_shapes[0], MXU_DTYPE)] * 2 + [_sds(lead_shapes[2]), _sds(lead_shapes[3])],
        compiler_params=_cparams("arbitrary", "arbitrary"),
    )(p, p0, logits, *sum_mats)
    return out[0:4], out[4:8]


def _hg_scan_args(b, k, q_ref, k_ref, o_ref, v_ref, z_ref, eg_ref):
    eg = jnp.stack([eg_ref[i, k, :, h * DH:(h + 1) * DH] for i, h in _pairs(b)], axis=0)
    return (_load_slabs(q_ref, b, k), _load_slabs(k_ref, b, k), _load_slabs(v_ref, b, k), eg, _load_slabs(o_ref, b, k),
            _load_slabs(z_ref, b, k))


def _hg_lead_args(b, q0_ref, k0_ref, o0_ref, p0_ref, eg0_ref):
    eg = jnp.stack([eg0_ref[:, h * DH:(h + 1) * DH] for _, h in _pairs(b)], axis=0)
    return (_lead_slabs(q0_ref[...], b), _lead_slabs(k0_ref[...], b), _lead_slabs(p0_ref[:, 2 * WIDTH:3 * WIDTH], b), eg,
            _lead_slabs(o0_ref[...], b), _lead_slabs(p0_ref[:, 3 * WIDTH:4 * WIDTH], b))


def _scan_specs(b, ng, reverse, chunks):
    group = (lambda i: ng - 1 - i) if reverse else (lambda i: i)
    slab = lambda lane_block: pl.BlockSpec((b, chunks * CHUNK, WIDTH), lambda i: (0, group(i), lane_block))
    per_chunk = lambda *tail: pl.BlockSpec((b, chunks) + tail, lambda i: (0, group(i)) + (0,) * len(tail))
    const = lambda a: pl.BlockSpec(a.shape, lambda i: (0,) * a.ndim)
    return slab, per_chunk, const


def run_scans(parts, nc, name):
    n_in = [len(p["args"]) for p in parts]
    n_out = [len(p["out_shape"]) for p in parts]
    n_scr = [len(p["scratch_shapes"]) for p in parts]

    def body(*refs):
        ins, outs, scr = refs[:sum(n_in)], refs[sum(n_in):sum(n_in) + sum(n_out)], refs[sum(n_in) + sum(n_out):]
        for i, part in enumerate(parts):
            part["body"](*ins[sum(n_in[:i]):sum(n_in[:i + 1])], *outs[sum(n_out[:i]):sum(n_out[:i + 1])],
                         *scr[sum(n_scr[:i]):sum(n_scr[:i + 1])])

    flat = lambda key: [v for p in parts for v in p[key]]
    out = pl.pallas_call(body, grid=(nc,), name=name, in_specs=flat("in_specs"), out_specs=flat("out_specs"),
                         out_shape=flat("out_shape"), scratch_shapes=flat("scratch_shapes"),
                         compiler_params=_cparams("arbitrary"))(*flat("args"))
    return [out[sum(n_out[:i]):sum(n_out[:i + 1])] for i in range(len(parts))]


def hg_scan_fwd(p, p0, local, lead, nw):
    b, seq, _ = p.shape
    q_in, k_out, o_intra, eg = local
    slab, per_chunk, const = _scan_specs(b, seq // (SCAN_CHUNKS_FWD * CHUNK), False, SCAN_CHUNKS_FWD)

    def body(q_ref, k_ref, o_ref, v_ref, z_ref, eg_ref, q0_ref, k0_ref, o0_ref, p0_ref, eg0_ref, nw_ref, y_ref, ss_ref, st):
        @pl.when(pl.program_id(0) == 0)
        def _():
            st[...] = hg_scan(*_hg_lead_args(b, q0_ref, k0_ref, o0_ref, p0_ref, eg0_ref), nw_ref[...], jnp.zeros(st.shape, F32))[1]

        s = st[...]
        for k in range(SCAN_CHUNKS_FWD):
            _save_states(ss_ref, s, b, k)
            y, s = hg_scan(*_hg_scan_args(b, k, q_ref, k_ref, o_ref, v_ref, z_ref, eg_ref), nw_ref[...], s)
            _store_slabs(y_ref, y, b, k)
        st[...] = s

    return dict(
        body=body, args=(q_in, k_out, o_intra, p, p, eg, lead[0], lead[1], lead[2], p0, lead[3], nw),
        in_specs=[slab(0), slab(0), slab(0), slab(2), slab(3), per_chunk(1, WIDTH)] + [const(a) for a in lead[0:3]]
        + [const(p0), const(lead[3]), const(nw)],
        out_specs=[slab(0), per_chunk(WIDTH, DH)],
        out_shape=[_sds((b, seq, WIDTH), MXU_DTYPE), _sds((b, seq // CHUNK, WIDTH, DH))],
        scratch_shapes=[pltpu.VMEM((b * HEADS, DH, DH), F32)])


def hg_scan_bwd(p, p0, local, lead, nw, ssave, dy):
    b, seq, _ = p.shape
    ng = seq // (SCAN_CHUNKS * CHUNK)
    q_in, k_out, o_intra, eg = local
    slab, per_chunk, const = _scan_specs(b, ng, True, SCAN_CHUNKS)

    def body(q_ref, k_ref, o_ref, v_ref, z_ref, eg_ref, q0_ref, k0_ref, o0_ref, p0_ref, eg0_ref, nw_ref, ss_ref, dy_ref,
             dq_ref, dk_ref, do_ref, dv_ref, dz_ref, deg_ref, dq0_ref, dk0_ref, do0_ref, dv0_ref, dz0_ref, deg0_ref, dnw_ref,
             dst):
        i = pl.program_id(0)

        @pl.when(i == 0)
        def _():
            dst[...] = jnp.zeros_like(dst)
            dnw_ref[...] = jnp.zeros_like(dnw_ref)

        ds = dst[...]
        for k in reversed(range(SCAN_CHUNKS)):
            args = _hg_scan_args(b, k, q_ref, k_ref, o_ref, v_ref, z_ref, eg_ref)
            _, vjp = jax.vjp(hg_scan, *args, nw_ref[...], _load_states(ss_ref, b, k))
            dq, dk, dv, deg, do, dz, dnw, ds = vjp((_load_slabs(dy_ref, b, k), ds))
            dnw_ref[...] += dnw
            for ref, val in ((dq_ref, dq), (dk_ref, dk), (do_ref, do), (dv_ref, dv), (dz_ref, dz)):
                _store_slabs(ref, val, b, k)
            for j in range(b):
                deg_ref[j, k] = _rows(deg, j)
        dst[...] = ds

        @pl.when(i == ng - 1)
        def _():
            args = _hg_lead_args(b, q0_ref, k0_ref, o0_ref, p0_ref, eg0_ref)
            _, vjp = jax.vjp(hg_scan, *args, nw_ref[...], jnp.zeros(dst.shape, F32))
            dq, dk, dv, deg, do, dz, dnw, _ = vjp((jnp.zeros((b * HEADS, CHUNK, DH), F32), ds))
            dnw_ref[...] += dnw
            for ref, val in ((dq0_ref, dq), (dk0_ref, dk), (do0_ref, do), (dv0_ref, dv), (dz0_ref, dz), (deg0_ref, deg)):
                ref[...] = _sum_rows(val, b)

    lead_out = [const(a) for a in lead[0:3]] + [const(lead[0]), const(lead[0]), const(lead[3])]
    return dict(
        body=body, args=(q_in, k_out, o_intra, p, p, eg, lead[0], lead[1], lead[2], p0, lead[3], nw, ssave, dy),
        in_specs=[slab(0), slab(0), slab(0), slab(2), slab(3), per_chunk(1, WIDTH)] + [const(a) for a in lead[0:3]]
        + [const(p0), const(lead[3]), const(nw), per_chunk(WIDTH, DH), slab(0)],
        out_specs=[slab(0)] * 5 + [per_chunk(1, WIDTH)] + lead_out + [const(nw)],
        out_shape=[_sds((b, seq, WIDTH))] * 5 + [_sds(eg.shape)] + [_sds((CHUNK, WIDTH))] * 5 + [_sds((1, WIDTH)), _sds(nw.shape)],
        scratch_shapes=[pltpu.VMEM((b * HEADS, DH, DH), F32)])


def _hg_local_vjp(sum_mats, p, logits, dq, dk, do, degs, dv, dz):
    _, vjp = jax.vjp(lambda p_, logits_: hg_local(p_, logits_, sum_mats), p, logits)
    dp, dlg = vjp((dq, dk, do, degs))
    return dp + jnp.concatenate([jnp.zeros((p.shape[0], 2 * WIDTH), F32), dv, dz], axis=1), dlg


def hg_local_bwd(p, p0, logits, cot, cot0):
    b, seq, _ = p.shape
    rows = LOCAL_CHUNKS * CHUNK

    def body(p_ref, p0_ref, lg_ref, s_ref, st_ref, dq_ref, dk_ref, do_ref, dv_ref, dz_ref, deg_ref, dq0_ref, dk0_ref, do0_ref,
             dv0_ref, dz0_ref, deg0_ref, dp_ref, dp0_ref, dlg_ref):
        sum_mats = (s_ref[...], st_ref[...])

        @pl.when((pl.program_id(0) == 0) & (pl.program_id(1) == 0))
        def _():
            dp0, dlg_ref[...] = _hg_local_vjp(sum_mats, p0_ref[...], lg_ref[...], dq0_ref[...], dk0_ref[...], do0_ref[...],
                                              (deg0_ref[...],), dv0_ref[...], dz0_ref[...])
            dp0_ref[...] = dp0.astype(MXU_DTYPE)

        degs = tuple(deg_ref[c] for c in range(LOCAL_CHUNKS))
        dp, dlg = _hg_local_vjp(sum_mats, p_ref[...], lg_ref[...], dq_ref[...], dk_ref[...], do_ref[...], degs, dv_ref[...],
                                dz_ref[...])
        dp_ref[...] = dp.astype(MXU_DTYPE)
        dlg_ref[...] += dlg

    slab = pl.BlockSpec((None, rows, WIDTH), lambda s, g: (s, g, 0))
    wide = pl.BlockSpec((None, rows, 4 * WIDTH), lambda s, g: (s, g, 0))
    const = lambda a: pl.BlockSpec(a.shape, lambda s, g: (0, 0))
    sum_mats = _summation_matrices(_hg_sums, (HG_LEVELS + 1) * CHUNK)
    return pl.pallas_call(
        body, grid=(b, seq // rows), name="hgrn2_local_bwd",
        in_specs=[wide, const(p0), const(logits), const(sum_mats[0]), const(sum_mats[1]), slab, slab, slab, slab, slab,
                  pl.BlockSpec((None, LOCAL_CHUNKS, 1, WIDTH), lambda s, g: (s, g, 0, 0))] + [const(a) for a in cot0],
        out_specs=[wide, const(p0), const(logits)],
        out_shape=[_sds(p.shape, MXU_DTYPE), _sds(p0.shape, MXU_DTYPE), _sds(logits.shape)],
        compiler_params=_cparams("arbitrary", "arbitrary"),
    )(p, p0, logits, *sum_mats, *cot, *cot0)


def _halo_block(g):
    return jnp.maximum((LOCAL_CHUNKS * CHUNK // HALO) * g - 1, 0)


def _gd_window(g, p_ref, halo_ref, p0_ref):
    halo = jnp.where(g == 0, p0_ref[CHUNK - HALO:CHUNK, 0:QKV], halo_ref[...])
    return jnp.concatenate([halo, p_ref[:, 0:QKV]], axis=0)


def _lead_window(p0_ref):
    return jnp.concatenate([jnp.zeros((HALO, QKV), F32), p0_ref[:, 0:QKV]], axis=0)


def gd_local_fwd(p, p0, ab, ab0, cw, alog, dtb):
    b, seq, _ = p.shape
    rows = LOCAL_CHUNKS * CHUNK
    nreal = seq // CHUNK

    def body(p_ref, halo_ref, p0_ref, ab_ref, ab0_ref, cw_ref, al_ref, dt_ref, s_ref, st_ref, u_ref, w_ref, qe_ref, ke_ref,
             qk_ref, ea_ref, inv_ref, u0_ref, w0_ref, qe0_ref, ke0_ref, qk0_ref, ea0_ref, inv0_ref):
        sum_mats = (s_ref[...], st_ref[...])

        @pl.when((pl.program_id(0) == 0) & (pl.program_id(1) == 0))
        def _():
            (u0_ref[...], ww, qe, ke, qk0_ref[...], (ea0_ref[...],)), inv0_ref[...] = gd_local(
                _lead_window(p0_ref), ab0_ref[...], cw_ref[...], al_ref[...], dt_ref[...], sum_mats, inverse=_tri_y_impl)
            w0_ref[...], qe0_ref[...], ke0_ref[...] = ww.astype(MXU_DTYPE), qe.astype(MXU_DTYPE), ke.astype(MXU_DTYPE)

        (uu, ww, qe, ke, qk, eas), inv = gd_local(_gd_window(pl.program_id(1), p_ref, halo_ref, p0_ref), ab_ref[...],
                                                  cw_ref[...], al_ref[...], dt_ref[...], sum_mats, inverse=_tri_y_impl)
        u_ref[...], w_ref[...], qe_ref[...], ke_ref[...] = uu, ww.astype(MXU_DTYPE), qe.astype(MXU_DTYPE), ke.astype(MXU_DTYPE)
        for c in range(LOCAL_CHUNKS):
            qk_ref[c] = qk[c * HEADS * CHUNK:(c + 1) * HEADS * CHUNK]
            inv_ref[c] = inv[c * HEADS * CHUNK:(c + 1) * HEADS * CHUNK]
            ea_ref[c] = eas[c]

    const = lambda shape: pl.BlockSpec(shape, lambda s, g: (0, 0))
    slab = pl.BlockSpec((None, rows, WIDTH), lambda s, g: (s, g, 0))
    mats = pl.BlockSpec((None, LOCAL_CHUNKS, HEADS * CHUNK, CHUNK), lambda s, g: (s, g, 0, 0))
    lead_out = [_sds((CHUNK, WIDTH))] + [_sds((CHUNK, WIDTH), MXU_DTYPE)] * 3 + [_sds((HEADS * CHUNK, CHUNK)), _sds((1, AB_PAD)),
                                                                                _sds((HEADS * CHUNK, CHUNK))]
    sum_mats = _summation_matrices(_running_sum, CHUNK)
    out = pl.pallas_call(
        body, grid=(b, seq // rows), name="gdn_local",
        in_specs=[pl.BlockSpec((None, rows, 4 * WIDTH), lambda s, g: (s, g, 0)),
                  pl.BlockSpec((None, HALO, QKV), lambda s, g: (s, _halo_block(g), 0)), const(p0.shape),
                  pl.BlockSpec((None, rows, AB_PAD), lambda s, g: (s, g, 0)), const(ab0.shape), const(cw.shape),
                  const(alog.shape), const(dtb.shape), const(sum_mats[0].shape), const(sum_mats[1].shape)],
        out_specs=[slab] * 4 + [mats, pl.BlockSpec((None, LOCAL_CHUNKS, 1, AB_PAD), lambda s, g: (s, g, 0, 0)), mats]
        + [const(s.shape) for s in lead_out],
        out_shape=[_sds((b, seq, WIDTH))] + [_sds((b, seq, WIDTH), MXU_DTYPE)] * 3
        + [_sds((b, nreal, HEADS * CHUNK, CHUNK)), _sds((b, nreal, 1, AB_PAD)), _sds((b, nreal, HEADS * CHUNK, CHUNK))] + lead_out,
        compiler_params=_cparams("arbitrary", "arbitrary"),
    )(p, p, p0, ab, ab0, cw, alog, dtb, *sum_mats)
    return out[0:6], out[6], out[7:13], out[13]


def _gd_scan_args(b, k, u_ref, w_ref, qe_ref, ke_ref, qk_ref, ea_ref, z_ref):
    qk = jnp.stack([qk_ref[i, k, h * CHUNK:(h + 1) * CHUNK, :] for i, h in _pairs(b)], axis=0)
    ea = jnp.stack([ea_ref[i, k, :, h:h + 1] for i, h in _pairs(b)], axis=0)
    return (_load_slabs(u_ref, b, k), _load_slabs(w_ref, b, k), _load_slabs(qe_ref, b, k), _load_slabs(ke_ref, b, k), qk, ea,
            _load_slabs(z_ref, b, k))


def _gd_lead_args(b, u0_ref, w0_ref, qe0_ref, ke0_ref, qk0_ref, ea0_ref, p0_ref):
    qk = jnp.stack([qk0_ref[h * CHUNK:(h + 1) * CHUNK, :] for _, h in _pairs(b)], axis=0)
    ea = jnp.stack([ea0_ref[:, h:h + 1] for _, h in _pairs(b)], axis=0)
    return (_lead_slabs(u0_ref[...], b), _lead_slabs(w0_ref[...], b), _lead_slabs(qe0_ref[...], b), _lead_slabs(ke0_ref[...], b),
            qk, ea, _lead_slabs(p0_ref[:, QKV:QKV + WIDTH], b))


def gd_scan_fwd(p, p0, local, lead, nw):
    b, seq, _ = p.shape
    slab, per_chunk, const = _scan_specs(b, seq // (SCAN_CHUNKS_FWD * CHUNK), False, SCAN_CHUNKS_FWD)

    def body(u_ref, w_ref, qe_ref, ke_ref, qk_ref, ea_ref, z_ref, u0_ref, w0_ref, qe0_ref, ke0_ref, qk0_ref, ea0_ref, p0_ref,
             nw_ref, y_ref, ss_ref, st):
        @pl.when(pl.program_id(0) == 0)
        def _():
            lead_args = _gd_lead_args(b, u0_ref, w0_ref, qe0_ref, ke0_ref, qk0_ref, ea0_ref, p0_ref)
            st[...] = gd_scan(*lead_args, nw_ref[...], jnp.zeros(st.shape, F32))[1]

        s = st[...]
        for k in range(SCAN_CHUNKS_FWD):
            _save_states(ss_ref, s, b, k)
            y, s = gd_scan(*_gd_scan_args(b, k, u_ref, w_ref, qe_ref, ke_ref, qk_ref, ea_ref, z_ref), nw_ref[...], s)
            _store_slabs(y_ref, y, b, k)
        st[...] = s

    return dict(
        body=body, args=(*local, p, *lead, p0, nw),
        in_specs=[slab(0)] * 4 + [per_chunk(HEADS * CHUNK, CHUNK), per_chunk(1, AB_PAD), slab(3)] + [const(a) for a in lead]
        + [const(p0), const(nw)],
        out_specs=[slab(0), per_chunk(WIDTH, DH)],
        out_shape=[_sds((b, seq, WIDTH), MXU_DTYPE), _sds((b, seq // CHUNK, WIDTH, DH))],
        scratch_shapes=[pltpu.VMEM((b * HEADS, DH, DH), F32)])


def gd_scan_bwd(p, p0, local, lead, nw, ssave, dy):
    b, seq, _ = p.shape
    ng = seq // (SCAN_CHUNKS * CHUNK)
    slab, per_chunk, const = _scan_specs(b, ng, True, SCAN_CHUNKS)

    def body(u_ref, w_ref, qe_ref, ke_ref, qk_ref, ea_ref, z_ref, u0_ref, w0_ref, qe0_ref, ke0_ref, qk0_ref, ea0_ref, p0_ref,
             nw_ref, ss_ref, dy_ref, du_ref, dw_ref, dqe_ref, dke_ref, dqk_ref, dea_ref, dz_ref, du0_ref, dw0_ref, dqe0_ref,
             dke0_ref, dqk0_ref, dea0_ref, dz0_ref, dnw_ref, dst):
        i = pl.program_id(0)
        lane = lax.broadcasted_iota(jnp.int32, (1, AB_PAD), 1)

        def gate_rows(dea, j):
            return sum(jnp.where(lane == h, dea[j * HEADS + h], 0.0) for h in range(HEADS))

        def matrix_rows(dqk, j):
            return jnp.concatenate([dqk[j * HEADS + h] for h in range(HEADS)], axis=0)

        @pl.when(i == 0)
        def _():
            dst[...] = jnp.zeros_like(dst)
            dnw_ref[...] = jnp.zeros_like(dnw_ref)

        ds = dst[...]
        for k in reversed(range(SCAN_CHUNKS)):
            args = _gd_scan_args(b, k, u_ref, w_ref, qe_ref, ke_ref, qk_ref, ea_ref, z_ref)
            _, vjp = jax.vjp(gd_scan, *args, nw_ref[...], _load_states(ss_ref, b, k))
            du, dw, dqe, dke, dqk, dea, dz, dnw, ds = vjp((_load_slabs(dy_ref, b, k), ds))
            dnw_ref[...] += dnw
            for ref, val in ((du_ref, du), (dw_ref, dw), (dqe_ref, dqe), (dke_ref, dke), (dz_ref, dz)):
                _store_slabs(ref, val, b, k)
            for j in range(b):
                dqk_ref[j, k] = matrix_rows(dqk, j)
                dea_ref[j, k] = gate_rows(dea, j)
        dst[...] = ds

        @pl.when(i == ng - 1)
        def _():
            args = _gd_lead_args(b, u0_ref, w0_ref, qe0_ref, ke0_ref, qk0_ref, ea0_ref, p0_ref)
            _, vjp = jax.vjp(gd_scan, *args, nw_ref[...], jnp.zeros(dst.shape, F32))
            du, dw, dqe, dke, dqk, dea, dz, dnw, _ = vjp((jnp.zeros((b * HEADS, CHUNK, DH), F32), ds))
            dnw_ref[...] += dnw
            for ref, val in ((du0_ref, du), (dw0_ref, dw), (dqe0_ref, dqe), (dke0_ref, dke), (dz0_ref, dz)):
                ref[...] = _sum_rows(val, b)
            dqk0_ref[...] = sum((matrix_rows(dqk, j) for j in range(1, b)), matrix_rows(dqk, 0))
            dea0_ref[...] = sum((gate_rows(dea, j) for j in range(1, b)), gate_rows(dea, 0))

    uu, ww, qe, ke, qk, ea = local
    return dict(
        body=body, args=(*local, p, *lead, p0, nw, ssave, dy),
        in_specs=[slab(0)] * 4 + [per_chunk(HEADS * CHUNK, CHUNK), per_chunk(1, AB_PAD), slab(3)] + [const(a) for a in lead]
        + [const(p0), const(nw), per_chunk(WIDTH, DH), slab(0)],
        out_specs=[slab(0)] * 4 + [per_chunk(HEADS * CHUNK, CHUNK), per_chunk(1, AB_PAD), slab(0)] + [const(a) for a in lead]
        + [const(lead[0]), const(nw)],
        out_shape=[_sds((b, seq, WIDTH))] * 4 + [_sds(qk.shape), _sds(ea.shape), _sds((b, seq, WIDTH))]
        + [_sds(a.shape) for a in lead] + [_sds(lead[0].shape), _sds(nw.shape)],
        scratch_shapes=[pltpu.VMEM((b * HEADS, DH, DH), F32)])


def _gd_local_vjp(sum_mats, inv_rows, xx, ab, cw, alog, dtb):
    nb = ab.shape[0] // CHUNK
    inv = jnp.stack([inv_rows[g * CHUNK:(g + 1) * CHUNK] for g in range(nb * HEADS)], axis=0)
    _, vjp, _ = jax.vjp(lambda *a: gd_local(*a, sum_mats, inverse=_saved_inverse(inv)), xx, ab, cw, alog, dtb, has_aux=True)
    return vjp


def gd_local_bwd(p, p0, ab, ab0, cw, alog, dtb, inv, inv0, cot, dz, cot0, dz0):
    b, seq, _ = p.shape
    rows = LOCAL_CHUNKS * CHUNK
    ng = seq // rows
    du, dw, dqe, dke, dqk, dea = cot

    def body(p_ref, halo_ref, p0_ref, ab_ref, ab0_ref, cw_ref, al_ref, dt_ref, s_ref, st_ref, inv_ref, inv0_ref, du_ref, dw_ref,
             dqe_ref, dke_ref, dqk_ref, dea_ref, dz_ref, du0_ref, dw0_ref, dqe0_ref, dke0_ref, dqk0_ref, dea0_ref, dz0_ref,
             dp_ref, dab_ref, dp0_ref, dab0_ref, dcw_ref, dal_ref, ddt_ref, dhalo, dtail):
        s, i = pl.program_id(0), pl.program_id(1)
        g = ng - 1 - i
        sum_mats = (s_ref[...], st_ref[...])

        @pl.when(i == 0)
        def _():
            dhalo[...] = jnp.zeros_like(dhalo)

        @pl.when((s == 0) & (i == 0))
        def _():
            dtail[...] = jnp.zeros_like(dtail)
            dcw_ref[...] = jnp.zeros_like(dcw_ref)
            dal_ref[...] = jnp.zeros_like(dal_ref)
            ddt_ref[...] = jnp.zeros_like(ddt_ref)

        def finish(dxx, dab, dcw, dal, ddt, before, n, dz_val, dp_out, dab_out):
            dqkv = dxx[HALO:HALO + n] + jnp.concatenate([jnp.zeros((n - HALO, QKV), F32), before], axis=0)
            dp_out[...] = jnp.concatenate([dqkv, dz_val], axis=1).astype(MXU_DTYPE)
            dab_out[...] = dab.astype(MXU_DTYPE)
            dcw_ref[...] += dcw
            dal_ref[...] += dal
            ddt_ref[...] += ddt

        inv_rows = jnp.concatenate([inv_ref[c] for c in range(LOCAL_CHUNKS)], axis=0)
        vjp = _gd_local_vjp(sum_mats, inv_rows, _gd_window(g, p_ref, halo_ref, p0_ref), ab_ref[...], cw_ref[...], al_ref[...],
                            dt_ref[...])
        dqk_all = jnp.concatenate([dqk_ref[c] for c in range(LOCAL_CHUNKS)], axis=0)
        deas = tuple(dea_ref[c] for c in range(LOCAL_CHUNKS))
        grads = vjp((du_ref[...], dw_ref[...], dqe_ref[...], dke_ref[...], dqk_all, deas))
        finish(*grads, dhalo[...], rows, dz_ref[...], dp_ref, dab_ref)
        dhalo[...] = grads[0][0:HALO]

        @pl.when(g == 0)
        def _():
            dtail[...] += grads[0][0:HALO]

        @pl.when((s == b - 1) & (g == 0))
        def _():
            vjp0 = _gd_local_vjp(sum_mats, inv0_ref[...], _lead_window(p0_ref), ab0_ref[...], cw_ref[...], al_ref[...],
                                 dt_ref[...])
            grads0 = vjp0((du0_ref[...], dw0_ref[...], dqe0_ref[...], dke0_ref[...], dqk0_ref[...], (dea0_ref[...],)))
            finish(*grads0, dtail[...], CHUNK, dz0_ref[...], dp0_ref, dab0_ref)

    rg = lambda i: ng - 1 - i
    const = lambda a: pl.BlockSpec(a.shape, lambda s, i: (0, 0))
    slab = pl.BlockSpec((None, rows, WIDTH), lambda s, i: (s, rg(i), 0))
    wide = pl.BlockSpec((None, rows, 4 * WIDTH), lambda s, i: (s, rg(i), 0))
    gates = pl.BlockSpec((None, rows, AB_PAD), lambda s, i: (s, rg(i), 0))
    mats = pl.BlockSpec((None, LOCAL_CHUNKS, HEADS * CHUNK, CHUNK), lambda s, i: (s, rg(i), 0, 0))
    sum_mats = _summation_matrices(_running_sum, CHUNK)
    return pl.pallas_call(
        body, grid=(b, ng), name="gdn_local_bwd",
        in_specs=[wide, pl.BlockSpec((None, HALO, QKV), lambda s, i: (s, _halo_block(rg(i)), 0)), const(p0), gates, const(ab0),
                  const(cw), const(alog), const(dtb), const(sum_mats[0]), const(sum_mats[1]), mats, const(inv0), slab, slab,
                  slab, slab, mats,
                  pl.BlockSpec((None, LOCAL_CHUNKS, 1, AB_PAD), lambda s, i: (s, rg(i), 0, 0)), slab]
        + [const(a) for a in cot0] + [const(dz0)],
        out_specs=[wide, gates, const(p0), const(ab0), const(cw), const(alog), const(dtb)],
        out_shape=[_sds(p.shape, MXU_DTYPE), _sds(ab.shape, MXU_DTYPE), _sds(p0.shape, MXU_DTYPE), _sds(ab0.shape, MXU_DTYPE),
                   _sds(cw.shape), _sds(alog.shape), _sds(dtb.shape)],
        scratch_shapes=[pltpu.VMEM((HALO, QKV), F32), pltpu.VMEM((HALO, QKV), F32)],
        compiler_params=_cparams("arbitrary", "arbitrary"),
    )(p, p, p0, ab, ab0, cw, alog, dtb, *sum_mats, inv, inv0, du, dw, dqe, dke, dqk, dea, dz, *cot0, dz0)


def _position():
    return lax.axis_index("x"), lax.axis_index("y"), lax.axis_index("c")


EXCHANGE_COPIES = 10


def _exchange_blocks(bufs, send_sems, recv_sems):
    x, y, c = _position()
    here, x_nbr, y_nbr, diag = (x, y), (1 - x, y), (x, 1 - y), (1 - x, 1 - y)
    sibling = (x, y, 1 - c)
    me = (x, y, c)
    n = range(len(bufs))

    def rows(a, chip, core, half=None):
        block = bufs[a].at[4 * chip[0] + 2 * chip[1] + core]
        if half is None:
            return block
        total = bufs[a].shape[1]
        tile = 8 * (4 // jnp.dtype(bufs[a].dtype).itemsize)
        split = total // 2 // tile * tile
        return block.at[pl.ds(0, split)] if half == 0 else block.at[pl.ds(split, total - split)]

    def copy(a, k, region, to):
        return pltpu.make_async_remote_copy(src_ref=region, dst_ref=region, send_sem=send_sems.at[a * EXCHANGE_COPIES + k],
                                            recv_sem=recv_sems.at[a * EXCHANGE_COPIES + k], device_id=to, device_id_type=MESH)

    sent = [copy(a, 0, rows(a, here, c), sibling) for a in n]
    sent += [cp for a in n for cp in (copy(a, 1, rows(a, here, c, 0), (*x_nbr, c)), copy(a, 4, rows(a, here, c, 1), (*y_nbr, c)))]
    sent += [cp for a in n for cp in (copy(a, 2, rows(a, here, c, 1), (*x_nbr, c)), copy(a, 3, rows(a, here, c, 0), (*y_nbr, c)))]
    for cp in sent:
        cp.start()

    def after(arrivals, a, k, region, to):
        for cp in arrivals:
            cp.wait_recv()
        sent.append(copy(a, k, region, to))
        sent[-1].start()

    for a in n:
        after([copy(a, 1, rows(a, x_nbr, c, 0), me)], a, 5, rows(a, x_nbr, c, 0), (*y_nbr, c))
        after([copy(a, 4, rows(a, y_nbr, c, 1), me)], a, 6, rows(a, y_nbr, c, 1), (*x_nbr, c))
    for a in n:
        after([copy(a, 2, rows(a, x_nbr, c, 1), me)], a, 7, rows(a, x_nbr, c), sibling)
        after([copy(a, 3, rows(a, y_nbr, c, 0), me)], a, 8, rows(a, y_nbr, c), sibling)
    for a in n:
        after([copy(a, 5, rows(a, diag, c, 0), me), copy(a, 6, rows(a, diag, c, 1), me)], a, 9, rows(a, diag, c), sibling)
    for a in n:
        copy(a, 0, rows(a, here, 1 - c), me).wait_recv()
        for k, chip in ((7, x_nbr), (8, y_nbr), (9, diag)):
            copy(a, k, rows(a, chip, 1 - c), me).wait_recv()
    for cp in sent:
        cp.wait_send()


def _exchange_sems(n_bufs):
    return [pltpu.SemaphoreType.DMA((n_bufs * EXCHANGE_COPIES,)), pltpu.SemaphoreType.DMA((n_bufs * EXCHANGE_COPIES,))]


def gather_weights(w_in_t, w_out, small, pad_rows):
    rows, _, cols = w_in_t.shape
    buf_rows = -(-rows // ROW_TILE_BF16) * ROW_TILE_BF16

    def body(wi_ref, wo_ref, sm_ref, wi_out, wo_out, sm_out, wi_buf, send_sems, recv_sems):
        x, y, c = _position()
        me = 4 * x + 2 * y + c
        wi_buf[me, pl.ds(0, rows), :] = wi_ref[:, 0, :].astype(MXU_DTYPE)
        wi_buf[me, pl.ds(rows, buf_rows - rows), :] = jnp.zeros((buf_rows - rows, cols), MXU_DTYPE)
        wo_out[me] = wo_ref[...].astype(MXU_DTYPE)
        sm_out[me] = sm_ref[...]
        _exchange_blocks([wi_buf, wo_out, sm_out], send_sems, recv_sems)
        for d in range(N_DEV):
            wi_out[pl.ds(d * rows, rows), :] = wi_buf[d, pl.ds(0, rows), :]
        wi_out[pl.ds(N_DEV * rows, pad_rows), :] = jnp.zeros((pad_rows, cols), MXU_DTYPE)

    return pl.pallas_call(
        body, name="gather_weights", in_specs=[VMEM_SPEC] * 3, out_specs=[VMEM_SPEC] * 3,
        out_shape=[jax.ShapeDtypeStruct((N_DEV * rows + pad_rows, cols), MXU_DTYPE),
                   jax.ShapeDtypeStruct((N_DEV,) + w_out.shape, MXU_DTYPE), jax.ShapeDtypeStruct((N_DEV,) + small.shape, F32)],
        scratch_shapes=[pltpu.VMEM((N_DEV, buf_rows, cols), MXU_DTYPE)] + _exchange_sems(3),
        compiler_params=pltpu.CompilerParams(vmem_limit_bytes=VMEM_LIMIT))(w_in_t, w_out, small)


HOPS = 6


def reduce_gradients(tensors, small, name):
    n_t = len(tensors)
    arrays = [a for parts, _ in tensors for a, _ in parts]
    first_array = [sum(len(parts) for parts, _ in tensors[:t]) for t in range(n_t)]

    def pieces(t, j):
        parts, block_rows = tensors[t]
        out, base = [], 0
        for pi, (_, valid) in enumerate(parts):
            lo, hi = max(j * block_rows, base), min((j + 1) * block_rows, base + valid)
            if lo < hi:
                out.append((first_array[t] + pi, lo - base, lo - j * block_rows, hi - lo))
            base += valid
        return out

    def body(*refs):
        n_a = len(arrays)
        in_refs, small_ref = refs[:n_a], refs[n_a]
        out_refs, small_sum = refs[n_a + 1:n_a + 1 + n_t], refs[n_a + 1 + n_t]
        bufs, small_buf = refs[n_a + 2 + n_t:n_a + 2 + 5 * n_t], refs[n_a + 2 + 5 * n_t]
        s1_sems, r1_sems, s2_sems, r2_sems, small_send, small_recv = refs[n_a + 3 + 5 * n_t:]
        x, y, c = _position()
        chip = 2 * x + y

        def put(t, dst, j, add=None):
            for ai, src_row, dst_row, size in pieces(t, j):
                v = in_refs[ai][pl.ds(src_row, size), :]
                if add is not None:
                    v = v + add[pl.ds(dst_row, size), :].astype(F32)
                dst[pl.ds(dst_row, size), :] = v.astype(dst.dtype)

        def swap(t, k):
            send1, recv1 = bufs[4 * t], bufs[4 * t + 1]
            return pltpu.make_async_remote_copy(src_ref=send1.at[k], dst_ref=recv1.at[k], send_sem=s1_sems.at[4 * t + k],
                                                recv_sem=r1_sems.at[4 * t + k], device_id=(x, y, 1 - c), device_id_type=MESH)

        to_x, to_y, to_diag = 2 * (1 - x) + y, 2 * x + (1 - y), 2 * (1 - x) + (1 - y)
        x_dev, y_dev = (1 - x, y, c), (x, 1 - y, c)

        def half(ref, h):
            total = ref.shape[0]
            split = total // 2 // ROW_TILE_BF16 * ROW_TILE_BF16
            return ref.at[pl.ds(0, split)] if h == 0 else ref.at[pl.ds(split, total - split)]

        def hop(t, copy_id, src, dst, to):
            return pltpu.make_async_remote_copy(src_ref=src, dst_ref=dst, send_sem=s2_sems.at[HOPS * t + copy_id],
                                                recv_sem=r2_sems.at[HOPS * t + copy_id], device_id=to, device_id_type=MESH)

        def hops(t):
            send2, landing = bufs[4 * t + 2], bufs[4 * t + 3]
            return [hop(t, 0, half(send2.at[to_diag], 0), half(landing.at[0], 0), x_dev),
                    hop(t, 1, half(send2.at[to_diag], 1), half(landing.at[0], 1), y_dev),
                    hop(t, 2, half(send2.at[to_x], 0), half(landing.at[1], 0), x_dev),
                    hop(t, 3, half(send2.at[to_y], 1), half(landing.at[2], 1), y_dev),
                    hop(t, 4, half(send2.at[to_x], 1), half(landing.at[1], 1), x_dev),
                    hop(t, 5, half(send2.at[to_y], 0), half(landing.at[2], 0), y_dev)]

        def add_relay(t, slot, h):
            dst, src = half(bufs[4 * t + 2].at[slot], h), half(bufs[4 * t + 3].at[0], h)
            dst[...] = (dst[...].astype(F32) + src[...].astype(F32)).astype(dst.dtype)

        for t in range(n_t):
            send2 = bufs[4 * t + 2]
            pad = send2.shape[1] - tensors[t][1]
            if pad:
                send2[:, pl.ds(tensors[t][1], pad), :] = jnp.zeros((4, pad, send2.shape[2]), send2.dtype)
            for j in range(N_DEV):
                @pl.when((j & 1) != c)
                def _():
                    put(t, bufs[4 * t].at[j >> 1], j)
            for k in range(4):
                swap(t, k).start()

        small_buf[4 * x + 2 * y + c] = small_ref[...]
        _exchange_blocks([small_buf], small_send, small_recv)
        total = small_buf[0]
        for d in range(1, N_DEV):
            total = total + small_buf[d]
        small_sum[...] = total

        for t in range(n_t):
            recv1 = bufs[4 * t + 1]
            for k in range(4):
                swap(t, k).wait_recv()
                for j in (2 * k, 2 * k + 1):
                    @pl.when(((j & 1) == c) & (k != chip))
                    def _():
                        put(t, bufs[4 * t + 2].at[k], j, add=recv1.at[k])

                    @pl.when(((j & 1) == c) & (k == chip))
                    def _():
                        put(t, out_refs[t], j, add=recv1.at[k])
            for cp in hops(t)[0:4]:
                cp.start()

        for t in range(n_t):
            cps = hops(t)
            cps[0].wait_recv()
            add_relay(t, to_y, 0)
            cps[5].start()
            cps[1].wait_recv()
            add_relay(t, to_x, 1)
            cps[4].start()

        for t in range(n_t):
            cps, rows = hops(t), tensors[t][1]
            for first, second, slot in ((cps[2], cps[4], 1), (cps[3], cps[5], 2)):
                first.wait_recv()
                second.wait_recv()
                out_refs[t][...] += bufs[4 * t + 3][slot, pl.ds(0, rows), :].astype(F32)

        for t in range(n_t):
            for cp in hops(t):
                cp.wait_send()
            for k in range(4):
                swap(t, k).wait_send()

    scratch, out_shape = [], []
    for parts, block_rows in tensors:
        cols = parts[0][0].shape[1]
        tiled_rows = -(-block_rows // ROW_TILE_BF16) * ROW_TILE_BF16
        scratch += [pltpu.VMEM((4, block_rows, cols), MXU_DTYPE)] * 2
        scratch += [pltpu.VMEM((4, tiled_rows, cols), MXU_DTYPE), pltpu.VMEM((3, tiled_rows, cols), MXU_DTYPE)]
        out_shape.append(jax.ShapeDtypeStruct((block_rows, cols), F32))
    out_shape.append(jax.ShapeDtypeStruct(small.shape, F32))
    scratch += [pltpu.VMEM((N_DEV,) + small.shape, F32)] + [pltpu.SemaphoreType.DMA((4 * n_t,))] * 2
    scratch += [pltpu.SemaphoreType.DMA((HOPS * n_t,))] * 2 + _exchange_sems(1)
    return pl.pallas_call(
        body, name=name, in_specs=[VMEM_SPEC] * (len(arrays) + 1), out_specs=[VMEM_SPEC] * (n_t + 1), out_shape=out_shape,
        scratch_shapes=scratch, compiler_params=pltpu.CompilerParams(vmem_limit_bytes=VMEM_LIMIT),
    )(*arrays, small)


def _adamw_step(w, g, m, v):
    mn = ADAM_B1 * m + (1.0 - ADAM_B1) * g
    vn = ADAM_B2 * v + (1.0 - ADAM_B2) * jnp.square(g)
    m_hat = mn / (1.0 - ADAM_B1 ** ADAM_STEP)
    v_hat = vn / (1.0 - ADAM_B2 ** ADAM_STEP)
    return -ADAM_LR * (m_hat / (jnp.sqrt(v_hat) + ADAM_EPS) + ADAM_WD * w), mn, vn


def adamw_small(ws, gs, ms, vs):
    k = len(ws)

    def body(*refs):
        ins, outs = refs[:4 * k], refs[4 * k:]
        for i in range(k):
            w_ref, g_ref, m_ref, v_ref = ins[i::k]
            outs[3 * i][...], outs[3 * i + 1][...], outs[3 * i + 2][...] = _adamw_step(w_ref[...], g_ref[...], m_ref[...],
                                                                                      v_ref[...])

    out = pl.pallas_call(body, name="adamw_small", in_specs=[VMEM_SPEC] * (4 * k), out_specs=[VMEM_SPEC] * (3 * k),
                         out_shape=[jax.ShapeDtypeStruct(w.shape, F32) for w in ws for _ in range(3)],
                         compiler_params=pltpu.CompilerParams(vmem_limit_bytes=VMEM_LIMIT))(*ws, *gs, *ms, *vs)
    return [out[3 * i:3 * i + 3] for i in range(k)]


def adamw_w_in(w, g_t, m, v):
    def body(w_ref, g_ref, m_ref, v_ref, go_ref, d_ref, nm_ref, nv_ref):
        g = g_ref[...]
        go_ref[:, 0, :] = g
        d_ref[:, 0, :], nm_ref[:, 0, :], nv_ref[:, 0, :] = _adamw_step(w_ref[:, 0, :], g, m_ref[:, 0, :], v_ref[:, 0, :])

    return pl.pallas_call(body, name="adamw_w_in", in_specs=[VMEM_SPEC] * 4, out_specs=[VMEM_SPEC] * 4,
                          out_shape=[jax.ShapeDtypeStruct(w.shape, F32)] * 4,
                          compiler_params=pltpu.CompilerParams(vmem_limit_bytes=VMEM_LIMIT))(w, g_t, m, v)


def _pad_rows(a, rows=8):
    return jnp.pad(a, ((0, rows - a.shape[0]), (0, 0)))


def _pad_lanes(a, lanes=128):
    return jnp.pad(a, ((0, 0), (0, lanes - a.shape[1])))


def kernel(x, meta_tokens, norm_w, w_in, conv_w, hg_lb_logits, hg_norm_w, gdn_A_log, gdn_dt_bias, gdn_norm_w, w_out, final_norm_w, loss_target, m_meta_tokens, m_norm_w, m_w_in, m_conv_w, m_hg_lb_logits, m_hg_norm_w, m_gdn_A_log, m_gdn_dt_bias, m_gdn_norm_w, m_w_out, m_final_norm_w, v_meta_tokens, v_norm_w, v_w_in, v_conv_w, v_hg_lb_logits, v_hg_norm_w, v_gdn_A_log, v_gdn_dt_bias, v_gdn_norm_w, v_w_out, v_final_norm_w):
    b, seq, _ = x.shape
    n = b * seq
    dev = 4 * lax.axis_index("x") + 2 * lax.axis_index("y") + lax.axis_index("c")
    col_shard = IN_COLS // N_DEV

    small_w = jnp.concatenate([_pad_lanes(meta_tokens, 256), _pad_rows(_pad_lanes(conv_w[0], 256))], axis=0)
    w_t, w_out_g, small_g = gather_weights(jnp.transpose(w_in, (2, 0, 1)), w_out[0], small_w, AB_PAD - 2 * HEADS)
    meta_g = small_g[:, 0:N_META, 0:D_MODEL // N_DEV]
    conv_g = small_g[:, N_META:N_META + CONV_TAPS, 0:QKV // N_DEV]
    w_out_full = w_out_g.reshape(2 * WIDTH, D_MODEL)
    cw = jnp.transpose(conv_g, (1, 0, 2)).reshape(CONV_TAPS, QKV)
    meta = jnp.transpose(meta_g, (1, 0, 2)).reshape(N_META, D_MODEL)
    alog = _pad_lanes(gdn_A_log)
    dtb = _pad_lanes(gdn_dt_bias)
    fw = final_norm_w.reshape(1, D_MODEL)

    h0 = jnp.concatenate([jnp.zeros((CHUNK - N_META, D_MODEL), F32), meta], axis=0)
    x2 = x.reshape(n, D_MODEL)
    phg, pgd, pab, phg0, pgd0, pab0, u0 = in_proj(x2, h0, norm_w, w_t)
    phg3, pgd3, pab3 = phg.reshape(b, seq, 4 * WIDTH), pgd.reshape(b, seq, 4 * WIDTH), pab.reshape(b, seq, AB_PAD)
    hg_loc, hg_lead = hg_local_fwd(phg3, phg0, hg_lb_logits)
    gd_loc, gd_inv, gd_lead, gd_inv0 = gd_local_fwd(pgd3, pgd0, pab3, pab0, cw, alog, dtb)
    (y_hg, s_hg), (y_gd, s_gd) = run_scans([hg_scan_fwd(phg3, phg0, hg_loc, hg_lead, hg_norm_w),
                                            gd_scan_fwd(pgd3, pgd0, gd_loc, gd_lead, gdn_norm_w)],
                                           seq // (SCAN_CHUNKS_FWD * CHUNK), "scans")

    dh2, dy_hg, dy_gd, g_w_out, loss_part, g_fw = out_proj_loss(
        x2, loss_target.reshape(n, D_MODEL), y_hg.reshape(n, WIDTH), y_gd.reshape(n, WIDTH), w_out_full, fw)

    hb, gb = run_scans([hg_scan_bwd(phg3, phg0, hg_loc, hg_lead, hg_norm_w, s_hg, dy_hg.reshape(b, seq, WIDTH)),
                        gd_scan_bwd(pgd3, pgd0, gd_loc, gd_lead, gdn_norm_w, s_gd, dy_gd.reshape(b, seq, WIDTH))],
                       seq // (SCAN_CHUNKS * CHUNK), "scans_bwd")
    dphg, dphg0, g_lb = hg_local_bwd(phg3, phg0, hg_lb_logits, hb[0:6], hb[6:12])
    g_hg_nw = hb[12]
    dpgd, dpab, dpgd0, dpab0, g_cw, g_alog, g_dtb = gd_local_bwd(pgd3, pgd0, pab3, pab0, cw, alog, dtb, gd_inv, gd_inv0,
                                                                 gb[0:6], gb[6], gb[7:13], gb[13])
    g_gd_nw = gb[14]
    dphg, dpgd, dpab = dphg.reshape(n, 4 * WIDTH), dpgd.reshape(n, 4 * WIDTH), dpab.reshape(n, AB_PAD)

    grad_x, dh0, g_nw, g_w_hg, g_w_gd, g_w_ab = in_proj_bwd(dphg, dpgd, dpab, w_t, x2, dh2, norm_w, h0, u0, dphg0, dpgd0, dpab0)

    small = jnp.concatenate([
        g_nw.reshape(8, 128), g_lb.reshape(8, 128), _pad_rows(g_hg_nw), _pad_rows(g_alog), _pad_rows(g_dtb), _pad_rows(g_gd_nw),
        g_fw.reshape(8, 128), g_cw.reshape(48, 128),
        dh0[CHUNK - N_META:CHUNK].reshape(128, 128), loss_part], axis=0)
    g_w_in_t, g_w_out, small = reduce_gradients(
        [([(g_w_hg, 4 * WIDTH), (g_w_gd, 4 * WIDTH), (g_w_ab, 2 * HEADS)], col_shard),
         ([(g_w_out, 2 * WIDTH)], (2 * WIDTH) // N_DEV)], small, "reduce_gradients")
    g_norm_w = small[0:8].reshape(1, D_MODEL)
    g_lb = small[8:16].reshape(2, WIDTH)
    g_hg_nw = small[16:17]
    g_alog = small[24:25, 0:HEADS]
    g_dtb = small[32:33, 0:HEADS]
    g_gd_nw = small[40:41]
    g_fw = small[48:56].reshape(1, D_MODEL)
    g_cw_full = small[56:104].reshape(CONV_TAPS, QKV)
    g_meta_full = small[104:232].reshape(N_META, D_MODEL)
    loss = small[232, 0]
    g_conv = lax.dynamic_slice_in_dim(g_cw_full, dev * (QKV // N_DEV), QKV // N_DEV, axis=1)
    g_meta = lax.dynamic_slice_in_dim(g_meta_full, dev * (D_MODEL // N_DEV), D_MODEL // N_DEV, axis=1)

    names = ["meta_tokens", "norm_w", "w_in", "conv_w", "hg_lb_logits", "hg_norm_w", "gdn_A_log", "gdn_dt_bias",
             "gdn_norm_w", "w_out", "final_norm_w"]
    weights = [meta_tokens, norm_w, w_in, conv_w, hg_lb_logits, hg_norm_w, gdn_A_log, gdn_dt_bias, gdn_norm_w, w_out,
               final_norm_w]
    moms = [m_meta_tokens, m_norm_w, m_w_in, m_conv_w, m_hg_lb_logits, m_hg_norm_w, m_gdn_A_log, m_gdn_dt_bias,
            m_gdn_norm_w, m_w_out, m_final_norm_w]
    vars_ = [v_meta_tokens, v_norm_w, v_w_in, v_conv_w, v_hg_lb_logits, v_hg_norm_w, v_gdn_A_log, v_gdn_dt_bias,
             v_gdn_norm_w, v_w_out, v_final_norm_w]
    grads2d = [g_meta, g_norm_w, g_w_in_t, g_conv, g_lb, g_hg_nw, g_alog, g_dtb, g_gd_nw, g_w_out, g_fw]
    i_w_in = names.index("w_in")
    others = [i for i in range(len(names)) if i != i_w_in]
    as_grad = lambda i, a: a.reshape(grads2d[i].shape)
    stepped = adamw_small([as_grad(i, weights[i]) for i in others], [grads2d[i] for i in others],
                          [as_grad(i, moms[i]) for i in others], [as_grad(i, vars_[i]) for i in others])
    results = {i: [a.reshape(weights[i].shape) for a in (grads2d[i], *stepped[j])] for j, i in enumerate(others)}
    to3, back = (lambda a: jnp.transpose(a, (2, 0, 1))), (lambda a: jnp.transpose(a, (1, 2, 0)))
    results[i_w_in] = [back(a) for a in adamw_w_in(to3(w_in), g_w_in_t, to3(m_w_in), to3(v_w_in))]
    grads, deltas, new_ms, new_vs = zip(*(results[i] for i in range(len(names))))
    return (loss, grad_x.reshape(x.shape), *grads, *deltas, *new_ms, *new_vs)
```

```python
import jax
import jax.numpy as jnp
import numpy as np
from jax import lax
from jax.experimental import pallas as pl
from jax.experimental.pallas import tpu as pltpu

F32 = jnp.float32
BF16 = jnp.bfloat16
MXU_DTYPE = BF16

D_MODEL = 1024
N_META = 16
CHUNK = 64
SUB = 16
ROW_TILE_BF16 = 16
HEADS = 4
DH = 128
WIDTH = HEADS * DH
QKV = 3 * WIDTH
CONV_TAPS = 4
HALO = 8
EPS = 1e-6
IN_COLS = 4 * WIDTH + 4 * WIDTH + 2 * HEADS
AB_PAD = 128
N_DEV = 8
LOCAL_CHUNKS = 4
SCAN_CHUNKS_FWD = 4
SCAN_CHUNKS = 2
VMEM_LIMIT = 56 * 1024 * 1024
VMEM_LIMIT_LARGE = 60 * 1024 * 1024

ADAM_LR = 0.001
ADAM_B1 = 0.9
ADAM_B2 = 0.999
ADAM_EPS = 1e-08
ADAM_WD = 0.01
ADAM_STEP = 10

VMEM_SPEC = pl.BlockSpec(memory_space=pltpu.VMEM)
MESH = pl.DeviceIdType.MESH


def _mm_tn(a, b):
    return lax.dot_general(a.astype(MXU_DTYPE), b.astype(MXU_DTYPE), (((0,), (0,)), ((), ())), preferred_element_type=F32)


def _bmm(a, b):
    return lax.dot_general(a.astype(MXU_DTYPE), b.astype(MXU_DTYPE), (((2,), (1,)), ((0,), (0,))), preferred_element_type=F32)


def _bmm_nt(a, b):
    return lax.dot_general(a.astype(MXU_DTYPE), b.astype(MXU_DTYPE), (((2,), (2,)), ((0,), (0,))), preferred_element_type=F32)


def _bmm_tn(a, b):
    return lax.dot_general(a.astype(MXU_DTYPE), b.astype(MXU_DTYPE), (((1,), (1,)), ((0,), (0,))), preferred_element_type=F32)


def _iota2(n, m):
    return lax.broadcasted_iota(jnp.int32, (n, m), 0), lax.broadcasted_iota(jnp.int32, (n, m), 1)


def _silu(x):
    return x * jax.nn.sigmoid(x)


def _gated_norm(o, z, nw):
    return o * lax.rsqrt(jnp.mean(o * o, axis=-1, keepdims=True) + EPS) * nw * _silu(z)


def _heads(a, nb):
    return jnp.stack([a[c * CHUNK:(c + 1) * CHUNK, h * DH:(h + 1) * DH] for c in range(nb) for h in range(HEADS)], axis=0)


def _unheads(a3, nb):
    return jnp.concatenate(
        [jnp.concatenate([a3[c * HEADS + h] for h in range(HEADS)], axis=1) for c in range(nb)], axis=0)


def _split3(x):
    hi = x.astype(BF16)
    r1 = x - hi.astype(F32)
    mid = r1.astype(BF16)
    return hi, mid, (r1 - mid.astype(F32)).astype(BF16)


def _summation_matrices(pattern, n_out):
    s = pattern(np.arange(n_out)[:, None], np.arange(CHUNK)[None, :]).astype(np.float32)
    return jnp.asarray(np.tile(s, (1, 3)), BF16), jnp.asarray(np.tile(s.T, (1, 2)), BF16)


def _select_rows(mats, chunks):
    width = chunks[0].shape[1]
    out = _summation(*mats, jnp.concatenate(chunks, axis=1))
    return [out[:, c * width:(c + 1) * width] for c in range(len(chunks))]


def _summation_impl(s, v):
    return jnp.dot(s, jnp.concatenate(_split3(v), axis=0), preferred_element_type=F32)


@jax.custom_vjp
def _summation(s, s_t, v):
    return _summation_impl(s, v)


def _summation_fwd(s, s_t, v):
    return _summation_impl(s, v), s_t


def _summation_bwd(s_t, d):
    hi = d.astype(BF16)
    return None, None, jnp.dot(s_t, jnp.concatenate([hi, (d - hi.astype(F32)).astype(BF16)], axis=0),
                               preferred_element_type=F32)


_summation.defvjp(_summation_fwd, _summation_bwd)


def _chunks(x, nb):
    return [x[c * CHUNK:(c + 1) * CHUNK] for c in range(nb)]


def _running_sum(i, j):
    return j <= i


HG_LEVELS = 6


def _hg_sums(i, j):
    lvl, t = i >> HG_LEVELS, i & (CHUNK - 1)
    last = t
    for l in range(1, HG_LEVELS + 1):
        width = HG_LEVELS + 1 - l
        last = np.where(lvl == l, ((t >> width) << width) + (CHUNK >> l) - 1, last)
    return j <= last


def hg_local(p, logits, sum_mats):
    nb = p.shape[0] // CHUNK
    l0, l1 = logits[0:1], logits[1:2]
    mx = jnp.maximum(l0, l1)
    e0, e1 = jnp.exp(l0 - mx), jnp.exp(l1 - mx)
    lb = e0 / (e0 + e1)
    q = _silu(p[:, 0:WIDTH])
    f = lb + (1.0 - lb) * jax.nn.sigmoid(p[:, WIDTH:2 * WIDTH])
    k = 1.0 - f
    logf = jnp.log(f)
    sums = _select_rows(sum_mats, _chunks(logf, nb))
    level = lambda l: _heads(jnp.concatenate([s[l * CHUNK:(l + 1) * CHUNK] for s in sums], axis=0), nb)
    q3, k3, v3, g3 = _heads(q, nb), _heads(k, nb), _heads(p[:, 2 * WIDTH:3 * WIDTH], nb), level(0)
    r, c = _iota2(CHUNK, CHUNK)
    row = lax.broadcasted_iota(jnp.int32, (CHUNK, DH), 0)
    a = jnp.where(r == c, _bmm_nt(q3, k3), 0.0)
    for l in range(1, HG_LEVELS + 1):
        sh = HG_LEVELS - l
        qk = jnp.where(((row >> sh) & 1) == 1, q3, k3) * jnp.exp(-jnp.abs(g3 - level(l)))
        pair = ((r >> (sh + 1)) == (c >> (sh + 1))) & (((r >> sh) & 1) == 1) & (((c >> sh) & 1) == 0)
        a = a + jnp.where(pair, _bmm_nt(qk, qk), 0.0)
    o = _bmm(a, v3)
    glast = g3[:, CHUNK - 1:CHUNK, :]
    egs = tuple(jnp.concatenate([jnp.exp(glast[c * HEADS + h]) for h in range(HEADS)], axis=1) for c in range(nb))
    return _unheads(q3 * jnp.exp(g3), nb), _unheads(k3 * jnp.exp(glast - g3), nb), _unheads(o, nb), egs


def hg_scan(q_in, k_out, v, eg, o_intra, z, nw, st):
    o = o_intra + _bmm_nt(q_in, st)
    return _gated_norm(o, z, nw), st * eg + _bmm_tn(v, k_out)


def _tri_y_impl(a):
    r, c = _iota2(CHUNK, CHUNK)
    same16 = (r // SUB) == (c // SUB)
    same32 = (r // (2 * SUB)) == (c // (2 * SUB))
    a0 = jnp.where(same16, a, 0.0)
    y = -a0
    pw = _bmm(a0, a0)
    for _ in range(2):
        y = y + pw + _bmm(y, pw)
        pw = _bmm(pw, pw)
    y = y + pw + _bmm(y, pw)
    for ak in (jnp.where(same32 & jnp.logical_not(same16), a, 0.0), jnp.where(same32, 0.0, a)):
        m = ak + _bmm(y, ak)
        y = y - (m + _bmm(m, y))
    return y


@jax.custom_vjp
def _tri_y(a):
    return _tri_y_impl(a)


def _tri_y_fwd(a):
    y = _tri_y_impl(a)
    return y, y


def _tri_y_bwd(y, dy):
    n = dy + _bmm_tn(y, dy)
    return (-(n + _bmm_nt(n, y)),)


_tri_y.defvjp(_tri_y_fwd, _tri_y_bwd)


def _saved_inverse(y):
    @jax.custom_vjp
    def inverse(a):
        return y

    inverse.defvjp(lambda a: (y, None), lambda _, dy: _tri_y_bwd(y, dy))
    return inverse


def _head_rows(a3, nb):
    return jnp.concatenate([a3[g] for g in range(nb * HEADS)], axis=0)


def _rows_down(x, s):
    rows = x.shape[0]

    @jax.custom_vjp
    def rotate(v):
        return pltpu.roll(v, s, 0)

    rotate.defvjp(lambda v: (pltpu.roll(v, s, 0), None), lambda _, d: (pltpu.roll(d, rows - s, 0),))
    return rotate(x)


def gd_local(xx, ab, cw, alog, dtb, sum_mats, inverse=_tri_y):
    n = ab.shape[0]
    nb = n // CHUNK
    conv = cw[CONV_TAPS - 1:CONV_TAPS] * xx[HALO:HALO + n]
    for j in range(CONV_TAPS - 1):
        conv = conv + cw[j:j + 1] * _rows_down(xx, CONV_TAPS - 1 - j)[HALO:HALO + n]
    act = _silu(conv)
    x = ab + dtb
    g_all = -jnp.exp(alog) * (jnp.maximum(x, 0.0) + jnp.log1p(jnp.exp(-jnp.abs(x))))
    beta_all = jax.nn.sigmoid(ab)
    gam_all = jnp.concatenate(_select_rows(sum_mats, _chunks(g_all, nb)), axis=0)
    q3, k3, v3 = _heads(act[:, 0:WIDTH], nb), _heads(act[:, WIDTH:2 * WIDTH], nb), _heads(act[:, 2 * WIDTH:QKV], nb)
    q3 = q3 * lax.rsqrt(jnp.sum(q3 * q3, axis=-1, keepdims=True) + EPS) * (DH ** -0.5)
    k3 = k3 * lax.rsqrt(jnp.sum(k3 * k3, axis=-1, keepdims=True) + EPS)
    pairs = [(c, h) for c in range(nb) for h in range(HEADS)]
    beta = jnp.stack([beta_all[c * CHUNK:(c + 1) * CHUNK, HEADS + h:HEADS + h + 1] for c, h in pairs], axis=0)
    gam = jnp.stack([gam_all[c * CHUNK:(c + 1) * CHUNK, h:h + 1] for c, h in pairs], axis=0)
    gam_t = [gam_all[c * CHUNK:(c + 1) * CHUNK].T for c in range(nb)]
    gam_row = jnp.stack([gam_t[c][h:h + 1, :] for c, h in pairs], axis=0)
    glast = gam[:, CHUNK - 1:CHUNK, :]
    r, c = _iota2(CHUNK, CHUNK)
    dec = jnp.exp(jnp.where(c < r, gam - gam_row, -jnp.inf))
    y = inverse(beta * _bmm_nt(k3, k3) * dec)
    eg = jnp.exp(gam)
    rhs = jnp.concatenate([beta * v3, (beta * eg) * k3], axis=2)
    sol = rhs + _bmm(y, rhs)
    qk = _bmm_nt(q3, k3) * jnp.where(r == c, 1.0, dec)
    eas = tuple(jnp.exp(gam_all[(c + 1) * CHUNK - 1:(c + 1) * CHUNK]) for c in range(nb))
    return (_unheads(sol[:, :, 0:DH], nb), _unheads(sol[:, :, DH:2 * DH], nb), _unheads(q3 * eg, nb),
            _unheads(k3 * jnp.exp(glast - gam), nb), _head_rows(qk, nb), eas), _head_rows(y, nb)


def gd_scan(uu, ww, qe, ke, qk, ea, z, nw, s):
    u = uu - _bmm(ww, s)
    o = _bmm(qe, s) + _bmm(qk, u)
    return _gated_norm(o, z, nw), ea * s + _bmm_tn(ke, u)


def _cparams(*sem):
    return pltpu.CompilerParams(dimension_semantics=sem, vmem_limit_bytes=VMEM_LIMIT)


def _row_tile(n):
    for t in (512, 256, 128, 64):
        if n % t == 0:
            return t
    raise ValueError(f"unsupported token count {n}")


def _w_in_specs():
    once = pl.Buffered(1)
    return [pl.BlockSpec((4 * WIDTH, D_MODEL), lambda *i: (0, 0), pipeline_mode=once),
            pl.BlockSpec((4 * WIDTH, D_MODEL), lambda *i: (1, 0), pipeline_mode=once),
            pl.BlockSpec((AB_PAD, D_MODEL), lambda *i: (8 * WIDTH // AB_PAD, 0), pipeline_mode=once)]


def in_proj(h, h0, norm_w, w_t):
    n = h.shape[0]
    tm = _row_tile(n)
    nt = (((1,), (1,)), ((), ()))

    def body(h_ref, h0_ref, nw_ref, whg_ref, wgd_ref, wab_ref, phg_ref, pgd_ref, pab_ref, phg0_ref, pgd0_ref, pab0_ref, u0_ref):
        def project(x, hg_ref, gd_ref, ab_ref):
            u = (x * lax.rsqrt(jnp.mean(x * x, axis=-1, keepdims=True) + EPS) * nw_ref[...]).astype(MXU_DTYPE)
            hg_ref[...] = lax.dot_general(u, whg_ref[...], nt, preferred_element_type=F32)
            gd_ref[...] = lax.dot_general(u, wgd_ref[...], nt, preferred_element_type=F32)
            ab_ref[...] = lax.dot_general(u, wab_ref[...], nt, preferred_element_type=F32)
            return u

        @pl.when(pl.program_id(0) == 0)
        def _():
            u0_ref[...] = project(h0_ref[...], phg0_ref, pgd0_ref, pab0_ref)

        project(h_ref[...], phg_ref, pgd_ref, pab_ref)

    n0 = h0.shape[0]
    row = lambda w: pl.BlockSpec((tm, w), lambda i: (i, 0))
    lead = lambda w: pl.BlockSpec((n0, w), lambda i: (0, 0))
    widths = [4 * WIDTH, 4 * WIDTH, AB_PAD]
    return pl.pallas_call(
        body, grid=(n // tm,), name="in_proj",
        in_specs=[row(D_MODEL), lead(D_MODEL), pl.BlockSpec(norm_w.shape, lambda i: (0, 0))] + _w_in_specs(),
        out_specs=[row(w) for w in widths] + [lead(w) for w in widths] + [lead(D_MODEL)],
        out_shape=[jax.ShapeDtypeStruct((n, w), F32) for w in widths] + [jax.ShapeDtypeStruct((n0, w), F32) for w in widths]
        + [jax.ShapeDtypeStruct((n0, D_MODEL), MXU_DTYPE)],
        compiler_params=_cparams("arbitrary"),
    )(h, h0, norm_w, w_t, w_t, w_t)


def out_proj_loss(x, tgt, y_hg, y_gd, w_out, fw):
    n = x.shape[0]
    tm = _row_tile(n)
    inv_d = 1.0 / D_MODEL

    def body(x_ref, t_ref, yh_ref, yg_ref, w_ref, fw_ref, dh_ref, dyh_ref, dyg_ref, dw_ref, loss_ref, dfw_ref):
        @pl.when(pl.program_id(0) == 0)
        def _():
            dw_ref[...] = jnp.zeros_like(dw_ref)
            loss_ref[...] = jnp.zeros_like(loss_ref)
            dfw_ref[...] = jnp.zeros_like(dfw_ref)

        yh, yg = yh_ref[...], yg_ref[...]
        wa, wb = w_ref[0:WIDTH, :], w_ref[WIDTH:2 * WIDTH, :]
        h2 = x_ref[...] + jnp.dot(yh, wa, preferred_element_type=F32) + jnp.dot(yg, wb, preferred_element_type=F32)
        r2 = lax.rsqrt(jnp.mean(h2 * h2, axis=-1, keepdims=True) + EPS)
        nrm = h2 * r2
        fwv = fw_ref[...]
        err = nrm * fwv - t_ref[...]
        loss_ref[...] += jnp.full(loss_ref.shape, 0.5 * inv_d * jnp.sum(err * err), F32)
        dout = err * inv_d
        dfw_ref[...] += jnp.sum(dout * nrm, axis=0, keepdims=True)
        dn = dout * fwv
        dh2 = r2 * (dn - nrm * jnp.mean(dn * nrm, axis=-1, keepdims=True))
        dh_ref[...] = dh2
        dhb = dh2.astype(MXU_DTYPE)
        dyh_ref[...] = lax.dot_general(dhb, wa, (((1,), (1,)), ((), ())), preferred_element_type=F32)
        dyg_ref[...] = lax.dot_general(dhb, wb, (((1,), (1,)), ((), ())), preferred_element_type=F32)
        dw_ref[0:WIDTH, :] += lax.dot_general(yh, dhb, (((0,), (0,)), ((), ())), preferred_element_type=F32)
        dw_ref[WIDTH:2 * WIDTH, :] += lax.dot_general(yg, dhb, (((0,), (0,)), ((), ())), preferred_element_type=F32)

    row = lambda w: pl.BlockSpec((tm, w), lambda i: (i, 0))
    full = lambda s: pl.BlockSpec(s, lambda i: (0, 0))
    return pl.pallas_call(
        body, grid=(n // tm,), name="out_proj_loss",
        in_specs=[row(D_MODEL), row(D_MODEL), row(WIDTH), row(WIDTH), full(w_out.shape), full(fw.shape)],
        out_specs=[row(D_MODEL), row(WIDTH), row(WIDTH), full((2 * WIDTH, D_MODEL)), full((8, 128)), full((1, D_MODEL))],
        out_shape=[jax.ShapeDtypeStruct((n, D_MODEL), F32), jax.ShapeDtypeStruct((n, WIDTH), F32),
                   jax.ShapeDtypeStruct((n, WIDTH), F32), jax.ShapeDtypeStruct((2 * WIDTH, D_MODEL), F32),
                   jax.ShapeDtypeStruct((8, 128), F32), jax.ShapeDtypeStruct((1, D_MODEL), F32)],
        compiler_params=_cparams("arbitrary"),
    )(x, tgt, y_hg, y_gd, w_out, fw)


def in_proj_bwd(dphg, dpgd, dpab, w_t, h, dh2, norm_w, h0, u0, dphg0, dpgd0, dpab0):
    n = h.shape[0]
    tm = _row_tile(n)
    steps = n // tm

    def body(dphg_ref, dpgd_ref, dpab_ref, whg_ref, wgd_ref, wab_ref, h_ref, dh2_ref, nw_ref, h0_ref, u0_ref, d0hg_ref,
             d0gd_ref, d0ab_ref, dx_ref, dx0_ref, dnw_ref, ghg_ref, ggd_ref, gab_ref, acc_hg, acc_gd, acc_ab):
        i = pl.program_id(0)
        nwv = nw_ref[...]

        def norm_bwd(dps, x):
            du = jnp.dot(dps[0], whg_ref[...], preferred_element_type=F32)
            du += jnp.dot(dps[1], wgd_ref[...], preferred_element_type=F32)
            du += jnp.dot(dps[2], wab_ref[...], preferred_element_type=F32)
            r = lax.rsqrt(jnp.mean(x * x, axis=-1, keepdims=True) + EPS)
            nrm = x * r
            dn = du * nwv
            return r * (dn - nrm * jnp.mean(dn * nrm, axis=-1, keepdims=True)), nrm, jnp.sum(du * nrm, axis=0, keepdims=True)

        def accumulate(dps, u, first):
            for acc, dp in zip((acc_hg, acc_gd, acc_ab), dps):
                step = min(acc.shape[0], 512)
                for lo in range(0, acc.shape[0], step):
                    part = _mm_tn(dp[:, lo:lo + step], u)
                    acc[lo:lo + step, :] = part if first else acc[lo:lo + step, :] + part

        @pl.when(i == 0)
        def _():
            dps0 = (d0hg_ref[...], d0gd_ref[...], d0ab_ref[...])
            dx0_ref[...], _, dnw_ref[...] = norm_bwd(dps0, h0_ref[...])
            accumulate(dps0, u0_ref[...], True)

        dps = (dphg_ref[...], dpgd_ref[...], dpab_ref[...])
        dx, nrm, dnw = norm_bwd(dps, h_ref[...])
        dx_ref[...] = dh2_ref[...] + dx
        dnw_ref[...] += dnw
        accumulate(dps, (nrm * nwv).astype(MXU_DTYPE), False)

        @pl.when(i == steps - 1)
        def _():
            pltpu.sync_copy(acc_hg, ghg_ref)
            pltpu.sync_copy(acc_gd, ggd_ref)
            pltpu.sync_copy(acc_ab, gab_ref)

    row = lambda w: pl.BlockSpec((tm, w), lambda i: (i, 0))
    full = lambda a: pl.BlockSpec(a.shape, lambda i: (0, 0), pipeline_mode=pl.Buffered(1))
    anywhere = pl.BlockSpec(memory_space=pl.ANY)
    return pl.pallas_call(
        body, grid=(steps,), name="in_proj_bwd",
        in_specs=[row(4 * WIDTH), row(4 * WIDTH), row(AB_PAD)] + _w_in_specs() + [row(D_MODEL), row(D_MODEL), full(norm_w),
                                                                                   full(h0), full(u0), full(dphg0), full(dpgd0),
                                                                                   full(dpab0)],
        out_specs=[row(D_MODEL), pl.BlockSpec(h0.shape, lambda i: (0, 0)), pl.BlockSpec((1, D_MODEL), lambda i: (0, 0)),
                   anywhere, anywhere, anywhere],
        out_shape=[jax.ShapeDtypeStruct((n, D_MODEL), F32), jax.ShapeDtypeStruct(h0.shape, F32),
                   jax.ShapeDtypeStruct((1, D_MODEL), F32), jax.ShapeDtypeStruct((4 * WIDTH, D_MODEL), F32),
                   jax.ShapeDtypeStruct((4 * WIDTH, D_MODEL), F32), jax.ShapeDtypeStruct((AB_PAD, D_MODEL), F32)],
        scratch_shapes=[pltpu.VMEM((4 * WIDTH, D_MODEL), F32), pltpu.VMEM((4 * WIDTH, D_MODEL), F32),
                        pltpu.VMEM((AB_PAD, D_MODEL), F32)],
        compiler_params=pltpu.CompilerParams(dimension_semantics=("arbitrary",), vmem_limit_bytes=VMEM_LIMIT_LARGE),
    )(dphg, dpgd, dpab, w_t, w_t, w_t, h, dh2, norm_w, h0, u0, dphg0, dpgd0, dpab0)


def _sds(shape, dtype=F32):
    return jax.ShapeDtypeStruct(shape, dtype)


def _pairs(b):
    return [(i, h) for i in range(b) for h in range(HEADS)]


def _load_slabs(ref, b, k):
    return jnp.stack([ref[i, k * CHUNK:(k + 1) * CHUNK, h * DH:(h + 1) * DH].astype(F32) for i, h in _pairs(b)], axis=0)


def _lead_slabs(a, b):
    return jnp.stack([a[:, h * DH:(h + 1) * DH].astype(F32) for _, h in _pairs(b)], axis=0)


def _rows(a3, i):
    return jnp.concatenate([a3[i * HEADS + h] for h in range(HEADS)], axis=1)


def _store_slabs(ref, a3, b, k):
    for i in range(b):
        ref[i, k * CHUNK:(k + 1) * CHUNK, :] = _rows(a3, i).astype(ref.dtype)


def _sum_rows(a3, b):
    out = _rows(a3, 0)
    for i in range(1, b):
        out = out + _rows(a3, i)
    return out


def _save_states(ref, s, b, k):
    for i in range(b):
        ref[i, k] = jnp.concatenate([s[i * HEADS + h] for h in range(HEADS)], axis=0).astype(ref.dtype)


def _load_states(ref, b, k):
    return jnp.stack([ref[i, k, h * DH:(h + 1) * DH, :].astype(F32) for i, h in _pairs(b)], axis=0)


def hg_local_fwd(p, p0, logits):
    b, seq, _ = p.shape
    rows = LOCAL_CHUNKS * CHUNK
    nreal = seq // CHUNK

    def body(p_ref, p0_ref, lg_ref, s_ref, st_ref, q_ref, k_ref, o_ref, eg_ref, q0_ref, k0_ref, o0_ref, eg0_ref):
        sum_mats = (s_ref[...], st_ref[...])

        @pl.when((pl.program_id(0) == 0) & (pl.program_id(1) == 0))
        def _():
            q_in, k_out, o0_ref[...], (eg0_ref[...],) = hg_local(p0_ref[...], lg_ref[...], sum_mats)
            q0_ref[...], k0_ref[...] = q_in.astype(MXU_DTYPE), k_out.astype(MXU_DTYPE)

        q_in, k_out, o_intra, egs = hg_local(p_ref[...], lg_ref[...], sum_mats)
        q_ref[...], k_ref[...], o_ref[...] = q_in.astype(MXU_DTYPE), k_out.astype(MXU_DTYPE), o_intra
        for c in range(LOCAL_CHUNKS):
            eg_ref[c] = egs[c]

    slab = pl.BlockSpec((None, rows, WIDTH), lambda s, g: (s, g, 0))
    const = lambda shape: pl.BlockSpec(shape, lambda s, g: (0, 0))
    lead_shapes = [(CHUNK, WIDTH)] * 3 + [(1, WIDTH)]
    sum_mats = _summation_matrices(_hg_sums, (HG_LEVELS + 1) * CHUNK)
    out = pl.pallas_call(
        body, grid=(b, seq // rows), name="hgrn2_local",
        in_specs=[pl.BlockSpec((None, rows, 4 * WIDTH), lambda s, g: (s, g, 0)), const(p0.shape), const(logits.shape)]
        + [const(a.shape) for a in sum_mats],
        out_specs=[slab, slab, slab, pl.BlockSpec((None, LOCAL_CHUNKS, 1, WIDTH), lambda s, g: (s, g, 0, 0))]
        + [const(s) for s in lead_shapes],
        out_shape=[_sds((b, seq, WIDTH), MXU_DTYPE)] * 2 + [_sds((b, seq, WIDTH)), _sds((b, nreal, 1, WIDTH))]
        + [_sds(lead_shapes[0], MXU_DTYPE)] * 2 + [_sds(lead_shapes[2]), _sds(lead_shapes[3])],
        compiler_params=_cparams("arbitrary", "arbitrary"),
    )(p, p0, logits, *sum_mats)
    return out[0:4], out[4:8]


def _hg_scan_args(b, k, q_ref, k_ref, o_ref, v_ref, z_ref, eg_ref):
    eg = jnp.stack([eg_ref[i, k, :, h * DH:(h + 1) * DH] for i, h in _pairs(b)], axis=0)
    return (_load_slabs(q_ref, b, k), _load_slabs(k_ref, b, k), _load_slabs(v_ref, b, k), eg, _load_slabs(o_ref, b, k),
            _load_slabs(z_ref, b, k))


def _hg_lead_args(b, q0_ref, k0_ref, o0_ref, p0_ref, eg0_ref):
    eg = jnp.stack([eg0_ref[:, h * DH:(h + 1) * DH] for _, h in _pairs(b)], axis=0)
    return (_lead_slabs(q0_ref[...], b), _lead_slabs(k0_ref[...], b), _lead_slabs(p0_ref[:, 2 * WIDTH:3 * WIDTH], b), eg,
            _lead_slabs(o0_ref[...], b), _lead_slabs(p0_ref[:, 3 * WIDTH:4 * WIDTH], b))


def _scan_specs(b, ng, reverse, chunks):
    group = (lambda i: ng - 1 - i) if reverse else (lambda i: i)
    slab = lambda lane_block: pl.BlockSpec((b, chunks * CHUNK, WIDTH), lambda i: (0, group(i), lane_block))
    per_chunk = lambda *tail: pl.BlockSpec((b, chunks) + tail, lambda i: (0, group(i)) + (0,) * len(tail))
    const = lambda a: pl.BlockSpec(a.shape, lambda i: (0,) * a.ndim)
    return slab, per_chunk, const


def run_scans(parts, nc, name):
    n_in = [len(p["args"]) for p in parts]
    n_out = [len(p["out_shape"]) for p in parts]
    n_scr = [len(p["scratch_shapes"]) for p in parts]

    def body(*refs):
        ins, outs, scr = refs[:sum(n_in)], refs[sum(n_in):sum(n_in) + sum(n_out)], refs[sum(n_in) + sum(n_out):]
        for i, part in enumerate(parts):
            part["body"](*ins[sum(n_in[:i]):sum(n_in[:i + 1])], *outs[sum(n_out[:i]):sum(n_out[:i + 1])],
                         *scr[sum(n_scr[:i]):sum(n_scr[:i + 1])])

    flat = lambda key: [v for p in parts for v in p[key]]
    out = pl.pallas_call(body, grid=(nc,), name=name, in_specs=flat("in_specs"), out_specs=flat("out_specs"),
                         out_shape=flat("out_shape"), scratch_shapes=flat("scratch_shapes"),
                         compiler_params=_cparams("arbitrary"))(*flat("args"))
    return [out[sum(n_out[:i]):sum(n_out[:i + 1])] for i in range(len(parts))]


def hg_scan_fwd(p, p0, local, lead, nw):
    b, seq, _ = p.shape
    q_in, k_out, o_intra, eg = local
    slab, per_chunk, const = _scan_specs(b, seq // (SCAN_CHUNKS_FWD * CHUNK), False, SCAN_CHUNKS_FWD)

    def body(q_ref, k_ref, o_ref, v_ref, z_ref, eg_ref, q0_ref, k0_ref, o0_ref, p0_ref, eg0_ref, nw_ref, y_ref, ss_ref, st):
        @pl.when(pl.program_id(0) == 0)
        def _():
            st[...] = hg_scan(*_hg_lead_args(b, q0_ref, k0_ref, o0_ref, p0_ref, eg0_ref), nw_ref[...], jnp.zeros(st.shape, F32))[1]

        s = st[...]
        for k in range(SCAN_CHUNKS_FWD):
            _save_states(ss_ref, s, b, k)
            y, s = hg_scan(*_hg_scan_args(b, k, q_ref, k_ref, o_ref, v_ref, z_ref, eg_ref), nw_ref[...], s)
            _store_slabs(y_ref, y, b, k)
        st[...] = s

    return dict(
        body=body, args=(q_in, k_out, o_intra, p, p, eg, lead[0], lead[1], lead[2], p0, lead[3], nw),
        in_specs=[slab(0), slab(0), slab(0), slab(2), slab(3), per_chunk(1, WIDTH)] + [const(a) for a in lead[0:3]]
        + [const(p0), const(lead[3]), const(nw)],
        out_specs=[slab(0), per_chunk(WIDTH, DH)],
        out_shape=[_sds((b, seq, WIDTH), MXU_DTYPE), _sds((b, seq // CHUNK, WIDTH, DH), MXU_DTYPE)],
        scratch_shapes=[pltpu.VMEM((b * HEADS, DH, DH), F32)])


def hg_scan_bwd(p, p0, local, lead, nw, ssave, dy):
    b, seq, _ = p.shape
    ng = seq // (SCAN_CHUNKS * CHUNK)
    q_in, k_out, o_intra, eg = local
    slab, per_chunk, const = _scan_specs(b, ng, True, SCAN_CHUNKS)

    def body(q_ref, k_ref, o_ref, v_ref, z_ref, eg_ref, q0_ref, k0_ref, o0_ref, p0_ref, eg0_ref, nw_ref, ss_ref, dy_ref,
             dq_ref, dk_ref, do_ref, dv_ref, dz_ref, deg_ref, dq0_ref, dk0_ref, do0_ref, dv0_ref, dz0_ref, deg0_ref, dnw_ref,
             dst):
        i = pl.program_id(0)

        @pl.when(i == 0)
        def _():
            dst[...] = jnp.zeros_like(dst)
            dnw_ref[...] = jnp.zeros_like(dnw_ref)

        ds = dst[...]
        for k in reversed(range(SCAN_CHUNKS)):
            args = _hg_scan_args(b, k, q_ref, k_ref, o_ref, v_ref, z_ref, eg_ref)
            _, vjp = jax.vjp(hg_scan, *args, nw_ref[...], _load_states(ss_ref, b, k))
            dq, dk, dv, deg, do, dz, dnw, ds = vjp((_load_slabs(dy_ref, b, k), ds))
            dnw_ref[...] += dnw
            for ref, val in ((dq_ref, dq), (dk_ref, dk), (do_ref, do), (dv_ref, dv), (dz_ref, dz)):
                _store_slabs(ref, val, b, k)
            for j in range(b):
                deg_ref[j, k] = _rows(deg, j)
        dst[...] = ds

        @pl.when(i == ng - 1)
        def _():
            args = _hg_lead_args(b, q0_ref, k0_ref, o0_ref, p0_ref, eg0_ref)
            _, vjp = jax.vjp(hg_scan, *args, nw_ref[...], jnp.zeros(dst.shape, F32))
            dq, dk, dv, deg, do, dz, dnw, _ = vjp((jnp.zeros((b * HEADS, CHUNK, DH), F32), ds))
            dnw_ref[...] += dnw
            for ref, val in ((dq0_ref, dq), (dk0_ref, dk), (do0_ref, do), (dv0_ref, dv), (dz0_ref, dz), (deg0_ref, deg)):
                ref[...] = _sum_rows(val, b)

    lead_out = [const(a) for a in lead[0:3]] + [const(lead[0]), const(lead[0]), const(lead[3])]
    return dict(
        body=body, args=(q_in, k_out, o_intra, p, p, eg, lead[0], lead[1], lead[2], p0, lead[3], nw, ssave, dy),
        in_specs=[slab(0), slab(0), slab(0), slab(2), slab(3), per_chunk(1, WIDTH)] + [const(a) for a in lead[0:3]]
        + [const(p0), const(lead[3]), const(nw), per_chunk(WIDTH, DH), slab(0)],
        out_specs=[slab(0)] * 5 + [per_chunk(1, WIDTH)] + lead_out + [const(nw)],
        out_shape=[_sds((b, seq, WIDTH))] * 3 + [_sds((b, seq, WIDTH), MXU_DTYPE)] * 2 + [_sds(eg.shape)]
        + [_sds((CHUNK, WIDTH))] * 5 + [_sds((1, WIDTH)), _sds(nw.shape)],
        scratch_shapes=[pltpu.VMEM((b * HEADS, DH, DH), F32)])


def _hg_local_vjp(sum_mats, p, logits, dq, dk, do, degs, dv, dz):
    _, vjp = jax.vjp(lambda p_, logits_: hg_local(p_, logits_, sum_mats), p, logits)
    dp, dlg = vjp((dq, dk, do, degs))
    return dp + jnp.concatenate([jnp.zeros((p.shape[0], 2 * WIDTH), F32), dv.astype(F32), dz.astype(F32)], axis=1), dlg


def hg_local_bwd(p, p0, logits, cot, cot0):
    b, seq, _ = p.shape
    rows = LOCAL_CHUNKS * CHUNK

    def body(p_ref, p0_ref, lg_ref, s_ref, st_ref, dq_ref, dk_ref, do_ref, dv_ref, dz_ref, deg_ref, dq0_ref, dk0_ref, do0_ref,
             dv0_ref, dz0_ref, deg0_ref, dp_ref, dp0_ref, dlg_ref):
        sum_mats = (s_ref[...], st_ref[...])

        @pl.when((pl.program_id(0) == 0) & (pl.program_id(1) == 0))
        def _():
            dp0, dlg_ref[...] = _hg_local_vjp(sum_mats, p0_ref[...], lg_ref[...], dq0_ref[...], dk0_ref[...], do0_ref[...],
                                              (deg0_ref[...],), dv0_ref[...], dz0_ref[...])
            dp0_ref[...] = dp0.astype(MXU_DTYPE)

        degs = tuple(deg_ref[c] for c in range(LOCAL_CHUNKS))
        dp, dlg = _hg_local_vjp(sum_mats, p_ref[...], lg_ref[...], dq_ref[...], dk_ref[...], do_ref[...], degs, dv_ref[...],
                                dz_ref[...])
        dp_ref[...] = dp.astype(MXU_DTYPE)
        dlg_ref[...] += dlg

    slab = pl.BlockSpec((None, rows, WIDTH), lambda s, g: (s, g, 0))
    wide = pl.BlockSpec((None, rows, 4 * WIDTH), lambda s, g: (s, g, 0))
    const = lambda a: pl.BlockSpec(a.shape, lambda s, g: (0, 0))
    sum_mats = _summation_matrices(_hg_sums, (HG_LEVELS + 1) * CHUNK)
    return pl.pallas_call(
        body, grid=(b, seq // rows), name="hgrn2_local_bwd",
        in_specs=[wide, const(p0), const(logits), const(sum_mats[0]), const(sum_mats[1]), slab, slab, slab, slab, slab,
                  pl.BlockSpec((None, LOCAL_CHUNKS, 1, WIDTH), lambda s, g: (s, g, 0, 0))] + [const(a) for a in cot0],
        out_specs=[wide, const(p0), const(logits)],
        out_shape=[_sds(p.shape, MXU_DTYPE), _sds(p0.shape, MXU_DTYPE), _sds(logits.shape)],
        compiler_params=_cparams("arbitrary", "arbitrary"),
    )(p, p0, logits, *sum_mats, *cot, *cot0)


def _halo_block(g):
    return jnp.maximum((LOCAL_CHUNKS * CHUNK // HALO) * g - 1, 0)


def _gd_window(g, p_ref, halo_ref, p0_ref):
    halo = jnp.where(g == 0, p0_ref[CHUNK - HALO:CHUNK, 0:QKV], halo_ref[...])
    return jnp.concatenate([halo, p_ref[:, 0:QKV]], axis=0)


def _lead_window(p0_ref):
    return jnp.concatenate([jnp.zeros((HALO, QKV), F32), p0_ref[:, 0:QKV]], axis=0)


def gd_local_fwd(p, p0, ab, ab0, cw, alog, dtb):
    b, seq, _ = p.shape
    rows = LOCAL_CHUNKS * CHUNK
    nreal = seq // CHUNK

    def body(p_ref, halo_ref, p0_ref, ab_ref, ab0_ref, cw_ref, al_ref, dt_ref, s_ref, st_ref, u_ref, w_ref, qe_ref, ke_ref,
             qk_ref, ea_ref, inv_ref, u0_ref, w0_ref, qe0_ref, ke0_ref, qk0_ref, ea0_ref, inv0_ref):
        sum_mats = (s_ref[...], st_ref[...])

        @pl.when((pl.program_id(0) == 0) & (pl.program_id(1) == 0))
        def _():
            (u0_ref[...], ww, qe, ke, qk0_ref[...], (ea0_ref[...],)), inv0_ref[...] = gd_local(
                _lead_window(p0_ref), ab0_ref[...], cw_ref[...], al_ref[...], dt_ref[...], sum_mats, inverse=_tri_y_impl)
            w0_ref[...], qe0_ref[...], ke0_ref[...] = ww.astype(MXU_DTYPE), qe.astype(MXU_DTYPE), ke.astype(MXU_DTYPE)

        (uu, ww, qe, ke, qk, eas), inv = gd_local(_gd_window(pl.program_id(1), p_ref, halo_ref, p0_ref), ab_ref[...],
                                                  cw_ref[...], al_ref[...], dt_ref[...], sum_mats, inverse=_tri_y_impl)
        u_ref[...], w_ref[...], qe_ref[...], ke_ref[...] = uu, ww.astype(MXU_DTYPE), qe.astype(MXU_DTYPE), ke.astype(MXU_DTYPE)
        for c in range(LOCAL_CHUNKS):
            qk_ref[c] = qk[c * HEADS * CHUNK:(c + 1) * HEADS * CHUNK]
            inv_ref[c] = inv[c * HEADS * CHUNK:(c + 1) * HEADS * CHUNK]
            ea_ref[c] = eas[c]

    const = lambda shape: pl.BlockSpec(shape, lambda s, g: (0, 0))
    slab = pl.BlockSpec((None, rows, WIDTH), lambda s, g: (s, g, 0))
    mats = pl.BlockSpec((None, LOCAL_CHUNKS, HEADS * CHUNK, CHUNK), lambda s, g: (s, g, 0, 0))
    lead_out = [_sds((CHUNK, WIDTH))] + [_sds((CHUNK, WIDTH), MXU_DTYPE)] * 3 + [_sds((HEADS * CHUNK, CHUNK)), _sds((1, AB_PAD)),
                                                                                _sds((HEADS * CHUNK, CHUNK))]
    sum_mats = _summation_matrices(_running_sum, CHUNK)
    out = pl.pallas_call(
        body, grid=(b, seq // rows), name="gdn_local",
        in_specs=[pl.BlockSpec((None, rows, 4 * WIDTH), lambda s, g: (s, g, 0)),
                  pl.BlockSpec((None, HALO, QKV), lambda s, g: (s, _halo_block(g), 0)), const(p0.shape),
                  pl.BlockSpec((None, rows, AB_PAD), lambda s, g: (s, g, 0)), const(ab0.shape), const(cw.shape),
                  const(alog.shape), const(dtb.shape), const(sum_mats[0].shape), const(sum_mats[1].shape)],
        out_specs=[slab] * 4 + [mats, pl.BlockSpec((None, LOCAL_CHUNKS, 1, AB_PAD), lambda s, g: (s, g, 0, 0)), mats]
        + [const(s.shape) for s in lead_out],
        out_shape=[_sds((b, seq, WIDTH))] + [_sds((b, seq, WIDTH), MXU_DTYPE)] * 3
        + [_sds((b, nreal, HEADS * CHUNK, CHUNK)), _sds((b, nreal, 1, AB_PAD)), _sds((b, nreal, HEADS * CHUNK, CHUNK))] + lead_out,
        compiler_params=_cparams("arbitrary", "arbitrary"),
    )(p, p, p0, ab, ab0, cw, alog, dtb, *sum_mats)
    return out[0:6], out[6], out[7:13], out[13]


def _gd_scan_args(b, k, u_ref, w_ref, qe_ref, ke_ref, qk_ref, ea_ref, z_ref):
    qk = jnp.stack([qk_ref[i, k, h * CHUNK:(h + 1) * CHUNK, :] for i, h in _pairs(b)], axis=0)
    ea = jnp.stack([ea_ref[i, k, :, h:h + 1] for i, h in _pairs(b)], axis=0)
    return (_load_slabs(u_ref, b, k), _load_slabs(w_ref, b, k), _load_slabs(qe_ref, b, k), _load_slabs(ke_ref, b, k), qk, ea,
            _load_slabs(z_ref, b, k))


def _gd_lead_args(b, u0_ref, w0_ref, qe0_ref, ke0_ref, qk0_ref, ea0_ref, p0_ref):
    qk = jnp.stack([qk0_ref[h * CHUNK:(h + 1) * CHUNK, :] for _, h in _pairs(b)], axis=0)
    ea = jnp.stack([ea0_ref[:, h:h + 1] for _, h in _pairs(b)], axis=0)
    return (_lead_slabs(u0_ref[...], b), _lead_slabs(w0_ref[...], b), _lead_slabs(qe0_ref[...], b), _lead_slabs(ke0_ref[...], b),
            qk, ea, _lead_slabs(p0_ref[:, QKV:QKV + WIDTH], b))


def gd_scan_fwd(p, p0, local, lead, nw):
    b, seq, _ = p.shape
    slab, per_chunk, const = _scan_specs(b, seq // (SCAN_CHUNKS_FWD * CHUNK), False, SCAN_CHUNKS_FWD)

    def body(u_ref, w_ref, qe_ref, ke_ref, qk_ref, ea_ref, z_ref, u0_ref, w0_ref, qe0_ref, ke0_ref, qk0_ref, ea0_ref, p0_ref,
             nw_ref, y_ref, ss_ref, st):
        @pl.when(pl.program_id(0) == 0)
        def _():
            lead_args = _gd_lead_args(b, u0_ref, w0_ref, qe0_ref, ke0_ref, qk0_ref, ea0_ref, p0_ref)
            st[...] = gd_scan(*lead_args, nw_ref[...], jnp.zeros(st.shape, F32))[1]

        s = st[...]
        for k in range(SCAN_CHUNKS_FWD):
            _save_states(ss_ref, s, b, k)
            y, s = gd_scan(*_gd_scan_args(b, k, u_ref, w_ref, qe_ref, ke_ref, qk_ref, ea_ref, z_ref), nw_ref[...], s)
            _store_slabs(y_ref, y, b, k)
        st[...] = s

    return dict(
        body=body, args=(*local, p, *lead, p0, nw),
        in_specs=[slab(0)] * 4 + [per_chunk(HEADS * CHUNK, CHUNK), per_chunk(1, AB_PAD), slab(3)] + [const(a) for a in lead]
        + [const(p0), const(nw)],
        out_specs=[slab(0), per_chunk(WIDTH, DH)],
        out_shape=[_sds((b, seq, WIDTH), MXU_DTYPE), _sds((b, seq // CHUNK, WIDTH, DH), MXU_DTYPE)],
        scratch_shapes=[pltpu.VMEM((b * HEADS, DH, DH), F32)])


def gd_scan_bwd(p, p0, local, lead, nw, ssave, dy):
    b, seq, _ = p.shape
    ng = seq // (SCAN_CHUNKS * CHUNK)
    slab, per_chunk, const = _scan_specs(b, ng, True, SCAN_CHUNKS)

    def body(u_ref, w_ref, qe_ref, ke_ref, qk_ref, ea_ref, z_ref, u0_ref, w0_ref, qe0_ref, ke0_ref, qk0_ref, ea0_ref, p0_ref,
             nw_ref, ss_ref, dy_ref, du_ref, dw_ref, dqe_ref, dke_ref, dqk_ref, dea_ref, dz_ref, du0_ref, dw0_ref, dqe0_ref,
             dke0_ref, dqk0_ref, dea0_ref, dz0_ref, dnw_ref, dst):
        i = pl.program_id(0)
        lane = lax.broadcasted_iota(jnp.int32, (1, AB_PAD), 1)

        def gate_rows(dea, j):
            return sum(jnp.where(lane == h, dea[j * HEADS + h], 0.0) for h in range(HEADS))

        def matrix_rows(dqk, j):
            return jnp.concatenate([dqk[j * HEADS + h] for h in range(HEADS)], axis=0)

        @pl.when(i == 0)
        def _():
            dst[...] = jnp.zeros_like(dst)
            dnw_ref[...] = jnp.zeros_like(dnw_ref)

        ds = dst[...]
        for k in reversed(range(SCAN_CHUNKS)):
            args = _gd_scan_args(b, k, u_ref, w_ref, qe_ref, ke_ref, qk_ref, ea_ref, z_ref)
            _, vjp = jax.vjp(gd_scan, *args, nw_ref[...], _load_states(ss_ref, b, k))
            du, dw, dqe, dke, dqk, dea, dz, dnw, ds = vjp((_load_slabs(dy_ref, b, k), ds))
            dnw_ref[...] += dnw
            for ref, val in ((du_ref, du), (dw_ref, dw), (dqe_ref, dqe), (dke_ref, dke), (dz_ref, dz)):
                _store_slabs(ref, val, b, k)
            for j in range(b):
                dqk_ref[j, k] = matrix_rows(dqk, j)
                dea_ref[j, k] = gate_rows(dea, j)
        dst[...] = ds

        @pl.when(i == ng - 1)
        def _():
            args = _gd_lead_args(b, u0_ref, w0_ref, qe0_ref, ke0_ref, qk0_ref, ea0_ref, p0_ref)
            _, vjp = jax.vjp(gd_scan, *args, nw_ref[...], jnp.zeros(dst.shape, F32))
            du, dw, dqe, dke, dqk, dea, dz, dnw, _ = vjp((jnp.zeros((b * HEADS, CHUNK, DH), F32), ds))
            dnw_ref[...] += dnw
            for ref, val in ((du0_ref, du), (dw0_ref, dw), (dqe0_ref, dqe), (dke0_ref, dke), (dz0_ref, dz)):
                ref[...] = _sum_rows(val, b)
            dqk0_ref[...] = sum((matrix_rows(dqk, j) for j in range(1, b)), matrix_rows(dqk, 0))
            dea0_ref[...] = sum((gate_rows(dea, j) for j in range(1, b)), gate_rows(dea, 0))

    uu, ww, qe, ke, qk, ea = local
    return dict(
        body=body, args=(*local, p, *lead, p0, nw, ssave, dy),
        in_specs=[slab(0)] * 4 + [per_chunk(HEADS * CHUNK, CHUNK), per_chunk(1, AB_PAD), slab(3)] + [const(a) for a in lead]
        + [const(p0), const(nw), per_chunk(WIDTH, DH), slab(0)],
        out_specs=[slab(0)] * 4 + [per_chunk(HEADS * CHUNK, CHUNK), per_chunk(1, AB_PAD), slab(0)] + [const(a) for a in lead]
        + [const(lead[0]), const(nw)],
        out_shape=[_sds((b, seq, WIDTH))] * 4 + [_sds(qk.shape), _sds(ea.shape), _sds((b, seq, WIDTH), MXU_DTYPE)]
        + [_sds(a.shape) for a in lead] + [_sds(lead[0].shape), _sds(nw.shape)],
        scratch_shapes=[pltpu.VMEM((b * HEADS, DH, DH), F32)])


def _gd_local_vjp(sum_mats, inv_rows, xx, ab, cw, alog, dtb):
    nb = ab.shape[0] // CHUNK
    inv = jnp.stack([inv_rows[g * CHUNK:(g + 1) * CHUNK] for g in range(nb * HEADS)], axis=0)
    _, vjp, _ = jax.vjp(lambda *a: gd_local(*a, sum_mats, inverse=_saved_inverse(inv)), xx, ab, cw, alog, dtb, has_aux=True)
    return vjp


def gd_local_bwd(p, p0, ab, ab0, cw, alog, dtb, inv, inv0, cot, dz, cot0, dz0):
    b, seq, _ = p.shape
    rows = LOCAL_CHUNKS * CHUNK
    ng = seq // rows
    du, dw, dqe, dke, dqk, dea = cot

    def body(p_ref, halo_ref, p0_ref, ab_ref, ab0_ref, cw_ref, al_ref, dt_ref, s_ref, st_ref, inv_ref, inv0_ref, du_ref, dw_ref,
             dqe_ref, dke_ref, dqk_ref, dea_ref, dz_ref, du0_ref, dw0_ref, dqe0_ref, dke0_ref, dqk0_ref, dea0_ref, dz0_ref,
             dp_ref, dab_ref, dp0_ref, dab0_ref, dcw_ref, dal_ref, ddt_ref, dhalo, dtail):
        s, i = pl.program_id(0), pl.program_id(1)
        g = ng - 1 - i
        sum_mats = (s_ref[...], st_ref[...])

        @pl.when(i == 0)
        def _():
            dhalo[...] = jnp.zeros_like(dhalo)

        @pl.when((s == 0) & (i == 0))
        def _():
            dtail[...] = jnp.zeros_like(dtail)
            dcw_ref[...] = jnp.zeros_like(dcw_ref)
            dal_ref[...] = jnp.zeros_like(dal_ref)
            ddt_ref[...] = jnp.zeros_like(ddt_ref)

        def finish(dxx, dab, dcw, dal, ddt, before, n, dz_val, dp_out, dab_out):
            dqkv = dxx[HALO:HALO + n] + jnp.concatenate([jnp.zeros((n - HALO, QKV), F32), before], axis=0)
            dp_out[...] = jnp.concatenate([dqkv.astype(MXU_DTYPE), dz_val.astype(MXU_DTYPE)], axis=1)
            dab_out[...] = dab.astype(MXU_DTYPE)
            dcw_ref[...] += dcw
            dal_ref[...] += dal
            ddt_ref[...] += ddt

        inv_rows = jnp.concatenate([inv_ref[c] for c in range(LOCAL_CHUNKS)], axis=0)
        vjp = _gd_local_vjp(sum_mats, inv_rows, _gd_window(g, p_ref, halo_ref, p0_ref), ab_ref[...], cw_ref[...], al_ref[...],
                            dt_ref[...])
        dqk_all = jnp.concatenate([dqk_ref[c] for c in range(LOCAL_CHUNKS)], axis=0)
        deas = tuple(dea_ref[c] for c in range(LOCAL_CHUNKS))
        grads = vjp((du_ref[...], dw_ref[...], dqe_ref[...], dke_ref[...], dqk_all, deas))
        finish(*grads, dhalo[...], rows, dz_ref[...], dp_ref, dab_ref)
        dhalo[...] = grads[0][0:HALO]

        @pl.when(g == 0)
        def _():
            dtail[...] += grads[0][0:HALO]

        @pl.when((s == b - 1) & (g == 0))
        def _():
            vjp0 = _gd_local_vjp(sum_mats, inv0_ref[...], _lead_window(p0_ref), ab0_ref[...], cw_ref[...], al_ref[...],
                                 dt_ref[...])
            grads0 = vjp0((du0_ref[...], dw0_ref[...], dqe0_ref[...], dke0_ref[...], dqk0_ref[...], (dea0_ref[...],)))
            finish(*grads0, dtail[...], CHUNK, dz0_ref[...], dp0_ref, dab0_ref)

    rg = lambda i: ng - 1 - i
    const = lambda a: pl.BlockSpec(a.shape, lambda s, i: (0, 0))
    slab = pl.BlockSpec((None, rows, WIDTH), lambda s, i: (s, rg(i), 0))
    wide = pl.BlockSpec((None, rows, 4 * WIDTH), lambda s, i: (s, rg(i), 0))
    gates = pl.BlockSpec((None, rows, AB_PAD), lambda s, i: (s, rg(i), 0))
    mats = pl.BlockSpec((None, LOCAL_CHUNKS, HEADS * CHUNK, CHUNK), lambda s, i: (s, rg(i), 0, 0))
    sum_mats = _summation_matrices(_running_sum, CHUNK)
    return pl.pallas_call(
        body, grid=(b, ng), name="gdn_local_bwd",
        in_specs=[wide, pl.BlockSpec((None, HALO, QKV), lambda s, i: (s, _halo_block(rg(i)), 0)), const(p0), gates, const(ab0),
                  const(cw), const(alog), const(dtb), const(sum_mats[0]), const(sum_mats[1]), mats, const(inv0), slab, slab,
                  slab, slab, mats,
                  pl.BlockSpec((None, LOCAL_CHUNKS, 1, AB_PAD), lambda s, i: (s, rg(i), 0, 0)), slab]
        + [const(a) for a in cot0] + [const(dz0)],
        out_specs=[wide, gates, const(p0), const(ab0), const(cw), const(alog), const(dtb)],
        out_shape=[_sds(p.shape, MXU_DTYPE), _sds(ab.shape, MXU_DTYPE), _sds(p0.shape, MXU_DTYPE), _sds(ab0.shape, MXU_DTYPE),
                   _sds(cw.shape), _sds(alog.shape), _sds(dtb.shape)],
        scratch_shapes=[pltpu.VMEM((HALO, QKV), F32), pltpu.VMEM((HALO, QKV), F32)],
        compiler_params=_cparams("arbitrary", "arbitrary"),
    )(p, p, p0, ab, ab0, cw, alog, dtb, *sum_mats, inv, inv0, du, dw, dqe, dke, dqk, dea, dz, *cot0, dz0)


def _position():
    return lax.axis_index("x"), lax.axis_index("y"), lax.axis_index("c")


EXCHANGE_COPIES = 10


def _exchange_blocks(bufs, send_sems, recv_sems):
    x, y, c = _position()
    here, x_nbr, y_nbr, diag = (x, y), (1 - x, y), (x, 1 - y), (1 - x, 1 - y)
    sibling = (x, y, 1 - c)
    me = (x, y, c)
    n = range(len(bufs))

    def rows(a, chip, core, half=None):
        block = bufs[a].at[4 * chip[0] + 2 * chip[1] + core]
        if half is None:
            return block
        total = bufs[a].shape[1]
        tile = 8 * (4 // jnp.dtype(bufs[a].dtype).itemsize)
        split = total // 2 // tile * tile
        return block.at[pl.ds(0, split)] if half == 0 else block.at[pl.ds(split, total - split)]

    def copy(a, k, region, to):
        return pltpu.make_async_remote_copy(src_ref=region, dst_ref=region, send_sem=send_sems.at[a * EXCHANGE_COPIES + k],
                                            recv_sem=recv_sems.at[a * EXCHANGE_COPIES + k], device_id=to, device_id_type=MESH)

    sent = [copy(a, 0, rows(a, here, c), sibling) for a in n]
    sent += [cp for a in n for cp in (copy(a, 1, rows(a, here, c, 0), (*x_nbr, c)), copy(a, 4, rows(a, here, c, 1), (*y_nbr, c)))]
    sent += [cp for a in n for cp in (copy(a, 2, rows(a, here, c, 1), (*x_nbr, c)), copy(a, 3, rows(a, here, c, 0), (*y_nbr, c)))]
    for cp in sent:
        cp.start()

    def after(arrivals, a, k, region, to):
        for cp in arrivals:
            cp.wait_recv()
        sent.append(copy(a, k, region, to))
        sent[-1].start()

    for a in n:
        after([copy(a, 1, rows(a, x_nbr, c, 0), me)], a, 5, rows(a, x_nbr, c, 0), (*y_nbr, c))
        after([copy(a, 4, rows(a, y_nbr, c, 1), me)], a, 6, rows(a, y_nbr, c, 1), (*x_nbr, c))
    for a in n:
        after([copy(a, 2, rows(a, x_nbr, c, 1), me)], a, 7, rows(a, x_nbr, c), sibling)
        after([copy(a, 3, rows(a, y_nbr, c, 0), me)], a, 8, rows(a, y_nbr, c), sibling)
    for a in n:
        after([copy(a, 5, rows(a, diag, c, 0), me), copy(a, 6, rows(a, diag, c, 1), me)], a, 9, rows(a, diag, c), sibling)
    for a in n:
        copy(a, 0, rows(a, here, 1 - c), me).wait_recv()
        for k, chip in ((7, x_nbr), (8, y_nbr), (9, diag)):
            copy(a, k, rows(a, chip, 1 - c), me).wait_recv()
    for cp in sent:
        cp.wait_send()


def _exchange_sems(n_bufs):
    return [pltpu.SemaphoreType.DMA((n_bufs * EXCHANGE_COPIES,)), pltpu.SemaphoreType.DMA((n_bufs * EXCHANGE_COPIES,))]


def gather_weights(w_in_t, w_out, small, pad_rows):
    rows, _, cols = w_in_t.shape
    buf_rows = -(-rows // ROW_TILE_BF16) * ROW_TILE_BF16

    def body(wi_ref, wo_ref, sm_ref, wi_out, wo_out, sm_out, wi_buf, send_sems, recv_sems):
        x, y, c = _position()
        me = 4 * x + 2 * y + c
        wi_buf[me, pl.ds(0, rows), :] = wi_ref[:, 0, :].astype(MXU_DTYPE)
        wi_buf[me, pl.ds(rows, buf_rows - rows), :] = jnp.zeros((buf_rows - rows, cols), MXU_DTYPE)
        wo_out[me] = wo_ref[...].astype(MXU_DTYPE)
        sm_out[me] = sm_ref[...]
        _exchange_blocks([wi_buf, wo_out, sm_out], send_sems, recv_sems)
        for d in range(N_DEV):
            wi_out[pl.ds(d * rows, rows), :] = wi_buf[d, pl.ds(0, rows), :]
        wi_out[pl.ds(N_DEV * rows, pad_rows), :] = jnp.zeros((pad_rows, cols), MXU_DTYPE)

    return pl.pallas_call(
        body, name="gather_weights", in_specs=[VMEM_SPEC] * 3, out_specs=[VMEM_SPEC] * 3,
        out_shape=[jax.ShapeDtypeStruct((N_DEV * rows + pad_rows, cols), MXU_DTYPE),
                   jax.ShapeDtypeStruct((N_DEV,) + w_out.shape, MXU_DTYPE), jax.ShapeDtypeStruct((N_DEV,) + small.shape, F32)],
        scratch_shapes=[pltpu.VMEM((N_DEV, buf_rows, cols), MXU_DTYPE)] + _exchange_sems(3),
        compiler_params=pltpu.CompilerParams(vmem_limit_bytes=VMEM_LIMIT))(w_in_t, w_out, small)


HOPS = 6


def reduce_gradients(tensors, small, name):
    n_t = len(tensors)
    arrays = [a for parts, _ in tensors for a, _ in parts]
    first_array = [sum(len(parts) for parts, _ in tensors[:t]) for t in range(n_t)]

    def pieces(t, j):
        parts, block_rows = tensors[t]
        out, base = [], 0
        for pi, (_, valid) in enumerate(parts):
            lo, hi = max(j * block_rows, base), min((j + 1) * block_rows, base + valid)
            if lo < hi:
                out.append((first_array[t] + pi, lo - base, lo - j * block_rows, hi - lo))
            base += valid
        return out

    def body(*refs):
        n_a = len(arrays)
        in_refs, small_ref = refs[:n_a], refs[n_a]
        out_refs, small_sum = refs[n_a + 1:n_a + 1 + n_t], refs[n_a + 1 + n_t]
        bufs, small_buf = refs[n_a + 2 + n_t:n_a + 2 + 5 * n_t], refs[n_a + 2 + 5 * n_t]
        s1_sems, r1_sems, s2_sems, r2_sems, small_send, small_recv = refs[n_a + 3 + 5 * n_t:]
        x, y, c = _position()
        chip = 2 * x + y

        def put(t, dst, j, add=None):
            for ai, src_row, dst_row, size in pieces(t, j):
                v = in_refs[ai][pl.ds(src_row, size), :]
                if add is not None:
                    v = v + add[pl.ds(dst_row, size), :].astype(F32)
                dst[pl.ds(dst_row, size), :] = v.astype(dst.dtype)

        def swap(t, k):
            send1, recv1 = bufs[4 * t], bufs[4 * t + 1]
            return pltpu.make_async_remote_copy(src_ref=send1.at[k], dst_ref=recv1.at[k], send_sem=s1_sems.at[4 * t + k],
                                                recv_sem=r1_sems.at[4 * t + k], device_id=(x, y, 1 - c), device_id_type=MESH)

        to_x, to_y, to_diag = 2 * (1 - x) + y, 2 * x + (1 - y), 2 * (1 - x) + (1 - y)
        x_dev, y_dev = (1 - x, y, c), (x, 1 - y, c)

        def half(ref, h):
            total = ref.shape[0]
            split = total // 2 // ROW_TILE_BF16 * ROW_TILE_BF16
            return ref.at[pl.ds(0, split)] if h == 0 else ref.at[pl.ds(split, total - split)]

        def hop(t, copy_id, src, dst, to):
            return pltpu.make_async_remote_copy(src_ref=src, dst_ref=dst, send_sem=s2_sems.at[HOPS * t + copy_id],
                                                recv_sem=r2_sems.at[HOPS * t + copy_id], device_id=to, device_id_type=MESH)

        def hops(t):
            send2, landing = bufs[4 * t + 2], bufs[4 * t + 3]
            return [hop(t, 0, half(send2.at[to_diag], 0), half(landing.at[0], 0), x_dev),
                    hop(t, 1, half(send2.at[to_diag], 1), half(landing.at[0], 1), y_dev),
                    hop(t, 2, half(send2.at[to_x], 0), half(landing.at[1], 0), x_dev),
                    hop(t, 3, half(send2.at[to_y], 1), half(landing.at[2], 1), y_dev),
                    hop(t, 4, half(send2.at[to_x], 1), half(landing.at[1], 1), x_dev),
                    hop(t, 5, half(send2.at[to_y], 0), half(landing.at[2], 0), y_dev)]

        def add_relay(t, slot, h):
            dst, src = half(bufs[4 * t + 2].at[slot], h), half(bufs[4 * t + 3].at[0], h)
            dst[...] = (dst[...].astype(F32) + src[...].astype(F32)).astype(dst.dtype)

        for t in range(n_t):
            send2 = bufs[4 * t + 2]
            pad = send2.shape[1] - tensors[t][1]
            if pad:
                send2[:, pl.ds(tensors[t][1], pad), :] = jnp.zeros((4, pad, send2.shape[2]), send2.dtype)
            for j in range(N_DEV):
                @pl.when((j & 1) != c)
                def _():
                    put(t, bufs[4 * t].at[j >> 1], j)
            for k in range(4):
                swap(t, k).start()

        small_buf[4 * x + 2 * y + c] = small_ref[...]
        _exchange_blocks([small_buf], small_send, small_recv)
        total = small_buf[0]
        for d in range(1, N_DEV):
            total = total + small_buf[d]
        small_sum[...] = total

        for t in range(n_t):
            recv1 = bufs[4 * t + 1]
            for k in range(4):
                swap(t, k).wait_recv()
                for j in (2 * k, 2 * k + 1):
                    @pl.when(((j & 1) == c) & (k != chip))
                    def _():
                        put(t, bufs[4 * t + 2].at[k], j, add=recv1.at[k])

                    @pl.when(((j & 1) == c) & (k == chip))
                    def _():
                        put(t, out_refs[t], j, add=recv1.at[k])
            for cp in hops(t)[0:4]:
                cp.start()

        for t in range(n_t):
            cps = hops(t)
            cps[0].wait_recv()
            add_relay(t, to_y, 0)
            cps[5].start()
            cps[1].wait_recv()
            add_relay(t, to_x, 1)
            cps[4].start()

        for t in range(n_t):
            cps, rows = hops(t), tensors[t][1]
            for first, second, slot in ((cps[2], cps[4], 1), (cps[3], cps[5], 2)):
                first.wait_recv()
                second.wait_recv()
                out_refs[t][...] += bufs[4 * t + 3][slot, pl.ds(0, rows), :].astype(F32)

        for t in range(n_t):
            for cp in hops(t):
                cp.wait_send()
            for k in range(4):
                swap(t, k).wait_send()

    scratch, out_shape = [], []
    for parts, block_rows in tensors:
        cols = parts[0][0].shape[1]
        tiled_rows = -(-block_rows // ROW_TILE_BF16) * ROW_TILE_BF16
        scratch += [pltpu.VMEM((4, block_rows, cols), MXU_DTYPE)] * 2
        scratch += [pltpu.VMEM((4, tiled_rows, cols), MXU_DTYPE), pltpu.VMEM((3, tiled_rows, cols), MXU_DTYPE)]
        out_shape.append(jax.ShapeDtypeStruct((block_rows, cols), F32))
    out_shape.append(jax.ShapeDtypeStruct(small.shape, F32))
    scratch += [pltpu.VMEM((N_DEV,) + small.shape, F32)] + [pltpu.SemaphoreType.DMA((4 * n_t,))] * 2
    scratch += [pltpu.SemaphoreType.DMA((HOPS * n_t,))] * 2 + _exchange_sems(1)
    return pl.pallas_call(
        body, name=name, in_specs=[VMEM_SPEC] * (len(arrays) + 1), out_specs=[VMEM_SPEC] * (n_t + 1), out_shape=out_shape,
        scratch_shapes=scratch, compiler_params=pltpu.CompilerParams(vmem_limit_bytes=VMEM_LIMIT),
    )(*arrays, small)


def _adamw_step(w, g, m, v):
    mn = ADAM_B1 * m + (1.0 - ADAM_B1) * g
    vn = ADAM_B2 * v + (1.0 - ADAM_B2) * jnp.square(g)
    m_hat = mn / (1.0 - ADAM_B1 ** ADAM_STEP)
    v_hat = vn / (1.0 - ADAM_B2 ** ADAM_STEP)
    return -ADAM_LR * (m_hat / (jnp.sqrt(v_hat) + ADAM_EPS) + ADAM_WD * w), mn, vn


def adamw_small(ws, gs, ms, vs):
    k = len(ws)

    def body(*refs):
        ins, outs = refs[:4 * k], refs[4 * k:]
        for i in range(k):
            w_ref, g_ref, m_ref, v_ref = ins[i::k]
            outs[3 * i][...], outs[3 * i + 1][...], outs[3 * i + 2][...] = _adamw_step(w_ref[...], g_ref[...], m_ref[...],
                                                                                      v_ref[...])

    out = pl.pallas_call(body, name="adamw_small", in_specs=[VMEM_SPEC] * (4 * k), out_specs=[VMEM_SPEC] * (3 * k),
                         out_shape=[jax.ShapeDtypeStruct(w.shape, F32) for w in ws for _ in range(3)],
                         compiler_params=pltpu.CompilerParams(vmem_limit_bytes=VMEM_LIMIT))(*ws, *gs, *ms, *vs)
    return [out[3 * i:3 * i + 3] for i in range(k)]


def adamw_w_in(w, g_t, m, v):
    def body(w_ref, g_ref, m_ref, v_ref, go_ref, d_ref, nm_ref, nv_ref):
        g = g_ref[...]
        go_ref[:, 0, :] = g
        d_ref[:, 0, :], nm_ref[:, 0, :], nv_ref[:, 0, :] = _adamw_step(w_ref[:, 0, :], g, m_ref[:, 0, :], v_ref[:, 0, :])

    return pl.pallas_call(body, name="adamw_w_in", in_specs=[VMEM_SPEC] * 4, out_specs=[VMEM_SPEC] * 4,
                          out_shape=[jax.ShapeDtypeStruct(w.shape, F32)] * 4,
                          compiler_params=pltpu.CompilerParams(vmem_limit_bytes=VMEM_LIMIT))(w, g_t, m, v)


def _pad_rows(a, rows=8):
    return jnp.pad(a, ((0, rows - a.shape[0]), (0, 0)))


def _pad_lanes(a, lanes=128):
    return jnp.pad(a, ((0, 0), (0, lanes - a.shape[1])))


def kernel(x, meta_tokens, norm_w, w_in, conv_w, hg_lb_logits, hg_norm_w, gdn_A_log, gdn_dt_bias, gdn_norm_w, w_out, final_norm_w, loss_target, m_meta_tokens, m_norm_w, m_w_in, m_conv_w, m_hg_lb_logits, m_hg_norm_w, m_gdn_A_log, m_gdn_dt_bias, m_gdn_norm_w, m_w_out, m_final_norm_w, v_meta_tokens, v_norm_w, v_w_in, v_conv_w, v_hg_lb_logits, v_hg_norm_w, v_gdn_A_log, v_gdn_dt_bias, v_gdn_norm_w, v_w_out, v_final_norm_w):
    b, seq, _ = x.shape
    n = b * seq
    dev = 4 * lax.axis_index("x") + 2 * lax.axis_index("y") + lax.axis_index("c")
    col_shard = IN_COLS // N_DEV

    small_w = jnp.concatenate([_pad_lanes(meta_tokens, 256), _pad_rows(_pad_lanes(conv_w[0], 256))], axis=0)
    w_t, w_out_g, small_g = gather_weights(jnp.transpose(w_in, (2, 0, 1)), w_out[0], small_w, AB_PAD - 2 * HEADS)
    meta_g = small_g[:, 0:N_META, 0:D_MODEL // N_DEV]
    conv_g = small_g[:, N_META:N_META + CONV_TAPS, 0:QKV // N_DEV]
    w_out_full = w_out_g.reshape(2 * WIDTH, D_MODEL)
    cw = jnp.transpose(conv_g, (1, 0, 2)).reshape(CONV_TAPS, QKV)
    meta = jnp.transpose(meta_g, (1, 0, 2)).reshape(N_META, D_MODEL)
    alog = _pad_lanes(gdn_A_log)
    dtb = _pad_lanes(gdn_dt_bias)
    fw = final_norm_w.reshape(1, D_MODEL)

    h0 = jnp.concatenate([jnp.zeros((CHUNK - N_META, D_MODEL), F32), meta], axis=0)
    x2 = x.reshape(n, D_MODEL)
    phg, pgd, pab, phg0, pgd0, pab0, u0 = in_proj(x2, h0, norm_w, w_t)
    phg3, pgd3, pab3 = phg.reshape(b, seq, 4 * WIDTH), pgd.reshape(b, seq, 4 * WIDTH), pab.reshape(b, seq, AB_PAD)
    hg_loc, hg_lead = hg_local_fwd(phg3, phg0, hg_lb_logits)
    gd_loc, gd_inv, gd_lead, gd_inv0 = gd_local_fwd(pgd3, pgd0, pab3, pab0, cw, alog, dtb)
    (y_hg, s_hg), (y_gd, s_gd) = run_scans([hg_scan_fwd(phg3, phg0, hg_loc, hg_lead, hg_norm_w),
                                            gd_scan_fwd(pgd3, pgd0, gd_loc, gd_lead, gdn_norm_w)],
                                           seq // (SCAN_CHUNKS_FWD * CHUNK), "scans")

    dh2, dy_hg, dy_gd, g_w_out, loss_part, g_fw = out_proj_loss(
        x2, loss_target.reshape(n, D_MODEL), y_hg.reshape(n, WIDTH), y_gd.reshape(n, WIDTH), w_out_full, fw)

    hb, gb = run_scans([hg_scan_bwd(phg3, phg0, hg_loc, hg_lead, hg_norm_w, s_hg, dy_hg.reshape(b, seq, WIDTH)),
                        gd_scan_bwd(pgd3, pgd0, gd_loc, gd_lead, gdn_norm_w, s_gd, dy_gd.reshape(b, seq, WIDTH))],
                       seq // (SCAN_CHUNKS * CHUNK), "scans_bwd")
    dphg, dphg0, g_lb = hg_local_bwd(phg3, phg0, hg_lb_logits, hb[0:6], hb[6:12])
    g_hg_nw = hb[12]
    dpgd, dpab, dpgd0, dpab0, g_cw, g_alog, g_dtb = gd_local_bwd(pgd3, pgd0, pab3, pab0, cw, alog, dtb, gd_inv, gd_inv0,
                                                                 gb[0:6], gb[6], gb[7:13], gb[13])
    g_gd_nw = gb[14]
    dphg, dpgd, dpab = dphg.reshape(n, 4 * WIDTH), dpgd.reshape(n, 4 * WIDTH), dpab.reshape(n, AB_PAD)

    grad_x, dh0, g_nw, g_w_hg, g_w_gd, g_w_ab = in_proj_bwd(dphg, dpgd, dpab, w_t, x2, dh2, norm_w, h0, u0, dphg0, dpgd0, dpab0)

    small = jnp.concatenate([
        g_nw.reshape(8, 128), g_lb.reshape(8, 128), _pad_rows(g_hg_nw), _pad_rows(g_alog), _pad_rows(g_dtb), _pad_rows(g_gd_nw),
        g_fw.reshape(8, 128), g_cw.reshape(48, 128),
        dh0[CHUNK - N_META:CHUNK].reshape(128, 128), loss_part], axis=0)
    g_w_in_t, g_w_out, small = reduce_gradients(
        [([(g_w_hg, 4 * WIDTH), (g_w_gd, 4 * WIDTH), (g_w_ab, 2 * HEADS)], col_shard),
         ([(g_w_out, 2 * WIDTH)], (2 * WIDTH) // N_DEV)], small, "reduce_gradients")
    g_norm_w = small[0:8].reshape(1, D_MODEL)
    g_lb = small[8:16].reshape(2, WIDTH)
    g_hg_nw = small[16:17]
    g_alog = small[24:25, 0:HEADS]
    g_dtb = small[32:33, 0:HEADS]
    g_gd_nw = small[40:41]
    g_fw = small[48:56].reshape(1, D_MODEL)
    g_cw_full = small[56:104].reshape(CONV_TAPS, QKV)
    g_meta_full = small[104:232].reshape(N_META, D_MODEL)
    loss = small[232, 0]
    g_conv = lax.dynamic_slice_in_dim(g_cw_full, dev * (QKV // N_DEV), QKV // N_DEV, axis=1)
    g_meta = lax.dynamic_slice_in_dim(g_meta_full, dev * (D_MODEL // N_DEV), D_MODEL // N_DEV, axis=1)

    names = ["meta_tokens", "norm_w", "w_in", "conv_w", "hg_lb_logits", "hg_norm_w", "gdn_A_log", "gdn_dt_bias",
             "gdn_norm_w", "w_out", "final_norm_w"]
    weights = [meta_tokens, norm_w, w_in, conv_w, hg_lb_logits, hg_norm_w, gdn_A_log, gdn_dt_bias, gdn_norm_w, w_out,
               final_norm_w]
    moms = [m_meta_tokens, m_norm_w, m_w_in, m_conv_w, m_hg_lb_logits, m_hg_norm_w, m_gdn_A_log, m_gdn_dt_bias,
            m_gdn_norm_w, m_w_out, m_final_norm_w]
    vars_ = [v_meta_tokens, v_norm_w, v_w_in, v_conv_w, v_hg_lb_logits, v_hg_norm_w, v_gdn_A_log, v_gdn_dt_bias,
             v_gdn_norm_w, v_w_out, v_final_norm_w]
    grads2d = [g_meta, g_norm_w, g_w_in_t, g_conv, g_lb, g_hg_nw, g_alog, g_dtb, g_gd_nw, g_w_out, g_fw]
    i_w_in = names.index("w_in")
    others = [i for i in range(len(names)) if i != i_w_in]
    as_grad = lambda i, a: a.reshape(grads2d[i].shape)
    stepped = adamw_small([as_grad(i, weights[i]) for i in others], [grads2d[i] for i in others],
                          [as_grad(i, moms[i]) for i in others], [as_grad(i, vars_[i]) for i in others])
    results = {i: [a.reshape(weights[i].shape) for a in (grads2d[i], *stepped[j])] for j, i in enumerate(others)}
    to3, back = (lambda a: jnp.transpose(a, (2, 0, 1))), (lambda a: jnp.transpose(a, (1, 2, 0)))
    results[i_w_in] = [back(a) for a in adamw_w_in(to3(w_in), g_w_in_t, to3(m_w_in), to3(v_w_in))]
    grads, deltas, new_ms, new_vs = zip(*(results[i] for i in range(len(names))))
    return (loss, grad_x.reshape(x.shape), *grads, *deltas, *new_ms, *new_vs)
```

```python
import jax
import jax.numpy as jnp
import numpy as np
from jax import lax
from jax.experimental import pallas as pl
from jax.experimental.pallas import tpu as pltpu

F32 = jnp.float32
BF16 = jnp.bfloat16
MXU_DTYPE = BF16

D_MODEL = 1024
N_META = 16
CHUNK = 64
SUB = 16
ROW_TILE_BF16 = 16
HEADS = 4
DH = 128
WIDTH = HEADS * DH
QKV = 3 * WIDTH
CONV_TAPS = 4
HALO = 8
EPS = 1e-6
IN_COLS = 4 * WIDTH + 4 * WIDTH + 2 * HEADS
AB_PAD = 128
N_DEV = 8
LOCAL_CHUNKS = 4
SCAN_CHUNKS_FWD = 8
SCAN_CHUNKS = 4
VMEM_LIMIT = 56 * 1024 * 1024
VMEM_LIMIT_LARGE = 60 * 1024 * 1024

ADAM_LR = 0.001
ADAM_B1 = 0.9
ADAM_B2 = 0.999
ADAM_EPS = 1e-08
ADAM_WD = 0.01
ADAM_STEP = 10

VMEM_SPEC = pl.BlockSpec(memory_space=pltpu.VMEM)
MESH = pl.DeviceIdType.MESH


def _mm_tn(a, b):
    return lax.dot_general(a.astype(MXU_DTYPE), b.astype(MXU_DTYPE), (((0,), (0,)), ((), ())), preferred_element_type=F32)


def _bmm(a, b):
    return lax.dot_general(a.astype(MXU_DTYPE), b.astype(MXU_DTYPE), (((2,), (1,)), ((0,), (0,))), preferred_element_type=F32)


def _bmm_nt(a, b):
    return lax.dot_general(a.astype(MXU_DTYPE), b.astype(MXU_DTYPE), (((2,), (2,)), ((0,), (0,))), preferred_element_type=F32)


def _bmm_tn(a, b):
    return lax.dot_general(a.astype(MXU_DTYPE), b.astype(MXU_DTYPE), (((1,), (1,)), ((0,), (0,))), preferred_element_type=F32)


def _iota2(n, m):
    return lax.broadcasted_iota(jnp.int32, (n, m), 0), lax.broadcasted_iota(jnp.int32, (n, m), 1)


def _silu(x):
    return x * jax.nn.sigmoid(x)


def _gated_norm(o, z, nw):
    return o * lax.rsqrt(jnp.mean(o * o, axis=-1, keepdims=True) + EPS) * nw * _silu(z)


def _heads(a, nb):
    return jnp.stack([a[c * CHUNK:(c + 1) * CHUNK, h * DH:(h + 1) * DH] for c in range(nb) for h in range(HEADS)], axis=0)


def _unheads(a3, nb):
    return jnp.concatenate(
        [jnp.concatenate([a3[c * HEADS + h] for h in range(HEADS)], axis=1) for c in range(nb)], axis=0)


def _split3(x):
    hi = x.astype(BF16)
    r1 = x - hi.astype(F32)
    mid = r1.astype(BF16)
    return hi, mid, (r1 - mid.astype(F32)).astype(BF16)


def _summation_matrices(pattern, n_out):
    s = pattern(np.arange(n_out)[:, None], np.arange(CHUNK)[None, :]).astype(np.float32)
    return jnp.asarray(np.tile(s, (1, 3)), BF16), jnp.asarray(np.tile(s.T, (1, 2)), BF16)


def _select_rows(mats, chunks):
    width = chunks[0].shape[1]
    out = _summation(*mats, jnp.concatenate(chunks, axis=1))
    return [out[:, c * width:(c + 1) * width] for c in range(len(chunks))]


def _summation_impl(s, v):
    return jnp.dot(s, jnp.concatenate(_split3(v), axis=0), preferred_element_type=F32)


@jax.custom_vjp
def _summation(s, s_t, v):
    return _summation_impl(s, v)


def _summation_fwd(s, s_t, v):
    return _summation_impl(s, v), s_t


def _summation_bwd(s_t, d):
    hi = d.astype(BF16)
    return None, None, jnp.dot(s_t, jnp.concatenate([hi, (d - hi.astype(F32)).astype(BF16)], axis=0),
                               preferred_element_type=F32)


_summation.defvjp(_summation_fwd, _summation_bwd)


def _chunks(x, nb):
    return [x[c * CHUNK:(c + 1) * CHUNK] for c in range(nb)]


def _running_sum(i, j):
    return j <= i


HG_LEVELS = 6


def _hg_sums(i, j):
    lvl, t = i >> HG_LEVELS, i & (CHUNK - 1)
    last = t
    for l in range(1, HG_LEVELS + 1):
        width = HG_LEVELS + 1 - l
        last = np.where(lvl == l, ((t >> width) << width) + (CHUNK >> l) - 1, last)
    return j <= last


def hg_local(p, logits, sum_mats):
    nb = p.shape[0] // CHUNK
    l0, l1 = logits[0:1], logits[1:2]
    mx = jnp.maximum(l0, l1)
    e0, e1 = jnp.exp(l0 - mx), jnp.exp(l1 - mx)
    lb = e0 / (e0 + e1)
    q = _silu(p[:, 0:WIDTH])
    f = lb + (1.0 - lb) * jax.nn.sigmoid(p[:, WIDTH:2 * WIDTH])
    k = 1.0 - f
    logf = jnp.log(f)
    sums = _select_rows(sum_mats, _chunks(logf, nb))
    level = lambda l: _heads(jnp.concatenate([s[l * CHUNK:(l + 1) * CHUNK] for s in sums], axis=0), nb)
    q3, k3, v3, g3 = _heads(q, nb), _heads(k, nb), _heads(p[:, 2 * WIDTH:3 * WIDTH], nb), level(0)
    r, c = _iota2(CHUNK, CHUNK)
    row = lax.broadcasted_iota(jnp.int32, (CHUNK, DH), 0)
    a = jnp.where(r == c, _bmm_nt(q3, k3), 0.0)
    for l in range(1, HG_LEVELS + 1):
        sh = HG_LEVELS - l
        qk = jnp.where(((row >> sh) & 1) == 1, q3, k3) * jnp.exp(-jnp.abs(g3 - level(l)))
        pair = ((r >> (sh + 1)) == (c >> (sh + 1))) & (((r >> sh) & 1) == 1) & (((c >> sh) & 1) == 0)
        a = a + jnp.where(pair, _bmm_nt(qk, qk), 0.0)
    o = _bmm(a, v3)
    glast = g3[:, CHUNK - 1:CHUNK, :]
    egs = tuple(jnp.concatenate([jnp.exp(glast[c * HEADS + h]) for h in range(HEADS)], axis=1) for c in range(nb))
    return _unheads(q3 * jnp.exp(g3), nb), _unheads(k3 * jnp.exp(glast - g3), nb), _unheads(o, nb), egs


def hg_scan(q_in, k_out, v, eg, o_intra, z, nw, st):
    o = o_intra + _bmm_nt(q_in, st)
    return _gated_norm(o, z, nw), st * eg + _bmm_tn(v, k_out)


def _tri_y_impl(a):
    r, c = _iota2(CHUNK, CHUNK)
    same16 = (r // SUB) == (c // SUB)
    same32 = (r // (2 * SUB)) == (c // (2 * SUB))
    a0 = jnp.where(same16, a, 0.0)
    y = -a0
    pw = _bmm(a0, a0)
    for _ in range(2):
        y = y + pw + _bmm(y, pw)
        pw = _bmm(pw, pw)
    y = y + pw + _bmm(y, pw)
    for ak in (jnp.where(same32 & jnp.logical_not(same16), a, 0.0), jnp.where(same32, 0.0, a)):
        m = ak + _bmm(y, ak)
        y = y - (m + _bmm(m, y))
    return y


@jax.custom_vjp
def _tri_y(a):
    return _tri_y_impl(a)


def _tri_y_fwd(a):
    y = _tri_y_impl(a)
    return y, y


def _tri_y_bwd(y, dy):
    n = dy + _bmm_tn(y, dy)
    return (-(n + _bmm_nt(n, y)),)


_tri_y.defvjp(_tri_y_fwd, _tri_y_bwd)


def _saved_inverse(y):
    @jax.custom_vjp
    def inverse(a):
        return y

    inverse.defvjp(lambda a: (y, None), lambda _, dy: _tri_y_bwd(y, dy))
    return inverse


def _head_rows(a3, nb):
    return jnp.concatenate([a3[g] for g in range(nb * HEADS)], axis=0)


def _rows_down(x, s):
    rows = x.shape[0]

    @jax.custom_vjp
    def rotate(v):
        return pltpu.roll(v, s, 0)

    rotate.defvjp(lambda v: (pltpu.roll(v, s, 0), None), lambda _, d: (pltpu.roll(d, rows - s, 0),))
    return rotate(x)


def gd_local(xx, ab, cw, alog, dtb, sum_mats, inverse=_tri_y):
    n = ab.shape[0]
    nb = n // CHUNK
    conv = cw[CONV_TAPS - 1:CONV_TAPS] * xx[HALO:HALO + n]
    for j in range(CONV_TAPS - 1):
        conv = conv + cw[j:j + 1] * _rows_down(xx, CONV_TAPS - 1 - j)[HALO:HALO + n]
    act = _silu(conv)
    x = ab + dtb
    g_all = -jnp.exp(alog) * (jnp.maximum(x, 0.0) + jnp.log1p(jnp.exp(-jnp.abs(x))))
    beta_all = jax.nn.sigmoid(ab)
    gam_all = jnp.concatenate(_select_rows(sum_mats, _chunks(g_all, nb)), axis=0)
    q3, k3, v3 = _heads(act[:, 0:WIDTH], nb), _heads(act[:, WIDTH:2 * WIDTH], nb), _heads(act[:, 2 * WIDTH:QKV], nb)
    q3 = q3 * lax.rsqrt(jnp.sum(q3 * q3, axis=-1, keepdims=True) + EPS) * (DH ** -0.5)
    k3 = k3 * lax.rsqrt(jnp.sum(k3 * k3, axis=-1, keepdims=True) + EPS)
    pairs = [(c, h) for c in range(nb) for h in range(HEADS)]
    beta = jnp.stack([beta_all[c * CHUNK:(c + 1) * CHUNK, HEADS + h:HEADS + h + 1] for c, h in pairs], axis=0)
    gam = jnp.stack([gam_all[c * CHUNK:(c + 1) * CHUNK, h:h + 1] for c, h in pairs], axis=0)
    gam_t = [gam_all[c * CHUNK:(c + 1) * CHUNK].T for c in range(nb)]
    gam_row = jnp.stack([gam_t[c][h:h + 1, :] for c, h in pairs], axis=0)
    glast = gam[:, CHUNK - 1:CHUNK, :]
    r, c = _iota2(CHUNK, CHUNK)
    dec = jnp.exp(jnp.where(c < r, gam - gam_row, -jnp.inf))
    y = inverse(beta * _bmm_nt(k3, k3) * dec)
    eg = jnp.exp(gam)
    rhs = jnp.concatenate([beta * v3, (beta * eg) * k3], axis=2)
    sol = rhs + _bmm(y, rhs)
    qk = _bmm_nt(q3, k3) * jnp.where(r == c, 1.0, dec)
    eas = tuple(jnp.exp(gam_all[(c + 1) * CHUNK - 1:(c + 1) * CHUNK]) for c in range(nb))
    return (_unheads(sol[:, :, 0:DH], nb), _unheads(sol[:, :, DH:2 * DH], nb), _unheads(q3 * eg, nb),
            _unheads(k3 * jnp.exp(glast - gam), nb), _head_rows(qk, nb), eas), _head_rows(y, nb)


def gd_scan(uu, ww, qe, ke, qk, ea, z, nw, s):
    u = uu - _bmm(ww, s)
    o = _bmm(qe, s) + _bmm(qk, u)
    return _gated_norm(o, z, nw), ea * s + _bmm_tn(ke, u)


def _cparams(*sem):
    return pltpu.CompilerParams(dimension_semantics=sem, vmem_limit_bytes=VMEM_LIMIT)


def _row_tile(n):
    for t in (512, 256, 128, 64):
        if n % t == 0:
            return t
    raise ValueError(f"unsupported token count {n}")


def _w_in_specs():
    once = pl.Buffered(1)
    return [pl.BlockSpec((4 * WIDTH, D_MODEL), lambda *i: (0, 0), pipeline_mode=once),
            pl.BlockSpec((4 * WIDTH, D_MODEL), lambda *i: (1, 0), pipeline_mode=once),
            pl.BlockSpec((AB_PAD, D_MODEL), lambda *i: (8 * WIDTH // AB_PAD, 0), pipeline_mode=once)]


def in_proj(h, h0, norm_w, w_t):
    n = h.shape[0]
    tm = _row_tile(n)
    nt = (((1,), (1,)), ((), ()))

    def body(h_ref, h0_ref, nw_ref, whg_ref, wgd_ref, wab_ref, phg_ref, pgd_ref, pab_ref, phg0_ref, pgd0_ref, pab0_ref, u0_ref):
        def project(x, hg_ref, gd_ref, ab_ref):
            u = (x * lax.rsqrt(jnp.mean(x * x, axis=-1, keepdims=True) + EPS) * nw_ref[...]).astype(MXU_DTYPE)
            hg_ref[...] = lax.dot_general(u, whg_ref[...], nt, preferred_element_type=F32)
            gd_ref[...] = lax.dot_general(u, wgd_ref[...], nt, preferred_element_type=F32)
            ab_ref[...] = lax.dot_general(u, wab_ref[...], nt, preferred_element_type=F32)
            return u

        @pl.when(pl.program_id(0) == 0)
        def _():
            u0_ref[...] = project(h0_ref[...], phg0_ref, pgd0_ref, pab0_ref)

        project(h_ref[...], phg_ref, pgd_ref, pab_ref)

    n0 = h0.shape[0]
    row = lambda w: pl.BlockSpec((tm, w), lambda i: (i, 0))
    lead = lambda w: pl.BlockSpec((n0, w), lambda i: (0, 0))
    widths = [4 * WIDTH, 4 * WIDTH, AB_PAD]
    return pl.pallas_call(
        body, grid=(n // tm,), name="in_proj",
        in_specs=[row(D_MODEL), lead(D_MODEL), pl.BlockSpec(norm_w.shape, lambda i: (0, 0))] + _w_in_specs(),
        out_specs=[row(w) for w in widths] + [lead(w) for w in widths] + [lead(D_MODEL)],
        out_shape=[jax.ShapeDtypeStruct((n, w), F32) for w in widths] + [jax.ShapeDtypeStruct((n0, w), F32) for w in widths]
        + [jax.ShapeDtypeStruct((n0, D_MODEL), MXU_DTYPE)],
        compiler_params=_cparams("arbitrary"),
    )(h, h0, norm_w, w_t, w_t, w_t)


def out_proj_loss(x, tgt, y_hg, y_gd, w_out, fw):
    n = x.shape[0]
    tm = _row_tile(n)
    inv_d = 1.0 / D_MODEL

    def body(x_ref, t_ref, yh_ref, yg_ref, w_ref, fw_ref, dh_ref, dyh_ref, dyg_ref, dw_ref, loss_ref, dfw_ref):
        @pl.when(pl.program_id(0) == 0)
        def _():
            dw_ref[...] = jnp.zeros_like(dw_ref)
            loss_ref[...] = jnp.zeros_like(loss_ref)
            dfw_ref[...] = jnp.zeros_like(dfw_ref)

        yh, yg = yh_ref[...], yg_ref[...]
        wa, wb = w_ref[0:WIDTH, :], w_ref[WIDTH:2 * WIDTH, :]
        h2 = x_ref[...] + jnp.dot(yh, wa, preferred_element_type=F32) + jnp.dot(yg, wb, preferred_element_type=F32)
        r2 = lax.rsqrt(jnp.mean(h2 * h2, axis=-1, keepdims=True) + EPS)
        nrm = h2 * r2
        fwv = fw_ref[...]
        err = nrm * fwv - t_ref[...]
        loss_ref[...] += jnp.full(loss_ref.shape, 0.5 * inv_d * jnp.sum(err * err), F32)
        dout = err * inv_d
        dfw_ref[...] += jnp.sum(dout * nrm, axis=0, keepdims=True)
        dn = dout * fwv
        dh2 = r2 * (dn - nrm * jnp.mean(dn * nrm, axis=-1, keepdims=True))
        dh_ref[...] = dh2
        dhb = dh2.astype(MXU_DTYPE)
        dyh_ref[...] = lax.dot_general(dhb, wa, (((1,), (1,)), ((), ())), preferred_element_type=F32)
        dyg_ref[...] = lax.dot_general(dhb, wb, (((1,), (1,)), ((), ())), preferred_element_type=F32)
        dw_ref[0:WIDTH, :] += lax.dot_general(yh, dhb, (((0,), (0,)), ((), ())), preferred_element_type=F32)
        dw_ref[WIDTH:2 * WIDTH, :] += lax.dot_general(yg, dhb, (((0,), (0,)), ((), ())), preferred_element_type=F32)

    row = lambda w: pl.BlockSpec((tm, w), lambda i: (i, 0))
    full = lambda s: pl.BlockSpec(s, lambda i: (0, 0))
    return pl.pallas_call(
        body, grid=(n // tm,), name="out_proj_loss",
        in_specs=[row(D_MODEL), row(D_MODEL), row(WIDTH), row(WIDTH), full(w_out.shape), full(fw.shape)],
        out_specs=[row(D_MODEL), row(WIDTH), row(WIDTH), full((2 * WIDTH, D_MODEL)), full((8, 128)), full((1, D_MODEL))],
        out_shape=[jax.ShapeDtypeStruct((n, D_MODEL), F32), jax.ShapeDtypeStruct((n, WIDTH), F32),
                   jax.ShapeDtypeStruct((n, WIDTH), F32), jax.ShapeDtypeStruct((2 * WIDTH, D_MODEL), F32),
                   jax.ShapeDtypeStruct((8, 128), F32), jax.ShapeDtypeStruct((1, D_MODEL), F32)],
        compiler_params=_cparams("arbitrary"),
    )(x, tgt, y_hg, y_gd, w_out, fw)


def in_proj_bwd(dphg, dpgd, dpab, w_t, h, dh2, norm_w, h0, u0, dphg0, dpgd0, dpab0):
    n = h.shape[0]
    tm = _row_tile(n)
    steps = n // tm

    def body(dphg_ref, dpgd_ref, dpab_ref, whg_ref, wgd_ref, wab_ref, h_ref, dh2_ref, nw_ref, h0_ref, u0_ref, d0hg_ref,
             d0gd_ref, d0ab_ref, dx_ref, dx0_ref, dnw_ref, ghg_ref, ggd_ref, gab_ref, acc_hg, acc_gd, acc_ab):
        i = pl.program_id(0)
        nwv = nw_ref[...]

        def norm_bwd(dps, x):
            du = jnp.dot(dps[0], whg_ref[...], preferred_element_type=F32)
            du += jnp.dot(dps[1], wgd_ref[...], preferred_element_type=F32)
            du += jnp.dot(dps[2], wab_ref[...], preferred_element_type=F32)
            r = lax.rsqrt(jnp.mean(x * x, axis=-1, keepdims=True) + EPS)
            nrm = x * r
            dn = du * nwv
            return r * (dn - nrm * jnp.mean(dn * nrm, axis=-1, keepdims=True)), nrm, jnp.sum(du * nrm, axis=0, keepdims=True)

        def accumulate(dps, u, first):
            for acc, dp in zip((acc_hg, acc_gd, acc_ab), dps):
                step = min(acc.shape[0], 512)
                for lo in range(0, acc.shape[0], step):
                    part = _mm_tn(dp[:, lo:lo + step], u)
                    acc[lo:lo + step, :] = part if first else acc[lo:lo + step, :] + part

        @pl.when(i == 0)
        def _():
            dps0 = (d0hg_ref[...], d0gd_ref[...], d0ab_ref[...])
            dx0_ref[...], _, dnw_ref[...] = norm_bwd(dps0, h0_ref[...])
            accumulate(dps0, u0_ref[...], True)

        dps = (dphg_ref[...], dpgd_ref[...], dpab_ref[...])
        dx, nrm, dnw = norm_bwd(dps, h_ref[...])
        dx_ref[...] = dh2_ref[...] + dx
        dnw_ref[...] += dnw
        accumulate(dps, (nrm * nwv).astype(MXU_DTYPE), False)

        @pl.when(i == steps - 1)
        def _():
            pltpu.sync_copy(acc_hg, ghg_ref)
            pltpu.sync_copy(acc_gd, ggd_ref)
            pltpu.sync_copy(acc_ab, gab_ref)

    row = lambda w: pl.BlockSpec((tm, w), lambda i: (i, 0))
    full = lambda a: pl.BlockSpec(a.shape, lambda i: (0, 0), pipeline_mode=pl.Buffered(1))
    anywhere = pl.BlockSpec(memory_space=pl.ANY)
    return pl.pallas_call(
        body, grid=(steps,), name="in_proj_bwd",
        in_specs=[row(4 * WIDTH), row(4 * WIDTH), row(AB_PAD)] + _w_in_specs() + [row(D_MODEL), row(D_MODEL), full(norm_w),
                                                                                   full(h0), full(u0), full(dphg0), full(dpgd0),
                                                                                   full(dpab0)],
        out_specs=[row(D_MODEL), pl.BlockSpec(h0.shape, lambda i: (0, 0)), pl.BlockSpec((1, D_MODEL), lambda i: (0, 0)),
                   anywhere, anywhere, anywhere],
        out_shape=[jax.ShapeDtypeStruct((n, D_MODEL), F32), jax.ShapeDtypeStruct(h0.shape, F32),
                   jax.ShapeDtypeStruct((1, D_MODEL), F32), jax.ShapeDtypeStruct((4 * WIDTH, D_MODEL), F32),
                   jax.ShapeDtypeStruct((4 * WIDTH, D_MODEL), F32), jax.ShapeDtypeStruct((AB_PAD, D_MODEL), F32)],
        scratch_shapes=[pltpu.VMEM((4 * WIDTH, D_MODEL), F32), pltpu.VMEM((4 * WIDTH, D_MODEL), F32),
                        pltpu.VMEM((AB_PAD, D_MODEL), F32)],
        compiler_params=pltpu.CompilerParams(dimension_semantics=("arbitrary",), vmem_limit_bytes=VMEM_LIMIT_LARGE),
    )(dphg, dpgd, dpab, w_t, w_t, w_t, h, dh2, norm_w, h0, u0, dphg0, dpgd0, dpab0)


def _sds(shape, dtype=F32):
    return jax.ShapeDtypeStruct(shape, dtype)


def _pairs(b):
    return [(i, h) for i in range(b) for h in range(HEADS)]


def _load_slabs(ref, b, k):
    return jnp.stack([ref[i, k * CHUNK:(k + 1) * CHUNK, h * DH:(h + 1) * DH].astype(F32) for i, h in _pairs(b)], axis=0)


def _lead_slabs(a, b):
    return jnp.stack([a[:, h * DH:(h + 1) * DH].astype(F32) for _, h in _pairs(b)], axis=0)


def _rows(a3, i):
    return jnp.concatenate([a3[i * HEADS + h] for h in range(HEADS)], axis=1)


def _store_slabs(ref, a3, b, k):
    for i in range(b):
        ref[i, k * CHUNK:(k + 1) * CHUNK, :] = _rows(a3, i).astype(ref.dtype)


def _sum_rows(a3, b):
    out = _rows(a3, 0)
    for i in range(1, b):
        out = out + _rows(a3, i)
    return out


def _save_states(ref, s, b, k):
    for i in range(b):
        ref[i, k] = jnp.concatenate([s[i * HEADS + h] for h in range(HEADS)], axis=0).astype(ref.dtype)


def _load_states(ref, b, k):
    return jnp.stack([ref[i, k, h * DH:(h + 1) * DH, :].astype(F32) for i, h in _pairs(b)], axis=0)


def hg_local_fwd(p, p0, logits):
    b, seq, _ = p.shape
    rows = LOCAL_CHUNKS * CHUNK
    nreal = seq // CHUNK

    def body(p_ref, p0_ref, lg_ref, s_ref, st_ref, q_ref, k_ref, o_ref, eg_ref, q0_ref, k0_ref, o0_ref, eg0_ref):
        sum_mats = (s_ref[...], st_ref[...])

        @pl.when((pl.program_id(0) == 0) & (pl.program_id(1) == 0))
        def _():
            q_in, k_out, o0_ref[...], (eg0_ref[...],) = hg_local(p0_ref[...], lg_ref[...], sum_mats)
            q0_ref[...], k0_ref[...] = q_in.astype(MXU_DTYPE), k_out.astype(MXU_DTYPE)

        q_in, k_out, o_intra, egs = hg_local(p_ref[...], lg_ref[...], sum_mats)
        q_ref[...], k_ref[...], o_ref[...] = q_in.astype(MXU_DTYPE), k_out.astype(MXU_DTYPE), o_intra
        for c in range(LOCAL_CHUNKS):
            eg_ref[c] = egs[c]

    slab = pl.BlockSpec((None, rows, WIDTH), lambda s, g: (s, g, 0))
    const = lambda shape: pl.BlockSpec(shape, lambda s, g: (0, 0))
    lead_shapes = [(CHUNK, WIDTH)] * 3 + [(1, WIDTH)]
    sum_mats = _summation_matrices(_hg_sums, (HG_LEVELS + 1) * CHUNK)
    out = pl.pallas_call(
        body, grid=(b, seq // rows), name="hgrn2_local",
        in_specs=[pl.BlockSpec((None, rows, 4 * WIDTH), lambda s, g: (s, g, 0)), const(p0.shape), const(logits.shape)]
        + [const(a.shape) for a in sum_mats],
        out_specs=[slab, slab, slab, pl.BlockSpec((None, LOCAL_CHUNKS, 1, WIDTH), lambda s, g: (s, g, 0, 0))]
        + [const(s) for s in lead_shapes],
        out_shape=[_sds((b, seq, WIDTH), MXU_DTYPE)] * 2 + [_sds((b, seq, WIDTH)), _sds((b, nreal, 1, WIDTH))]
        + [_sds(lead_shapes[0], MXU_DTYPE)] * 2 + [_sds(lead_shapes[2]), _sds(lead_shapes[3])],
        compiler_params=_cparams("arbitrary", "arbitrary"),
    )(p, p0, logits, *sum_mats)
    return out[0:4], out[4:8]


def _hg_scan_args(b, k, q_ref, k_ref, o_ref, v_ref, z_ref, eg_ref):
    eg = jnp.stack([eg_ref[i, k, :, h * DH:(h + 1) * DH] for i, h in _pairs(b)], axis=0)
    return (_load_slabs(q_ref, b, k), _load_slabs(k_ref, b, k), _load_slabs(v_ref, b, k), eg, _load_slabs(o_ref, b, k),
            _load_slabs(z_ref, b, k))


def _hg_lead_args(b, q0_ref, k0_ref, o0_ref, p0_ref, eg0_ref):
    eg = jnp.stack([eg0_ref[:, h * DH:(h + 1) * DH] for _, h in _pairs(b)], axis=0)
    return (_lead_slabs(q0_ref[...], b), _lead_slabs(k0_ref[...], b), _lead_slabs(p0_ref[:, 2 * WIDTH:3 * WIDTH], b), eg,
            _lead_slabs(o0_ref[...], b), _lead_slabs(p0_ref[:, 3 * WIDTH:4 * WIDTH], b))


def _scan_specs(b, ng, reverse, chunks):
    group = (lambda i: ng - 1 - i) if reverse else (lambda i: i)
    slab = lambda lane_block: pl.BlockSpec((b, chunks * CHUNK, WIDTH), lambda i: (0, group(i), lane_block))
    per_chunk = lambda *tail: pl.BlockSpec((b, chunks) + tail, lambda i: (0, group(i)) + (0,) * len(tail))
    const = lambda a: pl.BlockSpec(a.shape, lambda i: (0,) * a.ndim)
    return slab, per_chunk, const


def run_scans(parts, nc, name):
    n_in = [len(p["args"]) for p in parts]
    n_out = [len(p["out_shape"]) for p in parts]
    n_scr = [len(p["scratch_shapes"]) for p in parts]

    def body(*refs):
        ins, outs, scr = refs[:sum(n_in)], refs[sum(n_in):sum(n_in) + sum(n_out)], refs[sum(n_in) + sum(n_out):]
        for i, part in enumerate(parts):
            part["body"](*ins[sum(n_in[:i]):sum(n_in[:i + 1])], *outs[sum(n_out[:i]):sum(n_out[:i + 1])],
                         *scr[sum(n_scr[:i]):sum(n_scr[:i + 1])])

    flat = lambda key: [v for p in parts for v in p[key]]
    out = pl.pallas_call(body, grid=(nc,), name=name, in_specs=flat("in_specs"), out_specs=flat("out_specs"),
                         out_shape=flat("out_shape"), scratch_shapes=flat("scratch_shapes"),
                         compiler_params=_cparams("arbitrary"))(*flat("args"))
    return [out[sum(n_out[:i]):sum(n_out[:i + 1])] for i in range(len(parts))]


def hg_scan_fwd(p, p0, local, lead, nw):
    b, seq, _ = p.shape
    q_in, k_out, o_intra, eg = local
    slab, per_chunk, const = _scan_specs(b, seq // (SCAN_CHUNKS_FWD * CHUNK), False, SCAN_CHUNKS_FWD)

    def body(q_ref, k_ref, o_ref, v_ref, z_ref, eg_ref, q0_ref, k0_ref, o0_ref, p0_ref, eg0_ref, nw_ref, y_ref, ss_ref, st):
        @pl.when(pl.program_id(0) == 0)
        def _():
            st[...] = hg_scan(*_hg_lead_args(b, q0_ref, k0_ref, o0_ref, p0_ref, eg0_ref), nw_ref[...], jnp.zeros(st.shape, F32))[1]

        s = st[...]
        for k in range(SCAN_CHUNKS_FWD):
            _save_states(ss_ref, s, b, k)
            y, s = hg_scan(*_hg_scan_args(b, k, q_ref, k_ref, o_ref, v_ref, z_ref, eg_ref), nw_ref[...], s)
            _store_slabs(y_ref, y, b, k)
        st[...] = s

    return dict(
        body=body, args=(q_in, k_out, o_intra, p, p, eg, lead[0], lead[1], lead[2], p0, lead[3], nw),
        in_specs=[slab(0), slab(0), slab(0), slab(2), slab(3), per_chunk(1, WIDTH)] + [const(a) for a in lead[0:3]]
        + [const(p0), const(lead[3]), const(nw)],
        out_specs=[slab(0), per_chunk(WIDTH, DH)],
        out_shape=[_sds((b, seq, WIDTH), MXU_DTYPE), _sds((b, seq // CHUNK, WIDTH, DH), MXU_DTYPE)],
        scratch_shapes=[pltpu.VMEM((b * HEADS, DH, DH), F32)])


def hg_scan_bwd(p, p0, local, lead, nw, ssave, dy):
    b, seq, _ = p.shape
    ng = seq // (SCAN_CHUNKS * CHUNK)
    q_in, k_out, o_intra, eg = local
    slab, per_chunk, const = _scan_specs(b, ng, True, SCAN_CHUNKS)

    def body(q_ref, k_ref, o_ref, v_ref, z_ref, eg_ref, q0_ref, k0_ref, o0_ref, p0_ref, eg0_ref, nw_ref, ss_ref, dy_ref,
             dq_ref, dk_ref, do_ref, dv_ref, dz_ref, deg_ref, dq0_ref, dk0_ref, do0_ref, dv0_ref, dz0_ref, deg0_ref, dnw_ref,
             dst):
        i = pl.program_id(0)

        @pl.when(i == 0)
        def _():
            dst[...] = jnp.zeros_like(dst)
            dnw_ref[...] = jnp.zeros_like(dnw_ref)

        ds = dst[...]
        for k in reversed(range(SCAN_CHUNKS)):
            args = _hg_scan_args(b, k, q_ref, k_ref, o_ref, v_ref, z_ref, eg_ref)
            _, vjp = jax.vjp(hg_scan, *args, nw_ref[...], _load_states(ss_ref, b, k))
            dq, dk, dv, deg, do, dz, dnw, ds = vjp((_load_slabs(dy_ref, b, k), ds))
            dnw_ref[...] += dnw
            for ref, val in ((dq_ref, dq), (dk_ref, dk), (do_ref, do), (dv_ref, dv), (dz_ref, dz)):
                _store_slabs(ref, val, b, k)
            for j in range(b):
                deg_ref[j, k] = _rows(deg, j)
        dst[...] = ds

        @pl.when(i == ng - 1)
        def _():
            args = _hg_lead_args(b, q0_ref, k0_ref, o0_ref, p0_ref, eg0_ref)
            _, vjp = jax.vjp(hg_scan, *args, nw_ref[...], jnp.zeros(dst.shape, F32))
            dq, dk, dv, deg, do, dz, dnw, _ = vjp((jnp.zeros((b * HEADS, CHUNK, DH), F32), ds))
            dnw_ref[...] += dnw
            for ref, val in ((dq0_ref, dq), (dk0_ref, dk), (do0_ref, do), (dv0_ref, dv), (dz0_ref, dz), (deg0_ref, deg)):
                ref[...] = _sum_rows(val, b)

    lead_out = [const(a) for a in lead[0:3]] + [const(lead[0]), const(lead[0]), const(lead[3])]
    return dict(
        body=body, args=(q_in, k_out, o_intra, p, p, eg, lead[0], lead[1], lead[2], p0, lead[3], nw, ssave, dy),
        in_specs=[slab(0), slab(0), slab(0), slab(2), slab(3), per_chunk(1, WIDTH)] + [const(a) for a in lead[0:3]]
        + [const(p0), const(lead[3]), const(nw), per_chunk(WIDTH, DH), slab(0)],
        out_specs=[slab(0)] * 5 + [per_chunk(1, WIDTH)] + lead_out + [const(nw)],
        out_shape=[_sds((b, seq, WIDTH))] * 3 + [_sds((b, seq, WIDTH), MXU_DTYPE)] * 2 + [_sds(eg.shape)]
        + [_sds((CHUNK, WIDTH))] * 5 + [_sds((1, WIDTH)), _sds(nw.shape)],
        scratch_shapes=[pltpu.VMEM((b * HEADS, DH, DH), F32)])


def _hg_local_vjp(sum_mats, p, logits, dq, dk, do, degs, dv, dz):
    _, vjp = jax.vjp(lambda p_, logits_: hg_local(p_, logits_, sum_mats), p, logits)
    dp, dlg = vjp((dq, dk, do, degs))
    return dp + jnp.concatenate([jnp.zeros((p.shape[0], 2 * WIDTH), F32), dv.astype(F32), dz.astype(F32)], axis=1), dlg


def hg_local_bwd(p, p0, logits, cot, cot0):
    b, seq, _ = p.shape
    rows = LOCAL_CHUNKS * CHUNK

    def body(p_ref, p0_ref, lg_ref, s_ref, st_ref, dq_ref, dk_ref, do_ref, dv_ref, dz_ref, deg_ref, dq0_ref, dk0_ref, do0_ref,
             dv0_ref, dz0_ref, deg0_ref, dp_ref, dp0_ref, dlg_ref):
        sum_mats = (s_ref[...], st_ref[...])

        @pl.when((pl.program_id(0) == 0) & (pl.program_id(1) == 0))
        def _():
            dp0, dlg_ref[...] = _hg_local_vjp(sum_mats, p0_ref[...], lg_ref[...], dq0_ref[...], dk0_ref[...], do0_ref[...],
                                              (deg0_ref[...],), dv0_ref[...], dz0_ref[...])
            dp0_ref[...] = dp0.astype(MXU_DTYPE)

        degs = tuple(deg_ref[c] for c in range(LOCAL_CHUNKS))
        dp, dlg = _hg_local_vjp(sum_mats, p_ref[...], lg_ref[...], dq_ref[...], dk_ref[...], do_ref[...], degs, dv_ref[...],
                                dz_ref[...])
        dp_ref[...] = dp.astype(MXU_DTYPE)
        dlg_ref[...] += dlg

    slab = pl.BlockSpec((None, rows, WIDTH), lambda s, g: (s, g, 0))
    wide = pl.BlockSpec((None, rows, 4 * WIDTH), lambda s, g: (s, g, 0))
    const = lambda a: pl.BlockSpec(a.shape, lambda s, g: (0, 0))
    sum_mats = _summation_matrices(_hg_sums, (HG_LEVELS + 1) * CHUNK)
    return pl.pallas_call(
        body, grid=(b, seq // rows), name="hgrn2_local_bwd",
        in_specs=[wide, const(p0), const(logits), const(sum_mats[0]), const(sum_mats[1]), slab, slab, slab, slab, slab,
                  pl.BlockSpec((None, LOCAL_CHUNKS, 1, WIDTH), lambda s, g: (s, g, 0, 0))] + [const(a) for a in cot0],
        out_specs=[wide, const(p0), const(logits)],
        out_shape=[_sds(p.shape, MXU_DTYPE), _sds(p0.shape, MXU_DTYPE), _sds(logits.shape)],
        compiler_params=_cparams("arbitrary", "arbitrary"),
    )(p, p0, logits, *sum_mats, *cot, *cot0)


def _halo_block(g):
    return jnp.maximum((LOCAL_CHUNKS * CHUNK // HALO) * g - 1, 0)


def _gd_window(g, p_ref, halo_ref, p0_ref):
    halo = jnp.where(g == 0, p0_ref[CHUNK - HALO:CHUNK, 0:QKV], halo_ref[...])
    return jnp.concatenate([halo, p_ref[:, 0:QKV]], axis=0)


def _lead_window(p0_ref):
    return jnp.concatenate([jnp.zeros((HALO, QKV), F32), p0_ref[:, 0:QKV]], axis=0)


def gd_local_fwd(p, p0, ab, ab0, cw, alog, dtb):
    b, seq, _ = p.shape
    rows = LOCAL_CHUNKS * CHUNK
    nreal = seq // CHUNK

    def body(p_ref, halo_ref, p0_ref, ab_ref, ab0_ref, cw_ref, al_ref, dt_ref, s_ref, st_ref, u_ref, w_ref, qe_ref, ke_ref,
             qk_ref, ea_ref, inv_ref, u0_ref, w0_ref, qe0_ref, ke0_ref, qk0_ref, ea0_ref, inv0_ref):
        sum_mats = (s_ref[...], st_ref[...])

        @pl.when((pl.program_id(0) == 0) & (pl.program_id(1) == 0))
        def _():
            (u0_ref[...], ww, qe, ke, qk0_ref[...], (ea0_ref[...],)), inv0_ref[...] = gd_local(
                _lead_window(p0_ref), ab0_ref[...], cw_ref[...], al_ref[...], dt_ref[...], sum_mats, inverse=_tri_y_impl)
            w0_ref[...], qe0_ref[...], ke0_ref[...] = ww.astype(MXU_DTYPE), qe.astype(MXU_DTYPE), ke.astype(MXU_DTYPE)

        (uu, ww, qe, ke, qk, eas), inv = gd_local(_gd_window(pl.program_id(1), p_ref, halo_ref, p0_ref), ab_ref[...],
                                                  cw_ref[...], al_ref[...], dt_ref[...], sum_mats, inverse=_tri_y_impl)
        u_ref[...], w_ref[...], qe_ref[...], ke_ref[...] = uu, ww.astype(MXU_DTYPE), qe.astype(MXU_DTYPE), ke.astype(MXU_DTYPE)
        for c in range(LOCAL_CHUNKS):
            qk_ref[c] = qk[c * HEADS * CHUNK:(c + 1) * HEADS * CHUNK]
            inv_ref[c] = inv[c * HEADS * CHUNK:(c + 1) * HEADS * CHUNK]
            ea_ref[c] = eas[c]

    const = lambda shape: pl.BlockSpec(shape, lambda s, g: (0, 0))
    slab = pl.BlockSpec((None, rows, WIDTH), lambda s, g: (s, g, 0))
    mats = pl.BlockSpec((None, LOCAL_CHUNKS, HEADS * CHUNK, CHUNK), lambda s, g: (s, g, 0, 0))
    lead_out = [_sds((CHUNK, WIDTH))] + [_sds((CHUNK, WIDTH), MXU_DTYPE)] * 3 + [_sds((HEADS * CHUNK, CHUNK)), _sds((1, AB_PAD)),
                                                                                _sds((HEADS * CHUNK, CHUNK))]
    sum_mats = _summation_matrices(_running_sum, CHUNK)
    out = pl.pallas_call(
        body, grid=(b, seq // rows), name="gdn_local",
        in_specs=[pl.BlockSpec((None, rows, 4 * WIDTH), lambda s, g: (s, g, 0)),
                  pl.BlockSpec((None, HALO, QKV), lambda s, g: (s, _halo_block(g), 0)), const(p0.shape),
                  pl.BlockSpec((None, rows, AB_PAD), lambda s, g: (s, g, 0)), const(ab0.shape), const(cw.shape),
                  const(alog.shape), const(dtb.shape), const(sum_mats[0].shape), const(sum_mats[1].shape)],
        out_specs=[slab] * 4 + [mats, pl.BlockSpec((None, LOCAL_CHUNKS, 1, AB_PAD), lambda s, g: (s, g, 0, 0)), mats]
        + [const(s.shape) for s in lead_out],
        out_shape=[_sds((b, seq, WIDTH))] + [_sds((b, seq, WIDTH), MXU_DTYPE)] * 3
        + [_sds((b, nreal, HEADS * CHUNK, CHUNK)), _sds((b, nreal, 1, AB_PAD)), _sds((b, nreal, HEADS * CHUNK, CHUNK))] + lead_out,
        compiler_params=_cparams("arbitrary", "arbitrary"),
    )(p, p, p0, ab, ab0, cw, alog, dtb, *sum_mats)
    return out[0:6], out[6], out[7:13], out[13]


def _gd_scan_args(b, k, u_ref, w_ref, qe_ref, ke_ref, qk_ref, ea_ref, z_ref):
    qk = jnp.stack([qk_ref[i, k, h * CHUNK:(h + 1) * CHUNK, :] for i, h in _pairs(b)], axis=0)
    ea = jnp.stack([ea_ref[i, k, :, h:h + 1] for i, h in _pairs(b)], axis=0)
    return (_load_slabs(u_ref, b, k), _load_slabs(w_ref, b, k), _load_slabs(qe_ref, b, k), _load_slabs(ke_ref, b, k), qk, ea,
            _load_slabs(z_ref, b, k))


def _gd_lead_args(b, u0_ref, w0_ref, qe0_ref, ke0_ref, qk0_ref, ea0_ref, p0_ref):
    qk = jnp.stack([qk0_ref[h * CHUNK:(h + 1) * CHUNK, :] for _, h in _pairs(b)], axis=0)
    ea = jnp.stack([ea0_ref[:, h:h + 1] for _, h in _pairs(b)], axis=0)
    return (_lead_slabs(u0_ref[...], b), _lead_slabs(w0_ref[...], b), _lead_slabs(qe0_ref[...], b), _lead_slabs(ke0_ref[...], b),
            qk, ea, _lead_slabs(p0_ref[:, QKV:QKV + WIDTH], b))


def gd_scan_fwd(p, p0, local, lead, nw):
    b, seq, _ = p.shape
    slab, per_chunk, const = _scan_specs(b, seq // (SCAN_CHUNKS_FWD * CHUNK), False, SCAN_CHUNKS_FWD)

    def body(u_ref, w_ref, qe_ref, ke_ref, qk_ref, ea_ref, z_ref, u0_ref, w0_ref, qe0_ref, ke0_ref, qk0_ref, ea0_ref, p0_ref,
             nw_ref, y_ref, ss_ref, st):
        @pl.when(pl.program_id(0) == 0)
        def _():
            lead_args = _gd_lead_args(b, u0_ref, w0_ref, qe0_ref, ke0_ref, qk0_ref, ea0_ref, p0_ref)
            st[...] = gd_scan(*lead_args, nw_ref[...], jnp.zeros(st.shape, F32))[1]

        s = st[...]
        for k in range(SCAN_CHUNKS_FWD):
            _save_states(ss_ref, s, b, k)
            y, s = gd_scan(*_gd_scan_args(b, k, u_ref, w_ref, qe_ref, ke_ref, qk_ref, ea_ref, z_ref), nw_ref[...], s)
            _store_slabs(y_ref, y, b, k)
        st[...] = s

    return dict(
        body=body, args=(*local, p, *lead, p0, nw),
        in_specs=[slab(0)] * 4 + [per_chunk(HEADS * CHUNK, CHUNK), per_chunk(1, AB_PAD), slab(3)] + [const(a) for a in lead]
        + [const(p0), const(nw)],
        out_specs=[slab(0), per_chunk(WIDTH, DH)],
        out_shape=[_sds((b, seq, WIDTH), MXU_DTYPE), _sds((b, seq // CHUNK, WIDTH, DH), MXU_DTYPE)],
        scratch_shapes=[pltpu.VMEM((b * HEADS, DH, DH), F32)])


def gd_scan_bwd(p, p0, local, lead, nw, ssave, dy):
    b, seq, _ = p.shape
    ng = seq // (SCAN_CHUNKS * CHUNK)
    slab, per_chunk, const = _scan_specs(b, ng, True, SCAN_CHUNKS)

    def body(u_ref, w_ref, qe_ref, ke_ref, qk_ref, ea_ref, z_ref, u0_ref, w0_ref, qe0_ref, ke0_ref, qk0_ref, ea0_ref, p0_ref,
             nw_ref, ss_ref, dy_ref, du_ref, dw_ref, dqe_ref, dke_ref, dqk_ref, dea_ref, dz_ref, du0_ref, dw0_ref, dqe0_ref,
             dke0_ref, dqk0_ref, dea0_ref, dz0_ref, dnw_ref, dst):
        i = pl.program_id(0)
        lane = lax.broadcasted_iota(jnp.int32, (1, AB_PAD), 1)

        def gate_rows(dea, j):
            return sum(jnp.where(lane == h, dea[j * HEADS + h], 0.0) for h in range(HEADS))

        def matrix_rows(dqk, j):
            return jnp.concatenate([dqk[j * HEADS + h] for h in range(HEADS)], axis=0)

        @pl.when(i == 0)
        def _():
            dst[...] = jnp.zeros_like(dst)
            dnw_ref[...] = jnp.zeros_like(dnw_ref)

        ds = dst[...]
        for k in reversed(range(SCAN_CHUNKS)):
            args = _gd_scan_args(b, k, u_ref, w_ref, qe_ref, ke_ref, qk_ref, ea_ref, z_ref)
            _, vjp = jax.vjp(gd_scan, *args, nw_ref[...], _load_states(ss_ref, b, k))
            du, dw, dqe, dke, dqk, dea, dz, dnw, ds = vjp((_load_slabs(dy_ref, b, k), ds))
            dnw_ref[...] += dnw
            for ref, val in ((du_ref, du), (dw_ref, dw), (dqe_ref, dqe), (dke_ref, dke), (dz_ref, dz)):
                _store_slabs(ref, val, b, k)
            for j in range(b):
                dqk_ref[j, k] = matrix_rows(dqk, j)
                dea_ref[j, k] = gate_rows(dea, j)
        dst[...] = ds

        @pl.when(i == ng - 1)
        def _():
            args = _gd_lead_args(b, u0_ref, w0_ref, qe0_ref, ke0_ref, qk0_ref, ea0_ref, p0_ref)
            _, vjp = jax.vjp(gd_scan, *args, nw_ref[...], jnp.zeros(dst.shape, F32))
            du, dw, dqe, dke, dqk, dea, dz, dnw, _ = vjp((jnp.zeros((b * HEADS, CHUNK, DH), F32), ds))
            dnw_ref[...] += dnw
            for ref, val in ((du0_ref, du), (dw0_ref, dw), (dqe0_ref, dqe), (dke0_ref, dke), (dz0_ref, dz)):
                ref[...] = _sum_rows(val, b)
            dqk0_ref[...] = sum((matrix_rows(dqk, j) for j in range(1, b)), matrix_rows(dqk, 0))
            dea0_ref[...] = sum((gate_rows(dea, j) for j in range(1, b)), gate_rows(dea, 0))

    uu, ww, qe, ke, qk, ea = local
    return dict(
        body=body, args=(*local, p, *lead, p0, nw, ssave, dy),
        in_specs=[slab(0)] * 4 + [per_chunk(HEADS * CHUNK, CHUNK), per_chunk(1, AB_PAD), slab(3)] + [const(a) for a in lead]
        + [const(p0), const(nw), per_chunk(WIDTH, DH), slab(0)],
        out_specs=[slab(0)] * 4 + [per_chunk(HEADS * CHUNK, CHUNK), per_chunk(1, AB_PAD), slab(0)] + [const(a) for a in lead]
        + [const(lead[0]), const(nw)],
        out_shape=[_sds((b, seq, WIDTH))] * 4 + [_sds(qk.shape), _sds(ea.shape), _sds((b, seq, WIDTH), MXU_DTYPE)]
        + [_sds(a.shape) for a in lead] + [_sds(lead[0].shape), _sds(nw.shape)],
        scratch_shapes=[pltpu.VMEM((b * HEADS, DH, DH), F32)])


def _gd_local_vjp(sum_mats, inv_rows, xx, ab, cw, alog, dtb):
    nb = ab.shape[0] // CHUNK
    inv = jnp.stack([inv_rows[g * CHUNK:(g + 1) * CHUNK] for g in range(nb * HEADS)], axis=0)
    _, vjp, _ = jax.vjp(lambda *a: gd_local(*a, sum_mats, inverse=_saved_inverse(inv)), xx, ab, cw, alog, dtb, has_aux=True)
    return vjp


def gd_local_bwd(p, p0, ab, ab0, cw, alog, dtb, inv, inv0, cot, dz, cot0, dz0):
    b, seq, _ = p.shape
    rows = LOCAL_CHUNKS * CHUNK
    ng = seq // rows
    du, dw, dqe, dke, dqk, dea = cot

    def body(p_ref, halo_ref, p0_ref, ab_ref, ab0_ref, cw_ref, al_ref, dt_ref, s_ref, st_ref, inv_ref, inv0_ref, du_ref, dw_ref,
             dqe_ref, dke_ref, dqk_ref, dea_ref, dz_ref, du0_ref, dw0_ref, dqe0_ref, dke0_ref, dqk0_ref, dea0_ref, dz0_ref,
             dp_ref, dab_ref, dp0_ref, dab0_ref, dcw_ref, dal_ref, ddt_ref, dhalo, dtail):
        s, i = pl.program_id(0), pl.program_id(1)
        g = ng - 1 - i
        sum_mats = (s_ref[...], st_ref[...])

        @pl.when(i == 0)
        def _():
            dhalo[...] = jnp.zeros_like(dhalo)

        @pl.when((s == 0) & (i == 0))
        def _():
            dtail[...] = jnp.zeros_like(dtail)
            dcw_ref[...] = jnp.zeros_like(dcw_ref)
            dal_ref[...] = jnp.zeros_like(dal_ref)
            ddt_ref[...] = jnp.zeros_like(ddt_ref)

        def finish(dxx, dab, dcw, dal, ddt, before, n, dz_val, dp_out, dab_out):
            dqkv = dxx[HALO:HALO + n] + jnp.concatenate([jnp.zeros((n - HALO, QKV), F32), before], axis=0)
            dp_out[...] = jnp.concatenate([dqkv.astype(MXU_DTYPE), dz_val.astype(MXU_DTYPE)], axis=1)
            dab_out[...] = dab.astype(MXU_DTYPE)
            dcw_ref[...] += dcw
            dal_ref[...] += dal
            ddt_ref[...] += ddt

        inv_rows = jnp.concatenate([inv_ref[c] for c in range(LOCAL_CHUNKS)], axis=0)
        vjp = _gd_local_vjp(sum_mats, inv_rows, _gd_window(g, p_ref, halo_ref, p0_ref), ab_ref[...], cw_ref[...], al_ref[...],
                            dt_ref[...])
        dqk_all = jnp.concatenate([dqk_ref[c] for c in range(LOCAL_CHUNKS)], axis=0)
        deas = tuple(dea_ref[c] for c in range(LOCAL_CHUNKS))
        grads = vjp((du_ref[...], dw_ref[...], dqe_ref[...], dke_ref[...], dqk_all, deas))
        finish(*grads, dhalo[...], rows, dz_ref[...], dp_ref, dab_ref)
        dhalo[...] = grads[0][0:HALO]

        @pl.when(g == 0)
        def _():
            dtail[...] += grads[0][0:HALO]

        @pl.when((s == b - 1) & (g == 0))
        def _():
            vjp0 = _gd_local_vjp(sum_mats, inv0_ref[...], _lead_window(p0_ref), ab0_ref[...], cw_ref[...], al_ref[...],
                                 dt_ref[...])
            grads0 = vjp0((du0_ref[...], dw0_ref[...], dqe0_ref[...], dke0_ref[...], dqk0_ref[...], (dea0_ref[...],)))
            finish(*grads0, dtail[...], CHUNK, dz0_ref[...], dp0_ref, dab0_ref)

    rg = lambda i: ng - 1 - i
    const = lambda a: pl.BlockSpec(a.shape, lambda s, i: (0, 0))
    slab = pl.BlockSpec((None, rows, WIDTH), lambda s, i: (s, rg(i), 0))
    wide = pl.BlockSpec((None, rows, 4 * WIDTH), lambda s, i: (s, rg(i), 0))
    gates = pl.BlockSpec((None, rows, AB_PAD), lambda s, i: (s, rg(i), 0))
    mats = pl.BlockSpec((None, LOCAL_CHUNKS, HEADS * CHUNK, CHUNK), lambda s, i: (s, rg(i), 0, 0))
    sum_mats = _summation_matrices(_running_sum, CHUNK)
    return pl.pallas_call(
        body, grid=(b, ng), name="gdn_local_bwd",
        in_specs=[wide, pl.BlockSpec((None, HALO, QKV), lambda s, i: (s, _halo_block(rg(i)), 0)), const(p0), gates, const(ab0),
                  const(cw), const(alog), const(dtb), const(sum_mats[0]), const(sum_mats[1]), mats, const(inv0), slab, slab,
                  slab, slab, mats,
                  pl.BlockSpec((None, LOCAL_CHUNKS, 1, AB_PAD), lambda s, i: (s, rg(i), 0, 0)), slab]
        + [const(a) for a in cot0] + [const(dz0)],
        out_specs=[wide, gates, const(p0), const(ab0), const(cw), const(alog), const(dtb)],
        out_shape=[_sds(p.shape, MXU_DTYPE), _sds(ab.shape, MXU_DTYPE), _sds(p0.shape, MXU_DTYPE), _sds(ab0.shape, MXU_DTYPE),
                   _sds(cw.shape), _sds(alog.shape), _sds(dtb.shape)],
        scratch_shapes=[pltpu.VMEM((HALO, QKV), F32), pltpu.VMEM((HALO, QKV), F32)],
        compiler_params=_cparams("arbitrary", "arbitrary"),
    )(p, p, p0, ab, ab0, cw, alog, dtb, *sum_mats, inv, inv0, du, dw, dqe, dke, dqk, dea, dz, *cot0, dz0)


def _position():
    return lax.axis_index("x"), lax.axis_index("y"), lax.axis_index("c")


EXCHANGE_COPIES = 10


def _exchange_blocks(bufs, send_sems, recv_sems):
    x, y, c = _position()
    here, x_nbr, y_nbr, diag = (x, y), (1 - x, y), (x, 1 - y), (1 - x, 1 - y)
    sibling = (x, y, 1 - c)
    me = (x, y, c)
    n = range(len(bufs))

    def rows(a, chip, core, half=None):
        block = bufs[a].at[4 * chip[0] + 2 * chip[1] + core]
        if half is None:
            return block
        total = bufs[a].shape[1]
        tile = 8 * (4 // jnp.dtype(bufs[a].dtype).itemsize)
        split = total // 2 // tile * tile
        return block.at[pl.ds(0, split)] if half == 0 else block.at[pl.ds(split, total - split)]

    def copy(a, k, region, to):
        return pltpu.make_async_remote_copy(src_ref=region, dst_ref=region, send_sem=send_sems.at[a * EXCHANGE_COPIES + k],
                                            recv_sem=recv_sems.at[a * EXCHANGE_COPIES + k], device_id=to, device_id_type=MESH)

    sent = [copy(a, 0, rows(a, here, c), sibling) for a in n]
    sent += [cp for a in n for cp in (copy(a, 1, rows(a, here, c, 0), (*x_nbr, c)), copy(a, 4, rows(a, here, c, 1), (*y_nbr, c)))]
    sent += [cp for a in n for cp in (copy(a, 2, rows(a, here, c, 1), (*x_nbr, c)), copy(a, 3, rows(a, here, c, 0), (*y_nbr, c)))]
    for cp in sent:
        cp.start()

    def after(arrivals, a, k, region, to):
        for cp in arrivals:
            cp.wait_recv()
        sent.append(copy(a, k, region, to))
        sent[-1].start()

    for a in n:
        after([copy(a, 1, rows(a, x_nbr, c, 0), me)], a, 5, rows(a, x_nbr, c, 0), (*y_nbr, c))
        after([copy(a, 4, rows(a, y_nbr, c, 1), me)], a, 6, rows(a, y_nbr, c, 1), (*x_nbr, c))
    for a in n:
        after([copy(a, 2, rows(a, x_nbr, c, 1), me)], a, 7, rows(a, x_nbr, c), sibling)
        after([copy(a, 3, rows(a, y_nbr, c, 0), me)], a, 8, rows(a, y_nbr, c), sibling)
    for a in n:
        after([copy(a, 5, rows(a, diag, c, 0), me), copy(a, 6, rows(a, diag, c, 1), me)], a, 9, rows(a, diag, c), sibling)
    for a in n:
        copy(a, 0, rows(a, here, 1 - c), me).wait_recv()
        for k, chip in ((7, x_nbr), (8, y_nbr), (9, diag)):
            copy(a, k, rows(a, chip, 1 - c), me).wait_recv()
    for cp in sent:
        cp.wait_send()


def _exchange_sems(n_bufs):
    return [pltpu.SemaphoreType.DMA((n_bufs * EXCHANGE_COPIES,)), pltpu.SemaphoreType.DMA((n_bufs * EXCHANGE_COPIES,))]


def gather_weights(w_in_t, w_out, small, pad_rows):
    rows, _, cols = w_in_t.shape
    buf_rows = -(-rows // ROW_TILE_BF16) * ROW_TILE_BF16

    def body(wi_ref, wo_ref, sm_ref, wi_out, wo_out, sm_out, wi_buf, send_sems, recv_sems):
        x, y, c = _position()
        me = 4 * x + 2 * y + c
        wi_buf[me, pl.ds(0, rows), :] = wi_ref[:, 0, :].astype(MXU_DTYPE)
        wi_buf[me, pl.ds(rows, buf_rows - rows), :] = jnp.zeros((buf_rows - rows, cols), MXU_DTYPE)
        wo_out[me] = wo_ref[...].astype(MXU_DTYPE)
        sm_out[me] = sm_ref[...]
        _exchange_blocks([wi_buf, wo_out, sm_out], send_sems, recv_sems)
        for d in range(N_DEV):
            wi_out[pl.ds(d * rows, rows), :] = wi_buf[d, pl.ds(0, rows), :]
        wi_out[pl.ds(N_DEV * rows, pad_rows), :] = jnp.zeros((pad_rows, cols), MXU_DTYPE)

    return pl.pallas_call(
        body, name="gather_weights", in_specs=[VMEM_SPEC] * 3, out_specs=[VMEM_SPEC] * 3,
        out_shape=[jax.ShapeDtypeStruct((N_DEV * rows + pad_rows, cols), MXU_DTYPE),
                   jax.ShapeDtypeStruct((N_DEV,) + w_out.shape, MXU_DTYPE), jax.ShapeDtypeStruct((N_DEV,) + small.shape, F32)],
        scratch_shapes=[pltpu.VMEM((N_DEV, buf_rows, cols), MXU_DTYPE)] + _exchange_sems(3),
        compiler_params=pltpu.CompilerParams(vmem_limit_bytes=VMEM_LIMIT))(w_in_t, w_out, small)


HOPS = 6


def reduce_gradients(tensors, small, name):
    n_t = len(tensors)
    arrays = [a for parts, _ in tensors for a, _ in parts]
    first_array = [sum(len(parts) for parts, _ in tensors[:t]) for t in range(n_t)]

    def pieces(t, j):
        parts, block_rows = tensors[t]
        out, base = [], 0
        for pi, (_, valid) in enumerate(parts):
            lo, hi = max(j * block_rows, base), min((j + 1) * block_rows, base + valid)
            if lo < hi:
                out.append((first_array[t] + pi, lo - base, lo - j * block_rows, hi - lo))
            base += valid
        return out

    def body(*refs):
        n_a = len(arrays)
        in_refs, small_ref = refs[:n_a], refs[n_a]
        out_refs, small_sum = refs[n_a + 1:n_a + 1 + n_t], refs[n_a + 1 + n_t]
        bufs, small_buf = refs[n_a + 2 + n_t:n_a + 2 + 5 * n_t], refs[n_a + 2 + 5 * n_t]
        s1_sems, r1_sems, s2_sems, r2_sems, small_send, small_recv = refs[n_a + 3 + 5 * n_t:]
        x, y, c = _position()
        chip = 2 * x + y

        def put(t, dst, j, add=None):
            for ai, src_row, dst_row, size in pieces(t, j):
                v = in_refs[ai][pl.ds(src_row, size), :]
                if add is not None:
                    v = v + add[pl.ds(dst_row, size), :].astype(F32)
                dst[pl.ds(dst_row, size), :] = v.astype(dst.dtype)

        def swap(t, k):
            send1, recv1 = bufs[4 * t], bufs[4 * t + 1]
            return pltpu.make_async_remote_copy(src_ref=send1.at[k], dst_ref=recv1.at[k], send_sem=s1_sems.at[4 * t + k],
                                                recv_sem=r1_sems.at[4 * t + k], device_id=(x, y, 1 - c), device_id_type=MESH)

        to_x, to_y, to_diag = 2 * (1 - x) + y, 2 * x + (1 - y), 2 * (1 - x) + (1 - y)
        x_dev, y_dev = (1 - x, y, c), (x, 1 - y, c)

        def half(ref, h):
            total = ref.shape[0]
            split = total // 2 // ROW_TILE_BF16 * ROW_TILE_BF16
            return ref.at[pl.ds(0, split)] if h == 0 else ref.at[pl.ds(split, total - split)]

        def hop(t, copy_id, src, dst, to):
            return pltpu.make_async_remote_copy(src_ref=src, dst_ref=dst, send_sem=s2_sems.at[HOPS * t + copy_id],
                                                recv_sem=r2_sems.at[HOPS * t + copy_id], device_id=to, device_id_type=MESH)

        def hops(t):
            send2, landing = bufs[4 * t + 2], bufs[4 * t + 3]
            return [hop(t, 0, half(send2.at[to_diag], 0), half(landing.at[0], 0), x_dev),
                    hop(t, 1, half(send2.at[to_diag], 1), half(landing.at[0], 1), y_dev),
                    hop(t, 2, half(send2.at[to_x], 0), half(landing.at[1], 0), x_dev),
                    hop(t, 3, half(send2.at[to_y], 1), half(landing.at[2], 1), y_dev),
                    hop(t, 4, half(send2.at[to_x], 1), half(landing.at[1], 1), x_dev),
                    hop(t, 5, half(send2.at[to_y], 0), half(landing.at[2], 0), y_dev)]

        def add_relay(t, slot, h):
            dst, src = half(bufs[4 * t + 2].at[slot], h), half(bufs[4 * t + 3].at[0], h)
            dst[...] = (dst[...].astype(F32) + src[...].astype(F32)).astype(dst.dtype)

        for t in range(n_t):
            send2 = bufs[4 * t + 2]
            pad = send2.shape[1] - tensors[t][1]
            if pad:
                send2[:, pl.ds(tensors[t][1], pad), :] = jnp.zeros((4, pad, send2.shape[2]), send2.dtype)
            for j in range(N_DEV):
                @pl.when((j & 1) != c)
                def _():
                    put(t, bufs[4 * t].at[j >> 1], j)
            for k in range(4):
                swap(t, k).start()

        small_buf[4 * x + 2 * y + c] = small_ref[...]
        _exchange_blocks([small_buf], small_send, small_recv)
        total = small_buf[0]
        for d in range(1, N_DEV):
            total = total + small_buf[d]
        small_sum[...] = total

        for t in range(n_t):
            recv1 = bufs[4 * t + 1]
            for k in range(4):
                swap(t, k).wait_recv()
                for j in (2 * k, 2 * k + 1):
                    @pl.when(((j & 1) == c) & (k != chip))
                    def _():
                        put(t, bufs[4 * t + 2].at[k], j, add=recv1.at[k])

                    @pl.when(((j & 1) == c) & (k == chip))
                    def _():
                        put(t, out_refs[t], j, add=recv1.at[k])
            for cp in hops(t)[0:4]:
                cp.start()

        for t in range(n_t):
            cps = hops(t)
            cps[0].wait_recv()
            add_relay(t, to_y, 0)
            cps[5].start()
            cps[1].wait_recv()
            add_relay(t, to_x, 1)
            cps[4].start()

        for t in range(n_t):
            cps, rows = hops(t), tensors[t][1]
            for first, second, slot in ((cps[2], cps[4], 1), (cps[3], cps[5], 2)):
                first.wait_recv()
                second.wait_recv()
                out_refs[t][...] += bufs[4 * t + 3][slot, pl.ds(0, rows), :].astype(F32)

        for t in range(n_t):
            for cp in hops(t):
                cp.wait_send()
            for k in range(4):
                swap(t, k).wait_send()

    scratch, out_shape = [], []
    for parts, block_rows in tensors:
        cols = parts[0][0].shape[1]
        tiled_rows = -(-block_rows // ROW_TILE_BF16) * ROW_TILE_BF16
        scratch += [pltpu.VMEM((4, block_rows, cols), MXU_DTYPE)] * 2
        scratch += [pltpu.VMEM((4, tiled_rows, cols), MXU_DTYPE), pltpu.VMEM((3, tiled_rows, cols), MXU_DTYPE)]
        out_shape.append(jax.ShapeDtypeStruct((block_rows, cols), F32))
    out_shape.append(jax.ShapeDtypeStruct(small.shape, F32))
    scratch += [pltpu.VMEM((N_DEV,) + small.shape, F32)] + [pltpu.SemaphoreType.DMA((4 * n_t,))] * 2
    scratch += [pltpu.SemaphoreType.DMA((HOPS * n_t,))] * 2 + _exchange_sems(1)
    return pl.pallas_call(
        body, name=name, in_specs=[VMEM_SPEC] * (len(arrays) + 1), out_specs=[VMEM_SPEC] * (n_t + 1), out_shape=out_shape,
        scratch_shapes=scratch, compiler_params=pltpu.CompilerParams(vmem_limit_bytes=VMEM_LIMIT),
    )(*arrays, small)


def _adamw_step(w, g, m, v):
    mn = ADAM_B1 * m + (1.0 - ADAM_B1) * g
    vn = ADAM_B2 * v + (1.0 - ADAM_B2) * jnp.square(g)
    m_hat = mn / (1.0 - ADAM_B1 ** ADAM_STEP)
    v_hat = vn / (1.0 - ADAM_B2 ** ADAM_STEP)
    return -ADAM_LR * (m_hat / (jnp.sqrt(v_hat) + ADAM_EPS) + ADAM_WD * w), mn, vn


def adamw_small(ws, gs, ms, vs):
    k = len(ws)

    def body(*refs):
        ins, outs = refs[:4 * k], refs[4 * k:]
        for i in range(k):
            w_ref, g_ref, m_ref, v_ref = ins[i::k]
            outs[3 * i][...], outs[3 * i + 1][...], outs[3 * i + 2][...] = _adamw_step(w_ref[...], g_ref[...], m_ref[...],
                                                                                      v_ref[...])

    out = pl.pallas_call(body, name="adamw_small", in_specs=[VMEM_SPEC] * (4 * k), out_specs=[VMEM_SPEC] * (3 * k),
                         out_shape=[jax.ShapeDtypeStruct(w.shape, F32) for w in ws for _ in range(3)],
                         compiler_params=pltpu.CompilerParams(vmem_limit_bytes=VMEM_LIMIT))(*ws, *gs, *ms, *vs)
    return [out[3 * i:3 * i + 3] for i in range(k)]


def adamw_w_in(w, g_t, m, v):
    def body(w_ref, g_ref, m_ref, v_ref, go_ref, d_ref, nm_ref, nv_ref):
        g = g_ref[...]
        go_ref[:, 0, :] = g
        d_ref[:, 0, :], nm_ref[:, 0, :], nv_ref[:, 0, :] = _adamw_step(w_ref[:, 0, :], g, m_ref[:, 0, :], v_ref[:, 0, :])

    return pl.pallas_call(body, name="adamw_w_in", in_specs=[VMEM_SPEC] * 4, out_specs=[VMEM_SPEC] * 4,
                          out_shape=[jax.ShapeDtypeStruct(w.shape, F32)] * 4,
                          compiler_params=pltpu.CompilerParams(vmem_limit_bytes=VMEM_LIMIT))(w, g_t, m, v)


def _pad_rows(a, rows=8):
    return jnp.pad(a, ((0, rows - a.shape[0]), (0, 0)))


def _pad_lanes(a, lanes=128):
    return jnp.pad(a, ((0, 0), (0, lanes - a.shape[1])))


def kernel(x, meta_tokens, norm_w, w_in, conv_w, hg_lb_logits, hg_norm_w, gdn_A_log, gdn_dt_bias, gdn_norm_w, w_out, final_norm_w, loss_target, m_meta_tokens, m_norm_w, m_w_in, m_conv_w, m_hg_lb_logits, m_hg_norm_w, m_gdn_A_log, m_gdn_dt_bias, m_gdn_norm_w, m_w_out, m_final_norm_w, v_meta_tokens, v_norm_w, v_w_in, v_conv_w, v_hg_lb_logits, v_hg_norm_w, v_gdn_A_log, v_gdn_dt_bias, v_gdn_norm_w, v_w_out, v_final_norm_w):
    b, seq, _ = x.shape
    n = b * seq
    dev = 4 * lax.axis_index("x") + 2 * lax.axis_index("y") + lax.axis_index("c")
    col_shard = IN_COLS // N_DEV

    small_w = jnp.concatenate([_pad_lanes(meta_tokens, 256), _pad_rows(_pad_lanes(conv_w[0], 256))], axis=0)
    w_t, w_out_g, small_g = gather_weights(jnp.transpose(w_in, (2, 0, 1)), w_out[0], small_w, AB_PAD - 2 * HEADS)
    meta_g = small_g[:, 0:N_META, 0:D_MODEL // N_DEV]
    conv_g = small_g[:, N_META:N_META + CONV_TAPS, 0:QKV // N_DEV]
    w_out_full = w_out_g.reshape(2 * WIDTH, D_MODEL)
    cw = jnp.transpose(conv_g, (1, 0, 2)).reshape(CONV_TAPS, QKV)
    meta = jnp.transpose(meta_g, (1, 0, 2)).reshape(N_META, D_MODEL)
    alog = _pad_lanes(gdn_A_log)
    dtb = _pad_lanes(gdn_dt_bias)
    fw = final_norm_w.reshape(1, D_MODEL)

    h0 = jnp.concatenate([jnp.zeros((CHUNK - N_META, D_MODEL), F32), meta], axis=0)
    x2 = x.reshape(n, D_MODEL)
    phg, pgd, pab, phg0, pgd0, pab0, u0 = in_proj(x2, h0, norm_w, w_t)
    phg3, pgd3, pab3 = phg.reshape(b, seq, 4 * WIDTH), pgd.reshape(b, seq, 4 * WIDTH), pab.reshape(b, seq, AB_PAD)
    hg_loc, hg_lead = hg_local_fwd(phg3, phg0, hg_lb_logits)
    gd_loc, gd_inv, gd_lead, gd_inv0 = gd_local_fwd(pgd3, pgd0, pab3, pab0, cw, alog, dtb)
    (y_hg, s_hg), (y_gd, s_gd) = run_scans([hg_scan_fwd(phg3, phg0, hg_loc, hg_lead, hg_norm_w),
                                            gd_scan_fwd(pgd3, pgd0, gd_loc, gd_lead, gdn_norm_w)],
                                           seq // (SCAN_CHUNKS_FWD * CHUNK), "scans")

    dh2, dy_hg, dy_gd, g_w_out, loss_part, g_fw = out_proj_loss(
        x2, loss_target.reshape(n, D_MODEL), y_hg.reshape(n, WIDTH), y_gd.reshape(n, WIDTH), w_out_full, fw)

    hb, gb = run_scans([hg_scan_bwd(phg3, phg0, hg_loc, hg_lead, hg_norm_w, s_hg, dy_hg.reshape(b, seq, WIDTH)),
                        gd_scan_bwd(pgd3, pgd0, gd_loc, gd_lead, gdn_norm_w, s_gd, dy_gd.reshape(b, seq, WIDTH))],
                       seq // (SCAN_CHUNKS * CHUNK), "scans_bwd")
    dphg, dphg0, g_lb = hg_local_bwd(phg3, phg0, hg_lb_logits, hb[0:6], hb[6:12])
    g_hg_nw = hb[12]
    dpgd, dpab, dpgd0, dpab0, g_cw, g_alog, g_dtb = gd_local_bwd(pgd3, pgd0, pab3, pab0, cw, alog, dtb, gd_inv, gd_inv0,
                                                                 gb[0:6], gb[6], gb[7:13], gb[13])
    g_gd_nw = gb[14]
    dphg, dpgd, dpab = dphg.reshape(n, 4 * WIDTH), dpgd.reshape(n, 4 * WIDTH), dpab.reshape(n, AB_PAD)

    grad_x, dh0, g_nw, g_w_hg, g_w_gd, g_w_ab = in_proj_bwd(dphg, dpgd, dpab, w_t, x2, dh2, norm_w, h0, u0, dphg0, dpgd0, dpab0)

    small = jnp.concatenate([
        g_nw.reshape(8, 128), g_lb.reshape(8, 128), _pad_rows(g_hg_nw), _pad_rows(g_alog), _pad_rows(g_dtb), _pad_rows(g_gd_nw),
        g_fw.reshape(8, 128), g_cw.reshape(48, 128),
        dh0[CHUNK - N_META:CHUNK].reshape(128, 128), loss_part], axis=0)
    g_w_in_t, g_w_out, small = reduce_gradients(
        [([(g_w_hg, 4 * WIDTH), (g_w_gd, 4 * WIDTH), (g_w_ab, 2 * HEADS)], col_shard),
         ([(g_w_out, 2 * WIDTH)], (2 * WIDTH) // N_DEV)], small, "reduce_gradients")
    g_norm_w = small[0:8].reshape(1, D_MODEL)
    g_lb = small[8:16].reshape(2, WIDTH)
    g_hg_nw = small[16:17]
    g_alog = small[24:25, 0:HEADS]
    g_dtb = small[32:33, 0:HEADS]
    g_gd_nw = small[40:41]
    g_fw = small[48:56].reshape(1, D_MODEL)
    g_cw_full = small[56:104].reshape(CONV_TAPS, QKV)
    g_meta_full = small[104:232].reshape(N_META, D_MODEL)
    loss = small[232, 0]
    g_conv = lax.dynamic_slice_in_dim(g_cw_full, dev * (QKV // N_DEV), QKV // N_DEV, axis=1)
    g_meta = lax.dynamic_slice_in_dim(g_meta_full, dev * (D_MODEL // N_DEV), D_MODEL // N_DEV, axis=1)

    names = ["meta_tokens", "norm_w", "w_in", "conv_w", "hg_lb_logits", "hg_norm_w", "gdn_A_log", "gdn_dt_bias",
             "gdn_norm_w", "w_out", "final_norm_w"]
    weights = [meta_tokens, norm_w, w_in, conv_w, hg_lb_logits, hg_norm_w, gdn_A_log, gdn_dt_bias, gdn_norm_w, w_out,
               final_norm_w]
    moms = [m_meta_tokens, m_norm_w, m_w_in, m_conv_w, m_hg_lb_logits, m_hg_norm_w, m_gdn_A_log, m_gdn_dt_bias,
            m_gdn_norm_w, m_w_out, m_final_norm_w]
    vars_ = [v_meta_tokens, v_norm_w, v_w_in, v_conv_w, v_hg_lb_logits, v_hg_norm_w, v_gdn_A_log, v_gdn_dt_bias,
             v_gdn_norm_w, v_w_out, v_final_norm_w]
    grads2d = [g_meta, g_norm_w, g_w_in_t, g_conv, g_lb, g_hg_nw, g_alog, g_dtb, g_gd_nw, g_w_out, g_fw]
    i_w_in = names.index("w_in")
    others = [i for i in range(len(names)) if i != i_w_in]
    as_grad = lambda i, a: a.reshape(grads2d[i].shape)
    stepped = adamw_small([as_grad(i, weights[i]) for i in others], [grads2d[i] for i in others],
                          [as_grad(i, moms[i]) for i in others], [as_grad(i, vars_[i]) for i in others])
    results = {i: [a.reshape(weights[i].shape) for a in (grads2d[i], *stepped[j])] for j, i in enumerate(others)}
    to3, back = (lambda a: jnp.transpose(a, (2, 0, 1))), (lambda a: jnp.transpose(a, (1, 2, 0)))
    results[i_w_in] = [back(a) for a in adamw_w_in(to3(w_in), g_w_in_t, to3(m_w_in), to3(v_w_in))]
    grads, deltas, new_ms, new_vs = zip(*(results[i] for i in range(len(names))))
    return (loss, grad_x.reshape(x.shape), *grads, *deltas, *new_ms, *new_vs)
```

```python
import jax
import jax.numpy as jnp
import numpy as np
from jax import lax
from jax.experimental import pallas as pl
from jax.experimental.pallas import tpu as pltpu

F32 = jnp.float32
BF16 = jnp.bfloat16
MXU_DTYPE = BF16

D_MODEL = 1024
N_META = 16
CHUNK = 64
SUB = 16
ROW_TILE_BF16 = 16
HEADS = 4
DH = 128
WIDTH = HEADS * DH
QKV = 3 * WIDTH
CONV_TAPS = 4
HALO = 8
EPS = 1e-6
IN_COLS = 4 * WIDTH + 4 * WIDTH + 2 * HEADS
AB_PAD = 128
N_DEV = 8
LOCAL_CHUNKS = 4
SCAN_CHUNKS_FWD = 4
SCAN_CHUNKS = 2
VMEM_LIMIT = 56 * 1024 * 1024
VMEM_LIMIT_LARGE = 60 * 1024 * 1024

ADAM_LR = 0.001
ADAM_B1 = 0.9
ADAM_B2 = 0.999
ADAM_EPS = 1e-08
ADAM_WD = 0.01
ADAM_STEP = 10

VMEM_SPEC = pl.BlockSpec(memory_space=pltpu.VMEM)
MESH = pl.DeviceIdType.MESH


def _mm_tn(a, b):
    return lax.dot_general(a.astype(MXU_DTYPE), b.astype(MXU_DTYPE), (((0,), (0,)), ((), ())), preferred_element_type=F32)


def _bmm(a, b):
    return lax.dot_general(a.astype(MXU_DTYPE), b.astype(MXU_DTYPE), (((2,), (1,)), ((0,), (0,))), preferred_element_type=F32)


def _bmm_nt(a, b):
    return lax.dot_general(a.astype(MXU_DTYPE), b.astype(MXU_DTYPE), (((2,), (2,)), ((0,), (0,))), preferred_element_type=F32)


def _bmm_tn(a, b):
    return lax.dot_general(a.astype(MXU_DTYPE), b.astype(MXU_DTYPE), (((1,), (1,)), ((0,), (0,))), preferred_element_type=F32)


def _iota2(n, m):
    return lax.broadcasted_iota(jnp.int32, (n, m), 0), lax.broadcasted_iota(jnp.int32, (n, m), 1)


def _silu(x):
    return x * jax.nn.sigmoid(x)


def _gated_norm(o, z, nw):
    return o * lax.rsqrt(jnp.mean(o * o, axis=-1, keepdims=True) + EPS) * nw * _silu(z)


def _heads(a, nb):
    return jnp.stack([a[c * CHUNK:(c + 1) * CHUNK, h * DH:(h + 1) * DH] for c in range(nb) for h in range(HEADS)], axis=0)


def _unheads(a3, nb):
    return jnp.concatenate(
        [jnp.concatenate([a3[c * HEADS + h] for h in range(HEADS)], axis=1) for c in range(nb)], axis=0)


def _split3(x):
    hi = x.astype(BF16)
    r1 = x - hi.astype(F32)
    mid = r1.astype(BF16)
    return hi, mid, (r1 - mid.astype(F32)).astype(BF16)


def _summation_matrices(pattern, n_out):
    s = pattern(np.arange(n_out)[:, None], np.arange(CHUNK)[None, :]).astype(np.float32)
    return jnp.asarray(np.tile(s, (1, 3)), BF16), jnp.asarray(np.tile(s.T, (1, 2)), BF16)


def _select_rows(mats, chunks):
    width = chunks[0].shape[1]
    out = _summation(*mats, jnp.concatenate(chunks, axis=1))
    return [out[:, c * width:(c + 1) * width] for c in range(len(chunks))]


def _summation_impl(s, v):
    return jnp.dot(s, jnp.concatenate(_split3(v), axis=0), preferred_element_type=F32)


@jax.custom_vjp
def _summation(s, s_t, v):
    return _summation_impl(s, v)


def _summation_fwd(s, s_t, v):
    return _summation_impl(s, v), s_t


def _summation_bwd(s_t, d):
    hi = d.astype(BF16)
    return None, None, jnp.dot(s_t, jnp.concatenate([hi, (d - hi.astype(F32)).astype(BF16)], axis=0),
                               preferred_element_type=F32)


_summation.defvjp(_summation_fwd, _summation_bwd)


def _chunks(x, nb):
    return [x[c * CHUNK:(c + 1) * CHUNK] for c in range(nb)]


def _running_sum(i, j):
    return j <= i


HG_LEVELS = 6


def _hg_sums(i, j):
    lvl, t = i >> HG_LEVELS, i & (CHUNK - 1)
    last = t
    for l in range(1, HG_LEVELS + 1):
        width = HG_LEVELS + 1 - l
        last = np.where(lvl == l, ((t >> width) << width) + (CHUNK >> l) - 1, last)
    return j <= last


def _level_operand(sh, q3, k3, x):
    def second():
        return ((lax.broadcasted_iota(jnp.int32, (CHUNK, DH), 0) >> sh) & 1) == 1

    def forward(q3, k3, x):
        decay = jnp.exp(-jnp.abs(x))
        out = jnp.where(second(), q3, k3) * decay
        return out, (decay, out)

    def backward(saved, d):
        decay, out = saved
        d_side, t = d * decay, d * out
        return jnp.where(second(), d_side, 0.0), jnp.where(second(), 0.0, d_side), jnp.where(second(), t, -t)

    operand = jax.custom_vjp(lambda q3, k3, x: forward(q3, k3, x)[0])
    operand.defvjp(forward, backward)
    return operand(q3, k3, x)


def hg_local(p, logits, sum_mats):
    nb = p.shape[0] // CHUNK
    l0, l1 = logits[0:1], logits[1:2]
    mx = jnp.maximum(l0, l1)
    e0, e1 = jnp.exp(l0 - mx), jnp.exp(l1 - mx)
    lb = e0 / (e0 + e1)
    q = _silu(p[:, 0:WIDTH])
    f = lb + (1.0 - lb) * jax.nn.sigmoid(p[:, WIDTH:2 * WIDTH])
    k = 1.0 - f
    logf = jnp.log(f)
    sums = _select_rows(sum_mats, _chunks(logf, nb))
    level = lambda l: _heads(jnp.concatenate([s[l * CHUNK:(l + 1) * CHUNK] for s in sums], axis=0), nb)
    q3, k3, v3, g3 = _heads(q, nb), _heads(k, nb), _heads(p[:, 2 * WIDTH:3 * WIDTH], nb), level(0)
    r, c = _iota2(CHUNK, CHUNK)
    a = jnp.where(r == c, _bmm_nt(q3, k3), 0.0)
    for l in range(1, HG_LEVELS + 1):
        sh = HG_LEVELS - l
        qk = _level_operand(sh, q3, k3, g3 - level(l))
        pair = ((r >> (sh + 1)) == (c >> (sh + 1))) & (((r >> sh) & 1) == 1) & (((c >> sh) & 1) == 0)
        a = a + jnp.where(pair, _bmm_nt(qk, qk), 0.0)
    o = _bmm(a, v3)
    glast = g3[:, CHUNK - 1:CHUNK, :]
    egs = tuple(jnp.concatenate([jnp.exp(glast[c * HEADS + h]) for h in range(HEADS)], axis=1) for c in range(nb))
    return _unheads(q3 * jnp.exp(g3), nb), _unheads(k3 * jnp.exp(glast - g3), nb), _unheads(o, nb), egs


def hg_scan(q_in, k_out, v, eg, o_intra, z, nw, st):
    o = o_intra + _bmm_nt(q_in, st)
    return _gated_norm(o, z, nw), st * eg + _bmm_tn(v, k_out)


def _tri_y_impl(a):
    r, c = _iota2(CHUNK, CHUNK)
    same16 = (r // SUB) == (c // SUB)
    same32 = (r // (2 * SUB)) == (c // (2 * SUB))
    a0 = jnp.where(same16, a, 0.0)
    y = -a0
    pw = _bmm(a0, a0)
    for _ in range(2):
        y = y + pw + _bmm(y, pw)
        pw = _bmm(pw, pw)
    y = y + pw + _bmm(y, pw)
    for ak in (jnp.where(same32 & jnp.logical_not(same16), a, 0.0), jnp.where(same32, 0.0, a)):
        m = ak + _bmm(y, ak)
        y = y - (m + _bmm(m, y))
    return y


@jax.custom_vjp
def _tri_y(a):
    return _tri_y_impl(a)


def _tri_y_fwd(a):
    y = _tri_y_impl(a)
    return y, y


def _tri_y_bwd(y, dy):
    n = dy + _bmm_tn(y, dy)
    return (-(n + _bmm_nt(n, y)),)


_tri_y.defvjp(_tri_y_fwd, _tri_y_bwd)


def _saved_inverse(y):
    @jax.custom_vjp
    def inverse(a):
        return y

    inverse.defvjp(lambda a: (y, None), lambda _, dy: _tri_y_bwd(y, dy))
    return inverse


def _head_rows(a3, nb):
    return jnp.concatenate([a3[g] for g in range(nb * HEADS)], axis=0)


def _rows_down(x, s):
    rows = x.shape[0]

    @jax.custom_vjp
    def rotate(v):
        return pltpu.roll(v, s, 0)

    rotate.defvjp(lambda v: (pltpu.roll(v, s, 0), None), lambda _, d: (pltpu.roll(d, rows - s, 0),))
    return rotate(x)


def gd_local(xx, ab, cw, alog, dtb, sum_mats, inverse=_tri_y):
    n = ab.shape[0]
    nb = n // CHUNK
    conv = cw[CONV_TAPS - 1:CONV_TAPS] * xx[HALO:HALO + n]
    for j in range(CONV_TAPS - 1):
        conv = conv + cw[j:j + 1] * _rows_down(xx, CONV_TAPS - 1 - j)[HALO:HALO + n]
    act = _silu(conv)
    x = ab + dtb
    g_all = -jnp.exp(alog) * (jnp.maximum(x, 0.0) + jnp.log1p(jnp.exp(-jnp.abs(x))))
    beta_all = jax.nn.sigmoid(ab)
    gam_all = jnp.concatenate(_select_rows(sum_mats, _chunks(g_all, nb)), axis=0)
    q3, k3, v3 = _heads(act[:, 0:WIDTH], nb), _heads(act[:, WIDTH:2 * WIDTH], nb), _heads(act[:, 2 * WIDTH:QKV], nb)
    q3 = q3 * lax.rsqrt(jnp.sum(q3 * q3, axis=-1, keepdims=True) + EPS) * (DH ** -0.5)
    k3 = k3 * lax.rsqrt(jnp.sum(k3 * k3, axis=-1, keepdims=True) + EPS)
    pairs = [(c, h) for c in range(nb) for h in range(HEADS)]
    beta = jnp.stack([beta_all[c * CHUNK:(c + 1) * CHUNK, HEADS + h:HEADS + h + 1] for c, h in pairs], axis=0)
    gam = jnp.stack([gam_all[c * CHUNK:(c + 1) * CHUNK, h:h + 1] for c, h in pairs], axis=0)
    gam_t = [gam_all[c * CHUNK:(c + 1) * CHUNK].T for c in range(nb)]
    gam_row = jnp.stack([gam_t[c][h:h + 1, :] for c, h in pairs], axis=0)
    glast = gam[:, CHUNK - 1:CHUNK, :]
    r, c = _iota2(CHUNK, CHUNK)
    dec = jnp.exp(jnp.where(c < r, gam - gam_row, -jnp.inf))
    y = inverse(beta * _bmm_nt(k3, k3) * dec)
    eg = jnp.exp(gam)
    rhs = jnp.concatenate([beta * v3, (beta * eg) * k3], axis=2)
    sol = rhs + _bmm(y, rhs)
    qk = _bmm_nt(q3, k3) * jnp.where(r == c, 1.0, dec)
    eas = tuple(jnp.exp(gam_all[(c + 1) * CHUNK - 1:(c + 1) * CHUNK]) for c in range(nb))
    return (_unheads(sol[:, :, 0:DH], nb), _unheads(sol[:, :, DH:2 * DH], nb), _unheads(q3 * eg, nb),
            _unheads(k3 * jnp.exp(glast - gam), nb), _head_rows(qk, nb), eas), _head_rows(y, nb)


def gd_scan(uu, ww, qe, ke, qk, ea, z, nw, s):
    u = uu - _bmm(ww, s)
    o = _bmm(qe, s) + _bmm(qk, u)
    return _gated_norm(o, z, nw), ea * s + _bmm_tn(ke, u)


def _cparams(*sem):
    return pltpu.CompilerParams(dimension_semantics=sem, vmem_limit_bytes=VMEM_LIMIT)


def _row_tile(n):
    for t in (512, 256, 128, 64):
        if n % t == 0:
            return t
    raise ValueError(f"unsupported token count {n}")


def _w_in_specs():
    once = pl.Buffered(1)
    return [pl.BlockSpec((4 * WIDTH, D_MODEL), lambda *i: (0, 0), pipeline_mode=once),
            pl.BlockSpec((4 * WIDTH, D_MODEL), lambda *i: (1, 0), pipeline_mode=once),
            pl.BlockSpec((AB_PAD, D_MODEL), lambda *i: (8 * WIDTH // AB_PAD, 0), pipeline_mode=once)]


def in_proj(h, h0, norm_w, w_t):
    n = h.shape[0]
    tm = _row_tile(n)
    nt = (((1,), (1,)), ((), ()))

    def body(h_ref, h0_ref, nw_ref, whg_ref, wgd_ref, wab_ref, phg_ref, pgd_ref, pab_ref, phg0_ref, pgd0_ref, pab0_ref, u0_ref):
        def project(x, hg_ref, gd_ref, ab_ref):
            u = (x * lax.rsqrt(jnp.mean(x * x, axis=-1, keepdims=True) + EPS) * nw_ref[...]).astype(MXU_DTYPE)
            hg_ref[...] = lax.dot_general(u, whg_ref[...], nt, preferred_element_type=F32)
            gd_ref[...] = lax.dot_general(u, wgd_ref[...], nt, preferred_element_type=F32)
            ab_ref[...] = lax.dot_general(u, wab_ref[...], nt, preferred_element_type=F32)
            return u

        @pl.when(pl.program_id(0) == 0)
        def _():
            u0_ref[...] = project(h0_ref[...], phg0_ref, pgd0_ref, pab0_ref)

        project(h_ref[...], phg_ref, pgd_ref, pab_ref)

    n0 = h0.shape[0]
    row = lambda w: pl.BlockSpec((tm, w), lambda i: (i, 0))
    lead = lambda w: pl.BlockSpec((n0, w), lambda i: (0, 0))
    widths = [4 * WIDTH, 4 * WIDTH, AB_PAD]
    return pl.pallas_call(
        body, grid=(n // tm,), name="in_proj",
        in_specs=[row(D_MODEL), lead(D_MODEL), pl.BlockSpec(norm_w.shape, lambda i: (0, 0))] + _w_in_specs(),
        out_specs=[row(w) for w in widths] + [lead(w) for w in widths] + [lead(D_MODEL)],
        out_shape=[jax.ShapeDtypeStruct((n, w), F32) for w in widths] + [jax.ShapeDtypeStruct((n0, w), F32) for w in widths]
        + [jax.ShapeDtypeStruct((n0, D_MODEL), MXU_DTYPE)],
        compiler_params=_cparams("arbitrary"),
    )(h, h0, norm_w, w_t, w_t, w_t)


def out_proj_loss(x, tgt, y_hg, y_gd, w_out, fw):
    n = x.shape[0]
    tm = _row_tile(n)
    inv_d = 1.0 / D_MODEL

    def body(x_ref, t_ref, yh_ref, yg_ref, w_ref, fw_ref, dh_ref, dyh_ref, dyg_ref, dw_ref, loss_ref, dfw_ref):
        @pl.when(pl.program_id(0) == 0)
        def _():
            dw_ref[...] = jnp.zeros_like(dw_ref)
            loss_ref[...] = jnp.zeros_like(loss_ref)
            dfw_ref[...] = jnp.zeros_like(dfw_ref)

        yh, yg = yh_ref[...], yg_ref[...]
        wa, wb = w_ref[0:WIDTH, :], w_ref[WIDTH:2 * WIDTH, :]
        h2 = x_ref[...] + jnp.dot(yh, wa, preferred_element_type=F32) + jnp.dot(yg, wb, preferred_element_type=F32)
        r2 = lax.rsqrt(jnp.mean(h2 * h2, axis=-1, keepdims=True) + EPS)
        nrm = h2 * r2
        fwv = fw_ref[...]
        err = nrm * fwv - t_ref[...]
        loss_ref[...] += jnp.full(loss_ref.shape, 0.5 * inv_d * jnp.sum(err * err), F32)
        dout = err * inv_d
        dfw_ref[...] += jnp.sum(dout * nrm, axis=0, keepdims=True)
        dn = dout * fwv
        dh2 = r2 * (dn - nrm * jnp.mean(dn * nrm, axis=-1, keepdims=True))
        dh_ref[...] = dh2
        dhb = dh2.astype(MXU_DTYPE)
        dyh_ref[...] = lax.dot_general(dhb, wa, (((1,), (1,)), ((), ())), preferred_element_type=F32)
        dyg_ref[...] = lax.dot_general(dhb, wb, (((1,), (1,)), ((), ())), preferred_element_type=F32)
        dw_ref[0:WIDTH, :] += lax.dot_general(yh, dhb, (((0,), (0,)), ((), ())), preferred_element_type=F32)
        dw_ref[WIDTH:2 * WIDTH, :] += lax.dot_general(yg, dhb, (((0,), (0,)), ((), ())), preferred_element_type=F32)

    row = lambda w: pl.BlockSpec((tm, w), lambda i: (i, 0))
    full = lambda s: pl.BlockSpec(s, lambda i: (0, 0))
    return pl.pallas_call(
        body, grid=(n // tm,), name="out_proj_loss",
        in_specs=[row(D_MODEL), row(D_MODEL), row(WIDTH), row(WIDTH), full(w_out.shape), full(fw.shape)],
        out_specs=[row(D_MODEL), row(WIDTH), row(WIDTH), full((2 * WIDTH, D_MODEL)), full((8, 128)), full((1, D_MODEL))],
        out_shape=[jax.ShapeDtypeStruct((n, D_MODEL), F32), jax.ShapeDtypeStruct((n, WIDTH), F32),
                   jax.ShapeDtypeStruct((n, WIDTH), F32), jax.ShapeDtypeStruct((2 * WIDTH, D_MODEL), F32),
                   jax.ShapeDtypeStruct((8, 128), F32), jax.ShapeDtypeStruct((1, D_MODEL), F32)],
        compiler_params=_cparams("arbitrary"),
    )(x, tgt, y_hg, y_gd, w_out, fw)


def in_proj_bwd(dphg, dpgd, dpab, w_t, h, dh2, norm_w, h0, u0, dphg0, dpgd0, dpab0):
    n = h.shape[0]
    tm = _row_tile(n)
    steps = n // tm

    def body(dphg_ref, dpgd_ref, dpab_ref, whg_ref, wgd_ref, wab_ref, h_ref, dh2_ref, nw_ref, h0_ref, u0_ref, d0hg_ref,
             d0gd_ref, d0ab_ref, dx_ref, dx0_ref, dnw_ref, ghg_ref, ggd_ref, gab_ref, acc_hg, acc_gd, acc_ab):
        i = pl.program_id(0)
        nwv = nw_ref[...]

        def norm_bwd(dps, x):
            du = jnp.dot(dps[0], whg_ref[...], preferred_element_type=F32)
            du += jnp.dot(dps[1], wgd_ref[...], preferred_element_type=F32)
            du += jnp.dot(dps[2], wab_ref[...], preferred_element_type=F32)
            r = lax.rsqrt(jnp.mean(x * x, axis=-1, keepdims=True) + EPS)
            nrm = x * r
            dn = du * nwv
            return r * (dn - nrm * jnp.mean(dn * nrm, axis=-1, keepdims=True)), nrm, jnp.sum(du * nrm, axis=0, keepdims=True)

        def accumulate(dps, u, first):
            for acc, dp in zip((acc_hg, acc_gd, acc_ab), dps):
                step = min(acc.shape[0], 512)
                for lo in range(0, acc.shape[0], step):
                    part = _mm_tn(dp[:, lo:lo + step], u)
                    acc[lo:lo + step, :] = part if first else acc[lo:lo + step, :] + part

        @pl.when(i == 0)
        def _():
            dps0 = (d0hg_ref[...], d0gd_ref[...], d0ab_ref[...])
            dx0_ref[...], _, dnw_ref[...] = norm_bwd(dps0, h0_ref[...])
            accumulate(dps0, u0_ref[...], True)

        dps = (dphg_ref[...], dpgd_ref[...], dpab_ref[...])
        dx, nrm, dnw = norm_bwd(dps, h_ref[...])
        dx_ref[...] = dh2_ref[...] + dx
        dnw_ref[...] += dnw
        accumulate(dps, (nrm * nwv).astype(MXU_DTYPE), False)

        @pl.when(i == steps - 1)
        def _():
            pltpu.sync_copy(acc_hg, ghg_ref)
            pltpu.sync_copy(acc_gd, ggd_ref)
            pltpu.sync_copy(acc_ab, gab_ref)

    row = lambda w: pl.BlockSpec((tm, w), lambda i: (i, 0))
    full = lambda a: pl.BlockSpec(a.shape, lambda i: (0, 0), pipeline_mode=pl.Buffered(1))
    anywhere = pl.BlockSpec(memory_space=pl.ANY)
    return pl.pallas_call(
        body, grid=(steps,), name="in_proj_bwd",
        in_specs=[row(4 * WIDTH), row(4 * WIDTH), row(AB_PAD)] + _w_in_specs() + [row(D_MODEL), row(D_MODEL), full(norm_w),
                                                                                   full(h0), full(u0), full(dphg0), full(dpgd0),
                                                                                   full(dpab0)],
        out_specs=[row(D_MODEL), pl.BlockSpec(h0.shape, lambda i: (0, 0)), pl.BlockSpec((1, D_MODEL), lambda i: (0, 0)),
                   anywhere, anywhere, anywhere],
        out_shape=[jax.ShapeDtypeStruct((n, D_MODEL), F32), jax.ShapeDtypeStruct(h0.shape, F32),
                   jax.ShapeDtypeStruct((1, D_MODEL), F32), jax.ShapeDtypeStruct((4 * WIDTH, D_MODEL), F32),
                   jax.ShapeDtypeStruct((4 * WIDTH, D_MODEL), F32), jax.ShapeDtypeStruct((AB_PAD, D_MODEL), F32)],
        scratch_shapes=[pltpu.VMEM((4 * WIDTH, D_MODEL), F32), pltpu.VMEM((4 * WIDTH, D_MODEL), F32),
                        pltpu.VMEM((AB_PAD, D_MODEL), F32)],
        compiler_params=pltpu.CompilerParams(dimension_semantics=("arbitrary",), vmem_limit_bytes=VMEM_LIMIT_LARGE),
    )(dphg, dpgd, dpab, w_t, w_t, w_t, h, dh2, norm_w, h0, u0, dphg0, dpgd0, dpab0)


def _sds(shape, dtype=F32):
    return jax.ShapeDtypeStruct(shape, dtype)


def _pairs(b):
    return [(i, h) for i in range(b) for h in range(HEADS)]


def _load_slabs(ref, b, k):
    return jnp.stack([ref[i, k * CHUNK:(k + 1) * CHUNK, h * DH:(h + 1) * DH].astype(F32) for i, h in _pairs(b)], axis=0)


def _lead_slabs(a, b):
    return jnp.stack([a[:, h * DH:(h + 1) * DH].astype(F32) for _, h in _pairs(b)], axis=0)


def _rows(a3, i):
    return jnp.concatenate([a3[i * HEADS + h] for h in range(HEADS)], axis=1)


def _store_slabs(ref, a3, b, k):
    for i in range(b):
        ref[i, k * CHUNK:(k + 1) * CHUNK, :] = _rows(a3, i).astype(ref.dtype)


def _sum_rows(a3, b):
    out = _rows(a3, 0)
    for i in range(1, b):
        out = out + _rows(a3, i)
    return out


def _save_states(ref, s, b, k):
    for i in range(b):
        ref[i, k] = jnp.concatenate([s[i * HEADS + h] for h in range(HEADS)], axis=0).astype(ref.dtype)


def _load_states(ref, b, k):
    return jnp.stack([ref[i, k, h * DH:(h + 1) * DH, :].astype(F32) for i, h in _pairs(b)], axis=0)


def hg_local_fwd(p, p0, logits):
    b, seq, _ = p.shape
    rows = LOCAL_CHUNKS * CHUNK
    nreal = seq // CHUNK

    def body(p_ref, p0_ref, lg_ref, s_ref, st_ref, q_ref, k_ref, o_ref, eg_ref, q0_ref, k0_ref, o0_ref, eg0_ref):
        sum_mats = (s_ref[...], st_ref[...])

        @pl.when((pl.program_id(0) == 0) & (pl.program_id(1) == 0))
        def _():
            q_in, k_out, o0_ref[...], (eg0_ref[...],) = hg_local(p0_ref[...], lg_ref[...], sum_mats)
            q0_ref[...], k0_ref[...] = q_in.astype(MXU_DTYPE), k_out.astype(MXU_DTYPE)

        q_in, k_out, o_intra, egs = hg_local(p_ref[...], lg_ref[...], sum_mats)
        q_ref[...], k_ref[...], o_ref[...] = q_in.astype(MXU_DTYPE), k_out.astype(MXU_DTYPE), o_intra
        for c in range(LOCAL_CHUNKS):
            eg_ref[c] = egs[c]

    slab = pl.BlockSpec((None, rows, WIDTH), lambda s, g: (s, g, 0))
    const = lambda shape: pl.BlockSpec(shape, lambda s, g: (0, 0))
    lead_shapes = [(CHUNK, WIDTH)] * 3 + [(1, WIDTH)]
    sum_mats = _summation_matrices(_hg_sums, (HG_LEVELS + 1) * CHUNK)
    out = pl.pallas_call(
        body, grid=(b, seq // rows), name="hgrn2_local",
        in_specs=[pl.BlockSpec((None, rows, 4 * WIDTH), lambda s, g: (s, g, 0)), const(p0.shape), const(logits.shape)]
        + [const(a.shape) for a in sum_mats],
        out_specs=[slab, slab, slab, pl.BlockSpec((None, LOCAL_CHUNKS, 1, WIDTH), lambda s, g: (s, g, 0, 0))]
        + [const(s) for s in lead_shapes],
        out_shape=[_sds((b, seq, WIDTH), MXU_DTYPE)] * 2 + [_sds((b, seq, WIDTH)), _sds((b, nreal, 1, WIDTH))]
        + [_sds(lead_shapes[0], MXU_DTYPE)] * 2 + [_sds(lead_shapes[2]), _sds(lead_shapes[3])],
        compiler_params=_cparams("arbitrary", "arbitrary"),
    )(p, p0, logits, *sum_mats)
    return out[0:4], out[4:8]


def _hg_scan_args(b, k, q_ref, k_ref, o_ref, v_ref, z_ref, eg_ref):
    eg = jnp.stack([eg_ref[i, k, :, h * DH:(h + 1) * DH] for i, h in _pairs(b)], axis=0)
    return (_load_slabs(q_ref, b, k), _load_slabs(k_ref, b, k), _load_slabs(v_ref, b, k), eg, _load_slabs(o_ref, b, k),
            _load_slabs(z_ref, b, k))


def _hg_lead_args(b, q0_ref, k0_ref, o0_ref, p0_ref, eg0_ref):
    eg = jnp.stack([eg0_ref[:, h * DH:(h + 1) * DH] for _, h in _pairs(b)], axis=0)
    return (_lead_slabs(q0_ref[...], b), _lead_slabs(k0_ref[...], b), _lead_slabs(p0_ref[:, 2 * WIDTH:3 * WIDTH], b), eg,
            _lead_slabs(o0_ref[...], b), _lead_slabs(p0_ref[:, 3 * WIDTH:4 * WIDTH], b))


def _scan_specs(b, ng, reverse, chunks):
    group = (lambda i: ng - 1 - i) if reverse else (lambda i: i)
    slab = lambda lane_block: pl.BlockSpec((b, chunks * CHUNK, WIDTH), lambda i: (0, group(i), lane_block))
    per_chunk = lambda *tail: pl.BlockSpec((b, chunks) + tail, lambda i: (0, group(i)) + (0,) * len(tail))
    const = lambda a: pl.BlockSpec(a.shape, lambda i: (0,) * a.ndim)
    return slab, per_chunk, const


def run_scans(parts, nc, name):
    n_in = [len(p["args"]) for p in parts]
    n_out = [len(p["out_shape"]) for p in parts]
    n_scr = [len(p["scratch_shapes"]) for p in parts]

    def body(*refs):
        ins, outs, scr = refs[:sum(n_in)], refs[sum(n_in):sum(n_in) + sum(n_out)], refs[sum(n_in) + sum(n_out):]
        for i, part in enumerate(parts):
            part["body"](*ins[sum(n_in[:i]):sum(n_in[:i + 1])], *outs[sum(n_out[:i]):sum(n_out[:i + 1])],
                         *scr[sum(n_scr[:i]):sum(n_scr[:i + 1])])

    flat = lambda key: [v for p in parts for v in p[key]]
    out = pl.pallas_call(body, grid=(nc,), name=name, in_specs=flat("in_specs"), out_specs=flat("out_specs"),
                         out_shape=flat("out_shape"), scratch_shapes=flat("scratch_shapes"),
                         compiler_params=_cparams("arbitrary"))(*flat("args"))
    return [out[sum(n_out[:i]):sum(n_out[:i + 1])] for i in range(len(parts))]


def hg_scan_fwd(p, p0, local, lead, nw):
    b, seq, _ = p.shape
    q_in, k_out, o_intra, eg = local
    slab, per_chunk, const = _scan_specs(b, seq // (SCAN_CHUNKS_FWD * CHUNK), False, SCAN_CHUNKS_FWD)

    def body(q_ref, k_ref, o_ref, v_ref, z_ref, eg_ref, q0_ref, k0_ref, o0_ref, p0_ref, eg0_ref, nw_ref, y_ref, ss_ref, st):
        @pl.when(pl.program_id(0) == 0)
        def _():
            st[...] = hg_scan(*_hg_lead_args(b, q0_ref, k0_ref, o0_ref, p0_ref, eg0_ref), nw_ref[...], jnp.zeros(st.shape, F32))[1]

        s = st[...]
        for k in range(SCAN_CHUNKS_FWD):
            _save_states(ss_ref, s, b, k)
            y, s = hg_scan(*_hg_scan_args(b, k, q_ref, k_ref, o_ref, v_ref, z_ref, eg_ref), nw_ref[...], s)
            _store_slabs(y_ref, y, b, k)
        st[...] = s

    return dict(
        body=body, args=(q_in, k_out, o_intra, p, p, eg, lead[0], lead[1], lead[2], p0, lead[3], nw),
        in_specs=[slab(0), slab(0), slab(0), slab(2), slab(3), per_chunk(1, WIDTH)] + [const(a) for a in lead[0:3]]
        + [const(p0), const(lead[3]), const(nw)],
        out_specs=[slab(0), per_chunk(WIDTH, DH)],
        out_shape=[_sds((b, seq, WIDTH), MXU_DTYPE), _sds((b, seq // CHUNK, WIDTH, DH), MXU_DTYPE)],
        scratch_shapes=[pltpu.VMEM((b * HEADS, DH, DH), F32)])


def hg_scan_bwd(p, p0, local, lead, nw, ssave, dy):
    b, seq, _ = p.shape
    ng = seq // (SCAN_CHUNKS * CHUNK)
    q_in, k_out, o_intra, eg = local
    slab, per_chunk, const = _scan_specs(b, ng, True, SCAN_CHUNKS)

    def body(q_ref, k_ref, o_ref, v_ref, z_ref, eg_ref, q0_ref, k0_ref, o0_ref, p0_ref, eg0_ref, nw_ref, ss_ref, dy_ref,
             dq_ref, dk_ref, do_ref, dv_ref, dz_ref, deg_ref, dq0_ref, dk0_ref, do0_ref, dv0_ref, dz0_ref, deg0_ref, dnw_ref,
             dst):
        i = pl.program_id(0)

        @pl.when(i == 0)
        def _():
            dst[...] = jnp.zeros_like(dst)
            dnw_ref[...] = jnp.zeros_like(dnw_ref)

        ds = dst[...]
        for k in reversed(range(SCAN_CHUNKS)):
            args = _hg_scan_args(b, k, q_ref, k_ref, o_ref, v_ref, z_ref, eg_ref)
            _, vjp = jax.vjp(hg_scan, *args, nw_ref[...], _load_states(ss_ref, b, k))
            dq, dk, dv, deg, do, dz, dnw, ds = vjp((_load_slabs(dy_ref, b, k), ds))
            dnw_ref[...] += dnw
            for ref, val in ((dq_ref, dq), (dk_ref, dk), (do_ref, do), (dv_ref, dv), (dz_ref, dz)):
                _store_slabs(ref, val, b, k)
            for j in range(b):
                deg_ref[j, k] = _rows(deg, j)
        dst[...] = ds

        @pl.when(i == ng - 1)
        def _():
            args = _hg_lead_args(b, q0_ref, k0_ref, o0_ref, p0_ref, eg0_ref)
            _, vjp = jax.vjp(hg_scan, *args, nw_ref[...], jnp.zeros(dst.shape, F32))
            dq, dk, dv, deg, do, dz, dnw, _ = vjp((jnp.zeros((b * HEADS, CHUNK, DH), F32), ds))
            dnw_ref[...] += dnw
            for ref, val in ((dq0_ref, dq), (dk0_ref, dk), (do0_ref, do), (dv0_ref, dv), (dz0_ref, dz), (deg0_ref, deg)):
                ref[...] = _sum_rows(val, b)

    lead_out = [const(a) for a in lead[0:3]] + [const(lead[0]), const(lead[0]), const(lead[3])]
    return dict(
        body=body, args=(q_in, k_out, o_intra, p, p, eg, lead[0], lead[1], lead[2], p0, lead[3], nw, ssave, dy),
        in_specs=[slab(0), slab(0), slab(0), slab(2), slab(3), per_chunk(1, WIDTH)] + [const(a) for a in lead[0:3]]
        + [const(p0), const(lead[3]), const(nw), per_chunk(WIDTH, DH), slab(0)],
        out_specs=[slab(0)] * 5 + [per_chunk(1, WIDTH)] + lead_out + [const(nw)],
        out_shape=[_sds((b, seq, WIDTH))] * 3 + [_sds((b, seq, WIDTH), MXU_DTYPE)] * 2 + [_sds(eg.shape)]
        + [_sds((CHUNK, WIDTH))] * 5 + [_sds((1, WIDTH)), _sds(nw.shape)],
        scratch_shapes=[pltpu.VMEM((b * HEADS, DH, DH), F32)])


def _hg_local_vjp(sum_mats, p, logits, dq, dk, do, degs, dv, dz):
    _, vjp = jax.vjp(lambda p_, logits_: hg_local(p_, logits_, sum_mats), p, logits)
    dp, dlg = vjp((dq, dk, do, degs))
    return dp + jnp.concatenate([jnp.zeros((p.shape[0], 2 * WIDTH), F32), dv.astype(F32), dz.astype(F32)], axis=1), dlg


def hg_local_bwd(p, p0, logits, cot, cot0):
    b, seq, _ = p.shape
    rows = LOCAL_CHUNKS * CHUNK

    def body(p_ref, p0_ref, lg_ref, s_ref, st_ref, dq_ref, dk_ref, do_ref, dv_ref, dz_ref, deg_ref, dq0_ref, dk0_ref, do0_ref,
             dv0_ref, dz0_ref, deg0_ref, dp_ref, dp0_ref, dlg_ref):
        sum_mats = (s_ref[...], st_ref[...])

        @pl.when((pl.program_id(0) == 0) & (pl.program_id(1) == 0))
        def _():
            dp0, dlg_ref[...] = _hg_local_vjp(sum_mats, p0_ref[...], lg_ref[...], dq0_ref[...], dk0_ref[...], do0_ref[...],
                                              (deg0_ref[...],), dv0_ref[...], dz0_ref[...])
            dp0_ref[...] = dp0.astype(MXU_DTYPE)

        degs = tuple(deg_ref[c] for c in range(LOCAL_CHUNKS))
        dp, dlg = _hg_local_vjp(sum_mats, p_ref[...], lg_ref[...], dq_ref[...], dk_ref[...], do_ref[...], degs, dv_ref[...],
                                dz_ref[...])
        dp_ref[...] = dp.astype(MXU_DTYPE)
        dlg_ref[...] += dlg

    slab = pl.BlockSpec((None, rows, WIDTH), lambda s, g: (s, g, 0))
    wide = pl.BlockSpec((None, rows, 4 * WIDTH), lambda s, g: (s, g, 0))
    const = lambda a: pl.BlockSpec(a.shape, lambda s, g: (0, 0))
    sum_mats = _summation_matrices(_hg_sums, (HG_LEVELS + 1) * CHUNK)
    return pl.pallas_call(
        body, grid=(b, seq // rows), name="hgrn2_local_bwd",
        in_specs=[wide, const(p0), const(logits), const(sum_mats[0]), const(sum_mats[1]), slab, slab, slab, slab, slab,
                  pl.BlockSpec((None, LOCAL_CHUNKS, 1, WIDTH), lambda s, g: (s, g, 0, 0))] + [const(a) for a in cot0],
        out_specs=[wide, const(p0), const(logits)],
        out_shape=[_sds(p.shape, MXU_DTYPE), _sds(p0.shape, MXU_DTYPE), _sds(logits.shape)],
        compiler_params=_cparams("arbitrary", "arbitrary"),
    )(p, p0, logits, *sum_mats, *cot, *cot0)


def _halo_block(g):
    return jnp.maximum((LOCAL_CHUNKS * CHUNK // HALO) * g - 1, 0)


def _gd_window(g, p_ref, halo_ref, p0_ref):
    halo = jnp.where(g == 0, p0_ref[CHUNK - HALO:CHUNK, 0:QKV], halo_ref[...])
    return jnp.concatenate([halo, p_ref[:, 0:QKV]], axis=0)


def _lead_window(p0_ref):
    return jnp.concatenate([jnp.zeros((HALO, QKV), F32), p0_ref[:, 0:QKV]], axis=0)


def gd_local_fwd(p, p0, ab, ab0, cw, alog, dtb):
    b, seq, _ = p.shape
    rows = LOCAL_CHUNKS * CHUNK
    nreal = seq // CHUNK

    def body(p_ref, halo_ref, p0_ref, ab_ref, ab0_ref, cw_ref, al_ref, dt_ref, s_ref, st_ref, u_ref, w_ref, qe_ref, ke_ref,
             qk_ref, ea_ref, inv_ref, u0_ref, w0_ref, qe0_ref, ke0_ref, qk0_ref, ea0_ref, inv0_ref):
        sum_mats = (s_ref[...], st_ref[...])

        @pl.when((pl.program_id(0) == 0) & (pl.program_id(1) == 0))
        def _():
            (u0_ref[...], ww, qe, ke, qk0_ref[...], (ea0_ref[...],)), inv0_ref[...] = gd_local(
                _lead_window(p0_ref), ab0_ref[...], cw_ref[...], al_ref[...], dt_ref[...], sum_mats, inverse=_tri_y_impl)
            w0_ref[...], qe0_ref[...], ke0_ref[...] = ww.astype(MXU_DTYPE), qe.astype(MXU_DTYPE), ke.astype(MXU_DTYPE)

        (uu, ww, qe, ke, qk, eas), inv = gd_local(_gd_window(pl.program_id(1), p_ref, halo_ref, p0_ref), ab_ref[...],
                                                  cw_ref[...], al_ref[...], dt_ref[...], sum_mats, inverse=_tri_y_impl)
        u_ref[...], w_ref[...], qe_ref[...], ke_ref[...] = uu, ww.astype(MXU_DTYPE), qe.astype(MXU_DTYPE), ke.astype(MXU_DTYPE)
        for c in range(LOCAL_CHUNKS):
            qk_ref[c] = qk[c * HEADS * CHUNK:(c + 1) * HEADS * CHUNK]
            inv_ref[c] = inv[c * HEADS * CHUNK:(c + 1) * HEADS * CHUNK]
            ea_ref[c] = eas[c]

    const = lambda shape: pl.BlockSpec(shape, lambda s, g: (0, 0))
    slab = pl.BlockSpec((None, rows, WIDTH), lambda s, g: (s, g, 0))
    mats = pl.BlockSpec((None, LOCAL_CHUNKS, HEADS * CHUNK, CHUNK), lambda s, g: (s, g, 0, 0))
    lead_out = [_sds((CHUNK, WIDTH))] + [_sds((CHUNK, WIDTH), MXU_DTYPE)] * 3 + [_sds((HEADS * CHUNK, CHUNK)), _sds((1, AB_PAD)),
                                                                                _sds((HEADS * CHUNK, CHUNK))]
    sum_mats = _summation_matrices(_running_sum, CHUNK)
    out = pl.pallas_call(
        body, grid=(b, seq // rows), name="gdn_local",
        in_specs=[pl.BlockSpec((None, rows, 4 * WIDTH), lambda s, g: (s, g, 0)),
                  pl.BlockSpec((None, HALO, QKV), lambda s, g: (s, _halo_block(g), 0)), const(p0.shape),
                  pl.BlockSpec((None, rows, AB_PAD), lambda s, g: (s, g, 0)), const(ab0.shape), const(cw.shape),
                  const(alog.shape), const(dtb.shape), const(sum_mats[0].shape), const(sum_mats[1].shape)],
        out_specs=[slab] * 4 + [mats, pl.BlockSpec((None, LOCAL_CHUNKS, 1, AB_PAD), lambda s, g: (s, g, 0, 0)), mats]
        + [const(s.shape) for s in lead_out],
        out_shape=[_sds((b, seq, WIDTH))] + [_sds((b, seq, WIDTH), MXU_DTYPE)] * 3
        + [_sds((b, nreal, HEADS * CHUNK, CHUNK)), _sds((b, nreal, 1, AB_PAD)), _sds((b, nreal, HEADS * CHUNK, CHUNK))] + lead_out,
        compiler_params=_cparams("arbitrary", "arbitrary"),
    )(p, p, p0, ab, ab0, cw, alog, dtb, *sum_mats)
    return out[0:6], out[6], out[7:13], out[13]


def _gd_scan_args(b, k, u_ref, w_ref, qe_ref, ke_ref, qk_ref, ea_ref, z_ref):
    qk = jnp.stack([qk_ref[i, k, h * CHUNK:(h + 1) * CHUNK, :] for i, h in _pairs(b)], axis=0)
    ea = jnp.stack([ea_ref[i, k, :, h:h + 1] for i, h in _pairs(b)], axis=0)
    return (_load_slabs(u_ref, b, k), _load_slabs(w_ref, b, k), _load_slabs(qe_ref, b, k), _load_slabs(ke_ref, b, k), qk, ea,
            _load_slabs(z_ref, b, k))


def _gd_lead_args(b, u0_ref, w0_ref, qe0_ref, ke0_ref, qk0_ref, ea0_ref, p0_ref):
    qk = jnp.stack([qk0_ref[h * CHUNK:(h + 1) * CHUNK, :] for _, h in _pairs(b)], axis=0)
    ea = jnp.stack([ea0_ref[:, h:h + 1] for _, h in _pairs(b)], axis=0)
    return (_lead_slabs(u0_ref[...], b), _lead_slabs(w0_ref[...], b), _lead_slabs(qe0_ref[...], b), _lead_slabs(ke0_ref[...], b),
            qk, ea, _lead_slabs(p0_ref[:, QKV:QKV + WIDTH], b))


def gd_scan_fwd(p, p0, local, lead, nw):
    b, seq, _ = p.shape
    slab, per_chunk, const = _scan_specs(b, seq // (SCAN_CHUNKS_FWD * CHUNK), False, SCAN_CHUNKS_FWD)

    def body(u_ref, w_ref, qe_ref, ke_ref, qk_ref, ea_ref, z_ref, u0_ref, w0_ref, qe0_ref, ke0_ref, qk0_ref, ea0_ref, p0_ref,
             nw_ref, y_ref, ss_ref, st):
        @pl.when(pl.program_id(0) == 0)
        def _():
            lead_args = _gd_lead_args(b, u0_ref, w0_ref, qe0_ref, ke0_ref, qk0_ref, ea0_ref, p0_ref)
            st[...] = gd_scan(*lead_args, nw_ref[...], jnp.zeros(st.shape, F32))[1]

        s = st[...]
        for k in range(SCAN_CHUNKS_FWD):
            _save_states(ss_ref, s, b, k)
            y, s = gd_scan(*_gd_scan_args(b, k, u_ref, w_ref, qe_ref, ke_ref, qk_ref, ea_ref, z_ref), nw_ref[...], s)
            _store_slabs(y_ref, y, b, k)
        st[...] = s

    return dict(
        body=body, args=(*local, p, *lead, p0, nw),
        in_specs=[slab(0)] * 4 + [per_chunk(HEADS * CHUNK, CHUNK), per_chunk(1, AB_PAD), slab(3)] + [const(a) for a in lead]
        + [const(p0), const(nw)],
        out_specs=[slab(0), per_chunk(WIDTH, DH)],
        out_shape=[_sds((b, seq, WIDTH), MXU_DTYPE), _sds((b, seq // CHUNK, WIDTH, DH), MXU_DTYPE)],
        scratch_shapes=[pltpu.VMEM((b * HEADS, DH, DH), F32)])


def gd_scan_bwd(p, p0, local, lead, nw, ssave, dy):
    b, seq, _ = p.shape
    ng = seq // (SCAN_CHUNKS * CHUNK)
    slab, per_chunk, const = _scan_specs(b, ng, True, SCAN_CHUNKS)

    def body(u_ref, w_ref, qe_ref, ke_ref, qk_ref, ea_ref, z_ref, u0_ref, w0_ref, qe0_ref, ke0_ref, qk0_ref, ea0_ref, p0_ref,
             nw_ref, ss_ref, dy_ref, du_ref, dw_ref, dqe_ref, dke_ref, dqk_ref, dea_ref, dz_ref, du0_ref, dw0_ref, dqe0_ref,
             dke0_ref, dqk0_ref, dea0_ref, dz0_ref, dnw_ref, dst):
        i = pl.program_id(0)
        lane = lax.broadcasted_iota(jnp.int32, (1, AB_PAD), 1)

        def gate_rows(dea, j):
            return sum(jnp.where(lane == h, dea[j * HEADS + h], 0.0) for h in range(HEADS))

        def matrix_rows(dqk, j):
            return jnp.concatenate([dqk[j * HEADS + h] for h in range(HEADS)], axis=0)

        @pl.when(i == 0)
        def _():
            dst[...] = jnp.zeros_like(dst)
            dnw_ref[...] = jnp.zeros_like(dnw_ref)

        ds = dst[...]
        for k in reversed(range(SCAN_CHUNKS)):
            args = _gd_scan_args(b, k, u_ref, w_ref, qe_ref, ke_ref, qk_ref, ea_ref, z_ref)
            _, vjp = jax.vjp(gd_scan, *args, nw_ref[...], _load_states(ss_ref, b, k))
            du, dw, dqe, dke, dqk, dea, dz, dnw, ds = vjp((_load_slabs(dy_ref, b, k), ds))
            dnw_ref[...] += dnw
            for ref, val in ((du_ref, du), (dw_ref, dw), (dqe_ref, dqe), (dke_ref, dke), (dz_ref, dz)):
                _store_slabs(ref, val, b, k)
            for j in range(b):
                dqk_ref[j, k] = matrix_rows(dqk, j)
                dea_ref[j, k] = gate_rows(dea, j)
        dst[...] = ds

        @pl.when(i == ng - 1)
        def _():
            args = _gd_lead_args(b, u0_ref, w0_ref, qe0_ref, ke0_ref, qk0_ref, ea0_ref, p0_ref)
            _, vjp = jax.vjp(gd_scan, *args, nw_ref[...], jnp.zeros(dst.shape, F32))
            du, dw, dqe, dke, dqk, dea, dz, dnw, _ = vjp((jnp.zeros((b * HEADS, CHUNK, DH), F32), ds))
            dnw_ref[...] += dnw
            for ref, val in ((du0_ref, du), (dw0_ref, dw), (dqe0_ref, dqe), (dke0_ref, dke), (dz0_ref, dz)):
                ref[...] = _sum_rows(val, b)
            dqk0_ref[...] = sum((matrix_rows(dqk, j) for j in range(1, b)), matrix_rows(dqk, 0))
            dea0_ref[...] = sum((gate_rows(dea, j) for j in range(1, b)), gate_rows(dea, 0))

    uu, ww, qe, ke, qk, ea = local
    return dict(
        body=body, args=(*local, p, *lead, p0, nw, ssave, dy),
        in_specs=[slab(0)] * 4 + [per_chunk(HEADS * CHUNK, CHUNK), per_chunk(1, AB_PAD), slab(3)] + [const(a) for a in lead]
        + [const(p0), const(nw), per_chunk(WIDTH, DH), slab(0)],
        out_specs=[slab(0)] * 4 + [per_chunk(HEADS * CHUNK, CHUNK), per_chunk(1, AB_PAD), slab(0)] + [const(a) for a in lead]
        + [const(lead[0]), const(nw)],
        out_shape=[_sds((b, seq, WIDTH))] * 4 + [_sds(qk.shape), _sds(ea.shape), _sds((b, seq, WIDTH), MXU_DTYPE)]
        + [_sds(a.shape) for a in lead] + [_sds(lead[0].shape), _sds(nw.shape)],
        scratch_shapes=[pltpu.VMEM((b * HEADS, DH, DH), F32)])


def _gd_local_vjp(sum_mats, inv_rows, xx, ab, cw, alog, dtb):
    nb = ab.shape[0] // CHUNK
    inv = jnp.stack([inv_rows[g * CHUNK:(g + 1) * CHUNK] for g in range(nb * HEADS)], axis=0)
    _, vjp, _ = jax.vjp(lambda *a: gd_local(*a, sum_mats, inverse=_saved_inverse(inv)), xx, ab, cw, alog, dtb, has_aux=True)
    return vjp


def gd_local_bwd(p, p0, ab, ab0, cw, alog, dtb, inv, inv0, cot, dz, cot0, dz0):
    b, seq, _ = p.shape
    rows = LOCAL_CHUNKS * CHUNK
    ng = seq // rows
    du, dw, dqe, dke, dqk, dea = cot

    def body(p_ref, halo_ref, p0_ref, ab_ref, ab0_ref, cw_ref, al_ref, dt_ref, s_ref, st_ref, inv_ref, inv0_ref, du_ref, dw_ref,
             dqe_ref, dke_ref, dqk_ref, dea_ref, dz_ref, du0_ref, dw0_ref, dqe0_ref, dke0_ref, dqk0_ref, dea0_ref, dz0_ref,
             dp_ref, dab_ref, dp0_ref, dab0_ref, dcw_ref, dal_ref, ddt_ref, dhalo, dtail):
        s, i = pl.program_id(0), pl.program_id(1)
        g = ng - 1 - i
        sum_mats = (s_ref[...], st_ref[...])

        @pl.when(i == 0)
        def _():
            dhalo[...] = jnp.zeros_like(dhalo)

        @pl.when((s == 0) & (i == 0))
        def _():
            dtail[...] = jnp.zeros_like(dtail)
            dcw_ref[...] = jnp.zeros_like(dcw_ref)
            dal_ref[...] = jnp.zeros_like(dal_ref)
            ddt_ref[...] = jnp.zeros_like(ddt_ref)

        def finish(dxx, dab, dcw, dal, ddt, before, n, dz_val, dp_out, dab_out):
            dqkv = dxx[HALO:HALO + n] + jnp.concatenate([jnp.zeros((n - HALO, QKV), F32), before], axis=0)
            dp_out[...] = jnp.concatenate([dqkv.astype(MXU_DTYPE), dz_val.astype(MXU_DTYPE)], axis=1)
            dab_out[...] = dab.astype(MXU_DTYPE)
            dcw_ref[...] += dcw
            dal_ref[...] += dal
            ddt_ref[...] += ddt

        inv_rows = jnp.concatenate([inv_ref[c] for c in range(LOCAL_CHUNKS)], axis=0)
        vjp = _gd_local_vjp(sum_mats, inv_rows, _gd_window(g, p_ref, halo_ref, p0_ref), ab_ref[...], cw_ref[...], al_ref[...],
                            dt_ref[...])
        dqk_all = jnp.concatenate([dqk_ref[c] for c in range(LOCAL_CHUNKS)], axis=0)
        deas = tuple(dea_ref[c] for c in range(LOCAL_CHUNKS))
        grads = vjp((du_ref[...], dw_ref[...], dqe_ref[...], dke_ref[...], dqk_all, deas))
        finish(*grads, dhalo[...], rows, dz_ref[...], dp_ref, dab_ref)
        dhalo[...] = grads[0][0:HALO]

        @pl.when(g == 0)
        def _():
            dtail[...] += grads[0][0:HALO]

        @pl.when((s == b - 1) & (g == 0))
        def _():
            vjp0 = _gd_local_vjp(sum_mats, inv0_ref[...], _lead_window(p0_ref), ab0_ref[...], cw_ref[...], al_ref[...],
                                 dt_ref[...])
            grads0 = vjp0((du0_ref[...], dw0_ref[...], dqe0_ref[...], dke0_ref[...], dqk0_ref[...], (dea0_ref[...],)))
            finish(*grads0, dtail[...], CHUNK, dz0_ref[...], dp0_ref, dab0_ref)

    rg = lambda i: ng - 1 - i
    const = lambda a: pl.BlockSpec(a.shape, lambda s, i: (0, 0))
    slab = pl.BlockSpec((None, rows, WIDTH), lambda s, i: (s, rg(i), 0))
    wide = pl.BlockSpec((None, rows, 4 * WIDTH), lambda s, i: (s, rg(i), 0))
    gates = pl.BlockSpec((None, rows, AB_PAD), lambda s, i: (s, rg(i), 0))
    mats = pl.BlockSpec((None, LOCAL_CHUNKS, HEADS * CHUNK, CHUNK), lambda s, i: (s, rg(i), 0, 0))
    sum_mats = _summation_matrices(_running_sum, CHUNK)
    return pl.pallas_call(
        body, grid=(b, ng), name="gdn_local_bwd",
        in_specs=[wide, pl.BlockSpec((None, HALO, QKV), lambda s, i: (s, _halo_block(rg(i)), 0)), const(p0), gates, const(ab0),
                  const(cw), const(alog), const(dtb), const(sum_mats[0]), const(sum_mats[1]), mats, const(inv0), slab, slab,
                  slab, slab, mats,
                  pl.BlockSpec((None, LOCAL_CHUNKS, 1, AB_PAD), lambda s, i: (s, rg(i), 0, 0)), slab]
        + [const(a) for a in cot0] + [const(dz0)],
        out_specs=[wide, gates, const(p0), const(ab0), const(cw), const(alog), const(dtb)],
        out_shape=[_sds(p.shape, MXU_DTYPE), _sds(ab.shape, MXU_DTYPE), _sds(p0.shape, MXU_DTYPE), _sds(ab0.shape, MXU_DTYPE),
                   _sds(cw.shape), _sds(alog.shape), _sds(dtb.shape)],
        scratch_shapes=[pltpu.VMEM((HALO, QKV), F32), pltpu.VMEM((HALO, QKV), F32)],
        compiler_params=_cparams("arbitrary", "arbitrary"),
    )(p, p, p0, ab, ab0, cw, alog, dtb, *sum_mats, inv, inv0, du, dw, dqe, dke, dqk, dea, dz, *cot0, dz0)


def _position():
    return lax.axis_index("x"), lax.axis_index("y"), lax.axis_index("c")


EXCHANGE_COPIES = 10


def _exchange_blocks(bufs, send_sems, recv_sems):
    x, y, c = _position()
    here, x_nbr, y_nbr, diag = (x, y), (1 - x, y), (x, 1 - y), (1 - x, 1 - y)
    sibling = (x, y, 1 - c)
    me = (x, y, c)
    n = range(len(bufs))

    def rows(a, chip, core, half=None):
        block = bufs[a].at[4 * chip[0] + 2 * chip[1] + core]
        if half is None:
            return block
        total = bufs[a].shape[1]
        tile = 8 * (4 // jnp.dtype(bufs[a].dtype).itemsize)
        split = total // 2 // tile * tile
        return block.at[pl.ds(0, split)] if half == 0 else block.at[pl.ds(split, total - split)]

    def copy(a, k, region, to):
        return pltpu.make_async_remote_copy(src_ref=region, dst_ref=region, send_sem=send_sems.at[a * EXCHANGE_COPIES + k],
                                            recv_sem=recv_sems.at[a * EXCHANGE_COPIES + k], device_id=to, device_id_type=MESH)

    sent = [copy(a, 0, rows(a, here, c), sibling) for a in n]
    sent += [cp for a in n for cp in (copy(a, 1, rows(a, here, c, 0), (*x_nbr, c)), copy(a, 4, rows(a, here, c, 1), (*y_nbr, c)))]
    sent += [cp for a in n for cp in (copy(a, 2, rows(a, here, c, 1), (*x_nbr, c)), copy(a, 3, rows(a, here, c, 0), (*y_nbr, c)))]
    for cp in sent:
        cp.start()

    def after(arrivals, a, k, region, to):
        for cp in arrivals:
            cp.wait_recv()
        sent.append(copy(a, k, region, to))
        sent[-1].start()

    for a in n:
        after([copy(a, 1, rows(a, x_nbr, c, 0), me)], a, 5, rows(a, x_nbr, c, 0), (*y_nbr, c))
        after([copy(a, 4, rows(a, y_nbr, c, 1), me)], a, 6, rows(a, y_nbr, c, 1), (*x_nbr, c))
    for a in n:
        after([copy(a, 2, rows(a, x_nbr, c, 1), me)], a, 7, rows(a, x_nbr, c), sibling)
        after([copy(a, 3, rows(a, y_nbr, c, 0), me)], a, 8, rows(a, y_nbr, c), sibling)
    for a in n:
        after([copy(a, 5, rows(a, diag, c, 0), me), copy(a, 6, rows(a, diag, c, 1), me)], a, 9, rows(a, diag, c), sibling)
    for a in n:
        copy(a, 0, rows(a, here, 1 - c), me).wait_recv()
        for k, chip in ((7, x_nbr), (8, y_nbr), (9, diag)):
            copy(a, k, rows(a, chip, 1 - c), me).wait_recv()
    for cp in sent:
        cp.wait_send()


def _exchange_sems(n_bufs):
    return [pltpu.SemaphoreType.DMA((n_bufs * EXCHANGE_COPIES,)), pltpu.SemaphoreType.DMA((n_bufs * EXCHANGE_COPIES,))]


def gather_weights(w_in_t, w_out, small, pad_rows):
    rows, _, cols = w_in_t.shape
    buf_rows = -(-rows // ROW_TILE_BF16) * ROW_TILE_BF16

    def body(wi_ref, wo_ref, sm_ref, wi_out, wo_out, sm_out, wi_buf, send_sems, recv_sems):
        x, y, c = _position()
        me = 4 * x + 2 * y + c
        wi_buf[me, pl.ds(0, rows), :] = wi_ref[:, 0, :].astype(MXU_DTYPE)
        wi_buf[me, pl.ds(rows, buf_rows - rows), :] = jnp.zeros((buf_rows - rows, cols), MXU_DTYPE)
        wo_out[me] = wo_ref[...].astype(MXU_DTYPE)
        sm_out[me] = sm_ref[...]
        _exchange_blocks([wi_buf, wo_out, sm_out], send_sems, recv_sems)
        for d in range(N_DEV):
            wi_out[pl.ds(d * rows, rows), :] = wi_buf[d, pl.ds(0, rows), :]
        wi_out[pl.ds(N_DEV * rows, pad_rows), :] = jnp.zeros((pad_rows, cols), MXU_DTYPE)

    return pl.pallas_call(
        body, name="gather_weights", in_specs=[VMEM_SPEC] * 3, out_specs=[VMEM_SPEC] * 3,
        out_shape=[jax.ShapeDtypeStruct((N_DEV * rows + pad_rows, cols), MXU_DTYPE),
                   jax.ShapeDtypeStruct((N_DEV,) + w_out.shape, MXU_DTYPE), jax.ShapeDtypeStruct((N_DEV,) + small.shape, F32)],
        scratch_shapes=[pltpu.VMEM((N_DEV, buf_rows, cols), MXU_DTYPE)] + _exchange_sems(3),
        compiler_params=pltpu.CompilerParams(vmem_limit_bytes=VMEM_LIMIT))(w_in_t, w_out, small)


HOPS = 6


def reduce_gradients(tensors, small, name):
    n_t = len(tensors)
    arrays = [a for parts, _ in tensors for a, _ in parts]
    first_array = [sum(len(parts) for parts, _ in tensors[:t]) for t in range(n_t)]

    def pieces(t, j):
        parts, block_rows = tensors[t]
        out, base = [], 0
        for pi, (_, valid) in enumerate(parts):
            lo, hi = max(j * block_rows, base), min((j + 1) * block_rows, base + valid)
            if lo < hi:
                out.append((first_array[t] + pi, lo - base, lo - j * block_rows, hi - lo))
            base += valid
        return out

    def body(*refs):
        n_a = len(arrays)
        in_refs, small_ref = refs[:n_a], refs[n_a]
        out_refs, small_sum = refs[n_a + 1:n_a + 1 + n_t], refs[n_a + 1 + n_t]
        bufs, small_buf = refs[n_a + 2 + n_t:n_a + 2 + 5 * n_t], refs[n_a + 2 + 5 * n_t]
        s1_sems, r1_sems, s2_sems, r2_sems, small_send, small_recv = refs[n_a + 3 + 5 * n_t:]
        x, y, c = _position()
        chip = 2 * x + y

        def put(t, dst, j, add=None):
            for ai, src_row, dst_row, size in pieces(t, j):
                v = in_refs[ai][pl.ds(src_row, size), :]
                if add is not None:
                    v = v + add[pl.ds(dst_row, size), :].astype(F32)
                dst[pl.ds(dst_row, size), :] = v.astype(dst.dtype)

        def swap(t, k):
            send1, recv1 = bufs[4 * t], bufs[4 * t + 1]
            return pltpu.make_async_remote_copy(src_ref=send1.at[k], dst_ref=recv1.at[k], send_sem=s1_sems.at[4 * t + k],
                                                recv_sem=r1_sems.at[4 * t + k], device_id=(x, y, 1 - c), device_id_type=MESH)

        to_x, to_y, to_diag = 2 * (1 - x) + y, 2 * x + (1 - y), 2 * (1 - x) + (1 - y)
        x_dev, y_dev = (1 - x, y, c), (x, 1 - y, c)

        def half(ref, h):
            total = ref.shape[0]
            split = total // 2 // ROW_TILE_BF16 * ROW_TILE_BF16
            return ref.at[pl.ds(0, split)] if h == 0 else ref.at[pl.ds(split, total - split)]

        def hop(t, copy_id, src, dst, to):
            return pltpu.make_async_remote_copy(src_ref=src, dst_ref=dst, send_sem=s2_sems.at[HOPS * t + copy_id],
                                                recv_sem=r2_sems.at[HOPS * t + copy_id], device_id=to, device_id_type=MESH)

        def hops(t):
            send2, landing = bufs[4 * t + 2], bufs[4 * t + 3]
            return [hop(t, 0, half(send2.at[to_diag], 0), half(landing.at[0], 0), x_dev),
                    hop(t, 1, half(send2.at[to_diag], 1), half(landing.at[0], 1), y_dev),
                    hop(t, 2, half(send2.at[to_x], 0), half(landing.at[1], 0), x_dev),
                    hop(t, 3, half(send2.at[to_y], 1), half(landing.at[2], 1), y_dev),
                    hop(t, 4, half(send2.at[to_x], 1), half(landing.at[1], 1), x_dev),
                    hop(t, 5, half(send2.at[to_y], 0), half(landing.at[2], 0), y_dev)]

        def add_relay(t, slot, h):
            dst, src = half(bufs[4 * t + 2].at[slot], h), half(bufs[4 * t + 3].at[0], h)
            dst[...] = (dst[...].astype(F32) + src[...].astype(F32)).astype(dst.dtype)

        for t in range(n_t):
            send2 = bufs[4 * t + 2]
            pad = send2.shape[1] - tensors[t][1]
            if pad:
                send2[:, pl.ds(tensors[t][1], pad), :] = jnp.zeros((4, pad, send2.shape[2]), send2.dtype)
            for j in range(N_DEV):
                @pl.when((j & 1) != c)
                def _():
                    put(t, bufs[4 * t].at[j >> 1], j)
            for k in range(4):
                swap(t, k).start()

        small_buf[4 * x + 2 * y + c] = small_ref[...]
        _exchange_blocks([small_buf], small_send, small_recv)
        total = small_buf[0]
        for d in range(1, N_DEV):
            total = total + small_buf[d]
        small_sum[...] = total

        for t in range(n_t):
            recv1 = bufs[4 * t + 1]
            for k in range(4):
                swap(t, k).wait_recv()
                for j in (2 * k, 2 * k + 1):
                    @pl.when(((j & 1) == c) & (k != chip))
                    def _():
                        put(t, bufs[4 * t + 2].at[k], j, add=recv1.at[k])

                    @pl.when(((j & 1) == c) & (k == chip))
                    def _():
                        put(t, out_refs[t], j, add=recv1.at[k])
            for cp in hops(t)[0:4]:
                cp.start()

        for t in range(n_t):
            cps = hops(t)
            cps[0].wait_recv()
            add_relay(t, to_y, 0)
            cps[5].start()
            cps[1].wait_recv()
            add_relay(t, to_x, 1)
            cps[4].start()

        for t in range(n_t):
            cps, rows = hops(t), tensors[t][1]
            for first, second, slot in ((cps[2], cps[4], 1), (cps[3], cps[5], 2)):
                first.wait_recv()
                second.wait_recv()
                out_refs[t][...] += bufs[4 * t + 3][slot, pl.ds(0, rows), :].astype(F32)

        for t in range(n_t):
            for cp in hops(t):
                cp.wait_send()
            for k in range(4):
                swap(t, k).wait_send()

    scratch, out_shape = [], []
    for parts, block_rows in tensors:
        cols = parts[0][0].shape[1]
        tiled_rows = -(-block_rows // ROW_TILE_BF16) * ROW_TILE_BF16
        scratch += [pltpu.VMEM((4, block_rows, cols), MXU_DTYPE)] * 2
        scratch += [pltpu.VMEM((4, tiled_rows, cols), MXU_DTYPE), pltpu.VMEM((3, tiled_rows, cols), MXU_DTYPE)]
        out_shape.append(jax.ShapeDtypeStruct((block_rows, cols), F32))
    out_shape.append(jax.ShapeDtypeStruct(small.shape, F32))
    scratch += [pltpu.VMEM((N_DEV,) + small.shape, F32)] + [pltpu.SemaphoreType.DMA((4 * n_t,))] * 2
    scratch += [pltpu.SemaphoreType.DMA((HOPS * n_t,))] * 2 + _exchange_sems(1)
    return pl.pallas_call(
        body, name=name, in_specs=[VMEM_SPEC] * (len(arrays) + 1), out_specs=[VMEM_SPEC] * (n_t + 1), out_shape=out_shape,
        scratch_shapes=scratch, compiler_params=pltpu.CompilerParams(vmem_limit_bytes=VMEM_LIMIT),
    )(*arrays, small)


def _adamw_step(w, g, m, v):
    mn = ADAM_B1 * m + (1.0 - ADAM_B1) * g
    vn = ADAM_B2 * v + (1.0 - ADAM_B2) * jnp.square(g)
    m_hat = mn / (1.0 - ADAM_B1 ** ADAM_STEP)
    v_hat = vn / (1.0 - ADAM_B2 ** ADAM_STEP)
    return -ADAM_LR * (m_hat / (jnp.sqrt(v_hat) + ADAM_EPS) + ADAM_WD * w), mn, vn


def adamw_small(packed, first_rows, ws, gs, ms, vs):
    k = len(ws)
    given = [g for g in gs if g is not None]

    def body(*refs):
        packed_ref, w_refs, m_refs, v_refs = refs[0], refs[1:1 + k], refs[1 + k:1 + 2 * k], refs[1 + 2 * k:1 + 3 * k]
        g_refs, outs = iter(refs[1 + 3 * k:1 + 3 * k + len(given)]), refs[1 + 3 * k + len(given):]
        for i in range(k):
            rows, cols = w_refs[i].shape
            g = next(g_refs)[...] if gs[i] is not None else packed_ref[first_rows[i]:first_rows[i] + rows, 0:cols]
            outs[4 * i][...] = g
            outs[4 * i + 1][...], outs[4 * i + 2][...], outs[4 * i + 3][...] = _adamw_step(w_refs[i][...], g, m_refs[i][...],
                                                                                          v_refs[i][...])

    n_in = 1 + 3 * k + len(given)
    out = pl.pallas_call(body, name="adamw_small", in_specs=[VMEM_SPEC] * n_in, out_specs=[VMEM_SPEC] * (4 * k),
                         out_shape=[jax.ShapeDtypeStruct(w.shape, F32) for w in ws for _ in range(4)],
                         compiler_params=pltpu.CompilerParams(vmem_limit_bytes=VMEM_LIMIT))(packed, *ws, *ms, *vs, *given)
    return [out[4 * i:4 * i + 4] for i in range(k)]


def adamw_w_in(w, g_t, m, v):
    def body(w_ref, g_ref, m_ref, v_ref, go_ref, d_ref, nm_ref, nv_ref):
        g = g_ref[...]
        go_ref[:, 0, :] = g
        d_ref[:, 0, :], nm_ref[:, 0, :], nv_ref[:, 0, :] = _adamw_step(w_ref[:, 0, :], g, m_ref[:, 0, :], v_ref[:, 0, :])

    return pl.pallas_call(body, name="adamw_w_in", in_specs=[VMEM_SPEC] * 4, out_specs=[VMEM_SPEC] * 4,
                          out_shape=[jax.ShapeDtypeStruct(w.shape, F32)] * 4,
                          compiler_params=pltpu.CompilerParams(vmem_limit_bytes=VMEM_LIMIT))(w, g_t, m, v)


def _pad_rows(a, rows=8):
    return jnp.pad(a, ((0, rows - a.shape[0]), (0, 0)))


def _pad_lanes(a, lanes=128):
    return jnp.pad(a, ((0, 0), (0, lanes - a.shape[1])))


def kernel(x, meta_tokens, norm_w, w_in, conv_w, hg_lb_logits, hg_norm_w, gdn_A_log, gdn_dt_bias, gdn_norm_w, w_out, final_norm_w, loss_target, m_meta_tokens, m_norm_w, m_w_in, m_conv_w, m_hg_lb_logits, m_hg_norm_w, m_gdn_A_log, m_gdn_dt_bias, m_gdn_norm_w, m_w_out, m_final_norm_w, v_meta_tokens, v_norm_w, v_w_in, v_conv_w, v_hg_lb_logits, v_hg_norm_w, v_gdn_A_log, v_gdn_dt_bias, v_gdn_norm_w, v_w_out, v_final_norm_w):
    b, seq, _ = x.shape
    n = b * seq
    dev = 4 * lax.axis_index("x") + 2 * lax.axis_index("y") + lax.axis_index("c")
    col_shard = IN_COLS // N_DEV

    small_w = jnp.concatenate([_pad_lanes(meta_tokens, 256), _pad_rows(_pad_lanes(conv_w[0], 256))], axis=0)
    w_t, w_out_g, small_g = gather_weights(jnp.transpose(w_in, (2, 0, 1)), w_out[0], small_w, AB_PAD - 2 * HEADS)
    meta_g = small_g[:, 0:N_META, 0:D_MODEL // N_DEV]
    conv_g = small_g[:, N_META:N_META + CONV_TAPS, 0:QKV // N_DEV]
    w_out_full = w_out_g.reshape(2 * WIDTH, D_MODEL)
    cw = jnp.transpose(conv_g, (1, 0, 2)).reshape(CONV_TAPS, QKV)
    meta = jnp.transpose(meta_g, (1, 0, 2)).reshape(N_META, D_MODEL)
    alog = _pad_lanes(gdn_A_log)
    dtb = _pad_lanes(gdn_dt_bias)
    fw = final_norm_w.reshape(1, D_MODEL)

    h0 = jnp.concatenate([jnp.zeros((CHUNK - N_META, D_MODEL), F32), meta], axis=0)
    x2 = x.reshape(n, D_MODEL)
    phg, pgd, pab, phg0, pgd0, pab0, u0 = in_proj(x2, h0, norm_w, w_t)
    phg3, pgd3, pab3 = phg.reshape(b, seq, 4 * WIDTH), pgd.reshape(b, seq, 4 * WIDTH), pab.reshape(b, seq, AB_PAD)
    hg_loc, hg_lead = hg_local_fwd(phg3, phg0, hg_lb_logits)
    gd_loc, gd_inv, gd_lead, gd_inv0 = gd_local_fwd(pgd3, pgd0, pab3, pab0, cw, alog, dtb)
    (y_hg, s_hg), (y_gd, s_gd) = run_scans([hg_scan_fwd(phg3, phg0, hg_loc, hg_lead, hg_norm_w),
                                            gd_scan_fwd(pgd3, pgd0, gd_loc, gd_lead, gdn_norm_w)],
                                           seq // (SCAN_CHUNKS_FWD * CHUNK), "scans")

    dh2, dy_hg, dy_gd, g_w_out, loss_part, g_fw = out_proj_loss(
        x2, loss_target.reshape(n, D_MODEL), y_hg.reshape(n, WIDTH), y_gd.reshape(n, WIDTH), w_out_full, fw)

    hb, gb = run_scans([hg_scan_bwd(phg3, phg0, hg_loc, hg_lead, hg_norm_w, s_hg, dy_hg.reshape(b, seq, WIDTH)),
                        gd_scan_bwd(pgd3, pgd0, gd_loc, gd_lead, gdn_norm_w, s_gd, dy_gd.reshape(b, seq, WIDTH))],
                       seq // (SCAN_CHUNKS * CHUNK), "scans_bwd")
    dphg, dphg0, g_lb = hg_local_bwd(phg3, phg0, hg_lb_logits, hb[0:6], hb[6:12])
    g_hg_nw = hb[12]
    dpgd, dpab, dpgd0, dpab0, g_cw, g_alog, g_dtb = gd_local_bwd(pgd3, pgd0, pab3, pab0, cw, alog, dtb, gd_inv, gd_inv0,
                                                                 gb[0:6], gb[6], gb[7:13], gb[13])
    g_gd_nw = gb[14]
    dphg, dpgd, dpab = dphg.reshape(n, 4 * WIDTH), dpgd.reshape(n, 4 * WIDTH), dpab.reshape(n, AB_PAD)

    grad_x, dh0, g_nw, g_w_hg, g_w_gd, g_w_ab = in_proj_bwd(dphg, dpgd, dpab, w_t, x2, dh2, norm_w, h0, u0, dphg0, dpgd0, dpab0)

    small = jnp.concatenate([
        g_nw.reshape(8, 128), g_lb.reshape(8, 128), _pad_rows(g_hg_nw), _pad_rows(g_alog), _pad_rows(g_dtb), _pad_rows(g_gd_nw),
        g_fw.reshape(8, 128), g_cw.reshape(48, 128),
        dh0[CHUNK - N_META:CHUNK].reshape(128, 128), loss_part], axis=0)
    g_w_in_t, g_w_out, small = reduce_gradients(
        [([(g_w_hg, 4 * WIDTH), (g_w_gd, 4 * WIDTH), (g_w_ab, 2 * HEADS)], col_shard),
         ([(g_w_out, 2 * WIDTH)], (2 * WIDTH) // N_DEV)], small, "reduce_gradients")
    g_cw_full = small[56:104].reshape(CONV_TAPS, QKV)
    g_meta_full = small[104:232].reshape(N_META, D_MODEL)
    loss = small[232, 0]
    g_conv = lax.dynamic_slice_in_dim(g_cw_full, dev * (QKV // N_DEV), QKV // N_DEV, axis=1)
    g_meta = lax.dynamic_slice_in_dim(g_meta_full, dev * (D_MODEL // N_DEV), D_MODEL // N_DEV, axis=1)

    names = ["meta_tokens", "norm_w", "w_in", "conv_w", "hg_lb_logits", "hg_norm_w", "gdn_A_log", "gdn_dt_bias",
             "gdn_norm_w", "w_out", "final_norm_w"]
    weights = [meta_tokens, norm_w, w_in, conv_w, hg_lb_logits, hg_norm_w, gdn_A_log, gdn_dt_bias, gdn_norm_w, w_out,
               final_norm_w]
    moms = [m_meta_tokens, m_norm_w, m_w_in, m_conv_w, m_hg_lb_logits, m_hg_norm_w, m_gdn_A_log, m_gdn_dt_bias,
            m_gdn_norm_w, m_w_out, m_final_norm_w]
    vars_ = [v_meta_tokens, v_norm_w, v_w_in, v_conv_w, v_hg_lb_logits, v_hg_norm_w, v_gdn_A_log, v_gdn_dt_bias,
             v_gdn_norm_w, v_w_out, v_final_norm_w]
    gradient = [g_meta, 0, None, g_conv, 8, 16, 24, 32, 40, g_w_out, 48]
    shape2d = [g_meta.shape, (8, 128), None, g_conv.shape, (8, 128), (1, DH), (1, HEADS), (1, HEADS), (1, DH), g_w_out.shape,
               (8, 128)]
    i_w_in = names.index("w_in")
    others = [i for i in range(len(names)) if i != i_w_in]
    in_rows = lambda i: isinstance(gradient[i], int)
    stepped = adamw_small(small, [gradient[i] if in_rows(i) else None for i in others],
                          [weights[i].reshape(shape2d[i]) for i in others], [None if in_rows(i) else gradient[i] for i in others],
                          [moms[i].reshape(shape2d[i]) for i in others], [vars_[i].reshape(shape2d[i]) for i in others])
    results = {i: [a.reshape(weights[i].shape) for a in stepped[j]] for j, i in enumerate(others)}
    to3, back = (lambda a: jnp.transpose(a, (2, 0, 1))), (lambda a: jnp.transpose(a, (1, 2, 0)))
    results[i_w_in] = [back(a) for a in adamw_w_in(to3(w_in), g_w_in_t, to3(m_w_in), to3(v_w_in))]
    grads, deltas, new_ms, new_vs = zip(*(results[i] for i in range(len(names))))
    return (loss, grad_x.reshape(x.shape), *grads, *deltas, *new_ms, *new_vs)
```

```python
import jax
import jax.numpy as jnp
import numpy as np
from jax import lax
from jax.experimental import pallas as pl
from jax.experimental.pallas import tpu as pltpu

F32 = jnp.float32
BF16 = jnp.bfloat16
MXU_DTYPE = BF16

D_MODEL = 1024
N_META = 16
CHUNK = 64
SUB = 16
ROW_TILE_BF16 = 16
HEADS = 4
DH = 128
WIDTH = HEADS * DH
QKV = 3 * WIDTH
CONV_TAPS = 4
HALO = 8
EPS = 1e-6
IN_COLS = 4 * WIDTH + 4 * WIDTH + 2 * HEADS
AB_PAD = 128
N_DEV = 8
LOCAL_CHUNKS = 4
SCAN_CHUNKS_FWD = 4
SCAN_CHUNKS = 2
VMEM_LIMIT = 56 * 1024 * 1024
VMEM_LIMIT_LARGE = 60 * 1024 * 1024

ADAM_LR = 0.001
ADAM_B1 = 0.9
ADAM_B2 = 0.999
ADAM_EPS = 1e-08
ADAM_WD = 0.01
ADAM_STEP = 10

VMEM_SPEC = pl.BlockSpec(memory_space=pltpu.VMEM)
MESH = pl.DeviceIdType.MESH


def _mm_tn(a, b):
    return lax.dot_general(a.astype(MXU_DTYPE), b.astype(MXU_DTYPE), (((0,), (0,)), ((), ())), preferred_element_type=F32)


def _nn(a, b):
    return lax.dot_general(a.astype(MXU_DTYPE), b.astype(MXU_DTYPE), (((2,), (1,)), ((0,), (0,))), preferred_element_type=F32)


def _nt(a, b):
    return lax.dot_general(a.astype(MXU_DTYPE), b.astype(MXU_DTYPE), (((2,), (2,)), ((0,), (0,))), preferred_element_type=F32)


def _t(a):
    return jnp.swapaxes(a, 1, 2)


@jax.custom_vjp
def _bmm(a, b):
    return _nn(a, b)


_bmm.defvjp(lambda a, b: (_nn(a, b), (a, b)), lambda saved, d: (_nt(d, saved[1]), _nn(_t(saved[0]), d)))


@jax.custom_vjp
def _bmm_nt(a, b):
    return _nt(a, b)


_bmm_nt.defvjp(lambda a, b: (_nt(a, b), (a, b)), lambda saved, d: (_nn(d, saved[1]), _nn(_t(d), saved[0])))


@jax.custom_vjp
def _bmm_tn(a, b):
    return _nn(_t(a), b)


_bmm_tn.defvjp(lambda a, b: (_nn(_t(a), b), (a, b)), lambda saved, d: (_nt(saved[1], d), _nn(saved[0], d)))


def _iota2(n, m):
    return lax.broadcasted_iota(jnp.int32, (n, m), 0), lax.broadcasted_iota(jnp.int32, (n, m), 1)


def _silu(x):
    return x * jax.nn.sigmoid(x)


def _gated_norm(o, z, nw):
    return o * lax.rsqrt(jnp.mean(o * o, axis=-1, keepdims=True) + EPS) * nw * _silu(z)


def _heads(a, nb):
    return jnp.stack([a[c * CHUNK:(c + 1) * CHUNK, h * DH:(h + 1) * DH] for c in range(nb) for h in range(HEADS)], axis=0)


def _unheads(a3, nb):
    return jnp.concatenate(
        [jnp.concatenate([a3[c * HEADS + h] for h in range(HEADS)], axis=1) for c in range(nb)], axis=0)


def _split3(x):
    hi = x.astype(BF16)
    r1 = x - hi.astype(F32)
    mid = r1.astype(BF16)
    return hi, mid, (r1 - mid.astype(F32)).astype(BF16)


def _summation_matrices(pattern, n_out):
    s = pattern(np.arange(n_out)[:, None], np.arange(CHUNK)[None, :]).astype(np.float32)
    return jnp.asarray(np.tile(s, (1, 3)), BF16), jnp.asarray(np.tile(s.T, (1, 2)), BF16)


def _select_rows(mats, chunks):
    width = chunks[0].shape[1]
    out = _summation(*mats, jnp.concatenate(chunks, axis=1))
    return [out[:, c * width:(c + 1) * width] for c in range(len(chunks))]


def _summation_impl(s, v):
    return jnp.dot(s, jnp.concatenate(_split3(v), axis=0), preferred_element_type=F32)


@jax.custom_vjp
def _summation(s, s_t, v):
    return _summation_impl(s, v)


def _summation_fwd(s, s_t, v):
    return _summation_impl(s, v), s_t


def _summation_bwd(s_t, d):
    hi = d.astype(BF16)
    return None, None, jnp.dot(s_t, jnp.concatenate([hi, (d - hi.astype(F32)).astype(BF16)], axis=0),
                               preferred_element_type=F32)


_summation.defvjp(_summation_fwd, _summation_bwd)


def _chunks(x, nb):
    return [x[c * CHUNK:(c + 1) * CHUNK] for c in range(nb)]


def _running_sum(i, j):
    return j <= i


HG_LEVELS = 6


def _hg_sums(i, j):
    lvl, t = i >> HG_LEVELS, i & (CHUNK - 1)
    last = t
    for l in range(1, HG_LEVELS + 1):
        width = HG_LEVELS + 1 - l
        last = np.where(lvl == l, ((t >> width) << width) + (CHUNK >> l) - 1, last)
    return j <= last


def _level_operand(sh, q3, k3, x):
    def second():
        return ((lax.broadcasted_iota(jnp.int32, (CHUNK, DH), 0) >> sh) & 1) == 1

    def forward(q3, k3, x):
        decay = jnp.exp(-jnp.abs(x))
        out = jnp.where(second(), q3, k3) * decay
        return out, (decay, out)

    def backward(saved, d):
        decay, out = saved
        d_side, t = d * decay, d * out
        return jnp.where(second(), d_side, 0.0), jnp.where(second(), 0.0, d_side), jnp.where(second(), t, -t)

    operand = jax.custom_vjp(lambda q3, k3, x: forward(q3, k3, x)[0])
    operand.defvjp(forward, backward)
    return operand(q3, k3, x)


def hg_local(p, logits, sum_mats):
    nb = p.shape[0] // CHUNK
    l0, l1 = logits[0:1], logits[1:2]
    mx = jnp.maximum(l0, l1)
    e0, e1 = jnp.exp(l0 - mx), jnp.exp(l1 - mx)
    lb = e0 / (e0 + e1)
    q = _silu(p[:, 0:WIDTH])
    f = lb + (1.0 - lb) * jax.nn.sigmoid(p[:, WIDTH:2 * WIDTH])
    k = 1.0 - f
    logf = jnp.log(f)
    sums = _select_rows(sum_mats, _chunks(logf, nb))
    level = lambda l: _heads(jnp.concatenate([s[l * CHUNK:(l + 1) * CHUNK] for s in sums], axis=0), nb)
    q3, k3, v3, g3 = _heads(q, nb), _heads(k, nb), _heads(p[:, 2 * WIDTH:3 * WIDTH], nb), level(0)
    r, c = _iota2(CHUNK, CHUNK)
    a = jnp.where(r == c, _bmm_nt(q3, k3), 0.0)
    for l in range(1, HG_LEVELS + 1):
        sh = HG_LEVELS - l
        qk = _level_operand(sh, q3, k3, g3 - level(l))
        pair = ((r >> (sh + 1)) == (c >> (sh + 1))) & (((r >> sh) & 1) == 1) & (((c >> sh) & 1) == 0)
        a = a + jnp.where(pair, _bmm_nt(qk, qk), 0.0)
    o = _bmm(a, v3)
    glast = g3[:, CHUNK - 1:CHUNK, :]
    egs = tuple(jnp.concatenate([jnp.exp(glast[c * HEADS + h]) for h in range(HEADS)], axis=1) for c in range(nb))
    return _unheads(q3 * jnp.exp(g3), nb), _unheads(k3 * jnp.exp(glast - g3), nb), _unheads(o, nb), egs


def hg_scan(q_in, k_out, v, eg, o_intra, z, nw, st):
    o = o_intra + _bmm_nt(q_in, st)
    return _gated_norm(o, z, nw), st * eg + _bmm_tn(v, k_out)


def _tri_y_impl(a):
    r, c = _iota2(CHUNK, CHUNK)
    same16 = (r // SUB) == (c // SUB)
    same32 = (r // (2 * SUB)) == (c // (2 * SUB))
    a0 = jnp.where(same16, a, 0.0)
    y = -a0
    pw = _bmm(a0, a0)
    for _ in range(2):
        y = y + pw + _bmm(y, pw)
        pw = _bmm(pw, pw)
    y = y + pw + _bmm(y, pw)
    for ak in (jnp.where(same32 & jnp.logical_not(same16), a, 0.0), jnp.where(same32, 0.0, a)):
        m = ak + _bmm(y, ak)
        y = y - (m + _bmm(m, y))
    return y


@jax.custom_vjp
def _tri_y(a):
    return _tri_y_impl(a)


def _tri_y_fwd(a):
    y = _tri_y_impl(a)
    return y, y


def _tri_y_bwd(y, dy):
    n = dy + _bmm_tn(y, dy)
    return (-(n + _bmm_nt(n, y)),)


_tri_y.defvjp(_tri_y_fwd, _tri_y_bwd)


def _saved_inverse(y):
    @jax.custom_vjp
    def inverse(a):
        return y

    inverse.defvjp(lambda a: (y, None), lambda _, dy: _tri_y_bwd(y, dy))
    return inverse


def _head_rows(a3, nb):
    return jnp.concatenate([a3[g] for g in range(nb * HEADS)], axis=0)


def _rows_down(x, s):
    rows = x.shape[0]

    @jax.custom_vjp
    def rotate(v):
        return pltpu.roll(v, s, 0)

    rotate.defvjp(lambda v: (pltpu.roll(v, s, 0), None), lambda _, d: (pltpu.roll(d, rows - s, 0),))
    return rotate(x)


def gd_local(xx, ab, cw, alog, dtb, sum_mats, inverse=_tri_y):
    n = ab.shape[0]
    nb = n // CHUNK
    conv = cw[CONV_TAPS - 1:CONV_TAPS] * xx[HALO:HALO + n]
    for j in range(CONV_TAPS - 1):
        conv = conv + cw[j:j + 1] * _rows_down(xx, CONV_TAPS - 1 - j)[HALO:HALO + n]
    act = _silu(conv)
    x = ab + dtb
    g_all = -jnp.exp(alog) * (jnp.maximum(x, 0.0) + jnp.log1p(jnp.exp(-jnp.abs(x))))
    beta_all = jax.nn.sigmoid(ab)
    gam_all = jnp.concatenate(_select_rows(sum_mats, _chunks(g_all, nb)), axis=0)
    q3, k3, v3 = _heads(act[:, 0:WIDTH], nb), _heads(act[:, WIDTH:2 * WIDTH], nb), _heads(act[:, 2 * WIDTH:QKV], nb)
    q3 = q3 * lax.rsqrt(jnp.sum(q3 * q3, axis=-1, keepdims=True) + EPS) * (DH ** -0.5)
    k3 = k3 * lax.rsqrt(jnp.sum(k3 * k3, axis=-1, keepdims=True) + EPS)
    pairs = [(c, h) for c in range(nb) for h in range(HEADS)]
    beta = jnp.stack([beta_all[c * CHUNK:(c + 1) * CHUNK, HEADS + h:HEADS + h + 1] for c, h in pairs], axis=0)
    gam = jnp.stack([gam_all[c * CHUNK:(c + 1) * CHUNK, h:h + 1] for c, h in pairs], axis=0)
    gam_t = [gam_all[c * CHUNK:(c + 1) * CHUNK].T for c in range(nb)]
    gam_row = jnp.stack([gam_t[c][h:h + 1, :] for c, h in pairs], axis=0)
    glast = gam[:, CHUNK - 1:CHUNK, :]
    r, c = _iota2(CHUNK, CHUNK)
    dec = jnp.exp(jnp.where(c < r, gam - gam_row, -jnp.inf))
    y = inverse(beta * _bmm_nt(k3, k3) * dec)
    eg = jnp.exp(gam)
    rhs = jnp.concatenate([beta * v3, (beta * eg) * k3], axis=2)
    sol = rhs + _bmm(y, rhs)
    qk = _bmm_nt(q3, k3) * jnp.where(r == c, 1.0, dec)
    eas = tuple(jnp.exp(gam_all[(c + 1) * CHUNK - 1:(c + 1) * CHUNK]) for c in range(nb))
    return (_unheads(sol[:, :, 0:DH], nb), _unheads(sol[:, :, DH:2 * DH], nb), _unheads(q3 * eg, nb),
            _unheads(k3 * jnp.exp(glast - gam), nb), _head_rows(qk, nb), eas), _head_rows(y, nb)


def gd_scan(uu, ww, qe, ke, qk, ea, z, nw, s):
    u = uu - _bmm(ww, s)
    o = _bmm(qe, s) + _bmm(qk, u)
    return _gated_norm(o, z, nw), ea * s + _bmm_tn(ke, u)


def _cparams(*sem):
    return pltpu.CompilerParams(dimension_semantics=sem, vmem_limit_bytes=VMEM_LIMIT)


def _row_tile(n):
    for t in (512, 256, 128, 64):
        if n % t == 0:
            return t
    raise ValueError(f"unsupported token count {n}")


def _w_in_specs():
    once = pl.Buffered(1)
    return [pl.BlockSpec((4 * WIDTH, D_MODEL), lambda *i: (0, 0), pipeline_mode=once),
            pl.BlockSpec((4 * WIDTH, D_MODEL), lambda *i: (1, 0), pipeline_mode=once),
            pl.BlockSpec((AB_PAD, D_MODEL), lambda *i: (8 * WIDTH // AB_PAD, 0), pipeline_mode=once)]


def in_proj(h, h0, norm_w, w_t):
    n = h.shape[0]
    tm = _row_tile(n)
    nt = (((1,), (1,)), ((), ()))

    def body(h_ref, h0_ref, nw_ref, whg_ref, wgd_ref, wab_ref, phg_ref, pgd_ref, pab_ref, phg0_ref, pgd0_ref, pab0_ref, u0_ref):
        def project(x, hg_ref, gd_ref, ab_ref):
            u = (x * lax.rsqrt(jnp.mean(x * x, axis=-1, keepdims=True) + EPS) * nw_ref[...]).astype(MXU_DTYPE)
            hg_ref[...] = lax.dot_general(u, whg_ref[...], nt, preferred_element_type=F32)
            gd_ref[...] = lax.dot_general(u, wgd_ref[...], nt, preferred_element_type=F32)
            ab_ref[...] = lax.dot_general(u, wab_ref[...], nt, preferred_element_type=F32)
            return u

        @pl.when(pl.program_id(0) == 0)
        def _():
            u0_ref[...] = project(h0_ref[...], phg0_ref, pgd0_ref, pab0_ref)

        project(h_ref[...], phg_ref, pgd_ref, pab_ref)

    n0 = h0.shape[0]
    row = lambda w: pl.BlockSpec((tm, w), lambda i: (i, 0))
    lead = lambda w: pl.BlockSpec((n0, w), lambda i: (0, 0))
    widths = [4 * WIDTH, 4 * WIDTH, AB_PAD]
    return pl.pallas_call(
        body, grid=(n // tm,), name="in_proj",
        in_specs=[row(D_MODEL), lead(D_MODEL), pl.BlockSpec(norm_w.shape, lambda i: (0, 0))] + _w_in_specs(),
        out_specs=[row(w) for w in widths] + [lead(w) for w in widths] + [lead(D_MODEL)],
        out_shape=[jax.ShapeDtypeStruct((n, w), F32) for w in widths] + [jax.ShapeDtypeStruct((n0, w), F32) for w in widths]
        + [jax.ShapeDtypeStruct((n0, D_MODEL), MXU_DTYPE)],
        compiler_params=_cparams("arbitrary"),
    )(h, h0, norm_w, w_t, w_t, w_t)


def out_proj_loss(x, tgt, y_hg, y_gd, w_out, fw):
    n = x.shape[0]
    tm = _row_tile(n)
    inv_d = 1.0 / D_MODEL

    def body(x_ref, t_ref, yh_ref, yg_ref, w_ref, fw_ref, dh_ref, dyh_ref, dyg_ref, dw_ref, loss_ref, dfw_ref):
        @pl.when(pl.program_id(0) == 0)
        def _():
            dw_ref[...] = jnp.zeros_like(dw_ref)
            loss_ref[...] = jnp.zeros_like(loss_ref)
            dfw_ref[...] = jnp.zeros_like(dfw_ref)

        yh, yg = yh_ref[...], yg_ref[...]
        wa, wb = w_ref[0:WIDTH, :], w_ref[WIDTH:2 * WIDTH, :]
        h2 = x_ref[...] + jnp.dot(yh, wa, preferred_element_type=F32) + jnp.dot(yg, wb, preferred_element_type=F32)
        r2 = lax.rsqrt(jnp.mean(h2 * h2, axis=-1, keepdims=True) + EPS)
        nrm = h2 * r2
        fwv = fw_ref[...]
        err = nrm * fwv - t_ref[...]
        loss_ref[...] += jnp.full(loss_ref.shape, 0.5 * inv_d * jnp.sum(err * err), F32)
        dout = err * inv_d
        dfw_ref[...] += jnp.sum(dout * nrm, axis=0, keepdims=True)
        dn = dout * fwv
        dh2 = r2 * (dn - nrm * jnp.mean(dn * nrm, axis=-1, keepdims=True))
        dh_ref[...] = dh2
        dhb = dh2.astype(MXU_DTYPE)
        dyh_ref[...] = lax.dot_general(dhb, wa, (((1,), (1,)), ((), ())), preferred_element_type=F32)
        dyg_ref[...] = lax.dot_general(dhb, wb, (((1,), (1,)), ((), ())), preferred_element_type=F32)
        dw_ref[0:WIDTH, :] += lax.dot_general(yh, dhb, (((0,), (0,)), ((), ())), preferred_element_type=F32)
        dw_ref[WIDTH:2 * WIDTH, :] += lax.dot_general(yg, dhb, (((0,), (0,)), ((), ())), preferred_element_type=F32)

    row = lambda w: pl.BlockSpec((tm, w), lambda i: (i, 0))
    full = lambda s: pl.BlockSpec(s, lambda i: (0, 0))
    return pl.pallas_call(
        body, grid=(n // tm,), name="out_proj_loss",
        in_specs=[row(D_MODEL), row(D_MODEL), row(WIDTH), row(WIDTH), full(w_out.shape), full(fw.shape)],
        out_specs=[row(D_MODEL), row(WIDTH), row(WIDTH), full((2 * WIDTH, D_MODEL)), full((8, 128)), full((1, D_MODEL))],
        out_shape=[jax.ShapeDtypeStruct((n, D_MODEL), F32), jax.ShapeDtypeStruct((n, WIDTH), F32),
                   jax.ShapeDtypeStruct((n, WIDTH), F32), jax.ShapeDtypeStruct((2 * WIDTH, D_MODEL), F32),
                   jax.ShapeDtypeStruct((8, 128), F32), jax.ShapeDtypeStruct((1, D_MODEL), F32)],
        compiler_params=_cparams("arbitrary"),
    )(x, tgt, y_hg, y_gd, w_out, fw)


def in_proj_bwd(dphg, dpgd, dpab, w_t, h, dh2, norm_w, h0, u0, dphg0, dpgd0, dpab0):
    n = h.shape[0]
    tm = _row_tile(n)
    steps = n // tm

    def body(dphg_ref, dpgd_ref, dpab_ref, whg_ref, wgd_ref, wab_ref, h_ref, dh2_ref, nw_ref, h0_ref, u0_ref, d0hg_ref,
             d0gd_ref, d0ab_ref, dx_ref, dx0_ref, dnw_ref, ghg_ref, ggd_ref, gab_ref, acc_hg, acc_gd, acc_ab):
        i = pl.program_id(0)
        nwv = nw_ref[...]

        def norm_bwd(dps, x):
            du = jnp.dot(dps[0], whg_ref[...], preferred_element_type=F32)
            du += jnp.dot(dps[1], wgd_ref[...], preferred_element_type=F32)
            du += jnp.dot(dps[2], wab_ref[...], preferred_element_type=F32)
            r = lax.rsqrt(jnp.mean(x * x, axis=-1, keepdims=True) + EPS)
            nrm = x * r
            dn = du * nwv
            return r * (dn - nrm * jnp.mean(dn * nrm, axis=-1, keepdims=True)), nrm, jnp.sum(du * nrm, axis=0, keepdims=True)

        def accumulate(dps, u, first):
            for acc, dp in zip((acc_hg, acc_gd, acc_ab), dps):
                step = min(acc.shape[0], 512)
                for lo in range(0, acc.shape[0], step):
                    part = _mm_tn(dp[:, lo:lo + step], u)
                    acc[lo:lo + step, :] = part if first else acc[lo:lo + step, :] + part

        @pl.when(i == 0)
        def _():
            dps0 = (d0hg_ref[...], d0gd_ref[...], d0ab_ref[...])
            dx0_ref[...], _, dnw_ref[...] = norm_bwd(dps0, h0_ref[...])
            accumulate(dps0, u0_ref[...], True)

        dps = (dphg_ref[...], dpgd_ref[...], dpab_ref[...])
        dx, nrm, dnw = norm_bwd(dps, h_ref[...])
        dx_ref[...] = dh2_ref[...] + dx
        dnw_ref[...] += dnw
        accumulate(dps, (nrm * nwv).astype(MXU_DTYPE), False)

        @pl.when(i == steps - 1)
        def _():
            pltpu.sync_copy(acc_hg, ghg_ref)
            pltpu.sync_copy(acc_gd, ggd_ref)
            pltpu.sync_copy(acc_ab, gab_ref)

    row = lambda w: pl.BlockSpec((tm, w), lambda i: (i, 0))
    full = lambda a: pl.BlockSpec(a.shape, lambda i: (0, 0), pipeline_mode=pl.Buffered(1))
    anywhere = pl.BlockSpec(memory_space=pl.ANY)
    return pl.pallas_call(
        body, grid=(steps,), name="in_proj_bwd",
        in_specs=[row(4 * WIDTH), row(4 * WIDTH), row(AB_PAD)] + _w_in_specs() + [row(D_MODEL), row(D_MODEL), full(norm_w),
                                                                                   full(h0), full(u0), full(dphg0), full(dpgd0),
                                                                                   full(dpab0)],
        out_specs=[row(D_MODEL), pl.BlockSpec(h0.shape, lambda i: (0, 0)), pl.BlockSpec((1, D_MODEL), lambda i: (0, 0)),
                   anywhere, anywhere, anywhere],
        out_shape=[jax.ShapeDtypeStruct((n, D_MODEL), F32), jax.ShapeDtypeStruct(h0.shape, F32),
                   jax.ShapeDtypeStruct((1, D_MODEL), F32), jax.ShapeDtypeStruct((4 * WIDTH, D_MODEL), F32),
                   jax.ShapeDtypeStruct((4 * WIDTH, D_MODEL), F32), jax.ShapeDtypeStruct((AB_PAD, D_MODEL), F32)],
        scratch_shapes=[pltpu.VMEM((4 * WIDTH, D_MODEL), F32), pltpu.VMEM((4 * WIDTH, D_MODEL), F32),
                        pltpu.VMEM((AB_PAD, D_MODEL), F32)],
        compiler_params=pltpu.CompilerParams(dimension_semantics=("arbitrary",), vmem_limit_bytes=VMEM_LIMIT_LARGE),
    )(dphg, dpgd, dpab, w_t, w_t, w_t, h, dh2, norm_w, h0, u0, dphg0, dpgd0, dpab0)


def _sds(shape, dtype=F32):
    return jax.ShapeDtypeStruct(shape, dtype)


def _pairs(b):
    return [(i, h) for i in range(b) for h in range(HEADS)]


def _load_slabs(ref, b, k):
    return jnp.stack([ref[i, k * CHUNK:(k + 1) * CHUNK, h * DH:(h + 1) * DH].astype(F32) for i, h in _pairs(b)], axis=0)


def _lead_slabs(a, b):
    return jnp.stack([a[:, h * DH:(h + 1) * DH].astype(F32) for _, h in _pairs(b)], axis=0)


def _rows(a3, i):
    return jnp.concatenate([a3[i * HEADS + h] for h in range(HEADS)], axis=1)


def _store_slabs(ref, a3, b, k):
    for i in range(b):
        ref[i, k * CHUNK:(k + 1) * CHUNK, :] = _rows(a3, i).astype(ref.dtype)


def _sum_rows(a3, b):
    out = _rows(a3, 0)
    for i in range(1, b):
        out = out + _rows(a3, i)
    return out


def _save_states(ref, s, b, k):
    for i in range(b):
        ref[i, k] = jnp.concatenate([s[i * HEADS + h] for h in range(HEADS)], axis=0).astype(ref.dtype)


def _load_states(ref, b, k):
    return jnp.stack([ref[i, k, h * DH:(h + 1) * DH, :].astype(F32) for i, h in _pairs(b)], axis=0)


def hg_local_fwd(p, p0, logits):
    b, seq, _ = p.shape
    rows = LOCAL_CHUNKS * CHUNK
    nreal = seq // CHUNK

    def body(p_ref, p0_ref, lg_ref, s_ref, st_ref, q_ref, k_ref, o_ref, eg_ref, q0_ref, k0_ref, o0_ref, eg0_ref):
        sum_mats = (s_ref[...], st_ref[...])

        @pl.when((pl.program_id(0) == 0) & (pl.program_id(1) == 0))
        def _():
            q_in, k_out, o0_ref[...], (eg0_ref[...],) = hg_local(p0_ref[...], lg_ref[...], sum_mats)
            q0_ref[...], k0_ref[...] = q_in.astype(MXU_DTYPE), k_out.astype(MXU_DTYPE)

        q_in, k_out, o_intra, egs = hg_local(p_ref[...], lg_ref[...], sum_mats)
        q_ref[...], k_ref[...], o_ref[...] = q_in.astype(MXU_DTYPE), k_out.astype(MXU_DTYPE), o_intra
        for c in range(LOCAL_CHUNKS):
            eg_ref[c] = egs[c]

    slab = pl.BlockSpec((None, rows, WIDTH), lambda s, g: (s, g, 0))
    const = lambda shape: pl.BlockSpec(shape, lambda s, g: (0, 0))
    lead_shapes = [(CHUNK, WIDTH)] * 3 + [(1, WIDTH)]
    sum_mats = _summation_matrices(_hg_sums, (HG_LEVELS + 1) * CHUNK)
    out = pl.pallas_call(
        body, grid=(b, seq // rows), name="hgrn2_local",
        in_specs=[pl.BlockSpec((None, rows, 4 * WIDTH), lambda s, g: (s, g, 0)), const(p0.shape), const(logits.shape)]
        + [const(a.shape) for a in sum_mats],
        out_specs=[slab, slab, slab, pl.BlockSpec((None, LOCAL_CHUNKS, 1, WIDTH), lambda s, g: (s, g, 0, 0))]
        + [const(s) for s in lead_shapes],
        out_shape=[_sds((b, seq, WIDTH), MXU_DTYPE)] * 2 + [_sds((b, seq, WIDTH)), _sds((b, nreal, 1, WIDTH))]
        + [_sds(lead_shapes[0], MXU_DTYPE)] * 2 + [_sds(lead_shapes[2]), _sds(lead_shapes[3])],
        compiler_params=_cparams("arbitrary", "arbitrary"),
    )(p, p0, logits, *sum_mats)
    return out[0:4], out[4:8]


def _hg_scan_args(b, k, q_ref, k_ref, o_ref, v_ref, z_ref, eg_ref):
    eg = jnp.stack([eg_ref[i, k, :, h * DH:(h + 1) * DH] for i, h in _pairs(b)], axis=0)
    return (_load_slabs(q_ref, b, k), _load_slabs(k_ref, b, k), _load_slabs(v_ref, b, k), eg, _load_slabs(o_ref, b, k),
            _load_slabs(z_ref, b, k))


def _hg_lead_args(b, q0_ref, k0_ref, o0_ref, p0_ref, eg0_ref):
    eg = jnp.stack([eg0_ref[:, h * DH:(h + 1) * DH] for _, h in _pairs(b)], axis=0)
    return (_lead_slabs(q0_ref[...], b), _lead_slabs(k0_ref[...], b), _lead_slabs(p0_ref[:, 2 * WIDTH:3 * WIDTH], b), eg,
            _lead_slabs(o0_ref[...], b), _lead_slabs(p0_ref[:, 3 * WIDTH:4 * WIDTH], b))


def _scan_specs(b, ng, reverse, chunks):
    group = (lambda i: ng - 1 - i) if reverse else (lambda i: i)
    slab = lambda lane_block: pl.BlockSpec((b, chunks * CHUNK, WIDTH), lambda i: (0, group(i), lane_block))
    per_chunk = lambda *tail: pl.BlockSpec((b, chunks) + tail, lambda i: (0, group(i)) + (0,) * len(tail))
    const = lambda a: pl.BlockSpec(a.shape, lambda i: (0,) * a.ndim)
    return slab, per_chunk, const


def run_scans(parts, nc, name):
    n_in = [len(p["args"]) for p in parts]
    n_out = [len(p["out_shape"]) for p in parts]
    n_scr = [len(p["scratch_shapes"]) for p in parts]

    def body(*refs):
        ins, outs, scr = refs[:sum(n_in)], refs[sum(n_in):sum(n_in) + sum(n_out)], refs[sum(n_in) + sum(n_out):]
        for i, part in enumerate(parts):
            part["body"](*ins[sum(n_in[:i]):sum(n_in[:i + 1])], *outs[sum(n_out[:i]):sum(n_out[:i + 1])],
                         *scr[sum(n_scr[:i]):sum(n_scr[:i + 1])])

    flat = lambda key: [v for p in parts for v in p[key]]
    out = pl.pallas_call(body, grid=(nc,), name=name, in_specs=flat("in_specs"), out_specs=flat("out_specs"),
                         out_shape=flat("out_shape"), scratch_shapes=flat("scratch_shapes"),
                         compiler_params=_cparams("arbitrary"))(*flat("args"))
    return [out[sum(n_out[:i]):sum(n_out[:i + 1])] for i in range(len(parts))]


def hg_scan_fwd(p, p0, local, lead, nw):
    b, seq, _ = p.shape
    q_in, k_out, o_intra, eg = local
    slab, per_chunk, const = _scan_specs(b, seq // (SCAN_CHUNKS_FWD * CHUNK), False, SCAN_CHUNKS_FWD)

    def body(q_ref, k_ref, o_ref, v_ref, z_ref, eg_ref, q0_ref, k0_ref, o0_ref, p0_ref, eg0_ref, nw_ref, y_ref, ss_ref, st):
        @pl.when(pl.program_id(0) == 0)
        def _():
            st[...] = hg_scan(*_hg_lead_args(b, q0_ref, k0_ref, o0_ref, p0_ref, eg0_ref), nw_ref[...], jnp.zeros(st.shape, F32))[1]

        s = st[...]
        for k in range(SCAN_CHUNKS_FWD):
            _save_states(ss_ref, s, b, k)
            y, s = hg_scan(*_hg_scan_args(b, k, q_ref, k_ref, o_ref, v_ref, z_ref, eg_ref), nw_ref[...], s)
            _store_slabs(y_ref, y, b, k)
        st[...] = s

    return dict(
        body=body, args=(q_in, k_out, o_intra, p, p, eg, lead[0], lead[1], lead[2], p0, lead[3], nw),
        in_specs=[slab(0), slab(0), slab(0), slab(2), slab(3), per_chunk(1, WIDTH)] + [const(a) for a in lead[0:3]]
        + [const(p0), const(lead[3]), const(nw)],
        out_specs=[slab(0), per_chunk(WIDTH, DH)],
        out_shape=[_sds((b, seq, WIDTH), MXU_DTYPE), _sds((b, seq // CHUNK, WIDTH, DH), MXU_DTYPE)],
        scratch_shapes=[pltpu.VMEM((b * HEADS, DH, DH), F32)])


def hg_scan_bwd(p, p0, local, lead, nw, ssave, dy):
    b, seq, _ = p.shape
    ng = seq // (SCAN_CHUNKS * CHUNK)
    q_in, k_out, o_intra, eg = local
    slab, per_chunk, const = _scan_specs(b, ng, True, SCAN_CHUNKS)

    def body(q_ref, k_ref, o_ref, v_ref, z_ref, eg_ref, q0_ref, k0_ref, o0_ref, p0_ref, eg0_ref, nw_ref, ss_ref, dy_ref,
             dq_ref, dk_ref, do_ref, dv_ref, dz_ref, deg_ref, dq0_ref, dk0_ref, do0_ref, dv0_ref, dz0_ref, deg0_ref, dnw_ref,
             dst):
        i = pl.program_id(0)

        @pl.when(i == 0)
        def _():
            dst[...] = jnp.zeros_like(dst)
            dnw_ref[...] = jnp.zeros_like(dnw_ref)

        ds = dst[...]
        for k in reversed(range(SCAN_CHUNKS)):
            args = _hg_scan_args(b, k, q_ref, k_ref, o_ref, v_ref, z_ref, eg_ref)
            _, vjp = jax.vjp(hg_scan, *args, nw_ref[...], _load_states(ss_ref, b, k))
            dq, dk, dv, deg, do, dz, dnw, ds = vjp((_load_slabs(dy_ref, b, k), ds))
            dnw_ref[...] += dnw
            for ref, val in ((dq_ref, dq), (dk_ref, dk), (do_ref, do), (dv_ref, dv), (dz_ref, dz)):
                _store_slabs(ref, val, b, k)
            for j in range(b):
                deg_ref[j, k] = _rows(deg, j)
        dst[...] = ds

        @pl.when(i == ng - 1)
        def _():
            args = _hg_lead_args(b, q0_ref, k0_ref, o0_ref, p0_ref, eg0_ref)
            _, vjp = jax.vjp(hg_scan, *args, nw_ref[...], jnp.zeros(dst.shape, F32))
            dq, dk, dv, deg, do, dz, dnw, _ = vjp((jnp.zeros((b * HEADS, CHUNK, DH), F32), ds))
            dnw_ref[...] += dnw
            for ref, val in ((dq0_ref, dq), (dk0_ref, dk), (do0_ref, do), (dv0_ref, dv), (dz0_ref, dz), (deg0_ref, deg)):
                ref[...] = _sum_rows(val, b)

    lead_out = [const(a) for a in lead[0:3]] + [const(lead[0]), const(lead[0]), const(lead[3])]
    return dict(
        body=body, args=(q_in, k_out, o_intra, p, p, eg, lead[0], lead[1], lead[2], p0, lead[3], nw, ssave, dy),
        in_specs=[slab(0), slab(0), slab(0), slab(2), slab(3), per_chunk(1, WIDTH)] + [const(a) for a in lead[0:3]]
        + [const(p0), const(lead[3]), const(nw), per_chunk(WIDTH, DH), slab(0)],
        out_specs=[slab(0)] * 5 + [per_chunk(1, WIDTH)] + lead_out + [const(nw)],
        out_shape=[_sds((b, seq, WIDTH))] * 3 + [_sds((b, seq, WIDTH), MXU_DTYPE)] * 2 + [_sds(eg.shape)]
        + [_sds((CHUNK, WIDTH))] * 5 + [_sds((1, WIDTH)), _sds(nw.shape)],
        scratch_shapes=[pltpu.VMEM((b * HEADS, DH, DH), F32)])


def _hg_local_vjp(sum_mats, p, logits, dq, dk, do, degs, dv, dz):
    _, vjp = jax.vjp(lambda p_, logits_: hg_local(p_, logits_, sum_mats), p, logits)
    dp, dlg = vjp((dq, dk, do, degs))
    return dp + jnp.concatenate([jnp.zeros((p.shape[0], 2 * WIDTH), F32), dv.astype(F32), dz.astype(F32)], axis=1), dlg


def hg_local_bwd(p, p0, logits, cot, cot0):
    b, seq, _ = p.shape
    rows = LOCAL_CHUNKS * CHUNK

    def body(p_ref, p0_ref, lg_ref, s_ref, st_ref, dq_ref, dk_ref, do_ref, dv_ref, dz_ref, deg_ref, dq0_ref, dk0_ref, do0_ref,
             dv0_ref, dz0_ref, deg0_ref, dp_ref, dp0_ref, dlg_ref):
        sum_mats = (s_ref[...], st_ref[...])

        @pl.when((pl.program_id(0) == 0) & (pl.program_id(1) == 0))
        def _():
            dp0, dlg_ref[...] = _hg_local_vjp(sum_mats, p0_ref[...], lg_ref[...], dq0_ref[...], dk0_ref[...], do0_ref[...],
                                              (deg0_ref[...],), dv0_ref[...], dz0_ref[...])
            dp0_ref[...] = dp0.astype(MXU_DTYPE)

        degs = tuple(deg_ref[c] for c in range(LOCAL_CHUNKS))
        dp, dlg = _hg_local_vjp(sum_mats, p_ref[...], lg_ref[...], dq_ref[...], dk_ref[...], do_ref[...], degs, dv_ref[...],
                                dz_ref[...])
        dp_ref[...] = dp.astype(MXU_DTYPE)
        dlg_ref[...] += dlg

    slab = pl.BlockSpec((None, rows, WIDTH), lambda s, g: (s, g, 0))
    wide = pl.BlockSpec((None, rows, 4 * WIDTH), lambda s, g: (s, g, 0))
    const = lambda a: pl.BlockSpec(a.shape, lambda s, g: (0, 0))
    sum_mats = _summation_matrices(_hg_sums, (HG_LEVELS + 1) * CHUNK)
    return pl.pallas_call(
        body, grid=(b, seq // rows), name="hgrn2_local_bwd",
        in_specs=[wide, const(p0), const(logits), const(sum_mats[0]), const(sum_mats[1]), slab, slab, slab, slab, slab,
                  pl.BlockSpec((None, LOCAL_CHUNKS, 1, WIDTH), lambda s, g: (s, g, 0, 0))] + [const(a) for a in cot0],
        out_specs=[wide, const(p0), const(logits)],
        out_shape=[_sds(p.shape, MXU_DTYPE), _sds(p0.shape, MXU_DTYPE), _sds(logits.shape)],
        compiler_params=_cparams("arbitrary", "arbitrary"),
    )(p, p0, logits, *sum_mats, *cot, *cot0)


def _halo_block(g):
    return jnp.maximum((LOCAL_CHUNKS * CHUNK // HALO) * g - 1, 0)


def _gd_window(g, p_ref, halo_ref, p0_ref):
    halo = jnp.where(g == 0, p0_ref[CHUNK - HALO:CHUNK, 0:QKV], halo_ref[...])
    return jnp.concatenate([halo, p_ref[:, 0:QKV]], axis=0)


def _lead_window(p0_ref):
    return jnp.concatenate([jnp.zeros((HALO, QKV), F32), p0_ref[:, 0:QKV]], axis=0)


def gd_local_fwd(p, p0, ab, ab0, cw, alog, dtb):
    b, seq, _ = p.shape
    rows = LOCAL_CHUNKS * CHUNK
    nreal = seq // CHUNK

    def body(p_ref, halo_ref, p0_ref, ab_ref, ab0_ref, cw_ref, al_ref, dt_ref, s_ref, st_ref, u_ref, w_ref, qe_ref, ke_ref,
             qk_ref, ea_ref, inv_ref, u0_ref, w0_ref, qe0_ref, ke0_ref, qk0_ref, ea0_ref, inv0_ref):
        sum_mats = (s_ref[...], st_ref[...])

        @pl.when((pl.program_id(0) == 0) & (pl.program_id(1) == 0))
        def _():
            (u0_ref[...], ww, qe, ke, qk0_ref[...], (ea0_ref[...],)), inv0_ref[...] = gd_local(
                _lead_window(p0_ref), ab0_ref[...], cw_ref[...], al_ref[...], dt_ref[...], sum_mats, inverse=_tri_y_impl)
            w0_ref[...], qe0_ref[...], ke0_ref[...] = ww.astype(MXU_DTYPE), qe.astype(MXU_DTYPE), ke.astype(MXU_DTYPE)

        (uu, ww, qe, ke, qk, eas), inv = gd_local(_gd_window(pl.program_id(1), p_ref, halo_ref, p0_ref), ab_ref[...],
                                                  cw_ref[...], al_ref[...], dt_ref[...], sum_mats, inverse=_tri_y_impl)
        u_ref[...], w_ref[...], qe_ref[...], ke_ref[...] = uu, ww.astype(MXU_DTYPE), qe.astype(MXU_DTYPE), ke.astype(MXU_DTYPE)
        for c in range(LOCAL_CHUNKS):
            qk_ref[c] = qk[c * HEADS * CHUNK:(c + 1) * HEADS * CHUNK]
            inv_ref[c] = inv[c * HEADS * CHUNK:(c + 1) * HEADS * CHUNK]
            ea_ref[c] = eas[c]

    const = lambda shape: pl.BlockSpec(shape, lambda s, g: (0, 0))
    slab = pl.BlockSpec((None, rows, WIDTH), lambda s, g: (s, g, 0))
    mats = pl.BlockSpec((None, LOCAL_CHUNKS, HEADS * CHUNK, CHUNK), lambda s, g: (s, g, 0, 0))
    lead_out = [_sds((CHUNK, WIDTH))] + [_sds((CHUNK, WIDTH), MXU_DTYPE)] * 3 + [_sds((HEADS * CHUNK, CHUNK)), _sds((1, AB_PAD)),
                                                                                _sds((HEADS * CHUNK, CHUNK))]
    sum_mats = _summation_matrices(_running_sum, CHUNK)
    out = pl.pallas_call(
        body, grid=(b, seq // rows), name="gdn_local",
        in_specs=[pl.BlockSpec((None, rows, 4 * WIDTH), lambda s, g: (s, g, 0)),
                  pl.BlockSpec((None, HALO, QKV), lambda s, g: (s, _halo_block(g), 0)), const(p0.shape),
                  pl.BlockSpec((None, rows, AB_PAD), lambda s, g: (s, g, 0)), const(ab0.shape), const(cw.shape),
                  const(alog.shape), const(dtb.shape), const(sum_mats[0].shape), const(sum_mats[1].shape)],
        out_specs=[slab] * 4 + [mats, pl.BlockSpec((None, LOCAL_CHUNKS, 1, AB_PAD), lambda s, g: (s, g, 0, 0)), mats]
        + [const(s.shape) for s in lead_out],
        out_shape=[_sds((b, seq, WIDTH))] + [_sds((b, seq, WIDTH), MXU_DTYPE)] * 3
        + [_sds((b, nreal, HEADS * CHUNK, CHUNK)), _sds((b, nreal, 1, AB_PAD)), _sds((b, nreal, HEADS * CHUNK, CHUNK))] + lead_out,
        compiler_params=_cparams("arbitrary", "arbitrary"),
    )(p, p, p0, ab, ab0, cw, alog, dtb, *sum_mats)
    return out[0:6], out[6], out[7:13], out[13]


def _gd_scan_args(b, k, u_ref, w_ref, qe_ref, ke_ref, qk_ref, ea_ref, z_ref):
    qk = jnp.stack([qk_ref[i, k, h * CHUNK:(h + 1) * CHUNK, :] for i, h in _pairs(b)], axis=0)
    ea = jnp.stack([ea_ref[i, k, :, h:h + 1] for i, h in _pairs(b)], axis=0)
    return (_load_slabs(u_ref, b, k), _load_slabs(w_ref, b, k), _load_slabs(qe_ref, b, k), _load_slabs(ke_ref, b, k), qk, ea,
            _load_slabs(z_ref, b, k))


def _gd_lead_args(b, u0_ref, w0_ref, qe0_ref, ke0_ref, qk0_ref, ea0_ref, p0_ref):
    qk = jnp.stack([qk0_ref[h * CHUNK:(h + 1) * CHUNK, :] for _, h in _pairs(b)], axis=0)
    ea = jnp.stack([ea0_ref[:, h:h + 1] for _, h in _pairs(b)], axis=0)
    return (_lead_slabs(u0_ref[...], b), _lead_slabs(w0_ref[...], b), _lead_slabs(qe0_ref[...], b), _lead_slabs(ke0_ref[...], b),
            qk, ea, _lead_slabs(p0_ref[:, QKV:QKV + WIDTH], b))


def gd_scan_fwd(p, p0, local, lead, nw):
    b, seq, _ = p.shape
    slab, per_chunk, const = _scan_specs(b, seq // (SCAN_CHUNKS_FWD * CHUNK), False, SCAN_CHUNKS_FWD)

    def body(u_ref, w_ref, qe_ref, ke_ref, qk_ref, ea_ref, z_ref, u0_ref, w0_ref, qe0_ref, ke0_ref, qk0_ref, ea0_ref, p0_ref,
             nw_ref, y_ref, ss_ref, st):
        @pl.when(pl.program_id(0) == 0)
        def _():
            lead_args = _gd_lead_args(b, u0_ref, w0_ref, qe0_ref, ke0_ref, qk0_ref, ea0_ref, p0_ref)
            st[...] = gd_scan(*lead_args, nw_ref[...], jnp.zeros(st.shape, F32))[1]

        s = st[...]
        for k in range(SCAN_CHUNKS_FWD):
            _save_states(ss_ref, s, b, k)
            y, s = gd_scan(*_gd_scan_args(b, k, u_ref, w_ref, qe_ref, ke_ref, qk_ref, ea_ref, z_ref), nw_ref[...], s)
            _store_slabs(y_ref, y, b, k)
        st[...] = s

    return dict(
        body=body, args=(*local, p, *lead, p0, nw),
        in_specs=[slab(0)] * 4 + [per_chunk(HEADS * CHUNK, CHUNK), per_chunk(1, AB_PAD), slab(3)] + [const(a) for a in lead]
        + [const(p0), const(nw)],
        out_specs=[slab(0), per_chunk(WIDTH, DH)],
        out_shape=[_sds((b, seq, WIDTH), MXU_DTYPE), _sds((b, seq // CHUNK, WIDTH, DH), MXU_DTYPE)],
        scratch_shapes=[pltpu.VMEM((b * HEADS, DH, DH), F32)])


def gd_scan_bwd(p, p0, local, lead, nw, ssave, dy):
    b, seq, _ = p.shape
    ng = seq // (SCAN_CHUNKS * CHUNK)
    slab, per_chunk, const = _scan_specs(b, ng, True, SCAN_CHUNKS)

    def body(u_ref, w_ref, qe_ref, ke_ref, qk_ref, ea_ref, z_ref, u0_ref, w0_ref, qe0_ref, ke0_ref, qk0_ref, ea0_ref, p0_ref,
             nw_ref, ss_ref, dy_ref, du_ref, dw_ref, dqe_ref, dke_ref, dqk_ref, dea_ref, dz_ref, du0_ref, dw0_ref, dqe0_ref,
             dke0_ref, dqk0_ref, dea0_ref, dz0_ref, dnw_ref, dst):
        i = pl.program_id(0)
        lane = lax.broadcasted_iota(jnp.int32, (1, AB_PAD), 1)

        def gate_rows(dea, j):
            return sum(jnp.where(lane == h, dea[j * HEADS + h], 0.0) for h in range(HEADS))

        def matrix_rows(dqk, j):
            return jnp.concatenate([dqk[j * HEADS + h] for h in range(HEADS)], axis=0)

        @pl.when(i == 0)
        def _():
            dst[...] = jnp.zeros_like(dst)
            dnw_ref[...] = jnp.zeros_like(dnw_ref)

        ds = dst[...]
        for k in reversed(range(SCAN_CHUNKS)):
            args = _gd_scan_args(b, k, u_ref, w_ref, qe_ref, ke_ref, qk_ref, ea_ref, z_ref)
            _, vjp = jax.vjp(gd_scan, *args, nw_ref[...], _load_states(ss_ref, b, k))
            du, dw, dqe, dke, dqk, dea, dz, dnw, ds = vjp((_load_slabs(dy_ref, b, k), ds))
            dnw_ref[...] += dnw
            for ref, val in ((du_ref, du), (dw_ref, dw), (dqe_ref, dqe), (dke_ref, dke), (dz_ref, dz)):
                _store_slabs(ref, val, b, k)
            for j in range(b):
                dqk_ref[j, k] = matrix_rows(dqk, j)
                dea_ref[j, k] = gate_rows(dea, j)
        dst[...] = ds

        @pl.when(i == ng - 1)
        def _():
            args = _gd_lead_args(b, u0_ref, w0_ref, qe0_ref, ke0_ref, qk0_ref, ea0_ref, p0_ref)
            _, vjp = jax.vjp(gd_scan, *args, nw_ref[...], jnp.zeros(dst.shape, F32))
            du, dw, dqe, dke, dqk, dea, dz, dnw, _ = vjp((jnp.zeros((b * HEADS, CHUNK, DH), F32), ds))
            dnw_ref[...] += dnw
            for ref, val in ((du0_ref, du), (dw0_ref, dw), (dqe0_ref, dqe), (dke0_ref, dke), (dz0_ref, dz)):
                ref[...] = _sum_rows(val, b)
            dqk0_ref[...] = sum((matrix_rows(dqk, j) for j in range(1, b)), matrix_rows(dqk, 0))
            dea0_ref[...] = sum((gate_rows(dea, j) for j in range(1, b)), gate_rows(dea, 0))

    uu, ww, qe, ke, qk, ea = local
    return dict(
        body=body, args=(*local, p, *lead, p0, nw, ssave, dy),
        in_specs=[slab(0)] * 4 + [per_chunk(HEADS * CHUNK, CHUNK), per_chunk(1, AB_PAD), slab(3)] + [const(a) for a in lead]
        + [const(p0), const(nw), per_chunk(WIDTH, DH), slab(0)],
        out_specs=[slab(0)] * 4 + [per_chunk(HEADS * CHUNK, CHUNK), per_chunk(1, AB_PAD), slab(0)] + [const(a) for a in lead]
        + [const(lead[0]), const(nw)],
        out_shape=[_sds((b, seq, WIDTH))] * 4 + [_sds(qk.shape), _sds(ea.shape), _sds((b, seq, WIDTH), MXU_DTYPE)]
        + [_sds(a.shape) for a in lead] + [_sds(lead[0].shape), _sds(nw.shape)],
        scratch_shapes=[pltpu.VMEM((b * HEADS, DH, DH), F32)])


def _gd_local_vjp(sum_mats, inv_rows, xx, ab, cw, alog, dtb):
    nb = ab.shape[0] // CHUNK
    inv = jnp.stack([inv_rows[g * CHUNK:(g + 1) * CHUNK] for g in range(nb * HEADS)], axis=0)
    _, vjp, _ = jax.vjp(lambda *a: gd_local(*a, sum_mats, inverse=_saved_inverse(inv)), xx, ab, cw, alog, dtb, has_aux=True)
    return vjp


def gd_local_bwd(p, p0, ab, ab0, cw, alog, dtb, inv, inv0, cot, dz, cot0, dz0):
    b, seq, _ = p.shape
    rows = LOCAL_CHUNKS * CHUNK
    ng = seq // rows
    du, dw, dqe, dke, dqk, dea = cot

    def body(p_ref, halo_ref, p0_ref, ab_ref, ab0_ref, cw_ref, al_ref, dt_ref, s_ref, st_ref, inv_ref, inv0_ref, du_ref, dw_ref,
             dqe_ref, dke_ref, dqk_ref, dea_ref, dz_ref, du0_ref, dw0_ref, dqe0_ref, dke0_ref, dqk0_ref, dea0_ref, dz0_ref,
             dp_ref, dab_ref, dp0_ref, dab0_ref, dcw_ref, dal_ref, ddt_ref, dhalo, dtail):
        s, i = pl.program_id(0), pl.program_id(1)
        g = ng - 1 - i
        sum_mats = (s_ref[...], st_ref[...])

        @pl.when(i == 0)
        def _():
            dhalo[...] = jnp.zeros_like(dhalo)

        @pl.when((s == 0) & (i == 0))
        def _():
            dtail[...] = jnp.zeros_like(dtail)
            dcw_ref[...] = jnp.zeros_like(dcw_ref)
            dal_ref[...] = jnp.zeros_like(dal_ref)
            ddt_ref[...] = jnp.zeros_like(ddt_ref)

        def finish(dxx, dab, dcw, dal, ddt, before, n, dz_val, dp_out, dab_out):
            dqkv = dxx[HALO:HALO + n] + jnp.concatenate([jnp.zeros((n - HALO, QKV), F32), before], axis=0)
            dp_out[...] = jnp.concatenate([dqkv.astype(MXU_DTYPE), dz_val.astype(MXU_DTYPE)], axis=1)
            dab_out[...] = dab.astype(MXU_DTYPE)
            dcw_ref[...] += dcw
            dal_ref[...] += dal
            ddt_ref[...] += ddt

        inv_rows = jnp.concatenate([inv_ref[c] for c in range(LOCAL_CHUNKS)], axis=0)
        vjp = _gd_local_vjp(sum_mats, inv_rows, _gd_window(g, p_ref, halo_ref, p0_ref), ab_ref[...], cw_ref[...], al_ref[...],
                            dt_ref[...])
        dqk_all = jnp.concatenate([dqk_ref[c] for c in range(LOCAL_CHUNKS)], axis=0)
        deas = tuple(dea_ref[c] for c in range(LOCAL_CHUNKS))
        grads = vjp((du_ref[...], dw_ref[...], dqe_ref[...], dke_ref[...], dqk_all, deas))
        finish(*grads, dhalo[...], rows, dz_ref[...], dp_ref, dab_ref)
        dhalo[...] = grads[0][0:HALO]

        @pl.when(g == 0)
        def _():
            dtail[...] += grads[0][0:HALO]

        @pl.when((s == b - 1) & (g == 0))
        def _():
            vjp0 = _gd_local_vjp(sum_mats, inv0_ref[...], _lead_window(p0_ref), ab0_ref[...], cw_ref[...], al_ref[...],
                                 dt_ref[...])
            grads0 = vjp0((du0_ref[...], dw0_ref[...], dqe0_ref[...], dke0_ref[...], dqk0_ref[...], (dea0_ref[...],)))
            finish(*grads0, dtail[...], CHUNK, dz0_ref[...], dp0_ref, dab0_ref)

    rg = lambda i: ng - 1 - i
    const = lambda a: pl.BlockSpec(a.shape, lambda s, i: (0, 0))
    slab = pl.BlockSpec((None, rows, WIDTH), lambda s, i: (s, rg(i), 0))
    wide = pl.BlockSpec((None, rows, 4 * WIDTH), lambda s, i: (s, rg(i), 0))
    gates = pl.BlockSpec((None, rows, AB_PAD), lambda s, i: (s, rg(i), 0))
    mats = pl.BlockSpec((None, LOCAL_CHUNKS, HEADS * CHUNK, CHUNK), lambda s, i: (s, rg(i), 0, 0))
    sum_mats = _summation_matrices(_running_sum, CHUNK)
    return pl.pallas_call(
        body, grid=(b, ng), name="gdn_local_bwd",
        in_specs=[wide, pl.BlockSpec((None, HALO, QKV), lambda s, i: (s, _halo_block(rg(i)), 0)), const(p0), gates, const(ab0),
                  const(cw), const(alog), const(dtb), const(sum_mats[0]), const(sum_mats[1]), mats, const(inv0), slab, slab,
                  slab, slab, mats,
                  pl.BlockSpec((None, LOCAL_CHUNKS, 1, AB_PAD), lambda s, i: (s, rg(i), 0, 0)), slab]
        + [const(a) for a in cot0] + [const(dz0)],
        out_specs=[wide, gates, const(p0), const(ab0), const(cw), const(alog), const(dtb)],
        out_shape=[_sds(p.shape, MXU_DTYPE), _sds(ab.shape, MXU_DTYPE), _sds(p0.shape, MXU_DTYPE), _sds(ab0.shape, MXU_DTYPE),
                   _sds(cw.shape), _sds(alog.shape), _sds(dtb.shape)],
        scratch_shapes=[pltpu.VMEM((HALO, QKV), F32), pltpu.VMEM((HALO, QKV), F32)],
        compiler_params=_cparams("arbitrary", "arbitrary"),
    )(p, p, p0, ab, ab0, cw, alog, dtb, *sum_mats, inv, inv0, du, dw, dqe, dke, dqk, dea, dz, *cot0, dz0)


def _position():
    return lax.axis_index("x"), lax.axis_index("y"), lax.axis_index("c")


EXCHANGE_COPIES = 10


def _exchange_blocks(bufs, send_sems, recv_sems):
    x, y, c = _position()
    here, x_nbr, y_nbr, diag = (x, y), (1 - x, y), (x, 1 - y), (1 - x, 1 - y)
    sibling = (x, y, 1 - c)
    me = (x, y, c)
    n = range(len(bufs))

    def rows(a, chip, core, half=None):
        block = bufs[a].at[4 * chip[0] + 2 * chip[1] + core]
        if half is None:
            return block
        total = bufs[a].shape[1]
        tile = 8 * (4 // jnp.dtype(bufs[a].dtype).itemsize)
        split = total // 2 // tile * tile
        return block.at[pl.ds(0, split)] if half == 0 else block.at[pl.ds(split, total - split)]

    def copy(a, k, region, to):
        return pltpu.make_async_remote_copy(src_ref=region, dst_ref=region, send_sem=send_sems.at[a * EXCHANGE_COPIES + k],
                                            recv_sem=recv_sems.at[a * EXCHANGE_COPIES + k], device_id=to, device_id_type=MESH)

    sent = [copy(a, 0, rows(a, here, c), sibling) for a in n]
    sent += [cp for a in n for cp in (copy(a, 1, rows(a, here, c, 0), (*x_nbr, c)), copy(a, 4, rows(a, here, c, 1), (*y_nbr, c)))]
    sent += [cp for a in n for cp in (copy(a, 2, rows(a, here, c, 1), (*x_nbr, c)), copy(a, 3, rows(a, here, c, 0), (*y_nbr, c)))]
    for cp in sent:
        cp.start()

    def after(arrivals, a, k, region, to):
        for cp in arrivals:
            cp.wait_recv()
        sent.append(copy(a, k, region, to))
        sent[-1].start()

    for a in n:
        after([copy(a, 1, rows(a, x_nbr, c, 0), me)], a, 5, rows(a, x_nbr, c, 0), (*y_nbr, c))
        after([copy(a, 4, rows(a, y_nbr, c, 1), me)], a, 6, rows(a, y_nbr, c, 1), (*x_nbr, c))
    for a in n:
        after([copy(a, 2, rows(a, x_nbr, c, 1), me)], a, 7, rows(a, x_nbr, c), sibling)
        after([copy(a, 3, rows(a, y_nbr, c, 0), me)], a, 8, rows(a, y_nbr, c), sibling)
    for a in n:
        after([copy(a, 5, rows(a, diag, c, 0), me), copy(a, 6, rows(a, diag, c, 1), me)], a, 9, rows(a, diag, c), sibling)
    for a in n:
        copy(a, 0, rows(a, here, 1 - c), me).wait_recv()
        for k, chip in ((7, x_nbr), (8, y_nbr), (9, diag)):
            copy(a, k, rows(a, chip, 1 - c), me).wait_recv()
    for cp in sent:
        cp.wait_send()


def _exchange_sems(n_bufs):
    return [pltpu.SemaphoreType.DMA((n_bufs * EXCHANGE_COPIES,)), pltpu.SemaphoreType.DMA((n_bufs * EXCHANGE_COPIES,))]


def gather_weights(w_in_t, w_out, small, pad_rows):
    rows, _, cols = w_in_t.shape
    buf_rows = -(-rows // ROW_TILE_BF16) * ROW_TILE_BF16

    def body(wi_ref, wo_ref, sm_ref, wi_out, wo_out, sm_out, wi_buf, send_sems, recv_sems):
        x, y, c = _position()
        me = 4 * x + 2 * y + c
        wi_buf[me, pl.ds(0, rows), :] = wi_ref[:, 0, :].astype(MXU_DTYPE)
        wi_buf[me, pl.ds(rows, buf_rows - rows), :] = jnp.zeros((buf_rows - rows, cols), MXU_DTYPE)
        wo_out[me] = wo_ref[...].astype(MXU_DTYPE)
        sm_out[me] = sm_ref[...]
        _exchange_blocks([wi_buf, wo_out, sm_out], send_sems, recv_sems)
        for d in range(N_DEV):
            wi_out[pl.ds(d * rows, rows), :] = wi_buf[d, pl.ds(0, rows), :]
        wi_out[pl.ds(N_DEV * rows, pad_rows), :] = jnp.zeros((pad_rows, cols), MXU_DTYPE)

    return pl.pallas_call(
        body, name="gather_weights", in_specs=[VMEM_SPEC] * 3, out_specs=[VMEM_SPEC] * 3,
        out_shape=[jax.ShapeDtypeStruct((N_DEV * rows + pad_rows, cols), MXU_DTYPE),
                   jax.ShapeDtypeStruct((N_DEV,) + w_out.shape, MXU_DTYPE), jax.ShapeDtypeStruct((N_DEV,) + small.shape, F32)],
        scratch_shapes=[pltpu.VMEM((N_DEV, buf_rows, cols), MXU_DTYPE)] + _exchange_sems(3),
        compiler_params=pltpu.CompilerParams(vmem_limit_bytes=VMEM_LIMIT))(w_in_t, w_out, small)


HOPS = 6


def reduce_gradients(tensors, small, name):
    n_t = len(tensors)
    arrays = [a for parts, _ in tensors for a, _ in parts]
    first_array = [sum(len(parts) for parts, _ in tensors[:t]) for t in range(n_t)]

    def pieces(t, j):
        parts, block_rows = tensors[t]
        out, base = [], 0
        for pi, (_, valid) in enumerate(parts):
            lo, hi = max(j * block_rows, base), min((j + 1) * block_rows, base + valid)
            if lo < hi:
                out.append((first_array[t] + pi, lo - base, lo - j * block_rows, hi - lo))
            base += valid
        return out

    def body(*refs):
        n_a = len(arrays)
        in_refs, small_ref = refs[:n_a], refs[n_a]
        out_refs, small_sum = refs[n_a + 1:n_a + 1 + n_t], refs[n_a + 1 + n_t]
        bufs, small_buf = refs[n_a + 2 + n_t:n_a + 2 + 5 * n_t], refs[n_a + 2 + 5 * n_t]
        s1_sems, r1_sems, s2_sems, r2_sems, small_send, small_recv = refs[n_a + 3 + 5 * n_t:]
        x, y, c = _position()
        chip = 2 * x + y

        def put(t, dst, j, add=None):
            for ai, src_row, dst_row, size in pieces(t, j):
                v = in_refs[ai][pl.ds(src_row, size), :]
                if add is not None:
                    v = v + add[pl.ds(dst_row, size), :].astype(F32)
                dst[pl.ds(dst_row, size), :] = v.astype(dst.dtype)

        def swap(t, k):
            send1, recv1 = bufs[4 * t], bufs[4 * t + 1]
            return pltpu.make_async_remote_copy(src_ref=send1.at[k], dst_ref=recv1.at[k], send_sem=s1_sems.at[4 * t + k],
                                                recv_sem=r1_sems.at[4 * t + k], device_id=(x, y, 1 - c), device_id_type=MESH)

        to_x, to_y, to_diag = 2 * (1 - x) + y, 2 * x + (1 - y), 2 * (1 - x) + (1 - y)
        x_dev, y_dev = (1 - x, y, c), (x, 1 - y, c)

        def half(ref, h):
            total = ref.shape[0]
            split = total // 2 // ROW_TILE_BF16 * ROW_TILE_BF16
            return ref.at[pl.ds(0, split)] if h == 0 else ref.at[pl.ds(split, total - split)]

        def hop(t, copy_id, src, dst, to):
            return pltpu.make_async_remote_copy(src_ref=src, dst_ref=dst, send_sem=s2_sems.at[HOPS * t + copy_id],
                                                recv_sem=r2_sems.at[HOPS * t + copy_id], device_id=to, device_id_type=MESH)

        def hops(t):
            send2, landing = bufs[4 * t + 2], bufs[4 * t + 3]
            return [hop(t, 0, half(send2.at[to_diag], 0), half(landing.at[0], 0), x_dev),
                    hop(t, 1, half(send2.at[to_diag], 1), half(landing.at[0], 1), y_dev),
                    hop(t, 2, half(send2.at[to_x], 0), half(landing.at[1], 0), x_dev),
                    hop(t, 3, half(send2.at[to_y], 1), half(landing.at[2], 1), y_dev),
                    hop(t, 4, half(send2.at[to_x], 1), half(landing.at[1], 1), x_dev),
                    hop(t, 5, half(send2.at[to_y], 0), half(landing.at[2], 0), y_dev)]

        def add_relay(t, slot, h):
            dst, src = half(bufs[4 * t + 2].at[slot], h), half(bufs[4 * t + 3].at[0], h)
            dst[...] = (dst[...].astype(F32) + src[...].astype(F32)).astype(dst.dtype)

        for t in range(n_t):
            send2 = bufs[4 * t + 2]
            pad = send2.shape[1] - tensors[t][1]
            if pad:
                send2[:, pl.ds(tensors[t][1], pad), :] = jnp.zeros((4, pad, send2.shape[2]), send2.dtype)
            for j in range(N_DEV):
                @pl.when((j & 1) != c)
                def _():
                    put(t, bufs[4 * t].at[j >> 1], j)
            for k in range(4):
                swap(t, k).start()

        small_buf[4 * x + 2 * y + c] = small_ref[...]
        _exchange_blocks([small_buf], small_send, small_recv)
        total = small_buf[0]
        for d in range(1, N_DEV):
            total = total + small_buf[d]
        small_sum[...] = total

        for t in range(n_t):
            recv1 = bufs[4 * t + 1]
            for k in range(4):
                swap(t, k).wait_recv()
                for j in (2 * k, 2 * k + 1):
                    @pl.when(((j & 1) == c) & (k != chip))
                    def _():
                        put(t, bufs[4 * t + 2].at[k], j, add=recv1.at[k])

                    @pl.when(((j & 1) == c) & (k == chip))
                    def _():
                        put(t, out_refs[t], j, add=recv1.at[k])
            for cp in hops(t)[0:4]:
                cp.start()

        for t in range(n_t):
            cps = hops(t)
            cps[0].wait_recv()
            add_relay(t, to_y, 0)
            cps[5].start()
            cps[1].wait_recv()
            add_relay(t, to_x, 1)
            cps[4].start()

        for t in range(n_t):
            cps, rows = hops(t), tensors[t][1]
            for first, second, slot in ((cps[2], cps[4], 1), (cps[3], cps[5], 2)):
                first.wait_recv()
                second.wait_recv()
                out_refs[t][...] += bufs[4 * t + 3][slot, pl.ds(0, rows), :].astype(F32)

        for t in range(n_t):
            for cp in hops(t):
                cp.wait_send()
            for k in range(4):
                swap(t, k).wait_send()

    scratch, out_shape = [], []
    for parts, block_rows in tensors:
        cols = parts[0][0].shape[1]
        tiled_rows = -(-block_rows // ROW_TILE_BF16) * ROW_TILE_BF16
        scratch += [pltpu.VMEM((4, block_rows, cols), MXU_DTYPE)] * 2
        scratch += [pltpu.VMEM((4, tiled_rows, cols), MXU_DTYPE), pltpu.VMEM((3, tiled_rows, cols), MXU_DTYPE)]
        out_shape.append(jax.ShapeDtypeStruct((block_rows, cols), F32))
    out_shape.append(jax.ShapeDtypeStruct(small.shape, F32))
    scratch += [pltpu.VMEM((N_DEV,) + small.shape, F32)] + [pltpu.SemaphoreType.DMA((4 * n_t,))] * 2
    scratch += [pltpu.SemaphoreType.DMA((HOPS * n_t,))] * 2 + _exchange_sems(1)
    return pl.pallas_call(
        body, name=name, in_specs=[VMEM_SPEC] * (len(arrays) + 1), out_specs=[VMEM_SPEC] * (n_t + 1), out_shape=out_shape,
        scratch_shapes=scratch, compiler_params=pltpu.CompilerParams(vmem_limit_bytes=VMEM_LIMIT),
    )(*arrays, small)


def _adamw_step(w, g, m, v):
    mn = ADAM_B1 * m + (1.0 - ADAM_B1) * g
    vn = ADAM_B2 * v + (1.0 - ADAM_B2) * jnp.square(g)
    m_hat = mn / (1.0 - ADAM_B1 ** ADAM_STEP)
    v_hat = vn / (1.0 - ADAM_B2 ** ADAM_STEP)
    return -ADAM_LR * (m_hat / (jnp.sqrt(v_hat) + ADAM_EPS) + ADAM_WD * w), mn, vn


def adamw_small(packed, first_rows, ws, gs, ms, vs):
    k = len(ws)
    given = [g for g in gs if g is not None]

    def body(*refs):
        packed_ref, w_refs, m_refs, v_refs = refs[0], refs[1:1 + k], refs[1 + k:1 + 2 * k], refs[1 + 2 * k:1 + 3 * k]
        g_refs, outs = iter(refs[1 + 3 * k:1 + 3 * k + len(given)]), refs[1 + 3 * k + len(given):]
        for i in range(k):
            rows, cols = w_refs[i].shape
            g = next(g_refs)[...] if gs[i] is not None else packed_ref[first_rows[i]:first_rows[i] + rows, 0:cols]
            outs[4 * i][...] = g
            outs[4 * i + 1][...], outs[4 * i + 2][...], outs[4 * i + 3][...] = _adamw_step(w_refs[i][...], g, m_refs[i][...],
                                                                                          v_refs[i][...])

    n_in = 1 + 3 * k + len(given)
    out = pl.pallas_call(body, name="adamw_small", in_specs=[VMEM_SPEC] * n_in, out_specs=[VMEM_SPEC] * (4 * k),
                         out_shape=[jax.ShapeDtypeStruct(w.shape, F32) for w in ws for _ in range(4)],
                         compiler_params=pltpu.CompilerParams(vmem_limit_bytes=VMEM_LIMIT))(packed, *ws, *ms, *vs, *given)
    return [out[4 * i:4 * i + 4] for i in range(k)]


def adamw_w_in(w, g_t, m, v):
    def body(w_ref, g_ref, m_ref, v_ref, go_ref, d_ref, nm_ref, nv_ref):
        g = g_ref[...]
        go_ref[:, 0, :] = g
        d_ref[:, 0, :], nm_ref[:, 0, :], nv_ref[:, 0, :] = _adamw_step(w_ref[:, 0, :], g, m_ref[:, 0, :], v_ref[:, 0, :])

    return pl.pallas_call(body, name="adamw_w_in", in_specs=[VMEM_SPEC] * 4, out_specs=[VMEM_SPEC] * 4,
                          out_shape=[jax.ShapeDtypeStruct(w.shape, F32)] * 4,
                          compiler_params=pltpu.CompilerParams(vmem_limit_bytes=VMEM_LIMIT))(w, g_t, m, v)


def _pad_rows(a, rows=8):
    return jnp.pad(a, ((0, rows - a.shape[0]), (0, 0)))


def _pad_lanes(a, lanes=128):
    return jnp.pad(a, ((0, 0), (0, lanes - a.shape[1])))


def kernel(x, meta_tokens, norm_w, w_in, conv_w, hg_lb_logits, hg_norm_w, gdn_A_log, gdn_dt_bias, gdn_norm_w, w_out, final_norm_w, loss_target, m_meta_tokens, m_norm_w, m_w_in, m_conv_w, m_hg_lb_logits, m_hg_norm_w, m_gdn_A_log, m_gdn_dt_bias, m_gdn_norm_w, m_w_out, m_final_norm_w, v_meta_tokens, v_norm_w, v_w_in, v_conv_w, v_hg_lb_logits, v_hg_norm_w, v_gdn_A_log, v_gdn_dt_bias, v_gdn_norm_w, v_w_out, v_final_norm_w):
    b, seq, _ = x.shape
    n = b * seq
    dev = 4 * lax.axis_index("x") + 2 * lax.axis_index("y") + lax.axis_index("c")
    col_shard = IN_COLS // N_DEV

    small_w = jnp.concatenate([_pad_lanes(meta_tokens, 256), _pad_rows(_pad_lanes(conv_w[0], 256))], axis=0)
    w_t, w_out_g, small_g = gather_weights(jnp.transpose(w_in, (2, 0, 1)), w_out[0], small_w, AB_PAD - 2 * HEADS)
    meta_g = small_g[:, 0:N_META, 0:D_MODEL // N_DEV]
    conv_g = small_g[:, N_META:N_META + CONV_TAPS, 0:QKV // N_DEV]
    w_out_full = w_out_g.reshape(2 * WIDTH, D_MODEL)
    cw = jnp.transpose(conv_g, (1, 0, 2)).reshape(CONV_TAPS, QKV)
    meta = jnp.transpose(meta_g, (1, 0, 2)).reshape(N_META, D_MODEL)
    alog = _pad_lanes(gdn_A_log)
    dtb = _pad_lanes(gdn_dt_bias)
    fw = final_norm_w.reshape(1, D_MODEL)

    h0 = jnp.concatenate([jnp.zeros((CHUNK - N_META, D_MODEL), F32), meta], axis=0)
    x2 = x.reshape(n, D_MODEL)
    phg, pgd, pab, phg0, pgd0, pab0, u0 = in_proj(x2, h0, norm_w, w_t)
    phg3, pgd3, pab3 = phg.reshape(b, seq, 4 * WIDTH), pgd.reshape(b, seq, 4 * WIDTH), pab.reshape(b, seq, AB_PAD)
    hg_loc, hg_lead = hg_local_fwd(phg3, phg0, hg_lb_logits)
    gd_loc, gd_inv, gd_lead, gd_inv0 = gd_local_fwd(pgd3, pgd0, pab3, pab0, cw, alog, dtb)
    (y_hg, s_hg), (y_gd, s_gd) = run_scans([hg_scan_fwd(phg3, phg0, hg_loc, hg_lead, hg_norm_w),
                                            gd_scan_fwd(pgd3, pgd0, gd_loc, gd_lead, gdn_norm_w)],
                                           seq // (SCAN_CHUNKS_FWD * CHUNK), "scans")

    dh2, dy_hg, dy_gd, g_w_out, loss_part, g_fw = out_proj_loss(
        x2, loss_target.reshape(n, D_MODEL), y_hg.reshape(n, WIDTH), y_gd.reshape(n, WIDTH), w_out_full, fw)

    hb, gb = run_scans([hg_scan_bwd(phg3, phg0, hg_loc, hg_lead, hg_norm_w, s_hg, dy_hg.reshape(b, seq, WIDTH)),
                        gd_scan_bwd(pgd3, pgd0, gd_loc, gd_lead, gdn_norm_w, s_gd, dy_gd.reshape(b, seq, WIDTH))],
                       seq // (SCAN_CHUNKS * CHUNK), "scans_bwd")
    dphg, dphg0, g_lb = hg_local_bwd(phg3, phg0, hg_lb_logits, hb[0:6], hb[6:12])
    g_hg_nw = hb[12]
    dpgd, dpab, dpgd0, dpab0, g_cw, g_alog, g_dtb = gd_local_bwd(pgd3, pgd0, pab3, pab0, cw, alog, dtb, gd_inv, gd_inv0,
                                                                 gb[0:6], gb[6], gb[7:13], gb[13])
    g_gd_nw = gb[14]
    dphg, dpgd, dpab = dphg.reshape(n, 4 * WIDTH), dpgd.reshape(n, 4 * WIDTH), dpab.reshape(n, AB_PAD)

    grad_x, dh0, g_nw, g_w_hg, g_w_gd, g_w_ab = in_proj_bwd(dphg, dpgd, dpab, w_t, x2, dh2, norm_w, h0, u0, dphg0, dpgd0, dpab0)

    small = jnp.concatenate([
        g_nw.reshape(8, 128), g_lb.reshape(8, 128), _pad_rows(g_hg_nw), _pad_rows(g_alog), _pad_rows(g_dtb), _pad_rows(g_gd_nw),
        g_fw.reshape(8, 128), g_cw.reshape(48, 128),
        dh0[CHUNK - N_META:CHUNK].reshape(128, 128), loss_part], axis=0)
    g_w_in_t, g_w_out, small = reduce_gradients(
        [([(g_w_hg, 4 * WIDTH), (g_w_gd, 4 * WIDTH), (g_w_ab, 2 * HEADS)], col_shard),
         ([(g_w_out, 2 * WIDTH)], (2 * WIDTH) // N_DEV)], small, "reduce_gradients")
    g_cw_full = small[56:104].reshape(CONV_TAPS, QKV)
    g_meta_full = small[104:232].reshape(N_META, D_MODEL)
    loss = small[232, 0]
    g_conv = lax.dynamic_slice_in_dim(g_cw_full, dev * (QKV // N_DEV), QKV // N_DEV, axis=1)
    g_meta = lax.dynamic_slice_in_dim(g_meta_full, dev * (D_MODEL // N_DEV), D_MODEL // N_DEV, axis=1)

    names = ["meta_tokens", "norm_w", "w_in", "conv_w", "hg_lb_logits", "hg_norm_w", "gdn_A_log", "gdn_dt_bias",
             "gdn_norm_w", "w_out", "final_norm_w"]
    weights = [meta_tokens, norm_w, w_in, conv_w, hg_lb_logits, hg_norm_w, gdn_A_log, gdn_dt_bias, gdn_norm_w, w_out,
               final_norm_w]
    moms = [m_meta_tokens, m_norm_w, m_w_in, m_conv_w, m_hg_lb_logits, m_hg_norm_w, m_gdn_A_log, m_gdn_dt_bias,
            m_gdn_norm_w, m_w_out, m_final_norm_w]
    vars_ = [v_meta_tokens, v_norm_w, v_w_in, v_conv_w, v_hg_lb_logits, v_hg_norm_w, v_gdn_A_log, v_gdn_dt_bias,
             v_gdn_norm_w, v_w_out, v_final_norm_w]
    gradient = [g_meta, 0, None, g_conv, 8, 16, 24, 32, 40, g_w_out, 48]
    shape2d = [g_meta.shape, (8, 128), None, g_conv.shape, (8, 128), (1, DH), (1, HEADS), (1, HEADS), (1, DH), g_w_out.shape,
               (8, 128)]
    i_w_in = names.index("w_in")
    others = [i for i in range(len(names)) if i != i_w_in]
    in_rows = lambda i: isinstance(gradient[i], int)
    stepped = adamw_small(small, [gradient[i] if in_rows(i) else None for i in others],
                          [weights[i].reshape(shape2d[i]) for i in others], [None if in_rows(i) else gradient[i] for i in others],
                          [moms[i].reshape(shape2d[i]) for i in others], [vars_[i].reshape(shape2d[i]) for i in others])
    results = {i: [a.reshape(weights[i].shape) for a in stepped[j]] for j, i in enumerate(others)}
    to3, back = (lambda a: jnp.transpose(a, (2, 0, 1))), (lambda a: jnp.transpose(a, (1, 2, 0)))
    results[i_w_in] = [back(a) for a in adamw_w_in(to3(w_in), g_w_in_t, to3(m_w_in), to3(v_w_in))]
    grads, deltas, new_ms, new_vs = zip(*(results[i] for i in range(len(names))))
    return (loss, grad_x.reshape(x.shape), *grads, *deltas, *new_ms, *new_vs)
```

```python
import jax
import jax.numpy as jnp
import numpy as np
from jax import lax
from jax.experimental import pallas as pl
from jax.experimental.pallas import tpu as pltpu

F32 = jnp.float32
BF16 = jnp.bfloat16
MXU_DTYPE = BF16

D_MODEL = 1024
N_META = 16
CHUNK = 64
SUB = 16
ROW_TILE_BF16 = 16
HEADS = 4
DH = 128
WIDTH = HEADS * DH
QKV = 3 * WIDTH
CONV_TAPS = 4
HALO = 8
EPS = 1e-6
IN_COLS = 4 * WIDTH + 4 * WIDTH + 2 * HEADS
AB_PAD = 128
N_DEV = 8
LOCAL_CHUNKS = 4
SCAN_CHUNKS_FWD = 4
SCAN_CHUNKS = 2
VMEM_LIMIT = 56 * 1024 * 1024
VMEM_LIMIT_LARGE = 60 * 1024 * 1024

ADAM_LR = 0.001
ADAM_B1 = 0.9
ADAM_B2 = 0.999
ADAM_EPS = 1e-08
ADAM_WD = 0.01
ADAM_STEP = 10

VMEM_SPEC = pl.BlockSpec(memory_space=pltpu.VMEM)
MESH = pl.DeviceIdType.MESH


def _mm_tn(a, b):
    return lax.dot_general(a.astype(MXU_DTYPE), b.astype(MXU_DTYPE), (((0,), (0,)), ((), ())), preferred_element_type=F32)


def _nn(a, b):
    return lax.dot_general(a.astype(MXU_DTYPE), b.astype(MXU_DTYPE), (((2,), (1,)), ((0,), (0,))), preferred_element_type=F32)


def _nt(a, b):
    return lax.dot_general(a.astype(MXU_DTYPE), b.astype(MXU_DTYPE), (((2,), (2,)), ((0,), (0,))), preferred_element_type=F32)


def _t(a):
    return jnp.swapaxes(a, 1, 2)


@jax.custom_vjp
def _bmm(a, b):
    return _nn(a, b)


_bmm.defvjp(lambda a, b: (_nn(a, b), (a, b)), lambda saved, d: (_nt(d, saved[1]), _nn(_t(saved[0]), d)))


@jax.custom_vjp
def _bmm_nt(a, b):
    return _nt(a, b)


_bmm_nt.defvjp(lambda a, b: (_nt(a, b), (a, b)), lambda saved, d: (_nn(d, saved[1]), _nn(_t(d), saved[0])))


@jax.custom_vjp
def _bmm_tn(a, b):
    return _nn(_t(a), b)


_bmm_tn.defvjp(lambda a, b: (_nn(_t(a), b), (a, b)), lambda saved, d: (_nt(saved[1], d), _nn(saved[0], d)))


def _iota2(n, m):
    return lax.broadcasted_iota(jnp.int32, (n, m), 0), lax.broadcasted_iota(jnp.int32, (n, m), 1)


def _silu(x):
    return x * jax.nn.sigmoid(x)


def _gated_norm(o, z, nw):
    return o * lax.rsqrt(jnp.mean(o * o, axis=-1, keepdims=True) + EPS) * nw * _silu(z)


def _heads(a, nb):
    return jnp.stack([a[c * CHUNK:(c + 1) * CHUNK, h * DH:(h + 1) * DH] for c in range(nb) for h in range(HEADS)], axis=0)


def _unheads(a3, nb):
    return jnp.concatenate(
        [jnp.concatenate([a3[c * HEADS + h] for h in range(HEADS)], axis=1) for c in range(nb)], axis=0)


def _split3(x):
    hi = x.astype(BF16)
    r1 = x - hi.astype(F32)
    mid = r1.astype(BF16)
    return hi, mid, (r1 - mid.astype(F32)).astype(BF16)


def _summation_matrices(pattern, n_out):
    s = pattern(np.arange(n_out)[:, None], np.arange(CHUNK)[None, :]).astype(np.float32)
    return jnp.asarray(np.tile(s, (1, 3)), BF16), jnp.asarray(np.tile(s.T, (1, 2)), BF16)


def _select_rows(mats, chunks):
    width = chunks[0].shape[1]
    out = _summation(*mats, jnp.concatenate(chunks, axis=1))
    return [out[:, c * width:(c + 1) * width] for c in range(len(chunks))]


def _summation_impl(s, v):
    return jnp.dot(s, jnp.concatenate(_split3(v), axis=0), preferred_element_type=F32)


@jax.custom_vjp
def _summation(s, s_t, v):
    return _summation_impl(s, v)


def _summation_fwd(s, s_t, v):
    return _summation_impl(s, v), s_t


def _summation_bwd(s_t, d):
    hi = d.astype(BF16)
    return None, None, jnp.dot(s_t, jnp.concatenate([hi, (d - hi.astype(F32)).astype(BF16)], axis=0),
                               preferred_element_type=F32)


_summation.defvjp(_summation_fwd, _summation_bwd)


def _chunks(x, nb):
    return [x[c * CHUNK:(c + 1) * CHUNK] for c in range(nb)]


def _running_sum(i, j):
    return j <= i


HG_LEVELS = 6


def _hg_sums(i, j):
    lvl, t = i >> HG_LEVELS, i & (CHUNK - 1)
    last = t
    for l in range(1, HG_LEVELS + 1):
        width = HG_LEVELS + 1 - l
        last = np.where(lvl == l, ((t >> width) << width) + (CHUNK >> l) - 1, last)
    return j <= last


def _level_operand(sh, q3, k3, x):
    def second():
        return ((lax.broadcasted_iota(jnp.int32, (CHUNK, DH), 0) >> sh) & 1) == 1

    def forward(q3, k3, x):
        decay = jnp.exp(-jnp.abs(x))
        out = jnp.where(second(), q3, k3) * decay
        return out, (decay, out)

    def backward(saved, d):
        decay, out = saved
        d_side, t = d * decay, d * out
        return jnp.where(second(), d_side, 0.0), jnp.where(second(), 0.0, d_side), jnp.where(second(), t, -t)

    operand = jax.custom_vjp(lambda q3, k3, x: forward(q3, k3, x)[0])
    operand.defvjp(forward, backward)
    return operand(q3, k3, x)


def hg_local(p, logits, sum_mats):
    nb = p.shape[0] // CHUNK
    l0, l1 = logits[0:1], logits[1:2]
    mx = jnp.maximum(l0, l1)
    e0, e1 = jnp.exp(l0 - mx), jnp.exp(l1 - mx)
    lb = e0 / (e0 + e1)
    q = _silu(p[:, 0:WIDTH])
    f = lb + (1.0 - lb) * jax.nn.sigmoid(p[:, WIDTH:2 * WIDTH])
    k = 1.0 - f
    logf = jnp.log(f)
    sums = _select_rows(sum_mats, _chunks(logf, nb))
    level = lambda l: _heads(jnp.concatenate([s[l * CHUNK:(l + 1) * CHUNK] for s in sums], axis=0), nb)
    q3, k3, v3, g3 = _heads(q, nb), _heads(k, nb), _heads(p[:, 2 * WIDTH:3 * WIDTH], nb), level(0)
    r, c = _iota2(CHUNK, CHUNK)
    a = jnp.where(r == c, _bmm_nt(q3, k3), 0.0)
    for l in range(1, HG_LEVELS + 1):
        sh = HG_LEVELS - l
        qk = _level_operand(sh, q3, k3, g3 - level(l))
        pair = ((r >> (sh + 1)) == (c >> (sh + 1))) & (((r >> sh) & 1) == 1) & (((c >> sh) & 1) == 0)
        a = a + jnp.where(pair, _bmm_nt(qk, qk), 0.0)
    o = _bmm(a, v3)
    glast = g3[:, CHUNK - 1:CHUNK, :]
    egs = tuple(jnp.concatenate([jnp.exp(glast[c * HEADS + h]) for h in range(HEADS)], axis=1) for c in range(nb))
    return _unheads(q3 * jnp.exp(g3), nb), _unheads(k3 * jnp.exp(glast - g3), nb), _unheads(o, nb), egs


def hg_scan(q_in, k_out, v, eg, o_intra, z, nw, st):
    o = o_intra + _bmm_nt(q_in, st)
    return _gated_norm(o, z, nw), st * eg + _bmm_tn(v, k_out)


def _tri_y_impl(a):
    r, c = _iota2(CHUNK, CHUNK)
    same16 = (r // SUB) == (c // SUB)
    same32 = (r // (2 * SUB)) == (c // (2 * SUB))
    a0 = jnp.where(same16, a, 0.0)
    y = -a0
    pw = _bmm(a0, a0)
    for _ in range(2):
        y = y + pw + _bmm(y, pw)
        pw = _bmm(pw, pw)
    y = y + pw + _bmm(y, pw)
    for ak in (jnp.where(same32 & jnp.logical_not(same16), a, 0.0), jnp.where(same32, 0.0, a)):
        m = ak + _bmm(y, ak)
        y = y - (m + _bmm(m, y))
    return y


@jax.custom_vjp
def _tri_y(a):
    return _tri_y_impl(a)


def _tri_y_fwd(a):
    y = _tri_y_impl(a)
    return y, y


def _tri_y_bwd(y, dy):
    n = dy + _bmm_tn(y, dy)
    return (-(n + _bmm_nt(n, y)),)


_tri_y.defvjp(_tri_y_fwd, _tri_y_bwd)


def _saved_inverse(y):
    @jax.custom_vjp
    def inverse(a):
        return y

    inverse.defvjp(lambda a: (y, None), lambda _, dy: _tri_y_bwd(y, dy))
    return inverse


def _head_rows(a3, nb):
    return jnp.concatenate([a3[g] for g in range(nb * HEADS)], axis=0)


def _rows_down(x, s):
    rows = x.shape[0]

    @jax.custom_vjp
    def rotate(v):
        return pltpu.roll(v, s, 0)

    rotate.defvjp(lambda v: (pltpu.roll(v, s, 0), None), lambda _, d: (pltpu.roll(d, rows - s, 0),))
    return rotate(x)


def gd_local(xx, ab, cw, alog, dtb, sum_mats, inverse=_tri_y):
    n = ab.shape[0]
    nb = n // CHUNK
    conv = cw[CONV_TAPS - 1:CONV_TAPS] * xx[HALO:HALO + n]
    for j in range(CONV_TAPS - 1):
        conv = conv + cw[j:j + 1] * _rows_down(xx, CONV_TAPS - 1 - j)[HALO:HALO + n]
    act = _silu(conv)
    x = ab + dtb
    g_all = -jnp.exp(alog) * (jnp.maximum(x, 0.0) + jnp.log1p(jnp.exp(-jnp.abs(x))))
    beta_all = jax.nn.sigmoid(ab)
    gam_all = jnp.concatenate(_select_rows(sum_mats, _chunks(g_all, nb)), axis=0)
    q3, k3, v3 = _heads(act[:, 0:WIDTH], nb), _heads(act[:, WIDTH:2 * WIDTH], nb), _heads(act[:, 2 * WIDTH:QKV], nb)
    q3 = q3 * lax.rsqrt(jnp.sum(q3 * q3, axis=-1, keepdims=True) + EPS) * (DH ** -0.5)
    k3 = k3 * lax.rsqrt(jnp.sum(k3 * k3, axis=-1, keepdims=True) + EPS)
    pairs = [(c, h) for c in range(nb) for h in range(HEADS)]
    beta = jnp.stack([beta_all[c * CHUNK:(c + 1) * CHUNK, HEADS + h:HEADS + h + 1] for c, h in pairs], axis=0)
    gam = jnp.stack([gam_all[c * CHUNK:(c + 1) * CHUNK, h:h + 1] for c, h in pairs], axis=0)
    gam_t = [gam_all[c * CHUNK:(c + 1) * CHUNK].T for c in range(nb)]
    gam_row = jnp.stack([gam_t[c][h:h + 1, :] for c, h in pairs], axis=0)
    glast = gam[:, CHUNK - 1:CHUNK, :]
    r, c = _iota2(CHUNK, CHUNK)
    dec = jnp.exp(jnp.where(c < r, gam - gam_row, -jnp.inf))
    y = inverse(beta * _bmm_nt(k3, k3) * dec)
    eg = jnp.exp(gam)
    rhs = jnp.concatenate([beta * v3, (beta * eg) * k3], axis=2)
    sol = rhs + _bmm(y, rhs)
    qk = _bmm_nt(q3, k3) * jnp.where(r == c, 1.0, dec)
    eas = tuple(jnp.exp(gam_all[(c + 1) * CHUNK - 1:(c + 1) * CHUNK]) for c in range(nb))
    return (_unheads(sol[:, :, 0:DH], nb), _unheads(sol[:, :, DH:2 * DH], nb), _unheads(q3 * eg, nb),
            _unheads(k3 * jnp.exp(glast - gam), nb), _head_rows(qk, nb), eas), _head_rows(y, nb)


def gd_scan(uu, ww, qe, ke, qk, ea, z, nw, s):
    u = uu - _bmm(ww, s)
    o = _bmm(qe, s) + _bmm(qk, u)
    return _gated_norm(o, z, nw), ea * s + _bmm_tn(ke, u)


def _cparams(*sem):
    return pltpu.CompilerParams(dimension_semantics=sem, vmem_limit_bytes=VMEM_LIMIT)


def _row_tile(n):
    for t in (512, 256, 128, 64):
        if n % t == 0:
            return t
    raise ValueError(f"unsupported token count {n}")


def _w_in_specs():
    once = pl.Buffered(1)
    return [pl.BlockSpec((4 * WIDTH, D_MODEL), lambda *i: (0, 0), pipeline_mode=once),
            pl.BlockSpec((4 * WIDTH, D_MODEL), lambda *i: (1, 0), pipeline_mode=once),
            pl.BlockSpec((AB_PAD, D_MODEL), lambda *i: (8 * WIDTH // AB_PAD, 0), pipeline_mode=once)]


def in_proj(h, h0, norm_w, w_t):
    n = h.shape[0]
    tm = _row_tile(n)
    nt = (((1,), (1,)), ((), ()))

    def body(h_ref, h0_ref, nw_ref, whg_ref, wgd_ref, wab_ref, phg_ref, pgd_ref, pab_ref, phg0_ref, pgd0_ref, pab0_ref, u0_ref):
        def project(x, hg_ref, gd_ref, ab_ref):
            u = (x * lax.rsqrt(jnp.mean(x * x, axis=-1, keepdims=True) + EPS) * nw_ref[...]).astype(MXU_DTYPE)
            hg_ref[...] = lax.dot_general(u, whg_ref[...], nt, preferred_element_type=F32)
            gd_ref[...] = lax.dot_general(u, wgd_ref[...], nt, preferred_element_type=F32)
            ab_ref[...] = lax.dot_general(u, wab_ref[...], nt, preferred_element_type=F32)
            return u

        @pl.when(pl.program_id(0) == 0)
        def _():
            u0_ref[...] = project(h0_ref[...], phg0_ref, pgd0_ref, pab0_ref)

        project(h_ref[...], phg_ref, pgd_ref, pab_ref)

    n0 = h0.shape[0]
    row = lambda w: pl.BlockSpec((tm, w), lambda i: (i, 0))
    lead = lambda w: pl.BlockSpec((n0, w), lambda i: (0, 0))
    widths = [4 * WIDTH, 4 * WIDTH, AB_PAD]
    return pl.pallas_call(
        body, grid=(n // tm,), name="in_proj",
        in_specs=[row(D_MODEL), lead(D_MODEL), pl.BlockSpec(norm_w.shape, lambda i: (0, 0))] + _w_in_specs(),
        out_specs=[row(w) for w in widths] + [lead(w) for w in widths] + [lead(D_MODEL)],
        out_shape=[jax.ShapeDtypeStruct((n, w), F32) for w in widths] + [jax.ShapeDtypeStruct((n0, w), F32) for w in widths]
        + [jax.ShapeDtypeStruct((n0, D_MODEL), MXU_DTYPE)],
        compiler_params=_cparams("arbitrary"),
    )(h, h0, norm_w, w_t, w_t, w_t)


def out_proj_loss(x, tgt, y_hg, y_gd, w_out, fw):
    n = x.shape[0]
    tm = _row_tile(n)
    inv_d = 1.0 / D_MODEL

    def body(x_ref, t_ref, yh_ref, yg_ref, w_ref, fw_ref, dh_ref, dyh_ref, dyg_ref, dw_ref, loss_ref, dfw_ref):
        @pl.when(pl.program_id(0) == 0)
        def _():
            dw_ref[...] = jnp.zeros_like(dw_ref)
            loss_ref[...] = jnp.zeros_like(loss_ref)
            dfw_ref[...] = jnp.zeros_like(dfw_ref)

        yh, yg = yh_ref[...], yg_ref[...]
        wa, wb = w_ref[0:WIDTH, :], w_ref[WIDTH:2 * WIDTH, :]
        h2 = x_ref[...] + jnp.dot(yh, wa, preferred_element_type=F32) + jnp.dot(yg, wb, preferred_element_type=F32)
        r2 = lax.rsqrt(jnp.mean(h2 * h2, axis=-1, keepdims=True) + EPS)
        nrm = h2 * r2
        fwv = fw_ref[...]
        err = nrm * fwv - t_ref[...]
        loss_ref[...] += jnp.full(loss_ref.shape, 0.5 * inv_d * jnp.sum(err * err), F32)
        dout = err * inv_d
        dfw_ref[...] += jnp.sum(dout * nrm, axis=0, keepdims=True)
        dn = dout * fwv
        dh2 = r2 * (dn - nrm * jnp.mean(dn * nrm, axis=-1, keepdims=True))
        dh_ref[...] = dh2
        dhb = dh2.astype(MXU_DTYPE)
        dyh_ref[...] = lax.dot_general(dhb, wa, (((1,), (1,)), ((), ())), preferred_element_type=F32)
        dyg_ref[...] = lax.dot_general(dhb, wb, (((1,), (1,)), ((), ())), preferred_element_type=F32)
        dw_ref[0:WIDTH, :] += lax.dot_general(yh, dhb, (((0,), (0,)), ((), ())), preferred_element_type=F32)
        dw_ref[WIDTH:2 * WIDTH, :] += lax.dot_general(yg, dhb, (((0,), (0,)), ((), ())), preferred_element_type=F32)

    row = lambda w: pl.BlockSpec((tm, w), lambda i: (i, 0))
    full = lambda s: pl.BlockSpec(s, lambda i: (0, 0))
    return pl.pallas_call(
        body, grid=(n // tm,), name="out_proj_loss",
        in_specs=[row(D_MODEL), row(D_MODEL), row(WIDTH), row(WIDTH), full(w_out.shape), full(fw.shape)],
        out_specs=[row(D_MODEL), row(WIDTH), row(WIDTH), full((2 * WIDTH, D_MODEL)), full((8, 128)), full((1, D_MODEL))],
        out_shape=[jax.ShapeDtypeStruct((n, D_MODEL), F32), jax.ShapeDtypeStruct((n, WIDTH), F32),
                   jax.ShapeDtypeStruct((n, WIDTH), F32), jax.ShapeDtypeStruct((2 * WIDTH, D_MODEL), F32),
                   jax.ShapeDtypeStruct((8, 128), F32), jax.ShapeDtypeStruct((1, D_MODEL), F32)],
        compiler_params=_cparams("arbitrary"),
    )(x, tgt, y_hg, y_gd, w_out, fw)


def in_proj_bwd(dphg, dpgd, dpab, w_t, h, dh2, norm_w, h0, u0, dphg0, dpgd0, dpab0):
    n = h.shape[0]
    tm = _row_tile(n)
    steps = n // tm

    def body(dphg_ref, dpgd_ref, dpab_ref, whg_ref, wgd_ref, wab_ref, h_ref, dh2_ref, nw_ref, h0_ref, u0_ref, d0hg_ref,
             d0gd_ref, d0ab_ref, dx_ref, dx0_ref, dnw_ref, ghg_ref, ggd_ref, gab_ref, acc_hg, acc_gd, acc_ab):
        i = pl.program_id(0)
        nwv = nw_ref[...]

        def norm_bwd(dps, x):
            du = jnp.dot(dps[0], whg_ref[...], preferred_element_type=F32)
            du += jnp.dot(dps[1], wgd_ref[...], preferred_element_type=F32)
            du += jnp.dot(dps[2], wab_ref[...], preferred_element_type=F32)
            r = lax.rsqrt(jnp.mean(x * x, axis=-1, keepdims=True) + EPS)
            nrm = x * r
            dn = du * nwv
            return r * (dn - nrm * jnp.mean(dn * nrm, axis=-1, keepdims=True)), nrm, jnp.sum(du * nrm, axis=0, keepdims=True)

        def accumulate(dps, u, first):
            for acc, dp in zip((acc_hg, acc_gd, acc_ab), dps):
                step = min(acc.shape[0], 512)
                for lo in range(0, acc.shape[0], step):
                    part = _mm_tn(dp[:, lo:lo + step], u)
                    acc[lo:lo + step, :] = part if first else acc[lo:lo + step, :] + part

        @pl.when(i == 0)
        def _():
            dps0 = (d0hg_ref[...], d0gd_ref[...], d0ab_ref[...])
            dx0_ref[...], _, dnw_ref[...] = norm_bwd(dps0, h0_ref[...])
            accumulate(dps0, u0_ref[...], True)

        dps = (dphg_ref[...], dpgd_ref[...], dpab_ref[...])
        dx, nrm, dnw = norm_bwd(dps, h_ref[...])
        dx_ref[...] = dh2_ref[...] + dx
        dnw_ref[...] += dnw
        accumulate(dps, (nrm * nwv).astype(MXU_DTYPE), False)

        @pl.when(i == steps - 1)
        def _():
            pltpu.sync_copy(acc_hg, ghg_ref)
            pltpu.sync_copy(acc_gd, ggd_ref)
            pltpu.sync_copy(acc_ab, gab_ref)

    row = lambda w: pl.BlockSpec((tm, w), lambda i: (i, 0))
    full = lambda a: pl.BlockSpec(a.shape, lambda i: (0, 0), pipeline_mode=pl.Buffered(1))
    anywhere = pl.BlockSpec(memory_space=pl.ANY)
    return pl.pallas_call(
        body, grid=(steps,), name="in_proj_bwd",
        in_specs=[row(4 * WIDTH), row(4 * WIDTH), row(AB_PAD)] + _w_in_specs() + [row(D_MODEL), row(D_MODEL), full(norm_w),
                                                                                   full(h0), full(u0), full(dphg0), full(dpgd0),
                                                                                   full(dpab0)],
        out_specs=[row(D_MODEL), pl.BlockSpec(h0.shape, lambda i: (0, 0)), pl.BlockSpec((1, D_MODEL), lambda i: (0, 0)),
                   anywhere, anywhere, anywhere],
        out_shape=[jax.ShapeDtypeStruct((n, D_MODEL), F32), jax.ShapeDtypeStruct(h0.shape, F32),
                   jax.ShapeDtypeStruct((1, D_MODEL), F32), jax.ShapeDtypeStruct((4 * WIDTH, D_MODEL), F32),
                   jax.ShapeDtypeStruct((4 * WIDTH, D_MODEL), F32), jax.ShapeDtypeStruct((AB_PAD, D_MODEL), F32)],
        scratch_shapes=[pltpu.VMEM((4 * WIDTH, D_MODEL), F32), pltpu.VMEM((4 * WIDTH, D_MODEL), F32),
                        pltpu.VMEM((AB_PAD, D_MODEL), F32)],
        compiler_params=pltpu.CompilerParams(dimension_semantics=("arbitrary",), vmem_limit_bytes=VMEM_LIMIT_LARGE),
    )(dphg, dpgd, dpab, w_t, w_t, w_t, h, dh2, norm_w, h0, u0, dphg0, dpgd0, dpab0)


def _sds(shape, dtype=F32):
    return jax.ShapeDtypeStruct(shape, dtype)


def _pairs(b):
    return [(i, h) for i in range(b) for h in range(HEADS)]


def _load_slabs(ref, b, k):
    return jnp.stack([ref[i, k * CHUNK:(k + 1) * CHUNK, h * DH:(h + 1) * DH].astype(F32) for i, h in _pairs(b)], axis=0)


def _lead_slabs(a, b):
    return jnp.stack([a[:, h * DH:(h + 1) * DH].astype(F32) for _, h in _pairs(b)], axis=0)


def _rows(a3, i):
    return jnp.concatenate([a3[i * HEADS + h] for h in range(HEADS)], axis=1)


def _store_slabs(ref, a3, b, k):
    for i in range(b):
        ref[i, k * CHUNK:(k + 1) * CHUNK, :] = _rows(a3, i).astype(ref.dtype)


def _sum_rows(a3, b):
    out = _rows(a3, 0)
    for i in range(1, b):
        out = out + _rows(a3, i)
    return out


def _save_states(ref, s, b, k):
    for i in range(b):
        ref[i, k] = jnp.concatenate([s[i * HEADS + h] for h in range(HEADS)], axis=0).astype(ref.dtype)


def _load_states(ref, b, k):
    return jnp.stack([ref[i, k, h * DH:(h + 1) * DH, :].astype(F32) for i, h in _pairs(b)], axis=0)


def hg_local_fwd(p, p0, logits):
    b, seq, _ = p.shape
    rows = LOCAL_CHUNKS * CHUNK
    nreal = seq // CHUNK

    def body(p_ref, p0_ref, lg_ref, s_ref, st_ref, q_ref, k_ref, o_ref, eg_ref, q0_ref, k0_ref, o0_ref, eg0_ref):
        sum_mats = (s_ref[...], st_ref[...])

        @pl.when((pl.program_id(0) == 0) & (pl.program_id(1) == 0))
        def _():
            q_in, k_out, o0_ref[...], (eg0_ref[...],) = hg_local(p0_ref[...], lg_ref[...], sum_mats)
            q0_ref[...], k0_ref[...] = q_in.astype(MXU_DTYPE), k_out.astype(MXU_DTYPE)

        q_in, k_out, o_intra, egs = hg_local(p_ref[...], lg_ref[...], sum_mats)
        q_ref[...], k_ref[...], o_ref[...] = q_in.astype(MXU_DTYPE), k_out.astype(MXU_DTYPE), o_intra
        for c in range(LOCAL_CHUNKS):
            eg_ref[c] = egs[c]

    slab = pl.BlockSpec((None, rows, WIDTH), lambda s, g: (s, g, 0))
    const = lambda shape: pl.BlockSpec(shape, lambda s, g: (0, 0))
    lead_shapes = [(CHUNK, WIDTH)] * 3 + [(1, WIDTH)]
    sum_mats = _summation_matrices(_hg_sums, (HG_LEVELS + 1) * CHUNK)
    out = pl.pallas_call(
        body, grid=(b, seq // rows), name="hgrn2_local",
        in_specs=[pl.BlockSpec((None, rows, 4 * WIDTH), lambda s, g: (s, g, 0)), const(p0.shape), const(logits.shape)]
        + [const(a.shape) for a in sum_mats],
        out_specs=[slab, slab, slab, pl.BlockSpec((None, LOCAL_CHUNKS, 1, WIDTH), lambda s, g: (s, g, 0, 0))]
        + [const(s) for s in lead_shapes],
        out_shape=[_sds((b, seq, WIDTH), MXU_DTYPE)] * 2 + [_sds((b, seq, WIDTH)), _sds((b, nreal, 1, WIDTH))]
        + [_sds(lead_shapes[0], MXU_DTYPE)] * 2 + [_sds(lead_shapes[2]), _sds(lead_shapes[3])],
        compiler_params=_cparams("arbitrary", "arbitrary"),
    )(p, p0, logits, *sum_mats)
    return out[0:4], out[4:8]


def _hg_scan_args(b, k, q_ref, k_ref, o_ref, v_ref, z_ref, eg_ref):
    eg = jnp.stack([eg_ref[i, k, :, h * DH:(h + 1) * DH] for i, h in _pairs(b)], axis=0)
    return (_load_slabs(q_ref, b, k), _load_slabs(k_ref, b, k), _load_slabs(v_ref, b, k), eg, _load_slabs(o_ref, b, k),
            _load_slabs(z_ref, b, k))


def _hg_lead_args(b, q0_ref, k0_ref, o0_ref, p0_ref, eg0_ref):
    eg = jnp.stack([eg0_ref[:, h * DH:(h + 1) * DH] for _, h in _pairs(b)], axis=0)
    return (_lead_slabs(q0_ref[...], b), _lead_slabs(k0_ref[...], b), _lead_slabs(p0_ref[:, 2 * WIDTH:3 * WIDTH], b), eg,
            _lead_slabs(o0_ref[...], b), _lead_slabs(p0_ref[:, 3 * WIDTH:4 * WIDTH], b))


def _scan_specs(b, ng, reverse, chunks):
    group = (lambda i: ng - 1 - i) if reverse else (lambda i: i)
    slab = lambda lane_block: pl.BlockSpec((b, chunks * CHUNK, WIDTH), lambda i: (0, group(i), lane_block))
    per_chunk = lambda *tail: pl.BlockSpec((b, chunks) + tail, lambda i: (0, group(i)) + (0,) * len(tail))
    const = lambda a: pl.BlockSpec(a.shape, lambda i: (0,) * a.ndim)
    return slab, per_chunk, const


def run_scans(parts, nc, name):
    n_in = [len(p["args"]) for p in parts]
    n_out = [len(p["out_shape"]) for p in parts]
    n_scr = [len(p["scratch_shapes"]) for p in parts]

    def body(*refs):
        ins, outs, scr = refs[:sum(n_in)], refs[sum(n_in):sum(n_in) + sum(n_out)], refs[sum(n_in) + sum(n_out):]
        for i, part in enumerate(parts):
            part["body"](*ins[sum(n_in[:i]):sum(n_in[:i + 1])], *outs[sum(n_out[:i]):sum(n_out[:i + 1])],
                         *scr[sum(n_scr[:i]):sum(n_scr[:i + 1])])

    flat = lambda key: [v for p in parts for v in p[key]]
    out = pl.pallas_call(body, grid=(nc,), name=name, in_specs=flat("in_specs"), out_specs=flat("out_specs"),
                         out_shape=flat("out_shape"), scratch_shapes=flat("scratch_shapes"),
                         compiler_params=_cparams("arbitrary"))(*flat("args"))
    return [out[sum(n_out[:i]):sum(n_out[:i + 1])] for i in range(len(parts))]


def hg_scan_fwd(p, p0, local, lead, nw):
    b, seq, _ = p.shape
    q_in, k_out, o_intra, eg = local
    slab, per_chunk, const = _scan_specs(b, seq // (SCAN_CHUNKS_FWD * CHUNK), False, SCAN_CHUNKS_FWD)

    def body(q_ref, k_ref, o_ref, v_ref, z_ref, eg_ref, q0_ref, k0_ref, o0_ref, p0_ref, eg0_ref, nw_ref, y_ref, ss_ref, st):
        @pl.when(pl.program_id(0) == 0)
        def _():
            st[...] = hg_scan(*_hg_lead_args(b, q0_ref, k0_ref, o0_ref, p0_ref, eg0_ref), nw_ref[...], jnp.zeros(st.shape, F32))[1]

        s = st[...]
        for k in range(SCAN_CHUNKS_FWD):
            _save_states(ss_ref, s, b, k)
            y, s = hg_scan(*_hg_scan_args(b, k, q_ref, k_ref, o_ref, v_ref, z_ref, eg_ref), nw_ref[...], s)
            _store_slabs(y_ref, y, b, k)
        st[...] = s

    return dict(
        body=body, args=(q_in, k_out, o_intra, p, p, eg, lead[0], lead[1], lead[2], p0, lead[3], nw),
        in_specs=[slab(0), slab(0), slab(0), slab(2), slab(3), per_chunk(1, WIDTH)] + [const(a) for a in lead[0:3]]
        + [const(p0), const(lead[3]), const(nw)],
        out_specs=[slab(0), per_chunk(WIDTH, DH)],
        out_shape=[_sds((b, seq, WIDTH), MXU_DTYPE), _sds((b, seq // CHUNK, WIDTH, DH), MXU_DTYPE)],
        scratch_shapes=[pltpu.VMEM((b * HEADS, DH, DH), F32)])


def hg_scan_bwd(p, p0, local, lead, nw, ssave, dy):
    b, seq, _ = p.shape
    ng = seq // (SCAN_CHUNKS * CHUNK)
    q_in, k_out, o_intra, eg = local
    slab, per_chunk, const = _scan_specs(b, ng, True, SCAN_CHUNKS)

    def body(q_ref, k_ref, o_ref, v_ref, z_ref, eg_ref, q0_ref, k0_ref, o0_ref, p0_ref, eg0_ref, nw_ref, ss_ref, dy_ref,
             dq_ref, dk_ref, do_ref, dv_ref, dz_ref, deg_ref, dq0_ref, dk0_ref, do0_ref, dv0_ref, dz0_ref, deg0_ref, dnw_ref,
             dst):
        i = pl.program_id(0)

        @pl.when(i == 0)
        def _():
            dst[...] = jnp.zeros_like(dst)
            dnw_ref[...] = jnp.zeros_like(dnw_ref)

        ds = dst[...]
        for k in reversed(range(SCAN_CHUNKS)):
            args = _hg_scan_args(b, k, q_ref, k_ref, o_ref, v_ref, z_ref, eg_ref)
            _, vjp = jax.vjp(hg_scan, *args, nw_ref[...], _load_states(ss_ref, b, k))
            dq, dk, dv, deg, do, dz, dnw, ds = vjp((_load_slabs(dy_ref, b, k), ds))
            dnw_ref[...] += dnw
            for ref, val in ((dq_ref, dq), (dk_ref, dk), (do_ref, do), (dv_ref, dv), (dz_ref, dz)):
                _store_slabs(ref, val, b, k)
            for j in range(b):
                deg_ref[j, k] = _rows(deg, j)
        dst[...] = ds

        @pl.when(i == ng - 1)
        def _():
            args = _hg_lead_args(b, q0_ref, k0_ref, o0_ref, p0_ref, eg0_ref)
            _, vjp = jax.vjp(hg_scan, *args, nw_ref[...], jnp.zeros(dst.shape, F32))
            dq, dk, dv, deg, do, dz, dnw, _ = vjp((jnp.zeros((b * HEADS, CHUNK, DH), F32), ds))
            dnw_ref[...] += dnw
            for ref, val in ((dq0_ref, dq), (dk0_ref, dk), (do0_ref, do), (dv0_ref, dv), (dz0_ref, dz), (deg0_ref, deg)):
                ref[...] = _sum_rows(val, b)

    lead_out = [const(a) for a in lead[0:3]] + [const(lead[0]), const(lead[0]), const(lead[3])]
    return dict(
        body=body, args=(q_in, k_out, o_intra, p, p, eg, lead[0], lead[1], lead[2], p0, lead[3], nw, ssave, dy),
        in_specs=[slab(0), slab(0), slab(0), slab(2), slab(3), per_chunk(1, WIDTH)] + [const(a) for a in lead[0:3]]
        + [const(p0), const(lead[3]), const(nw), per_chunk(WIDTH, DH), slab(0)],
        out_specs=[slab(0)] * 5 + [per_chunk(1, WIDTH)] + lead_out + [const(nw)],
        out_shape=[_sds((b, seq, WIDTH))] * 3 + [_sds((b, seq, WIDTH), MXU_DTYPE)] * 2 + [_sds(eg.shape)]
        + [_sds((CHUNK, WIDTH))] * 5 + [_sds((1, WIDTH)), _sds(nw.shape)],
        scratch_shapes=[pltpu.VMEM((b * HEADS, DH, DH), F32)])


def _hg_local_vjp(sum_mats, p, logits, dq, dk, do, degs, dv, dz):
    _, vjp = jax.vjp(lambda p_, logits_: hg_local(p_, logits_, sum_mats), p, logits)
    dp, dlg = vjp((dq, dk, do, degs))
    return dp + jnp.concatenate([jnp.zeros((p.shape[0], 2 * WIDTH), F32), dv.astype(F32), dz.astype(F32)], axis=1), dlg


def hg_local_bwd(p, p0, logits, cot, cot0):
    b, seq, _ = p.shape
    rows = LOCAL_CHUNKS * CHUNK
    ng = seq // rows

    def body(p_ref, p0_ref, lg_ref, s_ref, st_ref, dq_ref, dk_ref, do_ref, dv_ref, dz_ref, deg_ref, dq0_ref, dk0_ref, do0_ref,
             dv0_ref, dz0_ref, deg0_ref, dp_ref, dp0_ref, dlg_ref):
        sum_mats = (s_ref[...], st_ref[...])

        @pl.when(pl.program_id(0) == 0)
        def _():
            dp0, dlg_ref[...] = _hg_local_vjp(sum_mats, p0_ref[...], lg_ref[...], dq0_ref[...], dk0_ref[...], do0_ref[...],
                                              (deg0_ref[...],), dv0_ref[...], dz0_ref[...])
            dp0_ref[...] = dp0.astype(MXU_DTYPE)

        degs = tuple(deg_ref[c] for c in range(LOCAL_CHUNKS))
        dp, dlg = _hg_local_vjp(sum_mats, p_ref[...], lg_ref[...], dq_ref[...], dk_ref[...], do_ref[...], degs, dv_ref[...],
                                dz_ref[...])
        dp_ref[...] = dp.astype(MXU_DTYPE)
        dlg_ref[...] += dlg

    slab = pl.BlockSpec((None, rows, WIDTH), lambda i: (i // ng, i % ng, 0))
    wide = pl.BlockSpec((None, rows, 4 * WIDTH), lambda i: (i // ng, i % ng, 0))
    const = lambda a: pl.BlockSpec(a.shape, lambda i: (0, 0))
    sum_mats = _summation_matrices(_hg_sums, (HG_LEVELS + 1) * CHUNK)
    return dict(
        body=body, args=(p, p0, logits, *sum_mats, *cot, *cot0),
        in_specs=[wide, const(p0), const(logits), const(sum_mats[0]), const(sum_mats[1]), slab, slab, slab, slab, slab,
                  pl.BlockSpec((None, LOCAL_CHUNKS, 1, WIDTH), lambda i: (i // ng, i % ng, 0, 0))] + [const(a) for a in cot0],
        out_specs=[wide, const(p0), const(logits)],
        out_shape=[_sds(p.shape, MXU_DTYPE), _sds(p0.shape, MXU_DTYPE), _sds(logits.shape)],
        scratch_shapes=[])


def _halo_block(g):
    return jnp.maximum((LOCAL_CHUNKS * CHUNK // HALO) * g - 1, 0)


def _gd_window(g, p_ref, halo_ref, p0_ref):
    halo = jnp.where(g == 0, p0_ref[CHUNK - HALO:CHUNK, 0:QKV], halo_ref[...])
    return jnp.concatenate([halo, p_ref[:, 0:QKV]], axis=0)


def _lead_window(p0_ref):
    return jnp.concatenate([jnp.zeros((HALO, QKV), F32), p0_ref[:, 0:QKV]], axis=0)


def gd_local_fwd(p, p0, ab, ab0, cw, alog, dtb):
    b, seq, _ = p.shape
    rows = LOCAL_CHUNKS * CHUNK
    nreal = seq // CHUNK

    def body(p_ref, halo_ref, p0_ref, ab_ref, ab0_ref, cw_ref, al_ref, dt_ref, s_ref, st_ref, u_ref, w_ref, qe_ref, ke_ref,
             qk_ref, ea_ref, inv_ref, u0_ref, w0_ref, qe0_ref, ke0_ref, qk0_ref, ea0_ref, inv0_ref):
        sum_mats = (s_ref[...], st_ref[...])

        @pl.when((pl.program_id(0) == 0) & (pl.program_id(1) == 0))
        def _():
            (u0_ref[...], ww, qe, ke, qk0_ref[...], (ea0_ref[...],)), inv0_ref[...] = gd_local(
                _lead_window(p0_ref), ab0_ref[...], cw_ref[...], al_ref[...], dt_ref[...], sum_mats, inverse=_tri_y_impl)
            w0_ref[...], qe0_ref[...], ke0_ref[...] = ww.astype(MXU_DTYPE), qe.astype(MXU_DTYPE), ke.astype(MXU_DTYPE)

        (uu, ww, qe, ke, qk, eas), inv = gd_local(_gd_window(pl.program_id(1), p_ref, halo_ref, p0_ref), ab_ref[...],
                                                  cw_ref[...], al_ref[...], dt_ref[...], sum_mats, inverse=_tri_y_impl)
        u_ref[...], w_ref[...], qe_ref[...], ke_ref[...] = uu, ww.astype(MXU_DTYPE), qe.astype(MXU_DTYPE), ke.astype(MXU_DTYPE)
        for c in range(LOCAL_CHUNKS):
            qk_ref[c] = qk[c * HEADS * CHUNK:(c + 1) * HEADS * CHUNK]
            inv_ref[c] = inv[c * HEADS * CHUNK:(c + 1) * HEADS * CHUNK]
            ea_ref[c] = eas[c]

    const = lambda shape: pl.BlockSpec(shape, lambda s, g: (0, 0))
    slab = pl.BlockSpec((None, rows, WIDTH), lambda s, g: (s, g, 0))
    mats = pl.BlockSpec((None, LOCAL_CHUNKS, HEADS * CHUNK, CHUNK), lambda s, g: (s, g, 0, 0))
    lead_out = [_sds((CHUNK, WIDTH))] + [_sds((CHUNK, WIDTH), MXU_DTYPE)] * 3 + [_sds((HEADS * CHUNK, CHUNK)), _sds((1, AB_PAD)),
                                                                                _sds((HEADS * CHUNK, CHUNK))]
    sum_mats = _summation_matrices(_running_sum, CHUNK)
    out = pl.pallas_call(
        body, grid=(b, seq // rows), name="gdn_local",
        in_specs=[pl.BlockSpec((None, rows, 4 * WIDTH), lambda s, g: (s, g, 0)),
                  pl.BlockSpec((None, HALO, QKV), lambda s, g: (s, _halo_block(g), 0)), const(p0.shape),
                  pl.BlockSpec((None, rows, AB_PAD), lambda s, g: (s, g, 0)), const(ab0.shape), const(cw.shape),
                  const(alog.shape), const(dtb.shape), const(sum_mats[0].shape), const(sum_mats[1].shape)],
        out_specs=[slab] * 4 + [mats, pl.BlockSpec((None, LOCAL_CHUNKS, 1, AB_PAD), lambda s, g: (s, g, 0, 0)), mats]
        + [const(s.shape) for s in lead_out],
        out_shape=[_sds((b, seq, WIDTH))] + [_sds((b, seq, WIDTH), MXU_DTYPE)] * 3
        + [_sds((b, nreal, HEADS * CHUNK, CHUNK)), _sds((b, nreal, 1, AB_PAD)), _sds((b, nreal, HEADS * CHUNK, CHUNK))] + lead_out,
        compiler_params=_cparams("arbitrary", "arbitrary"),
    )(p, p, p0, ab, ab0, cw, alog, dtb, *sum_mats)
    return out[0:6], out[6], out[7:13], out[13]


def _gd_scan_args(b, k, u_ref, w_ref, qe_ref, ke_ref, qk_ref, ea_ref, z_ref):
    qk = jnp.stack([qk_ref[i, k, h * CHUNK:(h + 1) * CHUNK, :] for i, h in _pairs(b)], axis=0)
    ea = jnp.stack([ea_ref[i, k, :, h:h + 1] for i, h in _pairs(b)], axis=0)
    return (_load_slabs(u_ref, b, k), _load_slabs(w_ref, b, k), _load_slabs(qe_ref, b, k), _load_slabs(ke_ref, b, k), qk, ea,
            _load_slabs(z_ref, b, k))


def _gd_lead_args(b, u0_ref, w0_ref, qe0_ref, ke0_ref, qk0_ref, ea0_ref, p0_ref):
    qk = jnp.stack([qk0_ref[h * CHUNK:(h + 1) * CHUNK, :] for _, h in _pairs(b)], axis=0)
    ea = jnp.stack([ea0_ref[:, h:h + 1] for _, h in _pairs(b)], axis=0)
    return (_lead_slabs(u0_ref[...], b), _lead_slabs(w0_ref[...], b), _lead_slabs(qe0_ref[...], b), _lead_slabs(ke0_ref[...], b),
            qk, ea, _lead_slabs(p0_ref[:, QKV:QKV + WIDTH], b))


def gd_scan_fwd(p, p0, local, lead, nw):
    b, seq, _ = p.shape
    slab, per_chunk, const = _scan_specs(b, seq // (SCAN_CHUNKS_FWD * CHUNK), False, SCAN_CHUNKS_FWD)

    def body(u_ref, w_ref, qe_ref, ke_ref, qk_ref, ea_ref, z_ref, u0_ref, w0_ref, qe0_ref, ke0_ref, qk0_ref, ea0_ref, p0_ref,
             nw_ref, y_ref, ss_ref, st):
        @pl.when(pl.program_id(0) == 0)
        def _():
            lead_args = _gd_lead_args(b, u0_ref, w0_ref, qe0_ref, ke0_ref, qk0_ref, ea0_ref, p0_ref)
            st[...] = gd_scan(*lead_args, nw_ref[...], jnp.zeros(st.shape, F32))[1]

        s = st[...]
        for k in range(SCAN_CHUNKS_FWD):
            _save_states(ss_ref, s, b, k)
            y, s = gd_scan(*_gd_scan_args(b, k, u_ref, w_ref, qe_ref, ke_ref, qk_ref, ea_ref, z_ref), nw_ref[...], s)
            _store_slabs(y_ref, y, b, k)
        st[...] = s

    return dict(
        body=body, args=(*local, p, *lead, p0, nw),
        in_specs=[slab(0)] * 4 + [per_chunk(HEADS * CHUNK, CHUNK), per_chunk(1, AB_PAD), slab(3)] + [const(a) for a in lead]
        + [const(p0), const(nw)],
        out_specs=[slab(0), per_chunk(WIDTH, DH)],
        out_shape=[_sds((b, seq, WIDTH), MXU_DTYPE), _sds((b, seq // CHUNK, WIDTH, DH), MXU_DTYPE)],
        scratch_shapes=[pltpu.VMEM((b * HEADS, DH, DH), F32)])


def gd_scan_bwd(p, p0, local, lead, nw, ssave, dy):
    b, seq, _ = p.shape
    ng = seq // (SCAN_CHUNKS * CHUNK)
    slab, per_chunk, const = _scan_specs(b, ng, True, SCAN_CHUNKS)

    def body(u_ref, w_ref, qe_ref, ke_ref, qk_ref, ea_ref, z_ref, u0_ref, w0_ref, qe0_ref, ke0_ref, qk0_ref, ea0_ref, p0_ref,
             nw_ref, ss_ref, dy_ref, du_ref, dw_ref, dqe_ref, dke_ref, dqk_ref, dea_ref, dz_ref, du0_ref, dw0_ref, dqe0_ref,
             dke0_ref, dqk0_ref, dea0_ref, dz0_ref, dnw_ref, dst):
        i = pl.program_id(0)
        lane = lax.broadcasted_iota(jnp.int32, (1, AB_PAD), 1)

        def gate_rows(dea, j):
            return sum(jnp.where(lane == h, dea[j * HEADS + h], 0.0) for h in range(HEADS))

        def matrix_rows(dqk, j):
            return jnp.concatenate([dqk[j * HEADS + h] for h in range(HEADS)], axis=0)

        @pl.when(i == 0)
        def _():
            dst[...] = jnp.zeros_like(dst)
            dnw_ref[...] = jnp.zeros_like(dnw_ref)

        ds = dst[...]
        for k in reversed(range(SCAN_CHUNKS)):
            args = _gd_scan_args(b, k, u_ref, w_ref, qe_ref, ke_ref, qk_ref, ea_ref, z_ref)
            _, vjp = jax.vjp(gd_scan, *args, nw_ref[...], _load_states(ss_ref, b, k))
            du, dw, dqe, dke, dqk, dea, dz, dnw, ds = vjp((_load_slabs(dy_ref, b, k), ds))
            dnw_ref[...] += dnw
            for ref, val in ((du_ref, du), (dw_ref, dw), (dqe_ref, dqe), (dke_ref, dke), (dz_ref, dz)):
                _store_slabs(ref, val, b, k)
            for j in range(b):
                dqk_ref[j, k] = matrix_rows(dqk, j)
                dea_ref[j, k] = gate_rows(dea, j)
        dst[...] = ds

        @pl.when(i == ng - 1)
        def _():
            args = _gd_lead_args(b, u0_ref, w0_ref, qe0_ref, ke0_ref, qk0_ref, ea0_ref, p0_ref)
            _, vjp = jax.vjp(gd_scan, *args, nw_ref[...], jnp.zeros(dst.shape, F32))
            du, dw, dqe, dke, dqk, dea, dz, dnw, _ = vjp((jnp.zeros((b * HEADS, CHUNK, DH), F32), ds))
            dnw_ref[...] += dnw
            for ref, val in ((du0_ref, du), (dw0_ref, dw), (dqe0_ref, dqe), (dke0_ref, dke), (dz0_ref, dz)):
                ref[...] = _sum_rows(val, b)
            dqk0_ref[...] = sum((matrix_rows(dqk, j) for j in range(1, b)), matrix_rows(dqk, 0))
            dea0_ref[...] = sum((gate_rows(dea, j) for j in range(1, b)), gate_rows(dea, 0))

    uu, ww, qe, ke, qk, ea = local
    return dict(
        body=body, args=(*local, p, *lead, p0, nw, ssave, dy),
        in_specs=[slab(0)] * 4 + [per_chunk(HEADS * CHUNK, CHUNK), per_chunk(1, AB_PAD), slab(3)] + [const(a) for a in lead]
        + [const(p0), const(nw), per_chunk(WIDTH, DH), slab(0)],
        out_specs=[slab(0)] * 4 + [per_chunk(HEADS * CHUNK, CHUNK), per_chunk(1, AB_PAD), slab(0)] + [const(a) for a in lead]
        + [const(lead[0]), const(nw)],
        out_shape=[_sds((b, seq, WIDTH))] * 4 + [_sds(qk.shape), _sds(ea.shape), _sds((b, seq, WIDTH), MXU_DTYPE)]
        + [_sds(a.shape) for a in lead] + [_sds(lead[0].shape), _sds(nw.shape)],
        scratch_shapes=[pltpu.VMEM((b * HEADS, DH, DH), F32)])


def _gd_local_vjp(sum_mats, inv_rows, xx, ab, cw, alog, dtb):
    nb = ab.shape[0] // CHUNK
    inv = jnp.stack([inv_rows[g * CHUNK:(g + 1) * CHUNK] for g in range(nb * HEADS)], axis=0)
    _, vjp, _ = jax.vjp(lambda *a: gd_local(*a, sum_mats, inverse=_saved_inverse(inv)), xx, ab, cw, alog, dtb, has_aux=True)
    return vjp


def gd_local_bwd(p, p0, ab, ab0, cw, alog, dtb, inv, inv0, cot, dz, cot0, dz0):
    b, seq, _ = p.shape
    rows = LOCAL_CHUNKS * CHUNK
    ng = seq // rows
    du, dw, dqe, dke, dqk, dea = cot

    def body(p_ref, halo_ref, p0_ref, ab_ref, ab0_ref, cw_ref, al_ref, dt_ref, s_ref, st_ref, inv_ref, inv0_ref, du_ref, dw_ref,
             dqe_ref, dke_ref, dqk_ref, dea_ref, dz_ref, du0_ref, dw0_ref, dqe0_ref, dke0_ref, dqk0_ref, dea0_ref, dz0_ref,
             dp_ref, dab_ref, dp0_ref, dab0_ref, dcw_ref, dal_ref, ddt_ref, dhalo, dtail):
        s, i = pl.program_id(0), pl.program_id(1)
        g = ng - 1 - i
        sum_mats = (s_ref[...], st_ref[...])

        @pl.when(i == 0)
        def _():
            dhalo[...] = jnp.zeros_like(dhalo)

        @pl.when((s == 0) & (i == 0))
        def _():
            dtail[...] = jnp.zeros_like(dtail)
            dcw_ref[...] = jnp.zeros_like(dcw_ref)
            dal_ref[...] = jnp.zeros_like(dal_ref)
            ddt_ref[...] = jnp.zeros_like(ddt_ref)

        def finish(dxx, dab, dcw, dal, ddt, before, n, dz_val, dp_out, dab_out):
            dqkv = dxx[HALO:HALO + n] + jnp.concatenate([jnp.zeros((n - HALO, QKV), F32), before], axis=0)
            dp_out[...] = jnp.concatenate([dqkv.astype(MXU_DTYPE), dz_val.astype(MXU_DTYPE)], axis=1)
            dab_out[...] = dab.astype(MXU_DTYPE)
            dcw_ref[...] += dcw
            dal_ref[...] += dal
            ddt_ref[...] += ddt

        inv_rows = jnp.concatenate([inv_ref[c] for c in range(LOCAL_CHUNKS)], axis=0)
        vjp = _gd_local_vjp(sum_mats, inv_rows, _gd_window(g, p_ref, halo_ref, p0_ref), ab_ref[...], cw_ref[...], al_ref[...],
                            dt_ref[...])
        dqk_all = jnp.concatenate([dqk_ref[c] for c in range(LOCAL_CHUNKS)], axis=0)
        deas = tuple(dea_ref[c] for c in range(LOCAL_CHUNKS))
        grads = vjp((du_ref[...], dw_ref[...], dqe_ref[...], dke_ref[...], dqk_all, deas))
        finish(*grads, dhalo[...], rows, dz_ref[...], dp_ref, dab_ref)
        dhalo[...] = grads[0][0:HALO]

        @pl.when(g == 0)
        def _():
            dtail[...] += grads[0][0:HALO]

        @pl.when((s == b - 1) & (g == 0))
        def _():
            vjp0 = _gd_local_vjp(sum_mats, inv0_ref[...], _lead_window(p0_ref), ab0_ref[...], cw_ref[...], al_ref[...],
                                 dt_ref[...])
            grads0 = vjp0((du0_ref[...], dw0_ref[...], dqe0_ref[...], dke0_ref[...], dqk0_ref[...], (dea0_ref[...],)))
            finish(*grads0, dtail[...], CHUNK, dz0_ref[...], dp0_ref, dab0_ref)

    rg = lambda i: ng - 1 - i
    const = lambda a: pl.BlockSpec(a.shape, lambda s, i: (0, 0))
    slab = pl.BlockSpec((None, rows, WIDTH), lambda s, i: (s, rg(i), 0))
    wide = pl.BlockSpec((None, rows, 4 * WIDTH), lambda s, i: (s, rg(i), 0))
    gates = pl.BlockSpec((None, rows, AB_PAD), lambda s, i: (s, rg(i), 0))
    mats = pl.BlockSpec((None, LOCAL_CHUNKS, HEADS * CHUNK, CHUNK), lambda s, i: (s, rg(i), 0, 0))
    sum_mats = _summation_matrices(_running_sum, CHUNK)
    return pl.pallas_call(
        body, grid=(b, ng), name="gdn_local_bwd",
        in_specs=[wide, pl.BlockSpec((None, HALO, QKV), lambda s, i: (s, _halo_block(rg(i)), 0)), const(p0), gates, const(ab0),
                  const(cw), const(alog), const(dtb), const(sum_mats[0]), const(sum_mats[1]), mats, const(inv0), slab, slab,
                  slab, slab, mats,
                  pl.BlockSpec((None, LOCAL_CHUNKS, 1, AB_PAD), lambda s, i: (s, rg(i), 0, 0)), slab]
        + [const(a) for a in cot0] + [const(dz0)],
        out_specs=[wide, gates, const(p0), const(ab0), const(cw), const(alog), const(dtb)],
        out_shape=[_sds(p.shape, MXU_DTYPE), _sds(ab.shape, MXU_DTYPE), _sds(p0.shape, MXU_DTYPE), _sds(ab0.shape, MXU_DTYPE),
                   _sds(cw.shape), _sds(alog.shape), _sds(dtb.shape)],
        scratch_shapes=[pltpu.VMEM((HALO, QKV), F32), pltpu.VMEM((HALO, QKV), F32)],
        compiler_params=_cparams("arbitrary", "arbitrary"),
    )(p, p, p0, ab, ab0, cw, alog, dtb, *sum_mats, inv, inv0, du, dw, dqe, dke, dqk, dea, dz, *cot0, dz0)


def _position():
    return lax.axis_index("x"), lax.axis_index("y"), lax.axis_index("c")


EXCHANGE_COPIES = 10


def _exchange_blocks(bufs, send_sems, recv_sems):
    x, y, c = _position()
    here, x_nbr, y_nbr, diag = (x, y), (1 - x, y), (x, 1 - y), (1 - x, 1 - y)
    sibling = (x, y, 1 - c)
    me = (x, y, c)
    n = range(len(bufs))

    def rows(a, chip, core, half=None):
        block = bufs[a].at[4 * chip[0] + 2 * chip[1] + core]
        if half is None:
            return block
        total = bufs[a].shape[1]
        tile = 8 * (4 // jnp.dtype(bufs[a].dtype).itemsize)
        split = total // 2 // tile * tile
        return block.at[pl.ds(0, split)] if half == 0 else block.at[pl.ds(split, total - split)]

    def copy(a, k, region, to):
        return pltpu.make_async_remote_copy(src_ref=region, dst_ref=region, send_sem=send_sems.at[a * EXCHANGE_COPIES + k],
                                            recv_sem=recv_sems.at[a * EXCHANGE_COPIES + k], device_id=to, device_id_type=MESH)

    sent = [copy(a, 0, rows(a, here, c), sibling) for a in n]
    sent += [cp for a in n for cp in (copy(a, 1, rows(a, here, c, 0), (*x_nbr, c)), copy(a, 4, rows(a, here, c, 1), (*y_nbr, c)))]
    sent += [cp for a in n for cp in (copy(a, 2, rows(a, here, c, 1), (*x_nbr, c)), copy(a, 3, rows(a, here, c, 0), (*y_nbr, c)))]
    for cp in sent:
        cp.start()

    def after(arrivals, a, k, region, to):
        for cp in arrivals:
            cp.wait_recv()
        sent.append(copy(a, k, region, to))
        sent[-1].start()

    for a in n:
        after([copy(a, 1, rows(a, x_nbr, c, 0), me)], a, 5, rows(a, x_nbr, c, 0), (*y_nbr, c))
        after([copy(a, 4, rows(a, y_nbr, c, 1), me)], a, 6, rows(a, y_nbr, c, 1), (*x_nbr, c))
    for a in n:
        after([copy(a, 2, rows(a, x_nbr, c, 1), me)], a, 7, rows(a, x_nbr, c), sibling)
        after([copy(a, 3, rows(a, y_nbr, c, 0), me)], a, 8, rows(a, y_nbr, c), sibling)
    for a in n:
        after([copy(a, 5, rows(a, diag, c, 0), me), copy(a, 6, rows(a, diag, c, 1), me)], a, 9, rows(a, diag, c), sibling)
    for a in n:
        copy(a, 0, rows(a, here, 1 - c), me).wait_recv()
        for k, chip in ((7, x_nbr), (8, y_nbr), (9, diag)):
            copy(a, k, rows(a, chip, 1 - c), me).wait_recv()
    for cp in sent:
        cp.wait_send()


def _exchange_sems(n_bufs):
    return [pltpu.SemaphoreType.DMA((n_bufs * EXCHANGE_COPIES,)), pltpu.SemaphoreType.DMA((n_bufs * EXCHANGE_COPIES,))]


def gather_weights(w_in_t, w_out, small, pad_rows):
    rows, _, cols = w_in_t.shape
    buf_rows = -(-rows // ROW_TILE_BF16) * ROW_TILE_BF16

    def body(wi_ref, wo_ref, sm_ref, wi_out, wo_out, sm_out, wi_buf, send_sems, recv_sems):
        x, y, c = _position()
        me = 4 * x + 2 * y + c
        wi_buf[me, pl.ds(0, rows), :] = wi_ref[:, 0, :].astype(MXU_DTYPE)
        wi_buf[me, pl.ds(rows, buf_rows - rows), :] = jnp.zeros((buf_rows - rows, cols), MXU_DTYPE)
        wo_out[me] = wo_ref[...].astype(MXU_DTYPE)
        sm_out[me] = sm_ref[...]
        _exchange_blocks([wi_buf, wo_out, sm_out], send_sems, recv_sems)
        for d in range(N_DEV):
            wi_out[pl.ds(d * rows, rows), :] = wi_buf[d, pl.ds(0, rows), :]
        wi_out[pl.ds(N_DEV * rows, pad_rows), :] = jnp.zeros((pad_rows, cols), MXU_DTYPE)

    return pl.pallas_call(
        body, name="gather_weights", in_specs=[VMEM_SPEC] * 3, out_specs=[VMEM_SPEC] * 3,
        out_shape=[jax.ShapeDtypeStruct((N_DEV * rows + pad_rows, cols), MXU_DTYPE),
                   jax.ShapeDtypeStruct((N_DEV,) + w_out.shape, MXU_DTYPE), jax.ShapeDtypeStruct((N_DEV,) + small.shape, F32)],
        scratch_shapes=[pltpu.VMEM((N_DEV, buf_rows, cols), MXU_DTYPE)] + _exchange_sems(3),
        compiler_params=pltpu.CompilerParams(vmem_limit_bytes=VMEM_LIMIT))(w_in_t, w_out, small)


HOPS = 6


def reduce_gradients(tensors, small, name):
    n_t = len(tensors)
    arrays = [a for parts, _ in tensors for a, _ in parts]
    first_array = [sum(len(parts) for parts, _ in tensors[:t]) for t in range(n_t)]

    def pieces(t, j):
        parts, block_rows = tensors[t]
        out, base = [], 0
        for pi, (_, valid) in enumerate(parts):
            lo, hi = max(j * block_rows, base), min((j + 1) * block_rows, base + valid)
            if lo < hi:
                out.append((first_array[t] + pi, lo - base, lo - j * block_rows, hi - lo))
            base += valid
        return out

    def body(*refs):
        n_a = len(arrays)
        in_refs, small_ref = refs[:n_a], refs[n_a]
        out_refs, small_sum = refs[n_a + 1:n_a + 1 + n_t], refs[n_a + 1 + n_t]
        bufs, small_buf = refs[n_a + 2 + n_t:n_a + 2 + 5 * n_t], refs[n_a + 2 + 5 * n_t]
        s1_sems, r1_sems, s2_sems, r2_sems, small_send, small_recv = refs[n_a + 3 + 5 * n_t:]
        x, y, c = _position()
        chip = 2 * x + y

        def put(t, dst, j, add=None):
            for ai, src_row, dst_row, size in pieces(t, j):
                v = in_refs[ai][pl.ds(src_row, size), :]
                if add is not None:
                    v = v + add[pl.ds(dst_row, size), :].astype(F32)
                dst[pl.ds(dst_row, size), :] = v.astype(dst.dtype)

        def swap(t, k):
            send1, recv1 = bufs[4 * t], bufs[4 * t + 1]
            return pltpu.make_async_remote_copy(src_ref=send1.at[k], dst_ref=recv1.at[k], send_sem=s1_sems.at[4 * t + k],
                                                recv_sem=r1_sems.at[4 * t + k], device_id=(x, y, 1 - c), device_id_type=MESH)

        to_x, to_y, to_diag = 2 * (1 - x) + y, 2 * x + (1 - y), 2 * (1 - x) + (1 - y)
        x_dev, y_dev = (1 - x, y, c), (x, 1 - y, c)

        def half(ref, h):
            total = ref.shape[0]
            split = total // 2 // ROW_TILE_BF16 * ROW_TILE_BF16
            return ref.at[pl.ds(0, split)] if h == 0 else ref.at[pl.ds(split, total - split)]

        def hop(t, copy_id, src, dst, to):
            return pltpu.make_async_remote_copy(src_ref=src, dst_ref=dst, send_sem=s2_sems.at[HOPS * t + copy_id],
                                                recv_sem=r2_sems.at[HOPS * t + copy_id], device_id=to, device_id_type=MESH)

        def hops(t):
            send2, landing = bufs[4 * t + 2], bufs[4 * t + 3]
            return [hop(t, 0, half(send2.at[to_diag], 0), half(landing.at[0], 0), x_dev),
                    hop(t, 1, half(send2.at[to_diag], 1), half(landing.at[0], 1), y_dev),
                    hop(t, 2, half(send2.at[to_x], 0), half(landing.at[1], 0), x_dev),
                    hop(t, 3, half(send2.at[to_y], 1), half(landing.at[2], 1), y_dev),
                    hop(t, 4, half(send2.at[to_x], 1), half(landing.at[1], 1), x_dev),
                    hop(t, 5, half(send2.at[to_y], 0), half(landing.at[2], 0), y_dev)]

        def add_relay(t, slot, h):
            dst, src = half(bufs[4 * t + 2].at[slot], h), half(bufs[4 * t + 3].at[0], h)
            dst[...] = (dst[...].astype(F32) + src[...].astype(F32)).astype(dst.dtype)

        for t in range(n_t):
            send2 = bufs[4 * t + 2]
            pad = send2.shape[1] - tensors[t][1]
            if pad:
                send2[:, pl.ds(tensors[t][1], pad), :] = jnp.zeros((4, pad, send2.shape[2]), send2.dtype)
            for j in range(N_DEV):
                @pl.when((j & 1) != c)
                def _():
                    put(t, bufs[4 * t].at[j >> 1], j)
            for k in range(4):
                swap(t, k).start()

        small_buf[4 * x + 2 * y + c] = small_ref[...]
        _exchange_blocks([small_buf], small_send, small_recv)
        total = small_buf[0]
        for d in range(1, N_DEV):
            total = total + small_buf[d]
        small_sum[...] = total

        for t in range(n_t):
            recv1 = bufs[4 * t + 1]
            for k in range(4):
                swap(t, k).wait_recv()
                for j in (2 * k, 2 * k + 1):
                    @pl.when(((j & 1) == c) & (k != chip))
                    def _():
                        put(t, bufs[4 * t + 2].at[k], j, add=recv1.at[k])

                    @pl.when(((j & 1) == c) & (k == chip))
                    def _():
                        put(t, out_refs[t], j, add=recv1.at[k])
            for cp in hops(t)[0:4]:
                cp.start()

        for t in range(n_t):
            cps = hops(t)
            cps[0].wait_recv()
            add_relay(t, to_y, 0)
            cps[5].start()
            cps[1].wait_recv()
            add_relay(t, to_x, 1)
            cps[4].start()

        for t in range(n_t):
            cps, rows = hops(t), tensors[t][1]
            for first, second, slot in ((cps[2], cps[4], 1), (cps[3], cps[5], 2)):
                first.wait_recv()
                second.wait_recv()
                out_refs[t][...] += bufs[4 * t + 3][slot, pl.ds(0, rows), :].astype(F32)

        for t in range(n_t):
            for cp in hops(t):
                cp.wait_send()
            for k in range(4):
                swap(t, k).wait_send()

    scratch, out_shape = [], []
    for parts, block_rows in tensors:
        cols = parts[0][0].shape[1]
        tiled_rows = -(-block_rows // ROW_TILE_BF16) * ROW_TILE_BF16
        scratch += [pltpu.VMEM((4, block_rows, cols), MXU_DTYPE)] * 2
        scratch += [pltpu.VMEM((4, tiled_rows, cols), MXU_DTYPE), pltpu.VMEM((3, tiled_rows, cols), MXU_DTYPE)]
        out_shape.append(jax.ShapeDtypeStruct((block_rows, cols), F32))
    out_shape.append(jax.ShapeDtypeStruct(small.shape, F32))
    scratch += [pltpu.VMEM((N_DEV,) + small.shape, F32)] + [pltpu.SemaphoreType.DMA((4 * n_t,))] * 2
    scratch += [pltpu.SemaphoreType.DMA((HOPS * n_t,))] * 2 + _exchange_sems(1)
    return pl.pallas_call(
        body, name=name, in_specs=[VMEM_SPEC] * (len(arrays) + 1), out_specs=[VMEM_SPEC] * (n_t + 1), out_shape=out_shape,
        scratch_shapes=scratch, compiler_params=pltpu.CompilerParams(vmem_limit_bytes=VMEM_LIMIT),
    )(*arrays, small)


def _adamw_step(w, g, m, v):
    mn = ADAM_B1 * m + (1.0 - ADAM_B1) * g
    vn = ADAM_B2 * v + (1.0 - ADAM_B2) * jnp.square(g)
    m_hat = mn / (1.0 - ADAM_B1 ** ADAM_STEP)
    v_hat = vn / (1.0 - ADAM_B2 ** ADAM_STEP)
    return -ADAM_LR * (m_hat / (jnp.sqrt(v_hat) + ADAM_EPS) + ADAM_WD * w), mn, vn


def adamw_small(packed, first_rows, ws, gs, ms, vs):
    k = len(ws)
    given = [g for g in gs if g is not None]

    def body(*refs):
        packed_ref, w_refs, m_refs, v_refs = refs[0], refs[1:1 + k], refs[1 + k:1 + 2 * k], refs[1 + 2 * k:1 + 3 * k]
        g_refs, outs = iter(refs[1 + 3 * k:1 + 3 * k + len(given)]), refs[1 + 3 * k + len(given):]
        for i in range(k):
            rows, cols = w_refs[i].shape
            g = next(g_refs)[...] if gs[i] is not None else packed_ref[first_rows[i]:first_rows[i] + rows, 0:cols]
            outs[4 * i][...] = g
            outs[4 * i + 1][...], outs[4 * i + 2][...], outs[4 * i + 3][...] = _adamw_step(w_refs[i][...], g, m_refs[i][...],
                                                                                          v_refs[i][...])

    n_in = 1 + 3 * k + len(given)
    out = pl.pallas_call(body, name="adamw_small", in_specs=[VMEM_SPEC] * n_in, out_specs=[VMEM_SPEC] * (4 * k),
                         out_shape=[jax.ShapeDtypeStruct(w.shape, F32) for w in ws for _ in range(4)],
                         compiler_params=pltpu.CompilerParams(vmem_limit_bytes=VMEM_LIMIT))(packed, *ws, *ms, *vs, *given)
    return [out[4 * i:4 * i + 4] for i in range(k)]


def adamw_w_in(w, g_t, m, v):
    def body(w_ref, g_ref, m_ref, v_ref, go_ref, d_ref, nm_ref, nv_ref):
        g = g_ref[...]
        go_ref[:, 0, :] = g
        d_ref[:, 0, :], nm_ref[:, 0, :], nv_ref[:, 0, :] = _adamw_step(w_ref[:, 0, :], g, m_ref[:, 0, :], v_ref[:, 0, :])

    return pl.pallas_call(body, name="adamw_w_in", in_specs=[VMEM_SPEC] * 4, out_specs=[VMEM_SPEC] * 4,
                          out_shape=[jax.ShapeDtypeStruct(w.shape, F32)] * 4,
                          compiler_params=pltpu.CompilerParams(vmem_limit_bytes=VMEM_LIMIT))(w, g_t, m, v)


def _pad_rows(a, rows=8):
    return jnp.pad(a, ((0, rows - a.shape[0]), (0, 0)))


def _pad_lanes(a, lanes=128):
    return jnp.pad(a, ((0, 0), (0, lanes - a.shape[1])))


def kernel(x, meta_tokens, norm_w, w_in, conv_w, hg_lb_logits, hg_norm_w, gdn_A_log, gdn_dt_bias, gdn_norm_w, w_out, final_norm_w, loss_target, m_meta_tokens, m_norm_w, m_w_in, m_conv_w, m_hg_lb_logits, m_hg_norm_w, m_gdn_A_log, m_gdn_dt_bias, m_gdn_norm_w, m_w_out, m_final_norm_w, v_meta_tokens, v_norm_w, v_w_in, v_conv_w, v_hg_lb_logits, v_hg_norm_w, v_gdn_A_log, v_gdn_dt_bias, v_gdn_norm_w, v_w_out, v_final_norm_w):
    b, seq, _ = x.shape
    n = b * seq
    dev = 4 * lax.axis_index("x") + 2 * lax.axis_index("y") + lax.axis_index("c")
    col_shard = IN_COLS // N_DEV

    small_w = jnp.concatenate([_pad_lanes(meta_tokens, 256), _pad_rows(_pad_lanes(conv_w[0], 256))], axis=0)
    w_t, w_out_g, small_g = gather_weights(jnp.transpose(w_in, (2, 0, 1)), w_out[0], small_w, AB_PAD - 2 * HEADS)
    meta_g = small_g[:, 0:N_META, 0:D_MODEL // N_DEV]
    conv_g = small_g[:, N_META:N_META + CONV_TAPS, 0:QKV // N_DEV]
    w_out_full = w_out_g.reshape(2 * WIDTH, D_MODEL)
    cw = jnp.transpose(conv_g, (1, 0, 2)).reshape(CONV_TAPS, QKV)
    meta = jnp.transpose(meta_g, (1, 0, 2)).reshape(N_META, D_MODEL)
    alog = _pad_lanes(gdn_A_log)
    dtb = _pad_lanes(gdn_dt_bias)
    fw = final_norm_w.reshape(1, D_MODEL)

    h0 = jnp.concatenate([jnp.zeros((CHUNK - N_META, D_MODEL), F32), meta], axis=0)
    x2 = x.reshape(n, D_MODEL)
    phg, pgd, pab, phg0, pgd0, pab0, u0 = in_proj(x2, h0, norm_w, w_t)
    phg3, pgd3, pab3 = phg.reshape(b, seq, 4 * WIDTH), pgd.reshape(b, seq, 4 * WIDTH), pab.reshape(b, seq, AB_PAD)
    hg_loc, hg_lead = hg_local_fwd(phg3, phg0, hg_lb_logits)
    gd_loc, gd_inv, gd_lead, gd_inv0 = gd_local_fwd(pgd3, pgd0, pab3, pab0, cw, alog, dtb)
    (y_hg, s_hg), (y_gd, s_gd) = run_scans([hg_scan_fwd(phg3, phg0, hg_loc, hg_lead, hg_norm_w),
                                            gd_scan_fwd(pgd3, pgd0, gd_loc, gd_lead, gdn_norm_w)],
                                           seq // (SCAN_CHUNKS_FWD * CHUNK), "scans")

    dh2, dy_hg, dy_gd, g_w_out, loss_part, g_fw = out_proj_loss(
        x2, loss_target.reshape(n, D_MODEL), y_hg.reshape(n, WIDTH), y_gd.reshape(n, WIDTH), w_out_full, fw)

    steps = seq // (SCAN_CHUNKS * CHUNK)
    assert steps == b * seq // (LOCAL_CHUNKS * CHUNK)
    (hb,) = run_scans([hg_scan_bwd(phg3, phg0, hg_loc, hg_lead, hg_norm_w, s_hg, dy_hg.reshape(b, seq, WIDTH))], steps,
                      "hgrn2_scan_bwd")
    gb, (dphg, dphg0, g_lb) = run_scans([gd_scan_bwd(pgd3, pgd0, gd_loc, gd_lead, gdn_norm_w, s_gd, dy_gd.reshape(b, seq, WIDTH)),
                                         hg_local_bwd(phg3, phg0, hg_lb_logits, hb[0:6], hb[6:12])], steps,
                                        "gdn_scan_bwd_hgrn2_local_bwd")
    g_hg_nw = hb[12]
    dpgd, dpab, dpgd0, dpab0, g_cw, g_alog, g_dtb = gd_local_bwd(pgd3, pgd0, pab3, pab0, cw, alog, dtb, gd_inv, gd_inv0,
                                                                 gb[0:6], gb[6], gb[7:13], gb[13])
    g_gd_nw = gb[14]
    dphg, dpgd, dpab = dphg.reshape(n, 4 * WIDTH), dpgd.reshape(n, 4 * WIDTH), dpab.reshape(n, AB_PAD)

    grad_x, dh0, g_nw, g_w_hg, g_w_gd, g_w_ab = in_proj_bwd(dphg, dpgd, dpab, w_t, x2, dh2, norm_w, h0, u0, dphg0, dpgd0, dpab0)

    small = jnp.concatenate([
        g_nw.reshape(8, 128), g_lb.reshape(8, 128), _pad_rows(g_hg_nw), _pad_rows(g_alog), _pad_rows(g_dtb), _pad_rows(g_gd_nw),
        g_fw.reshape(8, 128), g_cw.reshape(48, 128),
        dh0[CHUNK - N_META:CHUNK].reshape(128, 128), loss_part], axis=0)
    g_w_in_t, g_w_out, small = reduce_gradients(
        [([(g_w_hg, 4 * WIDTH), (g_w_gd, 4 * WIDTH), (g_w_ab, 2 * HEADS)], col_shard),
         ([(g_w_out, 2 * WIDTH)], (2 * WIDTH) // N_DEV)], small, "reduce_gradients")
    g_cw_full = small[56:104].reshape(CONV_TAPS, QKV)
    g_meta_full = small[104:232].reshape(N_META, D_MODEL)
    loss = small[232, 0]
    g_conv = lax.dynamic_slice_in_dim(g_cw_full, dev * (QKV // N_DEV), QKV // N_DEV, axis=1)
    g_meta = lax.dynamic_slice_in_dim(g_meta_full, dev * (D_MODEL // N_DEV), D_MODEL // N_DEV, axis=1)

    names = ["meta_tokens", "norm_w", "w_in", "conv_w", "hg_lb_logits", "hg_norm_w", "gdn_A_log", "gdn_dt_bias",
             "gdn_norm_w", "w_out", "final_norm_w"]
    weights = [meta_tokens, norm_w, w_in, conv_w, hg_lb_logits, hg_norm_w, gdn_A_log, gdn_dt_bias, gdn_norm_w, w_out,
               final_norm_w]
    moms = [m_meta_tokens, m_norm_w, m_w_in, m_conv_w, m_hg_lb_logits, m_hg_norm_w, m_gdn_A_log, m_gdn_dt_bias,
            m_gdn_norm_w, m_w_out, m_final_norm_w]
    vars_ = [v_meta_tokens, v_norm_w, v_w_in, v_conv_w, v_hg_lb_logits, v_hg_norm_w, v_gdn_A_log, v_gdn_dt_bias,
             v_gdn_norm_w, v_w_out, v_final_norm_w]
    gradient = [g_meta, 0, None, g_conv, 8, 16, 24, 32, 40, g_w_out, 48]
    shape2d = [g_meta.shape, (8, 128), None, g_conv.shape, (8, 128), (1, DH), (1, HEADS), (1, HEADS), (1, DH), g_w_out.shape,
               (8, 128)]
    i_w_in = names.index("w_in")
    others = [i for i in range(len(names)) if i != i_w_in]
    in_rows = lambda i: isinstance(gradient[i], int)
    stepped = adamw_small(small, [gradient[i] if in_rows(i) else None for i in others],
                          [weights[i].reshape(shape2d[i]) for i in others], [None if in_rows(i) else gradient[i] for i in others],
                          [moms[i].reshape(shape2d[i]) for i in others], [vars_[i].reshape(shape2d[i]) for i in others])
    results = {i: [a.reshape(weights[i].shape) for a in stepped[j]] for j, i in enumerate(others)}
    to3, back = (lambda a: jnp.transpose(a, (2, 0, 1))), (lambda a: jnp.transpose(a, (1, 2, 0)))
    results[i_w_in] = [back(a) for a in adamw_w_in(to3(w_in), g_w_in_t, to3(m_w_in), to3(v_w_in))]
    grads, deltas, new_ms, new_vs = zip(*(results[i] for i in range(len(names))))
    return (loss, grad_x.reshape(x.shape), *grads, *deltas, *new_ms, *new_vs)
```

```python
import jax
import jax.numpy as jnp
import numpy as np
from jax import lax
from jax.experimental import pallas as pl
from jax.experimental.pallas import tpu as pltpu

F32 = jnp.float32
BF16 = jnp.bfloat16
MXU_DTYPE = BF16

D_MODEL = 1024
N_META = 16
CHUNK = 64
SUB = 16
ROW_TILE_BF16 = 16
HEADS = 4
DH = 128
WIDTH = HEADS * DH
QKV = 3 * WIDTH
CONV_TAPS = 4
HALO = 8
EPS = 1e-6
IN_COLS = 4 * WIDTH + 4 * WIDTH + 2 * HEADS
AB_PAD = 128
N_DEV = 8
LOCAL_CHUNKS = 4
SCAN_CHUNKS_FWD = 4
SCAN_CHUNKS = 2
VMEM_LIMIT = 56 * 1024 * 1024
VMEM_LIMIT_LARGE = 60 * 1024 * 1024

ADAM_LR = 0.001
ADAM_B1 = 0.9
ADAM_B2 = 0.999
ADAM_EPS = 1e-08
ADAM_WD = 0.01
ADAM_STEP = 10

VMEM_SPEC = pl.BlockSpec(memory_space=pltpu.VMEM)
MESH = pl.DeviceIdType.MESH


def _mm_tn(a, b):
    return lax.dot_general(a.astype(MXU_DTYPE), b.astype(MXU_DTYPE), (((0,), (0,)), ((), ())), preferred_element_type=F32)


def _nn(a, b):
    return lax.dot_general(a.astype(MXU_DTYPE), b.astype(MXU_DTYPE), (((2,), (1,)), ((0,), (0,))), preferred_element_type=F32)


def _nt(a, b):
    return lax.dot_general(a.astype(MXU_DTYPE), b.astype(MXU_DTYPE), (((2,), (2,)), ((0,), (0,))), preferred_element_type=F32)


def _t(a):
    return jnp.swapaxes(a, 1, 2)


@jax.custom_vjp
def _bmm(a, b):
    return _nn(a, b)


_bmm.defvjp(lambda a, b: (_nn(a, b), (a, b)), lambda saved, d: (_nt(d, saved[1]), _nn(_t(saved[0]), d)))


@jax.custom_vjp
def _bmm_nt(a, b):
    return _nt(a, b)


_bmm_nt.defvjp(lambda a, b: (_nt(a, b), (a, b)), lambda saved, d: (_nn(d, saved[1]), _nn(_t(d), saved[0])))


@jax.custom_vjp
def _bmm_tn(a, b):
    return _nn(_t(a), b)


_bmm_tn.defvjp(lambda a, b: (_nn(_t(a), b), (a, b)), lambda saved, d: (_nt(saved[1], d), _nn(saved[0], d)))


def _iota2(n, m):
    return lax.broadcasted_iota(jnp.int32, (n, m), 0), lax.broadcasted_iota(jnp.int32, (n, m), 1)


def _silu(x):
    return x * jax.nn.sigmoid(x)


def _gated_norm(o, z, nw):
    return o * lax.rsqrt(jnp.mean(o * o, axis=-1, keepdims=True) + EPS) * nw * _silu(z)


def _heads(a, nb):
    return jnp.stack([a[c * CHUNK:(c + 1) * CHUNK, h * DH:(h + 1) * DH] for c in range(nb) for h in range(HEADS)], axis=0)


def _unheads(a3, nb):
    return jnp.concatenate(
        [jnp.concatenate([a3[c * HEADS + h] for h in range(HEADS)], axis=1) for c in range(nb)], axis=0)


def _split3(x):
    hi = x.astype(BF16)
    r1 = x - hi.astype(F32)
    mid = r1.astype(BF16)
    return hi, mid, (r1 - mid.astype(F32)).astype(BF16)


def _summation_matrices(pattern, n_out):
    s = pattern(np.arange(n_out)[:, None], np.arange(CHUNK)[None, :]).astype(np.float32)
    return jnp.asarray(np.tile(s, (1, 3)), BF16), jnp.asarray(np.tile(s.T, (1, 2)), BF16)


def _select_rows(mats, chunks):
    width = chunks[0].shape[1]
    out = _summation(*mats, jnp.concatenate(chunks, axis=1))
    return [out[:, c * width:(c + 1) * width] for c in range(len(chunks))]


def _summation_impl(s, v):
    return jnp.dot(s, jnp.concatenate(_split3(v), axis=0), preferred_element_type=F32)


@jax.custom_vjp
def _summation(s, s_t, v):
    return _summation_impl(s, v)


def _summation_fwd(s, s_t, v):
    return _summation_impl(s, v), s_t


def _summation_bwd(s_t, d):
    hi = d.astype(BF16)
    return None, None, jnp.dot(s_t, jnp.concatenate([hi, (d - hi.astype(F32)).astype(BF16)], axis=0),
                               preferred_element_type=F32)


_summation.defvjp(_summation_fwd, _summation_bwd)


def _chunks(x, nb):
    return [x[c * CHUNK:(c + 1) * CHUNK] for c in range(nb)]


def _running_sum(i, j):
    return j <= i


HG_LEVELS = 6


def _hg_sums(i, j):
    lvl, t = i >> HG_LEVELS, i & (CHUNK - 1)
    last = t
    for l in range(1, HG_LEVELS + 1):
        width = HG_LEVELS + 1 - l
        last = np.where(lvl == l, ((t >> width) << width) + (CHUNK >> l) - 1, last)
    return j <= last


def _level_operand(sh, q3, k3, x):
    def second():
        return ((lax.broadcasted_iota(jnp.int32, (CHUNK, DH), 0) >> sh) & 1) == 1

    def forward(q3, k3, x):
        decay = jnp.exp(-jnp.abs(x))
        out = jnp.where(second(), q3, k3) * decay
        return out, (decay, out)

    def backward(saved, d):
        decay, out = saved
        d_side, t = d * decay, d * out
        return jnp.where(second(), d_side, 0.0), jnp.where(second(), 0.0, d_side), jnp.where(second(), t, -t)

    operand = jax.custom_vjp(lambda q3, k3, x: forward(q3, k3, x)[0])
    operand.defvjp(forward, backward)
    return operand(q3, k3, x)


def hg_local(p, logits, sum_mats):
    nb = p.shape[0] // CHUNK
    l0, l1 = logits[0:1], logits[1:2]
    mx = jnp.maximum(l0, l1)
    e0, e1 = jnp.exp(l0 - mx), jnp.exp(l1 - mx)
    lb = e0 / (e0 + e1)
    q = _silu(p[:, 0:WIDTH])
    f = lb + (1.0 - lb) * jax.nn.sigmoid(p[:, WIDTH:2 * WIDTH])
    k = 1.0 - f
    logf = jnp.log(f)
    sums = _select_rows(sum_mats, _chunks(logf, nb))
    level = lambda l: _heads(jnp.concatenate([s[l * CHUNK:(l + 1) * CHUNK] for s in sums], axis=0), nb)
    q3, k3, v3, g3 = _heads(q, nb), _heads(k, nb), _heads(p[:, 2 * WIDTH:3 * WIDTH], nb), level(0)
    r, c = _iota2(CHUNK, CHUNK)
    a = jnp.where(r == c, _bmm_nt(q3, k3), 0.0)
    for l in range(1, HG_LEVELS + 1):
        sh = HG_LEVELS - l
        qk = _level_operand(sh, q3, k3, g3 - level(l))
        pair = ((r >> (sh + 1)) == (c >> (sh + 1))) & (((r >> sh) & 1) == 1) & (((c >> sh) & 1) == 0)
        a = a + jnp.where(pair, _bmm_nt(qk, qk), 0.0)
    o = _bmm(a, v3)
    glast = g3[:, CHUNK - 1:CHUNK, :]
    egs = tuple(jnp.concatenate([jnp.exp(glast[c * HEADS + h]) for h in range(HEADS)], axis=1) for c in range(nb))
    return _unheads(q3 * jnp.exp(g3), nb), _unheads(k3 * jnp.exp(glast - g3), nb), _unheads(o, nb), egs


def hg_scan(q_in, k_out, v, eg, o_intra, z, nw, st):
    o = o_intra + _bmm_nt(q_in, st)
    return _gated_norm(o, z, nw), st * eg + _bmm_tn(v, k_out)


def _tri_y_impl(a):
    r, c = _iota2(CHUNK, CHUNK)
    same16 = (r // SUB) == (c // SUB)
    same32 = (r // (2 * SUB)) == (c // (2 * SUB))
    a0 = jnp.where(same16, a, 0.0)
    y = -a0
    pw = _bmm(a0, a0)
    for _ in range(2):
        y = y + pw + _bmm(y, pw)
        pw = _bmm(pw, pw)
    y = y + pw + _bmm(y, pw)
    for ak in (jnp.where(same32 & jnp.logical_not(same16), a, 0.0), jnp.where(same32, 0.0, a)):
        m = ak + _bmm(y, ak)
        y = y - (m + _bmm(m, y))
    return y


@jax.custom_vjp
def _tri_y(a):
    return _tri_y_impl(a)


def _tri_y_fwd(a):
    y = _tri_y_impl(a)
    return y, y


def _tri_y_bwd(y, dy):
    n = dy + _bmm_tn(y, dy)
    return (-(n + _bmm_nt(n, y)),)


_tri_y.defvjp(_tri_y_fwd, _tri_y_bwd)


def _saved_inverse(y):
    @jax.custom_vjp
    def inverse(a):
        return y

    inverse.defvjp(lambda a: (y, None), lambda _, dy: _tri_y_bwd(y, dy))
    return inverse


def _head_rows(a3, nb):
    return jnp.concatenate([a3[g] for g in range(nb * HEADS)], axis=0)


def _pack_heads(x):
    return jnp.concatenate([x[0:2 * CHUNK], x[2 * CHUNK:4 * CHUNK]], axis=1)


def _unpack_heads(y):
    return jnp.concatenate([y[:, 0:CHUNK], y[:, CHUNK:2 * CHUNK]], axis=0)


def _packed_head(ref, i, k, h):
    return ref[i, k, (h % 2) * CHUNK:(h % 2 + 1) * CHUNK, (h // 2) * CHUNK:(h // 2 + 1) * CHUNK]


def _rows_down(x, s):
    rows = x.shape[0]

    @jax.custom_vjp
    def rotate(v):
        return pltpu.roll(v, s, 0)

    rotate.defvjp(lambda v: (pltpu.roll(v, s, 0), None), lambda _, d: (pltpu.roll(d, rows - s, 0),))
    return rotate(x)


def gd_local(xx, ab, cw, alog, dtb, sum_mats, inverse=_tri_y):
    n = ab.shape[0]
    nb = n // CHUNK
    conv = cw[CONV_TAPS - 1:CONV_TAPS] * xx[HALO:HALO + n]
    for j in range(CONV_TAPS - 1):
        conv = conv + cw[j:j + 1] * _rows_down(xx, CONV_TAPS - 1 - j)[HALO:HALO + n]
    act = _silu(conv)
    x = ab + dtb
    g_all = -jnp.exp(alog) * (jnp.maximum(x, 0.0) + jnp.log1p(jnp.exp(-jnp.abs(x))))
    beta_all = jax.nn.sigmoid(ab)
    gam_all = jnp.concatenate(_select_rows(sum_mats, _chunks(g_all, nb)), axis=0)
    q3, k3, v3 = _heads(act[:, 0:WIDTH], nb), _heads(act[:, WIDTH:2 * WIDTH], nb), _heads(act[:, 2 * WIDTH:QKV], nb)
    q3 = q3 * lax.rsqrt(jnp.sum(q3 * q3, axis=-1, keepdims=True) + EPS) * (DH ** -0.5)
    k3 = k3 * lax.rsqrt(jnp.sum(k3 * k3, axis=-1, keepdims=True) + EPS)
    pairs = [(c, h) for c in range(nb) for h in range(HEADS)]
    beta = jnp.stack([beta_all[c * CHUNK:(c + 1) * CHUNK, HEADS + h:HEADS + h + 1] for c, h in pairs], axis=0)
    gam = jnp.stack([gam_all[c * CHUNK:(c + 1) * CHUNK, h:h + 1] for c, h in pairs], axis=0)
    gam_t = [gam_all[c * CHUNK:(c + 1) * CHUNK].T for c in range(nb)]
    gam_row = jnp.stack([gam_t[c][h:h + 1, :] for c, h in pairs], axis=0)
    glast = gam[:, CHUNK - 1:CHUNK, :]
    r, c = _iota2(CHUNK, CHUNK)
    dec = jnp.exp(jnp.where(c < r, gam - gam_row, -jnp.inf))
    y = inverse(beta * _bmm_nt(k3, k3) * dec)
    eg = jnp.exp(gam)
    rhs = jnp.concatenate([beta * v3, (beta * eg) * k3], axis=2)
    sol = rhs + _bmm(y, rhs)
    qk = _bmm_nt(q3, k3) * jnp.where(r == c, 1.0, dec)
    eas = tuple(jnp.exp(gam_all[(c + 1) * CHUNK - 1:(c + 1) * CHUNK]) for c in range(nb))
    return (_unheads(sol[:, :, 0:DH], nb), _unheads(sol[:, :, DH:2 * DH], nb), _unheads(q3 * eg, nb),
            _unheads(k3 * jnp.exp(glast - gam), nb), _head_rows(qk, nb), eas), _head_rows(y, nb)


def gd_scan(uu, ww, qe, ke, qk, ea, z, nw, s):
    u = uu - _bmm(ww, s)
    o = _bmm(qe, s) + _bmm(qk, u)
    return _gated_norm(o, z, nw), ea * s + _bmm_tn(ke, u)


def _cparams(*sem):
    return pltpu.CompilerParams(dimension_semantics=sem, vmem_limit_bytes=VMEM_LIMIT)


def _row_tile(n):
    for t in (512, 256, 128, 64):
        if n % t == 0:
            return t
    raise ValueError(f"unsupported token count {n}")


def _w_in_specs():
    once = pl.Buffered(1)
    return [pl.BlockSpec((4 * WIDTH, D_MODEL), lambda *i: (0, 0), pipeline_mode=once),
            pl.BlockSpec((4 * WIDTH, D_MODEL), lambda *i: (1, 0), pipeline_mode=once),
            pl.BlockSpec((AB_PAD, D_MODEL), lambda *i: (8 * WIDTH // AB_PAD, 0), pipeline_mode=once)]


def in_proj(h, h0, norm_w, w_t):
    n = h.shape[0]
    tm = _row_tile(n)
    nt = (((1,), (1,)), ((), ()))

    def body(h_ref, h0_ref, nw_ref, whg_ref, wgd_ref, wab_ref, phg_ref, pgd_ref, pab_ref, phg0_ref, pgd0_ref, pab0_ref, u0_ref):
        def project(x, hg_ref, gd_ref, ab_ref):
            u = (x * lax.rsqrt(jnp.mean(x * x, axis=-1, keepdims=True) + EPS) * nw_ref[...]).astype(MXU_DTYPE)
            hg_ref[...] = lax.dot_general(u, whg_ref[...], nt, preferred_element_type=F32)
            gd_ref[...] = lax.dot_general(u, wgd_ref[...], nt, preferred_element_type=F32)
            ab_ref[...] = lax.dot_general(u, wab_ref[...], nt, preferred_element_type=F32)
            return u

        @pl.when(pl.program_id(0) == 0)
        def _():
            u0_ref[...] = project(h0_ref[...], phg0_ref, pgd0_ref, pab0_ref)

        project(h_ref[...], phg_ref, pgd_ref, pab_ref)

    n0 = h0.shape[0]
    row = lambda w: pl.BlockSpec((tm, w), lambda i: (i, 0))
    lead = lambda w: pl.BlockSpec((n0, w), lambda i: (0, 0))
    widths = [4 * WIDTH, 4 * WIDTH, AB_PAD]
    return pl.pallas_call(
        body, grid=(n // tm,), name="in_proj",
        in_specs=[row(D_MODEL), lead(D_MODEL), pl.BlockSpec(norm_w.shape, lambda i: (0, 0))] + _w_in_specs(),
        out_specs=[row(w) for w in widths] + [lead(w) for w in widths] + [lead(D_MODEL)],
        out_shape=[jax.ShapeDtypeStruct((n, w), F32) for w in widths] + [jax.ShapeDtypeStruct((n0, w), F32) for w in widths]
        + [jax.ShapeDtypeStruct((n0, D_MODEL), MXU_DTYPE)],
        compiler_params=_cparams("arbitrary"),
    )(h, h0, norm_w, w_t, w_t, w_t)


def out_proj_loss(x, tgt, y_hg, y_gd, w_out, fw):
    n = x.shape[0]
    tm = _row_tile(n)
    inv_d = 1.0 / D_MODEL

    def body(x_ref, t_ref, yh_ref, yg_ref, w_ref, fw_ref, dh_ref, dyh_ref, dyg_ref, dw_ref, loss_ref, dfw_ref):
        @pl.when(pl.program_id(0) == 0)
        def _():
            dw_ref[...] = jnp.zeros_like(dw_ref)
            loss_ref[...] = jnp.zeros_like(loss_ref)
            dfw_ref[...] = jnp.zeros_like(dfw_ref)

        yh, yg = yh_ref[...], yg_ref[...]
        wa, wb = w_ref[0:WIDTH, :], w_ref[WIDTH:2 * WIDTH, :]
        h2 = x_ref[...] + jnp.dot(yh, wa, preferred_element_type=F32) + jnp.dot(yg, wb, preferred_element_type=F32)
        r2 = lax.rsqrt(jnp.mean(h2 * h2, axis=-1, keepdims=True) + EPS)
        nrm = h2 * r2
        fwv = fw_ref[...]
        err = nrm * fwv - t_ref[...]
        loss_ref[...] += jnp.full(loss_ref.shape, 0.5 * inv_d * jnp.sum(err * err), F32)
        dout = err * inv_d
        dfw_ref[...] += jnp.sum(dout * nrm, axis=0, keepdims=True)
        dn = dout * fwv
        dh2 = r2 * (dn - nrm * jnp.mean(dn * nrm, axis=-1, keepdims=True))
        dh_ref[...] = dh2
        dhb = dh2.astype(MXU_DTYPE)
        dyh_ref[...] = lax.dot_general(dhb, wa, (((1,), (1,)), ((), ())), preferred_element_type=F32)
        dyg_ref[...] = lax.dot_general(dhb, wb, (((1,), (1,)), ((), ())), preferred_element_type=F32)
        dw_ref[0:WIDTH, :] += lax.dot_general(yh, dhb, (((0,), (0,)), ((), ())), preferred_element_type=F32)
        dw_ref[WIDTH:2 * WIDTH, :] += lax.dot_general(yg, dhb, (((0,), (0,)), ((), ())), preferred_element_type=F32)

    row = lambda w: pl.BlockSpec((tm, w), lambda i: (i, 0))
    full = lambda s: pl.BlockSpec(s, lambda i: (0, 0))
    return pl.pallas_call(
        body, grid=(n // tm,), name="out_proj_loss",
        in_specs=[row(D_MODEL), row(D_MODEL), row(WIDTH), row(WIDTH), full(w_out.shape), full(fw.shape)],
        out_specs=[row(D_MODEL), row(WIDTH), row(WIDTH), full((2 * WIDTH, D_MODEL)), full((8, 128)), full((1, D_MODEL))],
        out_shape=[jax.ShapeDtypeStruct((n, D_MODEL), F32), jax.ShapeDtypeStruct((n, WIDTH), F32),
                   jax.ShapeDtypeStruct((n, WIDTH), F32), jax.ShapeDtypeStruct((2 * WIDTH, D_MODEL), F32),
                   jax.ShapeDtypeStruct((8, 128), F32), jax.ShapeDtypeStruct((1, D_MODEL), F32)],
        compiler_params=_cparams("arbitrary"),
    )(x, tgt, y_hg, y_gd, w_out, fw)


def in_proj_bwd(dphg, dpgd, dpab, w_t, h, dh2, norm_w, h0, u0, dphg0, dpgd0, dpab0):
    n = h.shape[0]
    tm = _row_tile(n)
    steps = n // tm

    def body(dphg_ref, dpgd_ref, dpab_ref, whg_ref, wgd_ref, wab_ref, h_ref, dh2_ref, nw_ref, h0_ref, u0_ref, d0hg_ref,
             d0gd_ref, d0ab_ref, dx_ref, dx0_ref, dnw_ref, ghg_ref, ggd_ref, gab_ref, acc_hg, acc_gd, acc_ab):
        i = pl.program_id(0)
        nwv = nw_ref[...]

        def norm_bwd(dps, x):
            du = jnp.dot(dps[0], whg_ref[...], preferred_element_type=F32)
            du += jnp.dot(dps[1], wgd_ref[...], preferred_element_type=F32)
            du += jnp.dot(dps[2], wab_ref[...], preferred_element_type=F32)
            r = lax.rsqrt(jnp.mean(x * x, axis=-1, keepdims=True) + EPS)
            nrm = x * r
            dn = du * nwv
            return r * (dn - nrm * jnp.mean(dn * nrm, axis=-1, keepdims=True)), nrm, jnp.sum(du * nrm, axis=0, keepdims=True)

        def accumulate(dps, u, first):
            for acc, dp in zip((acc_hg, acc_gd, acc_ab), dps):
                step = min(acc.shape[0], 512)
                for lo in range(0, acc.shape[0], step):
                    part = _mm_tn(dp[:, lo:lo + step], u)
                    acc[lo:lo + step, :] = part if first else acc[lo:lo + step, :] + part

        @pl.when(i == 0)
        def _():
            dps0 = (d0hg_ref[...], d0gd_ref[...], d0ab_ref[...])
            dx0_ref[...], _, dnw_ref[...] = norm_bwd(dps0, h0_ref[...])
            accumulate(dps0, u0_ref[...], True)

        dps = (dphg_ref[...], dpgd_ref[...], dpab_ref[...])
        dx, nrm, dnw = norm_bwd(dps, h_ref[...])
        dx_ref[...] = dh2_ref[...] + dx
        dnw_ref[...] += dnw
        accumulate(dps, (nrm * nwv).astype(MXU_DTYPE), False)

        @pl.when(i == steps - 1)
        def _():
            pltpu.sync_copy(acc_hg, ghg_ref)
            pltpu.sync_copy(acc_gd, ggd_ref)
            pltpu.sync_copy(acc_ab, gab_ref)

    row = lambda w: pl.BlockSpec((tm, w), lambda i: (i, 0))
    full = lambda a: pl.BlockSpec(a.shape, lambda i: (0, 0), pipeline_mode=pl.Buffered(1))
    anywhere = pl.BlockSpec(memory_space=pl.ANY)
    return pl.pallas_call(
        body, grid=(steps,), name="in_proj_bwd",
        in_specs=[row(4 * WIDTH), row(4 * WIDTH), row(AB_PAD)] + _w_in_specs() + [row(D_MODEL), row(D_MODEL), full(norm_w),
                                                                                   full(h0), full(u0), full(dphg0), full(dpgd0),
                                                                                   full(dpab0)],
        out_specs=[row(D_MODEL), pl.BlockSpec(h0.shape, lambda i: (0, 0)), pl.BlockSpec((1, D_MODEL), lambda i: (0, 0)),
                   anywhere, anywhere, anywhere],
        out_shape=[jax.ShapeDtypeStruct((n, D_MODEL), F32), jax.ShapeDtypeStruct(h0.shape, F32),
                   jax.ShapeDtypeStruct((1, D_MODEL), F32), jax.ShapeDtypeStruct((4 * WIDTH, D_MODEL), F32),
                   jax.ShapeDtypeStruct((4 * WIDTH, D_MODEL), F32), jax.ShapeDtypeStruct((AB_PAD, D_MODEL), F32)],
        scratch_shapes=[pltpu.VMEM((4 * WIDTH, D_MODEL), F32), pltpu.VMEM((4 * WIDTH, D_MODEL), F32),
                        pltpu.VMEM((AB_PAD, D_MODEL), F32)],
        compiler_params=pltpu.CompilerParams(dimension_semantics=("arbitrary",), vmem_limit_bytes=VMEM_LIMIT_LARGE),
    )(dphg, dpgd, dpab, w_t, w_t, w_t, h, dh2, norm_w, h0, u0, dphg0, dpgd0, dpab0)


def _sds(shape, dtype=F32):
    return jax.ShapeDtypeStruct(shape, dtype)


def _pairs(b):
    return [(i, h) for i in range(b) for h in range(HEADS)]


def _load_slabs(ref, b, k):
    return jnp.stack([ref[i, k * CHUNK:(k + 1) * CHUNK, h * DH:(h + 1) * DH].astype(F32) for i, h in _pairs(b)], axis=0)


def _lead_slabs(a, b):
    return jnp.stack([a[:, h * DH:(h + 1) * DH].astype(F32) for _, h in _pairs(b)], axis=0)


def _rows(a3, i):
    return jnp.concatenate([a3[i * HEADS + h] for h in range(HEADS)], axis=1)


def _store_slabs(ref, a3, b, k):
    for i in range(b):
        ref[i, k * CHUNK:(k + 1) * CHUNK, :] = _rows(a3, i).astype(ref.dtype)


def _sum_rows(a3, b):
    out = _rows(a3, 0)
    for i in range(1, b):
        out = out + _rows(a3, i)
    return out


def _save_states(ref, s, b, k):
    for i in range(b):
        ref[i, k] = jnp.concatenate([s[i * HEADS + h] for h in range(HEADS)], axis=0).astype(ref.dtype)


def _load_states(ref, b, k):
    return jnp.stack([ref[i, k, h * DH:(h + 1) * DH, :].astype(F32) for i, h in _pairs(b)], axis=0)


def hg_local_fwd(p, p0, logits):
    b, seq, _ = p.shape
    rows = LOCAL_CHUNKS * CHUNK
    nreal = seq // CHUNK

    def body(p_ref, p0_ref, lg_ref, s_ref, st_ref, q_ref, k_ref, o_ref, eg_ref, q0_ref, k0_ref, o0_ref, eg0_ref):
        sum_mats = (s_ref[...], st_ref[...])

        @pl.when((pl.program_id(0) == 0) & (pl.program_id(1) == 0))
        def _():
            q_in, k_out, o0_ref[...], (eg0_ref[...],) = hg_local(p0_ref[...], lg_ref[...], sum_mats)
            q0_ref[...], k0_ref[...] = q_in.astype(MXU_DTYPE), k_out.astype(MXU_DTYPE)

        q_in, k_out, o_intra, egs = hg_local(p_ref[...], lg_ref[...], sum_mats)
        q_ref[...], k_ref[...], o_ref[...] = q_in.astype(MXU_DTYPE), k_out.astype(MXU_DTYPE), o_intra
        for c in range(LOCAL_CHUNKS):
            eg_ref[c] = egs[c]

    slab = pl.BlockSpec((None, rows, WIDTH), lambda s, g: (s, g, 0))
    const = lambda shape: pl.BlockSpec(shape, lambda s, g: (0, 0))
    lead_shapes = [(CHUNK, WIDTH)] * 3 + [(1, WIDTH)]
    sum_mats = _summation_matrices(_hg_sums, (HG_LEVELS + 1) * CHUNK)
    out = pl.pallas_call(
        body, grid=(b, seq // rows), name="hgrn2_local",
        in_specs=[pl.BlockSpec((None, rows, 4 * WIDTH), lambda s, g: (s, g, 0)), const(p0.shape), const(logits.shape)]
        + [const(a.shape) for a in sum_mats],
        out_specs=[slab, slab, slab, pl.BlockSpec((None, LOCAL_CHUNKS, 1, WIDTH), lambda s, g: (s, g, 0, 0))]
        + [const(s) for s in lead_shapes],
        out_shape=[_sds((b, seq, WIDTH), MXU_DTYPE)] * 2 + [_sds((b, seq, WIDTH)), _sds((b, nreal, 1, WIDTH))]
        + [_sds(lead_shapes[0], MXU_DTYPE)] * 2 + [_sds(lead_shapes[2]), _sds(lead_shapes[3])],
        compiler_params=_cparams("arbitrary", "arbitrary"),
    )(p, p0, logits, *sum_mats)
    return out[0:4], out[4:8]


def _hg_scan_args(b, k, q_ref, k_ref, o_ref, v_ref, z_ref, eg_ref):
    eg = jnp.stack([eg_ref[i, k, :, h * DH:(h + 1) * DH] for i, h in _pairs(b)], axis=0)
    return (_load_slabs(q_ref, b, k), _load_slabs(k_ref, b, k), _load_slabs(v_ref, b, k), eg, _load_slabs(o_ref, b, k),
            _load_slabs(z_ref, b, k))


def _hg_lead_args(b, q0_ref, k0_ref, o0_ref, p0_ref, eg0_ref):
    eg = jnp.stack([eg0_ref[:, h * DH:(h + 1) * DH] for _, h in _pairs(b)], axis=0)
    return (_lead_slabs(q0_ref[...], b), _lead_slabs(k0_ref[...], b), _lead_slabs(p0_ref[:, 2 * WIDTH:3 * WIDTH], b), eg,
            _lead_slabs(o0_ref[...], b), _lead_slabs(p0_ref[:, 3 * WIDTH:4 * WIDTH], b))


def _scan_specs(b, ng, reverse, chunks):
    group = (lambda i: ng - 1 - i) if reverse else (lambda i: i)
    slab = lambda lane_block: pl.BlockSpec((b, chunks * CHUNK, WIDTH), lambda i: (0, group(i), lane_block))
    per_chunk = lambda *tail: pl.BlockSpec((b, chunks) + tail, lambda i: (0, group(i)) + (0,) * len(tail))
    const = lambda a: pl.BlockSpec(a.shape, lambda i: (0,) * a.ndim)
    return slab, per_chunk, const


def run_scans(parts, nc, name):
    n_in = [len(p["args"]) for p in parts]
    n_out = [len(p["out_shape"]) for p in parts]
    n_scr = [len(p["scratch_shapes"]) for p in parts]

    def body(*refs):
        ins, outs, scr = refs[:sum(n_in)], refs[sum(n_in):sum(n_in) + sum(n_out)], refs[sum(n_in) + sum(n_out):]
        for i, part in enumerate(parts):
            part["body"](*ins[sum(n_in[:i]):sum(n_in[:i + 1])], *outs[sum(n_out[:i]):sum(n_out[:i + 1])],
                         *scr[sum(n_scr[:i]):sum(n_scr[:i + 1])])

    flat = lambda key: [v for p in parts for v in p[key]]
    out = pl.pallas_call(body, grid=(nc,), name=name, in_specs=flat("in_specs"), out_specs=flat("out_specs"),
                         out_shape=flat("out_shape"), scratch_shapes=flat("scratch_shapes"),
                         compiler_params=_cparams("arbitrary"))(*flat("args"))
    return [out[sum(n_out[:i]):sum(n_out[:i + 1])] for i in range(len(parts))]


def hg_scan_fwd(p, p0, local, lead, nw):
    b, seq, _ = p.shape
    q_in, k_out, o_intra, eg = local
    slab, per_chunk, const = _scan_specs(b, seq // (SCAN_CHUNKS_FWD * CHUNK), False, SCAN_CHUNKS_FWD)

    def body(q_ref, k_ref, o_ref, v_ref, z_ref, eg_ref, q0_ref, k0_ref, o0_ref, p0_ref, eg0_ref, nw_ref, y_ref, ss_ref, st):
        @pl.when(pl.program_id(0) == 0)
        def _():
            st[...] = hg_scan(*_hg_lead_args(b, q0_ref, k0_ref, o0_ref, p0_ref, eg0_ref), nw_ref[...], jnp.zeros(st.shape, F32))[1]

        s = st[...]
        for k in range(SCAN_CHUNKS_FWD):
            _save_states(ss_ref, s, b, k)
            y, s = hg_scan(*_hg_scan_args(b, k, q_ref, k_ref, o_ref, v_ref, z_ref, eg_ref), nw_ref[...], s)
            _store_slabs(y_ref, y, b, k)
        st[...] = s

    return dict(
        body=body, args=(q_in, k_out, o_intra, p, p, eg, lead[0], lead[1], lead[2], p0, lead[3], nw),
        in_specs=[slab(0), slab(0), slab(0), slab(2), slab(3), per_chunk(1, WIDTH)] + [const(a) for a in lead[0:3]]
        + [const(p0), const(lead[3]), const(nw)],
        out_specs=[slab(0), per_chunk(WIDTH, DH)],
        out_shape=[_sds((b, seq, WIDTH), MXU_DTYPE), _sds((b, seq // CHUNK, WIDTH, DH), MXU_DTYPE)],
        scratch_shapes=[pltpu.VMEM((b * HEADS, DH, DH), F32)])


def hg_scan_bwd(p, p0, local, lead, nw, ssave, dy):
    b, seq, _ = p.shape
    ng = seq // (SCAN_CHUNKS * CHUNK)
    q_in, k_out, o_intra, eg = local
    slab, per_chunk, const = _scan_specs(b, ng, True, SCAN_CHUNKS)

    def body(q_ref, k_ref, o_ref, v_ref, z_ref, eg_ref, q0_ref, k0_ref, o0_ref, p0_ref, eg0_ref, nw_ref, ss_ref, dy_ref,
             dq_ref, dk_ref, do_ref, dv_ref, dz_ref, deg_ref, dq0_ref, dk0_ref, do0_ref, dv0_ref, dz0_ref, deg0_ref, dnw_ref,
             dst):
        i = pl.program_id(0)

        @pl.when(i == 0)
        def _():
            dst[...] = jnp.zeros_like(dst)
            dnw_ref[...] = jnp.zeros_like(dnw_ref)

        ds = dst[...]
        for k in reversed(range(SCAN_CHUNKS)):
            args = _hg_scan_args(b, k, q_ref, k_ref, o_ref, v_ref, z_ref, eg_ref)
            _, vjp = jax.vjp(hg_scan, *args, nw_ref[...], _load_states(ss_ref, b, k))
            dq, dk, dv, deg, do, dz, dnw, ds = vjp((_load_slabs(dy_ref, b, k), ds))
            dnw_ref[...] += dnw
            for ref, val in ((dq_ref, dq), (dk_ref, dk), (do_ref, do), (dv_ref, dv), (dz_ref, dz)):
                _store_slabs(ref, val, b, k)
            for j in range(b):
                deg_ref[j, k] = _rows(deg, j)
        dst[...] = ds

        @pl.when(i == ng - 1)
        def _():
            args = _hg_lead_args(b, q0_ref, k0_ref, o0_ref, p0_ref, eg0_ref)
            _, vjp = jax.vjp(hg_scan, *args, nw_ref[...], jnp.zeros(dst.shape, F32))
            dq, dk, dv, deg, do, dz, dnw, _ = vjp((jnp.zeros((b * HEADS, CHUNK, DH), F32), ds))
            dnw_ref[...] += dnw
            for ref, val in ((dq0_ref, dq), (dk0_ref, dk), (do0_ref, do), (dv0_ref, dv), (dz0_ref, dz), (deg0_ref, deg)):
                ref[...] = _sum_rows(val, b)

    lead_out = [const(a) for a in lead[0:3]] + [const(lead[0]), const(lead[0]), const(lead[3])]
    return dict(
        body=body, args=(q_in, k_out, o_intra, p, p, eg, lead[0], lead[1], lead[2], p0, lead[3], nw, ssave, dy),
        in_specs=[slab(0), slab(0), slab(0), slab(2), slab(3), per_chunk(1, WIDTH)] + [const(a) for a in lead[0:3]]
        + [const(p0), const(lead[3]), const(nw), per_chunk(WIDTH, DH), slab(0)],
        out_specs=[slab(0)] * 5 + [per_chunk(1, WIDTH)] + lead_out + [const(nw)],
        out_shape=[_sds((b, seq, WIDTH))] * 2 + [_sds((b, seq, WIDTH), MXU_DTYPE)] * 3 + [_sds(eg.shape)]
        + [_sds((CHUNK, WIDTH))] * 5 + [_sds((1, WIDTH)), _sds(nw.shape)],
        scratch_shapes=[pltpu.VMEM((b * HEADS, DH, DH), F32)])


def _hg_local_vjp(sum_mats, p, logits, dq, dk, do, degs, dv, dz):
    _, vjp = jax.vjp(lambda p_, logits_: hg_local(p_, logits_, sum_mats), p, logits)
    dp, dlg = vjp((dq, dk, do.astype(F32), degs))
    return dp + jnp.concatenate([jnp.zeros((p.shape[0], 2 * WIDTH), F32), dv.astype(F32), dz.astype(F32)], axis=1), dlg


def hg_local_bwd(p, p0, logits, cot, cot0):
    b, seq, _ = p.shape
    rows = LOCAL_CHUNKS * CHUNK

    def body(p_ref, p0_ref, lg_ref, s_ref, st_ref, dq_ref, dk_ref, do_ref, dv_ref, dz_ref, deg_ref, dq0_ref, dk0_ref, do0_ref,
             dv0_ref, dz0_ref, deg0_ref, dp_ref, dp0_ref, dlg_ref):
        sum_mats = (s_ref[...], st_ref[...])

        @pl.when((pl.program_id(0) == 0) & (pl.program_id(1) == 0))
        def _():
            dp0, dlg_ref[...] = _hg_local_vjp(sum_mats, p0_ref[...], lg_ref[...], dq0_ref[...], dk0_ref[...], do0_ref[...],
                                              (deg0_ref[...],), dv0_ref[...], dz0_ref[...])
            dp0_ref[...] = dp0.astype(MXU_DTYPE)

        degs = tuple(deg_ref[c] for c in range(LOCAL_CHUNKS))
        dp, dlg = _hg_local_vjp(sum_mats, p_ref[...], lg_ref[...], dq_ref[...], dk_ref[...], do_ref[...], degs, dv_ref[...],
                                dz_ref[...])
        dp_ref[...] = dp.astype(MXU_DTYPE)
        dlg_ref[...] += dlg

    slab = pl.BlockSpec((None, rows, WIDTH), lambda s, g: (s, g, 0))
    wide = pl.BlockSpec((None, rows, 4 * WIDTH), lambda s, g: (s, g, 0))
    const = lambda a: pl.BlockSpec(a.shape, lambda s, g: (0, 0))
    sum_mats = _summation_matrices(_hg_sums, (HG_LEVELS + 1) * CHUNK)
    return pl.pallas_call(
        body, grid=(b, seq // rows), name="hgrn2_local_bwd",
        in_specs=[wide, const(p0), const(logits), const(sum_mats[0]), const(sum_mats[1]), slab, slab, slab, slab, slab,
                  pl.BlockSpec((None, LOCAL_CHUNKS, 1, WIDTH), lambda s, g: (s, g, 0, 0))] + [const(a) for a in cot0],
        out_specs=[wide, const(p0), const(logits)],
        out_shape=[_sds(p.shape, MXU_DTYPE), _sds(p0.shape, MXU_DTYPE), _sds(logits.shape)],
        compiler_params=_cparams("arbitrary", "arbitrary"),
    )(p, p0, logits, *sum_mats, *cot, *cot0)


def _halo_block(g):
    return jnp.maximum((LOCAL_CHUNKS * CHUNK // HALO) * g - 1, 0)


def _gd_window(g, p_ref, halo_ref, p0_ref):
    halo = jnp.where(g == 0, p0_ref[CHUNK - HALO:CHUNK, 0:QKV], halo_ref[...])
    return jnp.concatenate([halo, p_ref[:, 0:QKV]], axis=0)


def _lead_window(p0_ref):
    return jnp.concatenate([jnp.zeros((HALO, QKV), F32), p0_ref[:, 0:QKV]], axis=0)


def gd_local_fwd(p, p0, ab, ab0, cw, alog, dtb):
    b, seq, _ = p.shape
    rows = LOCAL_CHUNKS * CHUNK
    nreal = seq // CHUNK

    def body(p_ref, halo_ref, p0_ref, ab_ref, ab0_ref, cw_ref, al_ref, dt_ref, s_ref, st_ref, u_ref, w_ref, qe_ref, ke_ref,
             qk_ref, ea_ref, inv_ref, u0_ref, w0_ref, qe0_ref, ke0_ref, qk0_ref, ea0_ref, inv0_ref):
        sum_mats = (s_ref[...], st_ref[...])

        @pl.when((pl.program_id(0) == 0) & (pl.program_id(1) == 0))
        def _():
            (u0_ref[...], ww, qe, ke, qk0_ref[...], (ea0_ref[...],)), inv0_ref[...] = gd_local(
                _lead_window(p0_ref), ab0_ref[...], cw_ref[...], al_ref[...], dt_ref[...], sum_mats, inverse=_tri_y_impl)
            w0_ref[...], qe0_ref[...], ke0_ref[...] = ww.astype(MXU_DTYPE), qe.astype(MXU_DTYPE), ke.astype(MXU_DTYPE)

        (uu, ww, qe, ke, qk, eas), inv = gd_local(_gd_window(pl.program_id(1), p_ref, halo_ref, p0_ref), ab_ref[...],
                                                  cw_ref[...], al_ref[...], dt_ref[...], sum_mats, inverse=_tri_y_impl)
        u_ref[...], w_ref[...], qe_ref[...], ke_ref[...] = uu, ww.astype(MXU_DTYPE), qe.astype(MXU_DTYPE), ke.astype(MXU_DTYPE)
        for c in range(LOCAL_CHUNKS):
            qk_ref[c] = _pack_heads(qk[c * HEADS * CHUNK:(c + 1) * HEADS * CHUNK])
            inv_ref[c] = inv[c * HEADS * CHUNK:(c + 1) * HEADS * CHUNK]
            ea_ref[c] = eas[c]

    const = lambda shape: pl.BlockSpec(shape, lambda s, g: (0, 0))
    slab = pl.BlockSpec((None, rows, WIDTH), lambda s, g: (s, g, 0))
    mats = pl.BlockSpec((None, LOCAL_CHUNKS, HEADS * CHUNK, CHUNK), lambda s, g: (s, g, 0, 0))
    lead_out = [_sds((CHUNK, WIDTH))] + [_sds((CHUNK, WIDTH), MXU_DTYPE)] * 3 + [_sds((HEADS * CHUNK, CHUNK)), _sds((1, AB_PAD)),
                                                                                _sds((HEADS * CHUNK, CHUNK))]
    sum_mats = _summation_matrices(_running_sum, CHUNK)
    out = pl.pallas_call(
        body, grid=(b, seq // rows), name="gdn_local",
        in_specs=[pl.BlockSpec((None, rows, 4 * WIDTH), lambda s, g: (s, g, 0)),
                  pl.BlockSpec((None, HALO, QKV), lambda s, g: (s, _halo_block(g), 0)), const(p0.shape),
                  pl.BlockSpec((None, rows, AB_PAD), lambda s, g: (s, g, 0)), const(ab0.shape), const(cw.shape),
                  const(alog.shape), const(dtb.shape), const(sum_mats[0].shape), const(sum_mats[1].shape)],
        out_specs=[slab] * 4 + [pl.BlockSpec((None, LOCAL_CHUNKS, 2 * CHUNK, 2 * CHUNK), lambda s, g: (s, g, 0, 0)),
                                pl.BlockSpec((None, LOCAL_CHUNKS, 1, AB_PAD), lambda s, g: (s, g, 0, 0)), mats]
        + [const(s.shape) for s in lead_out],
        out_shape=[_sds((b, seq, WIDTH))] + [_sds((b, seq, WIDTH), MXU_DTYPE)] * 3
        + [_sds((b, nreal, 2 * CHUNK, 2 * CHUNK)), _sds((b, nreal, 1, AB_PAD)), _sds((b, nreal, HEADS * CHUNK, CHUNK))] + lead_out,
        compiler_params=_cparams("arbitrary", "arbitrary"),
    )(p, p, p0, ab, ab0, cw, alog, dtb, *sum_mats)
    return out[0:6], out[6], out[7:13], out[13]


def _gd_scan_args(b, k, u_ref, w_ref, qe_ref, ke_ref, qk_ref, ea_ref, z_ref):
    qk = jnp.stack([_packed_head(qk_ref, i, k, h) for i, h in _pairs(b)], axis=0)
    ea = jnp.stack([ea_ref[i, k, :, h:h + 1] for i, h in _pairs(b)], axis=0)
    return (_load_slabs(u_ref, b, k), _load_slabs(w_ref, b, k), _load_slabs(qe_ref, b, k), _load_slabs(ke_ref, b, k), qk, ea,
            _load_slabs(z_ref, b, k))


def _gd_lead_args(b, u0_ref, w0_ref, qe0_ref, ke0_ref, qk0_ref, ea0_ref, p0_ref):
    qk = jnp.stack([qk0_ref[h * CHUNK:(h + 1) * CHUNK, :] for _, h in _pairs(b)], axis=0)
    ea = jnp.stack([ea0_ref[:, h:h + 1] for _, h in _pairs(b)], axis=0)
    return (_lead_slabs(u0_ref[...], b), _lead_slabs(w0_ref[...], b), _lead_slabs(qe0_ref[...], b), _lead_slabs(ke0_ref[...], b),
            qk, ea, _lead_slabs(p0_ref[:, QKV:QKV + WIDTH], b))


def gd_scan_fwd(p, p0, local, lead, nw):
    b, seq, _ = p.shape
    slab, per_chunk, const = _scan_specs(b, seq // (SCAN_CHUNKS_FWD * CHUNK), False, SCAN_CHUNKS_FWD)

    def body(u_ref, w_ref, qe_ref, ke_ref, qk_ref, ea_ref, z_ref, u0_ref, w0_ref, qe0_ref, ke0_ref, qk0_ref, ea0_ref, p0_ref,
             nw_ref, y_ref, ss_ref, st):
        @pl.when(pl.program_id(0) == 0)
        def _():
            lead_args = _gd_lead_args(b, u0_ref, w0_ref, qe0_ref, ke0_ref, qk0_ref, ea0_ref, p0_ref)
            st[...] = gd_scan(*lead_args, nw_ref[...], jnp.zeros(st.shape, F32))[1]

        s = st[...]
        for k in range(SCAN_CHUNKS_FWD):
            _save_states(ss_ref, s, b, k)
            y, s = gd_scan(*_gd_scan_args(b, k, u_ref, w_ref, qe_ref, ke_ref, qk_ref, ea_ref, z_ref), nw_ref[...], s)
            _store_slabs(y_ref, y, b, k)
        st[...] = s

    return dict(
        body=body, args=(*local, p, *lead, p0, nw),
        in_specs=[slab(0)] * 4 + [per_chunk(2 * CHUNK, 2 * CHUNK), per_chunk(1, AB_PAD), slab(3)] + [const(a) for a in lead]
        + [const(p0), const(nw)],
        out_specs=[slab(0), per_chunk(WIDTH, DH)],
        out_shape=[_sds((b, seq, WIDTH), MXU_DTYPE), _sds((b, seq // CHUNK, WIDTH, DH), MXU_DTYPE)],
        scratch_shapes=[pltpu.VMEM((b * HEADS, DH, DH), F32)])


def gd_scan_bwd(p, p0, local, lead, nw, ssave, dy):
    b, seq, _ = p.shape
    ng = seq // (SCAN_CHUNKS * CHUNK)
    slab, per_chunk, const = _scan_specs(b, ng, True, SCAN_CHUNKS)

    def body(u_ref, w_ref, qe_ref, ke_ref, qk_ref, ea_ref, z_ref, u0_ref, w0_ref, qe0_ref, ke0_ref, qk0_ref, ea0_ref, p0_ref,
             nw_ref, ss_ref, dy_ref, du_ref, dw_ref, dqe_ref, dke_ref, dqk_ref, dea_ref, dz_ref, du0_ref, dw0_ref, dqe0_ref,
             dke0_ref, dqk0_ref, dea0_ref, dz0_ref, dnw_ref, dst):
        i = pl.program_id(0)
        lane = lax.broadcasted_iota(jnp.int32, (1, AB_PAD), 1)

        def gate_rows(dea, j):
            return sum(jnp.where(lane == h, dea[j * HEADS + h], 0.0) for h in range(HEADS))

        def matrix_rows(dqk, j):
            return jnp.concatenate([dqk[j * HEADS + h] for h in range(HEADS)], axis=0)

        @pl.when(i == 0)
        def _():
            dst[...] = jnp.zeros_like(dst)
            dnw_ref[...] = jnp.zeros_like(dnw_ref)

        ds = dst[...]
        for k in reversed(range(SCAN_CHUNKS)):
            args = _gd_scan_args(b, k, u_ref, w_ref, qe_ref, ke_ref, qk_ref, ea_ref, z_ref)
            _, vjp = jax.vjp(gd_scan, *args, nw_ref[...], _load_states(ss_ref, b, k))
            du, dw, dqe, dke, dqk, dea, dz, dnw, ds = vjp((_load_slabs(dy_ref, b, k), ds))
            dnw_ref[...] += dnw
            for ref, val in ((du_ref, du), (dw_ref, dw), (dqe_ref, dqe), (dke_ref, dke), (dz_ref, dz)):
                _store_slabs(ref, val, b, k)
            for j in range(b):
                dqk_ref[j, k] = _pack_heads(matrix_rows(dqk, j))
                dea_ref[j, k] = gate_rows(dea, j)
        dst[...] = ds

        @pl.when(i == ng - 1)
        def _():
            args = _gd_lead_args(b, u0_ref, w0_ref, qe0_ref, ke0_ref, qk0_ref, ea0_ref, p0_ref)
            _, vjp = jax.vjp(gd_scan, *args, nw_ref[...], jnp.zeros(dst.shape, F32))
            du, dw, dqe, dke, dqk, dea, dz, dnw, _ = vjp((jnp.zeros((b * HEADS, CHUNK, DH), F32), ds))
            dnw_ref[...] += dnw
            for ref, val in ((du0_ref, du), (dw0_ref, dw), (dqe0_ref, dqe), (dke0_ref, dke), (dz0_ref, dz)):
                ref[...] = _sum_rows(val, b)
            dqk0_ref[...] = sum((matrix_rows(dqk, j) for j in range(1, b)), matrix_rows(dqk, 0))
            dea0_ref[...] = sum((gate_rows(dea, j) for j in range(1, b)), gate_rows(dea, 0))

    uu, ww, qe, ke, qk, ea = local
    return dict(
        body=body, args=(*local, p, *lead, p0, nw, ssave, dy),
        in_specs=[slab(0)] * 4 + [per_chunk(2 * CHUNK, 2 * CHUNK), per_chunk(1, AB_PAD), slab(3)] + [const(a) for a in lead]
        + [const(p0), const(nw), per_chunk(WIDTH, DH), slab(0)],
        out_specs=[slab(0)] * 4 + [per_chunk(2 * CHUNK, 2 * CHUNK), per_chunk(1, AB_PAD), slab(0)] + [const(a) for a in lead]
        + [const(lead[0]), const(nw)],
        out_shape=[_sds((b, seq, WIDTH))] * 4 + [_sds(qk.shape), _sds(ea.shape), _sds((b, seq, WIDTH), MXU_DTYPE)]
        + [_sds(a.shape) for a in lead] + [_sds(lead[0].shape), _sds(nw.shape)],
        scratch_shapes=[pltpu.VMEM((b * HEADS, DH, DH), F32)])


def _gd_local_vjp(sum_mats, inv_rows, xx, ab, cw, alog, dtb):
    nb = ab.shape[0] // CHUNK
    inv = jnp.stack([inv_rows[g * CHUNK:(g + 1) * CHUNK] for g in range(nb * HEADS)], axis=0)
    _, vjp, _ = jax.vjp(lambda *a: gd_local(*a, sum_mats, inverse=_saved_inverse(inv)), xx, ab, cw, alog, dtb, has_aux=True)
    return vjp


def gd_local_bwd(p, p0, ab, ab0, cw, alog, dtb, inv, inv0, cot, dz, cot0, dz0):
    b, seq, _ = p.shape
    rows = LOCAL_CHUNKS * CHUNK
    ng = seq // rows
    du, dw, dqe, dke, dqk, dea = cot

    def body(p_ref, halo_ref, p0_ref, ab_ref, ab0_ref, cw_ref, al_ref, dt_ref, s_ref, st_ref, inv_ref, inv0_ref, du_ref, dw_ref,
             dqe_ref, dke_ref, dqk_ref, dea_ref, dz_ref, du0_ref, dw0_ref, dqe0_ref, dke0_ref, dqk0_ref, dea0_ref, dz0_ref,
             dp_ref, dab_ref, dp0_ref, dab0_ref, dcw_ref, dal_ref, ddt_ref, dhalo, dtail):
        s, i = pl.program_id(0), pl.program_id(1)
        g = ng - 1 - i
        sum_mats = (s_ref[...], st_ref[...])

        @pl.when(i == 0)
        def _():
            dhalo[...] = jnp.zeros_like(dhalo)

        @pl.when((s == 0) & (i == 0))
        def _():
            dtail[...] = jnp.zeros_like(dtail)
            dcw_ref[...] = jnp.zeros_like(dcw_ref)
            dal_ref[...] = jnp.zeros_like(dal_ref)
            ddt_ref[...] = jnp.zeros_like(ddt_ref)

        def finish(dxx, dab, dcw, dal, ddt, before, n, dz_val, dp_out, dab_out):
            dqkv = dxx[HALO:HALO + n] + jnp.concatenate([jnp.zeros((n - HALO, QKV), F32), before], axis=0)
            dp_out[...] = jnp.concatenate([dqkv.astype(MXU_DTYPE), dz_val.astype(MXU_DTYPE)], axis=1)
            dab_out[...] = dab.astype(MXU_DTYPE)
            dcw_ref[...] += dcw
            dal_ref[...] += dal
            ddt_ref[...] += ddt

        inv_rows = jnp.concatenate([inv_ref[c] for c in range(LOCAL_CHUNKS)], axis=0)
        vjp = _gd_local_vjp(sum_mats, inv_rows, _gd_window(g, p_ref, halo_ref, p0_ref), ab_ref[...], cw_ref[...], al_ref[...],
                            dt_ref[...])
        dqk_all = jnp.concatenate([_unpack_heads(dqk_ref[c]) for c in range(LOCAL_CHUNKS)], axis=0)
        deas = tuple(dea_ref[c] for c in range(LOCAL_CHUNKS))
        grads = vjp((du_ref[...], dw_ref[...], dqe_ref[...], dke_ref[...], dqk_all, deas))
        finish(*grads, dhalo[...], rows, dz_ref[...], dp_ref, dab_ref)
        dhalo[...] = grads[0][0:HALO]

        @pl.when(g == 0)
        def _():
            dtail[...] += grads[0][0:HALO]

        @pl.when((s == b - 1) & (g == 0))
        def _():
            vjp0 = _gd_local_vjp(sum_mats, inv0_ref[...], _lead_window(p0_ref), ab0_ref[...], cw_ref[...], al_ref[...],
                                 dt_ref[...])
            grads0 = vjp0((du0_ref[...], dw0_ref[...], dqe0_ref[...], dke0_ref[...], dqk0_ref[...], (dea0_ref[...],)))
            finish(*grads0, dtail[...], CHUNK, dz0_ref[...], dp0_ref, dab0_ref)

    rg = lambda i: ng - 1 - i
    const = lambda a: pl.BlockSpec(a.shape, lambda s, i: (0, 0))
    slab = pl.BlockSpec((None, rows, WIDTH), lambda s, i: (s, rg(i), 0))
    wide = pl.BlockSpec((None, rows, 4 * WIDTH), lambda s, i: (s, rg(i), 0))
    gates = pl.BlockSpec((None, rows, AB_PAD), lambda s, i: (s, rg(i), 0))
    mats = pl.BlockSpec((None, LOCAL_CHUNKS, HEADS * CHUNK, CHUNK), lambda s, i: (s, rg(i), 0, 0))
    sum_mats = _summation_matrices(_running_sum, CHUNK)
    return pl.pallas_call(
        body, grid=(b, ng), name="gdn_local_bwd",
        in_specs=[wide, pl.BlockSpec((None, HALO, QKV), lambda s, i: (s, _halo_block(rg(i)), 0)), const(p0), gates, const(ab0),
                  const(cw), const(alog), const(dtb), const(sum_mats[0]), const(sum_mats[1]), mats, const(inv0), slab, slab,
                  slab, slab, pl.BlockSpec((None, LOCAL_CHUNKS, 2 * CHUNK, 2 * CHUNK), lambda s, i: (s, rg(i), 0, 0)),
                  pl.BlockSpec((None, LOCAL_CHUNKS, 1, AB_PAD), lambda s, i: (s, rg(i), 0, 0)), slab]
        + [const(a) for a in cot0] + [const(dz0)],
        out_specs=[wide, gates, const(p0), const(ab0), const(cw), const(alog), const(dtb)],
        out_shape=[_sds(p.shape, MXU_DTYPE), _sds(ab.shape, MXU_DTYPE), _sds(p0.shape, MXU_DTYPE), _sds(ab0.shape, MXU_DTYPE),
                   _sds(cw.shape), _sds(alog.shape), _sds(dtb.shape)],
        scratch_shapes=[pltpu.VMEM((HALO, QKV), F32), pltpu.VMEM((HALO, QKV), F32)],
        compiler_params=_cparams("arbitrary", "arbitrary"),
    )(p, p, p0, ab, ab0, cw, alog, dtb, *sum_mats, inv, inv0, du, dw, dqe, dke, dqk, dea, dz, *cot0, dz0)


def _position():
    return lax.axis_index("x"), lax.axis_index("y"), lax.axis_index("c")


EXCHANGE_COPIES = 10


def _exchange_blocks(bufs, send_sems, recv_sems):
    x, y, c = _position()
    here, x_nbr, y_nbr, diag = (x, y), (1 - x, y), (x, 1 - y), (1 - x, 1 - y)
    sibling = (x, y, 1 - c)
    me = (x, y, c)
    n = range(len(bufs))

    def rows(a, chip, core, half=None):
        block = bufs[a].at[4 * chip[0] + 2 * chip[1] + core]
        if half is None:
            return block
        total = bufs[a].shape[1]
        tile = 8 * (4 // jnp.dtype(bufs[a].dtype).itemsize)
        split = total // 2 // tile * tile
        return block.at[pl.ds(0, split)] if half == 0 else block.at[pl.ds(split, total - split)]

    def copy(a, k, region, to):
        return pltpu.make_async_remote_copy(src_ref=region, dst_ref=region, send_sem=send_sems.at[a * EXCHANGE_COPIES + k],
                                            recv_sem=recv_sems.at[a * EXCHANGE_COPIES + k], device_id=to, device_id_type=MESH)

    sent = [copy(a, 0, rows(a, here, c), sibling) for a in n]
    sent += [cp for a in n for cp in (copy(a, 1, rows(a, here, c, 0), (*x_nbr, c)), copy(a, 4, rows(a, here, c, 1), (*y_nbr, c)))]
    sent += [cp for a in n for cp in (copy(a, 2, rows(a, here, c, 1), (*x_nbr, c)), copy(a, 3, rows(a, here, c, 0), (*y_nbr, c)))]
    for cp in sent:
        cp.start()

    def after(arrivals, a, k, region, to):
        for cp in arrivals:
            cp.wait_recv()
        sent.append(copy(a, k, region, to))
        sent[-1].start()

    for a in n:
        after([copy(a, 1, rows(a, x_nbr, c, 0), me)], a, 5, rows(a, x_nbr, c, 0), (*y_nbr, c))
        after([copy(a, 4, rows(a, y_nbr, c, 1), me)], a, 6, rows(a, y_nbr, c, 1), (*x_nbr, c))
    for a in n:
        after([copy(a, 2, rows(a, x_nbr, c, 1), me)], a, 7, rows(a, x_nbr, c), sibling)
        after([copy(a, 3, rows(a, y_nbr, c, 0), me)], a, 8, rows(a, y_nbr, c), sibling)
    for a in n:
        after([copy(a, 5, rows(a, diag, c, 0), me), copy(a, 6, rows(a, diag, c, 1), me)], a, 9, rows(a, diag, c), sibling)
    for a in n:
        copy(a, 0, rows(a, here, 1 - c), me).wait_recv()
        for k, chip in ((7, x_nbr), (8, y_nbr), (9, diag)):
            copy(a, k, rows(a, chip, 1 - c), me).wait_recv()
    for cp in sent:
        cp.wait_send()


def _exchange_sems(n_bufs):
    return [pltpu.SemaphoreType.DMA((n_bufs * EXCHANGE_COPIES,)), pltpu.SemaphoreType.DMA((n_bufs * EXCHANGE_COPIES,))]


def gather_weights(w_in_t, w_out, small, pad_rows):
    rows, _, cols = w_in_t.shape
    buf_rows = -(-rows // ROW_TILE_BF16) * ROW_TILE_BF16

    def body(wi_ref, wo_ref, sm_ref, wi_out, wo_out, sm_out, wi_buf, send_sems, recv_sems):
        x, y, c = _position()
        me = 4 * x + 2 * y + c
        wi_buf[me, pl.ds(0, rows), :] = wi_ref[:, 0, :].astype(MXU_DTYPE)
        wi_buf[me, pl.ds(rows, buf_rows - rows), :] = jnp.zeros((buf_rows - rows, cols), MXU_DTYPE)
        wo_out[me] = wo_ref[...].astype(MXU_DTYPE)
        sm_out[me] = sm_ref[...]
        _exchange_blocks([wi_buf, wo_out, sm_out], send_sems, recv_sems)
        for d in range(N_DEV):
            wi_out[pl.ds(d * rows, rows), :] = wi_buf[d, pl.ds(0, rows), :]
        wi_out[pl.ds(N_DEV * rows, pad_rows), :] = jnp.zeros((pad_rows, cols), MXU_DTYPE)

    return pl.pallas_call(
        body, name="gather_weights", in_specs=[VMEM_SPEC] * 3, out_specs=[VMEM_SPEC] * 3,
        out_shape=[jax.ShapeDtypeStruct((N_DEV * rows + pad_rows, cols), MXU_DTYPE),
                   jax.ShapeDtypeStruct((N_DEV,) + w_out.shape, MXU_DTYPE), jax.ShapeDtypeStruct((N_DEV,) + small.shape, F32)],
        scratch_shapes=[pltpu.VMEM((N_DEV, buf_rows, cols), MXU_DTYPE)] + _exchange_sems(3),
        compiler_params=pltpu.CompilerParams(vmem_limit_bytes=VMEM_LIMIT))(w_in_t, w_out, small)


HOPS = 6


def reduce_gradients(tensors, small, name):
    n_t = len(tensors)
    arrays = [a for parts, _ in tensors for a, _ in parts]
    first_array = [sum(len(parts) for parts, _ in tensors[:t]) for t in range(n_t)]

    def pieces(t, j):
        parts, block_rows = tensors[t]
        out, base = [], 0
        for pi, (_, valid) in enumerate(parts):
            lo, hi = max(j * block_rows, base), min((j + 1) * block_rows, base + valid)
            if lo < hi:
                out.append((first_array[t] + pi, lo - base, lo - j * block_rows, hi - lo))
            base += valid
        return out

    def body(*refs):
        n_a = len(arrays)
        in_refs, small_ref = refs[:n_a], refs[n_a]
        out_refs, small_sum = refs[n_a + 1:n_a + 1 + n_t], refs[n_a + 1 + n_t]
        bufs, small_buf = refs[n_a + 2 + n_t:n_a + 2 + 5 * n_t], refs[n_a + 2 + 5 * n_t]
        s1_sems, r1_sems, s2_sems, r2_sems, small_send, small_recv = refs[n_a + 3 + 5 * n_t:]
        x, y, c = _position()
        chip = 2 * x + y

        def put(t, dst, j, add=None):
            for ai, src_row, dst_row, size in pieces(t, j):
                v = in_refs[ai][pl.ds(src_row, size), :]
                if add is not None:
                    v = v + add[pl.ds(dst_row, size), :].astype(F32)
                dst[pl.ds(dst_row, size), :] = v.astype(dst.dtype)

        def swap(t, k):
            send1, recv1 = bufs[4 * t], bufs[4 * t + 1]
            return pltpu.make_async_remote_copy(src_ref=send1.at[k], dst_ref=recv1.at[k], send_sem=s1_sems.at[4 * t + k],
                                                recv_sem=r1_sems.at[4 * t + k], device_id=(x, y, 1 - c), device_id_type=MESH)

        to_x, to_y, to_diag = 2 * (1 - x) + y, 2 * x + (1 - y), 2 * (1 - x) + (1 - y)
        x_dev, y_dev = (1 - x, y, c), (x, 1 - y, c)

        def half(ref, h):
            total = ref.shape[0]
            split = total // 2 // ROW_TILE_BF16 * ROW_TILE_BF16
            return ref.at[pl.ds(0, split)] if h == 0 else ref.at[pl.ds(split, total - split)]

        def hop(t, copy_id, src, dst, to):
            return pltpu.make_async_remote_copy(src_ref=src, dst_ref=dst, send_sem=s2_sems.at[HOPS * t + copy_id],
                                                recv_sem=r2_sems.at[HOPS * t + copy_id], device_id=to, device_id_type=MESH)

        def hops(t):
            send2, landing = bufs[4 * t + 2], bufs[4 * t + 3]
            return [hop(t, 0, half(send2.at[to_diag], 0), half(landing.at[0], 0), x_dev),
                    hop(t, 1, half(send2.at[to_diag], 1), half(landing.at[0], 1), y_dev),
                    hop(t, 2, half(send2.at[to_x], 0), half(landing.at[1], 0), x_dev),
                    hop(t, 3, half(send2.at[to_y], 1), half(landing.at[2], 1), y_dev),
                    hop(t, 4, half(send2.at[to_x], 1), half(landing.at[1], 1), x_dev),
                    hop(t, 5, half(send2.at[to_y], 0), half(landing.at[2], 0), y_dev)]

        def add_relay(t, slot, h):
            dst, src = half(bufs[4 * t + 2].at[slot], h), half(bufs[4 * t + 3].at[0], h)
            dst[...] = (dst[...].astype(F32) + src[...].astype(F32)).astype(dst.dtype)

        for t in range(n_t):
            send2 = bufs[4 * t + 2]
            pad = send2.shape[1] - tensors[t][1]
            if pad:
                send2[:, pl.ds(tensors[t][1], pad), :] = jnp.zeros((4, pad, send2.shape[2]), send2.dtype)
            for j in range(N_DEV):
                @pl.when((j & 1) != c)
                def _():
                    put(t, bufs[4 * t].at[j >> 1], j)
            for k in range(4):
                swap(t, k).start()

        small_buf[4 * x + 2 * y + c] = small_ref[...]
        _exchange_blocks([small_buf], small_send, small_recv)
        total = small_buf[0]
        for d in range(1, N_DEV):
            total = total + small_buf[d]
        small_sum[...] = total

        for t in range(n_t):
            recv1 = bufs[4 * t + 1]
            for k in range(4):
                swap(t, k).wait_recv()
                for j in (2 * k, 2 * k + 1):
                    @pl.when(((j & 1) == c) & (k != chip))
                    def _():
                        put(t, bufs[4 * t + 2].at[k], j, add=recv1.at[k])

                    @pl.when(((j & 1) == c) & (k == chip))
                    def _():
                        put(t, out_refs[t], j, add=recv1.at[k])
            for cp in hops(t)[0:4]:
                cp.start()

        for t in range(n_t):
            cps = hops(t)
            cps[0].wait_recv()
            add_relay(t, to_y, 0)
            cps[5].start()
            cps[1].wait_recv()
            add_relay(t, to_x, 1)
            cps[4].start()

        for t in range(n_t):
            cps, rows = hops(t), tensors[t][1]
            for first, second, slot in ((cps[2], cps[4], 1), (cps[3], cps[5], 2)):
                first.wait_recv()
                second.wait_recv()
                out_refs[t][...] += bufs[4 * t + 3][slot, pl.ds(0, rows), :].astype(F32)

        for t in range(n_t):
            for cp in hops(t):
                cp.wait_send()
            for k in range(4):
                swap(t, k).wait_send()

    scratch, out_shape = [], []
    for parts, block_rows in tensors:
        cols = parts[0][0].shape[1]
        tiled_rows = -(-block_rows // ROW_TILE_BF16) * ROW_TILE_BF16
        scratch += [pltpu.VMEM((4, block_rows, cols), MXU_DTYPE)] * 2
        scratch += [pltpu.VMEM((4, tiled_rows, cols), MXU_DTYPE), pltpu.VMEM((3, tiled_rows, cols), MXU_DTYPE)]
        out_shape.append(jax.ShapeDtypeStruct((block_rows, cols), F32))
    out_shape.append(jax.ShapeDtypeStruct(small.shape, F32))
    scratch += [pltpu.VMEM((N_DEV,) + small.shape, F32)] + [pltpu.SemaphoreType.DMA((4 * n_t,))] * 2
    scratch += [pltpu.SemaphoreType.DMA((HOPS * n_t,))] * 2 + _exchange_sems(1)
    return pl.pallas_call(
        body, name=name, in_specs=[VMEM_SPEC] * (len(arrays) + 1), out_specs=[VMEM_SPEC] * (n_t + 1), out_shape=out_shape,
        scratch_shapes=scratch, compiler_params=pltpu.CompilerParams(vmem_limit_bytes=VMEM_LIMIT),
    )(*arrays, small)


def _adamw_step(w, g, m, v):
    mn = ADAM_B1 * m + (1.0 - ADAM_B1) * g
    vn = ADAM_B2 * v + (1.0 - ADAM_B2) * jnp.square(g)
    m_hat = mn / (1.0 - ADAM_B1 ** ADAM_STEP)
    v_hat = vn / (1.0 - ADAM_B2 ** ADAM_STEP)
    return -ADAM_LR * (m_hat / (jnp.sqrt(v_hat) + ADAM_EPS) + ADAM_WD * w), mn, vn


def adamw_small(packed, first_rows, ws, gs, ms, vs):
    k = len(ws)
    given = [g for g in gs if g is not None]

    def body(*refs):
        packed_ref, w_refs, m_refs, v_refs = refs[0], refs[1:1 + k], refs[1 + k:1 + 2 * k], refs[1 + 2 * k:1 + 3 * k]
        g_refs, outs = iter(refs[1 + 3 * k:1 + 3 * k + len(given)]), refs[1 + 3 * k + len(given):]
        for i in range(k):
            rows, cols = w_refs[i].shape
            g = next(g_refs)[...] if gs[i] is not None else packed_ref[first_rows[i]:first_rows[i] + rows, 0:cols]
            outs[4 * i][...] = g
            outs[4 * i + 1][...], outs[4 * i + 2][...], outs[4 * i + 3][...] = _adamw_step(w_refs[i][...], g, m_refs[i][...],
                                                                                          v_refs[i][...])

    n_in = 1 + 3 * k + len(given)
    out = pl.pallas_call(body, name="adamw_small", in_specs=[VMEM_SPEC] * n_in, out_specs=[VMEM_SPEC] * (4 * k),
                         out_shape=[jax.ShapeDtypeStruct(w.shape, F32) for w in ws for _ in range(4)],
                         compiler_params=pltpu.CompilerParams(vmem_limit_bytes=VMEM_LIMIT))(packed, *ws, *ms, *vs, *given)
    return [out[4 * i:4 * i + 4] for i in range(k)]


def adamw_w_in(w, g_t, m, v):
    def body(w_ref, g_ref, m_ref, v_ref, go_ref, d_ref, nm_ref, nv_ref):
        g = g_ref[...]
        go_ref[:, 0, :] = g
        d_ref[:, 0, :], nm_ref[:, 0, :], nv_ref[:, 0, :] = _adamw_step(w_ref[:, 0, :], g, m_ref[:, 0, :], v_ref[:, 0, :])

    return pl.pallas_call(body, name="adamw_w_in", in_specs=[VMEM_SPEC] * 4, out_specs=[VMEM_SPEC] * 4,
                          out_shape=[jax.ShapeDtypeStruct(w.shape, F32)] * 4,
                          compiler_params=pltpu.CompilerParams(vmem_limit_bytes=VMEM_LIMIT))(w, g_t, m, v)


def _pad_rows(a, rows=8):
    return jnp.pad(a, ((0, rows - a.shape[0]), (0, 0)))


def _pad_lanes(a, lanes=128):
    return jnp.pad(a, ((0, 0), (0, lanes - a.shape[1])))


def kernel(x, meta_tokens, norm_w, w_in, conv_w, hg_lb_logits, hg_norm_w, gdn_A_log, gdn_dt_bias, gdn_norm_w, w_out, final_norm_w, loss_target, m_meta_tokens, m_norm_w, m_w_in, m_conv_w, m_hg_lb_logits, m_hg_norm_w, m_gdn_A_log, m_gdn_dt_bias, m_gdn_norm_w, m_w_out, m_final_norm_w, v_meta_tokens, v_norm_w, v_w_in, v_conv_w, v_hg_lb_logits, v_hg_norm_w, v_gdn_A_log, v_gdn_dt_bias, v_gdn_norm_w, v_w_out, v_final_norm_w):
    b, seq, _ = x.shape
    n = b * seq
    dev = 4 * lax.axis_index("x") + 2 * lax.axis_index("y") + lax.axis_index("c")
    col_shard = IN_COLS // N_DEV

    small_w = jnp.concatenate([_pad_lanes(meta_tokens, 256), _pad_rows(_pad_lanes(conv_w[0], 256))], axis=0)
    w_t, w_out_g, small_g = gather_weights(jnp.transpose(w_in, (2, 0, 1)), w_out[0], small_w, AB_PAD - 2 * HEADS)
    meta_g = small_g[:, 0:N_META, 0:D_MODEL // N_DEV]
    conv_g = small_g[:, N_META:N_META + CONV_TAPS, 0:QKV // N_DEV]
    w_out_full = w_out_g.reshape(2 * WIDTH, D_MODEL)
    cw = jnp.transpose(conv_g, (1, 0, 2)).reshape(CONV_TAPS, QKV)
    meta = jnp.transpose(meta_g, (1, 0, 2)).reshape(N_META, D_MODEL)
    alog = _pad_lanes(gdn_A_log)
    dtb = _pad_lanes(gdn_dt_bias)
    fw = final_norm_w.reshape(1, D_MODEL)

    h0 = jnp.concatenate([jnp.zeros((CHUNK - N_META, D_MODEL), F32), meta], axis=0)
    x2 = x.reshape(n, D_MODEL)
    phg, pgd, pab, phg0, pgd0, pab0, u0 = in_proj(x2, h0, norm_w, w_t)
    phg3, pgd3, pab3 = phg.reshape(b, seq, 4 * WIDTH), pgd.reshape(b, seq, 4 * WIDTH), pab.reshape(b, seq, AB_PAD)
    hg_loc, hg_lead = hg_local_fwd(phg3, phg0, hg_lb_logits)
    gd_loc, gd_inv, gd_lead, gd_inv0 = gd_local_fwd(pgd3, pgd0, pab3, pab0, cw, alog, dtb)
    (y_hg, s_hg), (y_gd, s_gd) = run_scans([hg_scan_fwd(phg3, phg0, hg_loc, hg_lead, hg_norm_w),
                                            gd_scan_fwd(pgd3, pgd0, gd_loc, gd_lead, gdn_norm_w)],
                                           seq // (SCAN_CHUNKS_FWD * CHUNK), "scans")

    dh2, dy_hg, dy_gd, g_w_out, loss_part, g_fw = out_proj_loss(
        x2, loss_target.reshape(n, D_MODEL), y_hg.reshape(n, WIDTH), y_gd.reshape(n, WIDTH), w_out_full, fw)

    hb, gb = run_scans([hg_scan_bwd(phg3, phg0, hg_loc, hg_lead, hg_norm_w, s_hg, dy_hg.reshape(b, seq, WIDTH)),
                        gd_scan_bwd(pgd3, pgd0, gd_loc, gd_lead, gdn_norm_w, s_gd, dy_gd.reshape(b, seq, WIDTH))],
                       seq // (SCAN_CHUNKS * CHUNK), "scans_bwd")
    dphg, dphg0, g_lb = hg_local_bwd(phg3, phg0, hg_lb_logits, hb[0:6], hb[6:12])
    g_hg_nw = hb[12]
    dpgd, dpab, dpgd0, dpab0, g_cw, g_alog, g_dtb = gd_local_bwd(pgd3, pgd0, pab3, pab0, cw, alog, dtb, gd_inv, gd_inv0,
                                                                 gb[0:6], gb[6], gb[7:13], gb[13])
    g_gd_nw = gb[14]
    dphg, dpgd, dpab = dphg.reshape(n, 4 * WIDTH), dpgd.reshape(n, 4 * WIDTH), dpab.reshape(n, AB_PAD)

    grad_x, dh0, g_nw, g_w_hg, g_w_gd, g_w_ab = in_proj_bwd(dphg, dpgd, dpab, w_t, x2, dh2, norm_w, h0, u0, dphg0, dpgd0, dpab0)

    small = jnp.concatenate([
        g_nw.reshape(8, 128), g_lb.reshape(8, 128), _pad_rows(g_hg_nw), _pad_rows(g_alog), _pad_rows(g_dtb), _pad_rows(g_gd_nw),
        g_fw.reshape(8, 128), g_cw.reshape(48, 128),
        dh0[CHUNK - N_META:CHUNK].reshape(128, 128), loss_part], axis=0)
    g_w_in_t, g_w_out, small = reduce_gradients(
        [([(g_w_hg, 4 * WIDTH), (g_w_gd, 4 * WIDTH), (g_w_ab, 2 * HEADS)], col_shard),
         ([(g_w_out, 2 * WIDTH)], (2 * WIDTH) // N_DEV)], small, "reduce_gradients")
    g_cw_full = small[56:104].reshape(CONV_TAPS, QKV)
    g_meta_full = small[104:232].reshape(N_META, D_MODEL)
    loss = small[232, 0]
    g_conv = lax.dynamic_slice_in_dim(g_cw_full, dev * (QKV // N_DEV), QKV // N_DEV, axis=1)
    g_meta = lax.dynamic_slice_in_dim(g_meta_full, dev * (D_MODEL // N_DEV), D_MODEL // N_DEV, axis=1)

    names = ["meta_tokens", "norm_w", "w_in", "conv_w", "hg_lb_logits", "hg_norm_w", "gdn_A_log", "gdn_dt_bias",
             "gdn_norm_w", "w_out", "final_norm_w"]
    weights = [meta_tokens, norm_w, w_in, conv_w, hg_lb_logits, hg_norm_w, gdn_A_log, gdn_dt_bias, gdn_norm_w, w_out,
               final_norm_w]
    moms = [m_meta_tokens, m_norm_w, m_w_in, m_conv_w, m_hg_lb_logits, m_hg_norm_w, m_gdn_A_log, m_gdn_dt_bias,
            m_gdn_norm_w, m_w_out, m_final_norm_w]
    vars_ = [v_meta_tokens, v_norm_w, v_w_in, v_conv_w, v_hg_lb_logits, v_hg_norm_w, v_gdn_A_log, v_gdn_dt_bias,
             v_gdn_norm_w, v_w_out, v_final_norm_w]
    gradient = [g_meta, 0, None, g_conv, 8, 16, 24, 32, 40, g_w_out, 48]
    shape2d = [g_meta.shape, (8, 128), None, g_conv.shape, (8, 128), (1, DH), (1, HEADS), (1, HEADS), (1, DH), g_w_out.shape,
               (8, 128)]
    i_w_in = names.index("w_in")
    others = [i for i in range(len(names)) if i != i_w_in]
    in_rows = lambda i: isinstance(gradient[i], int)
    stepped = adamw_small(small, [gradient[i] if in_rows(i) else None for i in others],
                          [weights[i].reshape(shape2d[i]) for i in others], [None if in_rows(i) else gradient[i] for i in others],
                          [moms[i].reshape(shape2d[i]) for i in others], [vars_[i].reshape(shape2d[i]) for i in others])
    results = {i: [a.reshape(weights[i].shape) for a in stepped[j]] for j, i in enumerate(others)}
    to3, back = (lambda a: jnp.transpose(a, (2, 0, 1))), (lambda a: jnp.transpose(a, (1, 2, 0)))
    results[i_w_in] = [back(a) for a in adamw_w_in(to3(w_in), g_w_in_t, to3(m_w_in), to3(v_w_in))]
    grads, deltas, new_ms, new_vs = zip(*(results[i] for i in range(len(names))))
    return (loss, grad_x.reshape(x.shape), *grads, *deltas, *new_ms, *new_vs)
```

```python
import jax
import jax.numpy as jnp
import numpy as np
from jax import lax
from jax.experimental import pallas as pl
from jax.experimental.pallas import tpu as pltpu

F32 = jnp.float32
BF16 = jnp.bfloat16
MXU_DTYPE = BF16

D_MODEL = 1024
N_META = 16
CHUNK = 64
SUB = 16
ROW_TILE_BF16 = 16
HEADS = 4
DH = 128
WIDTH = HEADS * DH
QKV = 3 * WIDTH
CONV_TAPS = 4
HALO = 8
EPS = 1e-6
IN_COLS = 4 * WIDTH + 4 * WIDTH + 2 * HEADS
AB_PAD = 128
N_DEV = 8
LOCAL_CHUNKS = 4
SCAN_CHUNKS_FWD = 4
SCAN_CHUNKS = 2
VMEM_LIMIT = 56 * 1024 * 1024
VMEM_LIMIT_LARGE = 60 * 1024 * 1024

ADAM_LR = 0.001
ADAM_B1 = 0.9
ADAM_B2 = 0.999
ADAM_EPS = 1e-08
ADAM_WD = 0.01
ADAM_STEP = 10

VMEM_SPEC = pl.BlockSpec(memory_space=pltpu.VMEM)
MESH = pl.DeviceIdType.MESH


def _mm_tn(a, b):
    return lax.dot_general(a.astype(MXU_DTYPE), b.astype(MXU_DTYPE), (((0,), (0,)), ((), ())), preferred_element_type=F32)


def _nn(a, b):
    return lax.dot_general(a.astype(MXU_DTYPE), b.astype(MXU_DTYPE), (((2,), (1,)), ((0,), (0,))), preferred_element_type=F32)


def _nt(a, b):
    return lax.dot_general(a.astype(MXU_DTYPE), b.astype(MXU_DTYPE), (((2,), (2,)), ((0,), (0,))), preferred_element_type=F32)


def _t(a):
    return jnp.swapaxes(a, 1, 2)


@jax.custom_vjp
def _bmm(a, b):
    return _nn(a, b)


_bmm.defvjp(lambda a, b: (_nn(a, b), (a, b)), lambda saved, d: (_nt(d, saved[1]), _nn(_t(saved[0]), d)))


@jax.custom_vjp
def _bmm_nt(a, b):
    return _nt(a, b)


_bmm_nt.defvjp(lambda a, b: (_nt(a, b), (a, b)), lambda saved, d: (_nn(d, saved[1]), _nn(_t(d), saved[0])))


@jax.custom_vjp
def _bmm_tn(a, b):
    return _nn(_t(a), b)


_bmm_tn.defvjp(lambda a, b: (_nn(_t(a), b), (a, b)), lambda saved, d: (_nt(saved[1], d), _nn(saved[0], d)))


def _iota2(n, m):
    return lax.broadcasted_iota(jnp.int32, (n, m), 0), lax.broadcasted_iota(jnp.int32, (n, m), 1)


def _silu(x):
    return x * jax.nn.sigmoid(x)


def _gated_norm(o, z, nw):
    return o * lax.rsqrt(jnp.mean(o * o, axis=-1, keepdims=True) + EPS) * nw * _silu(z)


def _heads(a, nb):
    return jnp.stack([a[c * CHUNK:(c + 1) * CHUNK, h * DH:(h + 1) * DH] for c in range(nb) for h in range(HEADS)], axis=0)


def _unheads(a3, nb):
    return jnp.concatenate(
        [jnp.concatenate([a3[c * HEADS + h] for h in range(HEADS)], axis=1) for c in range(nb)], axis=0)


def _split3(x):
    hi = x.astype(BF16)
    r1 = x - hi.astype(F32)
    mid = r1.astype(BF16)
    return hi, mid, (r1 - mid.astype(F32)).astype(BF16)


def _summation_matrices(pattern, n_out):
    s = pattern(np.arange(n_out)[:, None], np.arange(CHUNK)[None, :]).astype(np.float32)
    return jnp.asarray(np.tile(s, (1, 3)), BF16), jnp.asarray(np.tile(s.T, (1, 2)), BF16)


def _select_rows(mats, chunks):
    width = chunks[0].shape[1]
    out = _summation(*mats, jnp.concatenate(chunks, axis=1))
    return [out[:, c * width:(c + 1) * width] for c in range(len(chunks))]


def _summation_impl(s, v):
    return jnp.dot(s, jnp.concatenate(_split3(v), axis=0), preferred_element_type=F32)


@jax.custom_vjp
def _summation(s, s_t, v):
    return _summation_impl(s, v)


def _summation_fwd(s, s_t, v):
    return _summation_impl(s, v), s_t


def _summation_bwd(s_t, d):
    hi = d.astype(BF16)
    return None, None, jnp.dot(s_t, jnp.concatenate([hi, (d - hi.astype(F32)).astype(BF16)], axis=0),
                               preferred_element_type=F32)


_summation.defvjp(_summation_fwd, _summation_bwd)


def _chunks(x, nb):
    return [x[c * CHUNK:(c + 1) * CHUNK] for c in range(nb)]


def _running_sum(i, j):
    return j <= i


HG_LEVELS = 6


def _hg_sums(i, j):
    lvl, t = i >> HG_LEVELS, i & (CHUNK - 1)
    last = t
    for l in range(1, HG_LEVELS + 1):
        width = HG_LEVELS + 1 - l
        last = np.where(lvl == l, ((t >> width) << width) + (CHUNK >> l) - 1, last)
    return j <= last


def _level_operand(sh, q3, k3, x):
    def second():
        return ((lax.broadcasted_iota(jnp.int32, (CHUNK, DH), 0) >> sh) & 1) == 1

    def forward(q3, k3, x):
        decay = jnp.exp(-jnp.abs(x))
        out = jnp.where(second(), q3, k3) * decay
        return out, (decay, out)

    def backward(saved, d):
        decay, out = saved
        d_side, t = d * decay, d * out
        return jnp.where(second(), d_side, 0.0), jnp.where(second(), 0.0, d_side), jnp.where(second(), t, -t)

    operand = jax.custom_vjp(lambda q3, k3, x: forward(q3, k3, x)[0])
    operand.defvjp(forward, backward)
    return operand(q3, k3, x)


def hg_local(p, logits, sum_mats):
    nb = p.shape[0] // CHUNK
    l0, l1 = logits[0:1], logits[1:2]
    mx = jnp.maximum(l0, l1)
    e0, e1 = jnp.exp(l0 - mx), jnp.exp(l1 - mx)
    lb = e0 / (e0 + e1)
    q = _silu(p[:, 0:WIDTH])
    f = lb + (1.0 - lb) * jax.nn.sigmoid(p[:, WIDTH:2 * WIDTH])
    k = 1.0 - f
    logf = jnp.log(f)
    sums = _select_rows(sum_mats, _chunks(logf, nb))
    level = lambda l: _heads(jnp.concatenate([s[l * CHUNK:(l + 1) * CHUNK] for s in sums], axis=0), nb)
    q3, k3, v3, g3 = _heads(q, nb), _heads(k, nb), _heads(p[:, 2 * WIDTH:3 * WIDTH], nb), level(0)
    r, c = _iota2(CHUNK, CHUNK)
    a = jnp.where(r == c, _bmm_nt(q3, k3), 0.0)
    for l in range(1, HG_LEVELS + 1):
        sh = HG_LEVELS - l
        qk = _level_operand(sh, q3, k3, g3 - level(l))
        pair = ((r >> (sh + 1)) == (c >> (sh + 1))) & (((r >> sh) & 1) == 1) & (((c >> sh) & 1) == 0)
        a = a + jnp.where(pair, _bmm_nt(qk, qk), 0.0)
    o = _bmm(a, v3)
    glast = g3[:, CHUNK - 1:CHUNK, :]
    egs = tuple(jnp.concatenate([jnp.exp(glast[c * HEADS + h]) for h in range(HEADS)], axis=1) for c in range(nb))
    return _unheads(q3 * jnp.exp(g3), nb), _unheads(k3 * jnp.exp(glast - g3), nb), _unheads(o, nb), egs


def hg_scan(q_in, k_out, v, eg, o_intra, z, nw, st):
    o = o_intra + _bmm_nt(q_in, st)
    return _gated_norm(o, z, nw), st * eg + _bmm_tn(v, k_out)


def _tri_y_impl(a):
    r, c = _iota2(CHUNK, CHUNK)
    same16 = (r // SUB) == (c // SUB)
    same32 = (r // (2 * SUB)) == (c // (2 * SUB))
    a0 = jnp.where(same16, a, 0.0)
    y = -a0
    pw = _bmm(a0, a0)
    for _ in range(2):
        y = y + pw + _bmm(y, pw)
        pw = _bmm(pw, pw)
    y = y + pw + _bmm(y, pw)
    for ak in (jnp.where(same32 & jnp.logical_not(same16), a, 0.0), jnp.where(same32, 0.0, a)):
        m = ak + _bmm(y, ak)
        y = y - (m + _bmm(m, y))
    return y


@jax.custom_vjp
def _tri_y(a):
    return _tri_y_impl(a)


def _tri_y_fwd(a):
    y = _tri_y_impl(a)
    return y, y


def _tri_y_bwd(y, dy):
    n = dy + _bmm_tn(y, dy)
    return (-(n + _bmm_nt(n, y)),)


_tri_y.defvjp(_tri_y_fwd, _tri_y_bwd)


def _saved_inverse(y):
    @jax.custom_vjp
    def inverse(a):
        return y

    inverse.defvjp(lambda a: (y, None), lambda _, dy: _tri_y_bwd(y, dy))
    return inverse


def _head_rows(a3, nb):
    return jnp.concatenate([a3[g] for g in range(nb * HEADS)], axis=0)


def _pack_heads(x):
    return jnp.concatenate([x[0:2 * CHUNK], x[2 * CHUNK:4 * CHUNK]], axis=1)


def _unpack_heads(y):
    return jnp.concatenate([y[:, 0:CHUNK], y[:, CHUNK:2 * CHUNK]], axis=0)


def _packed_head(ref, i, k, h):
    return ref[i, k, (h % 2) * CHUNK:(h % 2 + 1) * CHUNK, (h // 2) * CHUNK:(h // 2 + 1) * CHUNK]


def _rows_down(x, s):
    rows = x.shape[0]

    @jax.custom_vjp
    def rotate(v):
        return pltpu.roll(v, s, 0)

    rotate.defvjp(lambda v: (pltpu.roll(v, s, 0), None), lambda _, d: (pltpu.roll(d, rows - s, 0),))
    return rotate(x)


def gd_local(xx, ab, cw, alog, dtb, sum_mats, inverse=_tri_y):
    n = ab.shape[0]
    nb = n // CHUNK
    conv = cw[CONV_TAPS - 1:CONV_TAPS] * xx[HALO:HALO + n]
    for j in range(CONV_TAPS - 1):
        conv = conv + cw[j:j + 1] * _rows_down(xx, CONV_TAPS - 1 - j)[HALO:HALO + n]
    act = _silu(conv)
    x = ab + dtb
    g_all = -jnp.exp(alog) * (jnp.maximum(x, 0.0) + jnp.log1p(jnp.exp(-jnp.abs(x))))
    beta_all = jax.nn.sigmoid(ab)
    gam_all = jnp.concatenate(_select_rows(sum_mats, _chunks(g_all, nb)), axis=0)
    q3, k3, v3 = _heads(act[:, 0:WIDTH], nb), _heads(act[:, WIDTH:2 * WIDTH], nb), _heads(act[:, 2 * WIDTH:QKV], nb)
    q3 = q3 * lax.rsqrt(jnp.sum(q3 * q3, axis=-1, keepdims=True) + EPS) * (DH ** -0.5)
    k3 = k3 * lax.rsqrt(jnp.sum(k3 * k3, axis=-1, keepdims=True) + EPS)
    pairs = [(c, h) for c in range(nb) for h in range(HEADS)]
    beta = jnp.stack([beta_all[c * CHUNK:(c + 1) * CHUNK, HEADS + h:HEADS + h + 1] for c, h in pairs], axis=0)
    gam = jnp.stack([gam_all[c * CHUNK:(c + 1) * CHUNK, h:h + 1] for c, h in pairs], axis=0)
    gam_t = [gam_all[c * CHUNK:(c + 1) * CHUNK].T for c in range(nb)]
    gam_row = jnp.stack([gam_t[c][h:h + 1, :] for c, h in pairs], axis=0)
    glast = gam[:, CHUNK - 1:CHUNK, :]
    r, c = _iota2(CHUNK, CHUNK)
    dec = jnp.exp(jnp.where(c < r, gam - gam_row, -jnp.inf))
    y = inverse(beta * _bmm_nt(k3, k3) * dec)
    eg = jnp.exp(gam)
    rhs = jnp.concatenate([beta * v3, (beta * eg) * k3], axis=2)
    sol = rhs + _bmm(y, rhs)
    qk = _bmm_nt(q3, k3) * jnp.where(r == c, 1.0, dec)
    eas = tuple(jnp.exp(gam_all[(c + 1) * CHUNK - 1:(c + 1) * CHUNK]) for c in range(nb))
    return (_unheads(sol[:, :, 0:DH], nb), _unheads(sol[:, :, DH:2 * DH], nb), _unheads(q3 * eg, nb),
            _unheads(k3 * jnp.exp(glast - gam), nb), _head_rows(qk, nb), eas), _head_rows(y, nb)


def gd_scan(uu, ww, qe, ke, qk, ea, z, nw, s):
    u = uu - _bmm(ww, s)
    o = _bmm(qe, s) + _bmm(qk, u)
    return _gated_norm(o, z, nw), ea * s + _bmm_tn(ke, u)


def _cparams(*sem):
    return pltpu.CompilerParams(dimension_semantics=sem, vmem_limit_bytes=VMEM_LIMIT)


def _row_tile(n):
    for t in (512, 256, 128, 64):
        if n % t == 0:
            return t
    raise ValueError(f"unsupported token count {n}")


def _w_in_specs():
    once = pl.Buffered(1)
    return [pl.BlockSpec((4 * WIDTH, D_MODEL), lambda *i: (0, 0), pipeline_mode=once),
            pl.BlockSpec((4 * WIDTH, D_MODEL), lambda *i: (1, 0), pipeline_mode=once),
            pl.BlockSpec((AB_PAD, D_MODEL), lambda *i: (8 * WIDTH // AB_PAD, 0), pipeline_mode=once)]


def in_proj(h, h0, norm_w, w_t):
    n = h.shape[0]
    tm = _row_tile(n)
    nt = (((1,), (1,)), ((), ()))

    def body(h_ref, h0_ref, nw_ref, whg_ref, wgd_ref, wab_ref, phg_ref, pgd_ref, pab_ref, phg0_ref, pgd0_ref, pab0_ref, u0_ref):
        def project(x, hg_ref, gd_ref, ab_ref):
            u = (x * lax.rsqrt(jnp.mean(x * x, axis=-1, keepdims=True) + EPS) * nw_ref[...]).astype(MXU_DTYPE)
            hg_ref[...] = lax.dot_general(u, whg_ref[...], nt, preferred_element_type=F32)
            gd_ref[...] = lax.dot_general(u, wgd_ref[...], nt, preferred_element_type=F32)
            ab_ref[...] = lax.dot_general(u, wab_ref[...], nt, preferred_element_type=F32)
            return u

        @pl.when(pl.program_id(0) == 0)
        def _():
            u0_ref[...] = project(h0_ref[...], phg0_ref, pgd0_ref, pab0_ref)

        project(h_ref[...], phg_ref, pgd_ref, pab_ref)

    n0 = h0.shape[0]
    row = lambda w: pl.BlockSpec((tm, w), lambda i: (i, 0))
    lead = lambda w: pl.BlockSpec((n0, w), lambda i: (0, 0))
    widths = [4 * WIDTH, 4 * WIDTH, AB_PAD]
    return pl.pallas_call(
        body, grid=(n // tm,), name="in_proj",
        in_specs=[row(D_MODEL), lead(D_MODEL), pl.BlockSpec(norm_w.shape, lambda i: (0, 0))] + _w_in_specs(),
        out_specs=[row(w) for w in widths] + [lead(w) for w in widths] + [lead(D_MODEL)],
        out_shape=[jax.ShapeDtypeStruct((n, w), F32) for w in widths] + [jax.ShapeDtypeStruct((n0, w), F32) for w in widths]
        + [jax.ShapeDtypeStruct((n0, D_MODEL), MXU_DTYPE)],
        compiler_params=_cparams("arbitrary"),
    )(h, h0, norm_w, w_t, w_t, w_t)


def out_proj_loss(x, tgt, y_hg, y_gd, w_out, fw):
    n = x.shape[0]
    tm = _row_tile(n)
    inv_d = 1.0 / D_MODEL

    def body(x_ref, t_ref, yh_ref, yg_ref, w_ref, fw_ref, dh_ref, dyh_ref, dyg_ref, dw_ref, loss_ref, dfw_ref):
        @pl.when(pl.program_id(0) == 0)
        def _():
            dw_ref[...] = jnp.zeros_like(dw_ref)
            loss_ref[...] = jnp.zeros_like(loss_ref)
            dfw_ref[...] = jnp.zeros_like(dfw_ref)

        yh, yg = yh_ref[...], yg_ref[...]
        wa, wb = w_ref[0:WIDTH, :], w_ref[WIDTH:2 * WIDTH, :]
        h2 = x_ref[...] + jnp.dot(yh, wa, preferred_element_type=F32) + jnp.dot(yg, wb, preferred_element_type=F32)
        r2 = lax.rsqrt(jnp.mean(h2 * h2, axis=-1, keepdims=True) + EPS)
        nrm = h2 * r2
        fwv = fw_ref[...]
        err = nrm * fwv - t_ref[...]
        loss_ref[...] += jnp.full(loss_ref.shape, 0.5 * inv_d * jnp.sum(err * err), F32)
        dout = err * inv_d
        dfw_ref[...] += jnp.sum(dout * nrm, axis=0, keepdims=True)
        dn = dout * fwv
        dh2 = r2 * (dn - nrm * jnp.mean(dn * nrm, axis=-1, keepdims=True))
        dh_ref[...] = dh2
        dhb = dh2.astype(MXU_DTYPE)
        dyh_ref[...] = lax.dot_general(dhb, wa, (((1,), (1,)), ((), ())), preferred_element_type=F32)
        dyg_ref[...] = lax.dot_general(dhb, wb, (((1,), (1,)), ((), ())), preferred_element_type=F32)
        dw_ref[0:WIDTH, :] += lax.dot_general(yh, dhb, (((0,), (0,)), ((), ())), preferred_element_type=F32)
        dw_ref[WIDTH:2 * WIDTH, :] += lax.dot_general(yg, dhb, (((0,), (0,)), ((), ())), preferred_element_type=F32)

    row = lambda w: pl.BlockSpec((tm, w), lambda i: (i, 0))
    full = lambda s: pl.BlockSpec(s, lambda i: (0, 0))
    return pl.pallas_call(
        body, grid=(n // tm,), name="out_proj_loss",
        in_specs=[row(D_MODEL), row(D_MODEL), row(WIDTH), row(WIDTH), full(w_out.shape), full(fw.shape)],
        out_specs=[row(D_MODEL), row(WIDTH), row(WIDTH), full((2 * WIDTH, D_MODEL)), full((8, 128)), full((1, D_MODEL))],
        out_shape=[jax.ShapeDtypeStruct((n, D_MODEL), F32), jax.ShapeDtypeStruct((n, WIDTH), F32),
                   jax.ShapeDtypeStruct((n, WIDTH), F32), jax.ShapeDtypeStruct((2 * WIDTH, D_MODEL), F32),
                   jax.ShapeDtypeStruct((8, 128), F32), jax.ShapeDtypeStruct((1, D_MODEL), F32)],
        compiler_params=_cparams("arbitrary"),
    )(x, tgt, y_hg, y_gd, w_out, fw)


def in_proj_bwd(dphg, dpgd, dpab, w_t, h, dh2, norm_w, h0, u0, dphg0, dpgd0, dpab0):
    n = h.shape[0]
    tm = _row_tile(n)
    steps = n // tm

    def body(dphg_ref, dpgd_ref, dpab_ref, whg_ref, wgd_ref, wab_ref, h_ref, dh2_ref, nw_ref, h0_ref, u0_ref, d0hg_ref,
             d0gd_ref, d0ab_ref, dx_ref, dx0_ref, dnw_ref, ghg_ref, ggd_ref, gab_ref, acc_hg, acc_gd, acc_ab):
        i = pl.program_id(0)
        nwv = nw_ref[...]

        def norm_bwd(dps, x):
            du = jnp.dot(dps[0], whg_ref[...], preferred_element_type=F32)
            du += jnp.dot(dps[1], wgd_ref[...], preferred_element_type=F32)
            du += jnp.dot(dps[2], wab_ref[...], preferred_element_type=F32)
            r = lax.rsqrt(jnp.mean(x * x, axis=-1, keepdims=True) + EPS)
            nrm = x * r
            dn = du * nwv
            return r * (dn - nrm * jnp.mean(dn * nrm, axis=-1, keepdims=True)), nrm, jnp.sum(du * nrm, axis=0, keepdims=True)

        def accumulate(dps, u, first):
            for acc, dp in zip((acc_hg, acc_gd, acc_ab), dps):
                step = min(acc.shape[0], 512)
                for lo in range(0, acc.shape[0], step):
                    part = _mm_tn(dp[:, lo:lo + step], u)
                    acc[lo:lo + step, :] = part if first else acc[lo:lo + step, :] + part

        @pl.when(i == 0)
        def _():
            dps0 = (d0hg_ref[...], d0gd_ref[...], d0ab_ref[...])
            dx0_ref[...], _, dnw_ref[...] = norm_bwd(dps0, h0_ref[...])
            accumulate(dps0, u0_ref[...], True)

        dps = (dphg_ref[...], dpgd_ref[...], dpab_ref[...])
        dx, nrm, dnw = norm_bwd(dps, h_ref[...])
        dx_ref[...] = dh2_ref[...] + dx
        dnw_ref[...] += dnw
        accumulate(dps, (nrm * nwv).astype(MXU_DTYPE), False)

        @pl.when(i == steps - 1)
        def _():
            pltpu.sync_copy(acc_hg, ghg_ref)
            pltpu.sync_copy(acc_gd, ggd_ref)
            pltpu.sync_copy(acc_ab, gab_ref)

    row = lambda w: pl.BlockSpec((tm, w), lambda i: (i, 0))
    full = lambda a: pl.BlockSpec(a.shape, lambda i: (0, 0), pipeline_mode=pl.Buffered(1))
    anywhere = pl.BlockSpec(memory_space=pl.ANY)
    return pl.pallas_call(
        body, grid=(steps,), name="in_proj_bwd",
        in_specs=[row(4 * WIDTH), row(4 * WIDTH), row(AB_PAD)] + _w_in_specs() + [row(D_MODEL), row(D_MODEL), full(norm_w),
                                                                                   full(h0), full(u0), full(dphg0), full(dpgd0),
                                                                                   full(dpab0)],
        out_specs=[row(D_MODEL), pl.BlockSpec(h0.shape, lambda i: (0, 0)), pl.BlockSpec((1, D_MODEL), lambda i: (0, 0)),
                   anywhere, anywhere, anywhere],
        out_shape=[jax.ShapeDtypeStruct((n, D_MODEL), F32), jax.ShapeDtypeStruct(h0.shape, F32),
                   jax.ShapeDtypeStruct((1, D_MODEL), F32), jax.ShapeDtypeStruct((4 * WIDTH, D_MODEL), F32),
                   jax.ShapeDtypeStruct((4 * WIDTH, D_MODEL), F32), jax.ShapeDtypeStruct((AB_PAD, D_MODEL), F32)],
        scratch_shapes=[pltpu.VMEM((4 * WIDTH, D_MODEL), F32), pltpu.VMEM((4 * WIDTH, D_MODEL), F32),
                        pltpu.VMEM((AB_PAD, D_MODEL), F32)],
        compiler_params=pltpu.CompilerParams(dimension_semantics=("arbitrary",), vmem_limit_bytes=VMEM_LIMIT_LARGE),
    )(dphg, dpgd, dpab, w_t, w_t, w_t, h, dh2, norm_w, h0, u0, dphg0, dpgd0, dpab0)


def _sds(shape, dtype=F32):
    return jax.ShapeDtypeStruct(shape, dtype)


def _pairs(b):
    return [(i, h) for i in range(b) for h in range(HEADS)]


def _load_slabs(ref, b, k):
    return jnp.stack([ref[i, k * CHUNK:(k + 1) * CHUNK, h * DH:(h + 1) * DH].astype(F32) for i, h in _pairs(b)], axis=0)


def _lead_slabs(a, b):
    return jnp.stack([a[:, h * DH:(h + 1) * DH].astype(F32) for _, h in _pairs(b)], axis=0)


def _rows(a3, i):
    return jnp.concatenate([a3[i * HEADS + h] for h in range(HEADS)], axis=1)


def _store_slabs(ref, a3, b, k):
    for i in range(b):
        ref[i, k * CHUNK:(k + 1) * CHUNK, :] = _rows(a3, i).astype(ref.dtype)


def _sum_rows(a3, b):
    out = _rows(a3, 0)
    for i in range(1, b):
        out = out + _rows(a3, i)
    return out


def _save_states(ref, s, b, k):
    for i in range(b):
        ref[i, k] = jnp.concatenate([s[i * HEADS + h] for h in range(HEADS)], axis=0).astype(ref.dtype)


def _load_states(ref, b, k):
    return jnp.stack([ref[i, k, h * DH:(h + 1) * DH, :].astype(F32) for i, h in _pairs(b)], axis=0)


def hg_local_fwd(p, p0, logits):
    b, seq, _ = p.shape
    rows = LOCAL_CHUNKS * CHUNK
    nreal = seq // CHUNK

    def body(p_ref, p0_ref, lg_ref, s_ref, st_ref, q_ref, k_ref, o_ref, eg_ref, q0_ref, k0_ref, o0_ref, eg0_ref):
        sum_mats = (s_ref[...], st_ref[...])

        @pl.when((pl.program_id(0) == 0) & (pl.program_id(1) == 0))
        def _():
            q_in, k_out, o0_ref[...], (eg0_ref[...],) = hg_local(p0_ref[...], lg_ref[...], sum_mats)
            q0_ref[...], k0_ref[...] = q_in.astype(MXU_DTYPE), k_out.astype(MXU_DTYPE)

        q_in, k_out, o_intra, egs = hg_local(p_ref[...], lg_ref[...], sum_mats)
        q_ref[...], k_ref[...], o_ref[...] = q_in.astype(MXU_DTYPE), k_out.astype(MXU_DTYPE), o_intra
        for c in range(LOCAL_CHUNKS):
            eg_ref[c] = egs[c]

    slab = pl.BlockSpec((None, rows, WIDTH), lambda s, g: (s, g, 0))
    const = lambda shape: pl.BlockSpec(shape, lambda s, g: (0, 0))
    lead_shapes = [(CHUNK, WIDTH)] * 3 + [(1, WIDTH)]
    sum_mats = _summation_matrices(_hg_sums, (HG_LEVELS + 1) * CHUNK)
    out = pl.pallas_call(
        body, grid=(b, seq // rows), name="hgrn2_local",
        in_specs=[pl.BlockSpec((None, rows, 4 * WIDTH), lambda s, g: (s, g, 0)), const(p0.shape), const(logits.shape)]
        + [const(a.shape) for a in sum_mats],
        out_specs=[slab, slab, slab, pl.BlockSpec((None, LOCAL_CHUNKS, 1, WIDTH), lambda s, g: (s, g, 0, 0))]
        + [const(s) for s in lead_shapes],
        out_shape=[_sds((b, seq, WIDTH), MXU_DTYPE)] * 2 + [_sds((b, seq, WIDTH)), _sds((b, nreal, 1, WIDTH))]
        + [_sds(lead_shapes[0], MXU_DTYPE)] * 2 + [_sds(lead_shapes[2]), _sds(lead_shapes[3])],
        compiler_params=_cparams("arbitrary", "arbitrary"),
    )(p, p0, logits, *sum_mats)
    return out[0:4], out[4:8]


def _hg_scan_args(b, k, q_ref, k_ref, o_ref, v_ref, z_ref, eg_ref):
    eg = jnp.stack([eg_ref[i, k, :, h * DH:(h + 1) * DH] for i, h in _pairs(b)], axis=0)
    return (_load_slabs(q_ref, b, k), _load_slabs(k_ref, b, k), _load_slabs(v_ref, b, k), eg, _load_slabs(o_ref, b, k),
            _load_slabs(z_ref, b, k))


def _hg_lead_args(b, q0_ref, k0_ref, o0_ref, p0_ref, eg0_ref):
    eg = jnp.stack([eg0_ref[:, h * DH:(h + 1) * DH] for _, h in _pairs(b)], axis=0)
    return (_lead_slabs(q0_ref[...], b), _lead_slabs(k0_ref[...], b), _lead_slabs(p0_ref[:, 2 * WIDTH:3 * WIDTH], b), eg,
            _lead_slabs(o0_ref[...], b), _lead_slabs(p0_ref[:, 3 * WIDTH:4 * WIDTH], b))


def _scan_specs(b, ng, reverse, chunks):
    group = (lambda i: ng - 1 - i) if reverse else (lambda i: i)
    slab = lambda lane_block: pl.BlockSpec((b, chunks * CHUNK, WIDTH), lambda i: (0, group(i), lane_block))
    per_chunk = lambda *tail: pl.BlockSpec((b, chunks) + tail, lambda i: (0, group(i)) + (0,) * len(tail))
    const = lambda a: pl.BlockSpec(a.shape, lambda i: (0,) * a.ndim)
    return slab, per_chunk, const


def run_scans(parts, nc, name):
    n_in = [len(p["args"]) for p in parts]
    n_out = [len(p["out_shape"]) for p in parts]
    n_scr = [len(p["scratch_shapes"]) for p in parts]

    def body(*refs):
        ins, outs, scr = refs[:sum(n_in)], refs[sum(n_in):sum(n_in) + sum(n_out)], refs[sum(n_in) + sum(n_out):]
        for i, part in enumerate(parts):
            part["body"](*ins[sum(n_in[:i]):sum(n_in[:i + 1])], *outs[sum(n_out[:i]):sum(n_out[:i + 1])],
                         *scr[sum(n_scr[:i]):sum(n_scr[:i + 1])])

    flat = lambda key: [v for p in parts for v in p[key]]
    out = pl.pallas_call(body, grid=(nc,), name=name, in_specs=flat("in_specs"), out_specs=flat("out_specs"),
                         out_shape=flat("out_shape"), scratch_shapes=flat("scratch_shapes"),
                         compiler_params=_cparams("arbitrary"))(*flat("args"))
    return [out[sum(n_out[:i]):sum(n_out[:i + 1])] for i in range(len(parts))]


def hg_scan_fwd(p, p0, local, lead, nw):
    b, seq, _ = p.shape
    q_in, k_out, o_intra, eg = local
    slab, per_chunk, const = _scan_specs(b, seq // (SCAN_CHUNKS_FWD * CHUNK), False, SCAN_CHUNKS_FWD)

    def body(q_ref, k_ref, o_ref, v_ref, z_ref, eg_ref, q0_ref, k0_ref, o0_ref, p0_ref, eg0_ref, nw_ref, y_ref, ss_ref, st):
        @pl.when(pl.program_id(0) == 0)
        def _():
            st[...] = hg_scan(*_hg_lead_args(b, q0_ref, k0_ref, o0_ref, p0_ref, eg0_ref), nw_ref[...], jnp.zeros(st.shape, F32))[1]

        s = st[...]
        for k in range(SCAN_CHUNKS_FWD):
            _save_states(ss_ref, s, b, k)
            y, s = hg_scan(*_hg_scan_args(b, k, q_ref, k_ref, o_ref, v_ref, z_ref, eg_ref), nw_ref[...], s)
            _store_slabs(y_ref, y, b, k)
        st[...] = s

    return dict(
        body=body, args=(q_in, k_out, o_intra, p, p, eg, lead[0], lead[1], lead[2], p0, lead[3], nw),
        in_specs=[slab(0), slab(0), slab(0), slab(2), slab(3), per_chunk(1, WIDTH)] + [const(a) for a in lead[0:3]]
        + [const(p0), const(lead[3]), const(nw)],
        out_specs=[slab(0), per_chunk(WIDTH, DH)],
        out_shape=[_sds((b, seq, WIDTH), MXU_DTYPE), _sds((b, seq // CHUNK, WIDTH, DH), MXU_DTYPE)],
        scratch_shapes=[pltpu.VMEM((b * HEADS, DH, DH), F32)])


def hg_scan_bwd(p, p0, local, lead, nw, ssave, dy):
    b, seq, _ = p.shape
    ng = seq // (SCAN_CHUNKS * CHUNK)
    q_in, k_out, o_intra, eg = local
    slab, per_chunk, const = _scan_specs(b, ng, True, SCAN_CHUNKS)

    def body(q_ref, k_ref, o_ref, v_ref, z_ref, eg_ref, q0_ref, k0_ref, o0_ref, p0_ref, eg0_ref, nw_ref, ss_ref, dy_ref,
             dq_ref, dk_ref, do_ref, dv_ref, dz_ref, deg_ref, dq0_ref, dk0_ref, do0_ref, dv0_ref, dz0_ref, deg0_ref, dnw_ref,
             dst):
        i = pl.program_id(0)

        @pl.when(i == 0)
        def _():
            dst[...] = jnp.zeros_like(dst)
            dnw_ref[...] = jnp.zeros_like(dnw_ref)

        ds = dst[...]
        for k in reversed(range(SCAN_CHUNKS)):
            args = _hg_scan_args(b, k, q_ref, k_ref, o_ref, v_ref, z_ref, eg_ref)
            _, vjp = jax.vjp(hg_scan, *args, nw_ref[...], _load_states(ss_ref, b, k))
            dq, dk, dv, deg, do, dz, dnw, ds = vjp((_load_slabs(dy_ref, b, k), ds))
            dnw_ref[...] += dnw
            for ref, val in ((dq_ref, dq), (dk_ref, dk), (do_ref, do), (dv_ref, dv), (dz_ref, dz)):
                _store_slabs(ref, val, b, k)
            for j in range(b):
                deg_ref[j, k] = _rows(deg, j)
        dst[...] = ds

        @pl.when(i == ng - 1)
        def _():
            args = _hg_lead_args(b, q0_ref, k0_ref, o0_ref, p0_ref, eg0_ref)
            _, vjp = jax.vjp(hg_scan, *args, nw_ref[...], jnp.zeros(dst.shape, F32))
            dq, dk, dv, deg, do, dz, dnw, _ = vjp((jnp.zeros((b * HEADS, CHUNK, DH), F32), ds))
            dnw_ref[...] += dnw
            for ref, val in ((dq0_ref, dq), (dk0_ref, dk), (do0_ref, do), (dv0_ref, dv), (dz0_ref, dz), (deg0_ref, deg)):
                ref[...] = _sum_rows(val, b)

    lead_out = [const(a) for a in lead[0:3]] + [const(lead[0]), const(lead[0]), const(lead[3])]
    return dict(
        body=body, args=(q_in, k_out, o_intra, p, p, eg, lead[0], lead[1], lead[2], p0, lead[3], nw, ssave, dy),
        in_specs=[slab(0), slab(0), slab(0), slab(2), slab(3), per_chunk(1, WIDTH)] + [const(a) for a in lead[0:3]]
        + [const(p0), const(lead[3]), const(nw), per_chunk(WIDTH, DH), slab(0)],
        out_specs=[slab(0)] * 5 + [per_chunk(1, WIDTH)] + lead_out + [const(nw)],
        out_shape=[_sds((b, seq, WIDTH))] * 2 + [_sds((b, seq, WIDTH), MXU_DTYPE)] * 3 + [_sds(eg.shape)]
        + [_sds((CHUNK, WIDTH))] * 5 + [_sds((1, WIDTH)), _sds(nw.shape)],
        scratch_shapes=[pltpu.VMEM((b * HEADS, DH, DH), F32)])


def _hg_local_vjp(sum_mats, p, logits, dq, dk, do, degs, dv, dz):
    _, vjp = jax.vjp(lambda p_, logits_: hg_local(p_, logits_, sum_mats), p, logits)
    dp, dlg = vjp((dq, dk, do.astype(F32), degs))
    return dp + jnp.concatenate([jnp.zeros((p.shape[0], 2 * WIDTH), F32), dv.astype(F32), dz.astype(F32)], axis=1), dlg


def hg_local_bwd(p, p0, logits, cot, cot0):
    b, seq, _ = p.shape
    rows = LOCAL_CHUNKS * CHUNK

    def body(p_ref, p0_ref, lg_ref, s_ref, st_ref, dq_ref, dk_ref, do_ref, dv_ref, dz_ref, deg_ref, dq0_ref, dk0_ref, do0_ref,
             dv0_ref, dz0_ref, deg0_ref, dp_ref, dp0_ref, dlg_ref):
        sum_mats = (s_ref[...], st_ref[...])

        @pl.when((pl.program_id(0) == 0) & (pl.program_id(1) == 0))
        def _():
            dp0, dlg_ref[...] = _hg_local_vjp(sum_mats, p0_ref[...], lg_ref[...], dq0_ref[...], dk0_ref[...], do0_ref[...],
                                              (deg0_ref[...],), dv0_ref[...], dz0_ref[...])
            dp0_ref[...] = dp0.astype(MXU_DTYPE)

        degs = tuple(deg_ref[c] for c in range(LOCAL_CHUNKS))
        dp, dlg = _hg_local_vjp(sum_mats, p_ref[...], lg_ref[...], dq_ref[...], dk_ref[...], do_ref[...], degs, dv_ref[...],
                                dz_ref[...])
        dp_ref[...] = dp.astype(MXU_DTYPE)
        dlg_ref[...] += dlg

    slab = pl.BlockSpec((None, rows, WIDTH), lambda s, g: (s, g, 0))
    wide = pl.BlockSpec((None, rows, 4 * WIDTH), lambda s, g: (s, g, 0))
    const = lambda a: pl.BlockSpec(a.shape, lambda s, g: (0, 0))
    sum_mats = _summation_matrices(_hg_sums, (HG_LEVELS + 1) * CHUNK)
    return pl.pallas_call(
        body, grid=(b, seq // rows), name="hgrn2_local_bwd",
        in_specs=[wide, const(p0), const(logits), const(sum_mats[0]), const(sum_mats[1]), slab, slab, slab, slab, slab,
                  pl.BlockSpec((None, LOCAL_CHUNKS, 1, WIDTH), lambda s, g: (s, g, 0, 0))] + [const(a) for a in cot0],
        out_specs=[wide, const(p0), const(logits)],
        out_shape=[_sds(p.shape, MXU_DTYPE), _sds(p0.shape, MXU_DTYPE), _sds(logits.shape)],
        compiler_params=_cparams("arbitrary", "arbitrary"),
    )(p, p0, logits, *sum_mats, *cot, *cot0)


def _halo_block(g):
    return jnp.maximum((LOCAL_CHUNKS * CHUNK // HALO) * g - 1, 0)


def _gd_window(g, p_ref, halo_ref, p0_ref):
    halo = jnp.where(g == 0, p0_ref[CHUNK - HALO:CHUNK, 0:QKV], halo_ref[...])
    return jnp.concatenate([halo, p_ref[:, 0:QKV]], axis=0)


def _lead_window(p0_ref):
    return jnp.concatenate([jnp.zeros((HALO, QKV), F32), p0_ref[:, 0:QKV]], axis=0)


def gd_local_fwd(p, p0, ab, ab0, cw, alog, dtb):
    b, seq, _ = p.shape
    rows = LOCAL_CHUNKS * CHUNK
    nreal = seq // CHUNK

    def body(p_ref, halo_ref, p0_ref, ab_ref, ab0_ref, cw_ref, al_ref, dt_ref, s_ref, st_ref, u_ref, w_ref, qe_ref, ke_ref,
             qk_ref, ea_ref, inv_ref, u0_ref, w0_ref, qe0_ref, ke0_ref, qk0_ref, ea0_ref, inv0_ref):
        sum_mats = (s_ref[...], st_ref[...])

        @pl.when((pl.program_id(0) == 0) & (pl.program_id(1) == 0))
        def _():
            (u0_ref[...], ww, qe, ke, qk0_ref[...], (ea0_ref[...],)), inv0_ref[...] = gd_local(
                _lead_window(p0_ref), ab0_ref[...], cw_ref[...], al_ref[...], dt_ref[...], sum_mats, inverse=_tri_y_impl)
            w0_ref[...], qe0_ref[...], ke0_ref[...] = ww.astype(MXU_DTYPE), qe.astype(MXU_DTYPE), ke.astype(MXU_DTYPE)

        (uu, ww, qe, ke, qk, eas), inv = gd_local(_gd_window(pl.program_id(1), p_ref, halo_ref, p0_ref), ab_ref[...],
                                                  cw_ref[...], al_ref[...], dt_ref[...], sum_mats, inverse=_tri_y_impl)
        u_ref[...], w_ref[...], qe_ref[...], ke_ref[...] = uu, ww.astype(MXU_DTYPE), qe.astype(MXU_DTYPE), ke.astype(MXU_DTYPE)
        for c in range(LOCAL_CHUNKS):
            qk_ref[c] = _pack_heads(qk[c * HEADS * CHUNK:(c + 1) * HEADS * CHUNK])
            inv_ref[c] = _pack_heads(inv[c * HEADS * CHUNK:(c + 1) * HEADS * CHUNK])
            ea_ref[c] = eas[c]

    const = lambda shape: pl.BlockSpec(shape, lambda s, g: (0, 0))
    slab = pl.BlockSpec((None, rows, WIDTH), lambda s, g: (s, g, 0))
    mats = pl.BlockSpec((None, LOCAL_CHUNKS, 2 * CHUNK, 2 * CHUNK), lambda s, g: (s, g, 0, 0))
    lead_out = [_sds((CHUNK, WIDTH))] + [_sds((CHUNK, WIDTH), MXU_DTYPE)] * 3 + [_sds((HEADS * CHUNK, CHUNK)), _sds((1, AB_PAD)),
                                                                                _sds((HEADS * CHUNK, CHUNK))]
    sum_mats = _summation_matrices(_running_sum, CHUNK)
    out = pl.pallas_call(
        body, grid=(b, seq // rows), name="gdn_local",
        in_specs=[pl.BlockSpec((None, rows, 4 * WIDTH), lambda s, g: (s, g, 0)),
                  pl.BlockSpec((None, HALO, QKV), lambda s, g: (s, _halo_block(g), 0)), const(p0.shape),
                  pl.BlockSpec((None, rows, AB_PAD), lambda s, g: (s, g, 0)), const(ab0.shape), const(cw.shape),
                  const(alog.shape), const(dtb.shape), const(sum_mats[0].shape), const(sum_mats[1].shape)],
        out_specs=[slab] * 4 + [mats, pl.BlockSpec((None, LOCAL_CHUNKS, 1, AB_PAD), lambda s, g: (s, g, 0, 0)), mats]
        + [const(s.shape) for s in lead_out],
        out_shape=[_sds((b, seq, WIDTH))] + [_sds((b, seq, WIDTH), MXU_DTYPE)] * 3
        + [_sds((b, nreal, 2 * CHUNK, 2 * CHUNK)), _sds((b, nreal, 1, AB_PAD)), _sds((b, nreal, 2 * CHUNK, 2 * CHUNK))] + lead_out,
        compiler_params=_cparams("arbitrary", "arbitrary"),
    )(p, p, p0, ab, ab0, cw, alog, dtb, *sum_mats)
    return out[0:6], out[6], out[7:13], out[13]


def _gd_scan_args(b, k, u_ref, w_ref, qe_ref, ke_ref, qk_ref, ea_ref, z_ref):
    qk = jnp.stack([_packed_head(qk_ref, i, k, h) for i, h in _pairs(b)], axis=0)
    ea = jnp.stack([ea_ref[i, k, :, h:h + 1] for i, h in _pairs(b)], axis=0)
    return (_load_slabs(u_ref, b, k), _load_slabs(w_ref, b, k), _load_slabs(qe_ref, b, k), _load_slabs(ke_ref, b, k), qk, ea,
            _load_slabs(z_ref, b, k))


def _gd_lead_args(b, u0_ref, w0_ref, qe0_ref, ke0_ref, qk0_ref, ea0_ref, p0_ref):
    qk = jnp.stack([qk0_ref[h * CHUNK:(h + 1) * CHUNK, :] for _, h in _pairs(b)], axis=0)
    ea = jnp.stack([ea0_ref[:, h:h + 1] for _, h in _pairs(b)], axis=0)
    return (_lead_slabs(u0_ref[...], b), _lead_slabs(w0_ref[...], b), _lead_slabs(qe0_ref[...], b), _lead_slabs(ke0_ref[...], b),
            qk, ea, _lead_slabs(p0_ref[:, QKV:QKV + WIDTH], b))


def gd_scan_fwd(p, p0, local, lead, nw):
    b, seq, _ = p.shape
    slab, per_chunk, const = _scan_specs(b, seq // (SCAN_CHUNKS_FWD * CHUNK), False, SCAN_CHUNKS_FWD)

    def body(u_ref, w_ref, qe_ref, ke_ref, qk_ref, ea_ref, z_ref, u0_ref, w0_ref, qe0_ref, ke0_ref, qk0_ref, ea0_ref, p0_ref,
             nw_ref, y_ref, ss_ref, st):
        @pl.when(pl.program_id(0) == 0)
        def _():
            lead_args = _gd_lead_args(b, u0_ref, w0_ref, qe0_ref, ke0_ref, qk0_ref, ea0_ref, p0_ref)
            st[...] = gd_scan(*lead_args, nw_ref[...], jnp.zeros(st.shape, F32))[1]

        s = st[...]
        for k in range(SCAN_CHUNKS_FWD):
            _save_states(ss_ref, s, b, k)
            y, s = gd_scan(*_gd_scan_args(b, k, u_ref, w_ref, qe_ref, ke_ref, qk_ref, ea_ref, z_ref), nw_ref[...], s)
            _store_slabs(y_ref, y, b, k)
        st[...] = s

    return dict(
        body=body, args=(*local, p, *lead, p0, nw),
        in_specs=[slab(0)] * 4 + [per_chunk(2 * CHUNK, 2 * CHUNK), per_chunk(1, AB_PAD), slab(3)] + [const(a) for a in lead]
        + [const(p0), const(nw)],
        out_specs=[slab(0), per_chunk(WIDTH, DH)],
        out_shape=[_sds((b, seq, WIDTH), MXU_DTYPE), _sds((b, seq // CHUNK, WIDTH, DH), MXU_DTYPE)],
        scratch_shapes=[pltpu.VMEM((b * HEADS, DH, DH), F32)])


def gd_scan_bwd(p, p0, local, lead, nw, ssave, dy):
    b, seq, _ = p.shape
    ng = seq // (SCAN_CHUNKS * CHUNK)
    slab, per_chunk, const = _scan_specs(b, ng, True, SCAN_CHUNKS)

    def body(u_ref, w_ref, qe_ref, ke_ref, qk_ref, ea_ref, z_ref, u0_ref, w0_ref, qe0_ref, ke0_ref, qk0_ref, ea0_ref, p0_ref,
             nw_ref, ss_ref, dy_ref, du_ref, dw_ref, dqe_ref, dke_ref, dqk_ref, dea_ref, dz_ref, du0_ref, dw0_ref, dqe0_ref,
             dke0_ref, dqk0_ref, dea0_ref, dz0_ref, dnw_ref, dst):
        i = pl.program_id(0)
        lane = lax.broadcasted_iota(jnp.int32, (1, AB_PAD), 1)

        def gate_rows(dea, j):
            return sum(jnp.where(lane == h, dea[j * HEADS + h], 0.0) for h in range(HEADS))

        def matrix_rows(dqk, j):
            return jnp.concatenate([dqk[j * HEADS + h] for h in range(HEADS)], axis=0)

        @pl.when(i == 0)
        def _():
            dst[...] = jnp.zeros_like(dst)
            dnw_ref[...] = jnp.zeros_like(dnw_ref)

        ds = dst[...]
        for k in reversed(range(SCAN_CHUNKS)):
            args = _gd_scan_args(b, k, u_ref, w_ref, qe_ref, ke_ref, qk_ref, ea_ref, z_ref)
            _, vjp = jax.vjp(gd_scan, *args, nw_ref[...], _load_states(ss_ref, b, k))
            du, dw, dqe, dke, dqk, dea, dz, dnw, ds = vjp((_load_slabs(dy_ref, b, k), ds))
            dnw_ref[...] += dnw
            for ref, val in ((du_ref, du), (dw_ref, dw), (dqe_ref, dqe), (dke_ref, dke), (dz_ref, dz)):
                _store_slabs(ref, val, b, k)
            for j in range(b):
                dqk_ref[j, k] = _pack_heads(matrix_rows(dqk, j))
                dea_ref[j, k] = gate_rows(dea, j)
        dst[...] = ds

        @pl.when(i == ng - 1)
        def _():
            args = _gd_lead_args(b, u0_ref, w0_ref, qe0_ref, ke0_ref, qk0_ref, ea0_ref, p0_ref)
            _, vjp = jax.vjp(gd_scan, *args, nw_ref[...], jnp.zeros(dst.shape, F32))
            du, dw, dqe, dke, dqk, dea, dz, dnw, _ = vjp((jnp.zeros((b * HEADS, CHUNK, DH), F32), ds))
            dnw_ref[...] += dnw
            for ref, val in ((du0_ref, du), (dw0_ref, dw), (dqe0_ref, dqe), (dke0_ref, dke), (dz0_ref, dz)):
                ref[...] = _sum_rows(val, b)
            dqk0_ref[...] = sum((matrix_rows(dqk, j) for j in range(1, b)), matrix_rows(dqk, 0))
            dea0_ref[...] = sum((gate_rows(dea, j) for j in range(1, b)), gate_rows(dea, 0))

    uu, ww, qe, ke, qk, ea = local
    return dict(
        body=body, args=(*local, p, *lead, p0, nw, ssave, dy),
        in_specs=[slab(0)] * 4 + [per_chunk(2 * CHUNK, 2 * CHUNK), per_chunk(1, AB_PAD), slab(3)] + [const(a) for a in lead]
        + [const(p0), const(nw), per_chunk(WIDTH, DH), slab(0)],
        out_specs=[slab(0)] * 4 + [per_chunk(2 * CHUNK, 2 * CHUNK), per_chunk(1, AB_PAD), slab(0)] + [const(a) for a in lead]
        + [const(lead[0]), const(nw)],
        out_shape=[_sds((b, seq, WIDTH))] * 4 + [_sds(qk.shape), _sds(ea.shape), _sds((b, seq, WIDTH), MXU_DTYPE)]
        + [_sds(a.shape) for a in lead] + [_sds(lead[0].shape), _sds(nw.shape)],
        scratch_shapes=[pltpu.VMEM((b * HEADS, DH, DH), F32)])


def _gd_local_vjp(sum_mats, inv_rows, xx, ab, cw, alog, dtb):
    nb = ab.shape[0] // CHUNK
    inv = jnp.stack([inv_rows[g * CHUNK:(g + 1) * CHUNK] for g in range(nb * HEADS)], axis=0)
    _, vjp, _ = jax.vjp(lambda *a: gd_local(*a, sum_mats, inverse=_saved_inverse(inv)), xx, ab, cw, alog, dtb, has_aux=True)
    return vjp


def gd_local_bwd(p, p0, ab, ab0, cw, alog, dtb, inv, inv0, cot, dz, cot0, dz0):
    b, seq, _ = p.shape
    rows = LOCAL_CHUNKS * CHUNK
    ng = seq // rows
    du, dw, dqe, dke, dqk, dea = cot

    def body(p_ref, halo_ref, p0_ref, ab_ref, ab0_ref, cw_ref, al_ref, dt_ref, s_ref, st_ref, inv_ref, inv0_ref, du_ref, dw_ref,
             dqe_ref, dke_ref, dqk_ref, dea_ref, dz_ref, du0_ref, dw0_ref, dqe0_ref, dke0_ref, dqk0_ref, dea0_ref, dz0_ref,
             dp_ref, dab_ref, dp0_ref, dab0_ref, dcw_ref, dal_ref, ddt_ref, dhalo, dtail):
        s, i = pl.program_id(0), pl.program_id(1)
        g = ng - 1 - i
        sum_mats = (s_ref[...], st_ref[...])

        @pl.when(i == 0)
        def _():
            dhalo[...] = jnp.zeros_like(dhalo)

        @pl.when((s == 0) & (i == 0))
        def _():
            dtail[...] = jnp.zeros_like(dtail)
            dcw_ref[...] = jnp.zeros_like(dcw_ref)
            dal_ref[...] = jnp.zeros_like(dal_ref)
            ddt_ref[...] = jnp.zeros_like(ddt_ref)

        def finish(dxx, dab, dcw, dal, ddt, before, n, dz_val, dp_out, dab_out):
            dqkv = dxx[HALO:HALO + n] + jnp.concatenate([jnp.zeros((n - HALO, QKV), F32), before], axis=0)
            dp_out[...] = jnp.concatenate([dqkv.astype(MXU_DTYPE), dz_val.astype(MXU_DTYPE)], axis=1)
            dab_out[...] = dab.astype(MXU_DTYPE)
            dcw_ref[...] += dcw
            dal_ref[...] += dal
            ddt_ref[...] += ddt

        inv_rows = jnp.concatenate([_unpack_heads(inv_ref[c]) for c in range(LOCAL_CHUNKS)], axis=0)
        vjp = _gd_local_vjp(sum_mats, inv_rows, _gd_window(g, p_ref, halo_ref, p0_ref), ab_ref[...], cw_ref[...], al_ref[...],
                            dt_ref[...])
        dqk_all = jnp.concatenate([_unpack_heads(dqk_ref[c]) for c in range(LOCAL_CHUNKS)], axis=0)
        deas = tuple(dea_ref[c] for c in range(LOCAL_CHUNKS))
        grads = vjp((du_ref[...], dw_ref[...], dqe_ref[...], dke_ref[...], dqk_all, deas))
        finish(*grads, dhalo[...], rows, dz_ref[...], dp_ref, dab_ref)
        dhalo[...] = grads[0][0:HALO]

        @pl.when(g == 0)
        def _():
            dtail[...] += grads[0][0:HALO]

        @pl.when((s == b - 1) & (g == 0))
        def _():
            vjp0 = _gd_local_vjp(sum_mats, inv0_ref[...], _lead_window(p0_ref), ab0_ref[...], cw_ref[...], al_ref[...],
                                 dt_ref[...])
            grads0 = vjp0((du0_ref[...], dw0_ref[...], dqe0_ref[...], dke0_ref[...], dqk0_ref[...], (dea0_ref[...],)))
            finish(*grads0, dtail[...], CHUNK, dz0_ref[...], dp0_ref, dab0_ref)

    rg = lambda i: ng - 1 - i
    const = lambda a: pl.BlockSpec(a.shape, lambda s, i: (0, 0))
    slab = pl.BlockSpec((None, rows, WIDTH), lambda s, i: (s, rg(i), 0))
    wide = pl.BlockSpec((None, rows, 4 * WIDTH), lambda s, i: (s, rg(i), 0))
    gates = pl.BlockSpec((None, rows, AB_PAD), lambda s, i: (s, rg(i), 0))
    mats = pl.BlockSpec((None, LOCAL_CHUNKS, 2 * CHUNK, 2 * CHUNK), lambda s, i: (s, rg(i), 0, 0))
    sum_mats = _summation_matrices(_running_sum, CHUNK)
    return pl.pallas_call(
        body, grid=(b, ng), name="gdn_local_bwd",
        in_specs=[wide, pl.BlockSpec((None, HALO, QKV), lambda s, i: (s, _halo_block(rg(i)), 0)), const(p0), gates, const(ab0),
                  const(cw), const(alog), const(dtb), const(sum_mats[0]), const(sum_mats[1]), mats, const(inv0), slab, slab,
                  slab, slab, mats,
                  pl.BlockSpec((None, LOCAL_CHUNKS, 1, AB_PAD), lambda s, i: (s, rg(i), 0, 0)), slab]
        + [const(a) for a in cot0] + [const(dz0)],
        out_specs=[wide, gates, const(p0), const(ab0), const(cw), const(alog), const(dtb)],
        out_shape=[_sds(p.shape, MXU_DTYPE), _sds(ab.shape, MXU_DTYPE), _sds(p0.shape, MXU_DTYPE), _sds(ab0.shape, MXU_DTYPE),
                   _sds(cw.shape), _sds(alog.shape), _sds(dtb.shape)],
        scratch_shapes=[pltpu.VMEM((HALO, QKV), F32), pltpu.VMEM((HALO, QKV), F32)],
        compiler_params=_cparams("arbitrary", "arbitrary"),
    )(p, p, p0, ab, ab0, cw, alog, dtb, *sum_mats, inv, inv0, du, dw, dqe, dke, dqk, dea, dz, *cot0, dz0)


def _position():
    return lax.axis_index("x"), lax.axis_index("y"), lax.axis_index("c")


EXCHANGE_COPIES = 10


def _exchange_blocks(bufs, send_sems, recv_sems):
    x, y, c = _position()
    here, x_nbr, y_nbr, diag = (x, y), (1 - x, y), (x, 1 - y), (1 - x, 1 - y)
    sibling = (x, y, 1 - c)
    me = (x, y, c)
    n = range(len(bufs))

    def rows(a, chip, core, half=None):
        block = bufs[a].at[4 * chip[0] + 2 * chip[1] + core]
        if half is None:
            return block
        total = bufs[a].shape[1]
        tile = 8 * (4 // jnp.dtype(bufs[a].dtype).itemsize)
        split = total // 2 // tile * tile
        return block.at[pl.ds(0, split)] if half == 0 else block.at[pl.ds(split, total - split)]

    def copy(a, k, region, to):
        return pltpu.make_async_remote_copy(src_ref=region, dst_ref=region, send_sem=send_sems.at[a * EXCHANGE_COPIES + k],
                                            recv_sem=recv_sems.at[a * EXCHANGE_COPIES + k], device_id=to, device_id_type=MESH)

    sent = [copy(a, 0, rows(a, here, c), sibling) for a in n]
    sent += [cp for a in n for cp in (copy(a, 1, rows(a, here, c, 0), (*x_nbr, c)), copy(a, 4, rows(a, here, c, 1), (*y_nbr, c)))]
    sent += [cp for a in n for cp in (copy(a, 2, rows(a, here, c, 1), (*x_nbr, c)), copy(a, 3, rows(a, here, c, 0), (*y_nbr, c)))]
    for cp in sent:
        cp.start()

    def after(arrivals, a, k, region, to):
        for cp in arrivals:
            cp.wait_recv()
        sent.append(copy(a, k, region, to))
        sent[-1].start()

    for a in n:
        after([copy(a, 1, rows(a, x_nbr, c, 0), me)], a, 5, rows(a, x_nbr, c, 0), (*y_nbr, c))
        after([copy(a, 4, rows(a, y_nbr, c, 1), me)], a, 6, rows(a, y_nbr, c, 1), (*x_nbr, c))
    for a in n:
        after([copy(a, 2, rows(a, x_nbr, c, 1), me)], a, 7, rows(a, x_nbr, c), sibling)
        after([copy(a, 3, rows(a, y_nbr, c, 0), me)], a, 8, rows(a, y_nbr, c), sibling)
    for a in n:
        after([copy(a, 5, rows(a, diag, c, 0), me), copy(a, 6, rows(a, diag, c, 1), me)], a, 9, rows(a, diag, c), sibling)
    for a in n:
        copy(a, 0, rows(a, here, 1 - c), me).wait_recv()
        for k, chip in ((7, x_nbr), (8, y_nbr), (9, diag)):
            copy(a, k, rows(a, chip, 1 - c), me).wait_recv()
    for cp in sent:
        cp.wait_send()


def _exchange_sems(n_bufs):
    return [pltpu.SemaphoreType.DMA((n_bufs * EXCHANGE_COPIES,)), pltpu.SemaphoreType.DMA((n_bufs * EXCHANGE_COPIES,))]


def gather_weights(w_in_t, w_out, small, pad_rows):
    rows, _, cols = w_in_t.shape
    buf_rows = -(-rows // ROW_TILE_BF16) * ROW_TILE_BF16

    def body(wi_ref, wo_ref, sm_ref, wi_out, wo_out, sm_out, wi_buf, send_sems, recv_sems):
        x, y, c = _position()
        me = 4 * x + 2 * y + c
        wi_buf[me, pl.ds(0, rows), :] = wi_ref[:, 0, :].astype(MXU_DTYPE)
        wi_buf[me, pl.ds(rows, buf_rows - rows), :] = jnp.zeros((buf_rows - rows, cols), MXU_DTYPE)
        wo_out[me] = wo_ref[...].astype(MXU_DTYPE)
        sm_out[me] = sm_ref[...]
        _exchange_blocks([wi_buf, wo_out, sm_out], send_sems, recv_sems)
        for d in range(N_DEV):
            wi_out[pl.ds(d * rows, rows), :] = wi_buf[d, pl.ds(0, rows), :]
        wi_out[pl.ds(N_DEV * rows, pad_rows), :] = jnp.zeros((pad_rows, cols), MXU_DTYPE)

    return pl.pallas_call(
        body, name="gather_weights", in_specs=[VMEM_SPEC] * 3, out_specs=[VMEM_SPEC] * 3,
        out_shape=[jax.ShapeDtypeStruct((N_DEV * rows + pad_rows, cols), MXU_DTYPE),
                   jax.ShapeDtypeStruct((N_DEV,) + w_out.shape, MXU_DTYPE), jax.ShapeDtypeStruct((N_DEV,) + small.shape, F32)],
        scratch_shapes=[pltpu.VMEM((N_DEV, buf_rows, cols), MXU_DTYPE)] + _exchange_sems(3),
        compiler_params=pltpu.CompilerParams(vmem_limit_bytes=VMEM_LIMIT))(w_in_t, w_out, small)


HOPS = 6


def reduce_gradients(tensors, small, name):
    n_t = len(tensors)
    arrays = [a for parts, _ in tensors for a, _ in parts]
    first_array = [sum(len(parts) for parts, _ in tensors[:t]) for t in range(n_t)]

    def pieces(t, j):
        parts, block_rows = tensors[t]
        out, base = [], 0
        for pi, (_, valid) in enumerate(parts):
            lo, hi = max(j * block_rows, base), min((j + 1) * block_rows, base + valid)
            if lo < hi:
                out.append((first_array[t] + pi, lo - base, lo - j * block_rows, hi - lo))
            base += valid
        return out

    def body(*refs):
        n_a = len(arrays)
        in_refs, small_ref = refs[:n_a], refs[n_a]
        out_refs, small_sum = refs[n_a + 1:n_a + 1 + n_t], refs[n_a + 1 + n_t]
        bufs, small_buf = refs[n_a + 2 + n_t:n_a + 2 + 5 * n_t], refs[n_a + 2 + 5 * n_t]
        s1_sems, r1_sems, s2_sems, r2_sems, small_send, small_recv = refs[n_a + 3 + 5 * n_t:]
        x, y, c = _position()
        chip = 2 * x + y

        def put(t, dst, j, add=None):
            for ai, src_row, dst_row, size in pieces(t, j):
                v = in_refs[ai][pl.ds(src_row, size), :]
                if add is not None:
                    v = v + add[pl.ds(dst_row, size), :].astype(F32)
                dst[pl.ds(dst_row, size), :] = v.astype(dst.dtype)

        def swap(t, k):
            send1, recv1 = bufs[4 * t], bufs[4 * t + 1]
            return pltpu.make_async_remote_copy(src_ref=send1.at[k], dst_ref=recv1.at[k], send_sem=s1_sems.at[4 * t + k],
                                                recv_sem=r1_sems.at[4 * t + k], device_id=(x, y, 1 - c), device_id_type=MESH)

        to_x, to_y, to_diag = 2 * (1 - x) + y, 2 * x + (1 - y), 2 * (1 - x) + (1 - y)
        x_dev, y_dev = (1 - x, y, c), (x, 1 - y, c)

        def half(ref, h):
            total = ref.shape[0]
            split = total // 2 // ROW_TILE_BF16 * ROW_TILE_BF16
            return ref.at[pl.ds(0, split)] if h == 0 else ref.at[pl.ds(split, total - split)]

        def hop(t, copy_id, src, dst, to):
            return pltpu.make_async_remote_copy(src_ref=src, dst_ref=dst, send_sem=s2_sems.at[HOPS * t + copy_id],
                                                recv_sem=r2_sems.at[HOPS * t + copy_id], device_id=to, device_id_type=MESH)

        def hops(t):
            send2, landing = bufs[4 * t + 2], bufs[4 * t + 3]
            return [hop(t, 0, half(send2.at[to_diag], 0), half(landing.at[0], 0), x_dev),
                    hop(t, 1, half(send2.at[to_diag], 1), half(landing.at[0], 1), y_dev),
                    hop(t, 2, half(send2.at[to_x], 0), half(landing.at[1], 0), x_dev),
                    hop(t, 3, half(send2.at[to_y], 1), half(landing.at[2], 1), y_dev),
                    hop(t, 4, half(send2.at[to_x], 1), half(landing.at[1], 1), x_dev),
                    hop(t, 5, half(send2.at[to_y], 0), half(landing.at[2], 0), y_dev)]

        def add_relay(t, slot, h):
            dst, src = half(bufs[4 * t + 2].at[slot], h), half(bufs[4 * t + 3].at[0], h)
            dst[...] = (dst[...].astype(F32) + src[...].astype(F32)).astype(dst.dtype)

        for t in range(n_t):
            send2 = bufs[4 * t + 2]
            pad = send2.shape[1] - tensors[t][1]
            if pad:
                send2[:, pl.ds(tensors[t][1], pad), :] = jnp.zeros((4, pad, send2.shape[2]), send2.dtype)
            for j in range(N_DEV):
                @pl.when((j & 1) != c)
                def _():
                    put(t, bufs[4 * t].at[j >> 1], j)
            for k in range(4):
                swap(t, k).start()

        small_buf[4 * x + 2 * y + c] = small_ref[...]
        _exchange_blocks([small_buf], small_send, small_recv)
        total = small_buf[0]
        for d in range(1, N_DEV):
            total = total + small_buf[d]
        small_sum[...] = total

        for t in range(n_t):
            recv1 = bufs[4 * t + 1]
            for k in range(4):
                swap(t, k).wait_recv()
                for j in (2 * k, 2 * k + 1):
                    @pl.when(((j & 1) == c) & (k != chip))
                    def _():
                        put(t, bufs[4 * t + 2].at[k], j, add=recv1.at[k])

                    @pl.when(((j & 1) == c) & (k == chip))
                    def _():
                        put(t, out_refs[t], j, add=recv1.at[k])
            for cp in hops(t)[0:4]:
                cp.start()

        for t in range(n_t):
            cps = hops(t)
            cps[0].wait_recv()
            add_relay(t, to_y, 0)
            cps[5].start()
            cps[1].wait_recv()
            add_relay(t, to_x, 1)
            cps[4].start()

        for t in range(n_t):
            cps, rows = hops(t), tensors[t][1]
            for first, second, slot in ((cps[2], cps[4], 1), (cps[3], cps[5], 2)):
                first.wait_recv()
                second.wait_recv()
                out_refs[t][...] += bufs[4 * t + 3][slot, pl.ds(0, rows), :].astype(F32)

        for t in range(n_t):
            for cp in hops(t):
                cp.wait_send()
            for k in range(4):
                swap(t, k).wait_send()

    scratch, out_shape = [], []
    for parts, block_rows in tensors:
        cols = parts[0][0].shape[1]
        tiled_rows = -(-block_rows // ROW_TILE_BF16) * ROW_TILE_BF16
        scratch += [pltpu.VMEM((4, block_rows, cols), MXU_DTYPE)] * 2
        scratch += [pltpu.VMEM((4, tiled_rows, cols), MXU_DTYPE), pltpu.VMEM((3, tiled_rows, cols), MXU_DTYPE)]
        out_shape.append(jax.ShapeDtypeStruct((block_rows, cols), F32))
    out_shape.append(jax.ShapeDtypeStruct(small.shape, F32))
    scratch += [pltpu.VMEM((N_DEV,) + small.shape, F32)] + [pltpu.SemaphoreType.DMA((4 * n_t,))] * 2
    scratch += [pltpu.SemaphoreType.DMA((HOPS * n_t,))] * 2 + _exchange_sems(1)
    return pl.pallas_call(
        body, name=name, in_specs=[VMEM_SPEC] * (len(arrays) + 1), out_specs=[VMEM_SPEC] * (n_t + 1), out_shape=out_shape,
        scratch_shapes=scratch, compiler_params=pltpu.CompilerParams(vmem_limit_bytes=VMEM_LIMIT),
    )(*arrays, small)


def _adamw_step(w, g, m, v):
    mn = ADAM_B1 * m + (1.0 - ADAM_B1) * g
    vn = ADAM_B2 * v + (1.0 - ADAM_B2) * jnp.square(g)
    m_hat = mn / (1.0 - ADAM_B1 ** ADAM_STEP)
    v_hat = vn / (1.0 - ADAM_B2 ** ADAM_STEP)
    return -ADAM_LR * (m_hat / (jnp.sqrt(v_hat) + ADAM_EPS) + ADAM_WD * w), mn, vn


def adamw_small(packed, first_rows, ws, gs, ms, vs):
    k = len(ws)
    given = [g for g in gs if g is not None]

    def body(*refs):
        packed_ref, w_refs, m_refs, v_refs = refs[0], refs[1:1 + k], refs[1 + k:1 + 2 * k], refs[1 + 2 * k:1 + 3 * k]
        g_refs, outs = iter(refs[1 + 3 * k:1 + 3 * k + len(given)]), refs[1 + 3 * k + len(given):]
        for i in range(k):
            rows, cols = w_refs[i].shape
            g = next(g_refs)[...] if gs[i] is not None else packed_ref[first_rows[i]:first_rows[i] + rows, 0:cols]
            outs[4 * i][...] = g
            outs[4 * i + 1][...], outs[4 * i + 2][...], outs[4 * i + 3][...] = _adamw_step(w_refs[i][...], g, m_refs[i][...],
                                                                                          v_refs[i][...])

    n_in = 1 + 3 * k + len(given)
    out = pl.pallas_call(body, name="adamw_small", in_specs=[VMEM_SPEC] * n_in, out_specs=[VMEM_SPEC] * (4 * k),
                         out_shape=[jax.ShapeDtypeStruct(w.shape, F32) for w in ws for _ in range(4)],
                         compiler_params=pltpu.CompilerParams(vmem_limit_bytes=VMEM_LIMIT))(packed, *ws, *ms, *vs, *given)
    return [out[4 * i:4 * i + 4] for i in range(k)]


def adamw_w_in(w, g_t, m, v):
    def body(w_ref, g_ref, m_ref, v_ref, go_ref, d_ref, nm_ref, nv_ref):
        g = g_ref[...]
        go_ref[:, 0, :] = g
        d_ref[:, 0, :], nm_ref[:, 0, :], nv_ref[:, 0, :] = _adamw_step(w_ref[:, 0, :], g, m_ref[:, 0, :], v_ref[:, 0, :])

    return pl.pallas_call(body, name="adamw_w_in", in_specs=[VMEM_SPEC] * 4, out_specs=[VMEM_SPEC] * 4,
                          out_shape=[jax.ShapeDtypeStruct(w.shape, F32)] * 4,
                          compiler_params=pltpu.CompilerParams(vmem_limit_bytes=VMEM_LIMIT))(w, g_t, m, v)


def _pad_rows(a, rows=8):
    return jnp.pad(a, ((0, rows - a.shape[0]), (0, 0)))


def _pad_lanes(a, lanes=128):
    return jnp.pad(a, ((0, 0), (0, lanes - a.shape[1])))


def kernel(x, meta_tokens, norm_w, w_in, conv_w, hg_lb_logits, hg_norm_w, gdn_A_log, gdn_dt_bias, gdn_norm_w, w_out, final_norm_w, loss_target, m_meta_tokens, m_norm_w, m_w_in, m_conv_w, m_hg_lb_logits, m_hg_norm_w, m_gdn_A_log, m_gdn_dt_bias, m_gdn_norm_w, m_w_out, m_final_norm_w, v_meta_tokens, v_norm_w, v_w_in, v_conv_w, v_hg_lb_logits, v_hg_norm_w, v_gdn_A_log, v_gdn_dt_bias, v_gdn_norm_w, v_w_out, v_final_norm_w):
    b, seq, _ = x.shape
    n = b * seq
    dev = 4 * lax.axis_index("x") + 2 * lax.axis_index("y") + lax.axis_index("c")
    col_shard = IN_COLS // N_DEV

    small_w = jnp.concatenate([_pad_lanes(meta_tokens, 256), _pad_rows(_pad_lanes(conv_w[0], 256))], axis=0)
    w_t, w_out_g, small_g = gather_weights(jnp.transpose(w_in, (2, 0, 1)), w_out[0], small_w, AB_PAD - 2 * HEADS)
    meta_g = small_g[:, 0:N_META, 0:D_MODEL // N_DEV]
    conv_g = small_g[:, N_META:N_META + CONV_TAPS, 0:QKV // N_DEV]
    w_out_full = w_out_g.reshape(2 * WIDTH, D_MODEL)
    cw = jnp.transpose(conv_g, (1, 0, 2)).reshape(CONV_TAPS, QKV)
    meta = jnp.transpose(meta_g, (1, 0, 2)).reshape(N_META, D_MODEL)
    alog = _pad_lanes(gdn_A_log)
    dtb = _pad_lanes(gdn_dt_bias)
    fw = final_norm_w.reshape(1, D_MODEL)

    h0 = jnp.concatenate([jnp.zeros((CHUNK - N_META, D_MODEL), F32), meta], axis=0)
    x2 = x.reshape(n, D_MODEL)
    phg, pgd, pab, phg0, pgd0, pab0, u0 = in_proj(x2, h0, norm_w, w_t)
    phg3, pgd3, pab3 = phg.reshape(b, seq, 4 * WIDTH), pgd.reshape(b, seq, 4 * WIDTH), pab.reshape(b, seq, AB_PAD)
    hg_loc, hg_lead = hg_local_fwd(phg3, phg0, hg_lb_logits)
    gd_loc, gd_inv, gd_lead, gd_inv0 = gd_local_fwd(pgd3, pgd0, pab3, pab0, cw, alog, dtb)
    (y_hg, s_hg), (y_gd, s_gd) = run_scans([hg_scan_fwd(phg3, phg0, hg_loc, hg_lead, hg_norm_w),
                                            gd_scan_fwd(pgd3, pgd0, gd_loc, gd_lead, gdn_norm_w)],
                                           seq // (SCAN_CHUNKS_FWD * CHUNK), "scans")

    dh2, dy_hg, dy_gd, g_w_out, loss_part, g_fw = out_proj_loss(
        x2, loss_target.reshape(n, D_MODEL), y_hg.reshape(n, WIDTH), y_gd.reshape(n, WIDTH), w_out_full, fw)

    hb, gb = run_scans([hg_scan_bwd(phg3, phg0, hg_loc, hg_lead, hg_norm_w, s_hg, dy_hg.reshape(b, seq, WIDTH)),
                        gd_scan_bwd(pgd3, pgd0, gd_loc, gd_lead, gdn_norm_w, s_gd, dy_gd.reshape(b, seq, WIDTH))],
                       seq // (SCAN_CHUNKS * CHUNK), "scans_bwd")
    dphg, dphg0, g_lb = hg_local_bwd(phg3, phg0, hg_lb_logits, hb[0:6], hb[6:12])
    g_hg_nw = hb[12]
    dpgd, dpab, dpgd0, dpab0, g_cw, g_alog, g_dtb = gd_local_bwd(pgd3, pgd0, pab3, pab0, cw, alog, dtb, gd_inv, gd_inv0,
                                                                 gb[0:6], gb[6], gb[7:13], gb[13])
    g_gd_nw = gb[14]
    dphg, dpgd, dpab = dphg.reshape(n, 4 * WIDTH), dpgd.reshape(n, 4 * WIDTH), dpab.reshape(n, AB_PAD)

    grad_x, dh0, g_nw, g_w_hg, g_w_gd, g_w_ab = in_proj_bwd(dphg, dpgd, dpab, w_t, x2, dh2, norm_w, h0, u0, dphg0, dpgd0, dpab0)

    small = jnp.concatenate([
        g_nw.reshape(8, 128), g_lb.reshape(8, 128), _pad_rows(g_hg_nw), _pad_rows(g_alog), _pad_rows(g_dtb), _pad_rows(g_gd_nw),
        g_fw.reshape(8, 128), g_cw.reshape(48, 128),
        dh0[CHUNK - N_META:CHUNK].reshape(128, 128), loss_part], axis=0)
    g_w_in_t, g_w_out, small = reduce_gradients(
        [([(g_w_hg, 4 * WIDTH), (g_w_gd, 4 * WIDTH), (g_w_ab, 2 * HEADS)], col_shard),
         ([(g_w_out, 2 * WIDTH)], (2 * WIDTH) // N_DEV)], small, "reduce_gradients")
    g_cw_full = small[56:104].reshape(CONV_TAPS, QKV)
    g_meta_full = small[104:232].reshape(N_META, D_MODEL)
    loss = small[232, 0]
    g_conv = lax.dynamic_slice_in_dim(g_cw_full, dev * (QKV // N_DEV), QKV // N_DEV, axis=1)
    g_meta = lax.dynamic_slice_in_dim(g_meta_full, dev * (D_MODEL // N_DEV), D_MODEL // N_DEV, axis=1)

    names = ["meta_tokens", "norm_w", "w_in", "conv_w", "hg_lb_logits", "hg_norm_w", "gdn_A_log", "gdn_dt_bias",
             "gdn_norm_w", "w_out", "final_norm_w"]
    weights = [meta_tokens, norm_w, w_in, conv_w, hg_lb_logits, hg_norm_w, gdn_A_log, gdn_dt_bias, gdn_norm_w, w_out,
               final_norm_w]
    moms = [m_meta_tokens, m_norm_w, m_w_in, m_conv_w, m_hg_lb_logits, m_hg_norm_w, m_gdn_A_log, m_gdn_dt_bias,
            m_gdn_norm_w, m_w_out, m_final_norm_w]
    vars_ = [v_meta_tokens, v_norm_w, v_w_in, v_conv_w, v_hg_lb_logits, v_hg_norm_w, v_gdn_A_log, v_gdn_dt_bias,
             v_gdn_norm_w, v_w_out, v_final_norm_w]
    gradient = [g_meta, 0, None, g_conv, 8, 16, 24, 32, 40, g_w_out, 48]
    shape2d = [g_meta.shape, (8, 128), None, g_conv.shape, (8, 128), (1, DH), (1, HEADS), (1, HEADS), (1, DH), g_w_out.shape,
               (8, 128)]
    i_w_in = names.index("w_in")
    others = [i for i in range(len(names)) if i != i_w_in]
    in_rows = lambda i: isinstance(gradient[i], int)
    stepped = adamw_small(small, [gradient[i] if in_rows(i) else None for i in others],
                          [weights[i].reshape(shape2d[i]) for i in others], [None if in_rows(i) else gradient[i] for i in others],
                          [moms[i].reshape(shape2d[i]) for i in others], [vars_[i].reshape(shape2d[i]) for i in others])
    results = {i: [a.reshape(weights[i].shape) for a in stepped[j]] for j, i in enumerate(others)}
    to3, back = (lambda a: jnp.transpose(a, (2, 0, 1))), (lambda a: jnp.transpose(a, (1, 2, 0)))
    results[i_w_in] = [back(a) for a in adamw_w_in(to3(w_in), g_w_in_t, to3(m_w_in), to3(v_w_in))]
    grads, deltas, new_ms, new_vs = zip(*(results[i] for i in range(len(names))))
    return (loss, grad_x.reshape(x.shape), *grads, *deltas, *new_ms, *new_vs)
```

```python
import jax
import jax.numpy as jnp
import numpy as np
from jax import lax
from jax.experimental import pallas as pl
from jax.experimental.pallas import tpu as pltpu

F32 = jnp.float32
BF16 = jnp.bfloat16
MXU_DTYPE = BF16

D_MODEL = 1024
N_META = 16
CHUNK = 64
SUB = 16
ROW_TILE_BF16 = 16
HEADS = 4
DH = 128
WIDTH = HEADS * DH
QKV = 3 * WIDTH
CONV_TAPS = 4
HALO = 8
EPS = 1e-6
IN_COLS = 4 * WIDTH + 4 * WIDTH + 2 * HEADS
AB_PAD = 128
N_DEV = 8
LOCAL_CHUNKS = 4
SCAN_CHUNKS_FWD = 4
SCAN_CHUNKS = 2
VMEM_LIMIT = 56 * 1024 * 1024
VMEM_LIMIT_LARGE = 60 * 1024 * 1024

ADAM_LR = 0.001
ADAM_B1 = 0.9
ADAM_B2 = 0.999
ADAM_EPS = 1e-08
ADAM_WD = 0.01
ADAM_STEP = 10

VMEM_SPEC = pl.BlockSpec(memory_space=pltpu.VMEM)
MESH = pl.DeviceIdType.MESH


def _mm_tn(a, b):
    return lax.dot_general(a.astype(MXU_DTYPE), b.astype(MXU_DTYPE), (((0,), (0,)), ((), ())), preferred_element_type=F32)


def _nn(a, b):
    return lax.dot_general(a.astype(MXU_DTYPE), b.astype(MXU_DTYPE), (((2,), (1,)), ((0,), (0,))), preferred_element_type=F32)


def _nt(a, b):
    return lax.dot_general(a.astype(MXU_DTYPE), b.astype(MXU_DTYPE), (((2,), (2,)), ((0,), (0,))), preferred_element_type=F32)


def _t(a):
    return jnp.swapaxes(a, 1, 2)


@jax.custom_vjp
def _bmm(a, b):
    return _nn(a, b)


_bmm.defvjp(lambda a, b: (_nn(a, b), (a, b)), lambda saved, d: (_nt(d, saved[1]), _nn(_t(saved[0]), d)))


@jax.custom_vjp
def _bmm_nt(a, b):
    return _nt(a, b)


_bmm_nt.defvjp(lambda a, b: (_nt(a, b), (a, b)), lambda saved, d: (_nn(d, saved[1]), _nn(_t(d), saved[0])))


@jax.custom_vjp
def _bmm_tn(a, b):
    return _nn(_t(a), b)


_bmm_tn.defvjp(lambda a, b: (_nn(_t(a), b), (a, b)), lambda saved, d: (_nt(saved[1], d), _nn(saved[0], d)))


def _iota2(n, m):
    return lax.broadcasted_iota(jnp.int32, (n, m), 0), lax.broadcasted_iota(jnp.int32, (n, m), 1)


def _silu(x):
    return x * jax.nn.sigmoid(x)


def _gated_norm(o, z, nw):
    return o * lax.rsqrt(jnp.mean(o * o, axis=-1, keepdims=True) + EPS) * nw * _silu(z)


def _heads(a, nb):
    return jnp.stack([a[c * CHUNK:(c + 1) * CHUNK, h * DH:(h + 1) * DH] for c in range(nb) for h in range(HEADS)], axis=0)


def _unheads(a3, nb):
    return jnp.concatenate(
        [jnp.concatenate([a3[c * HEADS + h] for h in range(HEADS)], axis=1) for c in range(nb)], axis=0)


def _split3(x):
    hi = x.astype(BF16)
    r1 = x - hi.astype(F32)
    mid = r1.astype(BF16)
    return hi, mid, (r1 - mid.astype(F32)).astype(BF16)


def _summation_matrices(pattern, n_out):
    s = pattern(np.arange(n_out)[:, None], np.arange(CHUNK)[None, :]).astype(np.float32)
    return jnp.asarray(np.tile(s, (1, 3)), BF16), jnp.asarray(np.tile(s.T, (1, 2)), BF16)


def _select_rows(mats, chunks):
    width = chunks[0].shape[1]
    out = _summation(*mats, jnp.concatenate(chunks, axis=1))
    return [out[:, c * width:(c + 1) * width] for c in range(len(chunks))]


def _summation_impl(s, v):
    return jnp.dot(s, jnp.concatenate(_split3(v), axis=0), preferred_element_type=F32)


@jax.custom_vjp
def _summation(s, s_t, v):
    return _summation_impl(s, v)


def _summation_fwd(s, s_t, v):
    return _summation_impl(s, v), s_t


def _summation_bwd(s_t, d):
    hi = d.astype(BF16)
    return None, None, jnp.dot(s_t, jnp.concatenate([hi, (d - hi.astype(F32)).astype(BF16)], axis=0),
                               preferred_element_type=F32)


_summation.defvjp(_summation_fwd, _summation_bwd)


def _chunks(x, nb):
    return [x[c * CHUNK:(c + 1) * CHUNK] for c in range(nb)]


def _running_sum(i, j):
    return j <= i


HG_LEVELS = 6


def _hg_sums(i, j):
    lvl, t = i >> HG_LEVELS, i & (CHUNK - 1)
    last = t
    for l in range(1, HG_LEVELS + 1):
        width = HG_LEVELS + 1 - l
        last = np.where(lvl == l, ((t >> width) << width) + (CHUNK >> l) - 1, last)
    return j <= last


def _level_operand(sh, q3, k3, x):
    def second():
        return ((lax.broadcasted_iota(jnp.int32, (CHUNK, DH), 0) >> sh) & 1) == 1

    def forward(q3, k3, x):
        decay = jnp.exp(-jnp.abs(x))
        out = jnp.where(second(), q3, k3) * decay
        return out, (decay, out)

    def backward(saved, d):
        decay, out = saved
        d_side, t = d * decay, d * out
        return jnp.where(second(), d_side, 0.0), jnp.where(second(), 0.0, d_side), jnp.where(second(), t, -t)

    operand = jax.custom_vjp(lambda q3, k3, x: forward(q3, k3, x)[0])
    operand.defvjp(forward, backward)
    return operand(q3, k3, x)


def hg_local(p, logits, sum_mats):
    nb = p.shape[0] // CHUNK
    l0, l1 = logits[0:1], logits[1:2]
    mx = jnp.maximum(l0, l1)
    e0, e1 = jnp.exp(l0 - mx), jnp.exp(l1 - mx)
    lb = e0 / (e0 + e1)
    q = _silu(p[:, 0:WIDTH])
    f = lb + (1.0 - lb) * jax.nn.sigmoid(p[:, WIDTH:2 * WIDTH])
    k = 1.0 - f
    logf = jnp.log(f)
    sums = _select_rows(sum_mats, _chunks(logf, nb))
    level = lambda l: _heads(jnp.concatenate([s[l * CHUNK:(l + 1) * CHUNK] for s in sums], axis=0), nb)
    q3, k3, v3, g3 = _heads(q, nb), _heads(k, nb), _heads(p[:, 2 * WIDTH:3 * WIDTH], nb), level(0)
    r, c = _iota2(CHUNK, CHUNK)
    a = jnp.where(r == c, _bmm_nt(q3, k3), 0.0)
    for l in range(1, HG_LEVELS + 1):
        sh = HG_LEVELS - l
        qk = _level_operand(sh, q3, k3, g3 - level(l))
        pair = ((r >> (sh + 1)) == (c >> (sh + 1))) & (((r >> sh) & 1) == 1) & (((c >> sh) & 1) == 0)
        a = a + jnp.where(pair, _bmm_nt(qk, qk), 0.0)
    o = _bmm(a, v3)
    glast = g3[:, CHUNK - 1:CHUNK, :]
    egs = tuple(jnp.concatenate([jnp.exp(glast[c * HEADS + h]) for h in range(HEADS)], axis=1) for c in range(nb))
    return _unheads(q3 * jnp.exp(g3), nb), _unheads(k3 * jnp.exp(glast - g3), nb), _unheads(o, nb), egs


def hg_scan(q_in, k_out, v, eg, o_intra, z, nw, st):
    o = o_intra + _bmm_nt(q_in, st)
    return _gated_norm(o, z, nw), st * eg + _bmm_tn(v, k_out)


def _tri_y_impl(a):
    r, c = _iota2(CHUNK, CHUNK)
    same16 = (r // SUB) == (c // SUB)
    same32 = (r // (2 * SUB)) == (c // (2 * SUB))
    a0 = jnp.where(same16, a, 0.0)
    y = -a0
    pw = _bmm(a0, a0)
    for _ in range(2):
        y = y + pw + _bmm(y, pw)
        pw = _bmm(pw, pw)
    y = y + pw + _bmm(y, pw)
    for ak in (jnp.where(same32 & jnp.logical_not(same16), a, 0.0), jnp.where(same32, 0.0, a)):
        m = ak + _bmm(y, ak)
        y = y - (m + _bmm(m, y))
    return y


@jax.custom_vjp
def _tri_y(a):
    return _tri_y_impl(a)


def _tri_y_fwd(a):
    y = _tri_y_impl(a)
    return y, y


def _tri_y_bwd(y, dy):
    n = dy + _bmm_tn(y, dy)
    return (-(n + _bmm_nt(n, y)),)


_tri_y.defvjp(_tri_y_fwd, _tri_y_bwd)


def _saved_inverse(y):
    @jax.custom_vjp
    def inverse(a):
        return y

    inverse.defvjp(lambda a: (y, None), lambda _, dy: _tri_y_bwd(y, dy))
    return inverse


def _head_rows(a3, nb):
    return jnp.concatenate([a3[g] for g in range(nb * HEADS)], axis=0)


def _pack_heads(x):
    return jnp.concatenate([x[0:2 * CHUNK], x[2 * CHUNK:4 * CHUNK]], axis=1)


def _unpack_heads(y):
    return jnp.concatenate([y[:, 0:CHUNK], y[:, CHUNK:2 * CHUNK]], axis=0)


def _packed_head(ref, i, k, h):
    return ref[i, k, (h % 2) * CHUNK:(h % 2 + 1) * CHUNK, (h // 2) * CHUNK:(h // 2 + 1) * CHUNK]


def _rows_down(x, s):
    rows = x.shape[0]

    @jax.custom_vjp
    def rotate(v):
        return pltpu.roll(v, s, 0)

    rotate.defvjp(lambda v: (pltpu.roll(v, s, 0), None), lambda _, d: (pltpu.roll(d, rows - s, 0),))
    return rotate(x)


def gd_local(xx, ab, cw, alog, dtb, sum_mats, inverse=_tri_y):
    n = ab.shape[0]
    nb = n // CHUNK
    conv = cw[CONV_TAPS - 1:CONV_TAPS] * xx[HALO:HALO + n]
    for j in range(CONV_TAPS - 1):
        conv = conv + cw[j:j + 1] * _rows_down(xx, CONV_TAPS - 1 - j)[HALO:HALO + n]
    act = _silu(conv)
    x = ab + dtb
    g_all = -jnp.exp(alog) * (jnp.maximum(x, 0.0) + jnp.log1p(jnp.exp(-jnp.abs(x))))
    beta_all = jax.nn.sigmoid(ab)
    gam_all = jnp.concatenate(_select_rows(sum_mats, _chunks(g_all, nb)), axis=0)
    q3, k3, v3 = _heads(act[:, 0:WIDTH], nb), _heads(act[:, WIDTH:2 * WIDTH], nb), _heads(act[:, 2 * WIDTH:QKV], nb)
    q3 = q3 * lax.rsqrt(jnp.sum(q3 * q3, axis=-1, keepdims=True) + EPS) * (DH ** -0.5)
    k3 = k3 * lax.rsqrt(jnp.sum(k3 * k3, axis=-1, keepdims=True) + EPS)
    pairs = [(c, h) for c in range(nb) for h in range(HEADS)]
    beta = jnp.stack([beta_all[c * CHUNK:(c + 1) * CHUNK, HEADS + h:HEADS + h + 1] for c, h in pairs], axis=0)
    gam = jnp.stack([gam_all[c * CHUNK:(c + 1) * CHUNK, h:h + 1] for c, h in pairs], axis=0)
    gam_t = [gam_all[c * CHUNK:(c + 1) * CHUNK].T for c in range(nb)]
    gam_row = jnp.stack([gam_t[c][h:h + 1, :] for c, h in pairs], axis=0)
    glast = gam[:, CHUNK - 1:CHUNK, :]
    r, c = _iota2(CHUNK, CHUNK)
    dec = jnp.exp(jnp.where(c < r, gam - gam_row, -jnp.inf))
    y = inverse(beta * _bmm_nt(k3, k3) * dec)
    eg = jnp.exp(gam)
    rhs = jnp.concatenate([beta * v3, (beta * eg) * k3], axis=2)
    sol = rhs + _bmm(y, rhs)
    qk = _bmm_nt(q3, k3) * jnp.where(r == c, 1.0, dec)
    eas = tuple(jnp.exp(gam_all[(c + 1) * CHUNK - 1:(c + 1) * CHUNK]) for c in range(nb))
    return (_unheads(sol[:, :, 0:DH], nb), _unheads(sol[:, :, DH:2 * DH], nb), _unheads(q3 * eg, nb),
            _unheads(k3 * jnp.exp(glast - gam), nb), _head_rows(qk, nb), eas), _head_rows(y, nb)


def gd_scan(uu, ww, qe, ke, qk, ea, z, nw, s):
    u = uu - _bmm(ww, s)
    o = _bmm(qe, s) + _bmm(qk, u)
    return _gated_norm(o, z, nw), ea * s + _bmm_tn(ke, u)


def _cparams(*sem):
    return pltpu.CompilerParams(dimension_semantics=sem, vmem_limit_bytes=VMEM_LIMIT)


def _row_tile(n):
    for t in (512, 256, 128, 64):
        if n % t == 0:
            return t
    raise ValueError(f"unsupported token count {n}")


def _w_in_specs():
    once = pl.Buffered(1)
    return [pl.BlockSpec((4 * WIDTH, D_MODEL), lambda *i: (0, 0), pipeline_mode=once),
            pl.BlockSpec((4 * WIDTH, D_MODEL), lambda *i: (1, 0), pipeline_mode=once),
            pl.BlockSpec((AB_PAD, D_MODEL), lambda *i: (8 * WIDTH // AB_PAD, 0), pipeline_mode=once)]


def in_proj(h, h0, norm_w, w_t):
    n = h.shape[0]
    tm = _row_tile(n)
    nt = (((1,), (1,)), ((), ()))

    def body(h_ref, h0_ref, nw_ref, whg_ref, wgd_ref, wab_ref, phg_ref, pgd_ref, pab_ref, phg0_ref, pgd0_ref, pab0_ref, u0_ref):
        def project(x, hg_ref, gd_ref, ab_ref):
            u = (x * lax.rsqrt(jnp.mean(x * x, axis=-1, keepdims=True) + EPS) * nw_ref[...]).astype(MXU_DTYPE)
            hg_ref[...] = lax.dot_general(u, whg_ref[...], nt, preferred_element_type=F32)
            gd_ref[...] = lax.dot_general(u, wgd_ref[...], nt, preferred_element_type=F32)
            ab_ref[...] = lax.dot_general(u, wab_ref[...], nt, preferred_element_type=F32)
            return u

        @pl.when(pl.program_id(0) == 0)
        def _():
            u0_ref[...] = project(h0_ref[...], phg0_ref, pgd0_ref, pab0_ref)

        project(h_ref[...], phg_ref, pgd_ref, pab_ref)

    n0 = h0.shape[0]
    row = lambda w: pl.BlockSpec((tm, w), lambda i: (i, 0))
    lead = lambda w: pl.BlockSpec((n0, w), lambda i: (0, 0))
    widths = [4 * WIDTH, 4 * WIDTH, AB_PAD]
    return pl.pallas_call(
        body, grid=(n // tm,), name="in_proj",
        in_specs=[row(D_MODEL), lead(D_MODEL), pl.BlockSpec(norm_w.shape, lambda i: (0, 0))] + _w_in_specs(),
        out_specs=[row(w) for w in widths] + [lead(w) for w in widths] + [lead(D_MODEL)],
        out_shape=[jax.ShapeDtypeStruct((n, w), F32) for w in widths] + [jax.ShapeDtypeStruct((n0, w), F32) for w in widths]
        + [jax.ShapeDtypeStruct((n0, D_MODEL), MXU_DTYPE)],
        compiler_params=_cparams("arbitrary"),
    )(h, h0, norm_w, w_t, w_t, w_t)


def out_proj_loss(x, tgt, y_hg, y_gd, w_out, fw):
    n = x.shape[0]
    tm = _row_tile(n)
    inv_d = 1.0 / D_MODEL

    def body(x_ref, t_ref, yh_ref, yg_ref, w_ref, fw_ref, dh_ref, dyh_ref, dyg_ref, dw_ref, loss_ref, dfw_ref):
        @pl.when(pl.program_id(0) == 0)
        def _():
            dw_ref[...] = jnp.zeros_like(dw_ref)
            loss_ref[...] = jnp.zeros_like(loss_ref)
            dfw_ref[...] = jnp.zeros_like(dfw_ref)

        yh, yg = yh_ref[...], yg_ref[...]
        wa, wb = w_ref[0:WIDTH, :], w_ref[WIDTH:2 * WIDTH, :]
        h2 = x_ref[...] + jnp.dot(yh, wa, preferred_element_type=F32) + jnp.dot(yg, wb, preferred_element_type=F32)
        r2 = lax.rsqrt(jnp.mean(h2 * h2, axis=-1, keepdims=True) + EPS)
        nrm = h2 * r2
        fwv = fw_ref[...]
        err = nrm * fwv - t_ref[...]
        loss_ref[...] += jnp.full(loss_ref.shape, 0.5 * inv_d * jnp.sum(err * err), F32)
        dout = err * inv_d
        dfw_ref[...] += jnp.sum(dout * nrm, axis=0, keepdims=True)
        dn = dout * fwv
        dh2 = r2 * (dn - nrm * jnp.mean(dn * nrm, axis=-1, keepdims=True))
        dh_ref[...] = dh2
        dhb = dh2.astype(MXU_DTYPE)
        dyh_ref[...] = lax.dot_general(dhb, wa, (((1,), (1,)), ((), ())), preferred_element_type=F32)
        dyg_ref[...] = lax.dot_general(dhb, wb, (((1,), (1,)), ((), ())), preferred_element_type=F32)
        dw_ref[0:WIDTH, :] += lax.dot_general(yh, dhb, (((0,), (0,)), ((), ())), preferred_element_type=F32)
        dw_ref[WIDTH:2 * WIDTH, :] += lax.dot_general(yg, dhb, (((0,), (0,)), ((), ())), preferred_element_type=F32)

    row = lambda w: pl.BlockSpec((tm, w), lambda i: (i, 0))
    full = lambda s: pl.BlockSpec(s, lambda i: (0, 0))
    return pl.pallas_call(
        body, grid=(n // tm,), name="out_proj_loss",
        in_specs=[row(D_MODEL), row(D_MODEL), row(WIDTH), row(WIDTH), full(w_out.shape), full(fw.shape)],
        out_specs=[row(D_MODEL), row(WIDTH), row(WIDTH), full((2 * WIDTH, D_MODEL)), full((8, 128)), full((1, D_MODEL))],
        out_shape=[jax.ShapeDtypeStruct((n, D_MODEL), F32), jax.ShapeDtypeStruct((n, WIDTH), F32),
                   jax.ShapeDtypeStruct((n, WIDTH), F32), jax.ShapeDtypeStruct((2 * WIDTH, D_MODEL), F32),
                   jax.ShapeDtypeStruct((8, 128), F32), jax.ShapeDtypeStruct((1, D_MODEL), F32)],
        compiler_params=_cparams("arbitrary"),
    )(x, tgt, y_hg, y_gd, w_out, fw)


def in_proj_bwd(dphg, dpgd, dpab, w_t, h, dh2, norm_w, h0, u0, dphg0, dpgd0, dpab0):
    n = h.shape[0]
    tm = _row_tile(n)
    steps = n // tm

    def body(dphg_ref, dpgd_ref, dpab_ref, whg_ref, wgd_ref, wab_ref, h_ref, dh2_ref, nw_ref, h0_ref, u0_ref, d0hg_ref,
             d0gd_ref, d0ab_ref, dx_ref, dx0_ref, dnw_ref, ghg_ref, ggd_ref, gab_ref, acc_hg, acc_gd, acc_ab):
        i = pl.program_id(0)
        nwv = nw_ref[...]

        def norm_bwd(dps, x):
            du = jnp.dot(dps[0], whg_ref[...], preferred_element_type=F32)
            du += jnp.dot(dps[1], wgd_ref[...], preferred_element_type=F32)
            du += jnp.dot(dps[2], wab_ref[...], preferred_element_type=F32)
            r = lax.rsqrt(jnp.mean(x * x, axis=-1, keepdims=True) + EPS)
            nrm = x * r
            dn = du * nwv
            return r * (dn - nrm * jnp.mean(dn * nrm, axis=-1, keepdims=True)), nrm, jnp.sum(du * nrm, axis=0, keepdims=True)

        def accumulate(dps, u, first):
            for acc, dp in zip((acc_hg, acc_gd, acc_ab), dps):
                step = min(acc.shape[0], 512)
                for lo in range(0, acc.shape[0], step):
                    part = _mm_tn(dp[:, lo:lo + step], u)
                    acc[lo:lo + step, :] = part if first else acc[lo:lo + step, :] + part

        @pl.when(i == 0)
        def _():
            dps0 = (d0hg_ref[...], d0gd_ref[...], d0ab_ref[...])
            dx0_ref[...], _, dnw_ref[...] = norm_bwd(dps0, h0_ref[...])
            accumulate(dps0, u0_ref[...], True)

        dps = (dphg_ref[...], dpgd_ref[...], dpab_ref[...])
        dx, nrm, dnw = norm_bwd(dps, h_ref[...])
        dx_ref[...] = dh2_ref[...] + dx
        dnw_ref[...] += dnw
        accumulate(dps, (nrm * nwv).astype(MXU_DTYPE), False)

        @pl.when(i == steps - 1)
        def _():
            pltpu.sync_copy(acc_hg, ghg_ref)
            pltpu.sync_copy(acc_gd, ggd_ref)
            pltpu.sync_copy(acc_ab, gab_ref)

    row = lambda w: pl.BlockSpec((tm, w), lambda i: (i, 0))
    full = lambda a: pl.BlockSpec(a.shape, lambda i: (0, 0), pipeline_mode=pl.Buffered(1))
    anywhere = pl.BlockSpec(memory_space=pl.ANY)
    return pl.pallas_call(
        body, grid=(steps,), name="in_proj_bwd",
        in_specs=[row(4 * WIDTH), row(4 * WIDTH), row(AB_PAD)] + _w_in_specs() + [row(D_MODEL), row(D_MODEL), full(norm_w),
                                                                                   full(h0), full(u0), full(dphg0), full(dpgd0),
                                                                                   full(dpab0)],
        out_specs=[row(D_MODEL), pl.BlockSpec(h0.shape, lambda i: (0, 0)), pl.BlockSpec((1, D_MODEL), lambda i: (0, 0)),
                   anywhere, anywhere, anywhere],
        out_shape=[jax.ShapeDtypeStruct((n, D_MODEL), F32), jax.ShapeDtypeStruct(h0.shape, F32),
                   jax.ShapeDtypeStruct((1, D_MODEL), F32), jax.ShapeDtypeStruct((4 * WIDTH, D_MODEL), F32),
                   jax.ShapeDtypeStruct((4 * WIDTH, D_MODEL), F32), jax.ShapeDtypeStruct((AB_PAD, D_MODEL), F32)],
        scratch_shapes=[pltpu.VMEM((4 * WIDTH, D_MODEL), F32), pltpu.VMEM((4 * WIDTH, D_MODEL), F32),
                        pltpu.VMEM((AB_PAD, D_MODEL), F32)],
        compiler_params=pltpu.CompilerParams(dimension_semantics=("arbitrary",), vmem_limit_bytes=VMEM_LIMIT_LARGE),
    )(dphg, dpgd, dpab, w_t, w_t, w_t, h, dh2, norm_w, h0, u0, dphg0, dpgd0, dpab0)


def _sds(shape, dtype=F32):
    return jax.ShapeDtypeStruct(shape, dtype)


def _pairs(b):
    return [(i, h) for i in range(b) for h in range(HEADS)]


def _load_slabs(ref, b, k):
    return jnp.stack([ref[i, k * CHUNK:(k + 1) * CHUNK, h * DH:(h + 1) * DH].astype(F32) for i, h in _pairs(b)], axis=0)


def _lead_slabs(a, b):
    return jnp.stack([a[:, h * DH:(h + 1) * DH].astype(F32) for _, h in _pairs(b)], axis=0)


def _rows(a3, i):
    return jnp.concatenate([a3[i * HEADS + h] for h in range(HEADS)], axis=1)


def _store_slabs(ref, a3, b, k):
    for i in range(b):
        ref[i, k * CHUNK:(k + 1) * CHUNK, :] = _rows(a3, i).astype(ref.dtype)


def _sum_rows(a3, b):
    out = _rows(a3, 0)
    for i in range(1, b):
        out = out + _rows(a3, i)
    return out


def _save_states(ref, s, b, k):
    for i in range(b):
        ref[i, k] = jnp.concatenate([s[i * HEADS + h] for h in range(HEADS)], axis=0).astype(ref.dtype)


def _load_states(ref, b, k):
    return jnp.stack([ref[i, k, h * DH:(h + 1) * DH, :].astype(F32) for i, h in _pairs(b)], axis=0)


def hg_local_fwd(p, p0, logits):
    b, seq, _ = p.shape
    rows = LOCAL_CHUNKS * CHUNK
    nreal = seq // CHUNK

    def body(p_ref, p0_ref, lg_ref, s_ref, st_ref, q_ref, k_ref, o_ref, eg_ref, q0_ref, k0_ref, o0_ref, eg0_ref):
        sum_mats = (s_ref[...], st_ref[...])

        @pl.when((pl.program_id(0) == 0) & (pl.program_id(1) == 0))
        def _():
            q_in, k_out, o0_ref[...], (eg0_ref[...],) = hg_local(p0_ref[...], lg_ref[...], sum_mats)
            q0_ref[...], k0_ref[...] = q_in.astype(MXU_DTYPE), k_out.astype(MXU_DTYPE)

        q_in, k_out, o_intra, egs = hg_local(p_ref[...], lg_ref[...], sum_mats)
        q_ref[...], k_ref[...], o_ref[...] = q_in.astype(MXU_DTYPE), k_out.astype(MXU_DTYPE), o_intra
        for c in range(LOCAL_CHUNKS):
            eg_ref[c] = egs[c]

    slab = pl.BlockSpec((None, rows, WIDTH), lambda s, g: (s, g, 0))
    const = lambda shape: pl.BlockSpec(shape, lambda s, g: (0, 0))
    lead_shapes = [(CHUNK, WIDTH)] * 3 + [(1, WIDTH)]
    sum_mats = _summation_matrices(_hg_sums, (HG_LEVELS + 1) * CHUNK)
    out = pl.pallas_call(
        body, grid=(b, seq // rows), name="hgrn2_local",
        in_specs=[pl.BlockSpec((None, rows, 4 * WIDTH), lambda s, g: (s, g, 0)), const(p0.shape), const(logits.shape)]
        + [const(a.shape) for a in sum_mats],
        out_specs=[slab, slab, slab, pl.BlockSpec((None, LOCAL_CHUNKS, 1, WIDTH), lambda s, g: (s, g, 0, 0))]
        + [const(s) for s in lead_shapes],
        out_shape=[_sds((b, seq, WIDTH), MXU_DTYPE)] * 2 + [_sds((b, seq, WIDTH)), _sds((b, nreal, 1, WIDTH))]
        + [_sds(lead_shapes[0], MXU_DTYPE)] * 2 + [_sds(lead_shapes[2]), _sds(lead_shapes[3])],
        compiler_params=_cparams("arbitrary", "arbitrary"),
    )(p, p0, logits, *sum_mats)
    return out[0:4], out[4:8]


def _hg_scan_args(b, k, q_ref, k_ref, o_ref, v_ref, z_ref, eg_ref):
    eg = jnp.stack([eg_ref[i, k, :, h * DH:(h + 1) * DH] for i, h in _pairs(b)], axis=0)
    return (_load_slabs(q_ref, b, k), _load_slabs(k_ref, b, k), _load_slabs(v_ref, b, k), eg, _load_slabs(o_ref, b, k),
            _load_slabs(z_ref, b, k))


def _hg_lead_args(b, q0_ref, k0_ref, o0_ref, p0_ref, eg0_ref):
    eg = jnp.stack([eg0_ref[:, h * DH:(h + 1) * DH] for _, h in _pairs(b)], axis=0)
    return (_lead_slabs(q0_ref[...], b), _lead_slabs(k0_ref[...], b), _lead_slabs(p0_ref[:, 2 * WIDTH:3 * WIDTH], b), eg,
            _lead_slabs(o0_ref[...], b), _lead_slabs(p0_ref[:, 3 * WIDTH:4 * WIDTH], b))


def _scan_specs(b, ng, reverse, chunks):
    group = (lambda i: ng - 1 - i) if reverse else (lambda i: i)
    slab = lambda lane_block: pl.BlockSpec((b, chunks * CHUNK, WIDTH), lambda i: (0, group(i), lane_block))
    per_chunk = lambda *tail: pl.BlockSpec((b, chunks) + tail, lambda i: (0, group(i)) + (0,) * len(tail))
    const = lambda a: pl.BlockSpec(a.shape, lambda i: (0,) * a.ndim)
    return slab, per_chunk, const


RING = 3


def run_scans(parts, nc, name, ring=False):
    n_in = [len(p["args"]) for p in parts]
    n_out = [len(p["out_shape"]) for p in parts]
    n_scr = [len(p["scratch_shapes"]) for p in parts]
    flat = lambda key: [v for p in parts for v in p[key]]
    in_specs, args = flat("in_specs"), flat("args")
    streams = [sum(n_in[:i]) + j for i, p in enumerate(parts) for j in range(p["streamed"])] if ring else []
    specs = {j: in_specs[j] for j in streams}

    def body(*refs):
        ins, outs = list(refs[:sum(n_in)]), refs[sum(n_in):sum(n_in) + sum(n_out)]
        scr = refs[sum(n_in) + sum(n_out):]
        if ring:
            bufs, sems, scr = scr[:len(streams)], scr[len(streams)], scr[len(streams) + 1:]
            s = pl.program_id(0)

            def copy(n, step):
                spec, hbm = specs[streams[n]], ins[streams[n]]
                window = tuple(pl.ds(ix * size, size) for ix, size in zip(spec.index_map(step), spec.block_shape))
                return pltpu.make_async_copy(hbm.at[window], bufs[n].at[step % RING], sems.at[n, step % RING])

            @pl.when(s == 0)
            def _():
                for n in range(len(streams)):
                    for step in range(min(RING - 1, nc)):
                        copy(n, step).start()

            @pl.when(s + RING - 1 < nc)
            def _():
                for n in range(len(streams)):
                    copy(n, s + RING - 1).start()

            for n in range(len(streams)):
                copy(n, s).wait()
            staged = list(ins)
            for n, j in enumerate(streams):
                staged[j] = bufs[n].at[s % RING]
            ins = staged
        for i, part in enumerate(parts):
            part["body"](*ins[sum(n_in[:i]):sum(n_in[:i + 1])], *outs[sum(n_out[:i]):sum(n_out[:i + 1])],
                         *scr[sum(n_scr[:i]):sum(n_scr[:i + 1])])

    ring_scratch = ([pltpu.VMEM((RING,) + tuple(specs[j].block_shape), args[j].dtype) for j in streams]
                    + [pltpu.SemaphoreType.DMA((len(streams), RING))]) if ring else []
    in_specs = [pl.BlockSpec(memory_space=pl.ANY) if j in specs else spec for j, spec in enumerate(in_specs)]
    out = pl.pallas_call(body, grid=(nc,), name=name, in_specs=in_specs, out_specs=flat("out_specs"),
                         out_shape=flat("out_shape"), scratch_shapes=ring_scratch + flat("scratch_shapes"),
                         compiler_params=_cparams("arbitrary"))(*args)
    return [out[sum(n_out[:i]):sum(n_out[:i + 1])] for i in range(len(parts))]


def hg_scan_fwd(p, p0, local, lead, nw):
    b, seq, _ = p.shape
    q_in, k_out, o_intra, eg = local
    slab, per_chunk, const = _scan_specs(b, seq // (SCAN_CHUNKS_FWD * CHUNK), False, SCAN_CHUNKS_FWD)

    def body(q_ref, k_ref, o_ref, v_ref, z_ref, eg_ref, q0_ref, k0_ref, o0_ref, p0_ref, eg0_ref, nw_ref, y_ref, ss_ref, st):
        @pl.when(pl.program_id(0) == 0)
        def _():
            st[...] = hg_scan(*_hg_lead_args(b, q0_ref, k0_ref, o0_ref, p0_ref, eg0_ref), nw_ref[...], jnp.zeros(st.shape, F32))[1]

        s = st[...]
        for k in range(SCAN_CHUNKS_FWD):
            _save_states(ss_ref, s, b, k)
            y, s = hg_scan(*_hg_scan_args(b, k, q_ref, k_ref, o_ref, v_ref, z_ref, eg_ref), nw_ref[...], s)
            _store_slabs(y_ref, y, b, k)
        st[...] = s

    return dict(
        body=body, args=(q_in, k_out, o_intra, p, p, eg, lead[0], lead[1], lead[2], p0, lead[3], nw), streamed=6,
        in_specs=[slab(0), slab(0), slab(0), slab(2), slab(3), per_chunk(1, WIDTH)] + [const(a) for a in lead[0:3]]
        + [const(p0), const(lead[3]), const(nw)],
        out_specs=[slab(0), per_chunk(WIDTH, DH)],
        out_shape=[_sds((b, seq, WIDTH), MXU_DTYPE), _sds((b, seq // CHUNK, WIDTH, DH), MXU_DTYPE)],
        scratch_shapes=[pltpu.VMEM((b * HEADS, DH, DH), F32)])


def hg_scan_bwd(p, p0, local, lead, nw, ssave, dy):
    b, seq, _ = p.shape
    ng = seq // (SCAN_CHUNKS * CHUNK)
    q_in, k_out, o_intra, eg = local
    slab, per_chunk, const = _scan_specs(b, ng, True, SCAN_CHUNKS)

    def body(q_ref, k_ref, o_ref, v_ref, z_ref, eg_ref, q0_ref, k0_ref, o0_ref, p0_ref, eg0_ref, nw_ref, ss_ref, dy_ref,
             dq_ref, dk_ref, do_ref, dv_ref, dz_ref, deg_ref, dq0_ref, dk0_ref, do0_ref, dv0_ref, dz0_ref, deg0_ref, dnw_ref,
             dst):
        i = pl.program_id(0)

        @pl.when(i == 0)
        def _():
            dst[...] = jnp.zeros_like(dst)
            dnw_ref[...] = jnp.zeros_like(dnw_ref)

        ds = dst[...]
        for k in reversed(range(SCAN_CHUNKS)):
            args = _hg_scan_args(b, k, q_ref, k_ref, o_ref, v_ref, z_ref, eg_ref)
            _, vjp = jax.vjp(hg_scan, *args, nw_ref[...], _load_states(ss_ref, b, k))
            dq, dk, dv, deg, do, dz, dnw, ds = vjp((_load_slabs(dy_ref, b, k), ds))
            dnw_ref[...] += dnw
            for ref, val in ((dq_ref, dq), (dk_ref, dk), (do_ref, do), (dv_ref, dv), (dz_ref, dz)):
                _store_slabs(ref, val, b, k)
            for j in range(b):
                deg_ref[j, k] = _rows(deg, j)
        dst[...] = ds

        @pl.when(i == ng - 1)
        def _():
            args = _hg_lead_args(b, q0_ref, k0_ref, o0_ref, p0_ref, eg0_ref)
            _, vjp = jax.vjp(hg_scan, *args, nw_ref[...], jnp.zeros(dst.shape, F32))
            dq, dk, dv, deg, do, dz, dnw, _ = vjp((jnp.zeros((b * HEADS, CHUNK, DH), F32), ds))
            dnw_ref[...] += dnw
            for ref, val in ((dq0_ref, dq), (dk0_ref, dk), (do0_ref, do), (dv0_ref, dv), (dz0_ref, dz), (deg0_ref, deg)):
                ref[...] = _sum_rows(val, b)

    lead_out = [const(a) for a in lead[0:3]] + [const(lead[0]), const(lead[0]), const(lead[3])]
    return dict(
        body=body, args=(q_in, k_out, o_intra, p, p, eg, lead[0], lead[1], lead[2], p0, lead[3], nw, ssave, dy),
        in_specs=[slab(0), slab(0), slab(0), slab(2), slab(3), per_chunk(1, WIDTH)] + [const(a) for a in lead[0:3]]
        + [const(p0), const(lead[3]), const(nw), per_chunk(WIDTH, DH), slab(0)],
        out_specs=[slab(0)] * 5 + [per_chunk(1, WIDTH)] + lead_out + [const(nw)],
        out_shape=[_sds((b, seq, WIDTH))] * 2 + [_sds((b, seq, WIDTH), MXU_DTYPE)] * 3 + [_sds(eg.shape)]
        + [_sds((CHUNK, WIDTH))] * 5 + [_sds((1, WIDTH)), _sds(nw.shape)],
        scratch_shapes=[pltpu.VMEM((b * HEADS, DH, DH), F32)])


def _hg_local_vjp(sum_mats, p, logits, dq, dk, do, degs, dv, dz):
    _, vjp = jax.vjp(lambda p_, logits_: hg_local(p_, logits_, sum_mats), p, logits)
    dp, dlg = vjp((dq, dk, do.astype(F32), degs))
    return dp + jnp.concatenate([jnp.zeros((p.shape[0], 2 * WIDTH), F32), dv.astype(F32), dz.astype(F32)], axis=1), dlg


def hg_local_bwd(p, p0, logits, cot, cot0):
    b, seq, _ = p.shape
    rows = LOCAL_CHUNKS * CHUNK

    def body(p_ref, p0_ref, lg_ref, s_ref, st_ref, dq_ref, dk_ref, do_ref, dv_ref, dz_ref, deg_ref, dq0_ref, dk0_ref, do0_ref,
             dv0_ref, dz0_ref, deg0_ref, dp_ref, dp0_ref, dlg_ref):
        sum_mats = (s_ref[...], st_ref[...])

        @pl.when((pl.program_id(0) == 0) & (pl.program_id(1) == 0))
        def _():
            dp0, dlg_ref[...] = _hg_local_vjp(sum_mats, p0_ref[...], lg_ref[...], dq0_ref[...], dk0_ref[...], do0_ref[...],
                                              (deg0_ref[...],), dv0_ref[...], dz0_ref[...])
            dp0_ref[...] = dp0.astype(MXU_DTYPE)

        degs = tuple(deg_ref[c] for c in range(LOCAL_CHUNKS))
        dp, dlg = _hg_local_vjp(sum_mats, p_ref[...], lg_ref[...], dq_ref[...], dk_ref[...], do_ref[...], degs, dv_ref[...],
                                dz_ref[...])
        dp_ref[...] = dp.astype(MXU_DTYPE)
        dlg_ref[...] += dlg

    slab = pl.BlockSpec((None, rows, WIDTH), lambda s, g: (s, g, 0))
    wide = pl.BlockSpec((None, rows, 4 * WIDTH), lambda s, g: (s, g, 0))
    const = lambda a: pl.BlockSpec(a.shape, lambda s, g: (0, 0))
    sum_mats = _summation_matrices(_hg_sums, (HG_LEVELS + 1) * CHUNK)
    return pl.pallas_call(
        body, grid=(b, seq // rows), name="hgrn2_local_bwd",
        in_specs=[wide, const(p0), const(logits), const(sum_mats[0]), const(sum_mats[1]), slab, slab, slab, slab, slab,
                  pl.BlockSpec((None, LOCAL_CHUNKS, 1, WIDTH), lambda s, g: (s, g, 0, 0))] + [const(a) for a in cot0],
        out_specs=[wide, const(p0), const(logits)],
        out_shape=[_sds(p.shape, MXU_DTYPE), _sds(p0.shape, MXU_DTYPE), _sds(logits.shape)],
        compiler_params=_cparams("arbitrary", "arbitrary"),
    )(p, p0, logits, *sum_mats, *cot, *cot0)


def _halo_block(g):
    return jnp.maximum((LOCAL_CHUNKS * CHUNK // HALO) * g - 1, 0)


def _gd_window(g, p_ref, halo_ref, p0_ref):
    halo = jnp.where(g == 0, p0_ref[CHUNK - HALO:CHUNK, 0:QKV], halo_ref[...])
    return jnp.concatenate([halo, p_ref[:, 0:QKV]], axis=0)


def _lead_window(p0_ref):
    return jnp.concatenate([jnp.zeros((HALO, QKV), F32), p0_ref[:, 0:QKV]], axis=0)


def gd_local_fwd(p, p0, ab, ab0, cw, alog, dtb):
    b, seq, _ = p.shape
    rows = LOCAL_CHUNKS * CHUNK
    nreal = seq // CHUNK

    def body(p_ref, halo_ref, p0_ref, ab_ref, ab0_ref, cw_ref, al_ref, dt_ref, s_ref, st_ref, u_ref, w_ref, qe_ref, ke_ref,
             qk_ref, ea_ref, inv_ref, u0_ref, w0_ref, qe0_ref, ke0_ref, qk0_ref, ea0_ref, inv0_ref):
        sum_mats = (s_ref[...], st_ref[...])

        @pl.when((pl.program_id(0) == 0) & (pl.program_id(1) == 0))
        def _():
            (u0_ref[...], ww, qe, ke, qk0_ref[...], (ea0_ref[...],)), inv0_ref[...] = gd_local(
                _lead_window(p0_ref), ab0_ref[...], cw_ref[...], al_ref[...], dt_ref[...], sum_mats, inverse=_tri_y_impl)
            w0_ref[...], qe0_ref[...], ke0_ref[...] = ww.astype(MXU_DTYPE), qe.astype(MXU_DTYPE), ke.astype(MXU_DTYPE)

        (uu, ww, qe, ke, qk, eas), inv = gd_local(_gd_window(pl.program_id(1), p_ref, halo_ref, p0_ref), ab_ref[...],
                                                  cw_ref[...], al_ref[...], dt_ref[...], sum_mats, inverse=_tri_y_impl)
        u_ref[...], w_ref[...], qe_ref[...], ke_ref[...] = uu, ww.astype(MXU_DTYPE), qe.astype(MXU_DTYPE), ke.astype(MXU_DTYPE)
        for c in range(LOCAL_CHUNKS):
            qk_ref[c] = _pack_heads(qk[c * HEADS * CHUNK:(c + 1) * HEADS * CHUNK])
            inv_ref[c] = _pack_heads(inv[c * HEADS * CHUNK:(c + 1) * HEADS * CHUNK])
            ea_ref[c] = eas[c]

    const = lambda shape: pl.BlockSpec(shape, lambda s, g: (0, 0))
    slab = pl.BlockSpec((None, rows, WIDTH), lambda s, g: (s, g, 0))
    mats = pl.BlockSpec((None, LOCAL_CHUNKS, 2 * CHUNK, 2 * CHUNK), lambda s, g: (s, g, 0, 0))
    lead_out = [_sds((CHUNK, WIDTH))] + [_sds((CHUNK, WIDTH), MXU_DTYPE)] * 3 + [_sds((HEADS * CHUNK, CHUNK)), _sds((1, AB_PAD)),
                                                                                _sds((HEADS * CHUNK, CHUNK))]
    sum_mats = _summation_matrices(_running_sum, CHUNK)
    out = pl.pallas_call(
        body, grid=(b, seq // rows), name="gdn_local",
        in_specs=[pl.BlockSpec((None, rows, 4 * WIDTH), lambda s, g: (s, g, 0)),
                  pl.BlockSpec((None, HALO, QKV), lambda s, g: (s, _halo_block(g), 0)), const(p0.shape),
                  pl.BlockSpec((None, rows, AB_PAD), lambda s, g: (s, g, 0)), const(ab0.shape), const(cw.shape),
                  const(alog.shape), const(dtb.shape), const(sum_mats[0].shape), const(sum_mats[1].shape)],
        out_specs=[slab] * 4 + [mats, pl.BlockSpec((None, LOCAL_CHUNKS, 1, AB_PAD), lambda s, g: (s, g, 0, 0)), mats]
        + [const(s.shape) for s in lead_out],
        out_shape=[_sds((b, seq, WIDTH))] + [_sds((b, seq, WIDTH), MXU_DTYPE)] * 3
        + [_sds((b, nreal, 2 * CHUNK, 2 * CHUNK)), _sds((b, nreal, 1, AB_PAD)), _sds((b, nreal, 2 * CHUNK, 2 * CHUNK))] + lead_out,
        compiler_params=_cparams("arbitrary", "arbitrary"),
    )(p, p, p0, ab, ab0, cw, alog, dtb, *sum_mats)
    return out[0:6], out[6], out[7:13], out[13]


def _gd_scan_args(b, k, u_ref, w_ref, qe_ref, ke_ref, qk_ref, ea_ref, z_ref):
    qk = jnp.stack([_packed_head(qk_ref, i, k, h) for i, h in _pairs(b)], axis=0)
    ea = jnp.stack([ea_ref[i, k, :, h:h + 1] for i, h in _pairs(b)], axis=0)
    return (_load_slabs(u_ref, b, k), _load_slabs(w_ref, b, k), _load_slabs(qe_ref, b, k), _load_slabs(ke_ref, b, k), qk, ea,
            _load_slabs(z_ref, b, k))


def _gd_lead_args(b, u0_ref, w0_ref, qe0_ref, ke0_ref, qk0_ref, ea0_ref, p0_ref):
    qk = jnp.stack([qk0_ref[h * CHUNK:(h + 1) * CHUNK, :] for _, h in _pairs(b)], axis=0)
    ea = jnp.stack([ea0_ref[:, h:h + 1] for _, h in _pairs(b)], axis=0)
    return (_lead_slabs(u0_ref[...], b), _lead_slabs(w0_ref[...], b), _lead_slabs(qe0_ref[...], b), _lead_slabs(ke0_ref[...], b),
            qk, ea, _lead_slabs(p0_ref[:, QKV:QKV + WIDTH], b))


def gd_scan_fwd(p, p0, local, lead, nw):
    b, seq, _ = p.shape
    slab, per_chunk, const = _scan_specs(b, seq // (SCAN_CHUNKS_FWD * CHUNK), False, SCAN_CHUNKS_FWD)

    def body(u_ref, w_ref, qe_ref, ke_ref, qk_ref, ea_ref, z_ref, u0_ref, w0_ref, qe0_ref, ke0_ref, qk0_ref, ea0_ref, p0_ref,
             nw_ref, y_ref, ss_ref, st):
        @pl.when(pl.program_id(0) == 0)
        def _():
            lead_args = _gd_lead_args(b, u0_ref, w0_ref, qe0_ref, ke0_ref, qk0_ref, ea0_ref, p0_ref)
            st[...] = gd_scan(*lead_args, nw_ref[...], jnp.zeros(st.shape, F32))[1]

        s = st[...]
        for k in range(SCAN_CHUNKS_FWD):
            _save_states(ss_ref, s, b, k)
            y, s = gd_scan(*_gd_scan_args(b, k, u_ref, w_ref, qe_ref, ke_ref, qk_ref, ea_ref, z_ref), nw_ref[...], s)
            _store_slabs(y_ref, y, b, k)
        st[...] = s

    return dict(
        body=body, args=(*local, p, *lead, p0, nw), streamed=7,
        in_specs=[slab(0)] * 4 + [per_chunk(2 * CHUNK, 2 * CHUNK), per_chunk(1, AB_PAD), slab(3)] + [const(a) for a in lead]
        + [const(p0), const(nw)],
        out_specs=[slab(0), per_chunk(WIDTH, DH)],
        out_shape=[_sds((b, seq, WIDTH), MXU_DTYPE), _sds((b, seq // CHUNK, WIDTH, DH), MXU_DTYPE)],
        scratch_shapes=[pltpu.VMEM((b * HEADS, DH, DH), F32)])


def gd_scan_bwd(p, p0, local, lead, nw, ssave, dy):
    b, seq, _ = p.shape
    ng = seq // (SCAN_CHUNKS * CHUNK)
    slab, per_chunk, const = _scan_specs(b, ng, True, SCAN_CHUNKS)

    def body(u_ref, w_ref, qe_ref, ke_ref, qk_ref, ea_ref, z_ref, u0_ref, w0_ref, qe0_ref, ke0_ref, qk0_ref, ea0_ref, p0_ref,
             nw_ref, ss_ref, dy_ref, du_ref, dw_ref, dqe_ref, dke_ref, dqk_ref, dea_ref, dz_ref, du0_ref, dw0_ref, dqe0_ref,
             dke0_ref, dqk0_ref, dea0_ref, dz0_ref, dnw_ref, dst):
        i = pl.program_id(0)
        lane = lax.broadcasted_iota(jnp.int32, (1, AB_PAD), 1)

        def gate_rows(dea, j):
            return sum(jnp.where(lane == h, dea[j * HEADS + h], 0.0) for h in range(HEADS))

        def matrix_rows(dqk, j):
            return jnp.concatenate([dqk[j * HEADS + h] for h in range(HEADS)], axis=0)

        @pl.when(i == 0)
        def _():
            dst[...] = jnp.zeros_like(dst)
            dnw_ref[...] = jnp.zeros_like(dnw_ref)

        ds = dst[...]
        for k in reversed(range(SCAN_CHUNKS)):
            args = _gd_scan_args(b, k, u_ref, w_ref, qe_ref, ke_ref, qk_ref, ea_ref, z_ref)
            _, vjp = jax.vjp(gd_scan, *args, nw_ref[...], _load_states(ss_ref, b, k))
            du, dw, dqe, dke, dqk, dea, dz, dnw, ds = vjp((_load_slabs(dy_ref, b, k), ds))
            dnw_ref[...] += dnw
            for ref, val in ((du_ref, du), (dw_ref, dw), (dqe_ref, dqe), (dke_ref, dke), (dz_ref, dz)):
                _store_slabs(ref, val, b, k)
            for j in range(b):
                dqk_ref[j, k] = _pack_heads(matrix_rows(dqk, j))
                dea_ref[j, k] = gate_rows(dea, j)
        dst[...] = ds

        @pl.when(i == ng - 1)
        def _():
            args = _gd_lead_args(b, u0_ref, w0_ref, qe0_ref, ke0_ref, qk0_ref, ea0_ref, p0_ref)
            _, vjp = jax.vjp(gd_scan, *args, nw_ref[...], jnp.zeros(dst.shape, F32))
            du, dw, dqe, dke, dqk, dea, dz, dnw, _ = vjp((jnp.zeros((b * HEADS, CHUNK, DH), F32), ds))
            dnw_ref[...] += dnw
            for ref, val in ((du0_ref, du), (dw0_ref, dw), (dqe0_ref, dqe), (dke0_ref, dke), (dz0_ref, dz)):
                ref[...] = _sum_rows(val, b)
            dqk0_ref[...] = sum((matrix_rows(dqk, j) for j in range(1, b)), matrix_rows(dqk, 0))
            dea0_ref[...] = sum((gate_rows(dea, j) for j in range(1, b)), gate_rows(dea, 0))

    uu, ww, qe, ke, qk, ea = local
    return dict(
        body=body, args=(*local, p, *lead, p0, nw, ssave, dy),
        in_specs=[slab(0)] * 4 + [per_chunk(2 * CHUNK, 2 * CHUNK), per_chunk(1, AB_PAD), slab(3)] + [const(a) for a in lead]
        + [const(p0), const(nw), per_chunk(WIDTH, DH), slab(0)],
        out_specs=[slab(0)] * 4 + [per_chunk(2 * CHUNK, 2 * CHUNK), per_chunk(1, AB_PAD), slab(0)] + [const(a) for a in lead]
        + [const(lead[0]), const(nw)],
        out_shape=[_sds((b, seq, WIDTH))] * 4 + [_sds(qk.shape), _sds(ea.shape), _sds((b, seq, WIDTH), MXU_DTYPE)]
        + [_sds(a.shape) for a in lead] + [_sds(lead[0].shape), _sds(nw.shape)],
        scratch_shapes=[pltpu.VMEM((b * HEADS, DH, DH), F32)])


def _gd_local_vjp(sum_mats, inv_rows, xx, ab, cw, alog, dtb):
    nb = ab.shape[0] // CHUNK
    inv = jnp.stack([inv_rows[g * CHUNK:(g + 1) * CHUNK] for g in range(nb * HEADS)], axis=0)
    _, vjp, _ = jax.vjp(lambda *a: gd_local(*a, sum_mats, inverse=_saved_inverse(inv)), xx, ab, cw, alog, dtb, has_aux=True)
    return vjp


def gd_local_bwd(p, p0, ab, ab0, cw, alog, dtb, inv, inv0, cot, dz, cot0, dz0):
    b, seq, _ = p.shape
    rows = LOCAL_CHUNKS * CHUNK
    ng = seq // rows
    du, dw, dqe, dke, dqk, dea = cot

    def body(p_ref, halo_ref, p0_ref, ab_ref, ab0_ref, cw_ref, al_ref, dt_ref, s_ref, st_ref, inv_ref, inv0_ref, du_ref, dw_ref,
             dqe_ref, dke_ref, dqk_ref, dea_ref, dz_ref, du0_ref, dw0_ref, dqe0_ref, dke0_ref, dqk0_ref, dea0_ref, dz0_ref,
             dp_ref, dab_ref, dp0_ref, dab0_ref, dcw_ref, dal_ref, ddt_ref, dhalo, dtail):
        s, i = pl.program_id(0), pl.program_id(1)
        g = ng - 1 - i
        sum_mats = (s_ref[...], st_ref[...])

        @pl.when(i == 0)
        def _():
            dhalo[...] = jnp.zeros_like(dhalo)

        @pl.when((s == 0) & (i == 0))
        def _():
            dtail[...] = jnp.zeros_like(dtail)
            dcw_ref[...] = jnp.zeros_like(dcw_ref)
            dal_ref[...] = jnp.zeros_like(dal_ref)
            ddt_ref[...] = jnp.zeros_like(ddt_ref)

        def finish(dxx, dab, dcw, dal, ddt, before, n, dz_val, dp_out, dab_out):
            dqkv = dxx[HALO:HALO + n] + jnp.concatenate([jnp.zeros((n - HALO, QKV), F32), before], axis=0)
            dp_out[...] = jnp.concatenate([dqkv.astype(MXU_DTYPE), dz_val.astype(MXU_DTYPE)], axis=1)
            dab_out[...] = dab.astype(MXU_DTYPE)
            dcw_ref[...] += dcw
            dal_ref[...] += dal
            ddt_ref[...] += ddt

        inv_rows = jnp.concatenate([_unpack_heads(inv_ref[c]) for c in range(LOCAL_CHUNKS)], axis=0)
        vjp = _gd_local_vjp(sum_mats, inv_rows, _gd_window(g, p_ref, halo_ref, p0_ref), ab_ref[...], cw_ref[...], al_ref[...],
                            dt_ref[...])
        dqk_all = jnp.concatenate([_unpack_heads(dqk_ref[c]) for c in range(LOCAL_CHUNKS)], axis=0)
        deas = tuple(dea_ref[c] for c in range(LOCAL_CHUNKS))
        grads = vjp((du_ref[...], dw_ref[...], dqe_ref[...], dke_ref[...], dqk_all, deas))
        finish(*grads, dhalo[...], rows, dz_ref[...], dp_ref, dab_ref)
        dhalo[...] = grads[0][0:HALO]

        @pl.when(g == 0)
        def _():
            dtail[...] += grads[0][0:HALO]

        @pl.when((s == b - 1) & (g == 0))
        def _():
            vjp0 = _gd_local_vjp(sum_mats, inv0_ref[...], _lead_window(p0_ref), ab0_ref[...], cw_ref[...], al_ref[...],
                                 dt_ref[...])
            grads0 = vjp0((du0_ref[...], dw0_ref[...], dqe0_ref[...], dke0_ref[...], dqk0_ref[...], (dea0_ref[...],)))
            finish(*grads0, dtail[...], CHUNK, dz0_ref[...], dp0_ref, dab0_ref)

    rg = lambda i: ng - 1 - i
    const = lambda a: pl.BlockSpec(a.shape, lambda s, i: (0, 0))
    slab = pl.BlockSpec((None, rows, WIDTH), lambda s, i: (s, rg(i), 0))
    wide = pl.BlockSpec((None, rows, 4 * WIDTH), lambda s, i: (s, rg(i), 0))
    gates = pl.BlockSpec((None, rows, AB_PAD), lambda s, i: (s, rg(i), 0))
    mats = pl.BlockSpec((None, LOCAL_CHUNKS, 2 * CHUNK, 2 * CHUNK), lambda s, i: (s, rg(i), 0, 0))
    sum_mats = _summation_matrices(_running_sum, CHUNK)
    return pl.pallas_call(
        body, grid=(b, ng), name="gdn_local_bwd",
        in_specs=[wide, pl.BlockSpec((None, HALO, QKV), lambda s, i: (s, _halo_block(rg(i)), 0)), const(p0), gates, const(ab0),
                  const(cw), const(alog), const(dtb), const(sum_mats[0]), const(sum_mats[1]), mats, const(inv0), slab, slab,
                  slab, slab, mats,
                  pl.BlockSpec((None, LOCAL_CHUNKS, 1, AB_PAD), lambda s, i: (s, rg(i), 0, 0)), slab]
        + [const(a) for a in cot0] + [const(dz0)],
        out_specs=[wide, gates, const(p0), const(ab0), const(cw), const(alog), const(dtb)],
        out_shape=[_sds(p.shape, MXU_DTYPE), _sds(ab.shape, MXU_DTYPE), _sds(p0.shape, MXU_DTYPE), _sds(ab0.shape, MXU_DTYPE),
                   _sds(cw.shape), _sds(alog.shape), _sds(dtb.shape)],
        scratch_shapes=[pltpu.VMEM((HALO, QKV), F32), pltpu.VMEM((HALO, QKV), F32)],
        compiler_params=_cparams("arbitrary", "arbitrary"),
    )(p, p, p0, ab, ab0, cw, alog, dtb, *sum_mats, inv, inv0, du, dw, dqe, dke, dqk, dea, dz, *cot0, dz0)


def _position():
    return lax.axis_index("x"), lax.axis_index("y"), lax.axis_index("c")


EXCHANGE_COPIES = 10


def _exchange_blocks(bufs, send_sems, recv_sems):
    x, y, c = _position()
    here, x_nbr, y_nbr, diag = (x, y), (1 - x, y), (x, 1 - y), (1 - x, 1 - y)
    sibling = (x, y, 1 - c)
    me = (x, y, c)
    n = range(len(bufs))

    def rows(a, chip, core, half=None):
        block = bufs[a].at[4 * chip[0] + 2 * chip[1] + core]
        if half is None:
            return block
        total = bufs[a].shape[1]
        tile = 8 * (4 // jnp.dtype(bufs[a].dtype).itemsize)
        split = total // 2 // tile * tile
        return block.at[pl.ds(0, split)] if half == 0 else block.at[pl.ds(split, total - split)]

    def copy(a, k, region, to):
        return pltpu.make_async_remote_copy(src_ref=region, dst_ref=region, send_sem=send_sems.at[a * EXCHANGE_COPIES + k],
                                            recv_sem=recv_sems.at[a * EXCHANGE_COPIES + k], device_id=to, device_id_type=MESH)

    sent = [copy(a, 0, rows(a, here, c), sibling) for a in n]
    sent += [cp for a in n for cp in (copy(a, 1, rows(a, here, c, 0), (*x_nbr, c)), copy(a, 4, rows(a, here, c, 1), (*y_nbr, c)))]
    sent += [cp for a in n for cp in (copy(a, 2, rows(a, here, c, 1), (*x_nbr, c)), copy(a, 3, rows(a, here, c, 0), (*y_nbr, c)))]
    for cp in sent:
        cp.start()

    def after(arrivals, a, k, region, to):
        for cp in arrivals:
            cp.wait_recv()
        sent.append(copy(a, k, region, to))
        sent[-1].start()

    for a in n:
        after([copy(a, 1, rows(a, x_nbr, c, 0), me)], a, 5, rows(a, x_nbr, c, 0), (*y_nbr, c))
        after([copy(a, 4, rows(a, y_nbr, c, 1), me)], a, 6, rows(a, y_nbr, c, 1), (*x_nbr, c))
    for a in n:
        after([copy(a, 2, rows(a, x_nbr, c, 1), me)], a, 7, rows(a, x_nbr, c), sibling)
        after([copy(a, 3, rows(a, y_nbr, c, 0), me)], a, 8, rows(a, y_nbr, c), sibling)
    for a in n:
        after([copy(a, 5, rows(a, diag, c, 0), me), copy(a, 6, rows(a, diag, c, 1), me)], a, 9, rows(a, diag, c), sibling)
    for a in n:
        copy(a, 0, rows(a, here, 1 - c), me).wait_recv()
        for k, chip in ((7, x_nbr), (8, y_nbr), (9, diag)):
            copy(a, k, rows(a, chip, 1 - c), me).wait_recv()
    for cp in sent:
        cp.wait_send()


def _exchange_sems(n_bufs):
    return [pltpu.SemaphoreType.DMA((n_bufs * EXCHANGE_COPIES,)), pltpu.SemaphoreType.DMA((n_bufs * EXCHANGE_COPIES,))]


def gather_weights(w_in_t, w_out, small, pad_rows):
    rows, _, cols = w_in_t.shape
    buf_rows = -(-rows // ROW_TILE_BF16) * ROW_TILE_BF16

    def body(wi_ref, wo_ref, sm_ref, wi_out, wo_out, sm_out, wi_buf, send_sems, recv_sems):
        x, y, c = _position()
        me = 4 * x + 2 * y + c
        wi_buf[me, pl.ds(0, rows), :] = wi_ref[:, 0, :].astype(MXU_DTYPE)
        wi_buf[me, pl.ds(rows, buf_rows - rows), :] = jnp.zeros((buf_rows - rows, cols), MXU_DTYPE)
        wo_out[me] = wo_ref[...].astype(MXU_DTYPE)
        sm_out[me] = sm_ref[...]
        _exchange_blocks([wi_buf, wo_out, sm_out], send_sems, recv_sems)
        for d in range(N_DEV):
            wi_out[pl.ds(d * rows, rows), :] = wi_buf[d, pl.ds(0, rows), :]
        wi_out[pl.ds(N_DEV * rows, pad_rows), :] = jnp.zeros((pad_rows, cols), MXU_DTYPE)

    return pl.pallas_call(
        body, name="gather_weights", in_specs=[VMEM_SPEC] * 3, out_specs=[VMEM_SPEC] * 3,
        out_shape=[jax.ShapeDtypeStruct((N_DEV * rows + pad_rows, cols), MXU_DTYPE),
                   jax.ShapeDtypeStruct((N_DEV,) + w_out.shape, MXU_DTYPE), jax.ShapeDtypeStruct((N_DEV,) + small.shape, F32)],
        scratch_shapes=[pltpu.VMEM((N_DEV, buf_rows, cols), MXU_DTYPE)] + _exchange_sems(3),
        compiler_params=pltpu.CompilerParams(vmem_limit_bytes=VMEM_LIMIT))(w_in_t, w_out, small)


HOPS = 6


def reduce_gradients(tensors, small, name):
    n_t = len(tensors)
    arrays = [a for parts, _ in tensors for a, _ in parts]
    first_array = [sum(len(parts) for parts, _ in tensors[:t]) for t in range(n_t)]

    def pieces(t, j):
        parts, block_rows = tensors[t]
        out, base = [], 0
        for pi, (_, valid) in enumerate(parts):
            lo, hi = max(j * block_rows, base), min((j + 1) * block_rows, base + valid)
            if lo < hi:
                out.append((first_array[t] + pi, lo - base, lo - j * block_rows, hi - lo))
            base += valid
        return out

    def body(*refs):
        n_a = len(arrays)
        in_refs, small_ref = refs[:n_a], refs[n_a]
        out_refs, small_sum = refs[n_a + 1:n_a + 1 + n_t], refs[n_a + 1 + n_t]
        bufs, small_buf = refs[n_a + 2 + n_t:n_a + 2 + 5 * n_t], refs[n_a + 2 + 5 * n_t]
        s1_sems, r1_sems, s2_sems, r2_sems, small_send, small_recv = refs[n_a + 3 + 5 * n_t:]
        x, y, c = _position()
        chip = 2 * x + y

        def put(t, dst, j, add=None):
            for ai, src_row, dst_row, size in pieces(t, j):
                v = in_refs[ai][pl.ds(src_row, size), :]
                if add is not None:
                    v = v + add[pl.ds(dst_row, size), :].astype(F32)
                dst[pl.ds(dst_row, size), :] = v.astype(dst.dtype)

        def swap(t, k):
            send1, recv1 = bufs[4 * t], bufs[4 * t + 1]
            return pltpu.make_async_remote_copy(src_ref=send1.at[k], dst_ref=recv1.at[k], send_sem=s1_sems.at[4 * t + k],
                                                recv_sem=r1_sems.at[4 * t + k], device_id=(x, y, 1 - c), device_id_type=MESH)

        to_x, to_y, to_diag = 2 * (1 - x) + y, 2 * x + (1 - y), 2 * (1 - x) + (1 - y)
        x_dev, y_dev = (1 - x, y, c), (x, 1 - y, c)

        def half(ref, h):
            total = ref.shape[0]
            split = total // 2 // ROW_TILE_BF16 * ROW_TILE_BF16
            return ref.at[pl.ds(0, split)] if h == 0 else ref.at[pl.ds(split, total - split)]

        def hop(t, copy_id, src, dst, to):
            return pltpu.make_async_remote_copy(src_ref=src, dst_ref=dst, send_sem=s2_sems.at[HOPS * t + copy_id],
                                                recv_sem=r2_sems.at[HOPS * t + copy_id], device_id=to, device_id_type=MESH)

        def hops(t):
            send2, landing = bufs[4 * t + 2], bufs[4 * t + 3]
            return [hop(t, 0, half(send2.at[to_diag], 0), half(landing.at[0], 0), x_dev),
                    hop(t, 1, half(send2.at[to_diag], 1), half(landing.at[0], 1), y_dev),
                    hop(t, 2, half(send2.at[to_x], 0), half(landing.at[1], 0), x_dev),
                    hop(t, 3, half(send2.at[to_y], 1), half(landing.at[2], 1), y_dev),
                    hop(t, 4, half(send2.at[to_x], 1), half(landing.at[1], 1), x_dev),
                    hop(t, 5, half(send2.at[to_y], 0), half(landing.at[2], 0), y_dev)]

        def add_relay(t, slot, h):
            dst, src = half(bufs[4 * t + 2].at[slot], h), half(bufs[4 * t + 3].at[0], h)
            dst[...] = (dst[...].astype(F32) + src[...].astype(F32)).astype(dst.dtype)

        for t in range(n_t):
            send2 = bufs[4 * t + 2]
            pad = send2.shape[1] - tensors[t][1]
            if pad:
                send2[:, pl.ds(tensors[t][1], pad), :] = jnp.zeros((4, pad, send2.shape[2]), send2.dtype)
            for j in range(N_DEV):
                @pl.when((j & 1) != c)
                def _():
                    put(t, bufs[4 * t].at[j >> 1], j)
            for k in range(4):
                swap(t, k).start()

        small_buf[4 * x + 2 * y + c] = small_ref[...]
        _exchange_blocks([small_buf], small_send, small_recv)
        total = small_buf[0]
        for d in range(1, N_DEV):
            total = total + small_buf[d]
        small_sum[...] = total

        for t in range(n_t):
            recv1 = bufs[4 * t + 1]
            for k in range(4):
                swap(t, k).wait_recv()
                for j in (2 * k, 2 * k + 1):
                    @pl.when(((j & 1) == c) & (k != chip))
                    def _():
                        put(t, bufs[4 * t + 2].at[k], j, add=recv1.at[k])

                    @pl.when(((j & 1) == c) & (k == chip))
                    def _():
                        put(t, out_refs[t], j, add=recv1.at[k])
            for cp in hops(t)[0:4]:
                cp.start()

        for t in range(n_t):
            cps = hops(t)
            cps[0].wait_recv()
            add_relay(t, to_y, 0)
            cps[5].start()
            cps[1].wait_recv()
            add_relay(t, to_x, 1)
            cps[4].start()

        for t in range(n_t):
            cps, rows = hops(t), tensors[t][1]
            for first, second, slot in ((cps[2], cps[4], 1), (cps[3], cps[5], 2)):
                first.wait_recv()
                second.wait_recv()
                out_refs[t][...] += bufs[4 * t + 3][slot, pl.ds(0, rows), :].astype(F32)

        for t in range(n_t):
            for cp in hops(t):
                cp.wait_send()
            for k in range(4):
                swap(t, k).wait_send()

    scratch, out_shape = [], []
    for parts, block_rows in tensors:
        cols = parts[0][0].shape[1]
        tiled_rows = -(-block_rows // ROW_TILE_BF16) * ROW_TILE_BF16
        scratch += [pltpu.VMEM((4, block_rows, cols), MXU_DTYPE)] * 2
        scratch += [pltpu.VMEM((4, tiled_rows, cols), MXU_DTYPE), pltpu.VMEM((3, tiled_rows, cols), MXU_DTYPE)]
        out_shape.append(jax.ShapeDtypeStruct((block_rows, cols), F32))
    out_shape.append(jax.ShapeDtypeStruct(small.shape, F32))
    scratch += [pltpu.VMEM((N_DEV,) + small.shape, F32)] + [pltpu.SemaphoreType.DMA((4 * n_t,))] * 2
    scratch += [pltpu.SemaphoreType.DMA((HOPS * n_t,))] * 2 + _exchange_sems(1)
    return pl.pallas_call(
        body, name=name, in_specs=[VMEM_SPEC] * (len(arrays) + 1), out_specs=[VMEM_SPEC] * (n_t + 1), out_shape=out_shape,
        scratch_shapes=scratch, compiler_params=pltpu.CompilerParams(vmem_limit_bytes=VMEM_LIMIT),
    )(*arrays, small)


def _adamw_step(w, g, m, v):
    mn = ADAM_B1 * m + (1.0 - ADAM_B1) * g
    vn = ADAM_B2 * v + (1.0 - ADAM_B2) * jnp.square(g)
    m_hat = mn / (1.0 - ADAM_B1 ** ADAM_STEP)
    v_hat = vn / (1.0 - ADAM_B2 ** ADAM_STEP)
    return -ADAM_LR * (m_hat / (jnp.sqrt(v_hat) + ADAM_EPS) + ADAM_WD * w), mn, vn


def adamw_small(packed, first_rows, ws, gs, ms, vs):
    k = len(ws)
    given = [g for g in gs if g is not None]

    def body(*refs):
        packed_ref, w_refs, m_refs, v_refs = refs[0], refs[1:1 + k], refs[1 + k:1 + 2 * k], refs[1 + 2 * k:1 + 3 * k]
        g_refs, outs = iter(refs[1 + 3 * k:1 + 3 * k + len(given)]), refs[1 + 3 * k + len(given):]
        for i in range(k):
            rows, cols = w_refs[i].shape
            g = next(g_refs)[...] if gs[i] is not None else packed_ref[first_rows[i]:first_rows[i] + rows, 0:cols]
            outs[4 * i][...] = g
            outs[4 * i + 1][...], outs[4 * i + 2][...], outs[4 * i + 3][...] = _adamw_step(w_refs[i][...], g, m_refs[i][...],
                                                                                          v_refs[i][...])

    n_in = 1 + 3 * k + len(given)
    out = pl.pallas_call(body, name="adamw_small", in_specs=[VMEM_SPEC] * n_in, out_specs=[VMEM_SPEC] * (4 * k),
                         out_shape=[jax.ShapeDtypeStruct(w.shape, F32) for w in ws for _ in range(4)],
                         compiler_params=pltpu.CompilerParams(vmem_limit_bytes=VMEM_LIMIT))(packed, *ws, *ms, *vs, *given)
    return [out[4 * i:4 * i + 4] for i in range(k)]


def adamw_w_in(w, g_t, m, v):
    def body(w_ref, g_ref, m_ref, v_ref, go_ref, d_ref, nm_ref, nv_ref):
        g = g_ref[...]
        go_ref[:, 0, :] = g
        d_ref[:, 0, :], nm_ref[:, 0, :], nv_ref[:, 0, :] = _adamw_step(w_ref[:, 0, :], g, m_ref[:, 0, :], v_ref[:, 0, :])

    return pl.pallas_call(body, name="adamw_w_in", in_specs=[VMEM_SPEC] * 4, out_specs=[VMEM_SPEC] * 4,
                          out_shape=[jax.ShapeDtypeStruct(w.shape, F32)] * 4,
                          compiler_params=pltpu.CompilerParams(vmem_limit_bytes=VMEM_LIMIT))(w, g_t, m, v)


def _pad_rows(a, rows=8):
    return jnp.pad(a, ((0, rows - a.shape[0]), (0, 0)))


def _pad_lanes(a, lanes=128):
    return jnp.pad(a, ((0, 0), (0, lanes - a.shape[1])))


def kernel(x, meta_tokens, norm_w, w_in, conv_w, hg_lb_logits, hg_norm_w, gdn_A_log, gdn_dt_bias, gdn_norm_w, w_out, final_norm_w, loss_target, m_meta_tokens, m_norm_w, m_w_in, m_conv_w, m_hg_lb_logits, m_hg_norm_w, m_gdn_A_log, m_gdn_dt_bias, m_gdn_norm_w, m_w_out, m_final_norm_w, v_meta_tokens, v_norm_w, v_w_in, v_conv_w, v_hg_lb_logits, v_hg_norm_w, v_gdn_A_log, v_gdn_dt_bias, v_gdn_norm_w, v_w_out, v_final_norm_w):
    b, seq, _ = x.shape
    n = b * seq
    dev = 4 * lax.axis_index("x") + 2 * lax.axis_index("y") + lax.axis_index("c")
    col_shard = IN_COLS // N_DEV

    small_w = jnp.concatenate([_pad_lanes(meta_tokens, 256), _pad_rows(_pad_lanes(conv_w[0], 256))], axis=0)
    w_t, w_out_g, small_g = gather_weights(jnp.transpose(w_in, (2, 0, 1)), w_out[0], small_w, AB_PAD - 2 * HEADS)
    meta_g = small_g[:, 0:N_META, 0:D_MODEL // N_DEV]
    conv_g = small_g[:, N_META:N_META + CONV_TAPS, 0:QKV // N_DEV]
    w_out_full = w_out_g.reshape(2 * WIDTH, D_MODEL)
    cw = jnp.transpose(conv_g, (1, 0, 2)).reshape(CONV_TAPS, QKV)
    meta = jnp.transpose(meta_g, (1, 0, 2)).reshape(N_META, D_MODEL)
    alog = _pad_lanes(gdn_A_log)
    dtb = _pad_lanes(gdn_dt_bias)
    fw = final_norm_w.reshape(1, D_MODEL)

    h0 = jnp.concatenate([jnp.zeros((CHUNK - N_META, D_MODEL), F32), meta], axis=0)
    x2 = x.reshape(n, D_MODEL)
    phg, pgd, pab, phg0, pgd0, pab0, u0 = in_proj(x2, h0, norm_w, w_t)
    phg3, pgd3, pab3 = phg.reshape(b, seq, 4 * WIDTH), pgd.reshape(b, seq, 4 * WIDTH), pab.reshape(b, seq, AB_PAD)
    hg_loc, hg_lead = hg_local_fwd(phg3, phg0, hg_lb_logits)
    gd_loc, gd_inv, gd_lead, gd_inv0 = gd_local_fwd(pgd3, pgd0, pab3, pab0, cw, alog, dtb)
    (y_hg, s_hg), (y_gd, s_gd) = run_scans([hg_scan_fwd(phg3, phg0, hg_loc, hg_lead, hg_norm_w),
                                            gd_scan_fwd(pgd3, pgd0, gd_loc, gd_lead, gdn_norm_w)],
                                           seq // (SCAN_CHUNKS_FWD * CHUNK), "scans", ring=True)

    dh2, dy_hg, dy_gd, g_w_out, loss_part, g_fw = out_proj_loss(
        x2, loss_target.reshape(n, D_MODEL), y_hg.reshape(n, WIDTH), y_gd.reshape(n, WIDTH), w_out_full, fw)

    hb, gb = run_scans([hg_scan_bwd(phg3, phg0, hg_loc, hg_lead, hg_norm_w, s_hg, dy_hg.reshape(b, seq, WIDTH)),
                        gd_scan_bwd(pgd3, pgd0, gd_loc, gd_lead, gdn_norm_w, s_gd, dy_gd.reshape(b, seq, WIDTH))],
                       seq // (SCAN_CHUNKS * CHUNK), "scans_bwd")
    dphg, dphg0, g_lb = hg_local_bwd(phg3, phg0, hg_lb_logits, hb[0:6], hb[6:12])
    g_hg_nw = hb[12]
    dpgd, dpab, dpgd0, dpab0, g_cw, g_alog, g_dtb = gd_local_bwd(pgd3, pgd0, pab3, pab0, cw, alog, dtb, gd_inv, gd_inv0,
                                                                 gb[0:6], gb[6], gb[7:13], gb[13])
    g_gd_nw = gb[14]
    dphg, dpgd, dpab = dphg.reshape(n, 4 * WIDTH), dpgd.reshape(n, 4 * WIDTH), dpab.reshape(n, AB_PAD)

    grad_x, dh0, g_nw, g_w_hg, g_w_gd, g_w_ab = in_proj_bwd(dphg, dpgd, dpab, w_t, x2, dh2, norm_w, h0, u0, dphg0, dpgd0, dpab0)

    small = jnp.concatenate([
        g_nw.reshape(8, 128), g_lb.reshape(8, 128), _pad_rows(g_hg_nw), _pad_rows(g_alog), _pad_rows(g_dtb), _pad_rows(g_gd_nw),
        g_fw.reshape(8, 128), g_cw.reshape(48, 128),
        dh0[CHUNK - N_META:CHUNK].reshape(128, 128), loss_part], axis=0)
    g_w_in_t, g_w_out, small = reduce_gradients(
        [([(g_w_hg, 4 * WIDTH), (g_w_gd, 4 * WIDTH), (g_w_ab, 2 * HEADS)], col_shard),
         ([(g_w_out, 2 * WIDTH)], (2 * WIDTH) // N_DEV)], small, "reduce_gradients")
    g_cw_full = small[56:104].reshape(CONV_TAPS, QKV)
    g_meta_full = small[104:232].reshape(N_META, D_MODEL)
    loss = small[232, 0]
    g_conv = lax.dynamic_slice_in_dim(g_cw_full, dev * (QKV // N_DEV), QKV // N_DEV, axis=1)
    g_meta = lax.dynamic_slice_in_dim(g_meta_full, dev * (D_MODEL // N_DEV), D_MODEL // N_DEV, axis=1)

    names = ["meta_tokens", "norm_w", "w_in", "conv_w", "hg_lb_logits", "hg_norm_w", "gdn_A_log", "gdn_dt_bias",
             "gdn_norm_w", "w_out", "final_norm_w"]
    weights = [meta_tokens, norm_w, w_in, conv_w, hg_lb_logits, hg_norm_w, gdn_A_log, gdn_dt_bias, gdn_norm_w, w_out,
               final_norm_w]
    moms = [m_meta_tokens, m_norm_w, m_w_in, m_conv_w, m_hg_lb_logits, m_hg_norm_w, m_gdn_A_log, m_gdn_dt_bias,
            m_gdn_norm_w, m_w_out, m_final_norm_w]
    vars_ = [v_meta_tokens, v_norm_w, v_w_in, v_conv_w, v_hg_lb_logits, v_hg_norm_w, v_gdn_A_log, v_gdn_dt_bias,
             v_gdn_norm_w, v_w_out, v_final_norm_w]
    gradient = [g_meta, 0, None, g_conv, 8, 16, 24, 32, 40, g_w_out, 48]
    shape2d = [g_meta.shape, (8, 128), None, g_conv.shape, (8, 128), (1, DH), (1, HEADS), (1, HEADS), (1, DH), g_w_out.shape,
               (8, 128)]
    i_w_in = names.index("w_in")
    others = [i for i in range(len(names)) if i != i_w_in]
    in_rows = lambda i: isinstance(gradient[i], int)
    stepped = adamw_small(small, [gradient[i] if in_rows(i) else None for i in others],
                          [weights[i].reshape(shape2d[i]) for i in others], [None if in_rows(i) else gradient[i] for i in others],
                          [moms[i].reshape(shape2d[i]) for i in others], [vars_[i].reshape(shape2d[i]) for i in others])
    results = {i: [a.reshape(weights[i].shape) for a in stepped[j]] for j, i in enumerate(others)}
    to3, back = (lambda a: jnp.transpose(a, (2, 0, 1))), (lambda a: jnp.transpose(a, (1, 2, 0)))
    results[i_w_in] = [back(a) for a in adamw_w_in(to3(w_in), g_w_in_t, to3(m_w_in), to3(v_w_in))]
    grads, deltas, new_ms, new_vs = zip(*(results[i] for i in range(len(names))))
    return (loss, grad_x.reshape(x.shape), *grads, *deltas, *new_ms, *new_vs)
```
